```python
import jax, jax.numpy as jnp
from jax import lax
import numpy as np

D_MODEL = 1024
BATCH = 8
SEQ = 4096
DEPTH = 1

D_MIX = 2 * D_MODEL
D_SSD = D_MIX // 2
D_ATT = D_MIX - D_SSD
SSD_HEADDIM = 64
SSD_HEADS = D_SSD // SSD_HEADDIM
SSD_GROUPS = 2
SSD_STATE = 128
CONV_K = 4
CHUNK = 128
ATT_HEAD_DIM = 64
ATT_Q_HEADS = D_ATT // ATT_HEAD_DIM
ATT_KV_HEADS = 4
WINDOW = 128
ROPE_THETA = 500000.0
ROPE_DIM = ATT_HEAD_DIM // 4
ALPHA = (2.0 * DEPTH) ** 0.25
BETA = (8.0 * DEPTH) ** -0.25
LN_EPS = 1e-5
RMS_EPS = 1e-5

D_BC = SSD_GROUPS * SSD_STATE
D_XBC = D_SSD + 2 * D_BC
D_KV = ATT_KV_HEADS * ATT_HEAD_DIM
OFF_Z = 0
OFF_XBC = OFF_Z + D_SSD
OFF_DT = OFF_XBC + D_XBC
OFF_Q = OFF_DT + SSD_HEADS
OFF_K = OFF_Q + D_ATT
OFF_V = OFF_K + D_KV
OFF_G = OFF_V + D_KV
D_IN_PROJ = OFF_G + D_ATT
SPLIT_IDX = (OFF_XBC, OFF_DT, OFF_Q, OFF_K, OFF_V, OFF_G)

kernel_name = "hybrid_ssd_swa_sink_deepnorm"


def layer_norm(x, g, b):
    xf = x.astype(jnp.float32)
    mu = jnp.mean(xf, axis=-1, keepdims=True)
    var = jnp.mean(jnp.square(xf - mu), axis=-1, keepdims=True)
    y = (xf - mu) * lax.rsqrt(var + LN_EPS) * g.astype(jnp.float32) + b.astype(jnp.float32)
    return y.astype(x.dtype)


def rope_tables(positions):
    inv = ROPE_THETA ** (-jnp.arange(0, ROPE_DIM, 2, dtype=jnp.float32) / ROPE_DIM)
    ang = positions.astype(jnp.float32)[..., None] * inv
    return jnp.cos(ang), jnp.sin(ang)


def apply_partial_rope(t, cos, sin):
    rot = t[..., :ROPE_DIM].astype(jnp.float32)
    rest = t[..., ROPE_DIM:]
    r1, r2 = rot[..., :ROPE_DIM // 2], rot[..., ROPE_DIM // 2:]
    c, s = cos[:, :, None, :], sin[:, :, None, :]
    rot = jnp.concatenate([r1 * c - r2 * s, r2 * c + r1 * s], axis=-1)
    return jnp.concatenate([rot.astype(t.dtype), rest], axis=-1)


def causal_depthwise_conv(u, w, b):
    c = u.shape[-1]
    out = lax.conv_general_dilated(
        u, w[:, None, :].astype(u.dtype), window_strides=(1,),
        padding=[(CONV_K - 1, 0)], dimension_numbers=("NWC", "WIO", "NWC"),
        feature_group_count=c)
    return out + b.astype(u.dtype)


def gated_rmsnorm(y, z, w):
    yf = (y.astype(jnp.float32) * jax.nn.silu(z.astype(jnp.float32)))
    yg = yf.reshape(*yf.shape[:-1], SSD_GROUPS, D_SSD // SSD_GROUPS)
    yg = yg * lax.rsqrt(jnp.mean(jnp.square(yg), axis=-1, keepdims=True) + RMS_EPS)
    return yg.reshape(yf.shape) * w.astype(jnp.float32)


def ssd_chunked(X, dA, Bc, Cc):
    a = jnp.moveaxis(dA, 2, -1).astype(jnp.float32)
    a_cs = jnp.cumsum(a, axis=-1)
    T = a.shape[-1]
    causal = jnp.tril(jnp.ones((T, T), dtype=bool))
    seg = a_cs[..., :, None] - a_cs[..., None, :]
    Lmat = jnp.where(causal, jnp.exp(jnp.where(causal, seg, 0.0)), 0.0)
    cb = jnp.einsum("bclgn,bcsgn->bcgls", Cc, Bc).astype(jnp.float32)
    M = cb[:, :, :, None] * Lmat
    y_diag = jnp.einsum("bcgrls,bcsgrp->bclgrp", M, X)
    decay_states = jnp.moveaxis(jnp.exp(a_cs[..., -1:] - a_cs), -1, 2)
    states = jnp.einsum("bclgn,bclgrp->bcgrpn", Bc, X * decay_states[..., None]).astype(jnp.float32)
    chunk_decay = jnp.exp(a_cs[..., -1])

    def step(h, inp):
        s_c, d_c = inp
        return h * d_c[..., None, None] + s_c, h

    h0 = jnp.zeros(states.shape[:1] + states.shape[2:], jnp.float32)
    _, prev = lax.scan(step, h0, (jnp.moveaxis(states, 1, 0), jnp.moveaxis(chunk_decay, 1, 0)))
    prev = jnp.moveaxis(prev, 0, 1)
    decay_out = jnp.moveaxis(jnp.exp(a_cs), -1, 2)
    y_off = jnp.einsum("bclgn,bcgrpn->bclgrp", Cc, prev) * decay_out[..., None]
    return y_diag + y_off


def ssd_branch(z, xbc, dt, conv_w, conv_b, dt_bias, a_log, d_skip, norm_w):
    b, L, _ = xbc.shape
    R = SSD_HEADS // SSD_GROUPS
    nc = L // CHUNK
    xbc = jax.nn.silu(causal_depthwise_conv(xbc, conv_w, conv_b))
    xs, Bm, Cm = jnp.split(xbc, [D_SSD, D_SSD + D_BC], axis=-1)
    dt = jax.nn.softplus(dt.astype(jnp.float32) + dt_bias.astype(jnp.float32))
    A = -jnp.exp(a_log.astype(jnp.float32))
    xh = xs.reshape(b, L, SSD_GROUPS, R, SSD_HEADDIM)
    dth = dt.reshape(b, L, SSD_GROUPS, R)
    X = (xh.astype(jnp.float32) * dth[..., None]).reshape(b, nc, CHUNK, SSD_GROUPS, R, SSD_HEADDIM)
    dA = (dth * A.reshape(SSD_GROUPS, R)).reshape(b, nc, CHUNK, SSD_GROUPS, R)
    Bc = Bm.reshape(b, nc, CHUNK, SSD_GROUPS, SSD_STATE)
    Cc = Cm.reshape(b, nc, CHUNK, SSD_GROUPS, SSD_STATE)
    y = ssd_chunked(X, dA, Bc, Cc).reshape(b, L, SSD_GROUPS, R, SSD_HEADDIM)
    y = y + d_skip.astype(jnp.float32).reshape(SSD_GROUPS, R, 1) * xh.astype(jnp.float32)
    y = gated_rmsnorm(y.reshape(b, L, D_SSD), z, norm_w)
    return y.astype(z.dtype)


def swa_branch(q, k, v, g, cos, sin, sinks):
    b, L, _ = q.shape
    nb = L // WINDOW
    R = ATT_Q_HEADS // ATT_KV_HEADS
    q = apply_partial_rope(q.reshape(b, L, ATT_Q_HEADS, ATT_HEAD_DIM), cos, sin)
    k = apply_partial_rope(k.reshape(b, L, ATT_KV_HEADS, ATT_HEAD_DIM), cos, sin)
    v = v.reshape(b, L, ATT_KV_HEADS, ATT_HEAD_DIM)
    qb = q.reshape(b, nb, WINDOW, ATT_KV_HEADS, R, ATT_HEAD_DIM)
    kb = k.reshape(b, nb, WINDOW, ATT_KV_HEADS, ATT_HEAD_DIM)
    vb = v.reshape(b, nb, WINDOW, ATT_KV_HEADS, ATT_HEAD_DIM)
    pad = ((0, 0), (1, 0), (0, 0), (0, 0), (0, 0))
    kk = jnp.concatenate([jnp.pad(kb[:, :-1], pad), kb], axis=2)
    vv = jnp.concatenate([jnp.pad(vb[:, :-1], pad), vb], axis=2)
    scale = ATT_HEAD_DIM ** -0.5
    s = jnp.einsum("bnqkrd,bnskd->bnkrqs", qb, kk).astype(jnp.float32) * scale
    qi = jnp.arange(WINDOW)[:, None]
    si = jnp.arange(2 * WINDOW)[None, :]
    band = (si > qi) & (si <= qi + WINDOW)
    valid = band[None] & ((jnp.arange(nb)[:, None, None] > 0) | (si[None] >= WINDOW))
    s = jnp.where(valid[None, :, None, None], s, -jnp.inf)
    sink = sinks.astype(jnp.float32).reshape(ATT_KV_HEADS, R)[None, None, :, :, None, None]
    m = jnp.maximum(jnp.max(s, axis=-1, keepdims=True), sink)
    p = jnp.exp(s - m)
    denom = jnp.sum(p, axis=-1, keepdims=True) + jnp.exp(sink - m)
    o = jnp.einsum("bnkrqs,bnskd->bnqkrd", (p / denom).astype(vv.dtype), vv)
    o = o.reshape(b, L, D_ATT)
    return (o * jax.nn.silu(g.astype(jnp.float32))).astype(q.dtype)


def hybrid_mixer(x, cos, sin, w_in, conv_w, conv_b, dt_bias, a_log, d_skip, ssd_norm_w, attn_sinks, w_out):
    proj = jnp.einsum("bld,de->ble", x, w_in)
    z, xbc, dt, q, k, v, g = jnp.split(proj, SPLIT_IDX, axis=-1)
    y_ssd = ssd_branch(z, xbc, dt, conv_w, conv_b, dt_bias, a_log, d_skip, ssd_norm_w)
    y_att = swa_branch(q, k, v, g, cos, sin, attn_sinks)
    y = jnp.concatenate([y_ssd, y_att], axis=-1)
    return jnp.einsum("ble,ed->bld", y, w_out)


def _fwd_setup_inputs(seed: int = 0) -> dict:
    key = jax.random.key(seed)
    ks = jax.random.split(key, 12)
    x = jax.random.normal(ks[0], (BATCH, SEQ, D_MODEL), jnp.float32)
    positions = jnp.broadcast_to(jnp.arange(SEQ, dtype=jnp.int32)[None, :], (BATCH, SEQ))
    col_scale = jnp.ones((D_IN_PROJ,), jnp.float32).at[OFF_V:OFF_V + D_KV].set(BETA)
    w_in = jax.random.normal(ks[1], (DEPTH, D_MODEL, D_IN_PROJ), jnp.float32) * (D_MODEL ** -0.5) * col_scale
    conv_w = jax.random.normal(ks[2], (DEPTH, CONV_K, D_XBC), jnp.float32) * (CONV_K ** -0.5)
    conv_b = 0.01 * jax.random.normal(ks[3], (DEPTH, D_XBC), jnp.float32)
    dt0 = jnp.exp(jax.random.uniform(ks[4], (DEPTH, SSD_HEADS), jnp.float32,
                                     minval=float(np.log(1e-3)), maxval=float(np.log(1e-1))))
    dt_bias = dt0 + jnp.log(-jnp.expm1(-dt0))
    a_log = jnp.log(jax.random.uniform(ks[5], (DEPTH, SSD_HEADS), jnp.float32, minval=1.0, maxval=16.0))
    d_skip = 1.0 + 0.1 * jax.random.normal(ks[6], (DEPTH, SSD_HEADS), jnp.float32)
    ssd_norm_w = 1.0 + 0.02 * jax.random.normal(ks[7], (DEPTH, D_SSD), jnp.float32)
    attn_sinks = 0.5 * jax.random.normal(ks[8], (DEPTH, ATT_Q_HEADS), jnp.float32)
    w_out = jax.random.normal(ks[9], (DEPTH, D_MIX, D_MODEL), jnp.float32) * (D_MIX ** -0.5) * BETA
    ln_g = 1.0 + 0.02 * jax.random.normal(ks[10], (DEPTH, D_MODEL), jnp.float32)
    ln_b = 0.02 * jax.random.normal(ks[11], (DEPTH, D_MODEL), jnp.float32)
    return {"x": x, "positions": positions, "w_in": w_in, "conv_w": conv_w, "conv_b": conv_b,
            "dt_bias": dt_bias, "a_log": a_log, "d_skip": d_skip, "ssd_norm_w": ssd_norm_w,
            "attn_sinks": attn_sinks, "w_out": w_out, "ln_g": ln_g, "ln_b": ln_b}


def _fwd_reference(x, positions, w_in, conv_w, conv_b, dt_bias, a_log, d_skip, ssd_norm_w,
              attn_sinks, w_out, ln_g, ln_b):
    cos, sin = rope_tables(positions)
    for l in range(DEPTH):
        h = hybrid_mixer(x, cos, sin, w_in[l], conv_w[l], conv_b[l], dt_bias[l], a_log[l],
                         d_skip[l], ssd_norm_w[l], attn_sinks[l], w_out[l])
        x = layer_norm(ALPHA * x + h.astype(x.dtype), ln_g[l], ln_b[l])
    return x


import jax as _jax
import jax.numpy as _jnp

TWIN_FORMAT = 'train_step'
FWD_PARAMS = ['x', 'positions', 'w_in', 'conv_w', 'conv_b', 'dt_bias', 'a_log', 'd_skip', 'ssd_norm_w', 'attn_sinks', 'w_out', 'ln_g', 'ln_b']
TWIN_WEIGHTS = ['w_in', 'conv_w', 'conv_b', 'dt_bias', 'a_log', 'd_skip', 'ssd_norm_w', 'attn_sinks', 'w_out', 'ln_g', 'ln_b']
TWIN_DIFF_INPUT = 'x'
TWIN_INPUTS = ['x', 'positions', 'w_in', 'conv_w', 'conv_b', 'dt_bias', 'a_log', 'd_skip', 'ssd_norm_w', 'attn_sinks', 'w_out', 'ln_g', 'ln_b', 'loss_target', 'm_w_in', 'm_conv_w', 'm_conv_b', 'm_dt_bias', 'm_a_log', 'm_d_skip', 'm_ssd_norm_w', 'm_attn_sinks', 'm_w_out', 'm_ln_g', 'm_ln_b', 'v_w_in', 'v_conv_w', 'v_conv_b', 'v_dt_bias', 'v_a_log', 'v_d_skip', 'v_ssd_norm_w', 'v_attn_sinks', 'v_w_out', 'v_ln_g', 'v_ln_b']
TWIN_OUTPUTS = ['loss', 'grad_x', 'grad_w_in', 'grad_conv_w', 'grad_conv_b', 'grad_dt_bias', 'grad_a_log', 'grad_d_skip', 'grad_ssd_norm_w', 'grad_attn_sinks', 'grad_w_out', 'grad_ln_g', 'grad_ln_b', 'delta_w_in', 'delta_conv_w', 'delta_conv_b', 'delta_dt_bias', 'delta_a_log', 'delta_d_skip', 'delta_ssd_norm_w', 'delta_attn_sinks', 'delta_w_out', 'delta_ln_g', 'delta_ln_b', 'new_m_w_in', 'new_m_conv_w', 'new_m_conv_b', 'new_m_dt_bias', 'new_m_a_log', 'new_m_d_skip', 'new_m_ssd_norm_w', 'new_m_attn_sinks', 'new_m_w_out', 'new_m_ln_g', 'new_m_ln_b', 'new_v_w_in', 'new_v_conv_w', 'new_v_conv_b', 'new_v_dt_bias', 'new_v_a_log', 'new_v_d_skip', 'new_v_ssd_norm_w', 'new_v_attn_sinks', 'new_v_w_out', 'new_v_ln_g', 'new_v_ln_b']
TWIN_LEAF_KINDS = {'loss': 'loss', 'grad_x': 'grad_x', 'grad_w_in': 'grad_w', 'grad_conv_w': 'grad_w', 'grad_conv_b': 'grad_w', 'grad_dt_bias': 'grad_w', 'grad_a_log': 'grad_w', 'grad_d_skip': 'grad_w', 'grad_ssd_norm_w': 'grad_w', 'grad_attn_sinks': 'grad_w', 'grad_w_out': 'grad_w', 'grad_ln_g': 'grad_w', 'grad_ln_b': 'grad_w', 'delta_w_in': 'delta_w', 'delta_conv_w': 'delta_w', 'delta_conv_b': 'delta_w', 'delta_dt_bias': 'delta_w', 'delta_a_log': 'delta_w', 'delta_d_skip': 'delta_w', 'delta_ssd_norm_w': 'delta_w', 'delta_attn_sinks': 'delta_w', 'delta_w_out': 'delta_w', 'delta_ln_g': 'delta_w', 'delta_ln_b': 'delta_w', 'new_m_w_in': 'new_m', 'new_m_conv_w': 'new_m', 'new_m_conv_b': 'new_m', 'new_m_dt_bias': 'new_m', 'new_m_a_log': 'new_m', 'new_m_d_skip': 'new_m', 'new_m_ssd_norm_w': 'new_m', 'new_m_attn_sinks': 'new_m', 'new_m_w_out': 'new_m', 'new_m_ln_g': 'new_m', 'new_m_ln_b': 'new_m', 'new_v_w_in': 'new_v', 'new_v_conv_w': 'new_v', 'new_v_conv_b': 'new_v', 'new_v_dt_bias': 'new_v', 'new_v_a_log': 'new_v', 'new_v_d_skip': 'new_v', 'new_v_ssd_norm_w': 'new_v', 'new_v_attn_sinks': 'new_v', 'new_v_w_out': 'new_v', 'new_v_ln_g': 'new_v', 'new_v_ln_b': 'new_v'}


def _forward(args):
    return _fwd_reference(*[args[k] for k in FWD_PARAMS])


def _output_shape():
    out = _jax.eval_shape(lambda: _forward(_fwd_setup_inputs(0)))
    return out.shape, out.dtype

N_MICROBATCH = 1
ADAM_LR = 0.001
ADAM_B1 = 0.9
ADAM_B2 = 0.999
ADAM_EPS = 1e-08
ADAM_WD = 0.01
ADAM_STEP = 10
PER_EXAMPLE_BATCH_AXIS = {'x': 0, 'positions': 0, 'loss_target': 0}
SHARED_INPUTS = []
_WEIGHT_DTYPES = {'w_in': _jnp.float32, 'conv_w': _jnp.float32, 'conv_b': _jnp.float32, 'dt_bias': _jnp.float32, 'a_log': _jnp.float32, 'd_skip': _jnp.float32, 'ssd_norm_w': _jnp.float32, 'attn_sinks': _jnp.float32, 'w_out': _jnp.float32, 'ln_g': _jnp.float32, 'ln_b': _jnp.float32}
MOMENT_SCALE = {'w_in': 3.999599e-02, 'conv_w': 5.180834e-02, 'conv_b': 7.658579e-02, 'dt_bias': 1.267887e-01, 'a_log': 1.401364e-01, 'd_skip': 4.052877e-01, 'ssd_norm_w': 6.048443e-02, 'attn_sinks': 2.906346e-03, 'w_out': 1.049944e-01, 'ln_g': 3.198548e+01, 'ln_b': 1.949424e+00}


def _to_microbatches(a, axis):
    t = _jnp.moveaxis(a, axis, 0)
    t = t.reshape((N_MICROBATCH, t.shape[0] // N_MICROBATCH) + t.shape[1:])
    return _jnp.moveaxis(t, 1, axis + 1)


def setup_inputs(seed: int = 0) -> dict:
    inp = _fwd_setup_inputs(seed)
    key = _jax.random.fold_in(_jax.random.key(seed), 7919)
    shape, _ = _output_shape()
    out = dict(inp)
    out["loss_target"] = _jax.random.normal(_jax.random.fold_in(key, 0), shape, _jnp.float32)
    for i, name in enumerate(TWIN_WEIGHTS):
        w = inp[name].astype(_jnp.float32)
        if MOMENT_SCALE is None:
            s = _jnp.sqrt(_jnp.mean(_jnp.square(w)) + 1e-30)
        else:
            s = MOMENT_SCALE[name]
        km, kv = _jax.random.split(_jax.random.fold_in(key, i + 1))
        out[name] = w
        out["m_" + name] = s * _jax.random.normal(km, w.shape, _jnp.float32)
        out["v_" + name] = (s * s) * _jax.random.uniform(kv, w.shape, _jnp.float32, 0.5, 1.5)
    if N_MICROBATCH > 1:
        for name, axis in PER_EXAMPLE_BATCH_AXIS.items():
            out[name] = _to_microbatches(out[name], axis)
    return {'x': out['x'], 'positions': out['positions'], 'w_in': out['w_in'], 'conv_w': out['conv_w'], 'conv_b': out['conv_b'], 'dt_bias': out['dt_bias'], 'a_log': out['a_log'], 'd_skip': out['d_skip'], 'ssd_norm_w': out['ssd_norm_w'], 'attn_sinks': out['attn_sinks'], 'w_out': out['w_out'], 'ln_g': out['ln_g'], 'ln_b': out['ln_b'], 'loss_target': out['loss_target'], 'm_w_in': out['m_w_in'], 'm_conv_w': out['m_conv_w'], 'm_conv_b': out['m_conv_b'], 'm_dt_bias': out['m_dt_bias'], 'm_a_log': out['m_a_log'], 'm_d_skip': out['m_d_skip'], 'm_ssd_norm_w': out['m_ssd_norm_w'], 'm_attn_sinks': out['m_attn_sinks'], 'm_w_out': out['m_w_out'], 'm_ln_g': out['m_ln_g'], 'm_ln_b': out['m_ln_b'], 'v_w_in': out['v_w_in'], 'v_conv_w': out['v_conv_w'], 'v_conv_b': out['v_conv_b'], 'v_dt_bias': out['v_dt_bias'], 'v_a_log': out['v_a_log'], 'v_d_skip': out['v_d_skip'], 'v_ssd_norm_w': out['v_ssd_norm_w'], 'v_attn_sinks': out['v_attn_sinks'], 'v_w_out': out['v_w_out'], 'v_ln_g': out['v_ln_g'], 'v_ln_b': out['v_ln_b']}


def _loss(weights, diff, rest, loss_target):
    with _jax.named_scope("forward"):
        args = {**rest, TWIN_DIFF_INPUT: diff, **{k: w.astype(_WEIGHT_DTYPES[k]) for k, w in weights.items()}}
        y = _forward(args)
    with _jax.named_scope("loss_head"):
        err = _jnp.square(y.astype(_jnp.float32) - loss_target)
        return 0.5 * _jnp.sum(_jnp.mean(err, axis=-1)) if err.ndim else 0.5 * err


def _adamw(w, g, m, v):
    m = ADAM_B1 * m + (1.0 - ADAM_B1) * g
    v = ADAM_B2 * v + (1.0 - ADAM_B2) * _jnp.square(g)
    m_hat = m / (1.0 - ADAM_B1 ** ADAM_STEP)
    v_hat = v / (1.0 - ADAM_B2 ** ADAM_STEP)
    delta = -ADAM_LR * (m_hat / (_jnp.sqrt(v_hat) + ADAM_EPS) + ADAM_WD * w)
    return delta, m, v


def reference(x, positions, w_in, conv_w, conv_b, dt_bias, a_log, d_skip, ssd_norm_w, attn_sinks, w_out, ln_g, ln_b, loss_target, m_w_in, m_conv_w, m_conv_b, m_dt_bias, m_a_log, m_d_skip, m_ssd_norm_w, m_attn_sinks, m_w_out, m_ln_g, m_ln_b, v_w_in, v_conv_w, v_conv_b, v_dt_bias, v_a_log, v_d_skip, v_ssd_norm_w, v_attn_sinks, v_w_out, v_ln_g, v_ln_b):
    given = dict(x=x, positions=positions, w_in=w_in, conv_w=conv_w, conv_b=conv_b, dt_bias=dt_bias, a_log=a_log, d_skip=d_skip, ssd_norm_w=ssd_norm_w, attn_sinks=attn_sinks, w_out=w_out, ln_g=ln_g, ln_b=ln_b, loss_target=loss_target, m_w_in=m_w_in, m_conv_w=m_conv_w, m_conv_b=m_conv_b, m_dt_bias=m_dt_bias, m_a_log=m_a_log, m_d_skip=m_d_skip, m_ssd_norm_w=m_ssd_norm_w, m_attn_sinks=m_attn_sinks, m_w_out=m_w_out, m_ln_g=m_ln_g, m_ln_b=m_ln_b, v_w_in=v_w_in, v_conv_w=v_conv_w, v_conv_b=v_conv_b, v_dt_bias=v_dt_bias, v_a_log=v_a_log, v_d_skip=v_d_skip, v_ssd_norm_w=v_ssd_norm_w, v_attn_sinks=v_attn_sinks, v_w_out=v_w_out, v_ln_g=v_ln_g, v_ln_b=v_ln_b)
    weights = {n: given[n] for n in TWIN_WEIGHTS}
    shared = {n: given[n] for n in SHARED_INPUTS}
    per_example = {n: given[n] for n in ['x', 'positions']}
    grad_fn = _jax.value_and_grad(_loss, argnums=(0, 1))

    def one_microbatch(ex, loss_target):
        ex = dict(ex)
        diff = ex.pop(TWIN_DIFF_INPUT)
        return grad_fn(weights, diff, {**shared, **ex}, loss_target)

    if N_MICROBATCH == 1:
        loss, (grad_w, grad_x) = one_microbatch(per_example, given["loss_target"])
    else:
        def body(carry, xs):
            loss_sum, grad_sum = carry
            l_k, (gw_k, gx_k) = one_microbatch(xs[0], xs[1])
            with _jax.named_scope("update"):
                return (loss_sum + l_k, _jax.tree.map(_jnp.add, grad_sum, gw_k)), gx_k

        init = (_jnp.zeros((), _jnp.float32), _jax.tree.map(_jnp.zeros_like, weights))
        (loss, grad_w), grad_x = _jax.lax.scan(body, init, (per_example, given["loss_target"]))
    with _jax.named_scope("update"):
        delta_w, new_m, new_v = {}, {}, {}
        for n in TWIN_WEIGHTS:
            delta_w[n], new_m[n], new_v[n] = _adamw(weights[n], grad_w[n], given["m_" + n], given["v_" + n])
    return (loss, grad_x, *[grad_w[n] for n in TWIN_WEIGHTS], *[delta_w[n] for n in TWIN_WEIGHTS],
            *[new_m[n] for n in TWIN_WEIGHTS], *[new_v[n] for n in TWIN_WEIGHTS])
```

```python
import functools

import numpy as np
import jax
import jax.numpy as jnp
from jax import lax
from jax.experimental import pallas as pl
from jax.experimental.pallas import tpu as pltpu

F32 = jnp.float32
BF16 = jnp.bfloat16
MESH = pl.DeviceIdType.MESH

D_MODEL = 1024
D_SSD = 1024
D_ATT = 1024
D_MIX = 2048
SSD_HEADS = 16
SSD_P = 64
SSD_GROUPS = 2
SSD_R = 8
SSD_N = 128
D_BC = 256
D_XBC = 1536
CONV_K = 4
CHUNK = 128
ATT_HD = 64
ATT_QH = 16
ATT_KVH = 4
ATT_R = 4
D_KV = 256
WINDOW = 128
ROPE_THETA = 500000.0
ROPE_DIM = 16
ALPHA = 2.0 ** 0.25
LN_EPS = 1e-5
RMS_EPS = 1e-5
D_IN_PROJ = 5136
O_Z, O_XBC, O_DT, O_Q, O_K, O_V, O_G = 0, 1024, 2560, 2576, 3600, 3856, 4112
P_Z, P_G, P_Q, P_XBC, P_KV, P_DT, P_END = 0, 1024, 2048, 3072, 4608, 5120, 5248
DT_PAD = 128
N_CHIPS = 4
W_IN_COLS = D_IN_PROJ // N_CHIPS
W_OUT_ROWS = D_MIX // N_CHIPS
CONV_COLS = D_XBC // N_CHIPS

ADAM_LR = 0.001
ADAM_B1 = 0.9
ADAM_B2 = 0.999
ADAM_EPS = 1e-08
ADAM_WD = 0.01
ADAM_STEP = 10

VMEM_LIMIT = 56 * 1024 * 1024
NEG_BIG = -1e30
HI = lax.Precision.HIGHEST


def _cparams(sem=None, **kw):
    if sem is not None:
        kw["dimension_semantics"] = sem
    return pltpu.CompilerParams(vmem_limit_bytes=VMEM_LIMIT, **kw)


def _dot(a, b):
    return jnp.dot(a, b, preferred_element_type=F32)


def _dot_nt(a, b):
    return lax.dot_general(a, b, (((1,), (1,)), ((), ())), preferred_element_type=F32)


def _dot_tn(a, b):
    return lax.dot_general(a, b, (((0,), (0,)), ((), ())), preferred_element_type=F32)


def _bf(a):
    return a.astype(BF16)


def _iota2(shape, dim):
    return lax.broadcasted_iota(jnp.int32, shape, dim)


def _to_rows(col):
    k = col.shape[1]
    eye = (_iota2((k, k), 0) == _iota2((k, k), 1)).astype(F32)
    return lax.dot_general(eye, col, (((1,), (1,)), ((), ())), preferred_element_type=F32, precision=HI)


def _to_cols(row):
    n = row.shape[1]
    eye = (_iota2((n, n), 0) == _iota2((n, n), 1)).astype(F32)
    return lax.dot_general(eye, row, (((1,), (1,)), ((), ())), preferred_element_type=F32, precision=HI)


def _sigmoid(x):
    return jax.nn.sigmoid(x)


def _in_proj(x, w):
    L = x.shape[0]
    tm = 256
    widths = (D_SSD, D_ATT, D_ATT, D_XBC, 2 * D_KV, DT_PAD)
    offs = (P_Z, P_G, P_Q, P_XBC, P_KV, P_DT)

    def body(x_ref, w_ref, z_ref, g_ref, q_ref, xbc_ref, kv_ref, dt_ref, xb_ref):
        xb = _bf(x_ref[...])
        xb_ref[...] = xb
        for o_ref, off, wd in zip((z_ref, g_ref, q_ref, xbc_ref, kv_ref, dt_ref), offs, widths):
            o_ref[...] = _dot(xb, w_ref[:, off:off + wd])

    row = lambda wd: pl.BlockSpec((tm, wd), lambda i: (i, 0))
    return pl.pallas_call(
        body, name="in_proj", grid=(L // tm,),
        in_specs=[row(D_MODEL), pl.BlockSpec((D_MODEL, P_END), lambda i: (0, 0))],
        out_specs=[row(wd) for wd in widths] + [row(D_MODEL)],
        out_shape=[jax.ShapeDtypeStruct((L, wd), F32) for wd in widths] + [jax.ShapeDtypeStruct((L, D_MODEL), BF16)],
        compiler_params=_cparams(("parallel",)),
    )(x, w)


def _matmul_tn(a, b, tn, name):
    K, M = a.shape
    N = b.shape[1]
    tk = 512
    nk = K // tk

    def body(a_ref, b_ref, o_ref, acc_ref):
        k = pl.program_id(1)

        @pl.when(k == 0)
        def _():
            acc_ref[...] = jnp.zeros_like(acc_ref)

        acc_ref[...] += _dot_tn(_bf(a_ref[...]), _bf(b_ref[...]))

        @pl.when(k == nk - 1)
        def _():
            o_ref[...] = acc_ref[...]

    return pl.pallas_call(
        body, name=name, grid=(N // tn, nk),
        in_specs=[pl.BlockSpec((tk, M), lambda j, k: (k, 0)), pl.BlockSpec((tk, tn), lambda j, k: (k, j))],
        out_specs=pl.BlockSpec((M, tn), lambda j, k: (0, j)),
        out_shape=jax.ShapeDtypeStruct((M, N), F32),
        scratch_shapes=[pltpu.VMEM((M, tn), F32)],
        compiler_params=_cparams(("parallel", "arbitrary")),
    )(a, b)


def _grad_x(dr, dz, dg, dq, dxbc, dkv, ddt, w):
    L = dr.shape[0]
    tm = 256
    widths = (D_SSD, D_ATT, D_ATT, D_XBC, 2 * D_KV, DT_PAD)
    offs = (P_Z, P_G, P_Q, P_XBC, P_KV, P_DT)

    def body(dr_ref, dz_ref, dg_ref, dq_ref, dxbc_ref, dkv_ref, ddt_ref, w_ref, o_ref):
        acc = ALPHA * dr_ref[...]
        for p_ref, off, wd in zip((dz_ref, dg_ref, dq_ref, dxbc_ref, dkv_ref, ddt_ref), offs, widths):
            acc = acc + _dot_nt(_bf(p_ref[...]), w_ref[:, off:off + wd])
        o_ref[...] = acc

    row = lambda wd: pl.BlockSpec((tm, wd), lambda i: (i, 0))
    return pl.pallas_call(
        body, name="grad_x", grid=(L // tm,),
        in_specs=[row(D_MODEL)] + [row(wd) for wd in widths] + [pl.BlockSpec((D_MODEL, P_END), lambda i: (0, 0))],
        out_specs=row(D_MODEL),
        out_shape=jax.ShapeDtypeStruct((L, D_MODEL), F32),
        compiler_params=_cparams(("parallel",)),
    )(dr, dz, dg, dq, dxbc, dkv, ddt, w)


def _ssd_chunk_pre(first, xbc_ref, tail_ref, dt_ref, cw_ref, cb_ref, dtb_ref, alog_ref, ext):
    tail = jnp.where(first, 0.0, tail_ref[...])
    ext[0:8, :] = tail
    ext[8:8 + CHUNK, :] = xbc_ref[...]
    u = cb_ref[...] + cw_ref[0:1, :] * ext[pl.ds(5, CHUNK), :]
    for k in range(1, CONV_K):
        u = u + cw_ref[k:k + 1, :] * ext[pl.ds(5 + k, CHUNK), :]
    sig = _sigmoid(u)
    xbc = u * sig
    dtraw = dt_ref[:, 0:SSD_HEADS] + dtb_ref[...]
    dt = jax.nn.softplus(dtraw)
    A = -jnp.exp(alog_ref[...])
    a = dt * A
    tril = (_iota2((CHUNK, CHUNK), 0) >= _iota2((CHUNK, CHUNK), 1)).astype(F32)
    acs = jnp.dot(tril, a, preferred_element_type=F32, precision=HI)
    acs_row = _to_rows(acs)
    return u, sig, xbc, dtraw, dt, A, acs, acs_row


def _ssd_fwd(z, xbc, dtp, conv_w, conv_b, dt_bias, a_log, d_skip, norm_w):
    L = z.shape[0]
    nc = L // CHUNK

    def body(z_ref, xbc_ref, tail_ref, dt_ref, cw_ref, cb_ref, dtb_ref, alog_ref, dsk_ref, nw_ref,
             y_ref, ypre_ref, prev_ref, state, ext, ybuf):
        c = pl.program_id(0)

        @pl.when(c == 0)
        def _():
            state[...] = jnp.zeros_like(state)

        u, sig, xbcv, dtraw, dt, A, acs, acs_row = _ssd_chunk_pre(
            c == 0, xbc_ref, tail_ref, dt_ref, cw_ref, cb_ref, dtb_ref, alog_ref, ext)
        prev_ref[0] = state[...]
        causal = _iota2((CHUNK, CHUNK), 0) >= _iota2((CHUNK, CHUNK), 1)
        alast = acs[CHUNK - 1:CHUNK, :]
        for g in range(SSD_GROUPS):
            Bg = _bf(xbcv[:, D_SSD + SSD_N * g:D_SSD + SSD_N * (g + 1)])
            Cg = _bf(xbcv[:, D_SSD + D_BC + SSD_N * g:D_SSD + D_BC + SSD_N * (g + 1)])
            cb = _dot_nt(Cg, Bg)
            for r in range(SSD_R):
                h = g * SSD_R + r
                hs = slice(SSD_P * h, SSD_P * (h + 1))
                acs_c = acs[:, h:h + 1]
                seg = acs_c - acs_row[h:h + 1, :]
                Lm = jnp.where(causal, jnp.exp(jnp.where(causal, seg, 0.0)), 0.0)
                M = cb * Lm
                xh = xbcv[:, hs]
                X = xh * dt[:, h:h + 1]
                prev_h = state[hs, :]
                ydiag = _dot(_bf(M), _bf(X))
                yoff = _dot_nt(Cg, _bf(prev_h)) * jnp.exp(acs_c)
                al = alast[:, h:h + 1]
                Xd = X * jnp.exp(al - acs_c)
                state[hs, :] = prev_h * jnp.exp(al) + _dot_tn(_bf(Xd), Bg)
                ybuf[:, hs] = ydiag + yoff + dsk_ref[:, h:h + 1] * xh
        y = ybuf[...]
        ypre_ref[...] = y
        zv = z_ref[...]
        yf = y * (zv * _sigmoid(zv))
        half = D_SSD // SSD_GROUPS
        for g in range(SSD_GROUPS):
            gs = slice(half * g, half * (g + 1))
            yg = yf[:, gs]
            ms = jnp.mean(yg * yg, axis=-1, keepdims=True)
            y_ref[:, gs] = yg * lax.rsqrt(ms + RMS_EPS) * nw_ref[:, gs]

    full = lambda shape: pl.BlockSpec(shape, lambda c: (0, 0))
    return pl.pallas_call(
        body, name="ssd_fwd", grid=(nc,),
        in_specs=[
            pl.BlockSpec((CHUNK, D_SSD), lambda c: (c, 0)),
            pl.BlockSpec((CHUNK, D_XBC), lambda c: (c, 0)),
            pl.BlockSpec((8, D_XBC), lambda c: (jnp.maximum(c * (CHUNK // 8) - 1, 0), 0)),
            pl.BlockSpec((CHUNK, DT_PAD), lambda c: (c, 0)),
            full((CONV_K, D_XBC)), full((1, D_XBC)), full((1, SSD_HEADS)), full((1, SSD_HEADS)), full((1, SSD_HEADS)),
            full((1, D_SSD)),
        ],
        out_specs=[
            pl.BlockSpec((CHUNK, D_SSD), lambda c: (c, 0)),
            pl.BlockSpec((CHUNK, D_SSD), lambda c: (c, 0)),
            pl.BlockSpec((1, SSD_HEADS * SSD_P, SSD_N), lambda c: (c, 0, 0)),
        ],
        out_shape=[
            jax.ShapeDtypeStruct((L, D_SSD), F32),
            jax.ShapeDtypeStruct((L, D_SSD), F32),
            jax.ShapeDtypeStruct((nc, SSD_HEADS * SSD_P, SSD_N), F32),
        ],
        scratch_shapes=[
            pltpu.VMEM((SSD_HEADS * SSD_P, SSD_N), F32),
            pltpu.VMEM((CHUNK + 8, D_XBC), F32),
            pltpu.VMEM((CHUNK, D_SSD), F32),
        ],
        compiler_params=_cparams(("arbitrary",)),
    )(z, xbc, xbc, dtp, conv_w, conv_b, dt_bias, a_log, d_skip, norm_w)


def _ssd_bwd(dy, z, ypre, xbc, dtp, prev, conv_w, conv_b, dt_bias, a_log, d_skip, norm_w):
    L = z.shape[0]
    nc = L // CHUNK

    def body(dy_ref, z_ref, ypre_ref, xbc_ref, tail_ref, dt_ref, prev_ref, cw_ref, cb_ref, dtb_ref, alog_ref, dsk_ref,
             nw_ref, dz_ref, dxbc_ref, ddt_ref, gcw_ref, gcb_ref, gdtb_ref, galog_ref, gdsk_ref, gnw_ref,
             dstate, dhead, ext, ext2, dpost):
        i = pl.program_id(0)
        c = nc - 1 - i

        @pl.when(i == 0)
        def _():
            dstate[...] = jnp.zeros_like(dstate)
            dhead[...] = jnp.zeros_like(dhead)
            gcw_ref[...] = jnp.zeros_like(gcw_ref)
            gcb_ref[...] = jnp.zeros_like(gcb_ref)
            gdtb_ref[...] = jnp.zeros_like(gdtb_ref)
            galog_ref[...] = jnp.zeros_like(galog_ref)
            gdsk_ref[...] = jnp.zeros_like(gdsk_ref)
            gnw_ref[...] = jnp.zeros_like(gnw_ref)

        u, sig, xbcv, dtraw, dt, A, acs, acs_row = _ssd_chunk_pre(
            c == 0, xbc_ref, tail_ref, dt_ref, cw_ref, cb_ref, dtb_ref, alog_ref, ext)

        zv = z_ref[...]
        ypre = ypre_ref[...]
        dyn = dy_ref[...]
        sz = _sigmoid(zv)
        silu_z = zv * sz
        yf = ypre * silu_z
        half = D_SSD // SSD_GROUPS
        dyf_parts = []
        for g in range(SSD_GROUPS):
            gs = slice(half * g, half * (g + 1))
            yg = yf[:, gs]
            rstd = lax.rsqrt(jnp.mean(yg * yg, axis=-1, keepdims=True) + RMS_EPS)
            dout = dyn[:, gs]
            gnw_ref[:, gs] += jnp.sum(dout * yg * rstd, axis=0, keepdims=True)
            dyhat = dout * nw_ref[:, gs]
            dyf_parts.append(rstd * (dyhat - yg * (rstd * rstd) * jnp.mean(dyhat * yg, axis=-1, keepdims=True)))
        dyf = jnp.concatenate(dyf_parts, axis=1)
        dz_ref[...] = dyf * ypre * (sz * (1.0 + zv * (1.0 - sz)))
        dypre = dyf * silu_z

        causal = _iota2((CHUNK, CHUNK), 0) >= _iota2((CHUNK, CHUNK), 1)
        alast = acs[CHUNK - 1:CHUNK, :]
        lane16 = _iota2((1, SSD_HEADS), 1)
        sub16 = _iota2((SSD_HEADS, 1), 0)
        dacs_col = jnp.zeros((CHUNK, SSD_HEADS), F32)
        dacs_row = jnp.zeros((SSD_HEADS, CHUNK), F32)
        ddt_col = jnp.zeros((CHUNK, SSD_HEADS), F32)
        dalast = jnp.zeros((1, SSD_HEADS), F32)
        gdsk = jnp.zeros((1, SSD_HEADS), F32)
        for g in range(SSD_GROUPS):
            bs = slice(D_SSD + SSD_N * g, D_SSD + SSD_N * (g + 1))
            cs = slice(D_SSD + D_BC + SSD_N * g, D_SSD + D_BC + SSD_N * (g + 1))
            Bg = _bf(xbcv[:, bs])
            Cg = _bf(xbcv[:, cs])
            cb = _dot_nt(Cg, Bg)
            dcb = jnp.zeros((CHUNK, CHUNK), F32)
            dB = jnp.zeros((CHUNK, SSD_N), F32)
            dC = jnp.zeros((CHUNK, SSD_N), F32)
            for r in range(SSD_R):
                h = g * SSD_R + r
                hs = slice(SSD_P * h, SSD_P * (h + 1))
                onehot = (lane16 == h).astype(F32)
                acs_c = acs[:, h:h + 1]
                seg = acs_c - acs_row[h:h + 1, :]
                Lm = jnp.where(causal, jnp.exp(jnp.where(causal, seg, 0.0)), 0.0)
                M = cb * Lm
                xh = xbcv[:, hs]
                dth = dt[:, h:h + 1]
                X = xh * dth
                Xb = _bf(X)
                dyh = dypre[:, hs]
                dyb = _bf(dyh)
                prev_h = prev_ref[0, hs, :]
                prevb = _bf(prev_h)
                dnext = dstate[hs, :]
                dnextb = _bf(dnext)
                al = alast[:, h:h + 1]
                eacs = jnp.exp(acs_c)
                eal = jnp.exp(al)
                dsd = jnp.exp(al - acs_c)
                G = _bf(dyh * eacs)
                dstate[hs, :] = dnext * eal + _dot_tn(G, Cg)
                dC = dC + _dot(G, prevb)
                yoff = _dot_nt(Cg, prevb) * eacs
                dacs_h = jnp.sum(dyh * yoff, axis=-1, keepdims=True)
                BdN = _dot_nt(Bg, dnextb)
                dX = dsd * BdN
                dB = dB + _dot(_bf(X * dsd), dnextb)
                t = jnp.sum(X * BdN, axis=-1, keepdims=True) * dsd
                dacs_h = dacs_h - t
                dal = jnp.sum(t, axis=0, keepdims=True) + jnp.sum(
                    jnp.sum(dnext * prev_h, axis=-1, keepdims=True), axis=0, keepdims=True) * eal
                dM = _dot_nt(dyb, Xb)
                dX = dX + _dot_tn(_bf(M), dyb)
                dseg = dM * M
                dcb = dcb + dM * Lm
                dacs_h = dacs_h + jnp.sum(dseg, axis=-1, keepdims=True)
                dacs_row = dacs_row - jnp.sum(dseg, axis=0, keepdims=True) * (sub16 == h).astype(F32)
                dacs_col = dacs_col + dacs_h * onehot
                dalast = dalast + dal * onehot
                ddt_col = ddt_col + jnp.sum(dX * xh, axis=-1, keepdims=True) * onehot
                gdsk = gdsk + jnp.sum(jnp.sum(dyh * xh, axis=-1, keepdims=True), axis=0, keepdims=True) * onehot
                dpost[:, hs] = dX * dth + dsk_ref[:, h:h + 1] * dyh
            dcbb = _bf(dcb)
            dpost[:, bs] = dB + _dot_tn(dcbb, Cg)
            dpost[:, cs] = dC + _dot(dcbb, Bg)

        is_last = (_iota2((CHUNK, 1), 0) == CHUNK - 1).astype(F32)
        dacs = dacs_col + _to_cols(dacs_row) + is_last * dalast
        triu = (_iota2((CHUNK, CHUNK), 0) <= _iota2((CHUNK, CHUNK), 1)).astype(F32)
        da = jnp.dot(triu, dacs, preferred_element_type=F32, precision=HI)
        ddt_tot = ddt_col + da * A
        galog_ref[...] += jnp.sum(da * dt, axis=0, keepdims=True) * A
        ddtraw = ddt_tot * _sigmoid(dtraw)
        gdtb_ref[...] += jnp.sum(ddtraw, axis=0, keepdims=True)
        gdsk_ref[...] += gdsk
        ddt_ref[...] = jnp.zeros_like(ddt_ref)
        ddt_ref[:, 0:SSD_HEADS] = ddtraw

        dconv = dpost[...] * (sig * (1.0 + u * (1.0 - sig)))
        gcb_ref[...] += jnp.sum(dconv, axis=0, keepdims=True)
        for k in range(CONV_K):
            gcw_ref[k:k + 1, :] += jnp.sum(dconv * ext[pl.ds(5 + k, CHUNK), :], axis=0, keepdims=True)
        ext2[0:CHUNK, :] = dconv
        ext2[CHUNK:CHUNK + 8, :] = dhead[...]
        dx = cw_ref[CONV_K - 1:CONV_K, :] * dconv
        for k in range(CONV_K - 1):
            dx = dx + cw_ref[k:k + 1, :] * ext2[pl.ds(CONV_K - 1 - k, CHUNK), :]
        dxbc_ref[...] = dx
        dhead[...] = dconv[0:8, :]

    full = lambda shape: pl.BlockSpec(shape, lambda i: (0, 0))
    rev = lambda wd: pl.BlockSpec((CHUNK, wd), lambda i: (nc - 1 - i, 0))
    return pl.pallas_call(
        body, name="ssd_bwd", grid=(nc,),
        in_specs=[
            rev(D_SSD), rev(D_SSD), rev(D_SSD), rev(D_XBC),
            pl.BlockSpec((8, D_XBC), lambda i: (jnp.maximum((nc - 1 - i) * (CHUNK // 8) - 1, 0), 0)),
            rev(DT_PAD),
            pl.BlockSpec((1, SSD_HEADS * SSD_P, SSD_N), lambda i: (nc - 1 - i, 0, 0)),
            full((CONV_K, D_XBC)), full((1, D_XBC)), full((1, SSD_HEADS)), full((1, SSD_HEADS)), full((1, SSD_HEADS)),
            full((1, D_SSD)),
        ],
        out_specs=[
            rev(D_SSD), rev(D_XBC), rev(DT_PAD),
            full((CONV_K, D_XBC)), full((1, D_XBC)), full((1, SSD_HEADS)), full((1, SSD_HEADS)), full((1, SSD_HEADS)),
            full((1, D_SSD)),
        ],
        out_shape=[
            jax.ShapeDtypeStruct((L, D_SSD), F32), jax.ShapeDtypeStruct((L, D_XBC), F32),
            jax.ShapeDtypeStruct((L, DT_PAD), F32),
            jax.ShapeDtypeStruct((CONV_K, D_XBC), F32), jax.ShapeDtypeStruct((1, D_XBC), F32),
            jax.ShapeDtypeStruct((1, SSD_HEADS), F32), jax.ShapeDtypeStruct((1, SSD_HEADS), F32),
            jax.ShapeDtypeStruct((1, SSD_HEADS), F32), jax.ShapeDtypeStruct((1, D_SSD), F32),
        ],
        scratch_shapes=[
            pltpu.VMEM((SSD_HEADS * SSD_P, SSD_N), F32),
            pltpu.VMEM((8, D_XBC), F32),
            pltpu.VMEM((CHUNK + 8, D_XBC), F32),
            pltpu.VMEM((CHUNK + 8, D_XBC), F32),
            pltpu.VMEM((CHUNK, D_XBC), F32),
        ],
        compiler_params=_cparams(("arbitrary",)),
    )(dy, z, ypre, xbc, xbc, dtp, prev, conv_w, conv_b, dt_bias, a_log, d_skip, norm_w)


def _rope_tables(pos_ref, inv_ref):
    ang = pos_ref[...].astype(F32) * inv_ref[...]
    d = _iota2((1, 2 * ATT_HD), 1) % ATT_HD
    s = jnp.sin(ang)
    return jnp.cos(ang), jnp.where(d < ROPE_DIM // 2, -s, 0.0), jnp.where((d >= ROPE_DIM // 2) & (d < ROPE_DIM), s, 0.0)


def _rope(t, tabs):
    c, s1, s2 = tabs
    n = t.shape[1]
    rep = n // c.shape[1]
    return (t * jnp.tile(c, (1, rep)) + pltpu.roll(t, n - ROPE_DIM // 2, 1) * jnp.tile(s1, (1, rep))
            + pltpu.roll(t, ROPE_DIM // 2, 1) * jnp.tile(s2, (1, rep)))


def _rope_t(t, tabs):
    c, s1, s2 = tabs
    n = t.shape[1]
    rep = n // c.shape[1]
    return (t * jnp.tile(c, (1, rep)) + pltpu.roll(t * jnp.tile(s1, (1, rep)), ROPE_DIM // 2, 1)
            + pltpu.roll(t * jnp.tile(s2, (1, rep)), n - ROPE_DIM // 2, 1))


def _swa_mask(first):
    qi = _iota2((WINDOW, 2 * WINDOW), 0)
    si = _iota2((WINDOW, 2 * WINDOW), 1)
    band = (si > qi) & (si <= qi + WINDOW)
    return band & (jnp.logical_not(first) | (si >= WINDOW))


def _swa_fwd(q, g, kv, pos, inv, sinks):
    L = q.shape[0]
    nb = L // WINDOW
    scale = ATT_HD ** -0.5

    def body(q_ref, g_ref, kvc_ref, kvp_ref, posc_ref, posp_ref, inv_ref, sink_ref, y_ref, o_ref, lse_ref):
        n = pl.program_id(0)
        tc = _rope_tables(posc_ref, inv_ref)
        tp = _rope_tables(posp_ref, inv_ref)
        qr = _rope(q_ref[...], tc)
        kc = _rope(kvc_ref[:, 0:D_KV], tc)
        kp = _rope(kvp_ref[:, 0:D_KV], tp)
        kk = _bf(jnp.concatenate([kp, kc], axis=0))
        vv = _bf(jnp.concatenate([kvp_ref[:, D_KV:2 * D_KV], kvc_ref[:, D_KV:2 * D_KV]], axis=0))
        valid = _swa_mask(n == 0)
        for j in range(ATT_KVH):
            js = slice(ATT_HD * j, ATT_HD * (j + 1))
            kj = kk[:, js]
            vj = vv[:, js]
            for r in range(ATT_R):
                h = j * ATT_R + r
                hs = slice(ATT_HD * h, ATT_HD * (h + 1))
                s = _dot_nt(_bf(qr[:, hs]), kj) * scale
                s = jnp.where(valid, s, NEG_BIG)
                sink = sink_ref[:, h:h + 1]
                m = jnp.maximum(jnp.max(s, axis=-1, keepdims=True), sink)
                p = jnp.exp(s - m)
                denom = jnp.sum(p, axis=-1, keepdims=True) + jnp.exp(sink - m)
                o_ref[:, hs] = _dot(_bf(p / denom), vj)
                lse_ref[:, h:h + 1] = m + jnp.log(denom)
        gv = g_ref[...]
        y_ref[...] = o_ref[...] * (gv * _sigmoid(gv))

    cur = lambda wd: pl.BlockSpec((WINDOW, wd), lambda n: (n, 0))
    prv = lambda wd: pl.BlockSpec((WINDOW, wd), lambda n: (jnp.maximum(n - 1, 0), 0))
    return pl.pallas_call(
        body, name="swa_fwd", grid=(nb,),
        in_specs=[cur(D_ATT), cur(D_ATT), cur(2 * D_KV), prv(2 * D_KV), cur(1), prv(1),
                  pl.BlockSpec((1, 2 * ATT_HD), lambda n: (0, 0)), pl.BlockSpec((1, ATT_QH), lambda n: (0, 0))],
        out_specs=[cur(D_ATT), cur(D_ATT), cur(ATT_QH)],
        out_shape=[jax.ShapeDtypeStruct((L, D_ATT), F32), jax.ShapeDtypeStruct((L, D_ATT), F32),
                   jax.ShapeDtypeStruct((L, ATT_QH), F32)],
        compiler_params=_cparams(("parallel",)),
    )(q, g, kv, kv, pos, pos, inv, sinks)


def _swa_bwd(dy, q, g, kv, o, lse, pos, inv, sinks):
    L = q.shape[0]
    nb = L // WINDOW
    scale = ATT_HD ** -0.5

    def body(dy_ref, q_ref, g_ref, kvc_ref, kvp_ref, o_ref, lse_ref, posc_ref, posp_ref, inv_ref, sink_ref,
             dq_ref, dg_ref, dkv_ref, dsink_ref, carry, dqbuf, dkbuf, dvbuf):
        n = pl.program_id(0)

        @pl.when(n == 0)
        def _():
            dsink_ref[...] = jnp.zeros_like(dsink_ref)

        @pl.when(n < nb)
        def _():
            tc = _rope_tables(posc_ref, inv_ref)
            tp = _rope_tables(posp_ref, inv_ref)
            qr = _rope(q_ref[...], tc)
            kc = _rope(kvc_ref[:, 0:D_KV], tc)
            kp = _rope(kvp_ref[:, 0:D_KV], tp)
            kk = _bf(jnp.concatenate([kp, kc], axis=0))
            vv = _bf(jnp.concatenate([kvp_ref[:, D_KV:2 * D_KV], kvc_ref[:, D_KV:2 * D_KV]], axis=0))
            valid = _swa_mask(n == 0)
            gv = g_ref[...]
            sg = _sigmoid(gv)
            dyv = dy_ref[...]
            ov = o_ref[...]
            dg_ref[...] = dyv * ov * (sg * (1.0 + gv * (1.0 - sg)))
            do = dyv * (gv * sg)
            lane16 = _iota2((1, ATT_QH), 1)
            dsink = jnp.zeros((1, ATT_QH), F32)
            for j in range(ATT_KVH):
                js = slice(ATT_HD * j, ATT_HD * (j + 1))
                kj = kk[:, js]
                vj = vv[:, js]
                dkj = jnp.zeros((2 * WINDOW, ATT_HD), F32)
                dvj = jnp.zeros((2 * WINDOW, ATT_HD), F32)
                for r in range(ATT_R):
                    h = j * ATT_R + r
                    hs = slice(ATT_HD * h, ATT_HD * (h + 1))
                    qh = _bf(qr[:, hs])
                    doh = do[:, hs]
                    dohb = _bf(doh)
                    lse_h = lse_ref[:, h:h + 1]
                    s = _dot_nt(qh, kj) * scale
                    p = jnp.exp(jnp.where(valid, s, NEG_BIG) - lse_h)
                    delta = jnp.sum(doh * ov[:, hs], axis=-1, keepdims=True)
                    dS = _bf(p * (_dot_nt(dohb, vj) - delta))
                    dqbuf[:, hs] = _dot(dS, kj) * scale
                    dkj = dkj + _dot_tn(dS, qh) * scale
                    dvj = dvj + _dot_tn(_bf(p), dohb)
                    psink = jnp.exp(sink_ref[:, h:h + 1] - lse_h)
                    dsink = dsink - jnp.sum(psink * delta, axis=0, keepdims=True) * (lane16 == h).astype(F32)
                dkbuf[:, js] = dkj
                dvbuf[:, js] = dvj
            dsink_ref[...] += dsink
            dq_ref[...] = _rope_t(dqbuf[...], tc)
            dkp = _rope_t(dkbuf[0:WINDOW, :], tp)
            dkc = _rope_t(dkbuf[WINDOW:2 * WINDOW, :], tc)

            @pl.when(n > 0)
            def _():
                dkv_ref[:, 0:D_KV] = carry[:, 0:D_KV] + dkp
                dkv_ref[:, D_KV:2 * D_KV] = carry[:, D_KV:2 * D_KV] + dvbuf[0:WINDOW, :]

            carry[:, 0:D_KV] = dkc
            carry[:, D_KV:2 * D_KV] = dvbuf[WINDOW:2 * WINDOW, :]

        @pl.when(n == nb)
        def _():
            dkv_ref[...] = carry[...]

    last = nb - 1
    cur = lambda wd: pl.BlockSpec((WINDOW, wd), lambda n: (jnp.minimum(n, last), 0))
    prv = lambda wd: pl.BlockSpec((WINDOW, wd), lambda n: (jnp.maximum(jnp.minimum(n, last) - 1, 0), 0))
    return pl.pallas_call(
        body, name="swa_bwd", grid=(nb + 1,),
        in_specs=[cur(D_ATT), cur(D_ATT), cur(D_ATT), cur(2 * D_KV), prv(2 * D_KV), cur(D_ATT), cur(ATT_QH), cur(1), prv(1),
                  pl.BlockSpec((1, 2 * ATT_HD), lambda n: (0, 0)), pl.BlockSpec((1, ATT_QH), lambda n: (0, 0))],
        out_specs=[cur(D_ATT), cur(D_ATT),
                   pl.BlockSpec((WINDOW, 2 * D_KV), lambda n: (jnp.maximum(n - 1, 0), 0)),
                   pl.BlockSpec((1, ATT_QH), lambda n: (0, 0))],
        out_shape=[jax.ShapeDtypeStruct((L, D_ATT), F32), jax.ShapeDtypeStruct((L, D_ATT), F32),
                   jax.ShapeDtypeStruct((L, 2 * D_KV), F32), jax.ShapeDtypeStruct((1, ATT_QH), F32)],
        scratch_shapes=[pltpu.VMEM((WINDOW, 2 * D_KV), F32), pltpu.VMEM((WINDOW, D_ATT), F32),
                        pltpu.VMEM((2 * WINDOW, D_KV), F32), pltpu.VMEM((2 * WINDOW, D_KV), F32)],
        compiler_params=_cparams(("arbitrary",)),
    )(dy, q, g, kv, kv, o, lse, pos, pos, inv, sinks)


def _out_ln_loss(y_ssd, y_att, x, target, w_out, ln_g, ln_b):
    L = x.shape[0]
    tm = 256
    inv_d = 1.0 / D_MODEL

    def body(ys_ref, ya_ref, x_ref, t_ref, w_ref, g_ref, b_ref, dr_ref, dys_ref, dya_ref, loss_ref, gg_ref, gb_ref):
        i = pl.program_id(0)

        @pl.when(i == 0)
        def _():
            loss_ref[...] = jnp.zeros_like(loss_ref)
            gg_ref[...] = jnp.zeros_like(gg_ref)
            gb_ref[...] = jnp.zeros_like(gb_ref)

        h = _dot(_bf(ys_ref[...]), w_ref[0:D_SSD, :]) + _dot(_bf(ya_ref[...]), w_ref[D_SSD:D_MIX, :])
        r = ALPHA * x_ref[...] + h
        mu = jnp.mean(r, axis=-1, keepdims=True)
        xc = r - mu
        rstd = lax.rsqrt(jnp.mean(xc * xc, axis=-1, keepdims=True) + LN_EPS)
        xhat = xc * rstd
        gam = g_ref[...]
        diff = xhat * gam + b_ref[...] - t_ref[...]
        part = jnp.sum(jnp.sum(diff * diff, axis=-1, keepdims=True), axis=0, keepdims=True)
        loss_ref[...] += (0.5 * inv_d) * part
        dout = diff * inv_d
        gg_ref[...] += jnp.sum(dout * xhat, axis=0, keepdims=True)
        gb_ref[...] += jnp.sum(dout, axis=0, keepdims=True)
        dxh = dout * gam
        dr = rstd * (dxh - jnp.mean(dxh, axis=-1, keepdims=True) - xhat * jnp.mean(dxh * xhat, axis=-1, keepdims=True))
        dr_ref[...] = dr
        drb = _bf(dr)
        dys_ref[...] = _dot_nt(drb, w_ref[0:D_SSD, :])
        dya_ref[...] = _dot_nt(drb, w_ref[D_SSD:D_MIX, :])

    row = pl.BlockSpec((tm, D_MODEL), lambda i: (i, 0))
    vec = pl.BlockSpec((1, D_MODEL), lambda i: (0, 0))
    return pl.pallas_call(
        body, name="out_ln_loss", grid=(L // tm,),
        in_specs=[row, row, row, row, pl.BlockSpec((D_MIX, D_MODEL), lambda i: (0, 0)), vec, vec],
        out_specs=[row, row, row, pl.BlockSpec((1, 128), lambda i: (0, 0)), vec, vec],
        out_shape=[jax.ShapeDtypeStruct((L, D_MODEL), F32)] * 3 + [jax.ShapeDtypeStruct((1, 128), F32)]
        + [jax.ShapeDtypeStruct((1, D_MODEL), F32)] * 2,
        compiler_params=_cparams(("arbitrary",)),
    )(y_ssd, y_att, x, target, w_out, ln_g, ln_b)


def _local_step(x, pos, target, w, w_out, conv_w, conv_b, dt_bias, a_log, d_skip, norm_w, sinks, ln_g, ln_b):
    inv8 = ROPE_THETA ** (-jnp.arange(0, ROPE_DIM, 2, dtype=F32) / ROPE_DIM)
    inv = jnp.tile(jnp.concatenate([inv8, inv8, jnp.zeros((ATT_HD - ROPE_DIM,), F32)]), 2).reshape(1, 2 * ATT_HD)

    z, g, q, xbc, kv, dtp, xb = _in_proj(x, w)
    y_ssd, y_pre, prev = _ssd_fwd(z, xbc, dtp, conv_w, conv_b, dt_bias, a_log, d_skip, norm_w)
    y_att, o, lse = _swa_fwd(q, g, kv, pos, inv, sinks)
    dr, dy_ssd, dy_att, loss, g_ln_g, g_ln_b = _out_ln_loss(y_ssd, y_att, x, target, w_out, ln_g, ln_b)
    gw_out_ssd = _matmul_tn(y_ssd, dr, 1024, "gw_out_ssd")
    gw_out_att = _matmul_tn(y_att, dr, 1024, "gw_out_att")
    dq, dg, dkv, g_sinks = _swa_bwd(dy_att, q, g, kv, o, lse, pos, inv, sinks)
    dz, dxbc, ddt, g_conv_w, g_conv_b, g_dt_bias, g_a_log, g_d_skip, g_norm_w = _ssd_bwd(
        dy_ssd, z, y_pre, xbc, dtp, prev, conv_w, conv_b, dt_bias, a_log, d_skip, norm_w)
    grad_x = _grad_x(dr, dz, dg, dq, dxbc, dkv, ddt, w)
    gw_z = _matmul_tn(xb, dz, 1024, "gw_z")
    gw_g = _matmul_tn(xb, dg, 1024, "gw_g")
    gw_q = _matmul_tn(xb, dq, 1024, "gw_q")
    gw_xbc = _matmul_tn(xb, dxbc, 768, "gw_xbc")
    gw_kv = _matmul_tn(xb, dkv, 512, "gw_kv")
    gw_dt = _matmul_tn(xb, ddt, 128, "gw_dt")
    gw_in = jnp.concatenate([gw_z, gw_xbc, gw_dt[:, 0:SSD_HEADS], gw_q, gw_kv, gw_g], axis=1)
    gw_out = jnp.concatenate([gw_out_ssd, gw_out_att], axis=0)
    small = dict(conv_w=g_conv_w, conv_b=g_conv_b, dt_bias=g_dt_bias, a_log=g_a_log, d_skip=g_d_skip,
                 ssd_norm_w=g_norm_w, attn_sinks=g_sinks, ln_g=g_ln_g, ln_b=g_ln_b)
    return loss[:, 0:1], grad_x, gw_in, gw_out, small


def _mesh_pos():
    return lax.axis_index("x"), lax.axis_index("y"), lax.axis_index("c")


def _gather_weights(w_in_s, w_out_s, conv_w_s):
    big = (w_in_s, w_out_s)

    def body(win_ref, wout_ref, cw_ref, owin_ref, owout_ref, ocw_ref, send_sems, recv_sems, small_send, small_recv,
             local_sems):
        x, y, c = _mesh_pos()
        me = 2 * x + y
        sibling = (x, y, 1 - c)
        chips = [(1 - x, y), (x, 1 - y), (1 - x, 1 - y)]
        locals_ = [pltpu.make_async_copy(src, dst.at[me], local_sems.at[t])
                   for t, (src, dst) in enumerate(((win_ref, owin_ref), (wout_ref, owout_ref), (cw_ref, ocw_ref)))]
        for cp in locals_:
            cp.start()
        started = []
        for t, (src, dst) in enumerate(((win_ref, owin_ref), (wout_ref, owout_ref))):
            hr = src.shape[0] // 2

            def half(ref, hc, hr=hr):
                return ref.at[pl.ds(hc * hr, hr), :]

            for j, (px, py) in enumerate(chips):
                cp = pltpu.make_async_remote_copy(
                    src_ref=half(src, c), dst_ref=half(dst.at[me], c), send_sem=send_sems.at[t, j],
                    recv_sem=recv_sems.at[t, j], device_id=(px, py, c), device_id_type=MESH)
                cp.start()
                started.append(cp)
        for j, (px, py) in enumerate(chips):
            cp = pltpu.make_async_remote_copy(
                src_ref=cw_ref, dst_ref=ocw_ref.at[me], send_sem=small_send.at[j], recv_sem=small_recv.at[j],
                device_id=(px, py, c), device_id_type=MESH)
            cp.start()
            started.append(cp)
        for t, (src, dst) in enumerate(((win_ref, owin_ref), (wout_ref, owout_ref))):
            hr = src.shape[0] // 2
            for j, (px, py) in enumerate(chips):
                src_chip = 2 * px + py
                blk = dst.at[src_chip].at[pl.ds(c * hr, hr), :]
                pltpu.make_async_remote_copy(
                    src_ref=blk, dst_ref=blk, send_sem=send_sems.at[t, j], recv_sem=recv_sems.at[t, j],
                    device_id=(px, py, c), device_id_type=MESH).wait_recv()
                cp = pltpu.make_async_remote_copy(
                    src_ref=blk, dst_ref=blk, send_sem=send_sems.at[t, 3 + j], recv_sem=recv_sems.at[t, 3 + j],
                    device_id=sibling, device_id_type=MESH)
                cp.start()
                started.append(cp)
        for t, (src, dst) in enumerate(((win_ref, owin_ref), (wout_ref, owout_ref))):
            hr = src.shape[0] // 2
            for j, (px, py) in enumerate(chips):
                src_chip = 2 * px + py
                blk = dst.at[src_chip].at[pl.ds((1 - c) * hr, hr), :]
                pltpu.make_async_remote_copy(
                    src_ref=blk, dst_ref=blk, send_sem=send_sems.at[t, 3 + j], recv_sem=recv_sems.at[t, 3 + j],
                    device_id=sibling, device_id_type=MESH).wait_recv()
        for j in range(3):
            pltpu.make_async_remote_copy(
                src_ref=cw_ref, dst_ref=ocw_ref.at[me], send_sem=small_send.at[j], recv_sem=small_recv.at[j],
                device_id=sibling, device_id_type=MESH).wait_recv()
        for cp in started:
            cp.wait_send()
        for cp in locals_:
            cp.wait()

    any_spec = pl.BlockSpec(memory_space=pl.ANY)
    return pl.pallas_call(
        body, name="gather_weights",
        in_specs=[any_spec] * 3, out_specs=[any_spec] * 3,
        out_shape=[jax.ShapeDtypeStruct((N_CHIPS,) + a.shape, a.dtype) for a in (w_in_s, w_out_s, conv_w_s)],
        scratch_shapes=[pltpu.SemaphoreType.DMA((2, 6)), pltpu.SemaphoreType.DMA((2, 6)),
                        pltpu.SemaphoreType.DMA((3,)), pltpu.SemaphoreType.DMA((3,)), pltpu.SemaphoreType.DMA((3,))],
    )(w_in_s, w_out_s, conv_w_s)


def _pair_exchange(gw_in, gw_out, small):
    k_small = small.shape[1]

    def body(gin_ref, gout_ref, sm_ref, min_ref, mout_ref, rin_ref, rout_ref, slots_ref, send_sems, recv_sems,
             small_send, small_recv, local_sems):
        x, y, c = _mesh_pos()
        me = 4 * x + 2 * y + c
        sibling = (x, y, 1 - c)
        mine = pltpu.make_async_copy(sm_ref, slots_ref.at[me], local_sems.at[2])
        mine.start()
        locals_ = [mine]
        started = []
        for t, (src, dst, own) in enumerate(((gin_ref, rin_ref, min_ref), (gout_ref, rout_ref, mout_ref))):
            hr = src.shape[1] // 2
            lc = pltpu.make_async_copy(src.at[:, pl.ds(c * hr, hr), :], own, local_sems.at[t])
            lc.start()
            locals_.append(lc)
            cp = pltpu.make_async_remote_copy(
                src_ref=src.at[:, pl.ds((1 - c) * hr, hr), :], dst_ref=dst, send_sem=send_sems.at[t],
                recv_sem=recv_sems.at[t], device_id=sibling, device_id_type=MESH)
            cp.start()
            started.append(cp)
        for k in range(1, 8):
            peer = (x ^ ((k >> 2) & 1), y ^ ((k >> 1) & 1), c ^ (k & 1))
            cp = pltpu.make_async_remote_copy(
                src_ref=sm_ref, dst_ref=slots_ref.at[me], send_sem=small_send.at[k - 1], recv_sem=small_recv.at[k - 1],
                device_id=peer, device_id_type=MESH)
            cp.start()
            started.append(cp)
        for t, (src, dst) in enumerate(((gin_ref, rin_ref), (gout_ref, rout_ref))):
            pltpu.make_async_remote_copy(
                src_ref=dst, dst_ref=dst, send_sem=send_sems.at[t], recv_sem=recv_sems.at[t],
                device_id=sibling, device_id_type=MESH).wait_recv()
        for k in range(1, 8):
            pltpu.make_async_remote_copy(
                src_ref=sm_ref, dst_ref=slots_ref.at[me], send_sem=small_send.at[k - 1], recv_sem=small_recv.at[k - 1],
                device_id=sibling, device_id_type=MESH).wait_recv()
        for cp in started:
            cp.wait_send()
        for lc in locals_:
            lc.wait()

    any_spec = pl.BlockSpec(memory_space=pl.ANY)
    half_in = jax.ShapeDtypeStruct((N_CHIPS, D_MODEL // 2, W_IN_COLS), F32)
    half_out = jax.ShapeDtypeStruct((N_CHIPS, W_OUT_ROWS // 2, D_MODEL), F32)
    return pl.pallas_call(
        body, name="pair_exchange",
        in_specs=[any_spec] * 3, out_specs=[any_spec] * 5,
        out_shape=[half_in, half_out, half_in, half_out, jax.ShapeDtypeStruct((8, 8, k_small), F32)],
        scratch_shapes=[pltpu.SemaphoreType.DMA((2,)), pltpu.SemaphoreType.DMA((2,)),
                        pltpu.SemaphoreType.DMA((7,)), pltpu.SemaphoreType.DMA((7,)), pltpu.SemaphoreType.DMA((3,))],
    )(gw_in, gw_out, small)


def _chip_exchange(s_in, s_out):
    def body(sin_ref, sout_ref, rin_ref, rout_ref, send_sems, recv_sems, local_sems):
        x, y, c = _mesh_pos()
        me = 2 * x + y
        chips = [(1 - x, y), (x, 1 - y), (1 - x, 1 - y)]
        started = []
        locals_ = []
        for t, (src, dst) in enumerate(((sin_ref, rin_ref), (sout_ref, rout_ref))):
            lc = pltpu.make_async_copy(src.at[me], dst.at[me], local_sems.at[t])
            lc.start()
            locals_.append(lc)
            for j, (px, py) in enumerate(chips):
                cp = pltpu.make_async_remote_copy(
                    src_ref=src.at[2 * px + py], dst_ref=dst.at[me], send_sem=send_sems.at[t, j],
                    recv_sem=recv_sems.at[t, j], device_id=(px, py, c), device_id_type=MESH)
                cp.start()
                started.append(cp)
        for t, (src, dst) in enumerate(((sin_ref, rin_ref), (sout_ref, rout_ref))):
            for j, (px, py) in enumerate(chips):
                blk = dst.at[2 * px + py]
                pltpu.make_async_remote_copy(
                    src_ref=blk, dst_ref=blk, send_sem=send_sems.at[t, j], recv_sem=recv_sems.at[t, j],
                    device_id=(px, py, c), device_id_type=MESH).wait_recv()
        for cp in started:
            cp.wait_send()
        for lc in locals_:
            lc.wait()

    any_spec = pl.BlockSpec(memory_space=pl.ANY)
    return pl.pallas_call(
        body, name="chip_exchange",
        in_specs=[any_spec] * 2, out_specs=[any_spec] * 2,
        out_shape=[jax.ShapeDtypeStruct(s_in.shape, F32), jax.ShapeDtypeStruct(s_out.shape, F32)],
        scratch_shapes=[pltpu.SemaphoreType.DMA((2, 3)), pltpu.SemaphoreType.DMA((2, 3)), pltpu.SemaphoreType.DMA((2,))],
    )(s_in, s_out)


def _pair_share(h_in, h_out):
    def body(hin_ref, hout_ref, fin_ref, fout_ref, send_sems, recv_sems, local_sems):
        x, y, c = _mesh_pos()
        sibling = (x, y, 1 - c)
        started = []
        locals_ = []
        for t, (src, dst) in enumerate(((hin_ref, fin_ref), (hout_ref, fout_ref))):
            hr = src.shape[0]
            lc = pltpu.make_async_copy(src, dst.at[pl.ds(c * hr, hr), :], local_sems.at[t])
            lc.start()
            locals_.append(lc)
            cp = pltpu.make_async_remote_copy(
                src_ref=src, dst_ref=dst.at[pl.ds(c * hr, hr), :], send_sem=send_sems.at[t], recv_sem=recv_sems.at[t],
                device_id=sibling, device_id_type=MESH)
            cp.start()
            started.append(cp)
        for t, (src, dst) in enumerate(((hin_ref, fin_ref), (hout_ref, fout_ref))):
            hr = src.shape[0]
            blk = dst.at[pl.ds((1 - c) * hr, hr), :]
            pltpu.make_async_remote_copy(
                src_ref=blk, dst_ref=blk, send_sem=send_sems.at[t], recv_sem=recv_sems.at[t],
                device_id=sibling, device_id_type=MESH).wait_recv()
        for cp in started:
            cp.wait_send()
        for lc in locals_:
            lc.wait()

    any_spec = pl.BlockSpec(memory_space=pl.ANY)
    return pl.pallas_call(
        body, name="pair_share",
        in_specs=[any_spec] * 2, out_specs=[any_spec] * 2,
        out_shape=[jax.ShapeDtypeStruct((2 * h_in.shape[0], h_in.shape[1]), F32),
                   jax.ShapeDtypeStruct((2 * h_out.shape[0], h_out.shape[1]), F32)],
        scratch_shapes=[pltpu.SemaphoreType.DMA((2,)), pltpu.SemaphoreType.DMA((2,)), pltpu.SemaphoreType.DMA((2,))],
    )(h_in, h_out)


def _pair_add(mine, recv, name):
    _, rows, C = mine.shape
    tr = 128

    def body(g_ref, r_ref, o_ref):
        o_ref[...] = g_ref[...] + r_ref[...]

    spec = pl.BlockSpec((1, tr, C), lambda j, i: (j, i, 0))
    return pl.pallas_call(
        body, name=name, grid=(N_CHIPS, rows // tr),
        in_specs=[spec, spec], out_specs=spec,
        out_shape=jax.ShapeDtypeStruct((N_CHIPS, rows, C), F32),
        compiler_params=_cparams(("parallel", "parallel")),
    )(mine, recv)


def _chip_add(parts, name):
    _, rows, C = parts.shape
    tr = 128

    def body(r0, r1, r2, r3, o_ref):
        o_ref[...] = ((r0[0] + r1[0]) + r2[0]) + r3[0]

    slab = lambda j: pl.BlockSpec((1, tr, C), lambda i, j=j: (j, i, 0))
    return pl.pallas_call(
        body, name=name, grid=(rows // tr,),
        in_specs=[slab(j) for j in range(N_CHIPS)],
        out_specs=pl.BlockSpec((tr, C), lambda i: (i, 0)),
        out_shape=jax.ShapeDtypeStruct((rows, C), F32),
        compiler_params=_cparams(("parallel",)),
    )(parts, parts, parts, parts)


def _adamw_math(w, g, m, v):
    m = ADAM_B1 * m + (1.0 - ADAM_B1) * g
    v = ADAM_B2 * v + (1.0 - ADAM_B2) * (g * g)
    m_hat = m / (1.0 - ADAM_B1 ** ADAM_STEP)
    v_hat = v / (1.0 - ADAM_B2 ** ADAM_STEP)
    delta = -ADAM_LR * (m_hat / (jnp.sqrt(v_hat) + ADAM_EPS) + ADAM_WD * w)
    return delta, m, v


def _adamw(w, g, m, v, name):
    R, C = w.shape
    tr = 128 if R % 128 == 0 else R

    def body(w_ref, g_ref, m_ref, v_ref, d_ref, nm_ref, nv_ref):
        d, nm, nv = _adamw_math(w_ref[...], g_ref[...], m_ref[...], v_ref[...])
        d_ref[...] = d
        nm_ref[...] = nm
        nv_ref[...] = nv

    spec = pl.BlockSpec((tr, C), lambda i: (i, 0))
    return pl.pallas_call(
        body, name=name, grid=(R // tr,),
        in_specs=[spec] * 4, out_specs=[spec] * 3,
        out_shape=[jax.ShapeDtypeStruct((R, C), F32)] * 3,
        compiler_params=_cparams(("parallel",)),
    )(w, g, m, v)


def _sum_slots(slots):
    k = slots.shape[2]

    def body(s_ref, o_ref):
        acc = s_ref[0]
        for d in range(1, 8):
            acc = acc + s_ref[d]
        o_ref[...] = acc

    return pl.pallas_call(
        body, name="sum_slots", out_shape=jax.ShapeDtypeStruct((8, k), F32),
        compiler_params=_cparams(),
    )(slots)


SMALL_NAMES = ("conv_b", "ssd_norm_w", "ln_g", "ln_b", "dt_bias", "a_log", "d_skip", "attn_sinks")
SMALL_SIZES = (D_XBC, D_SSD, D_MODEL, D_MODEL, SSD_HEADS, SSD_HEADS, SSD_HEADS, ATT_QH)


def _pack_vectors(vals):
    parts = []
    for v in vals:
        n = v.shape[1]
        pad = (-n) % 128
        parts.append(jnp.pad(v, ((0, 0), (0, pad))) if pad else v)
    return jnp.concatenate(parts, axis=1)


def _unpack_vectors(row):
    out, off = [], 0
    for n in SMALL_SIZES:
        out.append(row[:, off:off + n])
        off += n + ((-n) % 128)
    return out


def kernel(x, positions, w_in, conv_w, conv_b, dt_bias, a_log, d_skip, ssd_norm_w, attn_sinks, w_out, ln_g, ln_b, loss_target, m_w_in, m_conv_w, m_conv_b, m_dt_bias, m_a_log, m_d_skip, m_ssd_norm_w, m_attn_sinks, m_w_out, m_ln_g, m_ln_b, v_w_in, v_conv_w, v_conv_b, v_dt_bias, v_a_log, v_d_skip, v_ssd_norm_w, v_attn_sinks, v_w_out, v_ln_g, v_ln_b):
    mx, my, mc = _mesh_pos()
    chip = 2 * mx + my
    L = x.shape[1]

    conv_w_s8 = jnp.pad(conv_w[0], ((0, 8 - CONV_K), (0, 0)))
    ag_in, ag_out, ag_cw = _gather_weights(_bf(w_in[0]), _bf(w_out[0]), conv_w_s8)
    w_full = jnp.concatenate([ag_in[j] for j in range(N_CHIPS)], axis=1)
    w = jnp.concatenate([
        w_full[:, O_Z:O_Z + D_SSD], w_full[:, O_G:O_G + D_ATT], w_full[:, O_Q:O_Q + D_ATT],
        w_full[:, O_XBC:O_XBC + D_XBC], w_full[:, O_K:O_K + 2 * D_KV], w_full[:, O_DT:O_DT + SSD_HEADS],
        jnp.zeros((D_MODEL, DT_PAD - SSD_HEADS), BF16)], axis=1)
    w_out_full = ag_out.reshape(D_MIX, D_MODEL)
    conv_w_full = jnp.concatenate([ag_cw[j, 0:CONV_K] for j in range(N_CHIPS)], axis=1)

    loss_part, grad_x, gw_in, gw_out, small = _local_step(
        x[0], positions[0].reshape(L, 1), loss_target[0], w, w_out_full, conv_w_full, conv_b, dt_bias, a_log, d_skip,
        ssd_norm_w, attn_sinks, ln_g, ln_b)

    vec = _pack_vectors([small[n] for n in SMALL_NAMES])
    top = jnp.concatenate([vec, jnp.pad(loss_part, ((0, 0), (0, 127)))], axis=1)
    right = jnp.pad(top, ((0, 7), (0, 0)))
    packed = jnp.concatenate([jnp.pad(small["conv_w"], ((0, 8 - CONV_K), (0, 0))), right], axis=1)

    gw_in_slabs = jnp.stack([gw_in[:, W_IN_COLS * j:W_IN_COLS * (j + 1)] for j in range(N_CHIPS)])
    gw_out_slabs = gw_out.reshape(N_CHIPS, W_OUT_ROWS, D_MODEL)
    mine_in, mine_out, recv_in, recv_out, slots = _pair_exchange(gw_in_slabs, gw_out_slabs, packed)
    s_in = _pair_add(mine_in, recv_in, "pair_add_in")
    s_out = _pair_add(mine_out, recv_out, "pair_add_out")
    r_in, r_out = _chip_exchange(s_in, s_out)
    h_in = _chip_add(r_in, "chip_add_in")
    h_out = _chip_add(r_out, "chip_add_out")
    g_w_in, g_w_out = _pair_share(h_in, h_out)
    tot = _sum_slots(slots)

    d_w_in, nm_w_in, nv_w_in = _adamw(w_in[0], g_w_in, m_w_in[0], v_w_in[0], "adamw_w_in")
    d_w_out, nm_w_out, nv_w_out = _adamw(w_out[0], g_w_out, m_w_out[0], v_w_out[0], "adamw_w_out")
    g_conv_w_all = tot[0:CONV_K, 0:D_XBC]
    g_conv_w = lax.dynamic_slice(g_conv_w_all, (0, chip * CONV_COLS), (CONV_K, CONV_COLS))
    g_vecs = _unpack_vectors(tot[0:1, D_XBC:D_XBC + 5120])
    loss = tot[0, D_XBC + 5120]

    grads = dict(zip(SMALL_NAMES, g_vecs))
    params = dict(conv_b=conv_b, ssd_norm_w=ssd_norm_w, ln_g=ln_g, ln_b=ln_b, dt_bias=dt_bias, a_log=a_log,
                  d_skip=d_skip, attn_sinks=attn_sinks)
    moms = dict(conv_b=m_conv_b, ssd_norm_w=m_ssd_norm_w, ln_g=m_ln_g, ln_b=m_ln_b, dt_bias=m_dt_bias, a_log=m_a_log,
                d_skip=m_d_skip, attn_sinks=m_attn_sinks)
    vars_ = dict(conv_b=v_conv_b, ssd_norm_w=v_ssd_norm_w, ln_g=v_ln_g, ln_b=v_ln_b, dt_bias=v_dt_bias, a_log=v_a_log,
                 d_skip=v_d_skip, attn_sinks=v_attn_sinks)
    def small_block(cw, vecs):
        a = jnp.pad(cw, ((0, 0), (0, 5120 - CONV_COLS)))
        return jnp.concatenate([a, _pack_vectors(vecs), jnp.zeros((3, 5120), F32)], axis=0)

    sw = small_block(conv_w[0], [params[n] for n in SMALL_NAMES])
    sg = small_block(g_conv_w, [grads[n] for n in SMALL_NAMES])
    sm = small_block(m_conv_w[0], [moms[n] for n in SMALL_NAMES])
    sv = small_block(v_conv_w[0], [vars_[n] for n in SMALL_NAMES])
    sd, snm, snv = _adamw(sw, sg, sm, sv, "adamw_small")

    def split_small(blk):
        d = dict(zip(SMALL_NAMES, _unpack_vectors(blk[CONV_K:CONV_K + 1])))
        d["conv_w"] = blk[0:CONV_K, 0:CONV_COLS][None]
        return d

    delta, new_m, new_v = split_small(sd), split_small(snm), split_small(snv)
    grads["conv_w"] = g_conv_w[None]
    for dd, a_in, a_out in ((grads, g_w_in, g_w_out), (delta, d_w_in, d_w_out), (new_m, nm_w_in, nm_w_out),
                            (new_v, nv_w_in, nv_w_out)):
        dd["w_in"] = a_in[None]
        dd["w_out"] = a_out[None]
    order = ("w_in", "conv_w", "conv_b", "dt_bias", "a_log", "d_skip", "ssd_norm_w", "attn_sinks", "w_out", "ln_g", "ln_b")
    return (loss, grad_x[None], *[grads[n] for n in order], *[delta[n] for n in order], *[new_m[n] for n in order],
            *[new_v[n] for n in order])
```

```python
import functools

import numpy as np
import jax
import jax.numpy as jnp
from jax import lax
from jax.experimental import pallas as pl
from jax.experimental.pallas import tpu as pltpu

F32 = jnp.float32
BF16 = jnp.bfloat16
MESH = pl.DeviceIdType.MESH

D_MODEL = 1024
D_SSD = 1024
D_ATT = 1024
D_MIX = 2048
SSD_HEADS = 16
SSD_P = 64
SSD_GROUPS = 2
SSD_R = 8
SSD_N = 128
D_BC = 256
D_XBC = 1536
CONV_K = 4
CHUNK = 128
ATT_HD = 64
ATT_QH = 16
ATT_KVH = 4
ATT_R = 4
D_KV = 256
WINDOW = 128
ROPE_THETA = 500000.0
ROPE_DIM = 16
ALPHA = 2.0 ** 0.25
LN_EPS = 1e-5
RMS_EPS = 1e-5
D_IN_PROJ = 5136
O_Z, O_XBC, O_DT, O_Q, O_K, O_V, O_G = 0, 1024, 2560, 2576, 3600, 3856, 4112
P_Z, P_G, P_Q, P_XBC, P_KV, P_DT, P_END = 0, 1024, 2048, 3072, 4608, 5120, 5248
DT_PAD = 128
N_CHIPS = 4
W_IN_COLS = D_IN_PROJ // N_CHIPS
SLAB_ROWS = 1312
W_OUT_ROWS = D_MIX // N_CHIPS
CONV_COLS = D_XBC // N_CHIPS

ADAM_LR = 0.001
ADAM_B1 = 0.9
ADAM_B2 = 0.999
ADAM_EPS = 1e-08
ADAM_WD = 0.01
ADAM_STEP = 10

VMEM_LIMIT = 56 * 1024 * 1024
NEG_BIG = -1e30
HI = lax.Precision.HIGHEST


def _cparams(sem=None, **kw):
    if sem is not None:
        kw["dimension_semantics"] = sem
    return pltpu.CompilerParams(vmem_limit_bytes=VMEM_LIMIT, **kw)


def _dot(a, b):
    return jnp.dot(a, b, preferred_element_type=F32)


def _dot_nt(a, b):
    return lax.dot_general(a, b, (((1,), (1,)), ((), ())), preferred_element_type=F32)


def _dot_tn(a, b):
    return lax.dot_general(a, b, (((0,), (0,)), ((), ())), preferred_element_type=F32)


def _bf(a):
    return a.astype(BF16)


def _iota2(shape, dim):
    return lax.broadcasted_iota(jnp.int32, shape, dim)


def _to_rows(col):
    k = col.shape[1]
    eye = (_iota2((k, k), 0) == _iota2((k, k), 1)).astype(F32)
    return lax.dot_general(eye, col, (((1,), (1,)), ((), ())), preferred_element_type=F32, precision=HI)


def _to_cols(row):
    n = row.shape[1]
    eye = (_iota2((n, n), 0) == _iota2((n, n), 1)).astype(F32)
    return lax.dot_general(eye, row, (((1,), (1,)), ((), ())), preferred_element_type=F32, precision=HI)


def _sigmoid(x):
    return jax.nn.sigmoid(x)


def _in_proj(x, w):
    L = x.shape[0]
    tm = 256
    widths = (D_SSD, D_ATT, D_ATT, D_XBC, 2 * D_KV, DT_PAD)
    offs = (P_Z, P_G, P_Q, P_XBC, P_KV, P_DT)

    def body(x_ref, w_ref, z_ref, g_ref, q_ref, xbc_ref, kv_ref, dt_ref, xb_ref):
        xb = _bf(x_ref[...])
        xb_ref[...] = xb
        for o_ref, off, wd in zip((z_ref, g_ref, q_ref, xbc_ref, kv_ref, dt_ref), offs, widths):
            o_ref[...] = _dot_nt(xb, w_ref[off:off + wd, :])

    row = lambda wd: pl.BlockSpec((tm, wd), lambda i: (i, 0))
    return pl.pallas_call(
        body, name="in_proj", grid=(L // tm,),
        in_specs=[row(D_MODEL), pl.BlockSpec((P_END, D_MODEL), lambda i: (0, 0))],
        out_specs=[row(wd) for wd in widths] + [row(D_MODEL)],
        out_shape=[jax.ShapeDtypeStruct((L, wd), F32) for wd in widths] + [jax.ShapeDtypeStruct((L, D_MODEL), BF16)],
        compiler_params=_cparams(("parallel",)),
    )(x, w)


def _matmul_tn(a, b, tn, name):
    K, M = a.shape
    N = b.shape[1]
    tk = 512
    nk = K // tk

    def body(a_ref, b_ref, o_ref, acc_ref):
        k = pl.program_id(1)

        @pl.when(k == 0)
        def _():
            acc_ref[...] = jnp.zeros_like(acc_ref)

        acc_ref[...] += _dot_tn(_bf(a_ref[...]), _bf(b_ref[...]))

        @pl.when(k == nk - 1)
        def _():
            o_ref[...] = acc_ref[...]

    return pl.pallas_call(
        body, name=name, grid=(N // tn, nk),
        in_specs=[pl.BlockSpec((tk, M), lambda j, k: (k, 0)), pl.BlockSpec((tk, tn), lambda j, k: (k, j))],
        out_specs=pl.BlockSpec((M, tn), lambda j, k: (0, j)),
        out_shape=jax.ShapeDtypeStruct((M, N), F32),
        scratch_shapes=[pltpu.VMEM((M, tn), F32)],
        compiler_params=_cparams(("parallel", "arbitrary")),
    )(a, b)


def _grad_x(dr, dz, dg, dq, dxbc, dkv, ddt, w):
    L = dr.shape[0]
    tm = 256
    widths = (D_SSD, D_ATT, D_ATT, D_XBC, 2 * D_KV, DT_PAD)
    offs = (P_Z, P_G, P_Q, P_XBC, P_KV, P_DT)

    def body(dr_ref, dz_ref, dg_ref, dq_ref, dxbc_ref, dkv_ref, ddt_ref, w_ref, o_ref):
        acc = ALPHA * dr_ref[...]
        for p_ref, off, wd in zip((dz_ref, dg_ref, dq_ref, dxbc_ref, dkv_ref, ddt_ref), offs, widths):
            acc = acc + _dot(_bf(p_ref[...]), w_ref[off:off + wd, :])
        o_ref[...] = acc

    row = lambda wd: pl.BlockSpec((tm, wd), lambda i: (i, 0))
    return pl.pallas_call(
        body, name="grad_x", grid=(L // tm,),
        in_specs=[row(D_MODEL)] + [row(wd) for wd in widths] + [pl.BlockSpec((P_END, D_MODEL), lambda i: (0, 0))],
        out_specs=row(D_MODEL),
        out_shape=jax.ShapeDtypeStruct((L, D_MODEL), F32),
        compiler_params=_cparams(("parallel",)),
    )(dr, dz, dg, dq, dxbc, dkv, ddt, w)


def _ssd_chunk_pre(first, xbc_ref, tail_ref, dt_ref, cw_ref, cb_ref, dtb_ref, alog_ref, ext):
    tail = jnp.where(first, 0.0, tail_ref[...])
    ext[0:8, :] = tail
    ext[8:8 + CHUNK, :] = xbc_ref[...]
    u = cb_ref[...] + cw_ref[0:1, :] * ext[pl.ds(5, CHUNK), :]
    for k in range(1, CONV_K):
        u = u + cw_ref[k:k + 1, :] * ext[pl.ds(5 + k, CHUNK), :]
    sig = _sigmoid(u)
    xbc = u * sig
    dtraw = dt_ref[:, 0:SSD_HEADS] + dtb_ref[...]
    dt = jax.nn.softplus(dtraw)
    A = -jnp.exp(alog_ref[...])
    a = dt * A
    tril = (_iota2((CHUNK, CHUNK), 0) >= _iota2((CHUNK, CHUNK), 1)).astype(F32)
    acs = jnp.dot(tril, a, preferred_element_type=F32, precision=HI)
    acs_row = _to_rows(acs)
    return u, sig, xbc, dtraw, dt, A, acs, acs_row


def _ssd_fwd(z, xbc, dtp, conv_w, conv_b, dt_bias, a_log, d_skip, norm_w):
    L = z.shape[0]
    nc = L // CHUNK

    def body(z_ref, xbc_ref, tail_ref, dt_ref, cw_ref, cb_ref, dtb_ref, alog_ref, dsk_ref, nw_ref,
             y_ref, ypre_ref, prev_ref, state, ext, ybuf):
        c = pl.program_id(0)

        @pl.when(c == 0)
        def _():
            state[...] = jnp.zeros_like(state)

        u, sig, xbcv, dtraw, dt, A, acs, acs_row = _ssd_chunk_pre(
            c == 0, xbc_ref, tail_ref, dt_ref, cw_ref, cb_ref, dtb_ref, alog_ref, ext)
        prev_ref[0] = state[...]
        causal = _iota2((CHUNK, CHUNK), 0) >= _iota2((CHUNK, CHUNK), 1)
        alast = acs[CHUNK - 1:CHUNK, :]
        for g in range(SSD_GROUPS):
            Bg = _bf(xbcv[:, D_SSD + SSD_N * g:D_SSD + SSD_N * (g + 1)])
            Cg = _bf(xbcv[:, D_SSD + D_BC + SSD_N * g:D_SSD + D_BC + SSD_N * (g + 1)])
            cb = _dot_nt(Cg, Bg)
            for r in range(SSD_R):
                h = g * SSD_R + r
                hs = slice(SSD_P * h, SSD_P * (h + 1))
                acs_c = acs[:, h:h + 1]
                seg = acs_c - acs_row[h:h + 1, :]
                Lm = jnp.where(causal, jnp.exp(jnp.where(causal, seg, 0.0)), 0.0)
                M = cb * Lm
                xh = xbcv[:, hs]
                X = xh * dt[:, h:h + 1]
                prev_h = state[hs, :]
                ydiag = _dot(_bf(M), _bf(X))
                yoff = _dot_nt(Cg, _bf(prev_h)) * jnp.exp(acs_c)
                al = alast[:, h:h + 1]
                Xd = X * jnp.exp(al - acs_c)
                state[hs, :] = prev_h * jnp.exp(al) + _dot_tn(_bf(Xd), Bg)
                ybuf[:, hs] = ydiag + yoff + dsk_ref[:, h:h + 1] * xh
        y = ybuf[...]
        ypre_ref[...] = y
        zv = z_ref[...]
        yf = y * (zv * _sigmoid(zv))
        half = D_SSD // SSD_GROUPS
        for g in range(SSD_GROUPS):
            gs = slice(half * g, half * (g + 1))
            yg = yf[:, gs]
            ms = jnp.mean(yg * yg, axis=-1, keepdims=True)
            y_ref[:, gs] = yg * lax.rsqrt(ms + RMS_EPS) * nw_ref[:, gs]

    full = lambda shape: pl.BlockSpec(shape, lambda c: (0, 0))
    return pl.pallas_call(
        body, name="ssd_fwd", grid=(nc,),
        in_specs=[
            pl.BlockSpec((CHUNK, D_SSD), lambda c: (c, 0)),
            pl.BlockSpec((CHUNK, D_XBC), lambda c: (c, 0)),
            pl.BlockSpec((8, D_XBC), lambda c: (jnp.maximum(c * (CHUNK // 8) - 1, 0), 0)),
            pl.BlockSpec((CHUNK, DT_PAD), lambda c: (c, 0)),
            full((CONV_K, D_XBC)), full((1, D_XBC)), full((1, SSD_HEADS)), full((1, SSD_HEADS)), full((1, SSD_HEADS)),
            full((1, D_SSD)),
        ],
        out_specs=[
            pl.BlockSpec((CHUNK, D_SSD), lambda c: (c, 0)),
            pl.BlockSpec((CHUNK, D_SSD), lambda c: (c, 0)),
            pl.BlockSpec((1, SSD_HEADS * SSD_P, SSD_N), lambda c: (c, 0, 0)),
        ],
        out_shape=[
            jax.ShapeDtypeStruct((L, D_SSD), F32),
            jax.ShapeDtypeStruct((L, D_SSD), F32),
            jax.ShapeDtypeStruct((nc, SSD_HEADS * SSD_P, SSD_N), F32),
        ],
        scratch_shapes=[
            pltpu.VMEM((SSD_HEADS * SSD_P, SSD_N), F32),
            pltpu.VMEM((CHUNK + 8, D_XBC), F32),
            pltpu.VMEM((CHUNK, D_SSD), F32),
        ],
        compiler_params=_cparams(("arbitrary",)),
    )(z, xbc, xbc, dtp, conv_w, conv_b, dt_bias, a_log, d_skip, norm_w)


def _ssd_bwd(dy, z, ypre, xbc, dtp, prev, conv_w, conv_b, dt_bias, a_log, d_skip, norm_w):
    L = z.shape[0]
    nc = L // CHUNK

    def body(dy_ref, z_ref, ypre_ref, xbc_ref, tail_ref, dt_ref, prev_ref, cw_ref, cb_ref, dtb_ref, alog_ref, dsk_ref,
             nw_ref, dz_ref, dxbc_ref, ddt_ref, gcw_ref, gcb_ref, gdtb_ref, galog_ref, gdsk_ref, gnw_ref,
             dstate, dhead, ext, ext2, dpost):
        i = pl.program_id(0)
        c = nc - 1 - i

        @pl.when(i == 0)
        def _():
            dstate[...] = jnp.zeros_like(dstate)
            dhead[...] = jnp.zeros_like(dhead)
            gcw_ref[...] = jnp.zeros_like(gcw_ref)
            gcb_ref[...] = jnp.zeros_like(gcb_ref)
            gdtb_ref[...] = jnp.zeros_like(gdtb_ref)
            galog_ref[...] = jnp.zeros_like(galog_ref)
            gdsk_ref[...] = jnp.zeros_like(gdsk_ref)
            gnw_ref[...] = jnp.zeros_like(gnw_ref)

        u, sig, xbcv, dtraw, dt, A, acs, acs_row = _ssd_chunk_pre(
            c == 0, xbc_ref, tail_ref, dt_ref, cw_ref, cb_ref, dtb_ref, alog_ref, ext)

        zv = z_ref[...]
        ypre = ypre_ref[...]
        dyn = dy_ref[...]
        sz = _sigmoid(zv)
        silu_z = zv * sz
        yf = ypre * silu_z
        half = D_SSD // SSD_GROUPS
        dyf_parts = []
        for g in range(SSD_GROUPS):
            gs = slice(half * g, half * (g + 1))
            yg = yf[:, gs]
            rstd = lax.rsqrt(jnp.mean(yg * yg, axis=-1, keepdims=True) + RMS_EPS)
            dout = dyn[:, gs]
            gnw_ref[:, gs] += jnp.sum(dout * yg * rstd, axis=0, keepdims=True)
            dyhat = dout * nw_ref[:, gs]
            dyf_parts.append(rstd * (dyhat - yg * (rstd * rstd) * jnp.mean(dyhat * yg, axis=-1, keepdims=True)))
        dyf = jnp.concatenate(dyf_parts, axis=1)
        dz_ref[...] = dyf * ypre * (sz * (1.0 + zv * (1.0 - sz)))
        dypre = dyf * silu_z

        causal = _iota2((CHUNK, CHUNK), 0) >= _iota2((CHUNK, CHUNK), 1)
        alast = acs[CHUNK - 1:CHUNK, :]
        lane16 = _iota2((1, SSD_HEADS), 1)
        sub16 = _iota2((SSD_HEADS, 1), 0)
        dacs_col = jnp.zeros((CHUNK, SSD_HEADS), F32)
        dacs_row = jnp.zeros((SSD_HEADS, CHUNK), F32)
        ddt_col = jnp.zeros((CHUNK, SSD_HEADS), F32)
        dalast = jnp.zeros((1, SSD_HEADS), F32)
        gdsk = jnp.zeros((1, SSD_HEADS), F32)
        for g in range(SSD_GROUPS):
            bs = slice(D_SSD + SSD_N * g, D_SSD + SSD_N * (g + 1))
            cs = slice(D_SSD + D_BC + SSD_N * g, D_SSD + D_BC + SSD_N * (g + 1))
            Bg = _bf(xbcv[:, bs])
            Cg = _bf(xbcv[:, cs])
            cb = _dot_nt(Cg, Bg)
            dcb = jnp.zeros((CHUNK, CHUNK), F32)
            dB = jnp.zeros((CHUNK, SSD_N), F32)
            dC = jnp.zeros((CHUNK, SSD_N), F32)
            for r in range(SSD_R):
                h = g * SSD_R + r
                hs = slice(SSD_P * h, SSD_P * (h + 1))
                onehot = (lane16 == h).astype(F32)
                acs_c = acs[:, h:h + 1]
                seg = acs_c - acs_row[h:h + 1, :]
                Lm = jnp.where(causal, jnp.exp(jnp.where(causal, seg, 0.0)), 0.0)
                M = cb * Lm
                xh = xbcv[:, hs]
                dth = dt[:, h:h + 1]
                X = xh * dth
                Xb = _bf(X)
                dyh = dypre[:, hs]
                dyb = _bf(dyh)
                prev_h = prev_ref[0, hs, :]
                prevb = _bf(prev_h)
                dnext = dstate[hs, :]
                dnextb = _bf(dnext)
                al = alast[:, h:h + 1]
                eacs = jnp.exp(acs_c)
                eal = jnp.exp(al)
                dsd = jnp.exp(al - acs_c)
                G = _bf(dyh * eacs)
                dstate[hs, :] = dnext * eal + _dot_tn(G, Cg)
                dC = dC + _dot(G, prevb)
                yoff = _dot_nt(Cg, prevb) * eacs
                dacs_h = jnp.sum(dyh * yoff, axis=-1, keepdims=True)
                BdN = _dot_nt(Bg, dnextb)
                dX = dsd * BdN
                dB = dB + _dot(_bf(X * dsd), dnextb)
                t = jnp.sum(X * BdN, axis=-1, keepdims=True) * dsd
                dacs_h = dacs_h - t
                dal = jnp.sum(t, axis=0, keepdims=True) + jnp.sum(
                    jnp.sum(dnext * prev_h, axis=-1, keepdims=True), axis=0, keepdims=True) * eal
                dM = _dot_nt(dyb, Xb)
                dX = dX + _dot_tn(_bf(M), dyb)
                dseg = dM * M
                dcb = dcb + dM * Lm
                dacs_h = dacs_h + jnp.sum(dseg, axis=-1, keepdims=True)
                dacs_row = dacs_row - jnp.sum(dseg, axis=0, keepdims=True) * (sub16 == h).astype(F32)
                dacs_col = dacs_col + dacs_h * onehot
                dalast = dalast + dal * onehot
                ddt_col = ddt_col + jnp.sum(dX * xh, axis=-1, keepdims=True) * onehot
                gdsk = gdsk + jnp.sum(jnp.sum(dyh * xh, axis=-1, keepdims=True), axis=0, keepdims=True) * onehot
                dpost[:, hs] = dX * dth + dsk_ref[:, h:h + 1] * dyh
            dcbb = _bf(dcb)
            dpost[:, bs] = dB + _dot_tn(dcbb, Cg)
            dpost[:, cs] = dC + _dot(dcbb, Bg)

        is_last = (_iota2((CHUNK, 1), 0) == CHUNK - 1).astype(F32)
        dacs = dacs_col + _to_cols(dacs_row) + is_last * dalast
        triu = (_iota2((CHUNK, CHUNK), 0) <= _iota2((CHUNK, CHUNK), 1)).astype(F32)
        da = jnp.dot(triu, dacs, preferred_element_type=F32, precision=HI)
        ddt_tot = ddt_col + da * A
        galog_ref[...] += jnp.sum(da * dt, axis=0, keepdims=True) * A
        ddtraw = ddt_tot * _sigmoid(dtraw)
        gdtb_ref[...] += jnp.sum(ddtraw, axis=0, keepdims=True)
        gdsk_ref[...] += gdsk
        ddt_ref[...] = jnp.zeros_like(ddt_ref)
        ddt_ref[:, 0:SSD_HEADS] = ddtraw

        dconv = dpost[...] * (sig * (1.0 + u * (1.0 - sig)))
        gcb_ref[...] += jnp.sum(dconv, axis=0, keepdims=True)
        for k in range(CONV_K):
            gcw_ref[k:k + 1, :] += jnp.sum(dconv * ext[pl.ds(5 + k, CHUNK), :], axis=0, keepdims=True)
        ext2[0:CHUNK, :] = dconv
        ext2[CHUNK:CHUNK + 8, :] = dhead[...]
        dx = cw_ref[CONV_K - 1:CONV_K, :] * dconv
        for k in range(CONV_K - 1):
            dx = dx + cw_ref[k:k + 1, :] * ext2[pl.ds(CONV_K - 1 - k, CHUNK), :]
        dxbc_ref[...] = dx
        dhead[...] = dconv[0:8, :]

    full = lambda shape: pl.BlockSpec(shape, lambda i: (0, 0))
    rev = lambda wd: pl.BlockSpec((CHUNK, wd), lambda i: (nc - 1 - i, 0))
    return pl.pallas_call(
        body, name="ssd_bwd", grid=(nc,),
        in_specs=[
            rev(D_SSD), rev(D_SSD), rev(D_SSD), rev(D_XBC),
            pl.BlockSpec((8, D_XBC), lambda i: (jnp.maximum((nc - 1 - i) * (CHUNK // 8) - 1, 0), 0)),
            rev(DT_PAD),
            pl.BlockSpec((1, SSD_HEADS * SSD_P, SSD_N), lambda i: (nc - 1 - i, 0, 0)),
            full((CONV_K, D_XBC)), full((1, D_XBC)), full((1, SSD_HEADS)), full((1, SSD_HEADS)), full((1, SSD_HEADS)),
            full((1, D_SSD)),
        ],
        out_specs=[
            rev(D_SSD), rev(D_XBC), rev(DT_PAD),
            full((CONV_K, D_XBC)), full((1, D_XBC)), full((1, SSD_HEADS)), full((1, SSD_HEADS)), full((1, SSD_HEADS)),
            full((1, D_SSD)),
        ],
        out_shape=[
            jax.ShapeDtypeStruct((L, D_SSD), F32), jax.ShapeDtypeStruct((L, D_XBC), F32),
            jax.ShapeDtypeStruct((L, DT_PAD), F32),
            jax.ShapeDtypeStruct((CONV_K, D_XBC), F32), jax.ShapeDtypeStruct((1, D_XBC), F32),
            jax.ShapeDtypeStruct((1, SSD_HEADS), F32), jax.ShapeDtypeStruct((1, SSD_HEADS), F32),
            jax.ShapeDtypeStruct((1, SSD_HEADS), F32), jax.ShapeDtypeStruct((1, D_SSD), F32),
        ],
        scratch_shapes=[
            pltpu.VMEM((SSD_HEADS * SSD_P, SSD_N), F32),
            pltpu.VMEM((8, D_XBC), F32),
            pltpu.VMEM((CHUNK + 8, D_XBC), F32),
            pltpu.VMEM((CHUNK + 8, D_XBC), F32),
            pltpu.VMEM((CHUNK, D_XBC), F32),
        ],
        compiler_params=_cparams(("arbitrary",)),
    )(dy, z, ypre, xbc, xbc, dtp, prev, conv_w, conv_b, dt_bias, a_log, d_skip, norm_w)


def _rope_tables(pos_ref, inv_ref):
    ang = pos_ref[...].astype(F32) * inv_ref[...]
    d = _iota2((1, 2 * ATT_HD), 1) % ATT_HD
    s = jnp.sin(ang)
    return jnp.cos(ang), jnp.where(d < ROPE_DIM // 2, -s, 0.0), jnp.where((d >= ROPE_DIM // 2) & (d < ROPE_DIM), s, 0.0)


def _rope(t, tabs):
    c, s1, s2 = tabs
    n = t.shape[1]
    rep = n // c.shape[1]
    return (t * jnp.tile(c, (1, rep)) + pltpu.roll(t, n - ROPE_DIM // 2, 1) * jnp.tile(s1, (1, rep))
            + pltpu.roll(t, ROPE_DIM // 2, 1) * jnp.tile(s2, (1, rep)))


def _rope_t(t, tabs):
    c, s1, s2 = tabs
    n = t.shape[1]
    rep = n // c.shape[1]
    return (t * jnp.tile(c, (1, rep)) + pltpu.roll(t * jnp.tile(s1, (1, rep)), ROPE_DIM // 2, 1)
            + pltpu.roll(t * jnp.tile(s2, (1, rep)), n - ROPE_DIM // 2, 1))


def _swa_mask(first):
    qi = _iota2((WINDOW, 2 * WINDOW), 0)
    si = _iota2((WINDOW, 2 * WINDOW), 1)
    band = (si > qi) & (si <= qi + WINDOW)
    return band & (jnp.logical_not(first) | (si >= WINDOW))


def _swa_fwd(q, g, kv, pos, inv, sinks):
    L = q.shape[0]
    nb = L // WINDOW
    scale = ATT_HD ** -0.5

    def body(q_ref, g_ref, kvc_ref, kvp_ref, posc_ref, posp_ref, inv_ref, sink_ref, y_ref, o_ref, lse_ref):
        n = pl.program_id(0)
        tc = _rope_tables(posc_ref, inv_ref)
        tp = _rope_tables(posp_ref, inv_ref)
        qr = _rope(q_ref[...], tc)
        kc = _rope(kvc_ref[:, 0:D_KV], tc)
        kp = _rope(kvp_ref[:, 0:D_KV], tp)
        kk = _bf(jnp.concatenate([kp, kc], axis=0))
        vv = _bf(jnp.concatenate([kvp_ref[:, D_KV:2 * D_KV], kvc_ref[:, D_KV:2 * D_KV]], axis=0))
        valid = _swa_mask(n == 0)
        for j in range(ATT_KVH):
            js = slice(ATT_HD * j, ATT_HD * (j + 1))
            kj = kk[:, js]
            vj = vv[:, js]
            for r in range(ATT_R):
                h = j * ATT_R + r
                hs = slice(ATT_HD * h, ATT_HD * (h + 1))
                s = _dot_nt(_bf(qr[:, hs]), kj) * scale
                s = jnp.where(valid, s, NEG_BIG)
                sink = sink_ref[:, h:h + 1]
                m = jnp.maximum(jnp.max(s, axis=-1, keepdims=True), sink)
                p = jnp.exp(s - m)
                denom = jnp.sum(p, axis=-1, keepdims=True) + jnp.exp(sink - m)
                o_ref[:, hs] = _dot(_bf(p / denom), vj)
                lse_ref[:, h:h + 1] = m + jnp.log(denom)
        gv = g_ref[...]
        y_ref[...] = o_ref[...] * (gv * _sigmoid(gv))

    cur = lambda wd: pl.BlockSpec((WINDOW, wd), lambda n: (n, 0))
    prv = lambda wd: pl.BlockSpec((WINDOW, wd), lambda n: (jnp.maximum(n - 1, 0), 0))
    return pl.pallas_call(
        body, name="swa_fwd", grid=(nb,),
        in_specs=[cur(D_ATT), cur(D_ATT), cur(2 * D_KV), prv(2 * D_KV), cur(1), prv(1),
                  pl.BlockSpec((1, 2 * ATT_HD), lambda n: (0, 0)), pl.BlockSpec((1, ATT_QH), lambda n: (0, 0))],
        out_specs=[cur(D_ATT), cur(D_ATT), cur(ATT_QH)],
        out_shape=[jax.ShapeDtypeStruct((L, D_ATT), F32), jax.ShapeDtypeStruct((L, D_ATT), F32),
                   jax.ShapeDtypeStruct((L, ATT_QH), F32)],
        compiler_params=_cparams(("parallel",)),
    )(q, g, kv, kv, pos, pos, inv, sinks)


def _swa_bwd(dy, q, g, kv, o, lse, pos, inv, sinks):
    L = q.shape[0]
    nb = L // WINDOW
    scale = ATT_HD ** -0.5

    def body(dy_ref, q_ref, g_ref, kvc_ref, kvp_ref, o_ref, lse_ref, posc_ref, posp_ref, inv_ref, sink_ref,
             dq_ref, dg_ref, dkv_ref, dsink_ref, carry, dqbuf, dkbuf, dvbuf):
        n = pl.program_id(0)

        @pl.when(n == 0)
        def _():
            dsink_ref[...] = jnp.zeros_like(dsink_ref)

        @pl.when(n < nb)
        def _():
            tc = _rope_tables(posc_ref, inv_ref)
            tp = _rope_tables(posp_ref, inv_ref)
            qr = _rope(q_ref[...], tc)
            kc = _rope(kvc_ref[:, 0:D_KV], tc)
            kp = _rope(kvp_ref[:, 0:D_KV], tp)
            kk = _bf(jnp.concatenate([kp, kc], axis=0))
            vv = _bf(jnp.concatenate([kvp_ref[:, D_KV:2 * D_KV], kvc_ref[:, D_KV:2 * D_KV]], axis=0))
            valid = _swa_mask(n == 0)
            gv = g_ref[...]
            sg = _sigmoid(gv)
            dyv = dy_ref[...]
            ov = o_ref[...]
            dg_ref[...] = dyv * ov * (sg * (1.0 + gv * (1.0 - sg)))
            do = dyv * (gv * sg)
            lane16 = _iota2((1, ATT_QH), 1)
            dsink = jnp.zeros((1, ATT_QH), F32)
            for j in range(ATT_KVH):
                js = slice(ATT_HD * j, ATT_HD * (j + 1))
                kj = kk[:, js]
                vj = vv[:, js]
                dkj = jnp.zeros((2 * WINDOW, ATT_HD), F32)
                dvj = jnp.zeros((2 * WINDOW, ATT_HD), F32)
                for r in range(ATT_R):
                    h = j * ATT_R + r
                    hs = slice(ATT_HD * h, ATT_HD * (h + 1))
                    qh = _bf(qr[:, hs])
                    doh = do[:, hs]
                    dohb = _bf(doh)
                    lse_h = lse_ref[:, h:h + 1]
                    s = _dot_nt(qh, kj) * scale
                    p = jnp.exp(jnp.where(valid, s, NEG_BIG) - lse_h)
                    delta = jnp.sum(doh * ov[:, hs], axis=-1, keepdims=True)
                    dS = _bf(p * (_dot_nt(dohb, vj) - delta))
                    dqbuf[:, hs] = _dot(dS, kj) * scale
                    dkj = dkj + _dot_tn(dS, qh) * scale
                    dvj = dvj + _dot_tn(_bf(p), dohb)
                    psink = jnp.exp(sink_ref[:, h:h + 1] - lse_h)
                    dsink = dsink - jnp.sum(psink * delta, axis=0, keepdims=True) * (lane16 == h).astype(F32)
                dkbuf[:, js] = dkj
                dvbuf[:, js] = dvj
            dsink_ref[...] += dsink
            dq_ref[...] = _rope_t(dqbuf[...], tc)
            dkp = _rope_t(dkbuf[0:WINDOW, :], tp)
            dkc = _rope_t(dkbuf[WINDOW:2 * WINDOW, :], tc)

            @pl.when(n > 0)
            def _():
                dkv_ref[:, 0:D_KV] = carry[:, 0:D_KV] + dkp
                dkv_ref[:, D_KV:2 * D_KV] = carry[:, D_KV:2 * D_KV] + dvbuf[0:WINDOW, :]

            carry[:, 0:D_KV] = dkc
            carry[:, D_KV:2 * D_KV] = dvbuf[WINDOW:2 * WINDOW, :]

        @pl.when(n == nb)
        def _():
            dkv_ref[...] = carry[...]

    last = nb - 1
    cur = lambda wd: pl.BlockSpec((WINDOW, wd), lambda n: (jnp.minimum(n, last), 0))
    prv = lambda wd: pl.BlockSpec((WINDOW, wd), lambda n: (jnp.maximum(jnp.minimum(n, last) - 1, 0), 0))
    return pl.pallas_call(
        body, name="swa_bwd", grid=(nb + 1,),
        in_specs=[cur(D_ATT), cur(D_ATT), cur(D_ATT), cur(2 * D_KV), prv(2 * D_KV), cur(D_ATT), cur(ATT_QH), cur(1), prv(1),
                  pl.BlockSpec((1, 2 * ATT_HD), lambda n: (0, 0)), pl.BlockSpec((1, ATT_QH), lambda n: (0, 0))],
        out_specs=[cur(D_ATT), cur(D_ATT),
                   pl.BlockSpec((WINDOW, 2 * D_KV), lambda n: (jnp.maximum(n - 1, 0), 0)),
                   pl.BlockSpec((1, ATT_QH), lambda n: (0, 0))],
        out_shape=[jax.ShapeDtypeStruct((L, D_ATT), F32), jax.ShapeDtypeStruct((L, D_ATT), F32),
                   jax.ShapeDtypeStruct((L, 2 * D_KV), F32), jax.ShapeDtypeStruct((1, ATT_QH), F32)],
        scratch_shapes=[pltpu.VMEM((WINDOW, 2 * D_KV), F32), pltpu.VMEM((WINDOW, D_ATT), F32),
                        pltpu.VMEM((2 * WINDOW, D_KV), F32), pltpu.VMEM((2 * WINDOW, D_KV), F32)],
        compiler_params=_cparams(("arbitrary",)),
    )(dy, q, g, kv, kv, o, lse, pos, pos, inv, sinks)


def _out_ln_loss(y_ssd, y_att, x, target, w_out, ln_g, ln_b):
    L = x.shape[0]
    tm = 256
    inv_d = 1.0 / D_MODEL

    def body(ys_ref, ya_ref, x_ref, t_ref, w_ref, g_ref, b_ref, dr_ref, dys_ref, dya_ref, loss_ref, gg_ref, gb_ref):
        i = pl.program_id(0)

        @pl.when(i == 0)
        def _():
            loss_ref[...] = jnp.zeros_like(loss_ref)
            gg_ref[...] = jnp.zeros_like(gg_ref)
            gb_ref[...] = jnp.zeros_like(gb_ref)

        h = _dot(_bf(ys_ref[...]), w_ref[0:D_SSD, :]) + _dot(_bf(ya_ref[...]), w_ref[D_SSD:D_MIX, :])
        r = ALPHA * x_ref[...] + h
        mu = jnp.mean(r, axis=-1, keepdims=True)
        xc = r - mu
        rstd = lax.rsqrt(jnp.mean(xc * xc, axis=-1, keepdims=True) + LN_EPS)
        xhat = xc * rstd
        gam = g_ref[...]
        diff = xhat * gam + b_ref[...] - t_ref[...]
        part = jnp.sum(jnp.sum(diff * diff, axis=-1, keepdims=True), axis=0, keepdims=True)
        loss_ref[...] += (0.5 * inv_d) * part
        dout = diff * inv_d
        gg_ref[...] += jnp.sum(dout * xhat, axis=0, keepdims=True)
        gb_ref[...] += jnp.sum(dout, axis=0, keepdims=True)
        dxh = dout * gam
        dr = rstd * (dxh - jnp.mean(dxh, axis=-1, keepdims=True) - xhat * jnp.mean(dxh * xhat, axis=-1, keepdims=True))
        dr_ref[...] = dr
        drb = _bf(dr)
        dys_ref[...] = _dot_nt(drb, w_ref[0:D_SSD, :])
        dya_ref[...] = _dot_nt(drb, w_ref[D_SSD:D_MIX, :])

    row = pl.BlockSpec((tm, D_MODEL), lambda i: (i, 0))
    vec = pl.BlockSpec((1, D_MODEL), lambda i: (0, 0))
    return pl.pallas_call(
        body, name="out_ln_loss", grid=(L // tm,),
        in_specs=[row, row, row, row, pl.BlockSpec((D_MIX, D_MODEL), lambda i: (0, 0)), vec, vec],
        out_specs=[row, row, row, pl.BlockSpec((1, 128), lambda i: (0, 0)), vec, vec],
        out_shape=[jax.ShapeDtypeStruct((L, D_MODEL), F32)] * 3 + [jax.ShapeDtypeStruct((1, 128), F32)]
        + [jax.ShapeDtypeStruct((1, D_MODEL), F32)] * 2,
        compiler_params=_cparams(("arbitrary",)),
    )(y_ssd, y_att, x, target, w_out, ln_g, ln_b)


def _local_step(x, pos, target, w, w_out, conv_w, conv_b, dt_bias, a_log, d_skip, norm_w, sinks, ln_g, ln_b):
    inv8 = ROPE_THETA ** (-jnp.arange(0, ROPE_DIM, 2, dtype=F32) / ROPE_DIM)
    inv = jnp.tile(jnp.concatenate([inv8, inv8, jnp.zeros((ATT_HD - ROPE_DIM,), F32)]), 2).reshape(1, 2 * ATT_HD)

    z, g, q, xbc, kv, dtp, xb = _in_proj(x, w)
    y_ssd, y_pre, prev = _ssd_fwd(z, xbc, dtp, conv_w, conv_b, dt_bias, a_log, d_skip, norm_w)
    y_att, o, lse = _swa_fwd(q, g, kv, pos, inv, sinks)
    dr, dy_ssd, dy_att, loss, g_ln_g, g_ln_b = _out_ln_loss(y_ssd, y_att, x, target, w_out, ln_g, ln_b)
    gw_out_ssd = _matmul_tn(y_ssd, dr, 1024, "gw_out_ssd")
    gw_out_att = _matmul_tn(y_att, dr, 1024, "gw_out_att")
    dq, dg, dkv, g_sinks = _swa_bwd(dy_att, q, g, kv, o, lse, pos, inv, sinks)
    dz, dxbc, ddt, g_conv_w, g_conv_b, g_dt_bias, g_a_log, g_d_skip, g_norm_w = _ssd_bwd(
        dy_ssd, z, y_pre, xbc, dtp, prev, conv_w, conv_b, dt_bias, a_log, d_skip, norm_w)
    grad_x = _grad_x(dr, dz, dg, dq, dxbc, dkv, ddt, w)
    gw_z = _matmul_tn(dz, xb, 1024, "gw_z")
    gw_g = _matmul_tn(dg, xb, 1024, "gw_g")
    gw_q = _matmul_tn(dq, xb, 1024, "gw_q")
    gw_xbc = _matmul_tn(dxbc, xb, 1024, "gw_xbc")
    gw_kv = _matmul_tn(dkv, xb, 1024, "gw_kv")
    gw_dt = _matmul_tn(ddt, xb, 1024, "gw_dt")
    gw_in = jnp.concatenate([gw_z, gw_xbc, gw_dt[0:SSD_HEADS], gw_q, gw_kv, gw_g], axis=0)
    gw_out = jnp.concatenate([gw_out_ssd, gw_out_att], axis=0)
    small = dict(conv_w=g_conv_w, conv_b=g_conv_b, dt_bias=g_dt_bias, a_log=g_a_log, d_skip=g_d_skip,
                 ssd_norm_w=g_norm_w, attn_sinks=g_sinks, ln_g=g_ln_g, ln_b=g_ln_b)
    return loss[:, 0:1], grad_x, gw_in, gw_out, small


def _mesh_pos():
    return lax.axis_index("x"), lax.axis_index("y"), lax.axis_index("c")


def _gather_weights(w_in_s, w_out_s, conv_w_s):
    big = (w_in_s, w_out_s)

    def body(win_ref, wout_ref, cw_ref, owin_ref, owout_ref, ocw_ref, send_sems, recv_sems, small_send, small_recv,
             local_sems):
        x, y, c = _mesh_pos()
        me = 2 * x + y
        sibling = (x, y, 1 - c)
        chips = [(1 - x, y), (x, 1 - y), (1 - x, 1 - y)]
        locals_ = [pltpu.make_async_copy(src, dst.at[me], local_sems.at[t])
                   for t, (src, dst) in enumerate(((win_ref, owin_ref), (wout_ref, owout_ref), (cw_ref, ocw_ref)))]
        for cp in locals_:
            cp.start()
        started = []
        for t, (src, dst) in enumerate(((win_ref, owin_ref), (wout_ref, owout_ref))):
            hr = src.shape[0] // 2

            def half(ref, hc, hr=hr):
                return ref.at[pl.ds(hc * hr, hr), :]

            for j, (px, py) in enumerate(chips):
                cp = pltpu.make_async_remote_copy(
                    src_ref=half(src, c), dst_ref=half(dst.at[me], c), send_sem=send_sems.at[t, j],
                    recv_sem=recv_sems.at[t, j], device_id=(px, py, c), device_id_type=MESH)
                cp.start()
                started.append(cp)
        for j, (px, py) in enumerate(chips):
            cp = pltpu.make_async_remote_copy(
                src_ref=cw_ref, dst_ref=ocw_ref.at[me], send_sem=small_send.at[j], recv_sem=small_recv.at[j],
                device_id=(px, py, c), device_id_type=MESH)
            cp.start()
            started.append(cp)
        for t, (src, dst) in enumerate(((win_ref, owin_ref), (wout_ref, owout_ref))):
            hr = src.shape[0] // 2
            for j, (px, py) in enumerate(chips):
                src_chip = 2 * px + py
                blk = dst.at[src_chip].at[pl.ds(c * hr, hr), :]
                pltpu.make_async_remote_copy(
                    src_ref=blk, dst_ref=blk, send_sem=send_sems.at[t, j], recv_sem=recv_sems.at[t, j],
                    device_id=(px, py, c), device_id_type=MESH).wait_recv()
                cp = pltpu.make_async_remote_copy(
                    src_ref=blk, dst_ref=blk, send_sem=send_sems.at[t, 3 + j], recv_sem=recv_sems.at[t, 3 + j],
                    device_id=sibling, device_id_type=MESH)
                cp.start()
                started.append(cp)
        for t, (src, dst) in enumerate(((win_ref, owin_ref), (wout_ref, owout_ref))):
            hr = src.shape[0] // 2
            for j, (px, py) in enumerate(chips):
                src_chip = 2 * px + py
                blk = dst.at[src_chip].at[pl.ds((1 - c) * hr, hr), :]
                pltpu.make_async_remote_copy(
                    src_ref=blk, dst_ref=blk, send_sem=send_sems.at[t, 3 + j], recv_sem=recv_sems.at[t, 3 + j],
                    device_id=sibling, device_id_type=MESH).wait_recv()
        for j in range(3):
            pltpu.make_async_remote_copy(
                src_ref=cw_ref, dst_ref=ocw_ref.at[me], send_sem=small_send.at[j], recv_sem=small_recv.at[j],
                device_id=sibling, device_id_type=MESH).wait_recv()
        for cp in started:
            cp.wait_send()
        for cp in locals_:
            cp.wait()

    any_spec = pl.BlockSpec(memory_space=pl.ANY)
    return pl.pallas_call(
        body, name="gather_weights",
        in_specs=[any_spec] * 3, out_specs=[any_spec] * 3,
        out_shape=[jax.ShapeDtypeStruct((N_CHIPS,) + a.shape, a.dtype) for a in (w_in_s, w_out_s, conv_w_s)],
        scratch_shapes=[pltpu.SemaphoreType.DMA((2, 6)), pltpu.SemaphoreType.DMA((2, 6)),
                        pltpu.SemaphoreType.DMA((3,)), pltpu.SemaphoreType.DMA((3,)), pltpu.SemaphoreType.DMA((3,))],
    )(w_in_s, w_out_s, conv_w_s)


def _pair_exchange(gw_in, gw_out, small):
    k_small = small.shape[1]

    def body(gin_ref, gout_ref, sm_ref, min_ref, mout_ref, rin_ref, rout_ref, slots_ref, send_sems, recv_sems,
             small_send, small_recv, local_sems):
        x, y, c = _mesh_pos()
        me = 4 * x + 2 * y + c
        sibling = (x, y, 1 - c)
        mine = pltpu.make_async_copy(sm_ref, slots_ref.at[me], local_sems.at[2])
        mine.start()
        locals_ = [mine]
        started = []
        for t, (src, dst, own) in enumerate(((gin_ref, rin_ref, min_ref), (gout_ref, rout_ref, mout_ref))):
            hr = src.shape[1] // 2
            lc = pltpu.make_async_copy(src.at[:, pl.ds(c * hr, hr), :], own, local_sems.at[t])
            lc.start()
            locals_.append(lc)
            cp = pltpu.make_async_remote_copy(
                src_ref=src.at[:, pl.ds((1 - c) * hr, hr), :], dst_ref=dst, send_sem=send_sems.at[t],
                recv_sem=recv_sems.at[t], device_id=sibling, device_id_type=MESH)
            cp.start()
            started.append(cp)
        for k in range(1, 8):
            peer = (x ^ ((k >> 2) & 1), y ^ ((k >> 1) & 1), c ^ (k & 1))
            cp = pltpu.make_async_remote_copy(
                src_ref=sm_ref, dst_ref=slots_ref.at[me], send_sem=small_send.at[k - 1], recv_sem=small_recv.at[k - 1],
                device_id=peer, device_id_type=MESH)
            cp.start()
            started.append(cp)
        for t, (src, dst) in enumerate(((gin_ref, rin_ref), (gout_ref, rout_ref))):
            pltpu.make_async_remote_copy(
                src_ref=dst, dst_ref=dst, send_sem=send_sems.at[t], recv_sem=recv_sems.at[t],
                device_id=sibling, device_id_type=MESH).wait_recv()
        for k in range(1, 8):
            pltpu.make_async_remote_copy(
                src_ref=sm_ref, dst_ref=slots_ref.at[me], send_sem=small_send.at[k - 1], recv_sem=small_recv.at[k - 1],
                device_id=sibling, device_id_type=MESH).wait_recv()
        for cp in started:
            cp.wait_send()
        for lc in locals_:
            lc.wait()

    any_spec = pl.BlockSpec(memory_space=pl.ANY)
    half_in = jax.ShapeDtypeStruct((N_CHIPS, gw_in.shape[1] // 2, D_MODEL), F32)
    half_out = jax.ShapeDtypeStruct((N_CHIPS, gw_out.shape[1] // 2, D_MODEL), F32)
    return pl.pallas_call(
        body, name="pair_exchange",
        in_specs=[any_spec] * 3, out_specs=[any_spec] * 5,
        out_shape=[half_in, half_out, half_in, half_out, jax.ShapeDtypeStruct((8, 8, k_small), F32)],
        scratch_shapes=[pltpu.SemaphoreType.DMA((2,)), pltpu.SemaphoreType.DMA((2,)),
                        pltpu.SemaphoreType.DMA((7,)), pltpu.SemaphoreType.DMA((7,)), pltpu.SemaphoreType.DMA((3,))],
    )(gw_in, gw_out, small)


def _chip_exchange(s_in, s_out):
    def body(sin_ref, sout_ref, rin_ref, rout_ref, send_sems, recv_sems, local_sems):
        x, y, c = _mesh_pos()
        me = 2 * x + y
        chips = [(1 - x, y), (x, 1 - y), (1 - x, 1 - y)]
        started = []
        locals_ = []
        for t, (src, dst) in enumerate(((sin_ref, rin_ref), (sout_ref, rout_ref))):
            lc = pltpu.make_async_copy(src.at[me], dst.at[me], local_sems.at[t])
            lc.start()
            locals_.append(lc)
            for j, (px, py) in enumerate(chips):
                cp = pltpu.make_async_remote_copy(
                    src_ref=src.at[2 * px + py], dst_ref=dst.at[me], send_sem=send_sems.at[t, j],
                    recv_sem=recv_sems.at[t, j], device_id=(px, py, c), device_id_type=MESH)
                cp.start()
                started.append(cp)
        for t, (src, dst) in enumerate(((sin_ref, rin_ref), (sout_ref, rout_ref))):
            for j, (px, py) in enumerate(chips):
                blk = dst.at[2 * px + py]
                pltpu.make_async_remote_copy(
                    src_ref=blk, dst_ref=blk, send_sem=send_sems.at[t, j], recv_sem=recv_sems.at[t, j],
                    device_id=(px, py, c), device_id_type=MESH).wait_recv()
        for cp in started:
            cp.wait_send()
        for lc in locals_:
            lc.wait()

    any_spec = pl.BlockSpec(memory_space=pl.ANY)
    return pl.pallas_call(
        body, name="chip_exchange",
        in_specs=[any_spec] * 2, out_specs=[any_spec] * 2,
        out_shape=[jax.ShapeDtypeStruct(s_in.shape, s_in.dtype), jax.ShapeDtypeStruct(s_out.shape, s_out.dtype)],
        scratch_shapes=[pltpu.SemaphoreType.DMA((2, 3)), pltpu.SemaphoreType.DMA((2, 3)), pltpu.SemaphoreType.DMA((2,))],
    )(s_in, s_out)


def _pair_share(h_in, h_out):
    def body(hin_ref, hout_ref, fin_ref, fout_ref, send_sems, recv_sems, local_sems):
        x, y, c = _mesh_pos()
        sibling = (x, y, 1 - c)
        started = []
        locals_ = []
        for t, (src, dst) in enumerate(((hin_ref, fin_ref), (hout_ref, fout_ref))):
            hr = src.shape[0]
            lc = pltpu.make_async_copy(src, dst.at[pl.ds(c * hr, hr), :], local_sems.at[t])
            lc.start()
            locals_.append(lc)
            cp = pltpu.make_async_remote_copy(
                src_ref=src, dst_ref=dst.at[pl.ds(c * hr, hr), :], send_sem=send_sems.at[t], recv_sem=recv_sems.at[t],
                device_id=sibling, device_id_type=MESH)
            cp.start()
            started.append(cp)
        for t, (src, dst) in enumerate(((hin_ref, fin_ref), (hout_ref, fout_ref))):
            hr = src.shape[0]
            blk = dst.at[pl.ds((1 - c) * hr, hr), :]
            pltpu.make_async_remote_copy(
                src_ref=blk, dst_ref=blk, send_sem=send_sems.at[t], recv_sem=recv_sems.at[t],
                device_id=sibling, device_id_type=MESH).wait_recv()
        for cp in started:
            cp.wait_send()
        for lc in locals_:
            lc.wait()

    any_spec = pl.BlockSpec(memory_space=pl.ANY)
    return pl.pallas_call(
        body, name="pair_share",
        in_specs=[any_spec] * 2, out_specs=[any_spec] * 2,
        out_shape=[jax.ShapeDtypeStruct((2 * h_in.shape[0], h_in.shape[1]), F32),
                   jax.ShapeDtypeStruct((2 * h_out.shape[0], h_out.shape[1]), F32)],
        scratch_shapes=[pltpu.SemaphoreType.DMA((2,)), pltpu.SemaphoreType.DMA((2,)), pltpu.SemaphoreType.DMA((2,))],
    )(h_in, h_out)


def _pair_add(mine, recv, name):
    _, rows, C = mine.shape
    tc = 256

    def body(g_ref, r_ref, o_ref):
        o_ref[...] = _bf(g_ref[...] + r_ref[...])

    spec = pl.BlockSpec((1, rows, tc), lambda j, i: (j, 0, i))
    return pl.pallas_call(
        body, name=name, grid=(N_CHIPS, C // tc),
        in_specs=[spec, spec], out_specs=spec,
        out_shape=jax.ShapeDtypeStruct((N_CHIPS, rows, C), BF16),
        compiler_params=_cparams(("parallel", "parallel")),
    )(mine, recv)


def _chip_add(parts, name):
    _, rows, C = parts.shape
    tc = 256

    def body(r0, r1, r2, r3, o_ref):
        o_ref[...] = ((r0[0].astype(F32) + r1[0].astype(F32)) + r2[0].astype(F32)) + r3[0].astype(F32)

    slab = lambda j: pl.BlockSpec((1, rows, tc), lambda i, j=j: (j, 0, i))
    return pl.pallas_call(
        body, name=name, grid=(C // tc,),
        in_specs=[slab(j) for j in range(N_CHIPS)],
        out_specs=pl.BlockSpec((rows, tc), lambda i: (0, i)),
        out_shape=jax.ShapeDtypeStruct((rows, C), F32),
        compiler_params=_cparams(("parallel",)),
    )(parts, parts, parts, parts)


def _adamw_math(w, g, m, v):
    m = ADAM_B1 * m + (1.0 - ADAM_B1) * g
    v = ADAM_B2 * v + (1.0 - ADAM_B2) * (g * g)
    m_hat = m / (1.0 - ADAM_B1 ** ADAM_STEP)
    v_hat = v / (1.0 - ADAM_B2 ** ADAM_STEP)
    delta = -ADAM_LR * (m_hat / (jnp.sqrt(v_hat) + ADAM_EPS) + ADAM_WD * w)
    return delta, m, v


def _adamw(w, g, m, v, name):
    R, C = w.shape
    tc = 128

    def body(w_ref, g_ref, m_ref, v_ref, d_ref, nm_ref, nv_ref, go_ref):
        g = g_ref[0:R, :]
        d, nm, nv = _adamw_math(w_ref[...], g, m_ref[...], v_ref[...])
        d_ref[...] = d
        nm_ref[...] = nm
        nv_ref[...] = nv
        go_ref[...] = g

    spec = pl.BlockSpec((R, tc), lambda i: (0, i))
    gspec = pl.BlockSpec((g.shape[0], tc), lambda i: (0, i))
    return pl.pallas_call(
        body, name=name, grid=(C // tc,),
        in_specs=[spec, gspec, spec, spec], out_specs=[spec] * 4,
        out_shape=[jax.ShapeDtypeStruct((R, C), F32)] * 4,
        compiler_params=_cparams(("parallel",)),
    )(w, g, m, v)


def _sum_slots(slots):
    k = slots.shape[2]

    def body(s_ref, o_ref):
        acc = s_ref[0]
        for d in range(1, 8):
            acc = acc + s_ref[d]
        o_ref[...] = acc

    return pl.pallas_call(
        body, name="sum_slots", out_shape=jax.ShapeDtypeStruct((8, k), F32),
        compiler_params=_cparams(),
    )(slots)


SMALL_NAMES = ("conv_b", "ssd_norm_w", "ln_g", "ln_b", "dt_bias", "a_log", "d_skip", "attn_sinks")
SMALL_SIZES = (D_XBC, D_SSD, D_MODEL, D_MODEL, SSD_HEADS, SSD_HEADS, SSD_HEADS, ATT_QH)


def _pack_vectors(vals):
    parts = []
    for v in vals:
        n = v.shape[1]
        pad = (-n) % 128
        parts.append(jnp.pad(v, ((0, 0), (0, pad))) if pad else v)
    return jnp.concatenate(parts, axis=1)


def _unpack_vectors(row):
    out, off = [], 0
    for n in SMALL_SIZES:
        out.append(row[:, off:off + n])
        off += n + ((-n) % 128)
    return out


def kernel(x, positions, w_in, conv_w, conv_b, dt_bias, a_log, d_skip, ssd_norm_w, attn_sinks, w_out, ln_g, ln_b, loss_target, m_w_in, m_conv_w, m_conv_b, m_dt_bias, m_a_log, m_d_skip, m_ssd_norm_w, m_attn_sinks, m_w_out, m_ln_g, m_ln_b, v_w_in, v_conv_w, v_conv_b, v_dt_bias, v_a_log, v_d_skip, v_ssd_norm_w, v_attn_sinks, v_w_out, v_ln_g, v_ln_b):
    mx, my, mc = _mesh_pos()
    chip = 2 * mx + my
    L = x.shape[1]

    conv_w_s8 = jnp.pad(conv_w[0], ((0, 8 - CONV_K), (0, 0)))
    pad_rows = ((0, SLAB_ROWS - W_IN_COLS), (0, 0))
    w_in_t, m_w_in_t, v_w_in_t = w_in[0].T, m_w_in[0].T, v_w_in[0].T
    ag_in, ag_out, ag_cw = _gather_weights(jnp.pad(_bf(w_in_t), pad_rows), _bf(w_out[0]), conv_w_s8)
    w_full = jnp.concatenate([ag_in[j, 0:W_IN_COLS] for j in range(N_CHIPS)], axis=0)
    w = jnp.concatenate([
        w_full[O_Z:O_Z + D_SSD], w_full[O_G:O_G + D_ATT], w_full[O_Q:O_Q + D_ATT],
        w_full[O_XBC:O_XBC + D_XBC], w_full[O_K:O_K + 2 * D_KV], w_full[O_DT:O_DT + SSD_HEADS],
        jnp.zeros((DT_PAD - SSD_HEADS, D_MODEL), BF16)], axis=0)
    w_out_full = ag_out.reshape(D_MIX, D_MODEL)
    conv_w_full = jnp.concatenate([ag_cw[j, 0:CONV_K] for j in range(N_CHIPS)], axis=1)

    loss_part, grad_x, gw_in, gw_out, small = _local_step(
        x[0], positions[0].reshape(L, 1), loss_target[0], w, w_out_full, conv_w_full, conv_b, dt_bias, a_log, d_skip,
        ssd_norm_w, attn_sinks, ln_g, ln_b)

    vec = _pack_vectors([small[n] for n in SMALL_NAMES])
    top = jnp.concatenate([vec, jnp.pad(loss_part, ((0, 0), (0, 127)))], axis=1)
    right = jnp.pad(top, ((0, 7), (0, 0)))
    packed = jnp.concatenate([jnp.pad(small["conv_w"], ((0, 8 - CONV_K), (0, 0))), right], axis=1)

    gw_in_slabs = jnp.stack([jnp.pad(gw_in[W_IN_COLS * j:W_IN_COLS * (j + 1)], pad_rows) for j in range(N_CHIPS)])
    gw_out_slabs = gw_out.reshape(N_CHIPS, W_OUT_ROWS, D_MODEL)
    mine_in, mine_out, recv_in, recv_out, slots = _pair_exchange(gw_in_slabs, gw_out_slabs, packed)
    s_in = _pair_add(mine_in, recv_in, "pair_add_in")
    s_out = _pair_add(mine_out, recv_out, "pair_add_out")
    r_in, r_out = _chip_exchange(s_in, s_out)
    h_in = _chip_add(r_in, "chip_add_in")
    h_out = _chip_add(r_out, "chip_add_out")
    g_w_in_pad, g_w_out_full = _pair_share(h_in, h_out)
    tot = _sum_slots(slots)

    in_t = _adamw(w_in_t, g_w_in_pad, m_w_in_t, v_w_in_t, "adamw_w_in")
    d_w_in, nm_w_in, nv_w_in, g_w_in = [a.T for a in in_t]
    d_w_out, nm_w_out, nv_w_out, g_w_out = _adamw(w_out[0], g_w_out_full, m_w_out[0], v_w_out[0], "adamw_w_out")
    g_conv_w_all = tot[0:CONV_K, 0:D_XBC]
    g_conv_w = lax.dynamic_slice(g_conv_w_all, (0, chip * CONV_COLS), (CONV_K, CONV_COLS))
    g_vecs = _unpack_vectors(tot[0:1, D_XBC:D_XBC + 5120])
    loss = tot[0, D_XBC + 5120]

    grads = dict(zip(SMALL_NAMES, g_vecs))
    params = dict(conv_b=conv_b, ssd_norm_w=ssd_norm_w, ln_g=ln_g, ln_b=ln_b, dt_bias=dt_bias, a_log=a_log,
                  d_skip=d_skip, attn_sinks=attn_sinks)
    moms = dict(conv_b=m_conv_b, ssd_norm_w=m_ssd_norm_w, ln_g=m_ln_g, ln_b=m_ln_b, dt_bias=m_dt_bias, a_log=m_a_log,
                d_skip=m_d_skip, attn_sinks=m_attn_sinks)
    vars_ = dict(conv_b=v_conv_b, ssd_norm_w=v_ssd_norm_w, ln_g=v_ln_g, ln_b=v_ln_b, dt_bias=v_dt_bias, a_log=v_a_log,
                 d_skip=v_d_skip, attn_sinks=v_attn_sinks)
    def small_block(cw, vecs):
        a = jnp.pad(cw, ((0, 0), (0, 5120 - CONV_COLS)))
        return jnp.concatenate([a, _pack_vectors(vecs), jnp.zeros((3, 5120), F32)], axis=0)

    sw = small_block(conv_w[0], [params[n] for n in SMALL_NAMES])
    sg = small_block(g_conv_w, [grads[n] for n in SMALL_NAMES])
    sm = small_block(m_conv_w[0], [moms[n] for n in SMALL_NAMES])
    sv = small_block(v_conv_w[0], [vars_[n] for n in SMALL_NAMES])
    sd, snm, snv, _ = _adamw(sw, sg, sm, sv, "adamw_small")

    def split_small(blk):
        d = dict(zip(SMALL_NAMES, _unpack_vectors(blk[CONV_K:CONV_K + 1])))
        d["conv_w"] = blk[0:CONV_K, 0:CONV_COLS][None]
        return d

    delta, new_m, new_v = split_small(sd), split_small(snm), split_small(snv)
    grads["conv_w"] = g_conv_w[None]
    for dd, a_in, a_out in ((grads, g_w_in, g_w_out), (delta, d_w_in, d_w_out), (new_m, nm_w_in, nm_w_out),
                            (new_v, nv_w_in, nv_w_out)):
        dd["w_in"] = a_in[None]
        dd["w_out"] = a_out[None]
    order = ("w_in", "conv_w", "conv_b", "dt_bias", "a_log", "d_skip", "ssd_norm_w", "attn_sinks", "w_out", "ln_g", "ln_b")
    return (loss, grad_x[None], *[grads[n] for n in order], *[delta[n] for n in order], *[new_m[n] for n in order],
            *[new_v[n] for n in order])
```

```python
import functools

import numpy as np
import jax
import jax.numpy as jnp
from jax import lax
from jax.experimental import pallas as pl
from jax.experimental.pallas import tpu as pltpu

F32 = jnp.float32
BF16 = jnp.bfloat16
MESH = pl.DeviceIdType.MESH

D_MODEL = 1024
D_SSD = 1024
D_ATT = 1024
D_MIX = 2048
SSD_HEADS = 16
SSD_P = 64
SSD_GROUPS = 2
SSD_R = 8
SSD_N = 128
D_BC = 256
D_XBC = 1536
CONV_K = 4
CHUNK = 128
ATT_HD = 64
ATT_QH = 16
ATT_KVH = 4
ATT_R = 4
D_KV = 256
WINDOW = 128
ROPE_THETA = 500000.0
ROPE_DIM = 16
ALPHA = 2.0 ** 0.25
LN_EPS = 1e-5
RMS_EPS = 1e-5
D_IN_PROJ = 5136
O_Z, O_XBC, O_DT, O_Q, O_K, O_V, O_G = 0, 1024, 2560, 2576, 3600, 3856, 4112
P_Z, P_G, P_Q, P_XBC, P_KV, P_DT, P_END = 0, 1024, 2048, 3072, 4608, 5120, 5248
DT_PAD = 128
N_CHIPS = 4
W_IN_COLS = D_IN_PROJ // N_CHIPS
SLAB_ROWS = 1312
W_OUT_ROWS = D_MIX // N_CHIPS
CONV_COLS = D_XBC // N_CHIPS

ADAM_LR = 0.001
ADAM_B1 = 0.9
ADAM_B2 = 0.999
ADAM_EPS = 1e-08
ADAM_WD = 0.01
ADAM_STEP = 10

VMEM_LIMIT = 56 * 1024 * 1024
NEG_BIG = -1e30
HI = lax.Precision.HIGHEST


def _cparams(sem=None, **kw):
    if sem is not None:
        kw["dimension_semantics"] = sem
    return pltpu.CompilerParams(vmem_limit_bytes=VMEM_LIMIT, **kw)


def _dot(a, b):
    return jnp.dot(a, b, preferred_element_type=F32)


def _dot_nt(a, b):
    return lax.dot_general(a, b, (((1,), (1,)), ((), ())), preferred_element_type=F32)


def _dot_tn(a, b):
    return lax.dot_general(a, b, (((0,), (0,)), ((), ())), preferred_element_type=F32)


def _bf(a):
    return a.astype(BF16)


def _iota2(shape, dim):
    return lax.broadcasted_iota(jnp.int32, shape, dim)


def _to_rows(col):
    k = col.shape[1]
    eye = (_iota2((k, k), 0) == _iota2((k, k), 1)).astype(F32)
    return lax.dot_general(eye, col, (((1,), (1,)), ((), ())), preferred_element_type=F32, precision=HI)


def _to_cols(row):
    n = row.shape[1]
    eye = (_iota2((n, n), 0) == _iota2((n, n), 1)).astype(F32)
    return lax.dot_general(eye, row, (((1,), (1,)), ((), ())), preferred_element_type=F32, precision=HI)


def _sigmoid(x):
    return jax.nn.sigmoid(x)


def _in_proj(x, w):
    L = x.shape[0]
    tm = 256
    widths = (D_SSD, D_ATT, D_ATT, D_XBC, 2 * D_KV, DT_PAD)
    offs = (P_Z, P_G, P_Q, P_XBC, P_KV, P_DT)

    def body(x_ref, w_ref, z_ref, g_ref, q_ref, xbc_ref, kv_ref, dt_ref, xb_ref):
        xb = _bf(x_ref[...])
        xb_ref[...] = xb
        for o_ref, off, wd in zip((z_ref, g_ref, q_ref, xbc_ref, kv_ref, dt_ref), offs, widths):
            o_ref[...] = _dot_nt(xb, w_ref[off:off + wd, :])

    row = lambda wd: pl.BlockSpec((tm, wd), lambda i: (i, 0))
    return pl.pallas_call(
        body, name="in_proj", grid=(L // tm,),
        in_specs=[row(D_MODEL), pl.BlockSpec((P_END, D_MODEL), lambda i: (0, 0))],
        out_specs=[row(wd) for wd in widths] + [row(D_MODEL)],
        out_shape=[jax.ShapeDtypeStruct((L, wd), F32) for wd in widths] + [jax.ShapeDtypeStruct((L, D_MODEL), BF16)],
        compiler_params=_cparams(("parallel",)),
    )(x, w)


def _matmul_tn(a, b, tn, name):
    K, M = a.shape
    N = b.shape[1]
    tk = 512
    nk = K // tk

    def body(a_ref, b_ref, o_ref, acc_ref):
        k = pl.program_id(1)

        @pl.when(k == 0)
        def _():
            acc_ref[...] = jnp.zeros_like(acc_ref)

        acc_ref[...] += _dot_tn(_bf(a_ref[...]), _bf(b_ref[...]))

        @pl.when(k == nk - 1)
        def _():
            o_ref[...] = acc_ref[...]

    return pl.pallas_call(
        body, name=name, grid=(N // tn, nk),
        in_specs=[pl.BlockSpec((tk, M), lambda j, k: (k, 0)), pl.BlockSpec((tk, tn), lambda j, k: (k, j))],
        out_specs=pl.BlockSpec((M, tn), lambda j, k: (0, j)),
        out_shape=jax.ShapeDtypeStruct((M, N), F32),
        scratch_shapes=[pltpu.VMEM((M, tn), F32)],
        compiler_params=_cparams(("parallel", "arbitrary")),
    )(a, b)


def _grad_x(dr, dz, dg, dq, dxbc, dkv, ddt, w):
    L = dr.shape[0]
    tm = 256
    widths = (D_SSD, D_ATT, D_ATT, D_XBC, 2 * D_KV, DT_PAD)
    offs = (P_Z, P_G, P_Q, P_XBC, P_KV, P_DT)

    def body(dr_ref, dz_ref, dg_ref, dq_ref, dxbc_ref, dkv_ref, ddt_ref, w_ref, o_ref):
        acc = ALPHA * dr_ref[...]
        for p_ref, off, wd in zip((dz_ref, dg_ref, dq_ref, dxbc_ref, dkv_ref, ddt_ref), offs, widths):
            acc = acc + _dot(_bf(p_ref[...]), w_ref[off:off + wd, :])
        o_ref[...] = acc

    row = lambda wd: pl.BlockSpec((tm, wd), lambda i: (i, 0))
    return pl.pallas_call(
        body, name="grad_x", grid=(L // tm,),
        in_specs=[row(D_MODEL)] + [row(wd) for wd in widths] + [pl.BlockSpec((P_END, D_MODEL), lambda i: (0, 0))],
        out_specs=row(D_MODEL),
        out_shape=jax.ShapeDtypeStruct((L, D_MODEL), F32),
        compiler_params=_cparams(("parallel",)),
    )(dr, dz, dg, dq, dxbc, dkv, ddt, w)


def _ssd_chunk_pre(first, xbc_ref, tail_ref, dt_ref, cw_ref, cb_ref, dtb_ref, alog_ref, ext):
    tail = jnp.where(first, 0.0, tail_ref[...])
    ext[0:8, :] = tail
    ext[8:8 + CHUNK, :] = xbc_ref[...]
    u = cb_ref[...] + cw_ref[0:1, :] * ext[pl.ds(5, CHUNK), :]
    for k in range(1, CONV_K):
        u = u + cw_ref[k:k + 1, :] * ext[pl.ds(5 + k, CHUNK), :]
    sig = _sigmoid(u)
    xbc = u * sig
    dtraw = dt_ref[:, 0:SSD_HEADS] + dtb_ref[...]
    dt = jax.nn.softplus(dtraw)
    A = -jnp.exp(alog_ref[...])
    a = dt * A
    tril = (_iota2((CHUNK, CHUNK), 0) >= _iota2((CHUNK, CHUNK), 1)).astype(F32)
    acs = jnp.dot(tril, a, preferred_element_type=F32, precision=HI)
    acs_row = _to_rows(acs)
    return u, sig, xbc, dtraw, dt, A, acs, acs_row


def _ssd_fwd(z, xbc, dtp, conv_w, conv_b, dt_bias, a_log, d_skip, norm_w):
    L = z.shape[0]
    nc = L // CHUNK

    def body(z_ref, xbc_ref, tail_ref, dt_ref, cw_ref, cb_ref, dtb_ref, alog_ref, dsk_ref, nw_ref,
             y_ref, ypre_ref, prev_ref, state, ext, ybuf):
        c = pl.program_id(0)

        @pl.when(c == 0)
        def _():
            state[...] = jnp.zeros_like(state)

        u, sig, xbcv, dtraw, dt, A, acs, acs_row = _ssd_chunk_pre(
            c == 0, xbc_ref, tail_ref, dt_ref, cw_ref, cb_ref, dtb_ref, alog_ref, ext)
        prev_ref[0] = state[...]
        causal = _iota2((CHUNK, CHUNK), 0) >= _iota2((CHUNK, CHUNK), 1)
        alast = acs[CHUNK - 1:CHUNK, :]
        for g in range(SSD_GROUPS):
            Bg = _bf(xbcv[:, D_SSD + SSD_N * g:D_SSD + SSD_N * (g + 1)])
            Cg = _bf(xbcv[:, D_SSD + D_BC + SSD_N * g:D_SSD + D_BC + SSD_N * (g + 1)])
            cb = _dot_nt(Cg, Bg)
            for r in range(SSD_R):
                h = g * SSD_R + r
                hs = slice(SSD_P * h, SSD_P * (h + 1))
                acs_c = acs[:, h:h + 1]
                seg = acs_c - acs_row[h:h + 1, :]
                Lm = jnp.where(causal, jnp.exp(jnp.where(causal, seg, 0.0)), 0.0)
                M = cb * Lm
                xh = xbcv[:, hs]
                X = xh * dt[:, h:h + 1]
                prev_h = state[hs, :]
                ydiag = _dot(_bf(M), _bf(X))
                yoff = _dot_nt(Cg, _bf(prev_h)) * jnp.exp(acs_c)
                al = alast[:, h:h + 1]
                Xd = X * jnp.exp(al - acs_c)
                state[hs, :] = prev_h * jnp.exp(al) + _dot_tn(_bf(Xd), Bg)
                ybuf[:, hs] = ydiag + yoff + dsk_ref[:, h:h + 1] * xh
        y = ybuf[...]
        ypre_ref[...] = y
        zv = z_ref[...]
        yf = y * (zv * _sigmoid(zv))
        half = D_SSD // SSD_GROUPS
        for g in range(SSD_GROUPS):
            gs = slice(half * g, half * (g + 1))
            yg = yf[:, gs]
            ms = jnp.mean(yg * yg, axis=-1, keepdims=True)
            y_ref[:, gs] = yg * lax.rsqrt(ms + RMS_EPS) * nw_ref[:, gs]

    full = lambda shape: pl.BlockSpec(shape, lambda c: (0, 0))
    return pl.pallas_call(
        body, name="ssd_fwd", grid=(nc,),
        in_specs=[
            pl.BlockSpec((CHUNK, D_SSD), lambda c: (c, 0)),
            pl.BlockSpec((CHUNK, D_XBC), lambda c: (c, 0)),
            pl.BlockSpec((8, D_XBC), lambda c: (jnp.maximum(c * (CHUNK // 8) - 1, 0), 0)),
            pl.BlockSpec((CHUNK, DT_PAD), lambda c: (c, 0)),
            full((CONV_K, D_XBC)), full((1, D_XBC)), full((1, SSD_HEADS)), full((1, SSD_HEADS)), full((1, SSD_HEADS)),
            full((1, D_SSD)),
        ],
        out_specs=[
            pl.BlockSpec((CHUNK, D_SSD), lambda c: (c, 0)),
            pl.BlockSpec((CHUNK, D_SSD), lambda c: (c, 0)),
            pl.BlockSpec((1, SSD_HEADS * SSD_P, SSD_N), lambda c: (c, 0, 0)),
        ],
        out_shape=[
            jax.ShapeDtypeStruct((L, D_SSD), F32),
            jax.ShapeDtypeStruct((L, D_SSD), F32),
            jax.ShapeDtypeStruct((nc, SSD_HEADS * SSD_P, SSD_N), F32),
        ],
        scratch_shapes=[
            pltpu.VMEM((SSD_HEADS * SSD_P, SSD_N), F32),
            pltpu.VMEM((CHUNK + 8, D_XBC), F32),
            pltpu.VMEM((CHUNK, D_SSD), F32),
        ],
        compiler_params=_cparams(("arbitrary",)),
    )(z, xbc, xbc, dtp, conv_w, conv_b, dt_bias, a_log, d_skip, norm_w)


def _ssd_bwd(dy, z, ypre, xbc, dtp, prev, conv_w, conv_b, dt_bias, a_log, d_skip, norm_w):
    L = z.shape[0]
    nc = L // CHUNK

    def body(dy_ref, z_ref, ypre_ref, xbc_ref, tail_ref, dt_ref, prev_ref, cw_ref, cb_ref, dtb_ref, alog_ref, dsk_ref,
             nw_ref, dz_ref, dxbc_ref, ddt_ref, gcw_ref, gcb_ref, gdtb_ref, galog_ref, gdsk_ref, gnw_ref,
             dstate, dhead, ext, ext2, dpost):
        i = pl.program_id(0)
        c = nc - 1 - i

        @pl.when(i == 0)
        def _():
            dstate[...] = jnp.zeros_like(dstate)
            dhead[...] = jnp.zeros_like(dhead)
            gcw_ref[...] = jnp.zeros_like(gcw_ref)
            gcb_ref[...] = jnp.zeros_like(gcb_ref)
            gdtb_ref[...] = jnp.zeros_like(gdtb_ref)
            galog_ref[...] = jnp.zeros_like(galog_ref)
            gdsk_ref[...] = jnp.zeros_like(gdsk_ref)
            gnw_ref[...] = jnp.zeros_like(gnw_ref)

        u, sig, xbcv, dtraw, dt, A, acs, acs_row = _ssd_chunk_pre(
            c == 0, xbc_ref, tail_ref, dt_ref, cw_ref, cb_ref, dtb_ref, alog_ref, ext)

        zv = z_ref[...]
        ypre = ypre_ref[...]
        dyn = dy_ref[...]
        sz = _sigmoid(zv)
        silu_z = zv * sz
        yf = ypre * silu_z
        half = D_SSD // SSD_GROUPS
        dyf_parts = []
        for g in range(SSD_GROUPS):
            gs = slice(half * g, half * (g + 1))
            yg = yf[:, gs]
            rstd = lax.rsqrt(jnp.mean(yg * yg, axis=-1, keepdims=True) + RMS_EPS)
            dout = dyn[:, gs]
            gnw_ref[:, gs] += jnp.sum(dout * yg * rstd, axis=0, keepdims=True)
            dyhat = dout * nw_ref[:, gs]
            dyf_parts.append(rstd * (dyhat - yg * (rstd * rstd) * jnp.mean(dyhat * yg, axis=-1, keepdims=True)))
        dyf = jnp.concatenate(dyf_parts, axis=1)
        dz_ref[...] = dyf * ypre * (sz * (1.0 + zv * (1.0 - sz)))
        dypre = dyf * silu_z

        causal = _iota2((CHUNK, CHUNK), 0) >= _iota2((CHUNK, CHUNK), 1)
        alast = acs[CHUNK - 1:CHUNK, :]
        lane16 = _iota2((1, SSD_HEADS), 1)
        sub16 = _iota2((SSD_HEADS, 1), 0)
        dacs_col = jnp.zeros((CHUNK, SSD_HEADS), F32)
        dacs_row = jnp.zeros((SSD_HEADS, CHUNK), F32)
        ddt_col = jnp.zeros((CHUNK, SSD_HEADS), F32)
        dalast = jnp.zeros((1, SSD_HEADS), F32)
        gdsk = jnp.zeros((1, SSD_HEADS), F32)
        for g in range(SSD_GROUPS):
            bs = slice(D_SSD + SSD_N * g, D_SSD + SSD_N * (g + 1))
            cs = slice(D_SSD + D_BC + SSD_N * g, D_SSD + D_BC + SSD_N * (g + 1))
            Bg = _bf(xbcv[:, bs])
            Cg = _bf(xbcv[:, cs])
            cb = _dot_nt(Cg, Bg)
            dcb = jnp.zeros((CHUNK, CHUNK), F32)
            dB = jnp.zeros((CHUNK, SSD_N), F32)
            dC = jnp.zeros((CHUNK, SSD_N), F32)
            for r in range(SSD_R):
                h = g * SSD_R + r
                hs = slice(SSD_P * h, SSD_P * (h + 1))
                onehot = (lane16 == h).astype(F32)
                acs_c = acs[:, h:h + 1]
                seg = acs_c - acs_row[h:h + 1, :]
                Lm = jnp.where(causal, jnp.exp(jnp.where(causal, seg, 0.0)), 0.0)
                M = cb * Lm
                xh = xbcv[:, hs]
                dth = dt[:, h:h + 1]
                X = xh * dth
                Xb = _bf(X)
                dyh = dypre[:, hs]
                dyb = _bf(dyh)
                prev_h = prev_ref[0, hs, :]
                prevb = _bf(prev_h)
                dnext = dstate[hs, :]
                dnextb = _bf(dnext)
                al = alast[:, h:h + 1]
                eacs = jnp.exp(acs_c)
                eal = jnp.exp(al)
                dsd = jnp.exp(al - acs_c)
                G = _bf(dyh * eacs)
                dstate[hs, :] = dnext * eal + _dot_tn(G, Cg)
                dC = dC + _dot(G, prevb)
                yoff = _dot_nt(Cg, prevb) * eacs
                dacs_h = jnp.sum(dyh * yoff, axis=-1, keepdims=True)
                BdN = _dot_nt(Bg, dnextb)
                dX = dsd * BdN
                dB = dB + _dot(_bf(X * dsd), dnextb)
                t = jnp.sum(X * BdN, axis=-1, keepdims=True) * dsd
                dacs_h = dacs_h - t
                dal = jnp.sum(t, axis=0, keepdims=True) + jnp.sum(
                    jnp.sum(dnext * prev_h, axis=-1, keepdims=True), axis=0, keepdims=True) * eal
                dM = _dot_nt(dyb, Xb)
                dX = dX + _dot_tn(_bf(M), dyb)
                dseg = dM * M
                dcb = dcb + dM * Lm
                dacs_h = dacs_h + jnp.sum(dseg, axis=-1, keepdims=True)
                dacs_row = dacs_row - jnp.sum(dseg, axis=0, keepdims=True) * (sub16 == h).astype(F32)
                dacs_col = dacs_col + dacs_h * onehot
                dalast = dalast + dal * onehot
                ddt_col = ddt_col + jnp.sum(dX * xh, axis=-1, keepdims=True) * onehot
                gdsk = gdsk + jnp.sum(jnp.sum(dyh * xh, axis=-1, keepdims=True), axis=0, keepdims=True) * onehot
                dpost[:, hs] = dX * dth + dsk_ref[:, h:h + 1] * dyh
            dcbb = _bf(dcb)
            dpost[:, bs] = dB + _dot_tn(dcbb, Cg)
            dpost[:, cs] = dC + _dot(dcbb, Bg)

        is_last = (_iota2((CHUNK, 1), 0) == CHUNK - 1).astype(F32)
        dacs = dacs_col + _to_cols(dacs_row) + is_last * dalast
        triu = (_iota2((CHUNK, CHUNK), 0) <= _iota2((CHUNK, CHUNK), 1)).astype(F32)
        da = jnp.dot(triu, dacs, preferred_element_type=F32, precision=HI)
        ddt_tot = ddt_col + da * A
        galog_ref[...] += jnp.sum(da * dt, axis=0, keepdims=True) * A
        ddtraw = ddt_tot * _sigmoid(dtraw)
        gdtb_ref[...] += jnp.sum(ddtraw, axis=0, keepdims=True)
        gdsk_ref[...] += gdsk
        ddt_ref[...] = jnp.zeros_like(ddt_ref)
        ddt_ref[:, 0:SSD_HEADS] = ddtraw

        dconv = dpost[...] * (sig * (1.0 + u * (1.0 - sig)))
        gcb_ref[...] += jnp.sum(dconv, axis=0, keepdims=True)
        for k in range(CONV_K):
            gcw_ref[k:k + 1, :] += jnp.sum(dconv * ext[pl.ds(5 + k, CHUNK), :], axis=0, keepdims=True)
        ext2[0:CHUNK, :] = dconv
        ext2[CHUNK:CHUNK + 8, :] = dhead[...]
        dx = cw_ref[CONV_K - 1:CONV_K, :] * dconv
        for k in range(CONV_K - 1):
            dx = dx + cw_ref[k:k + 1, :] * ext2[pl.ds(CONV_K - 1 - k, CHUNK), :]
        dxbc_ref[...] = dx
        dhead[...] = dconv[0:8, :]

    full = lambda shape: pl.BlockSpec(shape, lambda i: (0, 0))
    rev = lambda wd: pl.BlockSpec((CHUNK, wd), lambda i: (nc - 1 - i, 0))
    return pl.pallas_call(
        body, name="ssd_bwd", grid=(nc,),
        in_specs=[
            rev(D_SSD), rev(D_SSD), rev(D_SSD), rev(D_XBC),
            pl.BlockSpec((8, D_XBC), lambda i: (jnp.maximum((nc - 1 - i) * (CHUNK // 8) - 1, 0), 0)),
            rev(DT_PAD),
            pl.BlockSpec((1, SSD_HEADS * SSD_P, SSD_N), lambda i: (nc - 1 - i, 0, 0)),
            full((CONV_K, D_XBC)), full((1, D_XBC)), full((1, SSD_HEADS)), full((1, SSD_HEADS)), full((1, SSD_HEADS)),
            full((1, D_SSD)),
        ],
        out_specs=[
            rev(D_SSD), rev(D_XBC), rev(DT_PAD),
            full((CONV_K, D_XBC)), full((1, D_XBC)), full((1, SSD_HEADS)), full((1, SSD_HEADS)), full((1, SSD_HEADS)),
            full((1, D_SSD)),
        ],
        out_shape=[
            jax.ShapeDtypeStruct((L, D_SSD), F32), jax.ShapeDtypeStruct((L, D_XBC), F32),
            jax.ShapeDtypeStruct((L, DT_PAD), F32),
            jax.ShapeDtypeStruct((CONV_K, D_XBC), F32), jax.ShapeDtypeStruct((1, D_XBC), F32),
            jax.ShapeDtypeStruct((1, SSD_HEADS), F32), jax.ShapeDtypeStruct((1, SSD_HEADS), F32),
            jax.ShapeDtypeStruct((1, SSD_HEADS), F32), jax.ShapeDtypeStruct((1, D_SSD), F32),
        ],
        scratch_shapes=[
            pltpu.VMEM((SSD_HEADS * SSD_P, SSD_N), F32),
            pltpu.VMEM((8, D_XBC), F32),
            pltpu.VMEM((CHUNK + 8, D_XBC), F32),
            pltpu.VMEM((CHUNK + 8, D_XBC), F32),
            pltpu.VMEM((CHUNK, D_XBC), F32),
        ],
        compiler_params=_cparams(("arbitrary",)),
    )(dy, z, ypre, xbc, xbc, dtp, prev, conv_w, conv_b, dt_bias, a_log, d_skip, norm_w)


def _rope_tables(pos_ref, inv_ref):
    ang = pos_ref[...].astype(F32) * inv_ref[...]
    d = _iota2((1, 2 * ATT_HD), 1) % ATT_HD
    s = jnp.sin(ang)
    return jnp.cos(ang), jnp.where(d < ROPE_DIM // 2, -s, 0.0), jnp.where((d >= ROPE_DIM // 2) & (d < ROPE_DIM), s, 0.0)


def _rope(t, tabs):
    c, s1, s2 = tabs
    n = t.shape[1]
    rep = n // c.shape[1]
    return (t * jnp.tile(c, (1, rep)) + pltpu.roll(t, n - ROPE_DIM // 2, 1) * jnp.tile(s1, (1, rep))
            + pltpu.roll(t, ROPE_DIM // 2, 1) * jnp.tile(s2, (1, rep)))


def _rope_t(t, tabs):
    c, s1, s2 = tabs
    n = t.shape[1]
    rep = n // c.shape[1]
    return (t * jnp.tile(c, (1, rep)) + pltpu.roll(t * jnp.tile(s1, (1, rep)), ROPE_DIM // 2, 1)
            + pltpu.roll(t * jnp.tile(s2, (1, rep)), n - ROPE_DIM // 2, 1))


def _swa_mask(first):
    qi = _iota2((WINDOW, 2 * WINDOW), 0)
    si = _iota2((WINDOW, 2 * WINDOW), 1)
    band = (si > qi) & (si <= qi + WINDOW)
    return band & (jnp.logical_not(first) | (si >= WINDOW))


def _swa_fwd(q, g, kv, pos, inv, sinks):
    L = q.shape[0]
    nb = L // WINDOW
    scale = ATT_HD ** -0.5

    def body(q_ref, g_ref, kvc_ref, kvp_ref, posc_ref, posp_ref, inv_ref, sink_ref, y_ref, o_ref, lse_ref):
        n = pl.program_id(0)
        tc = _rope_tables(posc_ref, inv_ref)
        tp = _rope_tables(posp_ref, inv_ref)
        qr = _rope(q_ref[...], tc)
        kc = _rope(kvc_ref[:, 0:D_KV], tc)
        kp = _rope(kvp_ref[:, 0:D_KV], tp)
        kk = _bf(jnp.concatenate([kp, kc], axis=0))
        vv = _bf(jnp.concatenate([kvp_ref[:, D_KV:2 * D_KV], kvc_ref[:, D_KV:2 * D_KV]], axis=0))
        valid = _swa_mask(n == 0)
        for j in range(ATT_KVH):
            js = slice(ATT_HD * j, ATT_HD * (j + 1))
            kj = kk[:, js]
            vj = vv[:, js]
            for r in range(ATT_R):
                h = j * ATT_R + r
                hs = slice(ATT_HD * h, ATT_HD * (h + 1))
                s = _dot_nt(_bf(qr[:, hs]), kj) * scale
                s = jnp.where(valid, s, NEG_BIG)
                sink = sink_ref[:, h:h + 1]
                m = jnp.maximum(jnp.max(s, axis=-1, keepdims=True), sink)
                p = jnp.exp(s - m)
                denom = jnp.sum(p, axis=-1, keepdims=True) + jnp.exp(sink - m)
                o_ref[:, hs] = _dot(_bf(p / denom), vj)
                lse_ref[:, h:h + 1] = m + jnp.log(denom)
        gv = g_ref[...]
        y_ref[...] = o_ref[...] * (gv * _sigmoid(gv))

    cur = lambda wd: pl.BlockSpec((WINDOW, wd), lambda n: (n, 0))
    prv = lambda wd: pl.BlockSpec((WINDOW, wd), lambda n: (jnp.maximum(n - 1, 0), 0))
    return pl.pallas_call(
        body, name="swa_fwd", grid=(nb,),
        in_specs=[cur(D_ATT), cur(D_ATT), cur(2 * D_KV), prv(2 * D_KV), cur(1), prv(1),
                  pl.BlockSpec((1, 2 * ATT_HD), lambda n: (0, 0)), pl.BlockSpec((1, ATT_QH), lambda n: (0, 0))],
        out_specs=[cur(D_ATT), cur(D_ATT), cur(ATT_QH)],
        out_shape=[jax.ShapeDtypeStruct((L, D_ATT), F32), jax.ShapeDtypeStruct((L, D_ATT), F32),
                   jax.ShapeDtypeStruct((L, ATT_QH), F32)],
        compiler_params=_cparams(("parallel",)),
    )(q, g, kv, kv, pos, pos, inv, sinks)


def _swa_bwd(dy, q, g, kv, o, lse, pos, inv, sinks):
    L = q.shape[0]
    nb = L // WINDOW
    scale = ATT_HD ** -0.5

    def body(dy_ref, q_ref, g_ref, kvc_ref, kvp_ref, o_ref, lse_ref, posc_ref, posp_ref, inv_ref, sink_ref,
             dq_ref, dg_ref, dkv_ref, dsink_ref, carry, dqbuf, dkbuf, dvbuf):
        n = pl.program_id(0)

        @pl.when(n == 0)
        def _():
            dsink_ref[...] = jnp.zeros_like(dsink_ref)

        @pl.when(n < nb)
        def _():
            tc = _rope_tables(posc_ref, inv_ref)
            tp = _rope_tables(posp_ref, inv_ref)
            qr = _rope(q_ref[...], tc)
            kc = _rope(kvc_ref[:, 0:D_KV], tc)
            kp = _rope(kvp_ref[:, 0:D_KV], tp)
            kk = _bf(jnp.concatenate([kp, kc], axis=0))
            vv = _bf(jnp.concatenate([kvp_ref[:, D_KV:2 * D_KV], kvc_ref[:, D_KV:2 * D_KV]], axis=0))
            valid = _swa_mask(n == 0)
            gv = g_ref[...]
            sg = _sigmoid(gv)
            dyv = dy_ref[...]
            ov = o_ref[...]
            dg_ref[...] = dyv * ov * (sg * (1.0 + gv * (1.0 - sg)))
            do = dyv * (gv * sg)
            lane16 = _iota2((1, ATT_QH), 1)
            dsink = jnp.zeros((1, ATT_QH), F32)
            for j in range(ATT_KVH):
                js = slice(ATT_HD * j, ATT_HD * (j + 1))
                kj = kk[:, js]
                vj = vv[:, js]
                dkj = jnp.zeros((2 * WINDOW, ATT_HD), F32)
                dvj = jnp.zeros((2 * WINDOW, ATT_HD), F32)
                for r in range(ATT_R):
                    h = j * ATT_R + r
                    hs = slice(ATT_HD * h, ATT_HD * (h + 1))
                    qh = _bf(qr[:, hs])
                    doh = do[:, hs]
                    dohb = _bf(doh)
                    lse_h = lse_ref[:, h:h + 1]
                    s = _dot_nt(qh, kj) * scale
                    p = jnp.exp(jnp.where(valid, s, NEG_BIG) - lse_h)
                    delta = jnp.sum(doh * ov[:, hs], axis=-1, keepdims=True)
                    dS = _bf(p * (_dot_nt(dohb, vj) - delta))
                    dqbuf[:, hs] = _dot(dS, kj) * scale
                    dkj = dkj + _dot_tn(dS, qh) * scale
                    dvj = dvj + _dot_tn(_bf(p), dohb)
                    psink = jnp.exp(sink_ref[:, h:h + 1] - lse_h)
                    dsink = dsink - jnp.sum(psink * delta, axis=0, keepdims=True) * (lane16 == h).astype(F32)
                dkbuf[:, js] = dkj
                dvbuf[:, js] = dvj
            dsink_ref[...] += dsink
            dq_ref[...] = _rope_t(dqbuf[...], tc)
            dkp = _rope_t(dkbuf[0:WINDOW, :], tp)
            dkc = _rope_t(dkbuf[WINDOW:2 * WINDOW, :], tc)

            @pl.when(n > 0)
            def _():
                dkv_ref[:, 0:D_KV] = carry[:, 0:D_KV] + dkp
                dkv_ref[:, D_KV:2 * D_KV] = carry[:, D_KV:2 * D_KV] + dvbuf[0:WINDOW, :]

            carry[:, 0:D_KV] = dkc
            carry[:, D_KV:2 * D_KV] = dvbuf[WINDOW:2 * WINDOW, :]

        @pl.when(n == nb)
        def _():
            dkv_ref[...] = carry[...]

    last = nb - 1
    cur = lambda wd: pl.BlockSpec((WINDOW, wd), lambda n: (jnp.minimum(n, last), 0))
    prv = lambda wd: pl.BlockSpec((WINDOW, wd), lambda n: (jnp.maximum(jnp.minimum(n, last) - 1, 0), 0))
    return pl.pallas_call(
        body, name="swa_bwd", grid=(nb + 1,),
        in_specs=[cur(D_ATT), cur(D_ATT), cur(D_ATT), cur(2 * D_KV), prv(2 * D_KV), cur(D_ATT), cur(ATT_QH), cur(1), prv(1),
                  pl.BlockSpec((1, 2 * ATT_HD), lambda n: (0, 0)), pl.BlockSpec((1, ATT_QH), lambda n: (0, 0))],
        out_specs=[cur(D_ATT), cur(D_ATT),
                   pl.BlockSpec((WINDOW, 2 * D_KV), lambda n: (jnp.maximum(n - 1, 0), 0)),
                   pl.BlockSpec((1, ATT_QH), lambda n: (0, 0))],
        out_shape=[jax.ShapeDtypeStruct((L, D_ATT), F32), jax.ShapeDtypeStruct((L, D_ATT), F32),
                   jax.ShapeDtypeStruct((L, 2 * D_KV), F32), jax.ShapeDtypeStruct((1, ATT_QH), F32)],
        scratch_shapes=[pltpu.VMEM((WINDOW, 2 * D_KV), F32), pltpu.VMEM((WINDOW, D_ATT), F32),
                        pltpu.VMEM((2 * WINDOW, D_KV), F32), pltpu.VMEM((2 * WINDOW, D_KV), F32)],
        compiler_params=_cparams(("arbitrary",)),
    )(dy, q, g, kv, kv, o, lse, pos, pos, inv, sinks)


def _out_ln_loss(y_ssd, y_att, x, target, w_out, ln_g, ln_b):
    L = x.shape[0]
    tm = 256
    inv_d = 1.0 / D_MODEL

    def body(ys_ref, ya_ref, x_ref, t_ref, w_ref, g_ref, b_ref, dr_ref, dys_ref, dya_ref, loss_ref, gg_ref, gb_ref):
        i = pl.program_id(0)

        @pl.when(i == 0)
        def _():
            loss_ref[...] = jnp.zeros_like(loss_ref)
            gg_ref[...] = jnp.zeros_like(gg_ref)
            gb_ref[...] = jnp.zeros_like(gb_ref)

        h = _dot(_bf(ys_ref[...]), w_ref[0:D_SSD, :]) + _dot(_bf(ya_ref[...]), w_ref[D_SSD:D_MIX, :])
        r = ALPHA * x_ref[...] + h
        mu = jnp.mean(r, axis=-1, keepdims=True)
        xc = r - mu
        rstd = lax.rsqrt(jnp.mean(xc * xc, axis=-1, keepdims=True) + LN_EPS)
        xhat = xc * rstd
        gam = g_ref[...]
        diff = xhat * gam + b_ref[...] - t_ref[...]
        part = jnp.sum(jnp.sum(diff * diff, axis=-1, keepdims=True), axis=0, keepdims=True)
        loss_ref[...] += (0.5 * inv_d) * part
        dout = diff * inv_d
        gg_ref[...] += jnp.sum(dout * xhat, axis=0, keepdims=True)
        gb_ref[...] += jnp.sum(dout, axis=0, keepdims=True)
        dxh = dout * gam
        dr = rstd * (dxh - jnp.mean(dxh, axis=-1, keepdims=True) - xhat * jnp.mean(dxh * xhat, axis=-1, keepdims=True))
        dr_ref[...] = dr
        drb = _bf(dr)
        dys_ref[...] = _dot_nt(drb, w_ref[0:D_SSD, :])
        dya_ref[...] = _dot_nt(drb, w_ref[D_SSD:D_MIX, :])

    row = pl.BlockSpec((tm, D_MODEL), lambda i: (i, 0))
    vec = pl.BlockSpec((1, D_MODEL), lambda i: (0, 0))
    return pl.pallas_call(
        body, name="out_ln_loss", grid=(L // tm,),
        in_specs=[row, row, row, row, pl.BlockSpec((D_MIX, D_MODEL), lambda i: (0, 0)), vec, vec],
        out_specs=[row, row, row, pl.BlockSpec((1, 128), lambda i: (0, 0)), vec, vec],
        out_shape=[jax.ShapeDtypeStruct((L, D_MODEL), F32)] * 3 + [jax.ShapeDtypeStruct((1, 128), F32)]
        + [jax.ShapeDtypeStruct((1, D_MODEL), F32)] * 2,
        compiler_params=_cparams(("arbitrary",)),
    )(y_ssd, y_att, x, target, w_out, ln_g, ln_b)


def _local_step(x, pos, target, w, w_out, conv_w, conv_b, dt_bias, a_log, d_skip, norm_w, sinks, ln_g, ln_b):
    inv8 = ROPE_THETA ** (-jnp.arange(0, ROPE_DIM, 2, dtype=F32) / ROPE_DIM)
    inv = jnp.tile(jnp.concatenate([inv8, inv8, jnp.zeros((ATT_HD - ROPE_DIM,), F32)]), 2).reshape(1, 2 * ATT_HD)

    z, g, q, xbc, kv, dtp, xb = _in_proj(x, w)
    y_ssd, y_pre, prev = _ssd_fwd(z, xbc, dtp, conv_w, conv_b, dt_bias, a_log, d_skip, norm_w)
    y_att, o, lse = _swa_fwd(q, g, kv, pos, inv, sinks)
    dr, dy_ssd, dy_att, loss, g_ln_g, g_ln_b = _out_ln_loss(y_ssd, y_att, x, target, w_out, ln_g, ln_b)
    gw_out_ssd = _matmul_tn(y_ssd, dr, 1024, "gw_out_ssd")
    gw_out_att = _matmul_tn(y_att, dr, 1024, "gw_out_att")
    dq, dg, dkv, g_sinks = _swa_bwd(dy_att, q, g, kv, o, lse, pos, inv, sinks)
    dz, dxbc, ddt, g_conv_w, g_conv_b, g_dt_bias, g_a_log, g_d_skip, g_norm_w = _ssd_bwd(
        dy_ssd, z, y_pre, xbc, dtp, prev, conv_w, conv_b, dt_bias, a_log, d_skip, norm_w)
    grad_x = _grad_x(dr, dz, dg, dq, dxbc, dkv, ddt, w)
    gw_z = _matmul_tn(dz, xb, 1024, "gw_z")
    gw_g = _matmul_tn(dg, xb, 1024, "gw_g")
    gw_q = _matmul_tn(dq, xb, 1024, "gw_q")
    gw_xbc = _matmul_tn(dxbc, xb, 1024, "gw_xbc")
    gw_kv = _matmul_tn(dkv, xb, 1024, "gw_kv")
    gw_dt = _matmul_tn(ddt, xb, 1024, "gw_dt")
    gw_in = jnp.concatenate([gw_z, gw_xbc, gw_dt[0:SSD_HEADS], gw_q, gw_kv, gw_g], axis=0)
    gw_out = jnp.concatenate([gw_out_ssd, gw_out_att], axis=0)
    small = dict(conv_w=g_conv_w, conv_b=g_conv_b, dt_bias=g_dt_bias, a_log=g_a_log, d_skip=g_d_skip,
                 ssd_norm_w=g_norm_w, attn_sinks=g_sinks, ln_g=g_ln_g, ln_b=g_ln_b)
    return loss[:, 0:1], grad_x, gw_in, gw_out, small


def _mesh_pos():
    return lax.axis_index("x"), lax.axis_index("y"), lax.axis_index("c")


def _gather_weights(w_in_s, w_out_s, conv_w_s):
    def body(win_ref, wout_ref, cw_ref, owin_ref, owout_ref, ocw_ref, send_sems, recv_sems, small_send, small_recv,
             local_sems):
        x, y, c = _mesh_pos()
        me = 2 * x + y
        sibling = (x, y, 1 - c)
        chips = [(1 - x, y), (x, 1 - y), (1 - x, 1 - y)]
        locals_ = [pltpu.make_async_copy(cw_ref, ocw_ref.at[me], local_sems.at[0])]
        for cp in locals_:
            cp.start()
        started = []
        for t, (src, dst) in enumerate(((win_ref, owin_ref), (wout_ref, owout_ref))):
            hr = src.shape[0] // 2

            def half(ref, hc, hr=hr):
                return ref.at[pl.ds(hc * hr, hr), :]

            for j, (px, py) in enumerate(chips):
                cp = pltpu.make_async_remote_copy(
                    src_ref=half(src, c), dst_ref=half(dst.at[me], c), send_sem=send_sems.at[t, j],
                    recv_sem=recv_sems.at[t, j], device_id=(px, py, c), device_id_type=MESH)
                cp.start()
                started.append(cp)
        for j, (px, py) in enumerate(chips):
            cp = pltpu.make_async_remote_copy(
                src_ref=cw_ref, dst_ref=ocw_ref.at[me], send_sem=small_send.at[j], recv_sem=small_recv.at[j],
                device_id=(px, py, c), device_id_type=MESH)
            cp.start()
            started.append(cp)
        for t, (src, dst) in enumerate(((win_ref, owin_ref), (wout_ref, owout_ref))):
            hr = src.shape[0] // 2
            for j, (px, py) in enumerate(chips):
                src_chip = 2 * px + py
                blk = dst.at[src_chip].at[pl.ds(c * hr, hr), :]
                pltpu.make_async_remote_copy(
                    src_ref=blk, dst_ref=blk, send_sem=send_sems.at[t, j], recv_sem=recv_sems.at[t, j],
                    device_id=(px, py, c), device_id_type=MESH).wait_recv()
                cp = pltpu.make_async_remote_copy(
                    src_ref=blk, dst_ref=blk, send_sem=send_sems.at[t, 3 + j], recv_sem=recv_sems.at[t, 3 + j],
                    device_id=sibling, device_id_type=MESH)
                cp.start()
                started.append(cp)
        for t, (src, dst) in enumerate(((win_ref, owin_ref), (wout_ref, owout_ref))):
            hr = src.shape[0] // 2
            for j, (px, py) in enumerate(chips):
                src_chip = 2 * px + py
                blk = dst.at[src_chip].at[pl.ds((1 - c) * hr, hr), :]
                pltpu.make_async_remote_copy(
                    src_ref=blk, dst_ref=blk, send_sem=send_sems.at[t, 3 + j], recv_sem=recv_sems.at[t, 3 + j],
                    device_id=sibling, device_id_type=MESH).wait_recv()
        for j in range(3):
            pltpu.make_async_remote_copy(
                src_ref=cw_ref, dst_ref=ocw_ref.at[me], send_sem=small_send.at[j], recv_sem=small_recv.at[j],
                device_id=sibling, device_id_type=MESH).wait_recv()
        for cp in started:
            cp.wait_send()
        for cp in locals_:
            cp.wait()

    any_spec = pl.BlockSpec(memory_space=pl.ANY)
    return pl.pallas_call(
        body, name="gather_weights",
        in_specs=[any_spec] * 3, out_specs=[any_spec] * 3,
        out_shape=[jax.ShapeDtypeStruct((N_CHIPS,) + a.shape, a.dtype) for a in (w_in_s, w_out_s, conv_w_s)],
        scratch_shapes=[pltpu.SemaphoreType.DMA((2, 6)), pltpu.SemaphoreType.DMA((2, 6)),
                        pltpu.SemaphoreType.DMA((3,)), pltpu.SemaphoreType.DMA((3,)), pltpu.SemaphoreType.DMA((3,))],
    )(w_in_s, w_out_s, conv_w_s)


def _pair_exchange(gw_in, gw_out, small):
    k_small = small.shape[1]

    def body(gin_ref, gout_ref, sm_ref, rin_ref, rout_ref, slots_ref, send_sems, recv_sems, small_send, small_recv,
             local_sem):
        x, y, c = _mesh_pos()
        me = 4 * x + 2 * y + c
        sibling = (x, y, 1 - c)
        mine = pltpu.make_async_copy(sm_ref, slots_ref.at[me], local_sem)
        mine.start()
        started = []
        for t, (src, dst) in enumerate(((gin_ref, rin_ref), (gout_ref, rout_ref))):
            hr = src.shape[1] // 2
            for j in range(N_CHIPS):
                cp = pltpu.make_async_remote_copy(
                    src_ref=src.at[j, pl.ds((1 - c) * hr, hr), :], dst_ref=dst.at[j], send_sem=send_sems.at[t, j],
                    recv_sem=recv_sems.at[t, j], device_id=sibling, device_id_type=MESH)
                cp.start()
                started.append(cp)
        for k in range(1, 8):
            peer = (x ^ ((k >> 2) & 1), y ^ ((k >> 1) & 1), c ^ (k & 1))
            cp = pltpu.make_async_remote_copy(
                src_ref=sm_ref, dst_ref=slots_ref.at[me], send_sem=small_send.at[k - 1], recv_sem=small_recv.at[k - 1],
                device_id=peer, device_id_type=MESH)
            cp.start()
            started.append(cp)
        for t, (src, dst) in enumerate(((gin_ref, rin_ref), (gout_ref, rout_ref))):
            for j in range(N_CHIPS):
                pltpu.make_async_remote_copy(
                    src_ref=dst.at[j], dst_ref=dst.at[j], send_sem=send_sems.at[t, j], recv_sem=recv_sems.at[t, j],
                    device_id=sibling, device_id_type=MESH).wait_recv()
        for k in range(1, 8):
            pltpu.make_async_remote_copy(
                src_ref=sm_ref, dst_ref=slots_ref.at[me], send_sem=small_send.at[k - 1], recv_sem=small_recv.at[k - 1],
                device_id=sibling, device_id_type=MESH).wait_recv()
        for cp in started:
            cp.wait_send()
        mine.wait()

    any_spec = pl.BlockSpec(memory_space=pl.ANY)
    half_in = jax.ShapeDtypeStruct((N_CHIPS, gw_in.shape[1] // 2, D_MODEL), F32)
    half_out = jax.ShapeDtypeStruct((N_CHIPS, gw_out.shape[1] // 2, D_MODEL), F32)
    return pl.pallas_call(
        body, name="pair_exchange",
        in_specs=[any_spec] * 3, out_specs=[any_spec] * 3,
        out_shape=[half_in, half_out, jax.ShapeDtypeStruct((8, 8, k_small), F32)],
        scratch_shapes=[pltpu.SemaphoreType.DMA((2, N_CHIPS)), pltpu.SemaphoreType.DMA((2, N_CHIPS)),
                        pltpu.SemaphoreType.DMA((7,)), pltpu.SemaphoreType.DMA((7,)), pltpu.SemaphoreType.DMA],
    )(gw_in, gw_out, small)


def _chip_exchange(s_in, s_out):
    def body(sin_ref, sout_ref, rin_ref, rout_ref, send_sems, recv_sems):
        x, y, c = _mesh_pos()
        me = 2 * x + y
        chips = [(1 - x, y), (x, 1 - y), (1 - x, 1 - y)]
        started = []
        for t, (src, dst) in enumerate(((sin_ref, rin_ref), (sout_ref, rout_ref))):
            for j, (px, py) in enumerate(chips):
                cp = pltpu.make_async_remote_copy(
                    src_ref=src.at[2 * px + py], dst_ref=dst.at[me], send_sem=send_sems.at[t, j],
                    recv_sem=recv_sems.at[t, j], device_id=(px, py, c), device_id_type=MESH)
                cp.start()
                started.append(cp)
        for t, (src, dst) in enumerate(((sin_ref, rin_ref), (sout_ref, rout_ref))):
            for j, (px, py) in enumerate(chips):
                blk = dst.at[2 * px + py]
                pltpu.make_async_remote_copy(
                    src_ref=blk, dst_ref=blk, send_sem=send_sems.at[t, j], recv_sem=recv_sems.at[t, j],
                    device_id=(px, py, c), device_id_type=MESH).wait_recv()
        for cp in started:
            cp.wait_send()

    any_spec = pl.BlockSpec(memory_space=pl.ANY)
    return pl.pallas_call(
        body, name="chip_exchange",
        in_specs=[any_spec] * 2, out_specs=[any_spec] * 2,
        out_shape=[jax.ShapeDtypeStruct(s_in.shape, s_in.dtype), jax.ShapeDtypeStruct(s_out.shape, s_out.dtype)],
        scratch_shapes=[pltpu.SemaphoreType.DMA((2, 3)), pltpu.SemaphoreType.DMA((2, 3))],
    )(s_in, s_out)


def _pair_share(h_in, h_out):
    def body(hin_ref, hout_ref, rin_ref, rout_ref, send_sems, recv_sems):
        x, y, c = _mesh_pos()
        sibling = (x, y, 1 - c)
        started = []
        for t, (src, dst) in enumerate(((hin_ref, rin_ref), (hout_ref, rout_ref))):
            cp = pltpu.make_async_remote_copy(
                src_ref=src, dst_ref=dst, send_sem=send_sems.at[t], recv_sem=recv_sems.at[t],
                device_id=sibling, device_id_type=MESH)
            cp.start()
            started.append(cp)
        for cp in started:
            cp.wait()

    any_spec = pl.BlockSpec(memory_space=pl.ANY)
    return pl.pallas_call(
        body, name="pair_share",
        in_specs=[any_spec] * 2, out_specs=[any_spec] * 2,
        out_shape=[jax.ShapeDtypeStruct(h_in.shape, F32), jax.ShapeDtypeStruct(h_out.shape, F32)],
        scratch_shapes=[pltpu.SemaphoreType.DMA((2,)), pltpu.SemaphoreType.DMA((2,))],
    )(h_in, h_out)


def _pair_add(g, recv, core, name):
    _, rows, C = recv.shape
    tc = 256

    def body(core_ref, g_ref, r_ref, o_ref):
        o_ref[...] = _bf(g_ref[...] + r_ref[...])

    spec = pl.BlockSpec((1, rows, tc), lambda j, i, core: (j, 0, i))
    return pl.pallas_call(
        body, name=name,
        grid_spec=pltpu.PrefetchScalarGridSpec(
            num_scalar_prefetch=1, grid=(N_CHIPS, C // tc),
            in_specs=[pl.BlockSpec((1, rows, tc), lambda j, i, core: (j, core[0], i)), spec], out_specs=spec),
        out_shape=jax.ShapeDtypeStruct((N_CHIPS, rows, C), BF16),
        compiler_params=_cparams(("parallel", "parallel")),
    )(core, g, recv)


def _chip_add(own, parts, chip, name):
    _, rows, C = parts.shape
    tc = 256

    def body(chip_ref, own_ref, r0, r1, r2, r3, o_ref):
        acc = None
        for j, r in enumerate((r0, r1, r2, r3)):
            term = jnp.where(chip_ref[0] == j, own_ref[0], r[0]).astype(F32)
            acc = term if acc is None else acc + term
        o_ref[...] = acc

    def slab(j):
        return pl.BlockSpec((1, rows, tc), lambda i, chip: (jnp.where(chip[0] == j, (j + 1) % N_CHIPS, j), 0, i))

    return pl.pallas_call(
        body, name=name,
        grid_spec=pltpu.PrefetchScalarGridSpec(
            num_scalar_prefetch=1, grid=(C // tc,),
            in_specs=[pl.BlockSpec((1, rows, tc), lambda i, chip: (chip[0], 0, i))] + [slab(j) for j in range(N_CHIPS)],
            out_specs=pl.BlockSpec((rows, tc), lambda i, chip: (0, i))),
        out_shape=jax.ShapeDtypeStruct((rows, C), F32),
        compiler_params=_cparams(("parallel",)),
    )(chip, own, parts, parts, parts, parts)


def _adamw_math(w, g, m, v):
    m = ADAM_B1 * m + (1.0 - ADAM_B1) * g
    v = ADAM_B2 * v + (1.0 - ADAM_B2) * (g * g)
    m_hat = m / (1.0 - ADAM_B1 ** ADAM_STEP)
    v_hat = v / (1.0 - ADAM_B2 ** ADAM_STEP)
    delta = -ADAM_LR * (m_hat / (jnp.sqrt(v_hat) + ADAM_EPS) + ADAM_WD * w)
    return delta, m, v


def _adamw_pair(w, g_own, g_sib, m, v, core, name):
    R, C = w.shape
    rows = g_own.shape[0]
    tc = 128

    def body(core_ref, w_ref, go_ref, gs_ref, m_ref, v_ref, d_ref, nm_ref, nv_ref, g_ref):
        first = core_ref[0] == 0
        own, sib = go_ref[...], gs_ref[...]
        g = jnp.concatenate([jnp.where(first, own, sib), jnp.where(first, sib, own)], axis=0)[0:R, :]
        d, nm, nv = _adamw_math(w_ref[...], g, m_ref[...], v_ref[...])
        d_ref[...] = d
        nm_ref[...] = nm
        nv_ref[...] = nv
        g_ref[...] = g

    spec = pl.BlockSpec((R, tc), lambda i, core: (0, i))
    gspec = pl.BlockSpec((rows, tc), lambda i, core: (0, i))
    return pl.pallas_call(
        body, name=name,
        grid_spec=pltpu.PrefetchScalarGridSpec(
            num_scalar_prefetch=1, grid=(C // tc,),
            in_specs=[spec, gspec, gspec, spec, spec], out_specs=[spec] * 4),
        out_shape=[jax.ShapeDtypeStruct((R, C), F32)] * 4,
        compiler_params=_cparams(("parallel",)),
    )(core, w, g_own, g_sib, m, v)


def _adamw(w, g, m, v, name):
    R, C = w.shape
    tc = 128

    def body(w_ref, g_ref, m_ref, v_ref, d_ref, nm_ref, nv_ref, go_ref):
        g = g_ref[0:R, :]
        d, nm, nv = _adamw_math(w_ref[...], g, m_ref[...], v_ref[...])
        d_ref[...] = d
        nm_ref[...] = nm
        nv_ref[...] = nv
        go_ref[...] = g

    spec = pl.BlockSpec((R, tc), lambda i: (0, i))
    gspec = pl.BlockSpec((g.shape[0], tc), lambda i: (0, i))
    return pl.pallas_call(
        body, name=name, grid=(C // tc,),
        in_specs=[spec, gspec, spec, spec], out_specs=[spec] * 4,
        out_shape=[jax.ShapeDtypeStruct((R, C), F32)] * 4,
        compiler_params=_cparams(("parallel",)),
    )(w, g, m, v)


def _sum_slots(slots):
    k = slots.shape[2]

    def body(s_ref, o_ref):
        acc = s_ref[0]
        for d in range(1, 8):
            acc = acc + s_ref[d]
        o_ref[...] = acc

    return pl.pallas_call(
        body, name="sum_slots", out_shape=jax.ShapeDtypeStruct((8, k), F32),
        compiler_params=_cparams(),
    )(slots)


SMALL_NAMES = ("conv_b", "ssd_norm_w", "ln_g", "ln_b", "dt_bias", "a_log", "d_skip", "attn_sinks")
SMALL_SIZES = (D_XBC, D_SSD, D_MODEL, D_MODEL, SSD_HEADS, SSD_HEADS, SSD_HEADS, ATT_QH)


def _pack_vectors(vals):
    parts = []
    for v in vals:
        n = v.shape[1]
        pad = (-n) % 128
        parts.append(jnp.pad(v, ((0, 0), (0, pad))) if pad else v)
    return jnp.concatenate(parts, axis=1)


def _unpack_vectors(row):
    out, off = [], 0
    for n in SMALL_SIZES:
        out.append(row[:, off:off + n])
        off += n + ((-n) % 128)
    return out


def kernel(x, positions, w_in, conv_w, conv_b, dt_bias, a_log, d_skip, ssd_norm_w, attn_sinks, w_out, ln_g, ln_b, loss_target, m_w_in, m_conv_w, m_conv_b, m_dt_bias, m_a_log, m_d_skip, m_ssd_norm_w, m_attn_sinks, m_w_out, m_ln_g, m_ln_b, v_w_in, v_conv_w, v_conv_b, v_dt_bias, v_a_log, v_d_skip, v_ssd_norm_w, v_attn_sinks, v_w_out, v_ln_g, v_ln_b):
    mx, my, mc = _mesh_pos()
    chip = 2 * mx + my
    L = x.shape[1]

    conv_w_s8 = jnp.pad(conv_w[0], ((0, 8 - CONV_K), (0, 0)))
    pad_rows = ((0, SLAB_ROWS - W_IN_COLS), (0, 0))
    w_in_t, m_w_in_t, v_w_in_t = w_in[0].T, m_w_in[0].T, v_w_in[0].T
    w_in_b, w_out_b = jnp.pad(_bf(w_in_t), pad_rows), _bf(w_out[0])
    ag_in, ag_out, ag_cw = _gather_weights(w_in_b, w_out_b, conv_w_s8)
    ag_in = jnp.where((jnp.arange(N_CHIPS) == chip)[:, None, None], w_in_b[None], ag_in)
    ag_out = jnp.where((jnp.arange(N_CHIPS) == chip)[:, None, None], w_out_b[None], ag_out)
    w_full = jnp.concatenate([ag_in[j, 0:W_IN_COLS] for j in range(N_CHIPS)], axis=0)
    w = jnp.concatenate([
        w_full[O_Z:O_Z + D_SSD], w_full[O_G:O_G + D_ATT], w_full[O_Q:O_Q + D_ATT],
        w_full[O_XBC:O_XBC + D_XBC], w_full[O_K:O_K + 2 * D_KV], w_full[O_DT:O_DT + SSD_HEADS],
        jnp.zeros((DT_PAD - SSD_HEADS, D_MODEL), BF16)], axis=0)
    w_out_full = ag_out.reshape(D_MIX, D_MODEL)
    conv_w_full = jnp.concatenate([ag_cw[j, 0:CONV_K] for j in range(N_CHIPS)], axis=1)

    loss_part, grad_x, gw_in, gw_out, small = _local_step(
        x[0], positions[0].reshape(L, 1), loss_target[0], w, w_out_full, conv_w_full, conv_b, dt_bias, a_log, d_skip,
        ssd_norm_w, attn_sinks, ln_g, ln_b)

    vec = _pack_vectors([small[n] for n in SMALL_NAMES])
    top = jnp.concatenate([vec, jnp.pad(loss_part, ((0, 0), (0, 127)))], axis=1)
    right = jnp.pad(top, ((0, 7), (0, 0)))
    packed = jnp.concatenate([jnp.pad(small["conv_w"], ((0, 8 - CONV_K), (0, 0))), right], axis=1)

    gw_in_slabs = jnp.stack([jnp.pad(gw_in[W_IN_COLS * j:W_IN_COLS * (j + 1)], pad_rows) for j in range(N_CHIPS)])
    gw_out_slabs = gw_out.reshape(N_CHIPS, W_OUT_ROWS, D_MODEL)
    core_id = mc.reshape(1).astype(jnp.int32)
    chip_id = chip.reshape(1).astype(jnp.int32)
    recv_in, recv_out, slots = _pair_exchange(gw_in_slabs, gw_out_slabs, packed)
    s_in = _pair_add(gw_in_slabs, recv_in, core_id, "pair_add_in")
    s_out = _pair_add(gw_out_slabs, recv_out, core_id, "pair_add_out")
    r_in, r_out = _chip_exchange(s_in, s_out)
    h_in = _chip_add(s_in, r_in, chip_id, "chip_add_in")
    h_out = _chip_add(s_out, r_out, chip_id, "chip_add_out")
    sib_in, sib_out = _pair_share(h_in, h_out)
    tot = _sum_slots(slots)

    in_t = _adamw_pair(w_in_t, h_in, sib_in, m_w_in_t, v_w_in_t, core_id, "adamw_w_in")
    d_w_in, nm_w_in, nv_w_in, g_w_in = [a.T for a in in_t]
    d_w_out, nm_w_out, nv_w_out, g_w_out = _adamw_pair(w_out[0], h_out, sib_out, m_w_out[0], v_w_out[0], core_id,
                                                       "adamw_w_out")
    g_conv_w_all = tot[0:CONV_K, 0:D_XBC]
    g_conv_w = lax.dynamic_slice(g_conv_w_all, (0, chip * CONV_COLS), (CONV_K, CONV_COLS))
    g_vecs = _unpack_vectors(tot[0:1, D_XBC:D_XBC + 5120])
    loss = tot[0, D_XBC + 5120]

    grads = dict(zip(SMALL_NAMES, g_vecs))
    params = dict(conv_b=conv_b, ssd_norm_w=ssd_norm_w, ln_g=ln_g, ln_b=ln_b, dt_bias=dt_bias, a_log=a_log,
                  d_skip=d_skip, attn_sinks=attn_sinks)
    moms = dict(conv_b=m_conv_b, ssd_norm_w=m_ssd_norm_w, ln_g=m_ln_g, ln_b=m_ln_b, dt_bias=m_dt_bias, a_log=m_a_log,
                d_skip=m_d_skip, attn_sinks=m_attn_sinks)
    vars_ = dict(conv_b=v_conv_b, ssd_norm_w=v_ssd_norm_w, ln_g=v_ln_g, ln_b=v_ln_b, dt_bias=v_dt_bias, a_log=v_a_log,
                 d_skip=v_d_skip, attn_sinks=v_attn_sinks)
    def small_block(cw, vecs):
        a = jnp.pad(cw, ((0, 0), (0, 5120 - CONV_COLS)))
        return jnp.concatenate([a, _pack_vectors(vecs), jnp.zeros((3, 5120), F32)], axis=0)

    sw = small_block(conv_w[0], [params[n] for n in SMALL_NAMES])
    sg = small_block(g_conv_w, [grads[n] for n in SMALL_NAMES])
    sm = small_block(m_conv_w[0], [moms[n] for n in SMALL_NAMES])
    sv = small_block(v_conv_w[0], [vars_[n] for n in SMALL_NAMES])
    sd, snm, snv, _ = _adamw(sw, sg, sm, sv, "adamw_small")

    def split_small(blk):
        d = dict(zip(SMALL_NAMES, _unpack_vectors(blk[CONV_K:CONV_K + 1])))
        d["conv_w"] = blk[0:CONV_K, 0:CONV_COLS][None]
        return d

    delta, new_m, new_v = split_small(sd), split_small(snm), split_small(snv)
    grads["conv_w"] = g_conv_w[None]
    for dd, a_in, a_out in ((grads, g_w_in, g_w_out), (delta, d_w_in, d_w_out), (new_m, nm_w_in, nm_w_out),
                            (new_v, nv_w_in, nv_w_out)):
        dd["w_in"] = a_in[None]
        dd["w_out"] = a_out[None]
    order = ("w_in", "conv_w", "conv_b", "dt_bias", "a_log", "d_skip", "ssd_norm_w", "attn_sinks", "w_out", "ln_g", "ln_b")
    return (loss, grad_x[None], *[grads[n] for n in order], *[delta[n] for n in order], *[new_m[n] for n in order],
            *[new_v[n] for n in order])
```

```python
import functools

import numpy as np
import jax
import jax.numpy as jnp
from jax import lax
from jax.experimental import pallas as pl
from jax.experimental.pallas import tpu as pltpu

F32 = jnp.float32
BF16 = jnp.bfloat16
MESH = pl.DeviceIdType.MESH

D_MODEL = 1024
D_SSD = 1024
D_ATT = 1024
D_MIX = 2048
SSD_HEADS = 16
SSD_P = 64
SSD_GROUPS = 2
SSD_R = 8
SSD_N = 128
D_BC = 256
D_XBC = 1536
CONV_K = 4
CHUNK = 128
ATT_HD = 64
ATT_QH = 16
ATT_KVH = 4
ATT_R = 4
D_KV = 256
WINDOW = 128
ROPE_THETA = 500000.0
ROPE_DIM = 16
ALPHA = 2.0 ** 0.25
LN_EPS = 1e-5
RMS_EPS = 1e-5
D_IN_PROJ = 5136
O_Z, O_XBC, O_DT, O_Q, O_K, O_V, O_G = 0, 1024, 2560, 2576, 3600, 3856, 4112
P_Z, P_G, P_Q, P_XBC, P_KV, P_DT, P_END = 0, 1024, 2048, 3072, 4608, 5120, 5248
DT_PAD = 128
N_CHIPS = 4
W_IN_COLS = D_IN_PROJ // N_CHIPS
SLAB_ROWS = 1312
W_OUT_ROWS = D_MIX // N_CHIPS
CONV_COLS = D_XBC // N_CHIPS

ADAM_LR = 0.001
ADAM_B1 = 0.9
ADAM_B2 = 0.999
ADAM_EPS = 1e-08
ADAM_WD = 0.01
ADAM_STEP = 10

VMEM_LIMIT = 56 * 1024 * 1024
NEG_BIG = -1e30
HI = lax.Precision.HIGHEST


def _cparams(sem=None, **kw):
    if sem is not None:
        kw["dimension_semantics"] = sem
    return pltpu.CompilerParams(vmem_limit_bytes=VMEM_LIMIT, **kw)


def _dot(a, b):
    return jnp.dot(a, b, preferred_element_type=F32)


def _dot_nt(a, b):
    return lax.dot_general(a, b, (((1,), (1,)), ((), ())), preferred_element_type=F32)


def _dot_tn(a, b):
    return lax.dot_general(a, b, (((0,), (0,)), ((), ())), preferred_element_type=F32)


def _bf(a):
    return a.astype(BF16)


def _iota2(shape, dim):
    return lax.broadcasted_iota(jnp.int32, shape, dim)


def _to_rows(col):
    k = col.shape[1]
    eye = (_iota2((k, k), 0) == _iota2((k, k), 1)).astype(F32)
    return lax.dot_general(eye, col, (((1,), (1,)), ((), ())), preferred_element_type=F32, precision=HI)


def _to_cols(row):
    n = row.shape[1]
    eye = (_iota2((n, n), 0) == _iota2((n, n), 1)).astype(F32)
    return lax.dot_general(eye, row, (((1,), (1,)), ((), ())), preferred_element_type=F32, precision=HI)


def _sigmoid(x):
    return jax.nn.sigmoid(x)


def _in_proj(x, w, pos, inv):
    L = x.shape[0]
    tm = 256
    widths = (D_SSD, D_ATT, D_ATT, D_XBC, 2 * D_KV, DT_PAD)

    def body(x_ref, w_ref, pos_ref, inv_ref, z_ref, g_ref, q_ref, xbc_ref, kv_ref, dt_ref, xb_ref):
        xb = _bf(x_ref[...])
        xb_ref[...] = xb
        for o_ref, off, wd in zip((z_ref, g_ref, xbc_ref, dt_ref), (P_Z, P_G, P_XBC, P_DT), (D_SSD, D_ATT, D_XBC, DT_PAD)):
            o_ref[...] = _dot_nt(xb, w_ref[off:off + wd, :])
        tabs = _rope_tables(pos_ref, inv_ref)
        q_ref[...] = _rope(_dot_nt(xb, w_ref[P_Q:P_Q + D_ATT, :]), tabs)
        kv_ref[:, 0:D_KV] = _rope(_dot_nt(xb, w_ref[P_KV:P_KV + D_KV, :]), tabs)
        kv_ref[:, D_KV:2 * D_KV] = _dot_nt(xb, w_ref[P_KV + D_KV:P_KV + 2 * D_KV, :])

    row = lambda wd: pl.BlockSpec((tm, wd), lambda i: (i, 0))
    return pl.pallas_call(
        body, name="in_proj", grid=(L // tm,),
        in_specs=[row(D_MODEL), pl.BlockSpec((P_END, D_MODEL), lambda i: (0, 0)), row(1),
                  pl.BlockSpec((1, 2 * ATT_HD), lambda i: (0, 0))],
        out_specs=[row(wd) for wd in widths] + [row(D_MODEL)],
        out_shape=[jax.ShapeDtypeStruct((L, wd), F32) for wd in widths] + [jax.ShapeDtypeStruct((L, D_MODEL), BF16)],
        compiler_params=_cparams(("parallel",)),
    )(x, w, pos, inv)


def _matmul_tn(a, b, tn, name):
    K, M = a.shape
    N = b.shape[1]
    tk = 512
    nk = K // tk

    def body(a_ref, b_ref, o_ref, acc_ref):
        k = pl.program_id(1)

        @pl.when(k == 0)
        def _():
            acc_ref[...] = jnp.zeros_like(acc_ref)

        acc_ref[...] += _dot_tn(_bf(a_ref[...]), _bf(b_ref[...]))

        @pl.when(k == nk - 1)
        def _():
            o_ref[...] = acc_ref[...]

    return pl.pallas_call(
        body, name=name, grid=(N // tn, nk),
        in_specs=[pl.BlockSpec((tk, M), lambda j, k: (k, 0)), pl.BlockSpec((tk, tn), lambda j, k: (k, j))],
        out_specs=pl.BlockSpec((M, tn), lambda j, k: (0, j)),
        out_shape=jax.ShapeDtypeStruct((M, N), F32),
        scratch_shapes=[pltpu.VMEM((M, tn), F32)],
        compiler_params=_cparams(("parallel", "arbitrary")),
    )(a, b)


def _grad_x(dr, dz, dg, dq, dxbc, dkv, ddt, w):
    L = dr.shape[0]
    tm = 256
    widths = (D_SSD, D_ATT, D_ATT, D_XBC, 2 * D_KV, DT_PAD)
    offs = (P_Z, P_G, P_Q, P_XBC, P_KV, P_DT)

    def body(dr_ref, dz_ref, dg_ref, dq_ref, dxbc_ref, dkv_ref, ddt_ref, w_ref, o_ref):
        acc = ALPHA * dr_ref[...]
        for p_ref, off, wd in zip((dz_ref, dg_ref, dq_ref, dxbc_ref, dkv_ref, ddt_ref), offs, widths):
            acc = acc + _dot(_bf(p_ref[...]), w_ref[off:off + wd, :])
        o_ref[...] = acc

    row = lambda wd: pl.BlockSpec((tm, wd), lambda i: (i, 0))
    return pl.pallas_call(
        body, name="grad_x", grid=(L // tm,),
        in_specs=[row(D_MODEL)] + [row(wd) for wd in widths] + [pl.BlockSpec((P_END, D_MODEL), lambda i: (0, 0))],
        out_specs=row(D_MODEL),
        out_shape=jax.ShapeDtypeStruct((L, D_MODEL), F32),
        compiler_params=_cparams(("parallel",)),
    )(dr, dz, dg, dq, dxbc, dkv, ddt, w)


def _ssd_chunk_pre(first, xbc_ref, tail_ref, dt_ref, cw_ref, cb_ref, dtb_ref, alog_ref, ext):
    tail = jnp.where(first, 0.0, tail_ref[...])
    ext[0:8, :] = tail
    ext[8:8 + CHUNK, :] = xbc_ref[...]
    u = cb_ref[...] + cw_ref[0:1, :] * ext[pl.ds(5, CHUNK), :]
    for k in range(1, CONV_K):
        u = u + cw_ref[k:k + 1, :] * ext[pl.ds(5 + k, CHUNK), :]
    sig = _sigmoid(u)
    xbc = u * sig
    dtraw = dt_ref[:, 0:SSD_HEADS] + dtb_ref[...]
    dt = jax.nn.softplus(dtraw)
    A = -jnp.exp(alog_ref[...])
    a = dt * A
    tril = (_iota2((CHUNK, CHUNK), 0) >= _iota2((CHUNK, CHUNK), 1)).astype(F32)
    acs = jnp.dot(tril, a, preferred_element_type=F32, precision=HI)
    acs_row = _to_rows(acs)
    return u, sig, xbc, dtraw, dt, A, acs, acs_row


def _ssd_fwd(z, xbc, dtp, conv_w, conv_b, dt_bias, a_log, d_skip, norm_w):
    L = z.shape[0]
    nc = L // CHUNK

    def body(z_ref, xbc_ref, tail_ref, dt_ref, cw_ref, cb_ref, dtb_ref, alog_ref, dsk_ref, nw_ref,
             y_ref, ypre_ref, prev_ref, state, ext, ybuf):
        c = pl.program_id(0)

        @pl.when(c == 0)
        def _():
            state[...] = jnp.zeros_like(state)

        u, sig, xbcv, dtraw, dt, A, acs, acs_row = _ssd_chunk_pre(
            c == 0, xbc_ref, tail_ref, dt_ref, cw_ref, cb_ref, dtb_ref, alog_ref, ext)
        prev_ref[0] = state[...]
        causal = _iota2((CHUNK, CHUNK), 0) >= _iota2((CHUNK, CHUNK), 1)
        alast = acs[CHUNK - 1:CHUNK, :]
        for g in range(SSD_GROUPS):
            Bg = _bf(xbcv[:, D_SSD + SSD_N * g:D_SSD + SSD_N * (g + 1)])
            Cg = _bf(xbcv[:, D_SSD + D_BC + SSD_N * g:D_SSD + D_BC + SSD_N * (g + 1)])
            cb = _dot_nt(Cg, Bg)
            for r in range(SSD_R):
                h = g * SSD_R + r
                hs = slice(SSD_P * h, SSD_P * (h + 1))
                acs_c = acs[:, h:h + 1]
                seg = acs_c - acs_row[h:h + 1, :]
                Lm = jnp.where(causal, jnp.exp(jnp.where(causal, seg, 0.0)), 0.0)
                M = cb * Lm
                xh = xbcv[:, hs]
                X = xh * dt[:, h:h + 1]
                prev_h = state[hs, :]
                ydiag = _dot(_bf(M), _bf(X))
                yoff = _dot_nt(Cg, _bf(prev_h)) * jnp.exp(acs_c)
                al = alast[:, h:h + 1]
                Xd = X * jnp.exp(al - acs_c)
                state[hs, :] = prev_h * jnp.exp(al) + _dot_tn(_bf(Xd), Bg)
                ybuf[:, hs] = ydiag + yoff + dsk_ref[:, h:h + 1] * xh
        y = ybuf[...]
        ypre_ref[...] = y
        zv = z_ref[...]
        yf = y * (zv * _sigmoid(zv))
        half = D_SSD // SSD_GROUPS
        for g in range(SSD_GROUPS):
            gs = slice(half * g, half * (g + 1))
            yg = yf[:, gs]
            ms = jnp.mean(yg * yg, axis=-1, keepdims=True)
            y_ref[:, gs] = yg * lax.rsqrt(ms + RMS_EPS) * nw_ref[:, gs]

    full = lambda shape: pl.BlockSpec(shape, lambda c: (0, 0))
    return pl.pallas_call(
        body, name="ssd_fwd", grid=(nc,),
        in_specs=[
            pl.BlockSpec((CHUNK, D_SSD), lambda c: (c, 0)),
            pl.BlockSpec((CHUNK, D_XBC), lambda c: (c, 0)),
            pl.BlockSpec((8, D_XBC), lambda c: (jnp.maximum(c * (CHUNK // 8) - 1, 0), 0)),
            pl.BlockSpec((CHUNK, DT_PAD), lambda c: (c, 0)),
            full((CONV_K, D_XBC)), full((1, D_XBC)), full((1, SSD_HEADS)), full((1, SSD_HEADS)), full((1, SSD_HEADS)),
            full((1, D_SSD)),
        ],
        out_specs=[
            pl.BlockSpec((CHUNK, D_SSD), lambda c: (c, 0)),
            pl.BlockSpec((CHUNK, D_SSD), lambda c: (c, 0)),
            pl.BlockSpec((1, SSD_HEADS * SSD_P, SSD_N), lambda c: (c, 0, 0)),
        ],
        out_shape=[
            jax.ShapeDtypeStruct((L, D_SSD), F32),
            jax.ShapeDtypeStruct((L, D_SSD), F32),
            jax.ShapeDtypeStruct((nc, SSD_HEADS * SSD_P, SSD_N), F32),
        ],
        scratch_shapes=[
            pltpu.VMEM((SSD_HEADS * SSD_P, SSD_N), F32),
            pltpu.VMEM((CHUNK + 8, D_XBC), F32),
            pltpu.VMEM((CHUNK, D_SSD), F32),
        ],
        compiler_params=_cparams(("arbitrary",)),
    )(z, xbc, xbc, dtp, conv_w, conv_b, dt_bias, a_log, d_skip, norm_w)


def _ssd_bwd(dy, z, ypre, xbc, dtp, prev, conv_w, conv_b, dt_bias, a_log, d_skip, norm_w):
    L = z.shape[0]
    nc = L // CHUNK

    def body(dy_ref, z_ref, ypre_ref, xbc_ref, tail_ref, dt_ref, prev_ref, cw_ref, cb_ref, dtb_ref, alog_ref, dsk_ref,
             nw_ref, dz_ref, dxbc_ref, ddt_ref, gcw_ref, gcb_ref, gdtb_ref, galog_ref, gdsk_ref, gnw_ref,
             dstate, dhead, ext, ext2, dpost):
        i = pl.program_id(0)
        c = nc - 1 - i

        @pl.when(i == 0)
        def _():
            dstate[...] = jnp.zeros_like(dstate)
            dhead[...] = jnp.zeros_like(dhead)
            gcw_ref[...] = jnp.zeros_like(gcw_ref)
            gcb_ref[...] = jnp.zeros_like(gcb_ref)
            gdtb_ref[...] = jnp.zeros_like(gdtb_ref)
            galog_ref[...] = jnp.zeros_like(galog_ref)
            gdsk_ref[...] = jnp.zeros_like(gdsk_ref)
            gnw_ref[...] = jnp.zeros_like(gnw_ref)

        u, sig, xbcv, dtraw, dt, A, acs, acs_row = _ssd_chunk_pre(
            c == 0, xbc_ref, tail_ref, dt_ref, cw_ref, cb_ref, dtb_ref, alog_ref, ext)

        zv = z_ref[...]
        ypre = ypre_ref[...]
        dyn = dy_ref[...]
        sz = _sigmoid(zv)
        silu_z = zv * sz
        yf = ypre * silu_z
        half = D_SSD // SSD_GROUPS
        dyf_parts = []
        for g in range(SSD_GROUPS):
            gs = slice(half * g, half * (g + 1))
            yg = yf[:, gs]
            rstd = lax.rsqrt(jnp.mean(yg * yg, axis=-1, keepdims=True) + RMS_EPS)
            dout = dyn[:, gs]
            gnw_ref[:, gs] += jnp.sum(dout * yg * rstd, axis=0, keepdims=True)
            dyhat = dout * nw_ref[:, gs]
            dyf_parts.append(rstd * (dyhat - yg * (rstd * rstd) * jnp.mean(dyhat * yg, axis=-1, keepdims=True)))
        dyf = jnp.concatenate(dyf_parts, axis=1)
        dz_ref[...] = dyf * ypre * (sz * (1.0 + zv * (1.0 - sz)))
        dypre = dyf * silu_z

        causal = _iota2((CHUNK, CHUNK), 0) >= _iota2((CHUNK, CHUNK), 1)
        alast = acs[CHUNK - 1:CHUNK, :]
        lane16 = _iota2((1, SSD_HEADS), 1)
        sub16 = _iota2((SSD_HEADS, 1), 0)
        dacs_col = jnp.zeros((CHUNK, SSD_HEADS), F32)
        dacs_row = jnp.zeros((SSD_HEADS, CHUNK), F32)
        ddt_col = jnp.zeros((CHUNK, SSD_HEADS), F32)
        dalast = jnp.zeros((1, SSD_HEADS), F32)
        gdsk = jnp.zeros((1, SSD_HEADS), F32)
        for g in range(SSD_GROUPS):
            bs = slice(D_SSD + SSD_N * g, D_SSD + SSD_N * (g + 1))
            cs = slice(D_SSD + D_BC + SSD_N * g, D_SSD + D_BC + SSD_N * (g + 1))
            Bg = _bf(xbcv[:, bs])
            Cg = _bf(xbcv[:, cs])
            cb = _dot_nt(Cg, Bg)
            dcb = jnp.zeros((CHUNK, CHUNK), F32)
            dB = jnp.zeros((CHUNK, SSD_N), F32)
            dC = jnp.zeros((CHUNK, SSD_N), F32)
            for r in range(SSD_R):
                h = g * SSD_R + r
                hs = slice(SSD_P * h, SSD_P * (h + 1))
                onehot = (lane16 == h).astype(F32)
                acs_c = acs[:, h:h + 1]
                seg = acs_c - acs_row[h:h + 1, :]
                Lm = jnp.where(causal, jnp.exp(jnp.where(causal, seg, 0.0)), 0.0)
                M = cb * Lm
                xh = xbcv[:, hs]
                dth = dt[:, h:h + 1]
                X = xh * dth
                Xb = _bf(X)
                dyh = dypre[:, hs]
                dyb = _bf(dyh)
                prev_h = prev_ref[0, hs, :]
                prevb = _bf(prev_h)
                dnext = dstate[hs, :]
                dnextb = _bf(dnext)
                al = alast[:, h:h + 1]
                eacs = jnp.exp(acs_c)
                eal = jnp.exp(al)
                dsd = jnp.exp(al - acs_c)
                G = _bf(dyh * eacs)
                dstate[hs, :] = dnext * eal + _dot_tn(G, Cg)
                dC = dC + _dot(G, prevb)
                yoff = _dot_nt(Cg, prevb) * eacs
                dacs_h = jnp.sum(dyh * yoff, axis=-1, keepdims=True)
                BdN = _dot_nt(Bg, dnextb)
                dX = dsd * BdN
                dB = dB + _dot(_bf(X * dsd), dnextb)
                t = jnp.sum(X * BdN, axis=-1, keepdims=True) * dsd
                dacs_h = dacs_h - t
                dal = jnp.sum(t, axis=0, keepdims=True) + jnp.sum(
                    jnp.sum(dnext * prev_h, axis=-1, keepdims=True), axis=0, keepdims=True) * eal
                dM = _dot_nt(dyb, Xb)
                dX = dX + _dot_tn(_bf(M), dyb)
                dseg = dM * M
                dcb = dcb + dM * Lm
                dacs_h = dacs_h + jnp.sum(dseg, axis=-1, keepdims=True)
                dacs_row = dacs_row - jnp.sum(dseg, axis=0, keepdims=True) * (sub16 == h).astype(F32)
                dacs_col = dacs_col + dacs_h * onehot
                dalast = dalast + dal * onehot
                ddt_col = ddt_col + jnp.sum(dX * xh, axis=-1, keepdims=True) * onehot
                gdsk = gdsk + jnp.sum(jnp.sum(dyh * xh, axis=-1, keepdims=True), axis=0, keepdims=True) * onehot
                dpost[:, hs] = dX * dth + dsk_ref[:, h:h + 1] * dyh
            dcbb = _bf(dcb)
            dpost[:, bs] = dB + _dot_tn(dcbb, Cg)
            dpost[:, cs] = dC + _dot(dcbb, Bg)

        is_last = (_iota2((CHUNK, 1), 0) == CHUNK - 1).astype(F32)
        dacs = dacs_col + _to_cols(dacs_row) + is_last * dalast
        triu = (_iota2((CHUNK, CHUNK), 0) <= _iota2((CHUNK, CHUNK), 1)).astype(F32)
        da = jnp.dot(triu, dacs, preferred_element_type=F32, precision=HI)
        ddt_tot = ddt_col + da * A
        galog_ref[...] += jnp.sum(da * dt, axis=0, keepdims=True) * A
        ddtraw = ddt_tot * _sigmoid(dtraw)
        gdtb_ref[...] += jnp.sum(ddtraw, axis=0, keepdims=True)
        gdsk_ref[...] += gdsk
        ddt_ref[...] = jnp.zeros_like(ddt_ref)
        ddt_ref[:, 0:SSD_HEADS] = ddtraw

        dconv = dpost[...] * (sig * (1.0 + u * (1.0 - sig)))
        gcb_ref[...] += jnp.sum(dconv, axis=0, keepdims=True)
        for k in range(CONV_K):
            gcw_ref[k:k + 1, :] += jnp.sum(dconv * ext[pl.ds(5 + k, CHUNK), :], axis=0, keepdims=True)
        ext2[0:CHUNK, :] = dconv
        ext2[CHUNK:CHUNK + 8, :] = dhead[...]
        dx = cw_ref[CONV_K - 1:CONV_K, :] * dconv
        for k in range(CONV_K - 1):
            dx = dx + cw_ref[k:k + 1, :] * ext2[pl.ds(CONV_K - 1 - k, CHUNK), :]
        dxbc_ref[...] = dx
        dhead[...] = dconv[0:8, :]

    full = lambda shape: pl.BlockSpec(shape, lambda i: (0, 0))
    rev = lambda wd: pl.BlockSpec((CHUNK, wd), lambda i: (nc - 1 - i, 0))
    return pl.pallas_call(
        body, name="ssd_bwd", grid=(nc,),
        in_specs=[
            rev(D_SSD), rev(D_SSD), rev(D_SSD), rev(D_XBC),
            pl.BlockSpec((8, D_XBC), lambda i: (jnp.maximum((nc - 1 - i) * (CHUNK // 8) - 1, 0), 0)),
            rev(DT_PAD),
            pl.BlockSpec((1, SSD_HEADS * SSD_P, SSD_N), lambda i: (nc - 1 - i, 0, 0)),
            full((CONV_K, D_XBC)), full((1, D_XBC)), full((1, SSD_HEADS)), full((1, SSD_HEADS)), full((1, SSD_HEADS)),
            full((1, D_SSD)),
        ],
        out_specs=[
            rev(D_SSD), rev(D_XBC), rev(DT_PAD),
            full((CONV_K, D_XBC)), full((1, D_XBC)), full((1, SSD_HEADS)), full((1, SSD_HEADS)), full((1, SSD_HEADS)),
            full((1, D_SSD)),
        ],
        out_shape=[
            jax.ShapeDtypeStruct((L, D_SSD), F32), jax.ShapeDtypeStruct((L, D_XBC), F32),
            jax.ShapeDtypeStruct((L, DT_PAD), F32),
            jax.ShapeDtypeStruct((CONV_K, D_XBC), F32), jax.ShapeDtypeStruct((1, D_XBC), F32),
            jax.ShapeDtypeStruct((1, SSD_HEADS), F32), jax.ShapeDtypeStruct((1, SSD_HEADS), F32),
            jax.ShapeDtypeStruct((1, SSD_HEADS), F32), jax.ShapeDtypeStruct((1, D_SSD), F32),
        ],
        scratch_shapes=[
            pltpu.VMEM((SSD_HEADS * SSD_P, SSD_N), F32),
            pltpu.VMEM((8, D_XBC), F32),
            pltpu.VMEM((CHUNK + 8, D_XBC), F32),
            pltpu.VMEM((CHUNK + 8, D_XBC), F32),
            pltpu.VMEM((CHUNK, D_XBC), F32),
        ],
        compiler_params=_cparams(("arbitrary",)),
    )(dy, z, ypre, xbc, xbc, dtp, prev, conv_w, conv_b, dt_bias, a_log, d_skip, norm_w)


def _head_expander():
    return (_iota2((SSD_HEADS, D_SSD), 1) // SSD_P == _iota2((SSD_HEADS, D_SSD), 0)).astype(F32)


def _expand(v, e):
    return jnp.dot(v, e, preferred_element_type=F32, precision=HI)


def _headsum(t, e):
    m = t.shape[0]
    if m < 8:
        t = jnp.broadcast_to(t[0:1], (8, t.shape[1]))
    out = lax.dot_general(t, e, (((1,), (1,)), ((), ())), preferred_element_type=F32, precision=HI)
    return out[0:m]


def _ssd_decays(dt, acs, dsk_ref, e):
    alast = acs[CHUNK - 1:CHUNK, :]
    stk = jnp.concatenate([dt, jnp.exp(acs), jnp.exp(alast - acs),
                           jnp.broadcast_to(jnp.exp(alast), (8, SSD_HEADS)),
                           jnp.broadcast_to(dsk_ref[...], (8, SSD_HEADS))], axis=0)
    ex = _expand(stk, e)
    return (ex[0:CHUNK], ex[CHUNK:2 * CHUNK], ex[2 * CHUNK:3 * CHUNK], ex[3 * CHUNK:3 * CHUNK + 1],
            ex[3 * CHUNK + 8:3 * CHUNK + 9])


def _ssd_fwd2(z, xbc, dtp, conv_w, conv_b, dt_bias, a_log, d_skip, norm_w):
    L = z.shape[0]
    nc = L // CHUNK
    half = D_SSD // SSD_GROUPS

    def body(z_ref, xbc_ref, tail_ref, dt_ref, cw_ref, cb_ref, dtb_ref, alog_ref, dsk_ref, nw_ref,
             y_ref, ypre_ref, prev_ref, state, ext, ybuf, mbuf):
        c = pl.program_id(0)

        @pl.when(c == 0)
        def _():
            state[...] = jnp.zeros_like(state)

        u, sig, xbcv, dtraw, dt, A, acs, acs_row = _ssd_chunk_pre(
            c == 0, xbc_ref, tail_ref, dt_ref, cw_ref, cb_ref, dtb_ref, alog_ref, ext)
        e = _head_expander()
        dtE, eacsE, dsdE, ealE, dskE = _ssd_decays(dt, acs, dsk_ref, e)
        xs = xbcv[:, 0:D_SSD]
        X = xs * dtE
        prev_ref[0] = state[...]
        causal = _iota2((CHUNK, CHUNK), 0) >= _iota2((CHUNK, CHUNK), 1)
        for g in range(SSD_GROUPS):
            gs = slice(half * g, half * (g + 1))
            Bg = _bf(xbcv[:, D_SSD + SSD_N * g:D_SSD + SSD_N * (g + 1)])
            Cg = _bf(xbcv[:, D_SSD + D_BC + SSD_N * g:D_SSD + D_BC + SSD_N * (g + 1)])
            cb = _dot_nt(Cg, Bg)
            for r in range(SSD_R):
                h = g * SSD_R + r
                seg = acs[:, h:h + 1] - acs_row[h:h + 1, :]
                mbuf[h] = _bf(cb * jnp.where(causal, jnp.exp(jnp.where(causal, seg, 0.0)), 0.0))
            st = state[:, gs]
            ybuf[:, gs] = _dot(Cg, _bf(st)) * eacsE[:, gs] + dskE[:, gs] * xs[:, gs]
            state[:, gs] = st * ealE[:, gs] + _dot_tn(Bg, _bf(X[:, gs] * dsdE[:, gs]))
        Xb = _bf(X)
        for h in range(SSD_HEADS):
            hs = slice(SSD_P * h, SSD_P * (h + 1))
            ybuf[:, hs] += _dot(mbuf[h], Xb[:, hs])
        y = ybuf[...]
        ypre_ref[...] = y
        zv = z_ref[...]
        yf = y * (zv * _sigmoid(zv))
        for g in range(SSD_GROUPS):
            gs = slice(half * g, half * (g + 1))
            yg = yf[:, gs]
            ms = jnp.mean(yg * yg, axis=-1, keepdims=True)
            y_ref[:, gs] = yg * lax.rsqrt(ms + RMS_EPS) * nw_ref[:, gs]

    full = lambda shape: pl.BlockSpec(shape, lambda c: (0, 0))
    return pl.pallas_call(
        body, name="ssd_fwd", grid=(nc,),
        in_specs=[
            pl.BlockSpec((CHUNK, D_SSD), lambda c: (c, 0)),
            pl.BlockSpec((CHUNK, D_XBC), lambda c: (c, 0)),
            pl.BlockSpec((8, D_XBC), lambda c: (jnp.maximum(c * (CHUNK // 8) - 1, 0), 0)),
            pl.BlockSpec((CHUNK, DT_PAD), lambda c: (c, 0)),
            full((CONV_K, D_XBC)), full((1, D_XBC)), full((1, SSD_HEADS)), full((1, SSD_HEADS)), full((1, SSD_HEADS)),
            full((1, D_SSD)),
        ],
        out_specs=[
            pl.BlockSpec((CHUNK, D_SSD), lambda c: (c, 0)),
            pl.BlockSpec((CHUNK, D_SSD), lambda c: (c, 0)),
            pl.BlockSpec((1, SSD_N, D_SSD), lambda c: (c, 0, 0)),
        ],
        out_shape=[
            jax.ShapeDtypeStruct((L, D_SSD), F32),
            jax.ShapeDtypeStruct((L, D_SSD), F32),
            jax.ShapeDtypeStruct((nc, SSD_N, D_SSD), F32),
        ],
        scratch_shapes=[
            pltpu.VMEM((SSD_N, D_SSD), F32),
            pltpu.VMEM((CHUNK + 8, D_XBC), F32),
            pltpu.VMEM((CHUNK, D_SSD), F32),
            pltpu.VMEM((SSD_HEADS, CHUNK, CHUNK), BF16),
        ],
        compiler_params=_cparams(("arbitrary",)),
    )(z, xbc, xbc, dtp, conv_w, conv_b, dt_bias, a_log, d_skip, norm_w)


def _ssd_bwd2(dy, z, ypre, xbc, dtp, prev, conv_w, conv_b, dt_bias, a_log, d_skip, norm_w):
    L = z.shape[0]
    nc = L // CHUNK
    half = D_SSD // SSD_GROUPS

    def body(dy_ref, z_ref, ypre_ref, xbc_ref, tail_ref, dt_ref, prev_ref, cw_ref, cb_ref, dtb_ref, alog_ref, dsk_ref,
             nw_ref, dz_ref, dxbc_ref, ddt_ref, gcw_ref, gcb_ref, gdtb_ref, galog_ref, gdsk_ref, gnw_ref,
             dstate, dhead, ext, ext2, dpost, yobuf, bdbuf, lmbuf, dmbuf, cbbuf):
        i = pl.program_id(0)
        c = nc - 1 - i

        @pl.when(i == 0)
        def _():
            dstate[...] = jnp.zeros_like(dstate)
            dhead[...] = jnp.zeros_like(dhead)
            gcw_ref[...] = jnp.zeros_like(gcw_ref)
            gcb_ref[...] = jnp.zeros_like(gcb_ref)
            gdtb_ref[...] = jnp.zeros_like(gdtb_ref)
            galog_ref[...] = jnp.zeros_like(galog_ref)
            gdsk_ref[...] = jnp.zeros_like(gdsk_ref)
            gnw_ref[...] = jnp.zeros_like(gnw_ref)

        u, sig, xbcv, dtraw, dt, A, acs, acs_row = _ssd_chunk_pre(
            c == 0, xbc_ref, tail_ref, dt_ref, cw_ref, cb_ref, dtb_ref, alog_ref, ext)
        e = _head_expander()
        dtE, eacsE, dsdE, ealE, dskE = _ssd_decays(dt, acs, dsk_ref, e)
        alast = acs[CHUNK - 1:CHUNK, :]
        xs = xbcv[:, 0:D_SSD]
        X = xs * dtE
        Xb = _bf(X)

        zv = z_ref[...]
        ypre = ypre_ref[...]
        dyn = dy_ref[...]
        sz = _sigmoid(zv)
        silu_z = zv * sz
        yf = ypre * silu_z
        dyf_parts = []
        for g in range(SSD_GROUPS):
            gs = slice(half * g, half * (g + 1))
            yg = yf[:, gs]
            rstd = lax.rsqrt(jnp.mean(yg * yg, axis=-1, keepdims=True) + RMS_EPS)
            dout = dyn[:, gs]
            gnw_ref[:, gs] += jnp.sum(dout * yg * rstd, axis=0, keepdims=True)
            dyhat = dout * nw_ref[:, gs]
            dyf_parts.append(rstd * (dyhat - yg * (rstd * rstd) * jnp.mean(dyhat * yg, axis=-1, keepdims=True)))
        dyf = jnp.concatenate(dyf_parts, axis=1)
        dz_ref[...] = dyf * ypre * (sz * (1.0 + zv * (1.0 - sz)))
        dyp = dyf * silu_z
        dyb = _bf(dyp)
        G = dyp * eacsE

        causal = _iota2((CHUNK, CHUNK), 0) >= _iota2((CHUNK, CHUNK), 1)
        ST = prev_ref[0]
        dST = dstate[...]
        for g in range(SSD_GROUPS):
            gs = slice(half * g, half * (g + 1))
            bs = slice(D_SSD + SSD_N * g, D_SSD + SSD_N * (g + 1))
            cs = slice(D_SSD + D_BC + SSD_N * g, D_SSD + D_BC + SSD_N * (g + 1))
            Bg = _bf(xbcv[:, bs])
            Cg = _bf(xbcv[:, cs])
            Gb = _bf(G[:, gs])
            STb = _bf(ST[:, gs])
            dSTb = _bf(dST[:, gs])
            dstate[:, gs] = dST[:, gs] * ealE[:, gs] + _dot_tn(Cg, Gb)
            yobuf[:, gs] = _dot(Cg, STb) * eacsE[:, gs]
            bdbuf[:, gs] = _dot(Bg, dSTb)
            dpost[:, cs] = _dot_nt(Gb, STb)
            dpost[:, bs] = _dot_nt(_bf(X[:, gs] * dsdE[:, gs]), dSTb)
            cbbuf[g] = _dot_nt(Cg, Bg)
            for r in range(SSD_R):
                h = g * SSD_R + r
                seg = acs[:, h:h + 1] - acs_row[h:h + 1, :]
                lmbuf[h] = jnp.where(causal, jnp.exp(jnp.where(causal, seg, 0.0)), 0.0)
        for h in range(SSD_HEADS):
            hs = slice(SSD_P * h, SSD_P * (h + 1))
            Mb = _bf(cbbuf[h // SSD_R] * lmbuf[h])
            dmbuf[h] = _dot_nt(dyb[:, hs], Xb[:, hs])
            dpost[:, hs] = _dot_tn(Mb, dyb[:, hs])
        lane16 = _iota2((1, SSD_HEADS), 1)
        sub16 = _iota2((SSD_HEADS, 1), 0)
        dacs_col = jnp.zeros((CHUNK, SSD_HEADS), F32)
        dacs_row = jnp.zeros((SSD_HEADS, CHUNK), F32)
        for g in range(SSD_GROUPS):
            bs = slice(D_SSD + SSD_N * g, D_SSD + SSD_N * (g + 1))
            cs = slice(D_SSD + D_BC + SSD_N * g, D_SSD + D_BC + SSD_N * (g + 1))
            cb = cbbuf[g]
            dcb = jnp.zeros((CHUNK, CHUNK), F32)
            for r in range(SSD_R):
                h = g * SSD_R + r
                dM = dmbuf[h]
                Lm = lmbuf[h]
                dcb = dcb + dM * Lm
                dseg = dM * (cb * Lm)
                dacs_col = dacs_col + jnp.sum(dseg, axis=-1, keepdims=True) * (lane16 == h).astype(F32)
                dacs_row = dacs_row - jnp.sum(dseg, axis=0, keepdims=True) * (sub16 == h).astype(F32)
            dcbb = _bf(dcb)
            dpost[:, bs] += _dot_tn(dcbb, _bf(xbcv[:, cs]))
            dpost[:, cs] += _dot(dcbb, _bf(xbcv[:, bs]))

        BD = bdbuf[...]
        dX = dpost[:, 0:D_SSD] + dsdE * BD
        dsd = jnp.exp(alast - acs)
        T = _headsum(X * BD, e) * dsd
        dalast = jnp.sum(T, axis=0, keepdims=True) + _headsum(
            jnp.sum(dST * ST, axis=0, keepdims=True), e) * jnp.exp(alast)
        is_last = (_iota2((CHUNK, 1), 0) == CHUNK - 1).astype(F32)
        dacs = dacs_col + _to_cols(dacs_row) + _headsum(dyp * yobuf[...], e) - T + is_last * dalast
        triu = (_iota2((CHUNK, CHUNK), 0) <= _iota2((CHUNK, CHUNK), 1)).astype(F32)
        da = jnp.dot(triu, dacs, preferred_element_type=F32, precision=HI)
        ddt_tot = _headsum(dX * xs, e) + da * A
        galog_ref[...] += jnp.sum(da * dt, axis=0, keepdims=True) * A
        ddtraw = ddt_tot * _sigmoid(dtraw)
        gdtb_ref[...] += jnp.sum(ddtraw, axis=0, keepdims=True)
        gdsk_ref[...] += _headsum(jnp.sum(dyp * xs, axis=0, keepdims=True), e)
        ddt_ref[...] = jnp.zeros_like(ddt_ref)
        ddt_ref[:, 0:SSD_HEADS] = ddtraw
        dpost[:, 0:D_SSD] = dX * dtE + dskE * dyp

        dconv = dpost[...] * (sig * (1.0 + u * (1.0 - sig)))
        gcb_ref[...] += jnp.sum(dconv, axis=0, keepdims=True)
        for k in range(CONV_K):
            gcw_ref[k:k + 1, :] += jnp.sum(dconv * ext[pl.ds(5 + k, CHUNK), :], axis=0, keepdims=True)
        ext2[0:CHUNK, :] = dconv
        ext2[CHUNK:CHUNK + 8, :] = dhead[...]
        dx = cw_ref[CONV_K - 1:CONV_K, :] * dconv
        for k in range(CONV_K - 1):
            dx = dx + cw_ref[k:k + 1, :] * ext2[pl.ds(CONV_K - 1 - k, CHUNK), :]
        dxbc_ref[...] = dx
        dhead[...] = dconv[0:8, :]

    full = lambda shape: pl.BlockSpec(shape, lambda i: (0, 0))
    rev = lambda wd: pl.BlockSpec((CHUNK, wd), lambda i: (nc - 1 - i, 0))
    return pl.pallas_call(
        body, name="ssd_bwd", grid=(nc,),
        in_specs=[
            rev(D_SSD), rev(D_SSD), rev(D_SSD), rev(D_XBC),
            pl.BlockSpec((8, D_XBC), lambda i: (jnp.maximum((nc - 1 - i) * (CHUNK // 8) - 1, 0), 0)),
            rev(DT_PAD),
            pl.BlockSpec((1, SSD_N, D_SSD), lambda i: (nc - 1 - i, 0, 0)),
            full((CONV_K, D_XBC)), full((1, D_XBC)), full((1, SSD_HEADS)), full((1, SSD_HEADS)), full((1, SSD_HEADS)),
            full((1, D_SSD)),
        ],
        out_specs=[
            rev(D_SSD), rev(D_XBC), rev(DT_PAD),
            full((CONV_K, D_XBC)), full((1, D_XBC)), full((1, SSD_HEADS)), full((1, SSD_HEADS)), full((1, SSD_HEADS)),
            full((1, D_SSD)),
        ],
        out_shape=[
            jax.ShapeDtypeStruct((L, D_SSD), F32), jax.ShapeDtypeStruct((L, D_XBC), F32),
            jax.ShapeDtypeStruct((L, DT_PAD), F32),
            jax.ShapeDtypeStruct((CONV_K, D_XBC), F32), jax.ShapeDtypeStruct((1, D_XBC), F32),
            jax.ShapeDtypeStruct((1, SSD_HEADS), F32), jax.ShapeDtypeStruct((1, SSD_HEADS), F32),
            jax.ShapeDtypeStruct((1, SSD_HEADS), F32), jax.ShapeDtypeStruct((1, D_SSD), F32),
        ],
        scratch_shapes=[
            pltpu.VMEM((SSD_N, D_SSD), F32),
            pltpu.VMEM((8, D_XBC), F32),
            pltpu.VMEM((CHUNK + 8, D_XBC), F32),
            pltpu.VMEM((CHUNK + 8, D_XBC), F32),
            pltpu.VMEM((CHUNK, D_XBC), F32),
            pltpu.VMEM((CHUNK, D_SSD), F32),
            pltpu.VMEM((CHUNK, D_SSD), F32),
            pltpu.VMEM((SSD_HEADS, CHUNK, CHUNK), F32),
            pltpu.VMEM((SSD_HEADS, CHUNK, CHUNK), F32),
            pltpu.VMEM((SSD_GROUPS, CHUNK, CHUNK), F32),
        ],
        compiler_params=_cparams(("arbitrary",)),
    )(dy, z, ypre, xbc, xbc, dtp, prev, conv_w, conv_b, dt_bias, a_log, d_skip, norm_w)


def _rope_tables(pos_ref, inv_ref):
    ang = pos_ref[...].astype(F32) * inv_ref[...]
    d = _iota2((1, 2 * ATT_HD), 1) % ATT_HD
    s = jnp.sin(ang)
    return jnp.cos(ang), jnp.where(d < ROPE_DIM // 2, -s, 0.0), jnp.where((d >= ROPE_DIM // 2) & (d < ROPE_DIM), s, 0.0)


def _rope(t, tabs):
    c, s1, s2 = tabs
    n = t.shape[1]
    rep = n // c.shape[1]
    return (t * jnp.tile(c, (1, rep)) + pltpu.roll(t, n - ROPE_DIM // 2, 1) * jnp.tile(s1, (1, rep))
            + pltpu.roll(t, ROPE_DIM // 2, 1) * jnp.tile(s2, (1, rep)))


def _rope_t(t, tabs):
    c, s1, s2 = tabs
    n = t.shape[1]
    rep = n // c.shape[1]
    return (t * jnp.tile(c, (1, rep)) + pltpu.roll(t * jnp.tile(s1, (1, rep)), ROPE_DIM // 2, 1)
            + pltpu.roll(t * jnp.tile(s2, (1, rep)), n - ROPE_DIM // 2, 1))


def _swa_mask(first):
    qi = _iota2((WINDOW, 2 * WINDOW), 0)
    si = _iota2((WINDOW, 2 * WINDOW), 1)
    band = (si > qi) & (si <= qi + WINDOW)
    return band & (jnp.logical_not(first) | (si >= WINDOW))


def _stack_heads(t, j):
    return jnp.concatenate([t[:, ATT_HD * (j * ATT_R + r):ATT_HD * (j * ATT_R + r + 1)] for r in range(ATT_R)], axis=0)


def _stack_cols(ref, j):
    cols = [jnp.broadcast_to(ref[:, j * ATT_R + r:j * ATT_R + r + 1], (WINDOW, 1)) for r in range(ATT_R)]
    return jnp.concatenate(cols, axis=0)


def _swa_fwd(q, g, kv, sinks):
    L = q.shape[0]
    nb = L // WINDOW
    scale = ATT_HD ** -0.5

    def body(q_ref, g_ref, kvc_ref, kvp_ref, sink_ref, y_ref, o_ref, lse_ref):
        n = pl.program_id(0)
        kk = _bf(jnp.concatenate([kvp_ref[:, 0:D_KV], kvc_ref[:, 0:D_KV]], axis=0))
        vv = _bf(jnp.concatenate([kvp_ref[:, D_KV:2 * D_KV], kvc_ref[:, D_KV:2 * D_KV]], axis=0))
        valid = jnp.tile(_swa_mask(n == 0), (ATT_R, 1))
        qv = q_ref[...]
        for j in range(ATT_KVH):
            js = slice(ATT_HD * j, ATT_HD * (j + 1))
            s = _dot_nt(_bf(_stack_heads(qv, j)), kk[:, js]) * scale
            s = jnp.where(valid, s, NEG_BIG)
            sink = _stack_cols(sink_ref, j)
            m = jnp.maximum(jnp.max(s, axis=-1, keepdims=True), sink)
            p = jnp.exp(s - m)
            denom = jnp.sum(p, axis=-1, keepdims=True) + jnp.exp(sink - m)
            o = _dot(_bf(p), vv[:, js]) * (1.0 / denom)
            lse = m + jnp.log(denom)
            for r in range(ATT_R):
                h = j * ATT_R + r
                rs = slice(WINDOW * r, WINDOW * (r + 1))
                o_ref[:, ATT_HD * h:ATT_HD * (h + 1)] = o[rs]
                lse_ref[:, h:h + 1] = lse[rs]
        gv = g_ref[...]
        y_ref[...] = o_ref[...] * (gv * _sigmoid(gv))

    cur = lambda wd: pl.BlockSpec((WINDOW, wd), lambda n: (n, 0))
    prv = lambda wd: pl.BlockSpec((WINDOW, wd), lambda n: (jnp.maximum(n - 1, 0), 0))
    return pl.pallas_call(
        body, name="swa_fwd", grid=(nb,),
        in_specs=[cur(D_ATT), cur(D_ATT), cur(2 * D_KV), prv(2 * D_KV), pl.BlockSpec((1, ATT_QH), lambda n: (0, 0))],
        out_specs=[cur(D_ATT), cur(D_ATT), cur(ATT_QH)],
        out_shape=[jax.ShapeDtypeStruct((L, D_ATT), F32), jax.ShapeDtypeStruct((L, D_ATT), F32),
                   jax.ShapeDtypeStruct((L, ATT_QH), F32)],
        compiler_params=_cparams(("parallel",)),
    )(q, g, kv, kv, sinks)


def _swa_bwd(dy, q, g, kv, o, lse, pos, inv, sinks):
    L = q.shape[0]
    nb = L // WINDOW
    scale = ATT_HD ** -0.5

    def body(dy_ref, q_ref, g_ref, kvc_ref, kvp_ref, o_ref, lse_ref, posc_ref, posp_ref, inv_ref, sink_ref,
             dq_ref, dg_ref, dkv_ref, dsink_ref, carry, dqbuf, dkbuf, dvbuf):
        n = pl.program_id(0)

        @pl.when(n == 0)
        def _():
            dsink_ref[...] = jnp.zeros_like(dsink_ref)

        @pl.when(n < nb)
        def _():
            tc = _rope_tables(posc_ref, inv_ref)
            tp = _rope_tables(posp_ref, inv_ref)
            kk = _bf(jnp.concatenate([kvp_ref[:, 0:D_KV], kvc_ref[:, 0:D_KV]], axis=0))
            vv = _bf(jnp.concatenate([kvp_ref[:, D_KV:2 * D_KV], kvc_ref[:, D_KV:2 * D_KV]], axis=0))
            valid = jnp.tile(_swa_mask(n == 0), (ATT_R, 1))
            qv = q_ref[...]
            gv = g_ref[...]
            sg = _sigmoid(gv)
            dyv = dy_ref[...]
            ov = o_ref[...]
            dg_ref[...] = dyv * ov * (sg * (1.0 + gv * (1.0 - sg)))
            do = dyv * (gv * sg)
            delta_all = do * ov
            lane16 = _iota2((1, ATT_QH), 1)
            dsink = jnp.zeros((1, ATT_QH), F32)
            for j in range(ATT_KVH):
                js = slice(ATT_HD * j, ATT_HD * (j + 1))
                kj = kk[:, js]
                vj = vv[:, js]
                qs = _bf(_stack_heads(qv, j))
                dos = _bf(_stack_heads(do, j))
                delta = jnp.sum(_stack_heads(delta_all, j), axis=-1, keepdims=True)
                lse = _stack_cols(lse_ref, j)
                s = _dot_nt(qs, kj) * scale
                p = jnp.exp(jnp.where(valid, s, NEG_BIG) - lse)
                dS = _bf(p * (_dot_nt(dos, vj) - delta))
                dqs = _dot(dS, kj) * scale
                dkbuf[:, js] = _dot_tn(dS, qs) * scale
                dvbuf[:, js] = _dot_tn(_bf(p), dos)
                sd = jnp.exp(_stack_cols(sink_ref, j) - lse) * delta
                for r in range(ATT_R):
                    h = j * ATT_R + r
                    rs = slice(WINDOW * r, WINDOW * (r + 1))
                    dqbuf[:, ATT_HD * h:ATT_HD * (h + 1)] = dqs[rs]
                    dsink = dsink - jnp.sum(sd[rs], axis=0, keepdims=True) * (lane16 == h).astype(F32)
            dsink_ref[...] += dsink
            dq_ref[...] = _rope_t(dqbuf[...], tc)
            dkp = _rope_t(dkbuf[0:WINDOW, :], tp)
            dkc = _rope_t(dkbuf[WINDOW:2 * WINDOW, :], tc)

            @pl.when(n > 0)
            def _():
                dkv_ref[:, 0:D_KV] = carry[:, 0:D_KV] + dkp
                dkv_ref[:, D_KV:2 * D_KV] = carry[:, D_KV:2 * D_KV] + dvbuf[0:WINDOW, :]

            carry[:, 0:D_KV] = dkc
            carry[:, D_KV:2 * D_KV] = dvbuf[WINDOW:2 * WINDOW, :]

        @pl.when(n == nb)
        def _():
            dkv_ref[...] = carry[...]

    last = nb - 1
    cur = lambda wd: pl.BlockSpec((WINDOW, wd), lambda n: (jnp.minimum(n, last), 0))
    prv = lambda wd: pl.BlockSpec((WINDOW, wd), lambda n: (jnp.maximum(jnp.minimum(n, last) - 1, 0), 0))
    return pl.pallas_call(
        body, name="swa_bwd", grid=(nb + 1,),
        in_specs=[cur(D_ATT), cur(D_ATT), cur(D_ATT), cur(2 * D_KV), prv(2 * D_KV), cur(D_ATT), cur(ATT_QH), cur(1), prv(1),
                  pl.BlockSpec((1, 2 * ATT_HD), lambda n: (0, 0)), pl.BlockSpec((1, ATT_QH), lambda n: (0, 0))],
        out_specs=[cur(D_ATT), cur(D_ATT),
                   pl.BlockSpec((WINDOW, 2 * D_KV), lambda n: (jnp.maximum(n - 1, 0), 0)),
                   pl.BlockSpec((1, ATT_QH), lambda n: (0, 0))],
        out_shape=[jax.ShapeDtypeStruct((L, D_ATT), F32), jax.ShapeDtypeStruct((L, D_ATT), F32),
                   jax.ShapeDtypeStruct((L, 2 * D_KV), F32), jax.ShapeDtypeStruct((1, ATT_QH), F32)],
        scratch_shapes=[pltpu.VMEM((WINDOW, 2 * D_KV), F32), pltpu.VMEM((WINDOW, D_ATT), F32),
                        pltpu.VMEM((2 * WINDOW, D_KV), F32), pltpu.VMEM((2 * WINDOW, D_KV), F32)],
        compiler_params=_cparams(("arbitrary",)),
    )(dy, q, g, kv, kv, o, lse, pos, pos, inv, sinks)


def _out_ln_loss(y_ssd, y_att, x, target, w_out, ln_g, ln_b):
    L = x.shape[0]
    tm = 256
    inv_d = 1.0 / D_MODEL

    def body(ys_ref, ya_ref, x_ref, t_ref, w_ref, g_ref, b_ref, dr_ref, dys_ref, dya_ref, loss_ref, gg_ref, gb_ref):
        i = pl.program_id(0)

        @pl.when(i == 0)
        def _():
            loss_ref[...] = jnp.zeros_like(loss_ref)
            gg_ref[...] = jnp.zeros_like(gg_ref)
            gb_ref[...] = jnp.zeros_like(gb_ref)

        h = _dot(_bf(ys_ref[...]), w_ref[0:D_SSD, :]) + _dot(_bf(ya_ref[...]), w_ref[D_SSD:D_MIX, :])
        r = ALPHA * x_ref[...] + h
        mu = jnp.mean(r, axis=-1, keepdims=True)
        xc = r - mu
        rstd = lax.rsqrt(jnp.mean(xc * xc, axis=-1, keepdims=True) + LN_EPS)
        xhat = xc * rstd
        gam = g_ref[...]
        diff = xhat * gam + b_ref[...] - t_ref[...]
        part = jnp.sum(jnp.sum(diff * diff, axis=-1, keepdims=True), axis=0, keepdims=True)
        loss_ref[...] += (0.5 * inv_d) * part
        dout = diff * inv_d
        gg_ref[...] += jnp.sum(dout * xhat, axis=0, keepdims=True)
        gb_ref[...] += jnp.sum(dout, axis=0, keepdims=True)
        dxh = dout * gam
        dr = rstd * (dxh - jnp.mean(dxh, axis=-1, keepdims=True) - xhat * jnp.mean(dxh * xhat, axis=-1, keepdims=True))
        dr_ref[...] = dr
        drb = _bf(dr)
        dys_ref[...] = _dot_nt(drb, w_ref[0:D_SSD, :])
        dya_ref[...] = _dot_nt(drb, w_ref[D_SSD:D_MIX, :])

    row = pl.BlockSpec((tm, D_MODEL), lambda i: (i, 0))
    vec = pl.BlockSpec((1, D_MODEL), lambda i: (0, 0))
    return pl.pallas_call(
        body, name="out_ln_loss", grid=(L // tm,),
        in_specs=[row, row, row, row, pl.BlockSpec((D_MIX, D_MODEL), lambda i: (0, 0)), vec, vec],
        out_specs=[row, row, row, pl.BlockSpec((1, 128), lambda i: (0, 0)), vec, vec],
        out_shape=[jax.ShapeDtypeStruct((L, D_MODEL), F32)] * 3 + [jax.ShapeDtypeStruct((1, 128), F32)]
        + [jax.ShapeDtypeStruct((1, D_MODEL), F32)] * 2,
        compiler_params=_cparams(("arbitrary",)),
    )(y_ssd, y_att, x, target, w_out, ln_g, ln_b)


def _local_step(x, pos, target, w, w_out, conv_w, conv_b, dt_bias, a_log, d_skip, norm_w, sinks, ln_g, ln_b):
    inv8 = ROPE_THETA ** (-jnp.arange(0, ROPE_DIM, 2, dtype=F32) / ROPE_DIM)
    inv = jnp.tile(jnp.concatenate([inv8, inv8, jnp.zeros((ATT_HD - ROPE_DIM,), F32)]), 2).reshape(1, 2 * ATT_HD)

    z, g, q, xbc, kv, dtp, xb = _in_proj(x, w, pos, inv)
    y_ssd, y_pre, prev = _ssd_fwd2(z, xbc, dtp, conv_w, conv_b, dt_bias, a_log, d_skip, norm_w)
    y_att, o, lse = _swa_fwd(q, g, kv, sinks)
    dr, dy_ssd, dy_att, loss, g_ln_g, g_ln_b = _out_ln_loss(y_ssd, y_att, x, target, w_out, ln_g, ln_b)
    gw_out_ssd = _matmul_tn(y_ssd, dr, 1024, "gw_out_ssd")
    gw_out_att = _matmul_tn(y_att, dr, 1024, "gw_out_att")
    dq, dg, dkv, g_sinks = _swa_bwd(dy_att, q, g, kv, o, lse, pos, inv, sinks)
    dz, dxbc, ddt, g_conv_w, g_conv_b, g_dt_bias, g_a_log, g_d_skip, g_norm_w = _ssd_bwd2(
        dy_ssd, z, y_pre, xbc, dtp, prev, conv_w, conv_b, dt_bias, a_log, d_skip, norm_w)
    grad_x = _grad_x(dr, dz, dg, dq, dxbc, dkv, ddt, w)
    gw_z = _matmul_tn(dz, xb, 1024, "gw_z")
    gw_g = _matmul_tn(dg, xb, 1024, "gw_g")
    gw_q = _matmul_tn(dq, xb, 1024, "gw_q")
    gw_xbc = _matmul_tn(dxbc, xb, 1024, "gw_xbc")
    gw_kv = _matmul_tn(dkv, xb, 1024, "gw_kv")
    gw_dt = _matmul_tn(ddt, xb, 1024, "gw_dt")
    gw_in = jnp.concatenate([gw_z, gw_xbc, gw_dt[0:SSD_HEADS], gw_q, gw_kv, gw_g], axis=0)
    gw_out = jnp.concatenate([gw_out_ssd, gw_out_att], axis=0)
    small = dict(conv_w=g_conv_w, conv_b=g_conv_b, dt_bias=g_dt_bias, a_log=g_a_log, d_skip=g_d_skip,
                 ssd_norm_w=g_norm_w, attn_sinks=g_sinks, ln_g=g_ln_g, ln_b=g_ln_b)
    return loss[:, 0:1], grad_x, gw_in, gw_out, small


def _mesh_pos():
    return lax.axis_index("x"), lax.axis_index("y"), lax.axis_index("c")


def _gather_weights(w_in_s, w_out_s, conv_w_s):
    def body(win_ref, wout_ref, cw_ref, owin_ref, owout_ref, ocw_ref, send_sems, recv_sems, small_send, small_recv,
             local_sems):
        x, y, c = _mesh_pos()
        me = 2 * x + y
        sibling = (x, y, 1 - c)
        chips = [(1 - x, y), (x, 1 - y), (1 - x, 1 - y)]
        locals_ = [pltpu.make_async_copy(cw_ref, ocw_ref.at[me], local_sems.at[0])]
        for cp in locals_:
            cp.start()
        started = []
        for t, (src, dst) in enumerate(((win_ref, owin_ref), (wout_ref, owout_ref))):
            hr = src.shape[0] // 2

            def half(ref, hc, hr=hr):
                return ref.at[pl.ds(hc * hr, hr), :]

            for j, (px, py) in enumerate(chips):
                cp = pltpu.make_async_remote_copy(
                    src_ref=half(src, c), dst_ref=half(dst.at[me], c), send_sem=send_sems.at[t, j],
                    recv_sem=recv_sems.at[t, j], device_id=(px, py, c), device_id_type=MESH)
                cp.start()
                started.append(cp)
        for j, (px, py) in enumerate(chips):
            cp = pltpu.make_async_remote_copy(
                src_ref=cw_ref, dst_ref=ocw_ref.at[me], send_sem=small_send.at[j], recv_sem=small_recv.at[j],
                device_id=(px, py, c), device_id_type=MESH)
            cp.start()
            started.append(cp)
        for t, (src, dst) in enumerate(((win_ref, owin_ref), (wout_ref, owout_ref))):
            hr = src.shape[0] // 2
            for j, (px, py) in enumerate(chips):
                src_chip = 2 * px + py
                blk = dst.at[src_chip].at[pl.ds(c * hr, hr), :]
                pltpu.make_async_remote_copy(
                    src_ref=blk, dst_ref=blk, send_sem=send_sems.at[t, j], recv_sem=recv_sems.at[t, j],
                    device_id=(px, py, c), device_id_type=MESH).wait_recv()
                cp = pltpu.make_async_remote_copy(
                    src_ref=blk, dst_ref=blk, send_sem=send_sems.at[t, 3 + j], recv_sem=recv_sems.at[t, 3 + j],
                    device_id=sibling, device_id_type=MESH)
                cp.start()
                started.append(cp)
        for t, (src, dst) in enumerate(((win_ref, owin_ref), (wout_ref, owout_ref))):
            hr = src.shape[0] // 2
            for j, (px, py) in enumerate(chips):
                src_chip = 2 * px + py
                blk = dst.at[src_chip].at[pl.ds((1 - c) * hr, hr), :]
                pltpu.make_async_remote_copy(
                    src_ref=blk, dst_ref=blk, send_sem=send_sems.at[t, 3 + j], recv_sem=recv_sems.at[t, 3 + j],
                    device_id=sibling, device_id_type=MESH).wait_recv()
        for j in range(3):
            pltpu.make_async_remote_copy(
                src_ref=cw_ref, dst_ref=ocw_ref.at[me], send_sem=small_send.at[j], recv_sem=small_recv.at[j],
                device_id=sibling, device_id_type=MESH).wait_recv()
        for cp in started:
            cp.wait_send()
        for cp in locals_:
            cp.wait()

    any_spec = pl.BlockSpec(memory_space=pl.ANY)
    return pl.pallas_call(
        body, name="gather_weights",
        in_specs=[any_spec] * 3, out_specs=[any_spec] * 3,
        out_shape=[jax.ShapeDtypeStruct((N_CHIPS,) + a.shape, a.dtype) for a in (w_in_s, w_out_s, conv_w_s)],
        scratch_shapes=[pltpu.SemaphoreType.DMA((2, 6)), pltpu.SemaphoreType.DMA((2, 6)),
                        pltpu.SemaphoreType.DMA((3,)), pltpu.SemaphoreType.DMA((3,)), pltpu.SemaphoreType.DMA((3,))],
    )(w_in_s, w_out_s, conv_w_s)


def _pair_exchange(gw_in, gw_out, small):
    k_small = small.shape[1]

    def body(gin_ref, gout_ref, sm_ref, rin_ref, rout_ref, slots_ref, send_sems, recv_sems, small_send, small_recv,
             local_sem):
        x, y, c = _mesh_pos()
        me = 4 * x + 2 * y + c
        sibling = (x, y, 1 - c)
        mine = pltpu.make_async_copy(sm_ref, slots_ref.at[me], local_sem)
        mine.start()
        started = []
        for t, (src, dst) in enumerate(((gin_ref, rin_ref), (gout_ref, rout_ref))):
            hr = src.shape[1] // 2
            for j in range(N_CHIPS):
                cp = pltpu.make_async_remote_copy(
                    src_ref=src.at[j, pl.ds((1 - c) * hr, hr), :], dst_ref=dst.at[j], send_sem=send_sems.at[t, j],
                    recv_sem=recv_sems.at[t, j], device_id=sibling, device_id_type=MESH)
                cp.start()
                started.append(cp)
        for k in range(1, 8):
            peer = (x ^ ((k >> 2) & 1), y ^ ((k >> 1) & 1), c ^ (k & 1))
            cp = pltpu.make_async_remote_copy(
                src_ref=sm_ref, dst_ref=slots_ref.at[me], send_sem=small_send.at[k - 1], recv_sem=small_recv.at[k - 1],
                device_id=peer, device_id_type=MESH)
            cp.start()
            started.append(cp)
        for t, (src, dst) in enumerate(((gin_ref, rin_ref), (gout_ref, rout_ref))):
            for j in range(N_CHIPS):
                pltpu.make_async_remote_copy(
                    src_ref=dst.at[j], dst_ref=dst.at[j], send_sem=send_sems.at[t, j], recv_sem=recv_sems.at[t, j],
                    device_id=sibling, device_id_type=MESH).wait_recv()
        for k in range(1, 8):
            pltpu.make_async_remote_copy(
                src_ref=sm_ref, dst_ref=slots_ref.at[me], send_sem=small_send.at[k - 1], recv_sem=small_recv.at[k - 1],
                device_id=sibling, device_id_type=MESH).wait_recv()
        for cp in started:
            cp.wait_send()
        mine.wait()

    any_spec = pl.BlockSpec(memory_space=pl.ANY)
    half_in = jax.ShapeDtypeStruct((N_CHIPS, gw_in.shape[1] // 2, D_MODEL), F32)
    half_out = jax.ShapeDtypeStruct((N_CHIPS, gw_out.shape[1] // 2, D_MODEL), F32)
    return pl.pallas_call(
        body, name="pair_exchange",
        in_specs=[any_spec] * 3, out_specs=[any_spec] * 3,
        out_shape=[half_in, half_out, jax.ShapeDtypeStruct((8, 8, k_small), F32)],
        scratch_shapes=[pltpu.SemaphoreType.DMA((2, N_CHIPS)), pltpu.SemaphoreType.DMA((2, N_CHIPS)),
                        pltpu.SemaphoreType.DMA((7,)), pltpu.SemaphoreType.DMA((7,)), pltpu.SemaphoreType.DMA],
    )(gw_in, gw_out, small)


def _chip_exchange(s_in, s_out):
    def body(sin_ref, sout_ref, rin_ref, rout_ref, send_sems, recv_sems):
        x, y, c = _mesh_pos()
        me = 2 * x + y
        chips = [(1 - x, y), (x, 1 - y), (1 - x, 1 - y)]
        started = []
        for t, (src, dst) in enumerate(((sin_ref, rin_ref), (sout_ref, rout_ref))):
            for j, (px, py) in enumerate(chips):
                cp = pltpu.make_async_remote_copy(
                    src_ref=src.at[2 * px + py], dst_ref=dst.at[me], send_sem=send_sems.at[t, j],
                    recv_sem=recv_sems.at[t, j], device_id=(px, py, c), device_id_type=MESH)
                cp.start()
                started.append(cp)
        for t, (src, dst) in enumerate(((sin_ref, rin_ref), (sout_ref, rout_ref))):
            for j, (px, py) in enumerate(chips):
                blk = dst.at[2 * px + py]
                pltpu.make_async_remote_copy(
                    src_ref=blk, dst_ref=blk, send_sem=send_sems.at[t, j], recv_sem=recv_sems.at[t, j],
                    device_id=(px, py, c), device_id_type=MESH).wait_recv()
        for cp in started:
            cp.wait_send()

    any_spec = pl.BlockSpec(memory_space=pl.ANY)
    return pl.pallas_call(
        body, name="chip_exchange",
        in_specs=[any_spec] * 2, out_specs=[any_spec] * 2,
        out_shape=[jax.ShapeDtypeStruct(s_in.shape, s_in.dtype), jax.ShapeDtypeStruct(s_out.shape, s_out.dtype)],
        scratch_shapes=[pltpu.SemaphoreType.DMA((2, 3)), pltpu.SemaphoreType.DMA((2, 3))],
    )(s_in, s_out)


def _pair_share(h_in, h_out):
    def body(hin_ref, hout_ref, rin_ref, rout_ref, send_sems, recv_sems):
        x, y, c = _mesh_pos()
        sibling = (x, y, 1 - c)
        started = []
        for t, (src, dst) in enumerate(((hin_ref, rin_ref), (hout_ref, rout_ref))):
            cp = pltpu.make_async_remote_copy(
                src_ref=src, dst_ref=dst, send_sem=send_sems.at[t], recv_sem=recv_sems.at[t],
                device_id=sibling, device_id_type=MESH)
            cp.start()
            started.append(cp)
        for cp in started:
            cp.wait()

    any_spec = pl.BlockSpec(memory_space=pl.ANY)
    return pl.pallas_call(
        body, name="pair_share",
        in_specs=[any_spec] * 2, out_specs=[any_spec] * 2,
        out_shape=[jax.ShapeDtypeStruct(h_in.shape, F32), jax.ShapeDtypeStruct(h_out.shape, F32)],
        scratch_shapes=[pltpu.SemaphoreType.DMA((2,)), pltpu.SemaphoreType.DMA((2,))],
    )(h_in, h_out)


def _pair_add(g, recv, core, name):
    _, rows, C = recv.shape
    tc = 256

    def body(core_ref, g_ref, r_ref, o_ref):
        o_ref[...] = _bf(g_ref[...] + r_ref[...])

    spec = pl.BlockSpec((1, rows, tc), lambda j, i, core: (j, 0, i))
    return pl.pallas_call(
        body, name=name,
        grid_spec=pltpu.PrefetchScalarGridSpec(
            num_scalar_prefetch=1, grid=(N_CHIPS, C // tc),
            in_specs=[pl.BlockSpec((1, rows, tc), lambda j, i, core: (j, core[0], i)), spec], out_specs=spec),
        out_shape=jax.ShapeDtypeStruct((N_CHIPS, rows, C), BF16),
        compiler_params=_cparams(("parallel", "parallel")),
    )(core, g, recv)


def _chip_add(own, parts, chip, name):
    _, rows, C = parts.shape
    tc = 256

    def body(chip_ref, own_ref, r0, r1, r2, r3, o_ref):
        acc = None
        for j, r in enumerate((r0, r1, r2, r3)):
            term = jnp.where(chip_ref[0] == j, own_ref[0], r[0]).astype(F32)
            acc = term if acc is None else acc + term
        o_ref[...] = acc

    def slab(j):
        return pl.BlockSpec((1, rows, tc), lambda i, chip: (jnp.where(chip[0] == j, (j + 1) % N_CHIPS, j), 0, i))

    return pl.pallas_call(
        body, name=name,
        grid_spec=pltpu.PrefetchScalarGridSpec(
            num_scalar_prefetch=1, grid=(C // tc,),
            in_specs=[pl.BlockSpec((1, rows, tc), lambda i, chip: (chip[0], 0, i))] + [slab(j) for j in range(N_CHIPS)],
            out_specs=pl.BlockSpec((rows, tc), lambda i, chip: (0, i))),
        out_shape=jax.ShapeDtypeStruct((rows, C), F32),
        compiler_params=_cparams(("parallel",)),
    )(chip, own, parts, parts, parts, parts)


def _adamw_math(w, g, m, v):
    m = ADAM_B1 * m + (1.0 - ADAM_B1) * g
    v = ADAM_B2 * v + (1.0 - ADAM_B2) * (g * g)
    m_hat = m / (1.0 - ADAM_B1 ** ADAM_STEP)
    v_hat = v / (1.0 - ADAM_B2 ** ADAM_STEP)
    delta = -ADAM_LR * (m_hat / (jnp.sqrt(v_hat) + ADAM_EPS) + ADAM_WD * w)
    return delta, m, v


def _adamw_pair(w, g_own, g_sib, m, v, core, name):
    R, C = w.shape
    rows = g_own.shape[0]
    tc = 128

    def body(core_ref, w_ref, go_ref, gs_ref, m_ref, v_ref, d_ref, nm_ref, nv_ref, g_ref):
        first = core_ref[0] == 0
        own, sib = go_ref[...], gs_ref[...]
        g = jnp.concatenate([jnp.where(first, own, sib), jnp.where(first, sib, own)], axis=0)[0:R, :]
        d, nm, nv = _adamw_math(w_ref[...], g, m_ref[...], v_ref[...])
        d_ref[...] = d
        nm_ref[...] = nm
        nv_ref[...] = nv
        g_ref[...] = g

    spec = pl.BlockSpec((R, tc), lambda i, core: (0, i))
    gspec = pl.BlockSpec((rows, tc), lambda i, core: (0, i))
    return pl.pallas_call(
        body, name=name,
        grid_spec=pltpu.PrefetchScalarGridSpec(
            num_scalar_prefetch=1, grid=(C // tc,),
            in_specs=[spec, gspec, gspec, spec, spec], out_specs=[spec] * 4),
        out_shape=[jax.ShapeDtypeStruct((R, C), F32)] * 4,
        compiler_params=_cparams(("parallel",)),
    )(core, w, g_own, g_sib, m, v)


def _adamw(w, g, m, v, name):
    R, C = w.shape
    tc = 128

    def body(w_ref, g_ref, m_ref, v_ref, d_ref, nm_ref, nv_ref, go_ref):
        g = g_ref[0:R, :]
        d, nm, nv = _adamw_math(w_ref[...], g, m_ref[...], v_ref[...])
        d_ref[...] = d
        nm_ref[...] = nm
        nv_ref[...] = nv
        go_ref[...] = g

    spec = pl.BlockSpec((R, tc), lambda i: (0, i))
    gspec = pl.BlockSpec((g.shape[0], tc), lambda i: (0, i))
    return pl.pallas_call(
        body, name=name, grid=(C // tc,),
        in_specs=[spec, gspec, spec, spec], out_specs=[spec] * 4,
        out_shape=[jax.ShapeDtypeStruct((R, C), F32)] * 4,
        compiler_params=_cparams(("parallel",)),
    )(w, g, m, v)


def _sum_slots(slots):
    k = slots.shape[2]

    def body(s_ref, o_ref):
        acc = s_ref[0]
        for d in range(1, 8):
            acc = acc + s_ref[d]
        o_ref[...] = acc

    return pl.pallas_call(
        body, name="sum_slots", out_shape=jax.ShapeDtypeStruct((8, k), F32),
        compiler_params=_cparams(),
    )(slots)


SMALL_NAMES = ("conv_b", "ssd_norm_w", "ln_g", "ln_b", "dt_bias", "a_log", "d_skip", "attn_sinks")
SMALL_SIZES = (D_XBC, D_SSD, D_MODEL, D_MODEL, SSD_HEADS, SSD_HEADS, SSD_HEADS, ATT_QH)


def _pack_vectors(vals):
    parts = []
    for v in vals:
        n = v.shape[1]
        pad = (-n) % 128
        parts.append(jnp.pad(v, ((0, 0), (0, pad))) if pad else v)
    return jnp.concatenate(parts, axis=1)


def _unpack_vectors(row):
    out, off = [], 0
    for n in SMALL_SIZES:
        out.append(row[:, off:off + n])
        off += n + ((-n) % 128)
    return out


def kernel(x, positions, w_in, conv_w, conv_b, dt_bias, a_log, d_skip, ssd_norm_w, attn_sinks, w_out, ln_g, ln_b, loss_target, m_w_in, m_conv_w, m_conv_b, m_dt_bias, m_a_log, m_d_skip, m_ssd_norm_w, m_attn_sinks, m_w_out, m_ln_g, m_ln_b, v_w_in, v_conv_w, v_conv_b, v_dt_bias, v_a_log, v_d_skip, v_ssd_norm_w, v_attn_sinks, v_w_out, v_ln_g, v_ln_b):
    mx, my, mc = _mesh_pos()
    chip = 2 * mx + my
    L = x.shape[1]

    conv_w_s8 = jnp.pad(conv_w[0], ((0, 8 - CONV_K), (0, 0)))
    pad_rows = ((0, SLAB_ROWS - W_IN_COLS), (0, 0))
    w_in_t, m_w_in_t, v_w_in_t = w_in[0].T, m_w_in[0].T, v_w_in[0].T
    w_in_b, w_out_b = jnp.pad(_bf(w_in_t), pad_rows), _bf(w_out[0])
    ag_in, ag_out, ag_cw = _gather_weights(w_in_b, w_out_b, conv_w_s8)
    ag_in = jnp.where((jnp.arange(N_CHIPS) == chip)[:, None, None], w_in_b[None], ag_in)
    ag_out = jnp.where((jnp.arange(N_CHIPS) == chip)[:, None, None], w_out_b[None], ag_out)
    w_full = jnp.concatenate([ag_in[j, 0:W_IN_COLS] for j in range(N_CHIPS)], axis=0)
    w = jnp.concatenate([
        w_full[O_Z:O_Z + D_SSD], w_full[O_G:O_G + D_ATT], w_full[O_Q:O_Q + D_ATT],
        w_full[O_XBC:O_XBC + D_XBC], w_full[O_K:O_K + 2 * D_KV], w_full[O_DT:O_DT + SSD_HEADS],
        jnp.zeros((DT_PAD - SSD_HEADS, D_MODEL), BF16)], axis=0)
    w_out_full = ag_out.reshape(D_MIX, D_MODEL)
    conv_w_full = jnp.concatenate([ag_cw[j, 0:CONV_K] for j in range(N_CHIPS)], axis=1)

    loss_part, grad_x, gw_in, gw_out, small = _local_step(
        x[0], positions[0].reshape(L, 1), loss_target[0], w, w_out_full, conv_w_full, conv_b, dt_bias, a_log, d_skip,
        ssd_norm_w, attn_sinks, ln_g, ln_b)

    vec = _pack_vectors([small[n] for n in SMALL_NAMES])
    top = jnp.concatenate([vec, jnp.pad(loss_part, ((0, 0), (0, 127)))], axis=1)
    right = jnp.pad(top, ((0, 7), (0, 0)))
    packed = jnp.concatenate([jnp.pad(small["conv_w"], ((0, 8 - CONV_K), (0, 0))), right], axis=1)

    gw_in_slabs = jnp.stack([jnp.pad(gw_in[W_IN_COLS * j:W_IN_COLS * (j + 1)], pad_rows) for j in range(N_CHIPS)])
    gw_out_slabs = gw_out.reshape(N_CHIPS, W_OUT_ROWS, D_MODEL)
    core_id = mc.reshape(1).astype(jnp.int32)
    chip_id = chip.reshape(1).astype(jnp.int32)
    recv_in, recv_out, slots = _pair_exchange(gw_in_slabs, gw_out_slabs, packed)
    s_in = _pair_add(gw_in_slabs, recv_in, core_id, "pair_add_in")
    s_out = _pair_add(gw_out_slabs, recv_out, core_id, "pair_add_out")
    r_in, r_out = _chip_exchange(s_in, s_out)
    h_in = _chip_add(s_in, r_in, chip_id, "chip_add_in")
    h_out = _chip_add(s_out, r_out, chip_id, "chip_add_out")
    sib_in, sib_out = _pair_share(h_in, h_out)
    tot = _sum_slots(slots)

    in_t = _adamw_pair(w_in_t, h_in, sib_in, m_w_in_t, v_w_in_t, core_id, "adamw_w_in")
    d_w_in, nm_w_in, nv_w_in, g_w_in = [a.T for a in in_t]
    d_w_out, nm_w_out, nv_w_out, g_w_out = _adamw_pair(w_out[0], h_out, sib_out, m_w_out[0], v_w_out[0], core_id,
                                                       "adamw_w_out")
    g_conv_w_all = tot[0:CONV_K, 0:D_XBC]
    g_conv_w = lax.dynamic_slice(g_conv_w_all, (0, chip * CONV_COLS), (CONV_K, CONV_COLS))
    g_vecs = _unpack_vectors(tot[0:1, D_XBC:D_XBC + 5120])
    loss = tot[0, D_XBC + 5120]

    grads = dict(zip(SMALL_NAMES, g_vecs))
    params = dict(conv_b=conv_b, ssd_norm_w=ssd_norm_w, ln_g=ln_g, ln_b=ln_b, dt_bias=dt_bias, a_log=a_log,
                  d_skip=d_skip, attn_sinks=attn_sinks)
    moms = dict(conv_b=m_conv_b, ssd_norm_w=m_ssd_norm_w, ln_g=m_ln_g, ln_b=m_ln_b, dt_bias=m_dt_bias, a_log=m_a_log,
                d_skip=m_d_skip, attn_sinks=m_attn_sinks)
    vars_ = dict(conv_b=v_conv_b, ssd_norm_w=v_ssd_norm_w, ln_g=v_ln_g, ln_b=v_ln_b, dt_bias=v_dt_bias, a_log=v_a_log,
                 d_skip=v_d_skip, attn_sinks=v_attn_sinks)
    def small_block(cw, vecs):
        a = jnp.pad(cw, ((0, 0), (0, 5120 - CONV_COLS)))
        return jnp.concatenate([a, _pack_vectors(vecs), jnp.zeros((3, 5120), F32)], axis=0)

    sw = small_block(conv_w[0], [params[n] for n in SMALL_NAMES])
    sg = small_block(g_conv_w, [grads[n] for n in SMALL_NAMES])
    sm = small_block(m_conv_w[0], [moms[n] for n in SMALL_NAMES])
    sv = small_block(v_conv_w[0], [vars_[n] for n in SMALL_NAMES])
    sd, snm, snv, _ = _adamw(sw, sg, sm, sv, "adamw_small")

    def split_small(blk):
        d = dict(zip(SMALL_NAMES, _unpack_vectors(blk[CONV_K:CONV_K + 1])))
        d["conv_w"] = blk[0:CONV_K, 0:CONV_COLS][None]
        return d

    delta, new_m, new_v = split_small(sd), split_small(snm), split_small(snv)
    grads["conv_w"] = g_conv_w[None]
    for dd, a_in, a_out in ((grads, g_w_in, g_w_out), (delta, d_w_in, d_w_out), (new_m, nm_w_in, nm_w_out),
                            (new_v, nv_w_in, nv_w_out)):
        dd["w_in"] = a_in[None]
        dd["w_out"] = a_out[None]
    order = ("w_in", "conv_w", "conv_b", "dt_bias", "a_log", "d_skip", "ssd_norm_w", "attn_sinks", "w_out", "ln_g", "ln_b")
    return (loss, grad_x[None], *[grads[n] for n in order], *[delta[n] for n in order], *[new_m[n] for n in order],
            *[new_v[n] for n in order])
```

```python
import functools

import numpy as np
import jax
import jax.numpy as jnp
from jax import lax
from jax.experimental import pallas as pl
from jax.experimental.pallas import tpu as pltpu

F32 = jnp.float32
BF16 = jnp.bfloat16
MESH = pl.DeviceIdType.MESH

D_MODEL = 1024
D_SSD = 1024
D_ATT = 1024
D_MIX = 2048
SSD_HEADS = 16
SSD_P = 64
SSD_GROUPS = 2
SSD_R = 8
SSD_N = 128
D_BC = 256
D_XBC = 1536
CONV_K = 4
CHUNK = 128
ATT_HD = 64
ATT_QH = 16
ATT_KVH = 4
ATT_R = 4
D_KV = 256
WINDOW = 128
ROPE_THETA = 500000.0
ROPE_DIM = 16
ALPHA = 2.0 ** 0.25
LN_EPS = 1e-5
RMS_EPS = 1e-5
D_IN_PROJ = 5136
O_Z, O_XBC, O_DT, O_Q, O_K, O_V, O_G = 0, 1024, 2560, 2576, 3600, 3856, 4112
P_Z, P_G, P_Q, P_XBC, P_KV, P_DT, P_END = 0, 1024, 2048, 3072, 4608, 5120, 5248
DT_PAD = 128
N_CHIPS = 4
W_IN_COLS = D_IN_PROJ // N_CHIPS
SLAB_ROWS = 1312
W_OUT_ROWS = D_MIX // N_CHIPS
CONV_COLS = D_XBC // N_CHIPS

ADAM_LR = 0.001
ADAM_B1 = 0.9
ADAM_B2 = 0.999
ADAM_EPS = 1e-08
ADAM_WD = 0.01
ADAM_STEP = 10

VMEM_LIMIT = 56 * 1024 * 1024
NEG_BIG = -1e30
HI = lax.Precision.HIGHEST


def _cparams(sem=None, **kw):
    if sem is not None:
        kw["dimension_semantics"] = sem
    return pltpu.CompilerParams(vmem_limit_bytes=VMEM_LIMIT, **kw)


def _dot(a, b):
    return jnp.dot(a, b, preferred_element_type=F32)


def _dot_nt(a, b):
    return lax.dot_general(a, b, (((1,), (1,)), ((), ())), preferred_element_type=F32)


def _dot_tn(a, b):
    return lax.dot_general(a, b, (((0,), (0,)), ((), ())), preferred_element_type=F32)


def _bf(a):
    return a.astype(BF16)


def _iota2(shape, dim):
    return lax.broadcasted_iota(jnp.int32, shape, dim)


def _to_rows(col):
    k = col.shape[1]
    eye = (_iota2((k, k), 0) == _iota2((k, k), 1)).astype(F32)
    return lax.dot_general(eye, col, (((1,), (1,)), ((), ())), preferred_element_type=F32, precision=HI)


def _to_cols(row):
    n = row.shape[1]
    eye = (_iota2((n, n), 0) == _iota2((n, n), 1)).astype(F32)
    return lax.dot_general(eye, row, (((1,), (1,)), ((), ())), preferred_element_type=F32, precision=HI)


def _sigmoid(x):
    return jax.nn.sigmoid(x)


def _in_proj(x, w, pos, inv):
    L = x.shape[0]
    tm = 256
    widths = (D_SSD, D_ATT, D_ATT, D_XBC, 2 * D_KV, DT_PAD)

    def body(x_ref, w_ref, pos_ref, inv_ref, z_ref, g_ref, q_ref, xbc_ref, kv_ref, dt_ref, xb_ref):
        xb = _bf(x_ref[...])
        xb_ref[...] = xb
        for o_ref, off, wd in zip((z_ref, g_ref, xbc_ref, dt_ref), (P_Z, P_G, P_XBC, P_DT), (D_SSD, D_ATT, D_XBC, DT_PAD)):
            o_ref[...] = _dot_nt(xb, w_ref[off:off + wd, :])
        tabs = _rope_tables(pos_ref, inv_ref)
        q_ref[...] = _rope(_dot_nt(xb, w_ref[P_Q:P_Q + D_ATT, :]), tabs)
        kv_ref[:, 0:D_KV] = _rope(_dot_nt(xb, w_ref[P_KV:P_KV + D_KV, :]), tabs)
        kv_ref[:, D_KV:2 * D_KV] = _dot_nt(xb, w_ref[P_KV + D_KV:P_KV + 2 * D_KV, :])

    row = lambda wd: pl.BlockSpec((tm, wd), lambda i: (i, 0))
    return pl.pallas_call(
        body, name="in_proj", grid=(L // tm,),
        in_specs=[row(D_MODEL), pl.BlockSpec((P_END, D_MODEL), lambda i: (0, 0)), row(1),
                  pl.BlockSpec((1, 2 * ATT_HD), lambda i: (0, 0))],
        out_specs=[row(wd) for wd in widths] + [row(D_MODEL)],
        out_shape=[jax.ShapeDtypeStruct((L, wd), F32) for wd in widths] + [jax.ShapeDtypeStruct((L, D_MODEL), BF16)],
        compiler_params=_cparams(("parallel",)),
    )(x, w, pos, inv)


def _matmul_tn(a, b, tn, name):
    K, M = a.shape
    N = b.shape[1]
    tk = 512
    nk = K // tk

    def body(a_ref, b_ref, o_ref, acc_ref):
        k = pl.program_id(1)

        @pl.when(k == 0)
        def _():
            acc_ref[...] = jnp.zeros_like(acc_ref)

        acc_ref[...] += _dot_tn(_bf(a_ref[...]), _bf(b_ref[...]))

        @pl.when(k == nk - 1)
        def _():
            o_ref[...] = acc_ref[...]

    return pl.pallas_call(
        body, name=name, grid=(N // tn, nk),
        in_specs=[pl.BlockSpec((tk, M), lambda j, k: (k, 0)), pl.BlockSpec((tk, tn), lambda j, k: (k, j))],
        out_specs=pl.BlockSpec((M, tn), lambda j, k: (0, j)),
        out_shape=jax.ShapeDtypeStruct((M, N), F32),
        scratch_shapes=[pltpu.VMEM((M, tn), F32)],
        compiler_params=_cparams(("parallel", "arbitrary")),
    )(a, b)


def _grad_x(dr, dz, dg, dq, dxbc, dkv, ddt, w):
    L = dr.shape[0]
    tm = 256
    widths = (D_SSD, D_ATT, D_ATT, D_XBC, 2 * D_KV, DT_PAD)
    offs = (P_Z, P_G, P_Q, P_XBC, P_KV, P_DT)

    def body(dr_ref, dz_ref, dg_ref, dq_ref, dxbc_ref, dkv_ref, ddt_ref, w_ref, o_ref):
        acc = ALPHA * dr_ref[...]
        for p_ref, off, wd in zip((dz_ref, dg_ref, dq_ref, dxbc_ref, dkv_ref, ddt_ref), offs, widths):
            acc = acc + _dot(_bf(p_ref[...]), w_ref[off:off + wd, :])
        o_ref[...] = acc

    row = lambda wd: pl.BlockSpec((tm, wd), lambda i: (i, 0))
    return pl.pallas_call(
        body, name="grad_x", grid=(L // tm,),
        in_specs=[row(D_MODEL)] + [row(wd) for wd in widths] + [pl.BlockSpec((P_END, D_MODEL), lambda i: (0, 0))],
        out_specs=row(D_MODEL),
        out_shape=jax.ShapeDtypeStruct((L, D_MODEL), F32),
        compiler_params=_cparams(("parallel",)),
    )(dr, dz, dg, dq, dxbc, dkv, ddt, w)


def _ssd_chunk_pre(first, xbc_ref, tail_ref, dt_ref, cw_ref, cb_ref, dtb_ref, alog_ref, ext):
    tail = jnp.where(first, 0.0, tail_ref[...])
    ext[0:8, :] = tail
    ext[8:8 + CHUNK, :] = xbc_ref[...]
    u = cb_ref[...] + cw_ref[0:1, :] * ext[pl.ds(5, CHUNK), :]
    for k in range(1, CONV_K):
        u = u + cw_ref[k:k + 1, :] * ext[pl.ds(5 + k, CHUNK), :]
    sig = _sigmoid(u)
    xbc = u * sig
    dtraw = dt_ref[:, 0:SSD_HEADS] + dtb_ref[...]
    dt = jax.nn.softplus(dtraw)
    A = -jnp.exp(alog_ref[...])
    a = dt * A
    tril = (_iota2((CHUNK, CHUNK), 0) >= _iota2((CHUNK, CHUNK), 1)).astype(F32)
    acs = jnp.dot(tril, a, preferred_element_type=F32, precision=HI)
    acs_row = _to_rows(acs)
    return u, sig, xbc, dtraw, dt, A, acs, acs_row


def _ssd_fwd(z, xbc, dtp, conv_w, conv_b, dt_bias, a_log, d_skip, norm_w):
    L = z.shape[0]
    nc = L // CHUNK

    def body(z_ref, xbc_ref, tail_ref, dt_ref, cw_ref, cb_ref, dtb_ref, alog_ref, dsk_ref, nw_ref,
             y_ref, ypre_ref, prev_ref, state, ext, ybuf):
        c = pl.program_id(0)

        @pl.when(c == 0)
        def _():
            state[...] = jnp.zeros_like(state)

        u, sig, xbcv, dtraw, dt, A, acs, acs_row = _ssd_chunk_pre(
            c == 0, xbc_ref, tail_ref, dt_ref, cw_ref, cb_ref, dtb_ref, alog_ref, ext)
        prev_ref[0] = state[...]
        causal = _iota2((CHUNK, CHUNK), 0) >= _iota2((CHUNK, CHUNK), 1)
        alast = acs[CHUNK - 1:CHUNK, :]
        for g in range(SSD_GROUPS):
            Bg = _bf(xbcv[:, D_SSD + SSD_N * g:D_SSD + SSD_N * (g + 1)])
            Cg = _bf(xbcv[:, D_SSD + D_BC + SSD_N * g:D_SSD + D_BC + SSD_N * (g + 1)])
            cb = _dot_nt(Cg, Bg)
            for r in range(SSD_R):
                h = g * SSD_R + r
                hs = slice(SSD_P * h, SSD_P * (h + 1))
                acs_c = acs[:, h:h + 1]
                seg = acs_c - acs_row[h:h + 1, :]
                Lm = jnp.where(causal, jnp.exp(jnp.where(causal, seg, 0.0)), 0.0)
                M = cb * Lm
                xh = xbcv[:, hs]
                X = xh * dt[:, h:h + 1]
                prev_h = state[hs, :]
                ydiag = _dot(_bf(M), _bf(X))
                yoff = _dot_nt(Cg, _bf(prev_h)) * jnp.exp(acs_c)
                al = alast[:, h:h + 1]
                Xd = X * jnp.exp(al - acs_c)
                state[hs, :] = prev_h * jnp.exp(al) + _dot_tn(_bf(Xd), Bg)
                ybuf[:, hs] = ydiag + yoff + dsk_ref[:, h:h + 1] * xh
        y = ybuf[...]
        ypre_ref[...] = y
        zv = z_ref[...]
        yf = y * (zv * _sigmoid(zv))
        half = D_SSD // SSD_GROUPS
        for g in range(SSD_GROUPS):
            gs = slice(half * g, half * (g + 1))
            yg = yf[:, gs]
            ms = jnp.mean(yg * yg, axis=-1, keepdims=True)
            y_ref[:, gs] = yg * lax.rsqrt(ms + RMS_EPS) * nw_ref[:, gs]

    full = lambda shape: pl.BlockSpec(shape, lambda c: (0, 0))
    return pl.pallas_call(
        body, name="ssd_fwd", grid=(nc,),
        in_specs=[
            pl.BlockSpec((CHUNK, D_SSD), lambda c: (c, 0)),
            pl.BlockSpec((CHUNK, D_XBC), lambda c: (c, 0)),
            pl.BlockSpec((8, D_XBC), lambda c: (jnp.maximum(c * (CHUNK // 8) - 1, 0), 0)),
            pl.BlockSpec((CHUNK, DT_PAD), lambda c: (c, 0)),
            full((CONV_K, D_XBC)), full((1, D_XBC)), full((1, SSD_HEADS)), full((1, SSD_HEADS)), full((1, SSD_HEADS)),
            full((1, D_SSD)),
        ],
        out_specs=[
            pl.BlockSpec((CHUNK, D_SSD), lambda c: (c, 0)),
            pl.BlockSpec((CHUNK, D_SSD), lambda c: (c, 0)),
            pl.BlockSpec((1, SSD_HEADS * SSD_P, SSD_N), lambda c: (c, 0, 0)),
        ],
        out_shape=[
            jax.ShapeDtypeStruct((L, D_SSD), F32),
            jax.ShapeDtypeStruct((L, D_SSD), F32),
            jax.ShapeDtypeStruct((nc, SSD_HEADS * SSD_P, SSD_N), F32),
        ],
        scratch_shapes=[
            pltpu.VMEM((SSD_HEADS * SSD_P, SSD_N), F32),
            pltpu.VMEM((CHUNK + 8, D_XBC), F32),
            pltpu.VMEM((CHUNK, D_SSD), F32),
        ],
        compiler_params=_cparams(("arbitrary",)),
    )(z, xbc, xbc, dtp, conv_w, conv_b, dt_bias, a_log, d_skip, norm_w)


def _ssd_bwd(dy, z, ypre, xbc, dtp, prev, conv_w, conv_b, dt_bias, a_log, d_skip, norm_w):
    L = z.shape[0]
    nc = L // CHUNK

    def body(dy_ref, z_ref, ypre_ref, xbc_ref, tail_ref, dt_ref, prev_ref, cw_ref, cb_ref, dtb_ref, alog_ref, dsk_ref,
             nw_ref, dz_ref, dxbc_ref, ddt_ref, gcw_ref, gcb_ref, gdtb_ref, galog_ref, gdsk_ref, gnw_ref,
             dstate, dhead, ext, ext2, dpost):
        i = pl.program_id(0)
        c = nc - 1 - i

        @pl.when(i == 0)
        def _():
            dstate[...] = jnp.zeros_like(dstate)
            dhead[...] = jnp.zeros_like(dhead)
            gcw_ref[...] = jnp.zeros_like(gcw_ref)
            gcb_ref[...] = jnp.zeros_like(gcb_ref)
            gdtb_ref[...] = jnp.zeros_like(gdtb_ref)
            galog_ref[...] = jnp.zeros_like(galog_ref)
            gdsk_ref[...] = jnp.zeros_like(gdsk_ref)
            gnw_ref[...] = jnp.zeros_like(gnw_ref)

        u, sig, xbcv, dtraw, dt, A, acs, acs_row = _ssd_chunk_pre(
            c == 0, xbc_ref, tail_ref, dt_ref, cw_ref, cb_ref, dtb_ref, alog_ref, ext)

        zv = z_ref[...]
        ypre = ypre_ref[...]
        dyn = dy_ref[...]
        sz = _sigmoid(zv)
        silu_z = zv * sz
        yf = ypre * silu_z
        half = D_SSD // SSD_GROUPS
        dyf_parts = []
        for g in range(SSD_GROUPS):
            gs = slice(half * g, half * (g + 1))
            yg = yf[:, gs]
            rstd = lax.rsqrt(jnp.mean(yg * yg, axis=-1, keepdims=True) + RMS_EPS)
            dout = dyn[:, gs]
            gnw_ref[:, gs] += jnp.sum(dout * yg * rstd, axis=0, keepdims=True)
            dyhat = dout * nw_ref[:, gs]
            dyf_parts.append(rstd * (dyhat - yg * (rstd * rstd) * jnp.mean(dyhat * yg, axis=-1, keepdims=True)))
        dyf = jnp.concatenate(dyf_parts, axis=1)
        dz_ref[...] = dyf * ypre * (sz * (1.0 + zv * (1.0 - sz)))
        dypre = dyf * silu_z

        causal = _iota2((CHUNK, CHUNK), 0) >= _iota2((CHUNK, CHUNK), 1)
        alast = acs[CHUNK - 1:CHUNK, :]
        lane16 = _iota2((1, SSD_HEADS), 1)
        sub16 = _iota2((SSD_HEADS, 1), 0)
        dacs_col = jnp.zeros((CHUNK, SSD_HEADS), F32)
        dacs_row = jnp.zeros((SSD_HEADS, CHUNK), F32)
        ddt_col = jnp.zeros((CHUNK, SSD_HEADS), F32)
        dalast = jnp.zeros((1, SSD_HEADS), F32)
        gdsk = jnp.zeros((1, SSD_HEADS), F32)
        for g in range(SSD_GROUPS):
            bs = slice(D_SSD + SSD_N * g, D_SSD + SSD_N * (g + 1))
            cs = slice(D_SSD + D_BC + SSD_N * g, D_SSD + D_BC + SSD_N * (g + 1))
            Bg = _bf(xbcv[:, bs])
            Cg = _bf(xbcv[:, cs])
            cb = _dot_nt(Cg, Bg)
            dcb = jnp.zeros((CHUNK, CHUNK), F32)
            dB = jnp.zeros((CHUNK, SSD_N), F32)
            dC = jnp.zeros((CHUNK, SSD_N), F32)
            for r in range(SSD_R):
                h = g * SSD_R + r
                hs = slice(SSD_P * h, SSD_P * (h + 1))
                onehot = (lane16 == h).astype(F32)
                acs_c = acs[:, h:h + 1]
                seg = acs_c - acs_row[h:h + 1, :]
                Lm = jnp.where(causal, jnp.exp(jnp.where(causal, seg, 0.0)), 0.0)
                M = cb * Lm
                xh = xbcv[:, hs]
                dth = dt[:, h:h + 1]
                X = xh * dth
                Xb = _bf(X)
                dyh = dypre[:, hs]
                dyb = _bf(dyh)
                prev_h = prev_ref[0, hs, :]
                prevb = _bf(prev_h)
                dnext = dstate[hs, :]
                dnextb = _bf(dnext)
                al = alast[:, h:h + 1]
                eacs = jnp.exp(acs_c)
                eal = jnp.exp(al)
                dsd = jnp.exp(al - acs_c)
                G = _bf(dyh * eacs)
                dstate[hs, :] = dnext * eal + _dot_tn(G, Cg)
                dC = dC + _dot(G, prevb)
                yoff = _dot_nt(Cg, prevb) * eacs
                dacs_h = jnp.sum(dyh * yoff, axis=-1, keepdims=True)
                BdN = _dot_nt(Bg, dnextb)
                dX = dsd * BdN
                dB = dB + _dot(_bf(X * dsd), dnextb)
                t = jnp.sum(X * BdN, axis=-1, keepdims=True) * dsd
                dacs_h = dacs_h - t
                dal = jnp.sum(t, axis=0, keepdims=True) + jnp.sum(
                    jnp.sum(dnext * prev_h, axis=-1, keepdims=True), axis=0, keepdims=True) * eal
                dM = _dot_nt(dyb, Xb)
                dX = dX + _dot_tn(_bf(M), dyb)
                dseg = dM * M
                dcb = dcb + dM * Lm
                dacs_h = dacs_h + jnp.sum(dseg, axis=-1, keepdims=True)
                dacs_row = dacs_row - jnp.sum(dseg, axis=0, keepdims=True) * (sub16 == h).astype(F32)
                dacs_col = dacs_col + dacs_h * onehot
                dalast = dalast + dal * onehot
                ddt_col = ddt_col + jnp.sum(dX * xh, axis=-1, keepdims=True) * onehot
                gdsk = gdsk + jnp.sum(jnp.sum(dyh * xh, axis=-1, keepdims=True), axis=0, keepdims=True) * onehot
                dpost[:, hs] = dX * dth + dsk_ref[:, h:h + 1] * dyh
            dcbb = _bf(dcb)
            dpost[:, bs] = dB + _dot_tn(dcbb, Cg)
            dpost[:, cs] = dC + _dot(dcbb, Bg)

        is_last = (_iota2((CHUNK, 1), 0) == CHUNK - 1).astype(F32)
        dacs = dacs_col + _to_cols(dacs_row) + is_last * dalast
        triu = (_iota2((CHUNK, CHUNK), 0) <= _iota2((CHUNK, CHUNK), 1)).astype(F32)
        da = jnp.dot(triu, dacs, preferred_element_type=F32, precision=HI)
        ddt_tot = ddt_col + da * A
        galog_ref[...] += jnp.sum(da * dt, axis=0, keepdims=True) * A
        ddtraw = ddt_tot * _sigmoid(dtraw)
        gdtb_ref[...] += jnp.sum(ddtraw, axis=0, keepdims=True)
        gdsk_ref[...] += gdsk
        ddt_ref[...] = jnp.zeros_like(ddt_ref)
        ddt_ref[:, 0:SSD_HEADS] = ddtraw

        dconv = dpost[...] * (sig * (1.0 + u * (1.0 - sig)))
        gcb_ref[...] += jnp.sum(dconv, axis=0, keepdims=True)
        for k in range(CONV_K):
            gcw_ref[k:k + 1, :] += jnp.sum(dconv * ext[pl.ds(5 + k, CHUNK), :], axis=0, keepdims=True)
        ext2[0:CHUNK, :] = dconv
        ext2[CHUNK:CHUNK + 8, :] = dhead[...]
        dx = cw_ref[CONV_K - 1:CONV_K, :] * dconv
        for k in range(CONV_K - 1):
            dx = dx + cw_ref[k:k + 1, :] * ext2[pl.ds(CONV_K - 1 - k, CHUNK), :]
        dxbc_ref[...] = dx
        dhead[...] = dconv[0:8, :]

    full = lambda shape: pl.BlockSpec(shape, lambda i: (0, 0))
    rev = lambda wd: pl.BlockSpec((CHUNK, wd), lambda i: (nc - 1 - i, 0))
    return pl.pallas_call(
        body, name="ssd_bwd", grid=(nc,),
        in_specs=[
            rev(D_SSD), rev(D_SSD), rev(D_SSD), rev(D_XBC),
            pl.BlockSpec((8, D_XBC), lambda i: (jnp.maximum((nc - 1 - i) * (CHUNK // 8) - 1, 0), 0)),
            rev(DT_PAD),
            pl.BlockSpec((1, SSD_HEADS * SSD_P, SSD_N), lambda i: (nc - 1 - i, 0, 0)),
            full((CONV_K, D_XBC)), full((1, D_XBC)), full((1, SSD_HEADS)), full((1, SSD_HEADS)), full((1, SSD_HEADS)),
            full((1, D_SSD)),
        ],
        out_specs=[
            rev(D_SSD), rev(D_XBC), rev(DT_PAD),
            full((CONV_K, D_XBC)), full((1, D_XBC)), full((1, SSD_HEADS)), full((1, SSD_HEADS)), full((1, SSD_HEADS)),
            full((1, D_SSD)),
        ],
        out_shape=[
            jax.ShapeDtypeStruct((L, D_SSD), F32), jax.ShapeDtypeStruct((L, D_XBC), F32),
            jax.ShapeDtypeStruct((L, DT_PAD), F32),
            jax.ShapeDtypeStruct((CONV_K, D_XBC), F32), jax.ShapeDtypeStruct((1, D_XBC), F32),
            jax.ShapeDtypeStruct((1, SSD_HEADS), F32), jax.ShapeDtypeStruct((1, SSD_HEADS), F32),
            jax.ShapeDtypeStruct((1, SSD_HEADS), F32), jax.ShapeDtypeStruct((1, D_SSD), F32),
        ],
        scratch_shapes=[
            pltpu.VMEM((SSD_HEADS * SSD_P, SSD_N), F32),
            pltpu.VMEM((8, D_XBC), F32),
            pltpu.VMEM((CHUNK + 8, D_XBC), F32),
            pltpu.VMEM((CHUNK + 8, D_XBC), F32),
            pltpu.VMEM((CHUNK, D_XBC), F32),
        ],
        compiler_params=_cparams(("arbitrary",)),
    )(dy, z, ypre, xbc, xbc, dtp, prev, conv_w, conv_b, dt_bias, a_log, d_skip, norm_w)


def _head_expander():
    return (_iota2((SSD_HEADS, D_SSD), 1) // SSD_P == _iota2((SSD_HEADS, D_SSD), 0)).astype(F32)


def _expand(v, e):
    return jnp.dot(v, e, preferred_element_type=F32, precision=HI)


def _headsum(t, e):
    m = t.shape[0]
    if m < 8:
        t = jnp.broadcast_to(t[0:1], (8, t.shape[1]))
    out = lax.dot_general(t, e, (((1,), (1,)), ((), ())), preferred_element_type=F32, precision=HI)
    return out[0:m]


def _ssd_decays(dt, acs, dsk_ref, e):
    alast = acs[CHUNK - 1:CHUNK, :]
    stk = jnp.concatenate([dt, jnp.exp(acs), jnp.exp(alast - acs),
                           jnp.broadcast_to(jnp.exp(alast), (8, SSD_HEADS)),
                           jnp.broadcast_to(dsk_ref[...], (8, SSD_HEADS))], axis=0)
    ex = _expand(stk, e)
    return (ex[0:CHUNK], ex[CHUNK:2 * CHUNK], ex[2 * CHUNK:3 * CHUNK], ex[3 * CHUNK:3 * CHUNK + 1],
            ex[3 * CHUNK + 8:3 * CHUNK + 9])


def _ssd_fwd2(z, xbc, dtp, conv_w, conv_b, dt_bias, a_log, d_skip, norm_w):
    L = z.shape[0]
    nc = L // CHUNK
    half = D_SSD // SSD_GROUPS

    def body(z_ref, xbc_ref, tail_ref, dt_ref, cw_ref, cb_ref, dtb_ref, alog_ref, dsk_ref, nw_ref,
             y_ref, ypre_ref, prev_ref, state, ext, ybuf, mbuf):
        c = pl.program_id(0)

        @pl.when(c == 0)
        def _():
            state[...] = jnp.zeros_like(state)

        u, sig, xbcv, dtraw, dt, A, acs, acs_row = _ssd_chunk_pre(
            c == 0, xbc_ref, tail_ref, dt_ref, cw_ref, cb_ref, dtb_ref, alog_ref, ext)
        e = _head_expander()
        dtE, eacsE, dsdE, ealE, dskE = _ssd_decays(dt, acs, dsk_ref, e)
        xs = xbcv[:, 0:D_SSD]
        X = xs * dtE
        prev_ref[0] = state[...]
        causal = _iota2((CHUNK, CHUNK), 0) >= _iota2((CHUNK, CHUNK), 1)
        for g in range(SSD_GROUPS):
            gs = slice(half * g, half * (g + 1))
            Bg = _bf(xbcv[:, D_SSD + SSD_N * g:D_SSD + SSD_N * (g + 1)])
            Cg = _bf(xbcv[:, D_SSD + D_BC + SSD_N * g:D_SSD + D_BC + SSD_N * (g + 1)])
            cb = _dot_nt(Cg, Bg)
            for r in range(SSD_R):
                h = g * SSD_R + r
                seg = acs[:, h:h + 1] - acs_row[h:h + 1, :]
                mbuf[h] = _bf(cb * jnp.where(causal, jnp.exp(jnp.where(causal, seg, 0.0)), 0.0))
            st = state[:, gs]
            ybuf[:, gs] = _dot(Cg, _bf(st)) * eacsE[:, gs] + dskE[:, gs] * xs[:, gs]
            state[:, gs] = st * ealE[:, gs] + _dot_tn(Bg, _bf(X[:, gs] * dsdE[:, gs]))
        Xb = _bf(X)
        for h in range(SSD_HEADS):
            hs = slice(SSD_P * h, SSD_P * (h + 1))
            ybuf[:, hs] += _dot(mbuf[h], Xb[:, hs])
        y = ybuf[...]
        ypre_ref[...] = y
        zv = z_ref[...]
        yf = y * (zv * _sigmoid(zv))
        for g in range(SSD_GROUPS):
            gs = slice(half * g, half * (g + 1))
            yg = yf[:, gs]
            ms = jnp.mean(yg * yg, axis=-1, keepdims=True)
            y_ref[:, gs] = yg * lax.rsqrt(ms + RMS_EPS) * nw_ref[:, gs]

    full = lambda shape: pl.BlockSpec(shape, lambda c: (0, 0))
    return pl.pallas_call(
        body, name="ssd_fwd", grid=(nc,),
        in_specs=[
            pl.BlockSpec((CHUNK, D_SSD), lambda c: (c, 0)),
            pl.BlockSpec((CHUNK, D_XBC), lambda c: (c, 0)),
            pl.BlockSpec((8, D_XBC), lambda c: (jnp.maximum(c * (CHUNK // 8) - 1, 0), 0)),
            pl.BlockSpec((CHUNK, DT_PAD), lambda c: (c, 0)),
            full((CONV_K, D_XBC)), full((1, D_XBC)), full((1, SSD_HEADS)), full((1, SSD_HEADS)), full((1, SSD_HEADS)),
            full((1, D_SSD)),
        ],
        out_specs=[
            pl.BlockSpec((CHUNK, D_SSD), lambda c: (c, 0)),
            pl.BlockSpec((CHUNK, D_SSD), lambda c: (c, 0)),
            pl.BlockSpec((1, SSD_N, D_SSD), lambda c: (c, 0, 0)),
        ],
        out_shape=[
            jax.ShapeDtypeStruct((L, D_SSD), F32),
            jax.ShapeDtypeStruct((L, D_SSD), F32),
            jax.ShapeDtypeStruct((nc, SSD_N, D_SSD), F32),
        ],
        scratch_shapes=[
            pltpu.VMEM((SSD_N, D_SSD), F32),
            pltpu.VMEM((CHUNK + 8, D_XBC), F32),
            pltpu.VMEM((CHUNK, D_SSD), F32),
            pltpu.VMEM((SSD_HEADS, CHUNK, CHUNK), BF16),
        ],
        compiler_params=_cparams(("arbitrary",)),
    )(z, xbc, xbc, dtp, conv_w, conv_b, dt_bias, a_log, d_skip, norm_w)


def _ssd_bwd2(dy, z, ypre, xbc, dtp, prev, conv_w, conv_b, dt_bias, a_log, d_skip, norm_w):
    L = z.shape[0]
    nc = L // CHUNK
    half = D_SSD // SSD_GROUPS

    def body(dy_ref, z_ref, ypre_ref, xbc_ref, tail_ref, dt_ref, prev_ref, cw_ref, cb_ref, dtb_ref, alog_ref, dsk_ref,
             nw_ref, dz_ref, dxbc_ref, ddt_ref, gcw_ref, gcb_ref, gdtb_ref, galog_ref, gdsk_ref, gnw_ref,
             dstate, dhead, ext, ext2, dpost, yobuf, bdbuf, lmbuf, dmbuf, cbbuf):
        i = pl.program_id(0)
        c = nc - 1 - i

        @pl.when(i == 0)
        def _():
            dstate[...] = jnp.zeros_like(dstate)
            dhead[...] = jnp.zeros_like(dhead)
            gcw_ref[...] = jnp.zeros_like(gcw_ref)
            gcb_ref[...] = jnp.zeros_like(gcb_ref)
            gdtb_ref[...] = jnp.zeros_like(gdtb_ref)
            galog_ref[...] = jnp.zeros_like(galog_ref)
            gdsk_ref[...] = jnp.zeros_like(gdsk_ref)
            gnw_ref[...] = jnp.zeros_like(gnw_ref)

        u, sig, xbcv, dtraw, dt, A, acs, acs_row = _ssd_chunk_pre(
            c == 0, xbc_ref, tail_ref, dt_ref, cw_ref, cb_ref, dtb_ref, alog_ref, ext)
        e = _head_expander()
        dtE, eacsE, dsdE, ealE, dskE = _ssd_decays(dt, acs, dsk_ref, e)
        alast = acs[CHUNK - 1:CHUNK, :]
        xs = xbcv[:, 0:D_SSD]
        X = xs * dtE
        Xb = _bf(X)

        zv = z_ref[...]
        ypre = ypre_ref[...]
        dyn = dy_ref[...]
        sz = _sigmoid(zv)
        silu_z = zv * sz
        yf = ypre * silu_z
        dyf_parts = []
        for g in range(SSD_GROUPS):
            gs = slice(half * g, half * (g + 1))
            yg = yf[:, gs]
            rstd = lax.rsqrt(jnp.mean(yg * yg, axis=-1, keepdims=True) + RMS_EPS)
            dout = dyn[:, gs]
            gnw_ref[:, gs] += jnp.sum(dout * yg * rstd, axis=0, keepdims=True)
            dyhat = dout * nw_ref[:, gs]
            dyf_parts.append(rstd * (dyhat - yg * (rstd * rstd) * jnp.mean(dyhat * yg, axis=-1, keepdims=True)))
        dyf = jnp.concatenate(dyf_parts, axis=1)
        dz_ref[...] = dyf * ypre * (sz * (1.0 + zv * (1.0 - sz)))
        dyp = dyf * silu_z
        dyb = _bf(dyp)
        G = dyp * eacsE

        causal = _iota2((CHUNK, CHUNK), 0) >= _iota2((CHUNK, CHUNK), 1)
        ST = prev_ref[0]
        dST = dstate[...]
        for g in range(SSD_GROUPS):
            gs = slice(half * g, half * (g + 1))
            bs = slice(D_SSD + SSD_N * g, D_SSD + SSD_N * (g + 1))
            cs = slice(D_SSD + D_BC + SSD_N * g, D_SSD + D_BC + SSD_N * (g + 1))
            Bg = _bf(xbcv[:, bs])
            Cg = _bf(xbcv[:, cs])
            Gb = _bf(G[:, gs])
            STb = _bf(ST[:, gs])
            dSTb = _bf(dST[:, gs])
            dstate[:, gs] = dST[:, gs] * ealE[:, gs] + _dot_tn(Cg, Gb)
            yobuf[:, gs] = _dot(Cg, STb) * eacsE[:, gs]
            bdbuf[:, gs] = _dot(Bg, dSTb)
            dpost[:, cs] = _dot_nt(Gb, STb)
            dpost[:, bs] = _dot_nt(_bf(X[:, gs] * dsdE[:, gs]), dSTb)
            cbbuf[g] = _dot_nt(Cg, Bg)
            for r in range(SSD_R):
                h = g * SSD_R + r
                seg = acs[:, h:h + 1] - acs_row[h:h + 1, :]
                lmbuf[h] = jnp.where(causal, jnp.exp(jnp.where(causal, seg, 0.0)), 0.0)
        for h in range(SSD_HEADS):
            hs = slice(SSD_P * h, SSD_P * (h + 1))
            Mb = _bf(cbbuf[h // SSD_R] * lmbuf[h])
            dmbuf[h] = _dot_nt(dyb[:, hs], Xb[:, hs])
            dpost[:, hs] = _dot_tn(Mb, dyb[:, hs])
        lane16 = _iota2((1, SSD_HEADS), 1)
        sub16 = _iota2((SSD_HEADS, 1), 0)
        dacs_col = jnp.zeros((CHUNK, SSD_HEADS), F32)
        dacs_row = jnp.zeros((SSD_HEADS, CHUNK), F32)
        for g in range(SSD_GROUPS):
            bs = slice(D_SSD + SSD_N * g, D_SSD + SSD_N * (g + 1))
            cs = slice(D_SSD + D_BC + SSD_N * g, D_SSD + D_BC + SSD_N * (g + 1))
            cb = cbbuf[g]
            dcb = jnp.zeros((CHUNK, CHUNK), F32)
            for r in range(SSD_R):
                h = g * SSD_R + r
                dM = dmbuf[h]
                Lm = lmbuf[h]
                dcb = dcb + dM * Lm
                dseg = dM * (cb * Lm)
                dacs_col = dacs_col + jnp.sum(dseg, axis=-1, keepdims=True) * (lane16 == h).astype(F32)
                dacs_row = dacs_row - jnp.sum(dseg, axis=0, keepdims=True) * (sub16 == h).astype(F32)
            dcbb = _bf(dcb)
            dpost[:, bs] += _dot_tn(dcbb, _bf(xbcv[:, cs]))
            dpost[:, cs] += _dot(dcbb, _bf(xbcv[:, bs]))

        BD = bdbuf[...]
        dX = dpost[:, 0:D_SSD] + dsdE * BD
        dsd = jnp.exp(alast - acs)
        T = _headsum(X * BD, e) * dsd
        dalast = jnp.sum(T, axis=0, keepdims=True) + _headsum(
            jnp.sum(dST * ST, axis=0, keepdims=True), e) * jnp.exp(alast)
        is_last = (_iota2((CHUNK, 1), 0) == CHUNK - 1).astype(F32)
        dacs = dacs_col + _to_cols(dacs_row) + _headsum(dyp * yobuf[...], e) - T + is_last * dalast
        triu = (_iota2((CHUNK, CHUNK), 0) <= _iota2((CHUNK, CHUNK), 1)).astype(F32)
        da = jnp.dot(triu, dacs, preferred_element_type=F32, precision=HI)
        ddt_tot = _headsum(dX * xs, e) + da * A
        galog_ref[...] += jnp.sum(da * dt, axis=0, keepdims=True) * A
        ddtraw = ddt_tot * _sigmoid(dtraw)
        gdtb_ref[...] += jnp.sum(ddtraw, axis=0, keepdims=True)
        gdsk_ref[...] += _headsum(jnp.sum(dyp * xs, axis=0, keepdims=True), e)
        ddt_ref[...] = jnp.zeros_like(ddt_ref)
        ddt_ref[:, 0:SSD_HEADS] = ddtraw
        dpost[:, 0:D_SSD] = dX * dtE + dskE * dyp

        dconv = dpost[...] * (sig * (1.0 + u * (1.0 - sig)))
        gcb_ref[...] += jnp.sum(dconv, axis=0, keepdims=True)
        for k in range(CONV_K):
            gcw_ref[k:k + 1, :] += jnp.sum(dconv * ext[pl.ds(5 + k, CHUNK), :], axis=0, keepdims=True)
        ext2[0:CHUNK, :] = dconv
        ext2[CHUNK:CHUNK + 8, :] = dhead[...]
        dx = cw_ref[CONV_K - 1:CONV_K, :] * dconv
        for k in range(CONV_K - 1):
            dx = dx + cw_ref[k:k + 1, :] * ext2[pl.ds(CONV_K - 1 - k, CHUNK), :]
        dxbc_ref[...] = dx
        dhead[...] = dconv[0:8, :]

    full = lambda shape: pl.BlockSpec(shape, lambda i: (0, 0))
    rev = lambda wd: pl.BlockSpec((CHUNK, wd), lambda i: (nc - 1 - i, 0))
    return pl.pallas_call(
        body, name="ssd_bwd", grid=(nc,),
        in_specs=[
            rev(D_SSD), rev(D_SSD), rev(D_SSD), rev(D_XBC),
            pl.BlockSpec((8, D_XBC), lambda i: (jnp.maximum((nc - 1 - i) * (CHUNK // 8) - 1, 0), 0)),
            rev(DT_PAD),
            pl.BlockSpec((1, SSD_N, D_SSD), lambda i: (nc - 1 - i, 0, 0)),
            full((CONV_K, D_XBC)), full((1, D_XBC)), full((1, SSD_HEADS)), full((1, SSD_HEADS)), full((1, SSD_HEADS)),
            full((1, D_SSD)),
        ],
        out_specs=[
            rev(D_SSD), rev(D_XBC), rev(DT_PAD),
            full((CONV_K, D_XBC)), full((1, D_XBC)), full((1, SSD_HEADS)), full((1, SSD_HEADS)), full((1, SSD_HEADS)),
            full((1, D_SSD)),
        ],
        out_shape=[
            jax.ShapeDtypeStruct((L, D_SSD), F32), jax.ShapeDtypeStruct((L, D_XBC), F32),
            jax.ShapeDtypeStruct((L, DT_PAD), F32),
            jax.ShapeDtypeStruct((CONV_K, D_XBC), F32), jax.ShapeDtypeStruct((1, D_XBC), F32),
            jax.ShapeDtypeStruct((1, SSD_HEADS), F32), jax.ShapeDtypeStruct((1, SSD_HEADS), F32),
            jax.ShapeDtypeStruct((1, SSD_HEADS), F32), jax.ShapeDtypeStruct((1, D_SSD), F32),
        ],
        scratch_shapes=[
            pltpu.VMEM((SSD_N, D_SSD), F32),
            pltpu.VMEM((8, D_XBC), F32),
            pltpu.VMEM((CHUNK + 8, D_XBC), F32),
            pltpu.VMEM((CHUNK + 8, D_XBC), F32),
            pltpu.VMEM((CHUNK, D_XBC), F32),
            pltpu.VMEM((CHUNK, D_SSD), F32),
            pltpu.VMEM((CHUNK, D_SSD), F32),
            pltpu.VMEM((SSD_HEADS, CHUNK, CHUNK), F32),
            pltpu.VMEM((SSD_HEADS, CHUNK, CHUNK), F32),
            pltpu.VMEM((SSD_GROUPS, CHUNK, CHUNK), F32),
        ],
        compiler_params=_cparams(("arbitrary",)),
    )(dy, z, ypre, xbc, xbc, dtp, prev, conv_w, conv_b, dt_bias, a_log, d_skip, norm_w)


def _rope_tables(pos_ref, inv_ref):
    ang = pos_ref[...].astype(F32) * inv_ref[...]
    d = _iota2((1, 2 * ATT_HD), 1) % ATT_HD
    s = jnp.sin(ang)
    return jnp.cos(ang), jnp.where(d < ROPE_DIM // 2, -s, 0.0), jnp.where((d >= ROPE_DIM // 2) & (d < ROPE_DIM), s, 0.0)


def _rope(t, tabs):
    c, s1, s2 = tabs
    n = t.shape[1]
    rep = n // c.shape[1]
    return (t * jnp.tile(c, (1, rep)) + pltpu.roll(t, n - ROPE_DIM // 2, 1) * jnp.tile(s1, (1, rep))
            + pltpu.roll(t, ROPE_DIM // 2, 1) * jnp.tile(s2, (1, rep)))


def _rope_t(t, tabs):
    c, s1, s2 = tabs
    n = t.shape[1]
    rep = n // c.shape[1]
    return (t * jnp.tile(c, (1, rep)) + pltpu.roll(t * jnp.tile(s1, (1, rep)), ROPE_DIM // 2, 1)
            + pltpu.roll(t * jnp.tile(s2, (1, rep)), n - ROPE_DIM // 2, 1))


def _swa_mask(first):
    qi = _iota2((WINDOW, 2 * WINDOW), 0)
    si = _iota2((WINDOW, 2 * WINDOW), 1)
    band = (si > qi) & (si <= qi + WINDOW)
    return band & (jnp.logical_not(first) | (si >= WINDOW))


def _stack_heads(t, j):
    return jnp.concatenate([t[:, ATT_HD * (j * ATT_R + r):ATT_HD * (j * ATT_R + r + 1)] for r in range(ATT_R)], axis=0)


def _stack_cols(ref, j):
    cols = [jnp.broadcast_to(ref[:, j * ATT_R + r:j * ATT_R + r + 1], (WINDOW, 1)) for r in range(ATT_R)]
    return jnp.concatenate(cols, axis=0)


def _swa_fwd(q, g, kv, sinks):
    L = q.shape[0]
    nb = L // WINDOW
    scale = ATT_HD ** -0.5

    def body(q_ref, g_ref, kvc_ref, kvp_ref, sink_ref, y_ref, o_ref, lse_ref):
        n = pl.program_id(0)
        kk = _bf(jnp.concatenate([kvp_ref[:, 0:D_KV], kvc_ref[:, 0:D_KV]], axis=0))
        vv = _bf(jnp.concatenate([kvp_ref[:, D_KV:2 * D_KV], kvc_ref[:, D_KV:2 * D_KV]], axis=0))
        valid = jnp.tile(_swa_mask(n == 0), (ATT_R, 1))
        qv = q_ref[...]
        for j in range(ATT_KVH):
            js = slice(ATT_HD * j, ATT_HD * (j + 1))
            s = _dot_nt(_bf(_stack_heads(qv, j)), kk[:, js]) * scale
            s = jnp.where(valid, s, NEG_BIG)
            sink = _stack_cols(sink_ref, j)
            m = jnp.maximum(jnp.max(s, axis=-1, keepdims=True), sink)
            p = jnp.exp(s - m)
            denom = jnp.sum(p, axis=-1, keepdims=True) + jnp.exp(sink - m)
            o = _dot(_bf(p), vv[:, js]) * (1.0 / denom)
            lse = m + jnp.log(denom)
            for r in range(ATT_R):
                h = j * ATT_R + r
                rs = slice(WINDOW * r, WINDOW * (r + 1))
                o_ref[:, ATT_HD * h:ATT_HD * (h + 1)] = o[rs]
                lse_ref[:, h:h + 1] = lse[rs]
        gv = g_ref[...]
        y_ref[...] = o_ref[...] * (gv * _sigmoid(gv))

    cur = lambda wd: pl.BlockSpec((WINDOW, wd), lambda n: (n, 0))
    prv = lambda wd: pl.BlockSpec((WINDOW, wd), lambda n: (jnp.maximum(n - 1, 0), 0))
    return pl.pallas_call(
        body, name="swa_fwd", grid=(nb,),
        in_specs=[cur(D_ATT), cur(D_ATT), cur(2 * D_KV), prv(2 * D_KV), pl.BlockSpec((1, ATT_QH), lambda n: (0, 0))],
        out_specs=[cur(D_ATT), cur(D_ATT), cur(ATT_QH)],
        out_shape=[jax.ShapeDtypeStruct((L, D_ATT), F32), jax.ShapeDtypeStruct((L, D_ATT), F32),
                   jax.ShapeDtypeStruct((L, ATT_QH), F32)],
        compiler_params=_cparams(("parallel",)),
    )(q, g, kv, kv, sinks)


def _swa_bwd(dy, q, g, kv, o, lse, pos, inv, sinks):
    L = q.shape[0]
    nb = L // WINDOW
    scale = ATT_HD ** -0.5

    def body(dy_ref, q_ref, g_ref, kvc_ref, kvp_ref, o_ref, lse_ref, posc_ref, posp_ref, inv_ref, sink_ref,
             dq_ref, dg_ref, dkv_ref, dsink_ref, carry, dqbuf, dkbuf, dvbuf):
        n = pl.program_id(0)

        @pl.when(n == 0)
        def _():
            dsink_ref[...] = jnp.zeros_like(dsink_ref)

        @pl.when(n < nb)
        def _():
            tc = _rope_tables(posc_ref, inv_ref)
            tp = _rope_tables(posp_ref, inv_ref)
            kk = _bf(jnp.concatenate([kvp_ref[:, 0:D_KV], kvc_ref[:, 0:D_KV]], axis=0))
            vv = _bf(jnp.concatenate([kvp_ref[:, D_KV:2 * D_KV], kvc_ref[:, D_KV:2 * D_KV]], axis=0))
            valid = jnp.tile(_swa_mask(n == 0), (ATT_R, 1))
            qv = q_ref[...]
            gv = g_ref[...]
            sg = _sigmoid(gv)
            dyv = dy_ref[...]
            ov = o_ref[...]
            dg_ref[...] = dyv * ov * (sg * (1.0 + gv * (1.0 - sg)))
            do = dyv * (gv * sg)
            delta_all = do * ov
            lane16 = _iota2((1, ATT_QH), 1)
            dsink = jnp.zeros((1, ATT_QH), F32)
            for j in range(ATT_KVH):
                js = slice(ATT_HD * j, ATT_HD * (j + 1))
                kj = kk[:, js]
                vj = vv[:, js]
                qs = _bf(_stack_heads(qv, j))
                dos = _bf(_stack_heads(do, j))
                delta = jnp.sum(_stack_heads(delta_all, j), axis=-1, keepdims=True)
                lse = _stack_cols(lse_ref, j)
                s = _dot_nt(qs, kj) * scale
                p = jnp.exp(jnp.where(valid, s, NEG_BIG) - lse)
                dS = _bf(p * (_dot_nt(dos, vj) - delta))
                dqs = _dot(dS, kj) * scale
                dkbuf[:, js] = _dot_tn(dS, qs) * scale
                dvbuf[:, js] = _dot_tn(_bf(p), dos)
                sd = jnp.exp(_stack_cols(sink_ref, j) - lse) * delta
                for r in range(ATT_R):
                    h = j * ATT_R + r
                    rs = slice(WINDOW * r, WINDOW * (r + 1))
                    dqbuf[:, ATT_HD * h:ATT_HD * (h + 1)] = dqs[rs]
                    dsink = dsink - jnp.sum(sd[rs], axis=0, keepdims=True) * (lane16 == h).astype(F32)
            dsink_ref[...] += dsink
            dq_ref[...] = _rope_t(dqbuf[...], tc)
            dkp = _rope_t(dkbuf[0:WINDOW, :], tp)
            dkc = _rope_t(dkbuf[WINDOW:2 * WINDOW, :], tc)

            @pl.when(n > 0)
            def _():
                dkv_ref[:, 0:D_KV] = carry[:, 0:D_KV] + dkp
                dkv_ref[:, D_KV:2 * D_KV] = carry[:, D_KV:2 * D_KV] + dvbuf[0:WINDOW, :]

            carry[:, 0:D_KV] = dkc
            carry[:, D_KV:2 * D_KV] = dvbuf[WINDOW:2 * WINDOW, :]

        @pl.when(n == nb)
        def _():
            dkv_ref[...] = carry[...]

    last = nb - 1
    cur = lambda wd: pl.BlockSpec((WINDOW, wd), lambda n: (jnp.minimum(n, last), 0))
    prv = lambda wd: pl.BlockSpec((WINDOW, wd), lambda n: (jnp.maximum(jnp.minimum(n, last) - 1, 0), 0))
    return pl.pallas_call(
        body, name="swa_bwd", grid=(nb + 1,),
        in_specs=[cur(D_ATT), cur(D_ATT), cur(D_ATT), cur(2 * D_KV), prv(2 * D_KV), cur(D_ATT), cur(ATT_QH), cur(1), prv(1),
                  pl.BlockSpec((1, 2 * ATT_HD), lambda n: (0, 0)), pl.BlockSpec((1, ATT_QH), lambda n: (0, 0))],
        out_specs=[cur(D_ATT), cur(D_ATT),
                   pl.BlockSpec((WINDOW, 2 * D_KV), lambda n: (jnp.maximum(n - 1, 0), 0)),
                   pl.BlockSpec((1, ATT_QH), lambda n: (0, 0))],
        out_shape=[jax.ShapeDtypeStruct((L, D_ATT), F32), jax.ShapeDtypeStruct((L, D_ATT), F32),
                   jax.ShapeDtypeStruct((L, 2 * D_KV), F32), jax.ShapeDtypeStruct((1, ATT_QH), F32)],
        scratch_shapes=[pltpu.VMEM((WINDOW, 2 * D_KV), F32), pltpu.VMEM((WINDOW, D_ATT), F32),
                        pltpu.VMEM((2 * WINDOW, D_KV), F32), pltpu.VMEM((2 * WINDOW, D_KV), F32)],
        compiler_params=_cparams(("arbitrary",)),
    )(dy, q, g, kv, kv, o, lse, pos, pos, inv, sinks)


def _out_ln_loss(y_ssd, y_att, x, target, w_out, ln_g, ln_b):
    L = x.shape[0]
    tm = 256
    inv_d = 1.0 / D_MODEL

    def body(ys_ref, ya_ref, x_ref, t_ref, w_ref, g_ref, b_ref, dr_ref, dys_ref, dya_ref, loss_ref, gg_ref, gb_ref):
        i = pl.program_id(0)

        @pl.when(i == 0)
        def _():
            loss_ref[...] = jnp.zeros_like(loss_ref)
            gg_ref[...] = jnp.zeros_like(gg_ref)
            gb_ref[...] = jnp.zeros_like(gb_ref)

        h = _dot(_bf(ys_ref[...]), w_ref[0:D_SSD, :]) + _dot(_bf(ya_ref[...]), w_ref[D_SSD:D_MIX, :])
        r = ALPHA * x_ref[...] + h
        mu = jnp.mean(r, axis=-1, keepdims=True)
        xc = r - mu
        rstd = lax.rsqrt(jnp.mean(xc * xc, axis=-1, keepdims=True) + LN_EPS)
        xhat = xc * rstd
        gam = g_ref[...]
        diff = xhat * gam + b_ref[...] - t_ref[...]
        part = jnp.sum(jnp.sum(diff * diff, axis=-1, keepdims=True), axis=0, keepdims=True)
        loss_ref[...] += (0.5 * inv_d) * part
        dout = diff * inv_d
        gg_ref[...] += jnp.sum(dout * xhat, axis=0, keepdims=True)
        gb_ref[...] += jnp.sum(dout, axis=0, keepdims=True)
        dxh = dout * gam
        dr = rstd * (dxh - jnp.mean(dxh, axis=-1, keepdims=True) - xhat * jnp.mean(dxh * xhat, axis=-1, keepdims=True))
        dr_ref[...] = dr
        drb = _bf(dr)
        dys_ref[...] = _dot_nt(drb, w_ref[0:D_SSD, :])
        dya_ref[...] = _dot_nt(drb, w_ref[D_SSD:D_MIX, :])

    row = pl.BlockSpec((tm, D_MODEL), lambda i: (i, 0))
    vec = pl.BlockSpec((1, D_MODEL), lambda i: (0, 0))
    return pl.pallas_call(
        body, name="out_ln_loss", grid=(L // tm,),
        in_specs=[row, row, row, row, pl.BlockSpec((D_MIX, D_MODEL), lambda i: (0, 0)), vec, vec],
        out_specs=[row, row, row, pl.BlockSpec((1, 128), lambda i: (0, 0)), vec, vec],
        out_shape=[jax.ShapeDtypeStruct((L, D_MODEL), F32)] * 3 + [jax.ShapeDtypeStruct((1, 128), F32)]
        + [jax.ShapeDtypeStruct((1, D_MODEL), F32)] * 2,
        compiler_params=_cparams(("arbitrary",)),
    )(y_ssd, y_att, x, target, w_out, ln_g, ln_b)


def _local_step(x, pos, target, w, w_out, conv_w, conv_b, dt_bias, a_log, d_skip, norm_w, sinks, ln_g, ln_b):
    inv8 = ROPE_THETA ** (-jnp.arange(0, ROPE_DIM, 2, dtype=F32) / ROPE_DIM)
    inv = jnp.tile(jnp.concatenate([inv8, inv8, jnp.zeros((ATT_HD - ROPE_DIM,), F32)]), 2).reshape(1, 2 * ATT_HD)

    z, g, q, xbc, kv, dtp, xb = _in_proj(x, w, pos, inv)
    y_ssd, y_pre, prev = _ssd_fwd2(z, xbc, dtp, conv_w, conv_b, dt_bias, a_log, d_skip, norm_w)
    y_att, o, lse = _swa_fwd(q, g, kv, sinks)
    dr, dy_ssd, dy_att, loss, g_ln_g, g_ln_b = _out_ln_loss(y_ssd, y_att, x, target, w_out, ln_g, ln_b)
    gw_out_ssd = _matmul_tn(y_ssd, dr, 1024, "gw_out_ssd")
    gw_out_att = _matmul_tn(y_att, dr, 1024, "gw_out_att")
    dq, dg, dkv, g_sinks = _swa_bwd(dy_att, q, g, kv, o, lse, pos, inv, sinks)
    dz, dxbc, ddt, g_conv_w, g_conv_b, g_dt_bias, g_a_log, g_d_skip, g_norm_w = _ssd_bwd2(
        dy_ssd, z, y_pre, xbc, dtp, prev, conv_w, conv_b, dt_bias, a_log, d_skip, norm_w)
    grad_x = _grad_x(dr, dz, dg, dq, dxbc, dkv, ddt, w)
    gw_z = _matmul_tn(dz, xb, 1024, "gw_z")
    gw_g = _matmul_tn(dg, xb, 1024, "gw_g")
    gw_q = _matmul_tn(dq, xb, 1024, "gw_q")
    gw_xbc = _matmul_tn(dxbc, xb, 1024, "gw_xbc")
    gw_kv = _matmul_tn(dkv, xb, 1024, "gw_kv")
    gw_dt = _matmul_tn(ddt, xb, 1024, "gw_dt")
    gw_in = jnp.concatenate([gw_z, gw_xbc, gw_dt[0:SSD_HEADS], gw_q, gw_kv, gw_g], axis=0)
    gw_out = jnp.concatenate([gw_out_ssd, gw_out_att], axis=0)
    small = dict(conv_w=g_conv_w, conv_b=g_conv_b, dt_bias=g_dt_bias, a_log=g_a_log, d_skip=g_d_skip,
                 ssd_norm_w=g_norm_w, attn_sinks=g_sinks, ln_g=g_ln_g, ln_b=g_ln_b)
    return loss, grad_x, gw_in, gw_out, small


def _mesh_pos():
    return lax.axis_index("x"), lax.axis_index("y"), lax.axis_index("c")


def _gather_weights(w_in_s, w_out_s, conv_w_s):
    def body(win_ref, wout_ref, cw_ref, owin_ref, owout_ref, ocw_ref, send_sems, recv_sems, small_send, small_recv,
             local_sems):
        x, y, c = _mesh_pos()
        me = 2 * x + y
        sibling = (x, y, 1 - c)
        chips = [(1 - x, y), (x, 1 - y), (1 - x, 1 - y)]
        locals_ = [pltpu.make_async_copy(cw_ref, ocw_ref.at[me], local_sems.at[0])]
        for cp in locals_:
            cp.start()
        started = []
        for t, (src, dst) in enumerate(((win_ref, owin_ref), (wout_ref, owout_ref))):
            hr = src.shape[0] // 2

            def half(ref, hc, hr=hr):
                return ref.at[pl.ds(hc * hr, hr), :]

            for j, (px, py) in enumerate(chips):
                cp = pltpu.make_async_remote_copy(
                    src_ref=half(src, c), dst_ref=half(dst.at[me], c), send_sem=send_sems.at[t, j],
                    recv_sem=recv_sems.at[t, j], device_id=(px, py, c), device_id_type=MESH)
                cp.start()
                started.append(cp)
        for j, (px, py) in enumerate(chips):
            cp = pltpu.make_async_remote_copy(
                src_ref=cw_ref, dst_ref=ocw_ref.at[me], send_sem=small_send.at[j], recv_sem=small_recv.at[j],
                device_id=(px, py, c), device_id_type=MESH)
            cp.start()
            started.append(cp)
        for t, (src, dst) in enumerate(((win_ref, owin_ref), (wout_ref, owout_ref))):
            hr = src.shape[0] // 2
            for j, (px, py) in enumerate(chips):
                src_chip = 2 * px + py
                blk = dst.at[src_chip].at[pl.ds(c * hr, hr), :]
                pltpu.make_async_remote_copy(
                    src_ref=blk, dst_ref=blk, send_sem=send_sems.at[t, j], recv_sem=recv_sems.at[t, j],
                    device_id=(px, py, c), device_id_type=MESH).wait_recv()
                cp = pltpu.make_async_remote_copy(
                    src_ref=blk, dst_ref=blk, send_sem=send_sems.at[t, 3 + j], recv_sem=recv_sems.at[t, 3 + j],
                    device_id=sibling, device_id_type=MESH)
                cp.start()
                started.append(cp)
        for t, (src, dst) in enumerate(((win_ref, owin_ref), (wout_ref, owout_ref))):
            hr = src.shape[0] // 2
            for j, (px, py) in enumerate(chips):
                src_chip = 2 * px + py
                blk = dst.at[src_chip].at[pl.ds((1 - c) * hr, hr), :]
                pltpu.make_async_remote_copy(
                    src_ref=blk, dst_ref=blk, send_sem=send_sems.at[t, 3 + j], recv_sem=recv_sems.at[t, 3 + j],
                    device_id=sibling, device_id_type=MESH).wait_recv()
        for j in range(3):
            pltpu.make_async_remote_copy(
                src_ref=cw_ref, dst_ref=ocw_ref.at[me], send_sem=small_send.at[j], recv_sem=small_recv.at[j],
                device_id=sibling, device_id_type=MESH).wait_recv()
        for cp in started:
            cp.wait_send()
        for cp in locals_:
            cp.wait()

    any_spec = pl.BlockSpec(memory_space=pl.ANY)
    return pl.pallas_call(
        body, name="gather_weights",
        in_specs=[any_spec] * 3, out_specs=[any_spec] * 3,
        out_shape=[jax.ShapeDtypeStruct((N_CHIPS,) + a.shape, a.dtype) for a in (w_in_s, w_out_s, conv_w_s)],
        scratch_shapes=[pltpu.SemaphoreType.DMA((2, 6)), pltpu.SemaphoreType.DMA((2, 6)),
                        pltpu.SemaphoreType.DMA((3,)), pltpu.SemaphoreType.DMA((3,)), pltpu.SemaphoreType.DMA((3,))],
    )(w_in_s, w_out_s, conv_w_s)


def _pair_exchange(gw_in, gw_out, small):
    k_small = small.shape[1]

    def body(gin_ref, gout_ref, sm_ref, rin_ref, rout_ref, slots_ref, send_sems, recv_sems, small_send, small_recv,
             local_sem):
        x, y, c = _mesh_pos()
        me = 4 * x + 2 * y + c
        sibling = (x, y, 1 - c)
        mine = pltpu.make_async_copy(sm_ref, slots_ref.at[me], local_sem)
        mine.start()
        started = []
        for t, (src, dst) in enumerate(((gin_ref, rin_ref), (gout_ref, rout_ref))):
            hr = src.shape[1] // 2
            for j in range(N_CHIPS):
                cp = pltpu.make_async_remote_copy(
                    src_ref=src.at[j, pl.ds((1 - c) * hr, hr), :], dst_ref=dst.at[j], send_sem=send_sems.at[t, j],
                    recv_sem=recv_sems.at[t, j], device_id=sibling, device_id_type=MESH)
                cp.start()
                started.append(cp)
        for k in range(1, 8):
            peer = (x ^ ((k >> 2) & 1), y ^ ((k >> 1) & 1), c ^ (k & 1))
            cp = pltpu.make_async_remote_copy(
                src_ref=sm_ref, dst_ref=slots_ref.at[me], send_sem=small_send.at[k - 1], recv_sem=small_recv.at[k - 1],
                device_id=peer, device_id_type=MESH)
            cp.start()
            started.append(cp)
        for t, (src, dst) in enumerate(((gin_ref, rin_ref), (gout_ref, rout_ref))):
            for j in range(N_CHIPS):
                pltpu.make_async_remote_copy(
                    src_ref=dst.at[j], dst_ref=dst.at[j], send_sem=send_sems.at[t, j], recv_sem=recv_sems.at[t, j],
                    device_id=sibling, device_id_type=MESH).wait_recv()
        for k in range(1, 8):
            pltpu.make_async_remote_copy(
                src_ref=sm_ref, dst_ref=slots_ref.at[me], send_sem=small_send.at[k - 1], recv_sem=small_recv.at[k - 1],
                device_id=sibling, device_id_type=MESH).wait_recv()
        for cp in started:
            cp.wait_send()
        mine.wait()

    any_spec = pl.BlockSpec(memory_space=pl.ANY)
    half_in = jax.ShapeDtypeStruct((N_CHIPS, gw_in.shape[1] // 2, D_MODEL), F32)
    half_out = jax.ShapeDtypeStruct((N_CHIPS, gw_out.shape[1] // 2, D_MODEL), F32)
    return pl.pallas_call(
        body, name="pair_exchange",
        in_specs=[any_spec] * 3, out_specs=[any_spec] * 3,
        out_shape=[half_in, half_out, jax.ShapeDtypeStruct((8, 8, k_small), F32)],
        scratch_shapes=[pltpu.SemaphoreType.DMA((2, N_CHIPS)), pltpu.SemaphoreType.DMA((2, N_CHIPS)),
                        pltpu.SemaphoreType.DMA((7,)), pltpu.SemaphoreType.DMA((7,)), pltpu.SemaphoreType.DMA],
    )(gw_in, gw_out, small)


def _chip_exchange(s_in, s_out):
    def body(sin_ref, sout_ref, rin_ref, rout_ref, send_sems, recv_sems):
        x, y, c = _mesh_pos()
        me = 2 * x + y
        chips = [(1 - x, y), (x, 1 - y), (1 - x, 1 - y)]
        started = []
        for t, (src, dst) in enumerate(((sin_ref, rin_ref), (sout_ref, rout_ref))):
            for j, (px, py) in enumerate(chips):
                cp = pltpu.make_async_remote_copy(
                    src_ref=src.at[2 * px + py], dst_ref=dst.at[me], send_sem=send_sems.at[t, j],
                    recv_sem=recv_sems.at[t, j], device_id=(px, py, c), device_id_type=MESH)
                cp.start()
                started.append(cp)
        for t, (src, dst) in enumerate(((sin_ref, rin_ref), (sout_ref, rout_ref))):
            for j, (px, py) in enumerate(chips):
                blk = dst.at[2 * px + py]
                pltpu.make_async_remote_copy(
                    src_ref=blk, dst_ref=blk, send_sem=send_sems.at[t, j], recv_sem=recv_sems.at[t, j],
                    device_id=(px, py, c), device_id_type=MESH).wait_recv()
        for cp in started:
            cp.wait_send()

    any_spec = pl.BlockSpec(memory_space=pl.ANY)
    return pl.pallas_call(
        body, name="chip_exchange",
        in_specs=[any_spec] * 2, out_specs=[any_spec] * 2,
        out_shape=[jax.ShapeDtypeStruct(s_in.shape, s_in.dtype), jax.ShapeDtypeStruct(s_out.shape, s_out.dtype)],
        scratch_shapes=[pltpu.SemaphoreType.DMA((2, 3)), pltpu.SemaphoreType.DMA((2, 3))],
    )(s_in, s_out)


def _pair_share(h_in, h_out):
    def body(hin_ref, hout_ref, rin_ref, rout_ref, send_sems, recv_sems):
        x, y, c = _mesh_pos()
        sibling = (x, y, 1 - c)
        started = []
        for t, (src, dst) in enumerate(((hin_ref, rin_ref), (hout_ref, rout_ref))):
            cp = pltpu.make_async_remote_copy(
                src_ref=src, dst_ref=dst, send_sem=send_sems.at[t], recv_sem=recv_sems.at[t],
                device_id=sibling, device_id_type=MESH)
            cp.start()
            started.append(cp)
        for cp in started:
            cp.wait()

    any_spec = pl.BlockSpec(memory_space=pl.ANY)
    return pl.pallas_call(
        body, name="pair_share",
        in_specs=[any_spec] * 2, out_specs=[any_spec] * 2,
        out_shape=[jax.ShapeDtypeStruct(h_in.shape, F32), jax.ShapeDtypeStruct(h_out.shape, F32)],
        scratch_shapes=[pltpu.SemaphoreType.DMA((2,)), pltpu.SemaphoreType.DMA((2,))],
    )(h_in, h_out)


def _pair_add(g, recv, core, name):
    _, rows, C = recv.shape
    tc = 256

    def body(core_ref, g_ref, r_ref, o_ref):
        o_ref[...] = _bf(g_ref[...] + r_ref[...])

    spec = pl.BlockSpec((1, rows, tc), lambda j, i, core: (j, 0, i))
    return pl.pallas_call(
        body, name=name,
        grid_spec=pltpu.PrefetchScalarGridSpec(
            num_scalar_prefetch=1, grid=(N_CHIPS, C // tc),
            in_specs=[pl.BlockSpec((1, rows, tc), lambda j, i, core: (j, core[0], i)), spec], out_specs=spec),
        out_shape=jax.ShapeDtypeStruct((N_CHIPS, rows, C), BF16),
        compiler_params=_cparams(("parallel", "parallel")),
    )(core, g, recv)


def _chip_add(own, parts, chip, name):
    _, rows, C = parts.shape
    tc = 256

    def body(chip_ref, own_ref, r0, r1, r2, r3, o_ref):
        acc = None
        for j, r in enumerate((r0, r1, r2, r3)):
            term = jnp.where(chip_ref[0] == j, own_ref[0], r[0]).astype(F32)
            acc = term if acc is None else acc + term
        o_ref[...] = acc

    def slab(j):
        return pl.BlockSpec((1, rows, tc), lambda i, chip: (jnp.where(chip[0] == j, (j + 1) % N_CHIPS, j), 0, i))

    return pl.pallas_call(
        body, name=name,
        grid_spec=pltpu.PrefetchScalarGridSpec(
            num_scalar_prefetch=1, grid=(C // tc,),
            in_specs=[pl.BlockSpec((1, rows, tc), lambda i, chip: (chip[0], 0, i))] + [slab(j) for j in range(N_CHIPS)],
            out_specs=pl.BlockSpec((rows, tc), lambda i, chip: (0, i))),
        out_shape=jax.ShapeDtypeStruct((rows, C), F32),
        compiler_params=_cparams(("parallel",)),
    )(chip, own, parts, parts, parts, parts)


def _adamw_math(w, g, m, v):
    m = ADAM_B1 * m + (1.0 - ADAM_B1) * g
    v = ADAM_B2 * v + (1.0 - ADAM_B2) * (g * g)
    m_hat = m / (1.0 - ADAM_B1 ** ADAM_STEP)
    v_hat = v / (1.0 - ADAM_B2 ** ADAM_STEP)
    delta = -ADAM_LR * (m_hat / (jnp.sqrt(v_hat) + ADAM_EPS) + ADAM_WD * w)
    return delta, m, v


def _adamw_pair(w, g_own, g_sib, m, v, core, name):
    unit = w.ndim == 3
    R, C = w.shape[0], w.shape[-1]
    rows = g_own.shape[0]
    tc = 128

    def body(core_ref, w_ref, go_ref, gs_ref, m_ref, v_ref, d_ref, nm_ref, nv_ref, g_ref):
        first = core_ref[0] == 0
        own, sib = go_ref[...], gs_ref[...]
        g = jnp.concatenate([jnp.where(first, own, sib), jnp.where(first, sib, own)], axis=0)[0:R, :]
        idx = (slice(None), 0, slice(None)) if unit else (slice(None), slice(None))
        d, nm, nv = _adamw_math(w_ref[idx], g, m_ref[idx], v_ref[idx])
        d_ref[idx] = d
        nm_ref[idx] = nm
        nv_ref[idx] = nv
        g_ref[idx] = g

    if unit:
        spec = pl.BlockSpec((R, 1, tc), lambda i, core: (0, 0, i))
    else:
        spec = pl.BlockSpec((R, tc), lambda i, core: (0, i))
    gspec = pl.BlockSpec((rows, tc), lambda i, core: (0, i))
    return pl.pallas_call(
        body, name=name,
        grid_spec=pltpu.PrefetchScalarGridSpec(
            num_scalar_prefetch=1, grid=(C // tc,),
            in_specs=[spec, gspec, gspec, spec, spec], out_specs=[spec] * 4),
        out_shape=[jax.ShapeDtypeStruct(w.shape, F32)] * 4,
        compiler_params=_cparams(("parallel",)),
    )(core, w, g_own, g_sib, m, v)


SMALL_NAMES = ("conv_b", "ssd_norm_w", "ln_g", "ln_b", "dt_bias", "a_log", "d_skip", "attn_sinks")
SMALL_SIZES = (D_XBC, D_SSD, D_MODEL, D_MODEL, SSD_HEADS, SSD_HEADS, SSD_HEADS, ATT_QH)
SMALL_OFFS = tuple(D_XBC + sum(-(-n // 128) * 128 for n in SMALL_SIZES[:k]) for k in range(len(SMALL_SIZES)))
LOSS_OFF = D_XBC + sum(-(-n // 128) * 128 for n in SMALL_SIZES)
K_SMALL = LOSS_OFF + 128


def _pack_small(g_conv_w, vecs, loss):
    def body(cw_ref, *refs):
        o_ref = refs[-1]
        o_ref[...] = jnp.zeros_like(o_ref)
        o_ref[0:CONV_K, 0:D_XBC] = cw_ref[...]
        for v_ref, off, n in zip(refs[:-2], SMALL_OFFS, SMALL_SIZES):
            o_ref[0:1, off:off + n] = v_ref[...]
        o_ref[0:1, LOSS_OFF:LOSS_OFF + 128] = refs[-2][...]

    return pl.pallas_call(
        body, name="pack_small", out_shape=jax.ShapeDtypeStruct((8, K_SMALL), F32), compiler_params=_cparams(),
    )(g_conv_w, *vecs, loss)


def _adamw_small(slots, chip, conv_w, m_conv_w, v_conv_w, params, moms, vars_):
    n_vec = len(SMALL_NAMES)

    def body(chip_ref, s_ref, *refs):
        ins = refs[:3 * (n_vec + 1)]
        outs = refs[3 * (n_vec + 1):-1]
        tot_ref = refs[-1]
        tot = s_ref[0]
        for d in range(1, 8):
            tot = tot + s_ref[d]
        outs[0][...] = tot[0:1, LOSS_OFF:LOSS_OFF + 1]
        off = pl.multiple_of(chip_ref[0] * CONV_COLS, 128)
        tot_ref[...] = tot
        grads = [tot_ref[0:CONV_K, pl.ds(off, CONV_COLS)]]
        grads += [tot[0:1, o:o + n] for o, n in zip(SMALL_OFFS, SMALL_SIZES)]
        for k, g in enumerate(grads):
            w_ref, m_ref, v_ref = ins[3 * k:3 * k + 3]
            full = (0,) if k == 0 else (Ellipsis,)
            d, nm, nv = _adamw_math(w_ref[full], g, m_ref[full], v_ref[full])
            for o_ref, val in zip(outs[1 + 4 * k:5 + 4 * k], (g, d, nm, nv)):
                o_ref[full] = val

    args = [conv_w, m_conv_w, v_conv_w]
    for w, m, v in zip(params, moms, vars_):
        args += [w, m, v]
    shapes = [jax.ShapeDtypeStruct((1, 1), F32)] + [jax.ShapeDtypeStruct(conv_w.shape, F32)] * 4
    for w in params:
        shapes += [jax.ShapeDtypeStruct(w.shape, F32)] * 4
    vmem = pl.BlockSpec(memory_space=pltpu.VMEM)
    return pl.pallas_call(
        body, name="adamw_small",
        grid_spec=pltpu.PrefetchScalarGridSpec(
            num_scalar_prefetch=1, grid=(1,),
            in_specs=[pl.BlockSpec(slots.shape, lambda i, chip: (0, 0, 0))] + [vmem] * len(args),
            out_specs=[vmem] * len(shapes), scratch_shapes=[pltpu.VMEM((8, K_SMALL), F32)]),
        out_shape=shapes, compiler_params=_cparams(),
    )(chip, slots, *args)


def kernel(x, positions, w_in, conv_w, conv_b, dt_bias, a_log, d_skip, ssd_norm_w, attn_sinks, w_out, ln_g, ln_b, loss_target, m_w_in, m_conv_w, m_conv_b, m_dt_bias, m_a_log, m_d_skip, m_ssd_norm_w, m_attn_sinks, m_w_out, m_ln_g, m_ln_b, v_w_in, v_conv_w, v_conv_b, v_dt_bias, v_a_log, v_d_skip, v_ssd_norm_w, v_attn_sinks, v_w_out, v_ln_g, v_ln_b):
    mx, my, mc = _mesh_pos()
    chip = 2 * mx + my
    L = x.shape[1]

    conv_w_s8 = jnp.pad(conv_w[0], ((0, 8 - CONV_K), (0, 0)))
    pad_rows = ((0, SLAB_ROWS - W_IN_COLS), (0, 0))
    w_in_t = w_in[0].T
    w_in_b, w_out_b = jnp.pad(_bf(w_in_t), pad_rows), _bf(w_out[0])
    ag_in, ag_out, ag_cw = _gather_weights(w_in_b, w_out_b, conv_w_s8)
    ag_in = jnp.where((jnp.arange(N_CHIPS) == chip)[:, None, None], w_in_b[None], ag_in)
    ag_out = jnp.where((jnp.arange(N_CHIPS) == chip)[:, None, None], w_out_b[None], ag_out)
    w_full = jnp.concatenate([ag_in[j, 0:W_IN_COLS] for j in range(N_CHIPS)], axis=0)
    w = jnp.concatenate([
        w_full[O_Z:O_Z + D_SSD], w_full[O_G:O_G + D_ATT], w_full[O_Q:O_Q + D_ATT],
        w_full[O_XBC:O_XBC + D_XBC], w_full[O_K:O_K + 2 * D_KV], w_full[O_DT:O_DT + SSD_HEADS],
        jnp.zeros((DT_PAD - SSD_HEADS, D_MODEL), BF16)], axis=0)
    w_out_full = ag_out.reshape(D_MIX, D_MODEL)
    conv_w_full = jnp.concatenate([ag_cw[j, 0:CONV_K] for j in range(N_CHIPS)], axis=1)

    loss_part, grad_x, gw_in, gw_out, small = _local_step(
        x[0], positions[0].reshape(L, 1), loss_target[0], w, w_out_full, conv_w_full, conv_b, dt_bias, a_log, d_skip,
        ssd_norm_w, attn_sinks, ln_g, ln_b)

    packed = _pack_small(small["conv_w"], [small[n] for n in SMALL_NAMES], loss_part)

    gw_in_slabs = jnp.stack([jnp.pad(gw_in[W_IN_COLS * j:W_IN_COLS * (j + 1)], pad_rows) for j in range(N_CHIPS)])
    gw_out_slabs = gw_out.reshape(N_CHIPS, W_OUT_ROWS, D_MODEL)
    core_id = mc.reshape(1).astype(jnp.int32)
    chip_id = chip.reshape(1).astype(jnp.int32)
    recv_in, recv_out, slots = _pair_exchange(gw_in_slabs, gw_out_slabs, packed)
    s_in = _pair_add(gw_in_slabs, recv_in, core_id, "pair_add_in")
    s_out = _pair_add(gw_out_slabs, recv_out, core_id, "pair_add_out")
    r_in, r_out = _chip_exchange(s_in, s_out)
    h_in = _chip_add(s_in, r_in, chip_id, "chip_add_in")
    h_out = _chip_add(s_out, r_out, chip_id, "chip_add_out")
    sib_in, sib_out = _pair_share(h_in, h_out)

    to_rows = lambda a: jnp.transpose(a, (2, 0, 1))
    in_t = _adamw_pair(to_rows(w_in), h_in, sib_in, to_rows(m_w_in), to_rows(v_w_in), core_id, "adamw_w_in")
    d_w_in, nm_w_in, nv_w_in, g_w_in = [jnp.transpose(a, (1, 2, 0)) for a in in_t]
    out_t = _adamw_pair(w_out[0], h_out, sib_out, m_w_out[0], v_w_out[0], core_id, "adamw_w_out")
    d_w_out, nm_w_out, nv_w_out, g_w_out = [a[None] for a in out_t]

    params = dict(conv_b=conv_b, ssd_norm_w=ssd_norm_w, ln_g=ln_g, ln_b=ln_b, dt_bias=dt_bias, a_log=a_log,
                  d_skip=d_skip, attn_sinks=attn_sinks)
    moms = dict(conv_b=m_conv_b, ssd_norm_w=m_ssd_norm_w, ln_g=m_ln_g, ln_b=m_ln_b, dt_bias=m_dt_bias, a_log=m_a_log,
                d_skip=m_d_skip, attn_sinks=m_attn_sinks)
    vars_ = dict(conv_b=v_conv_b, ssd_norm_w=v_ssd_norm_w, ln_g=v_ln_g, ln_b=v_ln_b, dt_bias=v_dt_bias, a_log=v_a_log,
                 d_skip=v_d_skip, attn_sinks=v_attn_sinks)
    res = _adamw_small(slots, chip_id, conv_w, m_conv_w, v_conv_w, [params[n] for n in SMALL_NAMES],
                       [moms[n] for n in SMALL_NAMES], [vars_[n] for n in SMALL_NAMES])
    loss = res[0][0, 0]
    grads, delta, new_m, new_v = {}, {}, {}, {}
    for k, n in enumerate(("conv_w",) + SMALL_NAMES):
        grads[n], delta[n], new_m[n], new_v[n] = res[1 + 4 * k:5 + 4 * k]
    for dd, a_in, a_out in ((grads, g_w_in, g_w_out), (delta, d_w_in, d_w_out), (new_m, nm_w_in, nm_w_out),
                            (new_v, nv_w_in, nv_w_out)):
        dd["w_in"] = a_in
        dd["w_out"] = a_out
    order = ("w_in", "conv_w", "conv_b", "dt_bias", "a_log", "d_skip", "ssd_norm_w", "attn_sinks", "w_out", "ln_g", "ln_b")
    return (loss, grad_x[None], *[grads[n] for n in order], *[delta[n] for n in order], *[new_m[n] for n in order],
            *[new_v[n] for n in order])
```

```python
import functools

import numpy as np
import jax
import jax.numpy as jnp
from jax import lax
from jax.experimental import pallas as pl
from jax.experimental.pallas import tpu as pltpu

F32 = jnp.float32
BF16 = jnp.bfloat16
MESH = pl.DeviceIdType.MESH

D_MODEL = 1024
D_SSD = 1024
D_ATT = 1024
D_MIX = 2048
SSD_HEADS = 16
SSD_P = 64
SSD_GROUPS = 2
SSD_R = 8
SSD_N = 128
D_BC = 256
D_XBC = 1536
CONV_K = 4
CHUNK = 128
ATT_HD = 64
ATT_QH = 16
ATT_KVH = 4
ATT_R = 4
D_KV = 256
WINDOW = 128
ROPE_THETA = 500000.0
ROPE_DIM = 16
ALPHA = 2.0 ** 0.25
LN_EPS = 1e-5
RMS_EPS = 1e-5
D_IN_PROJ = 5136
O_Z, O_XBC, O_DT, O_Q, O_K, O_V, O_G = 0, 1024, 2560, 2576, 3600, 3856, 4112
P_Z, P_G, P_Q, P_XBC, P_KV, P_DT, P_END = 0, 1024, 2048, 3072, 4608, 5120, 5248
DT_PAD = 128
N_CHIPS = 4
W_IN_COLS = D_IN_PROJ // N_CHIPS
SLAB_ROWS = 1312
W_OUT_ROWS = D_MIX // N_CHIPS
CONV_COLS = D_XBC // N_CHIPS

ADAM_LR = 0.001
ADAM_B1 = 0.9
ADAM_B2 = 0.999
ADAM_EPS = 1e-08
ADAM_WD = 0.01
ADAM_STEP = 10

VMEM_LIMIT = 56 * 1024 * 1024
ROW_TILE = 512
NEG_BIG = -1e30
HI = lax.Precision.HIGHEST


def _cparams(sem=None, **kw):
    if sem is not None:
        kw["dimension_semantics"] = sem
    return pltpu.CompilerParams(vmem_limit_bytes=VMEM_LIMIT, **kw)


def _dot(a, b):
    return jnp.dot(a, b, preferred_element_type=F32)


def _dot_nt(a, b):
    return lax.dot_general(a, b, (((1,), (1,)), ((), ())), preferred_element_type=F32)


def _dot_tn(a, b):
    return lax.dot_general(a, b, (((0,), (0,)), ((), ())), preferred_element_type=F32)


def _bf(a):
    return a.astype(BF16)


def _iota2(shape, dim):
    return lax.broadcasted_iota(jnp.int32, shape, dim)


def _to_rows(col):
    k = col.shape[1]
    eye = (_iota2((k, k), 0) == _iota2((k, k), 1)).astype(F32)
    return lax.dot_general(eye, col, (((1,), (1,)), ((), ())), preferred_element_type=F32, precision=HI)


def _to_cols(row):
    n = row.shape[1]
    eye = (_iota2((n, n), 0) == _iota2((n, n), 1)).astype(F32)
    return lax.dot_general(eye, row, (((1,), (1,)), ((), ())), preferred_element_type=F32, precision=HI)


def _sigmoid(x):
    return jax.nn.sigmoid(x)


def _in_proj(x, w, pos, inv):
    L = x.shape[0]
    tm = ROW_TILE
    widths = (D_SSD, D_ATT, D_ATT, D_XBC, 2 * D_KV, DT_PAD)

    def body(x_ref, w_ref, pos_ref, inv_ref, z_ref, g_ref, q_ref, xbc_ref, kv_ref, dt_ref, xb_ref):
        xb = _bf(x_ref[...])
        xb_ref[...] = xb
        for o_ref, off, wd in zip((z_ref, g_ref, xbc_ref, dt_ref), (P_Z, P_G, P_XBC, P_DT), (D_SSD, D_ATT, D_XBC, DT_PAD)):
            o_ref[...] = _dot_nt(xb, w_ref[off:off + wd, :])
        tabs = _rope_tables(pos_ref, inv_ref)
        q_ref[...] = _rope(_dot_nt(xb, w_ref[P_Q:P_Q + D_ATT, :]), tabs)
        kv_ref[:, 0:D_KV] = _rope(_dot_nt(xb, w_ref[P_KV:P_KV + D_KV, :]), tabs)
        kv_ref[:, D_KV:2 * D_KV] = _dot_nt(xb, w_ref[P_KV + D_KV:P_KV + 2 * D_KV, :])

    row = lambda wd: pl.BlockSpec((tm, wd), lambda i: (i, 0))
    return pl.pallas_call(
        body, name="in_proj", grid=(L // tm,),
        in_specs=[row(D_MODEL), pl.BlockSpec((P_END, D_MODEL), lambda i: (0, 0), pipeline_mode=pl.Buffered(1)), row(1),
                  pl.BlockSpec((1, 2 * ATT_HD), lambda i: (0, 0))],
        out_specs=[row(wd) for wd in widths] + [row(D_MODEL)],
        out_shape=[jax.ShapeDtypeStruct((L, wd), F32) for wd in widths] + [jax.ShapeDtypeStruct((L, D_MODEL), BF16)],
        compiler_params=_cparams(("parallel",)),
    )(x, w, pos, inv)


def _matmul_tn(a, b, tn, name):
    K, M = a.shape
    N = b.shape[1]
    tk = 512
    nk = K // tk

    def body(a_ref, b_ref, o_ref, acc_ref):
        k = pl.program_id(1)

        @pl.when(k == 0)
        def _():
            acc_ref[...] = jnp.zeros_like(acc_ref)

        acc_ref[...] += _dot_tn(_bf(a_ref[...]), _bf(b_ref[...]))

        @pl.when(k == nk - 1)
        def _():
            o_ref[...] = acc_ref[...]

    return pl.pallas_call(
        body, name=name, grid=(N // tn, nk),
        in_specs=[pl.BlockSpec((tk, M), lambda j, k: (k, 0)), pl.BlockSpec((tk, tn), lambda j, k: (k, j))],
        out_specs=pl.BlockSpec((M, tn), lambda j, k: (0, j)),
        out_shape=jax.ShapeDtypeStruct((M, N), F32),
        scratch_shapes=[pltpu.VMEM((M, tn), F32)],
        compiler_params=_cparams(("parallel", "arbitrary")),
    )(a, b)


def _grad_x(dr, dz, dg, dq, dxbc, dkv, ddt, w):
    L = dr.shape[0]
    tm = ROW_TILE
    widths = (D_SSD, D_ATT, D_ATT, D_XBC, 2 * D_KV, DT_PAD)
    offs = (P_Z, P_G, P_Q, P_XBC, P_KV, P_DT)

    def body(dr_ref, dz_ref, dg_ref, dq_ref, dxbc_ref, dkv_ref, ddt_ref, w_ref, o_ref):
        acc = ALPHA * dr_ref[...]
        for p_ref, off, wd in zip((dz_ref, dg_ref, dq_ref, dxbc_ref, dkv_ref, ddt_ref), offs, widths):
            acc = acc + _dot(_bf(p_ref[...]), w_ref[off:off + wd, :])
        o_ref[...] = acc

    row = lambda wd: pl.BlockSpec((tm, wd), lambda i: (i, 0))
    return pl.pallas_call(
        body, name="grad_x", grid=(L // tm,),
        in_specs=[row(D_MODEL)] + [row(wd) for wd in widths] + [pl.BlockSpec((P_END, D_MODEL), lambda i: (0, 0), pipeline_mode=pl.Buffered(1))],
        out_specs=row(D_MODEL),
        out_shape=jax.ShapeDtypeStruct((L, D_MODEL), F32),
        compiler_params=_cparams(("parallel",)),
    )(dr, dz, dg, dq, dxbc, dkv, ddt, w)


def _ssd_chunk_pre(first, xbc_ref, tail_ref, dt_ref, cw_ref, cb_ref, dtb_ref, alog_ref, ext):
    tail = jnp.where(first, 0.0, tail_ref[...])
    ext[0:8, :] = tail
    ext[8:8 + CHUNK, :] = xbc_ref[...]
    u = cb_ref[...] + cw_ref[0:1, :] * ext[pl.ds(5, CHUNK), :]
    for k in range(1, CONV_K):
        u = u + cw_ref[k:k + 1, :] * ext[pl.ds(5 + k, CHUNK), :]
    sig = _sigmoid(u)
    xbc = u * sig
    dtraw = dt_ref[:, 0:SSD_HEADS] + dtb_ref[...]
    dt = jax.nn.softplus(dtraw)
    A = -jnp.exp(alog_ref[...])
    a = dt * A
    tril = (_iota2((CHUNK, CHUNK), 0) >= _iota2((CHUNK, CHUNK), 1)).astype(F32)
    acs = jnp.dot(tril, a, preferred_element_type=F32, precision=HI)
    acs_row = _to_rows(acs)
    return u, sig, xbc, dtraw, dt, A, acs, acs_row


def _ssd_fwd(z, xbc, dtp, conv_w, conv_b, dt_bias, a_log, d_skip, norm_w):
    L = z.shape[0]
    nc = L // CHUNK

    def body(z_ref, xbc_ref, tail_ref, dt_ref, cw_ref, cb_ref, dtb_ref, alog_ref, dsk_ref, nw_ref,
             y_ref, ypre_ref, prev_ref, state, ext, ybuf):
        c = pl.program_id(0)

        @pl.when(c == 0)
        def _():
            state[...] = jnp.zeros_like(state)

        u, sig, xbcv, dtraw, dt, A, acs, acs_row = _ssd_chunk_pre(
            c == 0, xbc_ref, tail_ref, dt_ref, cw_ref, cb_ref, dtb_ref, alog_ref, ext)
        prev_ref[0] = state[...]
        causal = _iota2((CHUNK, CHUNK), 0) >= _iota2((CHUNK, CHUNK), 1)
        alast = acs[CHUNK - 1:CHUNK, :]
        for g in range(SSD_GROUPS):
            Bg = _bf(xbcv[:, D_SSD + SSD_N * g:D_SSD + SSD_N * (g + 1)])
            Cg = _bf(xbcv[:, D_SSD + D_BC + SSD_N * g:D_SSD + D_BC + SSD_N * (g + 1)])
            cb = _dot_nt(Cg, Bg)
            for r in range(SSD_R):
                h = g * SSD_R + r
                hs = slice(SSD_P * h, SSD_P * (h + 1))
                acs_c = acs[:, h:h + 1]
                seg = acs_c - acs_row[h:h + 1, :]
                Lm = jnp.where(causal, jnp.exp(jnp.where(causal, seg, 0.0)), 0.0)
                M = cb * Lm
                xh = xbcv[:, hs]
                X = xh * dt[:, h:h + 1]
                prev_h = state[hs, :]
                ydiag = _dot(_bf(M), _bf(X))
                yoff = _dot_nt(Cg, _bf(prev_h)) * jnp.exp(acs_c)
                al = alast[:, h:h + 1]
                Xd = X * jnp.exp(al - acs_c)
                state[hs, :] = prev_h * jnp.exp(al) + _dot_tn(_bf(Xd), Bg)
                ybuf[:, hs] = ydiag + yoff + dsk_ref[:, h:h + 1] * xh
        y = ybuf[...]
        ypre_ref[...] = y
        zv = z_ref[...]
        yf = y * (zv * _sigmoid(zv))
        half = D_SSD // SSD_GROUPS
        for g in range(SSD_GROUPS):
            gs = slice(half * g, half * (g + 1))
            yg = yf[:, gs]
            ms = jnp.mean(yg * yg, axis=-1, keepdims=True)
            y_ref[:, gs] = yg * lax.rsqrt(ms + RMS_EPS) * nw_ref[:, gs]

    full = lambda shape: pl.BlockSpec(shape, lambda c: (0, 0))
    return pl.pallas_call(
        body, name="ssd_fwd", grid=(nc,),
        in_specs=[
            pl.BlockSpec((CHUNK, D_SSD), lambda c: (c, 0)),
            pl.BlockSpec((CHUNK, D_XBC), lambda c: (c, 0)),
            pl.BlockSpec((8, D_XBC), lambda c: (jnp.maximum(c * (CHUNK // 8) - 1, 0), 0)),
            pl.BlockSpec((CHUNK, DT_PAD), lambda c: (c, 0)),
            full((CONV_K, D_XBC)), full((1, D_XBC)), full((1, SSD_HEADS)), full((1, SSD_HEADS)), full((1, SSD_HEADS)),
            full((1, D_SSD)),
        ],
        out_specs=[
            pl.BlockSpec((CHUNK, D_SSD), lambda c: (c, 0)),
            pl.BlockSpec((CHUNK, D_SSD), lambda c: (c, 0)),
            pl.BlockSpec((1, SSD_HEADS * SSD_P, SSD_N), lambda c: (c, 0, 0)),
        ],
        out_shape=[
            jax.ShapeDtypeStruct((L, D_SSD), F32),
            jax.ShapeDtypeStruct((L, D_SSD), F32),
            jax.ShapeDtypeStruct((nc, SSD_HEADS * SSD_P, SSD_N), F32),
        ],
        scratch_shapes=[
            pltpu.VMEM((SSD_HEADS * SSD_P, SSD_N), F32),
            pltpu.VMEM((CHUNK + 8, D_XBC), F32),
            pltpu.VMEM((CHUNK, D_SSD), F32),
        ],
        compiler_params=_cparams(("arbitrary",)),
    )(z, xbc, xbc, dtp, conv_w, conv_b, dt_bias, a_log, d_skip, norm_w)


def _ssd_bwd(dy, z, ypre, xbc, dtp, prev, conv_w, conv_b, dt_bias, a_log, d_skip, norm_w):
    L = z.shape[0]
    nc = L // CHUNK

    def body(dy_ref, z_ref, ypre_ref, xbc_ref, tail_ref, dt_ref, prev_ref, cw_ref, cb_ref, dtb_ref, alog_ref, dsk_ref,
             nw_ref, dz_ref, dxbc_ref, ddt_ref, gcw_ref, gcb_ref, gdtb_ref, galog_ref, gdsk_ref, gnw_ref,
             dstate, dhead, ext, ext2, dpost):
        i = pl.program_id(0)
        c = nc - 1 - i

        @pl.when(i == 0)
        def _():
            dstate[...] = jnp.zeros_like(dstate)
            dhead[...] = jnp.zeros_like(dhead)
            gcw_ref[...] = jnp.zeros_like(gcw_ref)
            gcb_ref[...] = jnp.zeros_like(gcb_ref)
            gdtb_ref[...] = jnp.zeros_like(gdtb_ref)
            galog_ref[...] = jnp.zeros_like(galog_ref)
            gdsk_ref[...] = jnp.zeros_like(gdsk_ref)
            gnw_ref[...] = jnp.zeros_like(gnw_ref)

        u, sig, xbcv, dtraw, dt, A, acs, acs_row = _ssd_chunk_pre(
            c == 0, xbc_ref, tail_ref, dt_ref, cw_ref, cb_ref, dtb_ref, alog_ref, ext)

        zv = z_ref[...]
        ypre = ypre_ref[...]
        dyn = dy_ref[...]
        sz = _sigmoid(zv)
        silu_z = zv * sz
        yf = ypre * silu_z
        half = D_SSD // SSD_GROUPS
        dyf_parts = []
        for g in range(SSD_GROUPS):
            gs = slice(half * g, half * (g + 1))
            yg = yf[:, gs]
            rstd = lax.rsqrt(jnp.mean(yg * yg, axis=-1, keepdims=True) + RMS_EPS)
            dout = dyn[:, gs]
            gnw_ref[:, gs] += jnp.sum(dout * yg * rstd, axis=0, keepdims=True)
            dyhat = dout * nw_ref[:, gs]
            dyf_parts.append(rstd * (dyhat - yg * (rstd * rstd) * jnp.mean(dyhat * yg, axis=-1, keepdims=True)))
        dyf = jnp.concatenate(dyf_parts, axis=1)
        dz_ref[...] = dyf * ypre * (sz * (1.0 + zv * (1.0 - sz)))
        dypre = dyf * silu_z

        causal = _iota2((CHUNK, CHUNK), 0) >= _iota2((CHUNK, CHUNK), 1)
        alast = acs[CHUNK - 1:CHUNK, :]
        lane16 = _iota2((1, SSD_HEADS), 1)
        sub16 = _iota2((SSD_HEADS, 1), 0)
        dacs_col = jnp.zeros((CHUNK, SSD_HEADS), F32)
        dacs_row = jnp.zeros((SSD_HEADS, CHUNK), F32)
        ddt_col = jnp.zeros((CHUNK, SSD_HEADS), F32)
        dalast = jnp.zeros((1, SSD_HEADS), F32)
        gdsk = jnp.zeros((1, SSD_HEADS), F32)
        for g in range(SSD_GROUPS):
            bs = slice(D_SSD + SSD_N * g, D_SSD + SSD_N * (g + 1))
            cs = slice(D_SSD + D_BC + SSD_N * g, D_SSD + D_BC + SSD_N * (g + 1))
            Bg = _bf(xbcv[:, bs])
            Cg = _bf(xbcv[:, cs])
            cb = _dot_nt(Cg, Bg)
            dcb = jnp.zeros((CHUNK, CHUNK), F32)
            dB = jnp.zeros((CHUNK, SSD_N), F32)
            dC = jnp.zeros((CHUNK, SSD_N), F32)
            for r in range(SSD_R):
                h = g * SSD_R + r
                hs = slice(SSD_P * h, SSD_P * (h + 1))
                onehot = (lane16 == h).astype(F32)
                acs_c = acs[:, h:h + 1]
                seg = acs_c - acs_row[h:h + 1, :]
                Lm = jnp.where(causal, jnp.exp(jnp.where(causal, seg, 0.0)), 0.0)
                M = cb * Lm
                xh = xbcv[:, hs]
                dth = dt[:, h:h + 1]
                X = xh * dth
                Xb = _bf(X)
                dyh = dypre[:, hs]
                dyb = _bf(dyh)
                prev_h = prev_ref[0, hs, :]
                prevb = _bf(prev_h)
                dnext = dstate[hs, :]
                dnextb = _bf(dnext)
                al = alast[:, h:h + 1]
                eacs = jnp.exp(acs_c)
                eal = jnp.exp(al)
                dsd = jnp.exp(al - acs_c)
                G = _bf(dyh * eacs)
                dstate[hs, :] = dnext * eal + _dot_tn(G, Cg)
                dC = dC + _dot(G, prevb)
                yoff = _dot_nt(Cg, prevb) * eacs
                dacs_h = jnp.sum(dyh * yoff, axis=-1, keepdims=True)
                BdN = _dot_nt(Bg, dnextb)
                dX = dsd * BdN
                dB = dB + _dot(_bf(X * dsd), dnextb)
                t = jnp.sum(X * BdN, axis=-1, keepdims=True) * dsd
                dacs_h = dacs_h - t
                dal = jnp.sum(t, axis=0, keepdims=True) + jnp.sum(
                    jnp.sum(dnext * prev_h, axis=-1, keepdims=True), axis=0, keepdims=True) * eal
                dM = _dot_nt(dyb, Xb)
                dX = dX + _dot_tn(_bf(M), dyb)
                dseg = dM * M
                dcb = dcb + dM * Lm
                dacs_h = dacs_h + jnp.sum(dseg, axis=-1, keepdims=True)
                dacs_row = dacs_row - jnp.sum(dseg, axis=0, keepdims=True) * (sub16 == h).astype(F32)
                dacs_col = dacs_col + dacs_h * onehot
                dalast = dalast + dal * onehot
                ddt_col = ddt_col + jnp.sum(dX * xh, axis=-1, keepdims=True) * onehot
                gdsk = gdsk + jnp.sum(jnp.sum(dyh * xh, axis=-1, keepdims=True), axis=0, keepdims=True) * onehot
                dpost[:, hs] = dX * dth + dsk_ref[:, h:h + 1] * dyh
            dcbb = _bf(dcb)
            dpost[:, bs] = dB + _dot_tn(dcbb, Cg)
            dpost[:, cs] = dC + _dot(dcbb, Bg)

        is_last = (_iota2((CHUNK, 1), 0) == CHUNK - 1).astype(F32)
        dacs = dacs_col + _to_cols(dacs_row) + is_last * dalast
        triu = (_iota2((CHUNK, CHUNK), 0) <= _iota2((CHUNK, CHUNK), 1)).astype(F32)
        da = jnp.dot(triu, dacs, preferred_element_type=F32, precision=HI)
        ddt_tot = ddt_col + da * A
        galog_ref[...] += jnp.sum(da * dt, axis=0, keepdims=True) * A
        ddtraw = ddt_tot * _sigmoid(dtraw)
        gdtb_ref[...] += jnp.sum(ddtraw, axis=0, keepdims=True)
        gdsk_ref[...] += gdsk
        ddt_ref[...] = jnp.zeros_like(ddt_ref)
        ddt_ref[:, 0:SSD_HEADS] = ddtraw

        dconv = dpost[...] * (sig * (1.0 + u * (1.0 - sig)))
        gcb_ref[...] += jnp.sum(dconv, axis=0, keepdims=True)
        for k in range(CONV_K):
            gcw_ref[k:k + 1, :] += jnp.sum(dconv * ext[pl.ds(5 + k, CHUNK), :], axis=0, keepdims=True)
        ext2[0:CHUNK, :] = dconv
        ext2[CHUNK:CHUNK + 8, :] = dhead[...]
        dx = cw_ref[CONV_K - 1:CONV_K, :] * dconv
        for k in range(CONV_K - 1):
            dx = dx + cw_ref[k:k + 1, :] * ext2[pl.ds(CONV_K - 1 - k, CHUNK), :]
        dxbc_ref[...] = dx
        dhead[...] = dconv[0:8, :]

    full = lambda shape: pl.BlockSpec(shape, lambda i: (0, 0))
    rev = lambda wd: pl.BlockSpec((CHUNK, wd), lambda i: (nc - 1 - i, 0))
    return pl.pallas_call(
        body, name="ssd_bwd", grid=(nc,),
        in_specs=[
            rev(D_SSD), rev(D_SSD), rev(D_SSD), rev(D_XBC),
            pl.BlockSpec((8, D_XBC), lambda i: (jnp.maximum((nc - 1 - i) * (CHUNK // 8) - 1, 0), 0)),
            rev(DT_PAD),
            pl.BlockSpec((1, SSD_HEADS * SSD_P, SSD_N), lambda i: (nc - 1 - i, 0, 0)),
            full((CONV_K, D_XBC)), full((1, D_XBC)), full((1, SSD_HEADS)), full((1, SSD_HEADS)), full((1, SSD_HEADS)),
            full((1, D_SSD)),
        ],
        out_specs=[
            rev(D_SSD), rev(D_XBC), rev(DT_PAD),
            full((CONV_K, D_XBC)), full((1, D_XBC)), full((1, SSD_HEADS)), full((1, SSD_HEADS)), full((1, SSD_HEADS)),
            full((1, D_SSD)),
        ],
        out_shape=[
            jax.ShapeDtypeStruct((L, D_SSD), F32), jax.ShapeDtypeStruct((L, D_XBC), F32),
            jax.ShapeDtypeStruct((L, DT_PAD), F32),
            jax.ShapeDtypeStruct((CONV_K, D_XBC), F32), jax.ShapeDtypeStruct((1, D_XBC), F32),
            jax.ShapeDtypeStruct((1, SSD_HEADS), F32), jax.ShapeDtypeStruct((1, SSD_HEADS), F32),
            jax.ShapeDtypeStruct((1, SSD_HEADS), F32), jax.ShapeDtypeStruct((1, D_SSD), F32),
        ],
        scratch_shapes=[
            pltpu.VMEM((SSD_HEADS * SSD_P, SSD_N), F32),
            pltpu.VMEM((8, D_XBC), F32),
            pltpu.VMEM((CHUNK + 8, D_XBC), F32),
            pltpu.VMEM((CHUNK + 8, D_XBC), F32),
            pltpu.VMEM((CHUNK, D_XBC), F32),
        ],
        compiler_params=_cparams(("arbitrary",)),
    )(dy, z, ypre, xbc, xbc, dtp, prev, conv_w, conv_b, dt_bias, a_log, d_skip, norm_w)


def _head_expander():
    return (_iota2((SSD_HEADS, D_SSD), 1) // SSD_P == _iota2((SSD_HEADS, D_SSD), 0)).astype(F32)


def _expand(v, e):
    return jnp.dot(v, e, preferred_element_type=F32, precision=HI)


def _headsum(t, e):
    m = t.shape[0]
    if m < 8:
        t = jnp.broadcast_to(t[0:1], (8, t.shape[1]))
    out = lax.dot_general(t, e, (((1,), (1,)), ((), ())), preferred_element_type=F32, precision=HI)
    return out[0:m]


def _ssd_decays(dt, acs, dsk_ref, e):
    alast = acs[CHUNK - 1:CHUNK, :]
    stk = jnp.concatenate([dt, jnp.exp(acs), jnp.exp(alast - acs),
                           jnp.broadcast_to(jnp.exp(alast), (8, SSD_HEADS)),
                           jnp.broadcast_to(dsk_ref[...], (8, SSD_HEADS))], axis=0)
    ex = _expand(stk, e)
    return (ex[0:CHUNK], ex[CHUNK:2 * CHUNK], ex[2 * CHUNK:3 * CHUNK], ex[3 * CHUNK:3 * CHUNK + 1],
            ex[3 * CHUNK + 8:3 * CHUNK + 9])


def _ssd_fwd2(z, xbc, dtp, conv_w, conv_b, dt_bias, a_log, d_skip, norm_w):
    L = z.shape[0]
    nc = L // CHUNK
    half = D_SSD // SSD_GROUPS

    def body(z_ref, xbc_ref, tail_ref, dt_ref, cw_ref, cb_ref, dtb_ref, alog_ref, dsk_ref, nw_ref,
             y_ref, ypre_ref, prev_ref, state, ext, ybuf, mbuf):
        c = pl.program_id(0)

        @pl.when(c == 0)
        def _():
            state[...] = jnp.zeros_like(state)

        u, sig, xbcv, dtraw, dt, A, acs, acs_row = _ssd_chunk_pre(
            c == 0, xbc_ref, tail_ref, dt_ref, cw_ref, cb_ref, dtb_ref, alog_ref, ext)
        e = _head_expander()
        dtE, eacsE, dsdE, ealE, dskE = _ssd_decays(dt, acs, dsk_ref, e)
        xs = xbcv[:, 0:D_SSD]
        X = xs * dtE
        prev_ref[0] = state[...]
        causal = _iota2((CHUNK, CHUNK), 0) >= _iota2((CHUNK, CHUNK), 1)
        for g in range(SSD_GROUPS):
            gs = slice(half * g, half * (g + 1))
            Bg = _bf(xbcv[:, D_SSD + SSD_N * g:D_SSD + SSD_N * (g + 1)])
            Cg = _bf(xbcv[:, D_SSD + D_BC + SSD_N * g:D_SSD + D_BC + SSD_N * (g + 1)])
            cb = _dot_nt(Cg, Bg)
            for r in range(SSD_R):
                h = g * SSD_R + r
                seg = acs[:, h:h + 1] - acs_row[h:h + 1, :]
                mbuf[h] = _bf(cb * jnp.where(causal, jnp.exp(jnp.where(causal, seg, 0.0)), 0.0))
            st = state[:, gs]
            ybuf[:, gs] = _dot(Cg, _bf(st)) * eacsE[:, gs] + dskE[:, gs] * xs[:, gs]
            state[:, gs] = st * ealE[:, gs] + _dot_tn(Bg, _bf(X[:, gs] * dsdE[:, gs]))
        Xb = _bf(X)
        for h in range(SSD_HEADS):
            hs = slice(SSD_P * h, SSD_P * (h + 1))
            ybuf[:, hs] += _dot(mbuf[h], Xb[:, hs])
        y = ybuf[...]
        ypre_ref[...] = y
        zv = z_ref[...]
        yf = y * (zv * _sigmoid(zv))
        for g in range(SSD_GROUPS):
            gs = slice(half * g, half * (g + 1))
            yg = yf[:, gs]
            ms = jnp.mean(yg * yg, axis=-1, keepdims=True)
            y_ref[:, gs] = yg * lax.rsqrt(ms + RMS_EPS) * nw_ref[:, gs]

    full = lambda shape: pl.BlockSpec(shape, lambda c: (0, 0))
    return pl.pallas_call(
        body, name="ssd_fwd", grid=(nc,),
        in_specs=[
            pl.BlockSpec((CHUNK, D_SSD), lambda c: (c, 0)),
            pl.BlockSpec((CHUNK, D_XBC), lambda c: (c, 0)),
            pl.BlockSpec((8, D_XBC), lambda c: (jnp.maximum(c * (CHUNK // 8) - 1, 0), 0)),
            pl.BlockSpec((CHUNK, DT_PAD), lambda c: (c, 0)),
            full((CONV_K, D_XBC)), full((1, D_XBC)), full((1, SSD_HEADS)), full((1, SSD_HEADS)), full((1, SSD_HEADS)),
            full((1, D_SSD)),
        ],
        out_specs=[
            pl.BlockSpec((CHUNK, D_SSD), lambda c: (c, 0)),
            pl.BlockSpec((CHUNK, D_SSD), lambda c: (c, 0)),
            pl.BlockSpec((1, SSD_N, D_SSD), lambda c: (c, 0, 0)),
        ],
        out_shape=[
            jax.ShapeDtypeStruct((L, D_SSD), F32),
            jax.ShapeDtypeStruct((L, D_SSD), F32),
            jax.ShapeDtypeStruct((nc, SSD_N, D_SSD), F32),
        ],
        scratch_shapes=[
            pltpu.VMEM((SSD_N, D_SSD), F32),
            pltpu.VMEM((CHUNK + 8, D_XBC), F32),
            pltpu.VMEM((CHUNK, D_SSD), F32),
            pltpu.VMEM((SSD_HEADS, CHUNK, CHUNK), BF16),
        ],
        compiler_params=_cparams(("arbitrary",)),
    )(z, xbc, xbc, dtp, conv_w, conv_b, dt_bias, a_log, d_skip, norm_w)


def _ssd_bwd2(dy, z, ypre, xbc, dtp, prev, conv_w, conv_b, dt_bias, a_log, d_skip, norm_w):
    L = z.shape[0]
    nc = L // CHUNK
    half = D_SSD // SSD_GROUPS

    def body(dy_ref, z_ref, ypre_ref, xbc_ref, tail_ref, dt_ref, prev_ref, cw_ref, cb_ref, dtb_ref, alog_ref, dsk_ref,
             nw_ref, dz_ref, dxbc_ref, ddt_ref, gcw_ref, gcb_ref, gdtb_ref, galog_ref, gdsk_ref, gnw_ref,
             dstate, dhead, ext, ext2, dpost, yobuf, bdbuf, lmbuf, dmbuf, cbbuf):
        i = pl.program_id(0)
        c = nc - 1 - i

        @pl.when(i == 0)
        def _():
            dstate[...] = jnp.zeros_like(dstate)
            dhead[...] = jnp.zeros_like(dhead)
            gcw_ref[...] = jnp.zeros_like(gcw_ref)
            gcb_ref[...] = jnp.zeros_like(gcb_ref)
            gdtb_ref[...] = jnp.zeros_like(gdtb_ref)
            galog_ref[...] = jnp.zeros_like(galog_ref)
            gdsk_ref[...] = jnp.zeros_like(gdsk_ref)
            gnw_ref[...] = jnp.zeros_like(gnw_ref)

        u, sig, xbcv, dtraw, dt, A, acs, acs_row = _ssd_chunk_pre(
            c == 0, xbc_ref, tail_ref, dt_ref, cw_ref, cb_ref, dtb_ref, alog_ref, ext)
        e = _head_expander()
        dtE, eacsE, dsdE, ealE, dskE = _ssd_decays(dt, acs, dsk_ref, e)
        alast = acs[CHUNK - 1:CHUNK, :]
        xs = xbcv[:, 0:D_SSD]
        X = xs * dtE
        Xb = _bf(X)

        zv = z_ref[...]
        ypre = ypre_ref[...]
        dyn = dy_ref[...]
        sz = _sigmoid(zv)
        silu_z = zv * sz
        yf = ypre * silu_z
        dyf_parts = []
        for g in range(SSD_GROUPS):
            gs = slice(half * g, half * (g + 1))
            yg = yf[:, gs]
            rstd = lax.rsqrt(jnp.mean(yg * yg, axis=-1, keepdims=True) + RMS_EPS)
            dout = dyn[:, gs]
            gnw_ref[:, gs] += jnp.sum(dout * yg * rstd, axis=0, keepdims=True)
            dyhat = dout * nw_ref[:, gs]
            dyf_parts.append(rstd * (dyhat - yg * (rstd * rstd) * jnp.mean(dyhat * yg, axis=-1, keepdims=True)))
        dyf = jnp.concatenate(dyf_parts, axis=1)
        dz_ref[...] = dyf * ypre * (sz * (1.0 + zv * (1.0 - sz)))
        dyp = dyf * silu_z
        dyb = _bf(dyp)
        G = dyp * eacsE

        causal = _iota2((CHUNK, CHUNK), 0) >= _iota2((CHUNK, CHUNK), 1)
        ST = prev_ref[0]
        dST = dstate[...]
        for g in range(SSD_GROUPS):
            gs = slice(half * g, half * (g + 1))
            bs = slice(D_SSD + SSD_N * g, D_SSD + SSD_N * (g + 1))
            cs = slice(D_SSD + D_BC + SSD_N * g, D_SSD + D_BC + SSD_N * (g + 1))
            Bg = _bf(xbcv[:, bs])
            Cg = _bf(xbcv[:, cs])
            Gb = _bf(G[:, gs])
            STb = _bf(ST[:, gs])
            dSTb = _bf(dST[:, gs])
            dstate[:, gs] = dST[:, gs] * ealE[:, gs] + _dot_tn(Cg, Gb)
            yobuf[:, gs] = _dot(Cg, STb) * eacsE[:, gs]
            bdbuf[:, gs] = _dot(Bg, dSTb)
            dpost[:, cs] = _dot_nt(Gb, STb)
            dpost[:, bs] = _dot_nt(_bf(X[:, gs] * dsdE[:, gs]), dSTb)
            cbbuf[g] = _dot_nt(Cg, Bg)
            for r in range(SSD_R):
                h = g * SSD_R + r
                seg = acs[:, h:h + 1] - acs_row[h:h + 1, :]
                lmbuf[h] = jnp.where(causal, jnp.exp(jnp.where(causal, seg, 0.0)), 0.0)
        for h in range(SSD_HEADS):
            hs = slice(SSD_P * h, SSD_P * (h + 1))
            Mb = _bf(cbbuf[h // SSD_R] * lmbuf[h])
            dmbuf[h] = _dot_nt(dyb[:, hs], Xb[:, hs])
            dpost[:, hs] = _dot_tn(Mb, dyb[:, hs])
        lane16 = _iota2((1, SSD_HEADS), 1)
        sub16 = _iota2((SSD_HEADS, 1), 0)
        dacs_col = jnp.zeros((CHUNK, SSD_HEADS), F32)
        dacs_row = jnp.zeros((SSD_HEADS, CHUNK), F32)
        for g in range(SSD_GROUPS):
            bs = slice(D_SSD + SSD_N * g, D_SSD + SSD_N * (g + 1))
            cs = slice(D_SSD + D_BC + SSD_N * g, D_SSD + D_BC + SSD_N * (g + 1))
            cb = cbbuf[g]
            dcb = jnp.zeros((CHUNK, CHUNK), F32)
            for r in range(SSD_R):
                h = g * SSD_R + r
                dM = dmbuf[h]
                Lm = lmbuf[h]
                dcb = dcb + dM * Lm
                dseg = dM * (cb * Lm)
                dacs_col = dacs_col + jnp.sum(dseg, axis=-1, keepdims=True) * (lane16 == h).astype(F32)
                dacs_row = dacs_row - jnp.sum(dseg, axis=0, keepdims=True) * (sub16 == h).astype(F32)
            dcbb = _bf(dcb)
            dpost[:, bs] += _dot_tn(dcbb, _bf(xbcv[:, cs]))
            dpost[:, cs] += _dot(dcbb, _bf(xbcv[:, bs]))

        BD = bdbuf[...]
        dX = dpost[:, 0:D_SSD] + dsdE * BD
        dsd = jnp.exp(alast - acs)
        T = _headsum(X * BD, e) * dsd
        dalast = jnp.sum(T, axis=0, keepdims=True) + _headsum(
            jnp.sum(dST * ST, axis=0, keepdims=True), e) * jnp.exp(alast)
        is_last = (_iota2((CHUNK, 1), 0) == CHUNK - 1).astype(F32)
        dacs = dacs_col + _to_cols(dacs_row) + _headsum(dyp * yobuf[...], e) - T + is_last * dalast
        triu = (_iota2((CHUNK, CHUNK), 0) <= _iota2((CHUNK, CHUNK), 1)).astype(F32)
        da = jnp.dot(triu, dacs, preferred_element_type=F32, precision=HI)
        ddt_tot = _headsum(dX * xs, e) + da * A
        galog_ref[...] += jnp.sum(da * dt, axis=0, keepdims=True) * A
        ddtraw = ddt_tot * _sigmoid(dtraw)
        gdtb_ref[...] += jnp.sum(ddtraw, axis=0, keepdims=True)
        gdsk_ref[...] += _headsum(jnp.sum(dyp * xs, axis=0, keepdims=True), e)
        ddt_ref[...] = jnp.zeros_like(ddt_ref)
        ddt_ref[:, 0:SSD_HEADS] = ddtraw
        dpost[:, 0:D_SSD] = dX * dtE + dskE * dyp

        dconv = dpost[...] * (sig * (1.0 + u * (1.0 - sig)))
        gcb_ref[...] += jnp.sum(dconv, axis=0, keepdims=True)
        for k in range(CONV_K):
            gcw_ref[k:k + 1, :] += jnp.sum(dconv * ext[pl.ds(5 + k, CHUNK), :], axis=0, keepdims=True)
        ext2[0:CHUNK, :] = dconv
        ext2[CHUNK:CHUNK + 8, :] = dhead[...]
        dx = cw_ref[CONV_K - 1:CONV_K, :] * dconv
        for k in range(CONV_K - 1):
            dx = dx + cw_ref[k:k + 1, :] * ext2[pl.ds(CONV_K - 1 - k, CHUNK), :]
        dxbc_ref[...] = dx
        dhead[...] = dconv[0:8, :]

    full = lambda shape: pl.BlockSpec(shape, lambda i: (0, 0))
    rev = lambda wd: pl.BlockSpec((CHUNK, wd), lambda i: (nc - 1 - i, 0))
    return pl.pallas_call(
        body, name="ssd_bwd", grid=(nc,),
        in_specs=[
            rev(D_SSD), rev(D_SSD), rev(D_SSD), rev(D_XBC),
            pl.BlockSpec((8, D_XBC), lambda i: (jnp.maximum((nc - 1 - i) * (CHUNK // 8) - 1, 0), 0)),
            rev(DT_PAD),
            pl.BlockSpec((1, SSD_N, D_SSD), lambda i: (nc - 1 - i, 0, 0)),
            full((CONV_K, D_XBC)), full((1, D_XBC)), full((1, SSD_HEADS)), full((1, SSD_HEADS)), full((1, SSD_HEADS)),
            full((1, D_SSD)),
        ],
        out_specs=[
            rev(D_SSD), rev(D_XBC), rev(DT_PAD),
            full((CONV_K, D_XBC)), full((1, D_XBC)), full((1, SSD_HEADS)), full((1, SSD_HEADS)), full((1, SSD_HEADS)),
            full((1, D_SSD)),
        ],
        out_shape=[
            jax.ShapeDtypeStruct((L, D_SSD), F32), jax.ShapeDtypeStruct((L, D_XBC), F32),
            jax.ShapeDtypeStruct((L, DT_PAD), F32),
            jax.ShapeDtypeStruct((CONV_K, D_XBC), F32), jax.ShapeDtypeStruct((1, D_XBC), F32),
            jax.ShapeDtypeStruct((1, SSD_HEADS), F32), jax.ShapeDtypeStruct((1, SSD_HEADS), F32),
            jax.ShapeDtypeStruct((1, SSD_HEADS), F32), jax.ShapeDtypeStruct((1, D_SSD), F32),
        ],
        scratch_shapes=[
            pltpu.VMEM((SSD_N, D_SSD), F32),
            pltpu.VMEM((8, D_XBC), F32),
            pltpu.VMEM((CHUNK + 8, D_XBC), F32),
            pltpu.VMEM((CHUNK + 8, D_XBC), F32),
            pltpu.VMEM((CHUNK, D_XBC), F32),
            pltpu.VMEM((CHUNK, D_SSD), F32),
            pltpu.VMEM((CHUNK, D_SSD), F32),
            pltpu.VMEM((SSD_HEADS, CHUNK, CHUNK), F32),
            pltpu.VMEM((SSD_HEADS, CHUNK, CHUNK), F32),
            pltpu.VMEM((SSD_GROUPS, CHUNK, CHUNK), F32),
        ],
        compiler_params=_cparams(("arbitrary",)),
    )(dy, z, ypre, xbc, xbc, dtp, prev, conv_w, conv_b, dt_bias, a_log, d_skip, norm_w)


def _rope_tables(pos_ref, inv_ref):
    ang = pos_ref[...].astype(F32) * inv_ref[...]
    d = _iota2((1, 2 * ATT_HD), 1) % ATT_HD
    s = jnp.sin(ang)
    return jnp.cos(ang), jnp.where(d < ROPE_DIM // 2, -s, 0.0), jnp.where((d >= ROPE_DIM // 2) & (d < ROPE_DIM), s, 0.0)


def _rope(t, tabs):
    c, s1, s2 = tabs
    n = t.shape[1]
    rep = n // c.shape[1]
    return (t * jnp.tile(c, (1, rep)) + pltpu.roll(t, n - ROPE_DIM // 2, 1) * jnp.tile(s1, (1, rep))
            + pltpu.roll(t, ROPE_DIM // 2, 1) * jnp.tile(s2, (1, rep)))


def _rope_t(t, tabs):
    c, s1, s2 = tabs
    n = t.shape[1]
    rep = n // c.shape[1]
    return (t * jnp.tile(c, (1, rep)) + pltpu.roll(t * jnp.tile(s1, (1, rep)), ROPE_DIM // 2, 1)
            + pltpu.roll(t * jnp.tile(s2, (1, rep)), n - ROPE_DIM // 2, 1))


def _swa_mask(first):
    qi = _iota2((WINDOW, 2 * WINDOW), 0)
    si = _iota2((WINDOW, 2 * WINDOW), 1)
    band = (si > qi) & (si <= qi + WINDOW)
    return band & (jnp.logical_not(first) | (si >= WINDOW))


def _stack_heads(t, j):
    return jnp.concatenate([t[:, ATT_HD * (j * ATT_R + r):ATT_HD * (j * ATT_R + r + 1)] for r in range(ATT_R)], axis=0)


def _stack_cols(ref, j):
    cols = [jnp.broadcast_to(ref[:, j * ATT_R + r:j * ATT_R + r + 1], (WINDOW, 1)) for r in range(ATT_R)]
    return jnp.concatenate(cols, axis=0)


def _swa_fwd(q, g, kv, sinks):
    L = q.shape[0]
    nb = L // WINDOW
    scale = ATT_HD ** -0.5

    def body(q_ref, g_ref, kvc_ref, kvp_ref, sink_ref, y_ref, o_ref, lse_ref, sbuf, pbuf, rbuf):
        n = pl.program_id(0)
        kk = _bf(jnp.concatenate([kvp_ref[:, 0:D_KV], kvc_ref[:, 0:D_KV]], axis=0))
        vv = _bf(jnp.concatenate([kvp_ref[:, D_KV:2 * D_KV], kvc_ref[:, D_KV:2 * D_KV]], axis=0))
        valid = jnp.tile(_swa_mask(n == 0), (ATT_R, 1))
        qv = q_ref[...]
        for j in range(ATT_KVH):
            s = _dot_nt(_bf(_stack_heads(qv, j)), kk[:, ATT_HD * j:ATT_HD * (j + 1)]) * scale
            sbuf[j] = jnp.where(valid, s, NEG_BIG)
        for j in range(ATT_KVH):
            s = sbuf[j]
            sink = _stack_cols(sink_ref, j)
            m = jnp.maximum(jnp.max(s, axis=-1, keepdims=True), sink)
            p = jnp.exp(s - m)
            pbuf[j] = _bf(p)
            denom = jnp.sum(p, axis=-1, keepdims=True) + jnp.exp(sink - m)
            rbuf[j] = 1.0 / denom
            lse = m + jnp.log(denom)
            for r in range(ATT_R):
                h = j * ATT_R + r
                lse_ref[:, h:h + 1] = lse[WINDOW * r:WINDOW * (r + 1)]
        for j in range(ATT_KVH):
            o = _dot(pbuf[j], vv[:, ATT_HD * j:ATT_HD * (j + 1)]) * rbuf[j]
            for r in range(ATT_R):
                h = j * ATT_R + r
                o_ref[:, ATT_HD * h:ATT_HD * (h + 1)] = o[WINDOW * r:WINDOW * (r + 1)]
        gv = g_ref[...]
        y_ref[...] = o_ref[...] * (gv * _sigmoid(gv))

    cur = lambda wd: pl.BlockSpec((WINDOW, wd), lambda n: (n, 0))
    prv = lambda wd: pl.BlockSpec((WINDOW, wd), lambda n: (jnp.maximum(n - 1, 0), 0))
    return pl.pallas_call(
        body, name="swa_fwd", grid=(nb,),
        in_specs=[cur(D_ATT), cur(D_ATT), cur(2 * D_KV), prv(2 * D_KV), pl.BlockSpec((1, ATT_QH), lambda n: (0, 0))],
        out_specs=[cur(D_ATT), cur(D_ATT), cur(ATT_QH)],
        out_shape=[jax.ShapeDtypeStruct((L, D_ATT), F32), jax.ShapeDtypeStruct((L, D_ATT), F32),
                   jax.ShapeDtypeStruct((L, ATT_QH), F32)],
        scratch_shapes=[pltpu.VMEM((ATT_KVH, ATT_R * WINDOW, 2 * WINDOW), F32),
                        pltpu.VMEM((ATT_KVH, ATT_R * WINDOW, 2 * WINDOW), BF16),
                        pltpu.VMEM((ATT_KVH, ATT_R * WINDOW, 1), F32)],
        compiler_params=_cparams(("parallel",)),
    )(q, g, kv, kv, sinks)


def _swa_bwd(dy, q, g, kv, o, lse, pos, inv, sinks):
    L = q.shape[0]
    nb = L // WINDOW
    scale = ATT_HD ** -0.5

    def body(dy_ref, q_ref, g_ref, kvc_ref, kvp_ref, o_ref, lse_ref, posc_ref, posp_ref, inv_ref, sink_ref,
             dq_ref, dg_ref, dkv_ref, dsink_ref, carry, dqbuf, dkbuf, dvbuf):
        n = pl.program_id(0)

        @pl.when(n == 0)
        def _():
            dsink_ref[...] = jnp.zeros_like(dsink_ref)

        @pl.when(n < nb)
        def _():
            tc = _rope_tables(posc_ref, inv_ref)
            tp = _rope_tables(posp_ref, inv_ref)
            kk = _bf(jnp.concatenate([kvp_ref[:, 0:D_KV], kvc_ref[:, 0:D_KV]], axis=0))
            vv = _bf(jnp.concatenate([kvp_ref[:, D_KV:2 * D_KV], kvc_ref[:, D_KV:2 * D_KV]], axis=0))
            valid = jnp.tile(_swa_mask(n == 0), (ATT_R, 1))
            qv = q_ref[...]
            gv = g_ref[...]
            sg = _sigmoid(gv)
            dyv = dy_ref[...]
            ov = o_ref[...]
            dg_ref[...] = dyv * ov * (sg * (1.0 + gv * (1.0 - sg)))
            do = dyv * (gv * sg)
            delta_all = do * ov
            lane16 = _iota2((1, ATT_QH), 1)
            dsink = jnp.zeros((1, ATT_QH), F32)
            for j in range(ATT_KVH):
                js = slice(ATT_HD * j, ATT_HD * (j + 1))
                kj = kk[:, js]
                vj = vv[:, js]
                qs = _bf(_stack_heads(qv, j))
                dos = _bf(_stack_heads(do, j))
                delta = jnp.sum(_stack_heads(delta_all, j), axis=-1, keepdims=True)
                lse = _stack_cols(lse_ref, j)
                s = _dot_nt(qs, kj) * scale
                p = jnp.exp(jnp.where(valid, s, NEG_BIG) - lse)
                dS = _bf(p * (_dot_nt(dos, vj) - delta))
                dqs = _dot(dS, kj) * scale
                dkbuf[:, js] = _dot_tn(dS, qs) * scale
                dvbuf[:, js] = _dot_tn(_bf(p), dos)
                sd = jnp.exp(_stack_cols(sink_ref, j) - lse) * delta
                for r in range(ATT_R):
                    h = j * ATT_R + r
                    rs = slice(WINDOW * r, WINDOW * (r + 1))
                    dqbuf[:, ATT_HD * h:ATT_HD * (h + 1)] = dqs[rs]
                    dsink = dsink - jnp.sum(sd[rs], axis=0, keepdims=True) * (lane16 == h).astype(F32)
            dsink_ref[...] += dsink
            dq_ref[...] = _rope_t(dqbuf[...], tc)
            dkp = _rope_t(dkbuf[0:WINDOW, :], tp)
            dkc = _rope_t(dkbuf[WINDOW:2 * WINDOW, :], tc)

            @pl.when(n > 0)
            def _():
                dkv_ref[:, 0:D_KV] = carry[:, 0:D_KV] + dkp
                dkv_ref[:, D_KV:2 * D_KV] = carry[:, D_KV:2 * D_KV] + dvbuf[0:WINDOW, :]

            carry[:, 0:D_KV] = dkc
            carry[:, D_KV:2 * D_KV] = dvbuf[WINDOW:2 * WINDOW, :]

        @pl.when(n == nb)
        def _():
            dkv_ref[...] = carry[...]

    last = nb - 1
    cur = lambda wd: pl.BlockSpec((WINDOW, wd), lambda n: (jnp.minimum(n, last), 0))
    prv = lambda wd: pl.BlockSpec((WINDOW, wd), lambda n: (jnp.maximum(jnp.minimum(n, last) - 1, 0), 0))
    return pl.pallas_call(
        body, name="swa_bwd", grid=(nb + 1,),
        in_specs=[cur(D_ATT), cur(D_ATT), cur(D_ATT), cur(2 * D_KV), prv(2 * D_KV), cur(D_ATT), cur(ATT_QH), cur(1), prv(1),
                  pl.BlockSpec((1, 2 * ATT_HD), lambda n: (0, 0)), pl.BlockSpec((1, ATT_QH), lambda n: (0, 0))],
        out_specs=[cur(D_ATT), cur(D_ATT),
                   pl.BlockSpec((WINDOW, 2 * D_KV), lambda n: (jnp.maximum(n - 1, 0), 0)),
                   pl.BlockSpec((1, ATT_QH), lambda n: (0, 0))],
        out_shape=[jax.ShapeDtypeStruct((L, D_ATT), F32), jax.ShapeDtypeStruct((L, D_ATT), F32),
                   jax.ShapeDtypeStruct((L, 2 * D_KV), F32), jax.ShapeDtypeStruct((1, ATT_QH), F32)],
        scratch_shapes=[pltpu.VMEM((WINDOW, 2 * D_KV), F32), pltpu.VMEM((WINDOW, D_ATT), F32),
                        pltpu.VMEM((2 * WINDOW, D_KV), F32), pltpu.VMEM((2 * WINDOW, D_KV), F32)],
        compiler_params=_cparams(("arbitrary",)),
    )(dy, q, g, kv, kv, o, lse, pos, pos, inv, sinks)


def _out_ln_loss(y_ssd, y_att, x, target, w_out, ln_g, ln_b):
    L = x.shape[0]
    tm = ROW_TILE
    inv_d = 1.0 / D_MODEL

    def body(ys_ref, ya_ref, x_ref, t_ref, w_ref, g_ref, b_ref, dr_ref, dys_ref, dya_ref, loss_ref, gg_ref, gb_ref):
        i = pl.program_id(0)

        @pl.when(i == 0)
        def _():
            loss_ref[...] = jnp.zeros_like(loss_ref)
            gg_ref[...] = jnp.zeros_like(gg_ref)
            gb_ref[...] = jnp.zeros_like(gb_ref)

        h = _dot(_bf(ys_ref[...]), w_ref[0:D_SSD, :]) + _dot(_bf(ya_ref[...]), w_ref[D_SSD:D_MIX, :])
        r = ALPHA * x_ref[...] + h
        mu = jnp.mean(r, axis=-1, keepdims=True)
        xc = r - mu
        rstd = lax.rsqrt(jnp.mean(xc * xc, axis=-1, keepdims=True) + LN_EPS)
        xhat = xc * rstd
        gam = g_ref[...]
        diff = xhat * gam + b_ref[...] - t_ref[...]
        part = jnp.sum(jnp.sum(diff * diff, axis=-1, keepdims=True), axis=0, keepdims=True)
        loss_ref[...] += (0.5 * inv_d) * part
        dout = diff * inv_d
        gg_ref[...] += jnp.sum(dout * xhat, axis=0, keepdims=True)
        gb_ref[...] += jnp.sum(dout, axis=0, keepdims=True)
        dxh = dout * gam
        dr = rstd * (dxh - jnp.mean(dxh, axis=-1, keepdims=True) - xhat * jnp.mean(dxh * xhat, axis=-1, keepdims=True))
        dr_ref[...] = dr
        drb = _bf(dr)
        dys_ref[...] = _dot_nt(drb, w_ref[0:D_SSD, :])
        dya_ref[...] = _dot_nt(drb, w_ref[D_SSD:D_MIX, :])

    row = pl.BlockSpec((tm, D_MODEL), lambda i: (i, 0))
    vec = pl.BlockSpec((1, D_MODEL), lambda i: (0, 0))
    return pl.pallas_call(
        body, name="out_ln_loss", grid=(L // tm,),
        in_specs=[row, row, row, row, pl.BlockSpec((D_MIX, D_MODEL), lambda i: (0, 0), pipeline_mode=pl.Buffered(1)), vec, vec],
        out_specs=[row, row, row, pl.BlockSpec((1, 128), lambda i: (0, 0)), vec, vec],
        out_shape=[jax.ShapeDtypeStruct((L, D_MODEL), F32)] * 3 + [jax.ShapeDtypeStruct((1, 128), F32)]
        + [jax.ShapeDtypeStruct((1, D_MODEL), F32)] * 2,
        compiler_params=_cparams(("arbitrary",)),
    )(y_ssd, y_att, x, target, w_out, ln_g, ln_b)


def _local_step(x, pos, target, w, w_out, conv_w, conv_b, dt_bias, a_log, d_skip, norm_w, sinks, ln_g, ln_b):
    inv8 = ROPE_THETA ** (-jnp.arange(0, ROPE_DIM, 2, dtype=F32) / ROPE_DIM)
    inv = jnp.tile(jnp.concatenate([inv8, inv8, jnp.zeros((ATT_HD - ROPE_DIM,), F32)]), 2).reshape(1, 2 * ATT_HD)

    z, g, q, xbc, kv, dtp, xb = _in_proj(x, w, pos, inv)
    y_ssd, y_pre, prev = _ssd_fwd2(z, xbc, dtp, conv_w, conv_b, dt_bias, a_log, d_skip, norm_w)
    y_att, o, lse = _swa_fwd(q, g, kv, sinks)
    dr, dy_ssd, dy_att, loss, g_ln_g, g_ln_b = _out_ln_loss(y_ssd, y_att, x, target, w_out, ln_g, ln_b)
    gw_out_ssd = _matmul_tn(y_ssd, dr, 1024, "gw_out_ssd")
    gw_out_att = _matmul_tn(y_att, dr, 1024, "gw_out_att")
    dq, dg, dkv, g_sinks = _swa_bwd(dy_att, q, g, kv, o, lse, pos, inv, sinks)
    dz, dxbc, ddt, g_conv_w, g_conv_b, g_dt_bias, g_a_log, g_d_skip, g_norm_w = _ssd_bwd2(
        dy_ssd, z, y_pre, xbc, dtp, prev, conv_w, conv_b, dt_bias, a_log, d_skip, norm_w)
    grad_x = _grad_x(dr, dz, dg, dq, dxbc, dkv, ddt, w)
    gw_z = _matmul_tn(dz, xb, 1024, "gw_z")
    gw_g = _matmul_tn(dg, xb, 1024, "gw_g")
    gw_q = _matmul_tn(dq, xb, 1024, "gw_q")
    gw_xbc = _matmul_tn(dxbc, xb, 1024, "gw_xbc")
    gw_kv = _matmul_tn(dkv, xb, 1024, "gw_kv")
    gw_dt = _matmul_tn(ddt, xb, 1024, "gw_dt")
    gw_in = jnp.concatenate([gw_z, gw_xbc, gw_dt[0:SSD_HEADS], gw_q, gw_kv, gw_g], axis=0)
    gw_out = jnp.concatenate([gw_out_ssd, gw_out_att], axis=0)
    small = dict(conv_w=g_conv_w, conv_b=g_conv_b, dt_bias=g_dt_bias, a_log=g_a_log, d_skip=g_d_skip,
                 ssd_norm_w=g_norm_w, attn_sinks=g_sinks, ln_g=g_ln_g, ln_b=g_ln_b)
    return loss, grad_x, gw_in, gw_out, small


def _mesh_pos():
    return lax.axis_index("x"), lax.axis_index("y"), lax.axis_index("c")


def _gather_weights(w_in_s, w_out_s, conv_w_s):
    def body(win_ref, wout_ref, cw_ref, owin_ref, owout_ref, ocw_ref, send_sems, recv_sems, small_send, small_recv,
             local_sems):
        x, y, c = _mesh_pos()
        me = 2 * x + y
        sibling = (x, y, 1 - c)
        chips = [(1 - x, y), (x, 1 - y), (1 - x, 1 - y)]
        locals_ = [pltpu.make_async_copy(cw_ref, ocw_ref.at[me], local_sems.at[0])]
        for cp in locals_:
            cp.start()
        started = []
        for t, (src, dst) in enumerate(((win_ref, owin_ref), (wout_ref, owout_ref))):
            hr = src.shape[0] // 2

            def half(ref, hc, hr=hr):
                return ref.at[pl.ds(hc * hr, hr), :]

            for j, (px, py) in enumerate(chips):
                cp = pltpu.make_async_remote_copy(
                    src_ref=half(src, c), dst_ref=half(dst.at[me], c), send_sem=send_sems.at[t, j],
                    recv_sem=recv_sems.at[t, j], device_id=(px, py, c), device_id_type=MESH)
                cp.start()
                started.append(cp)
        for j, (px, py) in enumerate(chips):
            cp = pltpu.make_async_remote_copy(
                src_ref=cw_ref, dst_ref=ocw_ref.at[me], send_sem=small_send.at[j], recv_sem=small_recv.at[j],
                device_id=(px, py, c), device_id_type=MESH)
            cp.start()
            started.append(cp)
        for t, (src, dst) in enumerate(((win_ref, owin_ref), (wout_ref, owout_ref))):
            hr = src.shape[0] // 2
            for j, (px, py) in enumerate(chips):
                src_chip = 2 * px + py
                blk = dst.at[src_chip].at[pl.ds(c * hr, hr), :]
                pltpu.make_async_remote_copy(
                    src_ref=blk, dst_ref=blk, send_sem=send_sems.at[t, j], recv_sem=recv_sems.at[t, j],
                    device_id=(px, py, c), device_id_type=MESH).wait_recv()
                cp = pltpu.make_async_remote_copy(
                    src_ref=blk, dst_ref=blk, send_sem=send_sems.at[t, 3 + j], recv_sem=recv_sems.at[t, 3 + j],
                    device_id=sibling, device_id_type=MESH)
                cp.start()
                started.append(cp)
        for t, (src, dst) in enumerate(((win_ref, owin_ref), (wout_ref, owout_ref))):
            hr = src.shape[0] // 2
            for j, (px, py) in enumerate(chips):
                src_chip = 2 * px + py
                blk = dst.at[src_chip].at[pl.ds((1 - c) * hr, hr), :]
                pltpu.make_async_remote_copy(
                    src_ref=blk, dst_ref=blk, send_sem=send_sems.at[t, 3 + j], recv_sem=recv_sems.at[t, 3 + j],
                    device_id=sibling, device_id_type=MESH).wait_recv()
        for j in range(3):
            pltpu.make_async_remote_copy(
                src_ref=cw_ref, dst_ref=ocw_ref.at[me], send_sem=small_send.at[j], recv_sem=small_recv.at[j],
                device_id=sibling, device_id_type=MESH).wait_recv()
        for cp in started:
            cp.wait_send()
        for cp in locals_:
            cp.wait()

    any_spec = pl.BlockSpec(memory_space=pl.ANY)
    return pl.pallas_call(
        body, name="gather_weights",
        in_specs=[any_spec] * 3, out_specs=[any_spec] * 3,
        out_shape=[jax.ShapeDtypeStruct((N_CHIPS,) + a.shape, a.dtype) for a in (w_in_s, w_out_s, conv_w_s)],
        scratch_shapes=[pltpu.SemaphoreType.DMA((2, 6)), pltpu.SemaphoreType.DMA((2, 6)),
                        pltpu.SemaphoreType.DMA((3,)), pltpu.SemaphoreType.DMA((3,)), pltpu.SemaphoreType.DMA((3,))],
    )(w_in_s, w_out_s, conv_w_s)


def _pair_exchange(gw_in, gw_out, small):
    k_small = small.shape[1]

    def body(gin_ref, gout_ref, sm_ref, rin_ref, rout_ref, slots_ref, send_sems, recv_sems, small_send, small_recv,
             local_sem):
        x, y, c = _mesh_pos()
        me = 4 * x + 2 * y + c
        sibling = (x, y, 1 - c)
        mine = pltpu.make_async_copy(sm_ref, slots_ref.at[me], local_sem)
        mine.start()
        started = []
        for t, (src, dst) in enumerate(((gin_ref, rin_ref), (gout_ref, rout_ref))):
            hr = src.shape[1] // 2
            for j in range(N_CHIPS):
                cp = pltpu.make_async_remote_copy(
                    src_ref=src.at[j, pl.ds((1 - c) * hr, hr), :], dst_ref=dst.at[j], send_sem=send_sems.at[t, j],
                    recv_sem=recv_sems.at[t, j], device_id=sibling, device_id_type=MESH)
                cp.start()
                started.append(cp)
        for k in range(1, 8):
            peer = (x ^ ((k >> 2) & 1), y ^ ((k >> 1) & 1), c ^ (k & 1))
            cp = pltpu.make_async_remote_copy(
                src_ref=sm_ref, dst_ref=slots_ref.at[me], send_sem=small_send.at[k - 1], recv_sem=small_recv.at[k - 1],
                device_id=peer, device_id_type=MESH)
            cp.start()
            started.append(cp)
        for t, (src, dst) in enumerate(((gin_ref, rin_ref), (gout_ref, rout_ref))):
            for j in range(N_CHIPS):
                pltpu.make_async_remote_copy(
                    src_ref=dst.at[j], dst_ref=dst.at[j], send_sem=send_sems.at[t, j], recv_sem=recv_sems.at[t, j],
                    device_id=sibling, device_id_type=MESH).wait_recv()
        for k in range(1, 8):
            pltpu.make_async_remote_copy(
                src_ref=sm_ref, dst_ref=slots_ref.at[me], send_sem=small_send.at[k - 1], recv_sem=small_recv.at[k - 1],
                device_id=sibling, device_id_type=MESH).wait_recv()
        for cp in started:
            cp.wait_send()
        mine.wait()

    any_spec = pl.BlockSpec(memory_space=pl.ANY)
    half_in = jax.ShapeDtypeStruct((N_CHIPS, gw_in.shape[1] // 2, D_MODEL), F32)
    half_out = jax.ShapeDtypeStruct((N_CHIPS, gw_out.shape[1] // 2, D_MODEL), F32)
    return pl.pallas_call(
        body, name="pair_exchange",
        in_specs=[any_spec] * 3, out_specs=[any_spec] * 3,
        out_shape=[half_in, half_out, jax.ShapeDtypeStruct((8, 8, k_small), F32)],
        scratch_shapes=[pltpu.SemaphoreType.DMA((2, N_CHIPS)), pltpu.SemaphoreType.DMA((2, N_CHIPS)),
                        pltpu.SemaphoreType.DMA((7,)), pltpu.SemaphoreType.DMA((7,)), pltpu.SemaphoreType.DMA],
    )(gw_in, gw_out, small)


def _chip_exchange(s_in, s_out):
    def body(sin_ref, sout_ref, rin_ref, rout_ref, send_sems, recv_sems):
        x, y, c = _mesh_pos()
        me = 2 * x + y
        chips = [(1 - x, y), (x, 1 - y), (1 - x, 1 - y)]
        started = []
        for t, (src, dst) in enumerate(((sin_ref, rin_ref), (sout_ref, rout_ref))):
            for j, (px, py) in enumerate(chips):
                cp = pltpu.make_async_remote_copy(
                    src_ref=src.at[2 * px + py], dst_ref=dst.at[me], send_sem=send_sems.at[t, j],
                    recv_sem=recv_sems.at[t, j], device_id=(px, py, c), device_id_type=MESH)
                cp.start()
                started.append(cp)
        for t, (src, dst) in enumerate(((sin_ref, rin_ref), (sout_ref, rout_ref))):
            for j, (px, py) in enumerate(chips):
                blk = dst.at[2 * px + py]
                pltpu.make_async_remote_copy(
                    src_ref=blk, dst_ref=blk, send_sem=send_sems.at[t, j], recv_sem=recv_sems.at[t, j],
                    device_id=(px, py, c), device_id_type=MESH).wait_recv()
        for cp in started:
            cp.wait_send()

    any_spec = pl.BlockSpec(memory_space=pl.ANY)
    return pl.pallas_call(
        body, name="chip_exchange",
        in_specs=[any_spec] * 2, out_specs=[any_spec] * 2,
        out_shape=[jax.ShapeDtypeStruct(s_in.shape, s_in.dtype), jax.ShapeDtypeStruct(s_out.shape, s_out.dtype)],
        scratch_shapes=[pltpu.SemaphoreType.DMA((2, 3)), pltpu.SemaphoreType.DMA((2, 3))],
    )(s_in, s_out)


def _pair_share(h_in, h_out):
    def body(hin_ref, hout_ref, rin_ref, rout_ref, send_sems, recv_sems):
        x, y, c = _mesh_pos()
        sibling = (x, y, 1 - c)
        started = []
        for t, (src, dst) in enumerate(((hin_ref, rin_ref), (hout_ref, rout_ref))):
            cp = pltpu.make_async_remote_copy(
                src_ref=src, dst_ref=dst, send_sem=send_sems.at[t], recv_sem=recv_sems.at[t],
                device_id=sibling, device_id_type=MESH)
            cp.start()
            started.append(cp)
        for cp in started:
            cp.wait()

    any_spec = pl.BlockSpec(memory_space=pl.ANY)
    return pl.pallas_call(
        body, name="pair_share",
        in_specs=[any_spec] * 2, out_specs=[any_spec] * 2,
        out_shape=[jax.ShapeDtypeStruct(h_in.shape, F32), jax.ShapeDtypeStruct(h_out.shape, F32)],
        scratch_shapes=[pltpu.SemaphoreType.DMA((2,)), pltpu.SemaphoreType.DMA((2,))],
    )(h_in, h_out)


def _pair_add(g, recv, core, name):
    _, rows, C = recv.shape
    tc = 256

    def body(core_ref, g_ref, r_ref, o_ref):
        o_ref[...] = _bf(g_ref[...] + r_ref[...])

    spec = pl.BlockSpec((1, rows, tc), lambda j, i, core: (j, 0, i))
    return pl.pallas_call(
        body, name=name,
        grid_spec=pltpu.PrefetchScalarGridSpec(
            num_scalar_prefetch=1, grid=(N_CHIPS, C // tc),
            in_specs=[pl.BlockSpec((1, rows, tc), lambda j, i, core: (j, core[0], i)), spec], out_specs=spec),
        out_shape=jax.ShapeDtypeStruct((N_CHIPS, rows, C), BF16),
        compiler_params=_cparams(("parallel", "parallel")),
    )(core, g, recv)


def _chip_add(own, parts, chip, name):
    _, rows, C = parts.shape
    tc = 256

    def body(chip_ref, own_ref, r0, r1, r2, r3, o_ref):
        acc = None
        for j, r in enumerate((r0, r1, r2, r3)):
            term = jnp.where(chip_ref[0] == j, own_ref[0], r[0]).astype(F32)
            acc = term if acc is None else acc + term
        o_ref[...] = acc

    def slab(j):
        return pl.BlockSpec((1, rows, tc), lambda i, chip: (jnp.where(chip[0] == j, (j + 1) % N_CHIPS, j), 0, i))

    return pl.pallas_call(
        body, name=name,
        grid_spec=pltpu.PrefetchScalarGridSpec(
            num_scalar_prefetch=1, grid=(C // tc,),
            in_specs=[pl.BlockSpec((1, rows, tc), lambda i, chip: (chip[0], 0, i))] + [slab(j) for j in range(N_CHIPS)],
            out_specs=pl.BlockSpec((rows, tc), lambda i, chip: (0, i))),
        out_shape=jax.ShapeDtypeStruct((rows, C), F32),
        compiler_params=_cparams(("parallel",)),
    )(chip, own, parts, parts, parts, parts)


def _adamw_math(w, g, m, v):
    m = ADAM_B1 * m + (1.0 - ADAM_B1) * g
    v = ADAM_B2 * v + (1.0 - ADAM_B2) * (g * g)
    m_hat = m / (1.0 - ADAM_B1 ** ADAM_STEP)
    v_hat = v / (1.0 - ADAM_B2 ** ADAM_STEP)
    delta = -ADAM_LR * (m_hat / (jnp.sqrt(v_hat) + ADAM_EPS) + ADAM_WD * w)
    return delta, m, v


def _adamw_pair(w, g_own, g_sib, m, v, core, name):
    unit = w.ndim == 3
    R, C = w.shape[0], w.shape[-1]
    rows = g_own.shape[0]
    tc = 128

    def body(core_ref, w_ref, go_ref, gs_ref, m_ref, v_ref, d_ref, nm_ref, nv_ref, g_ref):
        first = core_ref[0] == 0
        own, sib = go_ref[...], gs_ref[...]
        g = jnp.concatenate([jnp.where(first, own, sib), jnp.where(first, sib, own)], axis=0)[0:R, :]
        idx = (slice(None), 0, slice(None)) if unit else (slice(None), slice(None))
        d, nm, nv = _adamw_math(w_ref[idx], g, m_ref[idx], v_ref[idx])
        d_ref[idx] = d
        nm_ref[idx] = nm
        nv_ref[idx] = nv
        g_ref[idx] = g

    if unit:
        spec = pl.BlockSpec((R, 1, tc), lambda i, core: (0, 0, i))
    else:
        spec = pl.BlockSpec((R, tc), lambda i, core: (0, i))
    gspec = pl.BlockSpec((rows, tc), lambda i, core: (0, i))
    return pl.pallas_call(
        body, name=name,
        grid_spec=pltpu.PrefetchScalarGridSpec(
            num_scalar_prefetch=1, grid=(C // tc,),
            in_specs=[spec, gspec, gspec, spec, spec], out_specs=[spec] * 4),
        out_shape=[jax.ShapeDtypeStruct(w.shape, F32)] * 4,
        compiler_params=_cparams(("parallel",)),
    )(core, w, g_own, g_sib, m, v)


SMALL_NAMES = ("conv_b", "ssd_norm_w", "ln_g", "ln_b", "dt_bias", "a_log", "d_skip", "attn_sinks")
SMALL_SIZES = (D_XBC, D_SSD, D_MODEL, D_MODEL, SSD_HEADS, SSD_HEADS, SSD_HEADS, ATT_QH)
SMALL_OFFS = tuple(D_XBC + sum(-(-n // 128) * 128 for n in SMALL_SIZES[:k]) for k in range(len(SMALL_SIZES)))
LOSS_OFF = D_XBC + sum(-(-n // 128) * 128 for n in SMALL_SIZES)
K_SMALL = LOSS_OFF + 128


def _pack_small(g_conv_w, vecs, loss):
    def body(cw_ref, *refs):
        o_ref = refs[-1]
        o_ref[...] = jnp.zeros_like(o_ref)
        o_ref[0:CONV_K, 0:D_XBC] = cw_ref[...]
        for v_ref, off, n in zip(refs[:-2], SMALL_OFFS, SMALL_SIZES):
            o_ref[0:1, off:off + n] = v_ref[...]
        o_ref[0:1, LOSS_OFF:LOSS_OFF + 128] = refs[-2][...]

    return pl.pallas_call(
        body, name="pack_small", out_shape=jax.ShapeDtypeStruct((8, K_SMALL), F32), compiler_params=_cparams(),
    )(g_conv_w, *vecs, loss)


def _adamw_small(slots, chip, conv_w, m_conv_w, v_conv_w, params, moms, vars_):
    n_vec = len(SMALL_NAMES)

    def body(chip_ref, s_ref, *refs):
        ins = refs[:3 * (n_vec + 1)]
        outs = refs[3 * (n_vec + 1):-1]
        tot_ref = refs[-1]
        tot = s_ref[0]
        for d in range(1, 8):
            tot = tot + s_ref[d]
        outs[0][...] = tot[0:1, LOSS_OFF:LOSS_OFF + 1]
        off = pl.multiple_of(chip_ref[0] * CONV_COLS, 128)
        tot_ref[...] = tot
        grads = [tot_ref[0:CONV_K, pl.ds(off, CONV_COLS)]]
        grads += [tot[0:1, o:o + n] for o, n in zip(SMALL_OFFS, SMALL_SIZES)]
        for k, g in enumerate(grads):
            w_ref, m_ref, v_ref = ins[3 * k:3 * k + 3]
            full = (0,) if k == 0 else (Ellipsis,)
            d, nm, nv = _adamw_math(w_ref[full], g, m_ref[full], v_ref[full])
            for o_ref, val in zip(outs[1 + 4 * k:5 + 4 * k], (g, d, nm, nv)):
                o_ref[full] = val

    args = [conv_w, m_conv_w, v_conv_w]
    for w, m, v in zip(params, moms, vars_):
        args += [w, m, v]
    shapes = [jax.ShapeDtypeStruct((1, 1), F32)] + [jax.ShapeDtypeStruct(conv_w.shape, F32)] * 4
    for w in params:
        shapes += [jax.ShapeDtypeStruct(w.shape, F32)] * 4
    vmem = pl.BlockSpec(memory_space=pltpu.VMEM)
    return pl.pallas_call(
        body, name="adamw_small",
        grid_spec=pltpu.PrefetchScalarGridSpec(
            num_scalar_prefetch=1, grid=(1,),
            in_specs=[pl.BlockSpec(slots.shape, lambda i, chip: (0, 0, 0))] + [vmem] * len(args),
            out_specs=[vmem] * len(shapes), scratch_shapes=[pltpu.VMEM((8, K_SMALL), F32)]),
        out_shape=shapes, compiler_params=_cparams(),
    )(chip, slots, *args)


def kernel(x, positions, w_in, conv_w, conv_b, dt_bias, a_log, d_skip, ssd_norm_w, attn_sinks, w_out, ln_g, ln_b, loss_target, m_w_in, m_conv_w, m_conv_b, m_dt_bias, m_a_log, m_d_skip, m_ssd_norm_w, m_attn_sinks, m_w_out, m_ln_g, m_ln_b, v_w_in, v_conv_w, v_conv_b, v_dt_bias, v_a_log, v_d_skip, v_ssd_norm_w, v_attn_sinks, v_w_out, v_ln_g, v_ln_b):
    mx, my, mc = _mesh_pos()
    chip = 2 * mx + my
    L = x.shape[1]

    conv_w_s8 = jnp.pad(conv_w[0], ((0, 8 - CONV_K), (0, 0)))
    pad_rows = ((0, SLAB_ROWS - W_IN_COLS), (0, 0))
    w_in_t = w_in[0].T
    w_in_b, w_out_b = jnp.pad(_bf(w_in_t), pad_rows), _bf(w_out[0])
    ag_in, ag_out, ag_cw = _gather_weights(w_in_b, w_out_b, conv_w_s8)
    ag_in = jnp.where((jnp.arange(N_CHIPS) == chip)[:, None, None], w_in_b[None], ag_in)
    ag_out = jnp.where((jnp.arange(N_CHIPS) == chip)[:, None, None], w_out_b[None], ag_out)
    w_full = jnp.concatenate([ag_in[j, 0:W_IN_COLS] for j in range(N_CHIPS)], axis=0)
    w = jnp.concatenate([
        w_full[O_Z:O_Z + D_SSD], w_full[O_G:O_G + D_ATT], w_full[O_Q:O_Q + D_ATT],
        w_full[O_XBC:O_XBC + D_XBC], w_full[O_K:O_K + 2 * D_KV], w_full[O_DT:O_DT + SSD_HEADS],
        jnp.zeros((DT_PAD - SSD_HEADS, D_MODEL), BF16)], axis=0)
    w_out_full = ag_out.reshape(D_MIX, D_MODEL)
    conv_w_full = jnp.concatenate([ag_cw[j, 0:CONV_K] for j in range(N_CHIPS)], axis=1)

    loss_part, grad_x, gw_in, gw_out, small = _local_step(
        x[0], positions[0].reshape(L, 1), loss_target[0], w, w_out_full, conv_w_full, conv_b, dt_bias, a_log, d_skip,
        ssd_norm_w, attn_sinks, ln_g, ln_b)

    packed = _pack_small(small["conv_w"], [small[n] for n in SMALL_NAMES], loss_part)

    gw_in_slabs = jnp.stack([jnp.pad(gw_in[W_IN_COLS * j:W_IN_COLS * (j + 1)], pad_rows) for j in range(N_CHIPS)])
    gw_out_slabs = gw_out.reshape(N_CHIPS, W_OUT_ROWS, D_MODEL)
    core_id = mc.reshape(1).astype(jnp.int32)
    chip_id = chip.reshape(1).astype(jnp.int32)
    recv_in, recv_out, slots = _pair_exchange(gw_in_slabs, gw_out_slabs, packed)
    s_in = _pair_add(gw_in_slabs, recv_in, core_id, "pair_add_in")
    s_out = _pair_add(gw_out_slabs, recv_out, core_id, "pair_add_out")
    r_in, r_out = _chip_exchange(s_in, s_out)
    h_in = _chip_add(s_in, r_in, chip_id, "chip_add_in")
    h_out = _chip_add(s_out, r_out, chip_id, "chip_add_out")
    sib_in, sib_out = _pair_share(h_in, h_out)

    to_rows = lambda a: jnp.transpose(a, (2, 0, 1))
    in_t = _adamw_pair(to_rows(w_in), h_in, sib_in, to_rows(m_w_in), to_rows(v_w_in), core_id, "adamw_w_in")
    d_w_in, nm_w_in, nv_w_in, g_w_in = [jnp.transpose(a, (1, 2, 0)) for a in in_t]
    out_t = _adamw_pair(w_out[0], h_out, sib_out, m_w_out[0], v_w_out[0], core_id, "adamw_w_out")
    d_w_out, nm_w_out, nv_w_out, g_w_out = [a[None] for a in out_t]

    params = dict(conv_b=conv_b, ssd_norm_w=ssd_norm_w, ln_g=ln_g, ln_b=ln_b, dt_bias=dt_bias, a_log=a_log,
                  d_skip=d_skip, attn_sinks=attn_sinks)
    moms = dict(conv_b=m_conv_b, ssd_norm_w=m_ssd_norm_w, ln_g=m_ln_g, ln_b=m_ln_b, dt_bias=m_dt_bias, a_log=m_a_log,
                d_skip=m_d_skip, attn_sinks=m_attn_sinks)
    vars_ = dict(conv_b=v_conv_b, ssd_norm_w=v_ssd_norm_w, ln_g=v_ln_g, ln_b=v_ln_b, dt_bias=v_dt_bias, a_log=v_a_log,
                 d_skip=v_d_skip, attn_sinks=v_attn_sinks)
    res = _adamw_small(slots, chip_id, conv_w, m_conv_w, v_conv_w, [params[n] for n in SMALL_NAMES],
                       [moms[n] for n in SMALL_NAMES], [vars_[n] for n in SMALL_NAMES])
    loss = res[0][0, 0]
    grads, delta, new_m, new_v = {}, {}, {}, {}
    for k, n in enumerate(("conv_w",) + SMALL_NAMES):
        grads[n], delta[n], new_m[n], new_v[n] = res[1 + 4 * k:5 + 4 * k]
    for dd, a_in, a_out in ((grads, g_w_in, g_w_out), (delta, d_w_in, d_w_out), (new_m, nm_w_in, nm_w_out),
                            (new_v, nv_w_in, nv_w_out)):
        dd["w_in"] = a_in
        dd["w_out"] = a_out
    order = ("w_in", "conv_w", "conv_b", "dt_bias", "a_log", "d_skip", "ssd_norm_w", "attn_sinks", "w_out", "ln_g", "ln_b")
    return (loss, grad_x[None], *[grads[n] for n in order], *[delta[n] for n in order], *[new_m[n] for n in order],
            *[new_v[n] for n in order])
```

```python
import functools

import numpy as np
import jax
import jax.numpy as jnp
from jax import lax
from jax.experimental import pallas as pl
from jax.experimental.pallas import tpu as pltpu

F32 = jnp.float32
BF16 = jnp.bfloat16
MESH = pl.DeviceIdType.MESH

D_MODEL = 1024
D_SSD = 1024
D_ATT = 1024
D_MIX = 2048
SSD_HEADS = 16
SSD_P = 64
SSD_GROUPS = 2
SSD_R = 8
SSD_N = 128
D_BC = 256
D_XBC = 1536
CONV_K = 4
CHUNK = 128
ATT_HD = 64
ATT_QH = 16
ATT_KVH = 4
ATT_R = 4
D_KV = 256
WINDOW = 128
ROPE_THETA = 500000.0
ROPE_DIM = 16
ALPHA = 2.0 ** 0.25
LN_EPS = 1e-5
RMS_EPS = 1e-5
D_IN_PROJ = 5136
O_Z, O_XBC, O_DT, O_Q, O_K, O_V, O_G = 0, 1024, 2560, 2576, 3600, 3856, 4112
P_Z, P_G, P_Q, P_XBC, P_KV, P_DT, P_END = 0, 1024, 2048, 3072, 4608, 5120, 5248
DT_PAD = 128
N_CHIPS = 4
W_IN_COLS = D_IN_PROJ // N_CHIPS
SLAB_ROWS = 1312
W_OUT_ROWS = D_MIX // N_CHIPS
CONV_COLS = D_XBC // N_CHIPS

ADAM_LR = 0.001
ADAM_B1 = 0.9
ADAM_B2 = 0.999
ADAM_EPS = 1e-08
ADAM_WD = 0.01
ADAM_STEP = 10

VMEM_LIMIT = 56 * 1024 * 1024
ROW_TILE = 512
NEG_BIG = -1e30
HI = lax.Precision.HIGHEST


def _cparams(sem=None, **kw):
    if sem is not None:
        kw["dimension_semantics"] = sem
    return pltpu.CompilerParams(vmem_limit_bytes=VMEM_LIMIT, **kw)


def _dot(a, b):
    return jnp.dot(a, b, preferred_element_type=F32)


def _dot_nt(a, b):
    return lax.dot_general(a, b, (((1,), (1,)), ((), ())), preferred_element_type=F32)


def _dot_tn(a, b):
    return lax.dot_general(a, b, (((0,), (0,)), ((), ())), preferred_element_type=F32)


def _bf(a):
    return a.astype(BF16)


def _iota2(shape, dim):
    return lax.broadcasted_iota(jnp.int32, shape, dim)


def _to_rows(col):
    k = col.shape[1]
    eye = (_iota2((k, k), 0) == _iota2((k, k), 1)).astype(F32)
    return lax.dot_general(eye, col, (((1,), (1,)), ((), ())), preferred_element_type=F32, precision=HI)


def _to_cols(row):
    n = row.shape[1]
    eye = (_iota2((n, n), 0) == _iota2((n, n), 1)).astype(F32)
    return lax.dot_general(eye, row, (((1,), (1,)), ((), ())), preferred_element_type=F32, precision=HI)


def _sigmoid(x):
    return jax.nn.sigmoid(x)


def _in_proj(x, w, pos, inv):
    L = x.shape[0]
    tm = ROW_TILE
    widths = (D_SSD, D_ATT, D_ATT, D_XBC, 2 * D_KV, DT_PAD)

    def body(x_ref, w_ref, pos_ref, inv_ref, z_ref, g_ref, q_ref, xbc_ref, kv_ref, dt_ref, xb_ref):
        xb = _bf(x_ref[...])
        xb_ref[...] = xb
        for o_ref, off, wd in zip((z_ref, g_ref, xbc_ref, dt_ref), (P_Z, P_G, P_XBC, P_DT), (D_SSD, D_ATT, D_XBC, DT_PAD)):
            o_ref[...] = _dot_nt(xb, w_ref[off:off + wd, :])
        tabs = _rope_tables(pos_ref, inv_ref)
        q_ref[...] = _rope(_dot_nt(xb, w_ref[P_Q:P_Q + D_ATT, :]), tabs)
        kv_ref[:, 0:D_KV] = _rope(_dot_nt(xb, w_ref[P_KV:P_KV + D_KV, :]), tabs)
        kv_ref[:, D_KV:2 * D_KV] = _dot_nt(xb, w_ref[P_KV + D_KV:P_KV + 2 * D_KV, :])

    row = lambda wd: pl.BlockSpec((tm, wd), lambda i: (i, 0))
    return pl.pallas_call(
        body, name="in_proj", grid=(L // tm,),
        in_specs=[row(D_MODEL), pl.BlockSpec((P_END, D_MODEL), lambda i: (0, 0), pipeline_mode=pl.Buffered(1)), row(1),
                  pl.BlockSpec((1, 2 * ATT_HD), lambda i: (0, 0))],
        out_specs=[row(wd) for wd in widths] + [row(D_MODEL)],
        out_shape=[jax.ShapeDtypeStruct((L, wd), F32) for wd in widths] + [jax.ShapeDtypeStruct((L, D_MODEL), BF16)],
        compiler_params=_cparams(("parallel",)),
    )(x, w, pos, inv)


def _matmul_tn(a, b, tn, name):
    K, M = a.shape
    N = b.shape[1]
    tk = min(K, 2048 if M <= 1024 else 1024)
    nk = K // tk

    def body(a_ref, b_ref, o_ref, acc_ref):
        k = pl.program_id(1)

        @pl.when(k == 0)
        def _():
            acc_ref[...] = jnp.zeros_like(acc_ref)

        acc_ref[...] += _dot_tn(_bf(a_ref[...]), _bf(b_ref[...]))

        @pl.when(k == nk - 1)
        def _():
            o_ref[...] = acc_ref[...]

    return pl.pallas_call(
        body, name=name, grid=(N // tn, nk),
        in_specs=[pl.BlockSpec((tk, M), lambda j, k: (k, 0)), pl.BlockSpec((tk, tn), lambda j, k: (k, j))],
        out_specs=pl.BlockSpec((M, tn), lambda j, k: (0, j)),
        out_shape=jax.ShapeDtypeStruct((M, N), F32),
        scratch_shapes=[pltpu.VMEM((M, tn), F32)],
        compiler_params=_cparams(("parallel", "arbitrary")),
    )(a, b)


def _grad_x(dr, dz, dg, dq, dxbc, dkv, ddt, w):
    L = dr.shape[0]
    tm = ROW_TILE
    widths = (D_SSD, D_ATT, D_ATT, D_XBC, 2 * D_KV, DT_PAD)
    offs = (P_Z, P_G, P_Q, P_XBC, P_KV, P_DT)

    def body(dr_ref, dz_ref, dg_ref, dq_ref, dxbc_ref, dkv_ref, ddt_ref, w_ref, o_ref):
        acc = ALPHA * dr_ref[...]
        for p_ref, off, wd in zip((dz_ref, dg_ref, dq_ref, dxbc_ref, dkv_ref, ddt_ref), offs, widths):
            acc = acc + _dot(_bf(p_ref[...]), w_ref[off:off + wd, :])
        o_ref[...] = acc

    row = lambda wd: pl.BlockSpec((tm, wd), lambda i: (i, 0))
    return pl.pallas_call(
        body, name="grad_x", grid=(L // tm,),
        in_specs=[row(D_MODEL)] + [row(wd) for wd in widths] + [pl.BlockSpec((P_END, D_MODEL), lambda i: (0, 0), pipeline_mode=pl.Buffered(1))],
        out_specs=row(D_MODEL),
        out_shape=jax.ShapeDtypeStruct((L, D_MODEL), F32),
        compiler_params=_cparams(("parallel",)),
    )(dr, dz, dg, dq, dxbc, dkv, ddt, w)


def _ssd_chunk_pre(first, xbc_ref, tail_ref, dt_ref, cw_ref, cb_ref, dtb_ref, alog_ref, ext):
    tail = jnp.where(first, 0.0, tail_ref[...])
    ext[0:8, :] = tail
    ext[8:8 + CHUNK, :] = xbc_ref[...]
    u = cb_ref[...] + cw_ref[0:1, :] * ext[pl.ds(5, CHUNK), :]
    for k in range(1, CONV_K):
        u = u + cw_ref[k:k + 1, :] * ext[pl.ds(5 + k, CHUNK), :]
    sig = _sigmoid(u)
    xbc = u * sig
    dtraw = dt_ref[:, 0:SSD_HEADS] + dtb_ref[...]
    dt = jax.nn.softplus(dtraw)
    A = -jnp.exp(alog_ref[...])
    a = dt * A
    tril = (_iota2((CHUNK, CHUNK), 0) >= _iota2((CHUNK, CHUNK), 1)).astype(F32)
    acs = jnp.dot(tril, a, preferred_element_type=F32, precision=HI)
    acs_row = _to_rows(acs)
    return u, sig, xbc, dtraw, dt, A, acs, acs_row


def _ssd_fwd(z, xbc, dtp, conv_w, conv_b, dt_bias, a_log, d_skip, norm_w):
    L = z.shape[0]
    nc = L // CHUNK

    def body(z_ref, xbc_ref, tail_ref, dt_ref, cw_ref, cb_ref, dtb_ref, alog_ref, dsk_ref, nw_ref,
             y_ref, ypre_ref, prev_ref, state, ext, ybuf):
        c = pl.program_id(0)

        @pl.when(c == 0)
        def _():
            state[...] = jnp.zeros_like(state)

        u, sig, xbcv, dtraw, dt, A, acs, acs_row = _ssd_chunk_pre(
            c == 0, xbc_ref, tail_ref, dt_ref, cw_ref, cb_ref, dtb_ref, alog_ref, ext)
        prev_ref[0] = state[...]
        causal = _iota2((CHUNK, CHUNK), 0) >= _iota2((CHUNK, CHUNK), 1)
        alast = acs[CHUNK - 1:CHUNK, :]
        for g in range(SSD_GROUPS):
            Bg = _bf(xbcv[:, D_SSD + SSD_N * g:D_SSD + SSD_N * (g + 1)])
            Cg = _bf(xbcv[:, D_SSD + D_BC + SSD_N * g:D_SSD + D_BC + SSD_N * (g + 1)])
            cb = _dot_nt(Cg, Bg)
            for r in range(SSD_R):
                h = g * SSD_R + r
                hs = slice(SSD_P * h, SSD_P * (h + 1))
                acs_c = acs[:, h:h + 1]
                seg = acs_c - acs_row[h:h + 1, :]
                Lm = jnp.where(causal, jnp.exp(jnp.where(causal, seg, 0.0)), 0.0)
                M = cb * Lm
                xh = xbcv[:, hs]
                X = xh * dt[:, h:h + 1]
                prev_h = state[hs, :]
                ydiag = _dot(_bf(M), _bf(X))
                yoff = _dot_nt(Cg, _bf(prev_h)) * jnp.exp(acs_c)
                al = alast[:, h:h + 1]
                Xd = X * jnp.exp(al - acs_c)
                state[hs, :] = prev_h * jnp.exp(al) + _dot_tn(_bf(Xd), Bg)
                ybuf[:, hs] = ydiag + yoff + dsk_ref[:, h:h + 1] * xh
        y = ybuf[...]
        ypre_ref[...] = y
        zv = z_ref[...]
        yf = y * (zv * _sigmoid(zv))
        half = D_SSD // SSD_GROUPS
        for g in range(SSD_GROUPS):
            gs = slice(half * g, half * (g + 1))
            yg = yf[:, gs]
            ms = jnp.mean(yg * yg, axis=-1, keepdims=True)
            y_ref[:, gs] = yg * lax.rsqrt(ms + RMS_EPS) * nw_ref[:, gs]

    full = lambda shape: pl.BlockSpec(shape, lambda c: (0, 0))
    return pl.pallas_call(
        body, name="ssd_fwd", grid=(nc,),
        in_specs=[
            pl.BlockSpec((CHUNK, D_SSD), lambda c: (c, 0)),
            pl.BlockSpec((CHUNK, D_XBC), lambda c: (c, 0)),
            pl.BlockSpec((8, D_XBC), lambda c: (jnp.maximum(c * (CHUNK // 8) - 1, 0), 0)),
            pl.BlockSpec((CHUNK, DT_PAD), lambda c: (c, 0)),
            full((CONV_K, D_XBC)), full((1, D_XBC)), full((1, SSD_HEADS)), full((1, SSD_HEADS)), full((1, SSD_HEADS)),
            full((1, D_SSD)),
        ],
        out_specs=[
            pl.BlockSpec((CHUNK, D_SSD), lambda c: (c, 0)),
            pl.BlockSpec((CHUNK, D_SSD), lambda c: (c, 0)),
            pl.BlockSpec((1, SSD_HEADS * SSD_P, SSD_N), lambda c: (c, 0, 0)),
        ],
        out_shape=[
            jax.ShapeDtypeStruct((L, D_SSD), F32),
            jax.ShapeDtypeStruct((L, D_SSD), F32),
            jax.ShapeDtypeStruct((nc, SSD_HEADS * SSD_P, SSD_N), F32),
        ],
        scratch_shapes=[
            pltpu.VMEM((SSD_HEADS * SSD_P, SSD_N), F32),
            pltpu.VMEM((CHUNK + 8, D_XBC), F32),
            pltpu.VMEM((CHUNK, D_SSD), F32),
        ],
        compiler_params=_cparams(("arbitrary",)),
    )(z, xbc, xbc, dtp, conv_w, conv_b, dt_bias, a_log, d_skip, norm_w)


def _ssd_bwd(dy, z, ypre, xbc, dtp, prev, conv_w, conv_b, dt_bias, a_log, d_skip, norm_w):
    L = z.shape[0]
    nc = L // CHUNK

    def body(dy_ref, z_ref, ypre_ref, xbc_ref, tail_ref, dt_ref, prev_ref, cw_ref, cb_ref, dtb_ref, alog_ref, dsk_ref,
             nw_ref, dz_ref, dxbc_ref, ddt_ref, gcw_ref, gcb_ref, gdtb_ref, galog_ref, gdsk_ref, gnw_ref,
             dstate, dhead, ext, ext2, dpost):
        i = pl.program_id(0)
        c = nc - 1 - i

        @pl.when(i == 0)
        def _():
            dstate[...] = jnp.zeros_like(dstate)
            dhead[...] = jnp.zeros_like(dhead)
            gcw_ref[...] = jnp.zeros_like(gcw_ref)
            gcb_ref[...] = jnp.zeros_like(gcb_ref)
            gdtb_ref[...] = jnp.zeros_like(gdtb_ref)
            galog_ref[...] = jnp.zeros_like(galog_ref)
            gdsk_ref[...] = jnp.zeros_like(gdsk_ref)
            gnw_ref[...] = jnp.zeros_like(gnw_ref)

        u, sig, xbcv, dtraw, dt, A, acs, acs_row = _ssd_chunk_pre(
            c == 0, xbc_ref, tail_ref, dt_ref, cw_ref, cb_ref, dtb_ref, alog_ref, ext)

        zv = z_ref[...]
        ypre = ypre_ref[...]
        dyn = dy_ref[...]
        sz = _sigmoid(zv)
        silu_z = zv * sz
        yf = ypre * silu_z
        half = D_SSD // SSD_GROUPS
        dyf_parts = []
        for g in range(SSD_GROUPS):
            gs = slice(half * g, half * (g + 1))
            yg = yf[:, gs]
            rstd = lax.rsqrt(jnp.mean(yg * yg, axis=-1, keepdims=True) + RMS_EPS)
            dout = dyn[:, gs]
            gnw_ref[:, gs] += jnp.sum(dout * yg * rstd, axis=0, keepdims=True)
            dyhat = dout * nw_ref[:, gs]
            dyf_parts.append(rstd * (dyhat - yg * (rstd * rstd) * jnp.mean(dyhat * yg, axis=-1, keepdims=True)))
        dyf = jnp.concatenate(dyf_parts, axis=1)
        dz_ref[...] = dyf * ypre * (sz * (1.0 + zv * (1.0 - sz)))
        dypre = dyf * silu_z

        causal = _iota2((CHUNK, CHUNK), 0) >= _iota2((CHUNK, CHUNK), 1)
        alast = acs[CHUNK - 1:CHUNK, :]
        lane16 = _iota2((1, SSD_HEADS), 1)
        sub16 = _iota2((SSD_HEADS, 1), 0)
        dacs_col = jnp.zeros((CHUNK, SSD_HEADS), F32)
        dacs_row = jnp.zeros((SSD_HEADS, CHUNK), F32)
        ddt_col = jnp.zeros((CHUNK, SSD_HEADS), F32)
        dalast = jnp.zeros((1, SSD_HEADS), F32)
        gdsk = jnp.zeros((1, SSD_HEADS), F32)
        for g in range(SSD_GROUPS):
            bs = slice(D_SSD + SSD_N * g, D_SSD + SSD_N * (g + 1))
            cs = slice(D_SSD + D_BC + SSD_N * g, D_SSD + D_BC + SSD_N * (g + 1))
            Bg = _bf(xbcv[:, bs])
            Cg = _bf(xbcv[:, cs])
            cb = _dot_nt(Cg, Bg)
            dcb = jnp.zeros((CHUNK, CHUNK), F32)
            dB = jnp.zeros((CHUNK, SSD_N), F32)
            dC = jnp.zeros((CHUNK, SSD_N), F32)
            for r in range(SSD_R):
                h = g * SSD_R + r
                hs = slice(SSD_P * h, SSD_P * (h + 1))
                onehot = (lane16 == h).astype(F32)
                acs_c = acs[:, h:h + 1]
                seg = acs_c - acs_row[h:h + 1, :]
                Lm = jnp.where(causal, jnp.exp(jnp.where(causal, seg, 0.0)), 0.0)
                M = cb * Lm
                xh = xbcv[:, hs]
                dth = dt[:, h:h + 1]
                X = xh * dth
                Xb = _bf(X)
                dyh = dypre[:, hs]
                dyb = _bf(dyh)
                prev_h = prev_ref[0, hs, :]
                prevb = _bf(prev_h)
                dnext = dstate[hs, :]
                dnextb = _bf(dnext)
                al = alast[:, h:h + 1]
                eacs = jnp.exp(acs_c)
                eal = jnp.exp(al)
                dsd = jnp.exp(al - acs_c)
                G = _bf(dyh * eacs)
                dstate[hs, :] = dnext * eal + _dot_tn(G, Cg)
                dC = dC + _dot(G, prevb)
                yoff = _dot_nt(Cg, prevb) * eacs
                dacs_h = jnp.sum(dyh * yoff, axis=-1, keepdims=True)
                BdN = _dot_nt(Bg, dnextb)
                dX = dsd * BdN
                dB = dB + _dot(_bf(X * dsd), dnextb)
                t = jnp.sum(X * BdN, axis=-1, keepdims=True) * dsd
                dacs_h = dacs_h - t
                dal = jnp.sum(t, axis=0, keepdims=True) + jnp.sum(
                    jnp.sum(dnext * prev_h, axis=-1, keepdims=True), axis=0, keepdims=True) * eal
                dM = _dot_nt(dyb, Xb)
                dX = dX + _dot_tn(_bf(M), dyb)
                dseg = dM * M
                dcb = dcb + dM * Lm
                dacs_h = dacs_h + jnp.sum(dseg, axis=-1, keepdims=True)
                dacs_row = dacs_row - jnp.sum(dseg, axis=0, keepdims=True) * (sub16 == h).astype(F32)
                dacs_col = dacs_col + dacs_h * onehot
                dalast = dalast + dal * onehot
                ddt_col = ddt_col + jnp.sum(dX * xh, axis=-1, keepdims=True) * onehot
                gdsk = gdsk + jnp.sum(jnp.sum(dyh * xh, axis=-1, keepdims=True), axis=0, keepdims=True) * onehot
                dpost[:, hs] = dX * dth + dsk_ref[:, h:h + 1] * dyh
            dcbb = _bf(dcb)
            dpost[:, bs] = dB + _dot_tn(dcbb, Cg)
            dpost[:, cs] = dC + _dot(dcbb, Bg)

        is_last = (_iota2((CHUNK, 1), 0) == CHUNK - 1).astype(F32)
        dacs = dacs_col + _to_cols(dacs_row) + is_last * dalast
        triu = (_iota2((CHUNK, CHUNK), 0) <= _iota2((CHUNK, CHUNK), 1)).astype(F32)
        da = jnp.dot(triu, dacs, preferred_element_type=F32, precision=HI)
        ddt_tot = ddt_col + da * A
        galog_ref[...] += jnp.sum(da * dt, axis=0, keepdims=True) * A
        ddtraw = ddt_tot * _sigmoid(dtraw)
        gdtb_ref[...] += jnp.sum(ddtraw, axis=0, keepdims=True)
        gdsk_ref[...] += gdsk
        ddt_ref[...] = jnp.zeros_like(ddt_ref)
        ddt_ref[:, 0:SSD_HEADS] = ddtraw

        dconv = dpost[...] * (sig * (1.0 + u * (1.0 - sig)))
        gcb_ref[...] += jnp.sum(dconv, axis=0, keepdims=True)
        for k in range(CONV_K):
            gcw_ref[k:k + 1, :] += jnp.sum(dconv * ext[pl.ds(5 + k, CHUNK), :], axis=0, keepdims=True)
        ext2[0:CHUNK, :] = dconv
        ext2[CHUNK:CHUNK + 8, :] = dhead[...]
        dx = cw_ref[CONV_K - 1:CONV_K, :] * dconv
        for k in range(CONV_K - 1):
            dx = dx + cw_ref[k:k + 1, :] * ext2[pl.ds(CONV_K - 1 - k, CHUNK), :]
        dxbc_ref[...] = dx
        dhead[...] = dconv[0:8, :]

    full = lambda shape: pl.BlockSpec(shape, lambda i: (0, 0))
    rev = lambda wd: pl.BlockSpec((CHUNK, wd), lambda i: (nc - 1 - i, 0))
    return pl.pallas_call(
        body, name="ssd_bwd", grid=(nc,),
        in_specs=[
            rev(D_SSD), rev(D_SSD), rev(D_SSD), rev(D_XBC),
            pl.BlockSpec((8, D_XBC), lambda i: (jnp.maximum((nc - 1 - i) * (CHUNK // 8) - 1, 0), 0)),
            rev(DT_PAD),
            pl.BlockSpec((1, SSD_HEADS * SSD_P, SSD_N), lambda i: (nc - 1 - i, 0, 0)),
            full((CONV_K, D_XBC)), full((1, D_XBC)), full((1, SSD_HEADS)), full((1, SSD_HEADS)), full((1, SSD_HEADS)),
            full((1, D_SSD)),
        ],
        out_specs=[
            rev(D_SSD), rev(D_XBC), rev(DT_PAD),
            full((CONV_K, D_XBC)), full((1, D_XBC)), full((1, SSD_HEADS)), full((1, SSD_HEADS)), full((1, SSD_HEADS)),
            full((1, D_SSD)),
        ],
        out_shape=[
            jax.ShapeDtypeStruct((L, D_SSD), F32), jax.ShapeDtypeStruct((L, D_XBC), F32),
            jax.ShapeDtypeStruct((L, DT_PAD), F32),
            jax.ShapeDtypeStruct((CONV_K, D_XBC), F32), jax.ShapeDtypeStruct((1, D_XBC), F32),
            jax.ShapeDtypeStruct((1, SSD_HEADS), F32), jax.ShapeDtypeStruct((1, SSD_HEADS), F32),
            jax.ShapeDtypeStruct((1, SSD_HEADS), F32), jax.ShapeDtypeStruct((1, D_SSD), F32),
        ],
        scratch_shapes=[
            pltpu.VMEM((SSD_HEADS * SSD_P, SSD_N), F32),
            pltpu.VMEM((8, D_XBC), F32),
            pltpu.VMEM((CHUNK + 8, D_XBC), F32),
            pltpu.VMEM((CHUNK + 8, D_XBC), F32),
            pltpu.VMEM((CHUNK, D_XBC), F32),
        ],
        compiler_params=_cparams(("arbitrary",)),
    )(dy, z, ypre, xbc, xbc, dtp, prev, conv_w, conv_b, dt_bias, a_log, d_skip, norm_w)


def _head_expander():
    return (_iota2((SSD_HEADS, D_SSD), 1) // SSD_P == _iota2((SSD_HEADS, D_SSD), 0)).astype(BF16)


def _hi_lo(x):
    hi = _bf(x)
    return hi, _bf(x - hi.astype(F32))


def _expand(v, e):
    hi, lo = _hi_lo(v)
    return _dot(hi, e) + _dot(lo, e)


def _headsum(t, e):
    m = t.shape[0]
    if m < 8:
        t = jnp.broadcast_to(t[0:1], (8, t.shape[1]))
    hi, lo = _hi_lo(t)
    return (_dot_nt(hi, e) + _dot_nt(lo, e))[0:m]


def _ssd_decays(dt, acs, dsk_ref, e):
    alast = acs[CHUNK - 1:CHUNK, :]
    stk = jnp.concatenate([dt, jnp.exp(acs), jnp.exp(alast - acs),
                           jnp.broadcast_to(jnp.exp(alast), (8, SSD_HEADS)),
                           jnp.broadcast_to(dsk_ref[...], (8, SSD_HEADS))], axis=0)
    ex = _expand(stk, e)
    return (ex[0:CHUNK], ex[CHUNK:2 * CHUNK], ex[2 * CHUNK:3 * CHUNK], ex[3 * CHUNK:3 * CHUNK + 1],
            ex[3 * CHUNK + 8:3 * CHUNK + 9])


def _ssd_fwd2(z, xbc, dtp, conv_w, conv_b, dt_bias, a_log, d_skip, norm_w):
    L = z.shape[0]
    nc = L // CHUNK
    half = D_SSD // SSD_GROUPS

    def body(z_ref, xbc_ref, tail_ref, dt_ref, cw_ref, cb_ref, dtb_ref, alog_ref, dsk_ref, nw_ref,
             y_ref, ypre_ref, prev_ref, state, ext, ybuf, mbuf):
        c = pl.program_id(0)

        @pl.when(c == 0)
        def _():
            state[...] = jnp.zeros_like(state)

        u, sig, xbcv, dtraw, dt, A, acs, acs_row = _ssd_chunk_pre(
            c == 0, xbc_ref, tail_ref, dt_ref, cw_ref, cb_ref, dtb_ref, alog_ref, ext)
        e = _head_expander()
        dtE, eacsE, dsdE, ealE, dskE = _ssd_decays(dt, acs, dsk_ref, e)
        xs = xbcv[:, 0:D_SSD]
        X = xs * dtE
        prev_ref[0] = state[...]
        causal = _iota2((CHUNK, CHUNK), 0) >= _iota2((CHUNK, CHUNK), 1)
        for g in range(SSD_GROUPS):
            gs = slice(half * g, half * (g + 1))
            Bg = _bf(xbcv[:, D_SSD + SSD_N * g:D_SSD + SSD_N * (g + 1)])
            Cg = _bf(xbcv[:, D_SSD + D_BC + SSD_N * g:D_SSD + D_BC + SSD_N * (g + 1)])
            cb = _dot_nt(Cg, Bg)
            for r in range(SSD_R):
                h = g * SSD_R + r
                seg = acs[:, h:h + 1] - acs_row[h:h + 1, :]
                mbuf[h] = _bf(cb * jnp.where(causal, jnp.exp(jnp.where(causal, seg, 0.0)), 0.0))
            st = state[:, gs]
            ybuf[:, gs] = _dot(Cg, _bf(st)) * eacsE[:, gs] + dskE[:, gs] * xs[:, gs]
            state[:, gs] = st * ealE[:, gs] + _dot_tn(Bg, _bf(X[:, gs] * dsdE[:, gs]))
        Xb = _bf(X)
        for h in range(SSD_HEADS):
            hs = slice(SSD_P * h, SSD_P * (h + 1))
            ybuf[:, hs] += _dot(mbuf[h], Xb[:, hs])
        y = ybuf[...]
        ypre_ref[...] = y
        zv = z_ref[...]
        yf = y * (zv * _sigmoid(zv))
        for g in range(SSD_GROUPS):
            gs = slice(half * g, half * (g + 1))
            yg = yf[:, gs]
            ms = jnp.mean(yg * yg, axis=-1, keepdims=True)
            y_ref[:, gs] = yg * lax.rsqrt(ms + RMS_EPS) * nw_ref[:, gs]

    full = lambda shape: pl.BlockSpec(shape, lambda c: (0, 0))
    return pl.pallas_call(
        body, name="ssd_fwd", grid=(nc,),
        in_specs=[
            pl.BlockSpec((CHUNK, D_SSD), lambda c: (c, 0)),
            pl.BlockSpec((CHUNK, D_XBC), lambda c: (c, 0)),
            pl.BlockSpec((8, D_XBC), lambda c: (jnp.maximum(c * (CHUNK // 8) - 1, 0), 0)),
            pl.BlockSpec((CHUNK, DT_PAD), lambda c: (c, 0)),
            full((CONV_K, D_XBC)), full((1, D_XBC)), full((1, SSD_HEADS)), full((1, SSD_HEADS)), full((1, SSD_HEADS)),
            full((1, D_SSD)),
        ],
        out_specs=[
            pl.BlockSpec((CHUNK, D_SSD), lambda c: (c, 0)),
            pl.BlockSpec((CHUNK, D_SSD), lambda c: (c, 0)),
            pl.BlockSpec((1, SSD_N, D_SSD), lambda c: (c, 0, 0)),
        ],
        out_shape=[
            jax.ShapeDtypeStruct((L, D_SSD), F32),
            jax.ShapeDtypeStruct((L, D_SSD), F32),
            jax.ShapeDtypeStruct((nc, SSD_N, D_SSD), F32),
        ],
        scratch_shapes=[
            pltpu.VMEM((SSD_N, D_SSD), F32),
            pltpu.VMEM((CHUNK + 8, D_XBC), F32),
            pltpu.VMEM((CHUNK, D_SSD), F32),
            pltpu.VMEM((SSD_HEADS, CHUNK, CHUNK), BF16),
        ],
        compiler_params=_cparams(("arbitrary",)),
    )(z, xbc, xbc, dtp, conv_w, conv_b, dt_bias, a_log, d_skip, norm_w)


def _ssd_bwd2(dy, z, ypre, xbc, dtp, prev, conv_w, conv_b, dt_bias, a_log, d_skip, norm_w):
    L = z.shape[0]
    nc = L // CHUNK
    half = D_SSD // SSD_GROUPS

    def body(dy_ref, z_ref, ypre_ref, xbc_ref, tail_ref, dt_ref, prev_ref, cw_ref, cb_ref, dtb_ref, alog_ref, dsk_ref,
             nw_ref, dz_ref, dxbc_ref, ddt_ref, gcw_ref, gcb_ref, gdtb_ref, galog_ref, gdsk_ref, gnw_ref,
             dstate, dhead, ext, ext2, dpost, yobuf, bdbuf, lmbuf, dmbuf, cbbuf):
        i = pl.program_id(0)
        c = nc - 1 - i

        @pl.when(i == 0)
        def _():
            dstate[...] = jnp.zeros_like(dstate)
            dhead[...] = jnp.zeros_like(dhead)
            gcw_ref[...] = jnp.zeros_like(gcw_ref)
            gcb_ref[...] = jnp.zeros_like(gcb_ref)
            gdtb_ref[...] = jnp.zeros_like(gdtb_ref)
            galog_ref[...] = jnp.zeros_like(galog_ref)
            gdsk_ref[...] = jnp.zeros_like(gdsk_ref)
            gnw_ref[...] = jnp.zeros_like(gnw_ref)

        u, sig, xbcv, dtraw, dt, A, acs, acs_row = _ssd_chunk_pre(
            c == 0, xbc_ref, tail_ref, dt_ref, cw_ref, cb_ref, dtb_ref, alog_ref, ext)
        e = _head_expander()
        dtE, eacsE, dsdE, ealE, dskE = _ssd_decays(dt, acs, dsk_ref, e)
        alast = acs[CHUNK - 1:CHUNK, :]
        xs = xbcv[:, 0:D_SSD]
        X = xs * dtE
        Xb = _bf(X)

        zv = z_ref[...]
        ypre = ypre_ref[...]
        dyn = dy_ref[...]
        sz = _sigmoid(zv)
        silu_z = zv * sz
        yf = ypre * silu_z
        dyf_parts = []
        for g in range(SSD_GROUPS):
            gs = slice(half * g, half * (g + 1))
            yg = yf[:, gs]
            rstd = lax.rsqrt(jnp.mean(yg * yg, axis=-1, keepdims=True) + RMS_EPS)
            dout = dyn[:, gs]
            gnw_ref[:, gs] += jnp.sum(dout * yg * rstd, axis=0, keepdims=True)
            dyhat = dout * nw_ref[:, gs]
            dyf_parts.append(rstd * (dyhat - yg * (rstd * rstd) * jnp.mean(dyhat * yg, axis=-1, keepdims=True)))
        dyf = jnp.concatenate(dyf_parts, axis=1)
        dz_ref[...] = dyf * ypre * (sz * (1.0 + zv * (1.0 - sz)))
        dyp = dyf * silu_z
        dyb = _bf(dyp)
        G = dyp * eacsE

        causal = _iota2((CHUNK, CHUNK), 0) >= _iota2((CHUNK, CHUNK), 1)
        ST = prev_ref[0]
        dST = dstate[...]
        for g in range(SSD_GROUPS):
            gs = slice(half * g, half * (g + 1))
            bs = slice(D_SSD + SSD_N * g, D_SSD + SSD_N * (g + 1))
            cs = slice(D_SSD + D_BC + SSD_N * g, D_SSD + D_BC + SSD_N * (g + 1))
            Bg = _bf(xbcv[:, bs])
            Cg = _bf(xbcv[:, cs])
            Gb = _bf(G[:, gs])
            STb = _bf(ST[:, gs])
            dSTb = _bf(dST[:, gs])
            dstate[:, gs] = dST[:, gs] * ealE[:, gs] + _dot_tn(Cg, Gb)
            yobuf[:, gs] = _dot(Cg, STb) * eacsE[:, gs]
            bdbuf[:, gs] = _dot(Bg, dSTb)
            dpost[:, cs] = _dot_nt(Gb, STb)
            dpost[:, bs] = _dot_nt(_bf(X[:, gs] * dsdE[:, gs]), dSTb)
            cbbuf[g] = _dot_nt(Cg, Bg)
            for r in range(SSD_R):
                h = g * SSD_R + r
                seg = acs[:, h:h + 1] - acs_row[h:h + 1, :]
                lmbuf[h] = jnp.where(causal, jnp.exp(jnp.where(causal, seg, 0.0)), 0.0)
        for h in range(SSD_HEADS):
            hs = slice(SSD_P * h, SSD_P * (h + 1))
            Mb = _bf(cbbuf[h // SSD_R] * lmbuf[h])
            dmbuf[h] = _dot_nt(dyb[:, hs], Xb[:, hs])
            dpost[:, hs] = _dot_tn(Mb, dyb[:, hs])
        lane16 = _iota2((1, SSD_HEADS), 1)
        sub16 = _iota2((SSD_HEADS, 1), 0)
        dacs_col = jnp.zeros((CHUNK, SSD_HEADS), F32)
        dacs_row = jnp.zeros((SSD_HEADS, CHUNK), F32)
        for g in range(SSD_GROUPS):
            bs = slice(D_SSD + SSD_N * g, D_SSD + SSD_N * (g + 1))
            cs = slice(D_SSD + D_BC + SSD_N * g, D_SSD + D_BC + SSD_N * (g + 1))
            cb = cbbuf[g]
            dcb = jnp.zeros((CHUNK, CHUNK), F32)
            for r in range(SSD_R):
                h = g * SSD_R + r
                dM = dmbuf[h]
                Lm = lmbuf[h]
                dcb = dcb + dM * Lm
                dseg = dM * (cb * Lm)
                dacs_col = dacs_col + jnp.sum(dseg, axis=-1, keepdims=True) * (lane16 == h).astype(F32)
                dacs_row = dacs_row - jnp.sum(dseg, axis=0, keepdims=True) * (sub16 == h).astype(F32)
            dcbb = _bf(dcb)
            dpost[:, bs] += _dot_tn(dcbb, _bf(xbcv[:, cs]))
            dpost[:, cs] += _dot(dcbb, _bf(xbcv[:, bs]))

        BD = bdbuf[...]
        dX = dpost[:, 0:D_SSD] + dsdE * BD
        dsd = jnp.exp(alast - acs)
        T = _headsum(X * BD, e) * dsd
        dalast = jnp.sum(T, axis=0, keepdims=True) + _headsum(
            jnp.sum(dST * ST, axis=0, keepdims=True), e) * jnp.exp(alast)
        is_last = (_iota2((CHUNK, 1), 0) == CHUNK - 1).astype(F32)
        dacs = dacs_col + _to_cols(dacs_row) + _headsum(dyp * yobuf[...], e) - T + is_last * dalast
        triu = (_iota2((CHUNK, CHUNK), 0) <= _iota2((CHUNK, CHUNK), 1)).astype(F32)
        da = jnp.dot(triu, dacs, preferred_element_type=F32, precision=HI)
        ddt_tot = _headsum(dX * xs, e) + da * A
        galog_ref[...] += jnp.sum(da * dt, axis=0, keepdims=True) * A
        ddtraw = ddt_tot * _sigmoid(dtraw)
        gdtb_ref[...] += jnp.sum(ddtraw, axis=0, keepdims=True)
        gdsk_ref[...] += _headsum(jnp.sum(dyp * xs, axis=0, keepdims=True), e)
        ddt_ref[...] = jnp.zeros_like(ddt_ref)
        ddt_ref[:, 0:SSD_HEADS] = ddtraw
        dpost[:, 0:D_SSD] = dX * dtE + dskE * dyp

        dconv = dpost[...] * (sig * (1.0 + u * (1.0 - sig)))
        gcb_ref[...] += jnp.sum(dconv, axis=0, keepdims=True)
        for k in range(CONV_K):
            gcw_ref[k:k + 1, :] += jnp.sum(dconv * ext[pl.ds(5 + k, CHUNK), :], axis=0, keepdims=True)
        ext2[0:CHUNK, :] = dconv
        ext2[CHUNK:CHUNK + 8, :] = dhead[...]
        dx = cw_ref[CONV_K - 1:CONV_K, :] * dconv
        for k in range(CONV_K - 1):
            dx = dx + cw_ref[k:k + 1, :] * ext2[pl.ds(CONV_K - 1 - k, CHUNK), :]
        dxbc_ref[...] = dx
        dhead[...] = dconv[0:8, :]

    full = lambda shape: pl.BlockSpec(shape, lambda i: (0, 0))
    rev = lambda wd: pl.BlockSpec((CHUNK, wd), lambda i: (nc - 1 - i, 0))
    return pl.pallas_call(
        body, name="ssd_bwd", grid=(nc,),
        in_specs=[
            rev(D_SSD), rev(D_SSD), rev(D_SSD), rev(D_XBC),
            pl.BlockSpec((8, D_XBC), lambda i: (jnp.maximum((nc - 1 - i) * (CHUNK // 8) - 1, 0), 0)),
            rev(DT_PAD),
            pl.BlockSpec((1, SSD_N, D_SSD), lambda i: (nc - 1 - i, 0, 0)),
            full((CONV_K, D_XBC)), full((1, D_XBC)), full((1, SSD_HEADS)), full((1, SSD_HEADS)), full((1, SSD_HEADS)),
            full((1, D_SSD)),
        ],
        out_specs=[
            rev(D_SSD), rev(D_XBC), rev(DT_PAD),
            full((CONV_K, D_XBC)), full((1, D_XBC)), full((1, SSD_HEADS)), full((1, SSD_HEADS)), full((1, SSD_HEADS)),
            full((1, D_SSD)),
        ],
        out_shape=[
            jax.ShapeDtypeStruct((L, D_SSD), F32), jax.ShapeDtypeStruct((L, D_XBC), F32),
            jax.ShapeDtypeStruct((L, DT_PAD), F32),
            jax.ShapeDtypeStruct((CONV_K, D_XBC), F32), jax.ShapeDtypeStruct((1, D_XBC), F32),
            jax.ShapeDtypeStruct((1, SSD_HEADS), F32), jax.ShapeDtypeStruct((1, SSD_HEADS), F32),
            jax.ShapeDtypeStruct((1, SSD_HEADS), F32), jax.ShapeDtypeStruct((1, D_SSD), F32),
        ],
        scratch_shapes=[
            pltpu.VMEM((SSD_N, D_SSD), F32),
            pltpu.VMEM((8, D_XBC), F32),
            pltpu.VMEM((CHUNK + 8, D_XBC), F32),
            pltpu.VMEM((CHUNK + 8, D_XBC), F32),
            pltpu.VMEM((CHUNK, D_XBC), F32),
            pltpu.VMEM((CHUNK, D_SSD), F32),
            pltpu.VMEM((CHUNK, D_SSD), F32),
            pltpu.VMEM((SSD_HEADS, CHUNK, CHUNK), F32),
            pltpu.VMEM((SSD_HEADS, CHUNK, CHUNK), F32),
            pltpu.VMEM((SSD_GROUPS, CHUNK, CHUNK), F32),
        ],
        compiler_params=_cparams(("arbitrary",)),
    )(dy, z, ypre, xbc, xbc, dtp, prev, conv_w, conv_b, dt_bias, a_log, d_skip, norm_w)


def _rope_tables(pos_ref, inv_ref):
    ang = pos_ref[...].astype(F32) * inv_ref[...]
    d = _iota2((1, 2 * ATT_HD), 1) % ATT_HD
    s = jnp.sin(ang)
    return jnp.cos(ang), jnp.where(d < ROPE_DIM // 2, -s, 0.0), jnp.where((d >= ROPE_DIM // 2) & (d < ROPE_DIM), s, 0.0)


def _rope(t, tabs):
    c, s1, s2 = tabs
    n = t.shape[1]
    rep = n // c.shape[1]
    return (t * jnp.tile(c, (1, rep)) + pltpu.roll(t, n - ROPE_DIM // 2, 1) * jnp.tile(s1, (1, rep))
            + pltpu.roll(t, ROPE_DIM // 2, 1) * jnp.tile(s2, (1, rep)))


def _rope_t(t, tabs):
    c, s1, s2 = tabs
    n = t.shape[1]
    rep = n // c.shape[1]
    return (t * jnp.tile(c, (1, rep)) + pltpu.roll(t * jnp.tile(s1, (1, rep)), ROPE_DIM // 2, 1)
            + pltpu.roll(t * jnp.tile(s2, (1, rep)), n - ROPE_DIM // 2, 1))


def _swa_mask(first):
    qi = _iota2((WINDOW, 2 * WINDOW), 0)
    si = _iota2((WINDOW, 2 * WINDOW), 1)
    band = (si > qi) & (si <= qi + WINDOW)
    return band & (jnp.logical_not(first) | (si >= WINDOW))


def _stack_heads(t, j):
    return jnp.concatenate([t[:, ATT_HD * (j * ATT_R + r):ATT_HD * (j * ATT_R + r + 1)] for r in range(ATT_R)], axis=0)


def _stack_cols(ref, j):
    cols = [jnp.broadcast_to(ref[:, j * ATT_R + r:j * ATT_R + r + 1], (WINDOW, 1)) for r in range(ATT_R)]
    return jnp.concatenate(cols, axis=0)


def _swa_fwd(q, g, kv, sinks):
    L = q.shape[0]
    nb = L // WINDOW
    scale = ATT_HD ** -0.5

    def body(q_ref, g_ref, kvc_ref, kvp_ref, sink_ref, y_ref, o_ref, lse_ref, sbuf, pbuf, rbuf):
        n = pl.program_id(0)
        kk = _bf(jnp.concatenate([kvp_ref[:, 0:D_KV], kvc_ref[:, 0:D_KV]], axis=0))
        vv = _bf(jnp.concatenate([kvp_ref[:, D_KV:2 * D_KV], kvc_ref[:, D_KV:2 * D_KV]], axis=0))
        valid = jnp.tile(_swa_mask(n == 0), (ATT_R, 1))
        qv = q_ref[...]
        for j in range(ATT_KVH):
            s = _dot_nt(_bf(_stack_heads(qv, j)), kk[:, ATT_HD * j:ATT_HD * (j + 1)]) * scale
            sbuf[j] = jnp.where(valid, s, NEG_BIG)
        for j in range(ATT_KVH):
            s = sbuf[j]
            sink = _stack_cols(sink_ref, j)
            m = jnp.maximum(jnp.max(s, axis=-1, keepdims=True), sink)
            p = jnp.exp(s - m)
            pbuf[j] = _bf(p)
            denom = jnp.sum(p, axis=-1, keepdims=True) + jnp.exp(sink - m)
            rbuf[j] = 1.0 / denom
            lse = m + jnp.log(denom)
            for r in range(ATT_R):
                h = j * ATT_R + r
                lse_ref[:, h:h + 1] = lse[WINDOW * r:WINDOW * (r + 1)]
        for j in range(ATT_KVH):
            o = _dot(pbuf[j], vv[:, ATT_HD * j:ATT_HD * (j + 1)]) * rbuf[j]
            for r in range(ATT_R):
                h = j * ATT_R + r
                o_ref[:, ATT_HD * h:ATT_HD * (h + 1)] = o[WINDOW * r:WINDOW * (r + 1)]
        gv = g_ref[...]
        y_ref[...] = o_ref[...] * (gv * _sigmoid(gv))

    cur = lambda wd: pl.BlockSpec((WINDOW, wd), lambda n: (n, 0))
    prv = lambda wd: pl.BlockSpec((WINDOW, wd), lambda n: (jnp.maximum(n - 1, 0), 0))
    return pl.pallas_call(
        body, name="swa_fwd", grid=(nb,),
        in_specs=[cur(D_ATT), cur(D_ATT), cur(2 * D_KV), prv(2 * D_KV), pl.BlockSpec((1, ATT_QH), lambda n: (0, 0))],
        out_specs=[cur(D_ATT), cur(D_ATT), cur(ATT_QH)],
        out_shape=[jax.ShapeDtypeStruct((L, D_ATT), F32), jax.ShapeDtypeStruct((L, D_ATT), F32),
                   jax.ShapeDtypeStruct((L, ATT_QH), F32)],
        scratch_shapes=[pltpu.VMEM((ATT_KVH, ATT_R * WINDOW, 2 * WINDOW), F32),
                        pltpu.VMEM((ATT_KVH, ATT_R * WINDOW, 2 * WINDOW), BF16),
                        pltpu.VMEM((ATT_KVH, ATT_R * WINDOW, 1), F32)],
        compiler_params=_cparams(("parallel",)),
    )(q, g, kv, kv, sinks)


def _swa_bwd(dy, q, g, kv, o, lse, pos, inv, sinks):
    L = q.shape[0]
    nb = L // WINDOW
    scale = ATT_HD ** -0.5

    def body(dy_ref, q_ref, g_ref, kvc_ref, kvp_ref, o_ref, lse_ref, posc_ref, posp_ref, inv_ref, sink_ref,
             dq_ref, dg_ref, dkv_ref, dsink_ref, carry, dqbuf, dkbuf, dvbuf):
        n = pl.program_id(0)

        @pl.when(n == 0)
        def _():
            dsink_ref[...] = jnp.zeros_like(dsink_ref)

        @pl.when(n < nb)
        def _():
            tc = _rope_tables(posc_ref, inv_ref)
            tp = _rope_tables(posp_ref, inv_ref)
            kk = _bf(jnp.concatenate([kvp_ref[:, 0:D_KV], kvc_ref[:, 0:D_KV]], axis=0))
            vv = _bf(jnp.concatenate([kvp_ref[:, D_KV:2 * D_KV], kvc_ref[:, D_KV:2 * D_KV]], axis=0))
            valid = jnp.tile(_swa_mask(n == 0), (ATT_R, 1))
            qv = q_ref[...]
            gv = g_ref[...]
            sg = _sigmoid(gv)
            dyv = dy_ref[...]
            ov = o_ref[...]
            dg_ref[...] = dyv * ov * (sg * (1.0 + gv * (1.0 - sg)))
            do = dyv * (gv * sg)
            delta_all = do * ov
            lane16 = _iota2((1, ATT_QH), 1)
            dsink = jnp.zeros((1, ATT_QH), F32)
            for j in range(ATT_KVH):
                js = slice(ATT_HD * j, ATT_HD * (j + 1))
                kj = kk[:, js]
                vj = vv[:, js]
                qs = _bf(_stack_heads(qv, j))
                dos = _bf(_stack_heads(do, j))
                delta = jnp.sum(_stack_heads(delta_all, j), axis=-1, keepdims=True)
                lse = _stack_cols(lse_ref, j)
                s = _dot_nt(qs, kj) * scale
                p = jnp.exp(jnp.where(valid, s, NEG_BIG) - lse)
                dS = _bf(p * (_dot_nt(dos, vj) - delta))
                dqs = _dot(dS, kj) * scale
                dkbuf[:, js] = _dot_tn(dS, qs) * scale
                dvbuf[:, js] = _dot_tn(_bf(p), dos)
                sd = jnp.exp(_stack_cols(sink_ref, j) - lse) * delta
                for r in range(ATT_R):
                    h = j * ATT_R + r
                    rs = slice(WINDOW * r, WINDOW * (r + 1))
                    dqbuf[:, ATT_HD * h:ATT_HD * (h + 1)] = dqs[rs]
                    dsink = dsink - jnp.sum(sd[rs], axis=0, keepdims=True) * (lane16 == h).astype(F32)
            dsink_ref[...] += dsink
            dq_ref[...] = _rope_t(dqbuf[...], tc)
            dkp = _rope_t(dkbuf[0:WINDOW, :], tp)
            dkc = _rope_t(dkbuf[WINDOW:2 * WINDOW, :], tc)

            @pl.when(n > 0)
            def _():
                dkv_ref[:, 0:D_KV] = carry[:, 0:D_KV] + dkp
                dkv_ref[:, D_KV:2 * D_KV] = carry[:, D_KV:2 * D_KV] + dvbuf[0:WINDOW, :]

            carry[:, 0:D_KV] = dkc
            carry[:, D_KV:2 * D_KV] = dvbuf[WINDOW:2 * WINDOW, :]

        @pl.when(n == nb)
        def _():
            dkv_ref[...] = carry[...]

    last = nb - 1
    cur = lambda wd: pl.BlockSpec((WINDOW, wd), lambda n: (jnp.minimum(n, last), 0))
    prv = lambda wd: pl.BlockSpec((WINDOW, wd), lambda n: (jnp.maximum(jnp.minimum(n, last) - 1, 0), 0))
    return pl.pallas_call(
        body, name="swa_bwd", grid=(nb + 1,),
        in_specs=[cur(D_ATT), cur(D_ATT), cur(D_ATT), cur(2 * D_KV), prv(2 * D_KV), cur(D_ATT), cur(ATT_QH), cur(1), prv(1),
                  pl.BlockSpec((1, 2 * ATT_HD), lambda n: (0, 0)), pl.BlockSpec((1, ATT_QH), lambda n: (0, 0))],
        out_specs=[cur(D_ATT), cur(D_ATT),
                   pl.BlockSpec((WINDOW, 2 * D_KV), lambda n: (jnp.maximum(n - 1, 0), 0)),
                   pl.BlockSpec((1, ATT_QH), lambda n: (0, 0))],
        out_shape=[jax.ShapeDtypeStruct((L, D_ATT), F32), jax.ShapeDtypeStruct((L, D_ATT), F32),
                   jax.ShapeDtypeStruct((L, 2 * D_KV), F32), jax.ShapeDtypeStruct((1, ATT_QH), F32)],
        scratch_shapes=[pltpu.VMEM((WINDOW, 2 * D_KV), F32), pltpu.VMEM((WINDOW, D_ATT), F32),
                        pltpu.VMEM((2 * WINDOW, D_KV), F32), pltpu.VMEM((2 * WINDOW, D_KV), F32)],
        compiler_params=_cparams(("arbitrary",)),
    )(dy, q, g, kv, kv, o, lse, pos, pos, inv, sinks)


def _out_ln_loss(y_ssd, y_att, x, target, w_out, ln_g, ln_b):
    L = x.shape[0]
    tm = ROW_TILE
    inv_d = 1.0 / D_MODEL

    def body(ys_ref, ya_ref, x_ref, t_ref, w_ref, g_ref, b_ref, dr_ref, dys_ref, dya_ref, loss_ref, gg_ref, gb_ref):
        i = pl.program_id(0)

        @pl.when(i == 0)
        def _():
            loss_ref[...] = jnp.zeros_like(loss_ref)
            gg_ref[...] = jnp.zeros_like(gg_ref)
            gb_ref[...] = jnp.zeros_like(gb_ref)

        h = _dot(_bf(ys_ref[...]), w_ref[0:D_SSD, :]) + _dot(_bf(ya_ref[...]), w_ref[D_SSD:D_MIX, :])
        r = ALPHA * x_ref[...] + h
        mu = jnp.mean(r, axis=-1, keepdims=True)
        xc = r - mu
        rstd = lax.rsqrt(jnp.mean(xc * xc, axis=-1, keepdims=True) + LN_EPS)
        xhat = xc * rstd
        gam = g_ref[...]
        diff = xhat * gam + b_ref[...] - t_ref[...]
        part = jnp.sum(jnp.sum(diff * diff, axis=-1, keepdims=True), axis=0, keepdims=True)
        loss_ref[...] += (0.5 * inv_d) * part
        dout = diff * inv_d
        gg_ref[...] += jnp.sum(dout * xhat, axis=0, keepdims=True)
        gb_ref[...] += jnp.sum(dout, axis=0, keepdims=True)
        dxh = dout * gam
        dr = rstd * (dxh - jnp.mean(dxh, axis=-1, keepdims=True) - xhat * jnp.mean(dxh * xhat, axis=-1, keepdims=True))
        dr_ref[...] = dr
        drb = _bf(dr)
        dys_ref[...] = _dot_nt(drb, w_ref[0:D_SSD, :])
        dya_ref[...] = _dot_nt(drb, w_ref[D_SSD:D_MIX, :])

    row = pl.BlockSpec((tm, D_MODEL), lambda i: (i, 0))
    vec = pl.BlockSpec((1, D_MODEL), lambda i: (0, 0))
    return pl.pallas_call(
        body, name="out_ln_loss", grid=(L // tm,),
        in_specs=[row, row, row, row, pl.BlockSpec((D_MIX, D_MODEL), lambda i: (0, 0), pipeline_mode=pl.Buffered(1)), vec, vec],
        out_specs=[row, row, row, pl.BlockSpec((1, 128), lambda i: (0, 0)), vec, vec],
        out_shape=[jax.ShapeDtypeStruct((L, D_MODEL), F32)] * 3 + [jax.ShapeDtypeStruct((1, 128), F32)]
        + [jax.ShapeDtypeStruct((1, D_MODEL), F32)] * 2,
        compiler_params=_cparams(("arbitrary",)),
    )(y_ssd, y_att, x, target, w_out, ln_g, ln_b)


def _local_step(x, pos, target, w, w_out, conv_w, conv_b, dt_bias, a_log, d_skip, norm_w, sinks, ln_g, ln_b):
    inv8 = ROPE_THETA ** (-jnp.arange(0, ROPE_DIM, 2, dtype=F32) / ROPE_DIM)
    inv = jnp.tile(jnp.concatenate([inv8, inv8, jnp.zeros((ATT_HD - ROPE_DIM,), F32)]), 2).reshape(1, 2 * ATT_HD)

    z, g, q, xbc, kv, dtp, xb = _in_proj(x, w, pos, inv)
    y_ssd, y_pre, prev = _ssd_fwd2(z, xbc, dtp, conv_w, conv_b, dt_bias, a_log, d_skip, norm_w)
    y_att, o, lse = _swa_fwd(q, g, kv, sinks)
    dr, dy_ssd, dy_att, loss, g_ln_g, g_ln_b = _out_ln_loss(y_ssd, y_att, x, target, w_out, ln_g, ln_b)
    gw_out_ssd = _matmul_tn(y_ssd, dr, 1024, "gw_out_ssd")
    gw_out_att = _matmul_tn(y_att, dr, 1024, "gw_out_att")
    dq, dg, dkv, g_sinks = _swa_bwd(dy_att, q, g, kv, o, lse, pos, inv, sinks)
    dz, dxbc, ddt, g_conv_w, g_conv_b, g_dt_bias, g_a_log, g_d_skip, g_norm_w = _ssd_bwd2(
        dy_ssd, z, y_pre, xbc, dtp, prev, conv_w, conv_b, dt_bias, a_log, d_skip, norm_w)
    grad_x = _grad_x(dr, dz, dg, dq, dxbc, dkv, ddt, w)
    gw_z = _matmul_tn(dz, xb, 1024, "gw_z")
    gw_g = _matmul_tn(dg, xb, 1024, "gw_g")
    gw_q = _matmul_tn(dq, xb, 1024, "gw_q")
    gw_xbc = _matmul_tn(dxbc, xb, 1024, "gw_xbc")
    gw_kv = _matmul_tn(dkv, xb, 1024, "gw_kv")
    gw_dt = _matmul_tn(ddt, xb, 1024, "gw_dt")
    gw_in = jnp.concatenate([gw_z, gw_xbc, gw_dt[0:SSD_HEADS], gw_q, gw_kv, gw_g], axis=0)
    gw_out = jnp.concatenate([gw_out_ssd, gw_out_att], axis=0)
    small = dict(conv_w=g_conv_w, conv_b=g_conv_b, dt_bias=g_dt_bias, a_log=g_a_log, d_skip=g_d_skip,
                 ssd_norm_w=g_norm_w, attn_sinks=g_sinks, ln_g=g_ln_g, ln_b=g_ln_b)
    return loss, grad_x, gw_in, gw_out, small


def _mesh_pos():
    return lax.axis_index("x"), lax.axis_index("y"), lax.axis_index("c")


def _gather_weights(w_in_s, w_out_s, conv_w_s):
    def body(win_ref, wout_ref, cw_ref, owin_ref, owout_ref, ocw_ref, send_sems, recv_sems, small_send, small_recv,
             local_sems):
        x, y, c = _mesh_pos()
        me = 2 * x + y
        sibling = (x, y, 1 - c)
        chips = [(1 - x, y), (x, 1 - y), (1 - x, 1 - y)]
        locals_ = [pltpu.make_async_copy(cw_ref, ocw_ref.at[me], local_sems.at[0])]
        for cp in locals_:
            cp.start()
        started = []
        for t, (src, dst) in enumerate(((win_ref, owin_ref), (wout_ref, owout_ref))):
            hr = src.shape[0] // 2

            def half(ref, hc, hr=hr):
                return ref.at[pl.ds(hc * hr, hr), :]

            for j, (px, py) in enumerate(chips):
                cp = pltpu.make_async_remote_copy(
                    src_ref=half(src, c), dst_ref=half(dst.at[me], c), send_sem=send_sems.at[t, j],
                    recv_sem=recv_sems.at[t, j], device_id=(px, py, c), device_id_type=MESH)
                cp.start()
                started.append(cp)
        for j, (px, py) in enumerate(chips):
            cp = pltpu.make_async_remote_copy(
                src_ref=cw_ref, dst_ref=ocw_ref.at[me], send_sem=small_send.at[j], recv_sem=small_recv.at[j],
                device_id=(px, py, c), device_id_type=MESH)
            cp.start()
            started.append(cp)
        for t, (src, dst) in enumerate(((win_ref, owin_ref), (wout_ref, owout_ref))):
            hr = src.shape[0] // 2
            for j, (px, py) in enumerate(chips):
                src_chip = 2 * px + py
                blk = dst.at[src_chip].at[pl.ds(c * hr, hr), :]
                pltpu.make_async_remote_copy(
                    src_ref=blk, dst_ref=blk, send_sem=send_sems.at[t, j], recv_sem=recv_sems.at[t, j],
                    device_id=(px, py, c), device_id_type=MESH).wait_recv()
                cp = pltpu.make_async_remote_copy(
                    src_ref=blk, dst_ref=blk, send_sem=send_sems.at[t, 3 + j], recv_sem=recv_sems.at[t, 3 + j],
                    device_id=sibling, device_id_type=MESH)
                cp.start()
                started.append(cp)
        for t, (src, dst) in enumerate(((win_ref, owin_ref), (wout_ref, owout_ref))):
            hr = src.shape[0] // 2
            for j, (px, py) in enumerate(chips):
                src_chip = 2 * px + py
                blk = dst.at[src_chip].at[pl.ds((1 - c) * hr, hr), :]
                pltpu.make_async_remote_copy(
                    src_ref=blk, dst_ref=blk, send_sem=send_sems.at[t, 3 + j], recv_sem=recv_sems.at[t, 3 + j],
                    device_id=sibling, device_id_type=MESH).wait_recv()
        for j in range(3):
            pltpu.make_async_remote_copy(
                src_ref=cw_ref, dst_ref=ocw_ref.at[me], send_sem=small_send.at[j], recv_sem=small_recv.at[j],
                device_id=sibling, device_id_type=MESH).wait_recv()
        for cp in started:
            cp.wait_send()
        for cp in locals_:
            cp.wait()

    any_spec = pl.BlockSpec(memory_space=pl.ANY)
    return pl.pallas_call(
        body, name="gather_weights",
        in_specs=[any_spec] * 3, out_specs=[any_spec] * 3,
        out_shape=[jax.ShapeDtypeStruct((N_CHIPS,) + a.shape, a.dtype) for a in (w_in_s, w_out_s, conv_w_s)],
        scratch_shapes=[pltpu.SemaphoreType.DMA((2, 6)), pltpu.SemaphoreType.DMA((2, 6)),
                        pltpu.SemaphoreType.DMA((3,)), pltpu.SemaphoreType.DMA((3,)), pltpu.SemaphoreType.DMA((3,))],
    )(w_in_s, w_out_s, conv_w_s)


def _pair_exchange(gw_in, gw_out, small):
    k_small = small.shape[1]

    def body(gin_ref, gout_ref, sm_ref, rin_ref, rout_ref, slots_ref, send_sems, recv_sems, small_send, small_recv,
             local_sem):
        x, y, c = _mesh_pos()
        me = 4 * x + 2 * y + c
        sibling = (x, y, 1 - c)
        mine = pltpu.make_async_copy(sm_ref, slots_ref.at[me], local_sem)
        mine.start()
        started = []
        for t, (src, dst) in enumerate(((gin_ref, rin_ref), (gout_ref, rout_ref))):
            hr = src.shape[1] // 2
            for j in range(N_CHIPS):
                cp = pltpu.make_async_remote_copy(
                    src_ref=src.at[j, pl.ds((1 - c) * hr, hr), :], dst_ref=dst.at[j], send_sem=send_sems.at[t, j],
                    recv_sem=recv_sems.at[t, j], device_id=sibling, device_id_type=MESH)
                cp.start()
                started.append(cp)
        for k in range(1, 8):
            peer = (x ^ ((k >> 2) & 1), y ^ ((k >> 1) & 1), c ^ (k & 1))
            cp = pltpu.make_async_remote_copy(
                src_ref=sm_ref, dst_ref=slots_ref.at[me], send_sem=small_send.at[k - 1], recv_sem=small_recv.at[k - 1],
                device_id=peer, device_id_type=MESH)
            cp.start()
            started.append(cp)
        for t, (src, dst) in enumerate(((gin_ref, rin_ref), (gout_ref, rout_ref))):
            for j in range(N_CHIPS):
                pltpu.make_async_remote_copy(
                    src_ref=dst.at[j], dst_ref=dst.at[j], send_sem=send_sems.at[t, j], recv_sem=recv_sems.at[t, j],
                    device_id=sibling, device_id_type=MESH).wait_recv()
        for k in range(1, 8):
            pltpu.make_async_remote_copy(
                src_ref=sm_ref, dst_ref=slots_ref.at[me], send_sem=small_send.at[k - 1], recv_sem=small_recv.at[k - 1],
                device_id=sibling, device_id_type=MESH).wait_recv()
        for cp in started:
            cp.wait_send()
        mine.wait()

    any_spec = pl.BlockSpec(memory_space=pl.ANY)
    half_in = jax.ShapeDtypeStruct((N_CHIPS, gw_in.shape[1] // 2, D_MODEL), F32)
    half_out = jax.ShapeDtypeStruct((N_CHIPS, gw_out.shape[1] // 2, D_MODEL), F32)
    return pl.pallas_call(
        body, name="pair_exchange",
        in_specs=[any_spec] * 3, out_specs=[any_spec] * 3,
        out_shape=[half_in, half_out, jax.ShapeDtypeStruct((8, 8, k_small), F32)],
        scratch_shapes=[pltpu.SemaphoreType.DMA((2, N_CHIPS)), pltpu.SemaphoreType.DMA((2, N_CHIPS)),
                        pltpu.SemaphoreType.DMA((7,)), pltpu.SemaphoreType.DMA((7,)), pltpu.SemaphoreType.DMA],
    )(gw_in, gw_out, small)


def _chip_exchange(s_in, s_out):
    def body(sin_ref, sout_ref, rin_ref, rout_ref, send_sems, recv_sems):
        x, y, c = _mesh_pos()
        me = 2 * x + y
        chips = [(1 - x, y), (x, 1 - y), (1 - x, 1 - y)]
        started = []
        for t, (src, dst) in enumerate(((sin_ref, rin_ref), (sout_ref, rout_ref))):
            for j, (px, py) in enumerate(chips):
                cp = pltpu.make_async_remote_copy(
                    src_ref=src.at[2 * px + py], dst_ref=dst.at[me], send_sem=send_sems.at[t, j],
                    recv_sem=recv_sems.at[t, j], device_id=(px, py, c), device_id_type=MESH)
                cp.start()
                started.append(cp)
        for t, (src, dst) in enumerate(((sin_ref, rin_ref), (sout_ref, rout_ref))):
            for j, (px, py) in enumerate(chips):
                blk = dst.at[2 * px + py]
                pltpu.make_async_remote_copy(
                    src_ref=blk, dst_ref=blk, send_sem=send_sems.at[t, j], recv_sem=recv_sems.at[t, j],
                    device_id=(px, py, c), device_id_type=MESH).wait_recv()
        for cp in started:
            cp.wait_send()

    any_spec = pl.BlockSpec(memory_space=pl.ANY)
    return pl.pallas_call(
        body, name="chip_exchange",
        in_specs=[any_spec] * 2, out_specs=[any_spec] * 2,
        out_shape=[jax.ShapeDtypeStruct(s_in.shape, s_in.dtype), jax.ShapeDtypeStruct(s_out.shape, s_out.dtype)],
        scratch_shapes=[pltpu.SemaphoreType.DMA((2, 3)), pltpu.SemaphoreType.DMA((2, 3))],
    )(s_in, s_out)


def _pair_share(h_in, h_out):
    def body(hin_ref, hout_ref, rin_ref, rout_ref, send_sems, recv_sems):
        x, y, c = _mesh_pos()
        sibling = (x, y, 1 - c)
        started = []
        for t, (src, dst) in enumerate(((hin_ref, rin_ref), (hout_ref, rout_ref))):
            cp = pltpu.make_async_remote_copy(
                src_ref=src, dst_ref=dst, send_sem=send_sems.at[t], recv_sem=recv_sems.at[t],
                device_id=sibling, device_id_type=MESH)
            cp.start()
            started.append(cp)
        for cp in started:
            cp.wait()

    any_spec = pl.BlockSpec(memory_space=pl.ANY)
    return pl.pallas_call(
        body, name="pair_share",
        in_specs=[any_spec] * 2, out_specs=[any_spec] * 2,
        out_shape=[jax.ShapeDtypeStruct(h_in.shape, F32), jax.ShapeDtypeStruct(h_out.shape, F32)],
        scratch_shapes=[pltpu.SemaphoreType.DMA((2,)), pltpu.SemaphoreType.DMA((2,))],
    )(h_in, h_out)


def _pair_add(g, recv, core, name):
    _, rows, C = recv.shape
    tc = 256

    def body(core_ref, g_ref, r_ref, o_ref):
        o_ref[...] = _bf(g_ref[...] + r_ref[...])

    spec = pl.BlockSpec((1, rows, tc), lambda j, i, core: (j, 0, i))
    return pl.pallas_call(
        body, name=name,
        grid_spec=pltpu.PrefetchScalarGridSpec(
            num_scalar_prefetch=1, grid=(N_CHIPS, C // tc),
            in_specs=[pl.BlockSpec((1, rows, tc), lambda j, i, core: (j, core[0], i)), spec], out_specs=spec),
        out_shape=jax.ShapeDtypeStruct((N_CHIPS, rows, C), BF16),
        compiler_params=_cparams(("parallel", "parallel")),
    )(core, g, recv)


def _chip_add(own, parts, chip, name):
    _, rows, C = parts.shape
    tc = 256

    def body(chip_ref, own_ref, r0, r1, r2, r3, o_ref):
        acc = None
        for j, r in enumerate((r0, r1, r2, r3)):
            term = jnp.where(chip_ref[0] == j, own_ref[0], r[0]).astype(F32)
            acc = term if acc is None else acc + term
        o_ref[...] = acc

    def slab(j):
        return pl.BlockSpec((1, rows, tc), lambda i, chip: (jnp.where(chip[0] == j, (j + 1) % N_CHIPS, j), 0, i))

    return pl.pallas_call(
        body, name=name,
        grid_spec=pltpu.PrefetchScalarGridSpec(
            num_scalar_prefetch=1, grid=(C // tc,),
            in_specs=[pl.BlockSpec((1, rows, tc), lambda i, chip: (chip[0], 0, i))] + [slab(j) for j in range(N_CHIPS)],
            out_specs=pl.BlockSpec((rows, tc), lambda i, chip: (0, i))),
        out_shape=jax.ShapeDtypeStruct((rows, C), F32),
        compiler_params=_cparams(("parallel",)),
    )(chip, own, parts, parts, parts, parts)


def _adamw_math(w, g, m, v):
    m = ADAM_B1 * m + (1.0 - ADAM_B1) * g
    v = ADAM_B2 * v + (1.0 - ADAM_B2) * (g * g)
    m_hat = m / (1.0 - ADAM_B1 ** ADAM_STEP)
    v_hat = v / (1.0 - ADAM_B2 ** ADAM_STEP)
    delta = -ADAM_LR * (m_hat / (jnp.sqrt(v_hat) + ADAM_EPS) + ADAM_WD * w)
    return delta, m, v


def _adamw_pair(w, g_own, g_sib, m, v, core, name):
    unit = w.ndim == 3
    R, C = w.shape[0], w.shape[-1]
    rows = g_own.shape[0]
    tc = 128

    def body(core_ref, w_ref, go_ref, gs_ref, m_ref, v_ref, d_ref, nm_ref, nv_ref, g_ref):
        first = core_ref[0] == 0
        own, sib = go_ref[...], gs_ref[...]
        g = jnp.concatenate([jnp.where(first, own, sib), jnp.where(first, sib, own)], axis=0)[0:R, :]
        idx = (slice(None), 0, slice(None)) if unit else (slice(None), slice(None))
        d, nm, nv = _adamw_math(w_ref[idx], g, m_ref[idx], v_ref[idx])
        d_ref[idx] = d
        nm_ref[idx] = nm
        nv_ref[idx] = nv
        g_ref[idx] = g

    if unit:
        spec = pl.BlockSpec((R, 1, tc), lambda i, core: (0, 0, i))
    else:
        spec = pl.BlockSpec((R, tc), lambda i, core: (0, i))
    gspec = pl.BlockSpec((rows, tc), lambda i, core: (0, i))
    return pl.pallas_call(
        body, name=name,
        grid_spec=pltpu.PrefetchScalarGridSpec(
            num_scalar_prefetch=1, grid=(C // tc,),
            in_specs=[spec, gspec, gspec, spec, spec], out_specs=[spec] * 4),
        out_shape=[jax.ShapeDtypeStruct(w.shape, F32)] * 4,
        compiler_params=_cparams(("parallel",)),
    )(core, w, g_own, g_sib, m, v)


SMALL_NAMES = ("conv_b", "ssd_norm_w", "ln_g", "ln_b", "dt_bias", "a_log", "d_skip", "attn_sinks")
SMALL_SIZES = (D_XBC, D_SSD, D_MODEL, D_MODEL, SSD_HEADS, SSD_HEADS, SSD_HEADS, ATT_QH)
SMALL_OFFS = tuple(D_XBC + sum(-(-n // 128) * 128 for n in SMALL_SIZES[:k]) for k in range(len(SMALL_SIZES)))
LOSS_OFF = D_XBC + sum(-(-n // 128) * 128 for n in SMALL_SIZES)
K_SMALL = LOSS_OFF + 128


def _pack_small(g_conv_w, vecs, loss):
    def body(cw_ref, *refs):
        o_ref = refs[-1]
        o_ref[...] = jnp.zeros_like(o_ref)
        o_ref[0:CONV_K, 0:D_XBC] = cw_ref[...]
        for v_ref, off, n in zip(refs[:-2], SMALL_OFFS, SMALL_SIZES):
            o_ref[0:1, off:off + n] = v_ref[...]
        o_ref[0:1, LOSS_OFF:LOSS_OFF + 128] = refs[-2][...]

    return pl.pallas_call(
        body, name="pack_small", out_shape=jax.ShapeDtypeStruct((8, K_SMALL), F32), compiler_params=_cparams(),
    )(g_conv_w, *vecs, loss)


def _adamw_small(slots, chip, conv_w, m_conv_w, v_conv_w, params, moms, vars_):
    n_vec = len(SMALL_NAMES)

    def body(chip_ref, s_ref, *refs):
        ins = refs[:3 * (n_vec + 1)]
        outs = refs[3 * (n_vec + 1):-1]
        tot_ref = refs[-1]
        tot = s_ref[0]
        for d in range(1, 8):
            tot = tot + s_ref[d]
        outs[0][...] = tot[0:1, LOSS_OFF:LOSS_OFF + 1]
        off = pl.multiple_of(chip_ref[0] * CONV_COLS, 128)
        tot_ref[...] = tot
        grads = [tot_ref[0:CONV_K, pl.ds(off, CONV_COLS)]]
        grads += [tot[0:1, o:o + n] for o, n in zip(SMALL_OFFS, SMALL_SIZES)]
        for k, g in enumerate(grads):
            w_ref, m_ref, v_ref = ins[3 * k:3 * k + 3]
            full = (0,) if k == 0 else (Ellipsis,)
            d, nm, nv = _adamw_math(w_ref[full], g, m_ref[full], v_ref[full])
            for o_ref, val in zip(outs[1 + 4 * k:5 + 4 * k], (g, d, nm, nv)):
                o_ref[full] = val

    args = [conv_w, m_conv_w, v_conv_w]
    for w, m, v in zip(params, moms, vars_):
        args += [w, m, v]
    shapes = [jax.ShapeDtypeStruct((1, 1), F32)] + [jax.ShapeDtypeStruct(conv_w.shape, F32)] * 4
    for w in params:
        shapes += [jax.ShapeDtypeStruct(w.shape, F32)] * 4
    vmem = pl.BlockSpec(memory_space=pltpu.VMEM)
    return pl.pallas_call(
        body, name="adamw_small",
        grid_spec=pltpu.PrefetchScalarGridSpec(
            num_scalar_prefetch=1, grid=(1,),
            in_specs=[pl.BlockSpec(slots.shape, lambda i, chip: (0, 0, 0))] + [vmem] * len(args),
            out_specs=[vmem] * len(shapes), scratch_shapes=[pltpu.VMEM((8, K_SMALL), F32)]),
        out_shape=shapes, compiler_params=_cparams(),
    )(chip, slots, *args)


def kernel(x, positions, w_in, conv_w, conv_b, dt_bias, a_log, d_skip, ssd_norm_w, attn_sinks, w_out, ln_g, ln_b, loss_target, m_w_in, m_conv_w, m_conv_b, m_dt_bias, m_a_log, m_d_skip, m_ssd_norm_w, m_attn_sinks, m_w_out, m_ln_g, m_ln_b, v_w_in, v_conv_w, v_conv_b, v_dt_bias, v_a_log, v_d_skip, v_ssd_norm_w, v_attn_sinks, v_w_out, v_ln_g, v_ln_b):
    mx, my, mc = _mesh_pos()
    chip = 2 * mx + my
    L = x.shape[1]

    conv_w_s8 = jnp.pad(conv_w[0], ((0, 8 - CONV_K), (0, 0)))
    pad_rows = ((0, SLAB_ROWS - W_IN_COLS), (0, 0))
    w_in_t = w_in[0].T
    w_in_b, w_out_b = jnp.pad(_bf(w_in_t), pad_rows), _bf(w_out[0])
    ag_in, ag_out, ag_cw = _gather_weights(w_in_b, w_out_b, conv_w_s8)
    ag_in = jnp.where((jnp.arange(N_CHIPS) == chip)[:, None, None], w_in_b[None], ag_in)
    ag_out = jnp.where((jnp.arange(N_CHIPS) == chip)[:, None, None], w_out_b[None], ag_out)
    w_full = jnp.concatenate([ag_in[j, 0:W_IN_COLS] for j in range(N_CHIPS)], axis=0)
    w = jnp.concatenate([
        w_full[O_Z:O_Z + D_SSD], w_full[O_G:O_G + D_ATT], w_full[O_Q:O_Q + D_ATT],
        w_full[O_XBC:O_XBC + D_XBC], w_full[O_K:O_K + 2 * D_KV], w_full[O_DT:O_DT + SSD_HEADS],
        jnp.zeros((DT_PAD - SSD_HEADS, D_MODEL), BF16)], axis=0)
    w_out_full = ag_out.reshape(D_MIX, D_MODEL)
    conv_w_full = jnp.concatenate([ag_cw[j, 0:CONV_K] for j in range(N_CHIPS)], axis=1)

    loss_part, grad_x, gw_in, gw_out, small = _local_step(
        x[0], positions[0].reshape(L, 1), loss_target[0], w, w_out_full, conv_w_full, conv_b, dt_bias, a_log, d_skip,
        ssd_norm_w, attn_sinks, ln_g, ln_b)

    packed = _pack_small(small["conv_w"], [small[n] for n in SMALL_NAMES], loss_part)

    gw_in_slabs = jnp.stack([jnp.pad(gw_in[W_IN_COLS * j:W_IN_COLS * (j + 1)], pad_rows) for j in range(N_CHIPS)])
    gw_out_slabs = gw_out.reshape(N_CHIPS, W_OUT_ROWS, D_MODEL)
    core_id = mc.reshape(1).astype(jnp.int32)
    chip_id = chip.reshape(1).astype(jnp.int32)
    recv_in, recv_out, slots = _pair_exchange(gw_in_slabs, gw_out_slabs, packed)
    s_in = _pair_add(gw_in_slabs, recv_in, core_id, "pair_add_in")
    s_out = _pair_add(gw_out_slabs, recv_out, core_id, "pair_add_out")
    r_in, r_out = _chip_exchange(s_in, s_out)
    h_in = _chip_add(s_in, r_in, chip_id, "chip_add_in")
    h_out = _chip_add(s_out, r_out, chip_id, "chip_add_out")
    sib_in, sib_out = _pair_share(h_in, h_out)

    to_rows = lambda a: jnp.transpose(a, (2, 0, 1))
    in_t = _adamw_pair(to_rows(w_in), h_in, sib_in, to_rows(m_w_in), to_rows(v_w_in), core_id, "adamw_w_in")
    d_w_in, nm_w_in, nv_w_in, g_w_in = [jnp.transpose(a, (1, 2, 0)) for a in in_t]
    out_t = _adamw_pair(w_out[0], h_out, sib_out, m_w_out[0], v_w_out[0], core_id, "adamw_w_out")
    d_w_out, nm_w_out, nv_w_out, g_w_out = [a[None] for a in out_t]

    params = dict(conv_b=conv_b, ssd_norm_w=ssd_norm_w, ln_g=ln_g, ln_b=ln_b, dt_bias=dt_bias, a_log=a_log,
                  d_skip=d_skip, attn_sinks=attn_sinks)
    moms = dict(conv_b=m_conv_b, ssd_norm_w=m_ssd_norm_w, ln_g=m_ln_g, ln_b=m_ln_b, dt_bias=m_dt_bias, a_log=m_a_log,
                d_skip=m_d_skip, attn_sinks=m_attn_sinks)
    vars_ = dict(conv_b=v_conv_b, ssd_norm_w=v_ssd_norm_w, ln_g=v_ln_g, ln_b=v_ln_b, dt_bias=v_dt_bias, a_log=v_a_log,
                 d_skip=v_d_skip, attn_sinks=v_attn_sinks)
    res = _adamw_small(slots, chip_id, conv_w, m_conv_w, v_conv_w, [params[n] for n in SMALL_NAMES],
                       [moms[n] for n in SMALL_NAMES], [vars_[n] for n in SMALL_NAMES])
    loss = res[0][0, 0]
    grads, delta, new_m, new_v = {}, {}, {}, {}
    for k, n in enumerate(("conv_w",) + SMALL_NAMES):
        grads[n], delta[n], new_m[n], new_v[n] = res[1 + 4 * k:5 + 4 * k]
    for dd, a_in, a_out in ((grads, g_w_in, g_w_out), (delta, d_w_in, d_w_out), (new_m, nm_w_in, nm_w_out),
                            (new_v, nv_w_in, nv_w_out)):
        dd["w_in"] = a_in
        dd["w_out"] = a_out
    order = ("w_in", "conv_w", "conv_b", "dt_bias", "a_log", "d_skip", "ssd_norm_w", "attn_sinks", "w_out", "ln_g", "ln_b")
    return (loss, grad_x[None], *[grads[n] for n in order], *[delta[n] for n in order], *[new_m[n] for n in order],
            *[new_v[n] for n in order])
```

```python
import functools

import numpy as np
import jax
import jax.numpy as jnp
from jax import lax
from jax.experimental import pallas as pl
from jax.experimental.pallas import tpu as pltpu

F32 = jnp.float32
BF16 = jnp.bfloat16
MESH = pl.DeviceIdType.MESH

D_MODEL = 1024
D_SSD = 1024
D_ATT = 1024
D_MIX = 2048
SSD_HEADS = 16
SSD_P = 64
SSD_GROUPS = 2
SSD_R = 8
SSD_N = 128
D_BC = 256
D_XBC = 1536
CONV_K = 4
CHUNK = 128
ATT_HD = 64
ATT_QH = 16
ATT_KVH = 4
ATT_R = 4
D_KV = 256
WINDOW = 128
ROPE_THETA = 500000.0
ROPE_DIM = 16
ALPHA = 2.0 ** 0.25
LN_EPS = 1e-5
RMS_EPS = 1e-5
D_IN_PROJ = 5136
O_Z, O_XBC, O_DT, O_Q, O_K, O_V, O_G = 0, 1024, 2560, 2576, 3600, 3856, 4112
P_Z, P_G, P_Q, P_XBC, P_KV, P_DT, P_END = 0, 1024, 2048, 3072, 4608, 5120, 5248
DT_PAD = 128
N_CHIPS = 4
W_IN_COLS = D_IN_PROJ // N_CHIPS
SLAB_ROWS = 1312
W_OUT_ROWS = D_MIX // N_CHIPS
CONV_COLS = D_XBC // N_CHIPS

ADAM_LR = 0.001
ADAM_B1 = 0.9
ADAM_B2 = 0.999
ADAM_EPS = 1e-08
ADAM_WD = 0.01
ADAM_STEP = 10

VMEM_LIMIT = 56 * 1024 * 1024
ROW_TILE = 512
NEG_BIG = -1e30
HI = lax.Precision.HIGHEST


def _cparams(sem=None, **kw):
    if sem is not None:
        kw["dimension_semantics"] = sem
    return pltpu.CompilerParams(vmem_limit_bytes=VMEM_LIMIT, **kw)


def _dot(a, b):
    return jnp.dot(a, b, preferred_element_type=F32)


def _dot_nt(a, b):
    return lax.dot_general(a, b, (((1,), (1,)), ((), ())), preferred_element_type=F32)


def _dot_tn(a, b):
    return lax.dot_general(a, b, (((0,), (0,)), ((), ())), preferred_element_type=F32)


def _bf(a):
    return a.astype(BF16)


def _iota2(shape, dim):
    return lax.broadcasted_iota(jnp.int32, shape, dim)


def _to_rows(col):
    k = col.shape[1]
    eye = (_iota2((k, k), 0) == _iota2((k, k), 1)).astype(F32)
    return lax.dot_general(eye, col, (((1,), (1,)), ((), ())), preferred_element_type=F32, precision=HI)


def _to_cols(row):
    n = row.shape[1]
    eye = (_iota2((n, n), 0) == _iota2((n, n), 1)).astype(F32)
    return lax.dot_general(eye, row, (((1,), (1,)), ((), ())), preferred_element_type=F32, precision=HI)


def _sigmoid(x):
    return jax.nn.sigmoid(x)


def _in_proj(x, w, pos, inv):
    L = x.shape[0]
    tm = ROW_TILE
    widths = (D_SSD, D_ATT, D_ATT, D_XBC, 2 * D_KV, DT_PAD)

    def body(x_ref, w_ref, pos_ref, inv_ref, z_ref, g_ref, q_ref, xbc_ref, kv_ref, dt_ref, xb_ref):
        xb = _bf(x_ref[...])
        xb_ref[...] = xb
        for o_ref, off, wd in zip((z_ref, g_ref, xbc_ref, dt_ref), (P_Z, P_G, P_XBC, P_DT), (D_SSD, D_ATT, D_XBC, DT_PAD)):
            o_ref[...] = _dot_nt(xb, w_ref[off:off + wd, :])
        tabs = _rope_tables(pos_ref, inv_ref)
        q_ref[...] = _bf(_rope(_dot_nt(xb, w_ref[P_Q:P_Q + D_ATT, :]), tabs))
        kv_ref[:, 0:D_KV] = _bf(_rope(_dot_nt(xb, w_ref[P_KV:P_KV + D_KV, :]), tabs))
        kv_ref[:, D_KV:2 * D_KV] = _bf(_dot_nt(xb, w_ref[P_KV + D_KV:P_KV + 2 * D_KV, :]))

    row = lambda wd: pl.BlockSpec((tm, wd), lambda i: (i, 0))
    return pl.pallas_call(
        body, name="in_proj", grid=(L // tm,),
        in_specs=[row(D_MODEL), pl.BlockSpec((P_END, D_MODEL), lambda i: (0, 0), pipeline_mode=pl.Buffered(1)), row(1),
                  pl.BlockSpec((1, 2 * ATT_HD), lambda i: (0, 0))],
        out_specs=[row(wd) for wd in widths] + [row(D_MODEL)],
        out_shape=[jax.ShapeDtypeStruct((L, wd), dt) for wd, dt in zip(widths, (F32, F32, BF16, F32, BF16, F32))]
        + [jax.ShapeDtypeStruct((L, D_MODEL), BF16)],
        compiler_params=_cparams(("parallel",)),
    )(x, w, pos, inv)


def _matmul_tn(a, b, tn, name):
    K, M = a.shape
    N = b.shape[1]
    tk = min(K, 2048 if M <= 1024 else 1024)
    nk = K // tk

    def body(a_ref, b_ref, o_ref, acc_ref):
        k = pl.program_id(1)

        @pl.when(k == 0)
        def _():
            acc_ref[...] = jnp.zeros_like(acc_ref)

        acc_ref[...] += _dot_tn(_bf(a_ref[...]), _bf(b_ref[...]))

        @pl.when(k == nk - 1)
        def _():
            o_ref[...] = acc_ref[...]

    return pl.pallas_call(
        body, name=name, grid=(N // tn, nk),
        in_specs=[pl.BlockSpec((tk, M), lambda j, k: (k, 0)), pl.BlockSpec((tk, tn), lambda j, k: (k, j))],
        out_specs=pl.BlockSpec((M, tn), lambda j, k: (0, j)),
        out_shape=jax.ShapeDtypeStruct((M, N), F32),
        scratch_shapes=[pltpu.VMEM((M, tn), F32)],
        compiler_params=_cparams(("parallel", "arbitrary")),
    )(a, b)


def _grad_x(dr, dz, dg, dq, dxbc, dkv, ddt, w):
    L = dr.shape[0]
    tm = ROW_TILE
    widths = (D_SSD, D_ATT, D_ATT, D_XBC, 2 * D_KV, DT_PAD)
    offs = (P_Z, P_G, P_Q, P_XBC, P_KV, P_DT)

    def body(dr_ref, dz_ref, dg_ref, dq_ref, dxbc_ref, dkv_ref, ddt_ref, w_ref, o_ref):
        acc = ALPHA * dr_ref[...]
        for p_ref, off, wd in zip((dz_ref, dg_ref, dq_ref, dxbc_ref, dkv_ref, ddt_ref), offs, widths):
            acc = acc + _dot(_bf(p_ref[...]), w_ref[off:off + wd, :])
        o_ref[...] = acc

    row = lambda wd: pl.BlockSpec((tm, wd), lambda i: (i, 0))
    return pl.pallas_call(
        body, name="grad_x", grid=(L // tm,),
        in_specs=[row(D_MODEL)] + [row(wd) for wd in widths] + [pl.BlockSpec((P_END, D_MODEL), lambda i: (0, 0), pipeline_mode=pl.Buffered(1))],
        out_specs=row(D_MODEL),
        out_shape=jax.ShapeDtypeStruct((L, D_MODEL), F32),
        compiler_params=_cparams(("parallel",)),
    )(dr, dz, dg, dq, dxbc, dkv, ddt, w)


def _ssd_chunk_pre(first, xbc_ref, tail_ref, dt_ref, cw_ref, cb_ref, dtb_ref, alog_ref, ext):
    tail = jnp.where(first, 0.0, tail_ref[...])
    ext[0:8, :] = tail
    ext[8:8 + CHUNK, :] = xbc_ref[...]
    u = cb_ref[...] + cw_ref[0:1, :] * ext[pl.ds(5, CHUNK), :]
    for k in range(1, CONV_K):
        u = u + cw_ref[k:k + 1, :] * ext[pl.ds(5 + k, CHUNK), :]
    sig = _sigmoid(u)
    xbc = u * sig
    dtraw = dt_ref[:, 0:SSD_HEADS] + dtb_ref[...]
    dt = jax.nn.softplus(dtraw)
    A = -jnp.exp(alog_ref[...])
    a = dt * A
    tril = (_iota2((CHUNK, CHUNK), 0) >= _iota2((CHUNK, CHUNK), 1)).astype(F32)
    acs = jnp.dot(tril, a, preferred_element_type=F32, precision=HI)
    acs_row = _to_rows(acs)
    return u, sig, xbc, dtraw, dt, A, acs, acs_row


def _ssd_fwd(z, xbc, dtp, conv_w, conv_b, dt_bias, a_log, d_skip, norm_w):
    L = z.shape[0]
    nc = L // CHUNK

    def body(z_ref, xbc_ref, tail_ref, dt_ref, cw_ref, cb_ref, dtb_ref, alog_ref, dsk_ref, nw_ref,
             y_ref, ypre_ref, prev_ref, state, ext, ybuf):
        c = pl.program_id(0)

        @pl.when(c == 0)
        def _():
            state[...] = jnp.zeros_like(state)

        u, sig, xbcv, dtraw, dt, A, acs, acs_row = _ssd_chunk_pre(
            c == 0, xbc_ref, tail_ref, dt_ref, cw_ref, cb_ref, dtb_ref, alog_ref, ext)
        prev_ref[0] = state[...]
        causal = _iota2((CHUNK, CHUNK), 0) >= _iota2((CHUNK, CHUNK), 1)
        alast = acs[CHUNK - 1:CHUNK, :]
        for g in range(SSD_GROUPS):
            Bg = _bf(xbcv[:, D_SSD + SSD_N * g:D_SSD + SSD_N * (g + 1)])
            Cg = _bf(xbcv[:, D_SSD + D_BC + SSD_N * g:D_SSD + D_BC + SSD_N * (g + 1)])
            cb = _dot_nt(Cg, Bg)
            for r in range(SSD_R):
                h = g * SSD_R + r
                hs = slice(SSD_P * h, SSD_P * (h + 1))
                acs_c = acs[:, h:h + 1]
                seg = acs_c - acs_row[h:h + 1, :]
                Lm = jnp.where(causal, jnp.exp(jnp.where(causal, seg, 0.0)), 0.0)
                M = cb * Lm
                xh = xbcv[:, hs]
                X = xh * dt[:, h:h + 1]
                prev_h = state[hs, :]
                ydiag = _dot(_bf(M), _bf(X))
                yoff = _dot_nt(Cg, _bf(prev_h)) * jnp.exp(acs_c)
                al = alast[:, h:h + 1]
                Xd = X * jnp.exp(al - acs_c)
                state[hs, :] = prev_h * jnp.exp(al) + _dot_tn(_bf(Xd), Bg)
                ybuf[:, hs] = ydiag + yoff + dsk_ref[:, h:h + 1] * xh
        y = ybuf[...]
        ypre_ref[...] = y
        zv = z_ref[...]
        yf = y * (zv * _sigmoid(zv))
        half = D_SSD // SSD_GROUPS
        for g in range(SSD_GROUPS):
            gs = slice(half * g, half * (g + 1))
            yg = yf[:, gs]
            ms = jnp.mean(yg * yg, axis=-1, keepdims=True)
            y_ref[:, gs] = _bf(yg * lax.rsqrt(ms + RMS_EPS) * nw_ref[:, gs])

    full = lambda shape: pl.BlockSpec(shape, lambda c: (0, 0))
    return pl.pallas_call(
        body, name="ssd_fwd", grid=(nc,),
        in_specs=[
            pl.BlockSpec((CHUNK, D_SSD), lambda c: (c, 0)),
            pl.BlockSpec((CHUNK, D_XBC), lambda c: (c, 0)),
            pl.BlockSpec((8, D_XBC), lambda c: (jnp.maximum(c * (CHUNK // 8) - 1, 0), 0)),
            pl.BlockSpec((CHUNK, DT_PAD), lambda c: (c, 0)),
            full((CONV_K, D_XBC)), full((1, D_XBC)), full((1, SSD_HEADS)), full((1, SSD_HEADS)), full((1, SSD_HEADS)),
            full((1, D_SSD)),
        ],
        out_specs=[
            pl.BlockSpec((CHUNK, D_SSD), lambda c: (c, 0)),
            pl.BlockSpec((CHUNK, D_SSD), lambda c: (c, 0)),
            pl.BlockSpec((1, SSD_HEADS * SSD_P, SSD_N), lambda c: (c, 0, 0)),
        ],
        out_shape=[
            jax.ShapeDtypeStruct((L, D_SSD), F32),
            jax.ShapeDtypeStruct((L, D_SSD), F32),
            jax.ShapeDtypeStruct((nc, SSD_HEADS * SSD_P, SSD_N), F32),
        ],
        scratch_shapes=[
            pltpu.VMEM((SSD_HEADS * SSD_P, SSD_N), F32),
            pltpu.VMEM((CHUNK + 8, D_XBC), F32),
            pltpu.VMEM((CHUNK, D_SSD), F32),
        ],
        compiler_params=_cparams(("arbitrary",)),
    )(z, xbc, xbc, dtp, conv_w, conv_b, dt_bias, a_log, d_skip, norm_w)


def _ssd_bwd(dy, z, ypre, xbc, dtp, prev, conv_w, conv_b, dt_bias, a_log, d_skip, norm_w):
    L = z.shape[0]
    nc = L // CHUNK

    def body(dy_ref, z_ref, ypre_ref, xbc_ref, tail_ref, dt_ref, prev_ref, cw_ref, cb_ref, dtb_ref, alog_ref, dsk_ref,
             nw_ref, dz_ref, dxbc_ref, ddt_ref, gcw_ref, gcb_ref, gdtb_ref, galog_ref, gdsk_ref, gnw_ref,
             dstate, dhead, ext, ext2, dpost):
        i = pl.program_id(0)
        c = nc - 1 - i

        @pl.when(i == 0)
        def _():
            dstate[...] = jnp.zeros_like(dstate)
            dhead[...] = jnp.zeros_like(dhead)
            gcw_ref[...] = jnp.zeros_like(gcw_ref)
            gcb_ref[...] = jnp.zeros_like(gcb_ref)
            gdtb_ref[...] = jnp.zeros_like(gdtb_ref)
            galog_ref[...] = jnp.zeros_like(galog_ref)
            gdsk_ref[...] = jnp.zeros_like(gdsk_ref)
            gnw_ref[...] = jnp.zeros_like(gnw_ref)

        u, sig, xbcv, dtraw, dt, A, acs, acs_row = _ssd_chunk_pre(
            c == 0, xbc_ref, tail_ref, dt_ref, cw_ref, cb_ref, dtb_ref, alog_ref, ext)

        zv = z_ref[...]
        ypre = ypre_ref[...]
        dyn = dy_ref[...]
        sz = _sigmoid(zv)
        silu_z = zv * sz
        yf = ypre * silu_z
        half = D_SSD // SSD_GROUPS
        dyf_parts = []
        for g in range(SSD_GROUPS):
            gs = slice(half * g, half * (g + 1))
            yg = yf[:, gs]
            rstd = lax.rsqrt(jnp.mean(yg * yg, axis=-1, keepdims=True) + RMS_EPS)
            dout = dyn[:, gs]
            gnw_ref[:, gs] += jnp.sum(dout * yg * rstd, axis=0, keepdims=True)
            dyhat = dout * nw_ref[:, gs]
            dyf_parts.append(rstd * (dyhat - yg * (rstd * rstd) * jnp.mean(dyhat * yg, axis=-1, keepdims=True)))
        dyf = jnp.concatenate(dyf_parts, axis=1)
        dz_ref[...] = _bf(dyf * ypre * (sz * (1.0 + zv * (1.0 - sz))))
        dypre = dyf * silu_z

        causal = _iota2((CHUNK, CHUNK), 0) >= _iota2((CHUNK, CHUNK), 1)
        alast = acs[CHUNK - 1:CHUNK, :]
        lane16 = _iota2((1, SSD_HEADS), 1)
        sub16 = _iota2((SSD_HEADS, 1), 0)
        dacs_col = jnp.zeros((CHUNK, SSD_HEADS), F32)
        dacs_row = jnp.zeros((SSD_HEADS, CHUNK), F32)
        ddt_col = jnp.zeros((CHUNK, SSD_HEADS), F32)
        dalast = jnp.zeros((1, SSD_HEADS), F32)
        gdsk = jnp.zeros((1, SSD_HEADS), F32)
        for g in range(SSD_GROUPS):
            bs = slice(D_SSD + SSD_N * g, D_SSD + SSD_N * (g + 1))
            cs = slice(D_SSD + D_BC + SSD_N * g, D_SSD + D_BC + SSD_N * (g + 1))
            Bg = _bf(xbcv[:, bs])
            Cg = _bf(xbcv[:, cs])
            cb = _dot_nt(Cg, Bg)
            dcb = jnp.zeros((CHUNK, CHUNK), F32)
            dB = jnp.zeros((CHUNK, SSD_N), F32)
            dC = jnp.zeros((CHUNK, SSD_N), F32)
            for r in range(SSD_R):
                h = g * SSD_R + r
                hs = slice(SSD_P * h, SSD_P * (h + 1))
                onehot = (lane16 == h).astype(F32)
                acs_c = acs[:, h:h + 1]
                seg = acs_c - acs_row[h:h + 1, :]
                Lm = jnp.where(causal, jnp.exp(jnp.where(causal, seg, 0.0)), 0.0)
                M = cb * Lm
                xh = xbcv[:, hs]
                dth = dt[:, h:h + 1]
                X = xh * dth
                Xb = _bf(X)
                dyh = dypre[:, hs]
                dyb = _bf(dyh)
                prev_h = prev_ref[0, hs, :]
                prevb = _bf(prev_h)
                dnext = dstate[hs, :]
                dnextb = _bf(dnext)
                al = alast[:, h:h + 1]
                eacs = jnp.exp(acs_c)
                eal = jnp.exp(al)
                dsd = jnp.exp(al - acs_c)
                G = _bf(dyh * eacs)
                dstate[hs, :] = dnext * eal + _dot_tn(G, Cg)
                dC = dC + _dot(G, prevb)
                yoff = _dot_nt(Cg, prevb) * eacs
                dacs_h = jnp.sum(dyh * yoff, axis=-1, keepdims=True)
                BdN = _dot_nt(Bg, dnextb)
                dX = dsd * BdN
                dB = dB + _dot(_bf(X * dsd), dnextb)
                t = jnp.sum(X * BdN, axis=-1, keepdims=True) * dsd
                dacs_h = dacs_h - t
                dal = jnp.sum(t, axis=0, keepdims=True) + jnp.sum(
                    jnp.sum(dnext * prev_h, axis=-1, keepdims=True), axis=0, keepdims=True) * eal
                dM = _dot_nt(dyb, Xb)
                dX = dX + _dot_tn(_bf(M), dyb)
                dseg = dM * M
                dcb = dcb + dM * Lm
                dacs_h = dacs_h + jnp.sum(dseg, axis=-1, keepdims=True)
                dacs_row = dacs_row - jnp.sum(dseg, axis=0, keepdims=True) * (sub16 == h).astype(F32)
                dacs_col = dacs_col + dacs_h * onehot
                dalast = dalast + dal * onehot
                ddt_col = ddt_col + jnp.sum(dX * xh, axis=-1, keepdims=True) * onehot
                gdsk = gdsk + jnp.sum(jnp.sum(dyh * xh, axis=-1, keepdims=True), axis=0, keepdims=True) * onehot
                dpost[:, hs] = dX * dth + dsk_ref[:, h:h + 1] * dyh
            dcbb = _bf(dcb)
            dpost[:, bs] = dB + _dot_tn(dcbb, Cg)
            dpost[:, cs] = dC + _dot(dcbb, Bg)

        is_last = (_iota2((CHUNK, 1), 0) == CHUNK - 1).astype(F32)
        dacs = dacs_col + _to_cols(dacs_row) + is_last * dalast
        triu = (_iota2((CHUNK, CHUNK), 0) <= _iota2((CHUNK, CHUNK), 1)).astype(F32)
        da = jnp.dot(triu, dacs, preferred_element_type=F32, precision=HI)
        ddt_tot = ddt_col + da * A
        galog_ref[...] += jnp.sum(da * dt, axis=0, keepdims=True) * A
        ddtraw = ddt_tot * _sigmoid(dtraw)
        gdtb_ref[...] += jnp.sum(ddtraw, axis=0, keepdims=True)
        gdsk_ref[...] += gdsk
        ddt_ref[...] = jnp.zeros_like(ddt_ref)
        ddt_ref[:, 0:SSD_HEADS] = ddtraw

        dconv = dpost[...] * (sig * (1.0 + u * (1.0 - sig)))
        gcb_ref[...] += jnp.sum(dconv, axis=0, keepdims=True)
        for k in range(CONV_K):
            gcw_ref[k:k + 1, :] += jnp.sum(dconv * ext[pl.ds(5 + k, CHUNK), :], axis=0, keepdims=True)
        ext2[0:CHUNK, :] = dconv
        ext2[CHUNK:CHUNK + 8, :] = dhead[...]
        dx = cw_ref[CONV_K - 1:CONV_K, :] * dconv
        for k in range(CONV_K - 1):
            dx = dx + cw_ref[k:k + 1, :] * ext2[pl.ds(CONV_K - 1 - k, CHUNK), :]
        dxbc_ref[...] = _bf(dx)
        dhead[...] = dconv[0:8, :]

    full = lambda shape: pl.BlockSpec(shape, lambda i: (0, 0))
    rev = lambda wd: pl.BlockSpec((CHUNK, wd), lambda i: (nc - 1 - i, 0))
    return pl.pallas_call(
        body, name="ssd_bwd", grid=(nc,),
        in_specs=[
            rev(D_SSD), rev(D_SSD), rev(D_SSD), rev(D_XBC),
            pl.BlockSpec((8, D_XBC), lambda i: (jnp.maximum((nc - 1 - i) * (CHUNK // 8) - 1, 0), 0)),
            rev(DT_PAD),
            pl.BlockSpec((1, SSD_HEADS * SSD_P, SSD_N), lambda i: (nc - 1 - i, 0, 0)),
            full((CONV_K, D_XBC)), full((1, D_XBC)), full((1, SSD_HEADS)), full((1, SSD_HEADS)), full((1, SSD_HEADS)),
            full((1, D_SSD)),
        ],
        out_specs=[
            rev(D_SSD), rev(D_XBC), rev(DT_PAD),
            full((CONV_K, D_XBC)), full((1, D_XBC)), full((1, SSD_HEADS)), full((1, SSD_HEADS)), full((1, SSD_HEADS)),
            full((1, D_SSD)),
        ],
        out_shape=[
            jax.ShapeDtypeStruct((L, D_SSD), BF16), jax.ShapeDtypeStruct((L, D_XBC), BF16),
            jax.ShapeDtypeStruct((L, DT_PAD), F32),
            jax.ShapeDtypeStruct((CONV_K, D_XBC), F32), jax.ShapeDtypeStruct((1, D_XBC), F32),
            jax.ShapeDtypeStruct((1, SSD_HEADS), F32), jax.ShapeDtypeStruct((1, SSD_HEADS), F32),
            jax.ShapeDtypeStruct((1, SSD_HEADS), F32), jax.ShapeDtypeStruct((1, D_SSD), F32),
        ],
        scratch_shapes=[
            pltpu.VMEM((SSD_HEADS * SSD_P, SSD_N), F32),
            pltpu.VMEM((8, D_XBC), F32),
            pltpu.VMEM((CHUNK + 8, D_XBC), F32),
            pltpu.VMEM((CHUNK + 8, D_XBC), F32),
            pltpu.VMEM((CHUNK, D_XBC), F32),
        ],
        compiler_params=_cparams(("arbitrary",)),
    )(dy, z, ypre, xbc, xbc, dtp, prev, conv_w, conv_b, dt_bias, a_log, d_skip, norm_w)


def _head_expander():
    return (_iota2((SSD_HEADS, D_SSD), 1) // SSD_P == _iota2((SSD_HEADS, D_SSD), 0)).astype(BF16)


def _hi_lo(x):
    hi = _bf(x)
    return hi, _bf(x - hi.astype(F32))


def _expand(v, e):
    hi, lo = _hi_lo(v)
    return _dot(hi, e) + _dot(lo, e)


def _headsum(t, e):
    m = t.shape[0]
    if m < 8:
        t = jnp.broadcast_to(t[0:1], (8, t.shape[1]))
    hi, lo = _hi_lo(t)
    return (_dot_nt(hi, e) + _dot_nt(lo, e))[0:m]


def _ssd_decays(dt, acs, dsk_ref, e):
    alast = acs[CHUNK - 1:CHUNK, :]
    stk = jnp.concatenate([dt, jnp.exp(acs), jnp.exp(alast - acs),
                           jnp.broadcast_to(jnp.exp(alast), (8, SSD_HEADS)),
                           jnp.broadcast_to(dsk_ref[...], (8, SSD_HEADS))], axis=0)
    ex = _expand(stk, e)
    return (ex[0:CHUNK], ex[CHUNK:2 * CHUNK], ex[2 * CHUNK:3 * CHUNK], ex[3 * CHUNK:3 * CHUNK + 1],
            ex[3 * CHUNK + 8:3 * CHUNK + 9])


def _ssd_fwd2(z, xbc, dtp, conv_w, conv_b, dt_bias, a_log, d_skip, norm_w):
    L = z.shape[0]
    nc = L // CHUNK
    half = D_SSD // SSD_GROUPS

    def body(z_ref, xbc_ref, tail_ref, dt_ref, cw_ref, cb_ref, dtb_ref, alog_ref, dsk_ref, nw_ref,
             y_ref, ypre_ref, prev_ref, state, ext, ybuf, mbuf):
        c = pl.program_id(0)

        @pl.when(c == 0)
        def _():
            state[...] = jnp.zeros_like(state)

        u, sig, xbcv, dtraw, dt, A, acs, acs_row = _ssd_chunk_pre(
            c == 0, xbc_ref, tail_ref, dt_ref, cw_ref, cb_ref, dtb_ref, alog_ref, ext)
        e = _head_expander()
        dtE, eacsE, dsdE, ealE, dskE = _ssd_decays(dt, acs, dsk_ref, e)
        xs = xbcv[:, 0:D_SSD]
        X = xs * dtE
        prev_ref[0] = state[...]
        causal = _iota2((CHUNK, CHUNK), 0) >= _iota2((CHUNK, CHUNK), 1)
        for g in range(SSD_GROUPS):
            gs = slice(half * g, half * (g + 1))
            Bg = _bf(xbcv[:, D_SSD + SSD_N * g:D_SSD + SSD_N * (g + 1)])
            Cg = _bf(xbcv[:, D_SSD + D_BC + SSD_N * g:D_SSD + D_BC + SSD_N * (g + 1)])
            cb = _dot_nt(Cg, Bg)
            for r in range(SSD_R):
                h = g * SSD_R + r
                seg = acs[:, h:h + 1] - acs_row[h:h + 1, :]
                mbuf[h] = _bf(cb * jnp.where(causal, jnp.exp(jnp.where(causal, seg, 0.0)), 0.0))
            st = state[:, gs]
            ybuf[:, gs] = _dot(Cg, _bf(st)) * eacsE[:, gs] + dskE[:, gs] * xs[:, gs]
            state[:, gs] = st * ealE[:, gs] + _dot_tn(Bg, _bf(X[:, gs] * dsdE[:, gs]))
        Xb = _bf(X)
        for h in range(SSD_HEADS):
            hs = slice(SSD_P * h, SSD_P * (h + 1))
            ybuf[:, hs] += _dot(mbuf[h], Xb[:, hs])
        y = ybuf[...]
        ypre_ref[...] = y
        zv = z_ref[...]
        yf = y * (zv * _sigmoid(zv))
        for g in range(SSD_GROUPS):
            gs = slice(half * g, half * (g + 1))
            yg = yf[:, gs]
            ms = jnp.mean(yg * yg, axis=-1, keepdims=True)
            y_ref[:, gs] = _bf(yg * lax.rsqrt(ms + RMS_EPS) * nw_ref[:, gs])

    full = lambda shape: pl.BlockSpec(shape, lambda c: (0, 0))
    return pl.pallas_call(
        body, name="ssd_fwd", grid=(nc,),
        in_specs=[
            pl.BlockSpec((CHUNK, D_SSD), lambda c: (c, 0)),
            pl.BlockSpec((CHUNK, D_XBC), lambda c: (c, 0)),
            pl.BlockSpec((8, D_XBC), lambda c: (jnp.maximum(c * (CHUNK // 8) - 1, 0), 0)),
            pl.BlockSpec((CHUNK, DT_PAD), lambda c: (c, 0)),
            full((CONV_K, D_XBC)), full((1, D_XBC)), full((1, SSD_HEADS)), full((1, SSD_HEADS)), full((1, SSD_HEADS)),
            full((1, D_SSD)),
        ],
        out_specs=[
            pl.BlockSpec((CHUNK, D_SSD), lambda c: (c, 0)),
            pl.BlockSpec((CHUNK, D_SSD), lambda c: (c, 0)),
            pl.BlockSpec((1, SSD_N, D_SSD), lambda c: (c, 0, 0)),
        ],
        out_shape=[
            jax.ShapeDtypeStruct((L, D_SSD), BF16),
            jax.ShapeDtypeStruct((L, D_SSD), F32),
            jax.ShapeDtypeStruct((nc, SSD_N, D_SSD), F32),
        ],
        scratch_shapes=[
            pltpu.VMEM((SSD_N, D_SSD), F32),
            pltpu.VMEM((CHUNK + 8, D_XBC), F32),
            pltpu.VMEM((CHUNK, D_SSD), F32),
            pltpu.VMEM((SSD_HEADS, CHUNK, CHUNK), BF16),
        ],
        compiler_params=_cparams(("arbitrary",)),
    )(z, xbc, xbc, dtp, conv_w, conv_b, dt_bias, a_log, d_skip, norm_w)


def _ssd_bwd2(dy, z, ypre, xbc, dtp, prev, conv_w, conv_b, dt_bias, a_log, d_skip, norm_w):
    L = z.shape[0]
    nc = L // CHUNK
    half = D_SSD // SSD_GROUPS

    def body(dy_ref, z_ref, ypre_ref, xbc_ref, tail_ref, dt_ref, prev_ref, cw_ref, cb_ref, dtb_ref, alog_ref, dsk_ref,
             nw_ref, dz_ref, dxbc_ref, ddt_ref, gcw_ref, gcb_ref, gdtb_ref, galog_ref, gdsk_ref, gnw_ref,
             dstate, dhead, ext, ext2, dpost, yobuf, bdbuf, lmbuf, dmbuf, cbbuf):
        i = pl.program_id(0)
        c = nc - 1 - i

        @pl.when(i == 0)
        def _():
            dstate[...] = jnp.zeros_like(dstate)
            dhead[...] = jnp.zeros_like(dhead)
            gcw_ref[...] = jnp.zeros_like(gcw_ref)
            gcb_ref[...] = jnp.zeros_like(gcb_ref)
            gdtb_ref[...] = jnp.zeros_like(gdtb_ref)
            galog_ref[...] = jnp.zeros_like(galog_ref)
            gdsk_ref[...] = jnp.zeros_like(gdsk_ref)
            gnw_ref[...] = jnp.zeros_like(gnw_ref)

        u, sig, xbcv, dtraw, dt, A, acs, acs_row = _ssd_chunk_pre(
            c == 0, xbc_ref, tail_ref, dt_ref, cw_ref, cb_ref, dtb_ref, alog_ref, ext)
        e = _head_expander()
        dtE, eacsE, dsdE, ealE, dskE = _ssd_decays(dt, acs, dsk_ref, e)
        alast = acs[CHUNK - 1:CHUNK, :]
        xs = xbcv[:, 0:D_SSD]
        X = xs * dtE
        Xb = _bf(X)

        zv = z_ref[...]
        ypre = ypre_ref[...]
        dyn = dy_ref[...]
        sz = _sigmoid(zv)
        silu_z = zv * sz
        yf = ypre * silu_z
        dyf_parts = []
        for g in range(SSD_GROUPS):
            gs = slice(half * g, half * (g + 1))
            yg = yf[:, gs]
            rstd = lax.rsqrt(jnp.mean(yg * yg, axis=-1, keepdims=True) + RMS_EPS)
            dout = dyn[:, gs]
            gnw_ref[:, gs] += jnp.sum(dout * yg * rstd, axis=0, keepdims=True)
            dyhat = dout * nw_ref[:, gs]
            dyf_parts.append(rstd * (dyhat - yg * (rstd * rstd) * jnp.mean(dyhat * yg, axis=-1, keepdims=True)))
        dyf = jnp.concatenate(dyf_parts, axis=1)
        dz_ref[...] = _bf(dyf * ypre * (sz * (1.0 + zv * (1.0 - sz))))
        dyp = dyf * silu_z
        dyb = _bf(dyp)
        G = dyp * eacsE

        causal = _iota2((CHUNK, CHUNK), 0) >= _iota2((CHUNK, CHUNK), 1)
        ST = prev_ref[0]
        dST = dstate[...]
        for g in range(SSD_GROUPS):
            gs = slice(half * g, half * (g + 1))
            bs = slice(D_SSD + SSD_N * g, D_SSD + SSD_N * (g + 1))
            cs = slice(D_SSD + D_BC + SSD_N * g, D_SSD + D_BC + SSD_N * (g + 1))
            Bg = _bf(xbcv[:, bs])
            Cg = _bf(xbcv[:, cs])
            Gb = _bf(G[:, gs])
            STb = _bf(ST[:, gs])
            dSTb = _bf(dST[:, gs])
            dstate[:, gs] = dST[:, gs] * ealE[:, gs] + _dot_tn(Cg, Gb)
            yobuf[:, gs] = _dot(Cg, STb) * eacsE[:, gs]
            bdbuf[:, gs] = _dot(Bg, dSTb)
            dpost[:, cs] = _dot_nt(Gb, STb)
            dpost[:, bs] = _dot_nt(_bf(X[:, gs] * dsdE[:, gs]), dSTb)
            cbbuf[g] = _dot_nt(Cg, Bg)
            for r in range(SSD_R):
                h = g * SSD_R + r
                seg = acs[:, h:h + 1] - acs_row[h:h + 1, :]
                lmbuf[h] = jnp.where(causal, jnp.exp(jnp.where(causal, seg, 0.0)), 0.0)
        for h in range(SSD_HEADS):
            hs = slice(SSD_P * h, SSD_P * (h + 1))
            Mb = _bf(cbbuf[h // SSD_R] * lmbuf[h])
            dmbuf[h] = _dot_nt(dyb[:, hs], Xb[:, hs])
            dpost[:, hs] = _dot_tn(Mb, dyb[:, hs])
        lane16 = _iota2((1, SSD_HEADS), 1)
        sub16 = _iota2((SSD_HEADS, 1), 0)
        dacs_col = jnp.zeros((CHUNK, SSD_HEADS), F32)
        dacs_row = jnp.zeros((SSD_HEADS, CHUNK), F32)
        for g in range(SSD_GROUPS):
            bs = slice(D_SSD + SSD_N * g, D_SSD + SSD_N * (g + 1))
            cs = slice(D_SSD + D_BC + SSD_N * g, D_SSD + D_BC + SSD_N * (g + 1))
            cb = cbbuf[g]
            dcb = jnp.zeros((CHUNK, CHUNK), F32)
            for r in range(SSD_R):
                h = g * SSD_R + r
                dM = dmbuf[h]
                Lm = lmbuf[h]
                dcb = dcb + dM * Lm
                dseg = dM * (cb * Lm)
                dacs_col = dacs_col + jnp.sum(dseg, axis=-1, keepdims=True) * (lane16 == h).astype(F32)
                dacs_row = dacs_row - jnp.sum(dseg, axis=0, keepdims=True) * (sub16 == h).astype(F32)
            dcbb = _bf(dcb)
            dpost[:, bs] += _dot_tn(dcbb, _bf(xbcv[:, cs]))
            dpost[:, cs] += _dot(dcbb, _bf(xbcv[:, bs]))

        BD = bdbuf[...]
        dX = dpost[:, 0:D_SSD] + dsdE * BD
        dsd = jnp.exp(alast - acs)
        T = _headsum(X * BD, e) * dsd
        dalast = jnp.sum(T, axis=0, keepdims=True) + _headsum(
            jnp.sum(dST * ST, axis=0, keepdims=True), e) * jnp.exp(alast)
        is_last = (_iota2((CHUNK, 1), 0) == CHUNK - 1).astype(F32)
        dacs = dacs_col + _to_cols(dacs_row) + _headsum(dyp * yobuf[...], e) - T + is_last * dalast
        triu = (_iota2((CHUNK, CHUNK), 0) <= _iota2((CHUNK, CHUNK), 1)).astype(F32)
        da = jnp.dot(triu, dacs, preferred_element_type=F32, precision=HI)
        ddt_tot = _headsum(dX * xs, e) + da * A
        galog_ref[...] += jnp.sum(da * dt, axis=0, keepdims=True) * A
        ddtraw = ddt_tot * _sigmoid(dtraw)
        gdtb_ref[...] += jnp.sum(ddtraw, axis=0, keepdims=True)
        gdsk_ref[...] += _headsum(jnp.sum(dyp * xs, axis=0, keepdims=True), e)
        ddt_ref[...] = jnp.zeros_like(ddt_ref)
        ddt_ref[:, 0:SSD_HEADS] = ddtraw
        dpost[:, 0:D_SSD] = dX * dtE + dskE * dyp

        dconv = dpost[...] * (sig * (1.0 + u * (1.0 - sig)))
        gcb_ref[...] += jnp.sum(dconv, axis=0, keepdims=True)
        for k in range(CONV_K):
            gcw_ref[k:k + 1, :] += jnp.sum(dconv * ext[pl.ds(5 + k, CHUNK), :], axis=0, keepdims=True)
        ext2[0:CHUNK, :] = dconv
        ext2[CHUNK:CHUNK + 8, :] = dhead[...]
        dx = cw_ref[CONV_K - 1:CONV_K, :] * dconv
        for k in range(CONV_K - 1):
            dx = dx + cw_ref[k:k + 1, :] * ext2[pl.ds(CONV_K - 1 - k, CHUNK), :]
        dxbc_ref[...] = _bf(dx)
        dhead[...] = dconv[0:8, :]

    full = lambda shape: pl.BlockSpec(shape, lambda i: (0, 0))
    rev = lambda wd: pl.BlockSpec((CHUNK, wd), lambda i: (nc - 1 - i, 0))
    return pl.pallas_call(
        body, name="ssd_bwd", grid=(nc,),
        in_specs=[
            rev(D_SSD), rev(D_SSD), rev(D_SSD), rev(D_XBC),
            pl.BlockSpec((8, D_XBC), lambda i: (jnp.maximum((nc - 1 - i) * (CHUNK // 8) - 1, 0), 0)),
            rev(DT_PAD),
            pl.BlockSpec((1, SSD_N, D_SSD), lambda i: (nc - 1 - i, 0, 0)),
            full((CONV_K, D_XBC)), full((1, D_XBC)), full((1, SSD_HEADS)), full((1, SSD_HEADS)), full((1, SSD_HEADS)),
            full((1, D_SSD)),
        ],
        out_specs=[
            rev(D_SSD), rev(D_XBC), rev(DT_PAD),
            full((CONV_K, D_XBC)), full((1, D_XBC)), full((1, SSD_HEADS)), full((1, SSD_HEADS)), full((1, SSD_HEADS)),
            full((1, D_SSD)),
        ],
        out_shape=[
            jax.ShapeDtypeStruct((L, D_SSD), BF16), jax.ShapeDtypeStruct((L, D_XBC), BF16),
            jax.ShapeDtypeStruct((L, DT_PAD), F32),
            jax.ShapeDtypeStruct((CONV_K, D_XBC), F32), jax.ShapeDtypeStruct((1, D_XBC), F32),
            jax.ShapeDtypeStruct((1, SSD_HEADS), F32), jax.ShapeDtypeStruct((1, SSD_HEADS), F32),
            jax.ShapeDtypeStruct((1, SSD_HEADS), F32), jax.ShapeDtypeStruct((1, D_SSD), F32),
        ],
        scratch_shapes=[
            pltpu.VMEM((SSD_N, D_SSD), F32),
            pltpu.VMEM((8, D_XBC), F32),
            pltpu.VMEM((CHUNK + 8, D_XBC), F32),
            pltpu.VMEM((CHUNK + 8, D_XBC), F32),
            pltpu.VMEM((CHUNK, D_XBC), F32),
            pltpu.VMEM((CHUNK, D_SSD), F32),
            pltpu.VMEM((CHUNK, D_SSD), F32),
            pltpu.VMEM((SSD_HEADS, CHUNK, CHUNK), F32),
            pltpu.VMEM((SSD_HEADS, CHUNK, CHUNK), F32),
            pltpu.VMEM((SSD_GROUPS, CHUNK, CHUNK), F32),
        ],
        compiler_params=_cparams(("arbitrary",)),
    )(dy, z, ypre, xbc, xbc, dtp, prev, conv_w, conv_b, dt_bias, a_log, d_skip, norm_w)


def _rope_tables(pos_ref, inv_ref):
    ang = pos_ref[...].astype(F32) * inv_ref[...]
    d = _iota2((1, 2 * ATT_HD), 1) % ATT_HD
    s = jnp.sin(ang)
    return jnp.cos(ang), jnp.where(d < ROPE_DIM // 2, -s, 0.0), jnp.where((d >= ROPE_DIM // 2) & (d < ROPE_DIM), s, 0.0)


def _rope(t, tabs):
    c, s1, s2 = tabs
    n = t.shape[1]
    rep = n // c.shape[1]
    return (t * jnp.tile(c, (1, rep)) + pltpu.roll(t, n - ROPE_DIM // 2, 1) * jnp.tile(s1, (1, rep))
            + pltpu.roll(t, ROPE_DIM // 2, 1) * jnp.tile(s2, (1, rep)))


def _rope_t(t, tabs):
    c, s1, s2 = tabs
    n = t.shape[1]
    rep = n // c.shape[1]
    return (t * jnp.tile(c, (1, rep)) + pltpu.roll(t * jnp.tile(s1, (1, rep)), ROPE_DIM // 2, 1)
            + pltpu.roll(t * jnp.tile(s2, (1, rep)), n - ROPE_DIM // 2, 1))


def _swa_mask(first):
    qi = _iota2((WINDOW, 2 * WINDOW), 0)
    si = _iota2((WINDOW, 2 * WINDOW), 1)
    band = (si > qi) & (si <= qi + WINDOW)
    return band & (jnp.logical_not(first) | (si >= WINDOW))


def _stack_heads(t, j):
    return jnp.concatenate([t[:, ATT_HD * (j * ATT_R + r):ATT_HD * (j * ATT_R + r + 1)] for r in range(ATT_R)], axis=0)


def _stack_cols(ref, j):
    cols = [jnp.broadcast_to(ref[:, j * ATT_R + r:j * ATT_R + r + 1], (WINDOW, 1)) for r in range(ATT_R)]
    return jnp.concatenate(cols, axis=0)


def _swa_fwd(q, g, kv, sinks):
    L = q.shape[0]
    nb = L // WINDOW
    scale = ATT_HD ** -0.5

    def body(q_ref, g_ref, kvc_ref, kvp_ref, sink_ref, y_ref, o_ref, lse_ref, sbuf, pbuf, rbuf):
        n = pl.program_id(0)
        kk = _bf(jnp.concatenate([kvp_ref[:, 0:D_KV], kvc_ref[:, 0:D_KV]], axis=0))
        vv = _bf(jnp.concatenate([kvp_ref[:, D_KV:2 * D_KV], kvc_ref[:, D_KV:2 * D_KV]], axis=0))
        valid = jnp.tile(_swa_mask(n == 0), (ATT_R, 1))
        qv = q_ref[...]
        for j in range(ATT_KVH):
            s = _dot_nt(_bf(_stack_heads(qv, j)), kk[:, ATT_HD * j:ATT_HD * (j + 1)]) * scale
            sbuf[j] = jnp.where(valid, s, NEG_BIG)
        for j in range(ATT_KVH):
            s = sbuf[j]
            sink = _stack_cols(sink_ref, j)
            m = jnp.maximum(jnp.max(s, axis=-1, keepdims=True), sink)
            p = jnp.exp(s - m)
            pbuf[j] = _bf(p)
            denom = jnp.sum(p, axis=-1, keepdims=True) + jnp.exp(sink - m)
            rbuf[j] = 1.0 / denom
            lse = m + jnp.log(denom)
            for r in range(ATT_R):
                h = j * ATT_R + r
                lse_ref[:, h:h + 1] = lse[WINDOW * r:WINDOW * (r + 1)]
        for j in range(ATT_KVH):
            o = _dot(pbuf[j], vv[:, ATT_HD * j:ATT_HD * (j + 1)]) * rbuf[j]
            for r in range(ATT_R):
                h = j * ATT_R + r
                o_ref[:, ATT_HD * h:ATT_HD * (h + 1)] = o[WINDOW * r:WINDOW * (r + 1)]
        gv = g_ref[...]
        y_ref[...] = _bf(o_ref[...] * (gv * _sigmoid(gv)))

    cur = lambda wd: pl.BlockSpec((WINDOW, wd), lambda n: (n, 0))
    prv = lambda wd: pl.BlockSpec((WINDOW, wd), lambda n: (jnp.maximum(n - 1, 0), 0))
    return pl.pallas_call(
        body, name="swa_fwd", grid=(nb,),
        in_specs=[cur(D_ATT), cur(D_ATT), cur(2 * D_KV), prv(2 * D_KV), pl.BlockSpec((1, ATT_QH), lambda n: (0, 0))],
        out_specs=[cur(D_ATT), cur(D_ATT), cur(ATT_QH)],
        out_shape=[jax.ShapeDtypeStruct((L, D_ATT), BF16), jax.ShapeDtypeStruct((L, D_ATT), F32),
                   jax.ShapeDtypeStruct((L, ATT_QH), F32)],
        scratch_shapes=[pltpu.VMEM((ATT_KVH, ATT_R * WINDOW, 2 * WINDOW), F32),
                        pltpu.VMEM((ATT_KVH, ATT_R * WINDOW, 2 * WINDOW), BF16),
                        pltpu.VMEM((ATT_KVH, ATT_R * WINDOW, 1), F32)],
        compiler_params=_cparams(("parallel",)),
    )(q, g, kv, kv, sinks)


def _swa_bwd(dy, q, g, kv, o, lse, pos, inv, sinks):
    L = q.shape[0]
    nb = L // WINDOW
    scale = ATT_HD ** -0.5

    def body(dy_ref, q_ref, g_ref, kvc_ref, kvp_ref, o_ref, lse_ref, posc_ref, posp_ref, inv_ref, sink_ref,
             dq_ref, dg_ref, dkv_ref, dsink_ref, carry, dqbuf, dkbuf, dvbuf):
        n = pl.program_id(0)

        @pl.when(n == 0)
        def _():
            dsink_ref[...] = jnp.zeros_like(dsink_ref)

        @pl.when(n < nb)
        def _():
            tc = _rope_tables(posc_ref, inv_ref)
            tp = _rope_tables(posp_ref, inv_ref)
            kk = _bf(jnp.concatenate([kvp_ref[:, 0:D_KV], kvc_ref[:, 0:D_KV]], axis=0))
            vv = _bf(jnp.concatenate([kvp_ref[:, D_KV:2 * D_KV], kvc_ref[:, D_KV:2 * D_KV]], axis=0))
            valid = jnp.tile(_swa_mask(n == 0), (ATT_R, 1))
            qv = q_ref[...]
            gv = g_ref[...]
            sg = _sigmoid(gv)
            dyv = dy_ref[...]
            ov = o_ref[...]
            dg_ref[...] = _bf(dyv * ov * (sg * (1.0 + gv * (1.0 - sg))))
            do = dyv * (gv * sg)
            delta_all = do * ov
            lane16 = _iota2((1, ATT_QH), 1)
            dsink = jnp.zeros((1, ATT_QH), F32)
            for j in range(ATT_KVH):
                js = slice(ATT_HD * j, ATT_HD * (j + 1))
                kj = kk[:, js]
                vj = vv[:, js]
                qs = _bf(_stack_heads(qv, j))
                dos = _bf(_stack_heads(do, j))
                delta = jnp.sum(_stack_heads(delta_all, j), axis=-1, keepdims=True)
                lse = _stack_cols(lse_ref, j)
                s = _dot_nt(qs, kj) * scale
                p = jnp.exp(jnp.where(valid, s, NEG_BIG) - lse)
                dS = _bf(p * (_dot_nt(dos, vj) - delta))
                dqs = _dot(dS, kj) * scale
                dkbuf[:, js] = _dot_tn(dS, qs) * scale
                dvbuf[:, js] = _dot_tn(_bf(p), dos)
                sd = jnp.exp(_stack_cols(sink_ref, j) - lse) * delta
                for r in range(ATT_R):
                    h = j * ATT_R + r
                    rs = slice(WINDOW * r, WINDOW * (r + 1))
                    dqbuf[:, ATT_HD * h:ATT_HD * (h + 1)] = dqs[rs]
                    dsink = dsink - jnp.sum(sd[rs], axis=0, keepdims=True) * (lane16 == h).astype(F32)
            dsink_ref[...] += dsink
            dq_ref[...] = _bf(_rope_t(dqbuf[...], tc))
            dkp = _rope_t(dkbuf[0:WINDOW, :], tp)
            dkc = _rope_t(dkbuf[WINDOW:2 * WINDOW, :], tc)

            @pl.when(n > 0)
            def _():
                dkv_ref[:, 0:D_KV] = _bf(carry[:, 0:D_KV] + dkp)
                dkv_ref[:, D_KV:2 * D_KV] = _bf(carry[:, D_KV:2 * D_KV] + dvbuf[0:WINDOW, :])

            carry[:, 0:D_KV] = dkc
            carry[:, D_KV:2 * D_KV] = dvbuf[WINDOW:2 * WINDOW, :]

        @pl.when(n == nb)
        def _():
            dkv_ref[...] = _bf(carry[...])

    last = nb - 1
    cur = lambda wd: pl.BlockSpec((WINDOW, wd), lambda n: (jnp.minimum(n, last), 0))
    prv = lambda wd: pl.BlockSpec((WINDOW, wd), lambda n: (jnp.maximum(jnp.minimum(n, last) - 1, 0), 0))
    return pl.pallas_call(
        body, name="swa_bwd", grid=(nb + 1,),
        in_specs=[cur(D_ATT), cur(D_ATT), cur(D_ATT), cur(2 * D_KV), prv(2 * D_KV), cur(D_ATT), cur(ATT_QH), cur(1), prv(1),
                  pl.BlockSpec((1, 2 * ATT_HD), lambda n: (0, 0)), pl.BlockSpec((1, ATT_QH), lambda n: (0, 0))],
        out_specs=[cur(D_ATT), cur(D_ATT),
                   pl.BlockSpec((WINDOW, 2 * D_KV), lambda n: (jnp.maximum(n - 1, 0), 0)),
                   pl.BlockSpec((1, ATT_QH), lambda n: (0, 0))],
        out_shape=[jax.ShapeDtypeStruct((L, D_ATT), BF16), jax.ShapeDtypeStruct((L, D_ATT), BF16),
                   jax.ShapeDtypeStruct((L, 2 * D_KV), BF16), jax.ShapeDtypeStruct((1, ATT_QH), F32)],
        scratch_shapes=[pltpu.VMEM((WINDOW, 2 * D_KV), F32), pltpu.VMEM((WINDOW, D_ATT), F32),
                        pltpu.VMEM((2 * WINDOW, D_KV), F32), pltpu.VMEM((2 * WINDOW, D_KV), F32)],
        compiler_params=_cparams(("arbitrary",)),
    )(dy, q, g, kv, kv, o, lse, pos, pos, inv, sinks)


def _out_ln_loss(y_ssd, y_att, x, target, w_out, ln_g, ln_b):
    L = x.shape[0]
    tm = ROW_TILE
    inv_d = 1.0 / D_MODEL

    def body(ys_ref, ya_ref, x_ref, t_ref, w_ref, g_ref, b_ref, dr_ref, dys_ref, dya_ref, loss_ref, gg_ref, gb_ref):
        i = pl.program_id(0)

        @pl.when(i == 0)
        def _():
            loss_ref[...] = jnp.zeros_like(loss_ref)
            gg_ref[...] = jnp.zeros_like(gg_ref)
            gb_ref[...] = jnp.zeros_like(gb_ref)

        h = _dot(_bf(ys_ref[...]), w_ref[0:D_SSD, :]) + _dot(_bf(ya_ref[...]), w_ref[D_SSD:D_MIX, :])
        r = ALPHA * x_ref[...] + h
        mu = jnp.mean(r, axis=-1, keepdims=True)
        xc = r - mu
        rstd = lax.rsqrt(jnp.mean(xc * xc, axis=-1, keepdims=True) + LN_EPS)
        xhat = xc * rstd
        gam = g_ref[...]
        diff = xhat * gam + b_ref[...] - t_ref[...]
        part = jnp.sum(jnp.sum(diff * diff, axis=-1, keepdims=True), axis=0, keepdims=True)
        loss_ref[...] += (0.5 * inv_d) * part
        dout = diff * inv_d
        gg_ref[...] += jnp.sum(dout * xhat, axis=0, keepdims=True)
        gb_ref[...] += jnp.sum(dout, axis=0, keepdims=True)
        dxh = dout * gam
        dr = rstd * (dxh - jnp.mean(dxh, axis=-1, keepdims=True) - xhat * jnp.mean(dxh * xhat, axis=-1, keepdims=True))
        dr_ref[...] = dr
        drb = _bf(dr)
        dys_ref[...] = _dot_nt(drb, w_ref[0:D_SSD, :])
        dya_ref[...] = _dot_nt(drb, w_ref[D_SSD:D_MIX, :])

    row = pl.BlockSpec((tm, D_MODEL), lambda i: (i, 0))
    vec = pl.BlockSpec((1, D_MODEL), lambda i: (0, 0))
    return pl.pallas_call(
        body, name="out_ln_loss", grid=(L // tm,),
        in_specs=[row, row, row, row, pl.BlockSpec((D_MIX, D_MODEL), lambda i: (0, 0), pipeline_mode=pl.Buffered(1)), vec, vec],
        out_specs=[row, row, row, pl.BlockSpec((1, 128), lambda i: (0, 0)), vec, vec],
        out_shape=[jax.ShapeDtypeStruct((L, D_MODEL), F32)] * 3 + [jax.ShapeDtypeStruct((1, 128), F32)]
        + [jax.ShapeDtypeStruct((1, D_MODEL), F32)] * 2,
        compiler_params=_cparams(("arbitrary",)),
    )(y_ssd, y_att, x, target, w_out, ln_g, ln_b)


def _local_step(x, pos, target, w, w_out, conv_w, conv_b, dt_bias, a_log, d_skip, norm_w, sinks, ln_g, ln_b):
    inv8 = ROPE_THETA ** (-jnp.arange(0, ROPE_DIM, 2, dtype=F32) / ROPE_DIM)
    inv = jnp.tile(jnp.concatenate([inv8, inv8, jnp.zeros((ATT_HD - ROPE_DIM,), F32)]), 2).reshape(1, 2 * ATT_HD)

    z, g, q, xbc, kv, dtp, xb = _in_proj(x, w, pos, inv)
    y_ssd, y_pre, prev = _ssd_fwd2(z, xbc, dtp, conv_w, conv_b, dt_bias, a_log, d_skip, norm_w)
    y_att, o, lse = _swa_fwd(q, g, kv, sinks)
    dr, dy_ssd, dy_att, loss, g_ln_g, g_ln_b = _out_ln_loss(y_ssd, y_att, x, target, w_out, ln_g, ln_b)
    gw_out_ssd = _matmul_tn(y_ssd, dr, 1024, "gw_out_ssd")
    gw_out_att = _matmul_tn(y_att, dr, 1024, "gw_out_att")
    dq, dg, dkv, g_sinks = _swa_bwd(dy_att, q, g, kv, o, lse, pos, inv, sinks)
    dz, dxbc, ddt, g_conv_w, g_conv_b, g_dt_bias, g_a_log, g_d_skip, g_norm_w = _ssd_bwd2(
        dy_ssd, z, y_pre, xbc, dtp, prev, conv_w, conv_b, dt_bias, a_log, d_skip, norm_w)
    grad_x = _grad_x(dr, dz, dg, dq, dxbc, dkv, ddt, w)
    gw_z = _matmul_tn(dz, xb, 1024, "gw_z")
    gw_g = _matmul_tn(dg, xb, 1024, "gw_g")
    gw_q = _matmul_tn(dq, xb, 1024, "gw_q")
    gw_xbc = _matmul_tn(dxbc, xb, 1024, "gw_xbc")
    gw_kv = _matmul_tn(dkv, xb, 1024, "gw_kv")
    gw_dt = _matmul_tn(ddt, xb, 1024, "gw_dt")
    gw_in = jnp.concatenate([gw_z, gw_xbc, gw_dt[0:SSD_HEADS], gw_q, gw_kv, gw_g], axis=0)
    gw_out = jnp.concatenate([gw_out_ssd, gw_out_att], axis=0)
    small = dict(conv_w=g_conv_w, conv_b=g_conv_b, dt_bias=g_dt_bias, a_log=g_a_log, d_skip=g_d_skip,
                 ssd_norm_w=g_norm_w, attn_sinks=g_sinks, ln_g=g_ln_g, ln_b=g_ln_b)
    return loss, grad_x, gw_in, gw_out, small


def _mesh_pos():
    return lax.axis_index("x"), lax.axis_index("y"), lax.axis_index("c")


def _gather_weights(w_in_s, w_out_s, conv_w_s):
    def body(win_ref, wout_ref, cw_ref, owin_ref, owout_ref, ocw_ref, send_sems, recv_sems, small_send, small_recv,
             local_sems):
        x, y, c = _mesh_pos()
        me = 2 * x + y
        sibling = (x, y, 1 - c)
        chips = [(1 - x, y), (x, 1 - y), (1 - x, 1 - y)]
        locals_ = [pltpu.make_async_copy(cw_ref, ocw_ref.at[me], local_sems.at[0])]
        for cp in locals_:
            cp.start()
        started = []
        for t, (src, dst) in enumerate(((win_ref, owin_ref), (wout_ref, owout_ref))):
            hr = src.shape[0] // 2

            def half(ref, hc, hr=hr):
                return ref.at[pl.ds(hc * hr, hr), :]

            for j, (px, py) in enumerate(chips):
                cp = pltpu.make_async_remote_copy(
                    src_ref=half(src, c), dst_ref=half(dst.at[me], c), send_sem=send_sems.at[t, j],
                    recv_sem=recv_sems.at[t, j], device_id=(px, py, c), device_id_type=MESH)
                cp.start()
                started.append(cp)
        for j, (px, py) in enumerate(chips):
            cp = pltpu.make_async_remote_copy(
                src_ref=cw_ref, dst_ref=ocw_ref.at[me], send_sem=small_send.at[j], recv_sem=small_recv.at[j],
                device_id=(px, py, c), device_id_type=MESH)
            cp.start()
            started.append(cp)
        for t, (src, dst) in enumerate(((win_ref, owin_ref), (wout_ref, owout_ref))):
            hr = src.shape[0] // 2
            for j, (px, py) in enumerate(chips):
                src_chip = 2 * px + py
                blk = dst.at[src_chip].at[pl.ds(c * hr, hr), :]
                pltpu.make_async_remote_copy(
                    src_ref=blk, dst_ref=blk, send_sem=send_sems.at[t, j], recv_sem=recv_sems.at[t, j],
                    device_id=(px, py, c), device_id_type=MESH).wait_recv()
                cp = pltpu.make_async_remote_copy(
                    src_ref=blk, dst_ref=blk, send_sem=send_sems.at[t, 3 + j], recv_sem=recv_sems.at[t, 3 + j],
                    device_id=sibling, device_id_type=MESH)
                cp.start()
                started.append(cp)
        for t, (src, dst) in enumerate(((win_ref, owin_ref), (wout_ref, owout_ref))):
            hr = src.shape[0] // 2
            for j, (px, py) in enumerate(chips):
                src_chip = 2 * px + py
                blk = dst.at[src_chip].at[pl.ds((1 - c) * hr, hr), :]
                pltpu.make_async_remote_copy(
                    src_ref=blk, dst_ref=blk, send_sem=send_sems.at[t, 3 + j], recv_sem=recv_sems.at[t, 3 + j],
                    device_id=sibling, device_id_type=MESH).wait_recv()
        for j in range(3):
            pltpu.make_async_remote_copy(
                src_ref=cw_ref, dst_ref=ocw_ref.at[me], send_sem=small_send.at[j], recv_sem=small_recv.at[j],
                device_id=sibling, device_id_type=MESH).wait_recv()
        for cp in started:
            cp.wait_send()
        for cp in locals_:
            cp.wait()

    any_spec = pl.BlockSpec(memory_space=pl.ANY)
    return pl.pallas_call(
        body, name="gather_weights",
        in_specs=[any_spec] * 3, out_specs=[any_spec] * 3,
        out_shape=[jax.ShapeDtypeStruct((N_CHIPS,) + a.shape, a.dtype) for a in (w_in_s, w_out_s, conv_w_s)],
        scratch_shapes=[pltpu.SemaphoreType.DMA((2, 6)), pltpu.SemaphoreType.DMA((2, 6)),
                        pltpu.SemaphoreType.DMA((3,)), pltpu.SemaphoreType.DMA((3,)), pltpu.SemaphoreType.DMA((3,))],
    )(w_in_s, w_out_s, conv_w_s)


def _pair_exchange(gw_in, gw_out, small):
    k_small = small.shape[1]

    def body(gin_ref, gout_ref, sm_ref, rin_ref, rout_ref, slots_ref, send_sems, recv_sems, small_send, small_recv,
             local_sem):
        x, y, c = _mesh_pos()
        me = 4 * x + 2 * y + c
        sibling = (x, y, 1 - c)
        mine = pltpu.make_async_copy(sm_ref, slots_ref.at[me], local_sem)
        mine.start()
        started = []
        for t, (src, dst) in enumerate(((gin_ref, rin_ref), (gout_ref, rout_ref))):
            hr = src.shape[1] // 2
            for j in range(N_CHIPS):
                cp = pltpu.make_async_remote_copy(
                    src_ref=src.at[j, pl.ds((1 - c) * hr, hr), :], dst_ref=dst.at[j], send_sem=send_sems.at[t, j],
                    recv_sem=recv_sems.at[t, j], device_id=sibling, device_id_type=MESH)
                cp.start()
                started.append(cp)
        for k in range(1, 8):
            peer = (x ^ ((k >> 2) & 1), y ^ ((k >> 1) & 1), c ^ (k & 1))
            cp = pltpu.make_async_remote_copy(
                src_ref=sm_ref, dst_ref=slots_ref.at[me], send_sem=small_send.at[k - 1], recv_sem=small_recv.at[k - 1],
                device_id=peer, device_id_type=MESH)
            cp.start()
            started.append(cp)
        for t, (src, dst) in enumerate(((gin_ref, rin_ref), (gout_ref, rout_ref))):
            for j in range(N_CHIPS):
                pltpu.make_async_remote_copy(
                    src_ref=dst.at[j], dst_ref=dst.at[j], send_sem=send_sems.at[t, j], recv_sem=recv_sems.at[t, j],
                    device_id=sibling, device_id_type=MESH).wait_recv()
        for k in range(1, 8):
            pltpu.make_async_remote_copy(
                src_ref=sm_ref, dst_ref=slots_ref.at[me], send_sem=small_send.at[k - 1], recv_sem=small_recv.at[k - 1],
                device_id=sibling, device_id_type=MESH).wait_recv()
        for cp in started:
            cp.wait_send()
        mine.wait()

    any_spec = pl.BlockSpec(memory_space=pl.ANY)
    half_in = jax.ShapeDtypeStruct((N_CHIPS, gw_in.shape[1] // 2, D_MODEL), F32)
    half_out = jax.ShapeDtypeStruct((N_CHIPS, gw_out.shape[1] // 2, D_MODEL), F32)
    return pl.pallas_call(
        body, name="pair_exchange",
        in_specs=[any_spec] * 3, out_specs=[any_spec] * 3,
        out_shape=[half_in, half_out, jax.ShapeDtypeStruct((8, 8, k_small), F32)],
        scratch_shapes=[pltpu.SemaphoreType.DMA((2, N_CHIPS)), pltpu.SemaphoreType.DMA((2, N_CHIPS)),
                        pltpu.SemaphoreType.DMA((7,)), pltpu.SemaphoreType.DMA((7,)), pltpu.SemaphoreType.DMA],
    )(gw_in, gw_out, small)


def _chip_exchange(s_in, s_out):
    def body(sin_ref, sout_ref, rin_ref, rout_ref, send_sems, recv_sems):
        x, y, c = _mesh_pos()
        me = 2 * x + y
        chips = [(1 - x, y), (x, 1 - y), (1 - x, 1 - y)]
        started = []
        for t, (src, dst) in enumerate(((sin_ref, rin_ref), (sout_ref, rout_ref))):
            for j, (px, py) in enumerate(chips):
                cp = pltpu.make_async_remote_copy(
                    src_ref=src.at[2 * px + py], dst_ref=dst.at[me], send_sem=send_sems.at[t, j],
                    recv_sem=recv_sems.at[t, j], device_id=(px, py, c), device_id_type=MESH)
                cp.start()
                started.append(cp)
        for t, (src, dst) in enumerate(((sin_ref, rin_ref), (sout_ref, rout_ref))):
            for j, (px, py) in enumerate(chips):
                blk = dst.at[2 * px + py]
                pltpu.make_async_remote_copy(
                    src_ref=blk, dst_ref=blk, send_sem=send_sems.at[t, j], recv_sem=recv_sems.at[t, j],
                    device_id=(px, py, c), device_id_type=MESH).wait_recv()
        for cp in started:
            cp.wait_send()

    any_spec = pl.BlockSpec(memory_space=pl.ANY)
    return pl.pallas_call(
        body, name="chip_exchange",
        in_specs=[any_spec] * 2, out_specs=[any_spec] * 2,
        out_shape=[jax.ShapeDtypeStruct(s_in.shape, s_in.dtype), jax.ShapeDtypeStruct(s_out.shape, s_out.dtype)],
        scratch_shapes=[pltpu.SemaphoreType.DMA((2, 3)), pltpu.SemaphoreType.DMA((2, 3))],
    )(s_in, s_out)


def _pair_share(h_in, h_out):
    def body(hin_ref, hout_ref, rin_ref, rout_ref, send_sems, recv_sems):
        x, y, c = _mesh_pos()
        sibling = (x, y, 1 - c)
        started = []
        for t, (src, dst) in enumerate(((hin_ref, rin_ref), (hout_ref, rout_ref))):
            cp = pltpu.make_async_remote_copy(
                src_ref=src, dst_ref=dst, send_sem=send_sems.at[t], recv_sem=recv_sems.at[t],
                device_id=sibling, device_id_type=MESH)
            cp.start()
            started.append(cp)
        for cp in started:
            cp.wait()

    any_spec = pl.BlockSpec(memory_space=pl.ANY)
    return pl.pallas_call(
        body, name="pair_share",
        in_specs=[any_spec] * 2, out_specs=[any_spec] * 2,
        out_shape=[jax.ShapeDtypeStruct(h_in.shape, F32), jax.ShapeDtypeStruct(h_out.shape, F32)],
        scratch_shapes=[pltpu.SemaphoreType.DMA((2,)), pltpu.SemaphoreType.DMA((2,))],
    )(h_in, h_out)


def _pair_add(g, recv, core, name):
    _, rows, C = recv.shape
    tc = 256

    def body(core_ref, g_ref, r_ref, o_ref):
        o_ref[...] = _bf(g_ref[...] + r_ref[...])

    spec = pl.BlockSpec((1, rows, tc), lambda j, i, core: (j, 0, i))
    return pl.pallas_call(
        body, name=name,
        grid_spec=pltpu.PrefetchScalarGridSpec(
            num_scalar_prefetch=1, grid=(N_CHIPS, C // tc),
            in_specs=[pl.BlockSpec((1, rows, tc), lambda j, i, core: (j, core[0], i)), spec], out_specs=spec),
        out_shape=jax.ShapeDtypeStruct((N_CHIPS, rows, C), BF16),
        compiler_params=_cparams(("parallel", "parallel")),
    )(core, g, recv)


def _chip_add(own, parts, chip, name):
    _, rows, C = parts.shape
    tc = 256

    def body(chip_ref, own_ref, r0, r1, r2, r3, o_ref):
        acc = None
        for j, r in enumerate((r0, r1, r2, r3)):
            term = jnp.where(chip_ref[0] == j, own_ref[0], r[0]).astype(F32)
            acc = term if acc is None else acc + term
        o_ref[...] = acc

    def slab(j):
        return pl.BlockSpec((1, rows, tc), lambda i, chip: (jnp.where(chip[0] == j, (j + 1) % N_CHIPS, j), 0, i))

    return pl.pallas_call(
        body, name=name,
        grid_spec=pltpu.PrefetchScalarGridSpec(
            num_scalar_prefetch=1, grid=(C // tc,),
            in_specs=[pl.BlockSpec((1, rows, tc), lambda i, chip: (chip[0], 0, i))] + [slab(j) for j in range(N_CHIPS)],
            out_specs=pl.BlockSpec((rows, tc), lambda i, chip: (0, i))),
        out_shape=jax.ShapeDtypeStruct((rows, C), F32),
        compiler_params=_cparams(("parallel",)),
    )(chip, own, parts, parts, parts, parts)


def _adamw_math(w, g, m, v):
    m = ADAM_B1 * m + (1.0 - ADAM_B1) * g
    v = ADAM_B2 * v + (1.0 - ADAM_B2) * (g * g)
    m_hat = m / (1.0 - ADAM_B1 ** ADAM_STEP)
    v_hat = v / (1.0 - ADAM_B2 ** ADAM_STEP)
    delta = -ADAM_LR * (m_hat / (jnp.sqrt(v_hat) + ADAM_EPS) + ADAM_WD * w)
    return delta, m, v


def _adamw_pair(w, g_own, g_sib, m, v, core, name):
    unit = w.ndim == 3
    R, C = w.shape[0], w.shape[-1]
    rows = g_own.shape[0]
    tc = 128

    def body(core_ref, w_ref, go_ref, gs_ref, m_ref, v_ref, d_ref, nm_ref, nv_ref, g_ref):
        first = core_ref[0] == 0
        own, sib = go_ref[...], gs_ref[...]
        g = jnp.concatenate([jnp.where(first, own, sib), jnp.where(first, sib, own)], axis=0)[0:R, :]
        idx = (slice(None), 0, slice(None)) if unit else (slice(None), slice(None))
        d, nm, nv = _adamw_math(w_ref[idx], g, m_ref[idx], v_ref[idx])
        d_ref[idx] = d
        nm_ref[idx] = nm
        nv_ref[idx] = nv
        g_ref[idx] = g

    if unit:
        spec = pl.BlockSpec((R, 1, tc), lambda i, core: (0, 0, i))
    else:
        spec = pl.BlockSpec((R, tc), lambda i, core: (0, i))
    gspec = pl.BlockSpec((rows, tc), lambda i, core: (0, i))
    return pl.pallas_call(
        body, name=name,
        grid_spec=pltpu.PrefetchScalarGridSpec(
            num_scalar_prefetch=1, grid=(C // tc,),
            in_specs=[spec, gspec, gspec, spec, spec], out_specs=[spec] * 4),
        out_shape=[jax.ShapeDtypeStruct(w.shape, F32)] * 4,
        compiler_params=_cparams(("parallel",)),
    )(core, w, g_own, g_sib, m, v)


SMALL_NAMES = ("conv_b", "ssd_norm_w", "ln_g", "ln_b", "dt_bias", "a_log", "d_skip", "attn_sinks")
SMALL_SIZES = (D_XBC, D_SSD, D_MODEL, D_MODEL, SSD_HEADS, SSD_HEADS, SSD_HEADS, ATT_QH)
SMALL_OFFS = tuple(D_XBC + sum(-(-n // 128) * 128 for n in SMALL_SIZES[:k]) for k in range(len(SMALL_SIZES)))
LOSS_OFF = D_XBC + sum(-(-n // 128) * 128 for n in SMALL_SIZES)
K_SMALL = LOSS_OFF + 128


def _pack_small(g_conv_w, vecs, loss):
    def body(cw_ref, *refs):
        o_ref = refs[-1]
        o_ref[...] = jnp.zeros_like(o_ref)
        o_ref[0:CONV_K, 0:D_XBC] = cw_ref[...]
        for v_ref, off, n in zip(refs[:-2], SMALL_OFFS, SMALL_SIZES):
            o_ref[0:1, off:off + n] = v_ref[...]
        o_ref[0:1, LOSS_OFF:LOSS_OFF + 128] = refs[-2][...]

    return pl.pallas_call(
        body, name="pack_small", out_shape=jax.ShapeDtypeStruct((8, K_SMALL), F32), compiler_params=_cparams(),
    )(g_conv_w, *vecs, loss)


def _adamw_small(slots, chip, conv_w, m_conv_w, v_conv_w, params, moms, vars_):
    n_vec = len(SMALL_NAMES)

    def body(chip_ref, s_ref, *refs):
        ins = refs[:3 * (n_vec + 1)]
        outs = refs[3 * (n_vec + 1):-1]
        tot_ref = refs[-1]
        tot = s_ref[0]
        for d in range(1, 8):
            tot = tot + s_ref[d]
        outs[0][...] = tot[0:1, LOSS_OFF:LOSS_OFF + 1]
        off = pl.multiple_of(chip_ref[0] * CONV_COLS, 128)
        tot_ref[...] = tot
        grads = [tot_ref[0:CONV_K, pl.ds(off, CONV_COLS)]]
        grads += [tot[0:1, o:o + n] for o, n in zip(SMALL_OFFS, SMALL_SIZES)]
        for k, g in enumerate(grads):
            w_ref, m_ref, v_ref = ins[3 * k:3 * k + 3]
            full = (0,) if k == 0 else (Ellipsis,)
            d, nm, nv = _adamw_math(w_ref[full], g, m_ref[full], v_ref[full])
            for o_ref, val in zip(outs[1 + 4 * k:5 + 4 * k], (g, d, nm, nv)):
                o_ref[full] = val

    args = [conv_w, m_conv_w, v_conv_w]
    for w, m, v in zip(params, moms, vars_):
        args += [w, m, v]
    shapes = [jax.ShapeDtypeStruct((1, 1), F32)] + [jax.ShapeDtypeStruct(conv_w.shape, F32)] * 4
    for w in params:
        shapes += [jax.ShapeDtypeStruct(w.shape, F32)] * 4
    vmem = pl.BlockSpec(memory_space=pltpu.VMEM)
    return pl.pallas_call(
        body, name="adamw_small",
        grid_spec=pltpu.PrefetchScalarGridSpec(
            num_scalar_prefetch=1, grid=(1,),
            in_specs=[pl.BlockSpec(slots.shape, lambda i, chip: (0, 0, 0))] + [vmem] * len(args),
            out_specs=[vmem] * len(shapes), scratch_shapes=[pltpu.VMEM((8, K_SMALL), F32)]),
        out_shape=shapes, compiler_params=_cparams(),
    )(chip, slots, *args)


def kernel(x, positions, w_in, conv_w, conv_b, dt_bias, a_log, d_skip, ssd_norm_w, attn_sinks, w_out, ln_g, ln_b, loss_target, m_w_in, m_conv_w, m_conv_b, m_dt_bias, m_a_log, m_d_skip, m_ssd_norm_w, m_attn_sinks, m_w_out, m_ln_g, m_ln_b, v_w_in, v_conv_w, v_conv_b, v_dt_bias, v_a_log, v_d_skip, v_ssd_norm_w, v_attn_sinks, v_w_out, v_ln_g, v_ln_b):
    mx, my, mc = _mesh_pos()
    chip = 2 * mx + my
    L = x.shape[1]

    conv_w_s8 = jnp.pad(conv_w[0], ((0, 8 - CONV_K), (0, 0)))
    pad_rows = ((0, SLAB_ROWS - W_IN_COLS), (0, 0))
    w_in_t = w_in[0].T
    w_in_b, w_out_b = jnp.pad(_bf(w_in_t), pad_rows), _bf(w_out[0])
    ag_in, ag_out, ag_cw = _gather_weights(w_in_b, w_out_b, conv_w_s8)
    ag_in = jnp.where((jnp.arange(N_CHIPS) == chip)[:, None, None], w_in_b[None], ag_in)
    ag_out = jnp.where((jnp.arange(N_CHIPS) == chip)[:, None, None], w_out_b[None], ag_out)
    w_full = jnp.concatenate([ag_in[j, 0:W_IN_COLS] for j in range(N_CHIPS)], axis=0)
    w = jnp.concatenate([
        w_full[O_Z:O_Z + D_SSD], w_full[O_G:O_G + D_ATT], w_full[O_Q:O_Q + D_ATT],
        w_full[O_XBC:O_XBC + D_XBC], w_full[O_K:O_K + 2 * D_KV], w_full[O_DT:O_DT + SSD_HEADS],
        jnp.zeros((DT_PAD - SSD_HEADS, D_MODEL), BF16)], axis=0)
    w_out_full = ag_out.reshape(D_MIX, D_MODEL)
    conv_w_full = jnp.concatenate([ag_cw[j, 0:CONV_K] for j in range(N_CHIPS)], axis=1)

    loss_part, grad_x, gw_in, gw_out, small = _local_step(
        x[0], positions[0].reshape(L, 1), loss_target[0], w, w_out_full, conv_w_full, conv_b, dt_bias, a_log, d_skip,
        ssd_norm_w, attn_sinks, ln_g, ln_b)

    packed = _pack_small(small["conv_w"], [small[n] for n in SMALL_NAMES], loss_part)

    gw_in_slabs = jnp.stack([jnp.pad(gw_in[W_IN_COLS * j:W_IN_COLS * (j + 1)], pad_rows) for j in range(N_CHIPS)])
    gw_out_slabs = gw_out.reshape(N_CHIPS, W_OUT_ROWS, D_MODEL)
    core_id = mc.reshape(1).astype(jnp.int32)
    chip_id = chip.reshape(1).astype(jnp.int32)
    recv_in, recv_out, slots = _pair_exchange(gw_in_slabs, gw_out_slabs, packed)
    s_in = _pair_add(gw_in_slabs, recv_in, core_id, "pair_add_in")
    s_out = _pair_add(gw_out_slabs, recv_out, core_id, "pair_add_out")
    r_in, r_out = _chip_exchange(s_in, s_out)
    h_in = _chip_add(s_in, r_in, chip_id, "chip_add_in")
    h_out = _chip_add(s_out, r_out, chip_id, "chip_add_out")
    sib_in, sib_out = _pair_share(h_in, h_out)

    to_rows = lambda a: jnp.transpose(a, (2, 0, 1))
    in_t = _adamw_pair(to_rows(w_in), h_in, sib_in, to_rows(m_w_in), to_rows(v_w_in), core_id, "adamw_w_in")
    d_w_in, nm_w_in, nv_w_in, g_w_in = [jnp.transpose(a, (1, 2, 0)) for a in in_t]
    out_t = _adamw_pair(w_out[0], h_out, sib_out, m_w_out[0], v_w_out[0], core_id, "adamw_w_out")
    d_w_out, nm_w_out, nv_w_out, g_w_out = [a[None] for a in out_t]

    params = dict(conv_b=conv_b, ssd_norm_w=ssd_norm_w, ln_g=ln_g, ln_b=ln_b, dt_bias=dt_bias, a_log=a_log,
                  d_skip=d_skip, attn_sinks=attn_sinks)
    moms = dict(conv_b=m_conv_b, ssd_norm_w=m_ssd_norm_w, ln_g=m_ln_g, ln_b=m_ln_b, dt_bias=m_dt_bias, a_log=m_a_log,
                d_skip=m_d_skip, attn_sinks=m_attn_sinks)
    vars_ = dict(conv_b=v_conv_b, ssd_norm_w=v_ssd_norm_w, ln_g=v_ln_g, ln_b=v_ln_b, dt_bias=v_dt_bias, a_log=v_a_log,
                 d_skip=v_d_skip, attn_sinks=v_attn_sinks)
    res = _adamw_small(slots, chip_id, conv_w, m_conv_w, v_conv_w, [params[n] for n in SMALL_NAMES],
                       [moms[n] for n in SMALL_NAMES], [vars_[n] for n in SMALL_NAMES])
    loss = res[0][0, 0]
    grads, delta, new_m, new_v = {}, {}, {}, {}
    for k, n in enumerate(("conv_w",) + SMALL_NAMES):
        grads[n], delta[n], new_m[n], new_v[n] = res[1 + 4 * k:5 + 4 * k]
    for dd, a_in, a_out in ((grads, g_w_in, g_w_out), (delta, d_w_in, d_w_out), (new_m, nm_w_in, nm_w_out),
                            (new_v, nv_w_in, nv_w_out)):
        dd["w_in"] = a_in
        dd["w_out"] = a_out
    order = ("w_in", "conv_w", "conv_b", "dt_bias", "a_log", "d_skip", "ssd_norm_w", "attn_sinks", "w_out", "ln_g", "ln_b")
    return (loss, grad_x[None], *[grads[n] for n in order], *[delta[n] for n in order], *[new_m[n] for n in order],
            *[new_v[n] for n in order])
```

```python
import functools

import numpy as np
import jax
import jax.numpy as jnp
from jax import lax
from jax.experimental import pallas as pl
from jax.experimental.pallas import tpu as pltpu

F32 = jnp.float32
BF16 = jnp.bfloat16
MESH = pl.DeviceIdType.MESH

D_MODEL = 1024
D_SSD = 1024
D_ATT = 1024
D_MIX = 2048
SSD_HEADS = 16
SSD_P = 64
SSD_GROUPS = 2
SSD_R = 8
SSD_N = 128
D_BC = 256
D_XBC = 1536
CONV_K = 4
CHUNK = 128
ATT_HD = 64
ATT_QH = 16
ATT_KVH = 4
ATT_R = 4
D_KV = 256
WINDOW = 128
ROPE_THETA = 500000.0
ROPE_DIM = 16
ALPHA = 2.0 ** 0.25
LN_EPS = 1e-5
RMS_EPS = 1e-5
D_IN_PROJ = 5136
O_Z, O_XBC, O_DT, O_Q, O_K, O_V, O_G = 0, 1024, 2560, 2576, 3600, 3856, 4112
P_Z, P_G, P_Q, P_XBC, P_KV, P_DT, P_END = 0, 1024, 2048, 3072, 4608, 5120, 5248
DT_PAD = 128
N_CHIPS = 4
W_IN_COLS = D_IN_PROJ // N_CHIPS
SLAB_ROWS = 1312
W_OUT_ROWS = D_MIX // N_CHIPS
CONV_COLS = D_XBC // N_CHIPS

ADAM_LR = 0.001
ADAM_B1 = 0.9
ADAM_B2 = 0.999
ADAM_EPS = 1e-08
ADAM_WD = 0.01
ADAM_STEP = 10

VMEM_LIMIT = 56 * 1024 * 1024
ROW_TILE = 512
NEG_BIG = -1e30
HI = lax.Precision.HIGHEST


def _cparams(sem=None, **kw):
    if sem is not None:
        kw["dimension_semantics"] = sem
    return pltpu.CompilerParams(vmem_limit_bytes=VMEM_LIMIT, **kw)


def _dot(a, b):
    return jnp.dot(a, b, preferred_element_type=F32)


def _dot_nt(a, b):
    return lax.dot_general(a, b, (((1,), (1,)), ((), ())), preferred_element_type=F32)


def _dot_tn(a, b):
    return lax.dot_general(a, b, (((0,), (0,)), ((), ())), preferred_element_type=F32)


def _bf(a):
    return a.astype(BF16)


def _iota2(shape, dim):
    return lax.broadcasted_iota(jnp.int32, shape, dim)


def _to_rows(col):
    k = col.shape[1]
    eye = (_iota2((k, k), 0) == _iota2((k, k), 1)).astype(F32)
    return lax.dot_general(eye, col, (((1,), (1,)), ((), ())), preferred_element_type=F32, precision=HI)


def _to_cols(row):
    n = row.shape[1]
    eye = (_iota2((n, n), 0) == _iota2((n, n), 1)).astype(F32)
    return lax.dot_general(eye, row, (((1,), (1,)), ((), ())), preferred_element_type=F32, precision=HI)


def _sigmoid(x):
    return jax.nn.sigmoid(x)


def _in_proj(x, w, pos, inv):
    L = x.shape[0]
    tm = ROW_TILE
    widths = (D_SSD, D_ATT, D_ATT, D_XBC, 2 * D_KV, DT_PAD)

    def body(x_ref, w_ref, pos_ref, inv_ref, z_ref, g_ref, q_ref, xbc_ref, kv_ref, dt_ref, xb_ref):
        xb = _bf(x_ref[...])
        xb_ref[...] = xb
        for o_ref, off, wd in zip((z_ref, g_ref, xbc_ref, dt_ref), (P_Z, P_G, P_XBC, P_DT), (D_SSD, D_ATT, D_XBC, DT_PAD)):
            o_ref[...] = _dot_nt(xb, w_ref[off:off + wd, :])
        tabs = _rope_tables(pos_ref, inv_ref)
        q_ref[...] = _bf(_rope(_dot_nt(xb, w_ref[P_Q:P_Q + D_ATT, :]), tabs))
        kv_ref[:, 0:D_KV] = _bf(_rope(_dot_nt(xb, w_ref[P_KV:P_KV + D_KV, :]), tabs))
        kv_ref[:, D_KV:2 * D_KV] = _bf(_dot_nt(xb, w_ref[P_KV + D_KV:P_KV + 2 * D_KV, :]))

    row = lambda wd: pl.BlockSpec((tm, wd), lambda i: (i, 0))
    return pl.pallas_call(
        body, name="in_proj", grid=(L // tm,),
        in_specs=[row(D_MODEL), pl.BlockSpec((P_END, D_MODEL), lambda i: (0, 0), pipeline_mode=pl.Buffered(1)), row(1),
                  pl.BlockSpec((1, 2 * ATT_HD), lambda i: (0, 0))],
        out_specs=[row(wd) for wd in widths] + [row(D_MODEL)],
        out_shape=[jax.ShapeDtypeStruct((L, wd), dt) for wd, dt in zip(widths, (F32, F32, BF16, F32, BF16, F32))]
        + [jax.ShapeDtypeStruct((L, D_MODEL), BF16)],
        compiler_params=_cparams(("parallel",)),
    )(x, w, pos, inv)


def _matmul_tn(a, b, tn, name):
    K, M = a.shape
    N = b.shape[1]
    tk = min(K, 2048 if M <= 1024 else 1024)
    nk = K // tk

    def body(a_ref, b_ref, o_ref, acc_ref):
        k = pl.program_id(1)

        @pl.when(k == 0)
        def _():
            acc_ref[...] = jnp.zeros_like(acc_ref)

        acc_ref[...] += _dot_tn(_bf(a_ref[...]), _bf(b_ref[...]))

        @pl.when(k == nk - 1)
        def _():
            o_ref[...] = acc_ref[...]

    return pl.pallas_call(
        body, name=name, grid=(N // tn, nk),
        in_specs=[pl.BlockSpec((tk, M), lambda j, k: (k, 0)), pl.BlockSpec((tk, tn), lambda j, k: (k, j))],
        out_specs=pl.BlockSpec((M, tn), lambda j, k: (0, j)),
        out_shape=jax.ShapeDtypeStruct((M, N), F32),
        scratch_shapes=[pltpu.VMEM((M, tn), F32)],
        compiler_params=_cparams(("parallel", "arbitrary")),
    )(a, b)


def _grad_x(dr, dz, dg, dq, dxbc, dkv, ddt, w):
    L = dr.shape[0]
    tm = ROW_TILE
    widths = (D_SSD, D_ATT, D_ATT, D_XBC, 2 * D_KV, DT_PAD)
    offs = (P_Z, P_G, P_Q, P_XBC, P_KV, P_DT)

    def body(dr_ref, dz_ref, dg_ref, dq_ref, dxbc_ref, dkv_ref, ddt_ref, w_ref, o_ref):
        acc = ALPHA * dr_ref[...]
        for p_ref, off, wd in zip((dz_ref, dg_ref, dq_ref, dxbc_ref, dkv_ref, ddt_ref), offs, widths):
            acc = acc + _dot(_bf(p_ref[...]), w_ref[off:off + wd, :])
        o_ref[...] = acc

    row = lambda wd: pl.BlockSpec((tm, wd), lambda i: (i, 0))
    return pl.pallas_call(
        body, name="grad_x", grid=(L // tm,),
        in_specs=[row(D_MODEL)] + [row(wd) for wd in widths] + [pl.BlockSpec((P_END, D_MODEL), lambda i: (0, 0), pipeline_mode=pl.Buffered(1))],
        out_specs=row(D_MODEL),
        out_shape=jax.ShapeDtypeStruct((L, D_MODEL), F32),
        compiler_params=_cparams(("parallel",)),
    )(dr, dz, dg, dq, dxbc, dkv, ddt, w)


def _ssd_chunk_pre(first, xbc_ref, tail_ref, dt_ref, cw_ref, cb_ref, dtb_ref, alog_ref, ext):
    tail = jnp.where(first, 0.0, tail_ref[...])
    ext[0:8, :] = tail
    ext[8:8 + CHUNK, :] = xbc_ref[...]
    u = cb_ref[...] + cw_ref[0:1, :] * ext[pl.ds(5, CHUNK), :]
    for k in range(1, CONV_K):
        u = u + cw_ref[k:k + 1, :] * ext[pl.ds(5 + k, CHUNK), :]
    sig = _sigmoid(u)
    xbc = u * sig
    dtraw = dt_ref[:, 0:SSD_HEADS] + dtb_ref[...]
    dt = jax.nn.softplus(dtraw)
    A = -jnp.exp(alog_ref[...])
    a = dt * A
    tril = (_iota2((CHUNK, CHUNK), 0) >= _iota2((CHUNK, CHUNK), 1)).astype(F32)
    acs = jnp.dot(tril, a, preferred_element_type=F32, precision=HI)
    acs_row = _to_rows(acs)
    return u, sig, xbc, dtraw, dt, A, acs, acs_row


def _ssd_fwd(z, xbc, dtp, conv_w, conv_b, dt_bias, a_log, d_skip, norm_w):
    L = z.shape[0]
    nc = L // CHUNK

    def body(z_ref, xbc_ref, tail_ref, dt_ref, cw_ref, cb_ref, dtb_ref, alog_ref, dsk_ref, nw_ref,
             y_ref, ypre_ref, prev_ref, state, ext, ybuf):
        c = pl.program_id(0)

        @pl.when(c == 0)
        def _():
            state[...] = jnp.zeros_like(state)

        u, sig, xbcv, dtraw, dt, A, acs, acs_row = _ssd_chunk_pre(
            c == 0, xbc_ref, tail_ref, dt_ref, cw_ref, cb_ref, dtb_ref, alog_ref, ext)
        prev_ref[0] = state[...]
        causal = _iota2((CHUNK, CHUNK), 0) >= _iota2((CHUNK, CHUNK), 1)
        alast = acs[CHUNK - 1:CHUNK, :]
        for g in range(SSD_GROUPS):
            Bg = _bf(xbcv[:, D_SSD + SSD_N * g:D_SSD + SSD_N * (g + 1)])
            Cg = _bf(xbcv[:, D_SSD + D_BC + SSD_N * g:D_SSD + D_BC + SSD_N * (g + 1)])
            cb = _dot_nt(Cg, Bg)
            for r in range(SSD_R):
                h = g * SSD_R + r
                hs = slice(SSD_P * h, SSD_P * (h + 1))
                acs_c = acs[:, h:h + 1]
                seg = acs_c - acs_row[h:h + 1, :]
                Lm = jnp.where(causal, jnp.exp(jnp.where(causal, seg, 0.0)), 0.0)
                M = cb * Lm
                xh = xbcv[:, hs]
                X = xh * dt[:, h:h + 1]
                prev_h = state[hs, :]
                ydiag = _dot(_bf(M), _bf(X))
                yoff = _dot_nt(Cg, _bf(prev_h)) * jnp.exp(acs_c)
                al = alast[:, h:h + 1]
                Xd = X * jnp.exp(al - acs_c)
                state[hs, :] = prev_h * jnp.exp(al) + _dot_tn(_bf(Xd), Bg)
                ybuf[:, hs] = ydiag + yoff + dsk_ref[:, h:h + 1] * xh
        y = ybuf[...]
        ypre_ref[...] = y
        zv = z_ref[...]
        yf = y * (zv * _sigmoid(zv))
        half = D_SSD // SSD_GROUPS
        for g in range(SSD_GROUPS):
            gs = slice(half * g, half * (g + 1))
            yg = yf[:, gs]
            ms = jnp.mean(yg * yg, axis=-1, keepdims=True)
            y_ref[:, gs] = _bf(yg * lax.rsqrt(ms + RMS_EPS) * nw_ref[:, gs])

    full = lambda shape: pl.BlockSpec(shape, lambda c: (0, 0))
    return pl.pallas_call(
        body, name="ssd_fwd", grid=(nc,),
        in_specs=[
            pl.BlockSpec((CHUNK, D_SSD), lambda c: (c, 0)),
            pl.BlockSpec((CHUNK, D_XBC), lambda c: (c, 0)),
            pl.BlockSpec((8, D_XBC), lambda c: (jnp.maximum(c * (CHUNK // 8) - 1, 0), 0)),
            pl.BlockSpec((CHUNK, DT_PAD), lambda c: (c, 0)),
            full((CONV_K, D_XBC)), full((1, D_XBC)), full((1, SSD_HEADS)), full((1, SSD_HEADS)), full((1, SSD_HEADS)),
            full((1, D_SSD)),
        ],
        out_specs=[
            pl.BlockSpec((CHUNK, D_SSD), lambda c: (c, 0)),
            pl.BlockSpec((CHUNK, D_SSD), lambda c: (c, 0)),
            pl.BlockSpec((1, SSD_HEADS * SSD_P, SSD_N), lambda c: (c, 0, 0)),
        ],
        out_shape=[
            jax.ShapeDtypeStruct((L, D_SSD), F32),
            jax.ShapeDtypeStruct((L, D_SSD), F32),
            jax.ShapeDtypeStruct((nc, SSD_HEADS * SSD_P, SSD_N), F32),
        ],
        scratch_shapes=[
            pltpu.VMEM((SSD_HEADS * SSD_P, SSD_N), F32),
            pltpu.VMEM((CHUNK + 8, D_XBC), F32),
            pltpu.VMEM((CHUNK, D_SSD), F32),
        ],
        compiler_params=_cparams(("arbitrary",)),
    )(z, xbc, xbc, dtp, conv_w, conv_b, dt_bias, a_log, d_skip, norm_w)


def _ssd_bwd(dy, z, ypre, xbc, dtp, prev, conv_w, conv_b, dt_bias, a_log, d_skip, norm_w):
    L = z.shape[0]
    nc = L // CHUNK

    def body(dy_ref, z_ref, ypre_ref, xbc_ref, tail_ref, dt_ref, prev_ref, cw_ref, cb_ref, dtb_ref, alog_ref, dsk_ref,
             nw_ref, dz_ref, dxbc_ref, ddt_ref, gcw_ref, gcb_ref, gdtb_ref, galog_ref, gdsk_ref, gnw_ref,
             dstate, dhead, ext, ext2, dpost):
        i = pl.program_id(0)
        c = nc - 1 - i

        @pl.when(i == 0)
        def _():
            dstate[...] = jnp.zeros_like(dstate)
            dhead[...] = jnp.zeros_like(dhead)
            gcw_ref[...] = jnp.zeros_like(gcw_ref)
            gcb_ref[...] = jnp.zeros_like(gcb_ref)
            gdtb_ref[...] = jnp.zeros_like(gdtb_ref)
            galog_ref[...] = jnp.zeros_like(galog_ref)
            gdsk_ref[...] = jnp.zeros_like(gdsk_ref)
            gnw_ref[...] = jnp.zeros_like(gnw_ref)

        u, sig, xbcv, dtraw, dt, A, acs, acs_row = _ssd_chunk_pre(
            c == 0, xbc_ref, tail_ref, dt_ref, cw_ref, cb_ref, dtb_ref, alog_ref, ext)

        zv = z_ref[...]
        ypre = ypre_ref[...]
        dyn = dy_ref[...]
        sz = _sigmoid(zv)
        silu_z = zv * sz
        yf = ypre * silu_z
        half = D_SSD // SSD_GROUPS
        dyf_parts = []
        for g in range(SSD_GROUPS):
            gs = slice(half * g, half * (g + 1))
            yg = yf[:, gs]
            rstd = lax.rsqrt(jnp.mean(yg * yg, axis=-1, keepdims=True) + RMS_EPS)
            dout = dyn[:, gs]
            gnw_ref[:, gs] += jnp.sum(dout * yg * rstd, axis=0, keepdims=True)
            dyhat = dout * nw_ref[:, gs]
            dyf_parts.append(rstd * (dyhat - yg * (rstd * rstd) * jnp.mean(dyhat * yg, axis=-1, keepdims=True)))
        dyf = jnp.concatenate(dyf_parts, axis=1)
        dz_ref[...] = _bf(dyf * ypre * (sz * (1.0 + zv * (1.0 - sz))))
        dypre = dyf * silu_z

        causal = _iota2((CHUNK, CHUNK), 0) >= _iota2((CHUNK, CHUNK), 1)
        alast = acs[CHUNK - 1:CHUNK, :]
        lane16 = _iota2((1, SSD_HEADS), 1)
        sub16 = _iota2((SSD_HEADS, 1), 0)
        dacs_col = jnp.zeros((CHUNK, SSD_HEADS), F32)
        dacs_row = jnp.zeros((SSD_HEADS, CHUNK), F32)
        ddt_col = jnp.zeros((CHUNK, SSD_HEADS), F32)
        dalast = jnp.zeros((1, SSD_HEADS), F32)
        gdsk = jnp.zeros((1, SSD_HEADS), F32)
        for g in range(SSD_GROUPS):
            bs = slice(D_SSD + SSD_N * g, D_SSD + SSD_N * (g + 1))
            cs = slice(D_SSD + D_BC + SSD_N * g, D_SSD + D_BC + SSD_N * (g + 1))
            Bg = _bf(xbcv[:, bs])
            Cg = _bf(xbcv[:, cs])
            cb = _dot_nt(Cg, Bg)
            dcb = jnp.zeros((CHUNK, CHUNK), F32)
            dB = jnp.zeros((CHUNK, SSD_N), F32)
            dC = jnp.zeros((CHUNK, SSD_N), F32)
            for r in range(SSD_R):
                h = g * SSD_R + r
                hs = slice(SSD_P * h, SSD_P * (h + 1))
                onehot = (lane16 == h).astype(F32)
                acs_c = acs[:, h:h + 1]
                seg = acs_c - acs_row[h:h + 1, :]
                Lm = jnp.where(causal, jnp.exp(jnp.where(causal, seg, 0.0)), 0.0)
                M = cb * Lm
                xh = xbcv[:, hs]
                dth = dt[:, h:h + 1]
                X = xh * dth
                Xb = _bf(X)
                dyh = dypre[:, hs]
                dyb = _bf(dyh)
                prev_h = prev_ref[0, hs, :]
                prevb = _bf(prev_h)
                dnext = dstate[hs, :]
                dnextb = _bf(dnext)
                al = alast[:, h:h + 1]
                eacs = jnp.exp(acs_c)
                eal = jnp.exp(al)
                dsd = jnp.exp(al - acs_c)
                G = _bf(dyh * eacs)
                dstate[hs, :] = dnext * eal + _dot_tn(G, Cg)
                dC = dC + _dot(G, prevb)
                yoff = _dot_nt(Cg, prevb) * eacs
                dacs_h = jnp.sum(dyh * yoff, axis=-1, keepdims=True)
                BdN = _dot_nt(Bg, dnextb)
                dX = dsd * BdN
                dB = dB + _dot(_bf(X * dsd), dnextb)
                t = jnp.sum(X * BdN, axis=-1, keepdims=True) * dsd
                dacs_h = dacs_h - t
                dal = jnp.sum(t, axis=0, keepdims=True) + jnp.sum(
                    jnp.sum(dnext * prev_h, axis=-1, keepdims=True), axis=0, keepdims=True) * eal
                dM = _dot_nt(dyb, Xb)
                dX = dX + _dot_tn(_bf(M), dyb)
                dseg = dM * M
                dcb = dcb + dM * Lm
                dacs_h = dacs_h + jnp.sum(dseg, axis=-1, keepdims=True)
                dacs_row = dacs_row - jnp.sum(dseg, axis=0, keepdims=True) * (sub16 == h).astype(F32)
                dacs_col = dacs_col + dacs_h * onehot
                dalast = dalast + dal * onehot
                ddt_col = ddt_col + jnp.sum(dX * xh, axis=-1, keepdims=True) * onehot
                gdsk = gdsk + jnp.sum(jnp.sum(dyh * xh, axis=-1, keepdims=True), axis=0, keepdims=True) * onehot
                dpost[:, hs] = dX * dth + dsk_ref[:, h:h + 1] * dyh
            dcbb = _bf(dcb)
            dpost[:, bs] = dB + _dot_tn(dcbb, Cg)
            dpost[:, cs] = dC + _dot(dcbb, Bg)

        is_last = (_iota2((CHUNK, 1), 0) == CHUNK - 1).astype(F32)
        dacs = dacs_col + _to_cols(dacs_row) + is_last * dalast
        triu = (_iota2((CHUNK, CHUNK), 0) <= _iota2((CHUNK, CHUNK), 1)).astype(F32)
        da = jnp.dot(triu, dacs, preferred_element_type=F32, precision=HI)
        ddt_tot = ddt_col + da * A
        galog_ref[...] += jnp.sum(da * dt, axis=0, keepdims=True) * A
        ddtraw = ddt_tot * _sigmoid(dtraw)
        gdtb_ref[...] += jnp.sum(ddtraw, axis=0, keepdims=True)
        gdsk_ref[...] += gdsk
        ddt_ref[...] = jnp.zeros_like(ddt_ref)
        ddt_ref[:, 0:SSD_HEADS] = ddtraw

        dconv = dpost[...] * (sig * (1.0 + u * (1.0 - sig)))
        gcb_ref[...] += jnp.sum(dconv, axis=0, keepdims=True)
        for k in range(CONV_K):
            gcw_ref[k:k + 1, :] += jnp.sum(dconv * ext[pl.ds(5 + k, CHUNK), :], axis=0, keepdims=True)
        ext2[0:CHUNK, :] = dconv
        ext2[CHUNK:CHUNK + 8, :] = dhead[...]
        dx = cw_ref[CONV_K - 1:CONV_K, :] * dconv
        for k in range(CONV_K - 1):
            dx = dx + cw_ref[k:k + 1, :] * ext2[pl.ds(CONV_K - 1 - k, CHUNK), :]
        dxbc_ref[...] = _bf(dx)
        dhead[...] = dconv[0:8, :]

    full = lambda shape: pl.BlockSpec(shape, lambda i: (0, 0))
    rev = lambda wd: pl.BlockSpec((CHUNK, wd), lambda i: (nc - 1 - i, 0))
    return pl.pallas_call(
        body, name="ssd_bwd", grid=(nc,),
        in_specs=[
            rev(D_SSD), rev(D_SSD), rev(D_SSD), rev(D_XBC),
            pl.BlockSpec((8, D_XBC), lambda i: (jnp.maximum((nc - 1 - i) * (CHUNK // 8) - 1, 0), 0)),
            rev(DT_PAD),
            pl.BlockSpec((1, SSD_HEADS * SSD_P, SSD_N), lambda i: (nc - 1 - i, 0, 0)),
            full((CONV_K, D_XBC)), full((1, D_XBC)), full((1, SSD_HEADS)), full((1, SSD_HEADS)), full((1, SSD_HEADS)),
            full((1, D_SSD)),
        ],
        out_specs=[
            rev(D_SSD), rev(D_XBC), rev(DT_PAD),
            full((CONV_K, D_XBC)), full((1, D_XBC)), full((1, SSD_HEADS)), full((1, SSD_HEADS)), full((1, SSD_HEADS)),
            full((1, D_SSD)),
        ],
        out_shape=[
            jax.ShapeDtypeStruct((L, D_SSD), BF16), jax.ShapeDtypeStruct((L, D_XBC), BF16),
            jax.ShapeDtypeStruct((L, DT_PAD), F32),
            jax.ShapeDtypeStruct((CONV_K, D_XBC), F32), jax.ShapeDtypeStruct((1, D_XBC), F32),
            jax.ShapeDtypeStruct((1, SSD_HEADS), F32), jax.ShapeDtypeStruct((1, SSD_HEADS), F32),
            jax.ShapeDtypeStruct((1, SSD_HEADS), F32), jax.ShapeDtypeStruct((1, D_SSD), F32),
        ],
        scratch_shapes=[
            pltpu.VMEM((SSD_HEADS * SSD_P, SSD_N), F32),
            pltpu.VMEM((8, D_XBC), F32),
            pltpu.VMEM((CHUNK + 8, D_XBC), F32),
            pltpu.VMEM((CHUNK + 8, D_XBC), F32),
            pltpu.VMEM((CHUNK, D_XBC), F32),
        ],
        compiler_params=_cparams(("arbitrary",)),
    )(dy, z, ypre, xbc, xbc, dtp, prev, conv_w, conv_b, dt_bias, a_log, d_skip, norm_w)


def _head_expander():
    return (_iota2((SSD_HEADS, D_SSD), 1) // SSD_P == _iota2((SSD_HEADS, D_SSD), 0)).astype(BF16)


def _hi_lo(x):
    hi = _bf(x)
    return hi, _bf(x - hi.astype(F32))


def _expand(v, e):
    hi, lo = _hi_lo(v)
    return _dot(hi, e) + _dot(lo, e)


def _headsum(t, e):
    m = t.shape[0]
    if m < 8:
        t = jnp.broadcast_to(t[0:1], (8, t.shape[1]))
    hi, lo = _hi_lo(t)
    return (_dot_nt(hi, e) + _dot_nt(lo, e))[0:m]


def _ssd_decays(dt, acs, dsk_ref, e):
    alast = acs[CHUNK - 1:CHUNK, :]
    stk = jnp.concatenate([dt, jnp.exp(acs), jnp.exp(alast - acs),
                           jnp.broadcast_to(jnp.exp(alast), (8, SSD_HEADS)),
                           jnp.broadcast_to(dsk_ref[...], (8, SSD_HEADS))], axis=0)
    ex = _expand(stk, e)
    return (ex[0:CHUNK], ex[CHUNK:2 * CHUNK], ex[2 * CHUNK:3 * CHUNK], ex[3 * CHUNK:3 * CHUNK + 1],
            ex[3 * CHUNK + 8:3 * CHUNK + 9])


def _ssd_fwd2(z, xbc, dtp, conv_w, conv_b, dt_bias, a_log, d_skip, norm_w):
    L = z.shape[0]
    nc = L // CHUNK
    half = D_SSD // SSD_GROUPS

    def body(z_ref, xbc_ref, tail_ref, dt_ref, cw_ref, cb_ref, dtb_ref, alog_ref, dsk_ref, nw_ref,
             y_ref, ypre_ref, prev_ref, state, ext, ybuf, mbuf):
        c = pl.program_id(0)

        @pl.when(c == 0)
        def _():
            state[...] = jnp.zeros_like(state)

        u, sig, xbcv, dtraw, dt, A, acs, acs_row = _ssd_chunk_pre(
            c == 0, xbc_ref, tail_ref, dt_ref, cw_ref, cb_ref, dtb_ref, alog_ref, ext)
        e = _head_expander()
        dtE, eacsE, dsdE, ealE, dskE = _ssd_decays(dt, acs, dsk_ref, e)
        xs = xbcv[:, 0:D_SSD]
        X = xs * dtE
        prev_ref[0] = state[...]
        causal = _iota2((CHUNK, CHUNK), 0) >= _iota2((CHUNK, CHUNK), 1)
        for g in range(SSD_GROUPS):
            gs = slice(half * g, half * (g + 1))
            Bg = _bf(xbcv[:, D_SSD + SSD_N * g:D_SSD + SSD_N * (g + 1)])
            Cg = _bf(xbcv[:, D_SSD + D_BC + SSD_N * g:D_SSD + D_BC + SSD_N * (g + 1)])
            cb = _dot_nt(Cg, Bg)
            for r in range(SSD_R):
                h = g * SSD_R + r
                seg = acs[:, h:h + 1] - acs_row[h:h + 1, :]
                mbuf[h] = _bf(cb * jnp.where(causal, jnp.exp(jnp.where(causal, seg, 0.0)), 0.0))
            st = state[:, gs]
            ybuf[:, gs] = _dot(Cg, _bf(st)) * eacsE[:, gs] + dskE[:, gs] * xs[:, gs]
            state[:, gs] = st * ealE[:, gs] + _dot_tn(Bg, _bf(X[:, gs] * dsdE[:, gs]))
        Xb = _bf(X)
        for h in range(SSD_HEADS):
            hs = slice(SSD_P * h, SSD_P * (h + 1))
            ybuf[:, hs] += _dot(mbuf[h], Xb[:, hs])
        y = ybuf[...]
        ypre_ref[...] = y
        zv = z_ref[...]
        yf = y * (zv * _sigmoid(zv))
        for g in range(SSD_GROUPS):
            gs = slice(half * g, half * (g + 1))
            yg = yf[:, gs]
            ms = jnp.mean(yg * yg, axis=-1, keepdims=True)
            y_ref[:, gs] = _bf(yg * lax.rsqrt(ms + RMS_EPS) * nw_ref[:, gs])

    full = lambda shape: pl.BlockSpec(shape, lambda c: (0, 0))
    return pl.pallas_call(
        body, name="ssd_fwd", grid=(nc,),
        in_specs=[
            pl.BlockSpec((CHUNK, D_SSD), lambda c: (c, 0)),
            pl.BlockSpec((CHUNK, D_XBC), lambda c: (c, 0)),
            pl.BlockSpec((8, D_XBC), lambda c: (jnp.maximum(c * (CHUNK // 8) - 1, 0), 0)),
            pl.BlockSpec((CHUNK, DT_PAD), lambda c: (c, 0)),
            full((CONV_K, D_XBC)), full((1, D_XBC)), full((1, SSD_HEADS)), full((1, SSD_HEADS)), full((1, SSD_HEADS)),
            full((1, D_SSD)),
        ],
        out_specs=[
            pl.BlockSpec((CHUNK, D_SSD), lambda c: (c, 0)),
            pl.BlockSpec((CHUNK, D_SSD), lambda c: (c, 0)),
            pl.BlockSpec((1, SSD_N, D_SSD), lambda c: (c, 0, 0)),
        ],
        out_shape=[
            jax.ShapeDtypeStruct((L, D_SSD), BF16),
            jax.ShapeDtypeStruct((L, D_SSD), F32),
            jax.ShapeDtypeStruct((nc, SSD_N, D_SSD), F32),
        ],
        scratch_shapes=[
            pltpu.VMEM((SSD_N, D_SSD), F32),
            pltpu.VMEM((CHUNK + 8, D_XBC), F32),
            pltpu.VMEM((CHUNK, D_SSD), F32),
            pltpu.VMEM((SSD_HEADS, CHUNK, CHUNK), BF16),
        ],
        compiler_params=_cparams(("arbitrary",)),
    )(z, xbc, xbc, dtp, conv_w, conv_b, dt_bias, a_log, d_skip, norm_w)


def _ssd_bwd2(dy, z, ypre, xbc, dtp, prev, conv_w, conv_b, dt_bias, a_log, d_skip, norm_w):
    L = z.shape[0]
    nc = L // CHUNK
    half = D_SSD // SSD_GROUPS

    def body(dy_ref, z_ref, ypre_ref, xbc_ref, tail_ref, dt_ref, prev_ref, cw_ref, cb_ref, dtb_ref, alog_ref, dsk_ref,
             nw_ref, dz_ref, dxbc_ref, ddt_ref, gcw_ref, gcb_ref, gdtb_ref, galog_ref, gdsk_ref, gnw_ref,
             dstate, dhead, ext, ext2, dpost, yobuf, bdbuf, lmbuf, dmbuf, cbbuf):
        i = pl.program_id(0)
        c = nc - 1 - i

        @pl.when(i == 0)
        def _():
            dstate[...] = jnp.zeros_like(dstate)
            dhead[...] = jnp.zeros_like(dhead)
            gcw_ref[...] = jnp.zeros_like(gcw_ref)
            gcb_ref[...] = jnp.zeros_like(gcb_ref)
            gdtb_ref[...] = jnp.zeros_like(gdtb_ref)
            galog_ref[...] = jnp.zeros_like(galog_ref)
            gdsk_ref[...] = jnp.zeros_like(gdsk_ref)
            gnw_ref[...] = jnp.zeros_like(gnw_ref)

        u, sig, xbcv, dtraw, dt, A, acs, acs_row = _ssd_chunk_pre(
            c == 0, xbc_ref, tail_ref, dt_ref, cw_ref, cb_ref, dtb_ref, alog_ref, ext)
        e = _head_expander()
        dtE, eacsE, dsdE, ealE, dskE = _ssd_decays(dt, acs, dsk_ref, e)
        alast = acs[CHUNK - 1:CHUNK, :]
        xs = xbcv[:, 0:D_SSD]
        X = xs * dtE
        Xb = _bf(X)

        zv = z_ref[...]
        ypre = ypre_ref[...]
        dyn = dy_ref[...]
        sz = _sigmoid(zv)
        silu_z = zv * sz
        yf = ypre * silu_z
        dyf_parts = []
        for g in range(SSD_GROUPS):
            gs = slice(half * g, half * (g + 1))
            yg = yf[:, gs]
            rstd = lax.rsqrt(jnp.mean(yg * yg, axis=-1, keepdims=True) + RMS_EPS)
            dout = dyn[:, gs]
            gnw_ref[:, gs] += jnp.sum(dout * yg * rstd, axis=0, keepdims=True)
            dyhat = dout * nw_ref[:, gs]
            dyf_parts.append(rstd * (dyhat - yg * (rstd * rstd) * jnp.mean(dyhat * yg, axis=-1, keepdims=True)))
        dyf = jnp.concatenate(dyf_parts, axis=1)
        dz_ref[...] = _bf(dyf * ypre * (sz * (1.0 + zv * (1.0 - sz))))
        dyp = dyf * silu_z
        dyb = _bf(dyp)
        G = dyp * eacsE

        causal = _iota2((CHUNK, CHUNK), 0) >= _iota2((CHUNK, CHUNK), 1)
        ST = prev_ref[0]
        dST = dstate[...]
        for g in range(SSD_GROUPS):
            gs = slice(half * g, half * (g + 1))
            bs = slice(D_SSD + SSD_N * g, D_SSD + SSD_N * (g + 1))
            cs = slice(D_SSD + D_BC + SSD_N * g, D_SSD + D_BC + SSD_N * (g + 1))
            Bg = _bf(xbcv[:, bs])
            Cg = _bf(xbcv[:, cs])
            Gb = _bf(G[:, gs])
            STb = _bf(ST[:, gs])
            dSTb = _bf(dST[:, gs])
            dstate[:, gs] = dST[:, gs] * ealE[:, gs] + _dot_tn(Cg, Gb)
            yobuf[:, gs] = _dot(Cg, STb) * eacsE[:, gs]
            bdbuf[:, gs] = _dot(Bg, dSTb)
            dpost[:, cs] = _dot_nt(Gb, STb)
            dpost[:, bs] = _dot_nt(_bf(X[:, gs] * dsdE[:, gs]), dSTb)
            cbbuf[g] = _dot_nt(Cg, Bg)
            for r in range(SSD_R):
                h = g * SSD_R + r
                seg = acs[:, h:h + 1] - acs_row[h:h + 1, :]
                lmbuf[h] = jnp.where(causal, jnp.exp(jnp.where(causal, seg, 0.0)), 0.0)
        for h in range(SSD_HEADS):
            hs = slice(SSD_P * h, SSD_P * (h + 1))
            Mb = _bf(cbbuf[h // SSD_R] * lmbuf[h])
            dmbuf[h] = _dot_nt(dyb[:, hs], Xb[:, hs])
            dpost[:, hs] = _dot_tn(Mb, dyb[:, hs])
        lane16 = _iota2((1, SSD_HEADS), 1)
        sub16 = _iota2((SSD_HEADS, 1), 0)
        dacs_col = jnp.zeros((CHUNK, SSD_HEADS), F32)
        dacs_row = jnp.zeros((SSD_HEADS, CHUNK), F32)
        for g in range(SSD_GROUPS):
            bs = slice(D_SSD + SSD_N * g, D_SSD + SSD_N * (g + 1))
            cs = slice(D_SSD + D_BC + SSD_N * g, D_SSD + D_BC + SSD_N * (g + 1))
            cb = cbbuf[g]
            dcb = jnp.zeros((CHUNK, CHUNK), F32)
            for r in range(SSD_R):
                h = g * SSD_R + r
                dM = dmbuf[h]
                Lm = lmbuf[h]
                dcb = dcb + dM * Lm
                dseg = dM * (cb * Lm)
                dacs_col = dacs_col + jnp.sum(dseg, axis=-1, keepdims=True) * (lane16 == h).astype(F32)
                dacs_row = dacs_row - jnp.sum(dseg, axis=0, keepdims=True) * (sub16 == h).astype(F32)
            dcbb = _bf(dcb)
            dpost[:, bs] += _dot_tn(dcbb, _bf(xbcv[:, cs]))
            dpost[:, cs] += _dot(dcbb, _bf(xbcv[:, bs]))

        BD = bdbuf[...]
        dX = dpost[:, 0:D_SSD] + dsdE * BD
        dsd = jnp.exp(alast - acs)
        T = _headsum(X * BD, e) * dsd
        dalast = jnp.sum(T, axis=0, keepdims=True) + _headsum(
            jnp.sum(dST * ST, axis=0, keepdims=True), e) * jnp.exp(alast)
        is_last = (_iota2((CHUNK, 1), 0) == CHUNK - 1).astype(F32)
        dacs = dacs_col + _to_cols(dacs_row) + _headsum(dyp * yobuf[...], e) - T + is_last * dalast
        triu = (_iota2((CHUNK, CHUNK), 0) <= _iota2((CHUNK, CHUNK), 1)).astype(F32)
        da = jnp.dot(triu, dacs, preferred_element_type=F32, precision=HI)
        ddt_tot = _headsum(dX * xs, e) + da * A
        galog_ref[...] += jnp.sum(da * dt, axis=0, keepdims=True) * A
        ddtraw = ddt_tot * _sigmoid(dtraw)
        gdtb_ref[...] += jnp.sum(ddtraw, axis=0, keepdims=True)
        gdsk_ref[...] += _headsum(jnp.sum(dyp * xs, axis=0, keepdims=True), e)
        ddt_ref[...] = jnp.zeros_like(ddt_ref)
        ddt_ref[:, 0:SSD_HEADS] = ddtraw
        dpost[:, 0:D_SSD] = dX * dtE + dskE * dyp

        dconv = dpost[...] * (sig * (1.0 + u * (1.0 - sig)))
        gcb_ref[...] += jnp.sum(dconv, axis=0, keepdims=True)
        for k in range(CONV_K):
            gcw_ref[k:k + 1, :] += jnp.sum(dconv * ext[pl.ds(5 + k, CHUNK), :], axis=0, keepdims=True)
        ext2[0:CHUNK, :] = dconv
        ext2[CHUNK:CHUNK + 8, :] = dhead[...]
        dx = cw_ref[CONV_K - 1:CONV_K, :] * dconv
        for k in range(CONV_K - 1):
            dx = dx + cw_ref[k:k + 1, :] * ext2[pl.ds(CONV_K - 1 - k, CHUNK), :]
        dxbc_ref[...] = _bf(dx)
        dhead[...] = dconv[0:8, :]

    full = lambda shape: pl.BlockSpec(shape, lambda i: (0, 0))
    rev = lambda wd: pl.BlockSpec((CHUNK, wd), lambda i: (nc - 1 - i, 0))
    return pl.pallas_call(
        body, name="ssd_bwd", grid=(nc,),
        in_specs=[
            rev(D_SSD), rev(D_SSD), rev(D_SSD), rev(D_XBC),
            pl.BlockSpec((8, D_XBC), lambda i: (jnp.maximum((nc - 1 - i) * (CHUNK // 8) - 1, 0), 0)),
            rev(DT_PAD),
            pl.BlockSpec((1, SSD_N, D_SSD), lambda i: (nc - 1 - i, 0, 0)),
            full((CONV_K, D_XBC)), full((1, D_XBC)), full((1, SSD_HEADS)), full((1, SSD_HEADS)), full((1, SSD_HEADS)),
            full((1, D_SSD)),
        ],
        out_specs=[
            rev(D_SSD), rev(D_XBC), rev(DT_PAD),
            full((CONV_K, D_XBC)), full((1, D_XBC)), full((1, SSD_HEADS)), full((1, SSD_HEADS)), full((1, SSD_HEADS)),
            full((1, D_SSD)),
        ],
        out_shape=[
            jax.ShapeDtypeStruct((L, D_SSD), BF16), jax.ShapeDtypeStruct((L, D_XBC), BF16),
            jax.ShapeDtypeStruct((L, DT_PAD), F32),
            jax.ShapeDtypeStruct((CONV_K, D_XBC), F32), jax.ShapeDtypeStruct((1, D_XBC), F32),
            jax.ShapeDtypeStruct((1, SSD_HEADS), F32), jax.ShapeDtypeStruct((1, SSD_HEADS), F32),
            jax.ShapeDtypeStruct((1, SSD_HEADS), F32), jax.ShapeDtypeStruct((1, D_SSD), F32),
        ],
        scratch_shapes=[
            pltpu.VMEM((SSD_N, D_SSD), F32),
            pltpu.VMEM((8, D_XBC), F32),
            pltpu.VMEM((CHUNK + 8, D_XBC), F32),
            pltpu.VMEM((CHUNK + 8, D_XBC), F32),
            pltpu.VMEM((CHUNK, D_XBC), F32),
            pltpu.VMEM((CHUNK, D_SSD), F32),
            pltpu.VMEM((CHUNK, D_SSD), F32),
            pltpu.VMEM((SSD_HEADS, CHUNK, CHUNK), F32),
            pltpu.VMEM((SSD_HEADS, CHUNK, CHUNK), F32),
            pltpu.VMEM((SSD_GROUPS, CHUNK, CHUNK), F32),
        ],
        compiler_params=_cparams(("arbitrary",)),
    )(dy, z, ypre, xbc, xbc, dtp, prev, conv_w, conv_b, dt_bias, a_log, d_skip, norm_w)


def _rope_tables(pos_ref, inv_ref):
    ang = pos_ref[...].astype(F32) * inv_ref[...]
    d = _iota2((1, 2 * ATT_HD), 1) % ATT_HD
    s = jnp.sin(ang)
    return jnp.cos(ang), jnp.where(d < ROPE_DIM // 2, -s, 0.0), jnp.where((d >= ROPE_DIM // 2) & (d < ROPE_DIM), s, 0.0)


def _rope(t, tabs):
    c, s1, s2 = tabs
    n = t.shape[1]
    rep = n // c.shape[1]
    return (t * jnp.tile(c, (1, rep)) + pltpu.roll(t, n - ROPE_DIM // 2, 1) * jnp.tile(s1, (1, rep))
            + pltpu.roll(t, ROPE_DIM // 2, 1) * jnp.tile(s2, (1, rep)))


def _rope_t(t, tabs):
    c, s1, s2 = tabs
    n = t.shape[1]
    rep = n // c.shape[1]
    return (t * jnp.tile(c, (1, rep)) + pltpu.roll(t * jnp.tile(s1, (1, rep)), ROPE_DIM // 2, 1)
            + pltpu.roll(t * jnp.tile(s2, (1, rep)), n - ROPE_DIM // 2, 1))


def _swa_mask(first):
    qi = _iota2((WINDOW, 2 * WINDOW), 0)
    si = _iota2((WINDOW, 2 * WINDOW), 1)
    band = (si > qi) & (si <= qi + WINDOW)
    return band & (jnp.logical_not(first) | (si >= WINDOW))


def _stack_heads(t, j):
    return jnp.concatenate([t[:, ATT_HD * (j * ATT_R + r):ATT_HD * (j * ATT_R + r + 1)] for r in range(ATT_R)], axis=0)


def _stack_cols(ref, j):
    cols = [jnp.broadcast_to(ref[:, j * ATT_R + r:j * ATT_R + r + 1], (WINDOW, 1)) for r in range(ATT_R)]
    return jnp.concatenate(cols, axis=0)


def _swa_mask_t(first):
    si = _iota2((2 * WINDOW, ATT_R * WINDOW), 0)
    qi = _iota2((2 * WINDOW, ATT_R * WINDOW), 1) % WINDOW
    band = (si > qi) & (si <= qi + WINDOW)
    return band & (jnp.logical_not(first) | (si >= WINDOW))


def _head_rows(ref, j, rows=None):
    if ref.shape[0] == 1:
        parts = [jnp.broadcast_to(ref[:, j * ATT_R + r:j * ATT_R + r + 1], (1, WINDOW)) for r in range(ATT_R)]
    else:
        parts = [ref[j * ATT_R + r:j * ATT_R + r + 1, :] for r in range(ATT_R)]
    return jnp.concatenate(parts, axis=1)


def _swa_fwd(q, g, kv, sinks):
    L = q.shape[0]
    nb = L // WINDOW
    scale = ATT_HD ** -0.5

    def body(q_ref, g_ref, kvc_ref, kvp_ref, sink_ref, y_ref, o_ref, lse_ref, otbuf):
        n = pl.program_id(0)
        kk = jnp.concatenate([kvp_ref[:, 0:D_KV], kvc_ref[:, 0:D_KV]], axis=0)
        vv = jnp.concatenate([kvp_ref[:, D_KV:2 * D_KV], kvc_ref[:, D_KV:2 * D_KV]], axis=0)
        valid = _swa_mask_t(n == 0)
        qv = q_ref[...]
        for j in range(ATT_KVH):
            js = slice(ATT_HD * j, ATT_HD * (j + 1))
            st = _dot_nt(kk[:, js], _stack_heads(qv, j)) * scale
            st = jnp.where(valid, st, NEG_BIG)
            sink = _head_rows(sink_ref, j)
            m = jnp.maximum(jnp.max(st, axis=0, keepdims=True), sink)
            p = jnp.exp(st - m)
            denom = jnp.sum(p, axis=0, keepdims=True) + jnp.exp(sink - m)
            ot = _dot_tn(vv[:, js], _bf(p)) * (1.0 / denom)
            lse = m + jnp.log(denom)
            for r in range(ATT_R):
                h = j * ATT_R + r
                otbuf[ATT_HD * h:ATT_HD * (h + 1), :] = ot[:, WINDOW * r:WINDOW * (r + 1)]
                lse_ref[h:h + 1, :] = lse[:, WINDOW * r:WINDOW * (r + 1)]
        o = otbuf[...].T
        o_ref[...] = o
        gv = g_ref[...]
        y_ref[...] = _bf(o * (gv * _sigmoid(gv)))

    cur = lambda wd: pl.BlockSpec((WINDOW, wd), lambda n: (n, 0))
    prv = lambda wd: pl.BlockSpec((WINDOW, wd), lambda n: (jnp.maximum(n - 1, 0), 0))
    return pl.pallas_call(
        body, name="swa_fwd", grid=(nb,),
        in_specs=[cur(D_ATT), cur(D_ATT), cur(2 * D_KV), prv(2 * D_KV), pl.BlockSpec((1, ATT_QH), lambda n: (0, 0))],
        out_specs=[cur(D_ATT), cur(D_ATT), pl.BlockSpec((ATT_QH, WINDOW), lambda n: (0, n))],
        out_shape=[jax.ShapeDtypeStruct((L, D_ATT), BF16), jax.ShapeDtypeStruct((L, D_ATT), F32),
                   jax.ShapeDtypeStruct((ATT_QH, L), F32)],
        scratch_shapes=[pltpu.VMEM((D_ATT, WINDOW), F32)],
        compiler_params=_cparams(("parallel",)),
    )(q, g, kv, kv, sinks)


def _swa_bwd(dy, q, g, kv, o, lse, pos, inv, sinks):
    L = q.shape[0]
    nb = L // WINDOW
    scale = ATT_HD ** -0.5

    def body(dy_ref, q_ref, g_ref, kvc_ref, kvp_ref, o_ref, lse_ref, posc_ref, posp_ref, inv_ref, sink_ref,
             dq_ref, dg_ref, dkv_ref, dsink_ref, carry, dqbuf, dkbuf, dvbuf):
        n = pl.program_id(0)

        @pl.when(n == 0)
        def _():
            dsink_ref[...] = jnp.zeros_like(dsink_ref)

        @pl.when(n < nb)
        def _():
            tc = _rope_tables(posc_ref, inv_ref)
            tp = _rope_tables(posp_ref, inv_ref)
            kk = jnp.concatenate([kvp_ref[:, 0:D_KV], kvc_ref[:, 0:D_KV]], axis=0)
            vv = jnp.concatenate([kvp_ref[:, D_KV:2 * D_KV], kvc_ref[:, D_KV:2 * D_KV]], axis=0)
            valid = _swa_mask_t(n == 0)
            qv = q_ref[...]
            gv = g_ref[...]
            sg = _sigmoid(gv)
            dyv = dy_ref[...]
            ov = o_ref[...]
            dg_ref[...] = _bf(dyv * ov * (sg * (1.0 + gv * (1.0 - sg))))
            do = dyv * (gv * sg)
            dod = do * ov
            ones = jnp.ones((8, ATT_HD), BF16)
            lane16 = _iota2((1, ATT_QH), 1)
            dsink = jnp.zeros((1, ATT_QH), F32)
            for j in range(ATT_KVH):
                js = slice(ATT_HD * j, ATT_HD * (j + 1))
                kj = kk[:, js]
                vj = vv[:, js]
                qs = _stack_heads(qv, j)
                dos = _bf(_stack_heads(do, j))
                hi, lo = _hi_lo(_stack_heads(dod, j))
                delta = (_dot_nt(ones, hi) + _dot_nt(ones, lo))[0:1]
                lse = _head_rows(lse_ref, j)
                st = _dot_nt(kj, qs) * scale
                pt = jnp.exp(jnp.where(valid, st, NEG_BIG) - lse)
                dst = _bf(pt * (_dot_nt(vj, dos) - delta))
                dqt = _dot_tn(kj, dst) * scale
                dkbuf[:, js] = _dot(dst, qs) * scale
                dvbuf[:, js] = _dot(_bf(pt), dos)
                sd = jnp.exp(_head_rows(sink_ref, j) - lse) * delta
                for r in range(ATT_R):
                    h = j * ATT_R + r
                    ls = slice(WINDOW * r, WINDOW * (r + 1))
                    dqbuf[ATT_HD * h:ATT_HD * (h + 1), :] = dqt[:, ls]
                    dsink = dsink - jnp.sum(sd[:, ls], axis=1, keepdims=True) * (lane16 == h).astype(F32)
            dsink_ref[...] += dsink
            dq_ref[...] = _bf(_rope_t(dqbuf[...].T, tc))
            dkp = _rope_t(dkbuf[0:WINDOW, :], tp)
            dkc = _rope_t(dkbuf[WINDOW:2 * WINDOW, :], tc)

            @pl.when(n > 0)
            def _():
                dkv_ref[:, 0:D_KV] = _bf(carry[:, 0:D_KV] + dkp)
                dkv_ref[:, D_KV:2 * D_KV] = _bf(carry[:, D_KV:2 * D_KV] + dvbuf[0:WINDOW, :])

            carry[:, 0:D_KV] = dkc
            carry[:, D_KV:2 * D_KV] = dvbuf[WINDOW:2 * WINDOW, :]

        @pl.when(n == nb)
        def _():
            dkv_ref[...] = _bf(carry[...])

    last = nb - 1
    cur = lambda wd: pl.BlockSpec((WINDOW, wd), lambda n: (jnp.minimum(n, last), 0))
    prv = lambda wd: pl.BlockSpec((WINDOW, wd), lambda n: (jnp.maximum(jnp.minimum(n, last) - 1, 0), 0))
    return pl.pallas_call(
        body, name="swa_bwd", grid=(nb + 1,),
        in_specs=[cur(D_ATT), cur(D_ATT), cur(D_ATT), cur(2 * D_KV), prv(2 * D_KV), cur(D_ATT),
                  pl.BlockSpec((ATT_QH, WINDOW), lambda n: (0, jnp.minimum(n, last))), cur(1), prv(1),
                  pl.BlockSpec((1, 2 * ATT_HD), lambda n: (0, 0)), pl.BlockSpec((1, ATT_QH), lambda n: (0, 0))],
        out_specs=[cur(D_ATT), cur(D_ATT),
                   pl.BlockSpec((WINDOW, 2 * D_KV), lambda n: (jnp.maximum(n - 1, 0), 0)),
                   pl.BlockSpec((1, ATT_QH), lambda n: (0, 0))],
        out_shape=[jax.ShapeDtypeStruct((L, D_ATT), BF16), jax.ShapeDtypeStruct((L, D_ATT), BF16),
                   jax.ShapeDtypeStruct((L, 2 * D_KV), BF16), jax.ShapeDtypeStruct((1, ATT_QH), F32)],
        scratch_shapes=[pltpu.VMEM((WINDOW, 2 * D_KV), F32), pltpu.VMEM((D_ATT, WINDOW), F32),
                        pltpu.VMEM((2 * WINDOW, D_KV), F32), pltpu.VMEM((2 * WINDOW, D_KV), F32)],
        compiler_params=_cparams(("arbitrary",)),
    )(dy, q, g, kv, kv, o, lse, pos, pos, inv, sinks)


def _out_ln_loss(y_ssd, y_att, x, target, w_out, ln_g, ln_b):
    L = x.shape[0]
    tm = ROW_TILE
    inv_d = 1.0 / D_MODEL

    def body(ys_ref, ya_ref, x_ref, t_ref, w_ref, g_ref, b_ref, dr_ref, dys_ref, dya_ref, loss_ref, gg_ref, gb_ref):
        i = pl.program_id(0)

        @pl.when(i == 0)
        def _():
            loss_ref[...] = jnp.zeros_like(loss_ref)
            gg_ref[...] = jnp.zeros_like(gg_ref)
            gb_ref[...] = jnp.zeros_like(gb_ref)

        h = _dot(_bf(ys_ref[...]), w_ref[0:D_SSD, :]) + _dot(_bf(ya_ref[...]), w_ref[D_SSD:D_MIX, :])
        r = ALPHA * x_ref[...] + h
        mu = jnp.mean(r, axis=-1, keepdims=True)
        xc = r - mu
        rstd = lax.rsqrt(jnp.mean(xc * xc, axis=-1, keepdims=True) + LN_EPS)
        xhat = xc * rstd
        gam = g_ref[...]
        diff = xhat * gam + b_ref[...] - t_ref[...]
        part = jnp.sum(jnp.sum(diff * diff, axis=-1, keepdims=True), axis=0, keepdims=True)
        loss_ref[...] += (0.5 * inv_d) * part
        dout = diff * inv_d
        gg_ref[...] += jnp.sum(dout * xhat, axis=0, keepdims=True)
        gb_ref[...] += jnp.sum(dout, axis=0, keepdims=True)
        dxh = dout * gam
        dr = rstd * (dxh - jnp.mean(dxh, axis=-1, keepdims=True) - xhat * jnp.mean(dxh * xhat, axis=-1, keepdims=True))
        dr_ref[...] = dr
        drb = _bf(dr)
        dys_ref[...] = _dot_nt(drb, w_ref[0:D_SSD, :])
        dya_ref[...] = _dot_nt(drb, w_ref[D_SSD:D_MIX, :])

    row = pl.BlockSpec((tm, D_MODEL), lambda i: (i, 0))
    vec = pl.BlockSpec((1, D_MODEL), lambda i: (0, 0))
    return pl.pallas_call(
        body, name="out_ln_loss", grid=(L // tm,),
        in_specs=[row, row, row, row, pl.BlockSpec((D_MIX, D_MODEL), lambda i: (0, 0), pipeline_mode=pl.Buffered(1)), vec, vec],
        out_specs=[row, row, row, pl.BlockSpec((1, 128), lambda i: (0, 0)), vec, vec],
        out_shape=[jax.ShapeDtypeStruct((L, D_MODEL), F32)] * 3 + [jax.ShapeDtypeStruct((1, 128), F32)]
        + [jax.ShapeDtypeStruct((1, D_MODEL), F32)] * 2,
        compiler_params=_cparams(("arbitrary",)),
    )(y_ssd, y_att, x, target, w_out, ln_g, ln_b)


def _local_step(x, pos, target, w, w_out, conv_w, conv_b, dt_bias, a_log, d_skip, norm_w, sinks, ln_g, ln_b):
    inv8 = ROPE_THETA ** (-jnp.arange(0, ROPE_DIM, 2, dtype=F32) / ROPE_DIM)
    inv = jnp.tile(jnp.concatenate([inv8, inv8, jnp.zeros((ATT_HD - ROPE_DIM,), F32)]), 2).reshape(1, 2 * ATT_HD)

    z, g, q, xbc, kv, dtp, xb = _in_proj(x, w, pos, inv)
    y_ssd, y_pre, prev = _ssd_fwd2(z, xbc, dtp, conv_w, conv_b, dt_bias, a_log, d_skip, norm_w)
    y_att, o, lse = _swa_fwd(q, g, kv, sinks)
    dr, dy_ssd, dy_att, loss, g_ln_g, g_ln_b = _out_ln_loss(y_ssd, y_att, x, target, w_out, ln_g, ln_b)
    gw_out_ssd = _matmul_tn(y_ssd, dr, 1024, "gw_out_ssd")
    gw_out_att = _matmul_tn(y_att, dr, 1024, "gw_out_att")
    dq, dg, dkv, g_sinks = _swa_bwd(dy_att, q, g, kv, o, lse, pos, inv, sinks)
    dz, dxbc, ddt, g_conv_w, g_conv_b, g_dt_bias, g_a_log, g_d_skip, g_norm_w = _ssd_bwd2(
        dy_ssd, z, y_pre, xbc, dtp, prev, conv_w, conv_b, dt_bias, a_log, d_skip, norm_w)
    grad_x = _grad_x(dr, dz, dg, dq, dxbc, dkv, ddt, w)
    gw_z = _matmul_tn(dz, xb, 1024, "gw_z")
    gw_g = _matmul_tn(dg, xb, 1024, "gw_g")
    gw_q = _matmul_tn(dq, xb, 1024, "gw_q")
    gw_xbc = _matmul_tn(dxbc, xb, 1024, "gw_xbc")
    gw_kv = _matmul_tn(dkv, xb, 1024, "gw_kv")
    gw_dt = _matmul_tn(ddt, xb, 1024, "gw_dt")
    gw_in = jnp.concatenate([gw_z, gw_xbc, gw_dt[0:SSD_HEADS], gw_q, gw_kv, gw_g], axis=0)
    gw_out = jnp.concatenate([gw_out_ssd, gw_out_att], axis=0)
    small = dict(conv_w=g_conv_w, conv_b=g_conv_b, dt_bias=g_dt_bias, a_log=g_a_log, d_skip=g_d_skip,
                 ssd_norm_w=g_norm_w, attn_sinks=g_sinks, ln_g=g_ln_g, ln_b=g_ln_b)
    return loss, grad_x, gw_in, gw_out, small


def _mesh_pos():
    return lax.axis_index("x"), lax.axis_index("y"), lax.axis_index("c")


def _gather_weights(w_in_s, w_out_s, conv_w_s):
    def body(win_ref, wout_ref, cw_ref, owin_ref, owout_ref, ocw_ref, send_sems, recv_sems, small_send, small_recv,
             local_sems):
        x, y, c = _mesh_pos()
        me = 2 * x + y
        sibling = (x, y, 1 - c)
        chips = [(1 - x, y), (x, 1 - y), (1 - x, 1 - y)]
        locals_ = [pltpu.make_async_copy(cw_ref, ocw_ref.at[me], local_sems.at[0])]
        for cp in locals_:
            cp.start()
        started = []
        for t, (src, dst) in enumerate(((win_ref, owin_ref), (wout_ref, owout_ref))):
            hr = src.shape[0] // 2

            def half(ref, hc, hr=hr):
                return ref.at[pl.ds(hc * hr, hr), :]

            for j, (px, py) in enumerate(chips):
                cp = pltpu.make_async_remote_copy(
                    src_ref=half(src, c), dst_ref=half(dst.at[me], c), send_sem=send_sems.at[t, j],
                    recv_sem=recv_sems.at[t, j], device_id=(px, py, c), device_id_type=MESH)
                cp.start()
                started.append(cp)
        for j, (px, py) in enumerate(chips):
            cp = pltpu.make_async_remote_copy(
                src_ref=cw_ref, dst_ref=ocw_ref.at[me], send_sem=small_send.at[j], recv_sem=small_recv.at[j],
                device_id=(px, py, c), device_id_type=MESH)
            cp.start()
            started.append(cp)
        for t, (src, dst) in enumerate(((win_ref, owin_ref), (wout_ref, owout_ref))):
            hr = src.shape[0] // 2
            for j, (px, py) in enumerate(chips):
                src_chip = 2 * px + py
                blk = dst.at[src_chip].at[pl.ds(c * hr, hr), :]
                pltpu.make_async_remote_copy(
                    src_ref=blk, dst_ref=blk, send_sem=send_sems.at[t, j], recv_sem=recv_sems.at[t, j],
                    device_id=(px, py, c), device_id_type=MESH).wait_recv()
                cp = pltpu.make_async_remote_copy(
                    src_ref=blk, dst_ref=blk, send_sem=send_sems.at[t, 3 + j], recv_sem=recv_sems.at[t, 3 + j],
                    device_id=sibling, device_id_type=MESH)
                cp.start()
                started.append(cp)
        for t, (src, dst) in enumerate(((win_ref, owin_ref), (wout_ref, owout_ref))):
            hr = src.shape[0] // 2
            for j, (px, py) in enumerate(chips):
                src_chip = 2 * px + py
                blk = dst.at[src_chip].at[pl.ds((1 - c) * hr, hr), :]
                pltpu.make_async_remote_copy(
                    src_ref=blk, dst_ref=blk, send_sem=send_sems.at[t, 3 + j], recv_sem=recv_sems.at[t, 3 + j],
                    device_id=sibling, device_id_type=MESH).wait_recv()
        for j in range(3):
            pltpu.make_async_remote_copy(
                src_ref=cw_ref, dst_ref=ocw_ref.at[me], send_sem=small_send.at[j], recv_sem=small_recv.at[j],
                device_id=sibling, device_id_type=MESH).wait_recv()
        for cp in started:
            cp.wait_send()
        for cp in locals_:
            cp.wait()

    any_spec = pl.BlockSpec(memory_space=pl.ANY)
    return pl.pallas_call(
        body, name="gather_weights",
        in_specs=[any_spec] * 3, out_specs=[any_spec] * 3,
        out_shape=[jax.ShapeDtypeStruct((N_CHIPS,) + a.shape, a.dtype) for a in (w_in_s, w_out_s, conv_w_s)],
        scratch_shapes=[pltpu.SemaphoreType.DMA((2, 6)), pltpu.SemaphoreType.DMA((2, 6)),
                        pltpu.SemaphoreType.DMA((3,)), pltpu.SemaphoreType.DMA((3,)), pltpu.SemaphoreType.DMA((3,))],
    )(w_in_s, w_out_s, conv_w_s)


def _pair_exchange(gw_in, gw_out, small):
    k_small = small.shape[1]

    def body(gin_ref, gout_ref, sm_ref, rin_ref, rout_ref, slots_ref, send_sems, recv_sems, small_send, small_recv,
             local_sem):
        x, y, c = _mesh_pos()
        me = 4 * x + 2 * y + c
        sibling = (x, y, 1 - c)
        mine = pltpu.make_async_copy(sm_ref, slots_ref.at[me], local_sem)
        mine.start()
        started = []
        for t, (src, dst) in enumerate(((gin_ref, rin_ref), (gout_ref, rout_ref))):
            hr = src.shape[1] // 2
            for j in range(N_CHIPS):
                cp = pltpu.make_async_remote_copy(
                    src_ref=src.at[j, pl.ds((1 - c) * hr, hr), :], dst_ref=dst.at[j], send_sem=send_sems.at[t, j],
                    recv_sem=recv_sems.at[t, j], device_id=sibling, device_id_type=MESH)
                cp.start()
                started.append(cp)
        for k in range(1, 8):
            peer = (x ^ ((k >> 2) & 1), y ^ ((k >> 1) & 1), c ^ (k & 1))
            cp = pltpu.make_async_remote_copy(
                src_ref=sm_ref, dst_ref=slots_ref.at[me], send_sem=small_send.at[k - 1], recv_sem=small_recv.at[k - 1],
                device_id=peer, device_id_type=MESH)
            cp.start()
            started.append(cp)
        for t, (src, dst) in enumerate(((gin_ref, rin_ref), (gout_ref, rout_ref))):
            for j in range(N_CHIPS):
                pltpu.make_async_remote_copy(
                    src_ref=dst.at[j], dst_ref=dst.at[j], send_sem=send_sems.at[t, j], recv_sem=recv_sems.at[t, j],
                    device_id=sibling, device_id_type=MESH).wait_recv()
        for k in range(1, 8):
            pltpu.make_async_remote_copy(
                src_ref=sm_ref, dst_ref=slots_ref.at[me], send_sem=small_send.at[k - 1], recv_sem=small_recv.at[k - 1],
                device_id=sibling, device_id_type=MESH).wait_recv()
        for cp in started:
            cp.wait_send()
        mine.wait()

    any_spec = pl.BlockSpec(memory_space=pl.ANY)
    half_in = jax.ShapeDtypeStruct((N_CHIPS, gw_in.shape[1] // 2, D_MODEL), F32)
    half_out = jax.ShapeDtypeStruct((N_CHIPS, gw_out.shape[1] // 2, D_MODEL), F32)
    return pl.pallas_call(
        body, name="pair_exchange",
        in_specs=[any_spec] * 3, out_specs=[any_spec] * 3,
        out_shape=[half_in, half_out, jax.ShapeDtypeStruct((8, 8, k_small), F32)],
        scratch_shapes=[pltpu.SemaphoreType.DMA((2, N_CHIPS)), pltpu.SemaphoreType.DMA((2, N_CHIPS)),
                        pltpu.SemaphoreType.DMA((7,)), pltpu.SemaphoreType.DMA((7,)), pltpu.SemaphoreType.DMA],
    )(gw_in, gw_out, small)


def _chip_exchange(s_in, s_out):
    def body(sin_ref, sout_ref, rin_ref, rout_ref, send_sems, recv_sems):
        x, y, c = _mesh_pos()
        me = 2 * x + y
        chips = [(1 - x, y), (x, 1 - y), (1 - x, 1 - y)]
        started = []
        for t, (src, dst) in enumerate(((sin_ref, rin_ref), (sout_ref, rout_ref))):
            for j, (px, py) in enumerate(chips):
                cp = pltpu.make_async_remote_copy(
                    src_ref=src.at[2 * px + py], dst_ref=dst.at[me], send_sem=send_sems.at[t, j],
                    recv_sem=recv_sems.at[t, j], device_id=(px, py, c), device_id_type=MESH)
                cp.start()
                started.append(cp)
        for t, (src, dst) in enumerate(((sin_ref, rin_ref), (sout_ref, rout_ref))):
            for j, (px, py) in enumerate(chips):
                blk = dst.at[2 * px + py]
                pltpu.make_async_remote_copy(
                    src_ref=blk, dst_ref=blk, send_sem=send_sems.at[t, j], recv_sem=recv_sems.at[t, j],
                    device_id=(px, py, c), device_id_type=MESH).wait_recv()
        for cp in started:
            cp.wait_send()

    any_spec = pl.BlockSpec(memory_space=pl.ANY)
    return pl.pallas_call(
        body, name="chip_exchange",
        in_specs=[any_spec] * 2, out_specs=[any_spec] * 2,
        out_shape=[jax.ShapeDtypeStruct(s_in.shape, s_in.dtype), jax.ShapeDtypeStruct(s_out.shape, s_out.dtype)],
        scratch_shapes=[pltpu.SemaphoreType.DMA((2, 3)), pltpu.SemaphoreType.DMA((2, 3))],
    )(s_in, s_out)


def _pair_share(h_in, h_out):
    def body(hin_ref, hout_ref, rin_ref, rout_ref, send_sems, recv_sems):
        x, y, c = _mesh_pos()
        sibling = (x, y, 1 - c)
        started = []
        for t, (src, dst) in enumerate(((hin_ref, rin_ref), (hout_ref, rout_ref))):
            cp = pltpu.make_async_remote_copy(
                src_ref=src, dst_ref=dst, send_sem=send_sems.at[t], recv_sem=recv_sems.at[t],
                device_id=sibling, device_id_type=MESH)
            cp.start()
            started.append(cp)
        for cp in started:
            cp.wait()

    any_spec = pl.BlockSpec(memory_space=pl.ANY)
    return pl.pallas_call(
        body, name="pair_share",
        in_specs=[any_spec] * 2, out_specs=[any_spec] * 2,
        out_shape=[jax.ShapeDtypeStruct(h_in.shape, F32), jax.ShapeDtypeStruct(h_out.shape, F32)],
        scratch_shapes=[pltpu.SemaphoreType.DMA((2,)), pltpu.SemaphoreType.DMA((2,))],
    )(h_in, h_out)


def _pair_add(g, recv, core, name):
    _, rows, C = recv.shape
    tc = 256

    def body(core_ref, g_ref, r_ref, o_ref):
        o_ref[...] = _bf(g_ref[...] + r_ref[...])

    spec = pl.BlockSpec((1, rows, tc), lambda j, i, core: (j, 0, i))
    return pl.pallas_call(
        body, name=name,
        grid_spec=pltpu.PrefetchScalarGridSpec(
            num_scalar_prefetch=1, grid=(N_CHIPS, C // tc),
            in_specs=[pl.BlockSpec((1, rows, tc), lambda j, i, core: (j, core[0], i)), spec], out_specs=spec),
        out_shape=jax.ShapeDtypeStruct((N_CHIPS, rows, C), BF16),
        compiler_params=_cparams(("parallel", "parallel")),
    )(core, g, recv)


def _chip_add(own, parts, chip, name):
    _, rows, C = parts.shape
    tc = 256

    def body(chip_ref, own_ref, r0, r1, r2, r3, o_ref):
        acc = None
        for j, r in enumerate((r0, r1, r2, r3)):
            term = jnp.where(chip_ref[0] == j, own_ref[0], r[0]).astype(F32)
            acc = term if acc is None else acc + term
        o_ref[...] = acc

    def slab(j):
        return pl.BlockSpec((1, rows, tc), lambda i, chip: (jnp.where(chip[0] == j, (j + 1) % N_CHIPS, j), 0, i))

    return pl.pallas_call(
        body, name=name,
        grid_spec=pltpu.PrefetchScalarGridSpec(
            num_scalar_prefetch=1, grid=(C // tc,),
            in_specs=[pl.BlockSpec((1, rows, tc), lambda i, chip: (chip[0], 0, i))] + [slab(j) for j in range(N_CHIPS)],
            out_specs=pl.BlockSpec((rows, tc), lambda i, chip: (0, i))),
        out_shape=jax.ShapeDtypeStruct((rows, C), F32),
        compiler_params=_cparams(("parallel",)),
    )(chip, own, parts, parts, parts, parts)


def _adamw_math(w, g, m, v):
    m = ADAM_B1 * m + (1.0 - ADAM_B1) * g
    v = ADAM_B2 * v + (1.0 - ADAM_B2) * (g * g)
    m_hat = m / (1.0 - ADAM_B1 ** ADAM_STEP)
    v_hat = v / (1.0 - ADAM_B2 ** ADAM_STEP)
    delta = -ADAM_LR * (m_hat / (jnp.sqrt(v_hat) + ADAM_EPS) + ADAM_WD * w)
    return delta, m, v


def _adamw_pair(w, g_own, g_sib, m, v, core, name):
    unit = w.ndim == 3
    R, C = w.shape[0], w.shape[-1]
    rows = g_own.shape[0]
    tc = 128

    def body(core_ref, w_ref, go_ref, gs_ref, m_ref, v_ref, d_ref, nm_ref, nv_ref, g_ref):
        first = core_ref[0] == 0
        own, sib = go_ref[...], gs_ref[...]
        g = jnp.concatenate([jnp.where(first, own, sib), jnp.where(first, sib, own)], axis=0)[0:R, :]
        idx = (slice(None), 0, slice(None)) if unit else (slice(None), slice(None))
        d, nm, nv = _adamw_math(w_ref[idx], g, m_ref[idx], v_ref[idx])
        d_ref[idx] = d
        nm_ref[idx] = nm
        nv_ref[idx] = nv
        g_ref[idx] = g

    if unit:
        spec = pl.BlockSpec((R, 1, tc), lambda i, core: (0, 0, i))
    else:
        spec = pl.BlockSpec((R, tc), lambda i, core: (0, i))
    gspec = pl.BlockSpec((rows, tc), lambda i, core: (0, i))
    return pl.pallas_call(
        body, name=name,
        grid_spec=pltpu.PrefetchScalarGridSpec(
            num_scalar_prefetch=1, grid=(C // tc,),
            in_specs=[spec, gspec, gspec, spec, spec], out_specs=[spec] * 4),
        out_shape=[jax.ShapeDtypeStruct(w.shape, F32)] * 4,
        compiler_params=_cparams(("parallel",)),
    )(core, w, g_own, g_sib, m, v)


SMALL_NAMES = ("conv_b", "ssd_norm_w", "ln_g", "ln_b", "dt_bias", "a_log", "d_skip", "attn_sinks")
SMALL_SIZES = (D_XBC, D_SSD, D_MODEL, D_MODEL, SSD_HEADS, SSD_HEADS, SSD_HEADS, ATT_QH)
SMALL_OFFS = tuple(D_XBC + sum(-(-n // 128) * 128 for n in SMALL_SIZES[:k]) for k in range(len(SMALL_SIZES)))
LOSS_OFF = D_XBC + sum(-(-n // 128) * 128 for n in SMALL_SIZES)
K_SMALL = LOSS_OFF + 128


def _pack_small(g_conv_w, vecs, loss):
    def body(cw_ref, *refs):
        o_ref = refs[-1]
        o_ref[...] = jnp.zeros_like(o_ref)
        o_ref[0:CONV_K, 0:D_XBC] = cw_ref[...]
        for v_ref, off, n in zip(refs[:-2], SMALL_OFFS, SMALL_SIZES):
            o_ref[0:1, off:off + n] = v_ref[...]
        o_ref[0:1, LOSS_OFF:LOSS_OFF + 128] = refs[-2][...]

    return pl.pallas_call(
        body, name="pack_small", out_shape=jax.ShapeDtypeStruct((8, K_SMALL), F32), compiler_params=_cparams(),
    )(g_conv_w, *vecs, loss)


def _adamw_small(slots, chip, conv_w, m_conv_w, v_conv_w, params, moms, vars_):
    n_vec = len(SMALL_NAMES)

    def body(chip_ref, s_ref, *refs):
        ins = refs[:3 * (n_vec + 1)]
        outs = refs[3 * (n_vec + 1):-1]
        tot_ref = refs[-1]
        tot = s_ref[0]
        for d in range(1, 8):
            tot = tot + s_ref[d]
        outs[0][...] = tot[0:1, LOSS_OFF:LOSS_OFF + 1]
        off = pl.multiple_of(chip_ref[0] * CONV_COLS, 128)
        tot_ref[...] = tot
        grads = [tot_ref[0:CONV_K, pl.ds(off, CONV_COLS)]]
        grads += [tot[0:1, o:o + n] for o, n in zip(SMALL_OFFS, SMALL_SIZES)]
        for k, g in enumerate(grads):
            w_ref, m_ref, v_ref = ins[3 * k:3 * k + 3]
            full = (0,) if k == 0 else (Ellipsis,)
            d, nm, nv = _adamw_math(w_ref[full], g, m_ref[full], v_ref[full])
            for o_ref, val in zip(outs[1 + 4 * k:5 + 4 * k], (g, d, nm, nv)):
                o_ref[full] = val

    args = [conv_w, m_conv_w, v_conv_w]
    for w, m, v in zip(params, moms, vars_):
        args += [w, m, v]
    shapes = [jax.ShapeDtypeStruct((1, 1), F32)] + [jax.ShapeDtypeStruct(conv_w.shape, F32)] * 4
    for w in params:
        shapes += [jax.ShapeDtypeStruct(w.shape, F32)] * 4
    vmem = pl.BlockSpec(memory_space=pltpu.VMEM)
    return pl.pallas_call(
        body, name="adamw_small",
        grid_spec=pltpu.PrefetchScalarGridSpec(
            num_scalar_prefetch=1, grid=(1,),
            in_specs=[pl.BlockSpec(slots.shape, lambda i, chip: (0, 0, 0))] + [vmem] * len(args),
            out_specs=[vmem] * len(shapes), scratch_shapes=[pltpu.VMEM((8, K_SMALL), F32)]),
        out_shape=shapes, compiler_params=_cparams(),
    )(chip, slots, *args)


def kernel(x, positions, w_in, conv_w, conv_b, dt_bias, a_log, d_skip, ssd_norm_w, attn_sinks, w_out, ln_g, ln_b, loss_target, m_w_in, m_conv_w, m_conv_b, m_dt_bias, m_a_log, m_d_skip, m_ssd_norm_w, m_attn_sinks, m_w_out, m_ln_g, m_ln_b, v_w_in, v_conv_w, v_conv_b, v_dt_bias, v_a_log, v_d_skip, v_ssd_norm_w, v_attn_sinks, v_w_out, v_ln_g, v_ln_b):
    mx, my, mc = _mesh_pos()
    chip = 2 * mx + my
    L = x.shape[1]

    conv_w_s8 = jnp.pad(conv_w[0], ((0, 8 - CONV_K), (0, 0)))
    pad_rows = ((0, SLAB_ROWS - W_IN_COLS), (0, 0))
    w_in_t = w_in[0].T
    w_in_b, w_out_b = jnp.pad(_bf(w_in_t), pad_rows), _bf(w_out[0])
    ag_in, ag_out, ag_cw = _gather_weights(w_in_b, w_out_b, conv_w_s8)
    ag_in = jnp.where((jnp.arange(N_CHIPS) == chip)[:, None, None], w_in_b[None], ag_in)
    ag_out = jnp.where((jnp.arange(N_CHIPS) == chip)[:, None, None], w_out_b[None], ag_out)
    w_full = jnp.concatenate([ag_in[j, 0:W_IN_COLS] for j in range(N_CHIPS)], axis=0)
    w = jnp.concatenate([
        w_full[O_Z:O_Z + D_SSD], w_full[O_G:O_G + D_ATT], w_full[O_Q:O_Q + D_ATT],
        w_full[O_XBC:O_XBC + D_XBC], w_full[O_K:O_K + 2 * D_KV], w_full[O_DT:O_DT + SSD_HEADS],
        jnp.zeros((DT_PAD - SSD_HEADS, D_MODEL), BF16)], axis=0)
    w_out_full = ag_out.reshape(D_MIX, D_MODEL)
    conv_w_full = jnp.concatenate([ag_cw[j, 0:CONV_K] for j in range(N_CHIPS)], axis=1)

    loss_part, grad_x, gw_in, gw_out, small = _local_step(
        x[0], positions[0].reshape(L, 1), loss_target[0], w, w_out_full, conv_w_full, conv_b, dt_bias, a_log, d_skip,
        ssd_norm_w, attn_sinks, ln_g, ln_b)

    packed = _pack_small(small["conv_w"], [small[n] for n in SMALL_NAMES], loss_part)

    gw_in_slabs = jnp.stack([jnp.pad(gw_in[W_IN_COLS * j:W_IN_COLS * (j + 1)], pad_rows) for j in range(N_CHIPS)])
    gw_out_slabs = gw_out.reshape(N_CHIPS, W_OUT_ROWS, D_MODEL)
    core_id = mc.reshape(1).astype(jnp.int32)
    chip_id = chip.reshape(1).astype(jnp.int32)
    recv_in, recv_out, slots = _pair_exchange(gw_in_slabs, gw_out_slabs, packed)
    s_in = _pair_add(gw_in_slabs, recv_in, core_id, "pair_add_in")
    s_out = _pair_add(gw_out_slabs, recv_out, core_id, "pair_add_out")
    r_in, r_out = _chip_exchange(s_in, s_out)
    h_in = _chip_add(s_in, r_in, chip_id, "chip_add_in")
    h_out = _chip_add(s_out, r_out, chip_id, "chip_add_out")
    sib_in, sib_out = _pair_share(h_in, h_out)

    to_rows = lambda a: jnp.transpose(a, (2, 0, 1))
    in_t = _adamw_pair(to_rows(w_in), h_in, sib_in, to_rows(m_w_in), to_rows(v_w_in), core_id, "adamw_w_in")
    d_w_in, nm_w_in, nv_w_in, g_w_in = [jnp.transpose(a, (1, 2, 0)) for a in in_t]
    out_t = _adamw_pair(w_out[0], h_out, sib_out, m_w_out[0], v_w_out[0], core_id, "adamw_w_out")
    d_w_out, nm_w_out, nv_w_out, g_w_out = [a[None] for a in out_t]

    params = dict(conv_b=conv_b, ssd_norm_w=ssd_norm_w, ln_g=ln_g, ln_b=ln_b, dt_bias=dt_bias, a_log=a_log,
                  d_skip=d_skip, attn_sinks=attn_sinks)
    moms = dict(conv_b=m_conv_b, ssd_norm_w=m_ssd_norm_w, ln_g=m_ln_g, ln_b=m_ln_b, dt_bias=m_dt_bias, a_log=m_a_log,
                d_skip=m_d_skip, attn_sinks=m_attn_sinks)
    vars_ = dict(conv_b=v_conv_b, ssd_norm_w=v_ssd_norm_w, ln_g=v_ln_g, ln_b=v_ln_b, dt_bias=v_dt_bias, a_log=v_a_log,
                 d_skip=v_d_skip, attn_sinks=v_attn_sinks)
    res = _adamw_small(slots, chip_id, conv_w, m_conv_w, v_conv_w, [params[n] for n in SMALL_NAMES],
                       [moms[n] for n in SMALL_NAMES], [vars_[n] for n in SMALL_NAMES])
    loss = res[0][0, 0]
    grads, delta, new_m, new_v = {}, {}, {}, {}
    for k, n in enumerate(("conv_w",) + SMALL_NAMES):
        grads[n], delta[n], new_m[n], new_v[n] = res[1 + 4 * k:5 + 4 * k]
    for dd, a_in, a_out in ((grads, g_w_in, g_w_out), (delta, d_w_in, d_w_out), (new_m, nm_w_in, nm_w_out),
                            (new_v, nv_w_in, nv_w_out)):
        dd["w_in"] = a_in
        dd["w_out"] = a_out
    order = ("w_in", "conv_w", "conv_b", "dt_bias", "a_log", "d_skip", "ssd_norm_w", "attn_sinks", "w_out", "ln_g", "ln_b")
    return (loss, grad_x[None], *[grads[n] for n in order], *[delta[n] for n in order], *[new_m[n] for n in order],
            *[new_v[n] for n in order])
```

```python
import functools

import numpy as np
import jax
import jax.numpy as jnp
from jax import lax
from jax.experimental import pallas as pl
from jax.experimental.pallas import tpu as pltpu

F32 = jnp.float32
BF16 = jnp.bfloat16
MESH = pl.DeviceIdType.MESH

D_MODEL = 1024
D_SSD = 1024
D_ATT = 1024
D_MIX = 2048
SSD_HEADS = 16
SSD_P = 64
SSD_GROUPS = 2
SSD_R = 8
SSD_N = 128
D_BC = 256
D_XBC = 1536
CONV_K = 4
CHUNK = 128
ATT_HD = 64
ATT_QH = 16
ATT_KVH = 4
ATT_R = 4
D_KV = 256
WINDOW = 128
ROPE_THETA = 500000.0
ROPE_DIM = 16
ALPHA = 2.0 ** 0.25
LN_EPS = 1e-5
RMS_EPS = 1e-5
D_IN_PROJ = 5136
O_Z, O_XBC, O_DT, O_Q, O_K, O_V, O_G = 0, 1024, 2560, 2576, 3600, 3856, 4112
P_Z, P_G, P_Q, P_XBC, P_KV, P_DT, P_END = 0, 1024, 2048, 3072, 4608, 5120, 5248
DT_PAD = 128
N_CHIPS = 4
W_IN_COLS = D_IN_PROJ // N_CHIPS
SLAB_ROWS = 1312
W_OUT_ROWS = D_MIX // N_CHIPS
CONV_COLS = D_XBC // N_CHIPS

ADAM_LR = 0.001
ADAM_B1 = 0.9
ADAM_B2 = 0.999
ADAM_EPS = 1e-08
ADAM_WD = 0.01
ADAM_STEP = 10

VMEM_LIMIT = 56 * 1024 * 1024
ROW_TILE = 512
NEG_BIG = -1e30
HI = lax.Precision.HIGHEST


def _cparams(sem=None, **kw):
    if sem is not None:
        kw["dimension_semantics"] = sem
    return pltpu.CompilerParams(vmem_limit_bytes=VMEM_LIMIT, **kw)


def _dot(a, b):
    return jnp.dot(a, b, preferred_element_type=F32)


def _dot_nt(a, b):
    return lax.dot_general(a, b, (((1,), (1,)), ((), ())), preferred_element_type=F32)


def _dot_tn(a, b):
    return lax.dot_general(a, b, (((0,), (0,)), ((), ())), preferred_element_type=F32)


def _bf(a):
    return a.astype(BF16)


def _iota2(shape, dim):
    return lax.broadcasted_iota(jnp.int32, shape, dim)


def _to_rows(col):
    k = col.shape[1]
    eye = (_iota2((k, k), 0) == _iota2((k, k), 1)).astype(F32)
    return lax.dot_general(eye, col, (((1,), (1,)), ((), ())), preferred_element_type=F32, precision=HI)


def _to_cols(row):
    n = row.shape[1]
    eye = (_iota2((n, n), 0) == _iota2((n, n), 1)).astype(F32)
    return lax.dot_general(eye, row, (((1,), (1,)), ((), ())), preferred_element_type=F32, precision=HI)


def _sigmoid(x):
    return jax.nn.sigmoid(x)


def _in_proj(x, w, pos, inv):
    L = x.shape[0]
    tm = ROW_TILE
    widths = (D_SSD, D_ATT, D_ATT, D_XBC, 2 * D_KV, DT_PAD)

    def body(x_ref, w_ref, pos_ref, inv_ref, z_ref, g_ref, q_ref, xbc_ref, kv_ref, dt_ref, xb_ref):
        xb = _bf(x_ref[...])
        xb_ref[...] = xb
        for o_ref, off, wd in zip((z_ref, g_ref, xbc_ref, dt_ref), (P_Z, P_G, P_XBC, P_DT), (D_SSD, D_ATT, D_XBC, DT_PAD)):
            o_ref[...] = _dot_nt(xb, w_ref[off:off + wd, :])
        tabs = _rope_tables(pos_ref, inv_ref)
        q_ref[...] = _bf(_rope(_dot_nt(xb, w_ref[P_Q:P_Q + D_ATT, :]), tabs))
        kv_ref[:, 0:D_KV] = _bf(_rope(_dot_nt(xb, w_ref[P_KV:P_KV + D_KV, :]), tabs))
        kv_ref[:, D_KV:2 * D_KV] = _bf(_dot_nt(xb, w_ref[P_KV + D_KV:P_KV + 2 * D_KV, :]))

    row = lambda wd: pl.BlockSpec((tm, wd), lambda i: (i, 0))
    return pl.pallas_call(
        body, name="in_proj", grid=(L // tm,),
        in_specs=[row(D_MODEL), pl.BlockSpec((P_END, D_MODEL), lambda i: (0, 0), pipeline_mode=pl.Buffered(1)), row(1),
                  pl.BlockSpec((1, 2 * ATT_HD), lambda i: (0, 0))],
        out_specs=[row(wd) for wd in widths] + [row(D_MODEL)],
        out_shape=[jax.ShapeDtypeStruct((L, wd), dt) for wd, dt in zip(widths, (F32, F32, BF16, F32, BF16, F32))]
        + [jax.ShapeDtypeStruct((L, D_MODEL), BF16)],
        compiler_params=_cparams(("parallel",)),
    )(x, w, pos, inv)


def _matmuls_tn(a_list, b, name):
    K, N = b.shape
    tk = min(K, 1024)
    n = len(a_list)

    def body(*refs):
        b_ref = refs[n]
        k = pl.program_id(0)
        bb = _bf(b_ref[...])
        for a_ref, o_ref in zip(refs[:n], refs[n + 1:]):
            part = _dot_tn(_bf(a_ref[...]), bb)

            @pl.when(k == 0)
            def _():
                o_ref[...] = part

            @pl.when(k > 0)
            def _():
                o_ref[...] += part

    return pl.pallas_call(
        body, name=name, grid=(K // tk,),
        in_specs=[pl.BlockSpec((tk, a.shape[1]), lambda k: (k, 0)) for a in a_list] + [pl.BlockSpec((tk, N), lambda k: (k, 0))],
        out_specs=[pl.BlockSpec((a.shape[1], N), lambda k: (0, 0)) for a in a_list],
        out_shape=[jax.ShapeDtypeStruct((a.shape[1], N), F32) for a in a_list],
        compiler_params=_cparams(("arbitrary",)),
    )(*a_list, b)


def _grad_x(dr, dz, dg, dq, dxbc, dkv, ddt, w):
    L = dr.shape[0]
    tm = ROW_TILE
    widths = (D_SSD, D_ATT, D_ATT, D_XBC, 2 * D_KV, DT_PAD)
    offs = (P_Z, P_G, P_Q, P_XBC, P_KV, P_DT)

    def body(dr_ref, dz_ref, dg_ref, dq_ref, dxbc_ref, dkv_ref, ddt_ref, w_ref, o_ref):
        acc = ALPHA * dr_ref[...]
        for p_ref, off, wd in zip((dz_ref, dg_ref, dq_ref, dxbc_ref, dkv_ref, ddt_ref), offs, widths):
            acc = acc + _dot(_bf(p_ref[...]), w_ref[off:off + wd, :])
        o_ref[...] = acc

    row = lambda wd: pl.BlockSpec((tm, wd), lambda i: (i, 0))
    return pl.pallas_call(
        body, name="grad_x", grid=(L // tm,),
        in_specs=[row(D_MODEL)] + [row(wd) for wd in widths] + [pl.BlockSpec((P_END, D_MODEL), lambda i: (0, 0), pipeline_mode=pl.Buffered(1))],
        out_specs=row(D_MODEL),
        out_shape=jax.ShapeDtypeStruct((L, D_MODEL), F32),
        compiler_params=_cparams(("parallel",)),
    )(dr, dz, dg, dq, dxbc, dkv, ddt, w)


def _ssd_chunk_pre(first, xbc_ref, tail_ref, dt_ref, cw_ref, cb_ref, dtb_ref, alog_ref, ext):
    tail = jnp.where(first, 0.0, tail_ref[...])
    ext[0:8, :] = tail
    ext[8:8 + CHUNK, :] = xbc_ref[...]
    u = cb_ref[...] + cw_ref[0:1, :] * ext[pl.ds(5, CHUNK), :]
    for k in range(1, CONV_K):
        u = u + cw_ref[k:k + 1, :] * ext[pl.ds(5 + k, CHUNK), :]
    sig = _sigmoid(u)
    xbc = u * sig
    dtraw = dt_ref[:, 0:SSD_HEADS] + dtb_ref[...]
    dt = jax.nn.softplus(dtraw)
    A = -jnp.exp(alog_ref[...])
    a = dt * A
    tril = (_iota2((CHUNK, CHUNK), 0) >= _iota2((CHUNK, CHUNK), 1)).astype(F32)
    acs = jnp.dot(tril, a, preferred_element_type=F32, precision=HI)
    acs_row = _to_rows(acs)
    return u, sig, xbc, dtraw, dt, A, acs, acs_row


HALO = 16


def _shift_matrix(offsets):
    n = CHUNK + HALO
    m = np.zeros((len(offsets) * CHUNK, 2 * n), np.float32)
    for k, off in enumerate(offsets):
        t = np.arange(CHUNK)
        m[k * CHUNK + t, t + off] = 1.0
        m[k * CHUNK + t, n + t + off] = 1.0
    return jnp.asarray(m, BF16)


def _shifted_rows(first_part, second_part, smat_ref):
    h1, l1 = _hi_lo(first_part)
    h2, l2 = _hi_lo(second_part)
    sh = _dot(smat_ref[...], jnp.concatenate([h1, h2, l1, l2], axis=0))
    return sh[0:CHUNK], sh[CHUNK:2 * CHUNK], sh[2 * CHUNK:3 * CHUNK]


def _ssd_chunk_pre2(first, xbc_ref, tail_ref, dt_ref, cw_ref, cb_ref, dtb_ref, alog_ref, smat_ref):
    tail = jnp.where(first, 0.0, tail_ref[...])
    x = xbc_ref[...]
    taps = _shifted_rows(tail, x, smat_ref) + (x,)
    u = cb_ref[...] + cw_ref[0:1, :] * taps[0]
    for k in range(1, CONV_K):
        u = u + cw_ref[k:k + 1, :] * taps[k]
    sig = _sigmoid(u)
    xbc = u * sig
    dtraw = dt_ref[:, 0:SSD_HEADS] + dtb_ref[...]
    dt = jax.nn.softplus(dtraw)
    A = -jnp.exp(alog_ref[...])
    a = dt * A
    tril = (_iota2((CHUNK, CHUNK), 0) >= _iota2((CHUNK, CHUNK), 1)).astype(F32)
    acs = jnp.dot(tril, a, preferred_element_type=F32, precision=HI)
    acs_row = _to_rows(acs)
    return u, sig, xbc, dtraw, dt, A, acs, acs_row, taps


def _ssd_fwd(z, xbc, dtp, conv_w, conv_b, dt_bias, a_log, d_skip, norm_w):
    L = z.shape[0]
    nc = L // CHUNK

    def body(z_ref, xbc_ref, tail_ref, dt_ref, cw_ref, cb_ref, dtb_ref, alog_ref, dsk_ref, nw_ref,
             y_ref, ypre_ref, prev_ref, state, ext, ybuf):
        c = pl.program_id(0)

        @pl.when(c == 0)
        def _():
            state[...] = jnp.zeros_like(state)

        u, sig, xbcv, dtraw, dt, A, acs, acs_row = _ssd_chunk_pre(
            c == 0, xbc_ref, tail_ref, dt_ref, cw_ref, cb_ref, dtb_ref, alog_ref, ext)
        prev_ref[0] = state[...]
        causal = _iota2((CHUNK, CHUNK), 0) >= _iota2((CHUNK, CHUNK), 1)
        alast = acs[CHUNK - 1:CHUNK, :]
        for g in range(SSD_GROUPS):
            Bg = _bf(xbcv[:, D_SSD + SSD_N * g:D_SSD + SSD_N * (g + 1)])
            Cg = _bf(xbcv[:, D_SSD + D_BC + SSD_N * g:D_SSD + D_BC + SSD_N * (g + 1)])
            cb = _dot_nt(Cg, Bg)
            for r in range(SSD_R):
                h = g * SSD_R + r
                hs = slice(SSD_P * h, SSD_P * (h + 1))
                acs_c = acs[:, h:h + 1]
                seg = acs_c - acs_row[h:h + 1, :]
                Lm = jnp.where(causal, jnp.exp(jnp.where(causal, seg, 0.0)), 0.0)
                M = cb * Lm
                xh = xbcv[:, hs]
                X = xh * dt[:, h:h + 1]
                prev_h = state[hs, :]
                ydiag = _dot(_bf(M), _bf(X))
                yoff = _dot_nt(Cg, _bf(prev_h)) * jnp.exp(acs_c)
                al = alast[:, h:h + 1]
                Xd = X * jnp.exp(al - acs_c)
                state[hs, :] = prev_h * jnp.exp(al) + _dot_tn(_bf(Xd), Bg)
                ybuf[:, hs] = ydiag + yoff + dsk_ref[:, h:h + 1] * xh
        y = ybuf[...]
        ypre_ref[...] = y
        zv = z_ref[...]
        yf = y * (zv * _sigmoid(zv))
        half = D_SSD // SSD_GROUPS
        for g in range(SSD_GROUPS):
            gs = slice(half * g, half * (g + 1))
            yg = yf[:, gs]
            ms = jnp.mean(yg * yg, axis=-1, keepdims=True)
            y_ref[:, gs] = _bf(yg * lax.rsqrt(ms + RMS_EPS) * nw_ref[:, gs])

    full = lambda shape: pl.BlockSpec(shape, lambda c: (0, 0))
    return pl.pallas_call(
        body, name="ssd_fwd", grid=(nc,),
        in_specs=[
            pl.BlockSpec((CHUNK, D_SSD), lambda c: (c, 0)),
            pl.BlockSpec((CHUNK, D_XBC), lambda c: (c, 0)),
            pl.BlockSpec((8, D_XBC), lambda c: (jnp.maximum(c * (CHUNK // 8) - 1, 0), 0)),
            pl.BlockSpec((CHUNK, DT_PAD), lambda c: (c, 0)),
            full((CONV_K, D_XBC)), full((1, D_XBC)), full((1, SSD_HEADS)), full((1, SSD_HEADS)), full((1, SSD_HEADS)),
            full((1, D_SSD)),
        ],
        out_specs=[
            pl.BlockSpec((CHUNK, D_SSD), lambda c: (c, 0)),
            pl.BlockSpec((CHUNK, D_SSD), lambda c: (c, 0)),
            pl.BlockSpec((1, SSD_HEADS * SSD_P, SSD_N), lambda c: (c, 0, 0)),
        ],
        out_shape=[
            jax.ShapeDtypeStruct((L, D_SSD), F32),
            jax.ShapeDtypeStruct((L, D_SSD), F32),
            jax.ShapeDtypeStruct((nc, SSD_HEADS * SSD_P, SSD_N), F32),
        ],
        scratch_shapes=[
            pltpu.VMEM((SSD_HEADS * SSD_P, SSD_N), F32),
            pltpu.VMEM((CHUNK + 8, D_XBC), F32),
            pltpu.VMEM((CHUNK, D_SSD), F32),
        ],
        compiler_params=_cparams(("arbitrary",)),
    )(z, xbc, xbc, dtp, conv_w, conv_b, dt_bias, a_log, d_skip, norm_w)


def _ssd_bwd(dy, z, ypre, xbc, dtp, prev, conv_w, conv_b, dt_bias, a_log, d_skip, norm_w):
    L = z.shape[0]
    nc = L // CHUNK

    def body(dy_ref, z_ref, ypre_ref, xbc_ref, tail_ref, dt_ref, prev_ref, cw_ref, cb_ref, dtb_ref, alog_ref, dsk_ref,
             nw_ref, dz_ref, dxbc_ref, ddt_ref, gcw_ref, gcb_ref, gdtb_ref, galog_ref, gdsk_ref, gnw_ref,
             dstate, dhead, ext, ext2, dpost):
        i = pl.program_id(0)
        c = nc - 1 - i

        @pl.when(i == 0)
        def _():
            dstate[...] = jnp.zeros_like(dstate)
            dhead[...] = jnp.zeros_like(dhead)
            gcw_ref[...] = jnp.zeros_like(gcw_ref)
            gcb_ref[...] = jnp.zeros_like(gcb_ref)
            gdtb_ref[...] = jnp.zeros_like(gdtb_ref)
            galog_ref[...] = jnp.zeros_like(galog_ref)
            gdsk_ref[...] = jnp.zeros_like(gdsk_ref)
            gnw_ref[...] = jnp.zeros_like(gnw_ref)

        u, sig, xbcv, dtraw, dt, A, acs, acs_row = _ssd_chunk_pre(
            c == 0, xbc_ref, tail_ref, dt_ref, cw_ref, cb_ref, dtb_ref, alog_ref, ext)

        zv = z_ref[...]
        ypre = ypre_ref[...]
        dyn = dy_ref[...]
        sz = _sigmoid(zv)
        silu_z = zv * sz
        yf = ypre * silu_z
        half = D_SSD // SSD_GROUPS
        dyf_parts = []
        for g in range(SSD_GROUPS):
            gs = slice(half * g, half * (g + 1))
            yg = yf[:, gs]
            rstd = lax.rsqrt(jnp.mean(yg * yg, axis=-1, keepdims=True) + RMS_EPS)
            dout = dyn[:, gs]
            gnw_ref[:, gs] += jnp.sum(dout * yg * rstd, axis=0, keepdims=True)
            dyhat = dout * nw_ref[:, gs]
            dyf_parts.append(rstd * (dyhat - yg * (rstd * rstd) * jnp.mean(dyhat * yg, axis=-1, keepdims=True)))
        dyf = jnp.concatenate(dyf_parts, axis=1)
        dz_ref[...] = _bf(dyf * ypre * (sz * (1.0 + zv * (1.0 - sz))))
        dypre = dyf * silu_z

        causal = _iota2((CHUNK, CHUNK), 0) >= _iota2((CHUNK, CHUNK), 1)
        alast = acs[CHUNK - 1:CHUNK, :]
        lane16 = _iota2((1, SSD_HEADS), 1)
        sub16 = _iota2((SSD_HEADS, 1), 0)
        dacs_col = jnp.zeros((CHUNK, SSD_HEADS), F32)
        dacs_row = jnp.zeros((SSD_HEADS, CHUNK), F32)
        ddt_col = jnp.zeros((CHUNK, SSD_HEADS), F32)
        dalast = jnp.zeros((1, SSD_HEADS), F32)
        gdsk = jnp.zeros((1, SSD_HEADS), F32)
        for g in range(SSD_GROUPS):
            bs = slice(D_SSD + SSD_N * g, D_SSD + SSD_N * (g + 1))
            cs = slice(D_SSD + D_BC + SSD_N * g, D_SSD + D_BC + SSD_N * (g + 1))
            Bg = _bf(xbcv[:, bs])
            Cg = _bf(xbcv[:, cs])
            cb = _dot_nt(Cg, Bg)
            dcb = jnp.zeros((CHUNK, CHUNK), F32)
            dB = jnp.zeros((CHUNK, SSD_N), F32)
            dC = jnp.zeros((CHUNK, SSD_N), F32)
            for r in range(SSD_R):
                h = g * SSD_R + r
                hs = slice(SSD_P * h, SSD_P * (h + 1))
                onehot = (lane16 == h).astype(F32)
                acs_c = acs[:, h:h + 1]
                seg = acs_c - acs_row[h:h + 1, :]
                Lm = jnp.where(causal, jnp.exp(jnp.where(causal, seg, 0.0)), 0.0)
                M = cb * Lm
                xh = xbcv[:, hs]
                dth = dt[:, h:h + 1]
                X = xh * dth
                Xb = _bf(X)
                dyh = dypre[:, hs]
                dyb = _bf(dyh)
                prev_h = prev_ref[0, hs, :]
                prevb = _bf(prev_h)
                dnext = dstate[hs, :]
                dnextb = _bf(dnext)
                al = alast[:, h:h + 1]
                eacs = jnp.exp(acs_c)
                eal = jnp.exp(al)
                dsd = jnp.exp(al - acs_c)
                G = _bf(dyh * eacs)
                dstate[hs, :] = dnext * eal + _dot_tn(G, Cg)
                dC = dC + _dot(G, prevb)
                yoff = _dot_nt(Cg, prevb) * eacs
                dacs_h = jnp.sum(dyh * yoff, axis=-1, keepdims=True)
                BdN = _dot_nt(Bg, dnextb)
                dX = dsd * BdN
                dB = dB + _dot(_bf(X * dsd), dnextb)
                t = jnp.sum(X * BdN, axis=-1, keepdims=True) * dsd
                dacs_h = dacs_h - t
                dal = jnp.sum(t, axis=0, keepdims=True) + jnp.sum(
                    jnp.sum(dnext * prev_h, axis=-1, keepdims=True), axis=0, keepdims=True) * eal
                dM = _dot_nt(dyb, Xb)
                dX = dX + _dot_tn(_bf(M), dyb)
                dseg = dM * M
                dcb = dcb + dM * Lm
                dacs_h = dacs_h + jnp.sum(dseg, axis=-1, keepdims=True)
                dacs_row = dacs_row - jnp.sum(dseg, axis=0, keepdims=True) * (sub16 == h).astype(F32)
                dacs_col = dacs_col + dacs_h * onehot
                dalast = dalast + dal * onehot
                ddt_col = ddt_col + jnp.sum(dX * xh, axis=-1, keepdims=True) * onehot
                gdsk = gdsk + jnp.sum(jnp.sum(dyh * xh, axis=-1, keepdims=True), axis=0, keepdims=True) * onehot
                dpost[:, hs] = dX * dth + dsk_ref[:, h:h + 1] * dyh
            dcbb = _bf(dcb)
            dpost[:, bs] = dB + _dot_tn(dcbb, Cg)
            dpost[:, cs] = dC + _dot(dcbb, Bg)

        is_last = (_iota2((CHUNK, 1), 0) == CHUNK - 1).astype(F32)
        dacs = dacs_col + _to_cols(dacs_row) + is_last * dalast
        triu = (_iota2((CHUNK, CHUNK), 0) <= _iota2((CHUNK, CHUNK), 1)).astype(F32)
        da = jnp.dot(triu, dacs, preferred_element_type=F32, precision=HI)
        ddt_tot = ddt_col + da * A
        galog_ref[...] += jnp.sum(da * dt, axis=0, keepdims=True) * A
        ddtraw = ddt_tot * _sigmoid(dtraw)
        gdtb_ref[...] += jnp.sum(ddtraw, axis=0, keepdims=True)
        gdsk_ref[...] += gdsk
        ddt_ref[...] = jnp.zeros_like(ddt_ref)
        ddt_ref[:, 0:SSD_HEADS] = ddtraw

        dconv = dpost[...] * (sig * (1.0 + u * (1.0 - sig)))
        gcb_ref[...] += jnp.sum(dconv, axis=0, keepdims=True)
        for k in range(CONV_K):
            gcw_ref[k:k + 1, :] += jnp.sum(dconv * ext[pl.ds(5 + k, CHUNK), :], axis=0, keepdims=True)
        ext2[0:CHUNK, :] = dconv
        ext2[CHUNK:CHUNK + 8, :] = dhead[...]
        dx = cw_ref[CONV_K - 1:CONV_K, :] * dconv
        for k in range(CONV_K - 1):
            dx = dx + cw_ref[k:k + 1, :] * ext2[pl.ds(CONV_K - 1 - k, CHUNK), :]
        dxbc_ref[...] = _bf(dx)
        dhead[...] = dconv[0:8, :]

    full = lambda shape: pl.BlockSpec(shape, lambda i: (0, 0))
    rev = lambda wd: pl.BlockSpec((CHUNK, wd), lambda i: (nc - 1 - i, 0))
    return pl.pallas_call(
        body, name="ssd_bwd", grid=(nc,),
        in_specs=[
            rev(D_SSD), rev(D_SSD), rev(D_SSD), rev(D_XBC),
            pl.BlockSpec((8, D_XBC), lambda i: (jnp.maximum((nc - 1 - i) * (CHUNK // 8) - 1, 0), 0)),
            rev(DT_PAD),
            pl.BlockSpec((1, SSD_HEADS * SSD_P, SSD_N), lambda i: (nc - 1 - i, 0, 0)),
            full((CONV_K, D_XBC)), full((1, D_XBC)), full((1, SSD_HEADS)), full((1, SSD_HEADS)), full((1, SSD_HEADS)),
            full((1, D_SSD)),
        ],
        out_specs=[
            rev(D_SSD), rev(D_XBC), rev(DT_PAD),
            full((CONV_K, D_XBC)), full((1, D_XBC)), full((1, SSD_HEADS)), full((1, SSD_HEADS)), full((1, SSD_HEADS)),
            full((1, D_SSD)),
        ],
        out_shape=[
            jax.ShapeDtypeStruct((L, D_SSD), BF16), jax.ShapeDtypeStruct((L, D_XBC), BF16),
            jax.ShapeDtypeStruct((L, DT_PAD), F32),
            jax.ShapeDtypeStruct((CONV_K, D_XBC), F32), jax.ShapeDtypeStruct((1, D_XBC), F32),
            jax.ShapeDtypeStruct((1, SSD_HEADS), F32), jax.ShapeDtypeStruct((1, SSD_HEADS), F32),
            jax.ShapeDtypeStruct((1, SSD_HEADS), F32), jax.ShapeDtypeStruct((1, D_SSD), F32),
        ],
        scratch_shapes=[
            pltpu.VMEM((SSD_HEADS * SSD_P, SSD_N), F32),
            pltpu.VMEM((8, D_XBC), F32),
            pltpu.VMEM((CHUNK + 8, D_XBC), F32),
            pltpu.VMEM((CHUNK + 8, D_XBC), F32),
            pltpu.VMEM((CHUNK, D_XBC), F32),
        ],
        compiler_params=_cparams(("arbitrary",)),
    )(dy, z, ypre, xbc, xbc, dtp, prev, conv_w, conv_b, dt_bias, a_log, d_skip, norm_w)


def _head_expander():
    return (_iota2((SSD_HEADS, D_SSD), 1) // SSD_P == _iota2((SSD_HEADS, D_SSD), 0)).astype(BF16)


def _hi_lo(x):
    hi = _bf(x)
    return hi, _bf(x - hi.astype(F32))


def _expand(v, e):
    hi, lo = _hi_lo(v)
    return _dot(hi, e) + _dot(lo, e)


def _headsum(t, e):
    m = t.shape[0]
    if m < 8:
        t = jnp.broadcast_to(t[0:1], (8, t.shape[1]))
    hi, lo = _hi_lo(t)
    return (_dot_nt(hi, e) + _dot_nt(lo, e))[0:m]


def _ssd_decays(dt, acs, dsk_ref, e):
    alast = acs[CHUNK - 1:CHUNK, :]
    stk = jnp.concatenate([dt, jnp.exp(acs), jnp.exp(alast - acs),
                           jnp.broadcast_to(jnp.exp(alast), (8, SSD_HEADS)),
                           jnp.broadcast_to(dsk_ref[...], (8, SSD_HEADS))], axis=0)
    ex = _expand(stk, e)
    return (ex[0:CHUNK], ex[CHUNK:2 * CHUNK], ex[2 * CHUNK:3 * CHUNK], ex[3 * CHUNK:3 * CHUNK + 1],
            ex[3 * CHUNK + 8:3 * CHUNK + 9])


def _ssd_fwd2(z, xbc, dtp, conv_w, conv_b, dt_bias, a_log, d_skip, norm_w):
    L = z.shape[0]
    nc = L // CHUNK
    half = D_SSD // SSD_GROUPS

    def body(z_ref, xbc_ref, tail_ref, dt_ref, cw_ref, cb_ref, dtb_ref, alog_ref, dsk_ref, nw_ref, smat_ref,
             y_ref, ypre_ref, prev_ref, state, ybuf, mbuf):
        c = pl.program_id(0)

        @pl.when(c == 0)
        def _():
            state[...] = jnp.zeros_like(state)

        u, sig, xbcv, dtraw, dt, A, acs, acs_row, _ = _ssd_chunk_pre2(
            c == 0, xbc_ref, tail_ref, dt_ref, cw_ref, cb_ref, dtb_ref, alog_ref, smat_ref)
        e = _head_expander()
        dtE, eacsE, dsdE, ealE, dskE = _ssd_decays(dt, acs, dsk_ref, e)
        xs = xbcv[:, 0:D_SSD]
        X = xs * dtE
        prev_ref[0] = state[...]
        causal = _iota2((CHUNK, CHUNK), 0) >= _iota2((CHUNK, CHUNK), 1)
        for g in range(SSD_GROUPS):
            gs = slice(half * g, half * (g + 1))
            Bg = _bf(xbcv[:, D_SSD + SSD_N * g:D_SSD + SSD_N * (g + 1)])
            Cg = _bf(xbcv[:, D_SSD + D_BC + SSD_N * g:D_SSD + D_BC + SSD_N * (g + 1)])
            cb = _dot_nt(Cg, Bg)
            for r in range(SSD_R):
                h = g * SSD_R + r
                seg = acs[:, h:h + 1] - acs_row[h:h + 1, :]
                mbuf[h] = _bf(cb * jnp.where(causal, jnp.exp(jnp.where(causal, seg, 0.0)), 0.0))
            st = state[:, gs]
            ybuf[:, gs] = _dot(Cg, _bf(st)) * eacsE[:, gs] + dskE[:, gs] * xs[:, gs]
            state[:, gs] = st * ealE[:, gs] + _dot_tn(Bg, _bf(X[:, gs] * dsdE[:, gs]))
        Xb = _bf(X)
        for h in range(SSD_HEADS):
            hs = slice(SSD_P * h, SSD_P * (h + 1))
            ybuf[:, hs] += _dot(mbuf[h], Xb[:, hs])
        y = ybuf[...]
        ypre_ref[...] = y
        zv = z_ref[...]
        yf = y * (zv * _sigmoid(zv))
        for g in range(SSD_GROUPS):
            gs = slice(half * g, half * (g + 1))
            yg = yf[:, gs]
            ms = jnp.mean(yg * yg, axis=-1, keepdims=True)
            y_ref[:, gs] = _bf(yg * lax.rsqrt(ms + RMS_EPS) * nw_ref[:, gs])

    full = lambda shape: pl.BlockSpec(shape, lambda c: (0, 0))
    return pl.pallas_call(
        body, name="ssd_fwd", grid=(nc,),
        in_specs=[
            pl.BlockSpec((CHUNK, D_SSD), lambda c: (c, 0)),
            pl.BlockSpec((CHUNK, D_XBC), lambda c: (c, 0)),
            pl.BlockSpec((HALO, D_XBC), lambda c: (jnp.maximum(c * (CHUNK // HALO) - 1, 0), 0)),
            pl.BlockSpec((CHUNK, DT_PAD), lambda c: (c, 0)),
            full((CONV_K, D_XBC)), full((1, D_XBC)), full((1, SSD_HEADS)), full((1, SSD_HEADS)), full((1, SSD_HEADS)),
            full((1, D_SSD)), full((3 * CHUNK, 2 * (CHUNK + HALO))),
        ],
        out_specs=[
            pl.BlockSpec((CHUNK, D_SSD), lambda c: (c, 0)),
            pl.BlockSpec((CHUNK, D_SSD), lambda c: (c, 0)),
            pl.BlockSpec((1, SSD_N, D_SSD), lambda c: (c, 0, 0)),
        ],
        out_shape=[
            jax.ShapeDtypeStruct((L, D_SSD), BF16),
            jax.ShapeDtypeStruct((L, D_SSD), F32),
            jax.ShapeDtypeStruct((nc, SSD_N, D_SSD), F32),
        ],
        scratch_shapes=[
            pltpu.VMEM((SSD_N, D_SSD), F32),
            pltpu.VMEM((CHUNK, D_SSD), F32),
            pltpu.VMEM((SSD_HEADS, CHUNK, CHUNK), BF16),
        ],
        compiler_params=_cparams(("arbitrary",)),
    )(z, xbc, xbc, dtp, conv_w, conv_b, dt_bias, a_log, d_skip, norm_w, _shift_matrix((13, 14, 15)))


def _ssd_bwd2(dy, z, ypre, xbc, dtp, prev, conv_w, conv_b, dt_bias, a_log, d_skip, norm_w):
    L = z.shape[0]
    nc = L // CHUNK
    half = D_SSD // SSD_GROUPS

    def body(dy_ref, z_ref, ypre_ref, xbc_ref, tail_ref, dt_ref, prev_ref, cw_ref, cb_ref, dtb_ref, alog_ref, dsk_ref,
             nw_ref, smat_ref, smat2_ref, dz_ref, dxbc_ref, ddt_ref, gcw_ref, gcb_ref, gdtb_ref, galog_ref, gdsk_ref,
             gnw_ref, dstate, dhead, dpost, yobuf, bdbuf, lmbuf, dmbuf, cbbuf):
        i = pl.program_id(0)
        c = nc - 1 - i

        @pl.when(i == 0)
        def _():
            dstate[...] = jnp.zeros_like(dstate)
            dhead[...] = jnp.zeros_like(dhead)
            gcw_ref[...] = jnp.zeros_like(gcw_ref)
            gcb_ref[...] = jnp.zeros_like(gcb_ref)
            gdtb_ref[...] = jnp.zeros_like(gdtb_ref)
            galog_ref[...] = jnp.zeros_like(galog_ref)
            gdsk_ref[...] = jnp.zeros_like(gdsk_ref)
            gnw_ref[...] = jnp.zeros_like(gnw_ref)

        u, sig, xbcv, dtraw, dt, A, acs, acs_row, taps = _ssd_chunk_pre2(
            c == 0, xbc_ref, tail_ref, dt_ref, cw_ref, cb_ref, dtb_ref, alog_ref, smat_ref)
        e = _head_expander()
        dtE, eacsE, dsdE, ealE, dskE = _ssd_decays(dt, acs, dsk_ref, e)
        alast = acs[CHUNK - 1:CHUNK, :]
        xs = xbcv[:, 0:D_SSD]
        X = xs * dtE
        Xb = _bf(X)

        zv = z_ref[...]
        ypre = ypre_ref[...]
        dyn = dy_ref[...]
        sz = _sigmoid(zv)
        silu_z = zv * sz
        yf = ypre * silu_z
        dyf_parts = []
        for g in range(SSD_GROUPS):
            gs = slice(half * g, half * (g + 1))
            yg = yf[:, gs]
            rstd = lax.rsqrt(jnp.mean(yg * yg, axis=-1, keepdims=True) + RMS_EPS)
            dout = dyn[:, gs]
            gnw_ref[:, gs] += jnp.sum(dout * yg * rstd, axis=0, keepdims=True)
            dyhat = dout * nw_ref[:, gs]
            dyf_parts.append(rstd * (dyhat - yg * (rstd * rstd) * jnp.mean(dyhat * yg, axis=-1, keepdims=True)))
        dyf = jnp.concatenate(dyf_parts, axis=1)
        dz_ref[...] = _bf(dyf * ypre * (sz * (1.0 + zv * (1.0 - sz))))
        dyp = dyf * silu_z
        dyb = _bf(dyp)
        G = dyp * eacsE

        causal = _iota2((CHUNK, CHUNK), 0) >= _iota2((CHUNK, CHUNK), 1)
        ST = prev_ref[0]
        dST = dstate[...]
        for g in range(SSD_GROUPS):
            gs = slice(half * g, half * (g + 1))
            bs = slice(D_SSD + SSD_N * g, D_SSD + SSD_N * (g + 1))
            cs = slice(D_SSD + D_BC + SSD_N * g, D_SSD + D_BC + SSD_N * (g + 1))
            Bg = _bf(xbcv[:, bs])
            Cg = _bf(xbcv[:, cs])
            Gb = _bf(G[:, gs])
            STb = _bf(ST[:, gs])
            dSTb = _bf(dST[:, gs])
            dstate[:, gs] = dST[:, gs] * ealE[:, gs] + _dot_tn(Cg, Gb)
            yobuf[:, gs] = _dot(Cg, STb) * eacsE[:, gs]
            bdbuf[:, gs] = _dot(Bg, dSTb)
            dpost[:, cs] = _dot_nt(Gb, STb)
            dpost[:, bs] = _dot_nt(_bf(X[:, gs] * dsdE[:, gs]), dSTb)
            cbbuf[g] = _dot_nt(Cg, Bg)
            for r in range(SSD_R):
                h = g * SSD_R + r
                seg = acs[:, h:h + 1] - acs_row[h:h + 1, :]
                lmbuf[h] = jnp.where(causal, jnp.exp(jnp.where(causal, seg, 0.0)), 0.0)
        for h in range(SSD_HEADS):
            hs = slice(SSD_P * h, SSD_P * (h + 1))
            Mb = _bf(cbbuf[h // SSD_R] * lmbuf[h])
            dmbuf[h] = _dot_nt(dyb[:, hs], Xb[:, hs])
            dpost[:, hs] = _dot_tn(Mb, dyb[:, hs])
        lane16 = _iota2((1, SSD_HEADS), 1)
        sub16 = _iota2((SSD_HEADS, 1), 0)
        dacs_col = jnp.zeros((CHUNK, SSD_HEADS), F32)
        dacs_row = jnp.zeros((SSD_HEADS, CHUNK), F32)
        for g in range(SSD_GROUPS):
            bs = slice(D_SSD + SSD_N * g, D_SSD + SSD_N * (g + 1))
            cs = slice(D_SSD + D_BC + SSD_N * g, D_SSD + D_BC + SSD_N * (g + 1))
            cb = cbbuf[g]
            dcb = jnp.zeros((CHUNK, CHUNK), F32)
            for r in range(SSD_R):
                h = g * SSD_R + r
                dM = dmbuf[h]
                Lm = lmbuf[h]
                dcb = dcb + dM * Lm
                dseg = dM * (cb * Lm)
                dacs_col = dacs_col + jnp.sum(dseg, axis=-1, keepdims=True) * (lane16 == h).astype(F32)
                dacs_row = dacs_row - jnp.sum(dseg, axis=0, keepdims=True) * (sub16 == h).astype(F32)
            dcbb = _bf(dcb)
            dpost[:, bs] += _dot_tn(dcbb, _bf(xbcv[:, cs]))
            dpost[:, cs] += _dot(dcbb, _bf(xbcv[:, bs]))

        BD = bdbuf[...]
        dX = dpost[:, 0:D_SSD] + dsdE * BD
        dsd = jnp.exp(alast - acs)
        T = _headsum(X * BD, e) * dsd
        dalast = jnp.sum(T, axis=0, keepdims=True) + _headsum(
            jnp.sum(dST * ST, axis=0, keepdims=True), e) * jnp.exp(alast)
        is_last = (_iota2((CHUNK, 1), 0) == CHUNK - 1).astype(F32)
        dacs = dacs_col + _to_cols(dacs_row) + _headsum(dyp * yobuf[...], e) - T + is_last * dalast
        triu = (_iota2((CHUNK, CHUNK), 0) <= _iota2((CHUNK, CHUNK), 1)).astype(F32)
        da = jnp.dot(triu, dacs, preferred_element_type=F32, precision=HI)
        ddt_tot = _headsum(dX * xs, e) + da * A
        galog_ref[...] += jnp.sum(da * dt, axis=0, keepdims=True) * A
        ddtraw = ddt_tot * _sigmoid(dtraw)
        gdtb_ref[...] += jnp.sum(ddtraw, axis=0, keepdims=True)
        gdsk_ref[...] += _headsum(jnp.sum(dyp * xs, axis=0, keepdims=True), e)
        ddt_ref[...] = jnp.zeros_like(ddt_ref)
        ddt_ref[:, 0:SSD_HEADS] = ddtraw
        dpost[:, 0:D_SSD] = dX * dtE + dskE * dyp

        dconv = dpost[...] * (sig * (1.0 + u * (1.0 - sig)))
        gcb_ref[...] += jnp.sum(dconv, axis=0, keepdims=True)
        for k in range(CONV_K):
            gcw_ref[k:k + 1, :] += jnp.sum(dconv * taps[k], axis=0, keepdims=True)
        later = _shifted_rows(dconv, dhead[...], smat2_ref)
        dx = cw_ref[CONV_K - 1:CONV_K, :] * dconv
        for k in range(CONV_K - 1):
            dx = dx + cw_ref[k:k + 1, :] * later[k]
        dxbc_ref[...] = _bf(dx)
        dhead[...] = dconv[0:HALO, :]

    full = lambda shape: pl.BlockSpec(shape, lambda i: (0, 0))
    rev = lambda wd: pl.BlockSpec((CHUNK, wd), lambda i: (nc - 1 - i, 0))
    return pl.pallas_call(
        body, name="ssd_bwd", grid=(nc,),
        in_specs=[
            rev(D_SSD), rev(D_SSD), rev(D_SSD), rev(D_XBC),
            pl.BlockSpec((HALO, D_XBC), lambda i: (jnp.maximum((nc - 1 - i) * (CHUNK // HALO) - 1, 0), 0)),
            rev(DT_PAD),
            pl.BlockSpec((1, SSD_N, D_SSD), lambda i: (nc - 1 - i, 0, 0)),
            full((CONV_K, D_XBC)), full((1, D_XBC)), full((1, SSD_HEADS)), full((1, SSD_HEADS)), full((1, SSD_HEADS)),
            full((1, D_SSD)), full((3 * CHUNK, 2 * (CHUNK + HALO))), full((3 * CHUNK, 2 * (CHUNK + HALO))),
        ],
        out_specs=[
            rev(D_SSD), rev(D_XBC), rev(DT_PAD),
            full((CONV_K, D_XBC)), full((1, D_XBC)), full((1, SSD_HEADS)), full((1, SSD_HEADS)), full((1, SSD_HEADS)),
            full((1, D_SSD)),
        ],
        out_shape=[
            jax.ShapeDtypeStruct((L, D_SSD), BF16), jax.ShapeDtypeStruct((L, D_XBC), BF16),
            jax.ShapeDtypeStruct((L, DT_PAD), F32),
            jax.ShapeDtypeStruct((CONV_K, D_XBC), F32), jax.ShapeDtypeStruct((1, D_XBC), F32),
            jax.ShapeDtypeStruct((1, SSD_HEADS), F32), jax.ShapeDtypeStruct((1, SSD_HEADS), F32),
            jax.ShapeDtypeStruct((1, SSD_HEADS), F32), jax.ShapeDtypeStruct((1, D_SSD), F32),
        ],
        scratch_shapes=[
            pltpu.VMEM((SSD_N, D_SSD), F32),
            pltpu.VMEM((HALO, D_XBC), F32),
            pltpu.VMEM((CHUNK, D_XBC), F32),
            pltpu.VMEM((CHUNK, D_SSD), F32),
            pltpu.VMEM((CHUNK, D_SSD), F32),
            pltpu.VMEM((SSD_HEADS, CHUNK, CHUNK), F32),
            pltpu.VMEM((SSD_HEADS, CHUNK, CHUNK), F32),
            pltpu.VMEM((SSD_GROUPS, CHUNK, CHUNK), F32),
        ],
        compiler_params=_cparams(("arbitrary",)),
    )(dy, z, ypre, xbc, xbc, dtp, prev, conv_w, conv_b, dt_bias, a_log, d_skip, norm_w, _shift_matrix((13, 14, 15)),
      _shift_matrix((3, 2, 1)))


def _rope_tables(pos_ref, inv_ref):
    ang = pos_ref[...].astype(F32) * inv_ref[...]
    d = _iota2((1, 2 * ATT_HD), 1) % ATT_HD
    s = jnp.sin(ang)
    return jnp.cos(ang), jnp.where(d < ROPE_DIM // 2, -s, 0.0), jnp.where((d >= ROPE_DIM // 2) & (d < ROPE_DIM), s, 0.0)


def _rope(t, tabs):
    c, s1, s2 = tabs
    n = t.shape[1]
    rep = n // c.shape[1]
    return (t * jnp.tile(c, (1, rep)) + pltpu.roll(t, n - ROPE_DIM // 2, 1) * jnp.tile(s1, (1, rep))
            + pltpu.roll(t, ROPE_DIM // 2, 1) * jnp.tile(s2, (1, rep)))


def _rope_t(t, tabs):
    c, s1, s2 = tabs
    n = t.shape[1]
    rep = n // c.shape[1]
    return (t * jnp.tile(c, (1, rep)) + pltpu.roll(t * jnp.tile(s1, (1, rep)), ROPE_DIM // 2, 1)
            + pltpu.roll(t * jnp.tile(s2, (1, rep)), n - ROPE_DIM // 2, 1))


def _swa_mask(first):
    qi = _iota2((WINDOW, 2 * WINDOW), 0)
    si = _iota2((WINDOW, 2 * WINDOW), 1)
    band = (si > qi) & (si <= qi + WINDOW)
    return band & (jnp.logical_not(first) | (si >= WINDOW))


def _stack_heads(t, j):
    return jnp.concatenate([t[:, ATT_HD * (j * ATT_R + r):ATT_HD * (j * ATT_R + r + 1)] for r in range(ATT_R)], axis=0)


def _stack_cols(ref, j):
    cols = [jnp.broadcast_to(ref[:, j * ATT_R + r:j * ATT_R + r + 1], (WINDOW, 1)) for r in range(ATT_R)]
    return jnp.concatenate(cols, axis=0)


def _swa_mask_t(first):
    si = _iota2((2 * WINDOW, ATT_R * WINDOW), 0)
    qi = _iota2((2 * WINDOW, ATT_R * WINDOW), 1) % WINDOW
    band = (si > qi) & (si <= qi + WINDOW)
    return band & (jnp.logical_not(first) | (si >= WINDOW))


def _head_rows(ref, j, rows=None):
    if ref.shape[0] == 1:
        parts = [jnp.broadcast_to(ref[:, j * ATT_R + r:j * ATT_R + r + 1], (1, WINDOW)) for r in range(ATT_R)]
    else:
        parts = [ref[j * ATT_R + r:j * ATT_R + r + 1, :] for r in range(ATT_R)]
    return jnp.concatenate(parts, axis=1)


def _swa_fwd(q, g, kv, sinks):
    L = q.shape[0]
    nb = L // WINDOW
    scale = ATT_HD ** -0.5

    def body(q_ref, g_ref, kvc_ref, kvp_ref, sink_ref, y_ref, o_ref, lse_ref, otbuf):
        n = pl.program_id(0)
        kk = jnp.concatenate([kvp_ref[:, 0:D_KV], kvc_ref[:, 0:D_KV]], axis=0)
        vv = jnp.concatenate([kvp_ref[:, D_KV:2 * D_KV], kvc_ref[:, D_KV:2 * D_KV]], axis=0)
        valid = _swa_mask_t(n == 0)
        qv = q_ref[...]
        for j in range(ATT_KVH):
            js = slice(ATT_HD * j, ATT_HD * (j + 1))
            st = _dot_nt(kk[:, js], _stack_heads(qv, j)) * scale
            st = jnp.where(valid, st, NEG_BIG)
            sink = _head_rows(sink_ref, j)
            m = jnp.maximum(jnp.max(st, axis=0, keepdims=True), sink)
            p = jnp.exp(st - m)
            denom = jnp.sum(p, axis=0, keepdims=True) + jnp.exp(sink - m)
            ot = _dot_tn(vv[:, js], _bf(p)) * (1.0 / denom)
            lse = m + jnp.log(denom)
            for r in range(ATT_R):
                h = j * ATT_R + r
                otbuf[ATT_HD * h:ATT_HD * (h + 1), :] = ot[:, WINDOW * r:WINDOW * (r + 1)]
                lse_ref[h:h + 1, :] = lse[:, WINDOW * r:WINDOW * (r + 1)]
        o = otbuf[...].T
        o_ref[...] = o
        gv = g_ref[...]
        y_ref[...] = _bf(o * (gv * _sigmoid(gv)))

    cur = lambda wd: pl.BlockSpec((WINDOW, wd), lambda n: (n, 0))
    prv = lambda wd: pl.BlockSpec((WINDOW, wd), lambda n: (jnp.maximum(n - 1, 0), 0))
    return pl.pallas_call(
        body, name="swa_fwd", grid=(nb,),
        in_specs=[cur(D_ATT), cur(D_ATT), cur(2 * D_KV), prv(2 * D_KV), pl.BlockSpec((1, ATT_QH), lambda n: (0, 0))],
        out_specs=[cur(D_ATT), cur(D_ATT), pl.BlockSpec((ATT_QH, WINDOW), lambda n: (0, n))],
        out_shape=[jax.ShapeDtypeStruct((L, D_ATT), BF16), jax.ShapeDtypeStruct((L, D_ATT), F32),
                   jax.ShapeDtypeStruct((ATT_QH, L), F32)],
        scratch_shapes=[pltpu.VMEM((D_ATT, WINDOW), F32)],
        compiler_params=_cparams(("parallel",)),
    )(q, g, kv, kv, sinks)


def _swa_bwd(dy, q, g, kv, o, lse, pos, inv, sinks):
    L = q.shape[0]
    nb = L // WINDOW
    scale = ATT_HD ** -0.5

    def body(dy_ref, q_ref, g_ref, kvc_ref, kvp_ref, o_ref, lse_ref, posc_ref, posp_ref, inv_ref, sink_ref,
             dq_ref, dg_ref, dkv_ref, dsink_ref, carry, dqbuf, dkbuf, dvbuf):
        n = pl.program_id(0)

        @pl.when(n == 0)
        def _():
            dsink_ref[...] = jnp.zeros_like(dsink_ref)

        @pl.when(n < nb)
        def _():
            tc = _rope_tables(posc_ref, inv_ref)
            tp = _rope_tables(posp_ref, inv_ref)
            kk = jnp.concatenate([kvp_ref[:, 0:D_KV], kvc_ref[:, 0:D_KV]], axis=0)
            vv = jnp.concatenate([kvp_ref[:, D_KV:2 * D_KV], kvc_ref[:, D_KV:2 * D_KV]], axis=0)
            valid = _swa_mask_t(n == 0)
            qv = q_ref[...]
            gv = g_ref[...]
            sg = _sigmoid(gv)
            dyv = dy_ref[...]
            ov = o_ref[...]
            dg_ref[...] = _bf(dyv * ov * (sg * (1.0 + gv * (1.0 - sg))))
            do = dyv * (gv * sg)
            dod = do * ov
            ones = jnp.ones((8, ATT_HD), BF16)
            lane16 = _iota2((1, ATT_QH), 1)
            dsink = jnp.zeros((1, ATT_QH), F32)
            for j in range(ATT_KVH):
                js = slice(ATT_HD * j, ATT_HD * (j + 1))
                kj = kk[:, js]
                vj = vv[:, js]
                qs = _stack_heads(qv, j)
                dos = _bf(_stack_heads(do, j))
                hi, lo = _hi_lo(_stack_heads(dod, j))
                delta = (_dot_nt(ones, hi) + _dot_nt(ones, lo))[0:1]
                lse = _head_rows(lse_ref, j)
                st = _dot_nt(kj, qs) * scale
                pt = jnp.exp(jnp.where(valid, st, NEG_BIG) - lse)
                dst = _bf(pt * (_dot_nt(vj, dos) - delta))
                dqt = _dot_tn(kj, dst) * scale
                dkbuf[:, js] = _dot(dst, qs) * scale
                dvbuf[:, js] = _dot(_bf(pt), dos)
                sd = jnp.exp(_head_rows(sink_ref, j) - lse) * delta
                for r in range(ATT_R):
                    h = j * ATT_R + r
                    ls = slice(WINDOW * r, WINDOW * (r + 1))
                    dqbuf[ATT_HD * h:ATT_HD * (h + 1), :] = dqt[:, ls]
                    dsink = dsink - jnp.sum(sd[:, ls], axis=1, keepdims=True) * (lane16 == h).astype(F32)
            dsink_ref[...] += dsink
            dq_ref[...] = _bf(_rope_t(dqbuf[...].T, tc))
            dkp = _rope_t(dkbuf[0:WINDOW, :], tp)
            dkc = _rope_t(dkbuf[WINDOW:2 * WINDOW, :], tc)

            @pl.when(n > 0)
            def _():
                dkv_ref[:, 0:D_KV] = _bf(carry[:, 0:D_KV] + dkp)
                dkv_ref[:, D_KV:2 * D_KV] = _bf(carry[:, D_KV:2 * D_KV] + dvbuf[0:WINDOW, :])

            carry[:, 0:D_KV] = dkc
            carry[:, D_KV:2 * D_KV] = dvbuf[WINDOW:2 * WINDOW, :]

        @pl.when(n == nb)
        def _():
            dkv_ref[...] = _bf(carry[...])

    last = nb - 1
    cur = lambda wd: pl.BlockSpec((WINDOW, wd), lambda n: (jnp.minimum(n, last), 0))
    prv = lambda wd: pl.BlockSpec((WINDOW, wd), lambda n: (jnp.maximum(jnp.minimum(n, last) - 1, 0), 0))
    return pl.pallas_call(
        body, name="swa_bwd", grid=(nb + 1,),
        in_specs=[cur(D_ATT), cur(D_ATT), cur(D_ATT), cur(2 * D_KV), prv(2 * D_KV), cur(D_ATT),
                  pl.BlockSpec((ATT_QH, WINDOW), lambda n: (0, jnp.minimum(n, last))), cur(1), prv(1),
                  pl.BlockSpec((1, 2 * ATT_HD), lambda n: (0, 0)), pl.BlockSpec((1, ATT_QH), lambda n: (0, 0))],
        out_specs=[cur(D_ATT), cur(D_ATT),
                   pl.BlockSpec((WINDOW, 2 * D_KV), lambda n: (jnp.maximum(n - 1, 0), 0)),
                   pl.BlockSpec((1, ATT_QH), lambda n: (0, 0))],
        out_shape=[jax.ShapeDtypeStruct((L, D_ATT), BF16), jax.ShapeDtypeStruct((L, D_ATT), BF16),
                   jax.ShapeDtypeStruct((L, 2 * D_KV), BF16), jax.ShapeDtypeStruct((1, ATT_QH), F32)],
        scratch_shapes=[pltpu.VMEM((WINDOW, 2 * D_KV), F32), pltpu.VMEM((D_ATT, WINDOW), F32),
                        pltpu.VMEM((2 * WINDOW, D_KV), F32), pltpu.VMEM((2 * WINDOW, D_KV), F32)],
        compiler_params=_cparams(("arbitrary",)),
    )(dy, q, g, kv, kv, o, lse, pos, pos, inv, sinks)


def _out_ln_loss(y_ssd, y_att, x, target, w_out, ln_g, ln_b):
    L = x.shape[0]
    tm = ROW_TILE
    inv_d = 1.0 / D_MODEL

    def body(ys_ref, ya_ref, x_ref, t_ref, w_ref, g_ref, b_ref, dr_ref, dys_ref, dya_ref, loss_ref, gg_ref, gb_ref):
        i = pl.program_id(0)

        @pl.when(i == 0)
        def _():
            loss_ref[...] = jnp.zeros_like(loss_ref)
            gg_ref[...] = jnp.zeros_like(gg_ref)
            gb_ref[...] = jnp.zeros_like(gb_ref)

        h = _dot(_bf(ys_ref[...]), w_ref[0:D_SSD, :]) + _dot(_bf(ya_ref[...]), w_ref[D_SSD:D_MIX, :])
        r = ALPHA * x_ref[...] + h
        mu = jnp.mean(r, axis=-1, keepdims=True)
        xc = r - mu
        rstd = lax.rsqrt(jnp.mean(xc * xc, axis=-1, keepdims=True) + LN_EPS)
        xhat = xc * rstd
        gam = g_ref[...]
        diff = xhat * gam + b_ref[...] - t_ref[...]
        part = jnp.sum(jnp.sum(diff * diff, axis=-1, keepdims=True), axis=0, keepdims=True)
        loss_ref[...] += (0.5 * inv_d) * part
        dout = diff * inv_d
        gg_ref[...] += jnp.sum(dout * xhat, axis=0, keepdims=True)
        gb_ref[...] += jnp.sum(dout, axis=0, keepdims=True)
        dxh = dout * gam
        dr = rstd * (dxh - jnp.mean(dxh, axis=-1, keepdims=True) - xhat * jnp.mean(dxh * xhat, axis=-1, keepdims=True))
        dr_ref[...] = dr
        drb = _bf(dr)
        dys_ref[...] = _dot_nt(drb, w_ref[0:D_SSD, :])
        dya_ref[...] = _dot_nt(drb, w_ref[D_SSD:D_MIX, :])

    row = pl.BlockSpec((tm, D_MODEL), lambda i: (i, 0))
    vec = pl.BlockSpec((1, D_MODEL), lambda i: (0, 0))
    return pl.pallas_call(
        body, name="out_ln_loss", grid=(L // tm,),
        in_specs=[row, row, row, row, pl.BlockSpec((D_MIX, D_MODEL), lambda i: (0, 0), pipeline_mode=pl.Buffered(1)), vec, vec],
        out_specs=[row, row, row, pl.BlockSpec((1, 128), lambda i: (0, 0)), vec, vec],
        out_shape=[jax.ShapeDtypeStruct((L, D_MODEL), F32)] * 3 + [jax.ShapeDtypeStruct((1, 128), F32)]
        + [jax.ShapeDtypeStruct((1, D_MODEL), F32)] * 2,
        compiler_params=_cparams(("arbitrary",)),
    )(y_ssd, y_att, x, target, w_out, ln_g, ln_b)


def _local_step(x, pos, target, w, w_out, conv_w, conv_b, dt_bias, a_log, d_skip, norm_w, sinks, ln_g, ln_b):
    inv8 = ROPE_THETA ** (-jnp.arange(0, ROPE_DIM, 2, dtype=F32) / ROPE_DIM)
    inv = jnp.tile(jnp.concatenate([inv8, inv8, jnp.zeros((ATT_HD - ROPE_DIM,), F32)]), 2).reshape(1, 2 * ATT_HD)

    z, g, q, xbc, kv, dtp, xb = _in_proj(x, w, pos, inv)
    y_ssd, y_pre, prev = _ssd_fwd2(z, xbc, dtp, conv_w, conv_b, dt_bias, a_log, d_skip, norm_w)
    y_att, o, lse = _swa_fwd(q, g, kv, sinks)
    dr, dy_ssd, dy_att, loss, g_ln_g, g_ln_b = _out_ln_loss(y_ssd, y_att, x, target, w_out, ln_g, ln_b)
    gw_out_ssd, gw_out_att = _matmuls_tn([y_ssd, y_att], dr, "gw_out")
    dq, dg, dkv, g_sinks = _swa_bwd(dy_att, q, g, kv, o, lse, pos, inv, sinks)
    dz, dxbc, ddt, g_conv_w, g_conv_b, g_dt_bias, g_a_log, g_d_skip, g_norm_w = _ssd_bwd2(
        dy_ssd, z, y_pre, xbc, dtp, prev, conv_w, conv_b, dt_bias, a_log, d_skip, norm_w)
    grad_x = _grad_x(dr, dz, dg, dq, dxbc, dkv, ddt, w)
    gw_z, gw_g, gw_q = _matmuls_tn([dz, dg, dq], xb, "gw_zgq")
    gw_xbc, gw_kv, gw_dt = _matmuls_tn([dxbc, dkv, ddt], xb, "gw_xbc_kv_dt")
    gw_in = jnp.concatenate([gw_z, gw_xbc, gw_dt[0:SSD_HEADS], gw_q, gw_kv, gw_g], axis=0)
    gw_out = jnp.concatenate([gw_out_ssd, gw_out_att], axis=0)
    small = dict(conv_w=g_conv_w, conv_b=g_conv_b, dt_bias=g_dt_bias, a_log=g_a_log, d_skip=g_d_skip,
                 ssd_norm_w=g_norm_w, attn_sinks=g_sinks, ln_g=g_ln_g, ln_b=g_ln_b)
    return loss, grad_x, gw_in, gw_out, small


def _mesh_pos():
    return lax.axis_index("x"), lax.axis_index("y"), lax.axis_index("c")


def _gather_weights(w_in_s, w_out_s, conv_w_s):
    def body(win_ref, wout_ref, cw_ref, owin_ref, owout_ref, ocw_ref, send_sems, recv_sems, small_send, small_recv,
             local_sems):
        x, y, c = _mesh_pos()
        me = 2 * x + y
        sibling = (x, y, 1 - c)
        chips = [(1 - x, y), (x, 1 - y), (1 - x, 1 - y)]
        locals_ = [pltpu.make_async_copy(cw_ref, ocw_ref.at[me], local_sems.at[0])]
        for cp in locals_:
            cp.start()
        started = []
        for t, (src, dst) in enumerate(((win_ref, owin_ref), (wout_ref, owout_ref))):
            hr = src.shape[0] // 2

            def half(ref, hc, hr=hr):
                return ref.at[pl.ds(hc * hr, hr), :]

            for j, (px, py) in enumerate(chips):
                cp = pltpu.make_async_remote_copy(
                    src_ref=half(src, c), dst_ref=half(dst.at[me], c), send_sem=send_sems.at[t, j],
                    recv_sem=recv_sems.at[t, j], device_id=(px, py, c), device_id_type=MESH)
                cp.start()
                started.append(cp)
        for j, (px, py) in enumerate(chips):
            cp = pltpu.make_async_remote_copy(
                src_ref=cw_ref, dst_ref=ocw_ref.at[me], send_sem=small_send.at[j], recv_sem=small_recv.at[j],
                device_id=(px, py, c), device_id_type=MESH)
            cp.start()
            started.append(cp)
        for t, (src, dst) in enumerate(((win_ref, owin_ref), (wout_ref, owout_ref))):
            hr = src.shape[0] // 2
            for j, (px, py) in enumerate(chips):
                src_chip = 2 * px + py
                blk = dst.at[src_chip].at[pl.ds(c * hr, hr), :]
                pltpu.make_async_remote_copy(
                    src_ref=blk, dst_ref=blk, send_sem=send_sems.at[t, j], recv_sem=recv_sems.at[t, j],
                    device_id=(px, py, c), device_id_type=MESH).wait_recv()
                cp = pltpu.make_async_remote_copy(
                    src_ref=blk, dst_ref=blk, send_sem=send_sems.at[t, 3 + j], recv_sem=recv_sems.at[t, 3 + j],
                    device_id=sibling, device_id_type=MESH)
                cp.start()
                started.append(cp)
        for t, (src, dst) in enumerate(((win_ref, owin_ref), (wout_ref, owout_ref))):
            hr = src.shape[0] // 2
            for j, (px, py) in enumerate(chips):
                src_chip = 2 * px + py
                blk = dst.at[src_chip].at[pl.ds((1 - c) * hr, hr), :]
                pltpu.make_async_remote_copy(
                    src_ref=blk, dst_ref=blk, send_sem=send_sems.at[t, 3 + j], recv_sem=recv_sems.at[t, 3 + j],
                    device_id=sibling, device_id_type=MESH).wait_recv()
        for j in range(3):
            pltpu.make_async_remote_copy(
                src_ref=cw_ref, dst_ref=ocw_ref.at[me], send_sem=small_send.at[j], recv_sem=small_recv.at[j],
                device_id=sibling, device_id_type=MESH).wait_recv()
        for cp in started:
            cp.wait_send()
        for cp in locals_:
            cp.wait()

    any_spec = pl.BlockSpec(memory_space=pl.ANY)
    return pl.pallas_call(
        body, name="gather_weights",
        in_specs=[any_spec] * 3, out_specs=[any_spec] * 3,
        out_shape=[jax.ShapeDtypeStruct((N_CHIPS,) + a.shape, a.dtype) for a in (w_in_s, w_out_s, conv_w_s)],
        scratch_shapes=[pltpu.SemaphoreType.DMA((2, 6)), pltpu.SemaphoreType.DMA((2, 6)),
                        pltpu.SemaphoreType.DMA((3,)), pltpu.SemaphoreType.DMA((3,)), pltpu.SemaphoreType.DMA((3,))],
    )(w_in_s, w_out_s, conv_w_s)


def _pair_exchange(gw_in, gw_out, small):
    k_small = small.shape[1]

    def body(gin_ref, gout_ref, sm_ref, rin_ref, rout_ref, slots_ref, send_sems, recv_sems, small_send, small_recv,
             local_sem):
        x, y, c = _mesh_pos()
        me = 4 * x + 2 * y + c
        sibling = (x, y, 1 - c)
        mine = pltpu.make_async_copy(sm_ref, slots_ref.at[me], local_sem)
        mine.start()
        started = []
        for t, (src, dst) in enumerate(((gin_ref, rin_ref), (gout_ref, rout_ref))):
            hr = src.shape[1] // 2
            for j in range(N_CHIPS):
                cp = pltpu.make_async_remote_copy(
                    src_ref=src.at[j, pl.ds((1 - c) * hr, hr), :], dst_ref=dst.at[j], send_sem=send_sems.at[t, j],
                    recv_sem=recv_sems.at[t, j], device_id=sibling, device_id_type=MESH)
                cp.start()
                started.append(cp)
        for k in range(1, 8):
            peer = (x ^ ((k >> 2) & 1), y ^ ((k >> 1) & 1), c ^ (k & 1))
            cp = pltpu.make_async_remote_copy(
                src_ref=sm_ref, dst_ref=slots_ref.at[me], send_sem=small_send.at[k - 1], recv_sem=small_recv.at[k - 1],
                device_id=peer, device_id_type=MESH)
            cp.start()
            started.append(cp)
        for t, (src, dst) in enumerate(((gin_ref, rin_ref), (gout_ref, rout_ref))):
            for j in range(N_CHIPS):
                pltpu.make_async_remote_copy(
                    src_ref=dst.at[j], dst_ref=dst.at[j], send_sem=send_sems.at[t, j], recv_sem=recv_sems.at[t, j],
                    device_id=sibling, device_id_type=MESH).wait_recv()
        for k in range(1, 8):
            pltpu.make_async_remote_copy(
                src_ref=sm_ref, dst_ref=slots_ref.at[me], send_sem=small_send.at[k - 1], recv_sem=small_recv.at[k - 1],
                device_id=sibling, device_id_type=MESH).wait_recv()
        for cp in started:
            cp.wait_send()
        mine.wait()

    any_spec = pl.BlockSpec(memory_space=pl.ANY)
    half_in = jax.ShapeDtypeStruct((N_CHIPS, gw_in.shape[1] // 2, D_MODEL), F32)
    half_out = jax.ShapeDtypeStruct((N_CHIPS, gw_out.shape[1] // 2, D_MODEL), F32)
    return pl.pallas_call(
        body, name="pair_exchange",
        in_specs=[any_spec] * 3, out_specs=[any_spec] * 3,
        out_shape=[half_in, half_out, jax.ShapeDtypeStruct((8, 8, k_small), F32)],
        scratch_shapes=[pltpu.SemaphoreType.DMA((2, N_CHIPS)), pltpu.SemaphoreType.DMA((2, N_CHIPS)),
                        pltpu.SemaphoreType.DMA((7,)), pltpu.SemaphoreType.DMA((7,)), pltpu.SemaphoreType.DMA],
    )(gw_in, gw_out, small)


def _chip_exchange(s_in, s_out):
    def body(sin_ref, sout_ref, rin_ref, rout_ref, send_sems, recv_sems):
        x, y, c = _mesh_pos()
        me = 2 * x + y
        chips = [(1 - x, y), (x, 1 - y), (1 - x, 1 - y)]
        started = []
        for t, (src, dst) in enumerate(((sin_ref, rin_ref), (sout_ref, rout_ref))):
            for j, (px, py) in enumerate(chips):
                cp = pltpu.make_async_remote_copy(
                    src_ref=src.at[2 * px + py], dst_ref=dst.at[me], send_sem=send_sems.at[t, j],
                    recv_sem=recv_sems.at[t, j], device_id=(px, py, c), device_id_type=MESH)
                cp.start()
                started.append(cp)
        for t, (src, dst) in enumerate(((sin_ref, rin_ref), (sout_ref, rout_ref))):
            for j, (px, py) in enumerate(chips):
                blk = dst.at[2 * px + py]
                pltpu.make_async_remote_copy(
                    src_ref=blk, dst_ref=blk, send_sem=send_sems.at[t, j], recv_sem=recv_sems.at[t, j],
                    device_id=(px, py, c), device_id_type=MESH).wait_recv()
        for cp in started:
            cp.wait_send()

    any_spec = pl.BlockSpec(memory_space=pl.ANY)
    return pl.pallas_call(
        body, name="chip_exchange",
        in_specs=[any_spec] * 2, out_specs=[any_spec] * 2,
        out_shape=[jax.ShapeDtypeStruct(s_in.shape, s_in.dtype), jax.ShapeDtypeStruct(s_out.shape, s_out.dtype)],
        scratch_shapes=[pltpu.SemaphoreType.DMA((2, 3)), pltpu.SemaphoreType.DMA((2, 3))],
    )(s_in, s_out)


def _pair_share(h_in, h_out):
    def body(hin_ref, hout_ref, rin_ref, rout_ref, send_sems, recv_sems):
        x, y, c = _mesh_pos()
        sibling = (x, y, 1 - c)
        started = []
        for t, (src, dst) in enumerate(((hin_ref, rin_ref), (hout_ref, rout_ref))):
            cp = pltpu.make_async_remote_copy(
                src_ref=src, dst_ref=dst, send_sem=send_sems.at[t], recv_sem=recv_sems.at[t],
                device_id=sibling, device_id_type=MESH)
            cp.start()
            started.append(cp)
        for cp in started:
            cp.wait()

    any_spec = pl.BlockSpec(memory_space=pl.ANY)
    return pl.pallas_call(
        body, name="pair_share",
        in_specs=[any_spec] * 2, out_specs=[any_spec] * 2,
        out_shape=[jax.ShapeDtypeStruct(h_in.shape, F32), jax.ShapeDtypeStruct(h_out.shape, F32)],
        scratch_shapes=[pltpu.SemaphoreType.DMA((2,)), pltpu.SemaphoreType.DMA((2,))],
    )(h_in, h_out)


def _pair_add(g, recv, core, name):
    _, rows, C = recv.shape
    tc = 256

    def body(core_ref, g_ref, r_ref, o_ref):
        o_ref[...] = _bf(g_ref[...] + r_ref[...])

    spec = pl.BlockSpec((1, rows, tc), lambda j, i, core: (j, 0, i))
    return pl.pallas_call(
        body, name=name,
        grid_spec=pltpu.PrefetchScalarGridSpec(
            num_scalar_prefetch=1, grid=(N_CHIPS, C // tc),
            in_specs=[pl.BlockSpec((1, rows, tc), lambda j, i, core: (j, core[0], i)), spec], out_specs=spec),
        out_shape=jax.ShapeDtypeStruct((N_CHIPS, rows, C), BF16),
        compiler_params=_cparams(("parallel", "parallel")),
    )(core, g, recv)


def _chip_add(own, parts, chip, name):
    _, rows, C = parts.shape
    tc = 256

    def body(chip_ref, own_ref, r0, r1, r2, r3, o_ref):
        acc = None
        for j, r in enumerate((r0, r1, r2, r3)):
            term = jnp.where(chip_ref[0] == j, own_ref[0], r[0]).astype(F32)
            acc = term if acc is None else acc + term
        o_ref[...] = acc

    def slab(j):
        return pl.BlockSpec((1, rows, tc), lambda i, chip: (jnp.where(chip[0] == j, (j + 1) % N_CHIPS, j), 0, i))

    return pl.pallas_call(
        body, name=name,
        grid_spec=pltpu.PrefetchScalarGridSpec(
            num_scalar_prefetch=1, grid=(C // tc,),
            in_specs=[pl.BlockSpec((1, rows, tc), lambda i, chip: (chip[0], 0, i))] + [slab(j) for j in range(N_CHIPS)],
            out_specs=pl.BlockSpec((rows, tc), lambda i, chip: (0, i))),
        out_shape=jax.ShapeDtypeStruct((rows, C), F32),
        compiler_params=_cparams(("parallel",)),
    )(chip, own, parts, parts, parts, parts)


def _adamw_math(w, g, m, v):
    m = ADAM_B1 * m + (1.0 - ADAM_B1) * g
    v = ADAM_B2 * v + (1.0 - ADAM_B2) * (g * g)
    m_hat = m / (1.0 - ADAM_B1 ** ADAM_STEP)
    v_hat = v / (1.0 - ADAM_B2 ** ADAM_STEP)
    delta = -ADAM_LR * (m_hat / (jnp.sqrt(v_hat) + ADAM_EPS) + ADAM_WD * w)
    return delta, m, v


def _adamw_pair(w, g_own, g_sib, m, v, core, name):
    unit = w.ndim == 3
    R, C = w.shape[0], w.shape[-1]
    rows = g_own.shape[0]
    tc = 128

    def body(core_ref, w_ref, go_ref, gs_ref, m_ref, v_ref, d_ref, nm_ref, nv_ref, g_ref):
        first = core_ref[0] == 0
        own, sib = go_ref[...], gs_ref[...]
        g = jnp.concatenate([jnp.where(first, own, sib), jnp.where(first, sib, own)], axis=0)[0:R, :]
        idx = (slice(None), 0, slice(None)) if unit else (slice(None), slice(None))
        d, nm, nv = _adamw_math(w_ref[idx], g, m_ref[idx], v_ref[idx])
        d_ref[idx] = d
        nm_ref[idx] = nm
        nv_ref[idx] = nv
        g_ref[idx] = g

    if unit:
        spec = pl.BlockSpec((R, 1, tc), lambda i, core: (0, 0, i))
    else:
        spec = pl.BlockSpec((R, tc), lambda i, core: (0, i))
    gspec = pl.BlockSpec((rows, tc), lambda i, core: (0, i))
    return pl.pallas_call(
        body, name=name,
        grid_spec=pltpu.PrefetchScalarGridSpec(
            num_scalar_prefetch=1, grid=(C // tc,),
            in_specs=[spec, gspec, gspec, spec, spec], out_specs=[spec] * 4),
        out_shape=[jax.ShapeDtypeStruct(w.shape, F32)] * 4,
        compiler_params=_cparams(("parallel",)),
    )(core, w, g_own, g_sib, m, v)


SMALL_NAMES = ("conv_b", "ssd_norm_w", "ln_g", "ln_b", "dt_bias", "a_log", "d_skip", "attn_sinks")
SMALL_SIZES = (D_XBC, D_SSD, D_MODEL, D_MODEL, SSD_HEADS, SSD_HEADS, SSD_HEADS, ATT_QH)
SMALL_OFFS = tuple(D_XBC + sum(-(-n // 128) * 128 for n in SMALL_SIZES[:k]) for k in range(len(SMALL_SIZES)))
LOSS_OFF = D_XBC + sum(-(-n // 128) * 128 for n in SMALL_SIZES)
K_SMALL = LOSS_OFF + 128


def _pack_small(g_conv_w, vecs, loss):
    def body(cw_ref, *refs):
        o_ref = refs[-1]
        o_ref[...] = jnp.zeros_like(o_ref)
        o_ref[0:CONV_K, 0:D_XBC] = cw_ref[...]
        for v_ref, off, n in zip(refs[:-2], SMALL_OFFS, SMALL_SIZES):
            o_ref[0:1, off:off + n] = v_ref[...]
        o_ref[0:1, LOSS_OFF:LOSS_OFF + 128] = refs[-2][...]

    return pl.pallas_call(
        body, name="pack_small", out_shape=jax.ShapeDtypeStruct((8, K_SMALL), F32), compiler_params=_cparams(),
    )(g_conv_w, *vecs, loss)


def _adamw_small(slots, chip, conv_w, m_conv_w, v_conv_w, params, moms, vars_):
    n_vec = len(SMALL_NAMES)

    def body(chip_ref, s_ref, *refs):
        ins = refs[:3 * (n_vec + 1)]
        outs = refs[3 * (n_vec + 1):-1]
        tot_ref = refs[-1]
        tot = s_ref[0]
        for d in range(1, 8):
            tot = tot + s_ref[d]
        outs[0][...] = tot[0:1, LOSS_OFF:LOSS_OFF + 1]
        off = pl.multiple_of(chip_ref[0] * CONV_COLS, 128)
        tot_ref[...] = tot
        grads = [tot_ref[0:CONV_K, pl.ds(off, CONV_COLS)]]
        grads += [tot[0:1, o:o + n] for o, n in zip(SMALL_OFFS, SMALL_SIZES)]
        for k, g in enumerate(grads):
            w_ref, m_ref, v_ref = ins[3 * k:3 * k + 3]
            full = (0,) if k == 0 else (Ellipsis,)
            d, nm, nv = _adamw_math(w_ref[full], g, m_ref[full], v_ref[full])
            for o_ref, val in zip(outs[1 + 4 * k:5 + 4 * k], (g, d, nm, nv)):
                o_ref[full] = val

    args = [conv_w, m_conv_w, v_conv_w]
    for w, m, v in zip(params, moms, vars_):
        args += [w, m, v]
    shapes = [jax.ShapeDtypeStruct((1, 1), F32)] + [jax.ShapeDtypeStruct(conv_w.shape, F32)] * 4
    for w in params:
        shapes += [jax.ShapeDtypeStruct(w.shape, F32)] * 4
    vmem = pl.BlockSpec(memory_space=pltpu.VMEM)
    return pl.pallas_call(
        body, name="adamw_small",
        grid_spec=pltpu.PrefetchScalarGridSpec(
            num_scalar_prefetch=1, grid=(1,),
            in_specs=[pl.BlockSpec(slots.shape, lambda i, chip: (0, 0, 0))] + [vmem] * len(args),
            out_specs=[vmem] * len(shapes), scratch_shapes=[pltpu.VMEM((8, K_SMALL), F32)]),
        out_shape=shapes, compiler_params=_cparams(),
    )(chip, slots, *args)


def kernel(x, positions, w_in, conv_w, conv_b, dt_bias, a_log, d_skip, ssd_norm_w, attn_sinks, w_out, ln_g, ln_b, loss_target, m_w_in, m_conv_w, m_conv_b, m_dt_bias, m_a_log, m_d_skip, m_ssd_norm_w, m_attn_sinks, m_w_out, m_ln_g, m_ln_b, v_w_in, v_conv_w, v_conv_b, v_dt_bias, v_a_log, v_d_skip, v_ssd_norm_w, v_attn_sinks, v_w_out, v_ln_g, v_ln_b):
    mx, my, mc = _mesh_pos()
    chip = 2 * mx + my
    L = x.shape[1]

    conv_w_s8 = jnp.pad(conv_w[0], ((0, 8 - CONV_K), (0, 0)))
    pad_rows = ((0, SLAB_ROWS - W_IN_COLS), (0, 0))
    w_in_t = w_in[0].T
    w_in_b, w_out_b = jnp.pad(_bf(w_in_t), pad_rows), _bf(w_out[0])
    ag_in, ag_out, ag_cw = _gather_weights(w_in_b, w_out_b, conv_w_s8)
    ag_in = jnp.where((jnp.arange(N_CHIPS) == chip)[:, None, None], w_in_b[None], ag_in)
    ag_out = jnp.where((jnp.arange(N_CHIPS) == chip)[:, None, None], w_out_b[None], ag_out)
    w_full = jnp.concatenate([ag_in[j, 0:W_IN_COLS] for j in range(N_CHIPS)], axis=0)
    w = jnp.concatenate([
        w_full[O_Z:O_Z + D_SSD], w_full[O_G:O_G + D_ATT], w_full[O_Q:O_Q + D_ATT],
        w_full[O_XBC:O_XBC + D_XBC], w_full[O_K:O_K + 2 * D_KV], w_full[O_DT:O_DT + SSD_HEADS],
        jnp.zeros((DT_PAD - SSD_HEADS, D_MODEL), BF16)], axis=0)
    w_out_full = ag_out.reshape(D_MIX, D_MODEL)
    conv_w_full = jnp.concatenate([ag_cw[j, 0:CONV_K] for j in range(N_CHIPS)], axis=1)

    loss_part, grad_x, gw_in, gw_out, small = _local_step(
        x[0], positions[0].reshape(L, 1), loss_target[0], w, w_out_full, conv_w_full, conv_b, dt_bias, a_log, d_skip,
        ssd_norm_w, attn_sinks, ln_g, ln_b)

    packed = _pack_small(small["conv_w"], [small[n] for n in SMALL_NAMES], loss_part)

    gw_in_slabs = jnp.stack([jnp.pad(gw_in[W_IN_COLS * j:W_IN_COLS * (j + 1)], pad_rows) for j in range(N_CHIPS)])
    gw_out_slabs = gw_out.reshape(N_CHIPS, W_OUT_ROWS, D_MODEL)
    core_id = mc.reshape(1).astype(jnp.int32)
    chip_id = chip.reshape(1).astype(jnp.int32)
    recv_in, recv_out, slots = _pair_exchange(gw_in_slabs, gw_out_slabs, packed)
    s_in = _pair_add(gw_in_slabs, recv_in, core_id, "pair_add_in")
    s_out = _pair_add(gw_out_slabs, recv_out, core_id, "pair_add_out")
    r_in, r_out = _chip_exchange(s_in, s_out)
    h_in = _chip_add(s_in, r_in, chip_id, "chip_add_in")
    h_out = _chip_add(s_out, r_out, chip_id, "chip_add_out")
    sib_in, sib_out = _pair_share(h_in, h_out)

    to_rows = lambda a: jnp.transpose(a, (2, 0, 1))
    in_t = _adamw_pair(to_rows(w_in), h_in, sib_in, to_rows(m_w_in), to_rows(v_w_in), core_id, "adamw_w_in")
    d_w_in, nm_w_in, nv_w_in, g_w_in = [jnp.transpose(a, (1, 2, 0)) for a in in_t]
    out_t = _adamw_pair(w_out[0], h_out, sib_out, m_w_out[0], v_w_out[0], core_id, "adamw_w_out")
    d_w_out, nm_w_out, nv_w_out, g_w_out = [a[None] for a in out_t]

    params = dict(conv_b=conv_b, ssd_norm_w=ssd_norm_w, ln_g=ln_g, ln_b=ln_b, dt_bias=dt_bias, a_log=a_log,
                  d_skip=d_skip, attn_sinks=attn_sinks)
    moms = dict(conv_b=m_conv_b, ssd_norm_w=m_ssd_norm_w, ln_g=m_ln_g, ln_b=m_ln_b, dt_bias=m_dt_bias, a_log=m_a_log,
                d_skip=m_d_skip, attn_sinks=m_attn_sinks)
    vars_ = dict(conv_b=v_conv_b, ssd_norm_w=v_ssd_norm_w, ln_g=v_ln_g, ln_b=v_ln_b, dt_bias=v_dt_bias, a_log=v_a_log,
                 d_skip=v_d_skip, attn_sinks=v_attn_sinks)
    res = _adamw_small(slots, chip_id, conv_w, m_conv_w, v_conv_w, [params[n] for n in SMALL_NAMES],
                       [moms[n] for n in SMALL_NAMES], [vars_[n] for n in SMALL_NAMES])
    loss = res[0][0, 0]
    grads, delta, new_m, new_v = {}, {}, {}, {}
    for k, n in enumerate(("conv_w",) + SMALL_NAMES):
        grads[n], delta[n], new_m[n], new_v[n] = res[1 + 4 * k:5 + 4 * k]
    for dd, a_in, a_out in ((grads, g_w_in, g_w_out), (delta, d_w_in, d_w_out), (new_m, nm_w_in, nm_w_out),
                            (new_v, nv_w_in, nv_w_out)):
        dd["w_in"] = a_in
        dd["w_out"] = a_out
    order = ("w_in", "conv_w", "conv_b", "dt_bias", "a_log", "d_skip", "ssd_norm_w", "attn_sinks", "w_out", "ln_g", "ln_b")
    return (loss, grad_x[None], *[grads[n] for n in order], *[delta[n] for n in order], *[new_m[n] for n in order],
            *[new_v[n] for n in order])
```

```python
import functools

import numpy as np
import jax
import jax.numpy as jnp
from jax import lax
from jax.experimental import pallas as pl
from jax.experimental.pallas import tpu as pltpu

F32 = jnp.float32
BF16 = jnp.bfloat16
MESH = pl.DeviceIdType.MESH

D_MODEL = 1024
D_SSD = 1024
D_ATT = 1024
D_MIX = 2048
SSD_HEADS = 16
SSD_P = 64
SSD_GROUPS = 2
SSD_R = 8
SSD_N = 128
D_BC = 256
D_XBC = 1536
CONV_K = 4
CHUNK = 128
ATT_HD = 64
ATT_QH = 16
ATT_KVH = 4
ATT_R = 4
D_KV = 256
WINDOW = 128
ROPE_THETA = 500000.0
ROPE_DIM = 16
ALPHA = 2.0 ** 0.25
LN_EPS = 1e-5
RMS_EPS = 1e-5
D_IN_PROJ = 5136
O_Z, O_XBC, O_DT, O_Q, O_K, O_V, O_G = 0, 1024, 2560, 2576, 3600, 3856, 4112
P_Z, P_G, P_Q, P_XBC, P_KV, P_DT, P_END = 0, 1024, 2048, 3072, 4608, 5120, 5248
DT_PAD = 128
N_CHIPS = 4
W_IN_COLS = D_IN_PROJ // N_CHIPS
SLAB_ROWS = 1312
W_OUT_ROWS = D_MIX // N_CHIPS
CONV_COLS = D_XBC // N_CHIPS

ADAM_LR = 0.001
ADAM_B1 = 0.9
ADAM_B2 = 0.999
ADAM_EPS = 1e-08
ADAM_WD = 0.01
ADAM_STEP = 10

VMEM_LIMIT = 56 * 1024 * 1024
ROW_TILE = 512
NEG_BIG = -1e30
HI = lax.Precision.HIGHEST


def _cparams(sem=None, **kw):
    if sem is not None:
        kw["dimension_semantics"] = sem
    return pltpu.CompilerParams(vmem_limit_bytes=VMEM_LIMIT, **kw)


def _dot(a, b):
    return jnp.dot(a, b, preferred_element_type=F32)


def _dot_nt(a, b):
    return lax.dot_general(a, b, (((1,), (1,)), ((), ())), preferred_element_type=F32)


def _dot_tn(a, b):
    return lax.dot_general(a, b, (((0,), (0,)), ((), ())), preferred_element_type=F32)


def _bf(a):
    return a.astype(BF16)


def _iota2(shape, dim):
    return lax.broadcasted_iota(jnp.int32, shape, dim)


def _to_rows(col):
    k = col.shape[1]
    eye = (_iota2((k, k), 0) == _iota2((k, k), 1)).astype(F32)
    return lax.dot_general(eye, col, (((1,), (1,)), ((), ())), preferred_element_type=F32, precision=HI)


def _to_cols(row):
    n = row.shape[1]
    eye = (_iota2((n, n), 0) == _iota2((n, n), 1)).astype(F32)
    return lax.dot_general(eye, row, (((1,), (1,)), ((), ())), preferred_element_type=F32, precision=HI)


def _sigmoid(x):
    return jax.nn.sigmoid(x)


def _in_proj(x, w, pos, inv):
    L = x.shape[0]
    tm = ROW_TILE
    widths = (D_SSD, D_ATT, D_ATT, D_XBC, 2 * D_KV, DT_PAD)

    def body(x_ref, w_ref, pos_ref, inv_ref, z_ref, g_ref, q_ref, xbc_ref, kv_ref, dt_ref, xb_ref):
        xb = _bf(x_ref[...])
        xb_ref[...] = xb
        for o_ref, off, wd in zip((z_ref, g_ref, xbc_ref, dt_ref), (P_Z, P_G, P_XBC, P_DT), (D_SSD, D_ATT, D_XBC, DT_PAD)):
            o_ref[...] = _dot_nt(xb, w_ref[off:off + wd, :])
        tabs = _rope_tables(pos_ref, inv_ref)
        q_ref[...] = _bf(_rope(_dot_nt(xb, w_ref[P_Q:P_Q + D_ATT, :]), tabs))
        kv_ref[:, 0:D_KV] = _bf(_rope(_dot_nt(xb, w_ref[P_KV:P_KV + D_KV, :]), tabs))
        kv_ref[:, D_KV:2 * D_KV] = _bf(_dot_nt(xb, w_ref[P_KV + D_KV:P_KV + 2 * D_KV, :]))

    row = lambda wd: pl.BlockSpec((tm, wd), lambda i: (i, 0))
    return pl.pallas_call(
        body, name="in_proj", grid=(L // tm,),
        in_specs=[row(D_MODEL), pl.BlockSpec((P_END, D_MODEL), lambda i: (0, 0), pipeline_mode=pl.Buffered(1)), row(1),
                  pl.BlockSpec((1, 2 * ATT_HD), lambda i: (0, 0))],
        out_specs=[row(wd) for wd in widths] + [row(D_MODEL)],
        out_shape=[jax.ShapeDtypeStruct((L, wd), dt) for wd, dt in zip(widths, (F32, F32, BF16, F32, BF16, F32))]
        + [jax.ShapeDtypeStruct((L, D_MODEL), BF16)],
        compiler_params=_cparams(("parallel",)),
    )(x, w, pos, inv)


def _matmuls_tn(a_list, b, name):
    K, N = b.shape
    tk = min(K, 1024)
    n = len(a_list)

    def body(*refs):
        b_ref = refs[n]
        k = pl.program_id(0)
        bb = _bf(b_ref[...])
        for a_ref, o_ref in zip(refs[:n], refs[n + 1:]):
            part = _dot_tn(_bf(a_ref[...]), bb)

            @pl.when(k == 0)
            def _():
                o_ref[...] = part

            @pl.when(k > 0)
            def _():
                o_ref[...] += part

    return pl.pallas_call(
        body, name=name, grid=(K // tk,),
        in_specs=[pl.BlockSpec((tk, a.shape[1]), lambda k: (k, 0)) for a in a_list] + [pl.BlockSpec((tk, N), lambda k: (k, 0))],
        out_specs=[pl.BlockSpec((a.shape[1], N), lambda k: (0, 0)) for a in a_list],
        out_shape=[jax.ShapeDtypeStruct((a.shape[1], N), F32) for a in a_list],
        compiler_params=_cparams(("arbitrary",)),
    )(*a_list, b)


def _grad_x(dr, dz, dg, dq, dxbc, dkv, ddt, w):
    L = dr.shape[0]
    tm = ROW_TILE
    widths = (D_SSD, D_ATT, D_ATT, D_XBC, 2 * D_KV, DT_PAD)
    offs = (P_Z, P_G, P_Q, P_XBC, P_KV, P_DT)

    def body(dr_ref, dz_ref, dg_ref, dq_ref, dxbc_ref, dkv_ref, ddt_ref, w_ref, o_ref):
        acc = ALPHA * dr_ref[...]
        for p_ref, off, wd in zip((dz_ref, dg_ref, dq_ref, dxbc_ref, dkv_ref, ddt_ref), offs, widths):
            acc = acc + _dot(_bf(p_ref[...]), w_ref[off:off + wd, :])
        o_ref[...] = acc

    row = lambda wd: pl.BlockSpec((tm, wd), lambda i: (i, 0))
    return pl.pallas_call(
        body, name="grad_x", grid=(L // tm,),
        in_specs=[row(D_MODEL)] + [row(wd) for wd in widths] + [pl.BlockSpec((P_END, D_MODEL), lambda i: (0, 0), pipeline_mode=pl.Buffered(1))],
        out_specs=row(D_MODEL),
        out_shape=jax.ShapeDtypeStruct((L, D_MODEL), F32),
        compiler_params=_cparams(("parallel",)),
    )(dr, dz, dg, dq, dxbc, dkv, ddt, w)


def _ssd_chunk_pre(first, xbc_ref, tail_ref, dt_ref, cw_ref, cb_ref, dtb_ref, alog_ref, ext):
    tail = jnp.where(first, 0.0, tail_ref[...])
    ext[0:8, :] = tail
    ext[8:8 + CHUNK, :] = xbc_ref[...]
    u = cb_ref[...] + cw_ref[0:1, :] * ext[pl.ds(5, CHUNK), :]
    for k in range(1, CONV_K):
        u = u + cw_ref[k:k + 1, :] * ext[pl.ds(5 + k, CHUNK), :]
    sig = _sigmoid(u)
    xbc = u * sig
    dtraw = dt_ref[:, 0:SSD_HEADS] + dtb_ref[...]
    dt = jax.nn.softplus(dtraw)
    A = -jnp.exp(alog_ref[...])
    a = dt * A
    tril = (_iota2((CHUNK, CHUNK), 0) >= _iota2((CHUNK, CHUNK), 1)).astype(F32)
    acs = jnp.dot(tril, a, preferred_element_type=F32, precision=HI)
    acs_row = _to_rows(acs)
    return u, sig, xbc, dtraw, dt, A, acs, acs_row


HALO = 16


def _shift_matrix(offsets):
    n = CHUNK + HALO
    m = np.zeros((len(offsets) * CHUNK, 2 * n), np.float32)
    for k, off in enumerate(offsets):
        t = np.arange(CHUNK)
        m[k * CHUNK + t, t + off] = 1.0
        m[k * CHUNK + t, n + t + off] = 1.0
    return jnp.asarray(m, BF16)


def _shifted_rows(first_part, second_part, smat_ref):
    h1, l1 = _hi_lo(first_part)
    h2, l2 = _hi_lo(second_part)
    sh = _dot(smat_ref[...], jnp.concatenate([h1, h2, l1, l2], axis=0))
    return sh[0:CHUNK], sh[CHUNK:2 * CHUNK], sh[2 * CHUNK:3 * CHUNK]


def _ssd_chunk_pre2(first, xbc_ref, tail_ref, dt_ref, cw_ref, cb_ref, dtb_ref, alog_ref, smat_ref):
    tail = jnp.where(first, 0.0, tail_ref[...])
    x = xbc_ref[...]
    taps = _shifted_rows(tail, x, smat_ref) + (x,)
    u = cb_ref[...] + cw_ref[0:1, :] * taps[0]
    for k in range(1, CONV_K):
        u = u + cw_ref[k:k + 1, :] * taps[k]
    sig = _sigmoid(u)
    xbc = u * sig
    dtraw = dt_ref[:, 0:SSD_HEADS] + dtb_ref[...]
    dt = jax.nn.softplus(dtraw)
    A = -jnp.exp(alog_ref[...])
    a = dt * A
    tril = (_iota2((CHUNK, CHUNK), 0) >= _iota2((CHUNK, CHUNK), 1)).astype(F32)
    acs = jnp.dot(tril, a, preferred_element_type=F32, precision=HI)
    acs_row = _to_rows(acs)
    return u, sig, xbc, dtraw, dt, A, acs, acs_row, taps


def _ssd_fwd(z, xbc, dtp, conv_w, conv_b, dt_bias, a_log, d_skip, norm_w):
    L = z.shape[0]
    nc = L // CHUNK

    def body(z_ref, xbc_ref, tail_ref, dt_ref, cw_ref, cb_ref, dtb_ref, alog_ref, dsk_ref, nw_ref,
             y_ref, ypre_ref, prev_ref, state, ext, ybuf):
        c = pl.program_id(0)

        @pl.when(c == 0)
        def _():
            state[...] = jnp.zeros_like(state)

        u, sig, xbcv, dtraw, dt, A, acs, acs_row = _ssd_chunk_pre(
            c == 0, xbc_ref, tail_ref, dt_ref, cw_ref, cb_ref, dtb_ref, alog_ref, ext)
        prev_ref[0] = state[...]
        causal = _iota2((CHUNK, CHUNK), 0) >= _iota2((CHUNK, CHUNK), 1)
        alast = acs[CHUNK - 1:CHUNK, :]
        for g in range(SSD_GROUPS):
            Bg = _bf(xbcv[:, D_SSD + SSD_N * g:D_SSD + SSD_N * (g + 1)])
            Cg = _bf(xbcv[:, D_SSD + D_BC + SSD_N * g:D_SSD + D_BC + SSD_N * (g + 1)])
            cb = _dot_nt(Cg, Bg)
            for r in range(SSD_R):
                h = g * SSD_R + r
                hs = slice(SSD_P * h, SSD_P * (h + 1))
                acs_c = acs[:, h:h + 1]
                seg = acs_c - acs_row[h:h + 1, :]
                Lm = jnp.where(causal, jnp.exp(jnp.where(causal, seg, 0.0)), 0.0)
                M = cb * Lm
                xh = xbcv[:, hs]
                X = xh * dt[:, h:h + 1]
                prev_h = state[hs, :]
                ydiag = _dot(_bf(M), _bf(X))
                yoff = _dot_nt(Cg, _bf(prev_h)) * jnp.exp(acs_c)
                al = alast[:, h:h + 1]
                Xd = X * jnp.exp(al - acs_c)
                state[hs, :] = prev_h * jnp.exp(al) + _dot_tn(_bf(Xd), Bg)
                ybuf[:, hs] = ydiag + yoff + dsk_ref[:, h:h + 1] * xh
        y = ybuf[...]
        ypre_ref[...] = y
        zv = z_ref[...]
        yf = y * (zv * _sigmoid(zv))
        half = D_SSD // SSD_GROUPS
        for g in range(SSD_GROUPS):
            gs = slice(half * g, half * (g + 1))
            yg = yf[:, gs]
            ms = jnp.mean(yg * yg, axis=-1, keepdims=True)
            y_ref[:, gs] = _bf(yg * lax.rsqrt(ms + RMS_EPS) * nw_ref[:, gs])

    full = lambda shape: pl.BlockSpec(shape, lambda c: (0, 0))
    return pl.pallas_call(
        body, name="ssd_fwd", grid=(nc,),
        in_specs=[
            pl.BlockSpec((CHUNK, D_SSD), lambda c: (c, 0)),
            pl.BlockSpec((CHUNK, D_XBC), lambda c: (c, 0)),
            pl.BlockSpec((8, D_XBC), lambda c: (jnp.maximum(c * (CHUNK // 8) - 1, 0), 0)),
            pl.BlockSpec((CHUNK, DT_PAD), lambda c: (c, 0)),
            full((CONV_K, D_XBC)), full((1, D_XBC)), full((1, SSD_HEADS)), full((1, SSD_HEADS)), full((1, SSD_HEADS)),
            full((1, D_SSD)),
        ],
        out_specs=[
            pl.BlockSpec((CHUNK, D_SSD), lambda c: (c, 0)),
            pl.BlockSpec((CHUNK, D_SSD), lambda c: (c, 0)),
            pl.BlockSpec((1, SSD_HEADS * SSD_P, SSD_N), lambda c: (c, 0, 0)),
        ],
        out_shape=[
            jax.ShapeDtypeStruct((L, D_SSD), F32),
            jax.ShapeDtypeStruct((L, D_SSD), F32),
            jax.ShapeDtypeStruct((nc, SSD_HEADS * SSD_P, SSD_N), F32),
        ],
        scratch_shapes=[
            pltpu.VMEM((SSD_HEADS * SSD_P, SSD_N), F32),
            pltpu.VMEM((CHUNK + 8, D_XBC), F32),
            pltpu.VMEM((CHUNK, D_SSD), F32),
        ],
        compiler_params=_cparams(("arbitrary",)),
    )(z, xbc, xbc, dtp, conv_w, conv_b, dt_bias, a_log, d_skip, norm_w)


def _ssd_bwd(dy, z, ypre, xbc, dtp, prev, conv_w, conv_b, dt_bias, a_log, d_skip, norm_w):
    L = z.shape[0]
    nc = L // CHUNK

    def body(dy_ref, z_ref, ypre_ref, xbc_ref, tail_ref, dt_ref, prev_ref, cw_ref, cb_ref, dtb_ref, alog_ref, dsk_ref,
             nw_ref, dz_ref, dxbc_ref, ddt_ref, gcw_ref, gcb_ref, gdtb_ref, galog_ref, gdsk_ref, gnw_ref,
             dstate, dhead, ext, ext2, dpost):
        i = pl.program_id(0)
        c = nc - 1 - i

        @pl.when(i == 0)
        def _():
            dstate[...] = jnp.zeros_like(dstate)
            dhead[...] = jnp.zeros_like(dhead)
            gcw_ref[...] = jnp.zeros_like(gcw_ref)
            gcb_ref[...] = jnp.zeros_like(gcb_ref)
            gdtb_ref[...] = jnp.zeros_like(gdtb_ref)
            galog_ref[...] = jnp.zeros_like(galog_ref)
            gdsk_ref[...] = jnp.zeros_like(gdsk_ref)
            gnw_ref[...] = jnp.zeros_like(gnw_ref)

        u, sig, xbcv, dtraw, dt, A, acs, acs_row = _ssd_chunk_pre(
            c == 0, xbc_ref, tail_ref, dt_ref, cw_ref, cb_ref, dtb_ref, alog_ref, ext)

        zv = z_ref[...]
        ypre = ypre_ref[...]
        dyn = dy_ref[...]
        sz = _sigmoid(zv)
        silu_z = zv * sz
        yf = ypre * silu_z
        half = D_SSD // SSD_GROUPS
        dyf_parts = []
        for g in range(SSD_GROUPS):
            gs = slice(half * g, half * (g + 1))
            yg = yf[:, gs]
            rstd = lax.rsqrt(jnp.mean(yg * yg, axis=-1, keepdims=True) + RMS_EPS)
            dout = dyn[:, gs]
            gnw_ref[:, gs] += jnp.sum(dout * yg * rstd, axis=0, keepdims=True)
            dyhat = dout * nw_ref[:, gs]
            dyf_parts.append(rstd * (dyhat - yg * (rstd * rstd) * jnp.mean(dyhat * yg, axis=-1, keepdims=True)))
        dyf = jnp.concatenate(dyf_parts, axis=1)
        dz_ref[...] = _bf(dyf * ypre * (sz * (1.0 + zv * (1.0 - sz))))
        dypre = dyf * silu_z

        causal = _iota2((CHUNK, CHUNK), 0) >= _iota2((CHUNK, CHUNK), 1)
        alast = acs[CHUNK - 1:CHUNK, :]
        lane16 = _iota2((1, SSD_HEADS), 1)
        sub16 = _iota2((SSD_HEADS, 1), 0)
        dacs_col = jnp.zeros((CHUNK, SSD_HEADS), F32)
        dacs_row = jnp.zeros((SSD_HEADS, CHUNK), F32)
        ddt_col = jnp.zeros((CHUNK, SSD_HEADS), F32)
        dalast = jnp.zeros((1, SSD_HEADS), F32)
        gdsk = jnp.zeros((1, SSD_HEADS), F32)
        for g in range(SSD_GROUPS):
            bs = slice(D_SSD + SSD_N * g, D_SSD + SSD_N * (g + 1))
            cs = slice(D_SSD + D_BC + SSD_N * g, D_SSD + D_BC + SSD_N * (g + 1))
            Bg = _bf(xbcv[:, bs])
            Cg = _bf(xbcv[:, cs])
            cb = _dot_nt(Cg, Bg)
            dcb = jnp.zeros((CHUNK, CHUNK), F32)
            dB = jnp.zeros((CHUNK, SSD_N), F32)
            dC = jnp.zeros((CHUNK, SSD_N), F32)
            for r in range(SSD_R):
                h = g * SSD_R + r
                hs = slice(SSD_P * h, SSD_P * (h + 1))
                onehot = (lane16 == h).astype(F32)
                acs_c = acs[:, h:h + 1]
                seg = acs_c - acs_row[h:h + 1, :]
                Lm = jnp.where(causal, jnp.exp(jnp.where(causal, seg, 0.0)), 0.0)
                M = cb * Lm
                xh = xbcv[:, hs]
                dth = dt[:, h:h + 1]
                X = xh * dth
                Xb = _bf(X)
                dyh = dypre[:, hs]
                dyb = _bf(dyh)
                prev_h = prev_ref[0, hs, :]
                prevb = _bf(prev_h)
                dnext = dstate[hs, :]
                dnextb = _bf(dnext)
                al = alast[:, h:h + 1]
                eacs = jnp.exp(acs_c)
                eal = jnp.exp(al)
                dsd = jnp.exp(al - acs_c)
                G = _bf(dyh * eacs)
                dstate[hs, :] = dnext * eal + _dot_tn(G, Cg)
                dC = dC + _dot(G, prevb)
                yoff = _dot_nt(Cg, prevb) * eacs
                dacs_h = jnp.sum(dyh * yoff, axis=-1, keepdims=True)
                BdN = _dot_nt(Bg, dnextb)
                dX = dsd * BdN
                dB = dB + _dot(_bf(X * dsd), dnextb)
                t = jnp.sum(X * BdN, axis=-1, keepdims=True) * dsd
                dacs_h = dacs_h - t
                dal = jnp.sum(t, axis=0, keepdims=True) + jnp.sum(
                    jnp.sum(dnext * prev_h, axis=-1, keepdims=True), axis=0, keepdims=True) * eal
                dM = _dot_nt(dyb, Xb)
                dX = dX + _dot_tn(_bf(M), dyb)
                dseg = dM * M
                dcb = dcb + dM * Lm
                dacs_h = dacs_h + jnp.sum(dseg, axis=-1, keepdims=True)
                dacs_row = dacs_row - jnp.sum(dseg, axis=0, keepdims=True) * (sub16 == h).astype(F32)
                dacs_col = dacs_col + dacs_h * onehot
                dalast = dalast + dal * onehot
                ddt_col = ddt_col + jnp.sum(dX * xh, axis=-1, keepdims=True) * onehot
                gdsk = gdsk + jnp.sum(jnp.sum(dyh * xh, axis=-1, keepdims=True), axis=0, keepdims=True) * onehot
                dpost[:, hs] = dX * dth + dsk_ref[:, h:h + 1] * dyh
            dcbb = _bf(dcb)
            dpost[:, bs] = dB + _dot_tn(dcbb, Cg)
            dpost[:, cs] = dC + _dot(dcbb, Bg)

        is_last = (_iota2((CHUNK, 1), 0) == CHUNK - 1).astype(F32)
        dacs = dacs_col + _to_cols(dacs_row) + is_last * dalast
        triu = (_iota2((CHUNK, CHUNK), 0) <= _iota2((CHUNK, CHUNK), 1)).astype(F32)
        da = jnp.dot(triu, dacs, preferred_element_type=F32, precision=HI)
        ddt_tot = ddt_col + da * A
        galog_ref[...] += jnp.sum(da * dt, axis=0, keepdims=True) * A
        ddtraw = ddt_tot * _sigmoid(dtraw)
        gdtb_ref[...] += jnp.sum(ddtraw, axis=0, keepdims=True)
        gdsk_ref[...] += gdsk
        ddt_ref[...] = jnp.zeros_like(ddt_ref)
        ddt_ref[:, 0:SSD_HEADS] = ddtraw

        dconv = dpost[...] * (sig * (1.0 + u * (1.0 - sig)))
        gcb_ref[...] += jnp.sum(dconv, axis=0, keepdims=True)
        for k in range(CONV_K):
            gcw_ref[k:k + 1, :] += jnp.sum(dconv * ext[pl.ds(5 + k, CHUNK), :], axis=0, keepdims=True)
        ext2[0:CHUNK, :] = dconv
        ext2[CHUNK:CHUNK + 8, :] = dhead[...]
        dx = cw_ref[CONV_K - 1:CONV_K, :] * dconv
        for k in range(CONV_K - 1):
            dx = dx + cw_ref[k:k + 1, :] * ext2[pl.ds(CONV_K - 1 - k, CHUNK), :]
        dxbc_ref[...] = _bf(dx)
        dhead[...] = dconv[0:8, :]

    full = lambda shape: pl.BlockSpec(shape, lambda i: (0, 0))
    rev = lambda wd: pl.BlockSpec((CHUNK, wd), lambda i: (nc - 1 - i, 0))
    return pl.pallas_call(
        body, name="ssd_bwd", grid=(nc,),
        in_specs=[
            rev(D_SSD), rev(D_SSD), rev(D_SSD), rev(D_XBC),
            pl.BlockSpec((8, D_XBC), lambda i: (jnp.maximum((nc - 1 - i) * (CHUNK // 8) - 1, 0), 0)),
            rev(DT_PAD),
            pl.BlockSpec((1, SSD_HEADS * SSD_P, SSD_N), lambda i: (nc - 1 - i, 0, 0)),
            full((CONV_K, D_XBC)), full((1, D_XBC)), full((1, SSD_HEADS)), full((1, SSD_HEADS)), full((1, SSD_HEADS)),
            full((1, D_SSD)),
        ],
        out_specs=[
            rev(D_SSD), rev(D_XBC), rev(DT_PAD),
            full((CONV_K, D_XBC)), full((1, D_XBC)), full((1, SSD_HEADS)), full((1, SSD_HEADS)), full((1, SSD_HEADS)),
            full((1, D_SSD)),
        ],
        out_shape=[
            jax.ShapeDtypeStruct((L, D_SSD), BF16), jax.ShapeDtypeStruct((L, D_XBC), BF16),
            jax.ShapeDtypeStruct((L, DT_PAD), F32),
            jax.ShapeDtypeStruct((CONV_K, D_XBC), F32), jax.ShapeDtypeStruct((1, D_XBC), F32),
            jax.ShapeDtypeStruct((1, SSD_HEADS), F32), jax.ShapeDtypeStruct((1, SSD_HEADS), F32),
            jax.ShapeDtypeStruct((1, SSD_HEADS), F32), jax.ShapeDtypeStruct((1, D_SSD), F32),
        ],
        scratch_shapes=[
            pltpu.VMEM((SSD_HEADS * SSD_P, SSD_N), F32),
            pltpu.VMEM((8, D_XBC), F32),
            pltpu.VMEM((CHUNK + 8, D_XBC), F32),
            pltpu.VMEM((CHUNK + 8, D_XBC), F32),
            pltpu.VMEM((CHUNK, D_XBC), F32),
        ],
        compiler_params=_cparams(("arbitrary",)),
    )(dy, z, ypre, xbc, xbc, dtp, prev, conv_w, conv_b, dt_bias, a_log, d_skip, norm_w)


def _head_expander():
    return (_iota2((SSD_HEADS, D_SSD), 1) // SSD_P == _iota2((SSD_HEADS, D_SSD), 0)).astype(BF16)


def _hi_lo(x):
    hi = _bf(x)
    return hi, _bf(x - hi.astype(F32))


def _expand(v, e):
    hi, lo = _hi_lo(v)
    return _dot(hi, e) + _dot(lo, e)


def _headsum(t, e):
    m = t.shape[0]
    if m < 8:
        t = jnp.broadcast_to(t[0:1], (8, t.shape[1]))
    hi, lo = _hi_lo(t)
    return (_dot_nt(hi, e) + _dot_nt(lo, e))[0:m]


def _ssd_decays(dt, acs, dsk_ref, e):
    alast = acs[CHUNK - 1:CHUNK, :]
    stk = jnp.concatenate([dt, jnp.exp(acs), jnp.exp(alast - acs),
                           jnp.broadcast_to(jnp.exp(alast), (8, SSD_HEADS)),
                           jnp.broadcast_to(dsk_ref[...], (8, SSD_HEADS))], axis=0)
    ex = _expand(stk, e)
    return (ex[0:CHUNK], ex[CHUNK:2 * CHUNK], ex[2 * CHUNK:3 * CHUNK], ex[3 * CHUNK:3 * CHUNK + 1],
            ex[3 * CHUNK + 8:3 * CHUNK + 9])


def _ssd_fwd2(z, xbc, dtp, conv_w, conv_b, dt_bias, a_log, d_skip, norm_w):
    L = z.shape[0]
    nc = L // CHUNK
    half = D_SSD // SSD_GROUPS

    def body(z_ref, xbc_ref, tail_ref, dt_ref, cw_ref, cb_ref, dtb_ref, alog_ref, dsk_ref, nw_ref, smat_ref,
             y_ref, ypre_ref, prev_ref, state, ybuf, mbuf):
        c = pl.program_id(0)

        @pl.when(c == 0)
        def _():
            state[...] = jnp.zeros_like(state)

        u, sig, xbcv, dtraw, dt, A, acs, acs_row, _ = _ssd_chunk_pre2(
            c == 0, xbc_ref, tail_ref, dt_ref, cw_ref, cb_ref, dtb_ref, alog_ref, smat_ref)
        e = _head_expander()
        dtE, eacsE, dsdE, ealE, dskE = _ssd_decays(dt, acs, dsk_ref, e)
        xs = xbcv[:, 0:D_SSD]
        X = xs * dtE
        prev_ref[0] = state[...]
        causal = _iota2((CHUNK, CHUNK), 0) >= _iota2((CHUNK, CHUNK), 1)
        for g in range(SSD_GROUPS):
            gs = slice(half * g, half * (g + 1))
            Bg = _bf(xbcv[:, D_SSD + SSD_N * g:D_SSD + SSD_N * (g + 1)])
            Cg = _bf(xbcv[:, D_SSD + D_BC + SSD_N * g:D_SSD + D_BC + SSD_N * (g + 1)])
            cb = _dot_nt(Cg, Bg)
            for r in range(SSD_R):
                h = g * SSD_R + r
                seg = acs[:, h:h + 1] - acs_row[h:h + 1, :]
                mbuf[h] = _bf(cb * jnp.where(causal, jnp.exp(jnp.where(causal, seg, 0.0)), 0.0))
            st = state[:, gs]
            ybuf[:, gs] = _dot(Cg, _bf(st)) * eacsE[:, gs] + dskE[:, gs] * xs[:, gs]
            state[:, gs] = st * ealE[:, gs] + _dot_tn(Bg, _bf(X[:, gs] * dsdE[:, gs]))
        Xb = _bf(X)
        for h in range(SSD_HEADS):
            hs = slice(SSD_P * h, SSD_P * (h + 1))
            ybuf[:, hs] += _dot(mbuf[h], Xb[:, hs])
        y = ybuf[...]
        ypre_ref[...] = y
        zv = z_ref[...]
        yf = y * (zv * _sigmoid(zv))
        for g in range(SSD_GROUPS):
            gs = slice(half * g, half * (g + 1))
            yg = yf[:, gs]
            ms = jnp.mean(yg * yg, axis=-1, keepdims=True)
            y_ref[:, gs] = _bf(yg * lax.rsqrt(ms + RMS_EPS) * nw_ref[:, gs])

    full = lambda shape: pl.BlockSpec(shape, lambda c: (0, 0))
    return pl.pallas_call(
        body, name="ssd_fwd", grid=(nc,),
        in_specs=[
            pl.BlockSpec((CHUNK, D_SSD), lambda c: (c, 0)),
            pl.BlockSpec((CHUNK, D_XBC), lambda c: (c, 0)),
            pl.BlockSpec((HALO, D_XBC), lambda c: (jnp.maximum(c * (CHUNK // HALO) - 1, 0), 0)),
            pl.BlockSpec((CHUNK, DT_PAD), lambda c: (c, 0)),
            full((CONV_K, D_XBC)), full((1, D_XBC)), full((1, SSD_HEADS)), full((1, SSD_HEADS)), full((1, SSD_HEADS)),
            full((1, D_SSD)), full((3 * CHUNK, 2 * (CHUNK + HALO))),
        ],
        out_specs=[
            pl.BlockSpec((CHUNK, D_SSD), lambda c: (c, 0)),
            pl.BlockSpec((CHUNK, D_SSD), lambda c: (c, 0)),
            pl.BlockSpec((1, SSD_N, D_SSD), lambda c: (c, 0, 0)),
        ],
        out_shape=[
            jax.ShapeDtypeStruct((L, D_SSD), BF16),
            jax.ShapeDtypeStruct((L, D_SSD), F32),
            jax.ShapeDtypeStruct((nc, SSD_N, D_SSD), F32),
        ],
        scratch_shapes=[
            pltpu.VMEM((SSD_N, D_SSD), F32),
            pltpu.VMEM((CHUNK, D_SSD), F32),
            pltpu.VMEM((SSD_HEADS, CHUNK, CHUNK), BF16),
        ],
        compiler_params=_cparams(("arbitrary",)),
    )(z, xbc, xbc, dtp, conv_w, conv_b, dt_bias, a_log, d_skip, norm_w, _shift_matrix((13, 14, 15)))


def _ssd_bwd2(dy, z, ypre, xbc, dtp, prev, conv_w, conv_b, dt_bias, a_log, d_skip, norm_w):
    L = z.shape[0]
    nc = L // CHUNK
    half = D_SSD // SSD_GROUPS

    def body(dy_ref, z_ref, ypre_ref, xbc_ref, tail_ref, dt_ref, prev_ref, cw_ref, cb_ref, dtb_ref, alog_ref, dsk_ref,
             nw_ref, smat_ref, smat2_ref, dz_ref, dxbc_ref, ddt_ref, gcw_ref, gcb_ref, gdtb_ref, galog_ref, gdsk_ref,
             gnw_ref, dstate, dhead, dpost, yobuf, bdbuf, lmbuf, dmbuf, cbbuf):
        i = pl.program_id(0)
        c = nc - 1 - i

        @pl.when(i == 0)
        def _():
            dstate[...] = jnp.zeros_like(dstate)
            dhead[...] = jnp.zeros_like(dhead)
            gcw_ref[...] = jnp.zeros_like(gcw_ref)
            gcb_ref[...] = jnp.zeros_like(gcb_ref)
            gdtb_ref[...] = jnp.zeros_like(gdtb_ref)
            galog_ref[...] = jnp.zeros_like(galog_ref)
            gdsk_ref[...] = jnp.zeros_like(gdsk_ref)
            gnw_ref[...] = jnp.zeros_like(gnw_ref)

        u, sig, xbcv, dtraw, dt, A, acs, acs_row, taps = _ssd_chunk_pre2(
            c == 0, xbc_ref, tail_ref, dt_ref, cw_ref, cb_ref, dtb_ref, alog_ref, smat_ref)
        e = _head_expander()
        dtE, eacsE, dsdE, ealE, dskE = _ssd_decays(dt, acs, dsk_ref, e)
        alast = acs[CHUNK - 1:CHUNK, :]
        xs = xbcv[:, 0:D_SSD]
        X = xs * dtE
        Xb = _bf(X)

        zv = z_ref[...]
        ypre = ypre_ref[...]
        dyn = dy_ref[...]
        sz = _sigmoid(zv)
        silu_z = zv * sz
        yf = ypre * silu_z
        dyf_parts = []
        for g in range(SSD_GROUPS):
            gs = slice(half * g, half * (g + 1))
            yg = yf[:, gs]
            rstd = lax.rsqrt(jnp.mean(yg * yg, axis=-1, keepdims=True) + RMS_EPS)
            dout = dyn[:, gs]
            gnw_ref[:, gs] += jnp.sum(dout * yg * rstd, axis=0, keepdims=True)
            dyhat = dout * nw_ref[:, gs]
            dyf_parts.append(rstd * (dyhat - yg * (rstd * rstd) * jnp.mean(dyhat * yg, axis=-1, keepdims=True)))
        dyf = jnp.concatenate(dyf_parts, axis=1)
        dz_ref[...] = _bf(dyf * ypre * (sz * (1.0 + zv * (1.0 - sz))))
        dyp = dyf * silu_z
        dyb = _bf(dyp)
        G = dyp * eacsE

        causal = _iota2((CHUNK, CHUNK), 0) >= _iota2((CHUNK, CHUNK), 1)
        ST = prev_ref[0]
        dST = dstate[...]
        for g in range(SSD_GROUPS):
            gs = slice(half * g, half * (g + 1))
            bs = slice(D_SSD + SSD_N * g, D_SSD + SSD_N * (g + 1))
            cs = slice(D_SSD + D_BC + SSD_N * g, D_SSD + D_BC + SSD_N * (g + 1))
            Bg = _bf(xbcv[:, bs])
            Cg = _bf(xbcv[:, cs])
            Gb = _bf(G[:, gs])
            STb = _bf(ST[:, gs])
            dSTb = _bf(dST[:, gs])
            dstate[:, gs] = dST[:, gs] * ealE[:, gs] + _dot_tn(Cg, Gb)
            yobuf[:, gs] = _dot(Cg, STb) * eacsE[:, gs]
            bdbuf[:, gs] = _dot(Bg, dSTb)
            dpost[:, cs] = _dot_nt(Gb, STb)
            dpost[:, bs] = _dot_nt(_bf(X[:, gs] * dsdE[:, gs]), dSTb)
            cbbuf[g] = _dot_nt(Cg, Bg)
            for r in range(SSD_R):
                h = g * SSD_R + r
                seg = acs[:, h:h + 1] - acs_row[h:h + 1, :]
                lmbuf[h] = jnp.where(causal, jnp.exp(jnp.where(causal, seg, 0.0)), 0.0)
        for h in range(SSD_HEADS):
            hs = slice(SSD_P * h, SSD_P * (h + 1))
            Mb = _bf(cbbuf[h // SSD_R] * lmbuf[h])
            dmbuf[h] = _dot_nt(dyb[:, hs], Xb[:, hs])
            dpost[:, hs] = _dot_tn(Mb, dyb[:, hs])
        lane16 = _iota2((1, SSD_HEADS), 1)
        sub16 = _iota2((SSD_HEADS, 1), 0)
        dacs_col = jnp.zeros((CHUNK, SSD_HEADS), F32)
        dacs_row = jnp.zeros((SSD_HEADS, CHUNK), F32)
        for g in range(SSD_GROUPS):
            bs = slice(D_SSD + SSD_N * g, D_SSD + SSD_N * (g + 1))
            cs = slice(D_SSD + D_BC + SSD_N * g, D_SSD + D_BC + SSD_N * (g + 1))
            cb = cbbuf[g]
            dcb = jnp.zeros((CHUNK, CHUNK), F32)
            for r in range(SSD_R):
                h = g * SSD_R + r
                dM = dmbuf[h]
                Lm = lmbuf[h]
                dcb = dcb + dM * Lm
                dseg = dM * (cb * Lm)
                dacs_col = dacs_col + jnp.sum(dseg, axis=-1, keepdims=True) * (lane16 == h).astype(F32)
                dacs_row = dacs_row - jnp.sum(dseg, axis=0, keepdims=True) * (sub16 == h).astype(F32)
            dcbb = _bf(dcb)
            dpost[:, bs] += _dot_tn(dcbb, _bf(xbcv[:, cs]))
            dpost[:, cs] += _dot(dcbb, _bf(xbcv[:, bs]))

        BD = bdbuf[...]
        dX = dpost[:, 0:D_SSD] + dsdE * BD
        dsd = jnp.exp(alast - acs)
        T = _headsum(X * BD, e) * dsd
        dalast = jnp.sum(T, axis=0, keepdims=True) + _headsum(
            jnp.sum(dST * ST, axis=0, keepdims=True), e) * jnp.exp(alast)
        is_last = (_iota2((CHUNK, 1), 0) == CHUNK - 1).astype(F32)
        dacs = dacs_col + _to_cols(dacs_row) + _headsum(dyp * yobuf[...], e) - T + is_last * dalast
        triu = (_iota2((CHUNK, CHUNK), 0) <= _iota2((CHUNK, CHUNK), 1)).astype(F32)
        da = jnp.dot(triu, dacs, preferred_element_type=F32, precision=HI)
        ddt_tot = _headsum(dX * xs, e) + da * A
        galog_ref[...] += jnp.sum(da * dt, axis=0, keepdims=True) * A
        ddtraw = ddt_tot * _sigmoid(dtraw)
        gdtb_ref[...] += jnp.sum(ddtraw, axis=0, keepdims=True)
        gdsk_ref[...] += _headsum(jnp.sum(dyp * xs, axis=0, keepdims=True), e)
        ddt_ref[...] = jnp.zeros_like(ddt_ref)
        ddt_ref[:, 0:SSD_HEADS] = ddtraw
        dpost[:, 0:D_SSD] = dX * dtE + dskE * dyp

        dconv = dpost[...] * (sig * (1.0 + u * (1.0 - sig)))
        gcb_ref[...] += jnp.sum(dconv, axis=0, keepdims=True)
        for k in range(CONV_K):
            gcw_ref[k:k + 1, :] += jnp.sum(dconv * taps[k], axis=0, keepdims=True)
        later = _shifted_rows(dconv, dhead[...], smat2_ref)
        dx = cw_ref[CONV_K - 1:CONV_K, :] * dconv
        for k in range(CONV_K - 1):
            dx = dx + cw_ref[k:k + 1, :] * later[k]
        dxbc_ref[...] = _bf(dx)
        dhead[...] = dconv[0:HALO, :]

    full = lambda shape: pl.BlockSpec(shape, lambda i: (0, 0))
    rev = lambda wd: pl.BlockSpec((CHUNK, wd), lambda i: (nc - 1 - i, 0))
    return pl.pallas_call(
        body, name="ssd_bwd", grid=(nc,),
        in_specs=[
            rev(D_SSD), rev(D_SSD), rev(D_SSD), rev(D_XBC),
            pl.BlockSpec((HALO, D_XBC), lambda i: (jnp.maximum((nc - 1 - i) * (CHUNK // HALO) - 1, 0), 0)),
            rev(DT_PAD),
            pl.BlockSpec((1, SSD_N, D_SSD), lambda i: (nc - 1 - i, 0, 0)),
            full((CONV_K, D_XBC)), full((1, D_XBC)), full((1, SSD_HEADS)), full((1, SSD_HEADS)), full((1, SSD_HEADS)),
            full((1, D_SSD)), full((3 * CHUNK, 2 * (CHUNK + HALO))), full((3 * CHUNK, 2 * (CHUNK + HALO))),
        ],
        out_specs=[
            rev(D_SSD), rev(D_XBC), rev(DT_PAD),
            full((CONV_K, D_XBC)), full((1, D_XBC)), full((1, SSD_HEADS)), full((1, SSD_HEADS)), full((1, SSD_HEADS)),
            full((1, D_SSD)),
        ],
        out_shape=[
            jax.ShapeDtypeStruct((L, D_SSD), BF16), jax.ShapeDtypeStruct((L, D_XBC), BF16),
            jax.ShapeDtypeStruct((L, DT_PAD), F32),
            jax.ShapeDtypeStruct((CONV_K, D_XBC), F32), jax.ShapeDtypeStruct((1, D_XBC), F32),
            jax.ShapeDtypeStruct((1, SSD_HEADS), F32), jax.ShapeDtypeStruct((1, SSD_HEADS), F32),
            jax.ShapeDtypeStruct((1, SSD_HEADS), F32), jax.ShapeDtypeStruct((1, D_SSD), F32),
        ],
        scratch_shapes=[
            pltpu.VMEM((SSD_N, D_SSD), F32),
            pltpu.VMEM((HALO, D_XBC), F32),
            pltpu.VMEM((CHUNK, D_XBC), F32),
            pltpu.VMEM((CHUNK, D_SSD), F32),
            pltpu.VMEM((CHUNK, D_SSD), F32),
            pltpu.VMEM((SSD_HEADS, CHUNK, CHUNK), F32),
            pltpu.VMEM((SSD_HEADS, CHUNK, CHUNK), F32),
            pltpu.VMEM((SSD_GROUPS, CHUNK, CHUNK), F32),
        ],
        compiler_params=_cparams(("arbitrary",)),
    )(dy, z, ypre, xbc, xbc, dtp, prev, conv_w, conv_b, dt_bias, a_log, d_skip, norm_w, _shift_matrix((13, 14, 15)),
      _shift_matrix((3, 2, 1)))


def _rope_tables(pos_ref, inv_ref):
    ang = pos_ref[...].astype(F32) * inv_ref[...]
    d = _iota2((1, 2 * ATT_HD), 1) % ATT_HD
    s = jnp.sin(ang)
    return jnp.cos(ang), jnp.where(d < ROPE_DIM // 2, -s, 0.0), jnp.where((d >= ROPE_DIM // 2) & (d < ROPE_DIM), s, 0.0)


def _rope(t, tabs):
    c, s1, s2 = tabs
    n = t.shape[1]
    rep = n // c.shape[1]
    return (t * jnp.tile(c, (1, rep)) + pltpu.roll(t, n - ROPE_DIM // 2, 1) * jnp.tile(s1, (1, rep))
            + pltpu.roll(t, ROPE_DIM // 2, 1) * jnp.tile(s2, (1, rep)))


def _rope_t(t, tabs):
    c, s1, s2 = tabs
    n = t.shape[1]
    rep = n // c.shape[1]
    return (t * jnp.tile(c, (1, rep)) + pltpu.roll(t * jnp.tile(s1, (1, rep)), ROPE_DIM // 2, 1)
            + pltpu.roll(t * jnp.tile(s2, (1, rep)), n - ROPE_DIM // 2, 1))


def _swa_mask(first):
    qi = _iota2((WINDOW, 2 * WINDOW), 0)
    si = _iota2((WINDOW, 2 * WINDOW), 1)
    band = (si > qi) & (si <= qi + WINDOW)
    return band & (jnp.logical_not(first) | (si >= WINDOW))


def _stack_heads(t, j):
    return jnp.concatenate([t[:, ATT_HD * (j * ATT_R + r):ATT_HD * (j * ATT_R + r + 1)] for r in range(ATT_R)], axis=0)


def _stack_cols(ref, j):
    cols = [jnp.broadcast_to(ref[:, j * ATT_R + r:j * ATT_R + r + 1], (WINDOW, 1)) for r in range(ATT_R)]
    return jnp.concatenate(cols, axis=0)


def _swa_mask_t(first):
    si = _iota2((2 * WINDOW, ATT_R * WINDOW), 0)
    qi = _iota2((2 * WINDOW, ATT_R * WINDOW), 1) % WINDOW
    band = (si > qi) & (si <= qi + WINDOW)
    return band & (jnp.logical_not(first) | (si >= WINDOW))


def _head_rows(ref, j, rows=None):
    if ref.shape[0] == 1:
        parts = [jnp.broadcast_to(ref[:, j * ATT_R + r:j * ATT_R + r + 1], (1, WINDOW)) for r in range(ATT_R)]
    else:
        parts = [ref[j * ATT_R + r:j * ATT_R + r + 1, :] for r in range(ATT_R)]
    return jnp.concatenate(parts, axis=1)


def _swa_fwd(q, g, kv, sinks):
    L = q.shape[0]
    nb = L // WINDOW
    scale = ATT_HD ** -0.5

    def body(q_ref, g_ref, kvc_ref, kvp_ref, sink_ref, y_ref, o_ref, lse_ref, otbuf):
        n = pl.program_id(0)
        kk = jnp.concatenate([kvp_ref[:, 0:D_KV], kvc_ref[:, 0:D_KV]], axis=0)
        vv = jnp.concatenate([kvp_ref[:, D_KV:2 * D_KV], kvc_ref[:, D_KV:2 * D_KV]], axis=0)
        valid = _swa_mask_t(n == 0)
        qv = q_ref[...]
        for j in range(ATT_KVH):
            js = slice(ATT_HD * j, ATT_HD * (j + 1))
            st = _dot_nt(kk[:, js], _stack_heads(qv, j)) * scale
            st = jnp.where(valid, st, NEG_BIG)
            sink = _head_rows(sink_ref, j)
            m = jnp.maximum(jnp.max(st, axis=0, keepdims=True), sink)
            p = jnp.exp(st - m)
            denom = jnp.sum(p, axis=0, keepdims=True) + jnp.exp(sink - m)
            ot = _dot_tn(vv[:, js], _bf(p)) * (1.0 / denom)
            lse = m + jnp.log(denom)
            for r in range(ATT_R):
                h = j * ATT_R + r
                otbuf[ATT_HD * h:ATT_HD * (h + 1), :] = ot[:, WINDOW * r:WINDOW * (r + 1)]
                lse_ref[h:h + 1, :] = lse[:, WINDOW * r:WINDOW * (r + 1)]
        o = otbuf[...].T
        o_ref[...] = o
        gv = g_ref[...]
        y_ref[...] = _bf(o * (gv * _sigmoid(gv)))

    cur = lambda wd: pl.BlockSpec((WINDOW, wd), lambda n: (n, 0))
    prv = lambda wd: pl.BlockSpec((WINDOW, wd), lambda n: (jnp.maximum(n - 1, 0), 0))
    return pl.pallas_call(
        body, name="swa_fwd", grid=(nb,),
        in_specs=[cur(D_ATT), cur(D_ATT), cur(2 * D_KV), prv(2 * D_KV), pl.BlockSpec((1, ATT_QH), lambda n: (0, 0))],
        out_specs=[cur(D_ATT), cur(D_ATT), pl.BlockSpec((ATT_QH, WINDOW), lambda n: (0, n))],
        out_shape=[jax.ShapeDtypeStruct((L, D_ATT), BF16), jax.ShapeDtypeStruct((L, D_ATT), F32),
                   jax.ShapeDtypeStruct((ATT_QH, L), F32)],
        scratch_shapes=[pltpu.VMEM((D_ATT, WINDOW), F32)],
        compiler_params=_cparams(("parallel",)),
    )(q, g, kv, kv, sinks)


def _swa_bwd(dy, q, g, kv, o, lse, pos, inv, sinks):
    L = q.shape[0]
    nb = L // WINDOW
    scale = ATT_HD ** -0.5

    def body(dy_ref, q_ref, g_ref, kvc_ref, kvp_ref, o_ref, lse_ref, posc_ref, posp_ref, inv_ref, sink_ref,
             dq_ref, dg_ref, dkv_ref, dsink_ref, carry, dqbuf, dkbuf, dvbuf):
        n = pl.program_id(0)

        @pl.when(n == 0)
        def _():
            dsink_ref[...] = jnp.zeros_like(dsink_ref)

        @pl.when(n < nb)
        def _():
            tc = _rope_tables(posc_ref, inv_ref)
            tp = _rope_tables(posp_ref, inv_ref)
            kk = jnp.concatenate([kvp_ref[:, 0:D_KV], kvc_ref[:, 0:D_KV]], axis=0)
            vv = jnp.concatenate([kvp_ref[:, D_KV:2 * D_KV], kvc_ref[:, D_KV:2 * D_KV]], axis=0)
            valid = _swa_mask_t(n == 0)
            qv = q_ref[...]
            gv = g_ref[...]
            sg = _sigmoid(gv)
            dyv = dy_ref[...]
            ov = o_ref[...]
            dg_ref[...] = _bf(dyv * ov * (sg * (1.0 + gv * (1.0 - sg))))
            do = dyv * (gv * sg)
            dod = do * ov
            ones = jnp.ones((8, ATT_HD), BF16)
            lane16 = _iota2((1, ATT_QH), 1)
            dsink = jnp.zeros((1, ATT_QH), F32)
            for j in range(ATT_KVH):
                js = slice(ATT_HD * j, ATT_HD * (j + 1))
                kj = kk[:, js]
                vj = vv[:, js]
                qs = _stack_heads(qv, j)
                dos = _bf(_stack_heads(do, j))
                hi, lo = _hi_lo(_stack_heads(dod, j))
                delta = (_dot_nt(ones, hi) + _dot_nt(ones, lo))[0:1]
                lse = _head_rows(lse_ref, j)
                st = _dot_nt(kj, qs) * scale
                pt = jnp.exp(jnp.where(valid, st, NEG_BIG) - lse)
                dst = _bf(pt * (_dot_nt(vj, dos) - delta))
                dqt = _dot_tn(kj, dst) * scale
                dkbuf[:, js] = _dot(dst, qs) * scale
                dvbuf[:, js] = _dot(_bf(pt), dos)
                sd = jnp.exp(_head_rows(sink_ref, j) - lse) * delta
                for r in range(ATT_R):
                    h = j * ATT_R + r
                    ls = slice(WINDOW * r, WINDOW * (r + 1))
                    dqbuf[ATT_HD * h:ATT_HD * (h + 1), :] = dqt[:, ls]
                    dsink = dsink - jnp.sum(sd[:, ls], axis=1, keepdims=True) * (lane16 == h).astype(F32)
            dsink_ref[...] += dsink
            dq_ref[...] = _bf(_rope_t(dqbuf[...].T, tc))
            dkp = _rope_t(dkbuf[0:WINDOW, :], tp)
            dkc = _rope_t(dkbuf[WINDOW:2 * WINDOW, :], tc)

            @pl.when(n > 0)
            def _():
                dkv_ref[:, 0:D_KV] = _bf(carry[:, 0:D_KV] + dkp)
                dkv_ref[:, D_KV:2 * D_KV] = _bf(carry[:, D_KV:2 * D_KV] + dvbuf[0:WINDOW, :])

            carry[:, 0:D_KV] = dkc
            carry[:, D_KV:2 * D_KV] = dvbuf[WINDOW:2 * WINDOW, :]

        @pl.when(n == nb)
        def _():
            dkv_ref[...] = _bf(carry[...])

    last = nb - 1
    cur = lambda wd: pl.BlockSpec((WINDOW, wd), lambda n: (jnp.minimum(n, last), 0))
    prv = lambda wd: pl.BlockSpec((WINDOW, wd), lambda n: (jnp.maximum(jnp.minimum(n, last) - 1, 0), 0))
    return pl.pallas_call(
        body, name="swa_bwd", grid=(nb + 1,),
        in_specs=[cur(D_ATT), cur(D_ATT), cur(D_ATT), cur(2 * D_KV), prv(2 * D_KV), cur(D_ATT),
                  pl.BlockSpec((ATT_QH, WINDOW), lambda n: (0, jnp.minimum(n, last))), cur(1), prv(1),
                  pl.BlockSpec((1, 2 * ATT_HD), lambda n: (0, 0)), pl.BlockSpec((1, ATT_QH), lambda n: (0, 0))],
        out_specs=[cur(D_ATT), cur(D_ATT),
                   pl.BlockSpec((WINDOW, 2 * D_KV), lambda n: (jnp.maximum(n - 1, 0), 0)),
                   pl.BlockSpec((1, ATT_QH), lambda n: (0, 0))],
        out_shape=[jax.ShapeDtypeStruct((L, D_ATT), BF16), jax.ShapeDtypeStruct((L, D_ATT), BF16),
                   jax.ShapeDtypeStruct((L, 2 * D_KV), BF16), jax.ShapeDtypeStruct((1, ATT_QH), F32)],
        scratch_shapes=[pltpu.VMEM((WINDOW, 2 * D_KV), F32), pltpu.VMEM((D_ATT, WINDOW), F32),
                        pltpu.VMEM((2 * WINDOW, D_KV), F32), pltpu.VMEM((2 * WINDOW, D_KV), F32)],
        compiler_params=_cparams(("arbitrary",)),
    )(dy, q, g, kv, kv, o, lse, pos, pos, inv, sinks)


def _out_ln_loss(y_ssd, y_att, x, target, w_out, ln_g, ln_b):
    L = x.shape[0]
    tm = ROW_TILE
    inv_d = 1.0 / D_MODEL

    def body(ys_ref, ya_ref, x_ref, t_ref, w_ref, g_ref, b_ref, dr_ref, dys_ref, dya_ref, loss_ref, gg_ref, gb_ref):
        i = pl.program_id(0)

        @pl.when(i == 0)
        def _():
            loss_ref[...] = jnp.zeros_like(loss_ref)
            gg_ref[...] = jnp.zeros_like(gg_ref)
            gb_ref[...] = jnp.zeros_like(gb_ref)

        h = _dot(_bf(ys_ref[...]), w_ref[0:D_SSD, :]) + _dot(_bf(ya_ref[...]), w_ref[D_SSD:D_MIX, :])
        r = ALPHA * x_ref[...] + h
        mu = jnp.mean(r, axis=-1, keepdims=True)
        xc = r - mu
        rstd = lax.rsqrt(jnp.mean(xc * xc, axis=-1, keepdims=True) + LN_EPS)
        xhat = xc * rstd
        gam = g_ref[...]
        diff = xhat * gam + b_ref[...] - t_ref[...]
        part = jnp.sum(jnp.sum(diff * diff, axis=-1, keepdims=True), axis=0, keepdims=True)
        loss_ref[...] += (0.5 * inv_d) * part
        dout = diff * inv_d
        gg_ref[...] += jnp.sum(dout * xhat, axis=0, keepdims=True)
        gb_ref[...] += jnp.sum(dout, axis=0, keepdims=True)
        dxh = dout * gam
        dr = rstd * (dxh - jnp.mean(dxh, axis=-1, keepdims=True) - xhat * jnp.mean(dxh * xhat, axis=-1, keepdims=True))
        dr_ref[...] = dr
        drb = _bf(dr)
        dys_ref[...] = _dot_nt(drb, w_ref[0:D_SSD, :])
        dya_ref[...] = _dot_nt(drb, w_ref[D_SSD:D_MIX, :])

    row = pl.BlockSpec((tm, D_MODEL), lambda i: (i, 0))
    vec = pl.BlockSpec((1, D_MODEL), lambda i: (0, 0))
    return pl.pallas_call(
        body, name="out_ln_loss", grid=(L // tm,),
        in_specs=[row, row, row, row, pl.BlockSpec((D_MIX, D_MODEL), lambda i: (0, 0), pipeline_mode=pl.Buffered(1)), vec, vec],
        out_specs=[row, row, row, pl.BlockSpec((1, 128), lambda i: (0, 0)), vec, vec],
        out_shape=[jax.ShapeDtypeStruct((L, D_MODEL), F32)] * 3 + [jax.ShapeDtypeStruct((1, 128), F32)]
        + [jax.ShapeDtypeStruct((1, D_MODEL), F32)] * 2,
        compiler_params=_cparams(("arbitrary",)),
    )(y_ssd, y_att, x, target, w_out, ln_g, ln_b)


def _local_step(x, pos, target, w, get_w_out, token, conv_w, conv_b, dt_bias, a_log, d_skip, norm_w, sinks, ln_g, ln_b):
    inv8 = ROPE_THETA ** (-jnp.arange(0, ROPE_DIM, 2, dtype=F32) / ROPE_DIM)
    inv = jnp.tile(jnp.concatenate([inv8, inv8, jnp.zeros((ATT_HD - ROPE_DIM,), F32)]), 2).reshape(1, 2 * ATT_HD)
    inv = inv + token

    z, g, q, xbc, kv, dtp, xb = _in_proj(x, w, pos, inv)
    y_ssd, y_pre, prev = _ssd_fwd2(z, xbc, dtp, conv_w, conv_b, dt_bias, a_log, d_skip, norm_w)
    y_att, o, lse = _swa_fwd(q, g, kv, sinks)
    w_out = get_w_out(lse)
    dr, dy_ssd, dy_att, loss, g_ln_g, g_ln_b = _out_ln_loss(y_ssd, y_att, x, target, w_out, ln_g, ln_b)
    gw_out_ssd, gw_out_att = _matmuls_tn([y_ssd, y_att], dr, "gw_out")
    dq, dg, dkv, g_sinks = _swa_bwd(dy_att, q, g, kv, o, lse, pos, inv, sinks)
    dz, dxbc, ddt, g_conv_w, g_conv_b, g_dt_bias, g_a_log, g_d_skip, g_norm_w = _ssd_bwd2(
        dy_ssd, z, y_pre, xbc, dtp, prev, conv_w, conv_b, dt_bias, a_log, d_skip, norm_w)
    grad_x = _grad_x(dr, dz, dg, dq, dxbc, dkv, ddt, w)
    gw_z, gw_g, gw_q = _matmuls_tn([dz, dg, dq], xb, "gw_zgq")
    gw_xbc, gw_kv, gw_dt = _matmuls_tn([dxbc, dkv, ddt], xb, "gw_xbc_kv_dt")
    gw_in = jnp.concatenate([gw_z, gw_xbc, gw_dt[0:SSD_HEADS], gw_q, gw_kv, gw_g], axis=0)
    gw_out = jnp.concatenate([gw_out_ssd, gw_out_att], axis=0)
    small = dict(conv_w=g_conv_w, conv_b=g_conv_b, dt_bias=g_dt_bias, a_log=g_a_log, d_skip=g_d_skip,
                 ssd_norm_w=g_norm_w, attn_sinks=g_sinks, ln_g=g_ln_g, ln_b=g_ln_b)
    return loss, grad_x, gw_in, gw_out, small


def _mesh_pos():
    return lax.axis_index("x"), lax.axis_index("y"), lax.axis_index("c")


def _gather_weights(w_in_s, conv_w_s):
    def body(win_ref, cw_ref, owin_ref, ocw_ref, send_sems, recv_sems, small_send, small_recv, local_sems):
        x, y, c = _mesh_pos()
        me = 2 * x + y
        sibling = (x, y, 1 - c)
        chips = [(1 - x, y), (x, 1 - y), (1 - x, 1 - y)]
        locals_ = [pltpu.make_async_copy(cw_ref, ocw_ref.at[me], local_sems.at[0])]
        for cp in locals_:
            cp.start()
        started = []
        for t, (src, dst) in enumerate(((win_ref, owin_ref),)):
            hr = src.shape[0] // 2

            def half(ref, hc, hr=hr):
                return ref.at[pl.ds(hc * hr, hr), :]

            for j, (px, py) in enumerate(chips):
                cp = pltpu.make_async_remote_copy(
                    src_ref=half(src, c), dst_ref=half(dst.at[me], c), send_sem=send_sems.at[t, j],
                    recv_sem=recv_sems.at[t, j], device_id=(px, py, c), device_id_type=MESH)
                cp.start()
                started.append(cp)
        for j, (px, py) in enumerate(chips):
            cp = pltpu.make_async_remote_copy(
                src_ref=cw_ref, dst_ref=ocw_ref.at[me], send_sem=small_send.at[j], recv_sem=small_recv.at[j],
                device_id=(px, py, c), device_id_type=MESH)
            cp.start()
            started.append(cp)
        for t, (src, dst) in enumerate(((win_ref, owin_ref),)):
            hr = src.shape[0] // 2
            for j, (px, py) in enumerate(chips):
                src_chip = 2 * px + py
                blk = dst.at[src_chip].at[pl.ds(c * hr, hr), :]
                pltpu.make_async_remote_copy(
                    src_ref=blk, dst_ref=blk, send_sem=send_sems.at[t, j], recv_sem=recv_sems.at[t, j],
                    device_id=(px, py, c), device_id_type=MESH).wait_recv()
                cp = pltpu.make_async_remote_copy(
                    src_ref=blk, dst_ref=blk, send_sem=send_sems.at[t, 3 + j], recv_sem=recv_sems.at[t, 3 + j],
                    device_id=sibling, device_id_type=MESH)
                cp.start()
                started.append(cp)
        for t, (src, dst) in enumerate(((win_ref, owin_ref),)):
            hr = src.shape[0] // 2
            for j, (px, py) in enumerate(chips):
                src_chip = 2 * px + py
                blk = dst.at[src_chip].at[pl.ds((1 - c) * hr, hr), :]
                pltpu.make_async_remote_copy(
                    src_ref=blk, dst_ref=blk, send_sem=send_sems.at[t, 3 + j], recv_sem=recv_sems.at[t, 3 + j],
                    device_id=sibling, device_id_type=MESH).wait_recv()
        for j in range(3):
            pltpu.make_async_remote_copy(
                src_ref=cw_ref, dst_ref=ocw_ref.at[me], send_sem=small_send.at[j], recv_sem=small_recv.at[j],
                device_id=sibling, device_id_type=MESH).wait_recv()
        for cp in started:
            cp.wait_send()
        for cp in locals_:
            cp.wait()

    any_spec = pl.BlockSpec(memory_space=pl.ANY)
    return pl.pallas_call(
        body, name="gather_weights",
        in_specs=[any_spec] * 2, out_specs=[any_spec] * 2,
        out_shape=[jax.ShapeDtypeStruct((N_CHIPS,) + a.shape, a.dtype) for a in (w_in_s, conv_w_s)],
        scratch_shapes=[pltpu.SemaphoreType.DMA((1, 6)), pltpu.SemaphoreType.DMA((1, 6)),
                        pltpu.SemaphoreType.DMA((3,)), pltpu.SemaphoreType.DMA((3,)), pltpu.SemaphoreType.DMA((3,))],
    )(w_in_s, conv_w_s)


_HBM = pl.BlockSpec(memory_space=pltpu.HBM)
_SEM = pl.BlockSpec(memory_space=pltpu.SEMAPHORE)
_EFFECT = pltpu.SideEffectType.DATAFLOW_SIDE_EFFECTING


def _gather_w_out_start(w_out_s, after):
    def body(src_ref, land_ref, after_ref, s0, s1, s2, r0, r1, r2, src_thru, land_thru, token):
        x, y, c = _mesh_pos()
        me = 2 * x + y
        chips = [(1 - x, y), (x, 1 - y), (1 - x, 1 - y)]
        for (px, py), s, r in zip(chips, (s0, s1, s2), (r0, r1, r2)):
            pltpu.make_async_remote_copy(src_ref=src_ref, dst_ref=land_ref.at[me], send_sem=s, recv_sem=r,
                                         device_id=(px, py, c), device_id_type=MESH).start()
        token[...] = jnp.zeros_like(token)

    sem = pltpu.SemaphoreType.DMA(())
    land = lax.empty((N_CHIPS,) + w_out_s.shape, w_out_s.dtype)
    return pl.pallas_call(
        body, name="gather_w_out_start",
        out_shape=(sem,) * 6 + (pltpu.HBM(w_out_s.shape, w_out_s.dtype), pltpu.HBM(land.shape, land.dtype),
                                jax.ShapeDtypeStruct((8, 128), F32)),
        in_specs=(_HBM, _HBM, pl.BlockSpec(memory_space=pl.ANY)),
        out_specs=(_SEM,) * 6 + (_HBM, _HBM, pl.BlockSpec(memory_space=pltpu.VMEM)),
        input_output_aliases={0: 6, 1: 7},
        compiler_params=pltpu.CompilerParams(has_side_effects=_EFFECT),
    )(pltpu.with_memory_space_constraint(w_out_s, pltpu.HBM), pltpu.with_memory_space_constraint(land, pltpu.HBM), after)


def _gather_w_out_wait(sems, src_thru, land_thru, after):
    def body(src_ref, land_ref, s0, s1, s2, r0, r1, r2, after_ref, src_dead, got_ref):
        x, y, c = _mesh_pos()
        chips = [(1 - x, y), (x, 1 - y), (1 - x, 1 - y)]
        for (px, py), s, r in zip(chips, (s0, s1, s2), (r0, r1, r2)):
            cp = pltpu.make_async_remote_copy(src_ref=src_ref, dst_ref=land_ref.at[2 * px + py], send_sem=s, recv_sem=r,
                                              device_id=(px, py, c), device_id_type=MESH)
            cp.wait_send()
            cp.wait_recv()

    return pl.pallas_call(
        body, name="gather_w_out_wait",
        out_shape=(pltpu.HBM(src_thru.shape, src_thru.dtype), pltpu.HBM(land_thru.shape, land_thru.dtype)),
        in_specs=(_HBM, _HBM) + (_SEM,) * 6 + (pl.BlockSpec(memory_space=pl.ANY),),
        out_specs=(_HBM, _HBM), input_output_aliases={0: 0, 1: 1},
        compiler_params=pltpu.CompilerParams(has_side_effects=_EFFECT),
    )(src_thru, land_thru, *sems, after)[1]


def _pair_exchange(gw_in, gw_out, small):
    k_small = small.shape[1]

    def body(gin_ref, gout_ref, sm_ref, rin_ref, rout_ref, slots_ref, send_sems, recv_sems, small_send, small_recv,
             local_sem):
        x, y, c = _mesh_pos()
        me = 4 * x + 2 * y + c
        sibling = (x, y, 1 - c)
        mine = pltpu.make_async_copy(sm_ref, slots_ref.at[me], local_sem)
        mine.start()
        started = []
        for t, (src, dst) in enumerate(((gin_ref, rin_ref), (gout_ref, rout_ref))):
            hr = src.shape[1] // 2
            for j in range(N_CHIPS):
                cp = pltpu.make_async_remote_copy(
                    src_ref=src.at[j, pl.ds((1 - c) * hr, hr), :], dst_ref=dst.at[j], send_sem=send_sems.at[t, j],
                    recv_sem=recv_sems.at[t, j], device_id=sibling, device_id_type=MESH)
                cp.start()
                started.append(cp)
        for k in range(1, 8):
            peer = (x ^ ((k >> 2) & 1), y ^ ((k >> 1) & 1), c ^ (k & 1))
            cp = pltpu.make_async_remote_copy(
                src_ref=sm_ref, dst_ref=slots_ref.at[me], send_sem=small_send.at[k - 1], recv_sem=small_recv.at[k - 1],
                device_id=peer, device_id_type=MESH)
            cp.start()
            started.append(cp)
        for t, (src, dst) in enumerate(((gin_ref, rin_ref), (gout_ref, rout_ref))):
            for j in range(N_CHIPS):
                pltpu.make_async_remote_copy(
                    src_ref=dst.at[j], dst_ref=dst.at[j], send_sem=send_sems.at[t, j], recv_sem=recv_sems.at[t, j],
                    device_id=sibling, device_id_type=MESH).wait_recv()
        for k in range(1, 8):
            pltpu.make_async_remote_copy(
                src_ref=sm_ref, dst_ref=slots_ref.at[me], send_sem=small_send.at[k - 1], recv_sem=small_recv.at[k - 1],
                device_id=sibling, device_id_type=MESH).wait_recv()
        for cp in started:
            cp.wait_send()
        mine.wait()

    any_spec = pl.BlockSpec(memory_space=pl.ANY)
    half_in = jax.ShapeDtypeStruct((N_CHIPS, gw_in.shape[1] // 2, D_MODEL), F32)
    half_out = jax.ShapeDtypeStruct((N_CHIPS, gw_out.shape[1] // 2, D_MODEL), F32)
    return pl.pallas_call(
        body, name="pair_exchange",
        in_specs=[any_spec] * 3, out_specs=[any_spec] * 3,
        out_shape=[half_in, half_out, jax.ShapeDtypeStruct((8, 8, k_small), F32)],
        scratch_shapes=[pltpu.SemaphoreType.DMA((2, N_CHIPS)), pltpu.SemaphoreType.DMA((2, N_CHIPS)),
                        pltpu.SemaphoreType.DMA((7,)), pltpu.SemaphoreType.DMA((7,)), pltpu.SemaphoreType.DMA],
    )(gw_in, gw_out, small)


def _chip_exchange(s_in, s_out):
    def body(sin_ref, sout_ref, rin_ref, rout_ref, send_sems, recv_sems):
        x, y, c = _mesh_pos()
        me = 2 * x + y
        chips = [(1 - x, y), (x, 1 - y), (1 - x, 1 - y)]
        started = []
        for t, (src, dst) in enumerate(((sin_ref, rin_ref), (sout_ref, rout_ref))):
            for j, (px, py) in enumerate(chips):
                cp = pltpu.make_async_remote_copy(
                    src_ref=src.at[2 * px + py], dst_ref=dst.at[me], send_sem=send_sems.at[t, j],
                    recv_sem=recv_sems.at[t, j], device_id=(px, py, c), device_id_type=MESH)
                cp.start()
                started.append(cp)
        for t, (src, dst) in enumerate(((sin_ref, rin_ref), (sout_ref, rout_ref))):
            for j, (px, py) in enumerate(chips):
                blk = dst.at[2 * px + py]
                pltpu.make_async_remote_copy(
                    src_ref=blk, dst_ref=blk, send_sem=send_sems.at[t, j], recv_sem=recv_sems.at[t, j],
                    device_id=(px, py, c), device_id_type=MESH).wait_recv()
        for cp in started:
            cp.wait_send()

    any_spec = pl.BlockSpec(memory_space=pl.ANY)
    return pl.pallas_call(
        body, name="chip_exchange",
        in_specs=[any_spec] * 2, out_specs=[any_spec] * 2,
        out_shape=[jax.ShapeDtypeStruct(s_in.shape, s_in.dtype), jax.ShapeDtypeStruct(s_out.shape, s_out.dtype)],
        scratch_shapes=[pltpu.SemaphoreType.DMA((2, 3)), pltpu.SemaphoreType.DMA((2, 3))],
    )(s_in, s_out)


def _pair_share(h_in, h_out):
    def body(hin_ref, hout_ref, rin_ref, rout_ref, send_sems, recv_sems):
        x, y, c = _mesh_pos()
        sibling = (x, y, 1 - c)
        started = []
        for t, (src, dst) in enumerate(((hin_ref, rin_ref), (hout_ref, rout_ref))):
            cp = pltpu.make_async_remote_copy(
                src_ref=src, dst_ref=dst, send_sem=send_sems.at[t], recv_sem=recv_sems.at[t],
                device_id=sibling, device_id_type=MESH)
            cp.start()
            started.append(cp)
        for cp in started:
            cp.wait()

    any_spec = pl.BlockSpec(memory_space=pl.ANY)
    return pl.pallas_call(
        body, name="pair_share",
        in_specs=[any_spec] * 2, out_specs=[any_spec] * 2,
        out_shape=[jax.ShapeDtypeStruct(h_in.shape, F32), jax.ShapeDtypeStruct(h_out.shape, F32)],
        scratch_shapes=[pltpu.SemaphoreType.DMA((2,)), pltpu.SemaphoreType.DMA((2,))],
    )(h_in, h_out)


def _pair_add(g, recv, core, name):
    _, rows, C = recv.shape
    tc = 256

    def body(core_ref, g_ref, r_ref, o_ref):
        o_ref[...] = _bf(g_ref[...] + r_ref[...])

    spec = pl.BlockSpec((1, rows, tc), lambda j, i, core: (j, 0, i))
    return pl.pallas_call(
        body, name=name,
        grid_spec=pltpu.PrefetchScalarGridSpec(
            num_scalar_prefetch=1, grid=(N_CHIPS, C // tc),
            in_specs=[pl.BlockSpec((1, rows, tc), lambda j, i, core: (j, core[0], i)), spec], out_specs=spec),
        out_shape=jax.ShapeDtypeStruct((N_CHIPS, rows, C), BF16),
        compiler_params=_cparams(("parallel", "parallel")),
    )(core, g, recv)


def _chip_add(own, parts, chip, name):
    _, rows, C = parts.shape
    tc = 256

    def body(chip_ref, own_ref, r0, r1, r2, r3, o_ref):
        acc = None
        for j, r in enumerate((r0, r1, r2, r3)):
            term = jnp.where(chip_ref[0] == j, own_ref[0], r[0]).astype(F32)
            acc = term if acc is None else acc + term
        o_ref[...] = acc

    def slab(j):
        return pl.BlockSpec((1, rows, tc), lambda i, chip: (jnp.where(chip[0] == j, (j + 1) % N_CHIPS, j), 0, i))

    return pl.pallas_call(
        body, name=name,
        grid_spec=pltpu.PrefetchScalarGridSpec(
            num_scalar_prefetch=1, grid=(C // tc,),
            in_specs=[pl.BlockSpec((1, rows, tc), lambda i, chip: (chip[0], 0, i))] + [slab(j) for j in range(N_CHIPS)],
            out_specs=pl.BlockSpec((rows, tc), lambda i, chip: (0, i))),
        out_shape=jax.ShapeDtypeStruct((rows, C), F32),
        compiler_params=_cparams(("parallel",)),
    )(chip, own, parts, parts, parts, parts)


def _adamw_math(w, g, m, v):
    m = ADAM_B1 * m + (1.0 - ADAM_B1) * g
    v = ADAM_B2 * v + (1.0 - ADAM_B2) * (g * g)
    m_hat = m / (1.0 - ADAM_B1 ** ADAM_STEP)
    v_hat = v / (1.0 - ADAM_B2 ** ADAM_STEP)
    delta = -ADAM_LR * (m_hat / (jnp.sqrt(v_hat) + ADAM_EPS) + ADAM_WD * w)
    return delta, m, v


def _adamw_pair(w, g_own, g_sib, m, v, core, name):
    unit = w.ndim == 3
    R, C = w.shape[0], w.shape[-1]
    rows = g_own.shape[0]
    tc = 128

    def body(core_ref, w_ref, go_ref, gs_ref, m_ref, v_ref, d_ref, nm_ref, nv_ref, g_ref):
        first = core_ref[0] == 0
        own, sib = go_ref[...], gs_ref[...]
        g = jnp.concatenate([jnp.where(first, own, sib), jnp.where(first, sib, own)], axis=0)[0:R, :]
        idx = (slice(None), 0, slice(None)) if unit else (slice(None), slice(None))
        d, nm, nv = _adamw_math(w_ref[idx], g, m_ref[idx], v_ref[idx])
        d_ref[idx] = d
        nm_ref[idx] = nm
        nv_ref[idx] = nv
        g_ref[idx] = g

    if unit:
        spec = pl.BlockSpec((R, 1, tc), lambda i, core: (0, 0, i))
    else:
        spec = pl.BlockSpec((R, tc), lambda i, core: (0, i))
    gspec = pl.BlockSpec((rows, tc), lambda i, core: (0, i))
    return pl.pallas_call(
        body, name=name,
        grid_spec=pltpu.PrefetchScalarGridSpec(
            num_scalar_prefetch=1, grid=(C // tc,),
            in_specs=[spec, gspec, gspec, spec, spec], out_specs=[spec] * 4),
        out_shape=[jax.ShapeDtypeStruct(w.shape, F32)] * 4,
        compiler_params=_cparams(("parallel",)),
    )(core, w, g_own, g_sib, m, v)


SMALL_NAMES = ("conv_b", "ssd_norm_w", "ln_g", "ln_b", "dt_bias", "a_log", "d_skip", "attn_sinks")
SMALL_SIZES = (D_XBC, D_SSD, D_MODEL, D_MODEL, SSD_HEADS, SSD_HEADS, SSD_HEADS, ATT_QH)
SMALL_OFFS = tuple(D_XBC + sum(-(-n // 128) * 128 for n in SMALL_SIZES[:k]) for k in range(len(SMALL_SIZES)))
LOSS_OFF = D_XBC + sum(-(-n // 128) * 128 for n in SMALL_SIZES)
K_SMALL = LOSS_OFF + 128


def _pack_small(g_conv_w, vecs, loss):
    def body(cw_ref, *refs):
        o_ref = refs[-1]
        o_ref[...] = jnp.zeros_like(o_ref)
        o_ref[0:CONV_K, 0:D_XBC] = cw_ref[...]
        for v_ref, off, n in zip(refs[:-2], SMALL_OFFS, SMALL_SIZES):
            o_ref[0:1, off:off + n] = v_ref[...]
        o_ref[0:1, LOSS_OFF:LOSS_OFF + 128] = refs[-2][...]

    return pl.pallas_call(
        body, name="pack_small", out_shape=jax.ShapeDtypeStruct((8, K_SMALL), F32), compiler_params=_cparams(),
    )(g_conv_w, *vecs, loss)


def _adamw_small(slots, chip, conv_w, m_conv_w, v_conv_w, params, moms, vars_):
    n_vec = len(SMALL_NAMES)

    def body(chip_ref, s_ref, *refs):
        ins = refs[:3 * (n_vec + 1)]
        outs = refs[3 * (n_vec + 1):-1]
        tot_ref = refs[-1]
        tot = s_ref[0]
        for d in range(1, 8):
            tot = tot + s_ref[d]
        outs[0][...] = tot[0:1, LOSS_OFF:LOSS_OFF + 1]
        off = pl.multiple_of(chip_ref[0] * CONV_COLS, 128)
        tot_ref[...] = tot
        grads = [tot_ref[0:CONV_K, pl.ds(off, CONV_COLS)]]
        grads += [tot[0:1, o:o + n] for o, n in zip(SMALL_OFFS, SMALL_SIZES)]
        for k, g in enumerate(grads):
            w_ref, m_ref, v_ref = ins[3 * k:3 * k + 3]
            full = (0,) if k == 0 else (Ellipsis,)
            d, nm, nv = _adamw_math(w_ref[full], g, m_ref[full], v_ref[full])
            for o_ref, val in zip(outs[1 + 4 * k:5 + 4 * k], (g, d, nm, nv)):
                o_ref[full] = val

    args = [conv_w, m_conv_w, v_conv_w]
    for w, m, v in zip(params, moms, vars_):
        args += [w, m, v]
    shapes = [jax.ShapeDtypeStruct((1, 1), F32)] + [jax.ShapeDtypeStruct(conv_w.shape, F32)] * 4
    for w in params:
        shapes += [jax.ShapeDtypeStruct(w.shape, F32)] * 4
    vmem = pl.BlockSpec(memory_space=pltpu.VMEM)
    return pl.pallas_call(
        body, name="adamw_small",
        grid_spec=pltpu.PrefetchScalarGridSpec(
            num_scalar_prefetch=1, grid=(1,),
            in_specs=[pl.BlockSpec(slots.shape, lambda i, chip: (0, 0, 0))] + [vmem] * len(args),
            out_specs=[vmem] * len(shapes), scratch_shapes=[pltpu.VMEM((8, K_SMALL), F32)]),
        out_shape=shapes, compiler_params=_cparams(),
    )(chip, slots, *args)


def kernel(x, positions, w_in, conv_w, conv_b, dt_bias, a_log, d_skip, ssd_norm_w, attn_sinks, w_out, ln_g, ln_b, loss_target, m_w_in, m_conv_w, m_conv_b, m_dt_bias, m_a_log, m_d_skip, m_ssd_norm_w, m_attn_sinks, m_w_out, m_ln_g, m_ln_b, v_w_in, v_conv_w, v_conv_b, v_dt_bias, v_a_log, v_d_skip, v_ssd_norm_w, v_attn_sinks, v_w_out, v_ln_g, v_ln_b):
    mx, my, mc = _mesh_pos()
    chip = 2 * mx + my
    L = x.shape[1]

    conv_w_s8 = jnp.pad(conv_w[0], ((0, 8 - CONV_K), (0, 0)))
    pad_rows = ((0, SLAB_ROWS - W_IN_COLS), (0, 0))
    w_in_t = w_in[0].T
    w_in_b, w_out_b = jnp.pad(_bf(w_in_t), pad_rows), _bf(w_out[0])
    ag_in, ag_cw = _gather_weights(w_in_b, conv_w_s8)
    started = _gather_w_out_start(w_out_b, ag_cw)
    own = (jnp.arange(N_CHIPS) == chip)[:, None, None]

    def get_w_out(after):
        landed = _gather_w_out_wait(started[0:6], started[6], started[7], after)
        return jnp.where(own, w_out_b[None], landed).reshape(D_MIX, D_MODEL)

    ag_in = jnp.where(own, w_in_b[None], ag_in)
    w_full = jnp.concatenate([ag_in[j, 0:W_IN_COLS] for j in range(N_CHIPS)], axis=0)
    w = jnp.concatenate([
        w_full[O_Z:O_Z + D_SSD], w_full[O_G:O_G + D_ATT], w_full[O_Q:O_Q + D_ATT],
        w_full[O_XBC:O_XBC + D_XBC], w_full[O_K:O_K + 2 * D_KV], w_full[O_DT:O_DT + SSD_HEADS],
        jnp.zeros((DT_PAD - SSD_HEADS, D_MODEL), BF16)], axis=0)
    conv_w_full = jnp.concatenate([ag_cw[j, 0:CONV_K] for j in range(N_CHIPS)], axis=1)

    loss_part, grad_x, gw_in, gw_out, small = _local_step(
        x[0], positions[0].reshape(L, 1), loss_target[0], w, get_w_out, started[8][0:1, :], conv_w_full, conv_b, dt_bias,
        a_log, d_skip, ssd_norm_w, attn_sinks, ln_g, ln_b)

    packed = _pack_small(small["conv_w"], [small[n] for n in SMALL_NAMES], loss_part)

    gw_in_slabs = jnp.stack([jnp.pad(gw_in[W_IN_COLS * j:W_IN_COLS * (j + 1)], pad_rows) for j in range(N_CHIPS)])
    gw_out_slabs = gw_out.reshape(N_CHIPS, W_OUT_ROWS, D_MODEL)
    core_id = mc.reshape(1).astype(jnp.int32)
    chip_id = chip.reshape(1).astype(jnp.int32)
    recv_in, recv_out, slots = _pair_exchange(gw_in_slabs, gw_out_slabs, packed)
    s_in = _pair_add(gw_in_slabs, recv_in, core_id, "pair_add_in")
    s_out = _pair_add(gw_out_slabs, recv_out, core_id, "pair_add_out")
    r_in, r_out = _chip_exchange(s_in, s_out)
    h_in = _chip_add(s_in, r_in, chip_id, "chip_add_in")
    h_out = _chip_add(s_out, r_out, chip_id, "chip_add_out")
    sib_in, sib_out = _pair_share(h_in, h_out)

    to_rows = lambda a: jnp.transpose(a, (2, 0, 1))
    in_t = _adamw_pair(to_rows(w_in), h_in, sib_in, to_rows(m_w_in), to_rows(v_w_in), core_id, "adamw_w_in")
    d_w_in, nm_w_in, nv_w_in, g_w_in = [jnp.transpose(a, (1, 2, 0)) for a in in_t]
    out_t = _adamw_pair(w_out[0], h_out, sib_out, m_w_out[0], v_w_out[0], core_id, "adamw_w_out")
    d_w_out, nm_w_out, nv_w_out, g_w_out = [a[None] for a in out_t]

    params = dict(conv_b=conv_b, ssd_norm_w=ssd_norm_w, ln_g=ln_g, ln_b=ln_b, dt_bias=dt_bias, a_log=a_log,
                  d_skip=d_skip, attn_sinks=attn_sinks)
    moms = dict(conv_b=m_conv_b, ssd_norm_w=m_ssd_norm_w, ln_g=m_ln_g, ln_b=m_ln_b, dt_bias=m_dt_bias, a_log=m_a_log,
                d_skip=m_d_skip, attn_sinks=m_attn_sinks)
    vars_ = dict(conv_b=v_conv_b, ssd_norm_w=v_ssd_norm_w, ln_g=v_ln_g, ln_b=v_ln_b, dt_bias=v_dt_bias, a_log=v_a_log,
                 d_skip=v_d_skip, attn_sinks=v_attn_sinks)
    res = _adamw_small(slots, chip_id, conv_w, m_conv_w, v_conv_w, [params[n] for n in SMALL_NAMES],
                       [moms[n] for n in SMALL_NAMES], [vars_[n] for n in SMALL_NAMES])
    loss = res[0][0, 0]
    grads, delta, new_m, new_v = {}, {}, {}, {}
    for k, n in enumerate(("conv_w",) + SMALL_NAMES):
        grads[n], delta[n], new_m[n], new_v[n] = res[1 + 4 * k:5 + 4 * k]
    for dd, a_in, a_out in ((grads, g_w_in, g_w_out), (delta, d_w_in, d_w_out), (new_m, nm_w_in, nm_w_out),
                            (new_v, nv_w_in, nv_w_out)):
        dd["w_in"] = a_in
        dd["w_out"] = a_out
    order = ("w_in", "conv_w", "conv_b", "dt_bias", "a_log", "d_skip", "ssd_norm_w", "attn_sinks", "w_out", "ln_g", "ln_b")
    return (loss, grad_x[None], *[grads[n] for n in order], *[delta[n] for n in order], *[new_m[n] for n in order],
            *[new_v[n] for n in order])
```

```python
import functools

import numpy as np
import jax
import jax.numpy as jnp
from jax import lax
from jax.experimental import pallas as pl
from jax.experimental.pallas import tpu as pltpu

F32 = jnp.float32
BF16 = jnp.bfloat16
MESH = pl.DeviceIdType.MESH

D_MODEL = 1024
D_SSD = 1024
D_ATT = 1024
D_MIX = 2048
SSD_HEADS = 16
SSD_P = 64
SSD_GROUPS = 2
SSD_R = 8
SSD_N = 128
D_BC = 256
D_XBC = 1536
CONV_K = 4
CHUNK = 128
ATT_HD = 64
ATT_QH = 16
ATT_KVH = 4
ATT_R = 4
D_KV = 256
WINDOW = 128
ROPE_THETA = 500000.0
ROPE_DIM = 16
ALPHA = 2.0 ** 0.25
LN_EPS = 1e-5
RMS_EPS = 1e-5
D_IN_PROJ = 5136
O_Z, O_XBC, O_DT, O_Q, O_K, O_V, O_G = 0, 1024, 2560, 2576, 3600, 3856, 4112
P_Z, P_G, P_Q, P_XBC, P_KV, P_DT, P_END = 0, 1024, 2048, 3072, 4608, 5120, 5248
DT_PAD = 128
N_CHIPS = 4
W_IN_COLS = D_IN_PROJ // N_CHIPS
SLAB_ROWS = 1312
W_OUT_ROWS = D_MIX // N_CHIPS
CONV_COLS = D_XBC // N_CHIPS

ADAM_LR = 0.001
ADAM_B1 = 0.9
ADAM_B2 = 0.999
ADAM_EPS = 1e-08
ADAM_WD = 0.01
ADAM_STEP = 10

VMEM_LIMIT = 56 * 1024 * 1024
ROW_TILE = 512
NEG_BIG = -1e30
HI = lax.Precision.HIGHEST


def _cparams(sem=None, **kw):
    if sem is not None:
        kw["dimension_semantics"] = sem
    return pltpu.CompilerParams(vmem_limit_bytes=VMEM_LIMIT, **kw)


def _dot(a, b):
    return jnp.dot(a, b, preferred_element_type=F32)


def _dot_nt(a, b):
    return lax.dot_general(a, b, (((1,), (1,)), ((), ())), preferred_element_type=F32)


def _dot_tn(a, b):
    return lax.dot_general(a, b, (((0,), (0,)), ((), ())), preferred_element_type=F32)


def _bf(a):
    return a.astype(BF16)


def _iota2(shape, dim):
    return lax.broadcasted_iota(jnp.int32, shape, dim)


def _to_rows(col):
    k = col.shape[1]
    eye = (_iota2((k, k), 0) == _iota2((k, k), 1)).astype(F32)
    return lax.dot_general(eye, col, (((1,), (1,)), ((), ())), preferred_element_type=F32, precision=HI)


def _to_cols(row):
    n = row.shape[1]
    eye = (_iota2((n, n), 0) == _iota2((n, n), 1)).astype(F32)
    return lax.dot_general(eye, row, (((1,), (1,)), ((), ())), preferred_element_type=F32, precision=HI)


def _sigmoid(x):
    return jax.nn.sigmoid(x)


def _in_proj(x, w, pos, inv):
    L = x.shape[0]
    tm = ROW_TILE
    widths = (D_SSD, D_ATT, D_ATT, D_XBC, 2 * D_KV, DT_PAD)

    def body(x_ref, w_ref, pos_ref, inv_ref, z_ref, g_ref, q_ref, xbc_ref, kv_ref, dt_ref, xb_ref):
        xb = _bf(x_ref[...])
        xb_ref[...] = xb
        for o_ref, off, wd in zip((z_ref, g_ref, xbc_ref, dt_ref), (P_Z, P_G, P_XBC, P_DT), (D_SSD, D_ATT, D_XBC, DT_PAD)):
            o_ref[...] = _dot_nt(xb, w_ref[off:off + wd, :])
        tabs = _rope_tables(pos_ref, inv_ref)
        q_ref[...] = _bf(_rope(_dot_nt(xb, w_ref[P_Q:P_Q + D_ATT, :]), tabs))
        kv_ref[:, 0:D_KV] = _bf(_rope(_dot_nt(xb, w_ref[P_KV:P_KV + D_KV, :]), tabs))
        kv_ref[:, D_KV:2 * D_KV] = _bf(_dot_nt(xb, w_ref[P_KV + D_KV:P_KV + 2 * D_KV, :]))

    row = lambda wd: pl.BlockSpec((tm, wd), lambda i: (i, 0))
    return pl.pallas_call(
        body, name="in_proj", grid=(L // tm,),
        in_specs=[row(D_MODEL), pl.BlockSpec((P_END, D_MODEL), lambda i: (0, 0), pipeline_mode=pl.Buffered(1)), row(1),
                  pl.BlockSpec((1, 2 * ATT_HD), lambda i: (0, 0))],
        out_specs=[row(wd) for wd in widths] + [row(D_MODEL)],
        out_shape=[jax.ShapeDtypeStruct((L, wd), dt) for wd, dt in zip(widths, (F32, F32, BF16, F32, BF16, F32))]
        + [jax.ShapeDtypeStruct((L, D_MODEL), BF16)],
        compiler_params=_cparams(("parallel",)),
    )(x, w, pos, inv)


def _matmuls_tn(a_list, b, name, out_dtype=F32):
    K, N = b.shape
    tk = min(K, 1024)
    nk = K // tk
    n = len(a_list)
    in_place = out_dtype == F32

    def body(*refs):
        b_ref = refs[n]
        o_refs = refs[n + 1:2 * n + 1]
        acc_refs = o_refs if in_place else refs[2 * n + 1:]
        k = pl.program_id(0)
        bb = _bf(b_ref[...])
        for a_ref, o_ref, acc_ref in zip(refs[:n], o_refs, acc_refs):
            part = _dot_tn(_bf(a_ref[...]), bb)

            @pl.when(k == 0)
            def _():
                acc_ref[...] = part

            @pl.when(k > 0)
            def _():
                acc_ref[...] += part

            if not in_place:
                @pl.when(k == nk - 1)
                def _():
                    o_ref[...] = acc_ref[...].astype(out_dtype)

    return pl.pallas_call(
        body, name=name, grid=(nk,),
        in_specs=[pl.BlockSpec((tk, a.shape[1]), lambda k: (k, 0)) for a in a_list] + [pl.BlockSpec((tk, N), lambda k: (k, 0))],
        out_specs=[pl.BlockSpec((a.shape[1], N), lambda k: (0, 0)) for a in a_list],
        out_shape=[jax.ShapeDtypeStruct((a.shape[1], N), out_dtype) for a in a_list],
        scratch_shapes=[] if in_place else [pltpu.VMEM((a.shape[1], N), F32) for a in a_list],
        compiler_params=_cparams(("arbitrary",)),
    )(*a_list, b)


def _grad_x(dr, dz, dg, dq, dxbc, dkv, ddt, w):
    L = dr.shape[0]
    tm = ROW_TILE
    widths = (D_SSD, D_ATT, D_ATT, D_XBC, 2 * D_KV, DT_PAD)
    offs = (P_Z, P_G, P_Q, P_XBC, P_KV, P_DT)

    def body(dr_ref, dz_ref, dg_ref, dq_ref, dxbc_ref, dkv_ref, ddt_ref, w_ref, o_ref):
        acc = ALPHA * dr_ref[...]
        for p_ref, off, wd in zip((dz_ref, dg_ref, dq_ref, dxbc_ref, dkv_ref, ddt_ref), offs, widths):
            acc = acc + _dot(_bf(p_ref[...]), w_ref[off:off + wd, :])
        o_ref[...] = acc

    row = lambda wd: pl.BlockSpec((tm, wd), lambda i: (i, 0))
    return pl.pallas_call(
        body, name="grad_x", grid=(L // tm,),
        in_specs=[row(D_MODEL)] + [row(wd) for wd in widths] + [pl.BlockSpec((P_END, D_MODEL), lambda i: (0, 0), pipeline_mode=pl.Buffered(1))],
        out_specs=row(D_MODEL),
        out_shape=jax.ShapeDtypeStruct((L, D_MODEL), F32),
        compiler_params=_cparams(("parallel",)),
    )(dr, dz, dg, dq, dxbc, dkv, ddt, w)


def _ssd_chunk_pre(first, xbc_ref, tail_ref, dt_ref, cw_ref, cb_ref, dtb_ref, alog_ref, ext):
    tail = jnp.where(first, 0.0, tail_ref[...])
    ext[0:8, :] = tail
    ext[8:8 + CHUNK, :] = xbc_ref[...]
    u = cb_ref[...] + cw_ref[0:1, :] * ext[pl.ds(5, CHUNK), :]
    for k in range(1, CONV_K):
        u = u + cw_ref[k:k + 1, :] * ext[pl.ds(5 + k, CHUNK), :]
    sig = _sigmoid(u)
    xbc = u * sig
    dtraw = dt_ref[:, 0:SSD_HEADS] + dtb_ref[...]
    dt = jax.nn.softplus(dtraw)
    A = -jnp.exp(alog_ref[...])
    a = dt * A
    tril = (_iota2((CHUNK, CHUNK), 0) >= _iota2((CHUNK, CHUNK), 1)).astype(F32)
    acs = jnp.dot(tril, a, preferred_element_type=F32, precision=HI)
    acs_row = _to_rows(acs)
    return u, sig, xbc, dtraw, dt, A, acs, acs_row


HALO = 16


def _shift_matrix(offsets):
    n = CHUNK + HALO
    m = np.zeros((len(offsets) * CHUNK, 2 * n), np.float32)
    for k, off in enumerate(offsets):
        t = np.arange(CHUNK)
        m[k * CHUNK + t, t + off] = 1.0
        m[k * CHUNK + t, n + t + off] = 1.0
    return jnp.asarray(m, BF16)


def _shifted_rows(first_part, second_part, smat_ref):
    h1, l1 = _hi_lo(first_part)
    h2, l2 = _hi_lo(second_part)
    sh = _dot(smat_ref[...], jnp.concatenate([h1, h2, l1, l2], axis=0))
    return sh[0:CHUNK], sh[CHUNK:2 * CHUNK], sh[2 * CHUNK:3 * CHUNK]


def _ssd_chunk_pre2(first, xbc_ref, tail_ref, dt_ref, cw_ref, cb_ref, dtb_ref, alog_ref, smat_ref):
    tail = jnp.where(first, 0.0, tail_ref[...])
    x = xbc_ref[...]
    taps = _shifted_rows(tail, x, smat_ref) + (x,)
    u = cb_ref[...] + cw_ref[0:1, :] * taps[0]
    for k in range(1, CONV_K):
        u = u + cw_ref[k:k + 1, :] * taps[k]
    sig = _sigmoid(u)
    xbc = u * sig
    dtraw = dt_ref[:, 0:SSD_HEADS] + dtb_ref[...]
    dt = jax.nn.softplus(dtraw)
    A = -jnp.exp(alog_ref[...])
    a = dt * A
    tril = (_iota2((CHUNK, CHUNK), 0) >= _iota2((CHUNK, CHUNK), 1)).astype(F32)
    acs = jnp.dot(tril, a, preferred_element_type=F32, precision=HI)
    acs_row = _to_rows(acs)
    return u, sig, xbc, dtraw, dt, A, acs, acs_row, taps


def _ssd_fwd(z, xbc, dtp, conv_w, conv_b, dt_bias, a_log, d_skip, norm_w):
    L = z.shape[0]
    nc = L // CHUNK

    def body(z_ref, xbc_ref, tail_ref, dt_ref, cw_ref, cb_ref, dtb_ref, alog_ref, dsk_ref, nw_ref,
             y_ref, ypre_ref, prev_ref, state, ext, ybuf):
        c = pl.program_id(0)

        @pl.when(c == 0)
        def _():
            state[...] = jnp.zeros_like(state)

        u, sig, xbcv, dtraw, dt, A, acs, acs_row = _ssd_chunk_pre(
            c == 0, xbc_ref, tail_ref, dt_ref, cw_ref, cb_ref, dtb_ref, alog_ref, ext)
        prev_ref[0] = state[...]
        causal = _iota2((CHUNK, CHUNK), 0) >= _iota2((CHUNK, CHUNK), 1)
        alast = acs[CHUNK - 1:CHUNK, :]
        for g in range(SSD_GROUPS):
            Bg = _bf(xbcv[:, D_SSD + SSD_N * g:D_SSD + SSD_N * (g + 1)])
            Cg = _bf(xbcv[:, D_SSD + D_BC + SSD_N * g:D_SSD + D_BC + SSD_N * (g + 1)])
            cb = _dot_nt(Cg, Bg)
            for r in range(SSD_R):
                h = g * SSD_R + r
                hs = slice(SSD_P * h, SSD_P * (h + 1))
                acs_c = acs[:, h:h + 1]
                seg = acs_c - acs_row[h:h + 1, :]
                Lm = jnp.where(causal, jnp.exp(jnp.where(causal, seg, 0.0)), 0.0)
                M = cb * Lm
                xh = xbcv[:, hs]
                X = xh * dt[:, h:h + 1]
                prev_h = state[hs, :]
                ydiag = _dot(_bf(M), _bf(X))
                yoff = _dot_nt(Cg, _bf(prev_h)) * jnp.exp(acs_c)
                al = alast[:, h:h + 1]
                Xd = X * jnp.exp(al - acs_c)
                state[hs, :] = prev_h * jnp.exp(al) + _dot_tn(_bf(Xd), Bg)
                ybuf[:, hs] = ydiag + yoff + dsk_ref[:, h:h + 1] * xh
        y = ybuf[...]
        ypre_ref[...] = y
        zv = z_ref[...]
        yf = y * (zv * _sigmoid(zv))
        half = D_SSD // SSD_GROUPS
        for g in range(SSD_GROUPS):
            gs = slice(half * g, half * (g + 1))
            yg = yf[:, gs]
            ms = jnp.mean(yg * yg, axis=-1, keepdims=True)
            y_ref[:, gs] = _bf(yg * lax.rsqrt(ms + RMS_EPS) * nw_ref[:, gs])

    full = lambda shape: pl.BlockSpec(shape, lambda c: (0, 0))
    return pl.pallas_call(
        body, name="ssd_fwd", grid=(nc,),
        in_specs=[
            pl.BlockSpec((CHUNK, D_SSD), lambda c: (c, 0)),
            pl.BlockSpec((CHUNK, D_XBC), lambda c: (c, 0)),
            pl.BlockSpec((8, D_XBC), lambda c: (jnp.maximum(c * (CHUNK // 8) - 1, 0), 0)),
            pl.BlockSpec((CHUNK, DT_PAD), lambda c: (c, 0)),
            full((CONV_K, D_XBC)), full((1, D_XBC)), full((1, SSD_HEADS)), full((1, SSD_HEADS)), full((1, SSD_HEADS)),
            full((1, D_SSD)),
        ],
        out_specs=[
            pl.BlockSpec((CHUNK, D_SSD), lambda c: (c, 0)),
            pl.BlockSpec((CHUNK, D_SSD), lambda c: (c, 0)),
            pl.BlockSpec((1, SSD_HEADS * SSD_P, SSD_N), lambda c: (c, 0, 0)),
        ],
        out_shape=[
            jax.ShapeDtypeStruct((L, D_SSD), F32),
            jax.ShapeDtypeStruct((L, D_SSD), F32),
            jax.ShapeDtypeStruct((nc, SSD_HEADS * SSD_P, SSD_N), F32),
        ],
        scratch_shapes=[
            pltpu.VMEM((SSD_HEADS * SSD_P, SSD_N), F32),
            pltpu.VMEM((CHUNK + 8, D_XBC), F32),
            pltpu.VMEM((CHUNK, D_SSD), F32),
        ],
        compiler_params=_cparams(("arbitrary",)),
    )(z, xbc, xbc, dtp, conv_w, conv_b, dt_bias, a_log, d_skip, norm_w)


def _ssd_bwd(dy, z, ypre, xbc, dtp, prev, conv_w, conv_b, dt_bias, a_log, d_skip, norm_w):
    L = z.shape[0]
    nc = L // CHUNK

    def body(dy_ref, z_ref, ypre_ref, xbc_ref, tail_ref, dt_ref, prev_ref, cw_ref, cb_ref, dtb_ref, alog_ref, dsk_ref,
             nw_ref, dz_ref, dxbc_ref, ddt_ref, gcw_ref, gcb_ref, gdtb_ref, galog_ref, gdsk_ref, gnw_ref,
             dstate, dhead, ext, ext2, dpost):
        i = pl.program_id(0)
        c = nc - 1 - i

        @pl.when(i == 0)
        def _():
            dstate[...] = jnp.zeros_like(dstate)
            dhead[...] = jnp.zeros_like(dhead)
            gcw_ref[...] = jnp.zeros_like(gcw_ref)
            gcb_ref[...] = jnp.zeros_like(gcb_ref)
            gdtb_ref[...] = jnp.zeros_like(gdtb_ref)
            galog_ref[...] = jnp.zeros_like(galog_ref)
            gdsk_ref[...] = jnp.zeros_like(gdsk_ref)
            gnw_ref[...] = jnp.zeros_like(gnw_ref)

        u, sig, xbcv, dtraw, dt, A, acs, acs_row = _ssd_chunk_pre(
            c == 0, xbc_ref, tail_ref, dt_ref, cw_ref, cb_ref, dtb_ref, alog_ref, ext)

        zv = z_ref[...]
        ypre = ypre_ref[...]
        dyn = dy_ref[...]
        sz = _sigmoid(zv)
        silu_z = zv * sz
        yf = ypre * silu_z
        half = D_SSD // SSD_GROUPS
        dyf_parts = []
        for g in range(SSD_GROUPS):
            gs = slice(half * g, half * (g + 1))
            yg = yf[:, gs]
            rstd = lax.rsqrt(jnp.mean(yg * yg, axis=-1, keepdims=True) + RMS_EPS)
            dout = dyn[:, gs]
            gnw_ref[:, gs] += jnp.sum(dout * yg * rstd, axis=0, keepdims=True)
            dyhat = dout * nw_ref[:, gs]
            dyf_parts.append(rstd * (dyhat - yg * (rstd * rstd) * jnp.mean(dyhat * yg, axis=-1, keepdims=True)))
        dyf = jnp.concatenate(dyf_parts, axis=1)
        dz_ref[...] = _bf(dyf * ypre * (sz * (1.0 + zv * (1.0 - sz))))
        dypre = dyf * silu_z

        causal = _iota2((CHUNK, CHUNK), 0) >= _iota2((CHUNK, CHUNK), 1)
        alast = acs[CHUNK - 1:CHUNK, :]
        lane16 = _iota2((1, SSD_HEADS), 1)
        sub16 = _iota2((SSD_HEADS, 1), 0)
        dacs_col = jnp.zeros((CHUNK, SSD_HEADS), F32)
        dacs_row = jnp.zeros((SSD_HEADS, CHUNK), F32)
        ddt_col = jnp.zeros((CHUNK, SSD_HEADS), F32)
        dalast = jnp.zeros((1, SSD_HEADS), F32)
        gdsk = jnp.zeros((1, SSD_HEADS), F32)
        for g in range(SSD_GROUPS):
            bs = slice(D_SSD + SSD_N * g, D_SSD + SSD_N * (g + 1))
            cs = slice(D_SSD + D_BC + SSD_N * g, D_SSD + D_BC + SSD_N * (g + 1))
            Bg = _bf(xbcv[:, bs])
            Cg = _bf(xbcv[:, cs])
            cb = _dot_nt(Cg, Bg)
            dcb = jnp.zeros((CHUNK, CHUNK), F32)
            dB = jnp.zeros((CHUNK, SSD_N), F32)
            dC = jnp.zeros((CHUNK, SSD_N), F32)
            for r in range(SSD_R):
                h = g * SSD_R + r
                hs = slice(SSD_P * h, SSD_P * (h + 1))
                onehot = (lane16 == h).astype(F32)
                acs_c = acs[:, h:h + 1]
                seg = acs_c - acs_row[h:h + 1, :]
                Lm = jnp.where(causal, jnp.exp(jnp.where(causal, seg, 0.0)), 0.0)
                M = cb * Lm
                xh = xbcv[:, hs]
                dth = dt[:, h:h + 1]
                X = xh * dth
                Xb = _bf(X)
                dyh = dypre[:, hs]
                dyb = _bf(dyh)
                prev_h = prev_ref[0, hs, :]
                prevb = _bf(prev_h)
                dnext = dstate[hs, :]
                dnextb = _bf(dnext)
                al = alast[:, h:h + 1]
                eacs = jnp.exp(acs_c)
                eal = jnp.exp(al)
                dsd = jnp.exp(al - acs_c)
                G = _bf(dyh * eacs)
                dstate[hs, :] = dnext * eal + _dot_tn(G, Cg)
                dC = dC + _dot(G, prevb)
                yoff = _dot_nt(Cg, prevb) * eacs
                dacs_h = jnp.sum(dyh * yoff, axis=-1, keepdims=True)
                BdN = _dot_nt(Bg, dnextb)
                dX = dsd * BdN
                dB = dB + _dot(_bf(X * dsd), dnextb)
                t = jnp.sum(X * BdN, axis=-1, keepdims=True) * dsd
                dacs_h = dacs_h - t
                dal = jnp.sum(t, axis=0, keepdims=True) + jnp.sum(
                    jnp.sum(dnext * prev_h, axis=-1, keepdims=True), axis=0, keepdims=True) * eal
                dM = _dot_nt(dyb, Xb)
                dX = dX + _dot_tn(_bf(M), dyb)
                dseg = dM * M
                dcb = dcb + dM * Lm
                dacs_h = dacs_h + jnp.sum(dseg, axis=-1, keepdims=True)
                dacs_row = dacs_row - jnp.sum(dseg, axis=0, keepdims=True) * (sub16 == h).astype(F32)
                dacs_col = dacs_col + dacs_h * onehot
                dalast = dalast + dal * onehot
                ddt_col = ddt_col + jnp.sum(dX * xh, axis=-1, keepdims=True) * onehot
                gdsk = gdsk + jnp.sum(jnp.sum(dyh * xh, axis=-1, keepdims=True), axis=0, keepdims=True) * onehot
                dpost[:, hs] = dX * dth + dsk_ref[:, h:h + 1] * dyh
            dcbb = _bf(dcb)
            dpost[:, bs] = dB + _dot_tn(dcbb, Cg)
            dpost[:, cs] = dC + _dot(dcbb, Bg)

        is_last = (_iota2((CHUNK, 1), 0) == CHUNK - 1).astype(F32)
        dacs = dacs_col + _to_cols(dacs_row) + is_last * dalast
        triu = (_iota2((CHUNK, CHUNK), 0) <= _iota2((CHUNK, CHUNK), 1)).astype(F32)
        da = jnp.dot(triu, dacs, preferred_element_type=F32, precision=HI)
        ddt_tot = ddt_col + da * A
        galog_ref[...] += jnp.sum(da * dt, axis=0, keepdims=True) * A
        ddtraw = ddt_tot * _sigmoid(dtraw)
        gdtb_ref[...] += jnp.sum(ddtraw, axis=0, keepdims=True)
        gdsk_ref[...] += gdsk
        ddt_ref[...] = jnp.zeros_like(ddt_ref)
        ddt_ref[:, 0:SSD_HEADS] = ddtraw

        dconv = dpost[...] * (sig * (1.0 + u * (1.0 - sig)))
        gcb_ref[...] += jnp.sum(dconv, axis=0, keepdims=True)
        for k in range(CONV_K):
            gcw_ref[k:k + 1, :] += jnp.sum(dconv * ext[pl.ds(5 + k, CHUNK), :], axis=0, keepdims=True)
        ext2[0:CHUNK, :] = dconv
        ext2[CHUNK:CHUNK + 8, :] = dhead[...]
        dx = cw_ref[CONV_K - 1:CONV_K, :] * dconv
        for k in range(CONV_K - 1):
            dx = dx + cw_ref[k:k + 1, :] * ext2[pl.ds(CONV_K - 1 - k, CHUNK), :]
        dxbc_ref[...] = _bf(dx)
        dhead[...] = dconv[0:8, :]

    full = lambda shape: pl.BlockSpec(shape, lambda i: (0, 0))
    rev = lambda wd: pl.BlockSpec((CHUNK, wd), lambda i: (nc - 1 - i, 0))
    return pl.pallas_call(
        body, name="ssd_bwd", grid=(nc,),
        in_specs=[
            rev(D_SSD), rev(D_SSD), rev(D_SSD), rev(D_XBC),
            pl.BlockSpec((8, D_XBC), lambda i: (jnp.maximum((nc - 1 - i) * (CHUNK // 8) - 1, 0), 0)),
            rev(DT_PAD),
            pl.BlockSpec((1, SSD_HEADS * SSD_P, SSD_N), lambda i: (nc - 1 - i, 0, 0)),
            full((CONV_K, D_XBC)), full((1, D_XBC)), full((1, SSD_HEADS)), full((1, SSD_HEADS)), full((1, SSD_HEADS)),
            full((1, D_SSD)),
        ],
        out_specs=[
            rev(D_SSD), rev(D_XBC), rev(DT_PAD),
            full((CONV_K, D_XBC)), full((1, D_XBC)), full((1, SSD_HEADS)), full((1, SSD_HEADS)), full((1, SSD_HEADS)),
            full((1, D_SSD)),
        ],
        out_shape=[
            jax.ShapeDtypeStruct((L, D_SSD), BF16), jax.ShapeDtypeStruct((L, D_XBC), BF16),
            jax.ShapeDtypeStruct((L, DT_PAD), F32),
            jax.ShapeDtypeStruct((CONV_K, D_XBC), F32), jax.ShapeDtypeStruct((1, D_XBC), F32),
            jax.ShapeDtypeStruct((1, SSD_HEADS), F32), jax.ShapeDtypeStruct((1, SSD_HEADS), F32),
            jax.ShapeDtypeStruct((1, SSD_HEADS), F32), jax.ShapeDtypeStruct((1, D_SSD), F32),
        ],
        scratch_shapes=[
            pltpu.VMEM((SSD_HEADS * SSD_P, SSD_N), F32),
            pltpu.VMEM((8, D_XBC), F32),
            pltpu.VMEM((CHUNK + 8, D_XBC), F32),
            pltpu.VMEM((CHUNK + 8, D_XBC), F32),
            pltpu.VMEM((CHUNK, D_XBC), F32),
        ],
        compiler_params=_cparams(("arbitrary",)),
    )(dy, z, ypre, xbc, xbc, dtp, prev, conv_w, conv_b, dt_bias, a_log, d_skip, norm_w)


def _head_expander():
    return (_iota2((SSD_HEADS, D_SSD), 1) // SSD_P == _iota2((SSD_HEADS, D_SSD), 0)).astype(BF16)


def _hi_lo(x):
    hi = _bf(x)
    return hi, _bf(x - hi.astype(F32))


def _expand(v, e):
    hi, lo = _hi_lo(v)
    return _dot(hi, e) + _dot(lo, e)


def _headsum(t, e):
    m = t.shape[0]
    if m < 8:
        t = jnp.broadcast_to(t[0:1], (8, t.shape[1]))
    hi, lo = _hi_lo(t)
    return (_dot_nt(hi, e) + _dot_nt(lo, e))[0:m]


def _ssd_decays(dt, acs, dsk_ref, e):
    alast = acs[CHUNK - 1:CHUNK, :]
    stk = jnp.concatenate([dt, jnp.exp(acs), jnp.exp(alast - acs),
                           jnp.broadcast_to(jnp.exp(alast), (8, SSD_HEADS)),
                           jnp.broadcast_to(dsk_ref[...], (8, SSD_HEADS))], axis=0)
    ex = _expand(stk, e)
    return (ex[0:CHUNK], ex[CHUNK:2 * CHUNK], ex[2 * CHUNK:3 * CHUNK], ex[3 * CHUNK:3 * CHUNK + 1],
            ex[3 * CHUNK + 8:3 * CHUNK + 9])


def _ssd_fwd2(z, xbc, dtp, conv_w, conv_b, dt_bias, a_log, d_skip, norm_w):
    L = z.shape[0]
    nc = L // CHUNK
    half = D_SSD // SSD_GROUPS

    def body(z_ref, xbc_ref, tail_ref, dt_ref, cw_ref, cb_ref, dtb_ref, alog_ref, dsk_ref, nw_ref, smat_ref,
             y_ref, ypre_ref, prev_ref, state, ybuf, mbuf):
        c = pl.program_id(0)

        @pl.when(c == 0)
        def _():
            state[...] = jnp.zeros_like(state)

        u, sig, xbcv, dtraw, dt, A, acs, acs_row, _ = _ssd_chunk_pre2(
            c == 0, xbc_ref, tail_ref, dt_ref, cw_ref, cb_ref, dtb_ref, alog_ref, smat_ref)
        e = _head_expander()
        dtE, eacsE, dsdE, ealE, dskE = _ssd_decays(dt, acs, dsk_ref, e)
        xs = xbcv[:, 0:D_SSD]
        X = xs * dtE
        prev_ref[0] = state[...]
        causal = _iota2((CHUNK, CHUNK), 0) >= _iota2((CHUNK, CHUNK), 1)
        for g in range(SSD_GROUPS):
            gs = slice(half * g, half * (g + 1))
            Bg = _bf(xbcv[:, D_SSD + SSD_N * g:D_SSD + SSD_N * (g + 1)])
            Cg = _bf(xbcv[:, D_SSD + D_BC + SSD_N * g:D_SSD + D_BC + SSD_N * (g + 1)])
            cb = _dot_nt(Cg, Bg)
            for r in range(SSD_R):
                h = g * SSD_R + r
                seg = acs[:, h:h + 1] - acs_row[h:h + 1, :]
                mbuf[h] = _bf(cb * jnp.where(causal, jnp.exp(jnp.where(causal, seg, 0.0)), 0.0))
            st = state[:, gs]
            ybuf[:, gs] = _dot(Cg, _bf(st)) * eacsE[:, gs] + dskE[:, gs] * xs[:, gs]
            state[:, gs] = st * ealE[:, gs] + _dot_tn(Bg, _bf(X[:, gs] * dsdE[:, gs]))
        Xb = _bf(X)
        for h in range(SSD_HEADS):
            hs = slice(SSD_P * h, SSD_P * (h + 1))
            ybuf[:, hs] += _dot(mbuf[h], Xb[:, hs])
        y = ybuf[...]
        ypre_ref[...] = y
        zv = z_ref[...]
        yf = y * (zv * _sigmoid(zv))
        for g in range(SSD_GROUPS):
            gs = slice(half * g, half * (g + 1))
            yg = yf[:, gs]
            ms = jnp.mean(yg * yg, axis=-1, keepdims=True)
            y_ref[:, gs] = _bf(yg * lax.rsqrt(ms + RMS_EPS) * nw_ref[:, gs])

    full = lambda shape: pl.BlockSpec(shape, lambda c: (0, 0))
    return pl.pallas_call(
        body, name="ssd_fwd", grid=(nc,),
        in_specs=[
            pl.BlockSpec((CHUNK, D_SSD), lambda c: (c, 0)),
            pl.BlockSpec((CHUNK, D_XBC), lambda c: (c, 0)),
            pl.BlockSpec((HALO, D_XBC), lambda c: (jnp.maximum(c * (CHUNK // HALO) - 1, 0), 0)),
            pl.BlockSpec((CHUNK, DT_PAD), lambda c: (c, 0)),
            full((CONV_K, D_XBC)), full((1, D_XBC)), full((1, SSD_HEADS)), full((1, SSD_HEADS)), full((1, SSD_HEADS)),
            full((1, D_SSD)), full((3 * CHUNK, 2 * (CHUNK + HALO))),
        ],
        out_specs=[
            pl.BlockSpec((CHUNK, D_SSD), lambda c: (c, 0)),
            pl.BlockSpec((CHUNK, D_SSD), lambda c: (c, 0)),
            pl.BlockSpec((1, SSD_N, D_SSD), lambda c: (c, 0, 0)),
        ],
        out_shape=[
            jax.ShapeDtypeStruct((L, D_SSD), BF16),
            jax.ShapeDtypeStruct((L, D_SSD), F32),
            jax.ShapeDtypeStruct((nc, SSD_N, D_SSD), F32),
        ],
        scratch_shapes=[
            pltpu.VMEM((SSD_N, D_SSD), F32),
            pltpu.VMEM((CHUNK, D_SSD), F32),
            pltpu.VMEM((SSD_HEADS, CHUNK, CHUNK), BF16),
        ],
        compiler_params=_cparams(("arbitrary",)),
    )(z, xbc, xbc, dtp, conv_w, conv_b, dt_bias, a_log, d_skip, norm_w, _shift_matrix((13, 14, 15)))


def _ssd_bwd2(dy, z, ypre, xbc, dtp, prev, conv_w, conv_b, dt_bias, a_log, d_skip, norm_w):
    L = z.shape[0]
    nc = L // CHUNK
    half = D_SSD // SSD_GROUPS

    def body(dy_ref, z_ref, ypre_ref, xbc_ref, tail_ref, dt_ref, prev_ref, cw_ref, cb_ref, dtb_ref, alog_ref, dsk_ref,
             nw_ref, smat_ref, smat2_ref, dz_ref, dxbc_ref, ddt_ref, gcw_ref, gcb_ref, gdtb_ref, galog_ref, gdsk_ref,
             gnw_ref, dstate, dhead, dpost, yobuf, bdbuf, lmbuf, dmbuf, cbbuf):
        i = pl.program_id(0)
        c = nc - 1 - i

        @pl.when(i == 0)
        def _():
            dstate[...] = jnp.zeros_like(dstate)
            dhead[...] = jnp.zeros_like(dhead)
            gcw_ref[...] = jnp.zeros_like(gcw_ref)
            gcb_ref[...] = jnp.zeros_like(gcb_ref)
            gdtb_ref[...] = jnp.zeros_like(gdtb_ref)
            galog_ref[...] = jnp.zeros_like(galog_ref)
            gdsk_ref[...] = jnp.zeros_like(gdsk_ref)
            gnw_ref[...] = jnp.zeros_like(gnw_ref)

        u, sig, xbcv, dtraw, dt, A, acs, acs_row, taps = _ssd_chunk_pre2(
            c == 0, xbc_ref, tail_ref, dt_ref, cw_ref, cb_ref, dtb_ref, alog_ref, smat_ref)
        e = _head_expander()
        dtE, eacsE, dsdE, ealE, dskE = _ssd_decays(dt, acs, dsk_ref, e)
        alast = acs[CHUNK - 1:CHUNK, :]
        xs = xbcv[:, 0:D_SSD]
        X = xs * dtE
        Xb = _bf(X)

        zv = z_ref[...]
        ypre = ypre_ref[...]
        dyn = dy_ref[...]
        sz = _sigmoid(zv)
        silu_z = zv * sz
        yf = ypre * silu_z
        dyf_parts = []
        for g in range(SSD_GROUPS):
            gs = slice(half * g, half * (g + 1))
            yg = yf[:, gs]
            rstd = lax.rsqrt(jnp.mean(yg * yg, axis=-1, keepdims=True) + RMS_EPS)
            dout = dyn[:, gs]
            gnw_ref[:, gs] += jnp.sum(dout * yg * rstd, axis=0, keepdims=True)
            dyhat = dout * nw_ref[:, gs]
            dyf_parts.append(rstd * (dyhat - yg * (rstd * rstd) * jnp.mean(dyhat * yg, axis=-1, keepdims=True)))
        dyf = jnp.concatenate(dyf_parts, axis=1)
        dz_ref[...] = _bf(dyf * ypre * (sz * (1.0 + zv * (1.0 - sz))))
        dyp = dyf * silu_z
        dyb = _bf(dyp)
        G = dyp * eacsE

        causal = _iota2((CHUNK, CHUNK), 0) >= _iota2((CHUNK, CHUNK), 1)
        ST = prev_ref[0]
        dST = dstate[...]
        for g in range(SSD_GROUPS):
            gs = slice(half * g, half * (g + 1))
            bs = slice(D_SSD + SSD_N * g, D_SSD + SSD_N * (g + 1))
            cs = slice(D_SSD + D_BC + SSD_N * g, D_SSD + D_BC + SSD_N * (g + 1))
            Bg = _bf(xbcv[:, bs])
            Cg = _bf(xbcv[:, cs])
            Gb = _bf(G[:, gs])
            STb = _bf(ST[:, gs])
            dSTb = _bf(dST[:, gs])
            dstate[:, gs] = dST[:, gs] * ealE[:, gs] + _dot_tn(Cg, Gb)
            yobuf[:, gs] = _dot(Cg, STb) * eacsE[:, gs]
            bdbuf[:, gs] = _dot(Bg, dSTb)
            dpost[:, cs] = _dot_nt(Gb, STb)
            dpost[:, bs] = _dot_nt(_bf(X[:, gs] * dsdE[:, gs]), dSTb)
            cbbuf[g] = _dot_nt(Cg, Bg)
            for r in range(SSD_R):
                h = g * SSD_R + r
                seg = acs[:, h:h + 1] - acs_row[h:h + 1, :]
                lmbuf[h] = jnp.where(causal, jnp.exp(jnp.where(causal, seg, 0.0)), 0.0)
        for h in range(SSD_HEADS):
            hs = slice(SSD_P * h, SSD_P * (h + 1))
            Mb = _bf(cbbuf[h // SSD_R] * lmbuf[h])
            dmbuf[h] = _dot_nt(dyb[:, hs], Xb[:, hs])
            dpost[:, hs] = _dot_tn(Mb, dyb[:, hs])
        lane16 = _iota2((1, SSD_HEADS), 1)
        sub16 = _iota2((SSD_HEADS, 1), 0)
        dacs_col = jnp.zeros((CHUNK, SSD_HEADS), F32)
        dacs_row = jnp.zeros((SSD_HEADS, CHUNK), F32)
        for g in range(SSD_GROUPS):
            bs = slice(D_SSD + SSD_N * g, D_SSD + SSD_N * (g + 1))
            cs = slice(D_SSD + D_BC + SSD_N * g, D_SSD + D_BC + SSD_N * (g + 1))
            cb = cbbuf[g]
            dcb = jnp.zeros((CHUNK, CHUNK), F32)
            for r in range(SSD_R):
                h = g * SSD_R + r
                dM = dmbuf[h]
                Lm = lmbuf[h]
                dcb = dcb + dM * Lm
                dseg = dM * (cb * Lm)
                dacs_col = dacs_col + jnp.sum(dseg, axis=-1, keepdims=True) * (lane16 == h).astype(F32)
                dacs_row = dacs_row - jnp.sum(dseg, axis=0, keepdims=True) * (sub16 == h).astype(F32)
            dcbb = _bf(dcb)
            dpost[:, bs] += _dot_tn(dcbb, _bf(xbcv[:, cs]))
            dpost[:, cs] += _dot(dcbb, _bf(xbcv[:, bs]))

        BD = bdbuf[...]
        dX = dpost[:, 0:D_SSD] + dsdE * BD
        dsd = jnp.exp(alast - acs)
        T = _headsum(X * BD, e) * dsd
        dalast = jnp.sum(T, axis=0, keepdims=True) + _headsum(
            jnp.sum(dST * ST, axis=0, keepdims=True), e) * jnp.exp(alast)
        is_last = (_iota2((CHUNK, 1), 0) == CHUNK - 1).astype(F32)
        dacs = dacs_col + _to_cols(dacs_row) + _headsum(dyp * yobuf[...], e) - T + is_last * dalast
        triu = (_iota2((CHUNK, CHUNK), 0) <= _iota2((CHUNK, CHUNK), 1)).astype(F32)
        da = jnp.dot(triu, dacs, preferred_element_type=F32, precision=HI)
        ddt_tot = _headsum(dX * xs, e) + da * A
        galog_ref[...] += jnp.sum(da * dt, axis=0, keepdims=True) * A
        ddtraw = ddt_tot * _sigmoid(dtraw)
        gdtb_ref[...] += jnp.sum(ddtraw, axis=0, keepdims=True)
        gdsk_ref[...] += _headsum(jnp.sum(dyp * xs, axis=0, keepdims=True), e)
        ddt_ref[...] = jnp.zeros_like(ddt_ref)
        ddt_ref[:, 0:SSD_HEADS] = ddtraw
        dpost[:, 0:D_SSD] = dX * dtE + dskE * dyp

        dconv = dpost[...] * (sig * (1.0 + u * (1.0 - sig)))
        gcb_ref[...] += jnp.sum(dconv, axis=0, keepdims=True)
        for k in range(CONV_K):
            gcw_ref[k:k + 1, :] += jnp.sum(dconv * taps[k], axis=0, keepdims=True)
        later = _shifted_rows(dconv, dhead[...], smat2_ref)
        dx = cw_ref[CONV_K - 1:CONV_K, :] * dconv
        for k in range(CONV_K - 1):
            dx = dx + cw_ref[k:k + 1, :] * later[k]
        dxbc_ref[...] = _bf(dx)
        dhead[...] = dconv[0:HALO, :]

    full = lambda shape: pl.BlockSpec(shape, lambda i: (0, 0))
    rev = lambda wd: pl.BlockSpec((CHUNK, wd), lambda i: (nc - 1 - i, 0))
    return pl.pallas_call(
        body, name="ssd_bwd", grid=(nc,),
        in_specs=[
            rev(D_SSD), rev(D_SSD), rev(D_SSD), rev(D_XBC),
            pl.BlockSpec((HALO, D_XBC), lambda i: (jnp.maximum((nc - 1 - i) * (CHUNK // HALO) - 1, 0), 0)),
            rev(DT_PAD),
            pl.BlockSpec((1, SSD_N, D_SSD), lambda i: (nc - 1 - i, 0, 0)),
            full((CONV_K, D_XBC)), full((1, D_XBC)), full((1, SSD_HEADS)), full((1, SSD_HEADS)), full((1, SSD_HEADS)),
            full((1, D_SSD)), full((3 * CHUNK, 2 * (CHUNK + HALO))), full((3 * CHUNK, 2 * (CHUNK + HALO))),
        ],
        out_specs=[
            rev(D_SSD), rev(D_XBC), rev(DT_PAD),
            full((CONV_K, D_XBC)), full((1, D_XBC)), full((1, SSD_HEADS)), full((1, SSD_HEADS)), full((1, SSD_HEADS)),
            full((1, D_SSD)),
        ],
        out_shape=[
            jax.ShapeDtypeStruct((L, D_SSD), BF16), jax.ShapeDtypeStruct((L, D_XBC), BF16),
            jax.ShapeDtypeStruct((L, DT_PAD), F32),
            jax.ShapeDtypeStruct((CONV_K, D_XBC), F32), jax.ShapeDtypeStruct((1, D_XBC), F32),
            jax.ShapeDtypeStruct((1, SSD_HEADS), F32), jax.ShapeDtypeStruct((1, SSD_HEADS), F32),
            jax.ShapeDtypeStruct((1, SSD_HEADS), F32), jax.ShapeDtypeStruct((1, D_SSD), F32),
        ],
        scratch_shapes=[
            pltpu.VMEM((SSD_N, D_SSD), F32),
            pltpu.VMEM((HALO, D_XBC), F32),
            pltpu.VMEM((CHUNK, D_XBC), F32),
            pltpu.VMEM((CHUNK, D_SSD), F32),
            pltpu.VMEM((CHUNK, D_SSD), F32),
            pltpu.VMEM((SSD_HEADS, CHUNK, CHUNK), F32),
            pltpu.VMEM((SSD_HEADS, CHUNK, CHUNK), F32),
            pltpu.VMEM((SSD_GROUPS, CHUNK, CHUNK), F32),
        ],
        compiler_params=_cparams(("arbitrary",)),
    )(dy, z, ypre, xbc, xbc, dtp, prev, conv_w, conv_b, dt_bias, a_log, d_skip, norm_w, _shift_matrix((13, 14, 15)),
      _shift_matrix((3, 2, 1)))


def _rope_tables(pos_ref, inv_ref):
    ang = pos_ref[...].astype(F32) * inv_ref[...]
    d = _iota2((1, 2 * ATT_HD), 1) % ATT_HD
    s = jnp.sin(ang)
    return jnp.cos(ang), jnp.where(d < ROPE_DIM // 2, -s, 0.0), jnp.where((d >= ROPE_DIM // 2) & (d < ROPE_DIM), s, 0.0)


def _rope(t, tabs):
    c, s1, s2 = tabs
    n = t.shape[1]
    rep = n // c.shape[1]
    return (t * jnp.tile(c, (1, rep)) + pltpu.roll(t, n - ROPE_DIM // 2, 1) * jnp.tile(s1, (1, rep))
            + pltpu.roll(t, ROPE_DIM // 2, 1) * jnp.tile(s2, (1, rep)))


def _rope_t(t, tabs):
    c, s1, s2 = tabs
    n = t.shape[1]
    rep = n // c.shape[1]
    return (t * jnp.tile(c, (1, rep)) + pltpu.roll(t * jnp.tile(s1, (1, rep)), ROPE_DIM // 2, 1)
            + pltpu.roll(t * jnp.tile(s2, (1, rep)), n - ROPE_DIM // 2, 1))


def _swa_mask(first):
    qi = _iota2((WINDOW, 2 * WINDOW), 0)
    si = _iota2((WINDOW, 2 * WINDOW), 1)
    band = (si > qi) & (si <= qi + WINDOW)
    return band & (jnp.logical_not(first) | (si >= WINDOW))


def _stack_heads(t, j):
    return jnp.concatenate([t[:, ATT_HD * (j * ATT_R + r):ATT_HD * (j * ATT_R + r + 1)] for r in range(ATT_R)], axis=0)


def _stack_cols(ref, j):
    cols = [jnp.broadcast_to(ref[:, j * ATT_R + r:j * ATT_R + r + 1], (WINDOW, 1)) for r in range(ATT_R)]
    return jnp.concatenate(cols, axis=0)


def _swa_mask_t(first):
    si = _iota2((2 * WINDOW, ATT_R * WINDOW), 0)
    qi = _iota2((2 * WINDOW, ATT_R * WINDOW), 1) % WINDOW
    band = (si > qi) & (si <= qi + WINDOW)
    return band & (jnp.logical_not(first) | (si >= WINDOW))


def _head_rows(ref, j, rows=None):
    if ref.shape[0] == 1:
        parts = [jnp.broadcast_to(ref[:, j * ATT_R + r:j * ATT_R + r + 1], (1, WINDOW)) for r in range(ATT_R)]
    else:
        parts = [ref[j * ATT_R + r:j * ATT_R + r + 1, :] for r in range(ATT_R)]
    return jnp.concatenate(parts, axis=1)


def _swa_fwd(q, g, kv, sinks):
    L = q.shape[0]
    nb = L // WINDOW
    scale = ATT_HD ** -0.5

    def body(q_ref, g_ref, kvc_ref, kvp_ref, sink_ref, y_ref, o_ref, lse_ref, otbuf):
        n = pl.program_id(0)
        kk = jnp.concatenate([kvp_ref[:, 0:D_KV], kvc_ref[:, 0:D_KV]], axis=0)
        vv = jnp.concatenate([kvp_ref[:, D_KV:2 * D_KV], kvc_ref[:, D_KV:2 * D_KV]], axis=0)
        valid = _swa_mask_t(n == 0)
        qv = q_ref[...]
        for j in range(ATT_KVH):
            js = slice(ATT_HD * j, ATT_HD * (j + 1))
            st = _dot_nt(kk[:, js], _stack_heads(qv, j)) * scale
            st = jnp.where(valid, st, NEG_BIG)
            sink = _head_rows(sink_ref, j)
            m = jnp.maximum(jnp.max(st, axis=0, keepdims=True), sink)
            p = jnp.exp(st - m)
            denom = jnp.sum(p, axis=0, keepdims=True) + jnp.exp(sink - m)
            ot = _dot_tn(vv[:, js], _bf(p)) * (1.0 / denom)
            lse = m + jnp.log(denom)
            for r in range(ATT_R):
                h = j * ATT_R + r
                otbuf[ATT_HD * h:ATT_HD * (h + 1), :] = ot[:, WINDOW * r:WINDOW * (r + 1)]
                lse_ref[h:h + 1, :] = lse[:, WINDOW * r:WINDOW * (r + 1)]
        o = otbuf[...].T
        o_ref[...] = o
        gv = g_ref[...]
        y_ref[...] = _bf(o * (gv * _sigmoid(gv)))

    cur = lambda wd: pl.BlockSpec((WINDOW, wd), lambda n: (n, 0))
    prv = lambda wd: pl.BlockSpec((WINDOW, wd), lambda n: (jnp.maximum(n - 1, 0), 0))
    return pl.pallas_call(
        body, name="swa_fwd", grid=(nb,),
        in_specs=[cur(D_ATT), cur(D_ATT), cur(2 * D_KV), prv(2 * D_KV), pl.BlockSpec((1, ATT_QH), lambda n: (0, 0))],
        out_specs=[cur(D_ATT), cur(D_ATT), pl.BlockSpec((ATT_QH, WINDOW), lambda n: (0, n))],
        out_shape=[jax.ShapeDtypeStruct((L, D_ATT), BF16), jax.ShapeDtypeStruct((L, D_ATT), F32),
                   jax.ShapeDtypeStruct((ATT_QH, L), F32)],
        scratch_shapes=[pltpu.VMEM((D_ATT, WINDOW), F32)],
        compiler_params=_cparams(("parallel",)),
    )(q, g, kv, kv, sinks)


def _swa_bwd(dy, q, g, kv, o, lse, pos, inv, sinks):
    L = q.shape[0]
    nb = L // WINDOW
    scale = ATT_HD ** -0.5

    def body(dy_ref, q_ref, g_ref, kvc_ref, kvp_ref, o_ref, lse_ref, posc_ref, posp_ref, inv_ref, sink_ref,
             dq_ref, dg_ref, dkv_ref, dsink_ref, carry, dqbuf, dkbuf, dvbuf):
        n = pl.program_id(0)

        @pl.when(n == 0)
        def _():
            dsink_ref[...] = jnp.zeros_like(dsink_ref)

        @pl.when(n < nb)
        def _():
            tc = _rope_tables(posc_ref, inv_ref)
            tp = _rope_tables(posp_ref, inv_ref)
            kk = jnp.concatenate([kvp_ref[:, 0:D_KV], kvc_ref[:, 0:D_KV]], axis=0)
            vv = jnp.concatenate([kvp_ref[:, D_KV:2 * D_KV], kvc_ref[:, D_KV:2 * D_KV]], axis=0)
            valid = _swa_mask_t(n == 0)
            qv = q_ref[...]
            gv = g_ref[...]
            sg = _sigmoid(gv)
            dyv = dy_ref[...]
            ov = o_ref[...]
            dg_ref[...] = _bf(dyv * ov * (sg * (1.0 + gv * (1.0 - sg))))
            do = dyv * (gv * sg)
            dod = do * ov
            ones = jnp.ones((8, ATT_HD), BF16)
            lane16 = _iota2((1, ATT_QH), 1)
            dsink = jnp.zeros((1, ATT_QH), F32)
            for j in range(ATT_KVH):
                js = slice(ATT_HD * j, ATT_HD * (j + 1))
                kj = kk[:, js]
                vj = vv[:, js]
                qs = _stack_heads(qv, j)
                dos = _bf(_stack_heads(do, j))
                hi, lo = _hi_lo(_stack_heads(dod, j))
                delta = (_dot_nt(ones, hi) + _dot_nt(ones, lo))[0:1]
                lse = _head_rows(lse_ref, j)
                st = _dot_nt(kj, qs) * scale
                pt = jnp.exp(jnp.where(valid, st, NEG_BIG) - lse)
                dst = _bf(pt * (_dot_nt(vj, dos) - delta))
                dqt = _dot_tn(kj, dst) * scale
                dkbuf[:, js] = _dot(dst, qs) * scale
                dvbuf[:, js] = _dot(_bf(pt), dos)
                sd = jnp.exp(_head_rows(sink_ref, j) - lse) * delta
                for r in range(ATT_R):
                    h = j * ATT_R + r
                    ls = slice(WINDOW * r, WINDOW * (r + 1))
                    dqbuf[ATT_HD * h:ATT_HD * (h + 1), :] = dqt[:, ls]
                    dsink = dsink - jnp.sum(sd[:, ls], axis=1, keepdims=True) * (lane16 == h).astype(F32)
            dsink_ref[...] += dsink
            dq_ref[...] = _bf(_rope_t(dqbuf[...].T, tc))
            dkp = _rope_t(dkbuf[0:WINDOW, :], tp)
            dkc = _rope_t(dkbuf[WINDOW:2 * WINDOW, :], tc)

            @pl.when(n > 0)
            def _():
                dkv_ref[:, 0:D_KV] = _bf(carry[:, 0:D_KV] + dkp)
                dkv_ref[:, D_KV:2 * D_KV] = _bf(carry[:, D_KV:2 * D_KV] + dvbuf[0:WINDOW, :])

            carry[:, 0:D_KV] = dkc
            carry[:, D_KV:2 * D_KV] = dvbuf[WINDOW:2 * WINDOW, :]

        @pl.when(n == nb)
        def _():
            dkv_ref[...] = _bf(carry[...])

    last = nb - 1
    cur = lambda wd: pl.BlockSpec((WINDOW, wd), lambda n: (jnp.minimum(n, last), 0))
    prv = lambda wd: pl.BlockSpec((WINDOW, wd), lambda n: (jnp.maximum(jnp.minimum(n, last) - 1, 0), 0))
    return pl.pallas_call(
        body, name="swa_bwd", grid=(nb + 1,),
        in_specs=[cur(D_ATT), cur(D_ATT), cur(D_ATT), cur(2 * D_KV), prv(2 * D_KV), cur(D_ATT),
                  pl.BlockSpec((ATT_QH, WINDOW), lambda n: (0, jnp.minimum(n, last))), cur(1), prv(1),
                  pl.BlockSpec((1, 2 * ATT_HD), lambda n: (0, 0)), pl.BlockSpec((1, ATT_QH), lambda n: (0, 0))],
        out_specs=[cur(D_ATT), cur(D_ATT),
                   pl.BlockSpec((WINDOW, 2 * D_KV), lambda n: (jnp.maximum(n - 1, 0), 0)),
                   pl.BlockSpec((1, ATT_QH), lambda n: (0, 0))],
        out_shape=[jax.ShapeDtypeStruct((L, D_ATT), BF16), jax.ShapeDtypeStruct((L, D_ATT), BF16),
                   jax.ShapeDtypeStruct((L, 2 * D_KV), BF16), jax.ShapeDtypeStruct((1, ATT_QH), F32)],
        scratch_shapes=[pltpu.VMEM((WINDOW, 2 * D_KV), F32), pltpu.VMEM((D_ATT, WINDOW), F32),
                        pltpu.VMEM((2 * WINDOW, D_KV), F32), pltpu.VMEM((2 * WINDOW, D_KV), F32)],
        compiler_params=_cparams(("arbitrary",)),
    )(dy, q, g, kv, kv, o, lse, pos, pos, inv, sinks)


def _out_ln_loss(y_ssd, y_att, x, target, w_out, ln_g, ln_b):
    L = x.shape[0]
    tm = ROW_TILE
    inv_d = 1.0 / D_MODEL

    def body(ys_ref, ya_ref, x_ref, t_ref, w_ref, g_ref, b_ref, dr_ref, dys_ref, dya_ref, loss_ref, gg_ref, gb_ref):
        i = pl.program_id(0)

        @pl.when(i == 0)
        def _():
            loss_ref[...] = jnp.zeros_like(loss_ref)
            gg_ref[...] = jnp.zeros_like(gg_ref)
            gb_ref[...] = jnp.zeros_like(gb_ref)

        h = _dot(_bf(ys_ref[...]), w_ref[0:D_SSD, :]) + _dot(_bf(ya_ref[...]), w_ref[D_SSD:D_MIX, :])
        r = ALPHA * x_ref[...] + h
        mu = jnp.mean(r, axis=-1, keepdims=True)
        xc = r - mu
        rstd = lax.rsqrt(jnp.mean(xc * xc, axis=-1, keepdims=True) + LN_EPS)
        xhat = xc * rstd
        gam = g_ref[...]
        diff = xhat * gam + b_ref[...] - t_ref[...]
        part = jnp.sum(jnp.sum(diff * diff, axis=-1, keepdims=True), axis=0, keepdims=True)
        loss_ref[...] += (0.5 * inv_d) * part
        dout = diff * inv_d
        gg_ref[...] += jnp.sum(dout * xhat, axis=0, keepdims=True)
        gb_ref[...] += jnp.sum(dout, axis=0, keepdims=True)
        dxh = dout * gam
        dr = rstd * (dxh - jnp.mean(dxh, axis=-1, keepdims=True) - xhat * jnp.mean(dxh * xhat, axis=-1, keepdims=True))
        dr_ref[...] = dr
        drb = _bf(dr)
        dys_ref[...] = _dot_nt(drb, w_ref[0:D_SSD, :])
        dya_ref[...] = _dot_nt(drb, w_ref[D_SSD:D_MIX, :])

    row = pl.BlockSpec((tm, D_MODEL), lambda i: (i, 0))
    vec = pl.BlockSpec((1, D_MODEL), lambda i: (0, 0))
    return pl.pallas_call(
        body, name="out_ln_loss", grid=(L // tm,),
        in_specs=[row, row, row, row, pl.BlockSpec((D_MIX, D_MODEL), lambda i: (0, 0), pipeline_mode=pl.Buffered(1)), vec, vec],
        out_specs=[row, row, row, pl.BlockSpec((1, 128), lambda i: (0, 0)), vec, vec],
        out_shape=[jax.ShapeDtypeStruct((L, D_MODEL), F32)] * 3 + [jax.ShapeDtypeStruct((1, 128), F32)]
        + [jax.ShapeDtypeStruct((1, D_MODEL), F32)] * 2,
        compiler_params=_cparams(("arbitrary",)),
    )(y_ssd, y_att, x, target, w_out, ln_g, ln_b)


def _local_step(x, pos, target, w, get_w_out, token, conv_w, conv_b, dt_bias, a_log, d_skip, norm_w, sinks, ln_g, ln_b):
    inv8 = ROPE_THETA ** (-jnp.arange(0, ROPE_DIM, 2, dtype=F32) / ROPE_DIM)
    inv = jnp.tile(jnp.concatenate([inv8, inv8, jnp.zeros((ATT_HD - ROPE_DIM,), F32)]), 2).reshape(1, 2 * ATT_HD)
    inv = inv + token

    z, g, q, xbc, kv, dtp, xb = _in_proj(x, w, pos, inv)
    y_ssd, y_pre, prev = _ssd_fwd2(z, xbc, dtp, conv_w, conv_b, dt_bias, a_log, d_skip, norm_w)
    y_att, o, lse = _swa_fwd(q, g, kv, sinks)
    w_out = get_w_out(lse)
    dr, dy_ssd, dy_att, loss, g_ln_g, g_ln_b = _out_ln_loss(y_ssd, y_att, x, target, w_out, ln_g, ln_b)
    gw_out_ssd, gw_out_att = _matmuls_tn([y_ssd, y_att], dr, "gw_out", out_dtype=BF16)
    slabs = jnp.concatenate([gw_out_ssd, gw_out_att], axis=0).reshape(N_CHIPS, W_OUT_ROWS, D_MODEL)
    w_out_red = _reduce_w_out_start(slabs, loss)
    inv = inv + w_out_red[16][0:1, :]
    dq, dg, dkv, g_sinks = _swa_bwd(dy_att, q, g, kv, o, lse, pos, inv, sinks)
    dz, dxbc, ddt, g_conv_w, g_conv_b, g_dt_bias, g_a_log, g_d_skip, g_norm_w = _ssd_bwd2(
        dy_ssd, z, y_pre, xbc, dtp, prev, conv_w, conv_b, dt_bias, a_log, d_skip, norm_w)
    grad_x = _grad_x(dr, dz, dg, dq, dxbc, dkv, ddt, w)
    gw_z, gw_g, gw_q = _matmuls_tn([dz, dg, dq], xb, "gw_zgq")
    gw_xbc, gw_kv, gw_dt = _matmuls_tn([dxbc, dkv, ddt], xb, "gw_xbc_kv_dt")
    gw_in = jnp.concatenate([gw_z, gw_xbc, gw_dt[0:SSD_HEADS], gw_q, gw_kv, gw_g], axis=0)
    small = dict(conv_w=g_conv_w, conv_b=g_conv_b, dt_bias=g_dt_bias, a_log=g_a_log, d_skip=g_d_skip,
                 ssd_norm_w=g_norm_w, attn_sinks=g_sinks, ln_g=g_ln_g, ln_b=g_ln_b)
    return loss, grad_x, gw_in, w_out_red, small


def _mesh_pos():
    return lax.axis_index("x"), lax.axis_index("y"), lax.axis_index("c")


def _gather_weights(w_in_s, conv_w_s):
    def body(win_ref, cw_ref, owin_ref, ocw_ref, send_sems, recv_sems, small_send, small_recv, local_sems):
        x, y, c = _mesh_pos()
        me = 2 * x + y
        sibling = (x, y, 1 - c)
        chips = [(1 - x, y), (x, 1 - y), (1 - x, 1 - y)]
        locals_ = [pltpu.make_async_copy(cw_ref, ocw_ref.at[me], local_sems.at[0])]
        for cp in locals_:
            cp.start()
        started = []
        for t, (src, dst) in enumerate(((win_ref, owin_ref),)):
            hr = src.shape[0] // 2

            def half(ref, hc, hr=hr):
                return ref.at[pl.ds(hc * hr, hr), :]

            for j, (px, py) in enumerate(chips):
                cp = pltpu.make_async_remote_copy(
                    src_ref=half(src, c), dst_ref=half(dst.at[me], c), send_sem=send_sems.at[t, j],
                    recv_sem=recv_sems.at[t, j], device_id=(px, py, c), device_id_type=MESH)
                cp.start()
                started.append(cp)
        for j, (px, py) in enumerate(chips):
            cp = pltpu.make_async_remote_copy(
                src_ref=cw_ref, dst_ref=ocw_ref.at[me], send_sem=small_send.at[j], recv_sem=small_recv.at[j],
                device_id=(px, py, c), device_id_type=MESH)
            cp.start()
            started.append(cp)
        for t, (src, dst) in enumerate(((win_ref, owin_ref),)):
            hr = src.shape[0] // 2
            for j, (px, py) in enumerate(chips):
                src_chip = 2 * px + py
                blk = dst.at[src_chip].at[pl.ds(c * hr, hr), :]
                pltpu.make_async_remote_copy(
                    src_ref=blk, dst_ref=blk, send_sem=send_sems.at[t, j], recv_sem=recv_sems.at[t, j],
                    device_id=(px, py, c), device_id_type=MESH).wait_recv()
                cp = pltpu.make_async_remote_copy(
                    src_ref=blk, dst_ref=blk, send_sem=send_sems.at[t, 3 + j], recv_sem=recv_sems.at[t, 3 + j],
                    device_id=sibling, device_id_type=MESH)
                cp.start()
                started.append(cp)
        for t, (src, dst) in enumerate(((win_ref, owin_ref),)):
            hr = src.shape[0] // 2
            for j, (px, py) in enumerate(chips):
                src_chip = 2 * px + py
                blk = dst.at[src_chip].at[pl.ds((1 - c) * hr, hr), :]
                pltpu.make_async_remote_copy(
                    src_ref=blk, dst_ref=blk, send_sem=send_sems.at[t, 3 + j], recv_sem=recv_sems.at[t, 3 + j],
                    device_id=sibling, device_id_type=MESH).wait_recv()
        for j in range(3):
            pltpu.make_async_remote_copy(
                src_ref=cw_ref, dst_ref=ocw_ref.at[me], send_sem=small_send.at[j], recv_sem=small_recv.at[j],
                device_id=sibling, device_id_type=MESH).wait_recv()
        for cp in started:
            cp.wait_send()
        for cp in locals_:
            cp.wait()

    any_spec = pl.BlockSpec(memory_space=pl.ANY)
    return pl.pallas_call(
        body, name="gather_weights",
        in_specs=[any_spec] * 2, out_specs=[any_spec] * 2,
        out_shape=[jax.ShapeDtypeStruct((N_CHIPS,) + a.shape, a.dtype) for a in (w_in_s, conv_w_s)],
        scratch_shapes=[pltpu.SemaphoreType.DMA((1, 6)), pltpu.SemaphoreType.DMA((1, 6)),
                        pltpu.SemaphoreType.DMA((3,)), pltpu.SemaphoreType.DMA((3,)), pltpu.SemaphoreType.DMA((3,))],
    )(w_in_s, conv_w_s)


_HBM = pl.BlockSpec(memory_space=pltpu.HBM)
_SEM = pl.BlockSpec(memory_space=pltpu.SEMAPHORE)
_EFFECT = pltpu.SideEffectType.DATAFLOW_SIDE_EFFECTING


def _gather_w_out_start(w_out_s, after):
    def body(src_ref, land_ref, after_ref, s0, s1, s2, r0, r1, r2, src_thru, land_thru, token):
        x, y, c = _mesh_pos()
        me = 2 * x + y
        chips = [(1 - x, y), (x, 1 - y), (1 - x, 1 - y)]
        for (px, py), s, r in zip(chips, (s0, s1, s2), (r0, r1, r2)):
            pltpu.make_async_remote_copy(src_ref=src_ref, dst_ref=land_ref.at[me], send_sem=s, recv_sem=r,
                                         device_id=(px, py, c), device_id_type=MESH).start()
        token[...] = jnp.zeros_like(token)

    sem = pltpu.SemaphoreType.DMA(())
    land = lax.empty((N_CHIPS,) + w_out_s.shape, w_out_s.dtype)
    return pl.pallas_call(
        body, name="gather_w_out_start",
        out_shape=(sem,) * 6 + (pltpu.HBM(w_out_s.shape, w_out_s.dtype), pltpu.HBM(land.shape, land.dtype),
                                jax.ShapeDtypeStruct((8, 128), F32)),
        in_specs=(_HBM, _HBM, pl.BlockSpec(memory_space=pl.ANY)),
        out_specs=(_SEM,) * 6 + (_HBM, _HBM, pl.BlockSpec(memory_space=pltpu.VMEM)),
        input_output_aliases={0: 6, 1: 7},
        compiler_params=pltpu.CompilerParams(has_side_effects=_EFFECT),
    )(pltpu.with_memory_space_constraint(w_out_s, pltpu.HBM), pltpu.with_memory_space_constraint(land, pltpu.HBM), after)


def _gather_w_out_wait(sems, src_thru, land_thru, after):
    def body(src_ref, land_ref, s0, s1, s2, r0, r1, r2, after_ref, src_dead, got_ref):
        x, y, c = _mesh_pos()
        chips = [(1 - x, y), (x, 1 - y), (1 - x, 1 - y)]
        for (px, py), s, r in zip(chips, (s0, s1, s2), (r0, r1, r2)):
            cp = pltpu.make_async_remote_copy(src_ref=src_ref, dst_ref=land_ref.at[2 * px + py], send_sem=s, recv_sem=r,
                                              device_id=(px, py, c), device_id_type=MESH)
            cp.wait_send()
            cp.wait_recv()

    return pl.pallas_call(
        body, name="gather_w_out_wait",
        out_shape=(pltpu.HBM(src_thru.shape, src_thru.dtype), pltpu.HBM(land_thru.shape, land_thru.dtype)),
        in_specs=(_HBM, _HBM) + (_SEM,) * 6 + (pl.BlockSpec(memory_space=pl.ANY),),
        out_specs=(_HBM, _HBM), input_output_aliases={0: 0, 1: 1},
        compiler_params=pltpu.CompilerParams(has_side_effects=_EFFECT),
    )(src_thru, land_thru, *sems, after)[1]


def _pair_exchange(gw_in, small):
    k_small = small.shape[1]

    def body(gin_ref, sm_ref, rin_ref, slots_ref, send_sems, recv_sems, small_send, small_recv, local_sem):
        x, y, c = _mesh_pos()
        me = 4 * x + 2 * y + c
        sibling = (x, y, 1 - c)
        mine = pltpu.make_async_copy(sm_ref, slots_ref.at[me], local_sem)
        mine.start()
        started = []
        for t, (src, dst) in enumerate(((gin_ref, rin_ref),)):
            hr = src.shape[1] // 2
            for j in range(N_CHIPS):
                cp = pltpu.make_async_remote_copy(
                    src_ref=src.at[j, pl.ds((1 - c) * hr, hr), :], dst_ref=dst.at[j], send_sem=send_sems.at[t, j],
                    recv_sem=recv_sems.at[t, j], device_id=sibling, device_id_type=MESH)
                cp.start()
                started.append(cp)
        for k in range(1, 8):
            peer = (x ^ ((k >> 2) & 1), y ^ ((k >> 1) & 1), c ^ (k & 1))
            cp = pltpu.make_async_remote_copy(
                src_ref=sm_ref, dst_ref=slots_ref.at[me], send_sem=small_send.at[k - 1], recv_sem=small_recv.at[k - 1],
                device_id=peer, device_id_type=MESH)
            cp.start()
            started.append(cp)
        for t, (src, dst) in enumerate(((gin_ref, rin_ref),)):
            for j in range(N_CHIPS):
                pltpu.make_async_remote_copy(
                    src_ref=dst.at[j], dst_ref=dst.at[j], send_sem=send_sems.at[t, j], recv_sem=recv_sems.at[t, j],
                    device_id=sibling, device_id_type=MESH).wait_recv()
        for k in range(1, 8):
            pltpu.make_async_remote_copy(
                src_ref=sm_ref, dst_ref=slots_ref.at[me], send_sem=small_send.at[k - 1], recv_sem=small_recv.at[k - 1],
                device_id=sibling, device_id_type=MESH).wait_recv()
        for cp in started:
            cp.wait_send()
        mine.wait()

    any_spec = pl.BlockSpec(memory_space=pl.ANY)
    half_in = jax.ShapeDtypeStruct((N_CHIPS, gw_in.shape[1] // 2, D_MODEL), F32)
    return pl.pallas_call(
        body, name="pair_exchange",
        in_specs=[any_spec] * 2, out_specs=[any_spec] * 2,
        out_shape=[half_in, jax.ShapeDtypeStruct((8, 8, k_small), F32)],
        scratch_shapes=[pltpu.SemaphoreType.DMA((1, N_CHIPS)), pltpu.SemaphoreType.DMA((1, N_CHIPS)),
                        pltpu.SemaphoreType.DMA((7,)), pltpu.SemaphoreType.DMA((7,)), pltpu.SemaphoreType.DMA],
    )(gw_in, small)


def _chip_exchange(s_in):
    def body(sin_ref, rin_ref, send_sems, recv_sems):
        x, y, c = _mesh_pos()
        me = 2 * x + y
        chips = [(1 - x, y), (x, 1 - y), (1 - x, 1 - y)]
        started = []
        for j, (px, py) in enumerate(chips):
            cp = pltpu.make_async_remote_copy(
                src_ref=sin_ref.at[2 * px + py], dst_ref=rin_ref.at[me], send_sem=send_sems.at[j],
                recv_sem=recv_sems.at[j], device_id=(px, py, c), device_id_type=MESH)
            cp.start()
            started.append(cp)
        for j, (px, py) in enumerate(chips):
            blk = rin_ref.at[2 * px + py]
            pltpu.make_async_remote_copy(
                src_ref=blk, dst_ref=blk, send_sem=send_sems.at[j], recv_sem=recv_sems.at[j],
                device_id=(px, py, c), device_id_type=MESH).wait_recv()
        for cp in started:
            cp.wait_send()

    any_spec = pl.BlockSpec(memory_space=pl.ANY)
    return pl.pallas_call(
        body, name="chip_exchange",
        in_specs=[any_spec], out_specs=any_spec,
        out_shape=jax.ShapeDtypeStruct(s_in.shape, s_in.dtype),
        scratch_shapes=[pltpu.SemaphoreType.DMA((3,)), pltpu.SemaphoreType.DMA((3,))],
    )(s_in)


def _pair_share(h_in):
    def body(hin_ref, rin_ref, send_sem, recv_sem):
        x, y, c = _mesh_pos()
        cp = pltpu.make_async_remote_copy(
            src_ref=hin_ref, dst_ref=rin_ref, send_sem=send_sem, recv_sem=recv_sem,
            device_id=(x, y, 1 - c), device_id_type=MESH)
        cp.start()
        cp.wait()

    any_spec = pl.BlockSpec(memory_space=pl.ANY)
    return pl.pallas_call(
        body, name="pair_share",
        in_specs=[any_spec], out_specs=any_spec,
        out_shape=jax.ShapeDtypeStruct(h_in.shape, F32),
        scratch_shapes=[pltpu.SemaphoreType.DMA, pltpu.SemaphoreType.DMA],
    )(h_in)


def _reduce_w_out_start(slabs, after):
    def body(src_ref, land_ref, after_ref, *refs):
        x, y, c = _mesh_pos()
        me = 4 * x + 2 * y + c
        for k in range(1, 8):
            px, py, pc = x ^ ((k >> 2) & 1), y ^ ((k >> 1) & 1), c ^ (k & 1)
            pltpu.make_async_remote_copy(src_ref=src_ref.at[2 * px + py], dst_ref=land_ref.at[me], send_sem=refs[k - 1],
                                         recv_sem=refs[6 + k], device_id=(px, py, pc), device_id_type=MESH).start()
        refs[16][...] = jnp.zeros_like(refs[16])

    sem = pltpu.SemaphoreType.DMA(())
    land = lax.empty((8,) + slabs.shape[1:], slabs.dtype)
    return pl.pallas_call(
        body, name="reduce_w_out_start",
        out_shape=(sem,) * 14 + (pltpu.HBM(slabs.shape, slabs.dtype), pltpu.HBM(land.shape, land.dtype),
                                 jax.ShapeDtypeStruct((8, 128), F32)),
        in_specs=(_HBM, _HBM, pl.BlockSpec(memory_space=pl.ANY)),
        out_specs=(_SEM,) * 14 + (_HBM, _HBM, pl.BlockSpec(memory_space=pltpu.VMEM)),
        input_output_aliases={0: 14, 1: 15},
        compiler_params=pltpu.CompilerParams(has_side_effects=_EFFECT),
    )(pltpu.with_memory_space_constraint(slabs, pltpu.HBM), pltpu.with_memory_space_constraint(land, pltpu.HBM), after)


def _reduce_w_out_wait(sems, slabs_thru, land_thru, after):
    def body(src_ref, land_ref, *refs):
        x, y, c = _mesh_pos()
        for k in range(1, 8):
            px, py, pc = x ^ ((k >> 2) & 1), y ^ ((k >> 1) & 1), c ^ (k & 1)
            cp = pltpu.make_async_remote_copy(
                src_ref=src_ref.at[2 * px + py], dst_ref=land_ref.at[4 * px + 2 * py + pc], send_sem=refs[k - 1],
                recv_sem=refs[6 + k], device_id=(px, py, pc), device_id_type=MESH)
            cp.wait_send()
            cp.wait_recv()

    return pl.pallas_call(
        body, name="reduce_w_out_wait",
        out_shape=(pltpu.HBM(slabs_thru.shape, slabs_thru.dtype), pltpu.HBM(land_thru.shape, land_thru.dtype)),
        in_specs=(_HBM, _HBM) + (_SEM,) * 14 + (pl.BlockSpec(memory_space=pl.ANY),),
        out_specs=(_HBM, _HBM), input_output_aliases={0: 0, 1: 1},
        compiler_params=pltpu.CompilerParams(has_side_effects=_EFFECT),
    )(slabs_thru, land_thru, *sems, after)


def _pair_add(g, recv, core, name):
    _, rows, C = recv.shape
    tc = 256

    def body(core_ref, g_ref, r_ref, o_ref):
        o_ref[...] = _bf(g_ref[...] + r_ref[...])

    spec = pl.BlockSpec((1, rows, tc), lambda j, i, core: (j, 0, i))
    return pl.pallas_call(
        body, name=name,
        grid_spec=pltpu.PrefetchScalarGridSpec(
            num_scalar_prefetch=1, grid=(N_CHIPS, C // tc),
            in_specs=[pl.BlockSpec((1, rows, tc), lambda j, i, core: (j, core[0], i)), spec], out_specs=spec),
        out_shape=jax.ShapeDtypeStruct((N_CHIPS, rows, C), BF16),
        compiler_params=_cparams(("parallel", "parallel")),
    )(core, g, recv)


def _chip_add(own, parts, chip, name):
    _, rows, C = parts.shape
    tc = 256

    def body(chip_ref, own_ref, r0, r1, r2, r3, o_ref):
        acc = None
        for j, r in enumerate((r0, r1, r2, r3)):
            term = jnp.where(chip_ref[0] == j, own_ref[0], r[0]).astype(F32)
            acc = term if acc is None else acc + term
        o_ref[...] = acc

    def slab(j):
        return pl.BlockSpec((1, rows, tc), lambda i, chip: (jnp.where(chip[0] == j, (j + 1) % N_CHIPS, j), 0, i))

    return pl.pallas_call(
        body, name=name,
        grid_spec=pltpu.PrefetchScalarGridSpec(
            num_scalar_prefetch=1, grid=(C // tc,),
            in_specs=[pl.BlockSpec((1, rows, tc), lambda i, chip: (chip[0], 0, i))] + [slab(j) for j in range(N_CHIPS)],
            out_specs=pl.BlockSpec((rows, tc), lambda i, chip: (0, i))),
        out_shape=jax.ShapeDtypeStruct((rows, C), F32),
        compiler_params=_cparams(("parallel",)),
    )(chip, own, parts, parts, parts, parts)


def _adamw_math(w, g, m, v):
    m = ADAM_B1 * m + (1.0 - ADAM_B1) * g
    v = ADAM_B2 * v + (1.0 - ADAM_B2) * (g * g)
    m_hat = m / (1.0 - ADAM_B1 ** ADAM_STEP)
    v_hat = v / (1.0 - ADAM_B2 ** ADAM_STEP)
    delta = -ADAM_LR * (m_hat / (jnp.sqrt(v_hat) + ADAM_EPS) + ADAM_WD * w)
    return delta, m, v


def _adamw_pair(w, g_own, g_sib, m, v, core, name):
    unit = w.ndim == 3
    R, C = w.shape[0], w.shape[-1]
    rows = g_own.shape[0]
    tc = 128

    def body(core_ref, w_ref, go_ref, gs_ref, m_ref, v_ref, d_ref, nm_ref, nv_ref, g_ref):
        first = core_ref[0] == 0
        own, sib = go_ref[...], gs_ref[...]
        g = jnp.concatenate([jnp.where(first, own, sib), jnp.where(first, sib, own)], axis=0)[0:R, :]
        idx = (slice(None), 0, slice(None)) if unit else (slice(None), slice(None))
        d, nm, nv = _adamw_math(w_ref[idx], g, m_ref[idx], v_ref[idx])
        d_ref[idx] = d
        nm_ref[idx] = nm
        nv_ref[idx] = nv
        g_ref[idx] = g

    if unit:
        spec = pl.BlockSpec((R, 1, tc), lambda i, core: (0, 0, i))
    else:
        spec = pl.BlockSpec((R, tc), lambda i, core: (0, i))
    gspec = pl.BlockSpec((rows, tc), lambda i, core: (0, i))
    return pl.pallas_call(
        body, name=name,
        grid_spec=pltpu.PrefetchScalarGridSpec(
            num_scalar_prefetch=1, grid=(C // tc,),
            in_specs=[spec, gspec, gspec, spec, spec], out_specs=[spec] * 4),
        out_shape=[jax.ShapeDtypeStruct(w.shape, F32)] * 4,
        compiler_params=_cparams(("parallel",)),
    )(core, w, g_own, g_sib, m, v)


def _adamw_sum8(w, slabs, land, m, v, ids, name):
    R, C = w.shape
    tc = 128

    def body(ids_ref, w_ref, own_ref, *refs):
        lrefs, (m_ref, v_ref, d_ref, nm_ref, nv_ref, g_ref) = refs[:8], refs[8:]
        g = None
        for d, l_ref in enumerate(lrefs):
            term = jnp.where(ids_ref[0] == d, own_ref[0], l_ref[0]).astype(F32)
            g = term if g is None else g + term
        dl, nm, nv = _adamw_math(w_ref[...], g, m_ref[...], v_ref[...])
        d_ref[...] = dl
        nm_ref[...] = nm
        nv_ref[...] = nv
        g_ref[...] = g

    def slot(d):
        return pl.BlockSpec((1, R, tc), lambda i, ids: (jnp.where(ids[0] == d, (d + 1) % 8, d), 0, i))

    spec = pl.BlockSpec((R, tc), lambda i, ids: (0, i))
    return pl.pallas_call(
        body, name=name,
        grid_spec=pltpu.PrefetchScalarGridSpec(
            num_scalar_prefetch=1, grid=(C // tc,),
            in_specs=[spec, pl.BlockSpec((1, R, tc), lambda i, ids: (ids[1], 0, i))] + [slot(d) for d in range(8)]
            + [spec, spec],
            out_specs=[spec] * 4),
        out_shape=[jax.ShapeDtypeStruct((R, C), F32)] * 4,
        compiler_params=_cparams(("parallel",)),
    )(ids, w, slabs, *([land] * 8), m, v)


SMALL_NAMES = ("conv_b", "ssd_norm_w", "ln_g", "ln_b", "dt_bias", "a_log", "d_skip", "attn_sinks")
SMALL_SIZES = (D_XBC, D_SSD, D_MODEL, D_MODEL, SSD_HEADS, SSD_HEADS, SSD_HEADS, ATT_QH)
SMALL_OFFS = tuple(D_XBC + sum(-(-n // 128) * 128 for n in SMALL_SIZES[:k]) for k in range(len(SMALL_SIZES)))
LOSS_OFF = D_XBC + sum(-(-n // 128) * 128 for n in SMALL_SIZES)
K_SMALL = LOSS_OFF + 128


def _pack_small(g_conv_w, vecs, loss):
    def body(cw_ref, *refs):
        o_ref = refs[-1]
        o_ref[...] = jnp.zeros_like(o_ref)
        o_ref[0:CONV_K, 0:D_XBC] = cw_ref[...]
        for v_ref, off, n in zip(refs[:-2], SMALL_OFFS, SMALL_SIZES):
            o_ref[0:1, off:off + n] = v_ref[...]
        o_ref[0:1, LOSS_OFF:LOSS_OFF + 128] = refs[-2][...]

    return pl.pallas_call(
        body, name="pack_small", out_shape=jax.ShapeDtypeStruct((8, K_SMALL), F32), compiler_params=_cparams(),
    )(g_conv_w, *vecs, loss)


def _adamw_small(slots, chip, conv_w, m_conv_w, v_conv_w, params, moms, vars_):
    n_vec = len(SMALL_NAMES)

    def body(chip_ref, s_ref, *refs):
        ins = refs[:3 * (n_vec + 1)]
        outs = refs[3 * (n_vec + 1):-1]
        tot_ref = refs[-1]
        tot = s_ref[0]
        for d in range(1, 8):
            tot = tot + s_ref[d]
        outs[0][...] = tot[0:1, LOSS_OFF:LOSS_OFF + 1]
        off = pl.multiple_of(chip_ref[0] * CONV_COLS, 128)
        tot_ref[...] = tot
        grads = [tot_ref[0:CONV_K, pl.ds(off, CONV_COLS)]]
        grads += [tot[0:1, o:o + n] for o, n in zip(SMALL_OFFS, SMALL_SIZES)]
        for k, g in enumerate(grads):
            w_ref, m_ref, v_ref = ins[3 * k:3 * k + 3]
            full = (0,) if k == 0 else (Ellipsis,)
            d, nm, nv = _adamw_math(w_ref[full], g, m_ref[full], v_ref[full])
            for o_ref, val in zip(outs[1 + 4 * k:5 + 4 * k], (g, d, nm, nv)):
                o_ref[full] = val

    args = [conv_w, m_conv_w, v_conv_w]
    for w, m, v in zip(params, moms, vars_):
        args += [w, m, v]
    shapes = [jax.ShapeDtypeStruct((1, 1), F32)] + [jax.ShapeDtypeStruct(conv_w.shape, F32)] * 4
    for w in params:
        shapes += [jax.ShapeDtypeStruct(w.shape, F32)] * 4
    vmem = pl.BlockSpec(memory_space=pltpu.VMEM)
    return pl.pallas_call(
        body, name="adamw_small",
        grid_spec=pltpu.PrefetchScalarGridSpec(
            num_scalar_prefetch=1, grid=(1,),
            in_specs=[pl.BlockSpec(slots.shape, lambda i, chip: (0, 0, 0))] + [vmem] * len(args),
            out_specs=[vmem] * len(shapes), scratch_shapes=[pltpu.VMEM((8, K_SMALL), F32)]),
        out_shape=shapes, compiler_params=_cparams(),
    )(chip, slots, *args)


def kernel(x, positions, w_in, conv_w, conv_b, dt_bias, a_log, d_skip, ssd_norm_w, attn_sinks, w_out, ln_g, ln_b, loss_target, m_w_in, m_conv_w, m_conv_b, m_dt_bias, m_a_log, m_d_skip, m_ssd_norm_w, m_attn_sinks, m_w_out, m_ln_g, m_ln_b, v_w_in, v_conv_w, v_conv_b, v_dt_bias, v_a_log, v_d_skip, v_ssd_norm_w, v_attn_sinks, v_w_out, v_ln_g, v_ln_b):
    mx, my, mc = _mesh_pos()
    chip = 2 * mx + my
    L = x.shape[1]

    conv_w_s8 = jnp.pad(conv_w[0], ((0, 8 - CONV_K), (0, 0)))
    pad_rows = ((0, SLAB_ROWS - W_IN_COLS), (0, 0))
    w_in_t = w_in[0].T
    w_in_b, w_out_b = jnp.pad(_bf(w_in_t), pad_rows), _bf(w_out[0])
    ag_in, ag_cw = _gather_weights(w_in_b, conv_w_s8)
    started = _gather_w_out_start(w_out_b, ag_cw)
    own = (jnp.arange(N_CHIPS) == chip)[:, None, None]

    def get_w_out(after):
        landed = _gather_w_out_wait(started[0:6], started[6], started[7], after)
        return jnp.where(own, w_out_b[None], landed).reshape(D_MIX, D_MODEL)

    ag_in = jnp.where(own, w_in_b[None], ag_in)
    w_full = jnp.concatenate([ag_in[j, 0:W_IN_COLS] for j in range(N_CHIPS)], axis=0)
    w = jnp.concatenate([
        w_full[O_Z:O_Z + D_SSD], w_full[O_G:O_G + D_ATT], w_full[O_Q:O_Q + D_ATT],
        w_full[O_XBC:O_XBC + D_XBC], w_full[O_K:O_K + 2 * D_KV], w_full[O_DT:O_DT + SSD_HEADS],
        jnp.zeros((DT_PAD - SSD_HEADS, D_MODEL), BF16)], axis=0)
    conv_w_full = jnp.concatenate([ag_cw[j, 0:CONV_K] for j in range(N_CHIPS)], axis=1)

    loss_part, grad_x, gw_in, w_out_red, small = _local_step(
        x[0], positions[0].reshape(L, 1), loss_target[0], w, get_w_out, started[8][0:1, :], conv_w_full, conv_b, dt_bias,
        a_log, d_skip, ssd_norm_w, attn_sinks, ln_g, ln_b)

    packed = _pack_small(small["conv_w"], [small[n] for n in SMALL_NAMES], loss_part)

    gw_in_slabs = jnp.stack([jnp.pad(gw_in[W_IN_COLS * j:W_IN_COLS * (j + 1)], pad_rows) for j in range(N_CHIPS)])
    core_id = mc.reshape(1).astype(jnp.int32)
    chip_id = chip.reshape(1).astype(jnp.int32)
    recv_in, slots = _pair_exchange(gw_in_slabs, packed)
    s_in = _pair_add(gw_in_slabs, recv_in, core_id, "pair_add_in")
    r_in = _chip_exchange(s_in)
    h_in = _chip_add(s_in, r_in, chip_id, "chip_add_in")
    sib_in = _pair_share(h_in)

    to_rows = lambda a: jnp.transpose(a, (2, 0, 1))
    in_t = _adamw_pair(to_rows(w_in), h_in, sib_in, to_rows(m_w_in), to_rows(v_w_in), core_id, "adamw_w_in")
    d_w_in, nm_w_in, nv_w_in, g_w_in = [jnp.transpose(a, (1, 2, 0)) for a in in_t]
    own_slabs, landed = _reduce_w_out_wait(w_out_red[0:14], w_out_red[14], w_out_red[15], sib_in)
    ids = jnp.stack([4 * mx + 2 * my + mc, chip]).astype(jnp.int32)
    out_t = _adamw_sum8(w_out[0], own_slabs, landed, m_w_out[0], v_w_out[0], ids, "adamw_w_out")
    d_w_out, nm_w_out, nv_w_out, g_w_out = [a[None] for a in out_t]

    params = dict(conv_b=conv_b, ssd_norm_w=ssd_norm_w, ln_g=ln_g, ln_b=ln_b, dt_bias=dt_bias, a_log=a_log,
                  d_skip=d_skip, attn_sinks=attn_sinks)
    moms = dict(conv_b=m_conv_b, ssd_norm_w=m_ssd_norm_w, ln_g=m_ln_g, ln_b=m_ln_b, dt_bias=m_dt_bias, a_log=m_a_log,
                d_skip=m_d_skip, attn_sinks=m_attn_sinks)
    vars_ = dict(conv_b=v_conv_b, ssd_norm_w=v_ssd_norm_w, ln_g=v_ln_g, ln_b=v_ln_b, dt_bias=v_dt_bias, a_log=v_a_log,
                 d_skip=v_d_skip, attn_sinks=v_attn_sinks)
    res = _adamw_small(slots, chip_id, conv_w, m_conv_w, v_conv_w, [params[n] for n in SMALL_NAMES],
                       [moms[n] for n in SMALL_NAMES], [vars_[n] for n in SMALL_NAMES])
    loss = res[0][0, 0]
    grads, delta, new_m, new_v = {}, {}, {}, {}
    for k, n in enumerate(("conv_w",) + SMALL_NAMES):
        grads[n], delta[n], new_m[n], new_v[n] = res[1 + 4 * k:5 + 4 * k]
    for dd, a_in, a_out in ((grads, g_w_in, g_w_out), (delta, d_w_in, d_w_out), (new_m, nm_w_in, nm_w_out),
                            (new_v, nv_w_in, nv_w_out)):
        dd["w_in"] = a_in
        dd["w_out"] = a_out
    order = ("w_in", "conv_w", "conv_b", "dt_bias", "a_log", "d_skip", "ssd_norm_w", "attn_sinks", "w_out", "ln_g", "ln_b")
    return (loss, grad_x[None], *[grads[n] for n in order], *[delta[n] for n in order], *[new_m[n] for n in order],
            *[new_v[n] for n in order])
```

```python
import functools

import numpy as np
import jax
import jax.numpy as jnp
from jax import lax
from jax.experimental import pallas as pl
from jax.experimental.pallas import tpu as pltpu

F32 = jnp.float32
BF16 = jnp.bfloat16
MESH = pl.DeviceIdType.MESH

D_MODEL = 1024
D_SSD = 1024
D_ATT = 1024
D_MIX = 2048
SSD_HEADS = 16
SSD_P = 64
SSD_GROUPS = 2
SSD_R = 8
SSD_N = 128
D_BC = 256
D_XBC = 1536
CONV_K = 4
CHUNK = 128
ATT_HD = 64
ATT_QH = 16
ATT_KVH = 4
ATT_R = 4
D_KV = 256
WINDOW = 128
ROPE_THETA = 500000.0
ROPE_DIM = 16
ALPHA = 2.0 ** 0.25
LN_EPS = 1e-5
RMS_EPS = 1e-5
D_IN_PROJ = 5136
O_Z, O_XBC, O_DT, O_Q, O_K, O_V, O_G = 0, 1024, 2560, 2576, 3600, 3856, 4112
P_Z, P_G, P_Q, P_XBC, P_KV, P_DT, P_END = 0, 1024, 2048, 3072, 4608, 5120, 5248
DT_PAD = 128
N_CHIPS = 4
W_IN_COLS = D_IN_PROJ // N_CHIPS
SLAB_ROWS = 1312
W_OUT_ROWS = D_MIX // N_CHIPS
CONV_COLS = D_XBC // N_CHIPS

ADAM_LR = 0.001
ADAM_B1 = 0.9
ADAM_B2 = 0.999
ADAM_EPS = 1e-08
ADAM_WD = 0.01
ADAM_STEP = 10

VMEM_LIMIT = 56 * 1024 * 1024
ROW_TILE = 512
NEG_BIG = -1e30
HI = lax.Precision.HIGHEST


def _cparams(sem=None, **kw):
    if sem is not None:
        kw["dimension_semantics"] = sem
    return pltpu.CompilerParams(vmem_limit_bytes=VMEM_LIMIT, **kw)


def _dot(a, b):
    return jnp.dot(a, b, preferred_element_type=F32)


def _dot_nt(a, b):
    return lax.dot_general(a, b, (((1,), (1,)), ((), ())), preferred_element_type=F32)


def _dot_tn(a, b):
    return lax.dot_general(a, b, (((0,), (0,)), ((), ())), preferred_element_type=F32)


def _bf(a):
    return a.astype(BF16)


def _iota2(shape, dim):
    return lax.broadcasted_iota(jnp.int32, shape, dim)


def _to_rows(col):
    k = col.shape[1]
    eye = (_iota2((k, k), 0) == _iota2((k, k), 1)).astype(F32)
    return lax.dot_general(eye, col, (((1,), (1,)), ((), ())), preferred_element_type=F32, precision=HI)


def _to_cols(row):
    n = row.shape[1]
    eye = (_iota2((n, n), 0) == _iota2((n, n), 1)).astype(F32)
    return lax.dot_general(eye, row, (((1,), (1,)), ((), ())), preferred_element_type=F32, precision=HI)


def _sigmoid(x):
    return jax.nn.sigmoid(x)


def _in_proj(x, w, pos, inv):
    L = x.shape[0]
    tm = ROW_TILE
    widths = (D_SSD, D_ATT, D_ATT, D_XBC, 2 * D_KV, DT_PAD)

    def body(x_ref, w_ref, pos_ref, inv_ref, z_ref, g_ref, q_ref, xbc_ref, kv_ref, dt_ref, xb_ref):
        xb = _bf(x_ref[...])
        xb_ref[...] = xb
        for o_ref, off, wd in zip((z_ref, g_ref, xbc_ref, dt_ref), (P_Z, P_G, P_XBC, P_DT), (D_SSD, D_ATT, D_XBC, DT_PAD)):
            o_ref[...] = _dot_nt(xb, w_ref[off:off + wd, :])
        tabs = _rope_tables(pos_ref, inv_ref)
        q_ref[...] = _bf(_rope(_dot_nt(xb, w_ref[P_Q:P_Q + D_ATT, :]), tabs))
        kv_ref[:, 0:D_KV] = _bf(_rope(_dot_nt(xb, w_ref[P_KV:P_KV + D_KV, :]), tabs))
        kv_ref[:, D_KV:2 * D_KV] = _bf(_dot_nt(xb, w_ref[P_KV + D_KV:P_KV + 2 * D_KV, :]))

    row = lambda wd: pl.BlockSpec((tm, wd), lambda i: (i, 0))
    return pl.pallas_call(
        body, name="in_proj", grid=(L // tm,),
        in_specs=[row(D_MODEL), pl.BlockSpec((P_END, D_MODEL), lambda i: (0, 0), pipeline_mode=pl.Buffered(1)), row(1),
                  pl.BlockSpec((1, 2 * ATT_HD), lambda i: (0, 0))],
        out_specs=[row(wd) for wd in widths] + [row(D_MODEL)],
        out_shape=[jax.ShapeDtypeStruct((L, wd), dt) for wd, dt in zip(widths, (F32, F32, BF16, F32, BF16, F32))]
        + [jax.ShapeDtypeStruct((L, D_MODEL), BF16)],
        compiler_params=_cparams(("parallel",)),
    )(x, w, pos, inv)


def _matmuls_tn(a_list, b, name, out_dtype=F32):
    K, N = b.shape
    tk = min(K, 1024)
    nk = K // tk
    n = len(a_list)
    in_place = out_dtype == F32

    def body(*refs):
        b_ref = refs[n]
        o_refs = refs[n + 1:2 * n + 1]
        acc_refs = o_refs if in_place else refs[2 * n + 1:]
        k = pl.program_id(0)
        bb = _bf(b_ref[...])
        for a_ref, o_ref, acc_ref in zip(refs[:n], o_refs, acc_refs):
            part = _dot_tn(_bf(a_ref[...]), bb)

            @pl.when(k == 0)
            def _():
                acc_ref[...] = part

            @pl.when(k > 0)
            def _():
                acc_ref[...] += part

            if not in_place:
                @pl.when(k == nk - 1)
                def _():
                    o_ref[...] = acc_ref[...].astype(out_dtype)

    return pl.pallas_call(
        body, name=name, grid=(nk,),
        in_specs=[pl.BlockSpec((tk, a.shape[1]), lambda k: (k, 0)) for a in a_list] + [pl.BlockSpec((tk, N), lambda k: (k, 0))],
        out_specs=[pl.BlockSpec((a.shape[1], N), lambda k: (0, 0)) for a in a_list],
        out_shape=[jax.ShapeDtypeStruct((a.shape[1], N), out_dtype) for a in a_list],
        scratch_shapes=[] if in_place else [pltpu.VMEM((a.shape[1], N), F32) for a in a_list],
        compiler_params=_cparams(("arbitrary",)),
    )(*a_list, b)


def _grad_x(dr, dz, dg, dq, dxbc, dkv, ddt, w, after):
    L = dr.shape[0]
    tm = ROW_TILE
    widths = (D_SSD, D_ATT, D_ATT, D_XBC, 2 * D_KV, DT_PAD)
    offs = (P_Z, P_G, P_Q, P_XBC, P_KV, P_DT)

    def body(dr_ref, dz_ref, dg_ref, dq_ref, dxbc_ref, dkv_ref, ddt_ref, w_ref, after_ref, o_ref):
        acc = ALPHA * dr_ref[...]
        for p_ref, off, wd in zip((dz_ref, dg_ref, dq_ref, dxbc_ref, dkv_ref, ddt_ref), offs, widths):
            acc = acc + _dot(_bf(p_ref[...]), w_ref[off:off + wd, :])
        o_ref[...] = acc

    row = lambda wd: pl.BlockSpec((tm, wd), lambda i: (i, 0))
    return pl.pallas_call(
        body, name="grad_x", grid=(L // tm,),
        in_specs=[row(D_MODEL)] + [row(wd) for wd in widths]
        + [pl.BlockSpec((P_END, D_MODEL), lambda i: (0, 0), pipeline_mode=pl.Buffered(1)),
           pl.BlockSpec((8, 128), lambda i: (0, 0))],
        out_specs=row(D_MODEL),
        out_shape=jax.ShapeDtypeStruct((L, D_MODEL), F32),
        compiler_params=_cparams(("parallel",)),
    )(dr, dz, dg, dq, dxbc, dkv, ddt, w, after)


def _ssd_chunk_pre(first, xbc_ref, tail_ref, dt_ref, cw_ref, cb_ref, dtb_ref, alog_ref, ext):
    tail = jnp.where(first, 0.0, tail_ref[...])
    ext[0:8, :] = tail
    ext[8:8 + CHUNK, :] = xbc_ref[...]
    u = cb_ref[...] + cw_ref[0:1, :] * ext[pl.ds(5, CHUNK), :]
    for k in range(1, CONV_K):
        u = u + cw_ref[k:k + 1, :] * ext[pl.ds(5 + k, CHUNK), :]
    sig = _sigmoid(u)
    xbc = u * sig
    dtraw = dt_ref[:, 0:SSD_HEADS] + dtb_ref[...]
    dt = jax.nn.softplus(dtraw)
    A = -jnp.exp(alog_ref[...])
    a = dt * A
    tril = (_iota2((CHUNK, CHUNK), 0) >= _iota2((CHUNK, CHUNK), 1)).astype(F32)
    acs = jnp.dot(tril, a, preferred_element_type=F32, precision=HI)
    acs_row = _to_rows(acs)
    return u, sig, xbc, dtraw, dt, A, acs, acs_row


HALO = 16


def _shift_matrix(offsets):
    n = CHUNK + HALO
    m = np.zeros((len(offsets) * CHUNK, 2 * n), np.float32)
    for k, off in enumerate(offsets):
        t = np.arange(CHUNK)
        m[k * CHUNK + t, t + off] = 1.0
        m[k * CHUNK + t, n + t + off] = 1.0
    return jnp.asarray(m, BF16)


def _shifted_rows(first_part, second_part, smat_ref):
    h1, l1 = _hi_lo(first_part)
    h2, l2 = _hi_lo(second_part)
    sh = _dot(smat_ref[...], jnp.concatenate([h1, h2, l1, l2], axis=0))
    return sh[0:CHUNK], sh[CHUNK:2 * CHUNK], sh[2 * CHUNK:3 * CHUNK]


def _ssd_chunk_pre2(first, xbc_ref, tail_ref, dt_ref, cw_ref, cb_ref, dtb_ref, alog_ref, smat_ref):
    tail = jnp.where(first, 0.0, tail_ref[...])
    x = xbc_ref[...]
    taps = _shifted_rows(tail, x, smat_ref) + (x,)
    u = cb_ref[...] + cw_ref[0:1, :] * taps[0]
    for k in range(1, CONV_K):
        u = u + cw_ref[k:k + 1, :] * taps[k]
    sig = _sigmoid(u)
    xbc = u * sig
    dtraw = dt_ref[:, 0:SSD_HEADS] + dtb_ref[...]
    dt = jax.nn.softplus(dtraw)
    A = -jnp.exp(alog_ref[...])
    a = dt * A
    tril = (_iota2((CHUNK, CHUNK), 0) >= _iota2((CHUNK, CHUNK), 1)).astype(F32)
    acs = jnp.dot(tril, a, preferred_element_type=F32, precision=HI)
    acs_row = _to_rows(acs)
    return u, sig, xbc, dtraw, dt, A, acs, acs_row, taps


def _ssd_fwd(z, xbc, dtp, conv_w, conv_b, dt_bias, a_log, d_skip, norm_w):
    L = z.shape[0]
    nc = L // CHUNK

    def body(z_ref, xbc_ref, tail_ref, dt_ref, cw_ref, cb_ref, dtb_ref, alog_ref, dsk_ref, nw_ref,
             y_ref, ypre_ref, prev_ref, state, ext, ybuf):
        c = pl.program_id(0)

        @pl.when(c == 0)
        def _():
            state[...] = jnp.zeros_like(state)

        u, sig, xbcv, dtraw, dt, A, acs, acs_row = _ssd_chunk_pre(
            c == 0, xbc_ref, tail_ref, dt_ref, cw_ref, cb_ref, dtb_ref, alog_ref, ext)
        prev_ref[0] = state[...]
        causal = _iota2((CHUNK, CHUNK), 0) >= _iota2((CHUNK, CHUNK), 1)
        alast = acs[CHUNK - 1:CHUNK, :]
        for g in range(SSD_GROUPS):
            Bg = _bf(xbcv[:, D_SSD + SSD_N * g:D_SSD + SSD_N * (g + 1)])
            Cg = _bf(xbcv[:, D_SSD + D_BC + SSD_N * g:D_SSD + D_BC + SSD_N * (g + 1)])
            cb = _dot_nt(Cg, Bg)
            for r in range(SSD_R):
                h = g * SSD_R + r
                hs = slice(SSD_P * h, SSD_P * (h + 1))
                acs_c = acs[:, h:h + 1]
                seg = acs_c - acs_row[h:h + 1, :]
                Lm = jnp.where(causal, jnp.exp(jnp.where(causal, seg, 0.0)), 0.0)
                M = cb * Lm
                xh = xbcv[:, hs]
                X = xh * dt[:, h:h + 1]
                prev_h = state[hs, :]
                ydiag = _dot(_bf(M), _bf(X))
                yoff = _dot_nt(Cg, _bf(prev_h)) * jnp.exp(acs_c)
                al = alast[:, h:h + 1]
                Xd = X * jnp.exp(al - acs_c)
                state[hs, :] = prev_h * jnp.exp(al) + _dot_tn(_bf(Xd), Bg)
                ybuf[:, hs] = ydiag + yoff + dsk_ref[:, h:h + 1] * xh
        y = ybuf[...]
        ypre_ref[...] = y
        zv = z_ref[...]
        yf = y * (zv * _sigmoid(zv))
        half = D_SSD // SSD_GROUPS
        for g in range(SSD_GROUPS):
            gs = slice(half * g, half * (g + 1))
            yg = yf[:, gs]
            ms = jnp.mean(yg * yg, axis=-1, keepdims=True)
            y_ref[:, gs] = _bf(yg * lax.rsqrt(ms + RMS_EPS) * nw_ref[:, gs])

    full = lambda shape: pl.BlockSpec(shape, lambda c: (0, 0))
    return pl.pallas_call(
        body, name="ssd_fwd", grid=(nc,),
        in_specs=[
            pl.BlockSpec((CHUNK, D_SSD), lambda c: (c, 0)),
            pl.BlockSpec((CHUNK, D_XBC), lambda c: (c, 0)),
            pl.BlockSpec((8, D_XBC), lambda c: (jnp.maximum(c * (CHUNK // 8) - 1, 0), 0)),
            pl.BlockSpec((CHUNK, DT_PAD), lambda c: (c, 0)),
            full((CONV_K, D_XBC)), full((1, D_XBC)), full((1, SSD_HEADS)), full((1, SSD_HEADS)), full((1, SSD_HEADS)),
            full((1, D_SSD)),
        ],
        out_specs=[
            pl.BlockSpec((CHUNK, D_SSD), lambda c: (c, 0)),
            pl.BlockSpec((CHUNK, D_SSD), lambda c: (c, 0)),
            pl.BlockSpec((1, SSD_HEADS * SSD_P, SSD_N), lambda c: (c, 0, 0)),
        ],
        out_shape=[
            jax.ShapeDtypeStruct((L, D_SSD), F32),
            jax.ShapeDtypeStruct((L, D_SSD), F32),
            jax.ShapeDtypeStruct((nc, SSD_HEADS * SSD_P, SSD_N), F32),
        ],
        scratch_shapes=[
            pltpu.VMEM((SSD_HEADS * SSD_P, SSD_N), F32),
            pltpu.VMEM((CHUNK + 8, D_XBC), F32),
            pltpu.VMEM((CHUNK, D_SSD), F32),
        ],
        compiler_params=_cparams(("arbitrary",)),
    )(z, xbc, xbc, dtp, conv_w, conv_b, dt_bias, a_log, d_skip, norm_w)


def _ssd_bwd(dy, z, ypre, xbc, dtp, prev, conv_w, conv_b, dt_bias, a_log, d_skip, norm_w):
    L = z.shape[0]
    nc = L // CHUNK

    def body(dy_ref, z_ref, ypre_ref, xbc_ref, tail_ref, dt_ref, prev_ref, cw_ref, cb_ref, dtb_ref, alog_ref, dsk_ref,
             nw_ref, dz_ref, dxbc_ref, ddt_ref, gcw_ref, gcb_ref, gdtb_ref, galog_ref, gdsk_ref, gnw_ref,
             dstate, dhead, ext, ext2, dpost):
        i = pl.program_id(0)
        c = nc - 1 - i

        @pl.when(i == 0)
        def _():
            dstate[...] = jnp.zeros_like(dstate)
            dhead[...] = jnp.zeros_like(dhead)
            gcw_ref[...] = jnp.zeros_like(gcw_ref)
            gcb_ref[...] = jnp.zeros_like(gcb_ref)
            gdtb_ref[...] = jnp.zeros_like(gdtb_ref)
            galog_ref[...] = jnp.zeros_like(galog_ref)
            gdsk_ref[...] = jnp.zeros_like(gdsk_ref)
            gnw_ref[...] = jnp.zeros_like(gnw_ref)

        u, sig, xbcv, dtraw, dt, A, acs, acs_row = _ssd_chunk_pre(
            c == 0, xbc_ref, tail_ref, dt_ref, cw_ref, cb_ref, dtb_ref, alog_ref, ext)

        zv = z_ref[...]
        ypre = ypre_ref[...]
        dyn = dy_ref[...]
        sz = _sigmoid(zv)
        silu_z = zv * sz
        yf = ypre * silu_z
        half = D_SSD // SSD_GROUPS
        dyf_parts = []
        for g in range(SSD_GROUPS):
            gs = slice(half * g, half * (g + 1))
            yg = yf[:, gs]
            rstd = lax.rsqrt(jnp.mean(yg * yg, axis=-1, keepdims=True) + RMS_EPS)
            dout = dyn[:, gs]
            gnw_ref[:, gs] += jnp.sum(dout * yg * rstd, axis=0, keepdims=True)
            dyhat = dout * nw_ref[:, gs]
            dyf_parts.append(rstd * (dyhat - yg * (rstd * rstd) * jnp.mean(dyhat * yg, axis=-1, keepdims=True)))
        dyf = jnp.concatenate(dyf_parts, axis=1)
        dz_ref[...] = _bf(dyf * ypre * (sz * (1.0 + zv * (1.0 - sz))))
        dypre = dyf * silu_z

        causal = _iota2((CHUNK, CHUNK), 0) >= _iota2((CHUNK, CHUNK), 1)
        alast = acs[CHUNK - 1:CHUNK, :]
        lane16 = _iota2((1, SSD_HEADS), 1)
        sub16 = _iota2((SSD_HEADS, 1), 0)
        dacs_col = jnp.zeros((CHUNK, SSD_HEADS), F32)
        dacs_row = jnp.zeros((SSD_HEADS, CHUNK), F32)
        ddt_col = jnp.zeros((CHUNK, SSD_HEADS), F32)
        dalast = jnp.zeros((1, SSD_HEADS), F32)
        gdsk = jnp.zeros((1, SSD_HEADS), F32)
        for g in range(SSD_GROUPS):
            bs = slice(D_SSD + SSD_N * g, D_SSD + SSD_N * (g + 1))
            cs = slice(D_SSD + D_BC + SSD_N * g, D_SSD + D_BC + SSD_N * (g + 1))
            Bg = _bf(xbcv[:, bs])
            Cg = _bf(xbcv[:, cs])
            cb = _dot_nt(Cg, Bg)
            dcb = jnp.zeros((CHUNK, CHUNK), F32)
            dB = jnp.zeros((CHUNK, SSD_N), F32)
            dC = jnp.zeros((CHUNK, SSD_N), F32)
            for r in range(SSD_R):
                h = g * SSD_R + r
                hs = slice(SSD_P * h, SSD_P * (h + 1))
                onehot = (lane16 == h).astype(F32)
                acs_c = acs[:, h:h + 1]
                seg = acs_c - acs_row[h:h + 1, :]
                Lm = jnp.where(causal, jnp.exp(jnp.where(causal, seg, 0.0)), 0.0)
                M = cb * Lm
                xh = xbcv[:, hs]
                dth = dt[:, h:h + 1]
                X = xh * dth
                Xb = _bf(X)
                dyh = dypre[:, hs]
                dyb = _bf(dyh)
                prev_h = prev_ref[0, hs, :]
                prevb = _bf(prev_h)
                dnext = dstate[hs, :]
                dnextb = _bf(dnext)
                al = alast[:, h:h + 1]
                eacs = jnp.exp(acs_c)
                eal = jnp.exp(al)
                dsd = jnp.exp(al - acs_c)
                G = _bf(dyh * eacs)
                dstate[hs, :] = dnext * eal + _dot_tn(G, Cg)
                dC = dC + _dot(G, prevb)
                yoff = _dot_nt(Cg, prevb) * eacs
                dacs_h = jnp.sum(dyh * yoff, axis=-1, keepdims=True)
                BdN = _dot_nt(Bg, dnextb)
                dX = dsd * BdN
                dB = dB + _dot(_bf(X * dsd), dnextb)
                t = jnp.sum(X * BdN, axis=-1, keepdims=True) * dsd
                dacs_h = dacs_h - t
                dal = jnp.sum(t, axis=0, keepdims=True) + jnp.sum(
                    jnp.sum(dnext * prev_h, axis=-1, keepdims=True), axis=0, keepdims=True) * eal
                dM = _dot_nt(dyb, Xb)
                dX = dX + _dot_tn(_bf(M), dyb)
                dseg = dM * M
                dcb = dcb + dM * Lm
                dacs_h = dacs_h + jnp.sum(dseg, axis=-1, keepdims=True)
                dacs_row = dacs_row - jnp.sum(dseg, axis=0, keepdims=True) * (sub16 == h).astype(F32)
                dacs_col = dacs_col + dacs_h * onehot
                dalast = dalast + dal * onehot
                ddt_col = ddt_col + jnp.sum(dX * xh, axis=-1, keepdims=True) * onehot
                gdsk = gdsk + jnp.sum(jnp.sum(dyh * xh, axis=-1, keepdims=True), axis=0, keepdims=True) * onehot
                dpost[:, hs] = dX * dth + dsk_ref[:, h:h + 1] * dyh
            dcbb = _bf(dcb)
            dpost[:, bs] = dB + _dot_tn(dcbb, Cg)
            dpost[:, cs] = dC + _dot(dcbb, Bg)

        is_last = (_iota2((CHUNK, 1), 0) == CHUNK - 1).astype(F32)
        dacs = dacs_col + _to_cols(dacs_row) + is_last * dalast
        triu = (_iota2((CHUNK, CHUNK), 0) <= _iota2((CHUNK, CHUNK), 1)).astype(F32)
        da = jnp.dot(triu, dacs, preferred_element_type=F32, precision=HI)
        ddt_tot = ddt_col + da * A
        galog_ref[...] += jnp.sum(da * dt, axis=0, keepdims=True) * A
        ddtraw = ddt_tot * _sigmoid(dtraw)
        gdtb_ref[...] += jnp.sum(ddtraw, axis=0, keepdims=True)
        gdsk_ref[...] += gdsk
        ddt_ref[...] = jnp.zeros_like(ddt_ref)
        ddt_ref[:, 0:SSD_HEADS] = ddtraw

        dconv = dpost[...] * (sig * (1.0 + u * (1.0 - sig)))
        gcb_ref[...] += jnp.sum(dconv, axis=0, keepdims=True)
        for k in range(CONV_K):
            gcw_ref[k:k + 1, :] += jnp.sum(dconv * ext[pl.ds(5 + k, CHUNK), :], axis=0, keepdims=True)
        ext2[0:CHUNK, :] = dconv
        ext2[CHUNK:CHUNK + 8, :] = dhead[...]
        dx = cw_ref[CONV_K - 1:CONV_K, :] * dconv
        for k in range(CONV_K - 1):
            dx = dx + cw_ref[k:k + 1, :] * ext2[pl.ds(CONV_K - 1 - k, CHUNK), :]
        dxbc_ref[...] = _bf(dx)
        dhead[...] = dconv[0:8, :]

    full = lambda shape: pl.BlockSpec(shape, lambda i: (0, 0))
    rev = lambda wd: pl.BlockSpec((CHUNK, wd), lambda i: (nc - 1 - i, 0))
    return pl.pallas_call(
        body, name="ssd_bwd", grid=(nc,),
        in_specs=[
            rev(D_SSD), rev(D_SSD), rev(D_SSD), rev(D_XBC),
            pl.BlockSpec((8, D_XBC), lambda i: (jnp.maximum((nc - 1 - i) * (CHUNK // 8) - 1, 0), 0)),
            rev(DT_PAD),
            pl.BlockSpec((1, SSD_HEADS * SSD_P, SSD_N), lambda i: (nc - 1 - i, 0, 0)),
            full((CONV_K, D_XBC)), full((1, D_XBC)), full((1, SSD_HEADS)), full((1, SSD_HEADS)), full((1, SSD_HEADS)),
            full((1, D_SSD)),
        ],
        out_specs=[
            rev(D_SSD), rev(D_XBC), rev(DT_PAD),
            full((CONV_K, D_XBC)), full((1, D_XBC)), full((1, SSD_HEADS)), full((1, SSD_HEADS)), full((1, SSD_HEADS)),
            full((1, D_SSD)),
        ],
        out_shape=[
            jax.ShapeDtypeStruct((L, D_SSD), BF16), jax.ShapeDtypeStruct((L, D_XBC), BF16),
            jax.ShapeDtypeStruct((L, DT_PAD), F32),
            jax.ShapeDtypeStruct((CONV_K, D_XBC), F32), jax.ShapeDtypeStruct((1, D_XBC), F32),
            jax.ShapeDtypeStruct((1, SSD_HEADS), F32), jax.ShapeDtypeStruct((1, SSD_HEADS), F32),
            jax.ShapeDtypeStruct((1, SSD_HEADS), F32), jax.ShapeDtypeStruct((1, D_SSD), F32),
        ],
        scratch_shapes=[
            pltpu.VMEM((SSD_HEADS * SSD_P, SSD_N), F32),
            pltpu.VMEM((8, D_XBC), F32),
            pltpu.VMEM((CHUNK + 8, D_XBC), F32),
            pltpu.VMEM((CHUNK + 8, D_XBC), F32),
            pltpu.VMEM((CHUNK, D_XBC), F32),
        ],
        compiler_params=_cparams(("arbitrary",)),
    )(dy, z, ypre, xbc, xbc, dtp, prev, conv_w, conv_b, dt_bias, a_log, d_skip, norm_w)


def _head_expander():
    return (_iota2((SSD_HEADS, D_SSD), 1) // SSD_P == _iota2((SSD_HEADS, D_SSD), 0)).astype(BF16)


def _hi_lo(x):
    hi = _bf(x)
    return hi, _bf(x - hi.astype(F32))


def _expand(v, e):
    hi, lo = _hi_lo(v)
    return _dot(hi, e) + _dot(lo, e)


def _headsum(t, e):
    m = t.shape[0]
    if m < 8:
        t = jnp.broadcast_to(t[0:1], (8, t.shape[1]))
    hi, lo = _hi_lo(t)
    return (_dot_nt(hi, e) + _dot_nt(lo, e))[0:m]


def _ssd_decays(dt, acs, dsk_ref, e):
    alast = acs[CHUNK - 1:CHUNK, :]
    stk = jnp.concatenate([dt, jnp.exp(acs), jnp.exp(alast - acs),
                           jnp.broadcast_to(jnp.exp(alast), (8, SSD_HEADS)),
                           jnp.broadcast_to(dsk_ref[...], (8, SSD_HEADS))], axis=0)
    ex = _expand(stk, e)
    return (ex[0:CHUNK], ex[CHUNK:2 * CHUNK], ex[2 * CHUNK:3 * CHUNK], ex[3 * CHUNK:3 * CHUNK + 1],
            ex[3 * CHUNK + 8:3 * CHUNK + 9])


def _ssd_fwd2(z, xbc, dtp, conv_w, conv_b, dt_bias, a_log, d_skip, norm_w):
    L = z.shape[0]
    nc = L // CHUNK
    half = D_SSD // SSD_GROUPS

    def body(z_ref, xbc_ref, tail_ref, dt_ref, cw_ref, cb_ref, dtb_ref, alog_ref, dsk_ref, nw_ref, smat_ref,
             y_ref, ypre_ref, prev_ref, state, ybuf, mbuf):
        c = pl.program_id(0)

        @pl.when(c == 0)
        def _():
            state[...] = jnp.zeros_like(state)

        u, sig, xbcv, dtraw, dt, A, acs, acs_row, _ = _ssd_chunk_pre2(
            c == 0, xbc_ref, tail_ref, dt_ref, cw_ref, cb_ref, dtb_ref, alog_ref, smat_ref)
        e = _head_expander()
        dtE, eacsE, dsdE, ealE, dskE = _ssd_decays(dt, acs, dsk_ref, e)
        xs = xbcv[:, 0:D_SSD]
        X = xs * dtE
        prev_ref[0] = state[...]
        causal = _iota2((CHUNK, CHUNK), 0) >= _iota2((CHUNK, CHUNK), 1)
        for g in range(SSD_GROUPS):
            gs = slice(half * g, half * (g + 1))
            Bg = _bf(xbcv[:, D_SSD + SSD_N * g:D_SSD + SSD_N * (g + 1)])
            Cg = _bf(xbcv[:, D_SSD + D_BC + SSD_N * g:D_SSD + D_BC + SSD_N * (g + 1)])
            cb = _dot_nt(Cg, Bg)
            for r in range(SSD_R):
                h = g * SSD_R + r
                seg = acs[:, h:h + 1] - acs_row[h:h + 1, :]
                mbuf[h] = _bf(cb * jnp.where(causal, jnp.exp(jnp.where(causal, seg, 0.0)), 0.0))
            st = state[:, gs]
            ybuf[:, gs] = _dot(Cg, _bf(st)) * eacsE[:, gs] + dskE[:, gs] * xs[:, gs]
            state[:, gs] = st * ealE[:, gs] + _dot_tn(Bg, _bf(X[:, gs] * dsdE[:, gs]))
        Xb = _bf(X)
        for h in range(SSD_HEADS):
            hs = slice(SSD_P * h, SSD_P * (h + 1))
            ybuf[:, hs] += _dot(mbuf[h], Xb[:, hs])
        y = ybuf[...]
        ypre_ref[...] = y
        zv = z_ref[...]
        yf = y * (zv * _sigmoid(zv))
        for g in range(SSD_GROUPS):
            gs = slice(half * g, half * (g + 1))
            yg = yf[:, gs]
            ms = jnp.mean(yg * yg, axis=-1, keepdims=True)
            y_ref[:, gs] = _bf(yg * lax.rsqrt(ms + RMS_EPS) * nw_ref[:, gs])

    full = lambda shape: pl.BlockSpec(shape, lambda c: (0, 0))
    return pl.pallas_call(
        body, name="ssd_fwd", grid=(nc,),
        in_specs=[
            pl.BlockSpec((CHUNK, D_SSD), lambda c: (c, 0)),
            pl.BlockSpec((CHUNK, D_XBC), lambda c: (c, 0)),
            pl.BlockSpec((HALO, D_XBC), lambda c: (jnp.maximum(c * (CHUNK // HALO) - 1, 0), 0)),
            pl.BlockSpec((CHUNK, DT_PAD), lambda c: (c, 0)),
            full((CONV_K, D_XBC)), full((1, D_XBC)), full((1, SSD_HEADS)), full((1, SSD_HEADS)), full((1, SSD_HEADS)),
            full((1, D_SSD)), full((3 * CHUNK, 2 * (CHUNK + HALO))),
        ],
        out_specs=[
            pl.BlockSpec((CHUNK, D_SSD), lambda c: (c, 0)),
            pl.BlockSpec((CHUNK, D_SSD), lambda c: (c, 0)),
            pl.BlockSpec((1, SSD_N, D_SSD), lambda c: (c, 0, 0)),
        ],
        out_shape=[
            jax.ShapeDtypeStruct((L, D_SSD), BF16),
            jax.ShapeDtypeStruct((L, D_SSD), F32),
            jax.ShapeDtypeStruct((nc, SSD_N, D_SSD), F32),
        ],
        scratch_shapes=[
            pltpu.VMEM((SSD_N, D_SSD), F32),
            pltpu.VMEM((CHUNK, D_SSD), F32),
            pltpu.VMEM((SSD_HEADS, CHUNK, CHUNK), BF16),
        ],
        compiler_params=_cparams(("arbitrary",)),
    )(z, xbc, xbc, dtp, conv_w, conv_b, dt_bias, a_log, d_skip, norm_w, _shift_matrix((13, 14, 15)))


def _ssd_bwd2(dy, z, ypre, xbc, dtp, prev, conv_w, conv_b, dt_bias, a_log, d_skip, norm_w):
    L = z.shape[0]
    nc = L // CHUNK
    half = D_SSD // SSD_GROUPS

    def body(dy_ref, z_ref, ypre_ref, xbc_ref, tail_ref, dt_ref, prev_ref, cw_ref, cb_ref, dtb_ref, alog_ref, dsk_ref,
             nw_ref, smat_ref, smat2_ref, dz_ref, dxbc_ref, ddt_ref, gcw_ref, gcb_ref, gdtb_ref, galog_ref, gdsk_ref,
             gnw_ref, dstate, dhead, dpost, yobuf, bdbuf, lmbuf, dmbuf, cbbuf):
        i = pl.program_id(0)
        c = nc - 1 - i

        @pl.when(i == 0)
        def _():
            dstate[...] = jnp.zeros_like(dstate)
            dhead[...] = jnp.zeros_like(dhead)
            gcw_ref[...] = jnp.zeros_like(gcw_ref)
            gcb_ref[...] = jnp.zeros_like(gcb_ref)
            gdtb_ref[...] = jnp.zeros_like(gdtb_ref)
            galog_ref[...] = jnp.zeros_like(galog_ref)
            gdsk_ref[...] = jnp.zeros_like(gdsk_ref)
            gnw_ref[...] = jnp.zeros_like(gnw_ref)

        u, sig, xbcv, dtraw, dt, A, acs, acs_row, taps = _ssd_chunk_pre2(
            c == 0, xbc_ref, tail_ref, dt_ref, cw_ref, cb_ref, dtb_ref, alog_ref, smat_ref)
        e = _head_expander()
        dtE, eacsE, dsdE, ealE, dskE = _ssd_decays(dt, acs, dsk_ref, e)
        alast = acs[CHUNK - 1:CHUNK, :]
        xs = xbcv[:, 0:D_SSD]
        X = xs * dtE
        Xb = _bf(X)

        zv = z_ref[...]
        ypre = ypre_ref[...]
        dyn = dy_ref[...]
        sz = _sigmoid(zv)
        silu_z = zv * sz
        yf = ypre * silu_z
        dyf_parts = []
        for g in range(SSD_GROUPS):
            gs = slice(half * g, half * (g + 1))
            yg = yf[:, gs]
            rstd = lax.rsqrt(jnp.mean(yg * yg, axis=-1, keepdims=True) + RMS_EPS)
            dout = dyn[:, gs]
            gnw_ref[:, gs] += jnp.sum(dout * yg * rstd, axis=0, keepdims=True)
            dyhat = dout * nw_ref[:, gs]
            dyf_parts.append(rstd * (dyhat - yg * (rstd * rstd) * jnp.mean(dyhat * yg, axis=-1, keepdims=True)))
        dyf = jnp.concatenate(dyf_parts, axis=1)
        dz_ref[...] = _bf(dyf * ypre * (sz * (1.0 + zv * (1.0 - sz))))
        dyp = dyf * silu_z
        dyb = _bf(dyp)
        G = dyp * eacsE

        causal = _iota2((CHUNK, CHUNK), 0) >= _iota2((CHUNK, CHUNK), 1)
        ST = prev_ref[0]
        dST = dstate[...]
        for g in range(SSD_GROUPS):
            gs = slice(half * g, half * (g + 1))
            bs = slice(D_SSD + SSD_N * g, D_SSD + SSD_N * (g + 1))
            cs = slice(D_SSD + D_BC + SSD_N * g, D_SSD + D_BC + SSD_N * (g + 1))
            Bg = _bf(xbcv[:, bs])
            Cg = _bf(xbcv[:, cs])
            Gb = _bf(G[:, gs])
            STb = _bf(ST[:, gs])
            dSTb = _bf(dST[:, gs])
            dstate[:, gs] = dST[:, gs] * ealE[:, gs] + _dot_tn(Cg, Gb)
            yobuf[:, gs] = _dot(Cg, STb) * eacsE[:, gs]
            bdbuf[:, gs] = _dot(Bg, dSTb)
            dpost[:, cs] = _dot_nt(Gb, STb)
            dpost[:, bs] = _dot_nt(_bf(X[:, gs] * dsdE[:, gs]), dSTb)
            cbbuf[g] = _dot_nt(Cg, Bg)
            for r in range(SSD_R):
                h = g * SSD_R + r
                seg = acs[:, h:h + 1] - acs_row[h:h + 1, :]
                lmbuf[h] = jnp.where(causal, jnp.exp(jnp.where(causal, seg, 0.0)), 0.0)
        for h in range(SSD_HEADS):
            hs = slice(SSD_P * h, SSD_P * (h + 1))
            Mb = _bf(cbbuf[h // SSD_R] * lmbuf[h])
            dmbuf[h] = _dot_nt(dyb[:, hs], Xb[:, hs])
            dpost[:, hs] = _dot_tn(Mb, dyb[:, hs])
        lane16 = _iota2((1, SSD_HEADS), 1)
        sub16 = _iota2((SSD_HEADS, 1), 0)
        dacs_col = jnp.zeros((CHUNK, SSD_HEADS), F32)
        dacs_row = jnp.zeros((SSD_HEADS, CHUNK), F32)
        for g in range(SSD_GROUPS):
            bs = slice(D_SSD + SSD_N * g, D_SSD + SSD_N * (g + 1))
            cs = slice(D_SSD + D_BC + SSD_N * g, D_SSD + D_BC + SSD_N * (g + 1))
            cb = cbbuf[g]
            dcb = jnp.zeros((CHUNK, CHUNK), F32)
            for r in range(SSD_R):
                h = g * SSD_R + r
                dM = dmbuf[h]
                Lm = lmbuf[h]
                dcb = dcb + dM * Lm
                dseg = dM * (cb * Lm)
                dacs_col = dacs_col + jnp.sum(dseg, axis=-1, keepdims=True) * (lane16 == h).astype(F32)
                dacs_row = dacs_row - jnp.sum(dseg, axis=0, keepdims=True) * (sub16 == h).astype(F32)
            dcbb = _bf(dcb)
            dpost[:, bs] += _dot_tn(dcbb, _bf(xbcv[:, cs]))
            dpost[:, cs] += _dot(dcbb, _bf(xbcv[:, bs]))

        BD = bdbuf[...]
        dX = dpost[:, 0:D_SSD] + dsdE * BD
        dsd = jnp.exp(alast - acs)
        T = _headsum(X * BD, e) * dsd
        dalast = jnp.sum(T, axis=0, keepdims=True) + _headsum(
            jnp.sum(dST * ST, axis=0, keepdims=True), e) * jnp.exp(alast)
        is_last = (_iota2((CHUNK, 1), 0) == CHUNK - 1).astype(F32)
        dacs = dacs_col + _to_cols(dacs_row) + _headsum(dyp * yobuf[...], e) - T + is_last * dalast
        triu = (_iota2((CHUNK, CHUNK), 0) <= _iota2((CHUNK, CHUNK), 1)).astype(F32)
        da = jnp.dot(triu, dacs, preferred_element_type=F32, precision=HI)
        ddt_tot = _headsum(dX * xs, e) + da * A
        galog_ref[...] += jnp.sum(da * dt, axis=0, keepdims=True) * A
        ddtraw = ddt_tot * _sigmoid(dtraw)
        gdtb_ref[...] += jnp.sum(ddtraw, axis=0, keepdims=True)
        gdsk_ref[...] += _headsum(jnp.sum(dyp * xs, axis=0, keepdims=True), e)
        ddt_ref[...] = jnp.zeros_like(ddt_ref)
        ddt_ref[:, 0:SSD_HEADS] = ddtraw
        dpost[:, 0:D_SSD] = dX * dtE + dskE * dyp

        dconv = dpost[...] * (sig * (1.0 + u * (1.0 - sig)))
        gcb_ref[...] += jnp.sum(dconv, axis=0, keepdims=True)
        for k in range(CONV_K):
            gcw_ref[k:k + 1, :] += jnp.sum(dconv * taps[k], axis=0, keepdims=True)
        later = _shifted_rows(dconv, dhead[...], smat2_ref)
        dx = cw_ref[CONV_K - 1:CONV_K, :] * dconv
        for k in range(CONV_K - 1):
            dx = dx + cw_ref[k:k + 1, :] * later[k]
        dxbc_ref[...] = _bf(dx)
        dhead[...] = dconv[0:HALO, :]

    full = lambda shape: pl.BlockSpec(shape, lambda i: (0, 0))
    rev = lambda wd: pl.BlockSpec((CHUNK, wd), lambda i: (nc - 1 - i, 0))
    return pl.pallas_call(
        body, name="ssd_bwd", grid=(nc,),
        in_specs=[
            rev(D_SSD), rev(D_SSD), rev(D_SSD), rev(D_XBC),
            pl.BlockSpec((HALO, D_XBC), lambda i: (jnp.maximum((nc - 1 - i) * (CHUNK // HALO) - 1, 0), 0)),
            rev(DT_PAD),
            pl.BlockSpec((1, SSD_N, D_SSD), lambda i: (nc - 1 - i, 0, 0)),
            full((CONV_K, D_XBC)), full((1, D_XBC)), full((1, SSD_HEADS)), full((1, SSD_HEADS)), full((1, SSD_HEADS)),
            full((1, D_SSD)), full((3 * CHUNK, 2 * (CHUNK + HALO))), full((3 * CHUNK, 2 * (CHUNK + HALO))),
        ],
        out_specs=[
            rev(D_SSD), rev(D_XBC), rev(DT_PAD),
            full((CONV_K, D_XBC)), full((1, D_XBC)), full((1, SSD_HEADS)), full((1, SSD_HEADS)), full((1, SSD_HEADS)),
            full((1, D_SSD)),
        ],
        out_shape=[
            jax.ShapeDtypeStruct((L, D_SSD), BF16), jax.ShapeDtypeStruct((L, D_XBC), BF16),
            jax.ShapeDtypeStruct((L, DT_PAD), F32),
            jax.ShapeDtypeStruct((CONV_K, D_XBC), F32), jax.ShapeDtypeStruct((1, D_XBC), F32),
            jax.ShapeDtypeStruct((1, SSD_HEADS), F32), jax.ShapeDtypeStruct((1, SSD_HEADS), F32),
            jax.ShapeDtypeStruct((1, SSD_HEADS), F32), jax.ShapeDtypeStruct((1, D_SSD), F32),
        ],
        scratch_shapes=[
            pltpu.VMEM((SSD_N, D_SSD), F32),
            pltpu.VMEM((HALO, D_XBC), F32),
            pltpu.VMEM((CHUNK, D_XBC), F32),
            pltpu.VMEM((CHUNK, D_SSD), F32),
            pltpu.VMEM((CHUNK, D_SSD), F32),
            pltpu.VMEM((SSD_HEADS, CHUNK, CHUNK), F32),
            pltpu.VMEM((SSD_HEADS, CHUNK, CHUNK), F32),
            pltpu.VMEM((SSD_GROUPS, CHUNK, CHUNK), F32),
        ],
        compiler_params=_cparams(("arbitrary",)),
    )(dy, z, ypre, xbc, xbc, dtp, prev, conv_w, conv_b, dt_bias, a_log, d_skip, norm_w, _shift_matrix((13, 14, 15)),
      _shift_matrix((3, 2, 1)))


def _rope_tables(pos_ref, inv_ref):
    ang = pos_ref[...].astype(F32) * inv_ref[...]
    d = _iota2((1, 2 * ATT_HD), 1) % ATT_HD
    s = jnp.sin(ang)
    return jnp.cos(ang), jnp.where(d < ROPE_DIM // 2, -s, 0.0), jnp.where((d >= ROPE_DIM // 2) & (d < ROPE_DIM), s, 0.0)


def _rope(t, tabs):
    c, s1, s2 = tabs
    n = t.shape[1]
    rep = n // c.shape[1]
    return (t * jnp.tile(c, (1, rep)) + pltpu.roll(t, n - ROPE_DIM // 2, 1) * jnp.tile(s1, (1, rep))
            + pltpu.roll(t, ROPE_DIM // 2, 1) * jnp.tile(s2, (1, rep)))


def _rope_t(t, tabs):
    c, s1, s2 = tabs
    n = t.shape[1]
    rep = n // c.shape[1]
    return (t * jnp.tile(c, (1, rep)) + pltpu.roll(t * jnp.tile(s1, (1, rep)), ROPE_DIM // 2, 1)
            + pltpu.roll(t * jnp.tile(s2, (1, rep)), n - ROPE_DIM // 2, 1))


def _swa_mask(first):
    qi = _iota2((WINDOW, 2 * WINDOW), 0)
    si = _iota2((WINDOW, 2 * WINDOW), 1)
    band = (si > qi) & (si <= qi + WINDOW)
    return band & (jnp.logical_not(first) | (si >= WINDOW))


def _stack_heads(t, j):
    return jnp.concatenate([t[:, ATT_HD * (j * ATT_R + r):ATT_HD * (j * ATT_R + r + 1)] for r in range(ATT_R)], axis=0)


def _stack_cols(ref, j):
    cols = [jnp.broadcast_to(ref[:, j * ATT_R + r:j * ATT_R + r + 1], (WINDOW, 1)) for r in range(ATT_R)]
    return jnp.concatenate(cols, axis=0)


def _swa_mask_t(first):
    si = _iota2((2 * WINDOW, ATT_R * WINDOW), 0)
    qi = _iota2((2 * WINDOW, ATT_R * WINDOW), 1) % WINDOW
    band = (si > qi) & (si <= qi + WINDOW)
    return band & (jnp.logical_not(first) | (si >= WINDOW))


def _head_rows(ref, j, rows=None):
    if ref.shape[0] == 1:
        parts = [jnp.broadcast_to(ref[:, j * ATT_R + r:j * ATT_R + r + 1], (1, WINDOW)) for r in range(ATT_R)]
    else:
        parts = [ref[j * ATT_R + r:j * ATT_R + r + 1, :] for r in range(ATT_R)]
    return jnp.concatenate(parts, axis=1)


def _swa_fwd(q, g, kv, sinks):
    L = q.shape[0]
    nb = L // WINDOW
    scale = ATT_HD ** -0.5

    def body(q_ref, g_ref, kvc_ref, kvp_ref, sink_ref, y_ref, o_ref, lse_ref, otbuf):
        n = pl.program_id(0)
        kk = jnp.concatenate([kvp_ref[:, 0:D_KV], kvc_ref[:, 0:D_KV]], axis=0)
        vv = jnp.concatenate([kvp_ref[:, D_KV:2 * D_KV], kvc_ref[:, D_KV:2 * D_KV]], axis=0)
        valid = _swa_mask_t(n == 0)
        qv = q_ref[...]
        for j in range(ATT_KVH):
            js = slice(ATT_HD * j, ATT_HD * (j + 1))
            st = _dot_nt(kk[:, js], _stack_heads(qv, j)) * scale
            st = jnp.where(valid, st, NEG_BIG)
            sink = _head_rows(sink_ref, j)
            m = jnp.maximum(jnp.max(st, axis=0, keepdims=True), sink)
            p = jnp.exp(st - m)
            denom = jnp.sum(p, axis=0, keepdims=True) + jnp.exp(sink - m)
            ot = _dot_tn(vv[:, js], _bf(p)) * (1.0 / denom)
            lse = m + jnp.log(denom)
            for r in range(ATT_R):
                h = j * ATT_R + r
                otbuf[ATT_HD * h:ATT_HD * (h + 1), :] = ot[:, WINDOW * r:WINDOW * (r + 1)]
                lse_ref[h:h + 1, :] = lse[:, WINDOW * r:WINDOW * (r + 1)]
        o = otbuf[...].T
        o_ref[...] = o
        gv = g_ref[...]
        y_ref[...] = _bf(o * (gv * _sigmoid(gv)))

    cur = lambda wd: pl.BlockSpec((WINDOW, wd), lambda n: (n, 0))
    prv = lambda wd: pl.BlockSpec((WINDOW, wd), lambda n: (jnp.maximum(n - 1, 0), 0))
    return pl.pallas_call(
        body, name="swa_fwd", grid=(nb,),
        in_specs=[cur(D_ATT), cur(D_ATT), cur(2 * D_KV), prv(2 * D_KV), pl.BlockSpec((1, ATT_QH), lambda n: (0, 0))],
        out_specs=[cur(D_ATT), cur(D_ATT), pl.BlockSpec((ATT_QH, WINDOW), lambda n: (0, n))],
        out_shape=[jax.ShapeDtypeStruct((L, D_ATT), BF16), jax.ShapeDtypeStruct((L, D_ATT), F32),
                   jax.ShapeDtypeStruct((ATT_QH, L), F32)],
        scratch_shapes=[pltpu.VMEM((D_ATT, WINDOW), F32)],
        compiler_params=_cparams(("parallel",)),
    )(q, g, kv, kv, sinks)


def _swa_bwd(dy, q, g, kv, o, lse, pos, inv, sinks):
    L = q.shape[0]
    nb = L // WINDOW
    scale = ATT_HD ** -0.5

    def body(dy_ref, q_ref, g_ref, kvc_ref, kvp_ref, o_ref, lse_ref, posc_ref, posp_ref, inv_ref, sink_ref,
             dq_ref, dg_ref, dkv_ref, dsink_ref, carry, dqbuf, dkbuf, dvbuf):
        n = pl.program_id(0)

        @pl.when(n == 0)
        def _():
            dsink_ref[...] = jnp.zeros_like(dsink_ref)

        @pl.when(n < nb)
        def _():
            tc = _rope_tables(posc_ref, inv_ref)
            tp = _rope_tables(posp_ref, inv_ref)
            kk = jnp.concatenate([kvp_ref[:, 0:D_KV], kvc_ref[:, 0:D_KV]], axis=0)
            vv = jnp.concatenate([kvp_ref[:, D_KV:2 * D_KV], kvc_ref[:, D_KV:2 * D_KV]], axis=0)
            valid = _swa_mask_t(n == 0)
            qv = q_ref[...]
            gv = g_ref[...]
            sg = _sigmoid(gv)
            dyv = dy_ref[...]
            ov = o_ref[...]
            dg_ref[...] = _bf(dyv * ov * (sg * (1.0 + gv * (1.0 - sg))))
            do = dyv * (gv * sg)
            dod = do * ov
            ones = jnp.ones((8, ATT_HD), BF16)
            lane16 = _iota2((1, ATT_QH), 1)
            dsink = jnp.zeros((1, ATT_QH), F32)
            for j in range(ATT_KVH):
                js = slice(ATT_HD * j, ATT_HD * (j + 1))
                kj = kk[:, js]
                vj = vv[:, js]
                qs = _stack_heads(qv, j)
                dos = _bf(_stack_heads(do, j))
                hi, lo = _hi_lo(_stack_heads(dod, j))
                delta = (_dot_nt(ones, hi) + _dot_nt(ones, lo))[0:1]
                lse = _head_rows(lse_ref, j)
                st = _dot_nt(kj, qs) * scale
                pt = jnp.exp(jnp.where(valid, st, NEG_BIG) - lse)
                dst = _bf(pt * (_dot_nt(vj, dos) - delta))
                dqt = _dot_tn(kj, dst) * scale
                dkbuf[:, js] = _dot(dst, qs) * scale
                dvbuf[:, js] = _dot(_bf(pt), dos)
                sd = jnp.exp(_head_rows(sink_ref, j) - lse) * delta
                for r in range(ATT_R):
                    h = j * ATT_R + r
                    ls = slice(WINDOW * r, WINDOW * (r + 1))
                    dqbuf[ATT_HD * h:ATT_HD * (h + 1), :] = dqt[:, ls]
                    dsink = dsink - jnp.sum(sd[:, ls], axis=1, keepdims=True) * (lane16 == h).astype(F32)
            dsink_ref[...] += dsink
            dq_ref[...] = _bf(_rope_t(dqbuf[...].T, tc))
            dkp = _rope_t(dkbuf[0:WINDOW, :], tp)
            dkc = _rope_t(dkbuf[WINDOW:2 * WINDOW, :], tc)

            @pl.when(n > 0)
            def _():
                dkv_ref[:, 0:D_KV] = _bf(carry[:, 0:D_KV] + dkp)
                dkv_ref[:, D_KV:2 * D_KV] = _bf(carry[:, D_KV:2 * D_KV] + dvbuf[0:WINDOW, :])

            carry[:, 0:D_KV] = dkc
            carry[:, D_KV:2 * D_KV] = dvbuf[WINDOW:2 * WINDOW, :]

        @pl.when(n == nb)
        def _():
            dkv_ref[...] = _bf(carry[...])

    last = nb - 1
    cur = lambda wd: pl.BlockSpec((WINDOW, wd), lambda n: (jnp.minimum(n, last), 0))
    prv = lambda wd: pl.BlockSpec((WINDOW, wd), lambda n: (jnp.maximum(jnp.minimum(n, last) - 1, 0), 0))
    return pl.pallas_call(
        body, name="swa_bwd", grid=(nb + 1,),
        in_specs=[cur(D_ATT), cur(D_ATT), cur(D_ATT), cur(2 * D_KV), prv(2 * D_KV), cur(D_ATT),
                  pl.BlockSpec((ATT_QH, WINDOW), lambda n: (0, jnp.minimum(n, last))), cur(1), prv(1),
                  pl.BlockSpec((1, 2 * ATT_HD), lambda n: (0, 0)), pl.BlockSpec((1, ATT_QH), lambda n: (0, 0))],
        out_specs=[cur(D_ATT), cur(D_ATT),
                   pl.BlockSpec((WINDOW, 2 * D_KV), lambda n: (jnp.maximum(n - 1, 0), 0)),
                   pl.BlockSpec((1, ATT_QH), lambda n: (0, 0))],
        out_shape=[jax.ShapeDtypeStruct((L, D_ATT), BF16), jax.ShapeDtypeStruct((L, D_ATT), BF16),
                   jax.ShapeDtypeStruct((L, 2 * D_KV), BF16), jax.ShapeDtypeStruct((1, ATT_QH), F32)],
        scratch_shapes=[pltpu.VMEM((WINDOW, 2 * D_KV), F32), pltpu.VMEM((D_ATT, WINDOW), F32),
                        pltpu.VMEM((2 * WINDOW, D_KV), F32), pltpu.VMEM((2 * WINDOW, D_KV), F32)],
        compiler_params=_cparams(("arbitrary",)),
    )(dy, q, g, kv, kv, o, lse, pos, pos, inv, sinks)


def _out_ln_loss(y_ssd, y_att, x, target, w_out, ln_g, ln_b):
    L = x.shape[0]
    tm = ROW_TILE
    inv_d = 1.0 / D_MODEL

    def body(ys_ref, ya_ref, x_ref, t_ref, w_ref, g_ref, b_ref, dr_ref, dys_ref, dya_ref, loss_ref, gg_ref, gb_ref):
        i = pl.program_id(0)

        @pl.when(i == 0)
        def _():
            loss_ref[...] = jnp.zeros_like(loss_ref)
            gg_ref[...] = jnp.zeros_like(gg_ref)
            gb_ref[...] = jnp.zeros_like(gb_ref)

        h = _dot(_bf(ys_ref[...]), w_ref[0:D_SSD, :]) + _dot(_bf(ya_ref[...]), w_ref[D_SSD:D_MIX, :])
        r = ALPHA * x_ref[...] + h
        mu = jnp.mean(r, axis=-1, keepdims=True)
        xc = r - mu
        rstd = lax.rsqrt(jnp.mean(xc * xc, axis=-1, keepdims=True) + LN_EPS)
        xhat = xc * rstd
        gam = g_ref[...]
        diff = xhat * gam + b_ref[...] - t_ref[...]
        part = jnp.sum(jnp.sum(diff * diff, axis=-1, keepdims=True), axis=0, keepdims=True)
        loss_ref[...] += (0.5 * inv_d) * part
        dout = diff * inv_d
        gg_ref[...] += jnp.sum(dout * xhat, axis=0, keepdims=True)
        gb_ref[...] += jnp.sum(dout, axis=0, keepdims=True)
        dxh = dout * gam
        dr = rstd * (dxh - jnp.mean(dxh, axis=-1, keepdims=True) - xhat * jnp.mean(dxh * xhat, axis=-1, keepdims=True))
        dr_ref[...] = dr
        drb = _bf(dr)
        dys_ref[...] = _dot_nt(drb, w_ref[0:D_SSD, :])
        dya_ref[...] = _dot_nt(drb, w_ref[D_SSD:D_MIX, :])

    row = pl.BlockSpec((tm, D_MODEL), lambda i: (i, 0))
    vec = pl.BlockSpec((1, D_MODEL), lambda i: (0, 0))
    return pl.pallas_call(
        body, name="out_ln_loss", grid=(L // tm,),
        in_specs=[row, row, row, row, pl.BlockSpec((D_MIX, D_MODEL), lambda i: (0, 0), pipeline_mode=pl.Buffered(1)), vec, vec],
        out_specs=[row, row, row, pl.BlockSpec((1, 128), lambda i: (0, 0)), vec, vec],
        out_shape=[jax.ShapeDtypeStruct((L, D_MODEL), F32)] * 3 + [jax.ShapeDtypeStruct((1, 128), F32)]
        + [jax.ShapeDtypeStruct((1, D_MODEL), F32)] * 2,
        compiler_params=_cparams(("arbitrary",)),
    )(y_ssd, y_att, x, target, w_out, ln_g, ln_b)


def _local_step(x, pos, target, w, get_w_out, start_w_in, token, conv_w, conv_b, dt_bias, a_log, d_skip, norm_w, sinks,
                ln_g, ln_b):
    inv8 = ROPE_THETA ** (-jnp.arange(0, ROPE_DIM, 2, dtype=F32) / ROPE_DIM)
    inv = jnp.tile(jnp.concatenate([inv8, inv8, jnp.zeros((ATT_HD - ROPE_DIM,), F32)]), 2).reshape(1, 2 * ATT_HD)
    inv = inv + token

    z, g, q, xbc, kv, dtp, xb = _in_proj(x, w, pos, inv)
    y_ssd, y_pre, prev = _ssd_fwd2(z, xbc, dtp, conv_w, conv_b, dt_bias, a_log, d_skip, norm_w)
    y_att, o, lse = _swa_fwd(q, g, kv, sinks)
    w_out = get_w_out(lse)
    dr, dy_ssd, dy_att, loss, g_ln_g, g_ln_b = _out_ln_loss(y_ssd, y_att, x, target, w_out, ln_g, ln_b)
    gw_out_ssd, gw_out_att = _matmuls_tn([y_ssd, y_att], dr, "gw_out", out_dtype=BF16)
    slabs = jnp.concatenate([gw_out_ssd, gw_out_att], axis=0).reshape(N_CHIPS, W_OUT_ROWS, D_MODEL)
    w_out_red = _reduce_w_out_start(slabs, loss)
    inv = inv + w_out_red[16][0:1, :]
    dq, dg, dkv, g_sinks = _swa_bwd(dy_att, q, g, kv, o, lse, pos, inv, sinks)
    dz, dxbc, ddt, g_conv_w, g_conv_b, g_dt_bias, g_a_log, g_d_skip, g_norm_w = _ssd_bwd2(
        dy_ssd, z, y_pre, xbc, dtp, prev, conv_w, conv_b, dt_bias, a_log, d_skip, norm_w)
    gw_z, gw_g, gw_q = _matmuls_tn([dz, dg, dq], xb, "gw_zgq")
    gw_xbc, gw_kv, gw_dt = _matmuls_tn([dxbc, dkv, ddt], xb, "gw_xbc_kv_dt")
    gw_in = jnp.concatenate([gw_z, gw_xbc, gw_dt[0:SSD_HEADS], gw_q, gw_kv, gw_g], axis=0)
    w_in_red = start_w_in(gw_in)
    grad_x = _grad_x(dr, dz, dg, dq, dxbc, dkv, ddt, w, w_in_red[10])
    small = dict(conv_w=g_conv_w, conv_b=g_conv_b, dt_bias=g_dt_bias, a_log=g_a_log, d_skip=g_d_skip,
                 ssd_norm_w=g_norm_w, attn_sinks=g_sinks, ln_g=g_ln_g, ln_b=g_ln_b)
    return loss, grad_x, w_in_red, w_out_red, small


def _mesh_pos():
    return lax.axis_index("x"), lax.axis_index("y"), lax.axis_index("c")


def _gather_weights(w_in_s, conv_w_s):
    def body(win_ref, cw_ref, owin_ref, ocw_ref, send_sems, recv_sems, small_send, small_recv, local_sems):
        x, y, c = _mesh_pos()
        me = 2 * x + y
        sibling = (x, y, 1 - c)
        chips = [(1 - x, y), (x, 1 - y), (1 - x, 1 - y)]
        locals_ = [pltpu.make_async_copy(cw_ref, ocw_ref.at[me], local_sems.at[0])]
        for cp in locals_:
            cp.start()
        started = []
        for t, (src, dst) in enumerate(((win_ref, owin_ref),)):
            hr = src.shape[0] // 2

            def half(ref, hc, hr=hr):
                return ref.at[pl.ds(hc * hr, hr), :]

            for j, (px, py) in enumerate(chips):
                cp = pltpu.make_async_remote_copy(
                    src_ref=half(src, c), dst_ref=half(dst.at[me], c), send_sem=send_sems.at[t, j],
                    recv_sem=recv_sems.at[t, j], device_id=(px, py, c), device_id_type=MESH)
                cp.start()
                started.append(cp)
        for j, (px, py) in enumerate(chips):
            cp = pltpu.make_async_remote_copy(
                src_ref=cw_ref, dst_ref=ocw_ref.at[me], send_sem=small_send.at[j], recv_sem=small_recv.at[j],
                device_id=(px, py, c), device_id_type=MESH)
            cp.start()
            started.append(cp)
        for t, (src, dst) in enumerate(((win_ref, owin_ref),)):
            hr = src.shape[0] // 2
            for j, (px, py) in enumerate(chips):
                src_chip = 2 * px + py
                blk = dst.at[src_chip].at[pl.ds(c * hr, hr), :]
                pltpu.make_async_remote_copy(
                    src_ref=blk, dst_ref=blk, send_sem=send_sems.at[t, j], recv_sem=recv_sems.at[t, j],
                    device_id=(px, py, c), device_id_type=MESH).wait_recv()
                cp = pltpu.make_async_remote_copy(
                    src_ref=blk, dst_ref=blk, send_sem=send_sems.at[t, 3 + j], recv_sem=recv_sems.at[t, 3 + j],
                    device_id=sibling, device_id_type=MESH)
                cp.start()
                started.append(cp)
        for t, (src, dst) in enumerate(((win_ref, owin_ref),)):
            hr = src.shape[0] // 2
            for j, (px, py) in enumerate(chips):
                src_chip = 2 * px + py
                blk = dst.at[src_chip].at[pl.ds((1 - c) * hr, hr), :]
                pltpu.make_async_remote_copy(
                    src_ref=blk, dst_ref=blk, send_sem=send_sems.at[t, 3 + j], recv_sem=recv_sems.at[t, 3 + j],
                    device_id=sibling, device_id_type=MESH).wait_recv()
        for j in range(3):
            pltpu.make_async_remote_copy(
                src_ref=cw_ref, dst_ref=ocw_ref.at[me], send_sem=small_send.at[j], recv_sem=small_recv.at[j],
                device_id=sibling, device_id_type=MESH).wait_recv()
        for cp in started:
            cp.wait_send()
        for cp in locals_:
            cp.wait()

    any_spec = pl.BlockSpec(memory_space=pl.ANY)
    return pl.pallas_call(
        body, name="gather_weights",
        in_specs=[any_spec] * 2, out_specs=[any_spec] * 2,
        out_shape=[jax.ShapeDtypeStruct((N_CHIPS,) + a.shape, a.dtype) for a in (w_in_s, conv_w_s)],
        scratch_shapes=[pltpu.SemaphoreType.DMA((1, 6)), pltpu.SemaphoreType.DMA((1, 6)),
                        pltpu.SemaphoreType.DMA((3,)), pltpu.SemaphoreType.DMA((3,)), pltpu.SemaphoreType.DMA((3,))],
    )(w_in_s, conv_w_s)


_HBM = pl.BlockSpec(memory_space=pltpu.HBM)
_SEM = pl.BlockSpec(memory_space=pltpu.SEMAPHORE)
_EFFECT = pltpu.SideEffectType.DATAFLOW_SIDE_EFFECTING


def _gather_w_out_start(w_out_s, after):
    def body(src_ref, land_ref, after_ref, s0, s1, s2, r0, r1, r2, src_thru, land_thru, token):
        x, y, c = _mesh_pos()
        me = 2 * x + y
        chips = [(1 - x, y), (x, 1 - y), (1 - x, 1 - y)]
        for (px, py), s, r in zip(chips, (s0, s1, s2), (r0, r1, r2)):
            pltpu.make_async_remote_copy(src_ref=src_ref, dst_ref=land_ref.at[me], send_sem=s, recv_sem=r,
                                         device_id=(px, py, c), device_id_type=MESH).start()
        token[...] = jnp.zeros_like(token)

    sem = pltpu.SemaphoreType.DMA(())
    land = lax.empty((N_CHIPS,) + w_out_s.shape, w_out_s.dtype)
    return pl.pallas_call(
        body, name="gather_w_out_start",
        out_shape=(sem,) * 6 + (pltpu.HBM(w_out_s.shape, w_out_s.dtype), pltpu.HBM(land.shape, land.dtype),
                                jax.ShapeDtypeStruct((8, 128), F32)),
        in_specs=(_HBM, _HBM, pl.BlockSpec(memory_space=pl.ANY)),
        out_specs=(_SEM,) * 6 + (_HBM, _HBM, pl.BlockSpec(memory_space=pltpu.VMEM)),
        input_output_aliases={0: 6, 1: 7},
        compiler_params=pltpu.CompilerParams(has_side_effects=_EFFECT),
    )(pltpu.with_memory_space_constraint(w_out_s, pltpu.HBM), pltpu.with_memory_space_constraint(land, pltpu.HBM), after)


def _gather_w_out_wait(sems, src_thru, land_thru, after):
    def body(src_ref, land_ref, s0, s1, s2, r0, r1, r2, after_ref, src_dead, got_ref):
        x, y, c = _mesh_pos()
        chips = [(1 - x, y), (x, 1 - y), (1 - x, 1 - y)]
        for (px, py), s, r in zip(chips, (s0, s1, s2), (r0, r1, r2)):
            cp = pltpu.make_async_remote_copy(src_ref=src_ref, dst_ref=land_ref.at[2 * px + py], send_sem=s, recv_sem=r,
                                              device_id=(px, py, c), device_id_type=MESH)
            cp.wait_send()
            cp.wait_recv()

    return pl.pallas_call(
        body, name="gather_w_out_wait",
        out_shape=(pltpu.HBM(src_thru.shape, src_thru.dtype), pltpu.HBM(land_thru.shape, land_thru.dtype)),
        in_specs=(_HBM, _HBM) + (_SEM,) * 6 + (pl.BlockSpec(memory_space=pl.ANY),),
        out_specs=(_HBM, _HBM), input_output_aliases={0: 0, 1: 1},
        compiler_params=pltpu.CompilerParams(has_side_effects=_EFFECT),
    )(src_thru, land_thru, *sems, after)[1]


def _pair_start(gw_in, after):
    hr = gw_in.shape[1] // 2

    def body(src_ref, land_ref, after_ref, *refs):
        x, y, c = _mesh_pos()
        for j in range(N_CHIPS):
            pltpu.make_async_remote_copy(
                src_ref=src_ref.at[j, pl.ds((1 - c) * hr, hr), :], dst_ref=land_ref.at[j], send_sem=refs[j],
                recv_sem=refs[N_CHIPS + j], device_id=(x, y, 1 - c), device_id_type=MESH).start()
        refs[10][...] = jnp.zeros_like(refs[10])

    sem = pltpu.SemaphoreType.DMA(())
    land = lax.empty((N_CHIPS, hr, D_MODEL), F32)
    return pl.pallas_call(
        body, name="pair_start",
        out_shape=(sem,) * 8 + (pltpu.HBM(gw_in.shape, F32), pltpu.HBM(land.shape, F32), jax.ShapeDtypeStruct((8, 128), F32)),
        in_specs=(_HBM, _HBM, pl.BlockSpec(memory_space=pl.ANY)),
        out_specs=(_SEM,) * 8 + (_HBM, _HBM, pl.BlockSpec(memory_space=pltpu.VMEM)),
        input_output_aliases={0: 8, 1: 9},
        compiler_params=pltpu.CompilerParams(has_side_effects=_EFFECT),
    )(pltpu.with_memory_space_constraint(gw_in, pltpu.HBM), pltpu.with_memory_space_constraint(land, pltpu.HBM), after)


def _pair_wait(sems, gw_thru, land_thru, after):
    hr = land_thru.shape[1]

    def body(src_ref, land_ref, *refs):
        x, y, c = _mesh_pos()
        for j in range(N_CHIPS):
            cp = pltpu.make_async_remote_copy(
                src_ref=src_ref.at[j, pl.ds((1 - c) * hr, hr), :], dst_ref=land_ref.at[j], send_sem=refs[j],
                recv_sem=refs[N_CHIPS + j], device_id=(x, y, 1 - c), device_id_type=MESH)
            cp.wait_send()
            cp.wait_recv()

    return pl.pallas_call(
        body, name="pair_wait",
        out_shape=(pltpu.HBM(gw_thru.shape, F32), pltpu.HBM(land_thru.shape, F32)),
        in_specs=(_HBM, _HBM) + (_SEM,) * 8 + (pl.BlockSpec(memory_space=pl.ANY),),
        out_specs=(_HBM, _HBM), input_output_aliases={0: 0, 1: 1},
        compiler_params=pltpu.CompilerParams(has_side_effects=_EFFECT),
    )(gw_thru, land_thru, *sems, after)


def _small_exchange(small):
    k_small = small.shape[1]

    def body(sm_ref, slots_ref, small_send, small_recv, local_sem):
        x, y, c = _mesh_pos()
        me = 4 * x + 2 * y + c
        sibling = (x, y, 1 - c)
        mine = pltpu.make_async_copy(sm_ref, slots_ref.at[me], local_sem)
        mine.start()
        started = []
        for k in range(1, 8):
            peer = (x ^ ((k >> 2) & 1), y ^ ((k >> 1) & 1), c ^ (k & 1))
            cp = pltpu.make_async_remote_copy(
                src_ref=sm_ref, dst_ref=slots_ref.at[me], send_sem=small_send.at[k - 1], recv_sem=small_recv.at[k - 1],
                device_id=peer, device_id_type=MESH)
            cp.start()
            started.append(cp)
        for k in range(1, 8):
            pltpu.make_async_remote_copy(
                src_ref=sm_ref, dst_ref=slots_ref.at[me], send_sem=small_send.at[k - 1], recv_sem=small_recv.at[k - 1],
                device_id=sibling, device_id_type=MESH).wait_recv()
        for cp in started:
            cp.wait_send()
        mine.wait()

    any_spec = pl.BlockSpec(memory_space=pl.ANY)
    return pl.pallas_call(
        body, name="small_exchange",
        in_specs=[any_spec], out_specs=any_spec,
        out_shape=jax.ShapeDtypeStruct((8, 8, k_small), F32),
        scratch_shapes=[pltpu.SemaphoreType.DMA((7,)), pltpu.SemaphoreType.DMA((7,)), pltpu.SemaphoreType.DMA],
    )(small)


def _chip_exchange(s_in):
    def body(sin_ref, rin_ref, send_sems, recv_sems):
        x, y, c = _mesh_pos()
        me = 2 * x + y
        chips = [(1 - x, y), (x, 1 - y), (1 - x, 1 - y)]
        started = []
        for j, (px, py) in enumerate(chips):
            cp = pltpu.make_async_remote_copy(
                src_ref=sin_ref.at[2 * px + py], dst_ref=rin_ref.at[me], send_sem=send_sems.at[j],
                recv_sem=recv_sems.at[j], device_id=(px, py, c), device_id_type=MESH)
            cp.start()
            started.append(cp)
        for j, (px, py) in enumerate(chips):
            blk = rin_ref.at[2 * px + py]
            pltpu.make_async_remote_copy(
                src_ref=blk, dst_ref=blk, send_sem=send_sems.at[j], recv_sem=recv_sems.at[j],
                device_id=(px, py, c), device_id_type=MESH).wait_recv()
        for cp in started:
            cp.wait_send()

    any_spec = pl.BlockSpec(memory_space=pl.ANY)
    return pl.pallas_call(
        body, name="chip_exchange",
        in_specs=[any_spec], out_specs=any_spec,
        out_shape=jax.ShapeDtypeStruct(s_in.shape, s_in.dtype),
        scratch_shapes=[pltpu.SemaphoreType.DMA((3,)), pltpu.SemaphoreType.DMA((3,))],
    )(s_in)


def _pair_share(h_in):
    def body(hin_ref, rin_ref, send_sem, recv_sem):
        x, y, c = _mesh_pos()
        cp = pltpu.make_async_remote_copy(
            src_ref=hin_ref, dst_ref=rin_ref, send_sem=send_sem, recv_sem=recv_sem,
            device_id=(x, y, 1 - c), device_id_type=MESH)
        cp.start()
        cp.wait()

    any_spec = pl.BlockSpec(memory_space=pl.ANY)
    return pl.pallas_call(
        body, name="pair_share",
        in_specs=[any_spec], out_specs=any_spec,
        out_shape=jax.ShapeDtypeStruct(h_in.shape, F32),
        scratch_shapes=[pltpu.SemaphoreType.DMA, pltpu.SemaphoreType.DMA],
    )(h_in)


def _reduce_w_out_start(slabs, after):
    def body(src_ref, land_ref, after_ref, *refs):
        x, y, c = _mesh_pos()
        me = 4 * x + 2 * y + c
        for k in range(1, 8):
            px, py, pc = x ^ ((k >> 2) & 1), y ^ ((k >> 1) & 1), c ^ (k & 1)
            pltpu.make_async_remote_copy(src_ref=src_ref.at[2 * px + py], dst_ref=land_ref.at[me], send_sem=refs[k - 1],
                                         recv_sem=refs[6 + k], device_id=(px, py, pc), device_id_type=MESH).start()
        refs[16][...] = jnp.zeros_like(refs[16])

    sem = pltpu.SemaphoreType.DMA(())
    land = lax.empty((8,) + slabs.shape[1:], slabs.dtype)
    return pl.pallas_call(
        body, name="reduce_w_out_start",
        out_shape=(sem,) * 14 + (pltpu.HBM(slabs.shape, slabs.dtype), pltpu.HBM(land.shape, land.dtype),
                                 jax.ShapeDtypeStruct((8, 128), F32)),
        in_specs=(_HBM, _HBM, pl.BlockSpec(memory_space=pl.ANY)),
        out_specs=(_SEM,) * 14 + (_HBM, _HBM, pl.BlockSpec(memory_space=pltpu.VMEM)),
        input_output_aliases={0: 14, 1: 15},
        compiler_params=pltpu.CompilerParams(has_side_effects=_EFFECT),
    )(pltpu.with_memory_space_constraint(slabs, pltpu.HBM), pltpu.with_memory_space_constraint(land, pltpu.HBM), after)


def _reduce_w_out_wait(sems, slabs_thru, land_thru, after):
    def body(src_ref, land_ref, *refs):
        x, y, c = _mesh_pos()
        for k in range(1, 8):
            px, py, pc = x ^ ((k >> 2) & 1), y ^ ((k >> 1) & 1), c ^ (k & 1)
            cp = pltpu.make_async_remote_copy(
                src_ref=src_ref.at[2 * px + py], dst_ref=land_ref.at[4 * px + 2 * py + pc], send_sem=refs[k - 1],
                recv_sem=refs[6 + k], device_id=(px, py, pc), device_id_type=MESH)
            cp.wait_send()
            cp.wait_recv()

    return pl.pallas_call(
        body, name="reduce_w_out_wait",
        out_shape=(pltpu.HBM(slabs_thru.shape, slabs_thru.dtype), pltpu.HBM(land_thru.shape, land_thru.dtype)),
        in_specs=(_HBM, _HBM) + (_SEM,) * 14 + (pl.BlockSpec(memory_space=pl.ANY),),
        out_specs=(_HBM, _HBM), input_output_aliases={0: 0, 1: 1},
        compiler_params=pltpu.CompilerParams(has_side_effects=_EFFECT),
    )(slabs_thru, land_thru, *sems, after)


def _pair_add(g, recv, core, name):
    _, rows, C = recv.shape
    tc = 256

    def body(core_ref, g_ref, r_ref, o_ref):
        o_ref[...] = _bf(g_ref[...] + r_ref[...])

    spec = pl.BlockSpec((1, rows, tc), lambda j, i, core: (j, 0, i))
    return pl.pallas_call(
        body, name=name,
        grid_spec=pltpu.PrefetchScalarGridSpec(
            num_scalar_prefetch=1, grid=(N_CHIPS, C // tc),
            in_specs=[pl.BlockSpec((1, rows, tc), lambda j, i, core: (j, core[0], i)), spec], out_specs=spec),
        out_shape=jax.ShapeDtypeStruct((N_CHIPS, rows, C), BF16),
        compiler_params=_cparams(("parallel", "parallel")),
    )(core, g, recv)


def _chip_add(own, parts, chip, name):
    _, rows, C = parts.shape
    tc = 256

    def body(chip_ref, own_ref, r0, r1, r2, r3, o_ref):
        acc = None
        for j, r in enumerate((r0, r1, r2, r3)):
            term = jnp.where(chip_ref[0] == j, own_ref[0], r[0]).astype(F32)
            acc = term if acc is None else acc + term
        o_ref[...] = acc

    def slab(j):
        return pl.BlockSpec((1, rows, tc), lambda i, chip: (jnp.where(chip[0] == j, (j + 1) % N_CHIPS, j), 0, i))

    return pl.pallas_call(
        body, name=name,
        grid_spec=pltpu.PrefetchScalarGridSpec(
            num_scalar_prefetch=1, grid=(C // tc,),
            in_specs=[pl.BlockSpec((1, rows, tc), lambda i, chip: (chip[0], 0, i))] + [slab(j) for j in range(N_CHIPS)],
            out_specs=pl.BlockSpec((rows, tc), lambda i, chip: (0, i))),
        out_shape=jax.ShapeDtypeStruct((rows, C), F32),
        compiler_params=_cparams(("parallel",)),
    )(chip, own, parts, parts, parts, parts)


def _adamw_math(w, g, m, v):
    m = ADAM_B1 * m + (1.0 - ADAM_B1) * g
    v = ADAM_B2 * v + (1.0 - ADAM_B2) * (g * g)
    m_hat = m / (1.0 - ADAM_B1 ** ADAM_STEP)
    v_hat = v / (1.0 - ADAM_B2 ** ADAM_STEP)
    delta = -ADAM_LR * (m_hat / (jnp.sqrt(v_hat) + ADAM_EPS) + ADAM_WD * w)
    return delta, m, v


def _adamw_pair(w, g_own, g_sib, m, v, core, name):
    unit = w.ndim == 3
    R, C = w.shape[0], w.shape[-1]
    rows = g_own.shape[0]
    tc = 128

    def body(core_ref, w_ref, go_ref, gs_ref, m_ref, v_ref, d_ref, nm_ref, nv_ref, g_ref):
        first = core_ref[0] == 0
        own, sib = go_ref[...], gs_ref[...]
        g = jnp.concatenate([jnp.where(first, own, sib), jnp.where(first, sib, own)], axis=0)[0:R, :]
        idx = (slice(None), 0, slice(None)) if unit else (slice(None), slice(None))
        d, nm, nv = _adamw_math(w_ref[idx], g, m_ref[idx], v_ref[idx])
        d_ref[idx] = d
        nm_ref[idx] = nm
        nv_ref[idx] = nv
        g_ref[idx] = g

    if unit:
        spec = pl.BlockSpec((R, 1, tc), lambda i, core: (0, 0, i))
    else:
        spec = pl.BlockSpec((R, tc), lambda i, core: (0, i))
    gspec = pl.BlockSpec((rows, tc), lambda i, core: (0, i))
    return pl.pallas_call(
        body, name=name,
        grid_spec=pltpu.PrefetchScalarGridSpec(
            num_scalar_prefetch=1, grid=(C // tc,),
            in_specs=[spec, gspec, gspec, spec, spec], out_specs=[spec] * 4),
        out_shape=[jax.ShapeDtypeStruct(w.shape, F32)] * 4,
        compiler_params=_cparams(("parallel",)),
    )(core, w, g_own, g_sib, m, v)


def _adamw_sum8(w, slabs, land, m, v, ids, name):
    R, C = w.shape
    tc = 128

    def body(ids_ref, w_ref, own_ref, *refs):
        lrefs, (m_ref, v_ref, d_ref, nm_ref, nv_ref, g_ref) = refs[:8], refs[8:]
        g = None
        for d, l_ref in enumerate(lrefs):
            term = jnp.where(ids_ref[0] == d, own_ref[0], l_ref[0]).astype(F32)
            g = term if g is None else g + term
        dl, nm, nv = _adamw_math(w_ref[...], g, m_ref[...], v_ref[...])
        d_ref[...] = dl
        nm_ref[...] = nm
        nv_ref[...] = nv
        g_ref[...] = g

    def slot(d):
        return pl.BlockSpec((1, R, tc), lambda i, ids: (jnp.where(ids[0] == d, (d + 1) % 8, d), 0, i))

    spec = pl.BlockSpec((R, tc), lambda i, ids: (0, i))
    return pl.pallas_call(
        body, name=name,
        grid_spec=pltpu.PrefetchScalarGridSpec(
            num_scalar_prefetch=1, grid=(C // tc,),
            in_specs=[spec, pl.BlockSpec((1, R, tc), lambda i, ids: (ids[1], 0, i))] + [slot(d) for d in range(8)]
            + [spec, spec],
            out_specs=[spec] * 4),
        out_shape=[jax.ShapeDtypeStruct((R, C), F32)] * 4,
        compiler_params=_cparams(("parallel",)),
    )(ids, w, slabs, *([land] * 8), m, v)


SMALL_NAMES = ("conv_b", "ssd_norm_w", "ln_g", "ln_b", "dt_bias", "a_log", "d_skip", "attn_sinks")
SMALL_SIZES = (D_XBC, D_SSD, D_MODEL, D_MODEL, SSD_HEADS, SSD_HEADS, SSD_HEADS, ATT_QH)
SMALL_OFFS = tuple(D_XBC + sum(-(-n // 128) * 128 for n in SMALL_SIZES[:k]) for k in range(len(SMALL_SIZES)))
LOSS_OFF = D_XBC + sum(-(-n // 128) * 128 for n in SMALL_SIZES)
K_SMALL = LOSS_OFF + 128


def _pack_small(g_conv_w, vecs, loss):
    def body(cw_ref, *refs):
        o_ref = refs[-1]
        o_ref[...] = jnp.zeros_like(o_ref)
        o_ref[0:CONV_K, 0:D_XBC] = cw_ref[...]
        for v_ref, off, n in zip(refs[:-2], SMALL_OFFS, SMALL_SIZES):
            o_ref[0:1, off:off + n] = v_ref[...]
        o_ref[0:1, LOSS_OFF:LOSS_OFF + 128] = refs[-2][...]

    return pl.pallas_call(
        body, name="pack_small", out_shape=jax.ShapeDtypeStruct((8, K_SMALL), F32), compiler_params=_cparams(),
    )(g_conv_w, *vecs, loss)


def _adamw_small(slots, chip, conv_w, m_conv_w, v_conv_w, params, moms, vars_):
    n_vec = len(SMALL_NAMES)

    def body(chip_ref, s_ref, *refs):
        ins = refs[:3 * (n_vec + 1)]
        outs = refs[3 * (n_vec + 1):-1]
        tot_ref = refs[-1]
        tot = s_ref[0]
        for d in range(1, 8):
            tot = tot + s_ref[d]
        outs[0][...] = tot[0:1, LOSS_OFF:LOSS_OFF + 1]
        off = pl.multiple_of(chip_ref[0] * CONV_COLS, 128)
        tot_ref[...] = tot
        grads = [tot_ref[0:CONV_K, pl.ds(off, CONV_COLS)]]
        grads += [tot[0:1, o:o + n] for o, n in zip(SMALL_OFFS, SMALL_SIZES)]
        for k, g in enumerate(grads):
            w_ref, m_ref, v_ref = ins[3 * k:3 * k + 3]
            full = (0,) if k == 0 else (Ellipsis,)
            d, nm, nv = _adamw_math(w_ref[full], g, m_ref[full], v_ref[full])
            for o_ref, val in zip(outs[1 + 4 * k:5 + 4 * k], (g, d, nm, nv)):
                o_ref[full] = val

    args = [conv_w, m_conv_w, v_conv_w]
    for w, m, v in zip(params, moms, vars_):
        args += [w, m, v]
    shapes = [jax.ShapeDtypeStruct((1, 1), F32)] + [jax.ShapeDtypeStruct(conv_w.shape, F32)] * 4
    for w in params:
        shapes += [jax.ShapeDtypeStruct(w.shape, F32)] * 4
    vmem = pl.BlockSpec(memory_space=pltpu.VMEM)
    return pl.pallas_call(
        body, name="adamw_small",
        grid_spec=pltpu.PrefetchScalarGridSpec(
            num_scalar_prefetch=1, grid=(1,),
            in_specs=[pl.BlockSpec(slots.shape, lambda i, chip: (0, 0, 0))] + [vmem] * len(args),
            out_specs=[vmem] * len(shapes), scratch_shapes=[pltpu.VMEM((8, K_SMALL), F32)]),
        out_shape=shapes, compiler_params=_cparams(),
    )(chip, slots, *args)


def kernel(x, positions, w_in, conv_w, conv_b, dt_bias, a_log, d_skip, ssd_norm_w, attn_sinks, w_out, ln_g, ln_b, loss_target, m_w_in, m_conv_w, m_conv_b, m_dt_bias, m_a_log, m_d_skip, m_ssd_norm_w, m_attn_sinks, m_w_out, m_ln_g, m_ln_b, v_w_in, v_conv_w, v_conv_b, v_dt_bias, v_a_log, v_d_skip, v_ssd_norm_w, v_attn_sinks, v_w_out, v_ln_g, v_ln_b):
    mx, my, mc = _mesh_pos()
    chip = 2 * mx + my
    L = x.shape[1]

    conv_w_s8 = jnp.pad(conv_w[0], ((0, 8 - CONV_K), (0, 0)))
    pad_rows = ((0, SLAB_ROWS - W_IN_COLS), (0, 0))
    w_in_t = w_in[0].T
    w_in_b, w_out_b = jnp.pad(_bf(w_in_t), pad_rows), _bf(w_out[0])
    ag_in, ag_cw = _gather_weights(w_in_b, conv_w_s8)
    started = _gather_w_out_start(w_out_b, ag_cw)
    own = (jnp.arange(N_CHIPS) == chip)[:, None, None]

    def get_w_out(after):
        landed = _gather_w_out_wait(started[0:6], started[6], started[7], after)
        return jnp.where(own, w_out_b[None], landed).reshape(D_MIX, D_MODEL)

    ag_in = jnp.where(own, w_in_b[None], ag_in)
    w_full = jnp.concatenate([ag_in[j, 0:W_IN_COLS] for j in range(N_CHIPS)], axis=0)
    w = jnp.concatenate([
        w_full[O_Z:O_Z + D_SSD], w_full[O_G:O_G + D_ATT], w_full[O_Q:O_Q + D_ATT],
        w_full[O_XBC:O_XBC + D_XBC], w_full[O_K:O_K + 2 * D_KV], w_full[O_DT:O_DT + SSD_HEADS],
        jnp.zeros((DT_PAD - SSD_HEADS, D_MODEL), BF16)], axis=0)
    conv_w_full = jnp.concatenate([ag_cw[j, 0:CONV_K] for j in range(N_CHIPS)], axis=1)

    def start_w_in(gw_in):
        slabs = jnp.stack([jnp.pad(gw_in[W_IN_COLS * j:W_IN_COLS * (j + 1)], pad_rows) for j in range(N_CHIPS)])
        return _pair_start(slabs, gw_in[0:8, 0:128])

    loss_part, grad_x, w_in_red, w_out_red, small = _local_step(
        x[0], positions[0].reshape(L, 1), loss_target[0], w, get_w_out, start_w_in, started[8][0:1, :], conv_w_full,
        conv_b, dt_bias, a_log, d_skip, ssd_norm_w, attn_sinks, ln_g, ln_b)

    packed = _pack_small(small["conv_w"], [small[n] for n in SMALL_NAMES], loss_part)
    core_id = mc.reshape(1).astype(jnp.int32)
    chip_id = chip.reshape(1).astype(jnp.int32)
    gw_in_slabs, recv_in = _pair_wait(w_in_red[0:8], w_in_red[8], w_in_red[9], grad_x)
    slots = _small_exchange(packed)
    s_in = _pair_add(gw_in_slabs, recv_in, core_id, "pair_add_in")
    r_in = _chip_exchange(s_in)
    h_in = _chip_add(s_in, r_in, chip_id, "chip_add_in")
    sib_in = _pair_share(h_in)

    to_rows = lambda a: jnp.transpose(a, (2, 0, 1))
    in_t = _adamw_pair(to_rows(w_in), h_in, sib_in, to_rows(m_w_in), to_rows(v_w_in), core_id, "adamw_w_in")
    d_w_in, nm_w_in, nv_w_in, g_w_in = [jnp.transpose(a, (1, 2, 0)) for a in in_t]
    own_slabs, landed = _reduce_w_out_wait(w_out_red[0:14], w_out_red[14], w_out_red[15], sib_in)
    ids = jnp.stack([4 * mx + 2 * my + mc, chip]).astype(jnp.int32)
    out_t = _adamw_sum8(w_out[0], own_slabs, landed, m_w_out[0], v_w_out[0], ids, "adamw_w_out")
    d_w_out, nm_w_out, nv_w_out, g_w_out = [a[None] for a in out_t]

    params = dict(conv_b=conv_b, ssd_norm_w=ssd_norm_w, ln_g=ln_g, ln_b=ln_b, dt_bias=dt_bias, a_log=a_log,
                  d_skip=d_skip, attn_sinks=attn_sinks)
    moms = dict(conv_b=m_conv_b, ssd_norm_w=m_ssd_norm_w, ln_g=m_ln_g, ln_b=m_ln_b, dt_bias=m_dt_bias, a_log=m_a_log,
                d_skip=m_d_skip, attn_sinks=m_attn_sinks)
    vars_ = dict(conv_b=v_conv_b, ssd_norm_w=v_ssd_norm_w, ln_g=v_ln_g, ln_b=v_ln_b, dt_bias=v_dt_bias, a_log=v_a_log,
                 d_skip=v_d_skip, attn_sinks=v_attn_sinks)
    res = _adamw_small(slots, chip_id, conv_w, m_conv_w, v_conv_w, [params[n] for n in SMALL_NAMES],
                       [moms[n] for n in SMALL_NAMES], [vars_[n] for n in SMALL_NAMES])
    loss = res[0][0, 0]
    grads, delta, new_m, new_v = {}, {}, {}, {}
    for k, n in enumerate(("conv_w",) + SMALL_NAMES):
        grads[n], delta[n], new_m[n], new_v[n] = res[1 + 4 * k:5 + 4 * k]
    for dd, a_in, a_out in ((grads, g_w_in, g_w_out), (delta, d_w_in, d_w_out), (new_m, nm_w_in, nm_w_out),
                            (new_v, nv_w_in, nv_w_out)):
        dd["w_in"] = a_in
        dd["w_out"] = a_out
    order = ("w_in", "conv_w", "conv_b", "dt_bias", "a_log", "d_skip", "ssd_norm_w", "attn_sinks", "w_out", "ln_g", "ln_b")
    return (loss, grad_x[None], *[grads[n] for n in order], *[delta[n] for n in order], *[new_m[n] for n in order],
            *[new_v[n] for n in order])
```

```python
import functools

import numpy as np
import jax
import jax.numpy as jnp
from jax import lax
from jax.experimental import pallas as pl
from jax.experimental.pallas import tpu as pltpu

F32 = jnp.float32
BF16 = jnp.bfloat16
MESH = pl.DeviceIdType.MESH

D_MODEL = 1024
D_SSD = 1024
D_ATT = 1024
D_MIX = 2048
SSD_HEADS = 16
SSD_P = 64
SSD_GROUPS = 2
SSD_R = 8
SSD_N = 128
D_BC = 256
D_XBC = 1536
CONV_K = 4
CHUNK = 128
ATT_HD = 64
ATT_QH = 16
ATT_KVH = 4
ATT_R = 4
D_KV = 256
WINDOW = 128
ROPE_THETA = 500000.0
ROPE_DIM = 16
ALPHA = 2.0 ** 0.25
LN_EPS = 1e-5
RMS_EPS = 1e-5
D_IN_PROJ = 5136
O_Z, O_XBC, O_DT, O_Q, O_K, O_V, O_G = 0, 1024, 2560, 2576, 3600, 3856, 4112
P_Z, P_G, P_Q, P_XBC, P_KV, P_DT, P_END = 0, 1024, 2048, 3072, 4608, 5120, 5248
DT_PAD = 128
N_CHIPS = 4
W_IN_COLS = D_IN_PROJ // N_CHIPS
SLAB_ROWS = 1312
W_OUT_ROWS = D_MIX // N_CHIPS
CONV_COLS = D_XBC // N_CHIPS

ADAM_LR = 0.001
ADAM_B1 = 0.9
ADAM_B2 = 0.999
ADAM_EPS = 1e-08
ADAM_WD = 0.01
ADAM_STEP = 10

VMEM_LIMIT = 56 * 1024 * 1024
ROW_TILE = 512
NEG_BIG = -1e30
HI = lax.Precision.HIGHEST


def _cparams(sem=None, **kw):
    if sem is not None:
        kw["dimension_semantics"] = sem
    return pltpu.CompilerParams(vmem_limit_bytes=VMEM_LIMIT, **kw)


def _dot(a, b):
    return jnp.dot(a, b, preferred_element_type=F32)


def _dot_nt(a, b):
    return lax.dot_general(a, b, (((1,), (1,)), ((), ())), preferred_element_type=F32)


def _dot_tn(a, b):
    return lax.dot_general(a, b, (((0,), (0,)), ((), ())), preferred_element_type=F32)


def _bf(a):
    return a.astype(BF16)


def _iota2(shape, dim):
    return lax.broadcasted_iota(jnp.int32, shape, dim)


def _to_rows(col):
    k = col.shape[1]
    eye = (_iota2((k, k), 0) == _iota2((k, k), 1)).astype(F32)
    return lax.dot_general(eye, col, (((1,), (1,)), ((), ())), preferred_element_type=F32, precision=HI)


def _to_cols(row):
    n = row.shape[1]
    eye = (_iota2((n, n), 0) == _iota2((n, n), 1)).astype(F32)
    return lax.dot_general(eye, row, (((1,), (1,)), ((), ())), preferred_element_type=F32, precision=HI)


def _sigmoid(x):
    return jax.nn.sigmoid(x)


def _in_proj(x, w, pos, inv):
    L = x.shape[0]
    tm = ROW_TILE
    widths = (D_SSD, D_ATT, D_ATT, D_XBC, 2 * D_KV, DT_PAD)

    def body(x_ref, w_ref, pos_ref, inv_ref, z_ref, g_ref, q_ref, xbc_ref, kv_ref, dt_ref, xb_ref):
        xb = _bf(x_ref[...])
        xb_ref[...] = xb
        for o_ref, off, wd in zip((z_ref, g_ref, xbc_ref, dt_ref), (P_Z, P_G, P_XBC, P_DT), (D_SSD, D_ATT, D_XBC, DT_PAD)):
            o_ref[...] = _dot_nt(xb, w_ref[off:off + wd, :])
        tabs = _rope_tables(pos_ref, inv_ref)
        q_ref[...] = _bf(_rope(_dot_nt(xb, w_ref[P_Q:P_Q + D_ATT, :]), tabs))
        kv_ref[:, 0:D_KV] = _bf(_rope(_dot_nt(xb, w_ref[P_KV:P_KV + D_KV, :]), tabs))
        kv_ref[:, D_KV:2 * D_KV] = _bf(_dot_nt(xb, w_ref[P_KV + D_KV:P_KV + 2 * D_KV, :]))

    row = lambda wd: pl.BlockSpec((tm, wd), lambda i: (i, 0))
    return pl.pallas_call(
        body, name="in_proj", grid=(L // tm,),
        in_specs=[row(D_MODEL), pl.BlockSpec((P_END, D_MODEL), lambda i: (0, 0), pipeline_mode=pl.Buffered(1)), row(1),
                  pl.BlockSpec((1, 2 * ATT_HD), lambda i: (0, 0))],
        out_specs=[row(wd) for wd in widths] + [row(D_MODEL)],
        out_shape=[jax.ShapeDtypeStruct((L, wd), dt) for wd, dt in zip(widths, (F32, F32, BF16, F32, BF16, F32))]
        + [jax.ShapeDtypeStruct((L, D_MODEL), BF16)],
        compiler_params=_cparams(("parallel",)),
    )(x, w, pos, inv)


def _matmuls_tn(a_list, b, name, out_dtype=F32):
    K, N = b.shape
    tk = min(K, 1024)
    nk = K // tk
    n = len(a_list)
    in_place = out_dtype == F32

    def body(*refs):
        b_ref = refs[n]
        o_refs = refs[n + 1:2 * n + 1]
        acc_refs = o_refs if in_place else refs[2 * n + 1:]
        k = pl.program_id(0)
        bb = _bf(b_ref[...])
        for a_ref, o_ref, acc_ref in zip(refs[:n], o_refs, acc_refs):
            part = _dot_tn(_bf(a_ref[...]), bb)

            @pl.when(k == 0)
            def _():
                acc_ref[...] = part

            @pl.when(k > 0)
            def _():
                acc_ref[...] += part

            if not in_place:
                @pl.when(k == nk - 1)
                def _():
                    o_ref[...] = acc_ref[...].astype(out_dtype)

    return pl.pallas_call(
        body, name=name, grid=(nk,),
        in_specs=[pl.BlockSpec((tk, a.shape[1]), lambda k: (k, 0)) for a in a_list] + [pl.BlockSpec((tk, N), lambda k: (k, 0))],
        out_specs=[pl.BlockSpec((a.shape[1], N), lambda k: (0, 0)) for a in a_list],
        out_shape=[jax.ShapeDtypeStruct((a.shape[1], N), out_dtype) for a in a_list],
        scratch_shapes=[] if in_place else [pltpu.VMEM((a.shape[1], N), F32) for a in a_list],
        compiler_params=_cparams(("arbitrary",)),
    )(*a_list, b)


def _grad_x(dr, dz, dg, dq, dxbc, dkv, ddt, w, after):
    L = dr.shape[0]
    tm = ROW_TILE
    widths = (D_SSD, D_ATT, D_ATT, D_XBC, 2 * D_KV, DT_PAD)
    offs = (P_Z, P_G, P_Q, P_XBC, P_KV, P_DT)

    def body(dr_ref, dz_ref, dg_ref, dq_ref, dxbc_ref, dkv_ref, ddt_ref, w_ref, after_ref, o_ref):
        acc = ALPHA * dr_ref[...]
        for p_ref, off, wd in zip((dz_ref, dg_ref, dq_ref, dxbc_ref, dkv_ref, ddt_ref), offs, widths):
            acc = acc + _dot(_bf(p_ref[...]), w_ref[off:off + wd, :])
        o_ref[...] = acc

    row = lambda wd: pl.BlockSpec((tm, wd), lambda i: (i, 0))
    return pl.pallas_call(
        body, name="grad_x", grid=(L // tm,),
        in_specs=[row(D_MODEL)] + [row(wd) for wd in widths]
        + [pl.BlockSpec((P_END, D_MODEL), lambda i: (0, 0), pipeline_mode=pl.Buffered(1)),
           pl.BlockSpec((8, 128), lambda i: (0, 0))],
        out_specs=row(D_MODEL),
        out_shape=jax.ShapeDtypeStruct((L, D_MODEL), F32),
        compiler_params=_cparams(("parallel",)),
    )(dr, dz, dg, dq, dxbc, dkv, ddt, w, after)


def _ssd_chunk_pre(first, xbc_ref, tail_ref, dt_ref, cw_ref, cb_ref, dtb_ref, alog_ref, ext):
    tail = jnp.where(first, 0.0, tail_ref[...])
    ext[0:8, :] = tail
    ext[8:8 + CHUNK, :] = xbc_ref[...]
    u = cb_ref[...] + cw_ref[0:1, :] * ext[pl.ds(5, CHUNK), :]
    for k in range(1, CONV_K):
        u = u + cw_ref[k:k + 1, :] * ext[pl.ds(5 + k, CHUNK), :]
    sig = _sigmoid(u)
    xbc = u * sig
    dtraw = dt_ref[:, 0:SSD_HEADS] + dtb_ref[...]
    dt = jax.nn.softplus(dtraw)
    A = -jnp.exp(alog_ref[...])
    a = dt * A
    tril = (_iota2((CHUNK, CHUNK), 0) >= _iota2((CHUNK, CHUNK), 1)).astype(F32)
    acs = jnp.dot(tril, a, preferred_element_type=F32, precision=HI)
    acs_row = _to_rows(acs)
    return u, sig, xbc, dtraw, dt, A, acs, acs_row


HALO = 16


def _shift_matrix(offsets):
    n = CHUNK + HALO
    m = np.zeros((len(offsets) * CHUNK, 2 * n), np.float32)
    for k, off in enumerate(offsets):
        t = np.arange(CHUNK)
        m[k * CHUNK + t, t + off] = 1.0
        m[k * CHUNK + t, n + t + off] = 1.0
    return jnp.asarray(m, BF16)


def _shifted_rows(first_part, second_part, smat_ref):
    h1, l1 = _hi_lo(first_part)
    h2, l2 = _hi_lo(second_part)
    sh = _dot(smat_ref[...], jnp.concatenate([h1, h2, l1, l2], axis=0))
    return sh[0:CHUNK], sh[CHUNK:2 * CHUNK], sh[2 * CHUNK:3 * CHUNK]


def _ssd_chunk_pre2(first, xbc_ref, tail_ref, dt_ref, cw_ref, cb_ref, dtb_ref, alog_ref, smat_ref):
    tail = jnp.where(first, 0.0, tail_ref[...])
    x = xbc_ref[...]
    taps = _shifted_rows(tail, x, smat_ref) + (x,)
    u = cb_ref[...] + cw_ref[0:1, :] * taps[0]
    for k in range(1, CONV_K):
        u = u + cw_ref[k:k + 1, :] * taps[k]
    sig = _sigmoid(u)
    xbc = u * sig
    dtraw = dt_ref[:, 0:SSD_HEADS] + dtb_ref[...]
    dt = jax.nn.softplus(dtraw)
    A = -jnp.exp(alog_ref[...])
    a = dt * A
    tril = (_iota2((CHUNK, CHUNK), 0) >= _iota2((CHUNK, CHUNK), 1)).astype(F32)
    acs = jnp.dot(tril, a, preferred_element_type=F32, precision=HI)
    acs_row = _to_rows(acs)
    return u, sig, xbc, dtraw, dt, A, acs, acs_row, taps


def _ssd_fwd(z, xbc, dtp, conv_w, conv_b, dt_bias, a_log, d_skip, norm_w):
    L = z.shape[0]
    nc = L // CHUNK

    def body(z_ref, xbc_ref, tail_ref, dt_ref, cw_ref, cb_ref, dtb_ref, alog_ref, dsk_ref, nw_ref,
             y_ref, ypre_ref, prev_ref, state, ext, ybuf):
        c = pl.program_id(0)

        @pl.when(c == 0)
        def _():
            state[...] = jnp.zeros_like(state)

        u, sig, xbcv, dtraw, dt, A, acs, acs_row = _ssd_chunk_pre(
            c == 0, xbc_ref, tail_ref, dt_ref, cw_ref, cb_ref, dtb_ref, alog_ref, ext)
        prev_ref[0] = state[...]
        causal = _iota2((CHUNK, CHUNK), 0) >= _iota2((CHUNK, CHUNK), 1)
        alast = acs[CHUNK - 1:CHUNK, :]
        for g in range(SSD_GROUPS):
            Bg = _bf(xbcv[:, D_SSD + SSD_N * g:D_SSD + SSD_N * (g + 1)])
            Cg = _bf(xbcv[:, D_SSD + D_BC + SSD_N * g:D_SSD + D_BC + SSD_N * (g + 1)])
            cb = _dot_nt(Cg, Bg)
            for r in range(SSD_R):
                h = g * SSD_R + r
                hs = slice(SSD_P * h, SSD_P * (h + 1))
                acs_c = acs[:, h:h + 1]
                seg = acs_c - acs_row[h:h + 1, :]
                Lm = jnp.where(causal, jnp.exp(jnp.where(causal, seg, 0.0)), 0.0)
                M = cb * Lm
                xh = xbcv[:, hs]
                X = xh * dt[:, h:h + 1]
                prev_h = state[hs, :]
                ydiag = _dot(_bf(M), _bf(X))
                yoff = _dot_nt(Cg, _bf(prev_h)) * jnp.exp(acs_c)
                al = alast[:, h:h + 1]
                Xd = X * jnp.exp(al - acs_c)
                state[hs, :] = prev_h * jnp.exp(al) + _dot_tn(_bf(Xd), Bg)
                ybuf[:, hs] = ydiag + yoff + dsk_ref[:, h:h + 1] * xh
        y = ybuf[...]
        ypre_ref[...] = y
        zv = z_ref[...]
        yf = y * (zv * _sigmoid(zv))
        half = D_SSD // SSD_GROUPS
        for g in range(SSD_GROUPS):
            gs = slice(half * g, half * (g + 1))
            yg = yf[:, gs]
            ms = jnp.mean(yg * yg, axis=-1, keepdims=True)
            y_ref[:, gs] = _bf(yg * lax.rsqrt(ms + RMS_EPS) * nw_ref[:, gs])

    full = lambda shape: pl.BlockSpec(shape, lambda c: (0, 0))
    return pl.pallas_call(
        body, name="ssd_fwd", grid=(nc,),
        in_specs=[
            pl.BlockSpec((CHUNK, D_SSD), lambda c: (c, 0)),
            pl.BlockSpec((CHUNK, D_XBC), lambda c: (c, 0)),
            pl.BlockSpec((8, D_XBC), lambda c: (jnp.maximum(c * (CHUNK // 8) - 1, 0), 0)),
            pl.BlockSpec((CHUNK, DT_PAD), lambda c: (c, 0)),
            full((CONV_K, D_XBC)), full((1, D_XBC)), full((1, SSD_HEADS)), full((1, SSD_HEADS)), full((1, SSD_HEADS)),
            full((1, D_SSD)),
        ],
        out_specs=[
            pl.BlockSpec((CHUNK, D_SSD), lambda c: (c, 0)),
            pl.BlockSpec((CHUNK, D_SSD), lambda c: (c, 0)),
            pl.BlockSpec((1, SSD_HEADS * SSD_P, SSD_N), lambda c: (c, 0, 0)),
        ],
        out_shape=[
            jax.ShapeDtypeStruct((L, D_SSD), F32),
            jax.ShapeDtypeStruct((L, D_SSD), F32),
            jax.ShapeDtypeStruct((nc, SSD_HEADS * SSD_P, SSD_N), F32),
        ],
        scratch_shapes=[
            pltpu.VMEM((SSD_HEADS * SSD_P, SSD_N), F32),
            pltpu.VMEM((CHUNK + 8, D_XBC), F32),
            pltpu.VMEM((CHUNK, D_SSD), F32),
        ],
        compiler_params=_cparams(("arbitrary",)),
    )(z, xbc, xbc, dtp, conv_w, conv_b, dt_bias, a_log, d_skip, norm_w)


def _ssd_bwd(dy, z, ypre, xbc, dtp, prev, conv_w, conv_b, dt_bias, a_log, d_skip, norm_w):
    L = z.shape[0]
    nc = L // CHUNK

    def body(dy_ref, z_ref, ypre_ref, xbc_ref, tail_ref, dt_ref, prev_ref, cw_ref, cb_ref, dtb_ref, alog_ref, dsk_ref,
             nw_ref, dz_ref, dxbc_ref, ddt_ref, gcw_ref, gcb_ref, gdtb_ref, galog_ref, gdsk_ref, gnw_ref,
             dstate, dhead, ext, ext2, dpost):
        i = pl.program_id(0)
        c = nc - 1 - i

        @pl.when(i == 0)
        def _():
            dstate[...] = jnp.zeros_like(dstate)
            dhead[...] = jnp.zeros_like(dhead)
            gcw_ref[...] = jnp.zeros_like(gcw_ref)
            gcb_ref[...] = jnp.zeros_like(gcb_ref)
            gdtb_ref[...] = jnp.zeros_like(gdtb_ref)
            galog_ref[...] = jnp.zeros_like(galog_ref)
            gdsk_ref[...] = jnp.zeros_like(gdsk_ref)
            gnw_ref[...] = jnp.zeros_like(gnw_ref)

        u, sig, xbcv, dtraw, dt, A, acs, acs_row = _ssd_chunk_pre(
            c == 0, xbc_ref, tail_ref, dt_ref, cw_ref, cb_ref, dtb_ref, alog_ref, ext)

        zv = z_ref[...]
        ypre = ypre_ref[...]
        dyn = dy_ref[...]
        sz = _sigmoid(zv)
        silu_z = zv * sz
        yf = ypre * silu_z
        half = D_SSD // SSD_GROUPS
        dyf_parts = []
        for g in range(SSD_GROUPS):
            gs = slice(half * g, half * (g + 1))
            yg = yf[:, gs]
            rstd = lax.rsqrt(jnp.mean(yg * yg, axis=-1, keepdims=True) + RMS_EPS)
            dout = dyn[:, gs]
            gnw_ref[:, gs] += jnp.sum(dout * yg * rstd, axis=0, keepdims=True)
            dyhat = dout * nw_ref[:, gs]
            dyf_parts.append(rstd * (dyhat - yg * (rstd * rstd) * jnp.mean(dyhat * yg, axis=-1, keepdims=True)))
        dyf = jnp.concatenate(dyf_parts, axis=1)
        dz_ref[...] = _bf(dyf * ypre * (sz * (1.0 + zv * (1.0 - sz))))
        dypre = dyf * silu_z

        causal = _iota2((CHUNK, CHUNK), 0) >= _iota2((CHUNK, CHUNK), 1)
        alast = acs[CHUNK - 1:CHUNK, :]
        lane16 = _iota2((1, SSD_HEADS), 1)
        sub16 = _iota2((SSD_HEADS, 1), 0)
        dacs_col = jnp.zeros((CHUNK, SSD_HEADS), F32)
        dacs_row = jnp.zeros((SSD_HEADS, CHUNK), F32)
        ddt_col = jnp.zeros((CHUNK, SSD_HEADS), F32)
        dalast = jnp.zeros((1, SSD_HEADS), F32)
        gdsk = jnp.zeros((1, SSD_HEADS), F32)
        for g in range(SSD_GROUPS):
            bs = slice(D_SSD + SSD_N * g, D_SSD + SSD_N * (g + 1))
            cs = slice(D_SSD + D_BC + SSD_N * g, D_SSD + D_BC + SSD_N * (g + 1))
            Bg = _bf(xbcv[:, bs])
            Cg = _bf(xbcv[:, cs])
            cb = _dot_nt(Cg, Bg)
            dcb = jnp.zeros((CHUNK, CHUNK), F32)
            dB = jnp.zeros((CHUNK, SSD_N), F32)
            dC = jnp.zeros((CHUNK, SSD_N), F32)
            for r in range(SSD_R):
                h = g * SSD_R + r
                hs = slice(SSD_P * h, SSD_P * (h + 1))
                onehot = (lane16 == h).astype(F32)
                acs_c = acs[:, h:h + 1]
                seg = acs_c - acs_row[h:h + 1, :]
                Lm = jnp.where(causal, jnp.exp(jnp.where(causal, seg, 0.0)), 0.0)
                M = cb * Lm
                xh = xbcv[:, hs]
                dth = dt[:, h:h + 1]
                X = xh * dth
                Xb = _bf(X)
                dyh = dypre[:, hs]
                dyb = _bf(dyh)
                prev_h = prev_ref[0, hs, :]
                prevb = _bf(prev_h)
                dnext = dstate[hs, :]
                dnextb = _bf(dnext)
                al = alast[:, h:h + 1]
                eacs = jnp.exp(acs_c)
                eal = jnp.exp(al)
                dsd = jnp.exp(al - acs_c)
                G = _bf(dyh * eacs)
                dstate[hs, :] = dnext * eal + _dot_tn(G, Cg)
                dC = dC + _dot(G, prevb)
                yoff = _dot_nt(Cg, prevb) * eacs
                dacs_h = jnp.sum(dyh * yoff, axis=-1, keepdims=True)
                BdN = _dot_nt(Bg, dnextb)
                dX = dsd * BdN
                dB = dB + _dot(_bf(X * dsd), dnextb)
                t = jnp.sum(X * BdN, axis=-1, keepdims=True) * dsd
                dacs_h = dacs_h - t
                dal = jnp.sum(t, axis=0, keepdims=True) + jnp.sum(
                    jnp.sum(dnext * prev_h, axis=-1, keepdims=True), axis=0, keepdims=True) * eal
                dM = _dot_nt(dyb, Xb)
                dX = dX + _dot_tn(_bf(M), dyb)
                dseg = dM * M
                dcb = dcb + dM * Lm
                dacs_h = dacs_h + jnp.sum(dseg, axis=-1, keepdims=True)
                dacs_row = dacs_row - jnp.sum(dseg, axis=0, keepdims=True) * (sub16 == h).astype(F32)
                dacs_col = dacs_col + dacs_h * onehot
                dalast = dalast + dal * onehot
                ddt_col = ddt_col + jnp.sum(dX * xh, axis=-1, keepdims=True) * onehot
                gdsk = gdsk + jnp.sum(jnp.sum(dyh * xh, axis=-1, keepdims=True), axis=0, keepdims=True) * onehot
                dpost[:, hs] = dX * dth + dsk_ref[:, h:h + 1] * dyh
            dcbb = _bf(dcb)
            dpost[:, bs] = dB + _dot_tn(dcbb, Cg)
            dpost[:, cs] = dC + _dot(dcbb, Bg)

        is_last = (_iota2((CHUNK, 1), 0) == CHUNK - 1).astype(F32)
        dacs = dacs_col + _to_cols(dacs_row) + is_last * dalast
        triu = (_iota2((CHUNK, CHUNK), 0) <= _iota2((CHUNK, CHUNK), 1)).astype(F32)
        da = jnp.dot(triu, dacs, preferred_element_type=F32, precision=HI)
        ddt_tot = ddt_col + da * A
        galog_ref[...] += jnp.sum(da * dt, axis=0, keepdims=True) * A
        ddtraw = ddt_tot * _sigmoid(dtraw)
        gdtb_ref[...] += jnp.sum(ddtraw, axis=0, keepdims=True)
        gdsk_ref[...] += gdsk
        ddt_ref[...] = jnp.zeros_like(ddt_ref)
        ddt_ref[:, 0:SSD_HEADS] = ddtraw

        dconv = dpost[...] * (sig * (1.0 + u * (1.0 - sig)))
        gcb_ref[...] += jnp.sum(dconv, axis=0, keepdims=True)
        for k in range(CONV_K):
            gcw_ref[k:k + 1, :] += jnp.sum(dconv * ext[pl.ds(5 + k, CHUNK), :], axis=0, keepdims=True)
        ext2[0:CHUNK, :] = dconv
        ext2[CHUNK:CHUNK + 8, :] = dhead[...]
        dx = cw_ref[CONV_K - 1:CONV_K, :] * dconv
        for k in range(CONV_K - 1):
            dx = dx + cw_ref[k:k + 1, :] * ext2[pl.ds(CONV_K - 1 - k, CHUNK), :]
        dxbc_ref[...] = _bf(dx)
        dhead[...] = dconv[0:8, :]

    full = lambda shape: pl.BlockSpec(shape, lambda i: (0, 0))
    rev = lambda wd: pl.BlockSpec((CHUNK, wd), lambda i: (nc - 1 - i, 0))
    return pl.pallas_call(
        body, name="ssd_bwd", grid=(nc,),
        in_specs=[
            rev(D_SSD), rev(D_SSD), rev(D_SSD), rev(D_XBC),
            pl.BlockSpec((8, D_XBC), lambda i: (jnp.maximum((nc - 1 - i) * (CHUNK // 8) - 1, 0), 0)),
            rev(DT_PAD),
            pl.BlockSpec((1, SSD_HEADS * SSD_P, SSD_N), lambda i: (nc - 1 - i, 0, 0)),
            full((CONV_K, D_XBC)), full((1, D_XBC)), full((1, SSD_HEADS)), full((1, SSD_HEADS)), full((1, SSD_HEADS)),
            full((1, D_SSD)),
        ],
        out_specs=[
            rev(D_SSD), rev(D_XBC), rev(DT_PAD),
            full((CONV_K, D_XBC)), full((1, D_XBC)), full((1, SSD_HEADS)), full((1, SSD_HEADS)), full((1, SSD_HEADS)),
            full((1, D_SSD)),
        ],
        out_shape=[
            jax.ShapeDtypeStruct((L, D_SSD), BF16), jax.ShapeDtypeStruct((L, D_XBC), BF16),
            jax.ShapeDtypeStruct((L, DT_PAD), F32),
            jax.ShapeDtypeStruct((CONV_K, D_XBC), F32), jax.ShapeDtypeStruct((1, D_XBC), F32),
            jax.ShapeDtypeStruct((1, SSD_HEADS), F32), jax.ShapeDtypeStruct((1, SSD_HEADS), F32),
            jax.ShapeDtypeStruct((1, SSD_HEADS), F32), jax.ShapeDtypeStruct((1, D_SSD), F32),
        ],
        scratch_shapes=[
            pltpu.VMEM((SSD_HEADS * SSD_P, SSD_N), F32),
            pltpu.VMEM((8, D_XBC), F32),
            pltpu.VMEM((CHUNK + 8, D_XBC), F32),
            pltpu.VMEM((CHUNK + 8, D_XBC), F32),
            pltpu.VMEM((CHUNK, D_XBC), F32),
        ],
        compiler_params=_cparams(("arbitrary",)),
    )(dy, z, ypre, xbc, xbc, dtp, prev, conv_w, conv_b, dt_bias, a_log, d_skip, norm_w)


def _head_expander():
    return (_iota2((SSD_HEADS, D_SSD), 1) // SSD_P == _iota2((SSD_HEADS, D_SSD), 0)).astype(BF16)


def _hi_lo(x):
    hi = _bf(x)
    return hi, _bf(x - hi.astype(F32))


def _expand(v, e):
    hi, lo = _hi_lo(v)
    return _dot(hi, e) + _dot(lo, e)


def _headsum(t, e):
    m = t.shape[0]
    if m < 8:
        t = jnp.broadcast_to(t[0:1], (8, t.shape[1]))
    hi, lo = _hi_lo(t)
    return (_dot_nt(hi, e) + _dot_nt(lo, e))[0:m]


def _ssd_decays(dt, acs, dsk_ref, e):
    alast = acs[CHUNK - 1:CHUNK, :]
    stk = jnp.concatenate([dt, jnp.exp(acs), jnp.exp(alast - acs),
                           jnp.broadcast_to(jnp.exp(alast), (8, SSD_HEADS)),
                           jnp.broadcast_to(dsk_ref[...], (8, SSD_HEADS))], axis=0)
    ex = _expand(stk, e)
    return (ex[0:CHUNK], ex[CHUNK:2 * CHUNK], ex[2 * CHUNK:3 * CHUNK], ex[3 * CHUNK:3 * CHUNK + 1],
            ex[3 * CHUNK + 8:3 * CHUNK + 9])


def _ssd_fwd2(z, xbc, dtp, conv_w, conv_b, dt_bias, a_log, d_skip, norm_w):
    L = z.shape[0]
    nc = L // CHUNK
    half = D_SSD // SSD_GROUPS

    def body(z_ref, xbc_ref, tail_ref, dt_ref, cw_ref, cb_ref, dtb_ref, alog_ref, dsk_ref, nw_ref, smat_ref,
             y_ref, ypre_ref, prev_ref, state, ybuf, mbuf):
        c = pl.program_id(0)

        @pl.when(c == 0)
        def _():
            state[...] = jnp.zeros_like(state)

        u, sig, xbcv, dtraw, dt, A, acs, acs_row, _ = _ssd_chunk_pre2(
            c == 0, xbc_ref, tail_ref, dt_ref, cw_ref, cb_ref, dtb_ref, alog_ref, smat_ref)
        e = _head_expander()
        dtE, eacsE, dsdE, ealE, dskE = _ssd_decays(dt, acs, dsk_ref, e)
        xs = xbcv[:, 0:D_SSD]
        X = xs * dtE
        prev_ref[0] = state[...]
        causal = _iota2((CHUNK, CHUNK), 0) >= _iota2((CHUNK, CHUNK), 1)
        for g in range(SSD_GROUPS):
            gs = slice(half * g, half * (g + 1))
            Bg = _bf(xbcv[:, D_SSD + SSD_N * g:D_SSD + SSD_N * (g + 1)])
            Cg = _bf(xbcv[:, D_SSD + D_BC + SSD_N * g:D_SSD + D_BC + SSD_N * (g + 1)])
            cb = _dot_nt(Cg, Bg)
            for r in range(SSD_R):
                h = g * SSD_R + r
                seg = acs[:, h:h + 1] - acs_row[h:h + 1, :]
                mbuf[h] = _bf(cb * jnp.where(causal, jnp.exp(jnp.where(causal, seg, 0.0)), 0.0))
            st = state[:, gs]
            ybuf[:, gs] = _dot(Cg, _bf(st)) * eacsE[:, gs] + dskE[:, gs] * xs[:, gs]
            state[:, gs] = st * ealE[:, gs] + _dot_tn(Bg, _bf(X[:, gs] * dsdE[:, gs]))
        Xb = _bf(X)
        for h in range(SSD_HEADS):
            hs = slice(SSD_P * h, SSD_P * (h + 1))
            ybuf[:, hs] += _dot(mbuf[h], Xb[:, hs])
        y = ybuf[...]
        ypre_ref[...] = y
        zv = z_ref[...]
        yf = y * (zv * _sigmoid(zv))
        for g in range(SSD_GROUPS):
            gs = slice(half * g, half * (g + 1))
            yg = yf[:, gs]
            ms = jnp.mean(yg * yg, axis=-1, keepdims=True)
            y_ref[:, gs] = _bf(yg * lax.rsqrt(ms + RMS_EPS) * nw_ref[:, gs])

    full = lambda shape: pl.BlockSpec(shape, lambda c: (0, 0))
    return pl.pallas_call(
        body, name="ssd_fwd", grid=(nc,),
        in_specs=[
            pl.BlockSpec((CHUNK, D_SSD), lambda c: (c, 0)),
            pl.BlockSpec((CHUNK, D_XBC), lambda c: (c, 0)),
            pl.BlockSpec((HALO, D_XBC), lambda c: (jnp.maximum(c * (CHUNK // HALO) - 1, 0), 0)),
            pl.BlockSpec((CHUNK, DT_PAD), lambda c: (c, 0)),
            full((CONV_K, D_XBC)), full((1, D_XBC)), full((1, SSD_HEADS)), full((1, SSD_HEADS)), full((1, SSD_HEADS)),
            full((1, D_SSD)), full((3 * CHUNK, 2 * (CHUNK + HALO))),
        ],
        out_specs=[
            pl.BlockSpec((CHUNK, D_SSD), lambda c: (c, 0)),
            pl.BlockSpec((CHUNK, D_SSD), lambda c: (c, 0)),
            pl.BlockSpec((1, SSD_N, D_SSD), lambda c: (c, 0, 0)),
        ],
        out_shape=[
            jax.ShapeDtypeStruct((L, D_SSD), BF16),
            jax.ShapeDtypeStruct((L, D_SSD), F32),
            jax.ShapeDtypeStruct((nc, SSD_N, D_SSD), F32),
        ],
        scratch_shapes=[
            pltpu.VMEM((SSD_N, D_SSD), F32),
            pltpu.VMEM((CHUNK, D_SSD), F32),
            pltpu.VMEM((SSD_HEADS, CHUNK, CHUNK), BF16),
        ],
        compiler_params=_cparams(("arbitrary",)),
    )(z, xbc, xbc, dtp, conv_w, conv_b, dt_bias, a_log, d_skip, norm_w, _shift_matrix((13, 14, 15)))


def _ssd_bwd2(dy, z, ypre, xbc, dtp, prev, conv_w, conv_b, dt_bias, a_log, d_skip, norm_w):
    L = z.shape[0]
    nc = L // CHUNK
    half = D_SSD // SSD_GROUPS

    def body(dy_ref, z_ref, ypre_ref, xbc_ref, tail_ref, dt_ref, prev_ref, cw_ref, cb_ref, dtb_ref, alog_ref, dsk_ref,
             nw_ref, smat_ref, smat2_ref, dz_ref, dxbc_ref, ddt_ref, gcw_ref, gcb_ref, gdtb_ref, galog_ref, gdsk_ref,
             gnw_ref, dstate, dhead, dpost, yobuf, bdbuf, lmbuf, dmbuf, cbbuf):
        i = pl.program_id(0)
        c = nc - 1 - i

        @pl.when(i == 0)
        def _():
            dstate[...] = jnp.zeros_like(dstate)
            dhead[...] = jnp.zeros_like(dhead)
            gcw_ref[...] = jnp.zeros_like(gcw_ref)
            gcb_ref[...] = jnp.zeros_like(gcb_ref)
            gdtb_ref[...] = jnp.zeros_like(gdtb_ref)
            galog_ref[...] = jnp.zeros_like(galog_ref)
            gdsk_ref[...] = jnp.zeros_like(gdsk_ref)
            gnw_ref[...] = jnp.zeros_like(gnw_ref)

        u, sig, xbcv, dtraw, dt, A, acs, acs_row, taps = _ssd_chunk_pre2(
            c == 0, xbc_ref, tail_ref, dt_ref, cw_ref, cb_ref, dtb_ref, alog_ref, smat_ref)
        e = _head_expander()
        dtE, eacsE, dsdE, ealE, dskE = _ssd_decays(dt, acs, dsk_ref, e)
        alast = acs[CHUNK - 1:CHUNK, :]
        xs = xbcv[:, 0:D_SSD]
        X = xs * dtE
        Xb = _bf(X)

        zv = z_ref[...]
        ypre = ypre_ref[...]
        dyn = dy_ref[...]
        sz = _sigmoid(zv)
        silu_z = zv * sz
        yf = ypre * silu_z
        dyf_parts = []
        for g in range(SSD_GROUPS):
            gs = slice(half * g, half * (g + 1))
            yg = yf[:, gs]
            rstd = lax.rsqrt(jnp.mean(yg * yg, axis=-1, keepdims=True) + RMS_EPS)
            dout = dyn[:, gs]
            gnw_ref[:, gs] += jnp.sum(dout * yg * rstd, axis=0, keepdims=True)
            dyhat = dout * nw_ref[:, gs]
            dyf_parts.append(rstd * (dyhat - yg * (rstd * rstd) * jnp.mean(dyhat * yg, axis=-1, keepdims=True)))
        dyf = jnp.concatenate(dyf_parts, axis=1)
        dz_ref[...] = _bf(dyf * ypre * (sz * (1.0 + zv * (1.0 - sz))))
        dyp = dyf * silu_z
        dyb = _bf(dyp)
        G = dyp * eacsE

        causal = _iota2((CHUNK, CHUNK), 0) >= _iota2((CHUNK, CHUNK), 1)
        ST = prev_ref[0]
        dST = dstate[...]
        for g in range(SSD_GROUPS):
            gs = slice(half * g, half * (g + 1))
            bs = slice(D_SSD + SSD_N * g, D_SSD + SSD_N * (g + 1))
            cs = slice(D_SSD + D_BC + SSD_N * g, D_SSD + D_BC + SSD_N * (g + 1))
            Bg = _bf(xbcv[:, bs])
            Cg = _bf(xbcv[:, cs])
            Gb = _bf(G[:, gs])
            STb = _bf(ST[:, gs])
            dSTb = _bf(dST[:, gs])
            dstate[:, gs] = dST[:, gs] * ealE[:, gs] + _dot_tn(Cg, Gb)
            yobuf[:, gs] = _dot(Cg, STb) * eacsE[:, gs]
            bdbuf[:, gs] = _dot(Bg, dSTb)
            dpost[:, cs] = _dot_nt(Gb, STb)
            dpost[:, bs] = _dot_nt(_bf(X[:, gs] * dsdE[:, gs]), dSTb)
            cbbuf[g] = _dot_nt(Cg, Bg)
            for r in range(SSD_R):
                h = g * SSD_R + r
                seg = acs[:, h:h + 1] - acs_row[h:h + 1, :]
                lmbuf[h] = jnp.where(causal, jnp.exp(jnp.where(causal, seg, 0.0)), 0.0)
        for h in range(SSD_HEADS):
            hs = slice(SSD_P * h, SSD_P * (h + 1))
            Mb = _bf(cbbuf[h // SSD_R] * lmbuf[h])
            dmbuf[h] = _dot_nt(dyb[:, hs], Xb[:, hs])
            dpost[:, hs] = _dot_tn(Mb, dyb[:, hs])
        lane16 = _iota2((1, SSD_HEADS), 1)
        sub16 = _iota2((SSD_HEADS, 1), 0)
        dacs_col = jnp.zeros((CHUNK, SSD_HEADS), F32)
        dacs_row = jnp.zeros((SSD_HEADS, CHUNK), F32)
        for g in range(SSD_GROUPS):
            bs = slice(D_SSD + SSD_N * g, D_SSD + SSD_N * (g + 1))
            cs = slice(D_SSD + D_BC + SSD_N * g, D_SSD + D_BC + SSD_N * (g + 1))
            cb = cbbuf[g]
            dcb = jnp.zeros((CHUNK, CHUNK), F32)
            for r in range(SSD_R):
                h = g * SSD_R + r
                dM = dmbuf[h]
                Lm = lmbuf[h]
                dcb = dcb + dM * Lm
                dseg = dM * (cb * Lm)
                dacs_col = dacs_col + jnp.sum(dseg, axis=-1, keepdims=True) * (lane16 == h).astype(F32)
                dacs_row = dacs_row - jnp.sum(dseg, axis=0, keepdims=True) * (sub16 == h).astype(F32)
            dcbb = _bf(dcb)
            dpost[:, bs] += _dot_tn(dcbb, _bf(xbcv[:, cs]))
            dpost[:, cs] += _dot(dcbb, _bf(xbcv[:, bs]))

        BD = bdbuf[...]
        dX = dpost[:, 0:D_SSD] + dsdE * BD
        dsd = jnp.exp(alast - acs)
        T = _headsum(X * BD, e) * dsd
        dalast = jnp.sum(T, axis=0, keepdims=True) + _headsum(
            jnp.sum(dST * ST, axis=0, keepdims=True), e) * jnp.exp(alast)
        is_last = (_iota2((CHUNK, 1), 0) == CHUNK - 1).astype(F32)
        dacs = dacs_col + _to_cols(dacs_row) + _headsum(dyp * yobuf[...], e) - T + is_last * dalast
        triu = (_iota2((CHUNK, CHUNK), 0) <= _iota2((CHUNK, CHUNK), 1)).astype(F32)
        da = jnp.dot(triu, dacs, preferred_element_type=F32, precision=HI)
        ddt_tot = _headsum(dX * xs, e) + da * A
        galog_ref[...] += jnp.sum(da * dt, axis=0, keepdims=True) * A
        ddtraw = ddt_tot * _sigmoid(dtraw)
        gdtb_ref[...] += jnp.sum(ddtraw, axis=0, keepdims=True)
        gdsk_ref[...] += _headsum(jnp.sum(dyp * xs, axis=0, keepdims=True), e)
        ddt_ref[...] = jnp.zeros_like(ddt_ref)
        ddt_ref[:, 0:SSD_HEADS] = ddtraw
        dpost[:, 0:D_SSD] = dX * dtE + dskE * dyp

        dconv = dpost[...] * (sig * (1.0 + u * (1.0 - sig)))
        gcb_ref[...] += jnp.sum(dconv, axis=0, keepdims=True)
        for k in range(CONV_K):
            gcw_ref[k:k + 1, :] += jnp.sum(dconv * taps[k], axis=0, keepdims=True)
        later = _shifted_rows(dconv, dhead[...], smat2_ref)
        dx = cw_ref[CONV_K - 1:CONV_K, :] * dconv
        for k in range(CONV_K - 1):
            dx = dx + cw_ref[k:k + 1, :] * later[k]
        dxbc_ref[...] = _bf(dx)
        dhead[...] = dconv[0:HALO, :]

    full = lambda shape: pl.BlockSpec(shape, lambda i: (0, 0))
    rev = lambda wd: pl.BlockSpec((CHUNK, wd), lambda i: (nc - 1 - i, 0))
    return pl.pallas_call(
        body, name="ssd_bwd", grid=(nc,),
        in_specs=[
            rev(D_SSD), rev(D_SSD), rev(D_SSD), rev(D_XBC),
            pl.BlockSpec((HALO, D_XBC), lambda i: (jnp.maximum((nc - 1 - i) * (CHUNK // HALO) - 1, 0), 0)),
            rev(DT_PAD),
            pl.BlockSpec((1, SSD_N, D_SSD), lambda i: (nc - 1 - i, 0, 0)),
            full((CONV_K, D_XBC)), full((1, D_XBC)), full((1, SSD_HEADS)), full((1, SSD_HEADS)), full((1, SSD_HEADS)),
            full((1, D_SSD)), full((3 * CHUNK, 2 * (CHUNK + HALO))), full((3 * CHUNK, 2 * (CHUNK + HALO))),
        ],
        out_specs=[
            rev(D_SSD), rev(D_XBC), rev(DT_PAD),
            full((CONV_K, D_XBC)), full((1, D_XBC)), full((1, SSD_HEADS)), full((1, SSD_HEADS)), full((1, SSD_HEADS)),
            full((1, D_SSD)),
        ],
        out_shape=[
            jax.ShapeDtypeStruct((L, D_SSD), BF16), jax.ShapeDtypeStruct((L, D_XBC), BF16),
            jax.ShapeDtypeStruct((L, DT_PAD), F32),
            jax.ShapeDtypeStruct((CONV_K, D_XBC), F32), jax.ShapeDtypeStruct((1, D_XBC), F32),
            jax.ShapeDtypeStruct((1, SSD_HEADS), F32), jax.ShapeDtypeStruct((1, SSD_HEADS), F32),
            jax.ShapeDtypeStruct((1, SSD_HEADS), F32), jax.ShapeDtypeStruct((1, D_SSD), F32),
        ],
        scratch_shapes=[
            pltpu.VMEM((SSD_N, D_SSD), F32),
            pltpu.VMEM((HALO, D_XBC), F32),
            pltpu.VMEM((CHUNK, D_XBC), F32),
            pltpu.VMEM((CHUNK, D_SSD), F32),
            pltpu.VMEM((CHUNK, D_SSD), F32),
            pltpu.VMEM((SSD_HEADS, CHUNK, CHUNK), F32),
            pltpu.VMEM((SSD_HEADS, CHUNK, CHUNK), F32),
            pltpu.VMEM((SSD_GROUPS, CHUNK, CHUNK), F32),
        ],
        compiler_params=_cparams(("arbitrary",)),
    )(dy, z, ypre, xbc, xbc, dtp, prev, conv_w, conv_b, dt_bias, a_log, d_skip, norm_w, _shift_matrix((13, 14, 15)),
      _shift_matrix((3, 2, 1)))


def _rope_tables(pos_ref, inv_ref):
    ang = pos_ref[...].astype(F32) * inv_ref[...]
    d = _iota2((1, 2 * ATT_HD), 1) % ATT_HD
    s = jnp.sin(ang)
    return jnp.cos(ang), jnp.where(d < ROPE_DIM // 2, -s, 0.0), jnp.where((d >= ROPE_DIM // 2) & (d < ROPE_DIM), s, 0.0)


def _rope(t, tabs):
    c, s1, s2 = tabs
    n = t.shape[1]
    rep = n // c.shape[1]
    return (t * jnp.tile(c, (1, rep)) + pltpu.roll(t, n - ROPE_DIM // 2, 1) * jnp.tile(s1, (1, rep))
            + pltpu.roll(t, ROPE_DIM // 2, 1) * jnp.tile(s2, (1, rep)))


def _rope_t(t, tabs):
    c, s1, s2 = tabs
    n = t.shape[1]
    rep = n // c.shape[1]
    return (t * jnp.tile(c, (1, rep)) + pltpu.roll(t * jnp.tile(s1, (1, rep)), ROPE_DIM // 2, 1)
            + pltpu.roll(t * jnp.tile(s2, (1, rep)), n - ROPE_DIM // 2, 1))


def _swa_mask(first):
    qi = _iota2((WINDOW, 2 * WINDOW), 0)
    si = _iota2((WINDOW, 2 * WINDOW), 1)
    band = (si > qi) & (si <= qi + WINDOW)
    return band & (jnp.logical_not(first) | (si >= WINDOW))


def _stack_heads(t, j):
    return jnp.concatenate([t[:, ATT_HD * (j * ATT_R + r):ATT_HD * (j * ATT_R + r + 1)] for r in range(ATT_R)], axis=0)


def _stack_cols(ref, j):
    cols = [jnp.broadcast_to(ref[:, j * ATT_R + r:j * ATT_R + r + 1], (WINDOW, 1)) for r in range(ATT_R)]
    return jnp.concatenate(cols, axis=0)


def _swa_mask_t(first):
    si = _iota2((2 * WINDOW, ATT_R * WINDOW), 0)
    qi = _iota2((2 * WINDOW, ATT_R * WINDOW), 1) % WINDOW
    band = (si > qi) & (si <= qi + WINDOW)
    return band & (jnp.logical_not(first) | (si >= WINDOW))


def _head_rows(ref, j, rows=None):
    if ref.shape[0] == 1:
        parts = [jnp.broadcast_to(ref[:, j * ATT_R + r:j * ATT_R + r + 1], (1, WINDOW)) for r in range(ATT_R)]
    else:
        parts = [ref[j * ATT_R + r:j * ATT_R + r + 1, :] for r in range(ATT_R)]
    return jnp.concatenate(parts, axis=1)


def _swa_fwd(q, g, kv, sinks):
    L = q.shape[0]
    nb = L // WINDOW
    scale = ATT_HD ** -0.5

    def body(q_ref, g_ref, kvc_ref, kvp_ref, sink_ref, y_ref, o_ref, lse_ref, otbuf):
        n = pl.program_id(0)
        kk = jnp.concatenate([kvp_ref[:, 0:D_KV], kvc_ref[:, 0:D_KV]], axis=0)
        vv = jnp.concatenate([kvp_ref[:, D_KV:2 * D_KV], kvc_ref[:, D_KV:2 * D_KV]], axis=0)
        valid = _swa_mask_t(n == 0)
        qv = q_ref[...]
        for j in range(ATT_KVH):
            js = slice(ATT_HD * j, ATT_HD * (j + 1))
            st = _dot_nt(kk[:, js], _stack_heads(qv, j)) * scale
            st = jnp.where(valid, st, NEG_BIG)
            sink = _head_rows(sink_ref, j)
            m = jnp.maximum(jnp.max(st, axis=0, keepdims=True), sink)
            p = jnp.exp(st - m)
            denom = jnp.sum(p, axis=0, keepdims=True) + jnp.exp(sink - m)
            ot = _dot_tn(vv[:, js], _bf(p)) * (1.0 / denom)
            lse = m + jnp.log(denom)
            for r in range(ATT_R):
                h = j * ATT_R + r
                otbuf[ATT_HD * h:ATT_HD * (h + 1), :] = ot[:, WINDOW * r:WINDOW * (r + 1)]
                lse_ref[h:h + 1, :] = lse[:, WINDOW * r:WINDOW * (r + 1)]
        o = otbuf[...].T
        o_ref[...] = o
        gv = g_ref[...]
        y_ref[...] = _bf(o * (gv * _sigmoid(gv)))

    cur = lambda wd: pl.BlockSpec((WINDOW, wd), lambda n: (n, 0))
    prv = lambda wd: pl.BlockSpec((WINDOW, wd), lambda n: (jnp.maximum(n - 1, 0), 0))
    return pl.pallas_call(
        body, name="swa_fwd", grid=(nb,),
        in_specs=[cur(D_ATT), cur(D_ATT), cur(2 * D_KV), prv(2 * D_KV), pl.BlockSpec((1, ATT_QH), lambda n: (0, 0))],
        out_specs=[cur(D_ATT), cur(D_ATT), pl.BlockSpec((ATT_QH, WINDOW), lambda n: (0, n))],
        out_shape=[jax.ShapeDtypeStruct((L, D_ATT), BF16), jax.ShapeDtypeStruct((L, D_ATT), F32),
                   jax.ShapeDtypeStruct((ATT_QH, L), F32)],
        scratch_shapes=[pltpu.VMEM((D_ATT, WINDOW), F32)],
        compiler_params=_cparams(("parallel",)),
    )(q, g, kv, kv, sinks)


def _swa_bwd(dy, q, g, kv, o, lse, pos, inv, sinks):
    L = q.shape[0]
    nb = L // WINDOW
    scale = ATT_HD ** -0.5

    def body(dy_ref, q_ref, g_ref, kvc_ref, kvp_ref, o_ref, lse_ref, posc_ref, posp_ref, inv_ref, sink_ref,
             dq_ref, dg_ref, dkv_ref, dsink_ref, carry, dqbuf, dkbuf, dvbuf):
        n = pl.program_id(0)

        @pl.when(n == 0)
        def _():
            dsink_ref[...] = jnp.zeros_like(dsink_ref)

        @pl.when(n < nb)
        def _():
            tc = _rope_tables(posc_ref, inv_ref)
            tp = _rope_tables(posp_ref, inv_ref)
            kk = jnp.concatenate([kvp_ref[:, 0:D_KV], kvc_ref[:, 0:D_KV]], axis=0)
            vv = jnp.concatenate([kvp_ref[:, D_KV:2 * D_KV], kvc_ref[:, D_KV:2 * D_KV]], axis=0)
            valid = _swa_mask_t(n == 0)
            qv = q_ref[...]
            gv = g_ref[...]
            sg = _sigmoid(gv)
            dyv = dy_ref[...]
            ov = o_ref[...]
            dg_ref[...] = _bf(dyv * ov * (sg * (1.0 + gv * (1.0 - sg))))
            do = dyv * (gv * sg)
            dod = do * ov
            ones = jnp.ones((8, ATT_HD), BF16)
            lane16 = _iota2((1, ATT_QH), 1)
            dsink = jnp.zeros((1, ATT_QH), F32)
            for j in range(ATT_KVH):
                js = slice(ATT_HD * j, ATT_HD * (j + 1))
                kj = kk[:, js]
                vj = vv[:, js]
                qs = _stack_heads(qv, j)
                dos = _bf(_stack_heads(do, j))
                hi, lo = _hi_lo(_stack_heads(dod, j))
                delta = (_dot_nt(ones, hi) + _dot_nt(ones, lo))[0:1]
                lse = _head_rows(lse_ref, j)
                st = _dot_nt(kj, qs) * scale
                pt = jnp.exp(jnp.where(valid, st, NEG_BIG) - lse)
                dst = _bf(pt * (_dot_nt(vj, dos) - delta))
                dqt = _dot_tn(kj, dst) * scale
                dkbuf[:, js] = _dot(dst, qs) * scale
                dvbuf[:, js] = _dot(_bf(pt), dos)
                sd = jnp.exp(_head_rows(sink_ref, j) - lse) * delta
                for r in range(ATT_R):
                    h = j * ATT_R + r
                    ls = slice(WINDOW * r, WINDOW * (r + 1))
                    dqbuf[ATT_HD * h:ATT_HD * (h + 1), :] = dqt[:, ls]
                    dsink = dsink - jnp.sum(sd[:, ls], axis=1, keepdims=True) * (lane16 == h).astype(F32)
            dsink_ref[...] += dsink
            dq_ref[...] = _bf(_rope_t(dqbuf[...].T, tc))
            dkp = _rope_t(dkbuf[0:WINDOW, :], tp)
            dkc = _rope_t(dkbuf[WINDOW:2 * WINDOW, :], tc)

            @pl.when(n > 0)
            def _():
                dkv_ref[:, 0:D_KV] = _bf(carry[:, 0:D_KV] + dkp)
                dkv_ref[:, D_KV:2 * D_KV] = _bf(carry[:, D_KV:2 * D_KV] + dvbuf[0:WINDOW, :])

            carry[:, 0:D_KV] = dkc
            carry[:, D_KV:2 * D_KV] = dvbuf[WINDOW:2 * WINDOW, :]

        @pl.when(n == nb)
        def _():
            dkv_ref[...] = _bf(carry[...])

    last = nb - 1
    cur = lambda wd: pl.BlockSpec((WINDOW, wd), lambda n: (jnp.minimum(n, last), 0))
    prv = lambda wd: pl.BlockSpec((WINDOW, wd), lambda n: (jnp.maximum(jnp.minimum(n, last) - 1, 0), 0))
    return pl.pallas_call(
        body, name="swa_bwd", grid=(nb + 1,),
        in_specs=[cur(D_ATT), cur(D_ATT), cur(D_ATT), cur(2 * D_KV), prv(2 * D_KV), cur(D_ATT),
                  pl.BlockSpec((ATT_QH, WINDOW), lambda n: (0, jnp.minimum(n, last))), cur(1), prv(1),
                  pl.BlockSpec((1, 2 * ATT_HD), lambda n: (0, 0)), pl.BlockSpec((1, ATT_QH), lambda n: (0, 0))],
        out_specs=[cur(D_ATT), cur(D_ATT),
                   pl.BlockSpec((WINDOW, 2 * D_KV), lambda n: (jnp.maximum(n - 1, 0), 0)),
                   pl.BlockSpec((1, ATT_QH), lambda n: (0, 0))],
        out_shape=[jax.ShapeDtypeStruct((L, D_ATT), BF16), jax.ShapeDtypeStruct((L, D_ATT), BF16),
                   jax.ShapeDtypeStruct((L, 2 * D_KV), BF16), jax.ShapeDtypeStruct((1, ATT_QH), F32)],
        scratch_shapes=[pltpu.VMEM((WINDOW, 2 * D_KV), F32), pltpu.VMEM((D_ATT, WINDOW), F32),
                        pltpu.VMEM((2 * WINDOW, D_KV), F32), pltpu.VMEM((2 * WINDOW, D_KV), F32)],
        compiler_params=_cparams(("arbitrary",)),
    )(dy, q, g, kv, kv, o, lse, pos, pos, inv, sinks)


def _out_ln_loss(y_ssd, y_att, x, target, w_out, ln_g, ln_b):
    L = x.shape[0]
    tm = ROW_TILE
    inv_d = 1.0 / D_MODEL

    def body(ys_ref, ya_ref, x_ref, t_ref, w_ref, g_ref, b_ref, dr_ref, dys_ref, dya_ref, loss_ref, gg_ref, gb_ref):
        i = pl.program_id(0)

        @pl.when(i == 0)
        def _():
            loss_ref[...] = jnp.zeros_like(loss_ref)
            gg_ref[...] = jnp.zeros_like(gg_ref)
            gb_ref[...] = jnp.zeros_like(gb_ref)

        h = _dot(_bf(ys_ref[...]), w_ref[0:D_SSD, :]) + _dot(_bf(ya_ref[...]), w_ref[D_SSD:D_MIX, :])
        r = ALPHA * x_ref[...] + h
        mu = jnp.mean(r, axis=-1, keepdims=True)
        xc = r - mu
        rstd = lax.rsqrt(jnp.mean(xc * xc, axis=-1, keepdims=True) + LN_EPS)
        xhat = xc * rstd
        gam = g_ref[...]
        diff = xhat * gam + b_ref[...] - t_ref[...]
        part = jnp.sum(jnp.sum(diff * diff, axis=-1, keepdims=True), axis=0, keepdims=True)
        loss_ref[...] += (0.5 * inv_d) * part
        dout = diff * inv_d
        gg_ref[...] += jnp.sum(dout * xhat, axis=0, keepdims=True)
        gb_ref[...] += jnp.sum(dout, axis=0, keepdims=True)
        dxh = dout * gam
        dr = rstd * (dxh - jnp.mean(dxh, axis=-1, keepdims=True) - xhat * jnp.mean(dxh * xhat, axis=-1, keepdims=True))
        dr_ref[...] = dr
        drb = _bf(dr)
        dys_ref[...] = _dot_nt(drb, w_ref[0:D_SSD, :])
        dya_ref[...] = _dot_nt(drb, w_ref[D_SSD:D_MIX, :])

    row = pl.BlockSpec((tm, D_MODEL), lambda i: (i, 0))
    vec = pl.BlockSpec((1, D_MODEL), lambda i: (0, 0))
    return pl.pallas_call(
        body, name="out_ln_loss", grid=(L // tm,),
        in_specs=[row, row, row, row, pl.BlockSpec((D_MIX, D_MODEL), lambda i: (0, 0), pipeline_mode=pl.Buffered(1)), vec, vec],
        out_specs=[row, row, row, pl.BlockSpec((1, 128), lambda i: (0, 0)), vec, vec],
        out_shape=[jax.ShapeDtypeStruct((L, D_MODEL), F32)] * 3 + [jax.ShapeDtypeStruct((1, 128), F32)]
        + [jax.ShapeDtypeStruct((1, D_MODEL), F32)] * 2,
        compiler_params=_cparams(("arbitrary",)),
    )(y_ssd, y_att, x, target, w_out, ln_g, ln_b)


def _local_step(x, pos, target, w, get_w_out, start_w_in, token, conv_w, conv_b, dt_bias, a_log, d_skip, norm_w, sinks,
                ln_g, ln_b):
    inv8 = ROPE_THETA ** (-jnp.arange(0, ROPE_DIM, 2, dtype=F32) / ROPE_DIM)
    inv = jnp.tile(jnp.concatenate([inv8, inv8, jnp.zeros((ATT_HD - ROPE_DIM,), F32)]), 2).reshape(1, 2 * ATT_HD)
    inv = inv + token

    z, g, q, xbc, kv, dtp, xb = _in_proj(x, w, pos, inv)
    y_ssd, y_pre, prev = _ssd_fwd2(z, xbc, dtp, conv_w, conv_b, dt_bias, a_log, d_skip, norm_w)
    y_att, o, lse = _swa_fwd(q, g, kv, sinks)
    w_out = get_w_out(lse)
    dr, dy_ssd, dy_att, loss, g_ln_g, g_ln_b = _out_ln_loss(y_ssd, y_att, x, target, w_out, ln_g, ln_b)
    gw_out_ssd, gw_out_att = _matmuls_tn([y_ssd, y_att], dr, "gw_out", out_dtype=BF16)
    slabs = jnp.concatenate([gw_out_ssd, gw_out_att], axis=0).reshape(N_CHIPS, W_OUT_ROWS, D_MODEL)
    w_out_red = _reduce_w_out_start(slabs, loss)
    inv = inv + w_out_red[16][0:1, :]
    dq, dg, dkv, g_sinks = _swa_bwd(dy_att, q, g, kv, o, lse, pos, inv, sinks)
    dz, dxbc, ddt, g_conv_w, g_conv_b, g_dt_bias, g_a_log, g_d_skip, g_norm_w = _ssd_bwd2(
        dy_ssd, z, y_pre, xbc, dtp, prev, conv_w, conv_b, dt_bias, a_log, d_skip, norm_w)
    gw_z, gw_g, gw_q = _matmuls_tn([dz, dg, dq], xb, "gw_zgq")
    gw_xbc, gw_kv, gw_dt = _matmuls_tn([dxbc, dkv, ddt], xb, "gw_xbc_kv_dt")
    gw_in = jnp.concatenate([gw_z, gw_xbc, gw_dt[0:SSD_HEADS], gw_q, gw_kv, gw_g], axis=0)
    w_in_red = start_w_in(gw_in)
    grad_x = _grad_x(dr, dz, dg, dq, dxbc, dkv, ddt, w, w_in_red[10])
    small = dict(conv_w=g_conv_w, conv_b=g_conv_b, dt_bias=g_dt_bias, a_log=g_a_log, d_skip=g_d_skip,
                 ssd_norm_w=g_norm_w, attn_sinks=g_sinks, ln_g=g_ln_g, ln_b=g_ln_b)
    return loss, grad_x, w_in_red, w_out_red, small


def _mesh_pos():
    return lax.axis_index("x"), lax.axis_index("y"), lax.axis_index("c")


def _gather_weights(w_in_s, conv_w_s):
    def body(win_ref, cw_ref, owin_ref, ocw_ref, send_sems, recv_sems, small_send, small_recv, local_sems):
        x, y, c = _mesh_pos()
        me = 2 * x + y
        sibling = (x, y, 1 - c)
        chips = [(1 - x, y), (x, 1 - y), (1 - x, 1 - y)]
        locals_ = [pltpu.make_async_copy(cw_ref, ocw_ref.at[me], local_sems.at[0])]
        for cp in locals_:
            cp.start()
        started = []
        for t, (src, dst) in enumerate(((win_ref, owin_ref),)):
            hr = src.shape[0] // 2

            def half(ref, hc, hr=hr):
                return ref.at[pl.ds(hc * hr, hr), :]

            for j, (px, py) in enumerate(chips):
                cp = pltpu.make_async_remote_copy(
                    src_ref=half(src, c), dst_ref=half(dst.at[me], c), send_sem=send_sems.at[t, j],
                    recv_sem=recv_sems.at[t, j], device_id=(px, py, c), device_id_type=MESH)
                cp.start()
                started.append(cp)
        for j, (px, py) in enumerate(chips):
            cp = pltpu.make_async_remote_copy(
                src_ref=cw_ref, dst_ref=ocw_ref.at[me], send_sem=small_send.at[j], recv_sem=small_recv.at[j],
                device_id=(px, py, c), device_id_type=MESH)
            cp.start()
            started.append(cp)
        for t, (src, dst) in enumerate(((win_ref, owin_ref),)):
            hr = src.shape[0] // 2
            for j, (px, py) in enumerate(chips):
                src_chip = 2 * px + py
                blk = dst.at[src_chip].at[pl.ds(c * hr, hr), :]
                pltpu.make_async_remote_copy(
                    src_ref=blk, dst_ref=blk, send_sem=send_sems.at[t, j], recv_sem=recv_sems.at[t, j],
                    device_id=(px, py, c), device_id_type=MESH).wait_recv()
                cp = pltpu.make_async_remote_copy(
                    src_ref=blk, dst_ref=blk, send_sem=send_sems.at[t, 3 + j], recv_sem=recv_sems.at[t, 3 + j],
                    device_id=sibling, device_id_type=MESH)
                cp.start()
                started.append(cp)
        for t, (src, dst) in enumerate(((win_ref, owin_ref),)):
            hr = src.shape[0] // 2
            for j, (px, py) in enumerate(chips):
                src_chip = 2 * px + py
                blk = dst.at[src_chip].at[pl.ds((1 - c) * hr, hr), :]
                pltpu.make_async_remote_copy(
                    src_ref=blk, dst_ref=blk, send_sem=send_sems.at[t, 3 + j], recv_sem=recv_sems.at[t, 3 + j],
                    device_id=sibling, device_id_type=MESH).wait_recv()
        for j in range(3):
            pltpu.make_async_remote_copy(
                src_ref=cw_ref, dst_ref=ocw_ref.at[me], send_sem=small_send.at[j], recv_sem=small_recv.at[j],
                device_id=sibling, device_id_type=MESH).wait_recv()
        for cp in started:
            cp.wait_send()
        for cp in locals_:
            cp.wait()

    any_spec = pl.BlockSpec(memory_space=pl.ANY)
    return pl.pallas_call(
        body, name="gather_weights",
        in_specs=[any_spec] * 2, out_specs=[any_spec] * 2,
        out_shape=[jax.ShapeDtypeStruct((N_CHIPS,) + a.shape, a.dtype) for a in (w_in_s, conv_w_s)],
        scratch_shapes=[pltpu.SemaphoreType.DMA((1, 6)), pltpu.SemaphoreType.DMA((1, 6)),
                        pltpu.SemaphoreType.DMA((3,)), pltpu.SemaphoreType.DMA((3,)), pltpu.SemaphoreType.DMA((3,))],
    )(w_in_s, conv_w_s)


_HBM = pl.BlockSpec(memory_space=pltpu.HBM)
_SEM = pl.BlockSpec(memory_space=pltpu.SEMAPHORE)
_EFFECT = pltpu.SideEffectType.DATAFLOW_SIDE_EFFECTING


def _gather_w_out_start(w_out_s, after):
    def body(src_ref, land_ref, after_ref, s0, s1, s2, r0, r1, r2, src_thru, land_thru, token):
        x, y, c = _mesh_pos()
        me = 2 * x + y
        chips = [(1 - x, y), (x, 1 - y), (1 - x, 1 - y)]
        for (px, py), s, r in zip(chips, (s0, s1, s2), (r0, r1, r2)):
            pltpu.make_async_remote_copy(src_ref=src_ref, dst_ref=land_ref.at[me], send_sem=s, recv_sem=r,
                                         device_id=(px, py, c), device_id_type=MESH).start()
        token[...] = jnp.zeros_like(token)

    sem = pltpu.SemaphoreType.DMA(())
    land = lax.empty((N_CHIPS,) + w_out_s.shape, w_out_s.dtype)
    return pl.pallas_call(
        body, name="gather_w_out_start",
        out_shape=(sem,) * 6 + (pltpu.HBM(w_out_s.shape, w_out_s.dtype), pltpu.HBM(land.shape, land.dtype),
                                jax.ShapeDtypeStruct((8, 128), F32)),
        in_specs=(_HBM, _HBM, pl.BlockSpec(memory_space=pl.ANY)),
        out_specs=(_SEM,) * 6 + (_HBM, _HBM, pl.BlockSpec(memory_space=pltpu.VMEM)),
        input_output_aliases={0: 6, 1: 7},
        compiler_params=pltpu.CompilerParams(has_side_effects=_EFFECT),
    )(pltpu.with_memory_space_constraint(w_out_s, pltpu.HBM), pltpu.with_memory_space_constraint(land, pltpu.HBM), after)


def _gather_w_out_wait(sems, src_thru, land_thru, after):
    def body(src_ref, land_ref, s0, s1, s2, r0, r1, r2, after_ref, src_dead, got_ref):
        x, y, c = _mesh_pos()
        chips = [(1 - x, y), (x, 1 - y), (1 - x, 1 - y)]
        for (px, py), s, r in zip(chips, (s0, s1, s2), (r0, r1, r2)):
            cp = pltpu.make_async_remote_copy(src_ref=src_ref, dst_ref=land_ref.at[2 * px + py], send_sem=s, recv_sem=r,
                                              device_id=(px, py, c), device_id_type=MESH)
            cp.wait_send()
            cp.wait_recv()

    return pl.pallas_call(
        body, name="gather_w_out_wait",
        out_shape=(pltpu.HBM(src_thru.shape, src_thru.dtype), pltpu.HBM(land_thru.shape, land_thru.dtype)),
        in_specs=(_HBM, _HBM) + (_SEM,) * 6 + (pl.BlockSpec(memory_space=pl.ANY),),
        out_specs=(_HBM, _HBM), input_output_aliases={0: 0, 1: 1},
        compiler_params=pltpu.CompilerParams(has_side_effects=_EFFECT),
    )(src_thru, land_thru, *sems, after)[1]


def _pair_start(gw_in, after):
    hr = gw_in.shape[1] // 2

    def body(src_ref, land_ref, after_ref, *refs):
        x, y, c = _mesh_pos()
        for j in range(N_CHIPS):
            pltpu.make_async_remote_copy(
                src_ref=src_ref.at[j, pl.ds((1 - c) * hr, hr), :], dst_ref=land_ref.at[j], send_sem=refs[j],
                recv_sem=refs[N_CHIPS + j], device_id=(x, y, 1 - c), device_id_type=MESH).start()
        refs[10][...] = jnp.zeros_like(refs[10])

    sem = pltpu.SemaphoreType.DMA(())
    land = lax.empty((N_CHIPS, hr, D_MODEL), F32)
    return pl.pallas_call(
        body, name="pair_start",
        out_shape=(sem,) * 8 + (pltpu.HBM(gw_in.shape, F32), pltpu.HBM(land.shape, F32), jax.ShapeDtypeStruct((8, 128), F32)),
        in_specs=(_HBM, _HBM, pl.BlockSpec(memory_space=pl.ANY)),
        out_specs=(_SEM,) * 8 + (_HBM, _HBM, pl.BlockSpec(memory_space=pltpu.VMEM)),
        input_output_aliases={0: 8, 1: 9},
        compiler_params=pltpu.CompilerParams(has_side_effects=_EFFECT),
    )(pltpu.with_memory_space_constraint(gw_in, pltpu.HBM), pltpu.with_memory_space_constraint(land, pltpu.HBM), after)


def _pair_wait(sems, gw_thru, land_thru, after):
    hr = land_thru.shape[1]

    def body(src_ref, land_ref, *refs):
        x, y, c = _mesh_pos()
        for j in range(N_CHIPS):
            cp = pltpu.make_async_remote_copy(
                src_ref=src_ref.at[j, pl.ds((1 - c) * hr, hr), :], dst_ref=land_ref.at[j], send_sem=refs[j],
                recv_sem=refs[N_CHIPS + j], device_id=(x, y, 1 - c), device_id_type=MESH)
            cp.wait_send()
            cp.wait_recv()

    return pl.pallas_call(
        body, name="pair_wait",
        out_shape=(pltpu.HBM(gw_thru.shape, F32), pltpu.HBM(land_thru.shape, F32)),
        in_specs=(_HBM, _HBM) + (_SEM,) * 8 + (pl.BlockSpec(memory_space=pl.ANY),),
        out_specs=(_HBM, _HBM), input_output_aliases={0: 0, 1: 1},
        compiler_params=pltpu.CompilerParams(has_side_effects=_EFFECT),
    )(gw_thru, land_thru, *sems, after)


def _chip_exchange(s_in, small):
    def body(sin_ref, sm_ref, rin_ref, slots_ref, send_sems, recv_sems, small_send, small_recv, local_sem):
        x, y, c = _mesh_pos()
        me = 2 * x + y
        dev = 4 * x + 2 * y + c
        chips = [(1 - x, y), (x, 1 - y), (1 - x, 1 - y)]
        mine = pltpu.make_async_copy(sm_ref, slots_ref.at[dev], local_sem)
        mine.start()
        started = []
        for j, (px, py) in enumerate(chips):
            cp = pltpu.make_async_remote_copy(
                src_ref=sin_ref.at[2 * px + py], dst_ref=rin_ref.at[me], send_sem=send_sems.at[j],
                recv_sem=recv_sems.at[j], device_id=(px, py, c), device_id_type=MESH)
            cp.start()
            started.append(cp)
        for k in range(1, 8):
            peer = (x ^ ((k >> 2) & 1), y ^ ((k >> 1) & 1), c ^ (k & 1))
            cp = pltpu.make_async_remote_copy(
                src_ref=sm_ref, dst_ref=slots_ref.at[dev], send_sem=small_send.at[k - 1], recv_sem=small_recv.at[k - 1],
                device_id=peer, device_id_type=MESH)
            cp.start()
            started.append(cp)
        for j, (px, py) in enumerate(chips):
            blk = rin_ref.at[2 * px + py]
            pltpu.make_async_remote_copy(
                src_ref=blk, dst_ref=blk, send_sem=send_sems.at[j], recv_sem=recv_sems.at[j],
                device_id=(px, py, c), device_id_type=MESH).wait_recv()
        for k in range(1, 8):
            pltpu.make_async_remote_copy(
                src_ref=sm_ref, dst_ref=slots_ref.at[dev], send_sem=small_send.at[k - 1], recv_sem=small_recv.at[k - 1],
                device_id=(x, y, 1 - c), device_id_type=MESH).wait_recv()
        for cp in started:
            cp.wait_send()
        mine.wait()

    any_spec = pl.BlockSpec(memory_space=pl.ANY)
    return pl.pallas_call(
        body, name="chip_exchange",
        in_specs=[any_spec] * 2, out_specs=[any_spec] * 2,
        out_shape=[jax.ShapeDtypeStruct(s_in.shape, s_in.dtype), jax.ShapeDtypeStruct((8,) + small.shape, F32)],
        scratch_shapes=[pltpu.SemaphoreType.DMA((3,)), pltpu.SemaphoreType.DMA((3,)),
                        pltpu.SemaphoreType.DMA((7,)), pltpu.SemaphoreType.DMA((7,)), pltpu.SemaphoreType.DMA],
    )(s_in, small)


def _pair_share(h_in):
    def body(hin_ref, rin_ref, send_sem, recv_sem):
        x, y, c = _mesh_pos()
        cp = pltpu.make_async_remote_copy(
            src_ref=hin_ref, dst_ref=rin_ref, send_sem=send_sem, recv_sem=recv_sem,
            device_id=(x, y, 1 - c), device_id_type=MESH)
        cp.start()
        cp.wait()

    any_spec = pl.BlockSpec(memory_space=pl.ANY)
    return pl.pallas_call(
        body, name="pair_share",
        in_specs=[any_spec], out_specs=any_spec,
        out_shape=jax.ShapeDtypeStruct(h_in.shape, F32),
        scratch_shapes=[pltpu.SemaphoreType.DMA, pltpu.SemaphoreType.DMA],
    )(h_in)


def _reduce_w_out_start(slabs, after):
    def body(src_ref, land_ref, after_ref, *refs):
        x, y, c = _mesh_pos()
        me = 4 * x + 2 * y + c
        for k in range(1, 8):
            px, py, pc = x ^ ((k >> 2) & 1), y ^ ((k >> 1) & 1), c ^ (k & 1)
            pltpu.make_async_remote_copy(src_ref=src_ref.at[2 * px + py], dst_ref=land_ref.at[me], send_sem=refs[k - 1],
                                         recv_sem=refs[6 + k], device_id=(px, py, pc), device_id_type=MESH).start()
        refs[16][...] = jnp.zeros_like(refs[16])

    sem = pltpu.SemaphoreType.DMA(())
    land = lax.empty((8,) + slabs.shape[1:], slabs.dtype)
    return pl.pallas_call(
        body, name="reduce_w_out_start",
        out_shape=(sem,) * 14 + (pltpu.HBM(slabs.shape, slabs.dtype), pltpu.HBM(land.shape, land.dtype),
                                 jax.ShapeDtypeStruct((8, 128), F32)),
        in_specs=(_HBM, _HBM, pl.BlockSpec(memory_space=pl.ANY)),
        out_specs=(_SEM,) * 14 + (_HBM, _HBM, pl.BlockSpec(memory_space=pltpu.VMEM)),
        input_output_aliases={0: 14, 1: 15},
        compiler_params=pltpu.CompilerParams(has_side_effects=_EFFECT),
    )(pltpu.with_memory_space_constraint(slabs, pltpu.HBM), pltpu.with_memory_space_constraint(land, pltpu.HBM), after)


def _reduce_w_out_wait(sems, slabs_thru, land_thru, after):
    def body(src_ref, land_ref, *refs):
        x, y, c = _mesh_pos()
        for k in range(1, 8):
            px, py, pc = x ^ ((k >> 2) & 1), y ^ ((k >> 1) & 1), c ^ (k & 1)
            cp = pltpu.make_async_remote_copy(
                src_ref=src_ref.at[2 * px + py], dst_ref=land_ref.at[4 * px + 2 * py + pc], send_sem=refs[k - 1],
                recv_sem=refs[6 + k], device_id=(px, py, pc), device_id_type=MESH)
            cp.wait_send()
            cp.wait_recv()

    return pl.pallas_call(
        body, name="reduce_w_out_wait",
        out_shape=(pltpu.HBM(slabs_thru.shape, slabs_thru.dtype), pltpu.HBM(land_thru.shape, land_thru.dtype)),
        in_specs=(_HBM, _HBM) + (_SEM,) * 14 + (pl.BlockSpec(memory_space=pl.ANY),),
        out_specs=(_HBM, _HBM), input_output_aliases={0: 0, 1: 1},
        compiler_params=pltpu.CompilerParams(has_side_effects=_EFFECT),
    )(slabs_thru, land_thru, *sems, after)


def _pair_add(g, recv, core, name):
    _, rows, C = recv.shape
    tc = 256

    def body(core_ref, g_ref, r_ref, o_ref):
        o_ref[...] = _bf(g_ref[...] + r_ref[...])

    spec = pl.BlockSpec((1, rows, tc), lambda j, i, core: (j, 0, i))
    return pl.pallas_call(
        body, name=name,
        grid_spec=pltpu.PrefetchScalarGridSpec(
            num_scalar_prefetch=1, grid=(N_CHIPS, C // tc),
            in_specs=[pl.BlockSpec((1, rows, tc), lambda j, i, core: (j, core[0], i)), spec], out_specs=spec),
        out_shape=jax.ShapeDtypeStruct((N_CHIPS, rows, C), BF16),
        compiler_params=_cparams(("parallel", "parallel")),
    )(core, g, recv)


def _chip_add(own, parts, chip, name):
    _, rows, C = parts.shape
    tc = 256

    def body(chip_ref, own_ref, r0, r1, r2, r3, o_ref):
        acc = None
        for j, r in enumerate((r0, r1, r2, r3)):
            term = jnp.where(chip_ref[0] == j, own_ref[0], r[0]).astype(F32)
            acc = term if acc is None else acc + term
        o_ref[...] = acc

    def slab(j):
        return pl.BlockSpec((1, rows, tc), lambda i, chip: (jnp.where(chip[0] == j, (j + 1) % N_CHIPS, j), 0, i))

    return pl.pallas_call(
        body, name=name,
        grid_spec=pltpu.PrefetchScalarGridSpec(
            num_scalar_prefetch=1, grid=(C // tc,),
            in_specs=[pl.BlockSpec((1, rows, tc), lambda i, chip: (chip[0], 0, i))] + [slab(j) for j in range(N_CHIPS)],
            out_specs=pl.BlockSpec((rows, tc), lambda i, chip: (0, i))),
        out_shape=jax.ShapeDtypeStruct((rows, C), F32),
        compiler_params=_cparams(("parallel",)),
    )(chip, own, parts, parts, parts, parts)


def _adamw_math(w, g, m, v):
    m = ADAM_B1 * m + (1.0 - ADAM_B1) * g
    v = ADAM_B2 * v + (1.0 - ADAM_B2) * (g * g)
    m_hat = m / (1.0 - ADAM_B1 ** ADAM_STEP)
    v_hat = v / (1.0 - ADAM_B2 ** ADAM_STEP)
    delta = -ADAM_LR * (m_hat / (jnp.sqrt(v_hat) + ADAM_EPS) + ADAM_WD * w)
    return delta, m, v


def _adamw_pair(w, g_own, g_sib, m, v, core, name):
    unit = w.ndim == 3
    R, C = w.shape[0], w.shape[-1]
    rows = g_own.shape[0]
    tc = 128

    def body(core_ref, w_ref, go_ref, gs_ref, m_ref, v_ref, d_ref, nm_ref, nv_ref, g_ref):
        first = core_ref[0] == 0
        own, sib = go_ref[...], gs_ref[...]
        g = jnp.concatenate([jnp.where(first, own, sib), jnp.where(first, sib, own)], axis=0)[0:R, :]
        idx = (slice(None), 0, slice(None)) if unit else (slice(None), slice(None))
        d, nm, nv = _adamw_math(w_ref[idx], g, m_ref[idx], v_ref[idx])
        d_ref[idx] = d
        nm_ref[idx] = nm
        nv_ref[idx] = nv
        g_ref[idx] = g

    if unit:
        spec = pl.BlockSpec((R, 1, tc), lambda i, core: (0, 0, i))
    else:
        spec = pl.BlockSpec((R, tc), lambda i, core: (0, i))
    gspec = pl.BlockSpec((rows, tc), lambda i, core: (0, i))
    return pl.pallas_call(
        body, name=name,
        grid_spec=pltpu.PrefetchScalarGridSpec(
            num_scalar_prefetch=1, grid=(C // tc,),
            in_specs=[spec, gspec, gspec, spec, spec], out_specs=[spec] * 4),
        out_shape=[jax.ShapeDtypeStruct(w.shape, F32)] * 4,
        compiler_params=_cparams(("parallel",)),
    )(core, w, g_own, g_sib, m, v)


def _adamw_sum8(w, slabs, land, m, v, ids, name):
    R, C = w.shape
    tc = 128

    def body(ids_ref, w_ref, own_ref, *refs):
        lrefs, (m_ref, v_ref, d_ref, nm_ref, nv_ref, g_ref) = refs[:8], refs[8:]
        g = None
        for d, l_ref in enumerate(lrefs):
            term = jnp.where(ids_ref[0] == d, own_ref[0], l_ref[0]).astype(F32)
            g = term if g is None else g + term
        dl, nm, nv = _adamw_math(w_ref[...], g, m_ref[...], v_ref[...])
        d_ref[...] = dl
        nm_ref[...] = nm
        nv_ref[...] = nv
        g_ref[...] = g

    def slot(d):
        return pl.BlockSpec((1, R, tc), lambda i, ids: (jnp.where(ids[0] == d, (d + 1) % 8, d), 0, i))

    spec = pl.BlockSpec((R, tc), lambda i, ids: (0, i))
    return pl.pallas_call(
        body, name=name,
        grid_spec=pltpu.PrefetchScalarGridSpec(
            num_scalar_prefetch=1, grid=(C // tc,),
            in_specs=[spec, pl.BlockSpec((1, R, tc), lambda i, ids: (ids[1], 0, i))] + [slot(d) for d in range(8)]
            + [spec, spec],
            out_specs=[spec] * 4),
        out_shape=[jax.ShapeDtypeStruct((R, C), F32)] * 4,
        compiler_params=_cparams(("parallel",)),
    )(ids, w, slabs, *([land] * 8), m, v)


SMALL_NAMES = ("conv_b", "ssd_norm_w", "ln_g", "ln_b", "dt_bias", "a_log", "d_skip", "attn_sinks")
SMALL_FIELDS = ((4, 0, D_XBC), (5, 0, D_SSD), (6, 0, D_MODEL), (7, 0, D_MODEL), (5, 1024, SSD_HEADS), (5, 1152, SSD_HEADS),
                (5, 1280, SSD_HEADS), (5, 1408, ATT_QH))
LOSS_FIELD = (6, 1024, 128)
K_SMALL = D_XBC


def _pack_small(g_conv_w, vecs, loss):
    def body(cw_ref, *refs):
        o_ref = refs[-1]
        o_ref[...] = jnp.zeros_like(o_ref)
        o_ref[0:CONV_K, 0:D_XBC] = cw_ref[...]
        for v_ref, (row, off, n) in zip(refs[:-2], SMALL_FIELDS):
            o_ref[row:row + 1, off:off + n] = v_ref[...]
        o_ref[LOSS_FIELD[0]:LOSS_FIELD[0] + 1, LOSS_FIELD[1]:LOSS_FIELD[1] + LOSS_FIELD[2]] = refs[-2][...]

    return pl.pallas_call(
        body, name="pack_small", out_shape=jax.ShapeDtypeStruct((8, K_SMALL), F32), compiler_params=_cparams(),
    )(g_conv_w, *vecs, loss)


def _adamw_small(slots, chip, conv_w, m_conv_w, v_conv_w, params, moms, vars_):
    n_vec = len(SMALL_NAMES)

    def body(chip_ref, s_ref, *refs):
        ins = refs[:3 * (n_vec + 1)]
        outs = refs[3 * (n_vec + 1):-1]
        tot_ref = refs[-1]
        tot = s_ref[0]
        for d in range(1, 8):
            tot = tot + s_ref[d]
        outs[0][...] = tot[LOSS_FIELD[0]:LOSS_FIELD[0] + 1, LOSS_FIELD[1]:LOSS_FIELD[1] + 1]
        off = pl.multiple_of(chip_ref[0] * CONV_COLS, 128)
        tot_ref[...] = tot
        grads = [tot_ref[0:CONV_K, pl.ds(off, CONV_COLS)]]
        grads += [tot[row:row + 1, o:o + n] for row, o, n in SMALL_FIELDS]
        for k, g in enumerate(grads):
            w_ref, m_ref, v_ref = ins[3 * k:3 * k + 3]
            full = (0,) if k == 0 else (Ellipsis,)
            d, nm, nv = _adamw_math(w_ref[full], g, m_ref[full], v_ref[full])
            for o_ref, val in zip(outs[1 + 4 * k:5 + 4 * k], (g, d, nm, nv)):
                o_ref[full] = val

    args = [conv_w, m_conv_w, v_conv_w]
    for w, m, v in zip(params, moms, vars_):
        args += [w, m, v]
    shapes = [jax.ShapeDtypeStruct((1, 1), F32)] + [jax.ShapeDtypeStruct(conv_w.shape, F32)] * 4
    for w in params:
        shapes += [jax.ShapeDtypeStruct(w.shape, F32)] * 4
    vmem = pl.BlockSpec(memory_space=pltpu.VMEM)
    return pl.pallas_call(
        body, name="adamw_small",
        grid_spec=pltpu.PrefetchScalarGridSpec(
            num_scalar_prefetch=1, grid=(1,),
            in_specs=[pl.BlockSpec(slots.shape, lambda i, chip: (0, 0, 0))] + [vmem] * len(args),
            out_specs=[vmem] * len(shapes), scratch_shapes=[pltpu.VMEM((8, K_SMALL), F32)]),
        out_shape=shapes, compiler_params=_cparams(),
    )(chip, slots, *args)


def kernel(x, positions, w_in, conv_w, conv_b, dt_bias, a_log, d_skip, ssd_norm_w, attn_sinks, w_out, ln_g, ln_b, loss_target, m_w_in, m_conv_w, m_conv_b, m_dt_bias, m_a_log, m_d_skip, m_ssd_norm_w, m_attn_sinks, m_w_out, m_ln_g, m_ln_b, v_w_in, v_conv_w, v_conv_b, v_dt_bias, v_a_log, v_d_skip, v_ssd_norm_w, v_attn_sinks, v_w_out, v_ln_g, v_ln_b):
    mx, my, mc = _mesh_pos()
    chip = 2 * mx + my
    L = x.shape[1]

    conv_w_s8 = jnp.pad(conv_w[0], ((0, 8 - CONV_K), (0, 0)))
    pad_rows = ((0, SLAB_ROWS - W_IN_COLS), (0, 0))
    w_in_t = w_in[0].T
    w_in_b, w_out_b = jnp.pad(_bf(w_in_t), pad_rows), _bf(w_out[0])
    ag_in, ag_cw = _gather_weights(w_in_b, conv_w_s8)
    started = _gather_w_out_start(w_out_b, ag_cw)
    own = (jnp.arange(N_CHIPS) == chip)[:, None, None]

    def get_w_out(after):
        landed = _gather_w_out_wait(started[0:6], started[6], started[7], after)
        return jnp.where(own, w_out_b[None], landed).reshape(D_MIX, D_MODEL)

    ag_in = jnp.where(own, w_in_b[None], ag_in)
    w_full = jnp.concatenate([ag_in[j, 0:W_IN_COLS] for j in range(N_CHIPS)], axis=0)
    w = jnp.concatenate([
        w_full[O_Z:O_Z + D_SSD], w_full[O_G:O_G + D_ATT], w_full[O_Q:O_Q + D_ATT],
        w_full[O_XBC:O_XBC + D_XBC], w_full[O_K:O_K + 2 * D_KV], w_full[O_DT:O_DT + SSD_HEADS],
        jnp.zeros((DT_PAD - SSD_HEADS, D_MODEL), BF16)], axis=0)
    conv_w_full = jnp.concatenate([ag_cw[j, 0:CONV_K] for j in range(N_CHIPS)], axis=1)

    def start_w_in(gw_in):
        slabs = jnp.stack([jnp.pad(gw_in[W_IN_COLS * j:W_IN_COLS * (j + 1)], pad_rows) for j in range(N_CHIPS)])
        return _pair_start(slabs, gw_in[0:8, 0:128])

    loss_part, grad_x, w_in_red, w_out_red, small = _local_step(
        x[0], positions[0].reshape(L, 1), loss_target[0], w, get_w_out, start_w_in, started[8][0:1, :], conv_w_full,
        conv_b, dt_bias, a_log, d_skip, ssd_norm_w, attn_sinks, ln_g, ln_b)

    packed = _pack_small(small["conv_w"], [small[n] for n in SMALL_NAMES], loss_part)
    core_id = mc.reshape(1).astype(jnp.int32)
    chip_id = chip.reshape(1).astype(jnp.int32)
    gw_in_slabs, recv_in = _pair_wait(w_in_red[0:8], w_in_red[8], w_in_red[9], grad_x)
    s_in = _pair_add(gw_in_slabs, recv_in, core_id, "pair_add_in")
    r_in, slots = _chip_exchange(s_in, packed)
    h_in = _chip_add(s_in, r_in, chip_id, "chip_add_in")
    sib_in = _pair_share(h_in)

    to_rows = lambda a: jnp.transpose(a, (2, 0, 1))
    in_t = _adamw_pair(to_rows(w_in), h_in, sib_in, to_rows(m_w_in), to_rows(v_w_in), core_id, "adamw_w_in")
    d_w_in, nm_w_in, nv_w_in, g_w_in = [jnp.transpose(a, (1, 2, 0)) for a in in_t]
    own_slabs, landed = _reduce_w_out_wait(w_out_red[0:14], w_out_red[14], w_out_red[15], sib_in)
    ids = jnp.stack([4 * mx + 2 * my + mc, chip]).astype(jnp.int32)
    out_t = _adamw_sum8(w_out[0], own_slabs, landed, m_w_out[0], v_w_out[0], ids, "adamw_w_out")
    d_w_out, nm_w_out, nv_w_out, g_w_out = [a[None] for a in out_t]

    params = dict(conv_b=conv_b, ssd_norm_w=ssd_norm_w, ln_g=ln_g, ln_b=ln_b, dt_bias=dt_bias, a_log=a_log,
                  d_skip=d_skip, attn_sinks=attn_sinks)
    moms = dict(conv_b=m_conv_b, ssd_norm_w=m_ssd_norm_w, ln_g=m_ln_g, ln_b=m_ln_b, dt_bias=m_dt_bias, a_log=m_a_log,
                d_skip=m_d_skip, attn_sinks=m_attn_sinks)
    vars_ = dict(conv_b=v_conv_b, ssd_norm_w=v_ssd_norm_w, ln_g=v_ln_g, ln_b=v_ln_b, dt_bias=v_dt_bias, a_log=v_a_log,
                 d_skip=v_d_skip, attn_sinks=v_attn_sinks)
    res = _adamw_small(slots, chip_id, conv_w, m_conv_w, v_conv_w, [params[n] for n in SMALL_NAMES],
                       [moms[n] for n in SMALL_NAMES], [vars_[n] for n in SMALL_NAMES])
    loss = res[0][0, 0]
    grads, delta, new_m, new_v = {}, {}, {}, {}
    for k, n in enumerate(("conv_w",) + SMALL_NAMES):
        grads[n], delta[n], new_m[n], new_v[n] = res[1 + 4 * k:5 + 4 * k]
    for dd, a_in, a_out in ((grads, g_w_in, g_w_out), (delta, d_w_in, d_w_out), (new_m, nm_w_in, nm_w_out),
                            (new_v, nv_w_in, nv_w_out)):
        dd["w_in"] = a_in
        dd["w_out"] = a_out
    order = ("w_in", "conv_w", "conv_b", "dt_bias", "a_log", "d_skip", "ssd_norm_w", "attn_sinks", "w_out", "ln_g", "ln_b")
    return (loss, grad_x[None], *[grads[n] for n in order], *[delta[n] for n in order], *[new_m[n] for n in order],
            *[new_v[n] for n in order])
```

```python
import functools

import numpy as np
import jax
import jax.numpy as jnp
from jax import lax
from jax.experimental import pallas as pl
from jax.experimental.pallas import tpu as pltpu

F32 = jnp.float32
BF16 = jnp.bfloat16
MESH = pl.DeviceIdType.MESH

D_MODEL = 1024
D_SSD = 1024
D_ATT = 1024
D_MIX = 2048
SSD_HEADS = 16
SSD_P = 64
SSD_GROUPS = 2
SSD_R = 8
SSD_N = 128
D_BC = 256
D_XBC = 1536
CONV_K = 4
CHUNK = 128
ATT_HD = 64
ATT_QH = 16
ATT_KVH = 4
ATT_R = 4
D_KV = 256
WINDOW = 128
ROPE_THETA = 500000.0
ROPE_DIM = 16
ALPHA = 2.0 ** 0.25
LN_EPS = 1e-5
RMS_EPS = 1e-5
D_IN_PROJ = 5136
O_Z, O_XBC, O_DT, O_Q, O_K, O_V, O_G = 0, 1024, 2560, 2576, 3600, 3856, 4112
P_Z, P_G, P_Q, P_XBC, P_KV, P_DT, P_END = 0, 1024, 2048, 3072, 4608, 5120, 5248
DT_PAD = 128
N_CHIPS = 4
W_IN_COLS = D_IN_PROJ // N_CHIPS
SLAB_ROWS = 1312
W_OUT_ROWS = D_MIX // N_CHIPS
CONV_COLS = D_XBC // N_CHIPS

ADAM_LR = 0.001
ADAM_B1 = 0.9
ADAM_B2 = 0.999
ADAM_EPS = 1e-08
ADAM_WD = 0.01
ADAM_STEP = 10

VMEM_LIMIT = 56 * 1024 * 1024
ROW_TILE = 512
NEG_BIG = -1e30
HI = lax.Precision.HIGHEST


def _cparams(sem=None, **kw):
    if sem is not None:
        kw["dimension_semantics"] = sem
    return pltpu.CompilerParams(vmem_limit_bytes=VMEM_LIMIT, **kw)


def _dot(a, b):
    return jnp.dot(a, b, preferred_element_type=F32)


def _dot_nt(a, b):
    return lax.dot_general(a, b, (((1,), (1,)), ((), ())), preferred_element_type=F32)


def _dot_tn(a, b):
    return lax.dot_general(a, b, (((0,), (0,)), ((), ())), preferred_element_type=F32)


def _bf(a):
    return a.astype(BF16)


def _iota2(shape, dim):
    return lax.broadcasted_iota(jnp.int32, shape, dim)


def _to_rows(col):
    k = col.shape[1]
    eye = (_iota2((k, k), 0) == _iota2((k, k), 1)).astype(F32)
    return lax.dot_general(eye, col, (((1,), (1,)), ((), ())), preferred_element_type=F32, precision=HI)


def _to_cols(row):
    n = row.shape[1]
    eye = (_iota2((n, n), 0) == _iota2((n, n), 1)).astype(F32)
    return lax.dot_general(eye, row, (((1,), (1,)), ((), ())), preferred_element_type=F32, precision=HI)


def _sigmoid(x):
    return jax.nn.sigmoid(x)


def _in_proj(x, w, pos, inv):
    L = x.shape[0]
    tm = ROW_TILE
    widths = (D_SSD, D_ATT, D_ATT, D_XBC, 2 * D_KV, DT_PAD)

    def body(x_ref, w_ref, pos_ref, inv_ref, z_ref, g_ref, q_ref, xbc_ref, kv_ref, dt_ref, xb_ref):
        xb = _bf(x_ref[...])
        xb_ref[...] = xb
        for o_ref, off, wd in zip((z_ref, g_ref, xbc_ref, dt_ref), (P_Z, P_G, P_XBC, P_DT), (D_SSD, D_ATT, D_XBC, DT_PAD)):
            o_ref[...] = _dot_nt(xb, w_ref[off:off + wd, :])
        tabs = _rope_tables(pos_ref, inv_ref)
        q_ref[...] = _bf(_rope(_dot_nt(xb, w_ref[P_Q:P_Q + D_ATT, :]), tabs))
        kv_ref[:, 0:D_KV] = _bf(_rope(_dot_nt(xb, w_ref[P_KV:P_KV + D_KV, :]), tabs))
        kv_ref[:, D_KV:2 * D_KV] = _bf(_dot_nt(xb, w_ref[P_KV + D_KV:P_KV + 2 * D_KV, :]))

    row = lambda wd: pl.BlockSpec((tm, wd), lambda i: (i, 0))
    return pl.pallas_call(
        body, name="in_proj", grid=(L // tm,),
        in_specs=[row(D_MODEL), pl.BlockSpec((P_END, D_MODEL), lambda i: (0, 0), pipeline_mode=pl.Buffered(1)), row(1),
                  pl.BlockSpec((1, 2 * ATT_HD), lambda i: (0, 0))],
        out_specs=[row(wd) for wd in widths] + [row(D_MODEL)],
        out_shape=[jax.ShapeDtypeStruct((L, wd), dt) for wd, dt in zip(widths, (F32, F32, BF16, F32, BF16, F32))]
        + [jax.ShapeDtypeStruct((L, D_MODEL), BF16)],
        compiler_params=_cparams(("parallel",)),
    )(x, w, pos, inv)


def _matmuls_tn(a_list, b, name, out_dtype=F32):
    K, N = b.shape
    tk = min(K, 1024)
    nk = K // tk
    n = len(a_list)
    in_place = out_dtype == F32

    def body(*refs):
        b_ref = refs[n]
        o_refs = refs[n + 1:2 * n + 1]
        acc_refs = o_refs if in_place else refs[2 * n + 1:]
        k = pl.program_id(0)
        bb = _bf(b_ref[...])
        for a_ref, o_ref, acc_ref in zip(refs[:n], o_refs, acc_refs):
            part = _dot_tn(_bf(a_ref[...]), bb)

            @pl.when(k == 0)
            def _():
                acc_ref[...] = part

            @pl.when(k > 0)
            def _():
                acc_ref[...] += part

            if not in_place:
                @pl.when(k == nk - 1)
                def _():
                    o_ref[...] = acc_ref[...].astype(out_dtype)

    return pl.pallas_call(
        body, name=name, grid=(nk,),
        in_specs=[pl.BlockSpec((tk, a.shape[1]), lambda k: (k, 0)) for a in a_list] + [pl.BlockSpec((tk, N), lambda k: (k, 0))],
        out_specs=[pl.BlockSpec((a.shape[1], N), lambda k: (0, 0)) for a in a_list],
        out_shape=[jax.ShapeDtypeStruct((a.shape[1], N), out_dtype) for a in a_list],
        scratch_shapes=[] if in_place else [pltpu.VMEM((a.shape[1], N), F32) for a in a_list],
        compiler_params=_cparams(("arbitrary",)),
    )(*a_list, b)


def _grad_x(dr, dz, dg, dq, dxbc, dkv, ddt, w, after, part, prev=None):
    L = dr.shape[0]
    tm = min(ROW_TILE, L // 2)
    n = L // (2 * tm)
    widths = (D_SSD, D_ATT, D_ATT, D_XBC, 2 * D_KV, DT_PAD)
    offs = (P_Z, P_G, P_Q, P_XBC, P_KV, P_DT)

    def body(dr_ref, dz_ref, dg_ref, dq_ref, dxbc_ref, dkv_ref, ddt_ref, w_ref, after_ref, *rest):
        o_ref = rest[-1]
        acc = ALPHA * dr_ref[...]
        for p_ref, off, wd in zip((dz_ref, dg_ref, dq_ref, dxbc_ref, dkv_ref, ddt_ref), offs, widths):
            acc = acc + _dot(_bf(p_ref[...]), w_ref[off:off + wd, :])
        o_ref[...] = acc

    row = lambda wd: pl.BlockSpec((tm, wd), lambda i: (i + part * n, 0))
    ins = [dr, dz, dg, dq, dxbc, dkv, ddt, w, after]
    specs = ([row(D_MODEL)] + [row(wd) for wd in widths]
             + [pl.BlockSpec((P_END, D_MODEL), lambda i: (0, 0), pipeline_mode=pl.Buffered(1)),
                pl.BlockSpec((8, 128), lambda i: (0, 0))])
    if prev is not None:
        ins.append(prev)
        specs.append(pl.BlockSpec(memory_space=pl.ANY))
    return pl.pallas_call(
        body, name="grad_x_%d" % part, grid=(n,),
        in_specs=specs, out_specs=row(D_MODEL),
        out_shape=jax.ShapeDtypeStruct((L, D_MODEL), F32),
        input_output_aliases={} if prev is None else {len(ins) - 1: 0},
        compiler_params=_cparams(("parallel",)),
    )(*ins)


def _ssd_chunk_pre(first, xbc_ref, tail_ref, dt_ref, cw_ref, cb_ref, dtb_ref, alog_ref, ext):
    tail = jnp.where(first, 0.0, tail_ref[...])
    ext[0:8, :] = tail
    ext[8:8 + CHUNK, :] = xbc_ref[...]
    u = cb_ref[...] + cw_ref[0:1, :] * ext[pl.ds(5, CHUNK), :]
    for k in range(1, CONV_K):
        u = u + cw_ref[k:k + 1, :] * ext[pl.ds(5 + k, CHUNK), :]
    sig = _sigmoid(u)
    xbc = u * sig
    dtraw = dt_ref[:, 0:SSD_HEADS] + dtb_ref[...]
    dt = jax.nn.softplus(dtraw)
    A = -jnp.exp(alog_ref[...])
    a = dt * A
    tril = (_iota2((CHUNK, CHUNK), 0) >= _iota2((CHUNK, CHUNK), 1)).astype(F32)
    acs = jnp.dot(tril, a, preferred_element_type=F32, precision=HI)
    acs_row = _to_rows(acs)
    return u, sig, xbc, dtraw, dt, A, acs, acs_row


HALO = 16


def _shift_matrix(offsets):
    n = CHUNK + HALO
    m = np.zeros((len(offsets) * CHUNK, 2 * n), np.float32)
    for k, off in enumerate(offsets):
        t = np.arange(CHUNK)
        m[k * CHUNK + t, t + off] = 1.0
        m[k * CHUNK + t, n + t + off] = 1.0
    return jnp.asarray(m, BF16)


def _shifted_rows(first_part, second_part, smat_ref):
    h1, l1 = _hi_lo(first_part)
    h2, l2 = _hi_lo(second_part)
    sh = _dot(smat_ref[...], jnp.concatenate([h1, h2, l1, l2], axis=0))
    return sh[0:CHUNK], sh[CHUNK:2 * CHUNK], sh[2 * CHUNK:3 * CHUNK]


def _ssd_chunk_pre2(first, xbc_ref, tail_ref, dt_ref, cw_ref, cb_ref, dtb_ref, alog_ref, smat_ref):
    tail = jnp.where(first, 0.0, tail_ref[...])
    x = xbc_ref[...]
    taps = _shifted_rows(tail, x, smat_ref) + (x,)
    u = cb_ref[...] + cw_ref[0:1, :] * taps[0]
    for k in range(1, CONV_K):
        u = u + cw_ref[k:k + 1, :] * taps[k]
    sig = _sigmoid(u)
    xbc = u * sig
    dtraw = dt_ref[:, 0:SSD_HEADS] + dtb_ref[...]
    dt = jax.nn.softplus(dtraw)
    A = -jnp.exp(alog_ref[...])
    a = dt * A
    tril = (_iota2((CHUNK, CHUNK), 0) >= _iota2((CHUNK, CHUNK), 1)).astype(F32)
    acs = jnp.dot(tril, a, preferred_element_type=F32, precision=HI)
    acs_row = _to_rows(acs)
    return u, sig, xbc, dtraw, dt, A, acs, acs_row, taps


def _ssd_fwd(z, xbc, dtp, conv_w, conv_b, dt_bias, a_log, d_skip, norm_w):
    L = z.shape[0]
    nc = L // CHUNK

    def body(z_ref, xbc_ref, tail_ref, dt_ref, cw_ref, cb_ref, dtb_ref, alog_ref, dsk_ref, nw_ref,
             y_ref, ypre_ref, prev_ref, state, ext, ybuf):
        c = pl.program_id(0)

        @pl.when(c == 0)
        def _():
            state[...] = jnp.zeros_like(state)

        u, sig, xbcv, dtraw, dt, A, acs, acs_row = _ssd_chunk_pre(
            c == 0, xbc_ref, tail_ref, dt_ref, cw_ref, cb_ref, dtb_ref, alog_ref, ext)
        prev_ref[0] = state[...]
        causal = _iota2((CHUNK, CHUNK), 0) >= _iota2((CHUNK, CHUNK), 1)
        alast = acs[CHUNK - 1:CHUNK, :]
        for g in range(SSD_GROUPS):
            Bg = _bf(xbcv[:, D_SSD + SSD_N * g:D_SSD + SSD_N * (g + 1)])
            Cg = _bf(xbcv[:, D_SSD + D_BC + SSD_N * g:D_SSD + D_BC + SSD_N * (g + 1)])
            cb = _dot_nt(Cg, Bg)
            for r in range(SSD_R):
                h = g * SSD_R + r
                hs = slice(SSD_P * h, SSD_P * (h + 1))
                acs_c = acs[:, h:h + 1]
                seg = acs_c - acs_row[h:h + 1, :]
                Lm = jnp.where(causal, jnp.exp(jnp.where(causal, seg, 0.0)), 0.0)
                M = cb * Lm
                xh = xbcv[:, hs]
                X = xh * dt[:, h:h + 1]
                prev_h = state[hs, :]
                ydiag = _dot(_bf(M), _bf(X))
                yoff = _dot_nt(Cg, _bf(prev_h)) * jnp.exp(acs_c)
                al = alast[:, h:h + 1]
                Xd = X * jnp.exp(al - acs_c)
                state[hs, :] = prev_h * jnp.exp(al) + _dot_tn(_bf(Xd), Bg)
                ybuf[:, hs] = ydiag + yoff + dsk_ref[:, h:h + 1] * xh
        y = ybuf[...]
        ypre_ref[...] = y
        zv = z_ref[...]
        yf = y * (zv * _sigmoid(zv))
        half = D_SSD // SSD_GROUPS
        for g in range(SSD_GROUPS):
            gs = slice(half * g, half * (g + 1))
            yg = yf[:, gs]
            ms = jnp.mean(yg * yg, axis=-1, keepdims=True)
            y_ref[:, gs] = _bf(yg * lax.rsqrt(ms + RMS_EPS) * nw_ref[:, gs])

    full = lambda shape: pl.BlockSpec(shape, lambda c: (0, 0))
    return pl.pallas_call(
        body, name="ssd_fwd", grid=(nc,),
        in_specs=[
            pl.BlockSpec((CHUNK, D_SSD), lambda c: (c, 0)),
            pl.BlockSpec((CHUNK, D_XBC), lambda c: (c, 0)),
            pl.BlockSpec((8, D_XBC), lambda c: (jnp.maximum(c * (CHUNK // 8) - 1, 0), 0)),
            pl.BlockSpec((CHUNK, DT_PAD), lambda c: (c, 0)),
            full((CONV_K, D_XBC)), full((1, D_XBC)), full((1, SSD_HEADS)), full((1, SSD_HEADS)), full((1, SSD_HEADS)),
            full((1, D_SSD)),
        ],
        out_specs=[
            pl.BlockSpec((CHUNK, D_SSD), lambda c: (c, 0)),
            pl.BlockSpec((CHUNK, D_SSD), lambda c: (c, 0)),
            pl.BlockSpec((1, SSD_HEADS * SSD_P, SSD_N), lambda c: (c, 0, 0)),
        ],
        out_shape=[
            jax.ShapeDtypeStruct((L, D_SSD), F32),
            jax.ShapeDtypeStruct((L, D_SSD), F32),
            jax.ShapeDtypeStruct((nc, SSD_HEADS * SSD_P, SSD_N), F32),
        ],
        scratch_shapes=[
            pltpu.VMEM((SSD_HEADS * SSD_P, SSD_N), F32),
            pltpu.VMEM((CHUNK + 8, D_XBC), F32),
            pltpu.VMEM((CHUNK, D_SSD), F32),
        ],
        compiler_params=_cparams(("arbitrary",)),
    )(z, xbc, xbc, dtp, conv_w, conv_b, dt_bias, a_log, d_skip, norm_w)


def _ssd_bwd(dy, z, ypre, xbc, dtp, prev, conv_w, conv_b, dt_bias, a_log, d_skip, norm_w):
    L = z.shape[0]
    nc = L // CHUNK

    def body(dy_ref, z_ref, ypre_ref, xbc_ref, tail_ref, dt_ref, prev_ref, cw_ref, cb_ref, dtb_ref, alog_ref, dsk_ref,
             nw_ref, dz_ref, dxbc_ref, ddt_ref, gcw_ref, gcb_ref, gdtb_ref, galog_ref, gdsk_ref, gnw_ref,
             dstate, dhead, ext, ext2, dpost):
        i = pl.program_id(0)
        c = nc - 1 - i

        @pl.when(i == 0)
        def _():
            dstate[...] = jnp.zeros_like(dstate)
            dhead[...] = jnp.zeros_like(dhead)
            gcw_ref[...] = jnp.zeros_like(gcw_ref)
            gcb_ref[...] = jnp.zeros_like(gcb_ref)
            gdtb_ref[...] = jnp.zeros_like(gdtb_ref)
            galog_ref[...] = jnp.zeros_like(galog_ref)
            gdsk_ref[...] = jnp.zeros_like(gdsk_ref)
            gnw_ref[...] = jnp.zeros_like(gnw_ref)

        u, sig, xbcv, dtraw, dt, A, acs, acs_row = _ssd_chunk_pre(
            c == 0, xbc_ref, tail_ref, dt_ref, cw_ref, cb_ref, dtb_ref, alog_ref, ext)

        zv = z_ref[...]
        ypre = ypre_ref[...]
        dyn = dy_ref[...]
        sz = _sigmoid(zv)
        silu_z = zv * sz
        yf = ypre * silu_z
        half = D_SSD // SSD_GROUPS
        dyf_parts = []
        for g in range(SSD_GROUPS):
            gs = slice(half * g, half * (g + 1))
            yg = yf[:, gs]
            rstd = lax.rsqrt(jnp.mean(yg * yg, axis=-1, keepdims=True) + RMS_EPS)
            dout = dyn[:, gs]
            gnw_ref[:, gs] += jnp.sum(dout * yg * rstd, axis=0, keepdims=True)
            dyhat = dout * nw_ref[:, gs]
            dyf_parts.append(rstd * (dyhat - yg * (rstd * rstd) * jnp.mean(dyhat * yg, axis=-1, keepdims=True)))
        dyf = jnp.concatenate(dyf_parts, axis=1)
        dz_ref[...] = _bf(dyf * ypre * (sz * (1.0 + zv * (1.0 - sz))))
        dypre = dyf * silu_z

        causal = _iota2((CHUNK, CHUNK), 0) >= _iota2((CHUNK, CHUNK), 1)
        alast = acs[CHUNK - 1:CHUNK, :]
        lane16 = _iota2((1, SSD_HEADS), 1)
        sub16 = _iota2((SSD_HEADS, 1), 0)
        dacs_col = jnp.zeros((CHUNK, SSD_HEADS), F32)
        dacs_row = jnp.zeros((SSD_HEADS, CHUNK), F32)
        ddt_col = jnp.zeros((CHUNK, SSD_HEADS), F32)
        dalast = jnp.zeros((1, SSD_HEADS), F32)
        gdsk = jnp.zeros((1, SSD_HEADS), F32)
        for g in range(SSD_GROUPS):
            bs = slice(D_SSD + SSD_N * g, D_SSD + SSD_N * (g + 1))
            cs = slice(D_SSD + D_BC + SSD_N * g, D_SSD + D_BC + SSD_N * (g + 1))
            Bg = _bf(xbcv[:, bs])
            Cg = _bf(xbcv[:, cs])
            cb = _dot_nt(Cg, Bg)
            dcb = jnp.zeros((CHUNK, CHUNK), F32)
            dB = jnp.zeros((CHUNK, SSD_N), F32)
            dC = jnp.zeros((CHUNK, SSD_N), F32)
            for r in range(SSD_R):
                h = g * SSD_R + r
                hs = slice(SSD_P * h, SSD_P * (h + 1))
                onehot = (lane16 == h).astype(F32)
                acs_c = acs[:, h:h + 1]
                seg = acs_c - acs_row[h:h + 1, :]
                Lm = jnp.where(causal, jnp.exp(jnp.where(causal, seg, 0.0)), 0.0)
                M = cb * Lm
                xh = xbcv[:, hs]
                dth = dt[:, h:h + 1]
                X = xh * dth
                Xb = _bf(X)
                dyh = dypre[:, hs]
                dyb = _bf(dyh)
                prev_h = prev_ref[0, hs, :]
                prevb = _bf(prev_h)
                dnext = dstate[hs, :]
                dnextb = _bf(dnext)
                al = alast[:, h:h + 1]
                eacs = jnp.exp(acs_c)
                eal = jnp.exp(al)
                dsd = jnp.exp(al - acs_c)
                G = _bf(dyh * eacs)
                dstate[hs, :] = dnext * eal + _dot_tn(G, Cg)
                dC = dC + _dot(G, prevb)
                yoff = _dot_nt(Cg, prevb) * eacs
                dacs_h = jnp.sum(dyh * yoff, axis=-1, keepdims=True)
                BdN = _dot_nt(Bg, dnextb)
                dX = dsd * BdN
                dB = dB + _dot(_bf(X * dsd), dnextb)
                t = jnp.sum(X * BdN, axis=-1, keepdims=True) * dsd
                dacs_h = dacs_h - t
                dal = jnp.sum(t, axis=0, keepdims=True) + jnp.sum(
                    jnp.sum(dnext * prev_h, axis=-1, keepdims=True), axis=0, keepdims=True) * eal
                dM = _dot_nt(dyb, Xb)
                dX = dX + _dot_tn(_bf(M), dyb)
                dseg = dM * M
                dcb = dcb + dM * Lm
                dacs_h = dacs_h + jnp.sum(dseg, axis=-1, keepdims=True)
                dacs_row = dacs_row - jnp.sum(dseg, axis=0, keepdims=True) * (sub16 == h).astype(F32)
                dacs_col = dacs_col + dacs_h * onehot
                dalast = dalast + dal * onehot
                ddt_col = ddt_col + jnp.sum(dX * xh, axis=-1, keepdims=True) * onehot
                gdsk = gdsk + jnp.sum(jnp.sum(dyh * xh, axis=-1, keepdims=True), axis=0, keepdims=True) * onehot
                dpost[:, hs] = dX * dth + dsk_ref[:, h:h + 1] * dyh
            dcbb = _bf(dcb)
            dpost[:, bs] = dB + _dot_tn(dcbb, Cg)
            dpost[:, cs] = dC + _dot(dcbb, Bg)

        is_last = (_iota2((CHUNK, 1), 0) == CHUNK - 1).astype(F32)
        dacs = dacs_col + _to_cols(dacs_row) + is_last * dalast
        triu = (_iota2((CHUNK, CHUNK), 0) <= _iota2((CHUNK, CHUNK), 1)).astype(F32)
        da = jnp.dot(triu, dacs, preferred_element_type=F32, precision=HI)
        ddt_tot = ddt_col + da * A
        galog_ref[...] += jnp.sum(da * dt, axis=0, keepdims=True) * A
        ddtraw = ddt_tot * _sigmoid(dtraw)
        gdtb_ref[...] += jnp.sum(ddtraw, axis=0, keepdims=True)
        gdsk_ref[...] += gdsk
        ddt_ref[...] = jnp.zeros_like(ddt_ref)
        ddt_ref[:, 0:SSD_HEADS] = ddtraw

        dconv = dpost[...] * (sig * (1.0 + u * (1.0 - sig)))
        gcb_ref[...] += jnp.sum(dconv, axis=0, keepdims=True)
        for k in range(CONV_K):
            gcw_ref[k:k + 1, :] += jnp.sum(dconv * ext[pl.ds(5 + k, CHUNK), :], axis=0, keepdims=True)
        ext2[0:CHUNK, :] = dconv
        ext2[CHUNK:CHUNK + 8, :] = dhead[...]
        dx = cw_ref[CONV_K - 1:CONV_K, :] * dconv
        for k in range(CONV_K - 1):
            dx = dx + cw_ref[k:k + 1, :] * ext2[pl.ds(CONV_K - 1 - k, CHUNK), :]
        dxbc_ref[...] = _bf(dx)
        dhead[...] = dconv[0:8, :]

    full = lambda shape: pl.BlockSpec(shape, lambda i: (0, 0))
    rev = lambda wd: pl.BlockSpec((CHUNK, wd), lambda i: (nc - 1 - i, 0))
    return pl.pallas_call(
        body, name="ssd_bwd", grid=(nc,),
        in_specs=[
            rev(D_SSD), rev(D_SSD), rev(D_SSD), rev(D_XBC),
            pl.BlockSpec((8, D_XBC), lambda i: (jnp.maximum((nc - 1 - i) * (CHUNK // 8) - 1, 0), 0)),
            rev(DT_PAD),
            pl.BlockSpec((1, SSD_HEADS * SSD_P, SSD_N), lambda i: (nc - 1 - i, 0, 0)),
            full((CONV_K, D_XBC)), full((1, D_XBC)), full((1, SSD_HEADS)), full((1, SSD_HEADS)), full((1, SSD_HEADS)),
            full((1, D_SSD)),
        ],
        out_specs=[
            rev(D_SSD), rev(D_XBC), rev(DT_PAD),
            full((CONV_K, D_XBC)), full((1, D_XBC)), full((1, SSD_HEADS)), full((1, SSD_HEADS)), full((1, SSD_HEADS)),
            full((1, D_SSD)),
        ],
        out_shape=[
            jax.ShapeDtypeStruct((L, D_SSD), BF16), jax.ShapeDtypeStruct((L, D_XBC), BF16),
            jax.ShapeDtypeStruct((L, DT_PAD), F32),
            jax.ShapeDtypeStruct((CONV_K, D_XBC), F32), jax.ShapeDtypeStruct((1, D_XBC), F32),
            jax.ShapeDtypeStruct((1, SSD_HEADS), F32), jax.ShapeDtypeStruct((1, SSD_HEADS), F32),
            jax.ShapeDtypeStruct((1, SSD_HEADS), F32), jax.ShapeDtypeStruct((1, D_SSD), F32),
        ],
        scratch_shapes=[
            pltpu.VMEM((SSD_HEADS * SSD_P, SSD_N), F32),
            pltpu.VMEM((8, D_XBC), F32),
            pltpu.VMEM((CHUNK + 8, D_XBC), F32),
            pltpu.VMEM((CHUNK + 8, D_XBC), F32),
            pltpu.VMEM((CHUNK, D_XBC), F32),
        ],
        compiler_params=_cparams(("arbitrary",)),
    )(dy, z, ypre, xbc, xbc, dtp, prev, conv_w, conv_b, dt_bias, a_log, d_skip, norm_w)


def _head_expander():
    return (_iota2((SSD_HEADS, D_SSD), 1) // SSD_P == _iota2((SSD_HEADS, D_SSD), 0)).astype(BF16)


def _hi_lo(x):
    hi = _bf(x)
    return hi, _bf(x - hi.astype(F32))


def _expand(v, e):
    hi, lo = _hi_lo(v)
    return _dot(hi, e) + _dot(lo, e)


def _headsum(t, e):
    m = t.shape[0]
    if m < 8:
        t = jnp.broadcast_to(t[0:1], (8, t.shape[1]))
    hi, lo = _hi_lo(t)
    return (_dot_nt(hi, e) + _dot_nt(lo, e))[0:m]


def _ssd_decays(dt, acs, dsk_ref, e):
    alast = acs[CHUNK - 1:CHUNK, :]
    stk = jnp.concatenate([dt, jnp.exp(acs), jnp.exp(alast - acs),
                           jnp.broadcast_to(jnp.exp(alast), (8, SSD_HEADS)),
                           jnp.broadcast_to(dsk_ref[...], (8, SSD_HEADS))], axis=0)
    ex = _expand(stk, e)
    return (ex[0:CHUNK], ex[CHUNK:2 * CHUNK], ex[2 * CHUNK:3 * CHUNK], ex[3 * CHUNK:3 * CHUNK + 1],
            ex[3 * CHUNK + 8:3 * CHUNK + 9])


def _ssd_fwd2(z, xbc, dtp, conv_w, conv_b, dt_bias, a_log, d_skip, norm_w):
    L = z.shape[0]
    nc = L // CHUNK
    half = D_SSD // SSD_GROUPS

    def body(z_ref, xbc_ref, tail_ref, dt_ref, cw_ref, cb_ref, dtb_ref, alog_ref, dsk_ref, nw_ref, smat_ref,
             y_ref, ypre_ref, prev_ref, state, ybuf, mbuf):
        c = pl.program_id(0)

        @pl.when(c == 0)
        def _():
            state[...] = jnp.zeros_like(state)

        u, sig, xbcv, dtraw, dt, A, acs, acs_row, _ = _ssd_chunk_pre2(
            c == 0, xbc_ref, tail_ref, dt_ref, cw_ref, cb_ref, dtb_ref, alog_ref, smat_ref)
        e = _head_expander()
        dtE, eacsE, dsdE, ealE, dskE = _ssd_decays(dt, acs, dsk_ref, e)
        xs = xbcv[:, 0:D_SSD]
        X = xs * dtE
        prev_ref[0] = state[...]
        causal = _iota2((CHUNK, CHUNK), 0) >= _iota2((CHUNK, CHUNK), 1)
        for g in range(SSD_GROUPS):
            gs = slice(half * g, half * (g + 1))
            Bg = _bf(xbcv[:, D_SSD + SSD_N * g:D_SSD + SSD_N * (g + 1)])
            Cg = _bf(xbcv[:, D_SSD + D_BC + SSD_N * g:D_SSD + D_BC + SSD_N * (g + 1)])
            cb = _dot_nt(Cg, Bg)
            for r in range(SSD_R):
                h = g * SSD_R + r
                seg = acs[:, h:h + 1] - acs_row[h:h + 1, :]
                mbuf[h] = _bf(cb * jnp.where(causal, jnp.exp(jnp.where(causal, seg, 0.0)), 0.0))
            st = state[:, gs]
            ybuf[:, gs] = _dot(Cg, _bf(st)) * eacsE[:, gs] + dskE[:, gs] * xs[:, gs]
            state[:, gs] = st * ealE[:, gs] + _dot_tn(Bg, _bf(X[:, gs] * dsdE[:, gs]))
        Xb = _bf(X)
        for h in range(SSD_HEADS):
            hs = slice(SSD_P * h, SSD_P * (h + 1))
            ybuf[:, hs] += _dot(mbuf[h], Xb[:, hs])
        y = ybuf[...]
        ypre_ref[...] = y
        zv = z_ref[...]
        yf = y * (zv * _sigmoid(zv))
        for g in range(SSD_GROUPS):
            gs = slice(half * g, half * (g + 1))
            yg = yf[:, gs]
            ms = jnp.mean(yg * yg, axis=-1, keepdims=True)
            y_ref[:, gs] = _bf(yg * lax.rsqrt(ms + RMS_EPS) * nw_ref[:, gs])

    full = lambda shape: pl.BlockSpec(shape, lambda c: (0, 0))
    return pl.pallas_call(
        body, name="ssd_fwd", grid=(nc,),
        in_specs=[
            pl.BlockSpec((CHUNK, D_SSD), lambda c: (c, 0)),
            pl.BlockSpec((CHUNK, D_XBC), lambda c: (c, 0)),
            pl.BlockSpec((HALO, D_XBC), lambda c: (jnp.maximum(c * (CHUNK // HALO) - 1, 0), 0)),
            pl.BlockSpec((CHUNK, DT_PAD), lambda c: (c, 0)),
            full((CONV_K, D_XBC)), full((1, D_XBC)), full((1, SSD_HEADS)), full((1, SSD_HEADS)), full((1, SSD_HEADS)),
            full((1, D_SSD)), full((3 * CHUNK, 2 * (CHUNK + HALO))),
        ],
        out_specs=[
            pl.BlockSpec((CHUNK, D_SSD), lambda c: (c, 0)),
            pl.BlockSpec((CHUNK, D_SSD), lambda c: (c, 0)),
            pl.BlockSpec((1, SSD_N, D_SSD), lambda c: (c, 0, 0)),
        ],
        out_shape=[
            jax.ShapeDtypeStruct((L, D_SSD), BF16),
            jax.ShapeDtypeStruct((L, D_SSD), F32),
            jax.ShapeDtypeStruct((nc, SSD_N, D_SSD), F32),
        ],
        scratch_shapes=[
            pltpu.VMEM((SSD_N, D_SSD), F32),
            pltpu.VMEM((CHUNK, D_SSD), F32),
            pltpu.VMEM((SSD_HEADS, CHUNK, CHUNK), BF16),
        ],
        compiler_params=_cparams(("arbitrary",)),
    )(z, xbc, xbc, dtp, conv_w, conv_b, dt_bias, a_log, d_skip, norm_w, _shift_matrix((13, 14, 15)))


def _ssd_bwd2(dy, z, ypre, xbc, dtp, prev, conv_w, conv_b, dt_bias, a_log, d_skip, norm_w):
    L = z.shape[0]
    nc = L // CHUNK
    half = D_SSD // SSD_GROUPS

    def body(dy_ref, z_ref, ypre_ref, xbc_ref, tail_ref, dt_ref, prev_ref, cw_ref, cb_ref, dtb_ref, alog_ref, dsk_ref,
             nw_ref, smat_ref, smat2_ref, dz_ref, dxbc_ref, ddt_ref, gcw_ref, gcb_ref, gdtb_ref, galog_ref, gdsk_ref,
             gnw_ref, dstate, dhead, dpost, yobuf, bdbuf, lmbuf, dmbuf, cbbuf):
        i = pl.program_id(0)
        c = nc - 1 - i

        @pl.when(i == 0)
        def _():
            dstate[...] = jnp.zeros_like(dstate)
            dhead[...] = jnp.zeros_like(dhead)
            gcw_ref[...] = jnp.zeros_like(gcw_ref)
            gcb_ref[...] = jnp.zeros_like(gcb_ref)
            gdtb_ref[...] = jnp.zeros_like(gdtb_ref)
            galog_ref[...] = jnp.zeros_like(galog_ref)
            gdsk_ref[...] = jnp.zeros_like(gdsk_ref)
            gnw_ref[...] = jnp.zeros_like(gnw_ref)

        u, sig, xbcv, dtraw, dt, A, acs, acs_row, taps = _ssd_chunk_pre2(
            c == 0, xbc_ref, tail_ref, dt_ref, cw_ref, cb_ref, dtb_ref, alog_ref, smat_ref)
        e = _head_expander()
        dtE, eacsE, dsdE, ealE, dskE = _ssd_decays(dt, acs, dsk_ref, e)
        alast = acs[CHUNK - 1:CHUNK, :]
        xs = xbcv[:, 0:D_SSD]
        X = xs * dtE
        Xb = _bf(X)

        zv = z_ref[...]
        ypre = ypre_ref[...]
        dyn = dy_ref[...]
        sz = _sigmoid(zv)
        silu_z = zv * sz
        yf = ypre * silu_z
        dyf_parts = []
        for g in range(SSD_GROUPS):
            gs = slice(half * g, half * (g + 1))
            yg = yf[:, gs]
            rstd = lax.rsqrt(jnp.mean(yg * yg, axis=-1, keepdims=True) + RMS_EPS)
            dout = dyn[:, gs]
            gnw_ref[:, gs] += jnp.sum(dout * yg * rstd, axis=0, keepdims=True)
            dyhat = dout * nw_ref[:, gs]
            dyf_parts.append(rstd * (dyhat - yg * (rstd * rstd) * jnp.mean(dyhat * yg, axis=-1, keepdims=True)))
        dyf = jnp.concatenate(dyf_parts, axis=1)
        dz_ref[...] = _bf(dyf * ypre * (sz * (1.0 + zv * (1.0 - sz))))
        dyp = dyf * silu_z
        dyb = _bf(dyp)
        G = dyp * eacsE

        causal = _iota2((CHUNK, CHUNK), 0) >= _iota2((CHUNK, CHUNK), 1)
        ST = prev_ref[0]
        dST = dstate[...]
        for g in range(SSD_GROUPS):
            gs = slice(half * g, half * (g + 1))
            bs = slice(D_SSD + SSD_N * g, D_SSD + SSD_N * (g + 1))
            cs = slice(D_SSD + D_BC + SSD_N * g, D_SSD + D_BC + SSD_N * (g + 1))
            Bg = _bf(xbcv[:, bs])
            Cg = _bf(xbcv[:, cs])
            Gb = _bf(G[:, gs])
            STb = _bf(ST[:, gs])
            dSTb = _bf(dST[:, gs])
            dstate[:, gs] = dST[:, gs] * ealE[:, gs] + _dot_tn(Cg, Gb)
            yobuf[:, gs] = _dot(Cg, STb) * eacsE[:, gs]
            bdbuf[:, gs] = _dot(Bg, dSTb)
            dpost[:, cs] = _dot_nt(Gb, STb)
            dpost[:, bs] = _dot_nt(_bf(X[:, gs] * dsdE[:, gs]), dSTb)
            cbbuf[g] = _dot_nt(Cg, Bg)
            for r in range(SSD_R):
                h = g * SSD_R + r
                seg = acs[:, h:h + 1] - acs_row[h:h + 1, :]
                lmbuf[h] = jnp.where(causal, jnp.exp(jnp.where(causal, seg, 0.0)), 0.0)
        for h in range(SSD_HEADS):
            hs = slice(SSD_P * h, SSD_P * (h + 1))
            Mb = _bf(cbbuf[h // SSD_R] * lmbuf[h])
            dmbuf[h] = _dot_nt(dyb[:, hs], Xb[:, hs])
            dpost[:, hs] = _dot_tn(Mb, dyb[:, hs])
        lane16 = _iota2((1, SSD_HEADS), 1)
        sub16 = _iota2((SSD_HEADS, 1), 0)
        dacs_col = jnp.zeros((CHUNK, SSD_HEADS), F32)
        dacs_row = jnp.zeros((SSD_HEADS, CHUNK), F32)
        for g in range(SSD_GROUPS):
            bs = slice(D_SSD + SSD_N * g, D_SSD + SSD_N * (g + 1))
            cs = slice(D_SSD + D_BC + SSD_N * g, D_SSD + D_BC + SSD_N * (g + 1))
            cb = cbbuf[g]
            dcb = jnp.zeros((CHUNK, CHUNK), F32)
            for r in range(SSD_R):
                h = g * SSD_R + r
                dM = dmbuf[h]
                Lm = lmbuf[h]
                dcb = dcb + dM * Lm
                dseg = dM * (cb * Lm)
                dacs_col = dacs_col + jnp.sum(dseg, axis=-1, keepdims=True) * (lane16 == h).astype(F32)
                dacs_row = dacs_row - jnp.sum(dseg, axis=0, keepdims=True) * (sub16 == h).astype(F32)
            dcbb = _bf(dcb)
            dpost[:, bs] += _dot_tn(dcbb, _bf(xbcv[:, cs]))
            dpost[:, cs] += _dot(dcbb, _bf(xbcv[:, bs]))

        BD = bdbuf[...]
        dX = dpost[:, 0:D_SSD] + dsdE * BD
        dsd = jnp.exp(alast - acs)
        T = _headsum(X * BD, e) * dsd
        dalast = jnp.sum(T, axis=0, keepdims=True) + _headsum(
            jnp.sum(dST * ST, axis=0, keepdims=True), e) * jnp.exp(alast)
        is_last = (_iota2((CHUNK, 1), 0) == CHUNK - 1).astype(F32)
        dacs = dacs_col + _to_cols(dacs_row) + _headsum(dyp * yobuf[...], e) - T + is_last * dalast
        triu = (_iota2((CHUNK, CHUNK), 0) <= _iota2((CHUNK, CHUNK), 1)).astype(F32)
        da = jnp.dot(triu, dacs, preferred_element_type=F32, precision=HI)
        ddt_tot = _headsum(dX * xs, e) + da * A
        galog_ref[...] += jnp.sum(da * dt, axis=0, keepdims=True) * A
        ddtraw = ddt_tot * _sigmoid(dtraw)
        gdtb_ref[...] += jnp.sum(ddtraw, axis=0, keepdims=True)
        gdsk_ref[...] += _headsum(jnp.sum(dyp * xs, axis=0, keepdims=True), e)
        ddt_ref[...] = jnp.zeros_like(ddt_ref)
        ddt_ref[:, 0:SSD_HEADS] = ddtraw
        dpost[:, 0:D_SSD] = dX * dtE + dskE * dyp

        dconv = dpost[...] * (sig * (1.0 + u * (1.0 - sig)))
        gcb_ref[...] += jnp.sum(dconv, axis=0, keepdims=True)
        for k in range(CONV_K):
            gcw_ref[k:k + 1, :] += jnp.sum(dconv * taps[k], axis=0, keepdims=True)
        later = _shifted_rows(dconv, dhead[...], smat2_ref)
        dx = cw_ref[CONV_K - 1:CONV_K, :] * dconv
        for k in range(CONV_K - 1):
            dx = dx + cw_ref[k:k + 1, :] * later[k]
        dxbc_ref[...] = _bf(dx)
        dhead[...] = dconv[0:HALO, :]

    full = lambda shape: pl.BlockSpec(shape, lambda i: (0, 0))
    rev = lambda wd: pl.BlockSpec((CHUNK, wd), lambda i: (nc - 1 - i, 0))
    return pl.pallas_call(
        body, name="ssd_bwd", grid=(nc,),
        in_specs=[
            rev(D_SSD), rev(D_SSD), rev(D_SSD), rev(D_XBC),
            pl.BlockSpec((HALO, D_XBC), lambda i: (jnp.maximum((nc - 1 - i) * (CHUNK // HALO) - 1, 0), 0)),
            rev(DT_PAD),
            pl.BlockSpec((1, SSD_N, D_SSD), lambda i: (nc - 1 - i, 0, 0)),
            full((CONV_K, D_XBC)), full((1, D_XBC)), full((1, SSD_HEADS)), full((1, SSD_HEADS)), full((1, SSD_HEADS)),
            full((1, D_SSD)), full((3 * CHUNK, 2 * (CHUNK + HALO))), full((3 * CHUNK, 2 * (CHUNK + HALO))),
        ],
        out_specs=[
            rev(D_SSD), rev(D_XBC), rev(DT_PAD),
            full((CONV_K, D_XBC)), full((1, D_XBC)), full((1, SSD_HEADS)), full((1, SSD_HEADS)), full((1, SSD_HEADS)),
            full((1, D_SSD)),
        ],
        out_shape=[
            jax.ShapeDtypeStruct((L, D_SSD), BF16), jax.ShapeDtypeStruct((L, D_XBC), BF16),
            jax.ShapeDtypeStruct((L, DT_PAD), F32),
            jax.ShapeDtypeStruct((CONV_K, D_XBC), F32), jax.ShapeDtypeStruct((1, D_XBC), F32),
            jax.ShapeDtypeStruct((1, SSD_HEADS), F32), jax.ShapeDtypeStruct((1, SSD_HEADS), F32),
            jax.ShapeDtypeStruct((1, SSD_HEADS), F32), jax.ShapeDtypeStruct((1, D_SSD), F32),
        ],
        scratch_shapes=[
            pltpu.VMEM((SSD_N, D_SSD), F32),
            pltpu.VMEM((HALO, D_XBC), F32),
            pltpu.VMEM((CHUNK, D_XBC), F32),
            pltpu.VMEM((CHUNK, D_SSD), F32),
            pltpu.VMEM((CHUNK, D_SSD), F32),
            pltpu.VMEM((SSD_HEADS, CHUNK, CHUNK), F32),
            pltpu.VMEM((SSD_HEADS, CHUNK, CHUNK), F32),
            pltpu.VMEM((SSD_GROUPS, CHUNK, CHUNK), F32),
        ],
        compiler_params=_cparams(("arbitrary",)),
    )(dy, z, ypre, xbc, xbc, dtp, prev, conv_w, conv_b, dt_bias, a_log, d_skip, norm_w, _shift_matrix((13, 14, 15)),
      _shift_matrix((3, 2, 1)))


def _rope_tables(pos_ref, inv_ref):
    ang = pos_ref[...].astype(F32) * inv_ref[...]
    d = _iota2((1, 2 * ATT_HD), 1) % ATT_HD
    s = jnp.sin(ang)
    return jnp.cos(ang), jnp.where(d < ROPE_DIM // 2, -s, 0.0), jnp.where((d >= ROPE_DIM // 2) & (d < ROPE_DIM), s, 0.0)


def _rope(t, tabs):
    c, s1, s2 = tabs
    n = t.shape[1]
    rep = n // c.shape[1]
    return (t * jnp.tile(c, (1, rep)) + pltpu.roll(t, n - ROPE_DIM // 2, 1) * jnp.tile(s1, (1, rep))
            + pltpu.roll(t, ROPE_DIM // 2, 1) * jnp.tile(s2, (1, rep)))


def _rope_t(t, tabs):
    c, s1, s2 = tabs
    n = t.shape[1]
    rep = n // c.shape[1]
    return (t * jnp.tile(c, (1, rep)) + pltpu.roll(t * jnp.tile(s1, (1, rep)), ROPE_DIM // 2, 1)
            + pltpu.roll(t * jnp.tile(s2, (1, rep)), n - ROPE_DIM // 2, 1))


def _swa_mask(first):
    qi = _iota2((WINDOW, 2 * WINDOW), 0)
    si = _iota2((WINDOW, 2 * WINDOW), 1)
    band = (si > qi) & (si <= qi + WINDOW)
    return band & (jnp.logical_not(first) | (si >= WINDOW))


def _stack_heads(t, j):
    return jnp.concatenate([t[:, ATT_HD * (j * ATT_R + r):ATT_HD * (j * ATT_R + r + 1)] for r in range(ATT_R)], axis=0)


def _stack_cols(ref, j):
    cols = [jnp.broadcast_to(ref[:, j * ATT_R + r:j * ATT_R + r + 1], (WINDOW, 1)) for r in range(ATT_R)]
    return jnp.concatenate(cols, axis=0)


def _swa_mask_t(first):
    si = _iota2((2 * WINDOW, ATT_R * WINDOW), 0)
    qi = _iota2((2 * WINDOW, ATT_R * WINDOW), 1) % WINDOW
    band = (si > qi) & (si <= qi + WINDOW)
    return band & (jnp.logical_not(first) | (si >= WINDOW))


def _head_rows(ref, j, rows=None):
    if ref.shape[0] == 1:
        parts = [jnp.broadcast_to(ref[:, j * ATT_R + r:j * ATT_R + r + 1], (1, WINDOW)) for r in range(ATT_R)]
    else:
        parts = [ref[j * ATT_R + r:j * ATT_R + r + 1, :] for r in range(ATT_R)]
    return jnp.concatenate(parts, axis=1)


def _swa_fwd(q, g, kv, sinks):
    L = q.shape[0]
    nb = L // WINDOW
    scale = ATT_HD ** -0.5

    def body(q_ref, g_ref, kvc_ref, kvp_ref, sink_ref, y_ref, o_ref, lse_ref, otbuf):
        n = pl.program_id(0)
        kk = jnp.concatenate([kvp_ref[:, 0:D_KV], kvc_ref[:, 0:D_KV]], axis=0)
        vv = jnp.concatenate([kvp_ref[:, D_KV:2 * D_KV], kvc_ref[:, D_KV:2 * D_KV]], axis=0)
        valid = _swa_mask_t(n == 0)
        qv = q_ref[...]
        for j in range(ATT_KVH):
            js = slice(ATT_HD * j, ATT_HD * (j + 1))
            st = _dot_nt(kk[:, js], _stack_heads(qv, j)) * scale
            st = jnp.where(valid, st, NEG_BIG)
            sink = _head_rows(sink_ref, j)
            m = jnp.maximum(jnp.max(st, axis=0, keepdims=True), sink)
            p = jnp.exp(st - m)
            denom = jnp.sum(p, axis=0, keepdims=True) + jnp.exp(sink - m)
            ot = _dot_tn(vv[:, js], _bf(p)) * (1.0 / denom)
            lse = m + jnp.log(denom)
            for r in range(ATT_R):
                h = j * ATT_R + r
                otbuf[ATT_HD * h:ATT_HD * (h + 1), :] = ot[:, WINDOW * r:WINDOW * (r + 1)]
                lse_ref[h:h + 1, :] = lse[:, WINDOW * r:WINDOW * (r + 1)]
        o = otbuf[...].T
        o_ref[...] = o
        gv = g_ref[...]
        y_ref[...] = _bf(o * (gv * _sigmoid(gv)))

    cur = lambda wd: pl.BlockSpec((WINDOW, wd), lambda n: (n, 0))
    prv = lambda wd: pl.BlockSpec((WINDOW, wd), lambda n: (jnp.maximum(n - 1, 0), 0))
    return pl.pallas_call(
        body, name="swa_fwd", grid=(nb,),
        in_specs=[cur(D_ATT), cur(D_ATT), cur(2 * D_KV), prv(2 * D_KV), pl.BlockSpec((1, ATT_QH), lambda n: (0, 0))],
        out_specs=[cur(D_ATT), cur(D_ATT), pl.BlockSpec((ATT_QH, WINDOW), lambda n: (0, n))],
        out_shape=[jax.ShapeDtypeStruct((L, D_ATT), BF16), jax.ShapeDtypeStruct((L, D_ATT), F32),
                   jax.ShapeDtypeStruct((ATT_QH, L), F32)],
        scratch_shapes=[pltpu.VMEM((D_ATT, WINDOW), F32)],
        compiler_params=_cparams(("parallel",)),
    )(q, g, kv, kv, sinks)


def _swa_bwd(dy, q, g, kv, o, lse, pos, inv, sinks):
    L = q.shape[0]
    nb = L // WINDOW
    scale = ATT_HD ** -0.5

    def body(dy_ref, q_ref, g_ref, kvc_ref, kvp_ref, o_ref, lse_ref, posc_ref, posp_ref, inv_ref, sink_ref,
             dq_ref, dg_ref, dkv_ref, dsink_ref, carry, dqbuf, dkbuf, dvbuf):
        n = pl.program_id(0)

        @pl.when(n == 0)
        def _():
            dsink_ref[...] = jnp.zeros_like(dsink_ref)

        @pl.when(n < nb)
        def _():
            tc = _rope_tables(posc_ref, inv_ref)
            tp = _rope_tables(posp_ref, inv_ref)
            kk = jnp.concatenate([kvp_ref[:, 0:D_KV], kvc_ref[:, 0:D_KV]], axis=0)
            vv = jnp.concatenate([kvp_ref[:, D_KV:2 * D_KV], kvc_ref[:, D_KV:2 * D_KV]], axis=0)
            valid = _swa_mask_t(n == 0)
            qv = q_ref[...]
            gv = g_ref[...]
            sg = _sigmoid(gv)
            dyv = dy_ref[...]
            ov = o_ref[...]
            dg_ref[...] = _bf(dyv * ov * (sg * (1.0 + gv * (1.0 - sg))))
            do = dyv * (gv * sg)
            dod = do * ov
            ones = jnp.ones((8, ATT_HD), BF16)
            lane16 = _iota2((1, ATT_QH), 1)
            dsink = jnp.zeros((1, ATT_QH), F32)
            for j in range(ATT_KVH):
                js = slice(ATT_HD * j, ATT_HD * (j + 1))
                kj = kk[:, js]
                vj = vv[:, js]
                qs = _stack_heads(qv, j)
                dos = _bf(_stack_heads(do, j))
                hi, lo = _hi_lo(_stack_heads(dod, j))
                delta = (_dot_nt(ones, hi) + _dot_nt(ones, lo))[0:1]
                lse = _head_rows(lse_ref, j)
                st = _dot_nt(kj, qs) * scale
                pt = jnp.exp(jnp.where(valid, st, NEG_BIG) - lse)
                dst = _bf(pt * (_dot_nt(vj, dos) - delta))
                dqt = _dot_tn(kj, dst) * scale
                dkbuf[:, js] = _dot(dst, qs) * scale
                dvbuf[:, js] = _dot(_bf(pt), dos)
                sd = jnp.exp(_head_rows(sink_ref, j) - lse) * delta
                for r in range(ATT_R):
                    h = j * ATT_R + r
                    ls = slice(WINDOW * r, WINDOW * (r + 1))
                    dqbuf[ATT_HD * h:ATT_HD * (h + 1), :] = dqt[:, ls]
                    dsink = dsink - jnp.sum(sd[:, ls], axis=1, keepdims=True) * (lane16 == h).astype(F32)
            dsink_ref[...] += dsink
            dq_ref[...] = _bf(_rope_t(dqbuf[...].T, tc))
            dkp = _rope_t(dkbuf[0:WINDOW, :], tp)
            dkc = _rope_t(dkbuf[WINDOW:2 * WINDOW, :], tc)

            @pl.when(n > 0)
            def _():
                dkv_ref[:, 0:D_KV] = _bf(carry[:, 0:D_KV] + dkp)
                dkv_ref[:, D_KV:2 * D_KV] = _bf(carry[:, D_KV:2 * D_KV] + dvbuf[0:WINDOW, :])

            carry[:, 0:D_KV] = dkc
            carry[:, D_KV:2 * D_KV] = dvbuf[WINDOW:2 * WINDOW, :]

        @pl.when(n == nb)
        def _():
            dkv_ref[...] = _bf(carry[...])

    last = nb - 1
    cur = lambda wd: pl.BlockSpec((WINDOW, wd), lambda n: (jnp.minimum(n, last), 0))
    prv = lambda wd: pl.BlockSpec((WINDOW, wd), lambda n: (jnp.maximum(jnp.minimum(n, last) - 1, 0), 0))
    return pl.pallas_call(
        body, name="swa_bwd", grid=(nb + 1,),
        in_specs=[cur(D_ATT), cur(D_ATT), cur(D_ATT), cur(2 * D_KV), prv(2 * D_KV), cur(D_ATT),
                  pl.BlockSpec((ATT_QH, WINDOW), lambda n: (0, jnp.minimum(n, last))), cur(1), prv(1),
                  pl.BlockSpec((1, 2 * ATT_HD), lambda n: (0, 0)), pl.BlockSpec((1, ATT_QH), lambda n: (0, 0))],
        out_specs=[cur(D_ATT), cur(D_ATT),
                   pl.BlockSpec((WINDOW, 2 * D_KV), lambda n: (jnp.maximum(n - 1, 0), 0)),
                   pl.BlockSpec((1, ATT_QH), lambda n: (0, 0))],
        out_shape=[jax.ShapeDtypeStruct((L, D_ATT), BF16), jax.ShapeDtypeStruct((L, D_ATT), BF16),
                   jax.ShapeDtypeStruct((L, 2 * D_KV), BF16), jax.ShapeDtypeStruct((1, ATT_QH), F32)],
        scratch_shapes=[pltpu.VMEM((WINDOW, 2 * D_KV), F32), pltpu.VMEM((D_ATT, WINDOW), F32),
                        pltpu.VMEM((2 * WINDOW, D_KV), F32), pltpu.VMEM((2 * WINDOW, D_KV), F32)],
        compiler_params=_cparams(("arbitrary",)),
    )(dy, q, g, kv, kv, o, lse, pos, pos, inv, sinks)


def _out_ln_loss(y_ssd, y_att, x, target, w_out, ln_g, ln_b):
    L = x.shape[0]
    tm = ROW_TILE
    inv_d = 1.0 / D_MODEL

    def body(ys_ref, ya_ref, x_ref, t_ref, w_ref, g_ref, b_ref, dr_ref, dys_ref, dya_ref, loss_ref, gg_ref, gb_ref):
        i = pl.program_id(0)

        @pl.when(i == 0)
        def _():
            loss_ref[...] = jnp.zeros_like(loss_ref)
            gg_ref[...] = jnp.zeros_like(gg_ref)
            gb_ref[...] = jnp.zeros_like(gb_ref)

        h = _dot(_bf(ys_ref[...]), w_ref[0:D_SSD, :]) + _dot(_bf(ya_ref[...]), w_ref[D_SSD:D_MIX, :])
        r = ALPHA * x_ref[...] + h
        mu = jnp.mean(r, axis=-1, keepdims=True)
        xc = r - mu
        rstd = lax.rsqrt(jnp.mean(xc * xc, axis=-1, keepdims=True) + LN_EPS)
        xhat = xc * rstd
        gam = g_ref[...]
        diff = xhat * gam + b_ref[...] - t_ref[...]
        part = jnp.sum(jnp.sum(diff * diff, axis=-1, keepdims=True), axis=0, keepdims=True)
        loss_ref[...] += (0.5 * inv_d) * part
        dout = diff * inv_d
        gg_ref[...] += jnp.sum(dout * xhat, axis=0, keepdims=True)
        gb_ref[...] += jnp.sum(dout, axis=0, keepdims=True)
        dxh = dout * gam
        dr = rstd * (dxh - jnp.mean(dxh, axis=-1, keepdims=True) - xhat * jnp.mean(dxh * xhat, axis=-1, keepdims=True))
        dr_ref[...] = dr
        drb = _bf(dr)
        dys_ref[...] = _dot_nt(drb, w_ref[0:D_SSD, :])
        dya_ref[...] = _dot_nt(drb, w_ref[D_SSD:D_MIX, :])

    row = pl.BlockSpec((tm, D_MODEL), lambda i: (i, 0))
    vec = pl.BlockSpec((1, D_MODEL), lambda i: (0, 0))
    return pl.pallas_call(
        body, name="out_ln_loss", grid=(L // tm,),
        in_specs=[row, row, row, row, pl.BlockSpec((D_MIX, D_MODEL), lambda i: (0, 0), pipeline_mode=pl.Buffered(1)), vec, vec],
        out_specs=[row, row, row, pl.BlockSpec((1, 128), lambda i: (0, 0)), vec, vec],
        out_shape=[jax.ShapeDtypeStruct((L, D_MODEL), F32)] * 3 + [jax.ShapeDtypeStruct((1, 128), F32)]
        + [jax.ShapeDtypeStruct((1, D_MODEL), F32)] * 2,
        compiler_params=_cparams(("arbitrary",)),
    )(y_ssd, y_att, x, target, w_out, ln_g, ln_b)


def _local_step(x, pos, target, w, get_w_out, token, conv_w, conv_b, dt_bias, a_log, d_skip, norm_w, sinks, ln_g, ln_b):
    inv8 = ROPE_THETA ** (-jnp.arange(0, ROPE_DIM, 2, dtype=F32) / ROPE_DIM)
    inv = jnp.tile(jnp.concatenate([inv8, inv8, jnp.zeros((ATT_HD - ROPE_DIM,), F32)]), 2).reshape(1, 2 * ATT_HD)
    inv = inv + token

    z, g, q, xbc, kv, dtp, xb = _in_proj(x, w, pos, inv)
    y_ssd, y_pre, prev = _ssd_fwd2(z, xbc, dtp, conv_w, conv_b, dt_bias, a_log, d_skip, norm_w)
    y_att, o, lse = _swa_fwd(q, g, kv, sinks)
    w_out = get_w_out(lse)
    dr, dy_ssd, dy_att, loss, g_ln_g, g_ln_b = _out_ln_loss(y_ssd, y_att, x, target, w_out, ln_g, ln_b)
    gw_out_ssd, gw_out_att = _matmuls_tn([y_ssd, y_att], dr, "gw_out", out_dtype=BF16)
    slabs = jnp.concatenate([gw_out_ssd, gw_out_att], axis=0).reshape(N_CHIPS, W_OUT_ROWS, D_MODEL)
    w_out_red = _reduce_w_out_start(slabs, loss)
    inv = inv + w_out_red[16][0:1, :]
    dq, dg, dkv, g_sinks = _swa_bwd(dy_att, q, g, kv, o, lse, pos, inv, sinks)
    dz, dxbc, ddt, g_conv_w, g_conv_b, g_dt_bias, g_a_log, g_d_skip, g_norm_w = _ssd_bwd2(
        dy_ssd, z, y_pre, xbc, dtp, prev, conv_w, conv_b, dt_bias, a_log, d_skip, norm_w)
    gw_z, gw_g, gw_q = _matmuls_tn([dz, dg, dq], xb, "gw_zgq")
    gw_xbc, gw_kv, gw_dt = _matmuls_tn([dxbc, dkv, ddt], xb, "gw_xbc_kv_dt")
    gw_in = jnp.concatenate([gw_z, gw_xbc, gw_dt[0:SSD_HEADS], gw_q, gw_kv, gw_g], axis=0)
    small = dict(conv_w=g_conv_w, conv_b=g_conv_b, dt_bias=g_dt_bias, a_log=g_a_log, d_skip=g_d_skip,
                 ssd_norm_w=g_norm_w, attn_sinks=g_sinks, ln_g=g_ln_g, ln_b=g_ln_b)
    return loss, (dr, dz, dg, dq, dxbc, dkv, ddt, w), gw_in, w_out_red, small


def _mesh_pos():
    return lax.axis_index("x"), lax.axis_index("y"), lax.axis_index("c")


def _gather_weights(w_in_s, conv_w_s):
    def body(win_ref, cw_ref, owin_ref, ocw_ref, send_sems, recv_sems, small_send, small_recv, local_sems):
        x, y, c = _mesh_pos()
        me = 2 * x + y
        sibling = (x, y, 1 - c)
        chips = [(1 - x, y), (x, 1 - y), (1 - x, 1 - y)]
        locals_ = [pltpu.make_async_copy(cw_ref, ocw_ref.at[me], local_sems.at[0])]
        for cp in locals_:
            cp.start()
        started = []
        for t, (src, dst) in enumerate(((win_ref, owin_ref),)):
            hr = src.shape[0] // 2

            def half(ref, hc, hr=hr):
                return ref.at[pl.ds(hc * hr, hr), :]

            for j, (px, py) in enumerate(chips):
                cp = pltpu.make_async_remote_copy(
                    src_ref=half(src, c), dst_ref=half(dst.at[me], c), send_sem=send_sems.at[t, j],
                    recv_sem=recv_sems.at[t, j], device_id=(px, py, c), device_id_type=MESH)
                cp.start()
                started.append(cp)
        for j, (px, py) in enumerate(chips):
            cp = pltpu.make_async_remote_copy(
                src_ref=cw_ref, dst_ref=ocw_ref.at[me], send_sem=small_send.at[j], recv_sem=small_recv.at[j],
                device_id=(px, py, c), device_id_type=MESH)
            cp.start()
            started.append(cp)
        for t, (src, dst) in enumerate(((win_ref, owin_ref),)):
            hr = src.shape[0] // 2
            for j, (px, py) in enumerate(chips):
                src_chip = 2 * px + py
                blk = dst.at[src_chip].at[pl.ds(c * hr, hr), :]
                pltpu.make_async_remote_copy(
                    src_ref=blk, dst_ref=blk, send_sem=send_sems.at[t, j], recv_sem=recv_sems.at[t, j],
                    device_id=(px, py, c), device_id_type=MESH).wait_recv()
                cp = pltpu.make_async_remote_copy(
                    src_ref=blk, dst_ref=blk, send_sem=send_sems.at[t, 3 + j], recv_sem=recv_sems.at[t, 3 + j],
                    device_id=sibling, device_id_type=MESH)
                cp.start()
                started.append(cp)
        for t, (src, dst) in enumerate(((win_ref, owin_ref),)):
            hr = src.shape[0] // 2
            for j, (px, py) in enumerate(chips):
                src_chip = 2 * px + py
                blk = dst.at[src_chip].at[pl.ds((1 - c) * hr, hr), :]
                pltpu.make_async_remote_copy(
                    src_ref=blk, dst_ref=blk, send_sem=send_sems.at[t, 3 + j], recv_sem=recv_sems.at[t, 3 + j],
                    device_id=sibling, device_id_type=MESH).wait_recv()
        for j in range(3):
            pltpu.make_async_remote_copy(
                src_ref=cw_ref, dst_ref=ocw_ref.at[me], send_sem=small_send.at[j], recv_sem=small_recv.at[j],
                device_id=sibling, device_id_type=MESH).wait_recv()
        for cp in started:
            cp.wait_send()
        for cp in locals_:
            cp.wait()

    any_spec = pl.BlockSpec(memory_space=pl.ANY)
    return pl.pallas_call(
        body, name="gather_weights",
        in_specs=[any_spec] * 2, out_specs=[any_spec] * 2,
        out_shape=[jax.ShapeDtypeStruct((N_CHIPS,) + a.shape, a.dtype) for a in (w_in_s, conv_w_s)],
        scratch_shapes=[pltpu.SemaphoreType.DMA((1, 6)), pltpu.SemaphoreType.DMA((1, 6)),
                        pltpu.SemaphoreType.DMA((3,)), pltpu.SemaphoreType.DMA((3,)), pltpu.SemaphoreType.DMA((3,))],
    )(w_in_s, conv_w_s)


_HBM = pl.BlockSpec(memory_space=pltpu.HBM)
_SEM = pl.BlockSpec(memory_space=pltpu.SEMAPHORE)
_EFFECT = pltpu.SideEffectType.DATAFLOW_SIDE_EFFECTING


def _gather_w_out_start(w_out_s, after):
    def body(src_ref, land_ref, after_ref, s0, s1, s2, r0, r1, r2, src_thru, land_thru, token):
        x, y, c = _mesh_pos()
        me = 2 * x + y
        chips = [(1 - x, y), (x, 1 - y), (1 - x, 1 - y)]
        for (px, py), s, r in zip(chips, (s0, s1, s2), (r0, r1, r2)):
            pltpu.make_async_remote_copy(src_ref=src_ref, dst_ref=land_ref.at[me], send_sem=s, recv_sem=r,
                                         device_id=(px, py, c), device_id_type=MESH).start()
        token[...] = jnp.zeros_like(token)

    sem = pltpu.SemaphoreType.DMA(())
    land = lax.empty((N_CHIPS,) + w_out_s.shape, w_out_s.dtype)
    return pl.pallas_call(
        body, name="gather_w_out_start",
        out_shape=(sem,) * 6 + (pltpu.HBM(w_out_s.shape, w_out_s.dtype), pltpu.HBM(land.shape, land.dtype),
                                jax.ShapeDtypeStruct((8, 128), F32)),
        in_specs=(_HBM, _HBM, pl.BlockSpec(memory_space=pl.ANY)),
        out_specs=(_SEM,) * 6 + (_HBM, _HBM, pl.BlockSpec(memory_space=pltpu.VMEM)),
        input_output_aliases={0: 6, 1: 7},
        compiler_params=pltpu.CompilerParams(has_side_effects=_EFFECT),
    )(pltpu.with_memory_space_constraint(w_out_s, pltpu.HBM), pltpu.with_memory_space_constraint(land, pltpu.HBM), after)


def _gather_w_out_wait(sems, src_thru, land_thru, after):
    def body(src_ref, land_ref, s0, s1, s2, r0, r1, r2, after_ref, src_dead, got_ref):
        x, y, c = _mesh_pos()
        chips = [(1 - x, y), (x, 1 - y), (1 - x, 1 - y)]
        for (px, py), s, r in zip(chips, (s0, s1, s2), (r0, r1, r2)):
            cp = pltpu.make_async_remote_copy(src_ref=src_ref, dst_ref=land_ref.at[2 * px + py], send_sem=s, recv_sem=r,
                                              device_id=(px, py, c), device_id_type=MESH)
            cp.wait_send()
            cp.wait_recv()

    return pl.pallas_call(
        body, name="gather_w_out_wait",
        out_shape=(pltpu.HBM(src_thru.shape, src_thru.dtype), pltpu.HBM(land_thru.shape, land_thru.dtype)),
        in_specs=(_HBM, _HBM) + (_SEM,) * 6 + (pl.BlockSpec(memory_space=pl.ANY),),
        out_specs=(_HBM, _HBM), input_output_aliases={0: 0, 1: 1},
        compiler_params=pltpu.CompilerParams(has_side_effects=_EFFECT),
    )(src_thru, land_thru, *sems, after)[1]


def _pair_start(gw_in, after):
    hr = gw_in.shape[1] // 2

    def body(src_ref, land_ref, after_ref, *refs):
        x, y, c = _mesh_pos()
        for j in range(N_CHIPS):
            pltpu.make_async_remote_copy(
                src_ref=src_ref.at[j, pl.ds((1 - c) * hr, hr), :], dst_ref=land_ref.at[j], send_sem=refs[j],
                recv_sem=refs[N_CHIPS + j], device_id=(x, y, 1 - c), device_id_type=MESH).start()
        refs[10][...] = jnp.zeros_like(refs[10])

    sem = pltpu.SemaphoreType.DMA(())
    land = lax.empty((N_CHIPS, hr, D_MODEL), F32)
    return pl.pallas_call(
        body, name="pair_start",
        out_shape=(sem,) * 8 + (pltpu.HBM(gw_in.shape, F32), pltpu.HBM(land.shape, F32), jax.ShapeDtypeStruct((8, 128), F32)),
        in_specs=(_HBM, _HBM, pl.BlockSpec(memory_space=pl.ANY)),
        out_specs=(_SEM,) * 8 + (_HBM, _HBM, pl.BlockSpec(memory_space=pltpu.VMEM)),
        input_output_aliases={0: 8, 1: 9},
        compiler_params=pltpu.CompilerParams(has_side_effects=_EFFECT),
    )(pltpu.with_memory_space_constraint(gw_in, pltpu.HBM), pltpu.with_memory_space_constraint(land, pltpu.HBM), after)


def _pair_wait(sems, gw_thru, land_thru, after):
    hr = land_thru.shape[1]

    def body(src_ref, land_ref, *refs):
        x, y, c = _mesh_pos()
        for j in range(N_CHIPS):
            cp = pltpu.make_async_remote_copy(
                src_ref=src_ref.at[j, pl.ds((1 - c) * hr, hr), :], dst_ref=land_ref.at[j], send_sem=refs[j],
                recv_sem=refs[N_CHIPS + j], device_id=(x, y, 1 - c), device_id_type=MESH)
            cp.wait_send()
            cp.wait_recv()

    return pl.pallas_call(
        body, name="pair_wait",
        out_shape=(pltpu.HBM(gw_thru.shape, F32), pltpu.HBM(land_thru.shape, F32)),
        in_specs=(_HBM, _HBM) + (_SEM,) * 8 + (pl.BlockSpec(memory_space=pl.ANY),),
        out_specs=(_HBM, _HBM), input_output_aliases={0: 0, 1: 1},
        compiler_params=pltpu.CompilerParams(has_side_effects=_EFFECT),
    )(gw_thru, land_thru, *sems, after)


def _chip_start(s_in, after):
    def body(src_ref, land_ref, after_ref, *refs):
        x, y, c = _mesh_pos()
        me = 2 * x + y
        for j, (px, py) in enumerate([(1 - x, y), (x, 1 - y), (1 - x, 1 - y)]):
            pltpu.make_async_remote_copy(
                src_ref=src_ref.at[2 * px + py], dst_ref=land_ref.at[me], send_sem=refs[j], recv_sem=refs[3 + j],
                device_id=(px, py, c), device_id_type=MESH).start()
        refs[8][...] = jnp.zeros_like(refs[8])

    sem = pltpu.SemaphoreType.DMA(())
    land = lax.empty(s_in.shape, s_in.dtype)
    return pl.pallas_call(
        body, name="chip_start",
        out_shape=(sem,) * 6 + (pltpu.HBM(s_in.shape, s_in.dtype), pltpu.HBM(land.shape, land.dtype),
                                jax.ShapeDtypeStruct((8, 128), F32)),
        in_specs=(_HBM, _HBM, pl.BlockSpec(memory_space=pl.ANY)),
        out_specs=(_SEM,) * 6 + (_HBM, _HBM, pl.BlockSpec(memory_space=pltpu.VMEM)),
        input_output_aliases={0: 6, 1: 7},
        compiler_params=pltpu.CompilerParams(has_side_effects=_EFFECT),
    )(pltpu.with_memory_space_constraint(s_in, pltpu.HBM), pltpu.with_memory_space_constraint(land, pltpu.HBM), after)


def _chip_wait(sems, s_thru, land_thru, after):
    def body(src_ref, land_ref, *refs):
        x, y, c = _mesh_pos()
        for j, (px, py) in enumerate([(1 - x, y), (x, 1 - y), (1 - x, 1 - y)]):
            cp = pltpu.make_async_remote_copy(
                src_ref=src_ref.at[2 * px + py], dst_ref=land_ref.at[2 * px + py], send_sem=refs[j], recv_sem=refs[3 + j],
                device_id=(px, py, c), device_id_type=MESH)
            cp.wait_send()
            cp.wait_recv()

    return pl.pallas_call(
        body, name="chip_wait",
        out_shape=(pltpu.HBM(s_thru.shape, s_thru.dtype), pltpu.HBM(land_thru.shape, land_thru.dtype)),
        in_specs=(_HBM, _HBM) + (_SEM,) * 6 + (pl.BlockSpec(memory_space=pl.ANY),),
        out_specs=(_HBM, _HBM), input_output_aliases={0: 0, 1: 1},
        compiler_params=pltpu.CompilerParams(has_side_effects=_EFFECT),
    )(s_thru, land_thru, *sems, after)


def _pair_share(h_in, small):
    def body(hin_ref, sm_ref, rin_ref, slots_ref, send_sems, recv_sems, small_send, small_recv, local_sem):
        x, y, c = _mesh_pos()
        dev = 4 * x + 2 * y + c
        mine = pltpu.make_async_copy(sm_ref, slots_ref.at[dev], local_sem)
        mine.start()
        share = pltpu.make_async_remote_copy(
            src_ref=hin_ref, dst_ref=rin_ref, send_sem=send_sems.at[0], recv_sem=recv_sems.at[0],
            device_id=(x, y, 1 - c), device_id_type=MESH)
        share.start()
        started = []
        for k in range(1, 8):
            peer = (x ^ ((k >> 2) & 1), y ^ ((k >> 1) & 1), c ^ (k & 1))
            cp = pltpu.make_async_remote_copy(
                src_ref=sm_ref, dst_ref=slots_ref.at[dev], send_sem=small_send.at[k - 1], recv_sem=small_recv.at[k - 1],
                device_id=peer, device_id_type=MESH)
            cp.start()
            started.append(cp)
        share.wait()
        for k in range(1, 8):
            pltpu.make_async_remote_copy(
                src_ref=sm_ref, dst_ref=slots_ref.at[dev], send_sem=small_send.at[k - 1], recv_sem=small_recv.at[k - 1],
                device_id=(x, y, 1 - c), device_id_type=MESH).wait_recv()
        for cp in started:
            cp.wait_send()
        mine.wait()

    any_spec = pl.BlockSpec(memory_space=pl.ANY)
    return pl.pallas_call(
        body, name="pair_share",
        in_specs=[any_spec] * 2, out_specs=[any_spec] * 2,
        out_shape=[jax.ShapeDtypeStruct(h_in.shape, F32), jax.ShapeDtypeStruct((8,) + small.shape, F32)],
        scratch_shapes=[pltpu.SemaphoreType.DMA((1,)), pltpu.SemaphoreType.DMA((1,)),
                        pltpu.SemaphoreType.DMA((7,)), pltpu.SemaphoreType.DMA((7,)), pltpu.SemaphoreType.DMA],
    )(h_in, small)


def _reduce_w_out_start(slabs, after):
    def body(src_ref, land_ref, after_ref, *refs):
        x, y, c = _mesh_pos()
        me = 4 * x + 2 * y + c
        for k in range(1, 8):
            px, py, pc = x ^ ((k >> 2) & 1), y ^ ((k >> 1) & 1), c ^ (k & 1)
            pltpu.make_async_remote_copy(src_ref=src_ref.at[2 * px + py], dst_ref=land_ref.at[me], send_sem=refs[k - 1],
                                         recv_sem=refs[6 + k], device_id=(px, py, pc), device_id_type=MESH).start()
        refs[16][...] = jnp.zeros_like(refs[16])

    sem = pltpu.SemaphoreType.DMA(())
    land = lax.empty((8,) + slabs.shape[1:], slabs.dtype)
    return pl.pallas_call(
        body, name="reduce_w_out_start",
        out_shape=(sem,) * 14 + (pltpu.HBM(slabs.shape, slabs.dtype), pltpu.HBM(land.shape, land.dtype),
                                 jax.ShapeDtypeStruct((8, 128), F32)),
        in_specs=(_HBM, _HBM, pl.BlockSpec(memory_space=pl.ANY)),
        out_specs=(_SEM,) * 14 + (_HBM, _HBM, pl.BlockSpec(memory_space=pltpu.VMEM)),
        input_output_aliases={0: 14, 1: 15},
        compiler_params=pltpu.CompilerParams(has_side_effects=_EFFECT),
    )(pltpu.with_memory_space_constraint(slabs, pltpu.HBM), pltpu.with_memory_space_constraint(land, pltpu.HBM), after)


def _reduce_w_out_wait(sems, slabs_thru, land_thru, after):
    def body(src_ref, land_ref, *refs):
        x, y, c = _mesh_pos()
        for k in range(1, 8):
            px, py, pc = x ^ ((k >> 2) & 1), y ^ ((k >> 1) & 1), c ^ (k & 1)
            cp = pltpu.make_async_remote_copy(
                src_ref=src_ref.at[2 * px + py], dst_ref=land_ref.at[4 * px + 2 * py + pc], send_sem=refs[k - 1],
                recv_sem=refs[6 + k], device_id=(px, py, pc), device_id_type=MESH)
            cp.wait_send()
            cp.wait_recv()

    return pl.pallas_call(
        body, name="reduce_w_out_wait",
        out_shape=(pltpu.HBM(slabs_thru.shape, slabs_thru.dtype), pltpu.HBM(land_thru.shape, land_thru.dtype)),
        in_specs=(_HBM, _HBM) + (_SEM,) * 14 + (pl.BlockSpec(memory_space=pl.ANY),),
        out_specs=(_HBM, _HBM), input_output_aliases={0: 0, 1: 1},
        compiler_params=pltpu.CompilerParams(has_side_effects=_EFFECT),
    )(slabs_thru, land_thru, *sems, after)


def _pair_add(g, recv, core, name):
    _, rows, C = recv.shape
    tc = 256

    def body(core_ref, g_ref, r_ref, o_ref):
        o_ref[...] = _bf(g_ref[...] + r_ref[...])

    spec = pl.BlockSpec((1, rows, tc), lambda j, i, core: (j, 0, i))
    return pl.pallas_call(
        body, name=name,
        grid_spec=pltpu.PrefetchScalarGridSpec(
            num_scalar_prefetch=1, grid=(N_CHIPS, C // tc),
            in_specs=[pl.BlockSpec((1, rows, tc), lambda j, i, core: (j, core[0], i)), spec], out_specs=spec),
        out_shape=jax.ShapeDtypeStruct((N_CHIPS, rows, C), BF16),
        compiler_params=_cparams(("parallel", "parallel")),
    )(core, g, recv)


def _chip_add(own, parts, chip, name):
    _, rows, C = parts.shape
    tc = 256

    def body(chip_ref, own_ref, r0, r1, r2, r3, o_ref):
        acc = None
        for j, r in enumerate((r0, r1, r2, r3)):
            term = jnp.where(chip_ref[0] == j, own_ref[0], r[0]).astype(F32)
            acc = term if acc is None else acc + term
        o_ref[...] = acc

    def slab(j):
        return pl.BlockSpec((1, rows, tc), lambda i, chip: (jnp.where(chip[0] == j, (j + 1) % N_CHIPS, j), 0, i))

    return pl.pallas_call(
        body, name=name,
        grid_spec=pltpu.PrefetchScalarGridSpec(
            num_scalar_prefetch=1, grid=(C // tc,),
            in_specs=[pl.BlockSpec((1, rows, tc), lambda i, chip: (chip[0], 0, i))] + [slab(j) for j in range(N_CHIPS)],
            out_specs=pl.BlockSpec((rows, tc), lambda i, chip: (0, i))),
        out_shape=jax.ShapeDtypeStruct((rows, C), F32),
        compiler_params=_cparams(("parallel",)),
    )(chip, own, parts, parts, parts, parts)


def _adamw_math(w, g, m, v):
    m = ADAM_B1 * m + (1.0 - ADAM_B1) * g
    v = ADAM_B2 * v + (1.0 - ADAM_B2) * (g * g)
    m_hat = m / (1.0 - ADAM_B1 ** ADAM_STEP)
    v_hat = v / (1.0 - ADAM_B2 ** ADAM_STEP)
    delta = -ADAM_LR * (m_hat / (jnp.sqrt(v_hat) + ADAM_EPS) + ADAM_WD * w)
    return delta, m, v


def _adamw_pair(w, g_own, g_sib, m, v, core, name):
    unit = w.ndim == 3
    R, C = w.shape[0], w.shape[-1]
    rows = g_own.shape[0]
    tc = 128

    def body(core_ref, w_ref, go_ref, gs_ref, m_ref, v_ref, d_ref, nm_ref, nv_ref, g_ref):
        first = core_ref[0] == 0
        own, sib = go_ref[...], gs_ref[...]
        g = jnp.concatenate([jnp.where(first, own, sib), jnp.where(first, sib, own)], axis=0)[0:R, :]
        idx = (slice(None), 0, slice(None)) if unit else (slice(None), slice(None))
        d, nm, nv = _adamw_math(w_ref[idx], g, m_ref[idx], v_ref[idx])
        d_ref[idx] = d
        nm_ref[idx] = nm
        nv_ref[idx] = nv
        g_ref[idx] = g

    if unit:
        spec = pl.BlockSpec((R, 1, tc), lambda i, core: (0, 0, i))
    else:
        spec = pl.BlockSpec((R, tc), lambda i, core: (0, i))
    gspec = pl.BlockSpec((rows, tc), lambda i, core: (0, i))
    return pl.pallas_call(
        body, name=name,
        grid_spec=pltpu.PrefetchScalarGridSpec(
            num_scalar_prefetch=1, grid=(C // tc,),
            in_specs=[spec, gspec, gspec, spec, spec], out_specs=[spec] * 4),
        out_shape=[jax.ShapeDtypeStruct(w.shape, F32)] * 4,
        compiler_params=_cparams(("parallel",)),
    )(core, w, g_own, g_sib, m, v)


def _adamw_sum8(w, slabs, land, m, v, ids, name):
    R, C = w.shape
    tc = 128

    def body(ids_ref, w_ref, own_ref, *refs):
        lrefs, (m_ref, v_ref, d_ref, nm_ref, nv_ref, g_ref) = refs[:8], refs[8:]
        g = None
        for d, l_ref in enumerate(lrefs):
            term = jnp.where(ids_ref[0] == d, own_ref[0], l_ref[0]).astype(F32)
            g = term if g is None else g + term
        dl, nm, nv = _adamw_math(w_ref[...], g, m_ref[...], v_ref[...])
        d_ref[...] = dl
        nm_ref[...] = nm
        nv_ref[...] = nv
        g_ref[...] = g

    def slot(d):
        return pl.BlockSpec((1, R, tc), lambda i, ids: (jnp.where(ids[0] == d, (d + 1) % 8, d), 0, i))

    spec = pl.BlockSpec((R, tc), lambda i, ids: (0, i))
    return pl.pallas_call(
        body, name=name,
        grid_spec=pltpu.PrefetchScalarGridSpec(
            num_scalar_prefetch=1, grid=(C // tc,),
            in_specs=[spec, pl.BlockSpec((1, R, tc), lambda i, ids: (ids[1], 0, i))] + [slot(d) for d in range(8)]
            + [spec, spec],
            out_specs=[spec] * 4),
        out_shape=[jax.ShapeDtypeStruct((R, C), F32)] * 4,
        compiler_params=_cparams(("parallel",)),
    )(ids, w, slabs, *([land] * 8), m, v)


SMALL_NAMES = ("conv_b", "ssd_norm_w", "ln_g", "ln_b", "dt_bias", "a_log", "d_skip", "attn_sinks")
SMALL_FIELDS = ((4, 0, D_XBC), (5, 0, D_SSD), (6, 0, D_MODEL), (7, 0, D_MODEL), (5, 1024, SSD_HEADS), (5, 1152, SSD_HEADS),
                (5, 1280, SSD_HEADS), (5, 1408, ATT_QH))
LOSS_FIELD = (6, 1024, 128)
K_SMALL = D_XBC


def _pack_small(g_conv_w, vecs, loss):
    def body(cw_ref, *refs):
        o_ref = refs[-1]
        o_ref[...] = jnp.zeros_like(o_ref)
        o_ref[0:CONV_K, 0:D_XBC] = cw_ref[...]
        for v_ref, (row, off, n) in zip(refs[:-2], SMALL_FIELDS):
            o_ref[row:row + 1, off:off + n] = v_ref[...]
        o_ref[LOSS_FIELD[0]:LOSS_FIELD[0] + 1, LOSS_FIELD[1]:LOSS_FIELD[1] + LOSS_FIELD[2]] = refs[-2][...]

    return pl.pallas_call(
        body, name="pack_small", out_shape=jax.ShapeDtypeStruct((8, K_SMALL), F32), compiler_params=_cparams(),
    )(g_conv_w, *vecs, loss)


def _adamw_small(slots, chip, conv_w, m_conv_w, v_conv_w, params, moms, vars_):
    n_vec = len(SMALL_NAMES)

    def body(chip_ref, s_ref, *refs):
        ins = refs[:3 * (n_vec + 1)]
        outs = refs[3 * (n_vec + 1):-1]
        tot_ref = refs[-1]
        tot = s_ref[0]
        for d in range(1, 8):
            tot = tot + s_ref[d]
        outs[0][...] = tot[LOSS_FIELD[0]:LOSS_FIELD[0] + 1, LOSS_FIELD[1]:LOSS_FIELD[1] + 1]
        off = pl.multiple_of(chip_ref[0] * CONV_COLS, 128)
        tot_ref[...] = tot
        grads = [tot_ref[0:CONV_K, pl.ds(off, CONV_COLS)]]
        grads += [tot[row:row + 1, o:o + n] for row, o, n in SMALL_FIELDS]
        for k, g in enumerate(grads):
            w_ref, m_ref, v_ref = ins[3 * k:3 * k + 3]
            full = (0,) if k == 0 else (Ellipsis,)
            d, nm, nv = _adamw_math(w_ref[full], g, m_ref[full], v_ref[full])
            for o_ref, val in zip(outs[1 + 4 * k:5 + 4 * k], (g, d, nm, nv)):
                o_ref[full] = val

    args = [conv_w, m_conv_w, v_conv_w]
    for w, m, v in zip(params, moms, vars_):
        args += [w, m, v]
    shapes = [jax.ShapeDtypeStruct((1, 1), F32)] + [jax.ShapeDtypeStruct(conv_w.shape, F32)] * 4
    for w in params:
        shapes += [jax.ShapeDtypeStruct(w.shape, F32)] * 4
    vmem = pl.BlockSpec(memory_space=pltpu.VMEM)
    return pl.pallas_call(
        body, name="adamw_small",
        grid_spec=pltpu.PrefetchScalarGridSpec(
            num_scalar_prefetch=1, grid=(1,),
            in_specs=[pl.BlockSpec(slots.shape, lambda i, chip: (0, 0, 0))] + [vmem] * len(args),
            out_specs=[vmem] * len(shapes), scratch_shapes=[pltpu.VMEM((8, K_SMALL), F32)]),
        out_shape=shapes, compiler_params=_cparams(),
    )(chip, slots, *args)


def kernel(x, positions, w_in, conv_w, conv_b, dt_bias, a_log, d_skip, ssd_norm_w, attn_sinks, w_out, ln_g, ln_b, loss_target, m_w_in, m_conv_w, m_conv_b, m_dt_bias, m_a_log, m_d_skip, m_ssd_norm_w, m_attn_sinks, m_w_out, m_ln_g, m_ln_b, v_w_in, v_conv_w, v_conv_b, v_dt_bias, v_a_log, v_d_skip, v_ssd_norm_w, v_attn_sinks, v_w_out, v_ln_g, v_ln_b):
    mx, my, mc = _mesh_pos()
    chip = 2 * mx + my
    L = x.shape[1]

    conv_w_s8 = jnp.pad(conv_w[0], ((0, 8 - CONV_K), (0, 0)))
    pad_rows = ((0, SLAB_ROWS - W_IN_COLS), (0, 0))
    w_in_t = w_in[0].T
    w_in_b, w_out_b = jnp.pad(_bf(w_in_t), pad_rows), _bf(w_out[0])
    ag_in, ag_cw = _gather_weights(w_in_b, conv_w_s8)
    started = _gather_w_out_start(w_out_b, ag_cw)
    own = (jnp.arange(N_CHIPS) == chip)[:, None, None]

    def get_w_out(after):
        landed = _gather_w_out_wait(started[0:6], started[6], started[7], after)
        return jnp.where(own, w_out_b[None], landed).reshape(D_MIX, D_MODEL)

    ag_in = jnp.where(own, w_in_b[None], ag_in)
    w_full = jnp.concatenate([ag_in[j, 0:W_IN_COLS] for j in range(N_CHIPS)], axis=0)
    w = jnp.concatenate([
        w_full[O_Z:O_Z + D_SSD], w_full[O_G:O_G + D_ATT], w_full[O_Q:O_Q + D_ATT],
        w_full[O_XBC:O_XBC + D_XBC], w_full[O_K:O_K + 2 * D_KV], w_full[O_DT:O_DT + SSD_HEADS],
        jnp.zeros((DT_PAD - SSD_HEADS, D_MODEL), BF16)], axis=0)
    conv_w_full = jnp.concatenate([ag_cw[j, 0:CONV_K] for j in range(N_CHIPS)], axis=1)

    loss_part, gx_args, gw_in, w_out_red, small = _local_step(
        x[0], positions[0].reshape(L, 1), loss_target[0], w, get_w_out, started[8][0:1, :], conv_w_full,
        conv_b, dt_bias, a_log, d_skip, ssd_norm_w, attn_sinks, ln_g, ln_b)

    packed = _pack_small(small["conv_w"], [small[n] for n in SMALL_NAMES], loss_part)
    core_id = mc.reshape(1).astype(jnp.int32)
    chip_id = chip.reshape(1).astype(jnp.int32)
    ids = jnp.stack([4 * mx + 2 * my + mc, chip]).astype(jnp.int32)
    slabs = jnp.stack([jnp.pad(gw_in[W_IN_COLS * j:W_IN_COLS * (j + 1)], pad_rows) for j in range(N_CHIPS)])
    w_in_red = _pair_start(slabs, gw_in[0:8, 0:128])
    grad_x = _grad_x(*gx_args, w_in_red[10], 0)
    gw_in_slabs, recv_in = _pair_wait(w_in_red[0:8], w_in_red[8], w_in_red[9], grad_x[0:8, 0:128])
    s_in = _pair_add(gw_in_slabs, recv_in, core_id, "pair_add_in")
    chip_red = _chip_start(s_in, packed)
    grad_x = _grad_x(*gx_args, chip_red[8], 1, grad_x)
    own_slabs, landed = _reduce_w_out_wait(w_out_red[0:14], w_out_red[14], w_out_red[15], grad_x[L - 8:L, 0:128])
    out_t = _adamw_sum8(w_out[0], own_slabs, landed, m_w_out[0], v_w_out[0], ids, "adamw_w_out")
    d_w_out, nm_w_out, nv_w_out, g_w_out = [a[None] for a in out_t]
    s_in, r_in = _chip_wait(chip_red[0:6], chip_red[6], chip_red[7], out_t[0][0:8, 0:128])
    h_in = _chip_add(s_in, r_in, chip_id, "chip_add_in")
    sib_in, slots = _pair_share(h_in, packed)

    to_rows = lambda a: jnp.transpose(a, (2, 0, 1))
    in_t = _adamw_pair(to_rows(w_in), h_in, sib_in, to_rows(m_w_in), to_rows(v_w_in), core_id, "adamw_w_in")
    d_w_in, nm_w_in, nv_w_in, g_w_in = [jnp.transpose(a, (1, 2, 0)) for a in in_t]

    params = dict(conv_b=conv_b, ssd_norm_w=ssd_norm_w, ln_g=ln_g, ln_b=ln_b, dt_bias=dt_bias, a_log=a_log,
                  d_skip=d_skip, attn_sinks=attn_sinks)
    moms = dict(conv_b=m_conv_b, ssd_norm_w=m_ssd_norm_w, ln_g=m_ln_g, ln_b=m_ln_b, dt_bias=m_dt_bias, a_log=m_a_log,
                d_skip=m_d_skip, attn_sinks=m_attn_sinks)
    vars_ = dict(conv_b=v_conv_b, ssd_norm_w=v_ssd_norm_w, ln_g=v_ln_g, ln_b=v_ln_b, dt_bias=v_dt_bias, a_log=v_a_log,
                 d_skip=v_d_skip, attn_sinks=v_attn_sinks)
    res = _adamw_small(slots, chip_id, conv_w, m_conv_w, v_conv_w, [params[n] for n in SMALL_NAMES],
                       [moms[n] for n in SMALL_NAMES], [vars_[n] for n in SMALL_NAMES])
    loss = res[0][0, 0]
    grads, delta, new_m, new_v = {}, {}, {}, {}
    for k, n in enumerate(("conv_w",) + SMALL_NAMES):
        grads[n], delta[n], new_m[n], new_v[n] = res[1 + 4 * k:5 + 4 * k]
    for dd, a_in, a_out in ((grads, g_w_in, g_w_out), (delta, d_w_in, d_w_out), (new_m, nm_w_in, nm_w_out),
                            (new_v, nv_w_in, nv_w_out)):
        dd["w_in"] = a_in
        dd["w_out"] = a_out
    order = ("w_in", "conv_w", "conv_b", "dt_bias", "a_log", "d_skip", "ssd_norm_w", "attn_sinks", "w_out", "ln_g", "ln_b")
    return (loss, grad_x[None], *[grads[n] for n in order], *[delta[n] for n in order], *[new_m[n] for n in order],
            *[new_v[n] for n in order])
```

```python
import functools

import numpy as np
import jax
import jax.numpy as jnp
from jax import lax
from jax.experimental import pallas as pl
from jax.experimental.pallas import tpu as pltpu

F32 = jnp.float32
BF16 = jnp.bfloat16
MESH = pl.DeviceIdType.MESH

D_MODEL = 1024
D_SSD = 1024
D_ATT = 1024
D_MIX = 2048
SSD_HEADS = 16
SSD_P = 64
SSD_GROUPS = 2
SSD_R = 8
SSD_N = 128
D_BC = 256
D_XBC = 1536
CONV_K = 4
CHUNK = 128
ATT_HD = 64
ATT_QH = 16
ATT_KVH = 4
ATT_R = 4
D_KV = 256
WINDOW = 128
ROPE_THETA = 500000.0
ROPE_DIM = 16
ALPHA = 2.0 ** 0.25
LN_EPS = 1e-5
RMS_EPS = 1e-5
D_IN_PROJ = 5136
O_Z, O_XBC, O_DT, O_Q, O_K, O_V, O_G = 0, 1024, 2560, 2576, 3600, 3856, 4112
P_Z, P_G, P_Q, P_XBC, P_KV, P_DT, P_END = 0, 1024, 2048, 3072, 4608, 5120, 5248
DT_PAD = 128
N_CHIPS = 4
W_IN_COLS = D_IN_PROJ // N_CHIPS
SLAB_ROWS = 1312
W_OUT_ROWS = D_MIX // N_CHIPS
CONV_COLS = D_XBC // N_CHIPS

ADAM_LR = 0.001
ADAM_B1 = 0.9
ADAM_B2 = 0.999
ADAM_EPS = 1e-08
ADAM_WD = 0.01
ADAM_STEP = 10

VMEM_LIMIT = 56 * 1024 * 1024
ROW_TILE = 512
NEG_BIG = -1e30
HI = lax.Precision.HIGHEST


def _cparams(sem=None, **kw):
    if sem is not None:
        kw["dimension_semantics"] = sem
    return pltpu.CompilerParams(vmem_limit_bytes=VMEM_LIMIT, **kw)


def _dot(a, b):
    return jnp.dot(a, b, preferred_element_type=F32)


def _dot_nt(a, b):
    return lax.dot_general(a, b, (((1,), (1,)), ((), ())), preferred_element_type=F32)


def _dot_tn(a, b):
    return lax.dot_general(a, b, (((0,), (0,)), ((), ())), preferred_element_type=F32)


def _bf(a):
    return a.astype(BF16)


def _iota2(shape, dim):
    return lax.broadcasted_iota(jnp.int32, shape, dim)


def _to_rows(col):
    k = col.shape[1]
    eye = (_iota2((k, k), 0) == _iota2((k, k), 1)).astype(F32)
    return lax.dot_general(eye, col, (((1,), (1,)), ((), ())), preferred_element_type=F32, precision=HI)


def _to_cols(row):
    n = row.shape[1]
    eye = (_iota2((n, n), 0) == _iota2((n, n), 1)).astype(F32)
    return lax.dot_general(eye, row, (((1,), (1,)), ((), ())), preferred_element_type=F32, precision=HI)


def _sigmoid(x):
    return jax.nn.sigmoid(x)


def _in_proj(x, w, pos, inv):
    L = x.shape[0]
    tm = ROW_TILE
    widths = (D_SSD, D_ATT, D_ATT, D_XBC, 2 * D_KV, DT_PAD)

    def body(x_ref, w_ref, pos_ref, inv_ref, z_ref, g_ref, q_ref, xbc_ref, kv_ref, dt_ref, xb_ref):
        xb = _bf(x_ref[...])
        xb_ref[...] = xb
        for o_ref, off, wd in zip((z_ref, g_ref, xbc_ref, dt_ref), (P_Z, P_G, P_XBC, P_DT), (D_SSD, D_ATT, D_XBC, DT_PAD)):
            o_ref[...] = _dot_nt(xb, w_ref[off:off + wd, :])
        tabs = _rope_tables(pos_ref, inv_ref)
        q_ref[...] = _bf(_rope(_dot_nt(xb, w_ref[P_Q:P_Q + D_ATT, :]), tabs))
        kv_ref[:, 0:D_KV] = _bf(_rope(_dot_nt(xb, w_ref[P_KV:P_KV + D_KV, :]), tabs))
        kv_ref[:, D_KV:2 * D_KV] = _bf(_dot_nt(xb, w_ref[P_KV + D_KV:P_KV + 2 * D_KV, :]))

    row = lambda wd: pl.BlockSpec((tm, wd), lambda i: (i, 0))
    return pl.pallas_call(
        body, name="in_proj", grid=(L // tm,),
        in_specs=[row(D_MODEL), pl.BlockSpec((P_END, D_MODEL), lambda i: (0, 0), pipeline_mode=pl.Buffered(1)), row(1),
                  pl.BlockSpec((1, 2 * ATT_HD), lambda i: (0, 0))],
        out_specs=[row(wd) for wd in widths] + [row(D_MODEL)],
        out_shape=[jax.ShapeDtypeStruct((L, wd), dt) for wd, dt in zip(widths, (F32, F32, BF16, F32, BF16, F32))]
        + [jax.ShapeDtypeStruct((L, D_MODEL), BF16)],
        compiler_params=_cparams(("parallel",)),
    )(x, w, pos, inv)


def _matmuls_tn(a_list, b, name, out_dtype=F32):
    K, N = b.shape
    tk = min(K, 1024)
    nk = K // tk
    n = len(a_list)
    in_place = out_dtype == F32

    def body(*refs):
        b_ref = refs[n]
        o_refs = refs[n + 1:2 * n + 1]
        acc_refs = o_refs if in_place else refs[2 * n + 1:]
        k = pl.program_id(0)
        bb = _bf(b_ref[...])
        for a_ref, o_ref, acc_ref in zip(refs[:n], o_refs, acc_refs):
            part = _dot_tn(_bf(a_ref[...]), bb)

            @pl.when(k == 0)
            def _():
                acc_ref[...] = part

            @pl.when(k > 0)
            def _():
                acc_ref[...] += part

            if not in_place:
                @pl.when(k == nk - 1)
                def _():
                    o_ref[...] = acc_ref[...].astype(out_dtype)

    return pl.pallas_call(
        body, name=name, grid=(nk,),
        in_specs=[pl.BlockSpec((tk, a.shape[1]), lambda k: (k, 0)) for a in a_list] + [pl.BlockSpec((tk, N), lambda k: (k, 0))],
        out_specs=[pl.BlockSpec((a.shape[1], N), lambda k: (0, 0)) for a in a_list],
        out_shape=[jax.ShapeDtypeStruct((a.shape[1], N), out_dtype) for a in a_list],
        scratch_shapes=[] if in_place else [pltpu.VMEM((a.shape[1], N), F32) for a in a_list],
        compiler_params=_cparams(("arbitrary",)),
    )(*a_list, b)


def _grad_x(dr, dz, dg, dq, dxbc, dkv, ddt, w, after, part, prev=None):
    L = dr.shape[0]
    tm = min(ROW_TILE, L // 2)
    n = L // (2 * tm)
    widths = (D_SSD, D_ATT, D_ATT, D_XBC, 2 * D_KV, DT_PAD)
    offs = (P_Z, P_G, P_Q, P_XBC, P_KV, P_DT)

    def body(dr_ref, dz_ref, dg_ref, dq_ref, dxbc_ref, dkv_ref, ddt_ref, w_ref, after_ref, *rest):
        o_ref = rest[-1]
        acc = ALPHA * dr_ref[...]
        for p_ref, off, wd in zip((dz_ref, dg_ref, dq_ref, dxbc_ref, dkv_ref, ddt_ref), offs, widths):
            acc = acc + _dot(_bf(p_ref[...]), w_ref[off:off + wd, :])
        o_ref[...] = acc

    row = lambda wd: pl.BlockSpec((tm, wd), lambda i: (i + part * n, 0))
    ins = [dr, dz, dg, dq, dxbc, dkv, ddt, w, after]
    specs = ([row(D_MODEL)] + [row(wd) for wd in widths]
             + [pl.BlockSpec((P_END, D_MODEL), lambda i: (0, 0), pipeline_mode=pl.Buffered(1)),
                pl.BlockSpec((8, 128), lambda i: (0, 0))])
    if prev is not None:
        ins.append(prev)
        specs.append(pl.BlockSpec(memory_space=pl.ANY))
    return pl.pallas_call(
        body, name="grad_x_%d" % part, grid=(n,),
        in_specs=specs, out_specs=row(D_MODEL),
        out_shape=jax.ShapeDtypeStruct((L, D_MODEL), F32),
        input_output_aliases={} if prev is None else {len(ins) - 1: 0},
        compiler_params=_cparams(("parallel",)),
    )(*ins)


def _ssd_chunk_pre(first, xbc_ref, tail_ref, dt_ref, cw_ref, cb_ref, dtb_ref, alog_ref, ext):
    tail = jnp.where(first, 0.0, tail_ref[...])
    ext[0:8, :] = tail
    ext[8:8 + CHUNK, :] = xbc_ref[...]
    u = cb_ref[...] + cw_ref[0:1, :] * ext[pl.ds(5, CHUNK), :]
    for k in range(1, CONV_K):
        u = u + cw_ref[k:k + 1, :] * ext[pl.ds(5 + k, CHUNK), :]
    sig = _sigmoid(u)
    xbc = u * sig
    dtraw = dt_ref[:, 0:SSD_HEADS] + dtb_ref[...]
    dt = jax.nn.softplus(dtraw)
    A = -jnp.exp(alog_ref[...])
    a = dt * A
    tril = (_iota2((CHUNK, CHUNK), 0) >= _iota2((CHUNK, CHUNK), 1)).astype(F32)
    acs = jnp.dot(tril, a, preferred_element_type=F32, precision=HI)
    acs_row = _to_rows(acs)
    return u, sig, xbc, dtraw, dt, A, acs, acs_row


HALO = 16


def _shift_matrix(offsets):
    n = CHUNK + HALO
    m = np.zeros((len(offsets) * CHUNK, 2 * n), np.float32)
    for k, off in enumerate(offsets):
        t = np.arange(CHUNK)
        m[k * CHUNK + t, t + off] = 1.0
        m[k * CHUNK + t, n + t + off] = 1.0
    return jnp.asarray(m, BF16)


def _shifted_rows(first_part, second_part, smat_ref):
    h1, l1 = _hi_lo(first_part)
    h2, l2 = _hi_lo(second_part)
    sh = _dot(smat_ref[...], jnp.concatenate([h1, h2, l1, l2], axis=0))
    return sh[0:CHUNK], sh[CHUNK:2 * CHUNK], sh[2 * CHUNK:3 * CHUNK]


def _ssd_chunk_pre2(first, xbc_ref, tail_ref, dt_ref, cw_ref, cb_ref, dtb_ref, alog_ref, smat_ref):
    tail = jnp.where(first, 0.0, tail_ref[...])
    x = xbc_ref[...]
    taps = _shifted_rows(tail, x, smat_ref) + (x,)
    u = cb_ref[...] + cw_ref[0:1, :] * taps[0]
    for k in range(1, CONV_K):
        u = u + cw_ref[k:k + 1, :] * taps[k]
    sig = _sigmoid(u)
    xbc = u * sig
    dtraw = dt_ref[:, 0:SSD_HEADS] + dtb_ref[...]
    dt = jax.nn.softplus(dtraw)
    A = -jnp.exp(alog_ref[...])
    a = dt * A
    tril = (_iota2((CHUNK, CHUNK), 0) >= _iota2((CHUNK, CHUNK), 1)).astype(F32)
    acs = jnp.dot(tril, a, preferred_element_type=F32, precision=HI)
    acs_row = _to_rows(acs)
    return u, sig, xbc, dtraw, dt, A, acs, acs_row, taps


def _ssd_fwd(z, xbc, dtp, conv_w, conv_b, dt_bias, a_log, d_skip, norm_w):
    L = z.shape[0]
    nc = L // CHUNK

    def body(z_ref, xbc_ref, tail_ref, dt_ref, cw_ref, cb_ref, dtb_ref, alog_ref, dsk_ref, nw_ref,
             y_ref, ypre_ref, prev_ref, state, ext, ybuf):
        c = pl.program_id(0)

        @pl.when(c == 0)
        def _():
            state[...] = jnp.zeros_like(state)

        u, sig, xbcv, dtraw, dt, A, acs, acs_row = _ssd_chunk_pre(
            c == 0, xbc_ref, tail_ref, dt_ref, cw_ref, cb_ref, dtb_ref, alog_ref, ext)
        prev_ref[0] = state[...]
        causal = _iota2((CHUNK, CHUNK), 0) >= _iota2((CHUNK, CHUNK), 1)
        alast = acs[CHUNK - 1:CHUNK, :]
        for g in range(SSD_GROUPS):
            Bg = _bf(xbcv[:, D_SSD + SSD_N * g:D_SSD + SSD_N * (g + 1)])
            Cg = _bf(xbcv[:, D_SSD + D_BC + SSD_N * g:D_SSD + D_BC + SSD_N * (g + 1)])
            cb = _dot_nt(Cg, Bg)
            for r in range(SSD_R):
                h = g * SSD_R + r
                hs = slice(SSD_P * h, SSD_P * (h + 1))
                acs_c = acs[:, h:h + 1]
                seg = acs_c - acs_row[h:h + 1, :]
                Lm = jnp.where(causal, jnp.exp(jnp.where(causal, seg, 0.0)), 0.0)
                M = cb * Lm
                xh = xbcv[:, hs]
                X = xh * dt[:, h:h + 1]
                prev_h = state[hs, :]
                ydiag = _dot(_bf(M), _bf(X))
                yoff = _dot_nt(Cg, _bf(prev_h)) * jnp.exp(acs_c)
                al = alast[:, h:h + 1]
                Xd = X * jnp.exp(al - acs_c)
                state[hs, :] = prev_h * jnp.exp(al) + _dot_tn(_bf(Xd), Bg)
                ybuf[:, hs] = ydiag + yoff + dsk_ref[:, h:h + 1] * xh
        y = ybuf[...]
        ypre_ref[...] = y
        zv = z_ref[...]
        yf = y * (zv * _sigmoid(zv))
        half = D_SSD // SSD_GROUPS
        for g in range(SSD_GROUPS):
            gs = slice(half * g, half * (g + 1))
            yg = yf[:, gs]
            ms = jnp.mean(yg * yg, axis=-1, keepdims=True)
            y_ref[:, gs] = _bf(yg * lax.rsqrt(ms + RMS_EPS) * nw_ref[:, gs])

    full = lambda shape: pl.BlockSpec(shape, lambda c: (0, 0))
    return pl.pallas_call(
        body, name="ssd_fwd", grid=(nc,),
        in_specs=[
            pl.BlockSpec((CHUNK, D_SSD), lambda c: (c, 0)),
            pl.BlockSpec((CHUNK, D_XBC), lambda c: (c, 0)),
            pl.BlockSpec((8, D_XBC), lambda c: (jnp.maximum(c * (CHUNK // 8) - 1, 0), 0)),
            pl.BlockSpec((CHUNK, DT_PAD), lambda c: (c, 0)),
            full((CONV_K, D_XBC)), full((1, D_XBC)), full((1, SSD_HEADS)), full((1, SSD_HEADS)), full((1, SSD_HEADS)),
            full((1, D_SSD)),
        ],
        out_specs=[
            pl.BlockSpec((CHUNK, D_SSD), lambda c: (c, 0)),
            pl.BlockSpec((CHUNK, D_SSD), lambda c: (c, 0)),
            pl.BlockSpec((1, SSD_HEADS * SSD_P, SSD_N), lambda c: (c, 0, 0)),
        ],
        out_shape=[
            jax.ShapeDtypeStruct((L, D_SSD), F32),
            jax.ShapeDtypeStruct((L, D_SSD), F32),
            jax.ShapeDtypeStruct((nc, SSD_HEADS * SSD_P, SSD_N), F32),
        ],
        scratch_shapes=[
            pltpu.VMEM((SSD_HEADS * SSD_P, SSD_N), F32),
            pltpu.VMEM((CHUNK + 8, D_XBC), F32),
            pltpu.VMEM((CHUNK, D_SSD), F32),
        ],
        compiler_params=_cparams(("arbitrary",)),
    )(z, xbc, xbc, dtp, conv_w, conv_b, dt_bias, a_log, d_skip, norm_w)


def _ssd_bwd(dy, z, ypre, xbc, dtp, prev, conv_w, conv_b, dt_bias, a_log, d_skip, norm_w):
    L = z.shape[0]
    nc = L // CHUNK

    def body(dy_ref, z_ref, ypre_ref, xbc_ref, tail_ref, dt_ref, prev_ref, cw_ref, cb_ref, dtb_ref, alog_ref, dsk_ref,
             nw_ref, dz_ref, dxbc_ref, ddt_ref, gcw_ref, gcb_ref, gdtb_ref, galog_ref, gdsk_ref, gnw_ref,
             dstate, dhead, ext, ext2, dpost):
        i = pl.program_id(0)
        c = nc - 1 - i

        @pl.when(i == 0)
        def _():
            dstate[...] = jnp.zeros_like(dstate)
            dhead[...] = jnp.zeros_like(dhead)
            gcw_ref[...] = jnp.zeros_like(gcw_ref)
            gcb_ref[...] = jnp.zeros_like(gcb_ref)
            gdtb_ref[...] = jnp.zeros_like(gdtb_ref)
            galog_ref[...] = jnp.zeros_like(galog_ref)
            gdsk_ref[...] = jnp.zeros_like(gdsk_ref)
            gnw_ref[...] = jnp.zeros_like(gnw_ref)

        u, sig, xbcv, dtraw, dt, A, acs, acs_row = _ssd_chunk_pre(
            c == 0, xbc_ref, tail_ref, dt_ref, cw_ref, cb_ref, dtb_ref, alog_ref, ext)

        zv = z_ref[...]
        ypre = ypre_ref[...]
        dyn = dy_ref[...]
        sz = _sigmoid(zv)
        silu_z = zv * sz
        yf = ypre * silu_z
        half = D_SSD // SSD_GROUPS
        dyf_parts = []
        for g in range(SSD_GROUPS):
            gs = slice(half * g, half * (g + 1))
            yg = yf[:, gs]
            rstd = lax.rsqrt(jnp.mean(yg * yg, axis=-1, keepdims=True) + RMS_EPS)
            dout = dyn[:, gs]
            gnw_ref[:, gs] += jnp.sum(dout * yg * rstd, axis=0, keepdims=True)
            dyhat = dout * nw_ref[:, gs]
            dyf_parts.append(rstd * (dyhat - yg * (rstd * rstd) * jnp.mean(dyhat * yg, axis=-1, keepdims=True)))
        dyf = jnp.concatenate(dyf_parts, axis=1)
        dz_ref[...] = _bf(dyf * ypre * (sz * (1.0 + zv * (1.0 - sz))))
        dypre = dyf * silu_z

        causal = _iota2((CHUNK, CHUNK), 0) >= _iota2((CHUNK, CHUNK), 1)
        alast = acs[CHUNK - 1:CHUNK, :]
        lane16 = _iota2((1, SSD_HEADS), 1)
        sub16 = _iota2((SSD_HEADS, 1), 0)
        dacs_col = jnp.zeros((CHUNK, SSD_HEADS), F32)
        dacs_row = jnp.zeros((SSD_HEADS, CHUNK), F32)
        ddt_col = jnp.zeros((CHUNK, SSD_HEADS), F32)
        dalast = jnp.zeros((1, SSD_HEADS), F32)
        gdsk = jnp.zeros((1, SSD_HEADS), F32)
        for g in range(SSD_GROUPS):
            bs = slice(D_SSD + SSD_N * g, D_SSD + SSD_N * (g + 1))
            cs = slice(D_SSD + D_BC + SSD_N * g, D_SSD + D_BC + SSD_N * (g + 1))
            Bg = _bf(xbcv[:, bs])
            Cg = _bf(xbcv[:, cs])
            cb = _dot_nt(Cg, Bg)
            dcb = jnp.zeros((CHUNK, CHUNK), F32)
            dB = jnp.zeros((CHUNK, SSD_N), F32)
            dC = jnp.zeros((CHUNK, SSD_N), F32)
            for r in range(SSD_R):
                h = g * SSD_R + r
                hs = slice(SSD_P * h, SSD_P * (h + 1))
                onehot = (lane16 == h).astype(F32)
                acs_c = acs[:, h:h + 1]
                seg = acs_c - acs_row[h:h + 1, :]
                Lm = jnp.where(causal, jnp.exp(jnp.where(causal, seg, 0.0)), 0.0)
                M = cb * Lm
                xh = xbcv[:, hs]
                dth = dt[:, h:h + 1]
                X = xh * dth
                Xb = _bf(X)
                dyh = dypre[:, hs]
                dyb = _bf(dyh)
                prev_h = prev_ref[0, hs, :]
                prevb = _bf(prev_h)
                dnext = dstate[hs, :]
                dnextb = _bf(dnext)
                al = alast[:, h:h + 1]
                eacs = jnp.exp(acs_c)
                eal = jnp.exp(al)
                dsd = jnp.exp(al - acs_c)
                G = _bf(dyh * eacs)
                dstate[hs, :] = dnext * eal + _dot_tn(G, Cg)
                dC = dC + _dot(G, prevb)
                yoff = _dot_nt(Cg, prevb) * eacs
                dacs_h = jnp.sum(dyh * yoff, axis=-1, keepdims=True)
                BdN = _dot_nt(Bg, dnextb)
                dX = dsd * BdN
                dB = dB + _dot(_bf(X * dsd), dnextb)
                t = jnp.sum(X * BdN, axis=-1, keepdims=True) * dsd
                dacs_h = dacs_h - t
                dal = jnp.sum(t, axis=0, keepdims=True) + jnp.sum(
                    jnp.sum(dnext * prev_h, axis=-1, keepdims=True), axis=0, keepdims=True) * eal
                dM = _dot_nt(dyb, Xb)
                dX = dX + _dot_tn(_bf(M), dyb)
                dseg = dM * M
                dcb = dcb + dM * Lm
                dacs_h = dacs_h + jnp.sum(dseg, axis=-1, keepdims=True)
                dacs_row = dacs_row - jnp.sum(dseg, axis=0, keepdims=True) * (sub16 == h).astype(F32)
                dacs_col = dacs_col + dacs_h * onehot
                dalast = dalast + dal * onehot
                ddt_col = ddt_col + jnp.sum(dX * xh, axis=-1, keepdims=True) * onehot
                gdsk = gdsk + jnp.sum(jnp.sum(dyh * xh, axis=-1, keepdims=True), axis=0, keepdims=True) * onehot
                dpost[:, hs] = dX * dth + dsk_ref[:, h:h + 1] * dyh
            dcbb = _bf(dcb)
            dpost[:, bs] = dB + _dot_tn(dcbb, Cg)
            dpost[:, cs] = dC + _dot(dcbb, Bg)

        is_last = (_iota2((CHUNK, 1), 0) == CHUNK - 1).astype(F32)
        dacs = dacs_col + _to_cols(dacs_row) + is_last * dalast
        triu = (_iota2((CHUNK, CHUNK), 0) <= _iota2((CHUNK, CHUNK), 1)).astype(F32)
        da = jnp.dot(triu, dacs, preferred_element_type=F32, precision=HI)
        ddt_tot = ddt_col + da * A
        galog_ref[...] += jnp.sum(da * dt, axis=0, keepdims=True) * A
        ddtraw = ddt_tot * _sigmoid(dtraw)
        gdtb_ref[...] += jnp.sum(ddtraw, axis=0, keepdims=True)
        gdsk_ref[...] += gdsk
        ddt_ref[...] = jnp.zeros_like(ddt_ref)
        ddt_ref[:, 0:SSD_HEADS] = ddtraw

        dconv = dpost[...] * (sig * (1.0 + u * (1.0 - sig)))
        gcb_ref[...] += jnp.sum(dconv, axis=0, keepdims=True)
        for k in range(CONV_K):
            gcw_ref[k:k + 1, :] += jnp.sum(dconv * ext[pl.ds(5 + k, CHUNK), :], axis=0, keepdims=True)
        ext2[0:CHUNK, :] = dconv
        ext2[CHUNK:CHUNK + 8, :] = dhead[...]
        dx = cw_ref[CONV_K - 1:CONV_K, :] * dconv
        for k in range(CONV_K - 1):
            dx = dx + cw_ref[k:k + 1, :] * ext2[pl.ds(CONV_K - 1 - k, CHUNK), :]
        dxbc_ref[...] = _bf(dx)
        dhead[...] = dconv[0:8, :]

    full = lambda shape: pl.BlockSpec(shape, lambda i: (0, 0))
    rev = lambda wd: pl.BlockSpec((CHUNK, wd), lambda i: (nc - 1 - i, 0))
    return pl.pallas_call(
        body, name="ssd_bwd", grid=(nc,),
        in_specs=[
            rev(D_SSD), rev(D_SSD), rev(D_SSD), rev(D_XBC),
            pl.BlockSpec((8, D_XBC), lambda i: (jnp.maximum((nc - 1 - i) * (CHUNK // 8) - 1, 0), 0)),
            rev(DT_PAD),
            pl.BlockSpec((1, SSD_HEADS * SSD_P, SSD_N), lambda i: (nc - 1 - i, 0, 0)),
            full((CONV_K, D_XBC)), full((1, D_XBC)), full((1, SSD_HEADS)), full((1, SSD_HEADS)), full((1, SSD_HEADS)),
            full((1, D_SSD)),
        ],
        out_specs=[
            rev(D_SSD), rev(D_XBC), rev(DT_PAD),
            full((CONV_K, D_XBC)), full((1, D_XBC)), full((1, SSD_HEADS)), full((1, SSD_HEADS)), full((1, SSD_HEADS)),
            full((1, D_SSD)),
        ],
        out_shape=[
            jax.ShapeDtypeStruct((L, D_SSD), BF16), jax.ShapeDtypeStruct((L, D_XBC), BF16),
            jax.ShapeDtypeStruct((L, DT_PAD), F32),
            jax.ShapeDtypeStruct((CONV_K, D_XBC), F32), jax.ShapeDtypeStruct((1, D_XBC), F32),
            jax.ShapeDtypeStruct((1, SSD_HEADS), F32), jax.ShapeDtypeStruct((1, SSD_HEADS), F32),
            jax.ShapeDtypeStruct((1, SSD_HEADS), F32), jax.ShapeDtypeStruct((1, D_SSD), F32),
        ],
        scratch_shapes=[
            pltpu.VMEM((SSD_HEADS * SSD_P, SSD_N), F32),
            pltpu.VMEM((8, D_XBC), F32),
            pltpu.VMEM((CHUNK + 8, D_XBC), F32),
            pltpu.VMEM((CHUNK + 8, D_XBC), F32),
            pltpu.VMEM((CHUNK, D_XBC), F32),
        ],
        compiler_params=_cparams(("arbitrary",)),
    )(dy, z, ypre, xbc, xbc, dtp, prev, conv_w, conv_b, dt_bias, a_log, d_skip, norm_w)


def _head_expander():
    return (_iota2((SSD_HEADS, D_SSD), 1) // SSD_P == _iota2((SSD_HEADS, D_SSD), 0)).astype(BF16)


def _hi_lo(x):
    hi = _bf(x)
    return hi, _bf(x - hi.astype(F32))


def _expand(v, e):
    hi, lo = _hi_lo(v)
    return _dot(hi, e) + _dot(lo, e)


def _headsum(t, e):
    m = t.shape[0]
    if m < 8:
        t = jnp.broadcast_to(t[0:1], (8, t.shape[1]))
    hi, lo = _hi_lo(t)
    return (_dot_nt(hi, e) + _dot_nt(lo, e))[0:m]


def _ssd_decays(dt, acs, dsk_ref, e):
    alast = acs[CHUNK - 1:CHUNK, :]
    stk = jnp.concatenate([dt, jnp.exp(acs), jnp.exp(alast - acs),
                           jnp.broadcast_to(jnp.exp(alast), (8, SSD_HEADS)),
                           jnp.broadcast_to(dsk_ref[...], (8, SSD_HEADS))], axis=0)
    ex = _expand(stk, e)
    return (ex[0:CHUNK], ex[CHUNK:2 * CHUNK], ex[2 * CHUNK:3 * CHUNK], ex[3 * CHUNK:3 * CHUNK + 1],
            ex[3 * CHUNK + 8:3 * CHUNK + 9])


def _ssd_fwd2(z, xbc, dtp, conv_w, conv_b, dt_bias, a_log, d_skip, norm_w):
    L = z.shape[0]
    nc = L // CHUNK
    half = D_SSD // SSD_GROUPS

    def body(z_ref, xbc_ref, tail_ref, dt_ref, cw_ref, cb_ref, dtb_ref, alog_ref, dsk_ref, nw_ref, smat_ref,
             y_ref, ypre_ref, prev_ref, state, ybuf, mbuf):
        c = pl.program_id(0)

        @pl.when(c == 0)
        def _():
            state[...] = jnp.zeros_like(state)

        u, sig, xbcv, dtraw, dt, A, acs, acs_row, _ = _ssd_chunk_pre2(
            c == 0, xbc_ref, tail_ref, dt_ref, cw_ref, cb_ref, dtb_ref, alog_ref, smat_ref)
        e = _head_expander()
        dtE, eacsE, dsdE, ealE, dskE = _ssd_decays(dt, acs, dsk_ref, e)
        xs = xbcv[:, 0:D_SSD]
        X = xs * dtE
        prev_ref[0] = state[...]
        causal = _iota2((CHUNK, CHUNK), 0) >= _iota2((CHUNK, CHUNK), 1)
        for g in range(SSD_GROUPS):
            gs = slice(half * g, half * (g + 1))
            Bg = _bf(xbcv[:, D_SSD + SSD_N * g:D_SSD + SSD_N * (g + 1)])
            Cg = _bf(xbcv[:, D_SSD + D_BC + SSD_N * g:D_SSD + D_BC + SSD_N * (g + 1)])
            cb = _dot_nt(Cg, Bg)
            for r in range(SSD_R):
                h = g * SSD_R + r
                seg = acs[:, h:h + 1] - acs_row[h:h + 1, :]
                mbuf[h] = _bf(cb * jnp.where(causal, jnp.exp(jnp.where(causal, seg, 0.0)), 0.0))
            st = state[:, gs]
            ybuf[:, gs] = _dot(Cg, _bf(st)) * eacsE[:, gs] + dskE[:, gs] * xs[:, gs]
            state[:, gs] = st * ealE[:, gs] + _dot_tn(Bg, _bf(X[:, gs] * dsdE[:, gs]))
        Xb = _bf(X)
        for h in range(SSD_HEADS):
            hs = slice(SSD_P * h, SSD_P * (h + 1))
            ybuf[:, hs] += _dot(mbuf[h], Xb[:, hs])
        y = ybuf[...]
        ypre_ref[...] = y
        zv = z_ref[...]
        yf = y * (zv * _sigmoid(zv))
        for g in range(SSD_GROUPS):
            gs = slice(half * g, half * (g + 1))
            yg = yf[:, gs]
            ms = jnp.mean(yg * yg, axis=-1, keepdims=True)
            y_ref[:, gs] = _bf(yg * lax.rsqrt(ms + RMS_EPS) * nw_ref[:, gs])

    full = lambda shape: pl.BlockSpec(shape, lambda c: (0, 0))
    return pl.pallas_call(
        body, name="ssd_fwd", grid=(nc,),
        in_specs=[
            pl.BlockSpec((CHUNK, D_SSD), lambda c: (c, 0)),
            pl.BlockSpec((CHUNK, D_XBC), lambda c: (c, 0)),
            pl.BlockSpec((HALO, D_XBC), lambda c: (jnp.maximum(c * (CHUNK // HALO) - 1, 0), 0)),
            pl.BlockSpec((CHUNK, DT_PAD), lambda c: (c, 0)),
            full((CONV_K, D_XBC)), full((1, D_XBC)), full((1, SSD_HEADS)), full((1, SSD_HEADS)), full((1, SSD_HEADS)),
            full((1, D_SSD)), full((3 * CHUNK, 2 * (CHUNK + HALO))),
        ],
        out_specs=[
            pl.BlockSpec((CHUNK, D_SSD), lambda c: (c, 0)),
            pl.BlockSpec((CHUNK, D_SSD), lambda c: (c, 0)),
            pl.BlockSpec((1, SSD_N, D_SSD), lambda c: (c, 0, 0)),
        ],
        out_shape=[
            jax.ShapeDtypeStruct((L, D_SSD), BF16),
            jax.ShapeDtypeStruct((L, D_SSD), F32),
            jax.ShapeDtypeStruct((nc, SSD_N, D_SSD), F32),
        ],
        scratch_shapes=[
            pltpu.VMEM((SSD_N, D_SSD), F32),
            pltpu.VMEM((CHUNK, D_SSD), F32),
            pltpu.VMEM((SSD_HEADS, CHUNK, CHUNK), BF16),
        ],
        compiler_params=_cparams(("arbitrary",)),
    )(z, xbc, xbc, dtp, conv_w, conv_b, dt_bias, a_log, d_skip, norm_w, _shift_matrix((13, 14, 15)))


def _ssd_bwd2(dy, z, ypre, xbc, dtp, prev, conv_w, conv_b, dt_bias, a_log, d_skip, norm_w):
    L = z.shape[0]
    nc = L // CHUNK
    half = D_SSD // SSD_GROUPS

    def body(dy_ref, z_ref, ypre_ref, xbc_ref, tail_ref, dt_ref, prev_ref, cw_ref, cb_ref, dtb_ref, alog_ref, dsk_ref,
             nw_ref, smat_ref, smat2_ref, dz_ref, dxbc_ref, ddt_ref, gcw_ref, gcb_ref, gdtb_ref, galog_ref, gdsk_ref,
             gnw_ref, dstate, dhead, dpost, yobuf, bdbuf, lmbuf, dmbuf, cbbuf):
        i = pl.program_id(0)
        c = nc - 1 - i

        @pl.when(i == 0)
        def _():
            dstate[...] = jnp.zeros_like(dstate)
            dhead[...] = jnp.zeros_like(dhead)
            gcw_ref[...] = jnp.zeros_like(gcw_ref)
            gcb_ref[...] = jnp.zeros_like(gcb_ref)
            gdtb_ref[...] = jnp.zeros_like(gdtb_ref)
            galog_ref[...] = jnp.zeros_like(galog_ref)
            gdsk_ref[...] = jnp.zeros_like(gdsk_ref)
            gnw_ref[...] = jnp.zeros_like(gnw_ref)

        u, sig, xbcv, dtraw, dt, A, acs, acs_row, taps = _ssd_chunk_pre2(
            c == 0, xbc_ref, tail_ref, dt_ref, cw_ref, cb_ref, dtb_ref, alog_ref, smat_ref)
        e = _head_expander()
        dtE, eacsE, dsdE, ealE, dskE = _ssd_decays(dt, acs, dsk_ref, e)
        alast = acs[CHUNK - 1:CHUNK, :]
        xs = xbcv[:, 0:D_SSD]
        X = xs * dtE
        Xb = _bf(X)

        zv = z_ref[...]
        ypre = ypre_ref[...]
        dyn = dy_ref[...]
        sz = _sigmoid(zv)
        silu_z = zv * sz
        yf = ypre * silu_z
        dyf_parts = []
        for g in range(SSD_GROUPS):
            gs = slice(half * g, half * (g + 1))
            yg = yf[:, gs]
            rstd = lax.rsqrt(jnp.mean(yg * yg, axis=-1, keepdims=True) + RMS_EPS)
            dout = dyn[:, gs]
            gnw_ref[:, gs] += jnp.sum(dout * yg * rstd, axis=0, keepdims=True)
            dyhat = dout * nw_ref[:, gs]
            dyf_parts.append(rstd * (dyhat - yg * (rstd * rstd) * jnp.mean(dyhat * yg, axis=-1, keepdims=True)))
        dyf = jnp.concatenate(dyf_parts, axis=1)
        dz_ref[...] = _bf(dyf * ypre * (sz * (1.0 + zv * (1.0 - sz))))
        dyp = dyf * silu_z
        dyb = _bf(dyp)
        G = dyp * eacsE

        causal = _iota2((CHUNK, CHUNK), 0) >= _iota2((CHUNK, CHUNK), 1)
        ST = prev_ref[0]
        dST = dstate[...]
        for g in range(SSD_GROUPS):
            gs = slice(half * g, half * (g + 1))
            bs = slice(D_SSD + SSD_N * g, D_SSD + SSD_N * (g + 1))
            cs = slice(D_SSD + D_BC + SSD_N * g, D_SSD + D_BC + SSD_N * (g + 1))
            Bg = _bf(xbcv[:, bs])
            Cg = _bf(xbcv[:, cs])
            Gb = _bf(G[:, gs])
            STb = _bf(ST[:, gs])
            dSTb = _bf(dST[:, gs])
            dstate[:, gs] = dST[:, gs] * ealE[:, gs] + _dot_tn(Cg, Gb)
            yobuf[:, gs] = _dot(Cg, STb) * eacsE[:, gs]
            bdbuf[:, gs] = _dot(Bg, dSTb)
            dpost[:, cs] = _dot_nt(Gb, STb)
            dpost[:, bs] = _dot_nt(_bf(X[:, gs] * dsdE[:, gs]), dSTb)
            cbbuf[g] = _dot_nt(Cg, Bg)
            for r in range(SSD_R):
                h = g * SSD_R + r
                seg = acs[:, h:h + 1] - acs_row[h:h + 1, :]
                lmbuf[h] = jnp.where(causal, jnp.exp(jnp.where(causal, seg, 0.0)), 0.0)
        for h in range(SSD_HEADS):
            hs = slice(SSD_P * h, SSD_P * (h + 1))
            Mb = _bf(cbbuf[h // SSD_R] * lmbuf[h])
            dmbuf[h] = _dot_nt(dyb[:, hs], Xb[:, hs])
            dpost[:, hs] = _dot_tn(Mb, dyb[:, hs])
        lane16 = _iota2((1, SSD_HEADS), 1)
        sub16 = _iota2((SSD_HEADS, 1), 0)
        dacs_col = jnp.zeros((CHUNK, SSD_HEADS), F32)
        dacs_row = jnp.zeros((SSD_HEADS, CHUNK), F32)
        for g in range(SSD_GROUPS):
            bs = slice(D_SSD + SSD_N * g, D_SSD + SSD_N * (g + 1))
            cs = slice(D_SSD + D_BC + SSD_N * g, D_SSD + D_BC + SSD_N * (g + 1))
            cb = cbbuf[g]
            dcb = jnp.zeros((CHUNK, CHUNK), F32)
            for r in range(SSD_R):
                h = g * SSD_R + r
                dM = dmbuf[h]
                Lm = lmbuf[h]
                dcb = dcb + dM * Lm
                dseg = dM * (cb * Lm)
                dacs_col = dacs_col + jnp.sum(dseg, axis=-1, keepdims=True) * (lane16 == h).astype(F32)
                dacs_row = dacs_row - jnp.sum(dseg, axis=0, keepdims=True) * (sub16 == h).astype(F32)
            dcbb = _bf(dcb)
            dpost[:, bs] += _dot_tn(dcbb, _bf(xbcv[:, cs]))
            dpost[:, cs] += _dot(dcbb, _bf(xbcv[:, bs]))

        BD = bdbuf[...]
        dX = dpost[:, 0:D_SSD] + dsdE * BD
        dsd = jnp.exp(alast - acs)
        T = _headsum(X * BD, e) * dsd
        dalast = jnp.sum(T, axis=0, keepdims=True) + _headsum(
            jnp.sum(dST * ST, axis=0, keepdims=True), e) * jnp.exp(alast)
        is_last = (_iota2((CHUNK, 1), 0) == CHUNK - 1).astype(F32)
        dacs = dacs_col + _to_cols(dacs_row) + _headsum(dyp * yobuf[...], e) - T + is_last * dalast
        triu = (_iota2((CHUNK, CHUNK), 0) <= _iota2((CHUNK, CHUNK), 1)).astype(F32)
        da = jnp.dot(triu, dacs, preferred_element_type=F32, precision=HI)
        ddt_tot = _headsum(dX * xs, e) + da * A
        galog_ref[...] += jnp.sum(da * dt, axis=0, keepdims=True) * A
        ddtraw = ddt_tot * _sigmoid(dtraw)
        gdtb_ref[...] += jnp.sum(ddtraw, axis=0, keepdims=True)
        gdsk_ref[...] += _headsum(jnp.sum(dyp * xs, axis=0, keepdims=True), e)
        ddt_ref[...] = jnp.zeros_like(ddt_ref)
        ddt_ref[:, 0:SSD_HEADS] = ddtraw
        dpost[:, 0:D_SSD] = dX * dtE + dskE * dyp

        dconv = dpost[...] * (sig * (1.0 + u * (1.0 - sig)))
        gcb_ref[...] += jnp.sum(dconv, axis=0, keepdims=True)
        for k in range(CONV_K):
            gcw_ref[k:k + 1, :] += jnp.sum(dconv * taps[k], axis=0, keepdims=True)
        later = _shifted_rows(dconv, dhead[...], smat2_ref)
        dx = cw_ref[CONV_K - 1:CONV_K, :] * dconv
        for k in range(CONV_K - 1):
            dx = dx + cw_ref[k:k + 1, :] * later[k]
        dxbc_ref[...] = _bf(dx)
        dhead[...] = dconv[0:HALO, :]

    full = lambda shape: pl.BlockSpec(shape, lambda i: (0, 0))
    rev = lambda wd: pl.BlockSpec((CHUNK, wd), lambda i: (nc - 1 - i, 0))
    return pl.pallas_call(
        body, name="ssd_bwd", grid=(nc,),
        in_specs=[
            rev(D_SSD), rev(D_SSD), rev(D_SSD), rev(D_XBC),
            pl.BlockSpec((HALO, D_XBC), lambda i: (jnp.maximum((nc - 1 - i) * (CHUNK // HALO) - 1, 0), 0)),
            rev(DT_PAD),
            pl.BlockSpec((1, SSD_N, D_SSD), lambda i: (nc - 1 - i, 0, 0)),
            full((CONV_K, D_XBC)), full((1, D_XBC)), full((1, SSD_HEADS)), full((1, SSD_HEADS)), full((1, SSD_HEADS)),
            full((1, D_SSD)), full((3 * CHUNK, 2 * (CHUNK + HALO))), full((3 * CHUNK, 2 * (CHUNK + HALO))),
        ],
        out_specs=[
            rev(D_SSD), rev(D_XBC), rev(DT_PAD),
            full((CONV_K, D_XBC)), full((1, D_XBC)), full((1, SSD_HEADS)), full((1, SSD_HEADS)), full((1, SSD_HEADS)),
            full((1, D_SSD)),
        ],
        out_shape=[
            jax.ShapeDtypeStruct((L, D_SSD), BF16), jax.ShapeDtypeStruct((L, D_XBC), BF16),
            jax.ShapeDtypeStruct((L, DT_PAD), F32),
            jax.ShapeDtypeStruct((CONV_K, D_XBC), F32), jax.ShapeDtypeStruct((1, D_XBC), F32),
            jax.ShapeDtypeStruct((1, SSD_HEADS), F32), jax.ShapeDtypeStruct((1, SSD_HEADS), F32),
            jax.ShapeDtypeStruct((1, SSD_HEADS), F32), jax.ShapeDtypeStruct((1, D_SSD), F32),
        ],
        scratch_shapes=[
            pltpu.VMEM((SSD_N, D_SSD), F32),
            pltpu.VMEM((HALO, D_XBC), F32),
            pltpu.VMEM((CHUNK, D_XBC), F32),
            pltpu.VMEM((CHUNK, D_SSD), F32),
            pltpu.VMEM((CHUNK, D_SSD), F32),
            pltpu.VMEM((SSD_HEADS, CHUNK, CHUNK), F32),
            pltpu.VMEM((SSD_HEADS, CHUNK, CHUNK), F32),
            pltpu.VMEM((SSD_GROUPS, CHUNK, CHUNK), F32),
        ],
        compiler_params=_cparams(("arbitrary",)),
    )(dy, z, ypre, xbc, xbc, dtp, prev, conv_w, conv_b, dt_bias, a_log, d_skip, norm_w, _shift_matrix((13, 14, 15)),
      _shift_matrix((3, 2, 1)))


def _rope_tables(pos_ref, inv_ref):
    ang = pos_ref[...].astype(F32) * inv_ref[...]
    d = _iota2((1, 2 * ATT_HD), 1) % ATT_HD
    s = jnp.sin(ang)
    return jnp.cos(ang), jnp.where(d < ROPE_DIM // 2, -s, 0.0), jnp.where((d >= ROPE_DIM // 2) & (d < ROPE_DIM), s, 0.0)


def _rope(t, tabs):
    c, s1, s2 = tabs
    n = t.shape[1]
    rep = n // c.shape[1]
    return (t * jnp.tile(c, (1, rep)) + pltpu.roll(t, n - ROPE_DIM // 2, 1) * jnp.tile(s1, (1, rep))
            + pltpu.roll(t, ROPE_DIM // 2, 1) * jnp.tile(s2, (1, rep)))


def _rope_t(t, tabs):
    c, s1, s2 = tabs
    n = t.shape[1]
    rep = n // c.shape[1]
    return (t * jnp.tile(c, (1, rep)) + pltpu.roll(t * jnp.tile(s1, (1, rep)), ROPE_DIM // 2, 1)
            + pltpu.roll(t * jnp.tile(s2, (1, rep)), n - ROPE_DIM // 2, 1))


def _swa_mask(first):
    qi = _iota2((WINDOW, 2 * WINDOW), 0)
    si = _iota2((WINDOW, 2 * WINDOW), 1)
    band = (si > qi) & (si <= qi + WINDOW)
    return band & (jnp.logical_not(first) | (si >= WINDOW))


def _stack_heads(t, j):
    return jnp.concatenate([t[:, ATT_HD * (j * ATT_R + r):ATT_HD * (j * ATT_R + r + 1)] for r in range(ATT_R)], axis=0)


def _stack_cols(ref, j):
    cols = [jnp.broadcast_to(ref[:, j * ATT_R + r:j * ATT_R + r + 1], (WINDOW, 1)) for r in range(ATT_R)]
    return jnp.concatenate(cols, axis=0)


def _swa_mask_t(first):
    si = _iota2((2 * WINDOW, ATT_R * WINDOW), 0)
    qi = _iota2((2 * WINDOW, ATT_R * WINDOW), 1) % WINDOW
    band = (si > qi) & (si <= qi + WINDOW)
    return band & (jnp.logical_not(first) | (si >= WINDOW))


def _head_rows(ref, j, rows=None):
    if ref.shape[0] == 1:
        parts = [jnp.broadcast_to(ref[:, j * ATT_R + r:j * ATT_R + r + 1], (1, WINDOW)) for r in range(ATT_R)]
    else:
        parts = [ref[j * ATT_R + r:j * ATT_R + r + 1, :] for r in range(ATT_R)]
    return jnp.concatenate(parts, axis=1)


def _swa_fwd(q, g, kv, sinks):
    L = q.shape[0]
    nb = L // WINDOW
    scale = ATT_HD ** -0.5

    def body(q_ref, g_ref, kvc_ref, kvp_ref, sink_ref, y_ref, o_ref, lse_ref, otbuf):
        n = pl.program_id(0)
        kk = jnp.concatenate([kvp_ref[:, 0:D_KV], kvc_ref[:, 0:D_KV]], axis=0)
        vv = jnp.concatenate([kvp_ref[:, D_KV:2 * D_KV], kvc_ref[:, D_KV:2 * D_KV]], axis=0)
        valid = _swa_mask_t(n == 0)
        qv = q_ref[...]
        for j in range(ATT_KVH):
            js = slice(ATT_HD * j, ATT_HD * (j + 1))
            st = _dot_nt(kk[:, js], _stack_heads(qv, j)) * scale
            st = jnp.where(valid, st, NEG_BIG)
            sink = _head_rows(sink_ref, j)
            m = jnp.maximum(jnp.max(st, axis=0, keepdims=True), sink)
            p = jnp.exp(st - m)
            denom = jnp.sum(p, axis=0, keepdims=True) + jnp.exp(sink - m)
            ot = _dot_tn(vv[:, js], _bf(p)) * (1.0 / denom)
            lse = m + jnp.log(denom)
            for r in range(ATT_R):
                h = j * ATT_R + r
                otbuf[ATT_HD * h:ATT_HD * (h + 1), :] = ot[:, WINDOW * r:WINDOW * (r + 1)]
                lse_ref[h:h + 1, :] = lse[:, WINDOW * r:WINDOW * (r + 1)]
        o = otbuf[...].T
        o_ref[...] = o
        gv = g_ref[...]
        y_ref[...] = _bf(o * (gv * _sigmoid(gv)))

    cur = lambda wd: pl.BlockSpec((WINDOW, wd), lambda n: (n, 0))
    prv = lambda wd: pl.BlockSpec((WINDOW, wd), lambda n: (jnp.maximum(n - 1, 0), 0))
    return pl.pallas_call(
        body, name="swa_fwd", grid=(nb,),
        in_specs=[cur(D_ATT), cur(D_ATT), cur(2 * D_KV), prv(2 * D_KV), pl.BlockSpec((1, ATT_QH), lambda n: (0, 0))],
        out_specs=[cur(D_ATT), cur(D_ATT), pl.BlockSpec((ATT_QH, WINDOW), lambda n: (0, n))],
        out_shape=[jax.ShapeDtypeStruct((L, D_ATT), BF16), jax.ShapeDtypeStruct((L, D_ATT), F32),
                   jax.ShapeDtypeStruct((ATT_QH, L), F32)],
        scratch_shapes=[pltpu.VMEM((D_ATT, WINDOW), F32)],
        compiler_params=_cparams(("parallel",)),
    )(q, g, kv, kv, sinks)


def _swa_bwd(dy, q, g, kv, o, lse, pos, inv, sinks):
    L = q.shape[0]
    nb = L // WINDOW
    scale = ATT_HD ** -0.5

    def body(dy_ref, q_ref, g_ref, kvc_ref, kvp_ref, o_ref, lse_ref, posc_ref, posp_ref, inv_ref, sink_ref,
             dq_ref, dg_ref, dkv_ref, dsink_ref, carry, dqbuf, dkbuf, dvbuf):
        n = pl.program_id(0)

        @pl.when(n == 0)
        def _():
            dsink_ref[...] = jnp.zeros_like(dsink_ref)

        @pl.when(n < nb)
        def _():
            tc = _rope_tables(posc_ref, inv_ref)
            tp = _rope_tables(posp_ref, inv_ref)
            kk = jnp.concatenate([kvp_ref[:, 0:D_KV], kvc_ref[:, 0:D_KV]], axis=0)
            vv = jnp.concatenate([kvp_ref[:, D_KV:2 * D_KV], kvc_ref[:, D_KV:2 * D_KV]], axis=0)
            valid = _swa_mask_t(n == 0)
            qv = q_ref[...]
            gv = g_ref[...]
            sg = _sigmoid(gv)
            dyv = dy_ref[...]
            ov = o_ref[...]
            dg_ref[...] = _bf(dyv * ov * (sg * (1.0 + gv * (1.0 - sg))))
            do = dyv * (gv * sg)
            dod = do * ov
            ones = jnp.ones((8, ATT_HD), BF16)
            lane16 = _iota2((1, ATT_QH), 1)
            dsink = jnp.zeros((1, ATT_QH), F32)
            for j in range(ATT_KVH):
                js = slice(ATT_HD * j, ATT_HD * (j + 1))
                kj = kk[:, js]
                vj = vv[:, js]
                qs = _stack_heads(qv, j)
                dos = _bf(_stack_heads(do, j))
                hi, lo = _hi_lo(_stack_heads(dod, j))
                delta = (_dot_nt(ones, hi) + _dot_nt(ones, lo))[0:1]
                lse = _head_rows(lse_ref, j)
                st = _dot_nt(kj, qs) * scale
                pt = jnp.exp(jnp.where(valid, st, NEG_BIG) - lse)
                dst = _bf(pt * (_dot_nt(vj, dos) - delta))
                dqt = _dot_tn(kj, dst) * scale
                dkbuf[:, js] = _dot(dst, qs) * scale
                dvbuf[:, js] = _dot(_bf(pt), dos)
                sd = jnp.exp(_head_rows(sink_ref, j) - lse) * delta
                for r in range(ATT_R):
                    h = j * ATT_R + r
                    ls = slice(WINDOW * r, WINDOW * (r + 1))
                    dqbuf[ATT_HD * h:ATT_HD * (h + 1), :] = dqt[:, ls]
                    dsink = dsink - jnp.sum(sd[:, ls], axis=1, keepdims=True) * (lane16 == h).astype(F32)
            dsink_ref[...] += dsink
            dq_ref[...] = _bf(_rope_t(dqbuf[...].T, tc))
            dkp = _rope_t(dkbuf[0:WINDOW, :], tp)
            dkc = _rope_t(dkbuf[WINDOW:2 * WINDOW, :], tc)

            @pl.when(n > 0)
            def _():
                dkv_ref[:, 0:D_KV] = _bf(carry[:, 0:D_KV] + dkp)
                dkv_ref[:, D_KV:2 * D_KV] = _bf(carry[:, D_KV:2 * D_KV] + dvbuf[0:WINDOW, :])

            carry[:, 0:D_KV] = dkc
            carry[:, D_KV:2 * D_KV] = dvbuf[WINDOW:2 * WINDOW, :]

        @pl.when(n == nb)
        def _():
            dkv_ref[...] = _bf(carry[...])

    last = nb - 1
    cur = lambda wd: pl.BlockSpec((WINDOW, wd), lambda n: (jnp.minimum(n, last), 0))
    prv = lambda wd: pl.BlockSpec((WINDOW, wd), lambda n: (jnp.maximum(jnp.minimum(n, last) - 1, 0), 0))
    return pl.pallas_call(
        body, name="swa_bwd", grid=(nb + 1,),
        in_specs=[cur(D_ATT), cur(D_ATT), cur(D_ATT), cur(2 * D_KV), prv(2 * D_KV), cur(D_ATT),
                  pl.BlockSpec((ATT_QH, WINDOW), lambda n: (0, jnp.minimum(n, last))), cur(1), prv(1),
                  pl.BlockSpec((1, 2 * ATT_HD), lambda n: (0, 0)), pl.BlockSpec((1, ATT_QH), lambda n: (0, 0))],
        out_specs=[cur(D_ATT), cur(D_ATT),
                   pl.BlockSpec((WINDOW, 2 * D_KV), lambda n: (jnp.maximum(n - 1, 0), 0)),
                   pl.BlockSpec((1, ATT_QH), lambda n: (0, 0))],
        out_shape=[jax.ShapeDtypeStruct((L, D_ATT), BF16), jax.ShapeDtypeStruct((L, D_ATT), BF16),
                   jax.ShapeDtypeStruct((L, 2 * D_KV), BF16), jax.ShapeDtypeStruct((1, ATT_QH), F32)],
        scratch_shapes=[pltpu.VMEM((WINDOW, 2 * D_KV), F32), pltpu.VMEM((D_ATT, WINDOW), F32),
                        pltpu.VMEM((2 * WINDOW, D_KV), F32), pltpu.VMEM((2 * WINDOW, D_KV), F32)],
        compiler_params=_cparams(("arbitrary",)),
    )(dy, q, g, kv, kv, o, lse, pos, pos, inv, sinks)


def _out_ln_loss(y_ssd, y_att, x, target, w_out, ln_g, ln_b):
    L = x.shape[0]
    tm = ROW_TILE
    inv_d = 1.0 / D_MODEL

    def body(ys_ref, ya_ref, x_ref, t_ref, w_ref, g_ref, b_ref, dr_ref, dys_ref, dya_ref, loss_ref, gg_ref, gb_ref):
        i = pl.program_id(0)

        @pl.when(i == 0)
        def _():
            loss_ref[...] = jnp.zeros_like(loss_ref)
            gg_ref[...] = jnp.zeros_like(gg_ref)
            gb_ref[...] = jnp.zeros_like(gb_ref)

        h = _dot(_bf(ys_ref[...]), w_ref[0:D_SSD, :]) + _dot(_bf(ya_ref[...]), w_ref[D_SSD:D_MIX, :])
        r = ALPHA * x_ref[...] + h
        mu = jnp.mean(r, axis=-1, keepdims=True)
        xc = r - mu
        rstd = lax.rsqrt(jnp.mean(xc * xc, axis=-1, keepdims=True) + LN_EPS)
        xhat = xc * rstd
        gam = g_ref[...]
        diff = xhat * gam + b_ref[...] - t_ref[...]
        part = jnp.sum(jnp.sum(diff * diff, axis=-1, keepdims=True), axis=0, keepdims=True)
        loss_ref[...] += (0.5 * inv_d) * part
        dout = diff * inv_d
        gg_ref[...] += jnp.sum(dout * xhat, axis=0, keepdims=True)
        gb_ref[...] += jnp.sum(dout, axis=0, keepdims=True)
        dxh = dout * gam
        dr = rstd * (dxh - jnp.mean(dxh, axis=-1, keepdims=True) - xhat * jnp.mean(dxh * xhat, axis=-1, keepdims=True))
        dr_ref[...] = dr
        drb = _bf(dr)
        dys_ref[...] = _dot_nt(drb, w_ref[0:D_SSD, :])
        dya_ref[...] = _dot_nt(drb, w_ref[D_SSD:D_MIX, :])

    row = pl.BlockSpec((tm, D_MODEL), lambda i: (i, 0))
    vec = pl.BlockSpec((1, D_MODEL), lambda i: (0, 0))
    return pl.pallas_call(
        body, name="out_ln_loss", grid=(L // tm,),
        in_specs=[row, row, row, row, pl.BlockSpec((D_MIX, D_MODEL), lambda i: (0, 0), pipeline_mode=pl.Buffered(1)), vec, vec],
        out_specs=[row, row, row, pl.BlockSpec((1, 128), lambda i: (0, 0)), vec, vec],
        out_shape=[jax.ShapeDtypeStruct((L, D_MODEL), F32)] * 3 + [jax.ShapeDtypeStruct((1, 128), F32)]
        + [jax.ShapeDtypeStruct((1, D_MODEL), F32)] * 2,
        compiler_params=_cparams(("arbitrary",)),
    )(y_ssd, y_att, x, target, w_out, ln_g, ln_b)


def _local_step(x, pos, target, w, get_w_out, token, conv_w, conv_b, dt_bias, a_log, d_skip, norm_w, sinks, ln_g, ln_b):
    inv8 = ROPE_THETA ** (-jnp.arange(0, ROPE_DIM, 2, dtype=F32) / ROPE_DIM)
    inv = jnp.tile(jnp.concatenate([inv8, inv8, jnp.zeros((ATT_HD - ROPE_DIM,), F32)]), 2).reshape(1, 2 * ATT_HD)
    inv = inv + token

    z, g, q, xbc, kv, dtp, xb = _in_proj(x, w, pos, inv)
    y_ssd, y_pre, prev = _ssd_fwd2(z, xbc, dtp, conv_w, conv_b, dt_bias, a_log, d_skip, norm_w)
    y_att, o, lse = _swa_fwd(q, g, kv, sinks)
    w_out = get_w_out(lse)
    dr, dy_ssd, dy_att, loss, g_ln_g, g_ln_b = _out_ln_loss(y_ssd, y_att, x, target, w_out, ln_g, ln_b)
    gw_out_ssd, gw_out_att = _matmuls_tn([y_ssd, y_att], dr, "gw_out", out_dtype=BF16)
    slabs = jnp.concatenate([gw_out_ssd, gw_out_att], axis=0).reshape(N_CHIPS, W_OUT_ROWS, D_MODEL)
    w_out_red = _reduce_w_out_start(slabs, loss)
    inv = inv + w_out_red[16][0:1, :]
    dq, dg, dkv, g_sinks = _swa_bwd(dy_att, q, g, kv, o, lse, pos, inv, sinks)
    dz, dxbc, ddt, g_conv_w, g_conv_b, g_dt_bias, g_a_log, g_d_skip, g_norm_w = _ssd_bwd2(
        dy_ssd, z, y_pre, xbc, dtp, prev, conv_w, conv_b, dt_bias, a_log, d_skip, norm_w)
    gw_z, gw_g, gw_q = _matmuls_tn([dz, dg, dq], xb, "gw_zgq")
    gw_xbc, gw_kv, gw_dt = _matmuls_tn([dxbc, dkv, ddt], xb, "gw_xbc_kv_dt")
    gw_in = jnp.concatenate([gw_z, gw_xbc, gw_dt[0:SSD_HEADS], gw_q, gw_kv, gw_g], axis=0)
    small = dict(conv_w=g_conv_w, conv_b=g_conv_b, dt_bias=g_dt_bias, a_log=g_a_log, d_skip=g_d_skip,
                 ssd_norm_w=g_norm_w, attn_sinks=g_sinks, ln_g=g_ln_g, ln_b=g_ln_b)
    return loss, (dr, dz, dg, dq, dxbc, dkv, ddt, w), gw_in, w_out_red, small


def _mesh_pos():
    return lax.axis_index("x"), lax.axis_index("y"), lax.axis_index("c")


def _gather_weights(w_in_s, conv_w_s):
    def body(win_ref, cw_ref, owin_ref, ocw_ref, send_sems, recv_sems, small_send, small_recv, local_sems):
        x, y, c = _mesh_pos()
        me = 2 * x + y
        sibling = (x, y, 1 - c)
        chips = [(1 - x, y), (x, 1 - y), (1 - x, 1 - y)]
        locals_ = [pltpu.make_async_copy(cw_ref, ocw_ref.at[me], local_sems.at[0])]
        for cp in locals_:
            cp.start()
        started = []
        for t, (src, dst) in enumerate(((win_ref, owin_ref),)):
            hr = src.shape[0] // 2

            def half(ref, hc, hr=hr):
                return ref.at[pl.ds(hc * hr, hr), :]

            for j, (px, py) in enumerate(chips):
                cp = pltpu.make_async_remote_copy(
                    src_ref=half(src, c), dst_ref=half(dst.at[me], c), send_sem=send_sems.at[t, j],
                    recv_sem=recv_sems.at[t, j], device_id=(px, py, c), device_id_type=MESH)
                cp.start()
                started.append(cp)
        for j, (px, py) in enumerate(chips):
            cp = pltpu.make_async_remote_copy(
                src_ref=cw_ref, dst_ref=ocw_ref.at[me], send_sem=small_send.at[j], recv_sem=small_recv.at[j],
                device_id=(px, py, c), device_id_type=MESH)
            cp.start()
            started.append(cp)
        for t, (src, dst) in enumerate(((win_ref, owin_ref),)):
            hr = src.shape[0] // 2
            for j, (px, py) in enumerate(chips):
                src_chip = 2 * px + py
                blk = dst.at[src_chip].at[pl.ds(c * hr, hr), :]
                pltpu.make_async_remote_copy(
                    src_ref=blk, dst_ref=blk, send_sem=send_sems.at[t, j], recv_sem=recv_sems.at[t, j],
                    device_id=(px, py, c), device_id_type=MESH).wait_recv()
                cp = pltpu.make_async_remote_copy(
                    src_ref=blk, dst_ref=blk, send_sem=send_sems.at[t, 3 + j], recv_sem=recv_sems.at[t, 3 + j],
                    device_id=sibling, device_id_type=MESH)
                cp.start()
                started.append(cp)
        for t, (src, dst) in enumerate(((win_ref, owin_ref),)):
            hr = src.shape[0] // 2
            for j, (px, py) in enumerate(chips):
                src_chip = 2 * px + py
                blk = dst.at[src_chip].at[pl.ds((1 - c) * hr, hr), :]
                pltpu.make_async_remote_copy(
                    src_ref=blk, dst_ref=blk, send_sem=send_sems.at[t, 3 + j], recv_sem=recv_sems.at[t, 3 + j],
                    device_id=sibling, device_id_type=MESH).wait_recv()
        for j in range(3):
            pltpu.make_async_remote_copy(
                src_ref=cw_ref, dst_ref=ocw_ref.at[me], send_sem=small_send.at[j], recv_sem=small_recv.at[j],
                device_id=sibling, device_id_type=MESH).wait_recv()
        for cp in started:
            cp.wait_send()
        for cp in locals_:
            cp.wait()

    any_spec = pl.BlockSpec(memory_space=pl.ANY)
    return pl.pallas_call(
        body, name="gather_weights",
        in_specs=[any_spec] * 2, out_specs=[any_spec] * 2,
        out_shape=[jax.ShapeDtypeStruct((N_CHIPS,) + a.shape, a.dtype) for a in (w_in_s, conv_w_s)],
        scratch_shapes=[pltpu.SemaphoreType.DMA((1, 6)), pltpu.SemaphoreType.DMA((1, 6)),
                        pltpu.SemaphoreType.DMA((3,)), pltpu.SemaphoreType.DMA((3,)), pltpu.SemaphoreType.DMA((3,))],
    )(w_in_s, conv_w_s)


_HBM = pl.BlockSpec(memory_space=pltpu.HBM)
_SEM = pl.BlockSpec(memory_space=pltpu.SEMAPHORE)
_EFFECT = pltpu.SideEffectType.DATAFLOW_SIDE_EFFECTING


def _gather_w_out_start(w_out_s, after):
    def body(src_ref, land_ref, after_ref, s0, s1, s2, r0, r1, r2, src_thru, land_thru, token):
        x, y, c = _mesh_pos()
        me = 2 * x + y
        chips = [(1 - x, y), (x, 1 - y), (1 - x, 1 - y)]
        for (px, py), s, r in zip(chips, (s0, s1, s2), (r0, r1, r2)):
            pltpu.make_async_remote_copy(src_ref=src_ref, dst_ref=land_ref.at[me], send_sem=s, recv_sem=r,
                                         device_id=(px, py, c), device_id_type=MESH).start()
        token[...] = jnp.zeros_like(token)

    sem = pltpu.SemaphoreType.DMA(())
    land = lax.empty((N_CHIPS,) + w_out_s.shape, w_out_s.dtype)
    return pl.pallas_call(
        body, name="gather_w_out_start",
        out_shape=(sem,) * 6 + (pltpu.HBM(w_out_s.shape, w_out_s.dtype), pltpu.HBM(land.shape, land.dtype),
                                jax.ShapeDtypeStruct((8, 128), F32)),
        in_specs=(_HBM, _HBM, pl.BlockSpec(memory_space=pl.ANY)),
        out_specs=(_SEM,) * 6 + (_HBM, _HBM, pl.BlockSpec(memory_space=pltpu.VMEM)),
        input_output_aliases={0: 6, 1: 7},
        compiler_params=pltpu.CompilerParams(has_side_effects=_EFFECT),
    )(pltpu.with_memory_space_constraint(w_out_s, pltpu.HBM), pltpu.with_memory_space_constraint(land, pltpu.HBM), after)


def _gather_w_out_wait(sems, src_thru, land_thru, after):
    def body(src_ref, land_ref, s0, s1, s2, r0, r1, r2, after_ref, src_dead, got_ref):
        x, y, c = _mesh_pos()
        chips = [(1 - x, y), (x, 1 - y), (1 - x, 1 - y)]
        for (px, py), s, r in zip(chips, (s0, s1, s2), (r0, r1, r2)):
            cp = pltpu.make_async_remote_copy(src_ref=src_ref, dst_ref=land_ref.at[2 * px + py], send_sem=s, recv_sem=r,
                                              device_id=(px, py, c), device_id_type=MESH)
            cp.wait_send()
            cp.wait_recv()

    return pl.pallas_call(
        body, name="gather_w_out_wait",
        out_shape=(pltpu.HBM(src_thru.shape, src_thru.dtype), pltpu.HBM(land_thru.shape, land_thru.dtype)),
        in_specs=(_HBM, _HBM) + (_SEM,) * 6 + (pl.BlockSpec(memory_space=pl.ANY),),
        out_specs=(_HBM, _HBM), input_output_aliases={0: 0, 1: 1},
        compiler_params=pltpu.CompilerParams(has_side_effects=_EFFECT),
    )(src_thru, land_thru, *sems, after)[1]


def _pair_start(gw_in, after):
    hr = gw_in.shape[1] // 2

    def body(src_ref, land_ref, after_ref, *refs):
        x, y, c = _mesh_pos()
        for j in range(N_CHIPS):
            pltpu.make_async_remote_copy(
                src_ref=src_ref.at[j, pl.ds((1 - c) * hr, hr), :], dst_ref=land_ref.at[j], send_sem=refs[j],
                recv_sem=refs[N_CHIPS + j], device_id=(x, y, 1 - c), device_id_type=MESH).start()
        refs[10][...] = jnp.zeros_like(refs[10])

    sem = pltpu.SemaphoreType.DMA(())
    land = lax.empty((N_CHIPS, hr, D_MODEL), F32)
    return pl.pallas_call(
        body, name="pair_start",
        out_shape=(sem,) * 8 + (pltpu.HBM(gw_in.shape, F32), pltpu.HBM(land.shape, F32), jax.ShapeDtypeStruct((8, 128), F32)),
        in_specs=(_HBM, _HBM, pl.BlockSpec(memory_space=pl.ANY)),
        out_specs=(_SEM,) * 8 + (_HBM, _HBM, pl.BlockSpec(memory_space=pltpu.VMEM)),
        input_output_aliases={0: 8, 1: 9},
        compiler_params=pltpu.CompilerParams(has_side_effects=_EFFECT),
    )(pltpu.with_memory_space_constraint(gw_in, pltpu.HBM), pltpu.with_memory_space_constraint(land, pltpu.HBM), after)


def _pair_wait(sems, gw_thru, land_thru, after):
    hr = land_thru.shape[1]

    def body(src_ref, land_ref, *refs):
        x, y, c = _mesh_pos()
        for j in range(N_CHIPS):
            cp = pltpu.make_async_remote_copy(
                src_ref=src_ref.at[j, pl.ds((1 - c) * hr, hr), :], dst_ref=land_ref.at[j], send_sem=refs[j],
                recv_sem=refs[N_CHIPS + j], device_id=(x, y, 1 - c), device_id_type=MESH)
            cp.wait_send()
            cp.wait_recv()

    return pl.pallas_call(
        body, name="pair_wait",
        out_shape=(pltpu.HBM(gw_thru.shape, F32), pltpu.HBM(land_thru.shape, F32)),
        in_specs=(_HBM, _HBM) + (_SEM,) * 8 + (pl.BlockSpec(memory_space=pl.ANY),),
        out_specs=(_HBM, _HBM), input_output_aliases={0: 0, 1: 1},
        compiler_params=pltpu.CompilerParams(has_side_effects=_EFFECT),
    )(gw_thru, land_thru, *sems, after)


def _chip_start(s_in, after):
    def body(src_ref, land_ref, after_ref, *refs):
        x, y, c = _mesh_pos()
        me = 2 * x + y
        for j, (px, py) in enumerate([(1 - x, y), (x, 1 - y), (1 - x, 1 - y)]):
            pltpu.make_async_remote_copy(
                src_ref=src_ref.at[2 * px + py], dst_ref=land_ref.at[me], send_sem=refs[j], recv_sem=refs[3 + j],
                device_id=(px, py, c), device_id_type=MESH).start()
        refs[8][...] = jnp.zeros_like(refs[8])

    sem = pltpu.SemaphoreType.DMA(())
    land = lax.empty(s_in.shape, s_in.dtype)
    return pl.pallas_call(
        body, name="chip_start",
        out_shape=(sem,) * 6 + (pltpu.HBM(s_in.shape, s_in.dtype), pltpu.HBM(land.shape, land.dtype),
                                jax.ShapeDtypeStruct((8, 128), F32)),
        in_specs=(_HBM, _HBM, pl.BlockSpec(memory_space=pl.ANY)),
        out_specs=(_SEM,) * 6 + (_HBM, _HBM, pl.BlockSpec(memory_space=pltpu.VMEM)),
        input_output_aliases={0: 6, 1: 7},
        compiler_params=pltpu.CompilerParams(has_side_effects=_EFFECT),
    )(pltpu.with_memory_space_constraint(s_in, pltpu.HBM), pltpu.with_memory_space_constraint(land, pltpu.HBM), after)


def _chip_wait(sems, s_thru, land_thru, after):
    def body(src_ref, land_ref, *refs):
        x, y, c = _mesh_pos()
        for j, (px, py) in enumerate([(1 - x, y), (x, 1 - y), (1 - x, 1 - y)]):
            cp = pltpu.make_async_remote_copy(
                src_ref=src_ref.at[2 * px + py], dst_ref=land_ref.at[2 * px + py], send_sem=refs[j], recv_sem=refs[3 + j],
                device_id=(px, py, c), device_id_type=MESH)
            cp.wait_send()
            cp.wait_recv()

    return pl.pallas_call(
        body, name="chip_wait",
        out_shape=(pltpu.HBM(s_thru.shape, s_thru.dtype), pltpu.HBM(land_thru.shape, land_thru.dtype)),
        in_specs=(_HBM, _HBM) + (_SEM,) * 6 + (pl.BlockSpec(memory_space=pl.ANY),),
        out_specs=(_HBM, _HBM), input_output_aliases={0: 0, 1: 1},
        compiler_params=pltpu.CompilerParams(has_side_effects=_EFFECT),
    )(s_thru, land_thru, *sems, after)


def _pair_share(h_in, small):
    def body(hin_ref, sm_ref, rin_ref, slots_ref, send_sems, recv_sems, small_send, small_recv, local_sem):
        x, y, c = _mesh_pos()
        dev = 4 * x + 2 * y + c
        mine = pltpu.make_async_copy(sm_ref, slots_ref.at[dev], local_sem)
        mine.start()
        share = pltpu.make_async_remote_copy(
            src_ref=hin_ref, dst_ref=rin_ref, send_sem=send_sems.at[0], recv_sem=recv_sems.at[0],
            device_id=(x, y, 1 - c), device_id_type=MESH)
        share.start()
        started = []
        for k in range(1, 8):
            peer = (x ^ ((k >> 2) & 1), y ^ ((k >> 1) & 1), c ^ (k & 1))
            cp = pltpu.make_async_remote_copy(
                src_ref=sm_ref, dst_ref=slots_ref.at[dev], send_sem=small_send.at[k - 1], recv_sem=small_recv.at[k - 1],
                device_id=peer, device_id_type=MESH)
            cp.start()
            started.append(cp)
        share.wait()
        for k in range(1, 8):
            pltpu.make_async_remote_copy(
                src_ref=sm_ref, dst_ref=slots_ref.at[dev], send_sem=small_send.at[k - 1], recv_sem=small_recv.at[k - 1],
                device_id=(x, y, 1 - c), device_id_type=MESH).wait_recv()
        for cp in started:
            cp.wait_send()
        mine.wait()

    any_spec = pl.BlockSpec(memory_space=pl.ANY)
    return pl.pallas_call(
        body, name="pair_share",
        in_specs=[any_spec] * 2, out_specs=[any_spec] * 2,
        out_shape=[jax.ShapeDtypeStruct(h_in.shape, F32), jax.ShapeDtypeStruct((8,) + small.shape, F32)],
        scratch_shapes=[pltpu.SemaphoreType.DMA((1,)), pltpu.SemaphoreType.DMA((1,)),
                        pltpu.SemaphoreType.DMA((7,)), pltpu.SemaphoreType.DMA((7,)), pltpu.SemaphoreType.DMA],
    )(h_in, small)


def _reduce_w_out_start(slabs, after):
    def body(src_ref, land_ref, after_ref, *refs):
        x, y, c = _mesh_pos()
        me = 4 * x + 2 * y + c
        for k in range(1, 8):
            px, py, pc = x ^ ((k >> 2) & 1), y ^ ((k >> 1) & 1), c ^ (k & 1)
            pltpu.make_async_remote_copy(src_ref=src_ref.at[2 * px + py], dst_ref=land_ref.at[me], send_sem=refs[k - 1],
                                         recv_sem=refs[6 + k], device_id=(px, py, pc), device_id_type=MESH).start()
        refs[16][...] = jnp.zeros_like(refs[16])

    sem = pltpu.SemaphoreType.DMA(())
    land = lax.empty((8,) + slabs.shape[1:], slabs.dtype)
    return pl.pallas_call(
        body, name="reduce_w_out_start",
        out_shape=(sem,) * 14 + (pltpu.HBM(slabs.shape, slabs.dtype), pltpu.HBM(land.shape, land.dtype),
                                 jax.ShapeDtypeStruct((8, 128), F32)),
        in_specs=(_HBM, _HBM, pl.BlockSpec(memory_space=pl.ANY)),
        out_specs=(_SEM,) * 14 + (_HBM, _HBM, pl.BlockSpec(memory_space=pltpu.VMEM)),
        input_output_aliases={0: 14, 1: 15},
        compiler_params=pltpu.CompilerParams(has_side_effects=_EFFECT),
    )(pltpu.with_memory_space_constraint(slabs, pltpu.HBM), pltpu.with_memory_space_constraint(land, pltpu.HBM), after)


def _reduce_w_out_wait(sems, slabs_thru, land_thru, after):
    def body(src_ref, land_ref, *refs):
        x, y, c = _mesh_pos()
        for k in range(1, 8):
            px, py, pc = x ^ ((k >> 2) & 1), y ^ ((k >> 1) & 1), c ^ (k & 1)
            cp = pltpu.make_async_remote_copy(
                src_ref=src_ref.at[2 * px + py], dst_ref=land_ref.at[4 * px + 2 * py + pc], send_sem=refs[k - 1],
                recv_sem=refs[6 + k], device_id=(px, py, pc), device_id_type=MESH)
            cp.wait_send()
            cp.wait_recv()

    return pl.pallas_call(
        body, name="reduce_w_out_wait",
        out_shape=(pltpu.HBM(slabs_thru.shape, slabs_thru.dtype), pltpu.HBM(land_thru.shape, land_thru.dtype)),
        in_specs=(_HBM, _HBM) + (_SEM,) * 14 + (pl.BlockSpec(memory_space=pl.ANY),),
        out_specs=(_HBM, _HBM), input_output_aliases={0: 0, 1: 1},
        compiler_params=pltpu.CompilerParams(has_side_effects=_EFFECT),
    )(slabs_thru, land_thru, *sems, after)


def _pair_add(g, recv, core, name):
    _, rows, C = recv.shape
    tc = 256

    def body(core_ref, g_ref, r_ref, o_ref):
        o_ref[...] = _bf(g_ref[...] + r_ref[...])

    spec = pl.BlockSpec((1, rows, tc), lambda j, i, core: (j, 0, i))
    return pl.pallas_call(
        body, name=name,
        grid_spec=pltpu.PrefetchScalarGridSpec(
            num_scalar_prefetch=1, grid=(N_CHIPS, C // tc),
            in_specs=[pl.BlockSpec((1, rows, tc), lambda j, i, core: (j, core[0], i)), spec], out_specs=spec),
        out_shape=jax.ShapeDtypeStruct((N_CHIPS, rows, C), BF16),
        compiler_params=_cparams(("parallel", "parallel")),
    )(core, g, recv)


def _chip_add(own, parts, chip, name):
    _, rows, C = parts.shape
    tc = 256

    def body(chip_ref, own_ref, r0, r1, r2, r3, o_ref):
        acc = None
        for j, r in enumerate((r0, r1, r2, r3)):
            term = jnp.where(chip_ref[0] == j, own_ref[0], r[0]).astype(F32)
            acc = term if acc is None else acc + term
        o_ref[...] = acc

    def slab(j):
        return pl.BlockSpec((1, rows, tc), lambda i, chip: (jnp.where(chip[0] == j, (j + 1) % N_CHIPS, j), 0, i))

    return pl.pallas_call(
        body, name=name,
        grid_spec=pltpu.PrefetchScalarGridSpec(
            num_scalar_prefetch=1, grid=(C // tc,),
            in_specs=[pl.BlockSpec((1, rows, tc), lambda i, chip: (chip[0], 0, i))] + [slab(j) for j in range(N_CHIPS)],
            out_specs=pl.BlockSpec((rows, tc), lambda i, chip: (0, i))),
        out_shape=jax.ShapeDtypeStruct((rows, C), F32),
        compiler_params=_cparams(("parallel",)),
    )(chip, own, parts, parts, parts, parts)


def _adamw_math(w, g, m, v):
    m = ADAM_B1 * m + (1.0 - ADAM_B1) * g
    v = ADAM_B2 * v + (1.0 - ADAM_B2) * (g * g)
    m_hat = m / (1.0 - ADAM_B1 ** ADAM_STEP)
    v_hat = v / (1.0 - ADAM_B2 ** ADAM_STEP)
    delta = -ADAM_LR * (m_hat / (jnp.sqrt(v_hat) + ADAM_EPS) + ADAM_WD * w)
    return delta, m, v


def _adamw_pair(w, g_own, g_sib, m, v, core, name):
    unit = w.ndim == 3
    R, C = w.shape[0], w.shape[-1]
    rows = g_own.shape[0]
    tc = 128

    def body(core_ref, w_ref, go_ref, gs_ref, m_ref, v_ref, d_ref, nm_ref, nv_ref, g_ref):
        first = core_ref[0] == 0
        own, sib = go_ref[...], gs_ref[...]
        g = jnp.concatenate([jnp.where(first, own, sib), jnp.where(first, sib, own)], axis=0)[0:R, :]
        idx = (slice(None), 0, slice(None)) if unit else (slice(None), slice(None))
        d, nm, nv = _adamw_math(w_ref[idx], g, m_ref[idx], v_ref[idx])
        d_ref[idx] = d
        nm_ref[idx] = nm
        nv_ref[idx] = nv
        g_ref[idx] = g

    if unit:
        spec = pl.BlockSpec((R, 1, tc), lambda i, core: (0, 0, i))
    else:
        spec = pl.BlockSpec((R, tc), lambda i, core: (0, i))
    gspec = pl.BlockSpec((rows, tc), lambda i, core: (0, i))
    return pl.pallas_call(
        body, name=name,
        grid_spec=pltpu.PrefetchScalarGridSpec(
            num_scalar_prefetch=1, grid=(C // tc,),
            in_specs=[spec, gspec, gspec, spec, spec], out_specs=[spec] * 4),
        out_shape=[jax.ShapeDtypeStruct(w.shape, F32)] * 4,
        compiler_params=_cparams(("parallel",)),
    )(core, w, g_own, g_sib, m, v)


def _adamw_rows(w, g_own, g_sib, m, v, core, name):
    R, C = w.shape[0], w.shape[-1]
    rows = g_own.shape[0]
    step = 256
    chunks = [(r, min(step, R - r)) for r in range(0, R, step)]
    sub = 64

    def body(core_ref, w_hbm, go_hbm, gs_hbm, m_hbm, v_hbm, d_hbm, nm_hbm, nv_hbm, g_hbm,
             wbuf, mbuf, vbuf, gbuf, dbuf, nmbuf, nvbuf, in_sems, g_sems, out_sems):
        c = core_ref[0]
        flat = lambda ref: ref.at[:, 0, :]
        g_in = [pltpu.make_async_copy(go_hbm, gbuf.at[pl.ds(pl.multiple_of(c * rows, 8), rows), :], g_sems.at[0]),
                pltpu.make_async_copy(gs_hbm, gbuf.at[pl.ds(pl.multiple_of((1 - c) * rows, 8), rows), :], g_sems.at[1])]
        for cp in g_in:
            cp.start()
        loads = []
        for k, (r0, n) in enumerate(chunks):
            cps = [pltpu.make_async_copy(flat(src).at[pl.ds(r0, n), :], dst.at[pl.ds(r0, n), :], in_sems.at[a, k])
                   for a, (src, dst) in enumerate(((w_hbm, wbuf), (m_hbm, mbuf), (v_hbm, vbuf)))]
            for cp in cps:
                cp.start()
            loads.append(cps)
        for cp in g_in:
            cp.wait()
        stores = []
        for k, (r0, n) in enumerate(chunks):
            for cp in loads[k]:
                cp.wait()

            def update(rs):
                g = gbuf[rs, :]
                dl, nm, nv = _adamw_math(wbuf[rs, :], g, mbuf[rs, :], vbuf[rs, :])
                dbuf[rs, :] = dl
                nmbuf[rs, :] = nm
                nvbuf[rs, :] = nv

            if n % sub == 0:
                def block(i, carry, r0=r0):
                    update(pl.ds(pl.multiple_of(r0 + i * sub, 8), sub))
                    return carry
                lax.fori_loop(0, n // sub, block, 0)
            else:
                update(pl.ds(r0, n))
            cps = [pltpu.make_async_copy(src.at[pl.ds(r0, n), :], flat(dst).at[pl.ds(r0, n), :], out_sems.at[a, k])
                   for a, (src, dst) in enumerate(((dbuf, d_hbm), (nmbuf, nm_hbm), (nvbuf, nv_hbm), (gbuf, g_hbm)))]
            for cp in cps:
                cp.start()
            stores += cps
        for cp in stores:
            cp.wait()

    any_spec = pl.BlockSpec(memory_space=pl.ANY)
    dense = pltpu.VMEM((R, C), F32)
    return pl.pallas_call(
        body, name=name,
        grid_spec=pltpu.PrefetchScalarGridSpec(
            num_scalar_prefetch=1, grid=(1,),
            in_specs=[any_spec] * 5, out_specs=[any_spec] * 4,
            scratch_shapes=[dense, dense, dense, pltpu.VMEM((2 * rows, C), F32), dense, dense, dense,
                            pltpu.SemaphoreType.DMA((3, len(chunks))), pltpu.SemaphoreType.DMA((2,)),
                            pltpu.SemaphoreType.DMA((4, len(chunks)))]),
        out_shape=[jax.ShapeDtypeStruct(w.shape, F32)] * 4,
        compiler_params=_cparams(),
    )(core, w, g_own, g_sib, m, v)


def _adamw_sum8(w, slabs, land, m, v, ids, name):
    R, C = w.shape
    tc = 128

    def body(ids_ref, w_ref, own_ref, *refs):
        lrefs, (m_ref, v_ref, d_ref, nm_ref, nv_ref, g_ref) = refs[:8], refs[8:]
        g = None
        for d, l_ref in enumerate(lrefs):
            term = jnp.where(ids_ref[0] == d, own_ref[0], l_ref[0]).astype(F32)
            g = term if g is None else g + term
        dl, nm, nv = _adamw_math(w_ref[...], g, m_ref[...], v_ref[...])
        d_ref[...] = dl
        nm_ref[...] = nm
        nv_ref[...] = nv
        g_ref[...] = g

    def slot(d):
        return pl.BlockSpec((1, R, tc), lambda i, ids: (jnp.where(ids[0] == d, (d + 1) % 8, d), 0, i))

    spec = pl.BlockSpec((R, tc), lambda i, ids: (0, i))
    return pl.pallas_call(
        body, name=name,
        grid_spec=pltpu.PrefetchScalarGridSpec(
            num_scalar_prefetch=1, grid=(C // tc,),
            in_specs=[spec, pl.BlockSpec((1, R, tc), lambda i, ids: (ids[1], 0, i))] + [slot(d) for d in range(8)]
            + [spec, spec],
            out_specs=[spec] * 4),
        out_shape=[jax.ShapeDtypeStruct((R, C), F32)] * 4,
        compiler_params=_cparams(("parallel",)),
    )(ids, w, slabs, *([land] * 8), m, v)


SMALL_NAMES = ("conv_b", "ssd_norm_w", "ln_g", "ln_b", "dt_bias", "a_log", "d_skip", "attn_sinks")
SMALL_FIELDS = ((4, 0, D_XBC), (5, 0, D_SSD), (6, 0, D_MODEL), (7, 0, D_MODEL), (5, 1024, SSD_HEADS), (5, 1152, SSD_HEADS),
                (5, 1280, SSD_HEADS), (5, 1408, ATT_QH))
LOSS_FIELD = (6, 1024, 128)
K_SMALL = D_XBC


def _pack_small(g_conv_w, vecs, loss):
    def body(cw_ref, *refs):
        o_ref = refs[-1]
        o_ref[...] = jnp.zeros_like(o_ref)
        o_ref[0:CONV_K, 0:D_XBC] = cw_ref[...]
        for v_ref, (row, off, n) in zip(refs[:-2], SMALL_FIELDS):
            o_ref[row:row + 1, off:off + n] = v_ref[...]
        o_ref[LOSS_FIELD[0]:LOSS_FIELD[0] + 1, LOSS_FIELD[1]:LOSS_FIELD[1] + LOSS_FIELD[2]] = refs[-2][...]

    return pl.pallas_call(
        body, name="pack_small", out_shape=jax.ShapeDtypeStruct((8, K_SMALL), F32), compiler_params=_cparams(),
    )(g_conv_w, *vecs, loss)


def _adamw_small(slots, chip, conv_w, m_conv_w, v_conv_w, params, moms, vars_):
    n_vec = len(SMALL_NAMES)

    def body(chip_ref, s_ref, *refs):
        ins = refs[:3 * (n_vec + 1)]
        outs = refs[3 * (n_vec + 1):-1]
        tot_ref = refs[-1]
        tot = s_ref[0]
        for d in range(1, 8):
            tot = tot + s_ref[d]
        outs[0][...] = tot[LOSS_FIELD[0]:LOSS_FIELD[0] + 1, LOSS_FIELD[1]:LOSS_FIELD[1] + 1]
        off = pl.multiple_of(chip_ref[0] * CONV_COLS, 128)
        tot_ref[...] = tot
        grads = [tot_ref[0:CONV_K, pl.ds(off, CONV_COLS)]]
        grads += [tot[row:row + 1, o:o + n] for row, o, n in SMALL_FIELDS]
        for k, g in enumerate(grads):
            w_ref, m_ref, v_ref = ins[3 * k:3 * k + 3]
            full = (0,) if k == 0 else (Ellipsis,)
            d, nm, nv = _adamw_math(w_ref[full], g, m_ref[full], v_ref[full])
            for o_ref, val in zip(outs[1 + 4 * k:5 + 4 * k], (g, d, nm, nv)):
                o_ref[full] = val

    args = [conv_w, m_conv_w, v_conv_w]
    for w, m, v in zip(params, moms, vars_):
        args += [w, m, v]
    shapes = [jax.ShapeDtypeStruct((1, 1), F32)] + [jax.ShapeDtypeStruct(conv_w.shape, F32)] * 4
    for w in params:
        shapes += [jax.ShapeDtypeStruct(w.shape, F32)] * 4
    vmem = pl.BlockSpec(memory_space=pltpu.VMEM)
    return pl.pallas_call(
        body, name="adamw_small",
        grid_spec=pltpu.PrefetchScalarGridSpec(
            num_scalar_prefetch=1, grid=(1,),
            in_specs=[pl.BlockSpec(slots.shape, lambda i, chip: (0, 0, 0))] + [vmem] * len(args),
            out_specs=[vmem] * len(shapes), scratch_shapes=[pltpu.VMEM((8, K_SMALL), F32)]),
        out_shape=shapes, compiler_params=_cparams(),
    )(chip, slots, *args)


def kernel(x, positions, w_in, conv_w, conv_b, dt_bias, a_log, d_skip, ssd_norm_w, attn_sinks, w_out, ln_g, ln_b, loss_target, m_w_in, m_conv_w, m_conv_b, m_dt_bias, m_a_log, m_d_skip, m_ssd_norm_w, m_attn_sinks, m_w_out, m_ln_g, m_ln_b, v_w_in, v_conv_w, v_conv_b, v_dt_bias, v_a_log, v_d_skip, v_ssd_norm_w, v_attn_sinks, v_w_out, v_ln_g, v_ln_b):
    mx, my, mc = _mesh_pos()
    chip = 2 * mx + my
    L = x.shape[1]

    conv_w_s8 = jnp.pad(conv_w[0], ((0, 8 - CONV_K), (0, 0)))
    pad_rows = ((0, SLAB_ROWS - W_IN_COLS), (0, 0))
    w_in_t = w_in[0].T
    w_in_b, w_out_b = jnp.pad(_bf(w_in_t), pad_rows), _bf(w_out[0])
    ag_in, ag_cw = _gather_weights(w_in_b, conv_w_s8)
    started = _gather_w_out_start(w_out_b, ag_cw)
    own = (jnp.arange(N_CHIPS) == chip)[:, None, None]

    def get_w_out(after):
        landed = _gather_w_out_wait(started[0:6], started[6], started[7], after)
        return jnp.where(own, w_out_b[None], landed).reshape(D_MIX, D_MODEL)

    ag_in = jnp.where(own, w_in_b[None], ag_in)
    w_full = jnp.concatenate([ag_in[j, 0:W_IN_COLS] for j in range(N_CHIPS)], axis=0)
    w = jnp.concatenate([
        w_full[O_Z:O_Z + D_SSD], w_full[O_G:O_G + D_ATT], w_full[O_Q:O_Q + D_ATT],
        w_full[O_XBC:O_XBC + D_XBC], w_full[O_K:O_K + 2 * D_KV], w_full[O_DT:O_DT + SSD_HEADS],
        jnp.zeros((DT_PAD - SSD_HEADS, D_MODEL), BF16)], axis=0)
    conv_w_full = jnp.concatenate([ag_cw[j, 0:CONV_K] for j in range(N_CHIPS)], axis=1)

    loss_part, gx_args, gw_in, w_out_red, small = _local_step(
        x[0], positions[0].reshape(L, 1), loss_target[0], w, get_w_out, started[8][0:1, :], conv_w_full,
        conv_b, dt_bias, a_log, d_skip, ssd_norm_w, attn_sinks, ln_g, ln_b)

    packed = _pack_small(small["conv_w"], [small[n] for n in SMALL_NAMES], loss_part)
    core_id = mc.reshape(1).astype(jnp.int32)
    chip_id = chip.reshape(1).astype(jnp.int32)
    ids = jnp.stack([4 * mx + 2 * my + mc, chip]).astype(jnp.int32)
    slabs = jnp.stack([jnp.pad(gw_in[W_IN_COLS * j:W_IN_COLS * (j + 1)], pad_rows) for j in range(N_CHIPS)])
    w_in_red = _pair_start(slabs, gw_in[0:8, 0:128])
    grad_x = _grad_x(*gx_args, w_in_red[10], 0)
    gw_in_slabs, recv_in = _pair_wait(w_in_red[0:8], w_in_red[8], w_in_red[9], grad_x[0:8, 0:128])
    s_in = _pair_add(gw_in_slabs, recv_in, core_id, "pair_add_in")
    chip_red = _chip_start(s_in, packed)
    grad_x = _grad_x(*gx_args, chip_red[8], 1, grad_x)
    own_slabs, landed = _reduce_w_out_wait(w_out_red[0:14], w_out_red[14], w_out_red[15], grad_x[L - 8:L, 0:128])
    out_t = _adamw_sum8(w_out[0], own_slabs, landed, m_w_out[0], v_w_out[0], ids, "adamw_w_out")
    d_w_out, nm_w_out, nv_w_out, g_w_out = [a[None] for a in out_t]
    s_in, r_in = _chip_wait(chip_red[0:6], chip_red[6], chip_red[7], out_t[0][0:8, 0:128])
    h_in = _chip_add(s_in, r_in, chip_id, "chip_add_in")
    sib_in, slots = _pair_share(h_in, packed)

    to_rows = lambda a: jnp.transpose(a, (2, 0, 1))
    in_t = _adamw_rows(to_rows(w_in), h_in, sib_in, to_rows(m_w_in), to_rows(v_w_in), core_id, "adamw_w_in")
    d_w_in, nm_w_in, nv_w_in, g_w_in = [jnp.transpose(a, (1, 2, 0)) for a in in_t]

    params = dict(conv_b=conv_b, ssd_norm_w=ssd_norm_w, ln_g=ln_g, ln_b=ln_b, dt_bias=dt_bias, a_log=a_log,
                  d_skip=d_skip, attn_sinks=attn_sinks)
    moms = dict(conv_b=m_conv_b, ssd_norm_w=m_ssd_norm_w, ln_g=m_ln_g, ln_b=m_ln_b, dt_bias=m_dt_bias, a_log=m_a_log,
                d_skip=m_d_skip, attn_sinks=m_attn_sinks)
    vars_ = dict(conv_b=v_conv_b, ssd_norm_w=v_ssd_norm_w, ln_g=v_ln_g, ln_b=v_ln_b, dt_bias=v_dt_bias, a_log=v_a_log,
                 d_skip=v_d_skip, attn_sinks=v_attn_sinks)
    res = _adamw_small(slots, chip_id, conv_w, m_conv_w, v_conv_w, [params[n] for n in SMALL_NAMES],
                       [moms[n] for n in SMALL_NAMES], [vars_[n] for n in SMALL_NAMES])
    loss = res[0][0, 0]
    grads, delta, new_m, new_v = {}, {}, {}, {}
    for k, n in enumerate(("conv_w",) + SMALL_NAMES):
        grads[n], delta[n], new_m[n], new_v[n] = res[1 + 4 * k:5 + 4 * k]
    for dd, a_in, a_out in ((grads, g_w_in, g_w_out), (delta, d_w_in, d_w_out), (new_m, nm_w_in, nm_w_out),
                            (new_v, nv_w_in, nv_w_out)):
        dd["w_in"] = a_in
        dd["w_out"] = a_out
    order = ("w_in", "conv_w", "conv_b", "dt_bias", "a_log", "d_skip", "ssd_norm_w", "attn_sinks", "w_out", "ln_g", "ln_b")
    return (loss, grad_x[None], *[grads[n] for n in order], *[delta[n] for n in order], *[new_m[n] for n in order],
            *[new_v[n] for n in order])
```

```python
import numpy as np
import jax
import jax.numpy as jnp
from jax import lax
from jax.experimental import pallas as pl
from jax.experimental.pallas import tpu as pltpu

F32 = jnp.float32
BF16 = jnp.bfloat16
MESH = pl.DeviceIdType.MESH

D_MODEL = 1024
D_SSD = 1024
D_ATT = 1024
D_MIX = 2048
SSD_HEADS = 16
SSD_P = 64
SSD_GROUPS = 2
SSD_R = 8
SSD_N = 128
D_BC = 256
D_XBC = 1536
CONV_K = 4
CHUNK = 128
ATT_HD = 64
ATT_QH = 16
ATT_KVH = 4
ATT_R = 4
D_KV = 256
WINDOW = 128
ROPE_THETA = 500000.0
ROPE_DIM = 16
ALPHA = 2.0 ** 0.25
LN_EPS = 1e-5
RMS_EPS = 1e-5
D_IN_PROJ = 5136
O_Z, O_XBC, O_DT, O_Q, O_K, O_V, O_G = 0, 1024, 2560, 2576, 3600, 3856, 4112
P_Z, P_G, P_Q, P_XBC, P_KV, P_DT, P_END = 0, 1024, 2048, 3072, 4608, 5120, 5248
DT_PAD = 128
N_CHIPS = 4
W_IN_COLS = D_IN_PROJ // N_CHIPS
SLAB_ROWS = 1312
W_OUT_ROWS = D_MIX // N_CHIPS
CONV_COLS = D_XBC // N_CHIPS

ADAM_LR = 0.001
ADAM_B1 = 0.9
ADAM_B2 = 0.999
ADAM_EPS = 1e-08
ADAM_WD = 0.01
ADAM_STEP = 10

VMEM_LIMIT = 56 * 1024 * 1024
ROW_TILE = 512
NEG_BIG = -1e30
HI = lax.Precision.HIGHEST


def _cparams(sem=None, **kw):
    if sem is not None:
        kw["dimension_semantics"] = sem
    return pltpu.CompilerParams(vmem_limit_bytes=VMEM_LIMIT, **kw)


def _dot(a, b):
    return jnp.dot(a, b, preferred_element_type=F32)


def _dot_nt(a, b):
    return lax.dot_general(a, b, (((1,), (1,)), ((), ())), preferred_element_type=F32)


def _dot_tn(a, b):
    return lax.dot_general(a, b, (((0,), (0,)), ((), ())), preferred_element_type=F32)


def _bf(a):
    return a.astype(BF16)


def _iota2(shape, dim):
    return lax.broadcasted_iota(jnp.int32, shape, dim)


def _to_rows(col):
    k = col.shape[1]
    eye = (_iota2((k, k), 0) == _iota2((k, k), 1)).astype(F32)
    return lax.dot_general(eye, col, (((1,), (1,)), ((), ())), preferred_element_type=F32, precision=HI)


def _to_cols(row):
    n = row.shape[1]
    eye = (_iota2((n, n), 0) == _iota2((n, n), 1)).astype(F32)
    return lax.dot_general(eye, row, (((1,), (1,)), ((), ())), preferred_element_type=F32, precision=HI)


def _sigmoid(x):
    return jax.nn.sigmoid(x)


def _in_proj(x, w, pos, inv):
    L = x.shape[0]
    tm = ROW_TILE
    widths = (D_SSD, D_ATT, D_ATT, D_XBC, 2 * D_KV, DT_PAD)

    def body(x_ref, w_ref, pos_ref, inv_ref, z_ref, g_ref, q_ref, xbc_ref, kv_ref, dt_ref, xb_ref):
        xb = _bf(x_ref[...])
        xb_ref[...] = xb
        for o_ref, off, wd in zip((z_ref, g_ref, xbc_ref, dt_ref), (P_Z, P_G, P_XBC, P_DT), (D_SSD, D_ATT, D_XBC, DT_PAD)):
            o_ref[...] = _dot_nt(xb, w_ref[off:off + wd, :])
        tabs = _rope_tables(pos_ref, inv_ref)
        q_ref[...] = _bf(_rope(_dot_nt(xb, w_ref[P_Q:P_Q + D_ATT, :]), tabs))
        kv_ref[:, 0:D_KV] = _bf(_rope(_dot_nt(xb, w_ref[P_KV:P_KV + D_KV, :]), tabs))
        kv_ref[:, D_KV:2 * D_KV] = _bf(_dot_nt(xb, w_ref[P_KV + D_KV:P_KV + 2 * D_KV, :]))

    row = lambda wd: pl.BlockSpec((tm, wd), lambda i: (i, 0))
    return pl.pallas_call(
        body, name="in_proj", grid=(L // tm,),
        in_specs=[row(D_MODEL), pl.BlockSpec((P_END, D_MODEL), lambda i: (0, 0), pipeline_mode=pl.Buffered(1)), row(1),
                  pl.BlockSpec((1, 2 * ATT_HD), lambda i: (0, 0))],
        out_specs=[row(wd) for wd in widths] + [row(D_MODEL)],
        out_shape=[jax.ShapeDtypeStruct((L, wd), dt) for wd, dt in zip(widths, (F32, F32, BF16, F32, BF16, F32))]
        + [jax.ShapeDtypeStruct((L, D_MODEL), BF16)],
        compiler_params=_cparams(("parallel",)),
    )(x, w, pos, inv)


def _matmuls_tn(a_list, b, name, out_dtype=F32):
    K, N = b.shape
    tk = min(K, 1024)
    nk = K // tk
    n = len(a_list)
    in_place = out_dtype == F32

    def body(*refs):
        b_ref = refs[n]
        o_refs = refs[n + 1:2 * n + 1]
        acc_refs = o_refs if in_place else refs[2 * n + 1:]
        k = pl.program_id(0)
        bb = _bf(b_ref[...])
        for a_ref, o_ref, acc_ref in zip(refs[:n], o_refs, acc_refs):
            part = _dot_tn(_bf(a_ref[...]), bb)

            @pl.when(k == 0)
            def _():
                acc_ref[...] = part

            @pl.when(k > 0)
            def _():
                acc_ref[...] += part

            if not in_place:
                @pl.when(k == nk - 1)
                def _():
                    o_ref[...] = acc_ref[...].astype(out_dtype)

    return pl.pallas_call(
        body, name=name, grid=(nk,),
        in_specs=[pl.BlockSpec((tk, a.shape[1]), lambda k: (k, 0)) for a in a_list] + [pl.BlockSpec((tk, N), lambda k: (k, 0))],
        out_specs=[pl.BlockSpec((a.shape[1], N), lambda k: (0, 0)) for a in a_list],
        out_shape=[jax.ShapeDtypeStruct((a.shape[1], N), out_dtype) for a in a_list],
        scratch_shapes=[] if in_place else [pltpu.VMEM((a.shape[1], N), F32) for a in a_list],
        compiler_params=_cparams(("arbitrary",)),
    )(*a_list, b)


def _grad_x(dr, dz, dg, dq, dxbc, dkv, ddt, w, after, part, prev=None):
    L = dr.shape[0]
    tm = min(ROW_TILE, L // 2)
    n = L // (2 * tm)
    widths = (D_SSD, D_ATT, D_ATT, D_XBC, 2 * D_KV, DT_PAD)
    offs = (P_Z, P_G, P_Q, P_XBC, P_KV, P_DT)

    def body(dr_ref, dz_ref, dg_ref, dq_ref, dxbc_ref, dkv_ref, ddt_ref, w_ref, after_ref, *rest):
        o_ref = rest[-1]
        acc = ALPHA * dr_ref[...]
        for p_ref, off, wd in zip((dz_ref, dg_ref, dq_ref, dxbc_ref, dkv_ref, ddt_ref), offs, widths):
            acc = acc + _dot(_bf(p_ref[...]), w_ref[off:off + wd, :])
        o_ref[...] = acc

    row = lambda wd: pl.BlockSpec((tm, wd), lambda i: (i + part * n, 0))
    ins = [dr, dz, dg, dq, dxbc, dkv, ddt, w, after]
    specs = ([row(D_MODEL)] + [row(wd) for wd in widths]
             + [pl.BlockSpec((P_END, D_MODEL), lambda i: (0, 0), pipeline_mode=pl.Buffered(1)),
                pl.BlockSpec((8, 128), lambda i: (0, 0))])
    if prev is not None:
        ins.append(prev)
        specs.append(pl.BlockSpec(memory_space=pl.ANY))
    return pl.pallas_call(
        body, name="grad_x_%d" % part, grid=(n,),
        in_specs=specs, out_specs=row(D_MODEL),
        out_shape=jax.ShapeDtypeStruct((L, D_MODEL), F32),
        input_output_aliases={} if prev is None else {len(ins) - 1: 0},
        compiler_params=_cparams(("parallel",)),
    )(*ins)


HALO = 16


def _shift_matrix(offsets):
    n = CHUNK + HALO
    m = np.zeros((len(offsets) * CHUNK, 2 * n), np.float32)
    for k, off in enumerate(offsets):
        t = np.arange(CHUNK)
        m[k * CHUNK + t, t + off] = 1.0
        m[k * CHUNK + t, n + t + off] = 1.0
    return jnp.asarray(m, BF16)


def _shifted_rows(first_part, second_part, smat_ref):
    h1, l1 = _hi_lo(first_part)
    h2, l2 = _hi_lo(second_part)
    sh = _dot(smat_ref[...], jnp.concatenate([h1, h2, l1, l2], axis=0))
    return sh[0:CHUNK], sh[CHUNK:2 * CHUNK], sh[2 * CHUNK:3 * CHUNK]


def _ssd_chunk_pre(first, xbc_ref, tail_ref, dt_ref, cw_ref, cb_ref, dtb_ref, alog_ref, smat_ref):
    tail = jnp.where(first, 0.0, tail_ref[...])
    x = xbc_ref[...]
    taps = _shifted_rows(tail, x, smat_ref) + (x,)
    u = cb_ref[...] + cw_ref[0:1, :] * taps[0]
    for k in range(1, CONV_K):
        u = u + cw_ref[k:k + 1, :] * taps[k]
    sig = _sigmoid(u)
    xbc = u * sig
    dtraw = dt_ref[:, 0:SSD_HEADS] + dtb_ref[...]
    dt = jax.nn.softplus(dtraw)
    A = -jnp.exp(alog_ref[...])
    a = dt * A
    tril = (_iota2((CHUNK, CHUNK), 0) >= _iota2((CHUNK, CHUNK), 1)).astype(F32)
    acs = jnp.dot(tril, a, preferred_element_type=F32, precision=HI)
    acs_row = _to_rows(acs)
    return u, sig, xbc, dtraw, dt, A, acs, acs_row, taps


def _head_expander():
    return (_iota2((SSD_HEADS, D_SSD), 1) // SSD_P == _iota2((SSD_HEADS, D_SSD), 0)).astype(BF16)


def _hi_lo(x):
    hi = _bf(x)
    return hi, _bf(x - hi.astype(F32))


def _expand(v, e):
    hi, lo = _hi_lo(v)
    return _dot(hi, e) + _dot(lo, e)


def _headsum(t, e):
    m = t.shape[0]
    if m < 8:
        t = jnp.broadcast_to(t[0:1], (8, t.shape[1]))
    hi, lo = _hi_lo(t)
    return (_dot_nt(hi, e) + _dot_nt(lo, e))[0:m]


def _ssd_decays(dt, acs, dsk_ref, e):
    alast = acs[CHUNK - 1:CHUNK, :]
    stk = jnp.concatenate([dt, jnp.exp(acs), jnp.exp(alast - acs),
                           jnp.broadcast_to(jnp.exp(alast), (8, SSD_HEADS)),
                           jnp.broadcast_to(dsk_ref[...], (8, SSD_HEADS))], axis=0)
    ex = _expand(stk, e)
    return (ex[0:CHUNK], ex[CHUNK:2 * CHUNK], ex[2 * CHUNK:3 * CHUNK], ex[3 * CHUNK:3 * CHUNK + 1],
            ex[3 * CHUNK + 8:3 * CHUNK + 9])


def _ssd_fwd(z, xbc, dtp, conv_w, conv_b, dt_bias, a_log, d_skip, norm_w):
    L = z.shape[0]
    nc = L // CHUNK
    half = D_SSD // SSD_GROUPS

    def body(z_ref, xbc_ref, tail_ref, dt_ref, cw_ref, cb_ref, dtb_ref, alog_ref, dsk_ref, nw_ref, smat_ref,
             y_ref, ypre_ref, prev_ref, state, ybuf, mbuf):
        c = pl.program_id(0)

        @pl.when(c == 0)
        def _():
            state[...] = jnp.zeros_like(state)

        u, sig, xbcv, dtraw, dt, A, acs, acs_row, _ = _ssd_chunk_pre(
            c == 0, xbc_ref, tail_ref, dt_ref, cw_ref, cb_ref, dtb_ref, alog_ref, smat_ref)
        e = _head_expander()
        dtE, eacsE, dsdE, ealE, dskE = _ssd_decays(dt, acs, dsk_ref, e)
        xs = xbcv[:, 0:D_SSD]
        X = xs * dtE
        prev_ref[0] = state[...]
        causal = _iota2((CHUNK, CHUNK), 0) >= _iota2((CHUNK, CHUNK), 1)
        for g in range(SSD_GROUPS):
            gs = slice(half * g, half * (g + 1))
            Bg = _bf(xbcv[:, D_SSD + SSD_N * g:D_SSD + SSD_N * (g + 1)])
            Cg = _bf(xbcv[:, D_SSD + D_BC + SSD_N * g:D_SSD + D_BC + SSD_N * (g + 1)])
            cb = _dot_nt(Cg, Bg)
            for r in range(SSD_R):
                h = g * SSD_R + r
                seg = acs[:, h:h + 1] - acs_row[h:h + 1, :]
                mbuf[h] = _bf(cb * jnp.where(causal, jnp.exp(jnp.where(causal, seg, 0.0)), 0.0))
            st = state[:, gs]
            ybuf[:, gs] = _dot(Cg, _bf(st)) * eacsE[:, gs] + dskE[:, gs] * xs[:, gs]
            state[:, gs] = st * ealE[:, gs] + _dot_tn(Bg, _bf(X[:, gs] * dsdE[:, gs]))
        Xb = _bf(X)
        for h in range(SSD_HEADS):
            hs = slice(SSD_P * h, SSD_P * (h + 1))
            ybuf[:, hs] += _dot(mbuf[h], Xb[:, hs])
        y = ybuf[...]
        ypre_ref[...] = y
        zv = z_ref[...]
        yf = y * (zv * _sigmoid(zv))
        for g in range(SSD_GROUPS):
            gs = slice(half * g, half * (g + 1))
            yg = yf[:, gs]
            ms = jnp.mean(yg * yg, axis=-1, keepdims=True)
            y_ref[:, gs] = _bf(yg * lax.rsqrt(ms + RMS_EPS) * nw_ref[:, gs])

    full = lambda shape: pl.BlockSpec(shape, lambda c: (0, 0))
    return pl.pallas_call(
        body, name="ssd_fwd", grid=(nc,),
        in_specs=[
            pl.BlockSpec((CHUNK, D_SSD), lambda c: (c, 0)),
            pl.BlockSpec((CHUNK, D_XBC), lambda c: (c, 0)),
            pl.BlockSpec((HALO, D_XBC), lambda c: (jnp.maximum(c * (CHUNK // HALO) - 1, 0), 0)),
            pl.BlockSpec((CHUNK, DT_PAD), lambda c: (c, 0)),
            full((CONV_K, D_XBC)), full((1, D_XBC)), full((1, SSD_HEADS)), full((1, SSD_HEADS)), full((1, SSD_HEADS)),
            full((1, D_SSD)), full((3 * CHUNK, 2 * (CHUNK + HALO))),
        ],
        out_specs=[
            pl.BlockSpec((CHUNK, D_SSD), lambda c: (c, 0)),
            pl.BlockSpec((CHUNK, D_SSD), lambda c: (c, 0)),
            pl.BlockSpec((1, SSD_N, D_SSD), lambda c: (c, 0, 0)),
        ],
        out_shape=[
            jax.ShapeDtypeStruct((L, D_SSD), BF16),
            jax.ShapeDtypeStruct((L, D_SSD), F32),
            jax.ShapeDtypeStruct((nc, SSD_N, D_SSD), F32),
        ],
        scratch_shapes=[
            pltpu.VMEM((SSD_N, D_SSD), F32),
            pltpu.VMEM((CHUNK, D_SSD), F32),
            pltpu.VMEM((SSD_HEADS, CHUNK, CHUNK), BF16),
        ],
        compiler_params=_cparams(("arbitrary",)),
    )(z, xbc, xbc, dtp, conv_w, conv_b, dt_bias, a_log, d_skip, norm_w, _shift_matrix((13, 14, 15)))


def _ssd_bwd(dy, z, ypre, xbc, dtp, prev, conv_w, conv_b, dt_bias, a_log, d_skip, norm_w):
    L = z.shape[0]
    nc = L // CHUNK
    half = D_SSD // SSD_GROUPS

    def body(dy_ref, z_ref, ypre_ref, xbc_ref, tail_ref, dt_ref, prev_ref, cw_ref, cb_ref, dtb_ref, alog_ref, dsk_ref,
             nw_ref, smat_ref, smat2_ref, dz_ref, dxbc_ref, ddt_ref, gcw_ref, gcb_ref, gdtb_ref, galog_ref, gdsk_ref,
             gnw_ref, dstate, dhead, dpost, yobuf, bdbuf, lmbuf, dmbuf, cbbuf):
        i = pl.program_id(0)
        c = nc - 1 - i

        @pl.when(i == 0)
        def _():
            dstate[...] = jnp.zeros_like(dstate)
            dhead[...] = jnp.zeros_like(dhead)
            gcw_ref[...] = jnp.zeros_like(gcw_ref)
            gcb_ref[...] = jnp.zeros_like(gcb_ref)
            gdtb_ref[...] = jnp.zeros_like(gdtb_ref)
            galog_ref[...] = jnp.zeros_like(galog_ref)
            gdsk_ref[...] = jnp.zeros_like(gdsk_ref)
            gnw_ref[...] = jnp.zeros_like(gnw_ref)

        u, sig, xbcv, dtraw, dt, A, acs, acs_row, taps = _ssd_chunk_pre(
            c == 0, xbc_ref, tail_ref, dt_ref, cw_ref, cb_ref, dtb_ref, alog_ref, smat_ref)
        e = _head_expander()
        dtE, eacsE, dsdE, ealE, dskE = _ssd_decays(dt, acs, dsk_ref, e)
        alast = acs[CHUNK - 1:CHUNK, :]
        xs = xbcv[:, 0:D_SSD]
        X = xs * dtE
        Xb = _bf(X)

        zv = z_ref[...]
        ypre = ypre_ref[...]
        dyn = dy_ref[...]
        sz = _sigmoid(zv)
        silu_z = zv * sz
        yf = ypre * silu_z
        dyf_parts = []
        for g in range(SSD_GROUPS):
            gs = slice(half * g, half * (g + 1))
            yg = yf[:, gs]
            rstd = lax.rsqrt(jnp.mean(yg * yg, axis=-1, keepdims=True) + RMS_EPS)
            dout = dyn[:, gs]
            gnw_ref[:, gs] += jnp.sum(dout * yg * rstd, axis=0, keepdims=True)
            dyhat = dout * nw_ref[:, gs]
            dyf_parts.append(rstd * (dyhat - yg * (rstd * rstd) * jnp.mean(dyhat * yg, axis=-1, keepdims=True)))
        dyf = jnp.concatenate(dyf_parts, axis=1)
        dz_ref[...] = _bf(dyf * ypre * (sz * (1.0 + zv * (1.0 - sz))))
        dyp = dyf * silu_z
        dyb = _bf(dyp)
        G = dyp * eacsE

        causal = _iota2((CHUNK, CHUNK), 0) >= _iota2((CHUNK, CHUNK), 1)
        ST = prev_ref[0]
        dST = dstate[...]
        for g in range(SSD_GROUPS):
            gs = slice(half * g, half * (g + 1))
            bs = slice(D_SSD + SSD_N * g, D_SSD + SSD_N * (g + 1))
            cs = slice(D_SSD + D_BC + SSD_N * g, D_SSD + D_BC + SSD_N * (g + 1))
            Bg = _bf(xbcv[:, bs])
            Cg = _bf(xbcv[:, cs])
            Gb = _bf(G[:, gs])
            STb = _bf(ST[:, gs])
            dSTb = _bf(dST[:, gs])
            dstate[:, gs] = dST[:, gs] * ealE[:, gs] + _dot_tn(Cg, Gb)
            yobuf[:, gs] = _dot(Cg, STb) * eacsE[:, gs]
            bdbuf[:, gs] = _dot(Bg, dSTb)
            dpost[:, cs] = _dot_nt(Gb, STb)
            dpost[:, bs] = _dot_nt(_bf(X[:, gs] * dsdE[:, gs]), dSTb)
            cbbuf[g] = _dot_nt(Cg, Bg)
            for r in range(SSD_R):
                h = g * SSD_R + r
                seg = acs[:, h:h + 1] - acs_row[h:h + 1, :]
                lmbuf[h] = jnp.where(causal, jnp.exp(jnp.where(causal, seg, 0.0)), 0.0)
        for h in range(SSD_HEADS):
            hs = slice(SSD_P * h, SSD_P * (h + 1))
            Mb = _bf(cbbuf[h // SSD_R] * lmbuf[h])
            dmbuf[h] = _dot_nt(dyb[:, hs], Xb[:, hs])
            dpost[:, hs] = _dot_tn(Mb, dyb[:, hs])
        lane16 = _iota2((1, SSD_HEADS), 1)
        sub16 = _iota2((SSD_HEADS, 1), 0)
        dacs_col = jnp.zeros((CHUNK, SSD_HEADS), F32)
        dacs_row = jnp.zeros((SSD_HEADS, CHUNK), F32)
        for g in range(SSD_GROUPS):
            bs = slice(D_SSD + SSD_N * g, D_SSD + SSD_N * (g + 1))
            cs = slice(D_SSD + D_BC + SSD_N * g, D_SSD + D_BC + SSD_N * (g + 1))
            cb = cbbuf[g]
            dcb = jnp.zeros((CHUNK, CHUNK), F32)
            for r in range(SSD_R):
                h = g * SSD_R + r
                dM = dmbuf[h]
                Lm = lmbuf[h]
                dcb = dcb + dM * Lm
                dseg = dM * (cb * Lm)
                dacs_col = dacs_col + jnp.sum(dseg, axis=-1, keepdims=True) * (lane16 == h).astype(F32)
                dacs_row = dacs_row - jnp.sum(dseg, axis=0, keepdims=True) * (sub16 == h).astype(F32)
            dcbb = _bf(dcb)
            dpost[:, bs] += _dot_tn(dcbb, _bf(xbcv[:, cs]))
            dpost[:, cs] += _dot(dcbb, _bf(xbcv[:, bs]))

        BD = bdbuf[...]
        dX = dpost[:, 0:D_SSD] + dsdE * BD
        dsd = jnp.exp(alast - acs)
        T = _headsum(X * BD, e) * dsd
        dalast = jnp.sum(T, axis=0, keepdims=True) + _headsum(
            jnp.sum(dST * ST, axis=0, keepdims=True), e) * jnp.exp(alast)
        is_last = (_iota2((CHUNK, 1), 0) == CHUNK - 1).astype(F32)
        dacs = dacs_col + _to_cols(dacs_row) + _headsum(dyp * yobuf[...], e) - T + is_last * dalast
        triu = (_iota2((CHUNK, CHUNK), 0) <= _iota2((CHUNK, CHUNK), 1)).astype(F32)
        da = jnp.dot(triu, dacs, preferred_element_type=F32, precision=HI)
        ddt_tot = _headsum(dX * xs, e) + da * A
        galog_ref[...] += jnp.sum(da * dt, axis=0, keepdims=True) * A
        ddtraw = ddt_tot * _sigmoid(dtraw)
        gdtb_ref[...] += jnp.sum(ddtraw, axis=0, keepdims=True)
        gdsk_ref[...] += _headsum(jnp.sum(dyp * xs, axis=0, keepdims=True), e)
        ddt_ref[...] = jnp.zeros_like(ddt_ref)
        ddt_ref[:, 0:SSD_HEADS] = ddtraw
        dpost[:, 0:D_SSD] = dX * dtE + dskE * dyp

        dconv = dpost[...] * (sig * (1.0 + u * (1.0 - sig)))
        gcb_ref[...] += jnp.sum(dconv, axis=0, keepdims=True)
        for k in range(CONV_K):
            gcw_ref[k:k + 1, :] += jnp.sum(dconv * taps[k], axis=0, keepdims=True)
        later = _shifted_rows(dconv, dhead[...], smat2_ref)
        dx = cw_ref[CONV_K - 1:CONV_K, :] * dconv
        for k in range(CONV_K - 1):
            dx = dx + cw_ref[k:k + 1, :] * later[k]
        dxbc_ref[...] = _bf(dx)
        dhead[...] = dconv[0:HALO, :]

    full = lambda shape: pl.BlockSpec(shape, lambda i: (0, 0))
    rev = lambda wd: pl.BlockSpec((CHUNK, wd), lambda i: (nc - 1 - i, 0))
    return pl.pallas_call(
        body, name="ssd_bwd", grid=(nc,),
        in_specs=[
            rev(D_SSD), rev(D_SSD), rev(D_SSD), rev(D_XBC),
            pl.BlockSpec((HALO, D_XBC), lambda i: (jnp.maximum((nc - 1 - i) * (CHUNK // HALO) - 1, 0), 0)),
            rev(DT_PAD),
            pl.BlockSpec((1, SSD_N, D_SSD), lambda i: (nc - 1 - i, 0, 0)),
            full((CONV_K, D_XBC)), full((1, D_XBC)), full((1, SSD_HEADS)), full((1, SSD_HEADS)), full((1, SSD_HEADS)),
            full((1, D_SSD)), full((3 * CHUNK, 2 * (CHUNK + HALO))), full((3 * CHUNK, 2 * (CHUNK + HALO))),
        ],
        out_specs=[
            rev(D_SSD), rev(D_XBC), rev(DT_PAD),
            full((CONV_K, D_XBC)), full((1, D_XBC)), full((1, SSD_HEADS)), full((1, SSD_HEADS)), full((1, SSD_HEADS)),
            full((1, D_SSD)),
        ],
        out_shape=[
            jax.ShapeDtypeStruct((L, D_SSD), BF16), jax.ShapeDtypeStruct((L, D_XBC), BF16),
            jax.ShapeDtypeStruct((L, DT_PAD), F32),
            jax.ShapeDtypeStruct((CONV_K, D_XBC), F32), jax.ShapeDtypeStruct((1, D_XBC), F32),
            jax.ShapeDtypeStruct((1, SSD_HEADS), F32), jax.ShapeDtypeStruct((1, SSD_HEADS), F32),
            jax.ShapeDtypeStruct((1, SSD_HEADS), F32), jax.ShapeDtypeStruct((1, D_SSD), F32),
        ],
        scratch_shapes=[
            pltpu.VMEM((SSD_N, D_SSD), F32),
            pltpu.VMEM((HALO, D_XBC), F32),
            pltpu.VMEM((CHUNK, D_XBC), F32),
            pltpu.VMEM((CHUNK, D_SSD), F32),
            pltpu.VMEM((CHUNK, D_SSD), F32),
            pltpu.VMEM((SSD_HEADS, CHUNK, CHUNK), F32),
            pltpu.VMEM((SSD_HEADS, CHUNK, CHUNK), F32),
            pltpu.VMEM((SSD_GROUPS, CHUNK, CHUNK), F32),
        ],
        compiler_params=_cparams(("arbitrary",)),
    )(dy, z, ypre, xbc, xbc, dtp, prev, conv_w, conv_b, dt_bias, a_log, d_skip, norm_w, _shift_matrix((13, 14, 15)),
      _shift_matrix((3, 2, 1)))


def _rope_tables(pos_ref, inv_ref):
    ang = pos_ref[...].astype(F32) * inv_ref[...]
    d = _iota2((1, 2 * ATT_HD), 1) % ATT_HD
    s = jnp.sin(ang)
    return jnp.cos(ang), jnp.where(d < ROPE_DIM // 2, -s, 0.0), jnp.where((d >= ROPE_DIM // 2) & (d < ROPE_DIM), s, 0.0)


def _rope(t, tabs):
    c, s1, s2 = tabs
    n = t.shape[1]
    rep = n // c.shape[1]
    return (t * jnp.tile(c, (1, rep)) + pltpu.roll(t, n - ROPE_DIM // 2, 1) * jnp.tile(s1, (1, rep))
            + pltpu.roll(t, ROPE_DIM // 2, 1) * jnp.tile(s2, (1, rep)))


def _rope_t(t, tabs):
    c, s1, s2 = tabs
    n = t.shape[1]
    rep = n // c.shape[1]
    return (t * jnp.tile(c, (1, rep)) + pltpu.roll(t * jnp.tile(s1, (1, rep)), ROPE_DIM // 2, 1)
            + pltpu.roll(t * jnp.tile(s2, (1, rep)), n - ROPE_DIM // 2, 1))


def _stack_heads(t, j):
    return jnp.concatenate([t[:, ATT_HD * (j * ATT_R + r):ATT_HD * (j * ATT_R + r + 1)] for r in range(ATT_R)], axis=0)


def _swa_mask_t(first):
    si = _iota2((2 * WINDOW, ATT_R * WINDOW), 0)
    qi = _iota2((2 * WINDOW, ATT_R * WINDOW), 1) % WINDOW
    band = (si > qi) & (si <= qi + WINDOW)
    return band & (jnp.logical_not(first) | (si >= WINDOW))


def _head_rows(ref, j):
    if ref.shape[0] == 1:
        parts = [jnp.broadcast_to(ref[:, j * ATT_R + r:j * ATT_R + r + 1], (1, WINDOW)) for r in range(ATT_R)]
    else:
        parts = [ref[j * ATT_R + r:j * ATT_R + r + 1, :] for r in range(ATT_R)]
    return jnp.concatenate(parts, axis=1)


def _swa_fwd(q, g, kv, sinks):
    L = q.shape[0]
    nb = L // WINDOW
    scale = ATT_HD ** -0.5

    def body(q_ref, g_ref, kvc_ref, kvp_ref, sink_ref, y_ref, o_ref, lse_ref, otbuf):
        n = pl.program_id(0)
        kk = jnp.concatenate([kvp_ref[:, 0:D_KV], kvc_ref[:, 0:D_KV]], axis=0)
        vv = jnp.concatenate([kvp_ref[:, D_KV:2 * D_KV], kvc_ref[:, D_KV:2 * D_KV]], axis=0)
        valid = _swa_mask_t(n == 0)
        qv = q_ref[...]
        for j in range(ATT_KVH):
            js = slice(ATT_HD * j, ATT_HD * (j + 1))
            st = _dot_nt(kk[:, js], _stack_heads(qv, j)) * scale
            st = jnp.where(valid, st, NEG_BIG)
            sink = _head_rows(sink_ref, j)
            m = jnp.maximum(jnp.max(st, axis=0, keepdims=True), sink)
            p = jnp.exp(st - m)
            denom = jnp.sum(p, axis=0, keepdims=True) + jnp.exp(sink - m)
            ot = _dot_tn(vv[:, js], _bf(p)) * (1.0 / denom)
            lse = m + jnp.log(denom)
            for r in range(ATT_R):
                h = j * ATT_R + r
                otbuf[ATT_HD * h:ATT_HD * (h + 1), :] = ot[:, WINDOW * r:WINDOW * (r + 1)]
                lse_ref[h:h + 1, :] = lse[:, WINDOW * r:WINDOW * (r + 1)]
        o = otbuf[...].T
        o_ref[...] = o
        gv = g_ref[...]
        y_ref[...] = _bf(o * (gv * _sigmoid(gv)))

    cur = lambda wd: pl.BlockSpec((WINDOW, wd), lambda n: (n, 0))
    prv = lambda wd: pl.BlockSpec((WINDOW, wd), lambda n: (jnp.maximum(n - 1, 0), 0))
    return pl.pallas_call(
        body, name="swa_fwd", grid=(nb,),
        in_specs=[cur(D_ATT), cur(D_ATT), cur(2 * D_KV), prv(2 * D_KV), pl.BlockSpec((1, ATT_QH), lambda n: (0, 0))],
        out_specs=[cur(D_ATT), cur(D_ATT), pl.BlockSpec((ATT_QH, WINDOW), lambda n: (0, n))],
        out_shape=[jax.ShapeDtypeStruct((L, D_ATT), BF16), jax.ShapeDtypeStruct((L, D_ATT), F32),
                   jax.ShapeDtypeStruct((ATT_QH, L), F32)],
        scratch_shapes=[pltpu.VMEM((D_ATT, WINDOW), F32)],
        compiler_params=_cparams(("parallel",)),
    )(q, g, kv, kv, sinks)


def _swa_bwd(dy, q, g, kv, o, lse, pos, inv, sinks):
    L = q.shape[0]
    nb = L // WINDOW
    scale = ATT_HD ** -0.5

    def body(dy_ref, q_ref, g_ref, kvc_ref, kvp_ref, o_ref, lse_ref, posc_ref, posp_ref, inv_ref, sink_ref,
             dq_ref, dg_ref, dkv_ref, dsink_ref, carry, dqbuf, dkbuf, dvbuf):
        n = pl.program_id(0)

        @pl.when(n == 0)
        def _():
            dsink_ref[...] = jnp.zeros_like(dsink_ref)

        @pl.when(n < nb)
        def _():
            tc = _rope_tables(posc_ref, inv_ref)
            tp = _rope_tables(posp_ref, inv_ref)
            kk = jnp.concatenate([kvp_ref[:, 0:D_KV], kvc_ref[:, 0:D_KV]], axis=0)
            vv = jnp.concatenate([kvp_ref[:, D_KV:2 * D_KV], kvc_ref[:, D_KV:2 * D_KV]], axis=0)
            valid = _swa_mask_t(n == 0)
            qv = q_ref[...]
            gv = g_ref[...]
            sg = _sigmoid(gv)
            dyv = dy_ref[...]
            ov = o_ref[...]
            dg_ref[...] = _bf(dyv * ov * (sg * (1.0 + gv * (1.0 - sg))))
            do = dyv * (gv * sg)
            dod = do * ov
            ones = jnp.ones((8, ATT_HD), BF16)
            lane16 = _iota2((1, ATT_QH), 1)
            dsink = jnp.zeros((1, ATT_QH), F32)
            for j in range(ATT_KVH):
                js = slice(ATT_HD * j, ATT_HD * (j + 1))
                kj = kk[:, js]
                vj = vv[:, js]
                qs = _stack_heads(qv, j)
                dos = _bf(_stack_heads(do, j))
                hi, lo = _hi_lo(_stack_heads(dod, j))
                delta = (_dot_nt(ones, hi) + _dot_nt(ones, lo))[0:1]
                lse = _head_rows(lse_ref, j)
                st = _dot_nt(kj, qs) * scale
                pt = jnp.exp(jnp.where(valid, st, NEG_BIG) - lse)
                dst = _bf(pt * (_dot_nt(vj, dos) - delta))
                dqt = _dot_tn(kj, dst) * scale
                dkbuf[:, js] = _dot(dst, qs) * scale
                dvbuf[:, js] = _dot(_bf(pt), dos)
                sd = jnp.exp(_head_rows(sink_ref, j) - lse) * delta
                for r in range(ATT_R):
                    h = j * ATT_R + r
                    ls = slice(WINDOW * r, WINDOW * (r + 1))
                    dqbuf[ATT_HD * h:ATT_HD * (h + 1), :] = dqt[:, ls]
                    dsink = dsink - jnp.sum(sd[:, ls], axis=1, keepdims=True) * (lane16 == h).astype(F32)
            dsink_ref[...] += dsink
            dq_ref[...] = _bf(_rope_t(dqbuf[...].T, tc))
            dkp = _rope_t(dkbuf[0:WINDOW, :], tp)
            dkc = _rope_t(dkbuf[WINDOW:2 * WINDOW, :], tc)

            @pl.when(n > 0)
            def _():
                dkv_ref[:, 0:D_KV] = _bf(carry[:, 0:D_KV] + dkp)
                dkv_ref[:, D_KV:2 * D_KV] = _bf(carry[:, D_KV:2 * D_KV] + dvbuf[0:WINDOW, :])

            carry[:, 0:D_KV] = dkc
            carry[:, D_KV:2 * D_KV] = dvbuf[WINDOW:2 * WINDOW, :]

        @pl.when(n == nb)
        def _():
            dkv_ref[...] = _bf(carry[...])

    last = nb - 1
    cur = lambda wd: pl.BlockSpec((WINDOW, wd), lambda n: (jnp.minimum(n, last), 0))
    prv = lambda wd: pl.BlockSpec((WINDOW, wd), lambda n: (jnp.maximum(jnp.minimum(n, last) - 1, 0), 0))
    return pl.pallas_call(
        body, name="swa_bwd", grid=(nb + 1,),
        in_specs=[cur(D_ATT), cur(D_ATT), cur(D_ATT), cur(2 * D_KV), prv(2 * D_KV), cur(D_ATT),
                  pl.BlockSpec((ATT_QH, WINDOW), lambda n: (0, jnp.minimum(n, last))), cur(1), prv(1),
                  pl.BlockSpec((1, 2 * ATT_HD), lambda n: (0, 0)), pl.BlockSpec((1, ATT_QH), lambda n: (0, 0))],
        out_specs=[cur(D_ATT), cur(D_ATT),
                   pl.BlockSpec((WINDOW, 2 * D_KV), lambda n: (jnp.maximum(n - 1, 0), 0)),
                   pl.BlockSpec((1, ATT_QH), lambda n: (0, 0))],
        out_shape=[jax.ShapeDtypeStruct((L, D_ATT), BF16), jax.ShapeDtypeStruct((L, D_ATT), BF16),
                   jax.ShapeDtypeStruct((L, 2 * D_KV), BF16), jax.ShapeDtypeStruct((1, ATT_QH), F32)],
        scratch_shapes=[pltpu.VMEM((WINDOW, 2 * D_KV), F32), pltpu.VMEM((D_ATT, WINDOW), F32),
                        pltpu.VMEM((2 * WINDOW, D_KV), F32), pltpu.VMEM((2 * WINDOW, D_KV), F32)],
        compiler_params=_cparams(("arbitrary",)),
    )(dy, q, g, kv, kv, o, lse, pos, pos, inv, sinks)


def _out_ln_loss(y_ssd, y_att, x, target, w_out, ln_g, ln_b):
    L = x.shape[0]
    tm = ROW_TILE
    inv_d = 1.0 / D_MODEL

    def body(ys_ref, ya_ref, x_ref, t_ref, w_ref, g_ref, b_ref, dr_ref, dys_ref, dya_ref, loss_ref, gg_ref, gb_ref):
        i = pl.program_id(0)

        @pl.when(i == 0)
        def _():
            loss_ref[...] = jnp.zeros_like(loss_ref)
            gg_ref[...] = jnp.zeros_like(gg_ref)
            gb_ref[...] = jnp.zeros_like(gb_ref)

        h = _dot(_bf(ys_ref[...]), w_ref[0:D_SSD, :]) + _dot(_bf(ya_ref[...]), w_ref[D_SSD:D_MIX, :])
        r = ALPHA * x_ref[...] + h
        mu = jnp.mean(r, axis=-1, keepdims=True)
        xc = r - mu
        rstd = lax.rsqrt(jnp.mean(xc * xc, axis=-1, keepdims=True) + LN_EPS)
        xhat = xc * rstd
        gam = g_ref[...]
        diff = xhat * gam + b_ref[...] - t_ref[...]
        part = jnp.sum(jnp.sum(diff * diff, axis=-1, keepdims=True), axis=0, keepdims=True)
        loss_ref[...] += (0.5 * inv_d) * part
        dout = diff * inv_d
        gg_ref[...] += jnp.sum(dout * xhat, axis=0, keepdims=True)
        gb_ref[...] += jnp.sum(dout, axis=0, keepdims=True)
        dxh = dout * gam
        dr = rstd * (dxh - jnp.mean(dxh, axis=-1, keepdims=True) - xhat * jnp.mean(dxh * xhat, axis=-1, keepdims=True))
        dr_ref[...] = dr
        drb = _bf(dr)
        dys_ref[...] = _dot_nt(drb, w_ref[0:D_SSD, :])
        dya_ref[...] = _dot_nt(drb, w_ref[D_SSD:D_MIX, :])

    row = pl.BlockSpec((tm, D_MODEL), lambda i: (i, 0))
    vec = pl.BlockSpec((1, D_MODEL), lambda i: (0, 0))
    return pl.pallas_call(
        body, name="out_ln_loss", grid=(L // tm,),
        in_specs=[row, row, row, row, pl.BlockSpec((D_MIX, D_MODEL), lambda i: (0, 0), pipeline_mode=pl.Buffered(1)), vec, vec],
        out_specs=[row, row, row, pl.BlockSpec((1, 128), lambda i: (0, 0)), vec, vec],
        out_shape=[jax.ShapeDtypeStruct((L, D_MODEL), F32)] * 3 + [jax.ShapeDtypeStruct((1, 128), F32)]
        + [jax.ShapeDtypeStruct((1, D_MODEL), F32)] * 2,
        compiler_params=_cparams(("arbitrary",)),
    )(y_ssd, y_att, x, target, w_out, ln_g, ln_b)


def _local_step(x, pos, target, w, get_w_out, token, conv_w, conv_b, dt_bias, a_log, d_skip, norm_w, sinks, ln_g, ln_b):
    inv8 = ROPE_THETA ** (-jnp.arange(0, ROPE_DIM, 2, dtype=F32) / ROPE_DIM)
    inv = jnp.tile(jnp.concatenate([inv8, inv8, jnp.zeros((ATT_HD - ROPE_DIM,), F32)]), 2).reshape(1, 2 * ATT_HD)
    inv = inv + token

    z, g, q, xbc, kv, dtp, xb = _in_proj(x, w, pos, inv)
    y_ssd, y_pre, prev = _ssd_fwd(z, xbc, dtp, conv_w, conv_b, dt_bias, a_log, d_skip, norm_w)
    y_att, o, lse = _swa_fwd(q, g, kv, sinks)
    w_out = get_w_out(lse)
    dr, dy_ssd, dy_att, loss, g_ln_g, g_ln_b = _out_ln_loss(y_ssd, y_att, x, target, w_out, ln_g, ln_b)
    gw_out_ssd, gw_out_att = _matmuls_tn([y_ssd, y_att], dr, "gw_out", out_dtype=BF16)
    slabs = jnp.concatenate([gw_out_ssd, gw_out_att], axis=0).reshape(N_CHIPS, W_OUT_ROWS, D_MODEL)
    w_out_red = _reduce_w_out_start(slabs, loss)
    inv = inv + w_out_red[16][0:1, :]
    dq, dg, dkv, g_sinks = _swa_bwd(dy_att, q, g, kv, o, lse, pos, inv, sinks)
    dz, dxbc, ddt, g_conv_w, g_conv_b, g_dt_bias, g_a_log, g_d_skip, g_norm_w = _ssd_bwd(
        dy_ssd, z, y_pre, xbc, dtp, prev, conv_w, conv_b, dt_bias, a_log, d_skip, norm_w)
    gw_z, gw_g, gw_q = _matmuls_tn([dz, dg, dq], xb, "gw_zgq")
    gw_xbc, gw_kv, gw_dt = _matmuls_tn([dxbc, dkv, ddt], xb, "gw_xbc_kv_dt")
    gw_in = jnp.concatenate([gw_z, gw_xbc, gw_dt[0:SSD_HEADS], gw_q, gw_kv, gw_g], axis=0)
    small = dict(conv_w=g_conv_w, conv_b=g_conv_b, dt_bias=g_dt_bias, a_log=g_a_log, d_skip=g_d_skip,
                 ssd_norm_w=g_norm_w, attn_sinks=g_sinks, ln_g=g_ln_g, ln_b=g_ln_b)
    return loss, (dr, dz, dg, dq, dxbc, dkv, ddt, w), gw_in, w_out_red, small


def _mesh_pos():
    return lax.axis_index("x"), lax.axis_index("y"), lax.axis_index("c")


def _gather_weights(w_in_s, conv_w_s):
    hr = w_in_s.shape[0] // 2
    qa = 336
    quarters = ((0, qa), (qa, hr - qa))

    def body(win_ref, cw_ref, owin_ref, ocw_ref, send_sems, recv_sems, small_send, small_recv, local_sems):
        x, y, c = _mesh_pos()
        me = 2 * x + y
        sibling = (x, y, 1 - c)
        xn, yn, dg = (1 - x, y), (x, 1 - y), (1 - x, 1 - y)
        chips = [xn, yn, dg]
        locals_ = [pltpu.make_async_copy(cw_ref, ocw_ref.at[me], local_sems.at[0])]
        for cp in locals_:
            cp.start()
        started = []

        def piece(ref, chip, half, q):
            off, n = quarters[q]
            return ref.at[2 * chip[0] + chip[1]].at[pl.ds(half * hr + off, n), :]

        def mine(q):
            off, n = quarters[q]
            return win_ref.at[pl.ds(c * hr + off, n), :]

        def copy(src, dst, k, to):
            return pltpu.make_async_remote_copy(src_ref=src, dst_ref=dst, send_sem=send_sems.at[k], recv_sem=recv_sems.at[k],
                                                device_id=to, device_id_type=MESH)

        def go(cp):
            cp.start()
            started.append(cp)

        go(copy(mine(0), piece(owin_ref, (x, y), c, 0), 0, (*xn, c)))
        go(copy(mine(1), piece(owin_ref, (x, y), c, 1), 2, (*yn, c)))
        go(copy(mine(1), piece(owin_ref, (x, y), c, 1), 1, (*xn, c)))
        go(copy(mine(0), piece(owin_ref, (x, y), c, 0), 3, (*yn, c)))
        for j, (px, py) in enumerate(chips):
            cp = pltpu.make_async_remote_copy(
                src_ref=cw_ref, dst_ref=ocw_ref.at[me], send_sem=small_send.at[j], recv_sem=small_recv.at[j],
                device_id=(px, py, c), device_id_type=MESH)
            go(cp)
        arrivals = [(0, xn, 0, (4, (*yn, c))), (2, yn, 1, (5, (*xn, c))), (1, xn, 1, None), (3, yn, 0, None),
                    (4, dg, 0, None), (5, dg, 1, None)]
        for n, (k, chip, q, onward) in enumerate(arrivals):
            blk = piece(owin_ref, chip, c, q)
            copy(blk, blk, k, sibling).wait_recv()
            if onward is not None:
                go(copy(blk, blk, onward[0], onward[1]))
            go(copy(blk, blk, 6 + n, sibling))
        for n, (k, chip, q, onward) in enumerate(arrivals):
            blk = piece(owin_ref, chip, 1 - c, q)
            copy(blk, blk, 6 + n, sibling).wait_recv()
        for j in range(3):
            pltpu.make_async_remote_copy(
                src_ref=cw_ref, dst_ref=ocw_ref.at[me], send_sem=small_send.at[j], recv_sem=small_recv.at[j],
                device_id=sibling, device_id_type=MESH).wait_recv()
        for cp in started:
            cp.wait_send()
        for cp in locals_:
            cp.wait()

    any_spec = pl.BlockSpec(memory_space=pl.ANY)
    return pl.pallas_call(
        body, name="gather_weights",
        in_specs=[any_spec] * 2, out_specs=[any_spec] * 2,
        out_shape=[jax.ShapeDtypeStruct((N_CHIPS,) + a.shape, a.dtype) for a in (w_in_s, conv_w_s)],
        scratch_shapes=[pltpu.SemaphoreType.DMA((12,)), pltpu.SemaphoreType.DMA((12,)),
                        pltpu.SemaphoreType.DMA((3,)), pltpu.SemaphoreType.DMA((3,)), pltpu.SemaphoreType.DMA((3,))],
    )(w_in_s, conv_w_s)


_HBM = pl.BlockSpec(memory_space=pltpu.HBM)
_SEM = pl.BlockSpec(memory_space=pltpu.SEMAPHORE)
_EFFECT = pltpu.SideEffectType.DATAFLOW_SIDE_EFFECTING


def _gather_w_out_start(w_out_s, after):
    def body(src_ref, land_ref, after_ref, s0, s1, s2, r0, r1, r2, src_thru, land_thru, token):
        x, y, c = _mesh_pos()
        me = 2 * x + y
        chips = [(1 - x, y), (x, 1 - y), (1 - x, 1 - y)]
        for (px, py), s, r in zip(chips, (s0, s1, s2), (r0, r1, r2)):
            pltpu.make_async_remote_copy(src_ref=src_ref, dst_ref=land_ref.at[me], send_sem=s, recv_sem=r,
                                         device_id=(px, py, c), device_id_type=MESH).start()
        token[...] = jnp.zeros_like(token)

    sem = pltpu.SemaphoreType.DMA(())
    land = lax.empty((N_CHIPS,) + w_out_s.shape, w_out_s.dtype)
    return pl.pallas_call(
        body, name="gather_w_out_start",
        out_shape=(sem,) * 6 + (pltpu.HBM(w_out_s.shape, w_out_s.dtype), pltpu.HBM(land.shape, land.dtype),
                                jax.ShapeDtypeStruct((8, 128), F32)),
        in_specs=(_HBM, _HBM, pl.BlockSpec(memory_space=pl.ANY)),
        out_specs=(_SEM,) * 6 + (_HBM, _HBM, pl.BlockSpec(memory_space=pltpu.VMEM)),
        input_output_aliases={0: 6, 1: 7},
        compiler_params=pltpu.CompilerParams(has_side_effects=_EFFECT),
    )(pltpu.with_memory_space_constraint(w_out_s, pltpu.HBM), pltpu.with_memory_space_constraint(land, pltpu.HBM), after)


def _gather_w_out_wait(sems, src_thru, land_thru, after):
    def body(src_ref, land_ref, s0, s1, s2, r0, r1, r2, after_ref, src_dead, got_ref):
        x, y, c = _mesh_pos()
        chips = [(1 - x, y), (x, 1 - y), (1 - x, 1 - y)]
        for (px, py), s, r in zip(chips, (s0, s1, s2), (r0, r1, r2)):
            cp = pltpu.make_async_remote_copy(src_ref=src_ref, dst_ref=land_ref.at[2 * px + py], send_sem=s, recv_sem=r,
                                              device_id=(px, py, c), device_id_type=MESH)
            cp.wait_send()
            cp.wait_recv()

    return pl.pallas_call(
        body, name="gather_w_out_wait",
        out_shape=(pltpu.HBM(src_thru.shape, src_thru.dtype), pltpu.HBM(land_thru.shape, land_thru.dtype)),
        in_specs=(_HBM, _HBM) + (_SEM,) * 6 + (pl.BlockSpec(memory_space=pl.ANY),),
        out_specs=(_HBM, _HBM), input_output_aliases={0: 0, 1: 1},
        compiler_params=pltpu.CompilerParams(has_side_effects=_EFFECT),
    )(src_thru, land_thru, *sems, after)[1]


def _pair_start(gw_in, after):
    hr = gw_in.shape[1] // 2

    def body(src_ref, land_ref, after_ref, *refs):
        x, y, c = _mesh_pos()
        for j in range(N_CHIPS):
            pltpu.make_async_remote_copy(
                src_ref=src_ref.at[j, pl.ds((1 - c) * hr, hr), :], dst_ref=land_ref.at[j], send_sem=refs[j],
                recv_sem=refs[N_CHIPS + j], device_id=(x, y, 1 - c), device_id_type=MESH).start()
        refs[10][...] = jnp.zeros_like(refs[10])

    sem = pltpu.SemaphoreType.DMA(())
    land = lax.empty((N_CHIPS, hr, D_MODEL), F32)
    return pl.pallas_call(
        body, name="pair_start",
        out_shape=(sem,) * 8 + (pltpu.HBM(gw_in.shape, F32), pltpu.HBM(land.shape, F32), jax.ShapeDtypeStruct((8, 128), F32)),
        in_specs=(_HBM, _HBM, pl.BlockSpec(memory_space=pl.ANY)),
        out_specs=(_SEM,) * 8 + (_HBM, _HBM, pl.BlockSpec(memory_space=pltpu.VMEM)),
        input_output_aliases={0: 8, 1: 9},
        compiler_params=pltpu.CompilerParams(has_side_effects=_EFFECT),
    )(pltpu.with_memory_space_constraint(gw_in, pltpu.HBM), pltpu.with_memory_space_constraint(land, pltpu.HBM), after)


def _pair_wait(sems, gw_thru, land_thru, after):
    hr = land_thru.shape[1]

    def body(src_ref, land_ref, *refs):
        x, y, c = _mesh_pos()
        for j in range(N_CHIPS):
            cp = pltpu.make_async_remote_copy(
                src_ref=src_ref.at[j, pl.ds((1 - c) * hr, hr), :], dst_ref=land_ref.at[j], send_sem=refs[j],
                recv_sem=refs[N_CHIPS + j], device_id=(x, y, 1 - c), device_id_type=MESH)
            cp.wait_send()
            cp.wait_recv()

    return pl.pallas_call(
        body, name="pair_wait",
        out_shape=(pltpu.HBM(gw_thru.shape, F32), pltpu.HBM(land_thru.shape, F32)),
        in_specs=(_HBM, _HBM) + (_SEM,) * 8 + (pl.BlockSpec(memory_space=pl.ANY),),
        out_specs=(_HBM, _HBM), input_output_aliases={0: 0, 1: 1},
        compiler_params=pltpu.CompilerParams(has_side_effects=_EFFECT),
    )(gw_thru, land_thru, *sems, after)


def _chip_start(s_in, after):
    def body(src_ref, land_ref, after_ref, *refs):
        x, y, c = _mesh_pos()
        me = 2 * x + y
        for j, (px, py) in enumerate([(1 - x, y), (x, 1 - y), (1 - x, 1 - y)]):
            pltpu.make_async_remote_copy(
                src_ref=src_ref.at[2 * px + py], dst_ref=land_ref.at[me], send_sem=refs[j], recv_sem=refs[3 + j],
                device_id=(px, py, c), device_id_type=MESH).start()
        refs[8][...] = jnp.zeros_like(refs[8])

    sem = pltpu.SemaphoreType.DMA(())
    land = lax.empty(s_in.shape, s_in.dtype)
    return pl.pallas_call(
        body, name="chip_start",
        out_shape=(sem,) * 6 + (pltpu.HBM(s_in.shape, s_in.dtype), pltpu.HBM(land.shape, land.dtype),
                                jax.ShapeDtypeStruct((8, 128), F32)),
        in_specs=(_HBM, _HBM, pl.BlockSpec(memory_space=pl.ANY)),
        out_specs=(_SEM,) * 6 + (_HBM, _HBM, pl.BlockSpec(memory_space=pltpu.VMEM)),
        input_output_aliases={0: 6, 1: 7},
        compiler_params=pltpu.CompilerParams(has_side_effects=_EFFECT),
    )(pltpu.with_memory_space_constraint(s_in, pltpu.HBM), pltpu.with_memory_space_constraint(land, pltpu.HBM), after)


def _chip_wait(sems, s_thru, land_thru, after):
    def body(src_ref, land_ref, *refs):
        x, y, c = _mesh_pos()
        for j, (px, py) in enumerate([(1 - x, y), (x, 1 - y), (1 - x, 1 - y)]):
            cp = pltpu.make_async_remote_copy(
                src_ref=src_ref.at[2 * px + py], dst_ref=land_ref.at[2 * px + py], send_sem=refs[j], recv_sem=refs[3 + j],
                device_id=(px, py, c), device_id_type=MESH)
            cp.wait_send()
            cp.wait_recv()

    return pl.pallas_call(
        body, name="chip_wait",
        out_shape=(pltpu.HBM(s_thru.shape, s_thru.dtype), pltpu.HBM(land_thru.shape, land_thru.dtype)),
        in_specs=(_HBM, _HBM) + (_SEM,) * 6 + (pl.BlockSpec(memory_space=pl.ANY),),
        out_specs=(_HBM, _HBM), input_output_aliases={0: 0, 1: 1},
        compiler_params=pltpu.CompilerParams(has_side_effects=_EFFECT),
    )(s_thru, land_thru, *sems, after)


def _pair_share(h_in, small):
    def body(hin_ref, sm_ref, rin_ref, slots_ref, send_sems, recv_sems, small_send, small_recv, local_sem):
        x, y, c = _mesh_pos()
        dev = 4 * x + 2 * y + c
        mine = pltpu.make_async_copy(sm_ref, slots_ref.at[dev], local_sem)
        mine.start()
        share = pltpu.make_async_remote_copy(
            src_ref=hin_ref, dst_ref=rin_ref, send_sem=send_sems.at[0], recv_sem=recv_sems.at[0],
            device_id=(x, y, 1 - c), device_id_type=MESH)
        share.start()
        started = []
        for k in range(1, 8):
            peer = (x ^ ((k >> 2) & 1), y ^ ((k >> 1) & 1), c ^ (k & 1))
            cp = pltpu.make_async_remote_copy(
                src_ref=sm_ref, dst_ref=slots_ref.at[dev], send_sem=small_send.at[k - 1], recv_sem=small_recv.at[k - 1],
                device_id=peer, device_id_type=MESH)
            cp.start()
            started.append(cp)
        share.wait()
        for k in range(1, 8):
            pltpu.make_async_remote_copy(
                src_ref=sm_ref, dst_ref=slots_ref.at[dev], send_sem=small_send.at[k - 1], recv_sem=small_recv.at[k - 1],
                device_id=(x, y, 1 - c), device_id_type=MESH).wait_recv()
        for cp in started:
            cp.wait_send()
        mine.wait()

    any_spec = pl.BlockSpec(memory_space=pl.ANY)
    return pl.pallas_call(
        body, name="pair_share",
        in_specs=[any_spec] * 2, out_specs=[any_spec] * 2,
        out_shape=[jax.ShapeDtypeStruct(h_in.shape, F32), jax.ShapeDtypeStruct((8,) + small.shape, F32)],
        scratch_shapes=[pltpu.SemaphoreType.DMA((1,)), pltpu.SemaphoreType.DMA((1,)),
                        pltpu.SemaphoreType.DMA((7,)), pltpu.SemaphoreType.DMA((7,)), pltpu.SemaphoreType.DMA],
    )(h_in, small)


def _reduce_w_out_start(slabs, after):
    def body(src_ref, land_ref, after_ref, *refs):
        x, y, c = _mesh_pos()
        me = 4 * x + 2 * y + c
        for k in range(1, 8):
            px, py, pc = x ^ ((k >> 2) & 1), y ^ ((k >> 1) & 1), c ^ (k & 1)
            pltpu.make_async_remote_copy(src_ref=src_ref.at[2 * px + py], dst_ref=land_ref.at[me], send_sem=refs[k - 1],
                                         recv_sem=refs[6 + k], device_id=(px, py, pc), device_id_type=MESH).start()
        refs[16][...] = jnp.zeros_like(refs[16])

    sem = pltpu.SemaphoreType.DMA(())
    land = lax.empty((8,) + slabs.shape[1:], slabs.dtype)
    return pl.pallas_call(
        body, name="reduce_w_out_start",
        out_shape=(sem,) * 14 + (pltpu.HBM(slabs.shape, slabs.dtype), pltpu.HBM(land.shape, land.dtype),
                                 jax.ShapeDtypeStruct((8, 128), F32)),
        in_specs=(_HBM, _HBM, pl.BlockSpec(memory_space=pl.ANY)),
        out_specs=(_SEM,) * 14 + (_HBM, _HBM, pl.BlockSpec(memory_space=pltpu.VMEM)),
        input_output_aliases={0: 14, 1: 15},
        compiler_params=pltpu.CompilerParams(has_side_effects=_EFFECT),
    )(pltpu.with_memory_space_constraint(slabs, pltpu.HBM), pltpu.with_memory_space_constraint(land, pltpu.HBM), after)


def _reduce_w_out_wait(sems, slabs_thru, land_thru, after):
    def body(src_ref, land_ref, *refs):
        x, y, c = _mesh_pos()
        for k in range(1, 8):
            px, py, pc = x ^ ((k >> 2) & 1), y ^ ((k >> 1) & 1), c ^ (k & 1)
            cp = pltpu.make_async_remote_copy(
                src_ref=src_ref.at[2 * px + py], dst_ref=land_ref.at[4 * px + 2 * py + pc], send_sem=refs[k - 1],
                recv_sem=refs[6 + k], device_id=(px, py, pc), device_id_type=MESH)
            cp.wait_send()
            cp.wait_recv()

    return pl.pallas_call(
        body, name="reduce_w_out_wait",
        out_shape=(pltpu.HBM(slabs_thru.shape, slabs_thru.dtype), pltpu.HBM(land_thru.shape, land_thru.dtype)),
        in_specs=(_HBM, _HBM) + (_SEM,) * 14 + (pl.BlockSpec(memory_space=pl.ANY),),
        out_specs=(_HBM, _HBM), input_output_aliases={0: 0, 1: 1},
        compiler_params=pltpu.CompilerParams(has_side_effects=_EFFECT),
    )(slabs_thru, land_thru, *sems, after)


def _pair_add(g, recv, core, name):
    _, rows, C = recv.shape
    tc = 256

    def body(core_ref, g_ref, r_ref, o_ref):
        o_ref[...] = _bf(g_ref[...] + r_ref[...])

    spec = pl.BlockSpec((1, rows, tc), lambda j, i, core: (j, 0, i))
    return pl.pallas_call(
        body, name=name,
        grid_spec=pltpu.PrefetchScalarGridSpec(
            num_scalar_prefetch=1, grid=(N_CHIPS, C // tc),
            in_specs=[pl.BlockSpec((1, rows, tc), lambda j, i, core: (j, core[0], i)), spec], out_specs=spec),
        out_shape=jax.ShapeDtypeStruct((N_CHIPS, rows, C), BF16),
        compiler_params=_cparams(("parallel", "parallel")),
    )(core, g, recv)


def _chip_add(own, parts, chip, name):
    _, rows, C = parts.shape
    tc = 256

    def body(chip_ref, own_ref, r0, r1, r2, r3, o_ref):
        acc = None
        for j, r in enumerate((r0, r1, r2, r3)):
            term = jnp.where(chip_ref[0] == j, own_ref[0], r[0]).astype(F32)
            acc = term if acc is None else acc + term
        o_ref[...] = acc

    def slab(j):
        return pl.BlockSpec((1, rows, tc), lambda i, chip: (jnp.where(chip[0] == j, (j + 1) % N_CHIPS, j), 0, i))

    return pl.pallas_call(
        body, name=name,
        grid_spec=pltpu.PrefetchScalarGridSpec(
            num_scalar_prefetch=1, grid=(C // tc,),
            in_specs=[pl.BlockSpec((1, rows, tc), lambda i, chip: (chip[0], 0, i))] + [slab(j) for j in range(N_CHIPS)],
            out_specs=pl.BlockSpec((rows, tc), lambda i, chip: (0, i))),
        out_shape=jax.ShapeDtypeStruct((rows, C), F32),
        compiler_params=_cparams(("parallel",)),
    )(chip, own, parts, parts, parts, parts)


def _adamw_math(w, g, m, v):
    m = ADAM_B1 * m + (1.0 - ADAM_B1) * g
    v = ADAM_B2 * v + (1.0 - ADAM_B2) * (g * g)
    m_hat = m / (1.0 - ADAM_B1 ** ADAM_STEP)
    v_hat = v / (1.0 - ADAM_B2 ** ADAM_STEP)
    delta = -ADAM_LR * (m_hat / (jnp.sqrt(v_hat) + ADAM_EPS) + ADAM_WD * w)
    return delta, m, v


def _adamw_rows(w, g_own, g_sib, m, v, core, name):
    R, C = w.shape[0], w.shape[-1]
    rows = g_own.shape[0]
    step = 256
    chunks = [(r, min(step, R - r)) for r in range(0, R, step)]
    sub = 64

    def body(core_ref, w_hbm, go_hbm, gs_hbm, m_hbm, v_hbm, d_hbm, nm_hbm, nv_hbm, g_hbm,
             wbuf, mbuf, vbuf, gbuf, dbuf, nmbuf, nvbuf, in_sems, g_sems, out_sems):
        c = core_ref[0]
        flat = lambda ref: ref.at[:, 0, :]
        g_in = [pltpu.make_async_copy(go_hbm, gbuf.at[pl.ds(pl.multiple_of(c * rows, 8), rows), :], g_sems.at[0]),
                pltpu.make_async_copy(gs_hbm, gbuf.at[pl.ds(pl.multiple_of((1 - c) * rows, 8), rows), :], g_sems.at[1])]
        for cp in g_in:
            cp.start()
        loads = []
        for k, (r0, n) in enumerate(chunks):
            cps = [pltpu.make_async_copy(flat(src).at[pl.ds(r0, n), :], dst.at[pl.ds(r0, n), :], in_sems.at[a, k])
                   for a, (src, dst) in enumerate(((w_hbm, wbuf), (m_hbm, mbuf), (v_hbm, vbuf)))]
            for cp in cps:
                cp.start()
            loads.append(cps)
        for cp in g_in:
            cp.wait()
        stores = []
        for k, (r0, n) in enumerate(chunks):
            for cp in loads[k]:
                cp.wait()

            def update(rs):
                g = gbuf[rs, :]
                dl, nm, nv = _adamw_math(wbuf[rs, :], g, mbuf[rs, :], vbuf[rs, :])
                dbuf[rs, :] = dl
                nmbuf[rs, :] = nm
                nvbuf[rs, :] = nv

            if n % sub == 0:
                def block(i, carry, r0=r0):
                    update(pl.ds(pl.multiple_of(r0 + i * sub, 8), sub))
                    return carry
                lax.fori_loop(0, n // sub, block, 0)
            else:
                update(pl.ds(r0, n))
            cps = [pltpu.make_async_copy(src.at[pl.ds(r0, n), :], flat(dst).at[pl.ds(r0, n), :], out_sems.at[a, k])
                   for a, (src, dst) in enumerate(((dbuf, d_hbm), (nmbuf, nm_hbm), (nvbuf, nv_hbm), (gbuf, g_hbm)))]
            for cp in cps:
                cp.start()
            stores += cps
        for cp in stores:
            cp.wait()

    any_spec = pl.BlockSpec(memory_space=pl.ANY)
    dense = pltpu.VMEM((R, C), F32)
    return pl.pallas_call(
        body, name=name,
        grid_spec=pltpu.PrefetchScalarGridSpec(
            num_scalar_prefetch=1, grid=(1,),
            in_specs=[any_spec] * 5, out_specs=[any_spec] * 4,
            scratch_shapes=[dense, dense, dense, pltpu.VMEM((2 * rows, C), F32), dense, dense, dense,
                            pltpu.SemaphoreType.DMA((3, len(chunks))), pltpu.SemaphoreType.DMA((2,)),
                            pltpu.SemaphoreType.DMA((4, len(chunks)))]),
        out_shape=[jax.ShapeDtypeStruct(w.shape, F32)] * 4,
        compiler_params=_cparams(),
    )(core, w, g_own, g_sib, m, v)


def _adamw_sum8(w, slabs, land, m, v, ids, name):
    R, C = w.shape
    tc = 128

    def body(ids_ref, w_ref, own_ref, *refs):
        lrefs, (m_ref, v_ref, d_ref, nm_ref, nv_ref, g_ref) = refs[:8], refs[8:]
        g = None
        for d, l_ref in enumerate(lrefs):
            term = jnp.where(ids_ref[0] == d, own_ref[0], l_ref[0]).astype(F32)
            g = term if g is None else g + term
        dl, nm, nv = _adamw_math(w_ref[...], g, m_ref[...], v_ref[...])
        d_ref[...] = dl
        nm_ref[...] = nm
        nv_ref[...] = nv
        g_ref[...] = g

    def slot(d):
        return pl.BlockSpec((1, R, tc), lambda i, ids: (jnp.where(ids[0] == d, (d + 1) % 8, d), 0, i))

    spec = pl.BlockSpec((R, tc), lambda i, ids: (0, i))
    return pl.pallas_call(
        body, name=name,
        grid_spec=pltpu.PrefetchScalarGridSpec(
            num_scalar_prefetch=1, grid=(C // tc,),
            in_specs=[spec, pl.BlockSpec((1, R, tc), lambda i, ids: (ids[1], 0, i))] + [slot(d) for d in range(8)]
            + [spec, spec],
            out_specs=[spec] * 4),
        out_shape=[jax.ShapeDtypeStruct((R, C), F32)] * 4,
        compiler_params=_cparams(("parallel",)),
    )(ids, w, slabs, *([land] * 8), m, v)


SMALL_NAMES = ("conv_b", "ssd_norm_w", "ln_g", "ln_b", "dt_bias", "a_log", "d_skip", "attn_sinks")
SMALL_FIELDS = ((4, 0, D_XBC), (5, 0, D_SSD), (6, 0, D_MODEL), (7, 0, D_MODEL), (5, 1024, SSD_HEADS), (5, 1152, SSD_HEADS),
                (5, 1280, SSD_HEADS), (5, 1408, ATT_QH))
LOSS_FIELD = (6, 1024, 128)
K_SMALL = D_XBC


def _pack_small(g_conv_w, vecs, loss):
    def body(cw_ref, *refs):
        o_ref = refs[-1]
        o_ref[...] = jnp.zeros_like(o_ref)
        o_ref[0:CONV_K, 0:D_XBC] = cw_ref[...]
        for v_ref, (row, off, n) in zip(refs[:-2], SMALL_FIELDS):
            o_ref[row:row + 1, off:off + n] = v_ref[...]
        o_ref[LOSS_FIELD[0]:LOSS_FIELD[0] + 1, LOSS_FIELD[1]:LOSS_FIELD[1] + LOSS_FIELD[2]] = refs[-2][...]

    return pl.pallas_call(
        body, name="pack_small", out_shape=jax.ShapeDtypeStruct((8, K_SMALL), F32), compiler_params=_cparams(),
    )(g_conv_w, *vecs, loss)


def _adamw_small(slots, chip, conv_w, m_conv_w, v_conv_w, params, moms, vars_):
    n_vec = len(SMALL_NAMES)

    def body(chip_ref, s_ref, *refs):
        ins = refs[:3 * (n_vec + 1)]
        outs = refs[3 * (n_vec + 1):-1]
        tot_ref = refs[-1]
        tot = s_ref[0]
        for d in range(1, 8):
            tot = tot + s_ref[d]
        outs[0][...] = tot[LOSS_FIELD[0]:LOSS_FIELD[0] + 1, LOSS_FIELD[1]:LOSS_FIELD[1] + 1]
        off = pl.multiple_of(chip_ref[0] * CONV_COLS, 128)
        tot_ref[...] = tot
        grads = [tot_ref[0:CONV_K, pl.ds(off, CONV_COLS)]]
        grads += [tot[row:row + 1, o:o + n] for row, o, n in SMALL_FIELDS]
        for k, g in enumerate(grads):
            w_ref, m_ref, v_ref = ins[3 * k:3 * k + 3]
            full = (0,) if k == 0 else (Ellipsis,)
            d, nm, nv = _adamw_math(w_ref[full], g, m_ref[full], v_ref[full])
            for o_ref, val in zip(outs[1 + 4 * k:5 + 4 * k], (g, d, nm, nv)):
                o_ref[full] = val

    args = [conv_w, m_conv_w, v_conv_w]
    for w, m, v in zip(params, moms, vars_):
        args += [w, m, v]
    shapes = [jax.ShapeDtypeStruct((1, 1), F32)] + [jax.ShapeDtypeStruct(conv_w.shape, F32)] * 4
    for w in params:
        shapes += [jax.ShapeDtypeStruct(w.shape, F32)] * 4
    vmem = pl.BlockSpec(memory_space=pltpu.VMEM)
    return pl.pallas_call(
        body, name="adamw_small",
        grid_spec=pltpu.PrefetchScalarGridSpec(
            num_scalar_prefetch=1, grid=(1,),
            in_specs=[pl.BlockSpec(slots.shape, lambda i, chip: (0, 0, 0))] + [vmem] * len(args),
            out_specs=[vmem] * len(shapes), scratch_shapes=[pltpu.VMEM((8, K_SMALL), F32)]),
        out_shape=shapes, compiler_params=_cparams(),
    )(chip, slots, *args)


def kernel(x, positions, w_in, conv_w, conv_b, dt_bias, a_log, d_skip, ssd_norm_w, attn_sinks, w_out, ln_g, ln_b, loss_target, m_w_in, m_conv_w, m_conv_b, m_dt_bias, m_a_log, m_d_skip, m_ssd_norm_w, m_attn_sinks, m_w_out, m_ln_g, m_ln_b, v_w_in, v_conv_w, v_conv_b, v_dt_bias, v_a_log, v_d_skip, v_ssd_norm_w, v_attn_sinks, v_w_out, v_ln_g, v_ln_b):
    mx, my, mc = _mesh_pos()
    chip = 2 * mx + my
    L = x.shape[1]

    conv_w_s8 = jnp.pad(conv_w[0], ((0, 8 - CONV_K), (0, 0)))
    pad_rows = ((0, SLAB_ROWS - W_IN_COLS), (0, 0))
    w_in_t = w_in[0].T
    w_in_b, w_out_b = jnp.pad(_bf(w_in_t), pad_rows), _bf(w_out[0])
    ag_in, ag_cw = _gather_weights(w_in_b, conv_w_s8)
    started = _gather_w_out_start(w_out_b, ag_cw)
    own = (jnp.arange(N_CHIPS) == chip)[:, None, None]

    def get_w_out(after):
        landed = _gather_w_out_wait(started[0:6], started[6], started[7], after)
        return jnp.where(own, w_out_b[None], landed).reshape(D_MIX, D_MODEL)

    ag_in = jnp.where(own, w_in_b[None], ag_in)
    w_full = jnp.concatenate([ag_in[j, 0:W_IN_COLS] for j in range(N_CHIPS)], axis=0)
    w = jnp.concatenate([
        w_full[O_Z:O_Z + D_SSD], w_full[O_G:O_G + D_ATT], w_full[O_Q:O_Q + D_ATT],
        w_full[O_XBC:O_XBC + D_XBC], w_full[O_K:O_K + 2 * D_KV], w_full[O_DT:O_DT + SSD_HEADS],
        jnp.zeros((DT_PAD - SSD_HEADS, D_MODEL), BF16)], axis=0)
    conv_w_full = jnp.concatenate([ag_cw[j, 0:CONV_K] for j in range(N_CHIPS)], axis=1)

    loss_part, gx_args, gw_in, w_out_red, small = _local_step(
        x[0], positions[0].reshape(L, 1), loss_target[0], w, get_w_out, started[8][0:1, :], conv_w_full,
        conv_b, dt_bias, a_log, d_skip, ssd_norm_w, attn_sinks, ln_g, ln_b)

    packed = _pack_small(small["conv_w"], [small[n] for n in SMALL_NAMES], loss_part)
    core_id = mc.reshape(1).astype(jnp.int32)
    chip_id = chip.reshape(1).astype(jnp.int32)
    ids = jnp.stack([4 * mx + 2 * my + mc, chip]).astype(jnp.int32)
    slabs = jnp.stack([jnp.pad(gw_in[W_IN_COLS * j:W_IN_COLS * (j + 1)], pad_rows) for j in range(N_CHIPS)])
    w_in_red = _pair_start(slabs, gw_in[0:8, 0:128])
    grad_x = _grad_x(*gx_args, w_in_red[10], 0)
    gw_in_slabs, recv_in = _pair_wait(w_in_red[0:8], w_in_red[8], w_in_red[9], grad_x[0:8, 0:128])
    s_in = _pair_add(gw_in_slabs, recv_in, core_id, "pair_add_in")
    chip_red = _chip_start(s_in, packed)
    grad_x = _grad_x(*gx_args, chip_red[8], 1, grad_x)
    own_slabs, landed = _reduce_w_out_wait(w_out_red[0:14], w_out_red[14], w_out_red[15], grad_x[L - 8:L, 0:128])
    out_t = _adamw_sum8(w_out[0], own_slabs, landed, m_w_out[0], v_w_out[0], ids, "adamw_w_out")
    d_w_out, nm_w_out, nv_w_out, g_w_out = [a[None] for a in out_t]
    s_in, r_in = _chip_wait(chip_red[0:6], chip_red[6], chip_red[7], out_t[0][0:8, 0:128])
    h_in = _chip_add(s_in, r_in, chip_id, "chip_add_in")
    sib_in, slots = _pair_share(h_in, packed)

    to_rows = lambda a: jnp.transpose(a, (2, 0, 1))
    in_t = _adamw_rows(to_rows(w_in), h_in, sib_in, to_rows(m_w_in), to_rows(v_w_in), core_id, "adamw_w_in")
    d_w_in, nm_w_in, nv_w_in, g_w_in = [jnp.transpose(a, (1, 2, 0)) for a in in_t]

    params = dict(conv_b=conv_b, ssd_norm_w=ssd_norm_w, ln_g=ln_g, ln_b=ln_b, dt_bias=dt_bias, a_log=a_log,
                  d_skip=d_skip, attn_sinks=attn_sinks)
    moms = dict(conv_b=m_conv_b, ssd_norm_w=m_ssd_norm_w, ln_g=m_ln_g, ln_b=m_ln_b, dt_bias=m_dt_bias, a_log=m_a_log,
                d_skip=m_d_skip, attn_sinks=m_attn_sinks)
    vars_ = dict(conv_b=v_conv_b, ssd_norm_w=v_ssd_norm_w, ln_g=v_ln_g, ln_b=v_ln_b, dt_bias=v_dt_bias, a_log=v_a_log,
                 d_skip=v_d_skip, attn_sinks=v_attn_sinks)
    res = _adamw_small(slots, chip_id, conv_w, m_conv_w, v_conv_w, [params[n] for n in SMALL_NAMES],
                       [moms[n] for n in SMALL_NAMES], [vars_[n] for n in SMALL_NAMES])
    loss = res[0][0, 0]
    grads, delta, new_m, new_v = {}, {}, {}, {}
    for k, n in enumerate(("conv_w",) + SMALL_NAMES):
        grads[n], delta[n], new_m[n], new_v[n] = res[1 + 4 * k:5 + 4 * k]
    for dd, a_in, a_out in ((grads, g_w_in, g_w_out), (delta, d_w_in, d_w_out), (new_m, nm_w_in, nm_w_out),
                            (new_v, nv_w_in, nv_w_out)):
        dd["w_in"] = a_in
        dd["w_out"] = a_out
    order = ("w_in", "conv_w", "conv_b", "dt_bias", "a_log", "d_skip", "ssd_norm_w", "attn_sinks", "w_out", "ln_g", "ln_b")
    return (loss, grad_x[None], *[grads[n] for n in order], *[delta[n] for n in order], *[new_m[n] for n in order],
            *[new_v[n] for n in order])
```

```python
import numpy as np
import jax
import jax.numpy as jnp
from jax import lax
from jax.experimental import pallas as pl
from jax.experimental.pallas import tpu as pltpu

F32 = jnp.float32
BF16 = jnp.bfloat16
MESH = pl.DeviceIdType.MESH

D_MODEL = 1024
D_SSD = 1024
D_ATT = 1024
D_MIX = 2048
SSD_HEADS = 16
SSD_P = 64
SSD_GROUPS = 2
SSD_R = 8
SSD_N = 128
D_BC = 256
D_XBC = 1536
CONV_K = 4
CHUNK = 128
ATT_HD = 64
ATT_QH = 16
ATT_KVH = 4
ATT_R = 4
D_KV = 256
WINDOW = 128
ROPE_THETA = 500000.0
ROPE_DIM = 16
ALPHA = 2.0 ** 0.25
LN_EPS = 1e-5
RMS_EPS = 1e-5
D_IN_PROJ = 5136
O_Z, O_XBC, O_DT, O_Q, O_K, O_V, O_G = 0, 1024, 2560, 2576, 3600, 3856, 4112
P_Z, P_G, P_Q, P_XBC, P_KV, P_DT, P_END = 0, 1024, 2048, 3072, 4608, 5120, 5248
DT_PAD = 128
N_CHIPS = 4
W_IN_COLS = D_IN_PROJ // N_CHIPS
SLAB_ROWS = 1312
W_OUT_ROWS = D_MIX // N_CHIPS
CONV_COLS = D_XBC // N_CHIPS

ADAM_LR = 0.001
ADAM_B1 = 0.9
ADAM_B2 = 0.999
ADAM_EPS = 1e-08
ADAM_WD = 0.01
ADAM_STEP = 10

VMEM_LIMIT = 56 * 1024 * 1024
ROW_TILE = 512
NEG_BIG = -1e30
HI = lax.Precision.HIGHEST


def _cparams(sem=None, **kw):
    if sem is not None:
        kw["dimension_semantics"] = sem
    return pltpu.CompilerParams(vmem_limit_bytes=VMEM_LIMIT, **kw)


def _dot(a, b):
    return jnp.dot(a, b, preferred_element_type=F32)


def _dot_nt(a, b):
    return lax.dot_general(a, b, (((1,), (1,)), ((), ())), preferred_element_type=F32)


def _dot_tn(a, b):
    return lax.dot_general(a, b, (((0,), (0,)), ((), ())), preferred_element_type=F32)


def _bf(a):
    return a.astype(BF16)


def _iota2(shape, dim):
    return lax.broadcasted_iota(jnp.int32, shape, dim)


def _to_rows(col):
    k = col.shape[1]
    eye = (_iota2((k, k), 0) == _iota2((k, k), 1)).astype(F32)
    return lax.dot_general(eye, col, (((1,), (1,)), ((), ())), preferred_element_type=F32, precision=HI)


def _to_cols(row):
    n = row.shape[1]
    eye = (_iota2((n, n), 0) == _iota2((n, n), 1)).astype(F32)
    return lax.dot_general(eye, row, (((1,), (1,)), ((), ())), preferred_element_type=F32, precision=HI)


def _sigmoid(x):
    return jax.nn.sigmoid(x)


def _in_proj(x, w, pos, inv):
    L = x.shape[0]
    tm = ROW_TILE
    widths = (D_SSD, D_ATT, D_ATT, D_XBC, 2 * D_KV, DT_PAD)

    def body(x_ref, w_ref, pos_ref, inv_ref, z_ref, g_ref, q_ref, xbc_ref, kv_ref, dt_ref, xb_ref):
        xb = _bf(x_ref[...])
        xb_ref[...] = xb
        for o_ref, off, wd in zip((z_ref, g_ref, xbc_ref, dt_ref), (P_Z, P_G, P_XBC, P_DT), (D_SSD, D_ATT, D_XBC, DT_PAD)):
            o_ref[...] = _dot_nt(xb, w_ref[off:off + wd, :])
        tabs = _rope_tables(pos_ref, inv_ref)
        q_ref[...] = _bf(_rope(_dot_nt(xb, w_ref[P_Q:P_Q + D_ATT, :]), tabs))
        kv_ref[:, 0:D_KV] = _bf(_rope(_dot_nt(xb, w_ref[P_KV:P_KV + D_KV, :]), tabs))
        kv_ref[:, D_KV:2 * D_KV] = _bf(_dot_nt(xb, w_ref[P_KV + D_KV:P_KV + 2 * D_KV, :]))

    row = lambda wd: pl.BlockSpec((tm, wd), lambda i: (i, 0))
    return pl.pallas_call(
        body, name="in_proj", grid=(L // tm,),
        in_specs=[row(D_MODEL), pl.BlockSpec((P_END, D_MODEL), lambda i: (0, 0), pipeline_mode=pl.Buffered(1)), row(1),
                  pl.BlockSpec((1, 2 * ATT_HD), lambda i: (0, 0))],
        out_specs=[row(wd) for wd in widths] + [row(D_MODEL)],
        out_shape=[jax.ShapeDtypeStruct((L, wd), dt) for wd, dt in zip(widths, (F32, F32, BF16, F32, BF16, F32))]
        + [jax.ShapeDtypeStruct((L, D_MODEL), BF16)],
        compiler_params=_cparams(("parallel",)),
    )(x, w, pos, inv)


def _matmuls_tn(a_list, b, name, out_dtype=F32):
    K, N = b.shape
    tk = min(K, 1024)
    nk = K // tk
    n = len(a_list)
    in_place = out_dtype == F32

    def body(*refs):
        b_ref = refs[n]
        o_refs = refs[n + 1:2 * n + 1]
        acc_refs = o_refs if in_place else refs[2 * n + 1:]
        k = pl.program_id(0)
        bb = _bf(b_ref[...])
        for a_ref, o_ref, acc_ref in zip(refs[:n], o_refs, acc_refs):
            part = _dot_tn(_bf(a_ref[...]), bb)

            @pl.when(k == 0)
            def _():
                acc_ref[...] = part

            @pl.when(k > 0)
            def _():
                acc_ref[...] += part

            if not in_place:
                @pl.when(k == nk - 1)
                def _():
                    o_ref[...] = acc_ref[...].astype(out_dtype)

    return pl.pallas_call(
        body, name=name, grid=(nk,),
        in_specs=[pl.BlockSpec((tk, a.shape[1]), lambda k: (k, 0)) for a in a_list] + [pl.BlockSpec((tk, N), lambda k: (k, 0))],
        out_specs=[pl.BlockSpec((a.shape[1], N), lambda k: (0, 0)) for a in a_list],
        out_shape=[jax.ShapeDtypeStruct((a.shape[1], N), out_dtype) for a in a_list],
        scratch_shapes=[] if in_place else [pltpu.VMEM((a.shape[1], N), F32) for a in a_list],
        compiler_params=_cparams(("arbitrary",)),
    )(*a_list, b)


def _grad_x(dr, dz, dg, dq, dxbc, dkv, ddt, w, after, part, prev=None):
    L = dr.shape[0]
    tm = min(ROW_TILE, L // 4)
    first = L // (4 * tm)
    n = first if part == 0 else L // tm - first
    widths = (D_SSD, D_ATT, D_ATT, D_XBC, 2 * D_KV, DT_PAD)
    offs = (P_Z, P_G, P_Q, P_XBC, P_KV, P_DT)

    def body(dr_ref, dz_ref, dg_ref, dq_ref, dxbc_ref, dkv_ref, ddt_ref, w_ref, after_ref, *rest):
        o_ref = rest[-1]
        acc = ALPHA * dr_ref[...]
        for p_ref, off, wd in zip((dz_ref, dg_ref, dq_ref, dxbc_ref, dkv_ref, ddt_ref), offs, widths):
            acc = acc + _dot(_bf(p_ref[...]), w_ref[off:off + wd, :])
        o_ref[...] = acc

    row = lambda wd: pl.BlockSpec((tm, wd), lambda i: (i + part * first, 0))
    ins = [dr, dz, dg, dq, dxbc, dkv, ddt, w, after]
    specs = ([row(D_MODEL)] + [row(wd) for wd in widths]
             + [pl.BlockSpec((P_END, D_MODEL), lambda i: (0, 0), pipeline_mode=pl.Buffered(1)),
                pl.BlockSpec((8, 128), lambda i: (0, 0))])
    if prev is not None:
        ins.append(prev)
        specs.append(pl.BlockSpec(memory_space=pl.ANY))
    return pl.pallas_call(
        body, name="grad_x_%d" % part, grid=(n,),
        in_specs=specs, out_specs=row(D_MODEL),
        out_shape=jax.ShapeDtypeStruct((L, D_MODEL), F32),
        input_output_aliases={} if prev is None else {len(ins) - 1: 0},
        compiler_params=_cparams(("parallel",)),
    )(*ins)


HALO = 16


def _shift_matrix(offsets):
    n = CHUNK + HALO
    m = np.zeros((len(offsets) * CHUNK, 2 * n), np.float32)
    for k, off in enumerate(offsets):
        t = np.arange(CHUNK)
        m[k * CHUNK + t, t + off] = 1.0
        m[k * CHUNK + t, n + t + off] = 1.0
    return jnp.asarray(m, BF16)


def _shifted_rows(first_part, second_part, smat_ref):
    h1, l1 = _hi_lo(first_part)
    h2, l2 = _hi_lo(second_part)
    sh = _dot(smat_ref[...], jnp.concatenate([h1, h2, l1, l2], axis=0))
    return sh[0:CHUNK], sh[CHUNK:2 * CHUNK], sh[2 * CHUNK:3 * CHUNK]


def _ssd_chunk_pre(first, xbc_ref, tail_ref, dt_ref, cw_ref, cb_ref, dtb_ref, alog_ref, smat_ref):
    tail = jnp.where(first, 0.0, tail_ref[...])
    x = xbc_ref[...]
    taps = _shifted_rows(tail, x, smat_ref) + (x,)
    u = cb_ref[...] + cw_ref[0:1, :] * taps[0]
    for k in range(1, CONV_K):
        u = u + cw_ref[k:k + 1, :] * taps[k]
    sig = _sigmoid(u)
    xbc = u * sig
    dtraw = dt_ref[:, 0:SSD_HEADS] + dtb_ref[...]
    dt = jax.nn.softplus(dtraw)
    A = -jnp.exp(alog_ref[...])
    a = dt * A
    tril = (_iota2((CHUNK, CHUNK), 0) >= _iota2((CHUNK, CHUNK), 1)).astype(F32)
    acs = jnp.dot(tril, a, preferred_element_type=F32, precision=HI)
    acs_row = _to_rows(acs)
    return u, sig, xbc, dtraw, dt, A, acs, acs_row, taps


def _head_expander():
    return (_iota2((SSD_HEADS, D_SSD), 1) // SSD_P == _iota2((SSD_HEADS, D_SSD), 0)).astype(BF16)


def _hi_lo(x):
    hi = _bf(x)
    return hi, _bf(x - hi.astype(F32))


def _expand(v, e):
    hi, lo = _hi_lo(v)
    return _dot(hi, e) + _dot(lo, e)


def _headsum(t, e):
    m = t.shape[0]
    if m < 8:
        t = jnp.broadcast_to(t[0:1], (8, t.shape[1]))
    hi, lo = _hi_lo(t)
    return (_dot_nt(hi, e) + _dot_nt(lo, e))[0:m]


def _ssd_decays(dt, acs, dsk_ref, e):
    alast = acs[CHUNK - 1:CHUNK, :]
    stk = jnp.concatenate([dt, jnp.exp(acs), jnp.exp(alast - acs),
                           jnp.broadcast_to(jnp.exp(alast), (8, SSD_HEADS)),
                           jnp.broadcast_to(dsk_ref[...], (8, SSD_HEADS))], axis=0)
    ex = _expand(stk, e)
    return (ex[0:CHUNK], ex[CHUNK:2 * CHUNK], ex[2 * CHUNK:3 * CHUNK], ex[3 * CHUNK:3 * CHUNK + 1],
            ex[3 * CHUNK + 8:3 * CHUNK + 9])


def _ssd_fwd(z, xbc, dtp, conv_w, conv_b, dt_bias, a_log, d_skip, norm_w):
    L = z.shape[0]
    nc = L // CHUNK
    half = D_SSD // SSD_GROUPS

    def body(z_ref, xbc_ref, tail_ref, dt_ref, cw_ref, cb_ref, dtb_ref, alog_ref, dsk_ref, nw_ref, smat_ref,
             y_ref, ypre_ref, prev_ref, state, ybuf, mbuf):
        c = pl.program_id(0)

        @pl.when(c == 0)
        def _():
            state[...] = jnp.zeros_like(state)

        u, sig, xbcv, dtraw, dt, A, acs, acs_row, _ = _ssd_chunk_pre(
            c == 0, xbc_ref, tail_ref, dt_ref, cw_ref, cb_ref, dtb_ref, alog_ref, smat_ref)
        e = _head_expander()
        dtE, eacsE, dsdE, ealE, dskE = _ssd_decays(dt, acs, dsk_ref, e)
        xs = xbcv[:, 0:D_SSD]
        X = xs * dtE
        prev_ref[0] = state[...]
        causal = _iota2((CHUNK, CHUNK), 0) >= _iota2((CHUNK, CHUNK), 1)
        for g in range(SSD_GROUPS):
            gs = slice(half * g, half * (g + 1))
            Bg = _bf(xbcv[:, D_SSD + SSD_N * g:D_SSD + SSD_N * (g + 1)])
            Cg = _bf(xbcv[:, D_SSD + D_BC + SSD_N * g:D_SSD + D_BC + SSD_N * (g + 1)])
            cb = _dot_nt(Cg, Bg)
            for r in range(SSD_R):
                h = g * SSD_R + r
                seg = acs[:, h:h + 1] - acs_row[h:h + 1, :]
                mbuf[h] = _bf(cb * jnp.where(causal, jnp.exp(jnp.where(causal, seg, 0.0)), 0.0))
            st = state[:, gs]
            ybuf[:, gs] = _dot(Cg, _bf(st)) * eacsE[:, gs] + dskE[:, gs] * xs[:, gs]
            state[:, gs] = st * ealE[:, gs] + _dot_tn(Bg, _bf(X[:, gs] * dsdE[:, gs]))
        Xb = _bf(X)
        for h in range(SSD_HEADS):
            hs = slice(SSD_P * h, SSD_P * (h + 1))
            ybuf[:, hs] += _dot(mbuf[h], Xb[:, hs])
        y = ybuf[...]
        ypre_ref[...] = y
        zv = z_ref[...]
        yf = y * (zv * _sigmoid(zv))
        for g in range(SSD_GROUPS):
            gs = slice(half * g, half * (g + 1))
            yg = yf[:, gs]
            ms = jnp.mean(yg * yg, axis=-1, keepdims=True)
            y_ref[:, gs] = _bf(yg * lax.rsqrt(ms + RMS_EPS) * nw_ref[:, gs])

    full = lambda shape: pl.BlockSpec(shape, lambda c: (0, 0))
    return pl.pallas_call(
        body, name="ssd_fwd", grid=(nc,),
        in_specs=[
            pl.BlockSpec((CHUNK, D_SSD), lambda c: (c, 0)),
            pl.BlockSpec((CHUNK, D_XBC), lambda c: (c, 0)),
            pl.BlockSpec((HALO, D_XBC), lambda c: (jnp.maximum(c * (CHUNK // HALO) - 1, 0), 0)),
            pl.BlockSpec((CHUNK, DT_PAD), lambda c: (c, 0)),
            full((CONV_K, D_XBC)), full((1, D_XBC)), full((1, SSD_HEADS)), full((1, SSD_HEADS)), full((1, SSD_HEADS)),
            full((1, D_SSD)), full((3 * CHUNK, 2 * (CHUNK + HALO))),
        ],
        out_specs=[
            pl.BlockSpec((CHUNK, D_SSD), lambda c: (c, 0)),
            pl.BlockSpec((CHUNK, D_SSD), lambda c: (c, 0)),
            pl.BlockSpec((1, SSD_N, D_SSD), lambda c: (c, 0, 0)),
        ],
        out_shape=[
            jax.ShapeDtypeStruct((L, D_SSD), BF16),
            jax.ShapeDtypeStruct((L, D_SSD), F32),
            jax.ShapeDtypeStruct((nc, SSD_N, D_SSD), F32),
        ],
        scratch_shapes=[
            pltpu.VMEM((SSD_N, D_SSD), F32),
            pltpu.VMEM((CHUNK, D_SSD), F32),
            pltpu.VMEM((SSD_HEADS, CHUNK, CHUNK), BF16),
        ],
        compiler_params=_cparams(("arbitrary",)),
    )(z, xbc, xbc, dtp, conv_w, conv_b, dt_bias, a_log, d_skip, norm_w, _shift_matrix((13, 14, 15)))


def _ssd_bwd(dy, z, ypre, xbc, dtp, prev, conv_w, conv_b, dt_bias, a_log, d_skip, norm_w):
    L = z.shape[0]
    nc = L // CHUNK
    half = D_SSD // SSD_GROUPS

    def body(dy_ref, z_ref, ypre_ref, xbc_ref, tail_ref, dt_ref, prev_ref, cw_ref, cb_ref, dtb_ref, alog_ref, dsk_ref,
             nw_ref, smat_ref, smat2_ref, dz_ref, dxbc_ref, ddt_ref, gcw_ref, gcb_ref, gdtb_ref, galog_ref, gdsk_ref,
             gnw_ref, dstate, dhead, dpost, yobuf, bdbuf, lmbuf, dmbuf, cbbuf):
        i = pl.program_id(0)
        c = nc - 1 - i

        @pl.when(i == 0)
        def _():
            dstate[...] = jnp.zeros_like(dstate)
            dhead[...] = jnp.zeros_like(dhead)
            gcw_ref[...] = jnp.zeros_like(gcw_ref)
            gcb_ref[...] = jnp.zeros_like(gcb_ref)
            gdtb_ref[...] = jnp.zeros_like(gdtb_ref)
            galog_ref[...] = jnp.zeros_like(galog_ref)
            gdsk_ref[...] = jnp.zeros_like(gdsk_ref)
            gnw_ref[...] = jnp.zeros_like(gnw_ref)

        u, sig, xbcv, dtraw, dt, A, acs, acs_row, taps = _ssd_chunk_pre(
            c == 0, xbc_ref, tail_ref, dt_ref, cw_ref, cb_ref, dtb_ref, alog_ref, smat_ref)
        e = _head_expander()
        dtE, eacsE, dsdE, ealE, dskE = _ssd_decays(dt, acs, dsk_ref, e)
        alast = acs[CHUNK - 1:CHUNK, :]
        xs = xbcv[:, 0:D_SSD]
        X = xs * dtE
        Xb = _bf(X)

        zv = z_ref[...]
        ypre = ypre_ref[...]
        dyn = dy_ref[...]
        sz = _sigmoid(zv)
        silu_z = zv * sz
        yf = ypre * silu_z
        dyf_parts = []
        for g in range(SSD_GROUPS):
            gs = slice(half * g, half * (g + 1))
            yg = yf[:, gs]
            rstd = lax.rsqrt(jnp.mean(yg * yg, axis=-1, keepdims=True) + RMS_EPS)
            dout = dyn[:, gs]
            gnw_ref[:, gs] += jnp.sum(dout * yg * rstd, axis=0, keepdims=True)
            dyhat = dout * nw_ref[:, gs]
            dyf_parts.append(rstd * (dyhat - yg * (rstd * rstd) * jnp.mean(dyhat * yg, axis=-1, keepdims=True)))
        dyf = jnp.concatenate(dyf_parts, axis=1)
        dz_ref[...] = _bf(dyf * ypre * (sz * (1.0 + zv * (1.0 - sz))))
        dyp = dyf * silu_z
        dyb = _bf(dyp)
        G = dyp * eacsE

        causal = _iota2((CHUNK, CHUNK), 0) >= _iota2((CHUNK, CHUNK), 1)
        ST = prev_ref[0]
        dST = dstate[...]
        for g in range(SSD_GROUPS):
            gs = slice(half * g, half * (g + 1))
            bs = slice(D_SSD + SSD_N * g, D_SSD + SSD_N * (g + 1))
            cs = slice(D_SSD + D_BC + SSD_N * g, D_SSD + D_BC + SSD_N * (g + 1))
            Bg = _bf(xbcv[:, bs])
            Cg = _bf(xbcv[:, cs])
            Gb = _bf(G[:, gs])
            STb = _bf(ST[:, gs])
            dSTb = _bf(dST[:, gs])
            dstate[:, gs] = dST[:, gs] * ealE[:, gs] + _dot_tn(Cg, Gb)
            yobuf[:, gs] = _dot(Cg, STb) * eacsE[:, gs]
            bdbuf[:, gs] = _dot(Bg, dSTb)
            dpost[:, cs] = _dot_nt(Gb, STb)
            dpost[:, bs] = _dot_nt(_bf(X[:, gs] * dsdE[:, gs]), dSTb)
            cbbuf[g] = _dot_nt(Cg, Bg)
            for r in range(SSD_R):
                h = g * SSD_R + r
                seg = acs[:, h:h + 1] - acs_row[h:h + 1, :]
                lmbuf[h] = jnp.where(causal, jnp.exp(jnp.where(causal, seg, 0.0)), 0.0)
        for h in range(SSD_HEADS):
            hs = slice(SSD_P * h, SSD_P * (h + 1))
            Mb = _bf(cbbuf[h // SSD_R] * lmbuf[h])
            dmbuf[h] = _dot_nt(dyb[:, hs], Xb[:, hs])
            dpost[:, hs] = _dot_tn(Mb, dyb[:, hs])
        lane16 = _iota2((1, SSD_HEADS), 1)
        sub16 = _iota2((SSD_HEADS, 1), 0)
        dacs_col = jnp.zeros((CHUNK, SSD_HEADS), F32)
        dacs_row = jnp.zeros((SSD_HEADS, CHUNK), F32)
        for g in range(SSD_GROUPS):
            bs = slice(D_SSD + SSD_N * g, D_SSD + SSD_N * (g + 1))
            cs = slice(D_SSD + D_BC + SSD_N * g, D_SSD + D_BC + SSD_N * (g + 1))
            cb = cbbuf[g]
            dcb = jnp.zeros((CHUNK, CHUNK), F32)
            for r in range(SSD_R):
                h = g * SSD_R + r
                dM = dmbuf[h]
                Lm = lmbuf[h]
                dcb = dcb + dM * Lm
                dseg = dM * (cb * Lm)
                dacs_col = dacs_col + jnp.sum(dseg, axis=-1, keepdims=True) * (lane16 == h).astype(F32)
                dacs_row = dacs_row - jnp.sum(dseg, axis=0, keepdims=True) * (sub16 == h).astype(F32)
            dcbb = _bf(dcb)
            dpost[:, bs] += _dot_tn(dcbb, _bf(xbcv[:, cs]))
            dpost[:, cs] += _dot(dcbb, _bf(xbcv[:, bs]))

        BD = bdbuf[...]
        dX = dpost[:, 0:D_SSD] + dsdE * BD
        dsd = jnp.exp(alast - acs)
        T = _headsum(X * BD, e) * dsd
        dalast = jnp.sum(T, axis=0, keepdims=True) + _headsum(
            jnp.sum(dST * ST, axis=0, keepdims=True), e) * jnp.exp(alast)
        is_last = (_iota2((CHUNK, 1), 0) == CHUNK - 1).astype(F32)
        dacs = dacs_col + _to_cols(dacs_row) + _headsum(dyp * yobuf[...], e) - T + is_last * dalast
        triu = (_iota2((CHUNK, CHUNK), 0) <= _iota2((CHUNK, CHUNK), 1)).astype(F32)
        da = jnp.dot(triu, dacs, preferred_element_type=F32, precision=HI)
        ddt_tot = _headsum(dX * xs, e) + da * A
        galog_ref[...] += jnp.sum(da * dt, axis=0, keepdims=True) * A
        ddtraw = ddt_tot * _sigmoid(dtraw)
        gdtb_ref[...] += jnp.sum(ddtraw, axis=0, keepdims=True)
        gdsk_ref[...] += _headsum(jnp.sum(dyp * xs, axis=0, keepdims=True), e)
        ddt_ref[...] = jnp.zeros_like(ddt_ref)
        ddt_ref[:, 0:SSD_HEADS] = ddtraw
        dpost[:, 0:D_SSD] = dX * dtE + dskE * dyp

        dconv = dpost[...] * (sig * (1.0 + u * (1.0 - sig)))
        gcb_ref[...] += jnp.sum(dconv, axis=0, keepdims=True)
        for k in range(CONV_K):
            gcw_ref[k:k + 1, :] += jnp.sum(dconv * taps[k], axis=0, keepdims=True)
        later = _shifted_rows(dconv, dhead[...], smat2_ref)
        dx = cw_ref[CONV_K - 1:CONV_K, :] * dconv
        for k in range(CONV_K - 1):
            dx = dx + cw_ref[k:k + 1, :] * later[k]
        dxbc_ref[...] = _bf(dx)
        dhead[...] = dconv[0:HALO, :]

    full = lambda shape: pl.BlockSpec(shape, lambda i: (0, 0))
    rev = lambda wd: pl.BlockSpec((CHUNK, wd), lambda i: (nc - 1 - i, 0))
    return pl.pallas_call(
        body, name="ssd_bwd", grid=(nc,),
        in_specs=[
            rev(D_SSD), rev(D_SSD), rev(D_SSD), rev(D_XBC),
            pl.BlockSpec((HALO, D_XBC), lambda i: (jnp.maximum((nc - 1 - i) * (CHUNK // HALO) - 1, 0), 0)),
            rev(DT_PAD),
            pl.BlockSpec((1, SSD_N, D_SSD), lambda i: (nc - 1 - i, 0, 0)),
            full((CONV_K, D_XBC)), full((1, D_XBC)), full((1, SSD_HEADS)), full((1, SSD_HEADS)), full((1, SSD_HEADS)),
            full((1, D_SSD)), full((3 * CHUNK, 2 * (CHUNK + HALO))), full((3 * CHUNK, 2 * (CHUNK + HALO))),
        ],
        out_specs=[
            rev(D_SSD), rev(D_XBC), rev(DT_PAD),
            full((CONV_K, D_XBC)), full((1, D_XBC)), full((1, SSD_HEADS)), full((1, SSD_HEADS)), full((1, SSD_HEADS)),
            full((1, D_SSD)),
        ],
        out_shape=[
            jax.ShapeDtypeStruct((L, D_SSD), BF16), jax.ShapeDtypeStruct((L, D_XBC), BF16),
            jax.ShapeDtypeStruct((L, DT_PAD), F32),
            jax.ShapeDtypeStruct((CONV_K, D_XBC), F32), jax.ShapeDtypeStruct((1, D_XBC), F32),
            jax.ShapeDtypeStruct((1, SSD_HEADS), F32), jax.ShapeDtypeStruct((1, SSD_HEADS), F32),
            jax.ShapeDtypeStruct((1, SSD_HEADS), F32), jax.ShapeDtypeStruct((1, D_SSD), F32),
        ],
        scratch_shapes=[
            pltpu.VMEM((SSD_N, D_SSD), F32),
            pltpu.VMEM((HALO, D_XBC), F32),
            pltpu.VMEM((CHUNK, D_XBC), F32),
            pltpu.VMEM((CHUNK, D_SSD), F32),
            pltpu.VMEM((CHUNK, D_SSD), F32),
            pltpu.VMEM((SSD_HEADS, CHUNK, CHUNK), F32),
            pltpu.VMEM((SSD_HEADS, CHUNK, CHUNK), F32),
            pltpu.VMEM((SSD_GROUPS, CHUNK, CHUNK), F32),
        ],
        compiler_params=_cparams(("arbitrary",)),
    )(dy, z, ypre, xbc, xbc, dtp, prev, conv_w, conv_b, dt_bias, a_log, d_skip, norm_w, _shift_matrix((13, 14, 15)),
      _shift_matrix((3, 2, 1)))


def _rope_tables(pos_ref, inv_ref):
    ang = pos_ref[...].astype(F32) * inv_ref[...]
    d = _iota2((1, 2 * ATT_HD), 1) % ATT_HD
    s = jnp.sin(ang)
    return jnp.cos(ang), jnp.where(d < ROPE_DIM // 2, -s, 0.0), jnp.where((d >= ROPE_DIM // 2) & (d < ROPE_DIM), s, 0.0)


def _rope(t, tabs):
    c, s1, s2 = tabs
    n = t.shape[1]
    rep = n // c.shape[1]
    return (t * jnp.tile(c, (1, rep)) + pltpu.roll(t, n - ROPE_DIM // 2, 1) * jnp.tile(s1, (1, rep))
            + pltpu.roll(t, ROPE_DIM // 2, 1) * jnp.tile(s2, (1, rep)))


def _rope_t(t, tabs):
    c, s1, s2 = tabs
    n = t.shape[1]
    rep = n // c.shape[1]
    return (t * jnp.tile(c, (1, rep)) + pltpu.roll(t * jnp.tile(s1, (1, rep)), ROPE_DIM // 2, 1)
            + pltpu.roll(t * jnp.tile(s2, (1, rep)), n - ROPE_DIM // 2, 1))


def _stack_heads(t, j):
    return jnp.concatenate([t[:, ATT_HD * (j * ATT_R + r):ATT_HD * (j * ATT_R + r + 1)] for r in range(ATT_R)], axis=0)


def _swa_mask_t(first):
    si = _iota2((2 * WINDOW, ATT_R * WINDOW), 0)
    qi = _iota2((2 * WINDOW, ATT_R * WINDOW), 1) % WINDOW
    band = (si > qi) & (si <= qi + WINDOW)
    return band & (jnp.logical_not(first) | (si >= WINDOW))


def _head_rows(ref, j):
    if ref.shape[0] == 1:
        parts = [jnp.broadcast_to(ref[:, j * ATT_R + r:j * ATT_R + r + 1], (1, WINDOW)) for r in range(ATT_R)]
    else:
        parts = [ref[j * ATT_R + r:j * ATT_R + r + 1, :] for r in range(ATT_R)]
    return jnp.concatenate(parts, axis=1)


def _swa_fwd(q, g, kv, sinks):
    L = q.shape[0]
    nb = L // WINDOW
    scale = ATT_HD ** -0.5

    def body(q_ref, g_ref, kvc_ref, kvp_ref, sink_ref, y_ref, o_ref, lse_ref, otbuf):
        n = pl.program_id(0)
        kk = jnp.concatenate([kvp_ref[:, 0:D_KV], kvc_ref[:, 0:D_KV]], axis=0)
        vv = jnp.concatenate([kvp_ref[:, D_KV:2 * D_KV], kvc_ref[:, D_KV:2 * D_KV]], axis=0)
        valid = _swa_mask_t(n == 0)
        qv = q_ref[...]
        for j in range(ATT_KVH):
            js = slice(ATT_HD * j, ATT_HD * (j + 1))
            st = _dot_nt(kk[:, js], _stack_heads(qv, j)) * scale
            st = jnp.where(valid, st, NEG_BIG)
            sink = _head_rows(sink_ref, j)
            m = jnp.maximum(jnp.max(st, axis=0, keepdims=True), sink)
            p = jnp.exp(st - m)
            denom = jnp.sum(p, axis=0, keepdims=True) + jnp.exp(sink - m)
            ot = _dot_tn(vv[:, js], _bf(p)) * (1.0 / denom)
            lse = m + jnp.log(denom)
            for r in range(ATT_R):
                h = j * ATT_R + r
                otbuf[ATT_HD * h:ATT_HD * (h + 1), :] = ot[:, WINDOW * r:WINDOW * (r + 1)]
                lse_ref[h:h + 1, :] = lse[:, WINDOW * r:WINDOW * (r + 1)]
        o = otbuf[...].T
        o_ref[...] = o
        gv = g_ref[...]
        y_ref[...] = _bf(o * (gv * _sigmoid(gv)))

    cur = lambda wd: pl.BlockSpec((WINDOW, wd), lambda n: (n, 0))
    prv = lambda wd: pl.BlockSpec((WINDOW, wd), lambda n: (jnp.maximum(n - 1, 0), 0))
    return pl.pallas_call(
        body, name="swa_fwd", grid=(nb,),
        in_specs=[cur(D_ATT), cur(D_ATT), cur(2 * D_KV), prv(2 * D_KV), pl.BlockSpec((1, ATT_QH), lambda n: (0, 0))],
        out_specs=[cur(D_ATT), cur(D_ATT), pl.BlockSpec((ATT_QH, WINDOW), lambda n: (0, n))],
        out_shape=[jax.ShapeDtypeStruct((L, D_ATT), BF16), jax.ShapeDtypeStruct((L, D_ATT), F32),
                   jax.ShapeDtypeStruct((ATT_QH, L), F32)],
        scratch_shapes=[pltpu.VMEM((D_ATT, WINDOW), F32)],
        compiler_params=_cparams(("parallel",)),
    )(q, g, kv, kv, sinks)


def _swa_bwd(dy, q, g, kv, o, lse, pos, inv, sinks):
    L = q.shape[0]
    nb = L // WINDOW
    scale = ATT_HD ** -0.5

    def body(dy_ref, q_ref, g_ref, kvc_ref, kvp_ref, o_ref, lse_ref, posc_ref, posp_ref, inv_ref, sink_ref,
             dq_ref, dg_ref, dkv_ref, dsink_ref, carry, dqbuf, dkbuf, dvbuf):
        n = pl.program_id(0)

        @pl.when(n == 0)
        def _():
            dsink_ref[...] = jnp.zeros_like(dsink_ref)

        @pl.when(n < nb)
        def _():
            tc = _rope_tables(posc_ref, inv_ref)
            tp = _rope_tables(posp_ref, inv_ref)
            kk = jnp.concatenate([kvp_ref[:, 0:D_KV], kvc_ref[:, 0:D_KV]], axis=0)
            vv = jnp.concatenate([kvp_ref[:, D_KV:2 * D_KV], kvc_ref[:, D_KV:2 * D_KV]], axis=0)
            valid = _swa_mask_t(n == 0)
            qv = q_ref[...]
            gv = g_ref[...]
            sg = _sigmoid(gv)
            dyv = dy_ref[...]
            ov = o_ref[...]
            dg_ref[...] = _bf(dyv * ov * (sg * (1.0 + gv * (1.0 - sg))))
            do = dyv * (gv * sg)
            dod = do * ov
            ones = jnp.ones((8, ATT_HD), BF16)
            lane16 = _iota2((1, ATT_QH), 1)
            dsink = jnp.zeros((1, ATT_QH), F32)
            for j in range(ATT_KVH):
                js = slice(ATT_HD * j, ATT_HD * (j + 1))
                kj = kk[:, js]
                vj = vv[:, js]
                qs = _stack_heads(qv, j)
                dos = _bf(_stack_heads(do, j))
                hi, lo = _hi_lo(_stack_heads(dod, j))
                delta = (_dot_nt(ones, hi) + _dot_nt(ones, lo))[0:1]
                lse = _head_rows(lse_ref, j)
                st = _dot_nt(kj, qs) * scale
                pt = jnp.exp(jnp.where(valid, st, NEG_BIG) - lse)
                dst = _bf(pt * (_dot_nt(vj, dos) - delta))
                dqt = _dot_tn(kj, dst) * scale
                dkbuf[:, js] = _dot(dst, qs) * scale
                dvbuf[:, js] = _dot(_bf(pt), dos)
                sd = jnp.exp(_head_rows(sink_ref, j) - lse) * delta
                for r in range(ATT_R):
                    h = j * ATT_R + r
                    ls = slice(WINDOW * r, WINDOW * (r + 1))
                    dqbuf[ATT_HD * h:ATT_HD * (h + 1), :] = dqt[:, ls]
                    dsink = dsink - jnp.sum(sd[:, ls], axis=1, keepdims=True) * (lane16 == h).astype(F32)
            dsink_ref[...] += dsink
            dq_ref[...] = _bf(_rope_t(dqbuf[...].T, tc))
            dkp = _rope_t(dkbuf[0:WINDOW, :], tp)
            dkc = _rope_t(dkbuf[WINDOW:2 * WINDOW, :], tc)

            @pl.when(n > 0)
            def _():
                dkv_ref[:, 0:D_KV] = _bf(carry[:, 0:D_KV] + dkp)
                dkv_ref[:, D_KV:2 * D_KV] = _bf(carry[:, D_KV:2 * D_KV] + dvbuf[0:WINDOW, :])

            carry[:, 0:D_KV] = dkc
            carry[:, D_KV:2 * D_KV] = dvbuf[WINDOW:2 * WINDOW, :]

        @pl.when(n == nb)
        def _():
            dkv_ref[...] = _bf(carry[...])

    last = nb - 1
    cur = lambda wd: pl.BlockSpec((WINDOW, wd), lambda n: (jnp.minimum(n, last), 0))
    prv = lambda wd: pl.BlockSpec((WINDOW, wd), lambda n: (jnp.maximum(jnp.minimum(n, last) - 1, 0), 0))
    return pl.pallas_call(
        body, name="swa_bwd", grid=(nb + 1,),
        in_specs=[cur(D_ATT), cur(D_ATT), cur(D_ATT), cur(2 * D_KV), prv(2 * D_KV), cur(D_ATT),
                  pl.BlockSpec((ATT_QH, WINDOW), lambda n: (0, jnp.minimum(n, last))), cur(1), prv(1),
                  pl.BlockSpec((1, 2 * ATT_HD), lambda n: (0, 0)), pl.BlockSpec((1, ATT_QH), lambda n: (0, 0))],
        out_specs=[cur(D_ATT), cur(D_ATT),
                   pl.BlockSpec((WINDOW, 2 * D_KV), lambda n: (jnp.maximum(n - 1, 0), 0)),
                   pl.BlockSpec((1, ATT_QH), lambda n: (0, 0))],
        out_shape=[jax.ShapeDtypeStruct((L, D_ATT), BF16), jax.ShapeDtypeStruct((L, D_ATT), BF16),
                   jax.ShapeDtypeStruct((L, 2 * D_KV), BF16), jax.ShapeDtypeStruct((1, ATT_QH), F32)],
        scratch_shapes=[pltpu.VMEM((WINDOW, 2 * D_KV), F32), pltpu.VMEM((D_ATT, WINDOW), F32),
                        pltpu.VMEM((2 * WINDOW, D_KV), F32), pltpu.VMEM((2 * WINDOW, D_KV), F32)],
        compiler_params=_cparams(("arbitrary",)),
    )(dy, q, g, kv, kv, o, lse, pos, pos, inv, sinks)


def _out_ln_loss(y_ssd, y_att, x, target, w_out, ln_g, ln_b):
    L = x.shape[0]
    tm = ROW_TILE
    inv_d = 1.0 / D_MODEL

    def body(ys_ref, ya_ref, x_ref, t_ref, w_ref, g_ref, b_ref, dr_ref, dys_ref, dya_ref, loss_ref, gg_ref, gb_ref):
        i = pl.program_id(0)

        @pl.when(i == 0)
        def _():
            loss_ref[...] = jnp.zeros_like(loss_ref)
            gg_ref[...] = jnp.zeros_like(gg_ref)
            gb_ref[...] = jnp.zeros_like(gb_ref)

        h = _dot(_bf(ys_ref[...]), w_ref[0:D_SSD, :]) + _dot(_bf(ya_ref[...]), w_ref[D_SSD:D_MIX, :])
        r = ALPHA * x_ref[...] + h
        mu = jnp.mean(r, axis=-1, keepdims=True)
        xc = r - mu
        rstd = lax.rsqrt(jnp.mean(xc * xc, axis=-1, keepdims=True) + LN_EPS)
        xhat = xc * rstd
        gam = g_ref[...]
        diff = xhat * gam + b_ref[...] - t_ref[...]
        part = jnp.sum(jnp.sum(diff * diff, axis=-1, keepdims=True), axis=0, keepdims=True)
        loss_ref[...] += (0.5 * inv_d) * part
        dout = diff * inv_d
        gg_ref[...] += jnp.sum(dout * xhat, axis=0, keepdims=True)
        gb_ref[...] += jnp.sum(dout, axis=0, keepdims=True)
        dxh = dout * gam
        dr = rstd * (dxh - jnp.mean(dxh, axis=-1, keepdims=True) - xhat * jnp.mean(dxh * xhat, axis=-1, keepdims=True))
        dr_ref[...] = dr
        drb = _bf(dr)
        dys_ref[...] = _dot_nt(drb, w_ref[0:D_SSD, :])
        dya_ref[...] = _dot_nt(drb, w_ref[D_SSD:D_MIX, :])

    row = pl.BlockSpec((tm, D_MODEL), lambda i: (i, 0))
    vec = pl.BlockSpec((1, D_MODEL), lambda i: (0, 0))
    return pl.pallas_call(
        body, name="out_ln_loss", grid=(L // tm,),
        in_specs=[row, row, row, row, pl.BlockSpec((D_MIX, D_MODEL), lambda i: (0, 0), pipeline_mode=pl.Buffered(1)), vec, vec],
        out_specs=[row, row, row, pl.BlockSpec((1, 128), lambda i: (0, 0)), vec, vec],
        out_shape=[jax.ShapeDtypeStruct((L, D_MODEL), F32)] * 3 + [jax.ShapeDtypeStruct((1, 128), F32)]
        + [jax.ShapeDtypeStruct((1, D_MODEL), F32)] * 2,
        compiler_params=_cparams(("arbitrary",)),
    )(y_ssd, y_att, x, target, w_out, ln_g, ln_b)


def _local_step(x, pos, target, w, get_w_out, token, conv_w, conv_b, dt_bias, a_log, d_skip, norm_w, sinks, ln_g, ln_b):
    inv8 = ROPE_THETA ** (-jnp.arange(0, ROPE_DIM, 2, dtype=F32) / ROPE_DIM)
    inv = jnp.tile(jnp.concatenate([inv8, inv8, jnp.zeros((ATT_HD - ROPE_DIM,), F32)]), 2).reshape(1, 2 * ATT_HD)
    inv = inv + token

    z, g, q, xbc, kv, dtp, xb = _in_proj(x, w, pos, inv)
    y_ssd, y_pre, prev = _ssd_fwd(z, xbc, dtp, conv_w, conv_b, dt_bias, a_log, d_skip, norm_w)
    y_att, o, lse = _swa_fwd(q, g, kv, sinks)
    w_out = get_w_out(lse)
    dr, dy_ssd, dy_att, loss, g_ln_g, g_ln_b = _out_ln_loss(y_ssd, y_att, x, target, w_out, ln_g, ln_b)
    gw_out_ssd, gw_out_att = _matmuls_tn([y_ssd, y_att], dr, "gw_out", out_dtype=BF16)
    slabs = jnp.concatenate([gw_out_ssd, gw_out_att], axis=0).reshape(N_CHIPS, W_OUT_ROWS, D_MODEL)
    w_out_red = _reduce_w_out_start(slabs, loss)
    inv = inv + w_out_red[16][0:1, :]
    dq, dg, dkv, g_sinks = _swa_bwd(dy_att, q, g, kv, o, lse, pos, inv, sinks)
    dz, dxbc, ddt, g_conv_w, g_conv_b, g_dt_bias, g_a_log, g_d_skip, g_norm_w = _ssd_bwd(
        dy_ssd, z, y_pre, xbc, dtp, prev, conv_w, conv_b, dt_bias, a_log, d_skip, norm_w)
    gw_z, gw_g, gw_q = _matmuls_tn([dz, dg, dq], xb, "gw_zgq")
    gw_xbc, gw_kv, gw_dt = _matmuls_tn([dxbc, dkv, ddt], xb, "gw_xbc_kv_dt")
    gw_in = jnp.concatenate([gw_z, gw_xbc, gw_dt[0:SSD_HEADS], gw_q, gw_kv, gw_g], axis=0)
    small = dict(conv_w=g_conv_w, conv_b=g_conv_b, dt_bias=g_dt_bias, a_log=g_a_log, d_skip=g_d_skip,
                 ssd_norm_w=g_norm_w, attn_sinks=g_sinks, ln_g=g_ln_g, ln_b=g_ln_b)
    return loss, (dr, dz, dg, dq, dxbc, dkv, ddt, w), gw_in, w_out_red, small


def _mesh_pos():
    return lax.axis_index("x"), lax.axis_index("y"), lax.axis_index("c")


def _gather_weights(w_in_s, conv_w_s):
    hr = w_in_s.shape[0] // 2
    qa = 336
    quarters = ((0, qa), (qa, hr - qa))

    def body(win_ref, cw_ref, owin_ref, ocw_ref, send_sems, recv_sems, small_send, small_recv, local_sems):
        x, y, c = _mesh_pos()
        me = 2 * x + y
        sibling = (x, y, 1 - c)
        xn, yn, dg = (1 - x, y), (x, 1 - y), (1 - x, 1 - y)
        chips = [xn, yn, dg]
        locals_ = [pltpu.make_async_copy(cw_ref, ocw_ref.at[me], local_sems.at[0])]
        for cp in locals_:
            cp.start()
        started = []

        def piece(ref, chip, half, q):
            off, n = quarters[q]
            return ref.at[2 * chip[0] + chip[1]].at[pl.ds(half * hr + off, n), :]

        def mine(q):
            off, n = quarters[q]
            return win_ref.at[pl.ds(c * hr + off, n), :]

        def copy(src, dst, k, to):
            return pltpu.make_async_remote_copy(src_ref=src, dst_ref=dst, send_sem=send_sems.at[k], recv_sem=recv_sems.at[k],
                                                device_id=to, device_id_type=MESH)

        def go(cp):
            cp.start()
            started.append(cp)

        go(copy(mine(0), piece(owin_ref, (x, y), c, 0), 0, (*xn, c)))
        go(copy(mine(1), piece(owin_ref, (x, y), c, 1), 2, (*yn, c)))
        go(copy(mine(1), piece(owin_ref, (x, y), c, 1), 1, (*xn, c)))
        go(copy(mine(0), piece(owin_ref, (x, y), c, 0), 3, (*yn, c)))
        for j, (px, py) in enumerate(chips):
            cp = pltpu.make_async_remote_copy(
                src_ref=cw_ref, dst_ref=ocw_ref.at[me], send_sem=small_send.at[j], recv_sem=small_recv.at[j],
                device_id=(px, py, c), device_id_type=MESH)
            go(cp)
        arrivals = [(0, xn, 0, (4, (*yn, c))), (2, yn, 1, (5, (*xn, c))), (1, xn, 1, None), (3, yn, 0, None),
                    (4, dg, 0, None), (5, dg, 1, None)]
        for n, (k, chip, q, onward) in enumerate(arrivals):
            blk = piece(owin_ref, chip, c, q)
            copy(blk, blk, k, sibling).wait_recv()
            if onward is not None:
                go(copy(blk, blk, onward[0], onward[1]))
            go(copy(blk, blk, 6 + n, sibling))
        for n, (k, chip, q, onward) in enumerate(arrivals):
            blk = piece(owin_ref, chip, 1 - c, q)
            copy(blk, blk, 6 + n, sibling).wait_recv()
        for j in range(3):
            pltpu.make_async_remote_copy(
                src_ref=cw_ref, dst_ref=ocw_ref.at[me], send_sem=small_send.at[j], recv_sem=small_recv.at[j],
                device_id=sibling, device_id_type=MESH).wait_recv()
        for cp in started:
            cp.wait_send()
        for cp in locals_:
            cp.wait()

    any_spec = pl.BlockSpec(memory_space=pl.ANY)
    return pl.pallas_call(
        body, name="gather_weights",
        in_specs=[any_spec] * 2, out_specs=[any_spec] * 2,
        out_shape=[jax.ShapeDtypeStruct((N_CHIPS,) + a.shape, a.dtype) for a in (w_in_s, conv_w_s)],
        scratch_shapes=[pltpu.SemaphoreType.DMA((12,)), pltpu.SemaphoreType.DMA((12,)),
                        pltpu.SemaphoreType.DMA((3,)), pltpu.SemaphoreType.DMA((3,)), pltpu.SemaphoreType.DMA((3,))],
    )(w_in_s, conv_w_s)


_HBM = pl.BlockSpec(memory_space=pltpu.HBM)
_SEM = pl.BlockSpec(memory_space=pltpu.SEMAPHORE)
_EFFECT = pltpu.SideEffectType.DATAFLOW_SIDE_EFFECTING


def _gather_w_out_start(w_out_s, after):
    def body(src_ref, land_ref, after_ref, s0, s1, s2, r0, r1, r2, src_thru, land_thru, token):
        x, y, c = _mesh_pos()
        me = 2 * x + y
        chips = [(1 - x, y), (x, 1 - y), (1 - x, 1 - y)]
        for (px, py), s, r in zip(chips, (s0, s1, s2), (r0, r1, r2)):
            pltpu.make_async_remote_copy(src_ref=src_ref, dst_ref=land_ref.at[me], send_sem=s, recv_sem=r,
                                         device_id=(px, py, c), device_id_type=MESH).start()
        token[...] = jnp.zeros_like(token)

    sem = pltpu.SemaphoreType.DMA(())
    land = lax.empty((N_CHIPS,) + w_out_s.shape, w_out_s.dtype)
    return pl.pallas_call(
        body, name="gather_w_out_start",
        out_shape=(sem,) * 6 + (pltpu.HBM(w_out_s.shape, w_out_s.dtype), pltpu.HBM(land.shape, land.dtype),
                                jax.ShapeDtypeStruct((8, 128), F32)),
        in_specs=(_HBM, _HBM, pl.BlockSpec(memory_space=pl.ANY)),
        out_specs=(_SEM,) * 6 + (_HBM, _HBM, pl.BlockSpec(memory_space=pltpu.VMEM)),
        input_output_aliases={0: 6, 1: 7},
        compiler_params=pltpu.CompilerParams(has_side_effects=_EFFECT),
    )(pltpu.with_memory_space_constraint(w_out_s, pltpu.HBM), pltpu.with_memory_space_constraint(land, pltpu.HBM), after)


def _gather_w_out_wait(sems, src_thru, land_thru, after):
    def body(src_ref, land_ref, s0, s1, s2, r0, r1, r2, after_ref, src_dead, got_ref):
        x, y, c = _mesh_pos()
        chips = [(1 - x, y), (x, 1 - y), (1 - x, 1 - y)]
        for (px, py), s, r in zip(chips, (s0, s1, s2), (r0, r1, r2)):
            cp = pltpu.make_async_remote_copy(src_ref=src_ref, dst_ref=land_ref.at[2 * px + py], send_sem=s, recv_sem=r,
                                              device_id=(px, py, c), device_id_type=MESH)
            cp.wait_send()
            cp.wait_recv()

    return pl.pallas_call(
        body, name="gather_w_out_wait",
        out_shape=(pltpu.HBM(src_thru.shape, src_thru.dtype), pltpu.HBM(land_thru.shape, land_thru.dtype)),
        in_specs=(_HBM, _HBM) + (_SEM,) * 6 + (pl.BlockSpec(memory_space=pl.ANY),),
        out_specs=(_HBM, _HBM), input_output_aliases={0: 0, 1: 1},
        compiler_params=pltpu.CompilerParams(has_side_effects=_EFFECT),
    )(src_thru, land_thru, *sems, after)[1]


def _pair_start(gw_in, after):
    hr = gw_in.shape[1] // 2

    def body(src_ref, land_ref, after_ref, *refs):
        x, y, c = _mesh_pos()
        for j in range(N_CHIPS):
            pltpu.make_async_remote_copy(
                src_ref=src_ref.at[j, pl.ds((1 - c) * hr, hr), :], dst_ref=land_ref.at[j], send_sem=refs[j],
                recv_sem=refs[N_CHIPS + j], device_id=(x, y, 1 - c), device_id_type=MESH).start()
        refs[10][...] = jnp.zeros_like(refs[10])

    sem = pltpu.SemaphoreType.DMA(())
    land = lax.empty((N_CHIPS, hr, D_MODEL), F32)
    return pl.pallas_call(
        body, name="pair_start",
        out_shape=(sem,) * 8 + (pltpu.HBM(gw_in.shape, F32), pltpu.HBM(land.shape, F32), jax.ShapeDtypeStruct((8, 128), F32)),
        in_specs=(_HBM, _HBM, pl.BlockSpec(memory_space=pl.ANY)),
        out_specs=(_SEM,) * 8 + (_HBM, _HBM, pl.BlockSpec(memory_space=pltpu.VMEM)),
        input_output_aliases={0: 8, 1: 9},
        compiler_params=pltpu.CompilerParams(has_side_effects=_EFFECT),
    )(pltpu.with_memory_space_constraint(gw_in, pltpu.HBM), pltpu.with_memory_space_constraint(land, pltpu.HBM), after)


def _pair_wait(sems, gw_thru, land_thru, after):
    hr = land_thru.shape[1]

    def body(src_ref, land_ref, *refs):
        x, y, c = _mesh_pos()
        for j in range(N_CHIPS):
            cp = pltpu.make_async_remote_copy(
                src_ref=src_ref.at[j, pl.ds((1 - c) * hr, hr), :], dst_ref=land_ref.at[j], send_sem=refs[j],
                recv_sem=refs[N_CHIPS + j], device_id=(x, y, 1 - c), device_id_type=MESH)
            cp.wait_send()
            cp.wait_recv()

    return pl.pallas_call(
        body, name="pair_wait",
        out_shape=(pltpu.HBM(gw_thru.shape, F32), pltpu.HBM(land_thru.shape, F32)),
        in_specs=(_HBM, _HBM) + (_SEM,) * 8 + (pl.BlockSpec(memory_space=pl.ANY),),
        out_specs=(_HBM, _HBM), input_output_aliases={0: 0, 1: 1},
        compiler_params=pltpu.CompilerParams(has_side_effects=_EFFECT),
    )(gw_thru, land_thru, *sems, after)


def _chip_start(s_in, after):
    def body(src_ref, land_ref, after_ref, *refs):
        x, y, c = _mesh_pos()
        me = 2 * x + y
        for j, (px, py) in enumerate([(1 - x, y), (x, 1 - y), (1 - x, 1 - y)]):
            pltpu.make_async_remote_copy(
                src_ref=src_ref.at[2 * px + py], dst_ref=land_ref.at[me], send_sem=refs[j], recv_sem=refs[3 + j],
                device_id=(px, py, c), device_id_type=MESH).start()
        refs[8][...] = jnp.zeros_like(refs[8])

    sem = pltpu.SemaphoreType.DMA(())
    land = lax.empty(s_in.shape, s_in.dtype)
    return pl.pallas_call(
        body, name="chip_start",
        out_shape=(sem,) * 6 + (pltpu.HBM(s_in.shape, s_in.dtype), pltpu.HBM(land.shape, land.dtype),
                                jax.ShapeDtypeStruct((8, 128), F32)),
        in_specs=(_HBM, _HBM, pl.BlockSpec(memory_space=pl.ANY)),
        out_specs=(_SEM,) * 6 + (_HBM, _HBM, pl.BlockSpec(memory_space=pltpu.VMEM)),
        input_output_aliases={0: 6, 1: 7},
        compiler_params=pltpu.CompilerParams(has_side_effects=_EFFECT),
    )(pltpu.with_memory_space_constraint(s_in, pltpu.HBM), pltpu.with_memory_space_constraint(land, pltpu.HBM), after)


def _chip_wait(sems, s_thru, land_thru, after):
    def body(src_ref, land_ref, *refs):
        x, y, c = _mesh_pos()
        for j, (px, py) in enumerate([(1 - x, y), (x, 1 - y), (1 - x, 1 - y)]):
            cp = pltpu.make_async_remote_copy(
                src_ref=src_ref.at[2 * px + py], dst_ref=land_ref.at[2 * px + py], send_sem=refs[j], recv_sem=refs[3 + j],
                device_id=(px, py, c), device_id_type=MESH)
            cp.wait_send()
            cp.wait_recv()

    return pl.pallas_call(
        body, name="chip_wait",
        out_shape=(pltpu.HBM(s_thru.shape, s_thru.dtype), pltpu.HBM(land_thru.shape, land_thru.dtype)),
        in_specs=(_HBM, _HBM) + (_SEM,) * 6 + (pl.BlockSpec(memory_space=pl.ANY),),
        out_specs=(_HBM, _HBM), input_output_aliases={0: 0, 1: 1},
        compiler_params=pltpu.CompilerParams(has_side_effects=_EFFECT),
    )(s_thru, land_thru, *sems, after)


def _pair_share(h_in, small):
    def body(hin_ref, sm_ref, rin_ref, slots_ref, send_sems, recv_sems, small_send, small_recv, local_sem):
        x, y, c = _mesh_pos()
        dev = 4 * x + 2 * y + c
        mine = pltpu.make_async_copy(sm_ref, slots_ref.at[dev], local_sem)
        mine.start()
        share = pltpu.make_async_remote_copy(
            src_ref=hin_ref, dst_ref=rin_ref, send_sem=send_sems.at[0], recv_sem=recv_sems.at[0],
            device_id=(x, y, 1 - c), device_id_type=MESH)
        share.start()
        started = []
        for k in range(1, 8):
            peer = (x ^ ((k >> 2) & 1), y ^ ((k >> 1) & 1), c ^ (k & 1))
            cp = pltpu.make_async_remote_copy(
                src_ref=sm_ref, dst_ref=slots_ref.at[dev], send_sem=small_send.at[k - 1], recv_sem=small_recv.at[k - 1],
                device_id=peer, device_id_type=MESH)
            cp.start()
            started.append(cp)
        share.wait()
        for k in range(1, 8):
            pltpu.make_async_remote_copy(
                src_ref=sm_ref, dst_ref=slots_ref.at[dev], send_sem=small_send.at[k - 1], recv_sem=small_recv.at[k - 1],
                device_id=(x, y, 1 - c), device_id_type=MESH).wait_recv()
        for cp in started:
            cp.wait_send()
        mine.wait()

    any_spec = pl.BlockSpec(memory_space=pl.ANY)
    return pl.pallas_call(
        body, name="pair_share",
        in_specs=[any_spec] * 2, out_specs=[any_spec] * 2,
        out_shape=[jax.ShapeDtypeStruct(h_in.shape, F32), jax.ShapeDtypeStruct((8,) + small.shape, F32)],
        scratch_shapes=[pltpu.SemaphoreType.DMA((1,)), pltpu.SemaphoreType.DMA((1,)),
                        pltpu.SemaphoreType.DMA((7,)), pltpu.SemaphoreType.DMA((7,)), pltpu.SemaphoreType.DMA],
    )(h_in, small)


def _reduce_w_out_start(slabs, after):
    def body(src_ref, land_ref, after_ref, *refs):
        x, y, c = _mesh_pos()
        me = 4 * x + 2 * y + c
        for k in range(1, 8):
            px, py, pc = x ^ ((k >> 2) & 1), y ^ ((k >> 1) & 1), c ^ (k & 1)
            pltpu.make_async_remote_copy(src_ref=src_ref.at[2 * px + py], dst_ref=land_ref.at[me], send_sem=refs[k - 1],
                                         recv_sem=refs[6 + k], device_id=(px, py, pc), device_id_type=MESH).start()
        refs[16][...] = jnp.zeros_like(refs[16])

    sem = pltpu.SemaphoreType.DMA(())
    land = lax.empty((8,) + slabs.shape[1:], slabs.dtype)
    return pl.pallas_call(
        body, name="reduce_w_out_start",
        out_shape=(sem,) * 14 + (pltpu.HBM(slabs.shape, slabs.dtype), pltpu.HBM(land.shape, land.dtype),
                                 jax.ShapeDtypeStruct((8, 128), F32)),
        in_specs=(_HBM, _HBM, pl.BlockSpec(memory_space=pl.ANY)),
        out_specs=(_SEM,) * 14 + (_HBM, _HBM, pl.BlockSpec(memory_space=pltpu.VMEM)),
        input_output_aliases={0: 14, 1: 15},
        compiler_params=pltpu.CompilerParams(has_side_effects=_EFFECT),
    )(pltpu.with_memory_space_constraint(slabs, pltpu.HBM), pltpu.with_memory_space_constraint(land, pltpu.HBM), after)


def _reduce_w_out_wait(sems, slabs_thru, land_thru, after):
    def body(src_ref, land_ref, *refs):
        x, y, c = _mesh_pos()
        for k in range(1, 8):
            px, py, pc = x ^ ((k >> 2) & 1), y ^ ((k >> 1) & 1), c ^ (k & 1)
            cp = pltpu.make_async_remote_copy(
                src_ref=src_ref.at[2 * px + py], dst_ref=land_ref.at[4 * px + 2 * py + pc], send_sem=refs[k - 1],
                recv_sem=refs[6 + k], device_id=(px, py, pc), device_id_type=MESH)
            cp.wait_send()
            cp.wait_recv()

    return pl.pallas_call(
        body, name="reduce_w_out_wait",
        out_shape=(pltpu.HBM(slabs_thru.shape, slabs_thru.dtype), pltpu.HBM(land_thru.shape, land_thru.dtype)),
        in_specs=(_HBM, _HBM) + (_SEM,) * 14 + (pl.BlockSpec(memory_space=pl.ANY),),
        out_specs=(_HBM, _HBM), input_output_aliases={0: 0, 1: 1},
        compiler_params=pltpu.CompilerParams(has_side_effects=_EFFECT),
    )(slabs_thru, land_thru, *sems, after)


def _pair_add(g, recv, core, name):
    _, rows, C = recv.shape
    tc = 256

    def body(core_ref, g_ref, r_ref, o_ref):
        o_ref[...] = _bf(g_ref[...] + r_ref[...])

    spec = pl.BlockSpec((1, rows, tc), lambda j, i, core: (j, 0, i))
    return pl.pallas_call(
        body, name=name,
        grid_spec=pltpu.PrefetchScalarGridSpec(
            num_scalar_prefetch=1, grid=(N_CHIPS, C // tc),
            in_specs=[pl.BlockSpec((1, rows, tc), lambda j, i, core: (j, core[0], i)), spec], out_specs=spec),
        out_shape=jax.ShapeDtypeStruct((N_CHIPS, rows, C), BF16),
        compiler_params=_cparams(("parallel", "parallel")),
    )(core, g, recv)


def _chip_add(own, parts, chip, name):
    _, rows, C = parts.shape
    tc = 256

    def body(chip_ref, own_ref, r0, r1, r2, r3, o_ref):
        acc = None
        for j, r in enumerate((r0, r1, r2, r3)):
            term = jnp.where(chip_ref[0] == j, own_ref[0], r[0]).astype(F32)
            acc = term if acc is None else acc + term
        o_ref[...] = acc

    def slab(j):
        return pl.BlockSpec((1, rows, tc), lambda i, chip: (jnp.where(chip[0] == j, (j + 1) % N_CHIPS, j), 0, i))

    return pl.pallas_call(
        body, name=name,
        grid_spec=pltpu.PrefetchScalarGridSpec(
            num_scalar_prefetch=1, grid=(C // tc,),
            in_specs=[pl.BlockSpec((1, rows, tc), lambda i, chip: (chip[0], 0, i))] + [slab(j) for j in range(N_CHIPS)],
            out_specs=pl.BlockSpec((rows, tc), lambda i, chip: (0, i))),
        out_shape=jax.ShapeDtypeStruct((rows, C), F32),
        compiler_params=_cparams(("parallel",)),
    )(chip, own, parts, parts, parts, parts)


def _adamw_math(w, g, m, v):
    m = ADAM_B1 * m + (1.0 - ADAM_B1) * g
    v = ADAM_B2 * v + (1.0 - ADAM_B2) * (g * g)
    m_hat = m / (1.0 - ADAM_B1 ** ADAM_STEP)
    v_hat = v / (1.0 - ADAM_B2 ** ADAM_STEP)
    delta = -ADAM_LR * (m_hat / (jnp.sqrt(v_hat) + ADAM_EPS) + ADAM_WD * w)
    return delta, m, v


def _adamw_rows(w, g_own, g_sib, m, v, core, name):
    R, C = w.shape[0], w.shape[-1]
    rows = g_own.shape[0]
    step = 256
    chunks = [(r, min(step, R - r)) for r in range(0, R, step)]
    sub = 64

    def body(core_ref, w_hbm, go_hbm, gs_hbm, m_hbm, v_hbm, d_hbm, nm_hbm, nv_hbm, g_hbm,
             wbuf, mbuf, vbuf, gbuf, dbuf, nmbuf, nvbuf, in_sems, g_sems, out_sems):
        c = core_ref[0]
        flat = lambda ref: ref.at[:, 0, :]
        g_in = [pltpu.make_async_copy(go_hbm, gbuf.at[pl.ds(pl.multiple_of(c * rows, 8), rows), :], g_sems.at[0]),
                pltpu.make_async_copy(gs_hbm, gbuf.at[pl.ds(pl.multiple_of((1 - c) * rows, 8), rows), :], g_sems.at[1])]
        for cp in g_in:
            cp.start()
        loads = []
        for k, (r0, n) in enumerate(chunks):
            cps = [pltpu.make_async_copy(flat(src).at[pl.ds(r0, n), :], dst.at[pl.ds(r0, n), :], in_sems.at[a, k])
                   for a, (src, dst) in enumerate(((w_hbm, wbuf), (m_hbm, mbuf), (v_hbm, vbuf)))]
            for cp in cps:
                cp.start()
            loads.append(cps)
        for cp in g_in:
            cp.wait()
        stores = []
        for k, (r0, n) in enumerate(chunks):
            for cp in loads[k]:
                cp.wait()

            def update(rs):
                g = gbuf[rs, :]
                dl, nm, nv = _adamw_math(wbuf[rs, :], g, mbuf[rs, :], vbuf[rs, :])
                dbuf[rs, :] = dl
                nmbuf[rs, :] = nm
                nvbuf[rs, :] = nv

            if n % sub == 0:
                def block(i, carry, r0=r0):
                    update(pl.ds(pl.multiple_of(r0 + i * sub, 8), sub))
                    return carry
                lax.fori_loop(0, n // sub, block, 0)
            else:
                update(pl.ds(r0, n))
            cps = [pltpu.make_async_copy(src.at[pl.ds(r0, n), :], flat(dst).at[pl.ds(r0, n), :], out_sems.at[a, k])
                   for a, (src, dst) in enumerate(((dbuf, d_hbm), (nmbuf, nm_hbm), (nvbuf, nv_hbm), (gbuf, g_hbm)))]
            for cp in cps:
                cp.start()
            stores += cps
        for cp in stores:
            cp.wait()

    any_spec = pl.BlockSpec(memory_space=pl.ANY)
    dense = pltpu.VMEM((R, C), F32)
    return pl.pallas_call(
        body, name=name,
        grid_spec=pltpu.PrefetchScalarGridSpec(
            num_scalar_prefetch=1, grid=(1,),
            in_specs=[any_spec] * 5, out_specs=[any_spec] * 4,
            scratch_shapes=[dense, dense, dense, pltpu.VMEM((2 * rows, C), F32), dense, dense, dense,
                            pltpu.SemaphoreType.DMA((3, len(chunks))), pltpu.SemaphoreType.DMA((2,)),
                            pltpu.SemaphoreType.DMA((4, len(chunks)))]),
        out_shape=[jax.ShapeDtypeStruct(w.shape, F32)] * 4,
        compiler_params=_cparams(),
    )(core, w, g_own, g_sib, m, v)


def _adamw_sum8(w, slabs, land, m, v, ids, name):
    R, C = w.shape
    tc = 128

    def body(ids_ref, w_ref, own_ref, *refs):
        lrefs, (m_ref, v_ref, d_ref, nm_ref, nv_ref, g_ref) = refs[:8], refs[8:]
        g = None
        for d, l_ref in enumerate(lrefs):
            term = jnp.where(ids_ref[0] == d, own_ref[0], l_ref[0]).astype(F32)
            g = term if g is None else g + term
        dl, nm, nv = _adamw_math(w_ref[...], g, m_ref[...], v_ref[...])
        d_ref[...] = dl
        nm_ref[...] = nm
        nv_ref[...] = nv
        g_ref[...] = g

    def slot(d):
        return pl.BlockSpec((1, R, tc), lambda i, ids: (jnp.where(ids[0] == d, (d + 1) % 8, d), 0, i))

    spec = pl.BlockSpec((R, tc), lambda i, ids: (0, i))
    return pl.pallas_call(
        body, name=name,
        grid_spec=pltpu.PrefetchScalarGridSpec(
            num_scalar_prefetch=1, grid=(C // tc,),
            in_specs=[spec, pl.BlockSpec((1, R, tc), lambda i, ids: (ids[1], 0, i))] + [slot(d) for d in range(8)]
            + [spec, spec],
            out_specs=[spec] * 4),
        out_shape=[jax.ShapeDtypeStruct((R, C), F32)] * 4,
        compiler_params=_cparams(("parallel",)),
    )(ids, w, slabs, *([land] * 8), m, v)


SMALL_NAMES = ("conv_b", "ssd_norm_w", "ln_g", "ln_b", "dt_bias", "a_log", "d_skip", "attn_sinks")
SMALL_FIELDS = ((4, 0, D_XBC), (5, 0, D_SSD), (6, 0, D_MODEL), (7, 0, D_MODEL), (5, 1024, SSD_HEADS), (5, 1152, SSD_HEADS),
                (5, 1280, SSD_HEADS), (5, 1408, ATT_QH))
LOSS_FIELD = (6, 1024, 128)
K_SMALL = D_XBC


def _pack_small(g_conv_w, vecs, loss):
    def body(cw_ref, *refs):
        o_ref = refs[-1]
        o_ref[...] = jnp.zeros_like(o_ref)
        o_ref[0:CONV_K, 0:D_XBC] = cw_ref[...]
        for v_ref, (row, off, n) in zip(refs[:-2], SMALL_FIELDS):
            o_ref[row:row + 1, off:off + n] = v_ref[...]
        o_ref[LOSS_FIELD[0]:LOSS_FIELD[0] + 1, LOSS_FIELD[1]:LOSS_FIELD[1] + LOSS_FIELD[2]] = refs[-2][...]

    return pl.pallas_call(
        body, name="pack_small", out_shape=jax.ShapeDtypeStruct((8, K_SMALL), F32), compiler_params=_cparams(),
    )(g_conv_w, *vecs, loss)


def _adamw_small(slots, chip, conv_w, m_conv_w, v_conv_w, params, moms, vars_):
    n_vec = len(SMALL_NAMES)

    def body(chip_ref, s_ref, *refs):
        ins = refs[:3 * (n_vec + 1)]
        outs = refs[3 * (n_vec + 1):-1]
        tot_ref = refs[-1]
        tot = s_ref[0]
        for d in range(1, 8):
            tot = tot + s_ref[d]
        outs[0][...] = tot[LOSS_FIELD[0]:LOSS_FIELD[0] + 1, LOSS_FIELD[1]:LOSS_FIELD[1] + 1]
        off = pl.multiple_of(chip_ref[0] * CONV_COLS, 128)
        tot_ref[...] = tot
        grads = [tot_ref[0:CONV_K, pl.ds(off, CONV_COLS)]]
        grads += [tot[row:row + 1, o:o + n] for row, o, n in SMALL_FIELDS]
        for k, g in enumerate(grads):
            w_ref, m_ref, v_ref = ins[3 * k:3 * k + 3]
            full = (0,) if k == 0 else (Ellipsis,)
            d, nm, nv = _adamw_math(w_ref[full], g, m_ref[full], v_ref[full])
            for o_ref, val in zip(outs[1 + 4 * k:5 + 4 * k], (g, d, nm, nv)):
                o_ref[full] = val

    args = [conv_w, m_conv_w, v_conv_w]
    for w, m, v in zip(params, moms, vars_):
        args += [w, m, v]
    shapes = [jax.ShapeDtypeStruct((1, 1), F32)] + [jax.ShapeDtypeStruct(conv_w.shape, F32)] * 4
    for w in params:
        shapes += [jax.ShapeDtypeStruct(w.shape, F32)] * 4
    vmem = pl.BlockSpec(memory_space=pltpu.VMEM)
    return pl.pallas_call(
        body, name="adamw_small",
        grid_spec=pltpu.PrefetchScalarGridSpec(
            num_scalar_prefetch=1, grid=(1,),
            in_specs=[pl.BlockSpec(slots.shape, lambda i, chip: (0, 0, 0))] + [vmem] * len(args),
            out_specs=[vmem] * len(shapes), scratch_shapes=[pltpu.VMEM((8, K_SMALL), F32)]),
        out_shape=shapes, compiler_params=_cparams(),
    )(chip, slots, *args)


def kernel(x, positions, w_in, conv_w, conv_b, dt_bias, a_log, d_skip, ssd_norm_w, attn_sinks, w_out, ln_g, ln_b, loss_target, m_w_in, m_conv_w, m_conv_b, m_dt_bias, m_a_log, m_d_skip, m_ssd_norm_w, m_attn_sinks, m_w_out, m_ln_g, m_ln_b, v_w_in, v_conv_w, v_conv_b, v_dt_bias, v_a_log, v_d_skip, v_ssd_norm_w, v_attn_sinks, v_w_out, v_ln_g, v_ln_b):
    mx, my, mc = _mesh_pos()
    chip = 2 * mx + my
    L = x.shape[1]

    conv_w_s8 = jnp.pad(conv_w[0], ((0, 8 - CONV_K), (0, 0)))
    pad_rows = ((0, SLAB_ROWS - W_IN_COLS), (0, 0))
    w_in_t = w_in[0].T
    w_in_b, w_out_b = jnp.pad(_bf(w_in_t), pad_rows), _bf(w_out[0])
    ag_in, ag_cw = _gather_weights(w_in_b, conv_w_s8)
    started = _gather_w_out_start(w_out_b, ag_cw)
    own = (jnp.arange(N_CHIPS) == chip)[:, None, None]

    def get_w_out(after):
        landed = _gather_w_out_wait(started[0:6], started[6], started[7], after)
        return jnp.where(own, w_out_b[None], landed).reshape(D_MIX, D_MODEL)

    ag_in = jnp.where(own, w_in_b[None], ag_in)
    w_full = jnp.concatenate([ag_in[j, 0:W_IN_COLS] for j in range(N_CHIPS)], axis=0)
    w = jnp.concatenate([
        w_full[O_Z:O_Z + D_SSD], w_full[O_G:O_G + D_ATT], w_full[O_Q:O_Q + D_ATT],
        w_full[O_XBC:O_XBC + D_XBC], w_full[O_K:O_K + 2 * D_KV], w_full[O_DT:O_DT + SSD_HEADS],
        jnp.zeros((DT_PAD - SSD_HEADS, D_MODEL), BF16)], axis=0)
    conv_w_full = jnp.concatenate([ag_cw[j, 0:CONV_K] for j in range(N_CHIPS)], axis=1)

    loss_part, gx_args, gw_in, w_out_red, small = _local_step(
        x[0], positions[0].reshape(L, 1), loss_target[0], w, get_w_out, started[8][0:1, :], conv_w_full,
        conv_b, dt_bias, a_log, d_skip, ssd_norm_w, attn_sinks, ln_g, ln_b)

    packed = _pack_small(small["conv_w"], [small[n] for n in SMALL_NAMES], loss_part)
    core_id = mc.reshape(1).astype(jnp.int32)
    chip_id = chip.reshape(1).astype(jnp.int32)
    ids = jnp.stack([4 * mx + 2 * my + mc, chip]).astype(jnp.int32)
    slabs = jnp.stack([jnp.pad(gw_in[W_IN_COLS * j:W_IN_COLS * (j + 1)], pad_rows) for j in range(N_CHIPS)])
    w_in_red = _pair_start(slabs, packed)
    grad_x = _grad_x(*gx_args, w_in_red[10], 0)
    gw_in_slabs, recv_in = _pair_wait(w_in_red[0:8], w_in_red[8], w_in_red[9], grad_x[0:8, 0:128])
    s_in = _pair_add(gw_in_slabs, recv_in, core_id, "pair_add_in")
    chip_red = _chip_start(s_in, packed)
    grad_x = _grad_x(*gx_args, chip_red[8], 1, grad_x)
    own_slabs, landed = _reduce_w_out_wait(w_out_red[0:14], w_out_red[14], w_out_red[15], grad_x)
    out_t = _adamw_sum8(w_out[0], own_slabs, landed, m_w_out[0], v_w_out[0], ids, "adamw_w_out")
    d_w_out, nm_w_out, nv_w_out, g_w_out = [a[None] for a in out_t]
    s_in, r_in = _chip_wait(chip_red[0:6], chip_red[6], chip_red[7], out_t[0])
    h_in = _chip_add(s_in, r_in, chip_id, "chip_add_in")
    sib_in, slots = _pair_share(h_in, packed)

    to_rows = lambda a: jnp.transpose(a, (2, 0, 1))
    in_t = _adamw_rows(to_rows(w_in), h_in, sib_in, to_rows(m_w_in), to_rows(v_w_in), core_id, "adamw_w_in")
    d_w_in, nm_w_in, nv_w_in, g_w_in = [jnp.transpose(a, (1, 2, 0)) for a in in_t]

    params = dict(conv_b=conv_b, ssd_norm_w=ssd_norm_w, ln_g=ln_g, ln_b=ln_b, dt_bias=dt_bias, a_log=a_log,
                  d_skip=d_skip, attn_sinks=attn_sinks)
    moms = dict(conv_b=m_conv_b, ssd_norm_w=m_ssd_norm_w, ln_g=m_ln_g, ln_b=m_ln_b, dt_bias=m_dt_bias, a_log=m_a_log,
                d_skip=m_d_skip, attn_sinks=m_attn_sinks)
    vars_ = dict(conv_b=v_conv_b, ssd_norm_w=v_ssd_norm_w, ln_g=v_ln_g, ln_b=v_ln_b, dt_bias=v_dt_bias, a_log=v_a_log,
                 d_skip=v_d_skip, attn_sinks=v_attn_sinks)
    res = _adamw_small(slots, chip_id, conv_w, m_conv_w, v_conv_w, [params[n] for n in SMALL_NAMES],
                       [moms[n] for n in SMALL_NAMES], [vars_[n] for n in SMALL_NAMES])
    loss = res[0][0, 0]
    grads, delta, new_m, new_v = {}, {}, {}, {}
    for k, n in enumerate(("conv_w",) + SMALL_NAMES):
        grads[n], delta[n], new_m[n], new_v[n] = res[1 + 4 * k:5 + 4 * k]
    for dd, a_in, a_out in ((grads, g_w_in, g_w_out), (delta, d_w_in, d_w_out), (new_m, nm_w_in, nm_w_out),
                            (new_v, nv_w_in, nv_w_out)):
        dd["w_in"] = a_in
        dd["w_out"] = a_out
    order = ("w_in", "conv_w", "conv_b", "dt_bias", "a_log", "d_skip", "ssd_norm_w", "attn_sinks", "w_out", "ln_g", "ln_b")
    return (loss, grad_x[None], *[grads[n] for n in order], *[delta[n] for n in order], *[new_m[n] for n in order],
            *[new_v[n] for n in order])
```

```python
import numpy as np
import jax
import jax.numpy as jnp
from jax import lax
from jax.experimental import pallas as pl
from jax.experimental.pallas import tpu as pltpu

F32 = jnp.float32
BF16 = jnp.bfloat16
MESH = pl.DeviceIdType.MESH

D_MODEL = 1024
D_SSD = 1024
D_ATT = 1024
D_MIX = 2048
SSD_HEADS = 16
SSD_P = 64
SSD_GROUPS = 2
SSD_R = 8
SSD_N = 128
D_BC = 256
D_XBC = 1536
CONV_K = 4
CHUNK = 128
ATT_HD = 64
ATT_QH = 16
ATT_KVH = 4
ATT_R = 4
D_KV = 256
WINDOW = 128
ROPE_THETA = 500000.0
ROPE_DIM = 16
ALPHA = 2.0 ** 0.25
LN_EPS = 1e-5
RMS_EPS = 1e-5
D_IN_PROJ = 5136
O_Z, O_XBC, O_DT, O_Q, O_K, O_V, O_G = 0, 1024, 2560, 2576, 3600, 3856, 4112
P_Z, P_G, P_Q, P_XBC, P_KV, P_DT, P_END = 0, 1024, 2048, 3072, 4608, 5120, 5248
DT_PAD = 128
N_CHIPS = 4
W_IN_COLS = D_IN_PROJ // N_CHIPS
SLAB_ROWS = 1312
W_OUT_ROWS = D_MIX // N_CHIPS
CONV_COLS = D_XBC // N_CHIPS

ADAM_LR = 0.001
ADAM_B1 = 0.9
ADAM_B2 = 0.999
ADAM_EPS = 1e-08
ADAM_WD = 0.01
ADAM_STEP = 10

VMEM_LIMIT = 56 * 1024 * 1024
ROW_TILE = 512
NEG_BIG = -1e30
HI = lax.Precision.HIGHEST


def _cparams(sem=None, **kw):
    if sem is not None:
        kw["dimension_semantics"] = sem
    return pltpu.CompilerParams(vmem_limit_bytes=VMEM_LIMIT, **kw)


def _dot(a, b):
    return jnp.dot(a, b, preferred_element_type=F32)


def _dot_nt(a, b):
    return lax.dot_general(a, b, (((1,), (1,)), ((), ())), preferred_element_type=F32)


def _dot_tn(a, b):
    return lax.dot_general(a, b, (((0,), (0,)), ((), ())), preferred_element_type=F32)


def _bf(a):
    return a.astype(BF16)


def _iota2(shape, dim):
    return lax.broadcasted_iota(jnp.int32, shape, dim)


def _to_rows(col):
    k = col.shape[1]
    eye = (_iota2((k, k), 0) == _iota2((k, k), 1)).astype(F32)
    return lax.dot_general(eye, col, (((1,), (1,)), ((), ())), preferred_element_type=F32, precision=HI)


def _to_cols(row):
    n = row.shape[1]
    eye = (_iota2((n, n), 0) == _iota2((n, n), 1)).astype(F32)
    return lax.dot_general(eye, row, (((1,), (1,)), ((), ())), preferred_element_type=F32, precision=HI)


def _sigmoid(x):
    return jax.nn.sigmoid(x)


def _in_proj(x, w, pos, inv):
    L = x.shape[0]
    tm = ROW_TILE
    widths = (D_SSD, D_ATT, D_ATT, D_XBC, 2 * D_KV, DT_PAD)

    def body(x_ref, w_ref, pos_ref, inv_ref, z_ref, g_ref, q_ref, xbc_ref, kv_ref, dt_ref, xb_ref):
        xb = _bf(x_ref[...])
        xb_ref[...] = xb
        for o_ref, off, wd in zip((z_ref, g_ref, xbc_ref, dt_ref), (P_Z, P_G, P_XBC, P_DT), (D_SSD, D_ATT, D_XBC, DT_PAD)):
            o_ref[...] = _dot_nt(xb, w_ref[off:off + wd, :])
        tabs = _rope_tables(pos_ref, inv_ref)
        q_ref[...] = _bf(_rope(_dot_nt(xb, w_ref[P_Q:P_Q + D_ATT, :]), tabs))
        kv_ref[:, 0:D_KV] = _bf(_rope(_dot_nt(xb, w_ref[P_KV:P_KV + D_KV, :]), tabs))
        kv_ref[:, D_KV:2 * D_KV] = _bf(_dot_nt(xb, w_ref[P_KV + D_KV:P_KV + 2 * D_KV, :]))

    row = lambda wd: pl.BlockSpec((tm, wd), lambda i: (i, 0))
    return pl.pallas_call(
        body, name="in_proj", grid=(L // tm,),
        in_specs=[row(D_MODEL), pl.BlockSpec((P_END, D_MODEL), lambda i: (0, 0), pipeline_mode=pl.Buffered(1)), row(1),
                  pl.BlockSpec((1, 2 * ATT_HD), lambda i: (0, 0))],
        out_specs=[row(wd) for wd in widths] + [row(D_MODEL)],
        out_shape=[jax.ShapeDtypeStruct((L, wd), dt) for wd, dt in zip(widths, (F32, F32, BF16, F32, BF16, F32))]
        + [jax.ShapeDtypeStruct((L, D_MODEL), BF16)],
        compiler_params=_cparams(("parallel",)),
    )(x, w, pos, inv)


def _matmuls_tn(a_list, b, name):
    K, N = b.shape
    tk = min(K, 1024)
    n = len(a_list)

    def body(*refs):
        b_ref = refs[n]
        k = pl.program_id(0)
        bb = _bf(b_ref[...])
        for a_ref, o_ref in zip(refs[:n], refs[n + 1:]):
            part = _dot_tn(_bf(a_ref[...]), bb)

            @pl.when(k == 0)
            def _():
                o_ref[...] = part

            @pl.when(k > 0)
            def _():
                o_ref[...] += part

    return pl.pallas_call(
        body, name=name, grid=(K // tk,),
        in_specs=[pl.BlockSpec((tk, a.shape[1]), lambda k: (k, 0)) for a in a_list] + [pl.BlockSpec((tk, N), lambda k: (k, 0))],
        out_specs=[pl.BlockSpec((a.shape[1], N), lambda k: (0, 0)) for a in a_list],
        out_shape=[jax.ShapeDtypeStruct((a.shape[1], N), F32) for a in a_list],
        compiler_params=_cparams(("arbitrary",)),
    )(*a_list, b)


def _grad_x(dr, dz, dg, dq, dxbc, dkv, ddt, w, after, part, prev=None):
    L = dr.shape[0]
    tm = min(ROW_TILE, L // 4)
    first = L // (4 * tm)
    n = first if part == 0 else L // tm - first
    widths = (D_SSD, D_ATT, D_ATT, D_XBC, 2 * D_KV, DT_PAD)
    offs = (P_Z, P_G, P_Q, P_XBC, P_KV, P_DT)

    def body(dr_ref, dz_ref, dg_ref, dq_ref, dxbc_ref, dkv_ref, ddt_ref, w_ref, after_ref, *rest):
        o_ref = rest[-1]
        acc = ALPHA * dr_ref[...]
        for p_ref, off, wd in zip((dz_ref, dg_ref, dq_ref, dxbc_ref, dkv_ref, ddt_ref), offs, widths):
            acc = acc + _dot(_bf(p_ref[...]), w_ref[off:off + wd, :])
        o_ref[...] = acc

    row = lambda wd: pl.BlockSpec((tm, wd), lambda i: (i + part * first, 0))
    ins = [dr, dz, dg, dq, dxbc, dkv, ddt, w, after]
    specs = ([row(D_MODEL)] + [row(wd) for wd in widths]
             + [pl.BlockSpec((P_END, D_MODEL), lambda i: (0, 0), pipeline_mode=pl.Buffered(1)),
                pl.BlockSpec((8, 128), lambda i: (0, 0))])
    if prev is not None:
        ins.append(prev)
        specs.append(pl.BlockSpec(memory_space=pl.ANY))
    return pl.pallas_call(
        body, name="grad_x_%d" % part, grid=(n,),
        in_specs=specs, out_specs=row(D_MODEL),
        out_shape=jax.ShapeDtypeStruct((L, D_MODEL), F32),
        input_output_aliases={} if prev is None else {len(ins) - 1: 0},
        compiler_params=_cparams(("parallel",)),
    )(*ins)


HALO = 16


def _shift_matrix(offsets):
    n = CHUNK + HALO
    m = np.zeros((len(offsets) * CHUNK, 2 * n), np.float32)
    for k, off in enumerate(offsets):
        t = np.arange(CHUNK)
        m[k * CHUNK + t, t + off] = 1.0
        m[k * CHUNK + t, n + t + off] = 1.0
    return jnp.asarray(m, BF16)


def _shifted_rows(first_part, second_part, smat_ref):
    h1, l1 = _hi_lo(first_part)
    h2, l2 = _hi_lo(second_part)
    sh = _dot(smat_ref[...], jnp.concatenate([h1, h2, l1, l2], axis=0))
    return sh[0:CHUNK], sh[CHUNK:2 * CHUNK], sh[2 * CHUNK:3 * CHUNK]


def _ssd_chunk_pre(first, xbc_ref, tail_ref, dt_ref, cw_ref, cb_ref, dtb_ref, alog_ref, smat_ref):
    tail = jnp.where(first, 0.0, tail_ref[...])
    x = xbc_ref[...]
    taps = _shifted_rows(tail, x, smat_ref) + (x,)
    u = cb_ref[...] + cw_ref[0:1, :] * taps[0]
    for k in range(1, CONV_K):
        u = u + cw_ref[k:k + 1, :] * taps[k]
    sig = _sigmoid(u)
    xbc = u * sig
    dtraw = dt_ref[:, 0:SSD_HEADS] + dtb_ref[...]
    dt = jax.nn.softplus(dtraw)
    A = -jnp.exp(alog_ref[...])
    a = dt * A
    tril = (_iota2((CHUNK, CHUNK), 0) >= _iota2((CHUNK, CHUNK), 1)).astype(F32)
    acs = jnp.dot(tril, a, preferred_element_type=F32, precision=HI)
    acs_row = _to_rows(acs)
    return u, sig, xbc, dtraw, dt, A, acs, acs_row, taps


def _head_expander():
    return (_iota2((SSD_HEADS, D_SSD), 1) // SSD_P == _iota2((SSD_HEADS, D_SSD), 0)).astype(BF16)


def _hi_lo(x):
    hi = _bf(x)
    return hi, _bf(x - hi.astype(F32))


def _expand(v, e):
    hi, lo = _hi_lo(v)
    return _dot(hi, e) + _dot(lo, e)


def _headsum(t, e):
    m = t.shape[0]
    if m < 8:
        t = jnp.broadcast_to(t[0:1], (8, t.shape[1]))
    hi, lo = _hi_lo(t)
    return (_dot_nt(hi, e) + _dot_nt(lo, e))[0:m]


def _ssd_decays(dt, acs, dsk_ref, e):
    alast = acs[CHUNK - 1:CHUNK, :]
    stk = jnp.concatenate([dt, jnp.exp(acs), jnp.exp(alast - acs),
                           jnp.broadcast_to(jnp.exp(alast), (8, SSD_HEADS)),
                           jnp.broadcast_to(dsk_ref[...], (8, SSD_HEADS))], axis=0)
    ex = _expand(stk, e)
    return (ex[0:CHUNK], ex[CHUNK:2 * CHUNK], ex[2 * CHUNK:3 * CHUNK], ex[3 * CHUNK:3 * CHUNK + 1],
            ex[3 * CHUNK + 8:3 * CHUNK + 9])


def _ssd_fwd(z, xbc, dtp, conv_w, conv_b, dt_bias, a_log, d_skip, norm_w):
    L = z.shape[0]
    nc = L // CHUNK
    half = D_SSD // SSD_GROUPS

    def body(z_ref, xbc_ref, tail_ref, dt_ref, cw_ref, cb_ref, dtb_ref, alog_ref, dsk_ref, nw_ref, smat_ref,
             y_ref, ypre_ref, prev_ref, state, ybuf, mbuf):
        c = pl.program_id(0)

        @pl.when(c == 0)
        def _():
            state[...] = jnp.zeros_like(state)

        u, sig, xbcv, dtraw, dt, A, acs, acs_row, _ = _ssd_chunk_pre(
            c == 0, xbc_ref, tail_ref, dt_ref, cw_ref, cb_ref, dtb_ref, alog_ref, smat_ref)
        e = _head_expander()
        dtE, eacsE, dsdE, ealE, dskE = _ssd_decays(dt, acs, dsk_ref, e)
        xs = xbcv[:, 0:D_SSD]
        X = xs * dtE
        prev_ref[0] = state[...]
        causal = _iota2((CHUNK, CHUNK), 0) >= _iota2((CHUNK, CHUNK), 1)
        for g in range(SSD_GROUPS):
            gs = slice(half * g, half * (g + 1))
            Bg = _bf(xbcv[:, D_SSD + SSD_N * g:D_SSD + SSD_N * (g + 1)])
            Cg = _bf(xbcv[:, D_SSD + D_BC + SSD_N * g:D_SSD + D_BC + SSD_N * (g + 1)])
            cb = _dot_nt(Cg, Bg)
            for r in range(SSD_R):
                h = g * SSD_R + r
                seg = acs[:, h:h + 1] - acs_row[h:h + 1, :]
                mbuf[h] = _bf(cb * jnp.where(causal, jnp.exp(jnp.where(causal, seg, 0.0)), 0.0))
            st = state[:, gs]
            ybuf[:, gs] = _dot(Cg, _bf(st)) * eacsE[:, gs] + dskE[:, gs] * xs[:, gs]
            state[:, gs] = st * ealE[:, gs] + _dot_tn(Bg, _bf(X[:, gs] * dsdE[:, gs]))
        Xb = _bf(X)
        for h in range(SSD_HEADS):
            hs = slice(SSD_P * h, SSD_P * (h + 1))
            ybuf[:, hs] += _dot(mbuf[h], Xb[:, hs])
        y = ybuf[...]
        ypre_ref[...] = y
        zv = z_ref[...]
        yf = y * (zv * _sigmoid(zv))
        for g in range(SSD_GROUPS):
            gs = slice(half * g, half * (g + 1))
            yg = yf[:, gs]
            ms = jnp.mean(yg * yg, axis=-1, keepdims=True)
            y_ref[:, gs] = _bf(yg * lax.rsqrt(ms + RMS_EPS) * nw_ref[:, gs])

    full = lambda shape: pl.BlockSpec(shape, lambda c: (0, 0))
    return pl.pallas_call(
        body, name="ssd_fwd", grid=(nc,),
        in_specs=[
            pl.BlockSpec((CHUNK, D_SSD), lambda c: (c, 0)),
            pl.BlockSpec((CHUNK, D_XBC), lambda c: (c, 0)),
            pl.BlockSpec((HALO, D_XBC), lambda c: (jnp.maximum(c * (CHUNK // HALO) - 1, 0), 0)),
            pl.BlockSpec((CHUNK, DT_PAD), lambda c: (c, 0)),
            full((CONV_K, D_XBC)), full((1, D_XBC)), full((1, SSD_HEADS)), full((1, SSD_HEADS)), full((1, SSD_HEADS)),
            full((1, D_SSD)), full((3 * CHUNK, 2 * (CHUNK + HALO))),
        ],
        out_specs=[
            pl.BlockSpec((CHUNK, D_SSD), lambda c: (c, 0)),
            pl.BlockSpec((CHUNK, D_SSD), lambda c: (c, 0)),
            pl.BlockSpec((1, SSD_N, D_SSD), lambda c: (c, 0, 0)),
        ],
        out_shape=[
            jax.ShapeDtypeStruct((L, D_SSD), BF16),
            jax.ShapeDtypeStruct((L, D_SSD), F32),
            jax.ShapeDtypeStruct((nc, SSD_N, D_SSD), F32),
        ],
        scratch_shapes=[
            pltpu.VMEM((SSD_N, D_SSD), F32),
            pltpu.VMEM((CHUNK, D_SSD), F32),
            pltpu.VMEM((SSD_HEADS, CHUNK, CHUNK), BF16),
        ],
        compiler_params=_cparams(("arbitrary",)),
    )(z, xbc, xbc, dtp, conv_w, conv_b, dt_bias, a_log, d_skip, norm_w, _shift_matrix((13, 14, 15)))


def _ssd_bwd(dy, z, ypre, xbc, dtp, prev, conv_w, conv_b, dt_bias, a_log, d_skip, norm_w):
    L = z.shape[0]
    nc = L // CHUNK
    half = D_SSD // SSD_GROUPS

    def body(dy_ref, z_ref, ypre_ref, xbc_ref, tail_ref, dt_ref, prev_ref, cw_ref, cb_ref, dtb_ref, alog_ref, dsk_ref,
             nw_ref, smat_ref, smat2_ref, dz_ref, dxbc_ref, ddt_ref, gcw_ref, gcb_ref, gdtb_ref, galog_ref, gdsk_ref,
             gnw_ref, dstate, dhead, dpost, yobuf, bdbuf, lmbuf, dmbuf, cbbuf):
        i = pl.program_id(0)
        c = nc - 1 - i

        @pl.when(i == 0)
        def _():
            dstate[...] = jnp.zeros_like(dstate)
            dhead[...] = jnp.zeros_like(dhead)
            gcw_ref[...] = jnp.zeros_like(gcw_ref)
            gcb_ref[...] = jnp.zeros_like(gcb_ref)
            gdtb_ref[...] = jnp.zeros_like(gdtb_ref)
            galog_ref[...] = jnp.zeros_like(galog_ref)
            gdsk_ref[...] = jnp.zeros_like(gdsk_ref)
            gnw_ref[...] = jnp.zeros_like(gnw_ref)

        u, sig, xbcv, dtraw, dt, A, acs, acs_row, taps = _ssd_chunk_pre(
            c == 0, xbc_ref, tail_ref, dt_ref, cw_ref, cb_ref, dtb_ref, alog_ref, smat_ref)
        e = _head_expander()
        dtE, eacsE, dsdE, ealE, dskE = _ssd_decays(dt, acs, dsk_ref, e)
        alast = acs[CHUNK - 1:CHUNK, :]
        xs = xbcv[:, 0:D_SSD]
        X = xs * dtE
        Xb = _bf(X)

        zv = z_ref[...]
        ypre = ypre_ref[...]
        dyn = dy_ref[...]
        sz = _sigmoid(zv)
        silu_z = zv * sz
        yf = ypre * silu_z
        dyf_parts = []
        for g in range(SSD_GROUPS):
            gs = slice(half * g, half * (g + 1))
            yg = yf[:, gs]
            rstd = lax.rsqrt(jnp.mean(yg * yg, axis=-1, keepdims=True) + RMS_EPS)
            dout = dyn[:, gs]
            gnw_ref[:, gs] += jnp.sum(dout * yg * rstd, axis=0, keepdims=True)
            dyhat = dout * nw_ref[:, gs]
            dyf_parts.append(rstd * (dyhat - yg * (rstd * rstd) * jnp.mean(dyhat * yg, axis=-1, keepdims=True)))
        dyf = jnp.concatenate(dyf_parts, axis=1)
        dz_ref[...] = _bf(dyf * ypre * (sz * (1.0 + zv * (1.0 - sz))))
        dyp = dyf * silu_z
        dyb = _bf(dyp)
        G = dyp * eacsE

        causal = _iota2((CHUNK, CHUNK), 0) >= _iota2((CHUNK, CHUNK), 1)
        ST = prev_ref[0]
        dST = dstate[...]
        for g in range(SSD_GROUPS):
            gs = slice(half * g, half * (g + 1))
            bs = slice(D_SSD + SSD_N * g, D_SSD + SSD_N * (g + 1))
            cs = slice(D_SSD + D_BC + SSD_N * g, D_SSD + D_BC + SSD_N * (g + 1))
            Bg = _bf(xbcv[:, bs])
            Cg = _bf(xbcv[:, cs])
            Gb = _bf(G[:, gs])
            STb = _bf(ST[:, gs])
            dSTb = _bf(dST[:, gs])
            dstate[:, gs] = dST[:, gs] * ealE[:, gs] + _dot_tn(Cg, Gb)
            yobuf[:, gs] = _dot(Cg, STb) * eacsE[:, gs]
            bdbuf[:, gs] = _dot(Bg, dSTb)
            dpost[:, cs] = _dot_nt(Gb, STb)
            dpost[:, bs] = _dot_nt(_bf(X[:, gs] * dsdE[:, gs]), dSTb)
            cbbuf[g] = _dot_nt(Cg, Bg)
            for r in range(SSD_R):
                h = g * SSD_R + r
                seg = acs[:, h:h + 1] - acs_row[h:h + 1, :]
                lmbuf[h] = jnp.where(causal, jnp.exp(jnp.where(causal, seg, 0.0)), 0.0)
        for h in range(SSD_HEADS):
            hs = slice(SSD_P * h, SSD_P * (h + 1))
            Mb = _bf(cbbuf[h // SSD_R] * lmbuf[h])
            dmbuf[h] = _dot_nt(dyb[:, hs], Xb[:, hs])
            dpost[:, hs] = _dot_tn(Mb, dyb[:, hs])
        lane16 = _iota2((1, SSD_HEADS), 1)
        sub16 = _iota2((SSD_HEADS, 1), 0)
        dacs_col = jnp.zeros((CHUNK, SSD_HEADS), F32)
        dacs_row = jnp.zeros((SSD_HEADS, CHUNK), F32)
        for g in range(SSD_GROUPS):
            bs = slice(D_SSD + SSD_N * g, D_SSD + SSD_N * (g + 1))
            cs = slice(D_SSD + D_BC + SSD_N * g, D_SSD + D_BC + SSD_N * (g + 1))
            cb = cbbuf[g]
            dcb = jnp.zeros((CHUNK, CHUNK), F32)
            for r in range(SSD_R):
                h = g * SSD_R + r
                dM = dmbuf[h]
                Lm = lmbuf[h]
                dcb = dcb + dM * Lm
                dseg = dM * (cb * Lm)
                dacs_col = dacs_col + jnp.sum(dseg, axis=-1, keepdims=True) * (lane16 == h).astype(F32)
                dacs_row = dacs_row - jnp.sum(dseg, axis=0, keepdims=True) * (sub16 == h).astype(F32)
            dcbb = _bf(dcb)
            dpost[:, bs] += _dot_tn(dcbb, _bf(xbcv[:, cs]))
            dpost[:, cs] += _dot(dcbb, _bf(xbcv[:, bs]))

        BD = bdbuf[...]
        dX = dpost[:, 0:D_SSD] + dsdE * BD
        dsd = jnp.exp(alast - acs)
        T = _headsum(X * BD, e) * dsd
        dalast = jnp.sum(T, axis=0, keepdims=True) + _headsum(
            jnp.sum(dST * ST, axis=0, keepdims=True), e) * jnp.exp(alast)
        is_last = (_iota2((CHUNK, 1), 0) == CHUNK - 1).astype(F32)
        dacs = dacs_col + _to_cols(dacs_row) + _headsum(dyp * yobuf[...], e) - T + is_last * dalast
        triu = (_iota2((CHUNK, CHUNK), 0) <= _iota2((CHUNK, CHUNK), 1)).astype(F32)
        da = jnp.dot(triu, dacs, preferred_element_type=F32, precision=HI)
        ddt_tot = _headsum(dX * xs, e) + da * A
        galog_ref[...] += jnp.sum(da * dt, axis=0, keepdims=True) * A
        ddtraw = ddt_tot * _sigmoid(dtraw)
        gdtb_ref[...] += jnp.sum(ddtraw, axis=0, keepdims=True)
        gdsk_ref[...] += _headsum(jnp.sum(dyp * xs, axis=0, keepdims=True), e)
        ddt_ref[...] = jnp.zeros_like(ddt_ref)
        ddt_ref[:, 0:SSD_HEADS] = ddtraw
        dpost[:, 0:D_SSD] = dX * dtE + dskE * dyp

        dconv = dpost[...] * (sig * (1.0 + u * (1.0 - sig)))
        gcb_ref[...] += jnp.sum(dconv, axis=0, keepdims=True)
        for k in range(CONV_K):
            gcw_ref[k:k + 1, :] += jnp.sum(dconv * taps[k], axis=0, keepdims=True)
        later = _shifted_rows(dconv, dhead[...], smat2_ref)
        dx = cw_ref[CONV_K - 1:CONV_K, :] * dconv
        for k in range(CONV_K - 1):
            dx = dx + cw_ref[k:k + 1, :] * later[k]
        dxbc_ref[...] = _bf(dx)
        dhead[...] = dconv[0:HALO, :]

    full = lambda shape: pl.BlockSpec(shape, lambda i: (0, 0))
    rev = lambda wd: pl.BlockSpec((CHUNK, wd), lambda i: (nc - 1 - i, 0))
    return pl.pallas_call(
        body, name="ssd_bwd", grid=(nc,),
        in_specs=[
            rev(D_SSD), rev(D_SSD), rev(D_SSD), rev(D_XBC),
            pl.BlockSpec((HALO, D_XBC), lambda i: (jnp.maximum((nc - 1 - i) * (CHUNK // HALO) - 1, 0), 0)),
            rev(DT_PAD),
            pl.BlockSpec((1, SSD_N, D_SSD), lambda i: (nc - 1 - i, 0, 0)),
            full((CONV_K, D_XBC)), full((1, D_XBC)), full((1, SSD_HEADS)), full((1, SSD_HEADS)), full((1, SSD_HEADS)),
            full((1, D_SSD)), full((3 * CHUNK, 2 * (CHUNK + HALO))), full((3 * CHUNK, 2 * (CHUNK + HALO))),
        ],
        out_specs=[
            rev(D_SSD), rev(D_XBC), rev(DT_PAD),
            full((CONV_K, D_XBC)), full((1, D_XBC)), full((1, SSD_HEADS)), full((1, SSD_HEADS)), full((1, SSD_HEADS)),
            full((1, D_SSD)),
        ],
        out_shape=[
            jax.ShapeDtypeStruct((L, D_SSD), BF16), jax.ShapeDtypeStruct((L, D_XBC), BF16),
            jax.ShapeDtypeStruct((L, DT_PAD), F32),
            jax.ShapeDtypeStruct((CONV_K, D_XBC), F32), jax.ShapeDtypeStruct((1, D_XBC), F32),
            jax.ShapeDtypeStruct((1, SSD_HEADS), F32), jax.ShapeDtypeStruct((1, SSD_HEADS), F32),
            jax.ShapeDtypeStruct((1, SSD_HEADS), F32), jax.ShapeDtypeStruct((1, D_SSD), F32),
        ],
        scratch_shapes=[
            pltpu.VMEM((SSD_N, D_SSD), F32),
            pltpu.VMEM((HALO, D_XBC), F32),
            pltpu.VMEM((CHUNK, D_XBC), F32),
            pltpu.VMEM((CHUNK, D_SSD), F32),
            pltpu.VMEM((CHUNK, D_SSD), F32),
            pltpu.VMEM((SSD_HEADS, CHUNK, CHUNK), F32),
            pltpu.VMEM((SSD_HEADS, CHUNK, CHUNK), F32),
            pltpu.VMEM((SSD_GROUPS, CHUNK, CHUNK), F32),
        ],
        compiler_params=_cparams(("arbitrary",)),
    )(dy, z, ypre, xbc, xbc, dtp, prev, conv_w, conv_b, dt_bias, a_log, d_skip, norm_w, _shift_matrix((13, 14, 15)),
      _shift_matrix((3, 2, 1)))


def _rope_tables(pos_ref, inv_ref):
    ang = pos_ref[...].astype(F32) * inv_ref[...]
    d = _iota2((1, 2 * ATT_HD), 1) % ATT_HD
    s = jnp.sin(ang)
    return jnp.cos(ang), jnp.where(d < ROPE_DIM // 2, -s, 0.0), jnp.where((d >= ROPE_DIM // 2) & (d < ROPE_DIM), s, 0.0)


def _rope(t, tabs):
    c, s1, s2 = tabs
    n = t.shape[1]
    rep = n // c.shape[1]
    return (t * jnp.tile(c, (1, rep)) + pltpu.roll(t, n - ROPE_DIM // 2, 1) * jnp.tile(s1, (1, rep))
            + pltpu.roll(t, ROPE_DIM // 2, 1) * jnp.tile(s2, (1, rep)))


def _rope_t(t, tabs):
    c, s1, s2 = tabs
    n = t.shape[1]
    rep = n // c.shape[1]
    return (t * jnp.tile(c, (1, rep)) + pltpu.roll(t * jnp.tile(s1, (1, rep)), ROPE_DIM // 2, 1)
            + pltpu.roll(t * jnp.tile(s2, (1, rep)), n - ROPE_DIM // 2, 1))


def _stack_heads(t, j):
    return jnp.concatenate([t[:, ATT_HD * (j * ATT_R + r):ATT_HD * (j * ATT_R + r + 1)] for r in range(ATT_R)], axis=0)


def _swa_mask_t(first):
    si = _iota2((2 * WINDOW, ATT_R * WINDOW), 0)
    qi = _iota2((2 * WINDOW, ATT_R * WINDOW), 1) % WINDOW
    band = (si > qi) & (si <= qi + WINDOW)
    return band & (jnp.logical_not(first) | (si >= WINDOW))


def _head_rows(ref, j):
    if ref.shape[0] == 1:
        parts = [jnp.broadcast_to(ref[:, j * ATT_R + r:j * ATT_R + r + 1], (1, WINDOW)) for r in range(ATT_R)]
    else:
        parts = [ref[j * ATT_R + r:j * ATT_R + r + 1, :] for r in range(ATT_R)]
    return jnp.concatenate(parts, axis=1)


def _swa_fwd(q, g, kv, sinks):
    L = q.shape[0]
    nb = L // WINDOW
    scale = ATT_HD ** -0.5

    def body(q_ref, g_ref, kvc_ref, kvp_ref, sink_ref, y_ref, o_ref, lse_ref, otbuf):
        n = pl.program_id(0)
        kk = jnp.concatenate([kvp_ref[:, 0:D_KV], kvc_ref[:, 0:D_KV]], axis=0)
        vv = jnp.concatenate([kvp_ref[:, D_KV:2 * D_KV], kvc_ref[:, D_KV:2 * D_KV]], axis=0)
        valid = _swa_mask_t(n == 0)
        qv = q_ref[...]
        for j in range(ATT_KVH):
            js = slice(ATT_HD * j, ATT_HD * (j + 1))
            st = _dot_nt(kk[:, js], _stack_heads(qv, j)) * scale
            st = jnp.where(valid, st, NEG_BIG)
            sink = _head_rows(sink_ref, j)
            m = jnp.maximum(jnp.max(st, axis=0, keepdims=True), sink)
            p = jnp.exp(st - m)
            denom = jnp.sum(p, axis=0, keepdims=True) + jnp.exp(sink - m)
            ot = _dot_tn(vv[:, js], _bf(p)) * (1.0 / denom)
            lse = m + jnp.log(denom)
            for r in range(ATT_R):
                h = j * ATT_R + r
                otbuf[ATT_HD * h:ATT_HD * (h + 1), :] = ot[:, WINDOW * r:WINDOW * (r + 1)]
                lse_ref[h:h + 1, :] = lse[:, WINDOW * r:WINDOW * (r + 1)]
        o = otbuf[...].T
        o_ref[...] = o
        gv = g_ref[...]
        y_ref[...] = _bf(o * (gv * _sigmoid(gv)))

    cur = lambda wd: pl.BlockSpec((WINDOW, wd), lambda n: (n, 0))
    prv = lambda wd: pl.BlockSpec((WINDOW, wd), lambda n: (jnp.maximum(n - 1, 0), 0))
    return pl.pallas_call(
        body, name="swa_fwd", grid=(nb,),
        in_specs=[cur(D_ATT), cur(D_ATT), cur(2 * D_KV), prv(2 * D_KV), pl.BlockSpec((1, ATT_QH), lambda n: (0, 0))],
        out_specs=[cur(D_ATT), cur(D_ATT), pl.BlockSpec((ATT_QH, WINDOW), lambda n: (0, n))],
        out_shape=[jax.ShapeDtypeStruct((L, D_ATT), BF16), jax.ShapeDtypeStruct((L, D_ATT), F32),
                   jax.ShapeDtypeStruct((ATT_QH, L), F32)],
        scratch_shapes=[pltpu.VMEM((D_ATT, WINDOW), F32)],
        compiler_params=_cparams(("parallel",)),
    )(q, g, kv, kv, sinks)


def _swa_bwd(dy, q, g, kv, o, lse, pos, inv, sinks):
    L = q.shape[0]
    nb = L // WINDOW
    scale = ATT_HD ** -0.5

    def body(dy_ref, q_ref, g_ref, kvc_ref, kvp_ref, o_ref, lse_ref, posc_ref, posp_ref, inv_ref, sink_ref,
             dq_ref, dg_ref, dkv_ref, dsink_ref, carry, dqbuf, dkbuf, dvbuf):
        n = pl.program_id(0)

        @pl.when(n == 0)
        def _():
            dsink_ref[...] = jnp.zeros_like(dsink_ref)

        @pl.when(n < nb)
        def _():
            tc = _rope_tables(posc_ref, inv_ref)
            tp = _rope_tables(posp_ref, inv_ref)
            kk = jnp.concatenate([kvp_ref[:, 0:D_KV], kvc_ref[:, 0:D_KV]], axis=0)
            vv = jnp.concatenate([kvp_ref[:, D_KV:2 * D_KV], kvc_ref[:, D_KV:2 * D_KV]], axis=0)
            valid = _swa_mask_t(n == 0)
            qv = q_ref[...]
            gv = g_ref[...]
            sg = _sigmoid(gv)
            dyv = dy_ref[...]
            ov = o_ref[...]
            dg_ref[...] = _bf(dyv * ov * (sg * (1.0 + gv * (1.0 - sg))))
            do = dyv * (gv * sg)
            dod = do * ov
            ones = jnp.ones((8, ATT_HD), BF16)
            lane16 = _iota2((1, ATT_QH), 1)
            dsink = jnp.zeros((1, ATT_QH), F32)
            for j in range(ATT_KVH):
                js = slice(ATT_HD * j, ATT_HD * (j + 1))
                kj = kk[:, js]
                vj = vv[:, js]
                qs = _stack_heads(qv, j)
                dos = _bf(_stack_heads(do, j))
                hi, lo = _hi_lo(_stack_heads(dod, j))
                delta = (_dot_nt(ones, hi) + _dot_nt(ones, lo))[0:1]
                lse = _head_rows(lse_ref, j)
                st = _dot_nt(kj, qs) * scale
                pt = jnp.exp(jnp.where(valid, st, NEG_BIG) - lse)
                dst = _bf(pt * (_dot_nt(vj, dos) - delta))
                dqt = _dot_tn(kj, dst) * scale
                dkbuf[:, js] = _dot(dst, qs) * scale
                dvbuf[:, js] = _dot(_bf(pt), dos)
                sd = jnp.exp(_head_rows(sink_ref, j) - lse) * delta
                for r in range(ATT_R):
                    h = j * ATT_R + r
                    ls = slice(WINDOW * r, WINDOW * (r + 1))
                    dqbuf[ATT_HD * h:ATT_HD * (h + 1), :] = dqt[:, ls]
                    dsink = dsink - jnp.sum(sd[:, ls], axis=1, keepdims=True) * (lane16 == h).astype(F32)
            dsink_ref[...] += dsink
            dq_ref[...] = _bf(_rope_t(dqbuf[...].T, tc))
            dkp = _rope_t(dkbuf[0:WINDOW, :], tp)
            dkc = _rope_t(dkbuf[WINDOW:2 * WINDOW, :], tc)

            @pl.when(n > 0)
            def _():
                dkv_ref[:, 0:D_KV] = _bf(carry[:, 0:D_KV] + dkp)
                dkv_ref[:, D_KV:2 * D_KV] = _bf(carry[:, D_KV:2 * D_KV] + dvbuf[0:WINDOW, :])

            carry[:, 0:D_KV] = dkc
            carry[:, D_KV:2 * D_KV] = dvbuf[WINDOW:2 * WINDOW, :]

        @pl.when(n == nb)
        def _():
            dkv_ref[...] = _bf(carry[...])

    last = nb - 1
    cur = lambda wd: pl.BlockSpec((WINDOW, wd), lambda n: (jnp.minimum(n, last), 0))
    prv = lambda wd: pl.BlockSpec((WINDOW, wd), lambda n: (jnp.maximum(jnp.minimum(n, last) - 1, 0), 0))
    return pl.pallas_call(
        body, name="swa_bwd", grid=(nb + 1,),
        in_specs=[cur(D_ATT), cur(D_ATT), cur(D_ATT), cur(2 * D_KV), prv(2 * D_KV), cur(D_ATT),
                  pl.BlockSpec((ATT_QH, WINDOW), lambda n: (0, jnp.minimum(n, last))), cur(1), prv(1),
                  pl.BlockSpec((1, 2 * ATT_HD), lambda n: (0, 0)), pl.BlockSpec((1, ATT_QH), lambda n: (0, 0))],
        out_specs=[cur(D_ATT), cur(D_ATT),
                   pl.BlockSpec((WINDOW, 2 * D_KV), lambda n: (jnp.maximum(n - 1, 0), 0)),
                   pl.BlockSpec((1, ATT_QH), lambda n: (0, 0))],
        out_shape=[jax.ShapeDtypeStruct((L, D_ATT), BF16), jax.ShapeDtypeStruct((L, D_ATT), BF16),
                   jax.ShapeDtypeStruct((L, 2 * D_KV), BF16), jax.ShapeDtypeStruct((1, ATT_QH), F32)],
        scratch_shapes=[pltpu.VMEM((WINDOW, 2 * D_KV), F32), pltpu.VMEM((D_ATT, WINDOW), F32),
                        pltpu.VMEM((2 * WINDOW, D_KV), F32), pltpu.VMEM((2 * WINDOW, D_KV), F32)],
        compiler_params=_cparams(("arbitrary",)),
    )(dy, q, g, kv, kv, o, lse, pos, pos, inv, sinks)


def _out_ln_loss(y_ssd, y_att, x, target, w_out, ln_g, ln_b):
    L = x.shape[0]
    tm = min(ROW_TILE, L)
    nt = L // tm
    inv_d = 1.0 / D_MODEL

    def body(ys_ref, ya_ref, x_ref, t_ref, w_ref, g_ref, b_ref, dr_ref, dys_ref, dya_ref, loss_ref, gg_ref, gb_ref,
             gwo_ref, acc_ref):
        i = pl.program_id(0)

        @pl.when(i == 0)
        def _():
            loss_ref[...] = jnp.zeros_like(loss_ref)
            gg_ref[...] = jnp.zeros_like(gg_ref)
            gb_ref[...] = jnp.zeros_like(gb_ref)
            acc_ref[...] = jnp.zeros_like(acc_ref)

        h = _dot(_bf(ys_ref[...]), w_ref[0:D_SSD, :]) + _dot(_bf(ya_ref[...]), w_ref[D_SSD:D_MIX, :])
        r = ALPHA * x_ref[...] + h
        mu = jnp.mean(r, axis=-1, keepdims=True)
        xc = r - mu
        rstd = lax.rsqrt(jnp.mean(xc * xc, axis=-1, keepdims=True) + LN_EPS)
        xhat = xc * rstd
        gam = g_ref[...]
        diff = xhat * gam + b_ref[...] - t_ref[...]
        part = jnp.sum(jnp.sum(diff * diff, axis=-1, keepdims=True), axis=0, keepdims=True)
        loss_ref[...] += (0.5 * inv_d) * part
        dout = diff * inv_d
        gg_ref[...] += jnp.sum(dout * xhat, axis=0, keepdims=True)
        gb_ref[...] += jnp.sum(dout, axis=0, keepdims=True)
        dxh = dout * gam
        dr = rstd * (dxh - jnp.mean(dxh, axis=-1, keepdims=True) - xhat * jnp.mean(dxh * xhat, axis=-1, keepdims=True))
        dr_ref[...] = dr
        drb = _bf(dr)
        dys_ref[...] = _dot_nt(drb, w_ref[0:D_SSD, :])
        dya_ref[...] = _dot_nt(drb, w_ref[D_SSD:D_MIX, :])
        acc_ref[0:D_SSD, :] += _dot_tn(_bf(ys_ref[...]), drb)
        acc_ref[D_SSD:D_MIX, :] += _dot_tn(_bf(ya_ref[...]), drb)

        @pl.when(i == nt - 1)
        def _():
            gwo_ref[...] = _bf(acc_ref[...])

    row = pl.BlockSpec((tm, D_MODEL), lambda i: (i, 0))
    vec = pl.BlockSpec((1, D_MODEL), lambda i: (0, 0))
    return pl.pallas_call(
        body, name="out_ln_loss", grid=(nt,),
        in_specs=[row, row, row, row, pl.BlockSpec((D_MIX, D_MODEL), lambda i: (0, 0), pipeline_mode=pl.Buffered(1)), vec, vec],
        out_specs=[row, row, row, pl.BlockSpec((1, 128), lambda i: (0, 0)), vec, vec,
                   pl.BlockSpec((D_MIX, D_MODEL), lambda i: (0, 0))],
        out_shape=[jax.ShapeDtypeStruct((L, D_MODEL), F32)] * 3 + [jax.ShapeDtypeStruct((1, 128), F32)]
        + [jax.ShapeDtypeStruct((1, D_MODEL), F32)] * 2 + [jax.ShapeDtypeStruct((D_MIX, D_MODEL), BF16)],
        scratch_shapes=[pltpu.VMEM((D_MIX, D_MODEL), F32)],
        compiler_params=_cparams(("arbitrary",)),
    )(y_ssd, y_att, x, target, w_out, ln_g, ln_b)


def _local_step(x, pos, target, w, get_w_out, token, conv_w, conv_b, dt_bias, a_log, d_skip, norm_w, sinks, ln_g, ln_b):
    inv8 = ROPE_THETA ** (-jnp.arange(0, ROPE_DIM, 2, dtype=F32) / ROPE_DIM)
    inv = jnp.tile(jnp.concatenate([inv8, inv8, jnp.zeros((ATT_HD - ROPE_DIM,), F32)]), 2).reshape(1, 2 * ATT_HD)
    inv = inv + token

    z, g, q, xbc, kv, dtp, xb = _in_proj(x, w, pos, inv)
    y_ssd, y_pre, prev = _ssd_fwd(z, xbc, dtp, conv_w, conv_b, dt_bias, a_log, d_skip, norm_w)
    y_att, o, lse = _swa_fwd(q, g, kv, sinks)
    w_out = get_w_out(lse)
    dr, dy_ssd, dy_att, loss, g_ln_g, g_ln_b, gw_out = _out_ln_loss(y_ssd, y_att, x, target, w_out, ln_g, ln_b)
    w_out_red = _reduce_w_out_start(gw_out.reshape(N_CHIPS, W_OUT_ROWS, D_MODEL), loss)
    inv = inv + w_out_red[16][0:1, :]
    dq, dg, dkv, g_sinks = _swa_bwd(dy_att, q, g, kv, o, lse, pos, inv, sinks)
    dz, dxbc, ddt, g_conv_w, g_conv_b, g_dt_bias, g_a_log, g_d_skip, g_norm_w = _ssd_bwd(
        dy_ssd, z, y_pre, xbc, dtp, prev, conv_w, conv_b, dt_bias, a_log, d_skip, norm_w)
    gw_z, gw_g, gw_q = _matmuls_tn([dz, dg, dq], xb, "gw_zgq")
    gw_xbc, gw_kv, gw_dt = _matmuls_tn([dxbc, dkv, ddt], xb, "gw_xbc_kv_dt")
    gw_in = jnp.concatenate([gw_z, gw_xbc, gw_dt[0:SSD_HEADS], gw_q, gw_kv, gw_g], axis=0)
    small = dict(conv_w=g_conv_w, conv_b=g_conv_b, dt_bias=g_dt_bias, a_log=g_a_log, d_skip=g_d_skip,
                 ssd_norm_w=g_norm_w, attn_sinks=g_sinks, ln_g=g_ln_g, ln_b=g_ln_b)
    return loss, (dr, dz, dg, dq, dxbc, dkv, ddt, w), gw_in, w_out_red, small


def _mesh_pos():
    return lax.axis_index("x"), lax.axis_index("y"), lax.axis_index("c")


def _gather_weights(w_in_s, conv_w_s):
    hr = w_in_s.shape[0] // 2
    qa = 336
    quarters = ((0, qa), (qa, hr - qa))

    def body(win_ref, cw_ref, owin_ref, ocw_ref, stage, send_sems, recv_sems, small_send, small_recv, local_sems):
        x, y, c = _mesh_pos()
        me = 2 * x + y
        sibling = (x, y, 1 - c)
        xn, yn, dg = (1 - x, y), (x, 1 - y), (1 - x, 1 - y)
        chips = [xn, yn, dg]
        load = pltpu.make_async_copy(win_ref, stage, local_sems.at[1])
        load.start()
        locals_ = [pltpu.make_async_copy(cw_ref, ocw_ref.at[me], local_sems.at[0])]
        for cp in locals_:
            cp.start()
        started = []

        def piece(ref, chip, half, q):
            off, n = quarters[q]
            return ref.at[2 * chip[0] + chip[1]].at[pl.ds(half * hr + off, n), :]

        def mine(q):
            off, n = quarters[q]
            return win_ref.at[pl.ds(c * hr + off, n), :]

        def copy(src, dst, k, to):
            return pltpu.make_async_remote_copy(src_ref=src, dst_ref=dst, send_sem=send_sems.at[k], recv_sem=recv_sems.at[k],
                                                device_id=to, device_id_type=MESH)

        def go(cp):
            cp.start()
            started.append(cp)

        go(copy(mine(0), piece(owin_ref, (x, y), c, 0), 0, (*xn, c)))
        go(copy(mine(1), piece(owin_ref, (x, y), c, 1), 2, (*yn, c)))
        go(copy(mine(1), piece(owin_ref, (x, y), c, 1), 1, (*xn, c)))
        go(copy(mine(0), piece(owin_ref, (x, y), c, 0), 3, (*yn, c)))
        for j, (px, py) in enumerate(chips):
            cp = pltpu.make_async_remote_copy(
                src_ref=cw_ref, dst_ref=ocw_ref.at[me], send_sem=small_send.at[j], recv_sem=small_recv.at[j],
                device_id=(px, py, c), device_id_type=MESH)
            go(cp)
        load.wait()
        store = pltpu.make_async_copy(stage, owin_ref.at[me], local_sems.at[2])
        store.start()
        locals_.append(store)
        arrivals = [(0, xn, 0, (4, (*yn, c))), (2, yn, 1, (5, (*xn, c))), (1, xn, 1, None), (3, yn, 0, None),
                    (4, dg, 0, None), (5, dg, 1, None)]
        for n, (k, chip, q, onward) in enumerate(arrivals):
            blk = piece(owin_ref, chip, c, q)
            copy(blk, blk, k, sibling).wait_recv()
            if onward is not None:
                go(copy(blk, blk, onward[0], onward[1]))
            go(copy(blk, blk, 6 + n, sibling))
        for n, (k, chip, q, onward) in enumerate(arrivals):
            blk = piece(owin_ref, chip, 1 - c, q)
            copy(blk, blk, 6 + n, sibling).wait_recv()
        for j in range(3):
            pltpu.make_async_remote_copy(
                src_ref=cw_ref, dst_ref=ocw_ref.at[me], send_sem=small_send.at[j], recv_sem=small_recv.at[j],
                device_id=sibling, device_id_type=MESH).wait_recv()
        for cp in started:
            cp.wait_send()
        for cp in locals_:
            cp.wait()

    any_spec = pl.BlockSpec(memory_space=pl.ANY)
    return pl.pallas_call(
        body, name="gather_weights",
        in_specs=[any_spec] * 2, out_specs=[any_spec] * 2,
        out_shape=[jax.ShapeDtypeStruct((N_CHIPS,) + a.shape, a.dtype) for a in (w_in_s, conv_w_s)],
        scratch_shapes=[pltpu.VMEM(w_in_s.shape, w_in_s.dtype),
                        pltpu.SemaphoreType.DMA((12,)), pltpu.SemaphoreType.DMA((12,)),
                        pltpu.SemaphoreType.DMA((3,)), pltpu.SemaphoreType.DMA((3,)), pltpu.SemaphoreType.DMA((3,))],
    )(w_in_s, conv_w_s)


_HBM = pl.BlockSpec(memory_space=pltpu.HBM)
_SEM = pl.BlockSpec(memory_space=pltpu.SEMAPHORE)
_EFFECT = pltpu.SideEffectType.DATAFLOW_SIDE_EFFECTING


def _gather_w_out_start(w_out_s, after):
    def body(src_ref, land_ref, after_ref, s0, s1, s2, r0, r1, r2, src_thru, land_thru, token):
        x, y, c = _mesh_pos()
        me = 2 * x + y
        chips = [(1 - x, y), (x, 1 - y), (1 - x, 1 - y)]
        for (px, py), s, r in zip(chips, (s0, s1, s2), (r0, r1, r2)):
            pltpu.make_async_remote_copy(src_ref=src_ref, dst_ref=land_ref.at[me], send_sem=s, recv_sem=r,
                                         device_id=(px, py, c), device_id_type=MESH).start()
        token[...] = jnp.zeros_like(token)

    sem = pltpu.SemaphoreType.DMA(())
    land = lax.empty((N_CHIPS,) + w_out_s.shape, w_out_s.dtype)
    return pl.pallas_call(
        body, name="gather_w_out_start",
        out_shape=(sem,) * 6 + (pltpu.HBM(w_out_s.shape, w_out_s.dtype), pltpu.HBM(land.shape, land.dtype),
                                jax.ShapeDtypeStruct((8, 128), F32)),
        in_specs=(_HBM, _HBM, pl.BlockSpec(memory_space=pl.ANY)),
        out_specs=(_SEM,) * 6 + (_HBM, _HBM, pl.BlockSpec(memory_space=pltpu.VMEM)),
        input_output_aliases={0: 6, 1: 7},
        compiler_params=pltpu.CompilerParams(has_side_effects=_EFFECT),
    )(pltpu.with_memory_space_constraint(w_out_s, pltpu.HBM), pltpu.with_memory_space_constraint(land, pltpu.HBM), after)


def _gather_w_out_wait(sems, src_thru, land_thru, after):
    def body(src_ref, land_ref, s0, s1, s2, r0, r1, r2, after_ref, src_dead, got_ref):
        x, y, c = _mesh_pos()
        chips = [(1 - x, y), (x, 1 - y), (1 - x, 1 - y)]
        for (px, py), s, r in zip(chips, (s0, s1, s2), (r0, r1, r2)):
            cp = pltpu.make_async_remote_copy(src_ref=src_ref, dst_ref=land_ref.at[2 * px + py], send_sem=s, recv_sem=r,
                                              device_id=(px, py, c), device_id_type=MESH)
            cp.wait_send()
            cp.wait_recv()

    return pl.pallas_call(
        body, name="gather_w_out_wait",
        out_shape=(pltpu.HBM(src_thru.shape, src_thru.dtype), pltpu.HBM(land_thru.shape, land_thru.dtype)),
        in_specs=(_HBM, _HBM) + (_SEM,) * 6 + (pl.BlockSpec(memory_space=pl.ANY),),
        out_specs=(_HBM, _HBM), input_output_aliases={0: 0, 1: 1},
        compiler_params=pltpu.CompilerParams(has_side_effects=_EFFECT),
    )(src_thru, land_thru, *sems, after)[1]


def _pair_start(gw_in, after):
    hr = gw_in.shape[1] // 2

    def body(src_ref, land_ref, after_ref, *refs):
        x, y, c = _mesh_pos()
        for j in range(N_CHIPS):
            pltpu.make_async_remote_copy(
                src_ref=src_ref.at[j, pl.ds((1 - c) * hr, hr), :], dst_ref=land_ref.at[j], send_sem=refs[j],
                recv_sem=refs[N_CHIPS + j], device_id=(x, y, 1 - c), device_id_type=MESH).start()
        refs[10][...] = jnp.zeros_like(refs[10])

    sem = pltpu.SemaphoreType.DMA(())
    land = lax.empty((N_CHIPS, hr, D_MODEL), F32)
    return pl.pallas_call(
        body, name="pair_start",
        out_shape=(sem,) * 8 + (pltpu.HBM(gw_in.shape, F32), pltpu.HBM(land.shape, F32), jax.ShapeDtypeStruct((8, 128), F32)),
        in_specs=(_HBM, _HBM, pl.BlockSpec(memory_space=pl.ANY)),
        out_specs=(_SEM,) * 8 + (_HBM, _HBM, pl.BlockSpec(memory_space=pltpu.VMEM)),
        input_output_aliases={0: 8, 1: 9},
        compiler_params=pltpu.CompilerParams(has_side_effects=_EFFECT),
    )(pltpu.with_memory_space_constraint(gw_in, pltpu.HBM), pltpu.with_memory_space_constraint(land, pltpu.HBM), after)


def _pair_wait(sems, gw_thru, land_thru, after):
    hr = land_thru.shape[1]

    def body(src_ref, land_ref, *refs):
        x, y, c = _mesh_pos()
        for j in range(N_CHIPS):
            cp = pltpu.make_async_remote_copy(
                src_ref=src_ref.at[j, pl.ds((1 - c) * hr, hr), :], dst_ref=land_ref.at[j], send_sem=refs[j],
                recv_sem=refs[N_CHIPS + j], device_id=(x, y, 1 - c), device_id_type=MESH)
            cp.wait_send()
            cp.wait_recv()

    return pl.pallas_call(
        body, name="pair_wait",
        out_shape=(pltpu.HBM(gw_thru.shape, F32), pltpu.HBM(land_thru.shape, F32)),
        in_specs=(_HBM, _HBM) + (_SEM,) * 8 + (pl.BlockSpec(memory_space=pl.ANY),),
        out_specs=(_HBM, _HBM), input_output_aliases={0: 0, 1: 1},
        compiler_params=pltpu.CompilerParams(has_side_effects=_EFFECT),
    )(gw_thru, land_thru, *sems, after)


def _chip_start(s_in, after):
    def body(src_ref, land_ref, after_ref, *refs):
        x, y, c = _mesh_pos()
        me = 2 * x + y
        for j, (px, py) in enumerate([(1 - x, y), (x, 1 - y), (1 - x, 1 - y)]):
            pltpu.make_async_remote_copy(
                src_ref=src_ref.at[2 * px + py], dst_ref=land_ref.at[me], send_sem=refs[j], recv_sem=refs[3 + j],
                device_id=(px, py, c), device_id_type=MESH).start()
        refs[8][...] = jnp.zeros_like(refs[8])

    sem = pltpu.SemaphoreType.DMA(())
    land = lax.empty(s_in.shape, s_in.dtype)
    return pl.pallas_call(
        body, name="chip_start",
        out_shape=(sem,) * 6 + (pltpu.HBM(s_in.shape, s_in.dtype), pltpu.HBM(land.shape, land.dtype),
                                jax.ShapeDtypeStruct((8, 128), F32)),
        in_specs=(_HBM, _HBM, pl.BlockSpec(memory_space=pl.ANY)),
        out_specs=(_SEM,) * 6 + (_HBM, _HBM, pl.BlockSpec(memory_space=pltpu.VMEM)),
        input_output_aliases={0: 6, 1: 7},
        compiler_params=pltpu.CompilerParams(has_side_effects=_EFFECT),
    )(pltpu.with_memory_space_constraint(s_in, pltpu.HBM), pltpu.with_memory_space_constraint(land, pltpu.HBM), after)


def _chip_wait(sems, s_thru, land_thru, after):
    def body(src_ref, land_ref, *refs):
        x, y, c = _mesh_pos()
        for j, (px, py) in enumerate([(1 - x, y), (x, 1 - y), (1 - x, 1 - y)]):
            cp = pltpu.make_async_remote_copy(
                src_ref=src_ref.at[2 * px + py], dst_ref=land_ref.at[2 * px + py], send_sem=refs[j], recv_sem=refs[3 + j],
                device_id=(px, py, c), device_id_type=MESH)
            cp.wait_send()
            cp.wait_recv()

    return pl.pallas_call(
        body, name="chip_wait",
        out_shape=(pltpu.HBM(s_thru.shape, s_thru.dtype), pltpu.HBM(land_thru.shape, land_thru.dtype)),
        in_specs=(_HBM, _HBM) + (_SEM,) * 6 + (pl.BlockSpec(memory_space=pl.ANY),),
        out_specs=(_HBM, _HBM), input_output_aliases={0: 0, 1: 1},
        compiler_params=pltpu.CompilerParams(has_side_effects=_EFFECT),
    )(s_thru, land_thru, *sems, after)


def _pair_share(h_in, small):
    def body(hin_ref, sm_ref, rin_ref, slots_ref, send_sems, recv_sems, small_send, small_recv, local_sem):
        x, y, c = _mesh_pos()
        dev = 4 * x + 2 * y + c
        mine = pltpu.make_async_copy(sm_ref, slots_ref.at[dev], local_sem)
        mine.start()
        share = pltpu.make_async_remote_copy(
            src_ref=hin_ref, dst_ref=rin_ref, send_sem=send_sems.at[0], recv_sem=recv_sems.at[0],
            device_id=(x, y, 1 - c), device_id_type=MESH)
        share.start()
        started = []
        for k in range(1, 8):
            peer = (x ^ ((k >> 2) & 1), y ^ ((k >> 1) & 1), c ^ (k & 1))
            cp = pltpu.make_async_remote_copy(
                src_ref=sm_ref, dst_ref=slots_ref.at[dev], send_sem=small_send.at[k - 1], recv_sem=small_recv.at[k - 1],
                device_id=peer, device_id_type=MESH)
            cp.start()
            started.append(cp)
        share.wait()
        for k in range(1, 8):
            pltpu.make_async_remote_copy(
                src_ref=sm_ref, dst_ref=slots_ref.at[dev], send_sem=small_send.at[k - 1], recv_sem=small_recv.at[k - 1],
                device_id=(x, y, 1 - c), device_id_type=MESH).wait_recv()
        for cp in started:
            cp.wait_send()
        mine.wait()

    any_spec = pl.BlockSpec(memory_space=pl.ANY)
    return pl.pallas_call(
        body, name="pair_share",
        in_specs=[any_spec] * 2, out_specs=[any_spec] * 2,
        out_shape=[jax.ShapeDtypeStruct(h_in.shape, F32), jax.ShapeDtypeStruct((8,) + small.shape, F32)],
        scratch_shapes=[pltpu.SemaphoreType.DMA((1,)), pltpu.SemaphoreType.DMA((1,)),
                        pltpu.SemaphoreType.DMA((7,)), pltpu.SemaphoreType.DMA((7,)), pltpu.SemaphoreType.DMA],
    )(h_in, small)


def _reduce_w_out_start(slabs, after):
    def body(src_ref, land_ref, after_ref, *refs):
        x, y, c = _mesh_pos()
        me = 4 * x + 2 * y + c
        for k in range(1, 8):
            px, py, pc = x ^ ((k >> 2) & 1), y ^ ((k >> 1) & 1), c ^ (k & 1)
            pltpu.make_async_remote_copy(src_ref=src_ref.at[2 * px + py], dst_ref=land_ref.at[me], send_sem=refs[k - 1],
                                         recv_sem=refs[6 + k], device_id=(px, py, pc), device_id_type=MESH).start()
        refs[16][...] = jnp.zeros_like(refs[16])

    sem = pltpu.SemaphoreType.DMA(())
    land = lax.empty((8,) + slabs.shape[1:], slabs.dtype)
    return pl.pallas_call(
        body, name="reduce_w_out_start",
        out_shape=(sem,) * 14 + (pltpu.HBM(slabs.shape, slabs.dtype), pltpu.HBM(land.shape, land.dtype),
                                 jax.ShapeDtypeStruct((8, 128), F32)),
        in_specs=(_HBM, _HBM, pl.BlockSpec(memory_space=pl.ANY)),
        out_specs=(_SEM,) * 14 + (_HBM, _HBM, pl.BlockSpec(memory_space=pltpu.VMEM)),
        input_output_aliases={0: 14, 1: 15},
        compiler_params=pltpu.CompilerParams(has_side_effects=_EFFECT),
    )(pltpu.with_memory_space_constraint(slabs, pltpu.HBM), pltpu.with_memory_space_constraint(land, pltpu.HBM), after)


def _reduce_w_out_wait(sems, slabs_thru, land_thru, after):
    def body(src_ref, land_ref, *refs):
        x, y, c = _mesh_pos()
        for k in range(1, 8):
            px, py, pc = x ^ ((k >> 2) & 1), y ^ ((k >> 1) & 1), c ^ (k & 1)
            cp = pltpu.make_async_remote_copy(
                src_ref=src_ref.at[2 * px + py], dst_ref=land_ref.at[4 * px + 2 * py + pc], send_sem=refs[k - 1],
                recv_sem=refs[6 + k], device_id=(px, py, pc), device_id_type=MESH)
            cp.wait_send()
            cp.wait_recv()

    return pl.pallas_call(
        body, name="reduce_w_out_wait",
        out_shape=(pltpu.HBM(slabs_thru.shape, slabs_thru.dtype), pltpu.HBM(land_thru.shape, land_thru.dtype)),
        in_specs=(_HBM, _HBM) + (_SEM,) * 14 + (pl.BlockSpec(memory_space=pl.ANY),),
        out_specs=(_HBM, _HBM), input_output_aliases={0: 0, 1: 1},
        compiler_params=pltpu.CompilerParams(has_side_effects=_EFFECT),
    )(slabs_thru, land_thru, *sems, after)


def _pair_add(g, recv, core, name):
    _, rows, C = recv.shape
    tc = 256

    def body(core_ref, g_ref, r_ref, o_ref):
        o_ref[...] = _bf(g_ref[...] + r_ref[...])

    spec = pl.BlockSpec((1, rows, tc), lambda j, i, core: (j, 0, i))
    return pl.pallas_call(
        body, name=name,
        grid_spec=pltpu.PrefetchScalarGridSpec(
            num_scalar_prefetch=1, grid=(N_CHIPS, C // tc),
            in_specs=[pl.BlockSpec((1, rows, tc), lambda j, i, core: (j, core[0], i)), spec], out_specs=spec),
        out_shape=jax.ShapeDtypeStruct((N_CHIPS, rows, C), BF16),
        compiler_params=_cparams(("parallel", "parallel")),
    )(core, g, recv)


def _chip_add(own, parts, chip, name):
    _, rows, C = parts.shape
    tc = 256

    def body(chip_ref, own_ref, r0, r1, r2, r3, o_ref):
        acc = None
        for j, r in enumerate((r0, r1, r2, r3)):
            term = jnp.where(chip_ref[0] == j, own_ref[0], r[0]).astype(F32)
            acc = term if acc is None else acc + term
        o_ref[...] = acc

    def slab(j):
        return pl.BlockSpec((1, rows, tc), lambda i, chip: (jnp.where(chip[0] == j, (j + 1) % N_CHIPS, j), 0, i))

    return pl.pallas_call(
        body, name=name,
        grid_spec=pltpu.PrefetchScalarGridSpec(
            num_scalar_prefetch=1, grid=(C // tc,),
            in_specs=[pl.BlockSpec((1, rows, tc), lambda i, chip: (chip[0], 0, i))] + [slab(j) for j in range(N_CHIPS)],
            out_specs=pl.BlockSpec((rows, tc), lambda i, chip: (0, i))),
        out_shape=jax.ShapeDtypeStruct((rows, C), F32),
        compiler_params=_cparams(("parallel",)),
    )(chip, own, parts, parts, parts, parts)


def _adamw_math(w, g, m, v):
    m = ADAM_B1 * m + (1.0 - ADAM_B1) * g
    v = ADAM_B2 * v + (1.0 - ADAM_B2) * (g * g)
    m_hat = m / (1.0 - ADAM_B1 ** ADAM_STEP)
    v_hat = v / (1.0 - ADAM_B2 ** ADAM_STEP)
    delta = -ADAM_LR * (m_hat / (jnp.sqrt(v_hat) + ADAM_EPS) + ADAM_WD * w)
    return delta, m, v


def _adamw_rows(w, g_own, g_sib, m, v, core, name):
    R, C = w.shape[0], w.shape[-1]
    rows = g_own.shape[0]
    step = 256
    chunks = [(r, min(step, R - r)) for r in range(0, R, step)]
    sub = 64

    def body(core_ref, w_hbm, go_hbm, gs_hbm, m_hbm, v_hbm, d_hbm, nm_hbm, nv_hbm, g_hbm,
             wbuf, mbuf, vbuf, gbuf, dbuf, nmbuf, nvbuf, in_sems, g_sems, out_sems):
        c = core_ref[0]
        flat = lambda ref: ref.at[:, 0, :]
        g_in = [pltpu.make_async_copy(go_hbm, gbuf.at[pl.ds(pl.multiple_of(c * rows, 8), rows), :], g_sems.at[0]),
                pltpu.make_async_copy(gs_hbm, gbuf.at[pl.ds(pl.multiple_of((1 - c) * rows, 8), rows), :], g_sems.at[1])]
        for cp in g_in:
            cp.start()
        loads = []
        for k, (r0, n) in enumerate(chunks):
            cps = [pltpu.make_async_copy(flat(src).at[pl.ds(r0, n), :], dst.at[pl.ds(r0, n), :], in_sems.at[a, k])
                   for a, (src, dst) in enumerate(((w_hbm, wbuf), (m_hbm, mbuf), (v_hbm, vbuf)))]
            for cp in cps:
                cp.start()
            loads.append(cps)
        for cp in g_in:
            cp.wait()
        stores = []
        for k, (r0, n) in enumerate(chunks):
            for cp in loads[k]:
                cp.wait()

            def update(rs):
                g = gbuf[rs, :]
                dl, nm, nv = _adamw_math(wbuf[rs, :], g, mbuf[rs, :], vbuf[rs, :])
                dbuf[rs, :] = dl
                nmbuf[rs, :] = nm
                nvbuf[rs, :] = nv

            if n % sub == 0:
                def block(i, carry, r0=r0):
                    update(pl.ds(pl.multiple_of(r0 + i * sub, 8), sub))
                    return carry
                lax.fori_loop(0, n // sub, block, 0)
            else:
                update(pl.ds(r0, n))
            cps = [pltpu.make_async_copy(src.at[pl.ds(r0, n), :], flat(dst).at[pl.ds(r0, n), :], out_sems.at[a, k])
                   for a, (src, dst) in enumerate(((dbuf, d_hbm), (nmbuf, nm_hbm), (nvbuf, nv_hbm), (gbuf, g_hbm)))]
            for cp in cps:
                cp.start()
            stores += cps
        for cp in stores:
            cp.wait()

    any_spec = pl.BlockSpec(memory_space=pl.ANY)
    dense = pltpu.VMEM((R, C), F32)
    return pl.pallas_call(
        body, name=name,
        grid_spec=pltpu.PrefetchScalarGridSpec(
            num_scalar_prefetch=1, grid=(1,),
            in_specs=[any_spec] * 5, out_specs=[any_spec] * 4,
            scratch_shapes=[dense, dense, dense, pltpu.VMEM((2 * rows, C), F32), dense, dense, dense,
                            pltpu.SemaphoreType.DMA((3, len(chunks))), pltpu.SemaphoreType.DMA((2,)),
                            pltpu.SemaphoreType.DMA((4, len(chunks)))]),
        out_shape=[jax.ShapeDtypeStruct(w.shape, F32)] * 4,
        compiler_params=_cparams(),
    )(core, w, g_own, g_sib, m, v)


def _adamw_sum8(w, slabs, land, m, v, ids, name):
    R, C = w.shape
    tc = 128

    def body(ids_ref, w_ref, own_ref, *refs):
        lrefs, (m_ref, v_ref, d_ref, nm_ref, nv_ref, g_ref) = refs[:8], refs[8:]
        g = None
        for d, l_ref in enumerate(lrefs):
            term = jnp.where(ids_ref[0] == d, own_ref[0], l_ref[0]).astype(F32)
            g = term if g is None else g + term
        dl, nm, nv = _adamw_math(w_ref[...], g, m_ref[...], v_ref[...])
        d_ref[...] = dl
        nm_ref[...] = nm
        nv_ref[...] = nv
        g_ref[...] = g

    def slot(d):
        return pl.BlockSpec((1, R, tc), lambda i, ids: (jnp.where(ids[0] == d, (d + 1) % 8, d), 0, i))

    spec = pl.BlockSpec((R, tc), lambda i, ids: (0, i))
    return pl.pallas_call(
        body, name=name,
        grid_spec=pltpu.PrefetchScalarGridSpec(
            num_scalar_prefetch=1, grid=(C // tc,),
            in_specs=[spec, pl.BlockSpec((1, R, tc), lambda i, ids: (ids[1], 0, i))] + [slot(d) for d in range(8)]
            + [spec, spec],
            out_specs=[spec] * 4),
        out_shape=[jax.ShapeDtypeStruct((R, C), F32)] * 4,
        compiler_params=_cparams(("parallel",)),
    )(ids, w, slabs, *([land] * 8), m, v)


SMALL_NAMES = ("conv_b", "ssd_norm_w", "ln_g", "ln_b", "dt_bias", "a_log", "d_skip", "attn_sinks")
SMALL_FIELDS = ((4, 0, D_XBC), (5, 0, D_SSD), (6, 0, D_MODEL), (7, 0, D_MODEL), (5, 1024, SSD_HEADS), (5, 1152, SSD_HEADS),
                (5, 1280, SSD_HEADS), (5, 1408, ATT_QH))
LOSS_FIELD = (6, 1024, 128)
K_SMALL = D_XBC


def _pack_small(g_conv_w, vecs, loss):
    def body(cw_ref, *refs):
        o_ref = refs[-1]
        o_ref[...] = jnp.zeros_like(o_ref)
        o_ref[0:CONV_K, 0:D_XBC] = cw_ref[...]
        for v_ref, (row, off, n) in zip(refs[:-2], SMALL_FIELDS):
            o_ref[row:row + 1, off:off + n] = v_ref[...]
        o_ref[LOSS_FIELD[0]:LOSS_FIELD[0] + 1, LOSS_FIELD[1]:LOSS_FIELD[1] + LOSS_FIELD[2]] = refs[-2][...]

    return pl.pallas_call(
        body, name="pack_small", out_shape=jax.ShapeDtypeStruct((8, K_SMALL), F32), compiler_params=_cparams(),
    )(g_conv_w, *vecs, loss)


def _adamw_small(slots, chip, conv_w, m_conv_w, v_conv_w, params, moms, vars_):
    n_vec = len(SMALL_NAMES)

    def body(chip_ref, s_ref, *refs):
        ins = refs[:3 * (n_vec + 1)]
        outs = refs[3 * (n_vec + 1):-1]
        tot_ref = refs[-1]
        tot = s_ref[0]
        for d in range(1, 8):
            tot = tot + s_ref[d]
        outs[0][...] = tot[LOSS_FIELD[0]:LOSS_FIELD[0] + 1, LOSS_FIELD[1]:LOSS_FIELD[1] + 1]
        off = pl.multiple_of(chip_ref[0] * CONV_COLS, 128)
        tot_ref[...] = tot
        grads = [tot_ref[0:CONV_K, pl.ds(off, CONV_COLS)]]
        grads += [tot[row:row + 1, o:o + n] for row, o, n in SMALL_FIELDS]
        for k, g in enumerate(grads):
            w_ref, m_ref, v_ref = ins[3 * k:3 * k + 3]
            full = (0,) if k == 0 else (Ellipsis,)
            d, nm, nv = _adamw_math(w_ref[full], g, m_ref[full], v_ref[full])
            for o_ref, val in zip(outs[1 + 4 * k:5 + 4 * k], (g, d, nm, nv)):
                o_ref[full] = val

    args = [conv_w, m_conv_w, v_conv_w]
    for w, m, v in zip(params, moms, vars_):
        args += [w, m, v]
    shapes = [jax.ShapeDtypeStruct((1, 1), F32)] + [jax.ShapeDtypeStruct(conv_w.shape, F32)] * 4
    for w in params:
        shapes += [jax.ShapeDtypeStruct(w.shape, F32)] * 4
    vmem = pl.BlockSpec(memory_space=pltpu.VMEM)
    return pl.pallas_call(
        body, name="adamw_small",
        grid_spec=pltpu.PrefetchScalarGridSpec(
            num_scalar_prefetch=1, grid=(1,),
            in_specs=[pl.BlockSpec(slots.shape, lambda i, chip: (0, 0, 0))] + [vmem] * len(args),
            out_specs=[vmem] * len(shapes), scratch_shapes=[pltpu.VMEM((8, K_SMALL), F32)]),
        out_shape=shapes, compiler_params=_cparams(),
    )(chip, slots, *args)


def kernel(x, positions, w_in, conv_w, conv_b, dt_bias, a_log, d_skip, ssd_norm_w, attn_sinks, w_out, ln_g, ln_b, loss_target, m_w_in, m_conv_w, m_conv_b, m_dt_bias, m_a_log, m_d_skip, m_ssd_norm_w, m_attn_sinks, m_w_out, m_ln_g, m_ln_b, v_w_in, v_conv_w, v_conv_b, v_dt_bias, v_a_log, v_d_skip, v_ssd_norm_w, v_attn_sinks, v_w_out, v_ln_g, v_ln_b):
    mx, my, mc = _mesh_pos()
    chip = 2 * mx + my
    L = x.shape[1]

    conv_w_s8 = jnp.pad(conv_w[0], ((0, 8 - CONV_K), (0, 0)))
    pad_rows = ((0, SLAB_ROWS - W_IN_COLS), (0, 0))
    w_in_t = w_in[0].T
    w_in_b, w_out_b = jnp.pad(_bf(w_in_t), pad_rows), _bf(w_out[0])
    ag_in, ag_cw = _gather_weights(w_in_b, conv_w_s8)
    started = _gather_w_out_start(w_out_b, ag_cw)
    own = (jnp.arange(N_CHIPS) == chip)[:, None, None]

    def get_w_out(after):
        landed = _gather_w_out_wait(started[0:6], started[6], started[7], after)
        return jnp.where(own, w_out_b[None], landed).reshape(D_MIX, D_MODEL)

    w_full = jnp.concatenate([ag_in[j, 0:W_IN_COLS] for j in range(N_CHIPS)], axis=0)
    w = jnp.concatenate([
        w_full[O_Z:O_Z + D_SSD], w_full[O_G:O_G + D_ATT], w_full[O_Q:O_Q + D_ATT],
        w_full[O_XBC:O_XBC + D_XBC], w_full[O_K:O_K + 2 * D_KV], w_full[O_DT:O_DT + SSD_HEADS],
        jnp.zeros((DT_PAD - SSD_HEADS, D_MODEL), BF16)], axis=0)
    conv_w_full = jnp.concatenate([ag_cw[j, 0:CONV_K] for j in range(N_CHIPS)], axis=1)

    loss_part, gx_args, gw_in, w_out_red, small = _local_step(
        x[0], positions[0].reshape(L, 1), loss_target[0], w, get_w_out, started[8][0:1, :], conv_w_full,
        conv_b, dt_bias, a_log, d_skip, ssd_norm_w, attn_sinks, ln_g, ln_b)

    packed = _pack_small(small["conv_w"], [small[n] for n in SMALL_NAMES], loss_part)
    core_id = mc.reshape(1).astype(jnp.int32)
    chip_id = chip.reshape(1).astype(jnp.int32)
    ids = jnp.stack([4 * mx + 2 * my + mc, chip]).astype(jnp.int32)
    slabs = jnp.stack([jnp.pad(gw_in[W_IN_COLS * j:W_IN_COLS * (j + 1)], pad_rows) for j in range(N_CHIPS)])
    w_in_red = _pair_start(slabs, packed)
    grad_x = _grad_x(*gx_args, w_in_red[10], 0)
    gw_in_slabs, recv_in = _pair_wait(w_in_red[0:8], w_in_red[8], w_in_red[9], grad_x[0:8, 0:128])
    s_in = _pair_add(gw_in_slabs, recv_in, core_id, "pair_add_in")
    chip_red = _chip_start(s_in, packed)
    grad_x = _grad_x(*gx_args, chip_red[8], 1, grad_x)
    own_slabs, landed = _reduce_w_out_wait(w_out_red[0:14], w_out_red[14], w_out_red[15], grad_x)
    out_t = _adamw_sum8(w_out[0], own_slabs, landed, m_w_out[0], v_w_out[0], ids, "adamw_w_out")
    d_w_out, nm_w_out, nv_w_out, g_w_out = [a[None] for a in out_t]
    s_in, r_in = _chip_wait(chip_red[0:6], chip_red[6], chip_red[7], out_t[0])
    h_in = _chip_add(s_in, r_in, chip_id, "chip_add_in")
    sib_in, slots = _pair_share(h_in, packed)

    to_rows = lambda a: jnp.transpose(a, (2, 0, 1))
    in_t = _adamw_rows(to_rows(w_in), h_in, sib_in, to_rows(m_w_in), to_rows(v_w_in), core_id, "adamw_w_in")
    d_w_in, nm_w_in, nv_w_in, g_w_in = [jnp.transpose(a, (1, 2, 0)) for a in in_t]

    params = dict(conv_b=conv_b, ssd_norm_w=ssd_norm_w, ln_g=ln_g, ln_b=ln_b, dt_bias=dt_bias, a_log=a_log,
                  d_skip=d_skip, attn_sinks=attn_sinks)
    moms = dict(conv_b=m_conv_b, ssd_norm_w=m_ssd_norm_w, ln_g=m_ln_g, ln_b=m_ln_b, dt_bias=m_dt_bias, a_log=m_a_log,
                d_skip=m_d_skip, attn_sinks=m_attn_sinks)
    vars_ = dict(conv_b=v_conv_b, ssd_norm_w=v_ssd_norm_w, ln_g=v_ln_g, ln_b=v_ln_b, dt_bias=v_dt_bias, a_log=v_a_log,
                 d_skip=v_d_skip, attn_sinks=v_attn_sinks)
    res = _adamw_small(slots, chip_id, conv_w, m_conv_w, v_conv_w, [params[n] for n in SMALL_NAMES],
                       [moms[n] for n in SMALL_NAMES], [vars_[n] for n in SMALL_NAMES])
    loss = res[0][0, 0]
    grads, delta, new_m, new_v = {}, {}, {}, {}
    for k, n in enumerate(("conv_w",) + SMALL_NAMES):
        grads[n], delta[n], new_m[n], new_v[n] = res[1 + 4 * k:5 + 4 * k]
    for dd, a_in, a_out in ((grads, g_w_in, g_w_out), (delta, d_w_in, d_w_out), (new_m, nm_w_in, nm_w_out),
                            (new_v, nv_w_in, nv_w_out)):
        dd["w_in"] = a_in
        dd["w_out"] = a_out
    order = ("w_in", "conv_w", "conv_b", "dt_bias", "a_log", "d_skip", "ssd_norm_w", "attn_sinks", "w_out", "ln_g", "ln_b")
    return (loss, grad_x[None], *[grads[n] for n in order], *[delta[n] for n in order], *[new_m[n] for n in order],
            *[new_v[n] for n in order])
```

```python
import numpy as np
import jax
import jax.numpy as jnp
from jax import lax
from jax.experimental import pallas as pl
from jax.experimental.pallas import tpu as pltpu

F32 = jnp.float32
BF16 = jnp.bfloat16
MESH = pl.DeviceIdType.MESH

D_MODEL = 1024
D_SSD = 1024
D_ATT = 1024
D_MIX = 2048
SSD_HEADS = 16
SSD_P = 64
SSD_GROUPS = 2
SSD_R = 8
SSD_N = 128
D_BC = 256
D_XBC = 1536
CONV_K = 4
CHUNK = 128
ATT_HD = 64
ATT_QH = 16
ATT_KVH = 4
ATT_R = 4
D_KV = 256
WINDOW = 128
ROPE_THETA = 500000.0
ROPE_DIM = 16
ALPHA = 2.0 ** 0.25
LN_EPS = 1e-5
RMS_EPS = 1e-5
D_IN_PROJ = 5136
O_Z, O_XBC, O_DT, O_Q, O_K, O_V, O_G = 0, 1024, 2560, 2576, 3600, 3856, 4112
P_Z, P_G, P_Q, P_XBC, P_KV, P_DT, P_END = 0, 1024, 2048, 3072, 4608, 5120, 5248
DT_PAD = 128
N_CHIPS = 4
W_IN_COLS = D_IN_PROJ // N_CHIPS
SLAB_ROWS = 1312
W_OUT_ROWS = D_MIX // N_CHIPS
CONV_COLS = D_XBC // N_CHIPS

ADAM_LR = 0.001
ADAM_B1 = 0.9
ADAM_B2 = 0.999
ADAM_EPS = 1e-08
ADAM_WD = 0.01
ADAM_STEP = 10

VMEM_LIMIT = 56 * 1024 * 1024
ROW_TILE = 512
NEG_BIG = -1e30
HI = lax.Precision.HIGHEST


def _cparams(sem=None, **kw):
    if sem is not None:
        kw["dimension_semantics"] = sem
    return pltpu.CompilerParams(vmem_limit_bytes=VMEM_LIMIT, **kw)


def _dot(a, b):
    return jnp.dot(a, b, preferred_element_type=F32)


def _dot_nt(a, b):
    return lax.dot_general(a, b, (((1,), (1,)), ((), ())), preferred_element_type=F32)


def _dot_tn(a, b):
    return lax.dot_general(a, b, (((0,), (0,)), ((), ())), preferred_element_type=F32)


def _bf(a):
    return a.astype(BF16)


def _iota2(shape, dim):
    return lax.broadcasted_iota(jnp.int32, shape, dim)


def _to_rows(col):
    k = col.shape[1]
    eye = (_iota2((k, k), 0) == _iota2((k, k), 1)).astype(F32)
    return lax.dot_general(eye, col, (((1,), (1,)), ((), ())), preferred_element_type=F32, precision=HI)


def _to_cols(row):
    n = row.shape[1]
    eye = (_iota2((n, n), 0) == _iota2((n, n), 1)).astype(F32)
    return lax.dot_general(eye, row, (((1,), (1,)), ((), ())), preferred_element_type=F32, precision=HI)


def _sigmoid(x):
    return jax.nn.sigmoid(x)


def _in_proj(x, w, pos, inv):
    L = x.shape[0]
    tm = ROW_TILE
    widths = (D_SSD, D_ATT, D_ATT, D_XBC, 2 * D_KV, DT_PAD)

    def body(x_ref, w_ref, pos_ref, inv_ref, z_ref, g_ref, q_ref, xbc_ref, kv_ref, dt_ref, xb_ref):
        xb = _bf(x_ref[...])
        xb_ref[...] = xb
        for o_ref, off, wd in zip((z_ref, g_ref, xbc_ref, dt_ref), (P_Z, P_G, P_XBC, P_DT), (D_SSD, D_ATT, D_XBC, DT_PAD)):
            o_ref[...] = _dot_nt(xb, w_ref[off:off + wd, :]).astype(o_ref.dtype)
        tabs = _rope_tables(pos_ref, inv_ref)
        q_ref[...] = _bf(_rope(_dot_nt(xb, w_ref[P_Q:P_Q + D_ATT, :]), tabs))
        kv_ref[:, 0:D_KV] = _bf(_rope(_dot_nt(xb, w_ref[P_KV:P_KV + D_KV, :]), tabs))
        kv_ref[:, D_KV:2 * D_KV] = _bf(_dot_nt(xb, w_ref[P_KV + D_KV:P_KV + 2 * D_KV, :]))

    row = lambda wd: pl.BlockSpec((tm, wd), lambda i: (i, 0))
    return pl.pallas_call(
        body, name="in_proj", grid=(L // tm,),
        in_specs=[row(D_MODEL), pl.BlockSpec((P_END, D_MODEL), lambda i: (0, 0), pipeline_mode=pl.Buffered(1)), row(1),
                  pl.BlockSpec((1, 2 * ATT_HD), lambda i: (0, 0))],
        out_specs=[row(wd) for wd in widths] + [row(D_MODEL)],
        out_shape=[jax.ShapeDtypeStruct((L, wd), dt) for wd, dt in zip(widths, (BF16, BF16, BF16, F32, BF16, F32))]
        + [jax.ShapeDtypeStruct((L, D_MODEL), BF16)],
        compiler_params=_cparams(("parallel",)),
    )(x, w, pos, inv)


def _matmuls_tn(a_list, b, name):
    K, N = b.shape
    tk = min(K, 1024)
    n = len(a_list)

    def body(*refs):
        b_ref = refs[n]
        k = pl.program_id(0)
        bb = _bf(b_ref[...])
        for a_ref, o_ref in zip(refs[:n], refs[n + 1:]):
            part = _dot_tn(_bf(a_ref[...]), bb)

            @pl.when(k == 0)
            def _():
                o_ref[...] = part

            @pl.when(k > 0)
            def _():
                o_ref[...] += part

    return pl.pallas_call(
        body, name=name, grid=(K // tk,),
        in_specs=[pl.BlockSpec((tk, a.shape[1]), lambda k: (k, 0)) for a in a_list] + [pl.BlockSpec((tk, N), lambda k: (k, 0))],
        out_specs=[pl.BlockSpec((a.shape[1], N), lambda k: (0, 0)) for a in a_list],
        out_shape=[jax.ShapeDtypeStruct((a.shape[1], N), F32) for a in a_list],
        compiler_params=_cparams(("arbitrary",)),
    )(*a_list, b)


def _grad_x(dr, dz, dg, dq, dxbc, dkv, ddt, w, after, part, prev=None):
    L = dr.shape[0]
    tm = min(ROW_TILE, L // 4)
    first = L // (4 * tm)
    n = first if part == 0 else L // tm - first
    widths = (D_SSD, D_ATT, D_ATT, D_XBC, 2 * D_KV, DT_PAD)
    offs = (P_Z, P_G, P_Q, P_XBC, P_KV, P_DT)

    def body(dr_ref, dz_ref, dg_ref, dq_ref, dxbc_ref, dkv_ref, ddt_ref, w_ref, after_ref, *rest):
        o_ref = rest[-1]
        acc = ALPHA * dr_ref[...]
        for p_ref, off, wd in zip((dz_ref, dg_ref, dq_ref, dxbc_ref, dkv_ref, ddt_ref), offs, widths):
            acc = acc + _dot(_bf(p_ref[...]), w_ref[off:off + wd, :])
        o_ref[...] = acc

    row = lambda wd: pl.BlockSpec((tm, wd), lambda i: (i + part * first, 0))
    ins = [dr, dz, dg, dq, dxbc, dkv, ddt, w, after]
    specs = ([row(D_MODEL)] + [row(wd) for wd in widths]
             + [pl.BlockSpec((P_END, D_MODEL), lambda i: (0, 0), pipeline_mode=pl.Buffered(1)),
                pl.BlockSpec((8, 128), lambda i: (0, 0))])
    if prev is not None:
        ins.append(prev)
        specs.append(pl.BlockSpec(memory_space=pl.ANY))
    return pl.pallas_call(
        body, name="grad_x_%d" % part, grid=(n,),
        in_specs=specs, out_specs=row(D_MODEL),
        out_shape=jax.ShapeDtypeStruct((L, D_MODEL), F32),
        input_output_aliases={} if prev is None else {len(ins) - 1: 0},
        compiler_params=_cparams(("parallel",)),
    )(*ins)


HALO = 16


def _shift_matrix(offsets):
    n = CHUNK + HALO
    m = np.zeros((len(offsets) * CHUNK, 2 * n), np.float32)
    for k, off in enumerate(offsets):
        t = np.arange(CHUNK)
        m[k * CHUNK + t, t + off] = 1.0
        m[k * CHUNK + t, n + t + off] = 1.0
    return jnp.asarray(m, BF16)


def _shifted_rows(first_part, second_part, smat_ref):
    h1, l1 = _hi_lo(first_part)
    h2, l2 = _hi_lo(second_part)
    sh = _dot(smat_ref[...], jnp.concatenate([h1, h2, l1, l2], axis=0))
    return sh[0:CHUNK], sh[CHUNK:2 * CHUNK], sh[2 * CHUNK:3 * CHUNK]


def _ssd_chunk_pre(first, xbc_ref, tail_ref, dt_ref, cw_ref, cb_ref, dtb_ref, alog_ref, smat_ref):
    tail = jnp.where(first, 0.0, tail_ref[...])
    x = xbc_ref[...]
    taps = _shifted_rows(tail, x, smat_ref) + (x,)
    u = cb_ref[...] + cw_ref[0:1, :] * taps[0]
    for k in range(1, CONV_K):
        u = u + cw_ref[k:k + 1, :] * taps[k]
    sig = _sigmoid(u)
    xbc = u * sig
    dtraw = dt_ref[:, 0:SSD_HEADS] + dtb_ref[...]
    dt = jax.nn.softplus(dtraw)
    A = -jnp.exp(alog_ref[...])
    a = dt * A
    tril = (_iota2((CHUNK, CHUNK), 0) >= _iota2((CHUNK, CHUNK), 1)).astype(F32)
    acs = jnp.dot(tril, a, preferred_element_type=F32, precision=HI)
    acs_row = _to_rows(acs)
    return u, sig, xbc, dtraw, dt, A, acs, acs_row, taps


def _head_expander():
    return (_iota2((SSD_HEADS, D_SSD), 1) // SSD_P == _iota2((SSD_HEADS, D_SSD), 0)).astype(BF16)


def _hi_lo(x):
    hi = _bf(x)
    return hi, _bf(x - hi.astype(F32))


def _expand(v, e):
    hi, lo = _hi_lo(v)
    return _dot(hi, e) + _dot(lo, e)


def _headsum(t, e):
    m = t.shape[0]
    if m < 8:
        t = jnp.broadcast_to(t[0:1], (8, t.shape[1]))
    hi, lo = _hi_lo(t)
    return (_dot_nt(hi, e) + _dot_nt(lo, e))[0:m]


def _ssd_decays(dt, acs, dsk_ref, e):
    alast = acs[CHUNK - 1:CHUNK, :]
    stk = jnp.concatenate([dt, jnp.exp(acs), jnp.exp(alast - acs),
                           jnp.broadcast_to(jnp.exp(alast), (8, SSD_HEADS)),
                           jnp.broadcast_to(dsk_ref[...], (8, SSD_HEADS))], axis=0)
    ex = _expand(stk, e)
    return (ex[0:CHUNK], ex[CHUNK:2 * CHUNK], ex[2 * CHUNK:3 * CHUNK], ex[3 * CHUNK:3 * CHUNK + 1],
            ex[3 * CHUNK + 8:3 * CHUNK + 9])


def _ssd_fwd(z, xbc, dtp, conv_w, conv_b, dt_bias, a_log, d_skip, norm_w):
    L = z.shape[0]
    nc = L // CHUNK
    half = D_SSD // SSD_GROUPS

    def body(z_ref, xbc_ref, tail_ref, dt_ref, cw_ref, cb_ref, dtb_ref, alog_ref, dsk_ref, nw_ref, smat_ref,
             y_ref, ypre_ref, prev_ref, state, ybuf, mbuf):
        c = pl.program_id(0)

        @pl.when(c == 0)
        def _():
            state[...] = jnp.zeros_like(state)

        u, sig, xbcv, dtraw, dt, A, acs, acs_row, _ = _ssd_chunk_pre(
            c == 0, xbc_ref, tail_ref, dt_ref, cw_ref, cb_ref, dtb_ref, alog_ref, smat_ref)
        e = _head_expander()
        dtE, eacsE, dsdE, ealE, dskE = _ssd_decays(dt, acs, dsk_ref, e)
        xs = xbcv[:, 0:D_SSD]
        X = xs * dtE
        prev_ref[0] = state[...]
        causal = _iota2((CHUNK, CHUNK), 0) >= _iota2((CHUNK, CHUNK), 1)
        for g in range(SSD_GROUPS):
            gs = slice(half * g, half * (g + 1))
            Bg = _bf(xbcv[:, D_SSD + SSD_N * g:D_SSD + SSD_N * (g + 1)])
            Cg = _bf(xbcv[:, D_SSD + D_BC + SSD_N * g:D_SSD + D_BC + SSD_N * (g + 1)])
            cb = _dot_nt(Cg, Bg)
            for r in range(SSD_R):
                h = g * SSD_R + r
                seg = acs[:, h:h + 1] - acs_row[h:h + 1, :]
                mbuf[h] = _bf(cb * jnp.where(causal, jnp.exp(jnp.where(causal, seg, 0.0)), 0.0))
            st = state[:, gs]
            ybuf[:, gs] = _dot(Cg, _bf(st)) * eacsE[:, gs] + dskE[:, gs] * xs[:, gs]
            state[:, gs] = st * ealE[:, gs] + _dot_tn(Bg, _bf(X[:, gs] * dsdE[:, gs]))
        Xb = _bf(X)
        for h in range(SSD_HEADS):
            hs = slice(SSD_P * h, SSD_P * (h + 1))
            ybuf[:, hs] += _dot(mbuf[h], Xb[:, hs])
        y = ybuf[...]
        ypre_ref[...] = y
        zv = z_ref[...].astype(F32)
        yf = y * (zv * _sigmoid(zv))
        for g in range(SSD_GROUPS):
            gs = slice(half * g, half * (g + 1))
            yg = yf[:, gs]
            ms = jnp.mean(yg * yg, axis=-1, keepdims=True)
            y_ref[:, gs] = _bf(yg * lax.rsqrt(ms + RMS_EPS) * nw_ref[:, gs])

    full = lambda shape: pl.BlockSpec(shape, lambda c: (0, 0))
    return pl.pallas_call(
        body, name="ssd_fwd", grid=(nc,),
        in_specs=[
            pl.BlockSpec((CHUNK, D_SSD), lambda c: (c, 0)),
            pl.BlockSpec((CHUNK, D_XBC), lambda c: (c, 0)),
            pl.BlockSpec((HALO, D_XBC), lambda c: (jnp.maximum(c * (CHUNK // HALO) - 1, 0), 0)),
            pl.BlockSpec((CHUNK, DT_PAD), lambda c: (c, 0)),
            full((CONV_K, D_XBC)), full((1, D_XBC)), full((1, SSD_HEADS)), full((1, SSD_HEADS)), full((1, SSD_HEADS)),
            full((1, D_SSD)), full((3 * CHUNK, 2 * (CHUNK + HALO))),
        ],
        out_specs=[
            pl.BlockSpec((CHUNK, D_SSD), lambda c: (c, 0)),
            pl.BlockSpec((CHUNK, D_SSD), lambda c: (c, 0)),
            pl.BlockSpec((1, SSD_N, D_SSD), lambda c: (c, 0, 0)),
        ],
        out_shape=[
            jax.ShapeDtypeStruct((L, D_SSD), BF16),
            jax.ShapeDtypeStruct((L, D_SSD), F32),
            jax.ShapeDtypeStruct((nc, SSD_N, D_SSD), F32),
        ],
        scratch_shapes=[
            pltpu.VMEM((SSD_N, D_SSD), F32),
            pltpu.VMEM((CHUNK, D_SSD), F32),
            pltpu.VMEM((SSD_HEADS, CHUNK, CHUNK), BF16),
        ],
        compiler_params=_cparams(("arbitrary",)),
    )(z, xbc, xbc, dtp, conv_w, conv_b, dt_bias, a_log, d_skip, norm_w, _shift_matrix((13, 14, 15)))


def _ssd_bwd(dy, z, ypre, xbc, dtp, prev, conv_w, conv_b, dt_bias, a_log, d_skip, norm_w):
    L = z.shape[0]
    nc = L // CHUNK
    half = D_SSD // SSD_GROUPS

    def body(dy_ref, z_ref, ypre_ref, xbc_ref, tail_ref, dt_ref, prev_ref, cw_ref, cb_ref, dtb_ref, alog_ref, dsk_ref,
             nw_ref, smat_ref, smat2_ref, dz_ref, dxbc_ref, ddt_ref, gcw_ref, gcb_ref, gdtb_ref, galog_ref, gdsk_ref,
             gnw_ref, dstate, dhead, dpost, yobuf, bdbuf, lmbuf, dmbuf, cbbuf):
        i = pl.program_id(0)
        c = nc - 1 - i

        @pl.when(i == 0)
        def _():
            dstate[...] = jnp.zeros_like(dstate)
            dhead[...] = jnp.zeros_like(dhead)
            gcw_ref[...] = jnp.zeros_like(gcw_ref)
            gcb_ref[...] = jnp.zeros_like(gcb_ref)
            gdtb_ref[...] = jnp.zeros_like(gdtb_ref)
            galog_ref[...] = jnp.zeros_like(galog_ref)
            gdsk_ref[...] = jnp.zeros_like(gdsk_ref)
            gnw_ref[...] = jnp.zeros_like(gnw_ref)

        u, sig, xbcv, dtraw, dt, A, acs, acs_row, taps = _ssd_chunk_pre(
            c == 0, xbc_ref, tail_ref, dt_ref, cw_ref, cb_ref, dtb_ref, alog_ref, smat_ref)
        e = _head_expander()
        dtE, eacsE, dsdE, ealE, dskE = _ssd_decays(dt, acs, dsk_ref, e)
        alast = acs[CHUNK - 1:CHUNK, :]
        xs = xbcv[:, 0:D_SSD]
        X = xs * dtE
        Xb = _bf(X)

        zv = z_ref[...].astype(F32)
        ypre = ypre_ref[...]
        dyn = dy_ref[...].astype(F32)
        sz = _sigmoid(zv)
        silu_z = zv * sz
        yf = ypre * silu_z
        dyf_parts = []
        for g in range(SSD_GROUPS):
            gs = slice(half * g, half * (g + 1))
            yg = yf[:, gs]
            rstd = lax.rsqrt(jnp.mean(yg * yg, axis=-1, keepdims=True) + RMS_EPS)
            dout = dyn[:, gs]
            gnw_ref[:, gs] += jnp.sum(dout * yg * rstd, axis=0, keepdims=True)
            dyhat = dout * nw_ref[:, gs]
            dyf_parts.append(rstd * (dyhat - yg * (rstd * rstd) * jnp.mean(dyhat * yg, axis=-1, keepdims=True)))
        dyf = jnp.concatenate(dyf_parts, axis=1)
        dz_ref[...] = _bf(dyf * ypre * (sz * (1.0 + zv * (1.0 - sz))))
        dyp = dyf * silu_z
        dyb = _bf(dyp)
        G = dyp * eacsE

        causal = _iota2((CHUNK, CHUNK), 0) >= _iota2((CHUNK, CHUNK), 1)
        ST = prev_ref[0]
        dST = dstate[...]
        for g in range(SSD_GROUPS):
            gs = slice(half * g, half * (g + 1))
            bs = slice(D_SSD + SSD_N * g, D_SSD + SSD_N * (g + 1))
            cs = slice(D_SSD + D_BC + SSD_N * g, D_SSD + D_BC + SSD_N * (g + 1))
            Bg = _bf(xbcv[:, bs])
            Cg = _bf(xbcv[:, cs])
            Gb = _bf(G[:, gs])
            STb = _bf(ST[:, gs])
            dSTb = _bf(dST[:, gs])
            dstate[:, gs] = dST[:, gs] * ealE[:, gs] + _dot_tn(Cg, Gb)
            yobuf[:, gs] = _dot(Cg, STb) * eacsE[:, gs]
            bdbuf[:, gs] = _dot(Bg, dSTb)
            dpost[:, cs] = _dot_nt(Gb, STb)
            dpost[:, bs] = _dot_nt(_bf(X[:, gs] * dsdE[:, gs]), dSTb)
            cbbuf[g] = _dot_nt(Cg, Bg)
            for r in range(SSD_R):
                h = g * SSD_R + r
                seg = acs[:, h:h + 1] - acs_row[h:h + 1, :]
                lmbuf[h] = jnp.where(causal, jnp.exp(jnp.where(causal, seg, 0.0)), 0.0)
        for h in range(SSD_HEADS):
            hs = slice(SSD_P * h, SSD_P * (h + 1))
            Mb = _bf(cbbuf[h // SSD_R] * lmbuf[h])
            dmbuf[h] = _dot_nt(dyb[:, hs], Xb[:, hs])
            dpost[:, hs] = _dot_tn(Mb, dyb[:, hs])
        lane16 = _iota2((1, SSD_HEADS), 1)
        sub16 = _iota2((SSD_HEADS, 1), 0)
        dacs_col = jnp.zeros((CHUNK, SSD_HEADS), F32)
        dacs_row = jnp.zeros((SSD_HEADS, CHUNK), F32)
        for g in range(SSD_GROUPS):
            bs = slice(D_SSD + SSD_N * g, D_SSD + SSD_N * (g + 1))
            cs = slice(D_SSD + D_BC + SSD_N * g, D_SSD + D_BC + SSD_N * (g + 1))
            cb = cbbuf[g]
            dcb = jnp.zeros((CHUNK, CHUNK), F32)
            for r in range(SSD_R):
                h = g * SSD_R + r
                dM = dmbuf[h]
                Lm = lmbuf[h]
                dcb = dcb + dM * Lm
                dseg = dM * (cb * Lm)
                dacs_col = dacs_col + jnp.sum(dseg, axis=-1, keepdims=True) * (lane16 == h).astype(F32)
                dacs_row = dacs_row - jnp.sum(dseg, axis=0, keepdims=True) * (sub16 == h).astype(F32)
            dcbb = _bf(dcb)
            dpost[:, bs] += _dot_tn(dcbb, _bf(xbcv[:, cs]))
            dpost[:, cs] += _dot(dcbb, _bf(xbcv[:, bs]))

        BD = bdbuf[...]
        dX = dpost[:, 0:D_SSD] + dsdE * BD
        dsd = jnp.exp(alast - acs)
        T = _headsum(X * BD, e) * dsd
        dalast = jnp.sum(T, axis=0, keepdims=True) + _headsum(
            jnp.sum(dST * ST, axis=0, keepdims=True), e) * jnp.exp(alast)
        is_last = (_iota2((CHUNK, 1), 0) == CHUNK - 1).astype(F32)
        dacs = dacs_col + _to_cols(dacs_row) + _headsum(dyp * yobuf[...], e) - T + is_last * dalast
        triu = (_iota2((CHUNK, CHUNK), 0) <= _iota2((CHUNK, CHUNK), 1)).astype(F32)
        da = jnp.dot(triu, dacs, preferred_element_type=F32, precision=HI)
        ddt_tot = _headsum(dX * xs, e) + da * A
        galog_ref[...] += jnp.sum(da * dt, axis=0, keepdims=True) * A
        ddtraw = ddt_tot * _sigmoid(dtraw)
        gdtb_ref[...] += jnp.sum(ddtraw, axis=0, keepdims=True)
        gdsk_ref[...] += _headsum(jnp.sum(dyp * xs, axis=0, keepdims=True), e)
        ddt_ref[...] = jnp.zeros_like(ddt_ref)
        ddt_ref[:, 0:SSD_HEADS] = ddtraw
        dpost[:, 0:D_SSD] = dX * dtE + dskE * dyp

        dconv = dpost[...] * (sig * (1.0 + u * (1.0 - sig)))
        gcb_ref[...] += jnp.sum(dconv, axis=0, keepdims=True)
        for k in range(CONV_K):
            gcw_ref[k:k + 1, :] += jnp.sum(dconv * taps[k], axis=0, keepdims=True)
        later = _shifted_rows(dconv, dhead[...], smat2_ref)
        dx = cw_ref[CONV_K - 1:CONV_K, :] * dconv
        for k in range(CONV_K - 1):
            dx = dx + cw_ref[k:k + 1, :] * later[k]
        dxbc_ref[...] = _bf(dx)
        dhead[...] = dconv[0:HALO, :]

    full = lambda shape: pl.BlockSpec(shape, lambda i: (0, 0))
    rev = lambda wd: pl.BlockSpec((CHUNK, wd), lambda i: (nc - 1 - i, 0))
    return pl.pallas_call(
        body, name="ssd_bwd", grid=(nc,),
        in_specs=[
            rev(D_SSD), rev(D_SSD), rev(D_SSD), rev(D_XBC),
            pl.BlockSpec((HALO, D_XBC), lambda i: (jnp.maximum((nc - 1 - i) * (CHUNK // HALO) - 1, 0), 0)),
            rev(DT_PAD),
            pl.BlockSpec((1, SSD_N, D_SSD), lambda i: (nc - 1 - i, 0, 0)),
            full((CONV_K, D_XBC)), full((1, D_XBC)), full((1, SSD_HEADS)), full((1, SSD_HEADS)), full((1, SSD_HEADS)),
            full((1, D_SSD)), full((3 * CHUNK, 2 * (CHUNK + HALO))), full((3 * CHUNK, 2 * (CHUNK + HALO))),
        ],
        out_specs=[
            rev(D_SSD), rev(D_XBC), rev(DT_PAD),
            full((CONV_K, D_XBC)), full((1, D_XBC)), full((1, SSD_HEADS)), full((1, SSD_HEADS)), full((1, SSD_HEADS)),
            full((1, D_SSD)),
        ],
        out_shape=[
            jax.ShapeDtypeStruct((L, D_SSD), BF16), jax.ShapeDtypeStruct((L, D_XBC), BF16),
            jax.ShapeDtypeStruct((L, DT_PAD), F32),
            jax.ShapeDtypeStruct((CONV_K, D_XBC), F32), jax.ShapeDtypeStruct((1, D_XBC), F32),
            jax.ShapeDtypeStruct((1, SSD_HEADS), F32), jax.ShapeDtypeStruct((1, SSD_HEADS), F32),
            jax.ShapeDtypeStruct((1, SSD_HEADS), F32), jax.ShapeDtypeStruct((1, D_SSD), F32),
        ],
        scratch_shapes=[
            pltpu.VMEM((SSD_N, D_SSD), F32),
            pltpu.VMEM((HALO, D_XBC), F32),
            pltpu.VMEM((CHUNK, D_XBC), F32),
            pltpu.VMEM((CHUNK, D_SSD), F32),
            pltpu.VMEM((CHUNK, D_SSD), F32),
            pltpu.VMEM((SSD_HEADS, CHUNK, CHUNK), F32),
            pltpu.VMEM((SSD_HEADS, CHUNK, CHUNK), F32),
            pltpu.VMEM((SSD_GROUPS, CHUNK, CHUNK), F32),
        ],
        compiler_params=_cparams(("arbitrary",)),
    )(dy, z, ypre, xbc, xbc, dtp, prev, conv_w, conv_b, dt_bias, a_log, d_skip, norm_w, _shift_matrix((13, 14, 15)),
      _shift_matrix((3, 2, 1)))


def _rope_tables(pos_ref, inv_ref):
    ang = pos_ref[...].astype(F32) * inv_ref[...]
    d = _iota2((1, 2 * ATT_HD), 1) % ATT_HD
    s = jnp.sin(ang)
    return jnp.cos(ang), jnp.where(d < ROPE_DIM // 2, -s, 0.0), jnp.where((d >= ROPE_DIM // 2) & (d < ROPE_DIM), s, 0.0)


def _rope(t, tabs):
    c, s1, s2 = tabs
    n = t.shape[1]
    rep = n // c.shape[1]
    return (t * jnp.tile(c, (1, rep)) + pltpu.roll(t, n - ROPE_DIM // 2, 1) * jnp.tile(s1, (1, rep))
            + pltpu.roll(t, ROPE_DIM // 2, 1) * jnp.tile(s2, (1, rep)))


def _rope_t(t, tabs):
    c, s1, s2 = tabs
    n = t.shape[1]
    rep = n // c.shape[1]
    return (t * jnp.tile(c, (1, rep)) + pltpu.roll(t * jnp.tile(s1, (1, rep)), ROPE_DIM // 2, 1)
            + pltpu.roll(t * jnp.tile(s2, (1, rep)), n - ROPE_DIM // 2, 1))


def _stack_heads(t, j):
    return jnp.concatenate([t[:, ATT_HD * (j * ATT_R + r):ATT_HD * (j * ATT_R + r + 1)] for r in range(ATT_R)], axis=0)


def _swa_mask_t(first):
    si = _iota2((2 * WINDOW, ATT_R * WINDOW), 0)
    qi = _iota2((2 * WINDOW, ATT_R * WINDOW), 1) % WINDOW
    band = (si > qi) & (si <= qi + WINDOW)
    return band & (jnp.logical_not(first) | (si >= WINDOW))


def _head_rows(ref, j):
    if ref.shape[0] == 1:
        parts = [jnp.broadcast_to(ref[:, j * ATT_R + r:j * ATT_R + r + 1], (1, WINDOW)) for r in range(ATT_R)]
    else:
        parts = [ref[j * ATT_R + r:j * ATT_R + r + 1, :] for r in range(ATT_R)]
    return jnp.concatenate(parts, axis=1)


def _swa_fwd(q, g, kv, sinks):
    L = q.shape[0]
    nb = L // WINDOW
    scale = ATT_HD ** -0.5

    def body(q_ref, g_ref, kvc_ref, kvp_ref, sink_ref, y_ref, o_ref, lse_ref, otbuf):
        n = pl.program_id(0)
        kk = jnp.concatenate([kvp_ref[:, 0:D_KV], kvc_ref[:, 0:D_KV]], axis=0)
        vv = jnp.concatenate([kvp_ref[:, D_KV:2 * D_KV], kvc_ref[:, D_KV:2 * D_KV]], axis=0)
        valid = _swa_mask_t(n == 0)
        qv = q_ref[...]
        for j in range(ATT_KVH):
            js = slice(ATT_HD * j, ATT_HD * (j + 1))
            st = _dot_nt(kk[:, js], _stack_heads(qv, j)) * scale
            st = jnp.where(valid, st, NEG_BIG)
            sink = _head_rows(sink_ref, j)
            m = jnp.maximum(jnp.max(st, axis=0, keepdims=True), sink)
            p = jnp.exp(st - m)
            denom = jnp.sum(p, axis=0, keepdims=True) + jnp.exp(sink - m)
            ot = _dot_tn(vv[:, js], _bf(p)) * (1.0 / denom)
            lse = m + jnp.log(denom)
            for r in range(ATT_R):
                h = j * ATT_R + r
                otbuf[ATT_HD * h:ATT_HD * (h + 1), :] = ot[:, WINDOW * r:WINDOW * (r + 1)]
                lse_ref[h:h + 1, :] = lse[:, WINDOW * r:WINDOW * (r + 1)]
        o = otbuf[...].T
        o_ref[...] = o
        gv = g_ref[...].astype(F32)
        y_ref[...] = _bf(o * (gv * _sigmoid(gv)))

    cur = lambda wd: pl.BlockSpec((WINDOW, wd), lambda n: (n, 0))
    prv = lambda wd: pl.BlockSpec((WINDOW, wd), lambda n: (jnp.maximum(n - 1, 0), 0))
    return pl.pallas_call(
        body, name="swa_fwd", grid=(nb,),
        in_specs=[cur(D_ATT), cur(D_ATT), cur(2 * D_KV), prv(2 * D_KV), pl.BlockSpec((1, ATT_QH), lambda n: (0, 0))],
        out_specs=[cur(D_ATT), cur(D_ATT), pl.BlockSpec((ATT_QH, WINDOW), lambda n: (0, n))],
        out_shape=[jax.ShapeDtypeStruct((L, D_ATT), BF16), jax.ShapeDtypeStruct((L, D_ATT), F32),
                   jax.ShapeDtypeStruct((ATT_QH, L), F32)],
        scratch_shapes=[pltpu.VMEM((D_ATT, WINDOW), F32)],
        compiler_params=_cparams(("parallel",)),
    )(q, g, kv, kv, sinks)


def _swa_bwd(dy, q, g, kv, o, lse, pos, inv, sinks):
    L = q.shape[0]
    nb = L // WINDOW
    scale = ATT_HD ** -0.5

    def body(dy_ref, q_ref, g_ref, kvc_ref, kvp_ref, o_ref, lse_ref, posc_ref, posp_ref, inv_ref, sink_ref,
             dq_ref, dg_ref, dkv_ref, dsink_ref, carry, dqbuf, dkbuf, dvbuf):
        n = pl.program_id(0)

        @pl.when(n == 0)
        def _():
            dsink_ref[...] = jnp.zeros_like(dsink_ref)

        @pl.when(n < nb)
        def _():
            tc = _rope_tables(posc_ref, inv_ref)
            tp = _rope_tables(posp_ref, inv_ref)
            kk = jnp.concatenate([kvp_ref[:, 0:D_KV], kvc_ref[:, 0:D_KV]], axis=0)
            vv = jnp.concatenate([kvp_ref[:, D_KV:2 * D_KV], kvc_ref[:, D_KV:2 * D_KV]], axis=0)
            valid = _swa_mask_t(n == 0)
            qv = q_ref[...]
            gv = g_ref[...].astype(F32)
            sg = _sigmoid(gv)
            dyv = dy_ref[...].astype(F32)
            ov = o_ref[...]
            dg_ref[...] = _bf(dyv * ov * (sg * (1.0 + gv * (1.0 - sg))))
            do = dyv * (gv * sg)
            dod = do * ov
            ones = jnp.ones((8, ATT_HD), BF16)
            lane16 = _iota2((1, ATT_QH), 1)
            dsink = jnp.zeros((1, ATT_QH), F32)
            for j in range(ATT_KVH):
                js = slice(ATT_HD * j, ATT_HD * (j + 1))
                kj = kk[:, js]
                vj = vv[:, js]
                qs = _stack_heads(qv, j)
                dos = _bf(_stack_heads(do, j))
                hi, lo = _hi_lo(_stack_heads(dod, j))
                delta = (_dot_nt(ones, hi) + _dot_nt(ones, lo))[0:1]
                lse = _head_rows(lse_ref, j)
                st = _dot_nt(kj, qs) * scale
                pt = jnp.exp(jnp.where(valid, st, NEG_BIG) - lse)
                dst = _bf(pt * (_dot_nt(vj, dos) - delta))
                dqt = _dot_tn(kj, dst) * scale
                dkbuf[:, js] = _dot(dst, qs) * scale
                dvbuf[:, js] = _dot(_bf(pt), dos)
                sd = jnp.exp(_head_rows(sink_ref, j) - lse) * delta
                for r in range(ATT_R):
                    h = j * ATT_R + r
                    ls = slice(WINDOW * r, WINDOW * (r + 1))
                    dqbuf[ATT_HD * h:ATT_HD * (h + 1), :] = dqt[:, ls]
                    dsink = dsink - jnp.sum(sd[:, ls], axis=1, keepdims=True) * (lane16 == h).astype(F32)
            dsink_ref[...] += dsink
            dq_ref[...] = _bf(_rope_t(dqbuf[...].T, tc))
            dkp = _rope_t(dkbuf[0:WINDOW, :], tp)
            dkc = _rope_t(dkbuf[WINDOW:2 * WINDOW, :], tc)

            @pl.when(n > 0)
            def _():
                dkv_ref[:, 0:D_KV] = _bf(carry[:, 0:D_KV] + dkp)
                dkv_ref[:, D_KV:2 * D_KV] = _bf(carry[:, D_KV:2 * D_KV] + dvbuf[0:WINDOW, :])

            carry[:, 0:D_KV] = dkc
            carry[:, D_KV:2 * D_KV] = dvbuf[WINDOW:2 * WINDOW, :]

        @pl.when(n == nb)
        def _():
            dkv_ref[...] = _bf(carry[...])

    last = nb - 1
    cur = lambda wd: pl.BlockSpec((WINDOW, wd), lambda n: (jnp.minimum(n, last), 0))
    prv = lambda wd: pl.BlockSpec((WINDOW, wd), lambda n: (jnp.maximum(jnp.minimum(n, last) - 1, 0), 0))
    return pl.pallas_call(
        body, name="swa_bwd", grid=(nb + 1,),
        in_specs=[cur(D_ATT), cur(D_ATT), cur(D_ATT), cur(2 * D_KV), prv(2 * D_KV), cur(D_ATT),
                  pl.BlockSpec((ATT_QH, WINDOW), lambda n: (0, jnp.minimum(n, last))), cur(1), prv(1),
                  pl.BlockSpec((1, 2 * ATT_HD), lambda n: (0, 0)), pl.BlockSpec((1, ATT_QH), lambda n: (0, 0))],
        out_specs=[cur(D_ATT), cur(D_ATT),
                   pl.BlockSpec((WINDOW, 2 * D_KV), lambda n: (jnp.maximum(n - 1, 0), 0)),
                   pl.BlockSpec((1, ATT_QH), lambda n: (0, 0))],
        out_shape=[jax.ShapeDtypeStruct((L, D_ATT), BF16), jax.ShapeDtypeStruct((L, D_ATT), BF16),
                   jax.ShapeDtypeStruct((L, 2 * D_KV), BF16), jax.ShapeDtypeStruct((1, ATT_QH), F32)],
        scratch_shapes=[pltpu.VMEM((WINDOW, 2 * D_KV), F32), pltpu.VMEM((D_ATT, WINDOW), F32),
                        pltpu.VMEM((2 * WINDOW, D_KV), F32), pltpu.VMEM((2 * WINDOW, D_KV), F32)],
        compiler_params=_cparams(("arbitrary",)),
    )(dy, q, g, kv, kv, o, lse, pos, pos, inv, sinks)


def _out_ln_loss(y_ssd, y_att, x, target, w_out, ln_g, ln_b):
    L = x.shape[0]
    tm = min(ROW_TILE, L)
    nt = L // tm
    inv_d = 1.0 / D_MODEL

    def body(ys_ref, ya_ref, x_ref, t_ref, w_ref, g_ref, b_ref, dr_ref, dys_ref, dya_ref, loss_ref, gg_ref, gb_ref,
             gwo_ref, acc_ref):
        i = pl.program_id(0)

        @pl.when(i == 0)
        def _():
            loss_ref[...] = jnp.zeros_like(loss_ref)
            gg_ref[...] = jnp.zeros_like(gg_ref)
            gb_ref[...] = jnp.zeros_like(gb_ref)
            acc_ref[...] = jnp.zeros_like(acc_ref)

        h = _dot(_bf(ys_ref[...]), w_ref[0:D_SSD, :]) + _dot(_bf(ya_ref[...]), w_ref[D_SSD:D_MIX, :])
        r = ALPHA * x_ref[...] + h
        mu = jnp.mean(r, axis=-1, keepdims=True)
        xc = r - mu
        rstd = lax.rsqrt(jnp.mean(xc * xc, axis=-1, keepdims=True) + LN_EPS)
        xhat = xc * rstd
        gam = g_ref[...]
        diff = xhat * gam + b_ref[...] - t_ref[...]
        part = jnp.sum(jnp.sum(diff * diff, axis=-1, keepdims=True), axis=0, keepdims=True)
        loss_ref[...] += (0.5 * inv_d) * part
        dout = diff * inv_d
        gg_ref[...] += jnp.sum(dout * xhat, axis=0, keepdims=True)
        gb_ref[...] += jnp.sum(dout, axis=0, keepdims=True)
        dxh = dout * gam
        dr = rstd * (dxh - jnp.mean(dxh, axis=-1, keepdims=True) - xhat * jnp.mean(dxh * xhat, axis=-1, keepdims=True))
        dr_ref[...] = dr
        drb = _bf(dr)
        dys_ref[...] = _bf(_dot_nt(drb, w_ref[0:D_SSD, :]))
        dya_ref[...] = _bf(_dot_nt(drb, w_ref[D_SSD:D_MIX, :]))
        acc_ref[0:D_SSD, :] += _dot_tn(_bf(ys_ref[...]), drb)
        acc_ref[D_SSD:D_MIX, :] += _dot_tn(_bf(ya_ref[...]), drb)

        @pl.when(i == nt - 1)
        def _():
            gwo_ref[...] = _bf(acc_ref[...])

    row = pl.BlockSpec((tm, D_MODEL), lambda i: (i, 0))
    vec = pl.BlockSpec((1, D_MODEL), lambda i: (0, 0))
    return pl.pallas_call(
        body, name="out_ln_loss", grid=(nt,),
        in_specs=[row, row, row, row, pl.BlockSpec((D_MIX, D_MODEL), lambda i: (0, 0), pipeline_mode=pl.Buffered(1)), vec, vec],
        out_specs=[row, row, row, pl.BlockSpec((1, 128), lambda i: (0, 0)), vec, vec,
                   pl.BlockSpec((D_MIX, D_MODEL), lambda i: (0, 0))],
        out_shape=[jax.ShapeDtypeStruct((L, D_MODEL), F32)] + [jax.ShapeDtypeStruct((L, D_MODEL), BF16)] * 2
        + [jax.ShapeDtypeStruct((1, 128), F32)]
        + [jax.ShapeDtypeStruct((1, D_MODEL), F32)] * 2 + [jax.ShapeDtypeStruct((D_MIX, D_MODEL), BF16)],
        scratch_shapes=[pltpu.VMEM((D_MIX, D_MODEL), F32)],
        compiler_params=_cparams(("arbitrary",)),
    )(y_ssd, y_att, x, target, w_out, ln_g, ln_b)


def _local_step(x, pos, target, w, get_w_out, token, conv_w, conv_b, dt_bias, a_log, d_skip, norm_w, sinks, ln_g, ln_b):
    inv8 = ROPE_THETA ** (-jnp.arange(0, ROPE_DIM, 2, dtype=F32) / ROPE_DIM)
    inv = jnp.tile(jnp.concatenate([inv8, inv8, jnp.zeros((ATT_HD - ROPE_DIM,), F32)]), 2).reshape(1, 2 * ATT_HD)
    inv = inv + token

    z, g, q, xbc, kv, dtp, xb = _in_proj(x, w, pos, inv)
    y_ssd, y_pre, prev = _ssd_fwd(z, xbc, dtp, conv_w, conv_b, dt_bias, a_log, d_skip, norm_w)
    y_att, o, lse = _swa_fwd(q, g, kv, sinks)
    w_out = get_w_out(lse)
    dr, dy_ssd, dy_att, loss, g_ln_g, g_ln_b, gw_out = _out_ln_loss(y_ssd, y_att, x, target, w_out, ln_g, ln_b)
    w_out_red = _reduce_w_out_start(gw_out.reshape(N_CHIPS, W_OUT_ROWS, D_MODEL), loss)
    inv = inv + w_out_red[16][0:1, :]
    dq, dg, dkv, g_sinks = _swa_bwd(dy_att, q, g, kv, o, lse, pos, inv, sinks)
    dz, dxbc, ddt, g_conv_w, g_conv_b, g_dt_bias, g_a_log, g_d_skip, g_norm_w = _ssd_bwd(
        dy_ssd, z, y_pre, xbc, dtp, prev, conv_w, conv_b, dt_bias, a_log, d_skip, norm_w)
    gw_z, gw_g, gw_q = _matmuls_tn([dz, dg, dq], xb, "gw_zgq")
    gw_xbc, gw_kv, gw_dt = _matmuls_tn([dxbc, dkv, ddt], xb, "gw_xbc_kv_dt")
    gw_in = jnp.concatenate([gw_z, gw_xbc, gw_dt[0:SSD_HEADS], gw_q, gw_kv, gw_g], axis=0)
    small = dict(conv_w=g_conv_w, conv_b=g_conv_b, dt_bias=g_dt_bias, a_log=g_a_log, d_skip=g_d_skip,
                 ssd_norm_w=g_norm_w, attn_sinks=g_sinks, ln_g=g_ln_g, ln_b=g_ln_b)
    return loss, (dr, dz, dg, dq, dxbc, dkv, ddt, w), gw_in, w_out_red, small


def _mesh_pos():
    return lax.axis_index("x"), lax.axis_index("y"), lax.axis_index("c")


def _gather_weights(w_in_s, conv_w_s):
    hr = w_in_s.shape[0] // 2
    qa = 336
    quarters = ((0, qa), (qa, hr - qa))

    def body(win_ref, cw_ref, owin_ref, ocw_ref, stage, send_sems, recv_sems, small_send, small_recv, local_sems):
        x, y, c = _mesh_pos()
        me = 2 * x + y
        sibling = (x, y, 1 - c)
        xn, yn, dg = (1 - x, y), (x, 1 - y), (1 - x, 1 - y)
        chips = [xn, yn, dg]
        load = pltpu.make_async_copy(win_ref, stage, local_sems.at[1])
        load.start()
        locals_ = [pltpu.make_async_copy(cw_ref, ocw_ref.at[me], local_sems.at[0])]
        for cp in locals_:
            cp.start()
        started = []

        def piece(ref, chip, half, q):
            off, n = quarters[q]
            return ref.at[2 * chip[0] + chip[1]].at[pl.ds(half * hr + off, n), :]

        def mine(q):
            off, n = quarters[q]
            return win_ref.at[pl.ds(c * hr + off, n), :]

        def copy(src, dst, k, to):
            return pltpu.make_async_remote_copy(src_ref=src, dst_ref=dst, send_sem=send_sems.at[k], recv_sem=recv_sems.at[k],
                                                device_id=to, device_id_type=MESH)

        def go(cp):
            cp.start()
            started.append(cp)

        go(copy(mine(0), piece(owin_ref, (x, y), c, 0), 0, (*xn, c)))
        go(copy(mine(1), piece(owin_ref, (x, y), c, 1), 2, (*yn, c)))
        go(copy(mine(1), piece(owin_ref, (x, y), c, 1), 1, (*xn, c)))
        go(copy(mine(0), piece(owin_ref, (x, y), c, 0), 3, (*yn, c)))
        for j, (px, py) in enumerate(chips):
            cp = pltpu.make_async_remote_copy(
                src_ref=cw_ref, dst_ref=ocw_ref.at[me], send_sem=small_send.at[j], recv_sem=small_recv.at[j],
                device_id=(px, py, c), device_id_type=MESH)
            go(cp)
        load.wait()
        store = pltpu.make_async_copy(stage, owin_ref.at[me], local_sems.at[2])
        store.start()
        locals_.append(store)
        arrivals = [(0, xn, 0, (4, (*yn, c))), (2, yn, 1, (5, (*xn, c))), (1, xn, 1, None), (3, yn, 0, None),
                    (4, dg, 0, None), (5, dg, 1, None)]
        for n, (k, chip, q, onward) in enumerate(arrivals):
            blk = piece(owin_ref, chip, c, q)
            copy(blk, blk, k, sibling).wait_recv()
            if onward is not None:
                go(copy(blk, blk, onward[0], onward[1]))
            go(copy(blk, blk, 6 + n, sibling))
        for n, (k, chip, q, onward) in enumerate(arrivals):
            blk = piece(owin_ref, chip, 1 - c, q)
            copy(blk, blk, 6 + n, sibling).wait_recv()
        for j in range(3):
            pltpu.make_async_remote_copy(
                src_ref=cw_ref, dst_ref=ocw_ref.at[me], send_sem=small_send.at[j], recv_sem=small_recv.at[j],
                device_id=sibling, device_id_type=MESH).wait_recv()
        for cp in started:
            cp.wait_send()
        for cp in locals_:
            cp.wait()

    any_spec = pl.BlockSpec(memory_space=pl.ANY)
    return pl.pallas_call(
        body, name="gather_weights",
        in_specs=[any_spec] * 2, out_specs=[any_spec] * 2,
        out_shape=[jax.ShapeDtypeStruct((N_CHIPS,) + a.shape, a.dtype) for a in (w_in_s, conv_w_s)],
        scratch_shapes=[pltpu.VMEM(w_in_s.shape, w_in_s.dtype),
                        pltpu.SemaphoreType.DMA((12,)), pltpu.SemaphoreType.DMA((12,)),
                        pltpu.SemaphoreType.DMA((3,)), pltpu.SemaphoreType.DMA((3,)), pltpu.SemaphoreType.DMA((3,))],
    )(w_in_s, conv_w_s)


_HBM = pl.BlockSpec(memory_space=pltpu.HBM)
_SEM = pl.BlockSpec(memory_space=pltpu.SEMAPHORE)
_EFFECT = pltpu.SideEffectType.DATAFLOW_SIDE_EFFECTING


def _gather_w_out_start(w_out_s, after):
    def body(src_ref, land_ref, after_ref, s0, s1, s2, r0, r1, r2, src_thru, land_thru, token):
        x, y, c = _mesh_pos()
        me = 2 * x + y
        chips = [(1 - x, y), (x, 1 - y), (1 - x, 1 - y)]
        for (px, py), s, r in zip(chips, (s0, s1, s2), (r0, r1, r2)):
            pltpu.make_async_remote_copy(src_ref=src_ref, dst_ref=land_ref.at[me], send_sem=s, recv_sem=r,
                                         device_id=(px, py, c), device_id_type=MESH).start()
        token[...] = jnp.zeros_like(token)

    sem = pltpu.SemaphoreType.DMA(())
    land = lax.empty((N_CHIPS,) + w_out_s.shape, w_out_s.dtype)
    return pl.pallas_call(
        body, name="gather_w_out_start",
        out_shape=(sem,) * 6 + (pltpu.HBM(w_out_s.shape, w_out_s.dtype), pltpu.HBM(land.shape, land.dtype),
                                jax.ShapeDtypeStruct((8, 128), F32)),
        in_specs=(_HBM, _HBM, pl.BlockSpec(memory_space=pl.ANY)),
        out_specs=(_SEM,) * 6 + (_HBM, _HBM, pl.BlockSpec(memory_space=pltpu.VMEM)),
        input_output_aliases={0: 6, 1: 7},
        compiler_params=pltpu.CompilerParams(has_side_effects=_EFFECT),
    )(pltpu.with_memory_space_constraint(w_out_s, pltpu.HBM), pltpu.with_memory_space_constraint(land, pltpu.HBM), after)


def _gather_w_out_wait(sems, src_thru, land_thru, after):
    def body(src_ref, land_ref, s0, s1, s2, r0, r1, r2, after_ref, src_dead, got_ref):
        x, y, c = _mesh_pos()
        chips = [(1 - x, y), (x, 1 - y), (1 - x, 1 - y)]
        for (px, py), s, r in zip(chips, (s0, s1, s2), (r0, r1, r2)):
            cp = pltpu.make_async_remote_copy(src_ref=src_ref, dst_ref=land_ref.at[2 * px + py], send_sem=s, recv_sem=r,
                                              device_id=(px, py, c), device_id_type=MESH)
            cp.wait_send()
            cp.wait_recv()

    return pl.pallas_call(
        body, name="gather_w_out_wait",
        out_shape=(pltpu.HBM(src_thru.shape, src_thru.dtype), pltpu.HBM(land_thru.shape, land_thru.dtype)),
        in_specs=(_HBM, _HBM) + (_SEM,) * 6 + (pl.BlockSpec(memory_space=pl.ANY),),
        out_specs=(_HBM, _HBM), input_output_aliases={0: 0, 1: 1},
        compiler_params=pltpu.CompilerParams(has_side_effects=_EFFECT),
    )(src_thru, land_thru, *sems, after)[1]


def _pair_start(gw_in, after):
    hr = gw_in.shape[1] // 2

    def body(src_ref, land_ref, after_ref, *refs):
        x, y, c = _mesh_pos()
        for j in range(N_CHIPS):
            pltpu.make_async_remote_copy(
                src_ref=src_ref.at[j, pl.ds((1 - c) * hr, hr), :], dst_ref=land_ref.at[j], send_sem=refs[j],
                recv_sem=refs[N_CHIPS + j], device_id=(x, y, 1 - c), device_id_type=MESH).start()
        refs[10][...] = jnp.zeros_like(refs[10])

    sem = pltpu.SemaphoreType.DMA(())
    land = lax.empty((N_CHIPS, hr, D_MODEL), F32)
    return pl.pallas_call(
        body, name="pair_start",
        out_shape=(sem,) * 8 + (pltpu.HBM(gw_in.shape, F32), pltpu.HBM(land.shape, F32), jax.ShapeDtypeStruct((8, 128), F32)),
        in_specs=(_HBM, _HBM, pl.BlockSpec(memory_space=pl.ANY)),
        out_specs=(_SEM,) * 8 + (_HBM, _HBM, pl.BlockSpec(memory_space=pltpu.VMEM)),
        input_output_aliases={0: 8, 1: 9},
        compiler_params=pltpu.CompilerParams(has_side_effects=_EFFECT),
    )(pltpu.with_memory_space_constraint(gw_in, pltpu.HBM), pltpu.with_memory_space_constraint(land, pltpu.HBM), after)


def _pair_wait(sems, gw_thru, land_thru, after):
    hr = land_thru.shape[1]

    def body(src_ref, land_ref, *refs):
        x, y, c = _mesh_pos()
        for j in range(N_CHIPS):
            cp = pltpu.make_async_remote_copy(
                src_ref=src_ref.at[j, pl.ds((1 - c) * hr, hr), :], dst_ref=land_ref.at[j], send_sem=refs[j],
                recv_sem=refs[N_CHIPS + j], device_id=(x, y, 1 - c), device_id_type=MESH)
            cp.wait_send()
            cp.wait_recv()

    return pl.pallas_call(
        body, name="pair_wait",
        out_shape=(pltpu.HBM(gw_thru.shape, F32), pltpu.HBM(land_thru.shape, F32)),
        in_specs=(_HBM, _HBM) + (_SEM,) * 8 + (pl.BlockSpec(memory_space=pl.ANY),),
        out_specs=(_HBM, _HBM), input_output_aliases={0: 0, 1: 1},
        compiler_params=pltpu.CompilerParams(has_side_effects=_EFFECT),
    )(gw_thru, land_thru, *sems, after)


def _chip_start(s_in, after):
    def body(src_ref, land_ref, after_ref, *refs):
        x, y, c = _mesh_pos()
        me = 2 * x + y
        for j, (px, py) in enumerate([(1 - x, y), (x, 1 - y), (1 - x, 1 - y)]):
            pltpu.make_async_remote_copy(
                src_ref=src_ref.at[2 * px + py], dst_ref=land_ref.at[me], send_sem=refs[j], recv_sem=refs[3 + j],
                device_id=(px, py, c), device_id_type=MESH).start()
        refs[8][...] = jnp.zeros_like(refs[8])

    sem = pltpu.SemaphoreType.DMA(())
    land = lax.empty(s_in.shape, s_in.dtype)
    return pl.pallas_call(
        body, name="chip_start",
        out_shape=(sem,) * 6 + (pltpu.HBM(s_in.shape, s_in.dtype), pltpu.HBM(land.shape, land.dtype),
                                jax.ShapeDtypeStruct((8, 128), F32)),
        in_specs=(_HBM, _HBM, pl.BlockSpec(memory_space=pl.ANY)),
        out_specs=(_SEM,) * 6 + (_HBM, _HBM, pl.BlockSpec(memory_space=pltpu.VMEM)),
        input_output_aliases={0: 6, 1: 7},
        compiler_params=pltpu.CompilerParams(has_side_effects=_EFFECT),
    )(pltpu.with_memory_space_constraint(s_in, pltpu.HBM), pltpu.with_memory_space_constraint(land, pltpu.HBM), after)


def _chip_wait(sems, s_thru, land_thru, after):
    def body(src_ref, land_ref, *refs):
        x, y, c = _mesh_pos()
        for j, (px, py) in enumerate([(1 - x, y), (x, 1 - y), (1 - x, 1 - y)]):
            cp = pltpu.make_async_remote_copy(
                src_ref=src_ref.at[2 * px + py], dst_ref=land_ref.at[2 * px + py], send_sem=refs[j], recv_sem=refs[3 + j],
                device_id=(px, py, c), device_id_type=MESH)
            cp.wait_send()
            cp.wait_recv()

    return pl.pallas_call(
        body, name="chip_wait",
        out_shape=(pltpu.HBM(s_thru.shape, s_thru.dtype), pltpu.HBM(land_thru.shape, land_thru.dtype)),
        in_specs=(_HBM, _HBM) + (_SEM,) * 6 + (pl.BlockSpec(memory_space=pl.ANY),),
        out_specs=(_HBM, _HBM), input_output_aliases={0: 0, 1: 1},
        compiler_params=pltpu.CompilerParams(has_side_effects=_EFFECT),
    )(s_thru, land_thru, *sems, after)


def _pair_share(h_in, small):
    def body(hin_ref, sm_ref, rin_ref, slots_ref, send_sems, recv_sems, small_send, small_recv, local_sem):
        x, y, c = _mesh_pos()
        dev = 4 * x + 2 * y + c
        mine = pltpu.make_async_copy(sm_ref, slots_ref.at[dev], local_sem)
        mine.start()
        share = pltpu.make_async_remote_copy(
            src_ref=hin_ref, dst_ref=rin_ref, send_sem=send_sems.at[0], recv_sem=recv_sems.at[0],
            device_id=(x, y, 1 - c), device_id_type=MESH)
        share.start()
        started = []
        for k in range(1, 8):
            peer = (x ^ ((k >> 2) & 1), y ^ ((k >> 1) & 1), c ^ (k & 1))
            cp = pltpu.make_async_remote_copy(
                src_ref=sm_ref, dst_ref=slots_ref.at[dev], send_sem=small_send.at[k - 1], recv_sem=small_recv.at[k - 1],
                device_id=peer, device_id_type=MESH)
            cp.start()
            started.append(cp)
        share.wait()
        for k in range(1, 8):
            pltpu.make_async_remote_copy(
                src_ref=sm_ref, dst_ref=slots_ref.at[dev], send_sem=small_send.at[k - 1], recv_sem=small_recv.at[k - 1],
                device_id=(x, y, 1 - c), device_id_type=MESH).wait_recv()
        for cp in started:
            cp.wait_send()
        mine.wait()

    any_spec = pl.BlockSpec(memory_space=pl.ANY)
    return pl.pallas_call(
        body, name="pair_share",
        in_specs=[any_spec] * 2, out_specs=[any_spec] * 2,
        out_shape=[jax.ShapeDtypeStruct(h_in.shape, F32), jax.ShapeDtypeStruct((8,) + small.shape, F32)],
        scratch_shapes=[pltpu.SemaphoreType.DMA((1,)), pltpu.SemaphoreType.DMA((1,)),
                        pltpu.SemaphoreType.DMA((7,)), pltpu.SemaphoreType.DMA((7,)), pltpu.SemaphoreType.DMA],
    )(h_in, small)


def _reduce_w_out_start(slabs, after):
    def body(src_ref, land_ref, after_ref, *refs):
        x, y, c = _mesh_pos()
        me = 4 * x + 2 * y + c
        for k in range(1, 8):
            px, py, pc = x ^ ((k >> 2) & 1), y ^ ((k >> 1) & 1), c ^ (k & 1)
            pltpu.make_async_remote_copy(src_ref=src_ref.at[2 * px + py], dst_ref=land_ref.at[me], send_sem=refs[k - 1],
                                         recv_sem=refs[6 + k], device_id=(px, py, pc), device_id_type=MESH).start()
        refs[16][...] = jnp.zeros_like(refs[16])

    sem = pltpu.SemaphoreType.DMA(())
    land = lax.empty((8,) + slabs.shape[1:], slabs.dtype)
    return pl.pallas_call(
        body, name="reduce_w_out_start",
        out_shape=(sem,) * 14 + (pltpu.HBM(slabs.shape, slabs.dtype), pltpu.HBM(land.shape, land.dtype),
                                 jax.ShapeDtypeStruct((8, 128), F32)),
        in_specs=(_HBM, _HBM, pl.BlockSpec(memory_space=pl.ANY)),
        out_specs=(_SEM,) * 14 + (_HBM, _HBM, pl.BlockSpec(memory_space=pltpu.VMEM)),
        input_output_aliases={0: 14, 1: 15},
        compiler_params=pltpu.CompilerParams(has_side_effects=_EFFECT),
    )(pltpu.with_memory_space_constraint(slabs, pltpu.HBM), pltpu.with_memory_space_constraint(land, pltpu.HBM), after)


def _reduce_w_out_wait(sems, slabs_thru, land_thru, after):
    def body(src_ref, land_ref, *refs):
        x, y, c = _mesh_pos()
        for k in range(1, 8):
            px, py, pc = x ^ ((k >> 2) & 1), y ^ ((k >> 1) & 1), c ^ (k & 1)
            cp = pltpu.make_async_remote_copy(
                src_ref=src_ref.at[2 * px + py], dst_ref=land_ref.at[4 * px + 2 * py + pc], send_sem=refs[k - 1],
                recv_sem=refs[6 + k], device_id=(px, py, pc), device_id_type=MESH)
            cp.wait_send()
            cp.wait_recv()

    return pl.pallas_call(
        body, name="reduce_w_out_wait",
        out_shape=(pltpu.HBM(slabs_thru.shape, slabs_thru.dtype), pltpu.HBM(land_thru.shape, land_thru.dtype)),
        in_specs=(_HBM, _HBM) + (_SEM,) * 14 + (pl.BlockSpec(memory_space=pl.ANY),),
        out_specs=(_HBM, _HBM), input_output_aliases={0: 0, 1: 1},
        compiler_params=pltpu.CompilerParams(has_side_effects=_EFFECT),
    )(slabs_thru, land_thru, *sems, after)


def _pair_add(g, recv, core, name):
    _, rows, C = recv.shape
    tc = 256

    def body(core_ref, g_ref, r_ref, o_ref):
        o_ref[...] = _bf(g_ref[...] + r_ref[...])

    spec = pl.BlockSpec((1, rows, tc), lambda j, i, core: (j, 0, i))
    return pl.pallas_call(
        body, name=name,
        grid_spec=pltpu.PrefetchScalarGridSpec(
            num_scalar_prefetch=1, grid=(N_CHIPS, C // tc),
            in_specs=[pl.BlockSpec((1, rows, tc), lambda j, i, core: (j, core[0], i)), spec], out_specs=spec),
        out_shape=jax.ShapeDtypeStruct((N_CHIPS, rows, C), BF16),
        compiler_params=_cparams(("parallel", "parallel")),
    )(core, g, recv)


def _chip_add(own, parts, chip, name):
    _, rows, C = parts.shape
    tc = 256

    def body(chip_ref, own_ref, r0, r1, r2, r3, o_ref):
        acc = None
        for j, r in enumerate((r0, r1, r2, r3)):
            term = jnp.where(chip_ref[0] == j, own_ref[0], r[0]).astype(F32)
            acc = term if acc is None else acc + term
        o_ref[...] = acc

    def slab(j):
        return pl.BlockSpec((1, rows, tc), lambda i, chip: (jnp.where(chip[0] == j, (j + 1) % N_CHIPS, j), 0, i))

    return pl.pallas_call(
        body, name=name,
        grid_spec=pltpu.PrefetchScalarGridSpec(
            num_scalar_prefetch=1, grid=(C // tc,),
            in_specs=[pl.BlockSpec((1, rows, tc), lambda i, chip: (chip[0], 0, i))] + [slab(j) for j in range(N_CHIPS)],
            out_specs=pl.BlockSpec((rows, tc), lambda i, chip: (0, i))),
        out_shape=jax.ShapeDtypeStruct((rows, C), F32),
        compiler_params=_cparams(("parallel",)),
    )(chip, own, parts, parts, parts, parts)


def _adamw_math(w, g, m, v):
    m = ADAM_B1 * m + (1.0 - ADAM_B1) * g
    v = ADAM_B2 * v + (1.0 - ADAM_B2) * (g * g)
    m_hat = m / (1.0 - ADAM_B1 ** ADAM_STEP)
    v_hat = v / (1.0 - ADAM_B2 ** ADAM_STEP)
    delta = -ADAM_LR * (m_hat / (jnp.sqrt(v_hat) + ADAM_EPS) + ADAM_WD * w)
    return delta, m, v


def _adamw_rows(w, g_own, g_sib, m, v, core, name):
    R, C = w.shape[0], w.shape[-1]
    rows = g_own.shape[0]
    step = 256
    chunks = [(r, min(step, R - r)) for r in range(0, R, step)]
    sub = 64

    def body(core_ref, w_hbm, go_hbm, gs_hbm, m_hbm, v_hbm, d_hbm, nm_hbm, nv_hbm, g_hbm,
             wbuf, mbuf, vbuf, gbuf, dbuf, nmbuf, nvbuf, in_sems, g_sems, out_sems):
        c = core_ref[0]
        flat = lambda ref: ref.at[:, 0, :]
        g_in = [pltpu.make_async_copy(go_hbm, gbuf.at[pl.ds(pl.multiple_of(c * rows, 8), rows), :], g_sems.at[0]),
                pltpu.make_async_copy(gs_hbm, gbuf.at[pl.ds(pl.multiple_of((1 - c) * rows, 8), rows), :], g_sems.at[1])]
        for cp in g_in:
            cp.start()
        loads = []
        for k, (r0, n) in enumerate(chunks):
            cps = [pltpu.make_async_copy(flat(src).at[pl.ds(r0, n), :], dst.at[pl.ds(r0, n), :], in_sems.at[a, k])
                   for a, (src, dst) in enumerate(((w_hbm, wbuf), (m_hbm, mbuf), (v_hbm, vbuf)))]
            for cp in cps:
                cp.start()
            loads.append(cps)
        for cp in g_in:
            cp.wait()
        stores = []
        for k, (r0, n) in enumerate(chunks):
            for cp in loads[k]:
                cp.wait()

            def update(rs):
                g = gbuf[rs, :]
                dl, nm, nv = _adamw_math(wbuf[rs, :], g, mbuf[rs, :], vbuf[rs, :])
                dbuf[rs, :] = dl
                nmbuf[rs, :] = nm
                nvbuf[rs, :] = nv

            if n % sub == 0:
                def block(i, carry, r0=r0):
                    update(pl.ds(pl.multiple_of(r0 + i * sub, 8), sub))
                    return carry
                lax.fori_loop(0, n // sub, block, 0)
            else:
                update(pl.ds(r0, n))
            cps = [pltpu.make_async_copy(src.at[pl.ds(r0, n), :], flat(dst).at[pl.ds(r0, n), :], out_sems.at[a, k])
                   for a, (src, dst) in enumerate(((dbuf, d_hbm), (nmbuf, nm_hbm), (nvbuf, nv_hbm), (gbuf, g_hbm)))]
            for cp in cps:
                cp.start()
            stores += cps
        for cp in stores:
            cp.wait()

    any_spec = pl.BlockSpec(memory_space=pl.ANY)
    dense = pltpu.VMEM((R, C), F32)
    return pl.pallas_call(
        body, name=name,
        grid_spec=pltpu.PrefetchScalarGridSpec(
            num_scalar_prefetch=1, grid=(1,),
            in_specs=[any_spec] * 5, out_specs=[any_spec] * 4,
            scratch_shapes=[dense, dense, dense, pltpu.VMEM((2 * rows, C), F32), dense, dense, dense,
                            pltpu.SemaphoreType.DMA((3, len(chunks))), pltpu.SemaphoreType.DMA((2,)),
                            pltpu.SemaphoreType.DMA((4, len(chunks)))]),
        out_shape=[jax.ShapeDtypeStruct(w.shape, F32)] * 4,
        compiler_params=_cparams(),
    )(core, w, g_own, g_sib, m, v)


def _adamw_sum8(w, slabs, land, m, v, ids, name):
    R, C = w.shape
    tc = 128

    def body(ids_ref, w_ref, own_ref, *refs):
        lrefs, (m_ref, v_ref, d_ref, nm_ref, nv_ref, g_ref) = refs[:8], refs[8:]
        g = None
        for d, l_ref in enumerate(lrefs):
            term = jnp.where(ids_ref[0] == d, own_ref[0], l_ref[0]).astype(F32)
            g = term if g is None else g + term
        dl, nm, nv = _adamw_math(w_ref[...], g, m_ref[...], v_ref[...])
        d_ref[...] = dl
        nm_ref[...] = nm
        nv_ref[...] = nv
        g_ref[...] = g

    def slot(d):
        return pl.BlockSpec((1, R, tc), lambda i, ids: (jnp.where(ids[0] == d, (d + 1) % 8, d), 0, i))

    spec = pl.BlockSpec((R, tc), lambda i, ids: (0, i))
    return pl.pallas_call(
        body, name=name,
        grid_spec=pltpu.PrefetchScalarGridSpec(
            num_scalar_prefetch=1, grid=(C // tc,),
            in_specs=[spec, pl.BlockSpec((1, R, tc), lambda i, ids: (ids[1], 0, i))] + [slot(d) for d in range(8)]
            + [spec, spec],
            out_specs=[spec] * 4),
        out_shape=[jax.ShapeDtypeStruct((R, C), F32)] * 4,
        compiler_params=_cparams(("parallel",)),
    )(ids, w, slabs, *([land] * 8), m, v)


SMALL_NAMES = ("conv_b", "ssd_norm_w", "ln_g", "ln_b", "dt_bias", "a_log", "d_skip", "attn_sinks")
SMALL_FIELDS = ((4, 0, D_XBC), (5, 0, D_SSD), (6, 0, D_MODEL), (7, 0, D_MODEL), (5, 1024, SSD_HEADS), (5, 1152, SSD_HEADS),
                (5, 1280, SSD_HEADS), (5, 1408, ATT_QH))
LOSS_FIELD = (6, 1024, 128)
K_SMALL = D_XBC


def _pack_small(g_conv_w, vecs, loss):
    def body(cw_ref, *refs):
        o_ref = refs[-1]
        o_ref[...] = jnp.zeros_like(o_ref)
        o_ref[0:CONV_K, 0:D_XBC] = cw_ref[...]
        for v_ref, (row, off, n) in zip(refs[:-2], SMALL_FIELDS):
            o_ref[row:row + 1, off:off + n] = v_ref[...]
        o_ref[LOSS_FIELD[0]:LOSS_FIELD[0] + 1, LOSS_FIELD[1]:LOSS_FIELD[1] + LOSS_FIELD[2]] = refs[-2][...]

    return pl.pallas_call(
        body, name="pack_small", out_shape=jax.ShapeDtypeStruct((8, K_SMALL), F32), compiler_params=_cparams(),
    )(g_conv_w, *vecs, loss)


def _adamw_small(slots, chip, conv_w, m_conv_w, v_conv_w, params, moms, vars_):
    n_vec = len(SMALL_NAMES)

    def body(chip_ref, s_ref, *refs):
        ins = refs[:3 * (n_vec + 1)]
        outs = refs[3 * (n_vec + 1):-1]
        tot_ref = refs[-1]
        tot = s_ref[0]
        for d in range(1, 8):
            tot = tot + s_ref[d]
        outs[0][...] = tot[LOSS_FIELD[0]:LOSS_FIELD[0] + 1, LOSS_FIELD[1]:LOSS_FIELD[1] + 1]
        off = pl.multiple_of(chip_ref[0] * CONV_COLS, 128)
        tot_ref[...] = tot
        grads = [tot_ref[0:CONV_K, pl.ds(off, CONV_COLS)]]
        grads += [tot[row:row + 1, o:o + n] for row, o, n in SMALL_FIELDS]
        for k, g in enumerate(grads):
            w_ref, m_ref, v_ref = ins[3 * k:3 * k + 3]
            full = (0,) if k == 0 else (Ellipsis,)
            d, nm, nv = _adamw_math(w_ref[full], g, m_ref[full], v_ref[full])
            for o_ref, val in zip(outs[1 + 4 * k:5 + 4 * k], (g, d, nm, nv)):
                o_ref[full] = val

    args = [conv_w, m_conv_w, v_conv_w]
    for w, m, v in zip(params, moms, vars_):
        args += [w, m, v]
    shapes = [jax.ShapeDtypeStruct((1, 1), F32)] + [jax.ShapeDtypeStruct(conv_w.shape, F32)] * 4
    for w in params:
        shapes += [jax.ShapeDtypeStruct(w.shape, F32)] * 4
    vmem = pl.BlockSpec(memory_space=pltpu.VMEM)
    return pl.pallas_call(
        body, name="adamw_small",
        grid_spec=pltpu.PrefetchScalarGridSpec(
            num_scalar_prefetch=1, grid=(1,),
            in_specs=[pl.BlockSpec(slots.shape, lambda i, chip: (0, 0, 0))] + [vmem] * len(args),
            out_specs=[vmem] * len(shapes), scratch_shapes=[pltpu.VMEM((8, K_SMALL), F32)]),
        out_shape=shapes, compiler_params=_cparams(),
    )(chip, slots, *args)


def kernel(x, positions, w_in, conv_w, conv_b, dt_bias, a_log, d_skip, ssd_norm_w, attn_sinks, w_out, ln_g, ln_b, loss_target, m_w_in, m_conv_w, m_conv_b, m_dt_bias, m_a_log, m_d_skip, m_ssd_norm_w, m_attn_sinks, m_w_out, m_ln_g, m_ln_b, v_w_in, v_conv_w, v_conv_b, v_dt_bias, v_a_log, v_d_skip, v_ssd_norm_w, v_attn_sinks, v_w_out, v_ln_g, v_ln_b):
    mx, my, mc = _mesh_pos()
    chip = 2 * mx + my
    L = x.shape[1]

    conv_w_s8 = jnp.pad(conv_w[0], ((0, 8 - CONV_K), (0, 0)))
    pad_rows = ((0, SLAB_ROWS - W_IN_COLS), (0, 0))
    w_in_t = w_in[0].T
    w_in_b, w_out_b = jnp.pad(_bf(w_in_t), pad_rows), _bf(w_out[0])
    ag_in, ag_cw = _gather_weights(w_in_b, conv_w_s8)
    started = _gather_w_out_start(w_out_b, ag_cw)
    own = (jnp.arange(N_CHIPS) == chip)[:, None, None]

    def get_w_out(after):
        landed = _gather_w_out_wait(started[0:6], started[6], started[7], after)
        return jnp.where(own, w_out_b[None], landed).reshape(D_MIX, D_MODEL)

    w_full = jnp.concatenate([ag_in[j, 0:W_IN_COLS] for j in range(N_CHIPS)], axis=0)
    w = jnp.concatenate([
        w_full[O_Z:O_Z + D_SSD], w_full[O_G:O_G + D_ATT], w_full[O_Q:O_Q + D_ATT],
        w_full[O_XBC:O_XBC + D_XBC], w_full[O_K:O_K + 2 * D_KV], w_full[O_DT:O_DT + SSD_HEADS],
        jnp.zeros((DT_PAD - SSD_HEADS, D_MODEL), BF16)], axis=0)
    conv_w_full = jnp.concatenate([ag_cw[j, 0:CONV_K] for j in range(N_CHIPS)], axis=1)

    loss_part, gx_args, gw_in, w_out_red, small = _local_step(
        x[0], positions[0].reshape(L, 1), loss_target[0], w, get_w_out, started[8][0:1, :], conv_w_full,
        conv_b, dt_bias, a_log, d_skip, ssd_norm_w, attn_sinks, ln_g, ln_b)

    packed = _pack_small(small["conv_w"], [small[n] for n in SMALL_NAMES], loss_part)
    core_id = mc.reshape(1).astype(jnp.int32)
    chip_id = chip.reshape(1).astype(jnp.int32)
    ids = jnp.stack([4 * mx + 2 * my + mc, chip]).astype(jnp.int32)
    slabs = jnp.stack([jnp.pad(gw_in[W_IN_COLS * j:W_IN_COLS * (j + 1)], pad_rows) for j in range(N_CHIPS)])
    w_in_red = _pair_start(slabs, packed)
    grad_x = _grad_x(*gx_args, w_in_red[10], 0)
    gw_in_slabs, recv_in = _pair_wait(w_in_red[0:8], w_in_red[8], w_in_red[9], grad_x[0:8, 0:128])
    s_in = _pair_add(gw_in_slabs, recv_in, core_id, "pair_add_in")
    chip_red = _chip_start(s_in, packed)
    grad_x = _grad_x(*gx_args, chip_red[8], 1, grad_x)
    own_slabs, landed = _reduce_w_out_wait(w_out_red[0:14], w_out_red[14], w_out_red[15], grad_x)
    out_t = _adamw_sum8(w_out[0], own_slabs, landed, m_w_out[0], v_w_out[0], ids, "adamw_w_out")
    d_w_out, nm_w_out, nv_w_out, g_w_out = [a[None] for a in out_t]
    s_in, r_in = _chip_wait(chip_red[0:6], chip_red[6], chip_red[7], out_t[0])
    h_in = _chip_add(s_in, r_in, chip_id, "chip_add_in")
    sib_in, slots = _pair_share(h_in, packed)

    to_rows = lambda a: jnp.transpose(a, (2, 0, 1))
    in_t = _adamw_rows(to_rows(w_in), h_in, sib_in, to_rows(m_w_in), to_rows(v_w_in), core_id, "adamw_w_in")
    d_w_in, nm_w_in, nv_w_in, g_w_in = [jnp.transpose(a, (1, 2, 0)) for a in in_t]

    params = dict(conv_b=conv_b, ssd_norm_w=ssd_norm_w, ln_g=ln_g, ln_b=ln_b, dt_bias=dt_bias, a_log=a_log,
                  d_skip=d_skip, attn_sinks=attn_sinks)
    moms = dict(conv_b=m_conv_b, ssd_norm_w=m_ssd_norm_w, ln_g=m_ln_g, ln_b=m_ln_b, dt_bias=m_dt_bias, a_log=m_a_log,
                d_skip=m_d_skip, attn_sinks=m_attn_sinks)
    vars_ = dict(conv_b=v_conv_b, ssd_norm_w=v_ssd_norm_w, ln_g=v_ln_g, ln_b=v_ln_b, dt_bias=v_dt_bias, a_log=v_a_log,
                 d_skip=v_d_skip, attn_sinks=v_attn_sinks)
    res = _adamw_small(slots, chip_id, conv_w, m_conv_w, v_conv_w, [params[n] for n in SMALL_NAMES],
                       [moms[n] for n in SMALL_NAMES], [vars_[n] for n in SMALL_NAMES])
    loss = res[0][0, 0]
    grads, delta, new_m, new_v = {}, {}, {}, {}
    for k, n in enumerate(("conv_w",) + SMALL_NAMES):
        grads[n], delta[n], new_m[n], new_v[n] = res[1 + 4 * k:5 + 4 * k]
    for dd, a_in, a_out in ((grads, g_w_in, g_w_out), (delta, d_w_in, d_w_out), (new_m, nm_w_in, nm_w_out),
                            (new_v, nv_w_in, nv_w_out)):
        dd["w_in"] = a_in
        dd["w_out"] = a_out
    order = ("w_in", "conv_w", "conv_b", "dt_bias", "a_log", "d_skip", "ssd_norm_w", "attn_sinks", "w_out", "ln_g", "ln_b")
    return (loss, grad_x[None], *[grads[n] for n in order], *[delta[n] for n in order], *[new_m[n] for n in order],
            *[new_v[n] for n in order])
```

```python
import numpy as np
import jax
import jax.numpy as jnp
from jax import lax
from jax.experimental import pallas as pl
from jax.experimental.pallas import tpu as pltpu

F32 = jnp.float32
BF16 = jnp.bfloat16
MESH = pl.DeviceIdType.MESH

D_MODEL = 1024
D_SSD = 1024
D_ATT = 1024
D_MIX = 2048
SSD_HEADS = 16
SSD_P = 64
SSD_GROUPS = 2
SSD_R = 8
SSD_N = 128
D_BC = 256
D_XBC = 1536
CONV_K = 4
CHUNK = 128
ATT_HD = 64
ATT_QH = 16
ATT_KVH = 4
ATT_R = 4
D_KV = 256
WINDOW = 128
ROPE_THETA = 500000.0
ROPE_DIM = 16
ALPHA = 2.0 ** 0.25
LN_EPS = 1e-5
RMS_EPS = 1e-5
D_IN_PROJ = 5136
O_Z, O_XBC, O_DT, O_Q, O_K, O_V, O_G = 0, 1024, 2560, 2576, 3600, 3856, 4112
P_Z, P_G, P_Q, P_XBC, P_KV, P_DT, P_END = 0, 1024, 2048, 3072, 4608, 5120, 5248
DT_PAD = 128
N_CHIPS = 4
W_IN_COLS = D_IN_PROJ // N_CHIPS
SLAB_ROWS = 1312
W_OUT_ROWS = D_MIX // N_CHIPS
CONV_COLS = D_XBC // N_CHIPS

ADAM_LR = 0.001
ADAM_B1 = 0.9
ADAM_B2 = 0.999
ADAM_EPS = 1e-08
ADAM_WD = 0.01
ADAM_STEP = 10

VMEM_LIMIT = 56 * 1024 * 1024
ROW_TILE = 512
NEG_BIG = -1e30
HI = lax.Precision.HIGHEST


def _cparams(sem=None, **kw):
    if sem is not None:
        kw["dimension_semantics"] = sem
    return pltpu.CompilerParams(vmem_limit_bytes=VMEM_LIMIT, **kw)


def _dot(a, b):
    return jnp.dot(a, b, preferred_element_type=F32)


def _dot_nt(a, b):
    return lax.dot_general(a, b, (((1,), (1,)), ((), ())), preferred_element_type=F32)


def _dot_tn(a, b):
    return lax.dot_general(a, b, (((0,), (0,)), ((), ())), preferred_element_type=F32)


def _bf(a):
    return a.astype(BF16)


def _iota2(shape, dim):
    return lax.broadcasted_iota(jnp.int32, shape, dim)


def _to_rows(col):
    k = col.shape[1]
    eye = (_iota2((k, k), 0) == _iota2((k, k), 1)).astype(F32)
    return lax.dot_general(eye, col, (((1,), (1,)), ((), ())), preferred_element_type=F32, precision=HI)


def _to_cols(row):
    n = row.shape[1]
    eye = (_iota2((n, n), 0) == _iota2((n, n), 1)).astype(F32)
    return lax.dot_general(eye, row, (((1,), (1,)), ((), ())), preferred_element_type=F32, precision=HI)


def _sigmoid(x):
    return jax.nn.sigmoid(x)


def _in_proj(x, w, pos, inv):
    L = x.shape[0]
    tm = ROW_TILE
    widths = (D_SSD, D_ATT, D_ATT, D_XBC, 2 * D_KV, DT_PAD)

    def body(x_ref, w_ref, pos_ref, inv_ref, z_ref, g_ref, q_ref, xbc_ref, kv_ref, dt_ref, xb_ref):
        xb = _bf(x_ref[...])
        xb_ref[...] = xb
        tabs = _rope_tables(pos_ref, inv_ref)
        q_ref[...] = _bf(_rope(_dot_nt(xb, w_ref[P_Q:P_Q + D_ATT, :]), tabs))
        kv_ref[:, 0:D_KV] = _bf(_rope(_dot_nt(xb, w_ref[P_KV:P_KV + D_KV, :]), tabs))
        kv_ref[:, D_KV:2 * D_KV] = _bf(_dot_nt(xb, w_ref[P_KV + D_KV:P_KV + 2 * D_KV, :]))
        for o_ref, off, wd in zip((z_ref, g_ref, xbc_ref, dt_ref), (P_Z, P_G, P_XBC, P_DT), (D_SSD, D_ATT, D_XBC, DT_PAD)):
            o_ref[...] = _dot_nt(xb, w_ref[off:off + wd, :])

    row = lambda wd: pl.BlockSpec((tm, wd), lambda i: (i, 0))
    return pl.pallas_call(
        body, name="in_proj", grid=(L // tm,),
        in_specs=[row(D_MODEL), pl.BlockSpec((P_END, D_MODEL), lambda i: (0, 0), pipeline_mode=pl.Buffered(1)), row(1),
                  pl.BlockSpec((1, 2 * ATT_HD), lambda i: (0, 0))],
        out_specs=[row(wd) for wd in widths] + [row(D_MODEL)],
        out_shape=[jax.ShapeDtypeStruct((L, wd), dt) for wd, dt in zip(widths, (F32, F32, BF16, F32, BF16, F32))]
        + [jax.ShapeDtypeStruct((L, D_MODEL), BF16)],
        compiler_params=_cparams(("parallel",)),
    )(x, w, pos, inv)


def _matmuls_tn(a_list, b, name):
    K, N = b.shape
    tk = min(K, 1024)
    n = len(a_list)

    def body(*refs):
        b_ref = refs[n]

        @pl.when(pl.program_id(0) == 0)
        def _():
            for o_ref in refs[n + 1:]:
                o_ref[...] = jnp.zeros_like(o_ref)

        bb = _bf(b_ref[...])
        for a_ref, o_ref in zip(refs[:n], refs[n + 1:]):
            o_ref[...] += _dot_tn(_bf(a_ref[...]), bb)

    return pl.pallas_call(
        body, name=name, grid=(K // tk,),
        in_specs=[pl.BlockSpec((tk, a.shape[1]), lambda k: (k, 0)) for a in a_list] + [pl.BlockSpec((tk, N), lambda k: (k, 0))],
        out_specs=[pl.BlockSpec((a.shape[1], N), lambda k: (0, 0)) for a in a_list],
        out_shape=[jax.ShapeDtypeStruct((a.shape[1], N), F32) for a in a_list],
        compiler_params=_cparams(("arbitrary",)),
    )(*a_list, b)


def _grad_x(dr, dz, dg, dq, dxbc, dkv, ddt, w, after, part, prev=None):
    L = dr.shape[0]
    tm = min(ROW_TILE, L // 4)
    first = L // (4 * tm)
    n = first if part == 0 else L // tm - first
    widths = (D_SSD, D_ATT, D_ATT, D_XBC, 2 * D_KV, DT_PAD)
    offs = (P_Z, P_G, P_Q, P_XBC, P_KV, P_DT)

    def body(dr_ref, dz_ref, dg_ref, dq_ref, dxbc_ref, dkv_ref, ddt_ref, w_ref, after_ref, *rest):
        o_ref = rest[-1]
        acc = ALPHA * dr_ref[...]
        for p_ref, off, wd in zip((dz_ref, dg_ref, dq_ref, dxbc_ref, dkv_ref, ddt_ref), offs, widths):
            acc = acc + _dot(_bf(p_ref[...]), w_ref[off:off + wd, :])
        o_ref[...] = acc

    row = lambda wd: pl.BlockSpec((tm, wd), lambda i: (i + part * first, 0))
    ins = [dr, dz, dg, dq, dxbc, dkv, ddt, w, after]
    specs = ([row(D_MODEL)] + [row(wd) for wd in widths]
             + [pl.BlockSpec((P_END, D_MODEL), lambda i: (0, 0), pipeline_mode=pl.Buffered(1)),
                pl.BlockSpec((8, 128), lambda i: (0, 0))])
    if prev is not None:
        ins.append(prev)
        specs.append(pl.BlockSpec(memory_space=pl.ANY))
    return pl.pallas_call(
        body, name="grad_x_%d" % part, grid=(n,),
        in_specs=specs, out_specs=row(D_MODEL),
        out_shape=jax.ShapeDtypeStruct((L, D_MODEL), F32),
        input_output_aliases={} if prev is None else {len(ins) - 1: 0},
        compiler_params=_cparams(("parallel",)),
    )(*ins)


HALO = 16


def _shift_matrix(offsets):
    n = CHUNK + HALO
    m = np.zeros((len(offsets) * CHUNK, 2 * n), np.float32)
    for k, off in enumerate(offsets):
        t = np.arange(CHUNK)
        m[k * CHUNK + t, t + off] = 1.0
        m[k * CHUNK + t, n + t + off] = 1.0
    return jnp.asarray(m, BF16)


def _shifted_rows(first_part, second_part, smat_ref):
    h1, l1 = _hi_lo(first_part)
    h2, l2 = _hi_lo(second_part)
    sh = _dot(smat_ref[...], jnp.concatenate([h1, h2, l1, l2], axis=0))
    return sh[0:CHUNK], sh[CHUNK:2 * CHUNK], sh[2 * CHUNK:3 * CHUNK]


def _ssd_chunk_pre(first, xbc_ref, tail_ref, dt_ref, cw_ref, cb_ref, dtb_ref, alog_ref, smat_ref):
    tail = jnp.where(first, 0.0, tail_ref[...])
    x = xbc_ref[...]
    taps = _shifted_rows(tail, x, smat_ref) + (x,)
    u = cb_ref[...] + cw_ref[0:1, :] * taps[0]
    for k in range(1, CONV_K):
        u = u + cw_ref[k:k + 1, :] * taps[k]
    sig = _sigmoid(u)
    xbc = u * sig
    dtraw = dt_ref[:, 0:SSD_HEADS] + dtb_ref[...]
    dt = jax.nn.softplus(dtraw)
    A = -jnp.exp(alog_ref[...])
    a = dt * A
    tril = (_iota2((CHUNK, CHUNK), 0) >= _iota2((CHUNK, CHUNK), 1)).astype(F32)
    acs = jnp.dot(tril, a, preferred_element_type=F32, precision=HI)
    acs_row = _to_rows(acs)
    return u, sig, xbc, dtraw, dt, A, acs, acs_row, taps


def _head_expander():
    return (_iota2((SSD_HEADS, D_SSD), 1) // SSD_P == _iota2((SSD_HEADS, D_SSD), 0)).astype(BF16)


def _hi_lo(x):
    hi = _bf(x)
    return hi, _bf(x - hi.astype(F32))


def _expand(v, e):
    hi, lo = _hi_lo(v)
    return _dot(hi, e) + _dot(lo, e)


def _headsum(t, e):
    m = t.shape[0]
    if m < 8:
        t = jnp.broadcast_to(t[0:1], (8, t.shape[1]))
    hi, lo = _hi_lo(t)
    return (_dot_nt(hi, e) + _dot_nt(lo, e))[0:m]


def _ssd_decays(dt, acs, dsk_ref, e):
    alast = acs[CHUNK - 1:CHUNK, :]
    stk = jnp.concatenate([dt, jnp.exp(acs), jnp.exp(alast - acs),
                           jnp.broadcast_to(jnp.exp(alast), (8, SSD_HEADS)),
                           jnp.broadcast_to(dsk_ref[...], (8, SSD_HEADS))], axis=0)
    ex = _expand(stk, e)
    return (ex[0:CHUNK], ex[CHUNK:2 * CHUNK], ex[2 * CHUNK:3 * CHUNK], ex[3 * CHUNK:3 * CHUNK + 1],
            ex[3 * CHUNK + 8:3 * CHUNK + 9])


def _ssd_fwd(z, xbc, dtp, conv_w, conv_b, dt_bias, a_log, d_skip, norm_w):
    L = z.shape[0]
    nc = L // CHUNK
    half = D_SSD // SSD_GROUPS

    def body(z_ref, xbc_ref, tail_ref, dt_ref, cw_ref, cb_ref, dtb_ref, alog_ref, dsk_ref, nw_ref, smat_ref,
             y_ref, ypre_ref, prev_ref, state, ybuf, mbuf):
        c = pl.program_id(0)

        @pl.when(c == 0)
        def _():
            state[...] = jnp.zeros_like(state)

        u, sig, xbcv, dtraw, dt, A, acs, acs_row, _ = _ssd_chunk_pre(
            c == 0, xbc_ref, tail_ref, dt_ref, cw_ref, cb_ref, dtb_ref, alog_ref, smat_ref)
        e = _head_expander()
        dtE, eacsE, dsdE, ealE, dskE = _ssd_decays(dt, acs, dsk_ref, e)
        xs = xbcv[:, 0:D_SSD]
        X = xs * dtE
        prev_ref[0] = state[...]
        causal = _iota2((CHUNK, CHUNK), 0) >= _iota2((CHUNK, CHUNK), 1)
        for g in range(SSD_GROUPS):
            gs = slice(half * g, half * (g + 1))
            Bg = _bf(xbcv[:, D_SSD + SSD_N * g:D_SSD + SSD_N * (g + 1)])
            Cg = _bf(xbcv[:, D_SSD + D_BC + SSD_N * g:D_SSD + D_BC + SSD_N * (g + 1)])
            cb = _dot_nt(Cg, Bg)
            for r in range(SSD_R):
                h = g * SSD_R + r
                seg = acs[:, h:h + 1] - acs_row[h:h + 1, :]
                mbuf[h] = _bf(cb * jnp.where(causal, jnp.exp(jnp.where(causal, seg, 0.0)), 0.0))
            st = state[:, gs]
            ybuf[:, gs] = _dot(Cg, _bf(st)) * eacsE[:, gs] + dskE[:, gs] * xs[:, gs]
            state[:, gs] = st * ealE[:, gs] + _dot_tn(Bg, _bf(X[:, gs] * dsdE[:, gs]))
        Xb = _bf(X)
        for h in range(SSD_HEADS):
            hs = slice(SSD_P * h, SSD_P * (h + 1))
            ybuf[:, hs] += _dot(mbuf[h], Xb[:, hs])
        y = ybuf[...]
        ypre_ref[...] = y
        zv = z_ref[...]
        yf = y * (zv * _sigmoid(zv))
        for g in range(SSD_GROUPS):
            gs = slice(half * g, half * (g + 1))
            yg = yf[:, gs]
            ms = jnp.mean(yg * yg, axis=-1, keepdims=True)
            y_ref[:, gs] = _bf(yg * lax.rsqrt(ms + RMS_EPS) * nw_ref[:, gs])

    full = lambda shape: pl.BlockSpec(shape, lambda c: (0, 0))
    return pl.pallas_call(
        body, name="ssd_fwd", grid=(nc,),
        in_specs=[
            pl.BlockSpec((CHUNK, D_SSD), lambda c: (c, 0)),
            pl.BlockSpec((CHUNK, D_XBC), lambda c: (c, 0)),
            pl.BlockSpec((HALO, D_XBC), lambda c: (jnp.maximum(c * (CHUNK // HALO) - 1, 0), 0)),
            pl.BlockSpec((CHUNK, DT_PAD), lambda c: (c, 0)),
            full((CONV_K, D_XBC)), full((1, D_XBC)), full((1, SSD_HEADS)), full((1, SSD_HEADS)), full((1, SSD_HEADS)),
            full((1, D_SSD)), full((3 * CHUNK, 2 * (CHUNK + HALO))),
        ],
        out_specs=[
            pl.BlockSpec((CHUNK, D_SSD), lambda c: (c, 0)),
            pl.BlockSpec((CHUNK, D_SSD), lambda c: (c, 0)),
            pl.BlockSpec((1, SSD_N, D_SSD), lambda c: (c, 0, 0)),
        ],
        out_shape=[
            jax.ShapeDtypeStruct((L, D_SSD), BF16),
            jax.ShapeDtypeStruct((L, D_SSD), F32),
            jax.ShapeDtypeStruct((nc, SSD_N, D_SSD), F32),
        ],
        scratch_shapes=[
            pltpu.VMEM((SSD_N, D_SSD), F32),
            pltpu.VMEM((CHUNK, D_SSD), F32),
            pltpu.VMEM((SSD_HEADS, CHUNK, CHUNK), BF16),
        ],
        compiler_params=_cparams(("arbitrary",)),
    )(z, xbc, xbc, dtp, conv_w, conv_b, dt_bias, a_log, d_skip, norm_w, _shift_matrix((13, 14, 15)))


def _ssd_bwd(dy, z, ypre, xbc, dtp, prev, conv_w, conv_b, dt_bias, a_log, d_skip, norm_w):
    L = z.shape[0]
    nc = L // CHUNK
    half = D_SSD // SSD_GROUPS

    def body(dy_ref, z_ref, ypre_ref, xbc_ref, tail_ref, dt_ref, prev_ref, cw_ref, cb_ref, dtb_ref, alog_ref, dsk_ref,
             nw_ref, smat_ref, smat2_ref, dz_ref, dxbc_ref, ddt_ref, gcw_ref, gcb_ref, gdtb_ref, galog_ref, gdsk_ref,
             gnw_ref, dstate, dhead, dpost, yobuf, bdbuf, lmbuf, dmbuf, cbbuf):
        i = pl.program_id(0)
        c = nc - 1 - i

        @pl.when(i == 0)
        def _():
            dstate[...] = jnp.zeros_like(dstate)
            dhead[...] = jnp.zeros_like(dhead)
            gcw_ref[...] = jnp.zeros_like(gcw_ref)
            gcb_ref[...] = jnp.zeros_like(gcb_ref)
            gdtb_ref[...] = jnp.zeros_like(gdtb_ref)
            galog_ref[...] = jnp.zeros_like(galog_ref)
            gdsk_ref[...] = jnp.zeros_like(gdsk_ref)
            gnw_ref[...] = jnp.zeros_like(gnw_ref)

        u, sig, xbcv, dtraw, dt, A, acs, acs_row, taps = _ssd_chunk_pre(
            c == 0, xbc_ref, tail_ref, dt_ref, cw_ref, cb_ref, dtb_ref, alog_ref, smat_ref)
        e = _head_expander()
        dtE, eacsE, dsdE, ealE, dskE = _ssd_decays(dt, acs, dsk_ref, e)
        alast = acs[CHUNK - 1:CHUNK, :]
        xs = xbcv[:, 0:D_SSD]
        X = xs * dtE
        Xb = _bf(X)

        zv = z_ref[...]
        ypre = ypre_ref[...]
        dyn = dy_ref[...]
        sz = _sigmoid(zv)
        silu_z = zv * sz
        yf = ypre * silu_z
        dyf_parts = []
        for g in range(SSD_GROUPS):
            gs = slice(half * g, half * (g + 1))
            yg = yf[:, gs]
            rstd = lax.rsqrt(jnp.mean(yg * yg, axis=-1, keepdims=True) + RMS_EPS)
            dout = dyn[:, gs]
            gnw_ref[:, gs] += jnp.sum(dout * yg * rstd, axis=0, keepdims=True)
            dyhat = dout * nw_ref[:, gs]
            dyf_parts.append(rstd * (dyhat - yg * (rstd * rstd) * jnp.mean(dyhat * yg, axis=-1, keepdims=True)))
        dyf = jnp.concatenate(dyf_parts, axis=1)
        dz_ref[...] = _bf(dyf * ypre * (sz * (1.0 + zv * (1.0 - sz))))
        dyp = dyf * silu_z
        dyb = _bf(dyp)
        G = dyp * eacsE

        causal = _iota2((CHUNK, CHUNK), 0) >= _iota2((CHUNK, CHUNK), 1)
        ST = prev_ref[0]
        dST = dstate[...]
        for g in range(SSD_GROUPS):
            gs = slice(half * g, half * (g + 1))
            bs = slice(D_SSD + SSD_N * g, D_SSD + SSD_N * (g + 1))
            cs = slice(D_SSD + D_BC + SSD_N * g, D_SSD + D_BC + SSD_N * (g + 1))
            Bg = _bf(xbcv[:, bs])
            Cg = _bf(xbcv[:, cs])
            Gb = _bf(G[:, gs])
            STb = _bf(ST[:, gs])
            dSTb = _bf(dST[:, gs])
            dstate[:, gs] = dST[:, gs] * ealE[:, gs] + _dot_tn(Cg, Gb)
            yobuf[:, gs] = _dot(Cg, STb) * eacsE[:, gs]
            bdbuf[:, gs] = _dot(Bg, dSTb)
            dpost[:, cs] = _dot_nt(Gb, STb)
            dpost[:, bs] = _dot_nt(_bf(X[:, gs] * dsdE[:, gs]), dSTb)
            cbbuf[g] = _dot_nt(Cg, Bg)
            for r in range(SSD_R):
                h = g * SSD_R + r
                seg = acs[:, h:h + 1] - acs_row[h:h + 1, :]
                lmbuf[h] = jnp.where(causal, jnp.exp(jnp.where(causal, seg, 0.0)), 0.0)
        for h in range(SSD_HEADS):
            hs = slice(SSD_P * h, SSD_P * (h + 1))
            Mb = _bf(cbbuf[h // SSD_R] * lmbuf[h])
            dmbuf[h] = _dot_nt(dyb[:, hs], Xb[:, hs])
            dpost[:, hs] = _dot_tn(Mb, dyb[:, hs])
        lane16 = _iota2((1, SSD_HEADS), 1)
        sub16 = _iota2((SSD_HEADS, 1), 0)
        dacs_col = jnp.zeros((CHUNK, SSD_HEADS), F32)
        dacs_row = jnp.zeros((SSD_HEADS, CHUNK), F32)
        for g in range(SSD_GROUPS):
            bs = slice(D_SSD + SSD_N * g, D_SSD + SSD_N * (g + 1))
            cs = slice(D_SSD + D_BC + SSD_N * g, D_SSD + D_BC + SSD_N * (g + 1))
            cb = cbbuf[g]
            dcb = jnp.zeros((CHUNK, CHUNK), F32)
            for r in range(SSD_R):
                h = g * SSD_R + r
                dM = dmbuf[h]
                Lm = lmbuf[h]
                dcb = dcb + dM * Lm
                dseg = dM * (cb * Lm)
                dacs_col = dacs_col + jnp.sum(dseg, axis=-1, keepdims=True) * (lane16 == h).astype(F32)
                dacs_row = dacs_row - jnp.sum(dseg, axis=0, keepdims=True) * (sub16 == h).astype(F32)
            dcbb = _bf(dcb)
            dpost[:, bs] += _dot_tn(dcbb, _bf(xbcv[:, cs]))
            dpost[:, cs] += _dot(dcbb, _bf(xbcv[:, bs]))

        BD = bdbuf[...]
        dX = dpost[:, 0:D_SSD] + dsdE * BD
        dsd = jnp.exp(alast - acs)
        T = _headsum(X * BD, e) * dsd
        dalast = jnp.sum(T, axis=0, keepdims=True) + _headsum(
            jnp.sum(dST * ST, axis=0, keepdims=True), e) * jnp.exp(alast)
        is_last = (_iota2((CHUNK, 1), 0) == CHUNK - 1).astype(F32)
        dacs = dacs_col + _to_cols(dacs_row) + _headsum(dyp * yobuf[...], e) - T + is_last * dalast
        triu = (_iota2((CHUNK, CHUNK), 0) <= _iota2((CHUNK, CHUNK), 1)).astype(F32)
        da = jnp.dot(triu, dacs, preferred_element_type=F32, precision=HI)
        ddt_tot = _headsum(dX * xs, e) + da * A
        galog_ref[...] += jnp.sum(da * dt, axis=0, keepdims=True) * A
        ddtraw = ddt_tot * _sigmoid(dtraw)
        gdtb_ref[...] += jnp.sum(ddtraw, axis=0, keepdims=True)
        gdsk_ref[...] += _headsum(jnp.sum(dyp * xs, axis=0, keepdims=True), e)
        ddt_ref[...] = jnp.zeros_like(ddt_ref)
        ddt_ref[:, 0:SSD_HEADS] = ddtraw
        dpost[:, 0:D_SSD] = dX * dtE + dskE * dyp

        dconv = dpost[...] * (sig * (1.0 + u * (1.0 - sig)))
        gcb_ref[...] += jnp.sum(dconv, axis=0, keepdims=True)
        for k in range(CONV_K):
            gcw_ref[k:k + 1, :] += jnp.sum(dconv * taps[k], axis=0, keepdims=True)
        later = _shifted_rows(dconv, dhead[...], smat2_ref)
        dx = cw_ref[CONV_K - 1:CONV_K, :] * dconv
        for k in range(CONV_K - 1):
            dx = dx + cw_ref[k:k + 1, :] * later[k]
        dxbc_ref[...] = _bf(dx)
        dhead[...] = dconv[0:HALO, :]

    full = lambda shape: pl.BlockSpec(shape, lambda i: (0, 0))
    rev = lambda wd: pl.BlockSpec((CHUNK, wd), lambda i: (nc - 1 - i, 0))
    return pl.pallas_call(
        body, name="ssd_bwd", grid=(nc,),
        in_specs=[
            rev(D_SSD), rev(D_SSD), rev(D_SSD), rev(D_XBC),
            pl.BlockSpec((HALO, D_XBC), lambda i: (jnp.maximum((nc - 1 - i) * (CHUNK // HALO) - 1, 0), 0)),
            rev(DT_PAD),
            pl.BlockSpec((1, SSD_N, D_SSD), lambda i: (nc - 1 - i, 0, 0)),
            full((CONV_K, D_XBC)), full((1, D_XBC)), full((1, SSD_HEADS)), full((1, SSD_HEADS)), full((1, SSD_HEADS)),
            full((1, D_SSD)), full((3 * CHUNK, 2 * (CHUNK + HALO))), full((3 * CHUNK, 2 * (CHUNK + HALO))),
        ],
        out_specs=[
            rev(D_SSD), rev(D_XBC), rev(DT_PAD),
            full((CONV_K, D_XBC)), full((1, D_XBC)), full((1, SSD_HEADS)), full((1, SSD_HEADS)), full((1, SSD_HEADS)),
            full((1, D_SSD)),
        ],
        out_shape=[
            jax.ShapeDtypeStruct((L, D_SSD), BF16), jax.ShapeDtypeStruct((L, D_XBC), BF16),
            jax.ShapeDtypeStruct((L, DT_PAD), F32),
            jax.ShapeDtypeStruct((CONV_K, D_XBC), F32), jax.ShapeDtypeStruct((1, D_XBC), F32),
            jax.ShapeDtypeStruct((1, SSD_HEADS), F32), jax.ShapeDtypeStruct((1, SSD_HEADS), F32),
            jax.ShapeDtypeStruct((1, SSD_HEADS), F32), jax.ShapeDtypeStruct((1, D_SSD), F32),
        ],
        scratch_shapes=[
            pltpu.VMEM((SSD_N, D_SSD), F32),
            pltpu.VMEM((HALO, D_XBC), F32),
            pltpu.VMEM((CHUNK, D_XBC), F32),
            pltpu.VMEM((CHUNK, D_SSD), F32),
            pltpu.VMEM((CHUNK, D_SSD), F32),
            pltpu.VMEM((SSD_HEADS, CHUNK, CHUNK), F32),
            pltpu.VMEM((SSD_HEADS, CHUNK, CHUNK), F32),
            pltpu.VMEM((SSD_GROUPS, CHUNK, CHUNK), F32),
        ],
        compiler_params=_cparams(("arbitrary",)),
    )(dy, z, ypre, xbc, xbc, dtp, prev, conv_w, conv_b, dt_bias, a_log, d_skip, norm_w, _shift_matrix((13, 14, 15)),
      _shift_matrix((3, 2, 1)))


def _rope_tables(pos_ref, inv_ref):
    ang = pos_ref[...].astype(F32) * inv_ref[...]
    d = _iota2((1, 2 * ATT_HD), 1) % ATT_HD
    s = jnp.sin(ang)
    return jnp.cos(ang), jnp.where(d < ROPE_DIM // 2, -s, 0.0), jnp.where((d >= ROPE_DIM // 2) & (d < ROPE_DIM), s, 0.0)


def _rope(t, tabs):
    c, s1, s2 = tabs
    n = t.shape[1]
    rep = n // c.shape[1]
    return (t * jnp.tile(c, (1, rep)) + pltpu.roll(t, n - ROPE_DIM // 2, 1) * jnp.tile(s1, (1, rep))
            + pltpu.roll(t, ROPE_DIM // 2, 1) * jnp.tile(s2, (1, rep)))


def _rope_t(t, tabs):
    c, s1, s2 = tabs
    n = t.shape[1]
    rep = n // c.shape[1]
    return (t * jnp.tile(c, (1, rep)) + pltpu.roll(t * jnp.tile(s1, (1, rep)), ROPE_DIM // 2, 1)
            + pltpu.roll(t * jnp.tile(s2, (1, rep)), n - ROPE_DIM // 2, 1))


def _stack_heads(t, j):
    return jnp.concatenate([t[:, ATT_HD * (j * ATT_R + r):ATT_HD * (j * ATT_R + r + 1)] for r in range(ATT_R)], axis=0)


def _swa_mask_t(first):
    si = _iota2((2 * WINDOW, ATT_R * WINDOW), 0)
    qi = _iota2((2 * WINDOW, ATT_R * WINDOW), 1) % WINDOW
    band = (si > qi) & (si <= qi + WINDOW)
    return band & (jnp.logical_not(first) | (si >= WINDOW))


def _head_rows(ref, j):
    if ref.shape[0] == 1:
        parts = [jnp.broadcast_to(ref[:, j * ATT_R + r:j * ATT_R + r + 1], (1, WINDOW)) for r in range(ATT_R)]
    else:
        parts = [ref[j * ATT_R + r:j * ATT_R + r + 1, :] for r in range(ATT_R)]
    return jnp.concatenate(parts, axis=1)


def _swa_fwd(q, g, kv, sinks):
    L = q.shape[0]
    nb = L // WINDOW
    scale = ATT_HD ** -0.5

    def body(q_ref, g_ref, kvc_ref, kvp_ref, sink_ref, y_ref, o_ref, lse_ref, otbuf):
        n = pl.program_id(0)
        kk = jnp.concatenate([kvp_ref[:, 0:D_KV], kvc_ref[:, 0:D_KV]], axis=0)
        vv = jnp.concatenate([kvp_ref[:, D_KV:2 * D_KV], kvc_ref[:, D_KV:2 * D_KV]], axis=0)
        valid = _swa_mask_t(n == 0)
        qv = q_ref[...]
        for j in range(ATT_KVH):
            js = slice(ATT_HD * j, ATT_HD * (j + 1))
            st = _dot_nt(kk[:, js], _stack_heads(qv, j)) * scale
            st = jnp.where(valid, st, NEG_BIG)
            sink = _head_rows(sink_ref, j)
            m = jnp.maximum(jnp.max(st, axis=0, keepdims=True), sink)
            p = jnp.exp(st - m)
            denom = jnp.sum(p, axis=0, keepdims=True) + jnp.exp(sink - m)
            ot = _dot_tn(vv[:, js], _bf(p)) * (1.0 / denom)
            lse = m + jnp.log(denom)
            for r in range(ATT_R):
                h = j * ATT_R + r
                otbuf[ATT_HD * h:ATT_HD * (h + 1), :] = ot[:, WINDOW * r:WINDOW * (r + 1)]
                lse_ref[h:h + 1, :] = lse[:, WINDOW * r:WINDOW * (r + 1)]
        o = otbuf[...].T
        o_ref[...] = o
        gv = g_ref[...]
        y_ref[...] = _bf(o * (gv * _sigmoid(gv)))

    cur = lambda wd: pl.BlockSpec((WINDOW, wd), lambda n: (n, 0))
    prv = lambda wd: pl.BlockSpec((WINDOW, wd), lambda n: (jnp.maximum(n - 1, 0), 0))
    return pl.pallas_call(
        body, name="swa_fwd", grid=(nb,),
        in_specs=[cur(D_ATT), cur(D_ATT), cur(2 * D_KV), prv(2 * D_KV), pl.BlockSpec((1, ATT_QH), lambda n: (0, 0))],
        out_specs=[cur(D_ATT), cur(D_ATT), pl.BlockSpec((ATT_QH, WINDOW), lambda n: (0, n))],
        out_shape=[jax.ShapeDtypeStruct((L, D_ATT), BF16), jax.ShapeDtypeStruct((L, D_ATT), F32),
                   jax.ShapeDtypeStruct((ATT_QH, L), F32)],
        scratch_shapes=[pltpu.VMEM((D_ATT, WINDOW), F32)],
        compiler_params=_cparams(("parallel",)),
    )(q, g, kv, kv, sinks)


def _swa_bwd(dy, q, g, kv, o, lse, pos, inv, sinks):
    L = q.shape[0]
    nb = L // WINDOW
    scale = ATT_HD ** -0.5

    def body(dy_ref, q_ref, g_ref, kvc_ref, kvp_ref, o_ref, lse_ref, posc_ref, posp_ref, inv_ref, sink_ref,
             dq_ref, dg_ref, dkv_ref, dsink_ref, carry, dqbuf, dkbuf, dvbuf):
        n = pl.program_id(0)

        @pl.when(n == 0)
        def _():
            dsink_ref[...] = jnp.zeros_like(dsink_ref)

        @pl.when(n < nb)
        def _():
            tc = _rope_tables(posc_ref, inv_ref)
            tp = _rope_tables(posp_ref, inv_ref)
            kk = jnp.concatenate([kvp_ref[:, 0:D_KV], kvc_ref[:, 0:D_KV]], axis=0)
            vv = jnp.concatenate([kvp_ref[:, D_KV:2 * D_KV], kvc_ref[:, D_KV:2 * D_KV]], axis=0)
            valid = _swa_mask_t(n == 0)
            qv = q_ref[...]
            gv = g_ref[...]
            sg = _sigmoid(gv)
            dyv = dy_ref[...]
            ov = o_ref[...]
            dg_ref[...] = _bf(dyv * ov * (sg * (1.0 + gv * (1.0 - sg))))
            do = dyv * (gv * sg)
            dod = do * ov
            ones = jnp.ones((8, ATT_HD), BF16)
            lane16 = _iota2((1, ATT_QH), 1)
            dsink = jnp.zeros((1, ATT_QH), F32)
            for j in range(ATT_KVH):
                js = slice(ATT_HD * j, ATT_HD * (j + 1))
                kj = kk[:, js]
                vj = vv[:, js]
                qs = _stack_heads(qv, j)
                dos = _bf(_stack_heads(do, j))
                hi, lo = _hi_lo(_stack_heads(dod, j))
                delta = (_dot_nt(ones, hi) + _dot_nt(ones, lo))[0:1]
                lse = _head_rows(lse_ref, j)
                st = _dot_nt(kj, qs) * scale
                pt = jnp.exp(jnp.where(valid, st, NEG_BIG) - lse)
                dst = _bf(pt * (_dot_nt(vj, dos) - delta))
                dqt = _dot_tn(kj, dst) * scale
                dkbuf[:, js] = _dot(dst, qs) * scale
                dvbuf[:, js] = _dot(_bf(pt), dos)
                sd = jnp.exp(_head_rows(sink_ref, j) - lse) * delta
                for r in range(ATT_R):
                    h = j * ATT_R + r
                    ls = slice(WINDOW * r, WINDOW * (r + 1))
                    dqbuf[ATT_HD * h:ATT_HD * (h + 1), :] = dqt[:, ls]
                    dsink = dsink - jnp.sum(sd[:, ls], axis=1, keepdims=True) * (lane16 == h).astype(F32)
            dsink_ref[...] += dsink
            dq_ref[...] = _bf(_rope_t(dqbuf[...].T, tc))
            dkp = _rope_t(dkbuf[0:WINDOW, :], tp)
            dkc = _rope_t(dkbuf[WINDOW:2 * WINDOW, :], tc)

            @pl.when(n > 0)
            def _():
                dkv_ref[:, 0:D_KV] = _bf(carry[:, 0:D_KV] + dkp)
                dkv_ref[:, D_KV:2 * D_KV] = _bf(carry[:, D_KV:2 * D_KV] + dvbuf[0:WINDOW, :])

            carry[:, 0:D_KV] = dkc
            carry[:, D_KV:2 * D_KV] = dvbuf[WINDOW:2 * WINDOW, :]

        @pl.when(n == nb)
        def _():
            dkv_ref[...] = _bf(carry[...])

    last = nb - 1
    cur = lambda wd: pl.BlockSpec((WINDOW, wd), lambda n: (jnp.minimum(n, last), 0))
    prv = lambda wd: pl.BlockSpec((WINDOW, wd), lambda n: (jnp.maximum(jnp.minimum(n, last) - 1, 0), 0))
    return pl.pallas_call(
        body, name="swa_bwd", grid=(nb + 1,),
        in_specs=[cur(D_ATT), cur(D_ATT), cur(D_ATT), cur(2 * D_KV), prv(2 * D_KV), cur(D_ATT),
                  pl.BlockSpec((ATT_QH, WINDOW), lambda n: (0, jnp.minimum(n, last))), cur(1), prv(1),
                  pl.BlockSpec((1, 2 * ATT_HD), lambda n: (0, 0)), pl.BlockSpec((1, ATT_QH), lambda n: (0, 0))],
        out_specs=[cur(D_ATT), cur(D_ATT),
                   pl.BlockSpec((WINDOW, 2 * D_KV), lambda n: (jnp.maximum(n - 1, 0), 0)),
                   pl.BlockSpec((1, ATT_QH), lambda n: (0, 0))],
        out_shape=[jax.ShapeDtypeStruct((L, D_ATT), BF16), jax.ShapeDtypeStruct((L, D_ATT), BF16),
                   jax.ShapeDtypeStruct((L, 2 * D_KV), BF16), jax.ShapeDtypeStruct((1, ATT_QH), F32)],
        scratch_shapes=[pltpu.VMEM((WINDOW, 2 * D_KV), F32), pltpu.VMEM((D_ATT, WINDOW), F32),
                        pltpu.VMEM((2 * WINDOW, D_KV), F32), pltpu.VMEM((2 * WINDOW, D_KV), F32)],
        compiler_params=_cparams(("arbitrary",)),
    )(dy, q, g, kv, kv, o, lse, pos, pos, inv, sinks)


def _out_ln_loss(y_ssd, y_att, x, target, w_out, ln_g, ln_b):
    L = x.shape[0]
    tm = min(ROW_TILE, L)
    nt = L // tm
    inv_d = 1.0 / D_MODEL

    def body(ys_ref, ya_ref, x_ref, t_ref, w_ref, g_ref, b_ref, dr_ref, dys_ref, dya_ref, loss_ref, gg_ref, gb_ref,
             gwo_ref, acc_ref):
        i = pl.program_id(0)

        @pl.when(i == 0)
        def _():
            loss_ref[...] = jnp.zeros_like(loss_ref)
            gg_ref[...] = jnp.zeros_like(gg_ref)
            gb_ref[...] = jnp.zeros_like(gb_ref)
            acc_ref[...] = jnp.zeros_like(acc_ref)

        halves = [slice(0, tm // 2), slice(tm // 2, tm)]
        hs = [_dot(_bf(ys_ref[rs, :]), w_ref[0:D_SSD, :]) + _dot(_bf(ya_ref[rs, :]), w_ref[D_SSD:D_MIX, :]) for rs in halves]
        gam = g_ref[...]
        for rs, h in zip(halves, hs):
            r = ALPHA * x_ref[rs, :] + h
            mu = jnp.mean(r, axis=-1, keepdims=True)
            xc = r - mu
            rstd = lax.rsqrt(jnp.mean(xc * xc, axis=-1, keepdims=True) + LN_EPS)
            xhat = xc * rstd
            diff = xhat * gam + b_ref[...] - t_ref[rs, :]
            part = jnp.sum(jnp.sum(diff * diff, axis=-1, keepdims=True), axis=0, keepdims=True)
            loss_ref[...] += (0.5 * inv_d) * part
            dout = diff * inv_d
            gg_ref[...] += jnp.sum(dout * xhat, axis=0, keepdims=True)
            gb_ref[...] += jnp.sum(dout, axis=0, keepdims=True)
            dxh = dout * gam
            dr_ref[rs, :] = rstd * (dxh - jnp.mean(dxh, axis=-1, keepdims=True)
                                    - xhat * jnp.mean(dxh * xhat, axis=-1, keepdims=True))
        for rs in halves:
            drh = _bf(dr_ref[rs, :])
            dys_ref[rs, :] = _dot_nt(drh, w_ref[0:D_SSD, :])
            dya_ref[rs, :] = _dot_nt(drh, w_ref[D_SSD:D_MIX, :])
        drb = _bf(dr_ref[...])
        acc_ref[0:D_SSD, :] += _dot_tn(_bf(ys_ref[...]), drb)
        acc_ref[D_SSD:D_MIX, :] += _dot_tn(_bf(ya_ref[...]), drb)

        @pl.when(i == nt - 1)
        def _():
            gwo_ref[...] = _bf(acc_ref[...])

    row = pl.BlockSpec((tm, D_MODEL), lambda i: (i, 0))
    vec = pl.BlockSpec((1, D_MODEL), lambda i: (0, 0))
    return pl.pallas_call(
        body, name="out_ln_loss", grid=(nt,),
        in_specs=[row, row, row, row, pl.BlockSpec((D_MIX, D_MODEL), lambda i: (0, 0), pipeline_mode=pl.Buffered(1)), vec, vec],
        out_specs=[row, row, row, pl.BlockSpec((1, 128), lambda i: (0, 0)), vec, vec,
                   pl.BlockSpec((D_MIX, D_MODEL), lambda i: (0, 0))],
        out_shape=[jax.ShapeDtypeStruct((L, D_MODEL), F32)] * 3 + [jax.ShapeDtypeStruct((1, 128), F32)]
        + [jax.ShapeDtypeStruct((1, D_MODEL), F32)] * 2 + [jax.ShapeDtypeStruct((D_MIX, D_MODEL), BF16)],
        scratch_shapes=[pltpu.VMEM((D_MIX, D_MODEL), F32)],
        compiler_params=_cparams(("arbitrary",)),
    )(y_ssd, y_att, x, target, w_out, ln_g, ln_b)


def _local_step(x, pos, target, w, get_w_out, token, conv_w, conv_b, dt_bias, a_log, d_skip, norm_w, sinks, ln_g, ln_b):
    inv8 = ROPE_THETA ** (-jnp.arange(0, ROPE_DIM, 2, dtype=F32) / ROPE_DIM)
    inv = jnp.tile(jnp.concatenate([inv8, inv8, jnp.zeros((ATT_HD - ROPE_DIM,), F32)]), 2).reshape(1, 2 * ATT_HD)
    inv = inv + token

    z, g, q, xbc, kv, dtp, xb = _in_proj(x, w, pos, inv)
    y_ssd, y_pre, prev = _ssd_fwd(z, xbc, dtp, conv_w, conv_b, dt_bias, a_log, d_skip, norm_w)
    y_att, o, lse = _swa_fwd(q, g, kv, sinks)
    w_out = get_w_out(lse)
    dr, dy_ssd, dy_att, loss, g_ln_g, g_ln_b, gw_out = _out_ln_loss(y_ssd, y_att, x, target, w_out, ln_g, ln_b)
    w_out_red = _reduce_w_out_start(gw_out.reshape(N_CHIPS, W_OUT_ROWS, D_MODEL), loss)
    inv = inv + w_out_red[16][0:1, :]
    dq, dg, dkv, g_sinks = _swa_bwd(dy_att, q, g, kv, o, lse, pos, inv, sinks)
    dz, dxbc, ddt, g_conv_w, g_conv_b, g_dt_bias, g_a_log, g_d_skip, g_norm_w = _ssd_bwd(
        dy_ssd, z, y_pre, xbc, dtp, prev, conv_w, conv_b, dt_bias, a_log, d_skip, norm_w)
    gw_z, gw_g, gw_q = _matmuls_tn([dz, dg, dq], xb, "gw_zgq")
    gw_xbc, gw_kv, gw_dt = _matmuls_tn([dxbc, dkv, ddt], xb, "gw_xbc_kv_dt")
    gw_in = jnp.concatenate([gw_z, gw_xbc, gw_dt[0:SSD_HEADS], gw_q, gw_kv, gw_g], axis=0)
    small = dict(conv_w=g_conv_w, conv_b=g_conv_b, dt_bias=g_dt_bias, a_log=g_a_log, d_skip=g_d_skip,
                 ssd_norm_w=g_norm_w, attn_sinks=g_sinks, ln_g=g_ln_g, ln_b=g_ln_b)
    return loss, (dr, dz, dg, dq, dxbc, dkv, ddt, w), gw_in, w_out_red, small


def _mesh_pos():
    return lax.axis_index("x"), lax.axis_index("y"), lax.axis_index("c")


def _gather_weights(w_in_s, conv_w_s):
    hr = w_in_s.shape[0] // 2
    qa = 336
    quarters = ((0, qa), (qa, hr - qa))

    def body(win_ref, cw_ref, owin_ref, ocw_ref, stage, send_sems, recv_sems, small_send, small_recv, local_sems):
        x, y, c = _mesh_pos()
        me = 2 * x + y
        sibling = (x, y, 1 - c)
        xn, yn, dg = (1 - x, y), (x, 1 - y), (1 - x, 1 - y)
        chips = [xn, yn, dg]
        load = pltpu.make_async_copy(win_ref, stage, local_sems.at[1])
        load.start()
        locals_ = [pltpu.make_async_copy(cw_ref, ocw_ref.at[me], local_sems.at[0])]
        for cp in locals_:
            cp.start()
        started = []

        def piece(ref, chip, half, q):
            off, n = quarters[q]
            return ref.at[2 * chip[0] + chip[1]].at[pl.ds(half * hr + off, n), :]

        def mine(q):
            off, n = quarters[q]
            return win_ref.at[pl.ds(c * hr + off, n), :]

        def copy(src, dst, k, to):
            return pltpu.make_async_remote_copy(src_ref=src, dst_ref=dst, send_sem=send_sems.at[k], recv_sem=recv_sems.at[k],
                                                device_id=to, device_id_type=MESH)

        def go(cp):
            cp.start()
            started.append(cp)

        go(copy(mine(0), piece(owin_ref, (x, y), c, 0), 0, (*xn, c)))
        go(copy(mine(1), piece(owin_ref, (x, y), c, 1), 2, (*yn, c)))
        go(copy(mine(1), piece(owin_ref, (x, y), c, 1), 1, (*xn, c)))
        go(copy(mine(0), piece(owin_ref, (x, y), c, 0), 3, (*yn, c)))
        for j, (px, py) in enumerate(chips):
            cp = pltpu.make_async_remote_copy(
                src_ref=cw_ref, dst_ref=ocw_ref.at[me], send_sem=small_send.at[j], recv_sem=small_recv.at[j],
                device_id=(px, py, c), device_id_type=MESH)
            go(cp)
        load.wait()
        store = pltpu.make_async_copy(stage, owin_ref.at[me], local_sems.at[2])
        store.start()
        locals_.append(store)
        arrivals = [(0, xn, 0, (4, (*yn, c))), (2, yn, 1, (5, (*xn, c))), (1, xn, 1, None), (3, yn, 0, None),
                    (4, dg, 0, None), (5, dg, 1, None)]
        for n, (k, chip, q, onward) in enumerate(arrivals):
            blk = piece(owin_ref, chip, c, q)
            copy(blk, blk, k, sibling).wait_recv()
            if onward is not None:
                go(copy(blk, blk, onward[0], onward[1]))
            go(copy(blk, blk, 6 + n, sibling))
        for n, (k, chip, q, onward) in enumerate(arrivals):
            blk = piece(owin_ref, chip, 1 - c, q)
            copy(blk, blk, 6 + n, sibling).wait_recv()
        for j in range(3):
            pltpu.make_async_remote_copy(
                src_ref=cw_ref, dst_ref=ocw_ref.at[me], send_sem=small_send.at[j], recv_sem=small_recv.at[j],
                device_id=sibling, device_id_type=MESH).wait_recv()
        for cp in started:
            cp.wait_send()
        for cp in locals_:
            cp.wait()

    any_spec = pl.BlockSpec(memory_space=pl.ANY)
    return pl.pallas_call(
        body, name="gather_weights",
        in_specs=[any_spec] * 2, out_specs=[any_spec] * 2,
        out_shape=[jax.ShapeDtypeStruct((N_CHIPS,) + a.shape, a.dtype) for a in (w_in_s, conv_w_s)],
        scratch_shapes=[pltpu.VMEM(w_in_s.shape, w_in_s.dtype),
                        pltpu.SemaphoreType.DMA((12,)), pltpu.SemaphoreType.DMA((12,)),
                        pltpu.SemaphoreType.DMA((3,)), pltpu.SemaphoreType.DMA((3,)), pltpu.SemaphoreType.DMA((3,))],
    )(w_in_s, conv_w_s)


_HBM = pl.BlockSpec(memory_space=pltpu.HBM)
_SEM = pl.BlockSpec(memory_space=pltpu.SEMAPHORE)
_EFFECT = pltpu.SideEffectType.DATAFLOW_SIDE_EFFECTING


def _gather_w_out_start(w_out_s, after):
    def body(src_ref, land_ref, after_ref, s0, s1, s2, r0, r1, r2, src_thru, land_thru, token):
        x, y, c = _mesh_pos()
        me = 2 * x + y
        chips = [(1 - x, y), (x, 1 - y), (1 - x, 1 - y)]
        for (px, py), s, r in zip(chips, (s0, s1, s2), (r0, r1, r2)):
            pltpu.make_async_remote_copy(src_ref=src_ref, dst_ref=land_ref.at[me], send_sem=s, recv_sem=r,
                                         device_id=(px, py, c), device_id_type=MESH).start()
        token[...] = jnp.zeros_like(token)

    sem = pltpu.SemaphoreType.DMA(())
    land = lax.empty((N_CHIPS,) + w_out_s.shape, w_out_s.dtype)
    return pl.pallas_call(
        body, name="gather_w_out_start",
        out_shape=(sem,) * 6 + (pltpu.HBM(w_out_s.shape, w_out_s.dtype), pltpu.HBM(land.shape, land.dtype),
                                jax.ShapeDtypeStruct((8, 128), F32)),
        in_specs=(_HBM, _HBM, pl.BlockSpec(memory_space=pl.ANY)),
        out_specs=(_SEM,) * 6 + (_HBM, _HBM, pl.BlockSpec(memory_space=pltpu.VMEM)),
        input_output_aliases={0: 6, 1: 7},
        compiler_params=pltpu.CompilerParams(has_side_effects=_EFFECT),
    )(pltpu.with_memory_space_constraint(w_out_s, pltpu.HBM), pltpu.with_memory_space_constraint(land, pltpu.HBM), after)


def _gather_w_out_wait(sems, src_thru, land_thru, after):
    def body(src_ref, land_ref, s0, s1, s2, r0, r1, r2, after_ref, src_dead, got_ref):
        x, y, c = _mesh_pos()
        chips = [(1 - x, y), (x, 1 - y), (1 - x, 1 - y)]
        for (px, py), s, r in zip(chips, (s0, s1, s2), (r0, r1, r2)):
            cp = pltpu.make_async_remote_copy(src_ref=src_ref, dst_ref=land_ref.at[2 * px + py], send_sem=s, recv_sem=r,
                                              device_id=(px, py, c), device_id_type=MESH)
            cp.wait_send()
            cp.wait_recv()

    return pl.pallas_call(
        body, name="gather_w_out_wait",
        out_shape=(pltpu.HBM(src_thru.shape, src_thru.dtype), pltpu.HBM(land_thru.shape, land_thru.dtype)),
        in_specs=(_HBM, _HBM) + (_SEM,) * 6 + (pl.BlockSpec(memory_space=pl.ANY),),
        out_specs=(_HBM, _HBM), input_output_aliases={0: 0, 1: 1},
        compiler_params=pltpu.CompilerParams(has_side_effects=_EFFECT),
    )(src_thru, land_thru, *sems, after)[1]


def _pair_start(gw_in, after):
    hr = gw_in.shape[1] // 2

    def body(src_ref, land_ref, after_ref, *refs):
        x, y, c = _mesh_pos()
        for j in range(N_CHIPS):
            pltpu.make_async_remote_copy(
                src_ref=src_ref.at[j, pl.ds((1 - c) * hr, hr), :], dst_ref=land_ref.at[j], send_sem=refs[j],
                recv_sem=refs[N_CHIPS + j], device_id=(x, y, 1 - c), device_id_type=MESH).start()
        refs[10][...] = jnp.zeros_like(refs[10])

    sem = pltpu.SemaphoreType.DMA(())
    land = lax.empty((N_CHIPS, hr, D_MODEL), F32)
    return pl.pallas_call(
        body, name="pair_start",
        out_shape=(sem,) * 8 + (pltpu.HBM(gw_in.shape, F32), pltpu.HBM(land.shape, F32), jax.ShapeDtypeStruct((8, 128), F32)),
        in_specs=(_HBM, _HBM, pl.BlockSpec(memory_space=pl.ANY)),
        out_specs=(_SEM,) * 8 + (_HBM, _HBM, pl.BlockSpec(memory_space=pltpu.VMEM)),
        input_output_aliases={0: 8, 1: 9},
        compiler_params=pltpu.CompilerParams(has_side_effects=_EFFECT),
    )(pltpu.with_memory_space_constraint(gw_in, pltpu.HBM), pltpu.with_memory_space_constraint(land, pltpu.HBM), after)


def _pair_wait(sems, gw_thru, land_thru, after):
    hr = land_thru.shape[1]

    def body(src_ref, land_ref, *refs):
        x, y, c = _mesh_pos()
        for j in range(N_CHIPS):
            cp = pltpu.make_async_remote_copy(
                src_ref=src_ref.at[j, pl.ds((1 - c) * hr, hr), :], dst_ref=land_ref.at[j], send_sem=refs[j],
                recv_sem=refs[N_CHIPS + j], device_id=(x, y, 1 - c), device_id_type=MESH)
            cp.wait_send()
            cp.wait_recv()

    return pl.pallas_call(
        body, name="pair_wait",
        out_shape=(pltpu.HBM(gw_thru.shape, F32), pltpu.HBM(land_thru.shape, F32)),
        in_specs=(_HBM, _HBM) + (_SEM,) * 8 + (pl.BlockSpec(memory_space=pl.ANY),),
        out_specs=(_HBM, _HBM), input_output_aliases={0: 0, 1: 1},
        compiler_params=pltpu.CompilerParams(has_side_effects=_EFFECT),
    )(gw_thru, land_thru, *sems, after)


def _chip_start(s_in, after):
    def body(src_ref, land_ref, after_ref, *refs):
        x, y, c = _mesh_pos()
        me = 2 * x + y
        for j, (px, py) in enumerate([(1 - x, y), (x, 1 - y), (1 - x, 1 - y)]):
            pltpu.make_async_remote_copy(
                src_ref=src_ref.at[2 * px + py], dst_ref=land_ref.at[me], send_sem=refs[j], recv_sem=refs[3 + j],
                device_id=(px, py, c), device_id_type=MESH).start()
        refs[8][...] = jnp.zeros_like(refs[8])

    sem = pltpu.SemaphoreType.DMA(())
    land = lax.empty(s_in.shape, s_in.dtype)
    return pl.pallas_call(
        body, name="chip_start",
        out_shape=(sem,) * 6 + (pltpu.HBM(s_in.shape, s_in.dtype), pltpu.HBM(land.shape, land.dtype),
                                jax.ShapeDtypeStruct((8, 128), F32)),
        in_specs=(_HBM, _HBM, pl.BlockSpec(memory_space=pl.ANY)),
        out_specs=(_SEM,) * 6 + (_HBM, _HBM, pl.BlockSpec(memory_space=pltpu.VMEM)),
        input_output_aliases={0: 6, 1: 7},
        compiler_params=pltpu.CompilerParams(has_side_effects=_EFFECT),
    )(pltpu.with_memory_space_constraint(s_in, pltpu.HBM), pltpu.with_memory_space_constraint(land, pltpu.HBM), after)


def _chip_wait(sems, s_thru, land_thru, after):
    def body(src_ref, land_ref, *refs):
        x, y, c = _mesh_pos()
        for j, (px, py) in enumerate([(1 - x, y), (x, 1 - y), (1 - x, 1 - y)]):
            cp = pltpu.make_async_remote_copy(
                src_ref=src_ref.at[2 * px + py], dst_ref=land_ref.at[2 * px + py], send_sem=refs[j], recv_sem=refs[3 + j],
                device_id=(px, py, c), device_id_type=MESH)
            cp.wait_send()
            cp.wait_recv()

    return pl.pallas_call(
        body, name="chip_wait",
        out_shape=(pltpu.HBM(s_thru.shape, s_thru.dtype), pltpu.HBM(land_thru.shape, land_thru.dtype)),
        in_specs=(_HBM, _HBM) + (_SEM,) * 6 + (pl.BlockSpec(memory_space=pl.ANY),),
        out_specs=(_HBM, _HBM), input_output_aliases={0: 0, 1: 1},
        compiler_params=pltpu.CompilerParams(has_side_effects=_EFFECT),
    )(s_thru, land_thru, *sems, after)


def _pair_share(h_in, small):
    def body(hin_ref, sm_ref, rin_ref, slots_ref, send_sems, recv_sems, small_send, small_recv, local_sem):
        x, y, c = _mesh_pos()
        dev = 4 * x + 2 * y + c
        mine = pltpu.make_async_copy(sm_ref, slots_ref.at[dev], local_sem)
        mine.start()
        share = pltpu.make_async_remote_copy(
            src_ref=hin_ref, dst_ref=rin_ref, send_sem=send_sems.at[0], recv_sem=recv_sems.at[0],
            device_id=(x, y, 1 - c), device_id_type=MESH)
        share.start()
        started = []
        for k in range(1, 8):
            peer = (x ^ ((k >> 2) & 1), y ^ ((k >> 1) & 1), c ^ (k & 1))
            cp = pltpu.make_async_remote_copy(
                src_ref=sm_ref, dst_ref=slots_ref.at[dev], send_sem=small_send.at[k - 1], recv_sem=small_recv.at[k - 1],
                device_id=peer, device_id_type=MESH)
            cp.start()
            started.append(cp)
        share.wait()
        for k in range(1, 8):
            pltpu.make_async_remote_copy(
                src_ref=sm_ref, dst_ref=slots_ref.at[dev], send_sem=small_send.at[k - 1], recv_sem=small_recv.at[k - 1],
                device_id=(x, y, 1 - c), device_id_type=MESH).wait_recv()
        for cp in started:
            cp.wait_send()
        mine.wait()

    any_spec = pl.BlockSpec(memory_space=pl.ANY)
    return pl.pallas_call(
        body, name="pair_share",
        in_specs=[any_spec] * 2, out_specs=[any_spec] * 2,
        out_shape=[jax.ShapeDtypeStruct(h_in.shape, F32), jax.ShapeDtypeStruct((8,) + small.shape, F32)],
        scratch_shapes=[pltpu.SemaphoreType.DMA((1,)), pltpu.SemaphoreType.DMA((1,)),
                        pltpu.SemaphoreType.DMA((7,)), pltpu.SemaphoreType.DMA((7,)), pltpu.SemaphoreType.DMA],
    )(h_in, small)


def _reduce_w_out_start(slabs, after):
    def body(src_ref, land_ref, after_ref, *refs):
        x, y, c = _mesh_pos()
        me = 4 * x + 2 * y + c
        for k in range(1, 8):
            px, py, pc = x ^ ((k >> 2) & 1), y ^ ((k >> 1) & 1), c ^ (k & 1)
            pltpu.make_async_remote_copy(src_ref=src_ref.at[2 * px + py], dst_ref=land_ref.at[me], send_sem=refs[k - 1],
                                         recv_sem=refs[6 + k], device_id=(px, py, pc), device_id_type=MESH).start()
        refs[16][...] = jnp.zeros_like(refs[16])

    sem = pltpu.SemaphoreType.DMA(())
    land = lax.empty((8,) + slabs.shape[1:], slabs.dtype)
    return pl.pallas_call(
        body, name="reduce_w_out_start",
        out_shape=(sem,) * 14 + (pltpu.HBM(slabs.shape, slabs.dtype), pltpu.HBM(land.shape, land.dtype),
                                 jax.ShapeDtypeStruct((8, 128), F32)),
        in_specs=(_HBM, _HBM, pl.BlockSpec(memory_space=pl.ANY)),
        out_specs=(_SEM,) * 14 + (_HBM, _HBM, pl.BlockSpec(memory_space=pltpu.VMEM)),
        input_output_aliases={0: 14, 1: 15},
        compiler_params=pltpu.CompilerParams(has_side_effects=_EFFECT),
    )(pltpu.with_memory_space_constraint(slabs, pltpu.HBM), pltpu.with_memory_space_constraint(land, pltpu.HBM), after)


def _reduce_w_out_wait(sems, slabs_thru, land_thru, after):
    def body(src_ref, land_ref, *refs):
        x, y, c = _mesh_pos()
        for k in range(1, 8):
            px, py, pc = x ^ ((k >> 2) & 1), y ^ ((k >> 1) & 1), c ^ (k & 1)
            cp = pltpu.make_async_remote_copy(
                src_ref=src_ref.at[2 * px + py], dst_ref=land_ref.at[4 * px + 2 * py + pc], send_sem=refs[k - 1],
                recv_sem=refs[6 + k], device_id=(px, py, pc), device_id_type=MESH)
            cp.wait_send()
            cp.wait_recv()

    return pl.pallas_call(
        body, name="reduce_w_out_wait",
        out_shape=(pltpu.HBM(slabs_thru.shape, slabs_thru.dtype), pltpu.HBM(land_thru.shape, land_thru.dtype)),
        in_specs=(_HBM, _HBM) + (_SEM,) * 14 + (pl.BlockSpec(memory_space=pl.ANY),),
        out_specs=(_HBM, _HBM), input_output_aliases={0: 0, 1: 1},
        compiler_params=pltpu.CompilerParams(has_side_effects=_EFFECT),
    )(slabs_thru, land_thru, *sems, after)


def _pair_add(g, recv, core, name):
    _, rows, C = recv.shape
    tc = 256

    def body(core_ref, g_ref, r_ref, o_ref):
        o_ref[...] = _bf(g_ref[...] + r_ref[...])

    spec = pl.BlockSpec((1, rows, tc), lambda j, i, core: (j, 0, i))
    return pl.pallas_call(
        body, name=name,
        grid_spec=pltpu.PrefetchScalarGridSpec(
            num_scalar_prefetch=1, grid=(N_CHIPS, C // tc),
            in_specs=[pl.BlockSpec((1, rows, tc), lambda j, i, core: (j, core[0], i)), spec], out_specs=spec),
        out_shape=jax.ShapeDtypeStruct((N_CHIPS, rows, C), BF16),
        compiler_params=_cparams(("parallel", "parallel")),
    )(core, g, recv)


def _chip_add(own, parts, chip, name):
    _, rows, C = parts.shape
    tc = 256

    def body(chip_ref, own_ref, r0, r1, r2, r3, o_ref):
        acc = None
        for j, r in enumerate((r0, r1, r2, r3)):
            term = jnp.where(chip_ref[0] == j, own_ref[0], r[0]).astype(F32)
            acc = term if acc is None else acc + term
        o_ref[...] = acc

    def slab(j):
        return pl.BlockSpec((1, rows, tc), lambda i, chip: (jnp.where(chip[0] == j, (j + 1) % N_CHIPS, j), 0, i))

    return pl.pallas_call(
        body, name=name,
        grid_spec=pltpu.PrefetchScalarGridSpec(
            num_scalar_prefetch=1, grid=(C // tc,),
            in_specs=[pl.BlockSpec((1, rows, tc), lambda i, chip: (chip[0], 0, i))] + [slab(j) for j in range(N_CHIPS)],
            out_specs=pl.BlockSpec((rows, tc), lambda i, chip: (0, i))),
        out_shape=jax.ShapeDtypeStruct((rows, C), F32),
        compiler_params=_cparams(("parallel",)),
    )(chip, own, parts, parts, parts, parts)


def _adamw_math(w, g, m, v):
    m = ADAM_B1 * m + (1.0 - ADAM_B1) * g
    v = ADAM_B2 * v + (1.0 - ADAM_B2) * (g * g)
    m_hat = m / (1.0 - ADAM_B1 ** ADAM_STEP)
    v_hat = v / (1.0 - ADAM_B2 ** ADAM_STEP)
    delta = -ADAM_LR * (m_hat / (jnp.sqrt(v_hat) + ADAM_EPS) + ADAM_WD * w)
    return delta, m, v


def _adamw_rows(w, g_own, g_sib, m, v, core, name):
    R, C = w.shape[0], w.shape[-1]
    rows = g_own.shape[0]
    step = 256
    chunks = [(r, min(step, R - r)) for r in range(0, R, step)]
    sub = 64

    def body(core_ref, w_hbm, go_hbm, gs_hbm, m_hbm, v_hbm, d_hbm, nm_hbm, nv_hbm, g_hbm,
             wbuf, mbuf, vbuf, gbuf, dbuf, nmbuf, nvbuf, in_sems, g_sems, out_sems):
        c = core_ref[0]
        flat = lambda ref: ref.at[:, 0, :]
        g_in = [pltpu.make_async_copy(go_hbm, gbuf.at[pl.ds(pl.multiple_of(c * rows, 8), rows), :], g_sems.at[0]),
                pltpu.make_async_copy(gs_hbm, gbuf.at[pl.ds(pl.multiple_of((1 - c) * rows, 8), rows), :], g_sems.at[1])]
        for cp in g_in:
            cp.start()
        loads = []
        for k, (r0, n) in enumerate(chunks):
            cps = [pltpu.make_async_copy(flat(src).at[pl.ds(r0, n), :], dst.at[pl.ds(r0, n), :], in_sems.at[a, k])
                   for a, (src, dst) in enumerate(((w_hbm, wbuf), (m_hbm, mbuf), (v_hbm, vbuf)))]
            for cp in cps:
                cp.start()
            loads.append(cps)
        for cp in g_in:
            cp.wait()
        stores = []
        for k, (r0, n) in enumerate(chunks):
            for cp in loads[k]:
                cp.wait()

            def update(rs):
                g = gbuf[rs, :]
                dl, nm, nv = _adamw_math(wbuf[rs, :], g, mbuf[rs, :], vbuf[rs, :])
                dbuf[rs, :] = dl
                nmbuf[rs, :] = nm
                nvbuf[rs, :] = nv

            if n % sub == 0:
                def block(i, carry, r0=r0):
                    update(pl.ds(pl.multiple_of(r0 + i * sub, 8), sub))
                    return carry
                lax.fori_loop(0, n // sub, block, 0)
            else:
                update(pl.ds(r0, n))
            cps = [pltpu.make_async_copy(src.at[pl.ds(r0, n), :], flat(dst).at[pl.ds(r0, n), :], out_sems.at[a, k])
                   for a, (src, dst) in enumerate(((dbuf, d_hbm), (nmbuf, nm_hbm), (nvbuf, nv_hbm), (gbuf, g_hbm)))]
            for cp in cps:
                cp.start()
            stores += cps
        for cp in stores:
            cp.wait()

    any_spec = pl.BlockSpec(memory_space=pl.ANY)
    dense = pltpu.VMEM((R, C), F32)
    return pl.pallas_call(
        body, name=name,
        grid_spec=pltpu.PrefetchScalarGridSpec(
            num_scalar_prefetch=1, grid=(1,),
            in_specs=[any_spec] * 5, out_specs=[any_spec] * 4,
            scratch_shapes=[dense, dense, dense, pltpu.VMEM((2 * rows, C), F32), dense, dense, dense,
                            pltpu.SemaphoreType.DMA((3, len(chunks))), pltpu.SemaphoreType.DMA((2,)),
                            pltpu.SemaphoreType.DMA((4, len(chunks)))]),
        out_shape=[jax.ShapeDtypeStruct(w.shape, F32)] * 4,
        compiler_params=_cparams(),
    )(core, w, g_own, g_sib, m, v)


def _adamw_sum8(w, slabs, land, m, v, ids, name):
    R, C = w.shape
    tc = 128

    def body(ids_ref, w_ref, own_ref, *refs):
        lrefs, (m_ref, v_ref, d_ref, nm_ref, nv_ref, g_ref) = refs[:8], refs[8:]
        g = None
        for d, l_ref in enumerate(lrefs):
            term = jnp.where(ids_ref[0] == d, own_ref[0], l_ref[0]).astype(F32)
            g = term if g is None else g + term
        dl, nm, nv = _adamw_math(w_ref[...], g, m_ref[...], v_ref[...])
        d_ref[...] = dl
        nm_ref[...] = nm
        nv_ref[...] = nv
        g_ref[...] = g

    def slot(d):
        return pl.BlockSpec((1, R, tc), lambda i, ids: (jnp.where(ids[0] == d, (d + 1) % 8, d), 0, i))

    spec = pl.BlockSpec((R, tc), lambda i, ids: (0, i))
    return pl.pallas_call(
        body, name=name,
        grid_spec=pltpu.PrefetchScalarGridSpec(
            num_scalar_prefetch=1, grid=(C // tc,),
            in_specs=[spec, pl.BlockSpec((1, R, tc), lambda i, ids: (ids[1], 0, i))] + [slot(d) for d in range(8)]
            + [spec, spec],
            out_specs=[spec] * 4),
        out_shape=[jax.ShapeDtypeStruct((R, C), F32)] * 4,
        compiler_params=_cparams(("parallel",)),
    )(ids, w, slabs, *([land] * 8), m, v)


SMALL_NAMES = ("conv_b", "ssd_norm_w", "ln_g", "ln_b", "dt_bias", "a_log", "d_skip", "attn_sinks")
SMALL_FIELDS = ((4, 0, D_XBC), (5, 0, D_SSD), (6, 0, D_MODEL), (7, 0, D_MODEL), (5, 1024, SSD_HEADS), (5, 1152, SSD_HEADS),
                (5, 1280, SSD_HEADS), (5, 1408, ATT_QH))
LOSS_FIELD = (6, 1024, 128)
K_SMALL = D_XBC


def _pack_small(g_conv_w, vecs, loss):
    def body(cw_ref, *refs):
        o_ref = refs[-1]
        o_ref[...] = jnp.zeros_like(o_ref)
        o_ref[0:CONV_K, 0:D_XBC] = cw_ref[...]
        for v_ref, (row, off, n) in zip(refs[:-2], SMALL_FIELDS):
            o_ref[row:row + 1, off:off + n] = v_ref[...]
        o_ref[LOSS_FIELD[0]:LOSS_FIELD[0] + 1, LOSS_FIELD[1]:LOSS_FIELD[1] + LOSS_FIELD[2]] = refs[-2][...]

    return pl.pallas_call(
        body, name="pack_small", out_shape=jax.ShapeDtypeStruct((8, K_SMALL), F32), compiler_params=_cparams(),
    )(g_conv_w, *vecs, loss)


def _adamw_small(slots, chip, conv_w, m_conv_w, v_conv_w, params, moms, vars_):
    n_vec = len(SMALL_NAMES)

    def body(chip_ref, s_ref, *refs):
        ins = refs[:3 * (n_vec + 1)]
        outs = refs[3 * (n_vec + 1):-1]
        tot_ref = refs[-1]
        tot = s_ref[0]
        for d in range(1, 8):
            tot = tot + s_ref[d]
        outs[0][...] = tot[LOSS_FIELD[0]:LOSS_FIELD[0] + 1, LOSS_FIELD[1]:LOSS_FIELD[1] + 1]
        off = pl.multiple_of(chip_ref[0] * CONV_COLS, 128)
        tot_ref[...] = tot
        grads = [tot_ref[0:CONV_K, pl.ds(off, CONV_COLS)]]
        grads += [tot[row:row + 1, o:o + n] for row, o, n in SMALL_FIELDS]
        for k, g in enumerate(grads):
            w_ref, m_ref, v_ref = ins[3 * k:3 * k + 3]
            full = (0,) if k == 0 else (Ellipsis,)
            d, nm, nv = _adamw_math(w_ref[full], g, m_ref[full], v_ref[full])
            for o_ref, val in zip(outs[1 + 4 * k:5 + 4 * k], (g, d, nm, nv)):
                o_ref[full] = val

    args = [conv_w, m_conv_w, v_conv_w]
    for w, m, v in zip(params, moms, vars_):
        args += [w, m, v]
    shapes = [jax.ShapeDtypeStruct((1, 1), F32)] + [jax.ShapeDtypeStruct(conv_w.shape, F32)] * 4
    for w in params:
        shapes += [jax.ShapeDtypeStruct(w.shape, F32)] * 4
    vmem = pl.BlockSpec(memory_space=pltpu.VMEM)
    return pl.pallas_call(
        body, name="adamw_small",
        grid_spec=pltpu.PrefetchScalarGridSpec(
            num_scalar_prefetch=1, grid=(1,),
            in_specs=[pl.BlockSpec(slots.shape, lambda i, chip: (0, 0, 0))] + [vmem] * len(args),
            out_specs=[vmem] * len(shapes), scratch_shapes=[pltpu.VMEM((8, K_SMALL), F32)]),
        out_shape=shapes, compiler_params=_cparams(),
    )(chip, slots, *args)


def kernel(x, positions, w_in, conv_w, conv_b, dt_bias, a_log, d_skip, ssd_norm_w, attn_sinks, w_out, ln_g, ln_b, loss_target, m_w_in, m_conv_w, m_conv_b, m_dt_bias, m_a_log, m_d_skip, m_ssd_norm_w, m_attn_sinks, m_w_out, m_ln_g, m_ln_b, v_w_in, v_conv_w, v_conv_b, v_dt_bias, v_a_log, v_d_skip, v_ssd_norm_w, v_attn_sinks, v_w_out, v_ln_g, v_ln_b):
    mx, my, mc = _mesh_pos()
    chip = 2 * mx + my
    L = x.shape[1]

    conv_w_s8 = jnp.pad(conv_w[0], ((0, 8 - CONV_K), (0, 0)))
    pad_rows = ((0, SLAB_ROWS - W_IN_COLS), (0, 0))
    w_in_t = w_in[0].T
    w_in_b, w_out_b = jnp.pad(_bf(w_in_t), pad_rows), _bf(w_out[0])
    ag_in, ag_cw = _gather_weights(w_in_b, conv_w_s8)
    started = _gather_w_out_start(w_out_b, ag_cw)
    own = (jnp.arange(N_CHIPS) == chip)[:, None, None]

    def get_w_out(after):
        landed = _gather_w_out_wait(started[0:6], started[6], started[7], after)
        return jnp.where(own, w_out_b[None], landed).reshape(D_MIX, D_MODEL)

    w_full = jnp.concatenate([ag_in[j, 0:W_IN_COLS] for j in range(N_CHIPS)], axis=0)
    w = jnp.concatenate([
        w_full[O_Z:O_Z + D_SSD], w_full[O_G:O_G + D_ATT], w_full[O_Q:O_Q + D_ATT],
        w_full[O_XBC:O_XBC + D_XBC], w_full[O_K:O_K + 2 * D_KV], w_full[O_DT:O_DT + SSD_HEADS],
        jnp.zeros((DT_PAD - SSD_HEADS, D_MODEL), BF16)], axis=0)
    conv_w_full = jnp.concatenate([ag_cw[j, 0:CONV_K] for j in range(N_CHIPS)], axis=1)

    loss_part, gx_args, gw_in, w_out_red, small = _local_step(
        x[0], positions[0].reshape(L, 1), loss_target[0], w, get_w_out, started[8][0:1, :], conv_w_full,
        conv_b, dt_bias, a_log, d_skip, ssd_norm_w, attn_sinks, ln_g, ln_b)

    packed = _pack_small(small["conv_w"], [small[n] for n in SMALL_NAMES], loss_part)
    core_id = mc.reshape(1).astype(jnp.int32)
    chip_id = chip.reshape(1).astype(jnp.int32)
    ids = jnp.stack([4 * mx + 2 * my + mc, chip]).astype(jnp.int32)
    slabs = jnp.stack([jnp.pad(gw_in[W_IN_COLS * j:W_IN_COLS * (j + 1)], pad_rows) for j in range(N_CHIPS)])
    w_in_red = _pair_start(slabs, packed)
    grad_x = _grad_x(*gx_args, w_in_red[10], 0)
    gw_in_slabs, recv_in = _pair_wait(w_in_red[0:8], w_in_red[8], w_in_red[9], grad_x[0:8, 0:128])
    s_in = _pair_add(gw_in_slabs, recv_in, core_id, "pair_add_in")
    chip_red = _chip_start(s_in, packed)
    grad_x = _grad_x(*gx_args, chip_red[8], 1, grad_x)
    own_slabs, landed = _reduce_w_out_wait(w_out_red[0:14], w_out_red[14], w_out_red[15], grad_x)
    out_t = _adamw_sum8(w_out[0], own_slabs, landed, m_w_out[0], v_w_out[0], ids, "adamw_w_out")
    d_w_out, nm_w_out, nv_w_out, g_w_out = [a[None] for a in out_t]
    s_in, r_in = _chip_wait(chip_red[0:6], chip_red[6], chip_red[7], out_t[0])
    h_in = _chip_add(s_in, r_in, chip_id, "chip_add_in")
    sib_in, slots = _pair_share(h_in, packed)

    to_rows = lambda a: jnp.transpose(a, (2, 0, 1))
    in_t = _adamw_rows(to_rows(w_in), h_in, sib_in, to_rows(m_w_in), to_rows(v_w_in), core_id, "adamw_w_in")
    d_w_in, nm_w_in, nv_w_in, g_w_in = [jnp.transpose(a, (1, 2, 0)) for a in in_t]

    params = dict(conv_b=conv_b, ssd_norm_w=ssd_norm_w, ln_g=ln_g, ln_b=ln_b, dt_bias=dt_bias, a_log=a_log,
                  d_skip=d_skip, attn_sinks=attn_sinks)
    moms = dict(conv_b=m_conv_b, ssd_norm_w=m_ssd_norm_w, ln_g=m_ln_g, ln_b=m_ln_b, dt_bias=m_dt_bias, a_log=m_a_log,
                d_skip=m_d_skip, attn_sinks=m_attn_sinks)
    vars_ = dict(conv_b=v_conv_b, ssd_norm_w=v_ssd_norm_w, ln_g=v_ln_g, ln_b=v_ln_b, dt_bias=v_dt_bias, a_log=v_a_log,
                 d_skip=v_d_skip, attn_sinks=v_attn_sinks)
    res = _adamw_small(slots, chip_id, conv_w, m_conv_w, v_conv_w, [params[n] for n in SMALL_NAMES],
                       [moms[n] for n in SMALL_NAMES], [vars_[n] for n in SMALL_NAMES])
    loss = res[0][0, 0]
    grads, delta, new_m, new_v = {}, {}, {}, {}
    for k, n in enumerate(("conv_w",) + SMALL_NAMES):
        grads[n], delta[n], new_m[n], new_v[n] = res[1 + 4 * k:5 + 4 * k]
    for dd, a_in, a_out in ((grads, g_w_in, g_w_out), (delta, d_w_in, d_w_out), (new_m, nm_w_in, nm_w_out),
                            (new_v, nv_w_in, nv_w_out)):
        dd["w_in"] = a_in
        dd["w_out"] = a_out
    order = ("w_in", "conv_w", "conv_b", "dt_bias", "a_log", "d_skip", "ssd_norm_w", "attn_sinks", "w_out", "ln_g", "ln_b")
    return (loss, grad_x[None], *[grads[n] for n in order], *[delta[n] for n in order], *[new_m[n] for n in order],
            *[new_v[n] for n in order])
```

```python
import numpy as np
import jax
import jax.numpy as jnp
from jax import lax
from jax.experimental import pallas as pl
from jax.experimental.pallas import tpu as pltpu

F32 = jnp.float32
BF16 = jnp.bfloat16
MESH = pl.DeviceIdType.MESH

D_MODEL = 1024
D_SSD = 1024
D_ATT = 1024
D_MIX = 2048
SSD_HEADS = 16
SSD_P = 64
SSD_GROUPS = 2
SSD_R = 8
SSD_N = 128
D_BC = 256
D_XBC = 1536
CONV_K = 4
CHUNK = 128
ATT_HD = 64
ATT_QH = 16
ATT_KVH = 4
ATT_R = 4
D_KV = 256
WINDOW = 128
ROPE_THETA = 500000.0
ROPE_DIM = 16
ALPHA = 2.0 ** 0.25
LN_EPS = 1e-5
RMS_EPS = 1e-5
D_IN_PROJ = 5136
O_Z, O_XBC, O_DT, O_Q, O_K, O_V, O_G = 0, 1024, 2560, 2576, 3600, 3856, 4112
P_Z, P_G, P_Q, P_XBC, P_KV, P_DT, P_END = 0, 1024, 2048, 3072, 4608, 5120, 5248
DT_PAD = 128
N_CHIPS = 4
W_IN_COLS = D_IN_PROJ // N_CHIPS
SLAB_ROWS = 1312
W_OUT_ROWS = D_MIX // N_CHIPS
CONV_COLS = D_XBC // N_CHIPS

ADAM_LR = 0.001
ADAM_B1 = 0.9
ADAM_B2 = 0.999
ADAM_EPS = 1e-08
ADAM_WD = 0.01
ADAM_STEP = 10

VMEM_LIMIT = 56 * 1024 * 1024
ROW_TILE = 512
NEG_BIG = -1e30
HI = lax.Precision.HIGHEST


def _cparams(sem=None, **kw):
    if sem is not None:
        kw["dimension_semantics"] = sem
    return pltpu.CompilerParams(vmem_limit_bytes=VMEM_LIMIT, **kw)


def _dot(a, b):
    return jnp.dot(a, b, preferred_element_type=F32)


def _dot_nt(a, b):
    return lax.dot_general(a, b, (((1,), (1,)), ((), ())), preferred_element_type=F32)


def _dot_tn(a, b):
    return lax.dot_general(a, b, (((0,), (0,)), ((), ())), preferred_element_type=F32)


def _bf(a):
    return a.astype(BF16)


def _iota2(shape, dim):
    return lax.broadcasted_iota(jnp.int32, shape, dim)


def _to_rows(col):
    k = col.shape[1]
    eye = (_iota2((k, k), 0) == _iota2((k, k), 1)).astype(F32)
    return lax.dot_general(eye, col, (((1,), (1,)), ((), ())), preferred_element_type=F32, precision=HI)


def _to_cols(row):
    n = row.shape[1]
    eye = (_iota2((n, n), 0) == _iota2((n, n), 1)).astype(F32)
    return lax.dot_general(eye, row, (((1,), (1,)), ((), ())), preferred_element_type=F32, precision=HI)


def _sigmoid(x):
    return jax.nn.sigmoid(x)


def _in_proj(x, w, pos, inv):
    L = x.shape[0]
    tm = ROW_TILE
    widths = (D_SSD, D_ATT, D_ATT, D_XBC, 2 * D_KV, DT_PAD)

    def body(x_ref, w_ref, pos_ref, inv_ref, z_ref, g_ref, q_ref, xbc_ref, kv_ref, dt_ref, xb_ref):
        xb = _bf(x_ref[...])
        xb_ref[...] = xb
        tabs = _rope_tables(pos_ref, inv_ref)
        q_ref[...] = _bf(_rope(_dot_nt(xb, w_ref[P_Q:P_Q + D_ATT, :]), tabs))
        kv_ref[:, 0:D_KV] = _bf(_rope(_dot_nt(xb, w_ref[P_KV:P_KV + D_KV, :]), tabs))
        kv_ref[:, D_KV:2 * D_KV] = _bf(_dot_nt(xb, w_ref[P_KV + D_KV:P_KV + 2 * D_KV, :]))
        for o_ref, off, wd in zip((z_ref, g_ref, xbc_ref, dt_ref), (P_Z, P_G, P_XBC, P_DT), (D_SSD, D_ATT, D_XBC, DT_PAD)):
            o_ref[...] = _dot_nt(xb, w_ref[off:off + wd, :])

    row = lambda wd: pl.BlockSpec((tm, wd), lambda i: (i, 0))
    return pl.pallas_call(
        body, name="in_proj", grid=(L // tm,),
        in_specs=[row(D_MODEL), pl.BlockSpec((P_END, D_MODEL), lambda i: (0, 0), pipeline_mode=pl.Buffered(1)), row(1),
                  pl.BlockSpec((1, 2 * ATT_HD), lambda i: (0, 0))],
        out_specs=[row(wd) for wd in widths] + [row(D_MODEL)],
        out_shape=[jax.ShapeDtypeStruct((L, wd), dt) for wd, dt in zip(widths, (F32, F32, BF16, F32, BF16, F32))]
        + [jax.ShapeDtypeStruct((L, D_MODEL), BF16)],
        compiler_params=_cparams(("parallel",)),
    )(x, w, pos, inv)


def _matmuls_tn(a_list, b, name):
    K, N = b.shape
    tk = min(K, 1024)
    n = len(a_list)

    def body(*refs):
        b_ref = refs[n]

        @pl.when(pl.program_id(0) == 0)
        def _():
            for o_ref in refs[n + 1:]:
                o_ref[...] = jnp.zeros_like(o_ref)

        bb = _bf(b_ref[...])
        for a_ref, o_ref in zip(refs[:n], refs[n + 1:]):
            o_ref[...] += _dot_tn(_bf(a_ref[...]), bb)

    return pl.pallas_call(
        body, name=name, grid=(K // tk,),
        in_specs=[pl.BlockSpec((tk, a.shape[1]), lambda k: (k, 0)) for a in a_list] + [pl.BlockSpec((tk, N), lambda k: (k, 0))],
        out_specs=[pl.BlockSpec((a.shape[1], N), lambda k: (0, 0)) for a in a_list],
        out_shape=[jax.ShapeDtypeStruct((a.shape[1], N), F32) for a in a_list],
        compiler_params=_cparams(("arbitrary",)),
    )(*a_list, b)


def _grad_x(dr, dz, dg, dq, dxbc, dkv, ddt, w, after, part, prev=None):
    L = dr.shape[0]
    tm = min(ROW_TILE, L // 4)
    first = L // (4 * tm)
    n = first if part == 0 else L // tm - first
    widths = (D_SSD, D_ATT, D_ATT, D_XBC, 2 * D_KV, DT_PAD)
    offs = (P_Z, P_G, P_Q, P_XBC, P_KV, P_DT)

    def body(dr_ref, dz_ref, dg_ref, dq_ref, dxbc_ref, dkv_ref, ddt_ref, w_ref, after_ref, *rest):
        o_ref = rest[-1]
        acc = ALPHA * dr_ref[...]
        for p_ref, off, wd in zip((dz_ref, dg_ref, dq_ref, dxbc_ref, dkv_ref, ddt_ref), offs, widths):
            acc = acc + _dot(_bf(p_ref[...]), w_ref[off:off + wd, :])
        o_ref[...] = acc

    row = lambda wd: pl.BlockSpec((tm, wd), lambda i: (i + part * first, 0))
    ins = [dr, dz, dg, dq, dxbc, dkv, ddt, w, after]
    specs = ([row(D_MODEL)] + [row(wd) for wd in widths]
             + [pl.BlockSpec((P_END, D_MODEL), lambda i: (0, 0), pipeline_mode=pl.Buffered(1)),
                pl.BlockSpec((8, 128), lambda i: (0, 0))])
    if prev is not None:
        ins.append(prev)
        specs.append(pl.BlockSpec(memory_space=pl.ANY))
    return pl.pallas_call(
        body, name="grad_x_%d" % part, grid=(n,),
        in_specs=specs, out_specs=row(D_MODEL),
        out_shape=jax.ShapeDtypeStruct((L, D_MODEL), F32),
        input_output_aliases={} if prev is None else {len(ins) - 1: 0},
        compiler_params=_cparams(("parallel",)),
    )(*ins)


HALO = 16


def _shift_matrix(offsets):
    n = CHUNK + HALO
    m = np.zeros((len(offsets) * CHUNK, 2 * n), np.float32)
    for k, off in enumerate(offsets):
        t = np.arange(CHUNK)
        m[k * CHUNK + t, t + off] = 1.0
        m[k * CHUNK + t, n + t + off] = 1.0
    return jnp.asarray(m, BF16)


def _shifted_rows(first_part, second_part, smat_ref):
    h1, l1 = _hi_lo(first_part)
    h2, l2 = _hi_lo(second_part)
    sh = _dot(smat_ref[...], jnp.concatenate([h1, h2, l1, l2], axis=0))
    return sh[0:CHUNK], sh[CHUNK:2 * CHUNK], sh[2 * CHUNK:3 * CHUNK]


def _ssd_chunk_pre(first, xbc_ref, tail_ref, dt_ref, cw_ref, cb_ref, dtb_ref, alog_ref, smat_ref=None, ext=None):
    tail = jnp.where(first, 0.0, tail_ref[...])
    x = xbc_ref[...]
    if ext is None:
        taps = _shifted_rows(tail, x, smat_ref) + (x,)
    else:
        ext[0:HALO, :] = tail
        ext[HALO:HALO + CHUNK, :] = x
        taps = tuple(ext[pl.ds(HALO - (CONV_K - 1) + k, CHUNK), :] for k in range(CONV_K - 1)) + (x,)
    u = cb_ref[...] + cw_ref[0:1, :] * taps[0]
    for k in range(1, CONV_K):
        u = u + cw_ref[k:k + 1, :] * taps[k]
    sig = _sigmoid(u)
    xbc = u * sig
    dtraw = dt_ref[:, 0:SSD_HEADS] + dtb_ref[...]
    dt = jax.nn.softplus(dtraw)
    A = -jnp.exp(alog_ref[...])
    a = dt * A
    tril = (_iota2((CHUNK, CHUNK), 0) >= _iota2((CHUNK, CHUNK), 1)).astype(F32)
    acs = jnp.dot(tril, a, preferred_element_type=F32, precision=HI)
    acs_row = _to_rows(acs)
    return u, sig, xbc, dtraw, dt, A, acs, acs_row, taps


def _head_expander():
    return (_iota2((SSD_HEADS, D_SSD), 1) // SSD_P == _iota2((SSD_HEADS, D_SSD), 0)).astype(BF16)


def _hi_lo(x):
    hi = _bf(x)
    return hi, _bf(x - hi.astype(F32))


def _expand(v, e):
    hi, lo = _hi_lo(v)
    return _dot(hi, e) + _dot(lo, e)


def _headsum(t, e):
    m = t.shape[0]
    if m < 8:
        t = jnp.broadcast_to(t[0:1], (8, t.shape[1]))
    hi, lo = _hi_lo(t)
    return (_dot_nt(hi, e) + _dot_nt(lo, e))[0:m]


def _ssd_decays(dt, acs, dsk_ref, e):
    alast = acs[CHUNK - 1:CHUNK, :]
    stk = jnp.concatenate([dt, jnp.exp(acs), jnp.exp(alast - acs),
                           jnp.broadcast_to(jnp.exp(alast), (8, SSD_HEADS)),
                           jnp.broadcast_to(dsk_ref[...], (8, SSD_HEADS))], axis=0)
    ex = _expand(stk, e)
    return (ex[0:CHUNK], ex[CHUNK:2 * CHUNK], ex[2 * CHUNK:3 * CHUNK], ex[3 * CHUNK:3 * CHUNK + 1],
            ex[3 * CHUNK + 8:3 * CHUNK + 9])


def _ssd_fwd(z, xbc, dtp, conv_w, conv_b, dt_bias, a_log, d_skip, norm_w):
    L = z.shape[0]
    nc = L // CHUNK
    half = D_SSD // SSD_GROUPS

    def body(z_ref, xbc_ref, tail_ref, dt_ref, cw_ref, cb_ref, dtb_ref, alog_ref, dsk_ref, nw_ref,
             y_ref, ypre_ref, prev_ref, state, ybuf, mbuf, ext):
        c = pl.program_id(0)

        @pl.when(c == 0)
        def _():
            state[...] = jnp.zeros_like(state)

        u, sig, xbcv, dtraw, dt, A, acs, acs_row, _ = _ssd_chunk_pre(
            c == 0, xbc_ref, tail_ref, dt_ref, cw_ref, cb_ref, dtb_ref, alog_ref, ext=ext)
        e = _head_expander()
        dtE, eacsE, dsdE, ealE, dskE = _ssd_decays(dt, acs, dsk_ref, e)
        xs = xbcv[:, 0:D_SSD]
        X = xs * dtE
        prev_ref[0] = state[...]
        causal = _iota2((CHUNK, CHUNK), 0) >= _iota2((CHUNK, CHUNK), 1)
        for g in range(SSD_GROUPS):
            gs = slice(half * g, half * (g + 1))
            Bg = _bf(xbcv[:, D_SSD + SSD_N * g:D_SSD + SSD_N * (g + 1)])
            Cg = _bf(xbcv[:, D_SSD + D_BC + SSD_N * g:D_SSD + D_BC + SSD_N * (g + 1)])
            cb = _dot_nt(Cg, Bg)
            for r in range(SSD_R):
                h = g * SSD_R + r
                seg = acs[:, h:h + 1] - acs_row[h:h + 1, :]
                mbuf[h] = _bf(cb * jnp.where(causal, jnp.exp(jnp.where(causal, seg, 0.0)), 0.0))
            st = state[:, gs]
            ybuf[:, gs] = _dot(Cg, _bf(st)) * eacsE[:, gs] + dskE[:, gs] * xs[:, gs]
            state[:, gs] = st * ealE[:, gs] + _dot_tn(Bg, _bf(X[:, gs] * dsdE[:, gs]))
        Xb = _bf(X)
        for h in range(SSD_HEADS):
            hs = slice(SSD_P * h, SSD_P * (h + 1))
            ybuf[:, hs] += _dot(mbuf[h], Xb[:, hs])
        y = ybuf[...]
        ypre_ref[...] = y
        zv = z_ref[...]
        yf = y * (zv * _sigmoid(zv))
        for g in range(SSD_GROUPS):
            gs = slice(half * g, half * (g + 1))
            yg = yf[:, gs]
            ms = jnp.mean(yg * yg, axis=-1, keepdims=True)
            y_ref[:, gs] = _bf(yg * lax.rsqrt(ms + RMS_EPS) * nw_ref[:, gs])

    full = lambda shape: pl.BlockSpec(shape, lambda c: (0, 0))
    return pl.pallas_call(
        body, name="ssd_fwd", grid=(nc,),
        in_specs=[
            pl.BlockSpec((CHUNK, D_SSD), lambda c: (c, 0)),
            pl.BlockSpec((CHUNK, D_XBC), lambda c: (c, 0)),
            pl.BlockSpec((HALO, D_XBC), lambda c: (jnp.maximum(c * (CHUNK // HALO) - 1, 0), 0)),
            pl.BlockSpec((CHUNK, DT_PAD), lambda c: (c, 0)),
            full((CONV_K, D_XBC)), full((1, D_XBC)), full((1, SSD_HEADS)), full((1, SSD_HEADS)), full((1, SSD_HEADS)),
            full((1, D_SSD)),
        ],
        out_specs=[
            pl.BlockSpec((CHUNK, D_SSD), lambda c: (c, 0)),
            pl.BlockSpec((CHUNK, D_SSD), lambda c: (c, 0)),
            pl.BlockSpec((1, SSD_N, D_SSD), lambda c: (c, 0, 0)),
        ],
        out_shape=[
            jax.ShapeDtypeStruct((L, D_SSD), BF16),
            jax.ShapeDtypeStruct((L, D_SSD), F32),
            jax.ShapeDtypeStruct((nc, SSD_N, D_SSD), F32),
        ],
        scratch_shapes=[
            pltpu.VMEM((SSD_N, D_SSD), F32),
            pltpu.VMEM((CHUNK, D_SSD), F32),
            pltpu.VMEM((SSD_HEADS, CHUNK, CHUNK), BF16),
            pltpu.VMEM((CHUNK + HALO, D_XBC), F32),
        ],
        compiler_params=_cparams(("arbitrary",)),
    )(z, xbc, xbc, dtp, conv_w, conv_b, dt_bias, a_log, d_skip, norm_w)


def _ssd_bwd(dy, z, ypre, xbc, dtp, prev, conv_w, conv_b, dt_bias, a_log, d_skip, norm_w):
    L = z.shape[0]
    nc = L // CHUNK
    half = D_SSD // SSD_GROUPS

    def body(dy_ref, z_ref, ypre_ref, xbc_ref, tail_ref, dt_ref, prev_ref, cw_ref, cb_ref, dtb_ref, alog_ref, dsk_ref,
             nw_ref, smat_ref, smat2_ref, dz_ref, dxbc_ref, ddt_ref, gcw_ref, gcb_ref, gdtb_ref, galog_ref, gdsk_ref,
             gnw_ref, dstate, dhead, dpost, yobuf, bdbuf, lmbuf, dmbuf, cbbuf):
        i = pl.program_id(0)
        c = nc - 1 - i

        @pl.when(i == 0)
        def _():
            dstate[...] = jnp.zeros_like(dstate)
            dhead[...] = jnp.zeros_like(dhead)
            gcw_ref[...] = jnp.zeros_like(gcw_ref)
            gcb_ref[...] = jnp.zeros_like(gcb_ref)
            gdtb_ref[...] = jnp.zeros_like(gdtb_ref)
            galog_ref[...] = jnp.zeros_like(galog_ref)
            gdsk_ref[...] = jnp.zeros_like(gdsk_ref)
            gnw_ref[...] = jnp.zeros_like(gnw_ref)

        u, sig, xbcv, dtraw, dt, A, acs, acs_row, taps = _ssd_chunk_pre(
            c == 0, xbc_ref, tail_ref, dt_ref, cw_ref, cb_ref, dtb_ref, alog_ref, smat_ref)
        e = _head_expander()
        dtE, eacsE, dsdE, ealE, dskE = _ssd_decays(dt, acs, dsk_ref, e)
        alast = acs[CHUNK - 1:CHUNK, :]
        xs = xbcv[:, 0:D_SSD]
        X = xs * dtE
        Xb = _bf(X)

        zv = z_ref[...]
        ypre = ypre_ref[...]
        dyn = dy_ref[...]
        sz = _sigmoid(zv)
        silu_z = zv * sz
        yf = ypre * silu_z
        dyf_parts = []
        for g in range(SSD_GROUPS):
            gs = slice(half * g, half * (g + 1))
            yg = yf[:, gs]
            rstd = lax.rsqrt(jnp.mean(yg * yg, axis=-1, keepdims=True) + RMS_EPS)
            dout = dyn[:, gs]
            gnw_ref[:, gs] += jnp.sum(dout * yg * rstd, axis=0, keepdims=True)
            dyhat = dout * nw_ref[:, gs]
            dyf_parts.append(rstd * (dyhat - yg * (rstd * rstd) * jnp.mean(dyhat * yg, axis=-1, keepdims=True)))
        dyf = jnp.concatenate(dyf_parts, axis=1)
        dz_ref[...] = _bf(dyf * ypre * (sz * (1.0 + zv * (1.0 - sz))))
        dyp = dyf * silu_z
        dyb = _bf(dyp)
        G = dyp * eacsE

        causal = _iota2((CHUNK, CHUNK), 0) >= _iota2((CHUNK, CHUNK), 1)
        ST = prev_ref[0]
        dST = dstate[...]
        for g in range(SSD_GROUPS):
            gs = slice(half * g, half * (g + 1))
            bs = slice(D_SSD + SSD_N * g, D_SSD + SSD_N * (g + 1))
            cs = slice(D_SSD + D_BC + SSD_N * g, D_SSD + D_BC + SSD_N * (g + 1))
            Bg = _bf(xbcv[:, bs])
            Cg = _bf(xbcv[:, cs])
            Gb = _bf(G[:, gs])
            STb = _bf(ST[:, gs])
            dSTb = _bf(dST[:, gs])
            dstate[:, gs] = dST[:, gs] * ealE[:, gs] + _dot_tn(Cg, Gb)
            yobuf[:, gs] = _dot(Cg, STb) * eacsE[:, gs]
            bdbuf[:, gs] = _dot(Bg, dSTb)
            dpost[:, cs] = _dot_nt(Gb, STb)
            dpost[:, bs] = _dot_nt(_bf(X[:, gs] * dsdE[:, gs]), dSTb)
            cbbuf[g] = _dot_nt(Cg, Bg)
            for r in range(SSD_R):
                h = g * SSD_R + r
                seg = acs[:, h:h + 1] - acs_row[h:h + 1, :]
                lmbuf[h] = jnp.where(causal, jnp.exp(jnp.where(causal, seg, 0.0)), 0.0)
        for h in range(SSD_HEADS):
            hs = slice(SSD_P * h, SSD_P * (h + 1))
            Mb = _bf(cbbuf[h // SSD_R] * lmbuf[h])
            dmbuf[h] = _dot_nt(dyb[:, hs], Xb[:, hs])
            dpost[:, hs] = _dot_tn(Mb, dyb[:, hs])
        lane16 = _iota2((1, SSD_HEADS), 1)
        sub16 = _iota2((SSD_HEADS, 1), 0)
        dacs_col = jnp.zeros((CHUNK, SSD_HEADS), F32)
        dacs_row = jnp.zeros((SSD_HEADS, CHUNK), F32)
        for g in range(SSD_GROUPS):
            bs = slice(D_SSD + SSD_N * g, D_SSD + SSD_N * (g + 1))
            cs = slice(D_SSD + D_BC + SSD_N * g, D_SSD + D_BC + SSD_N * (g + 1))
            cb = cbbuf[g]
            dcb = jnp.zeros((CHUNK, CHUNK), F32)
            for r in range(SSD_R):
                h = g * SSD_R + r
                dM = dmbuf[h]
                Lm = lmbuf[h]
                dcb = dcb + dM * Lm
                dseg = dM * (cb * Lm)
                dacs_col = dacs_col + jnp.sum(dseg, axis=-1, keepdims=True) * (lane16 == h).astype(F32)
                dacs_row = dacs_row - jnp.sum(dseg, axis=0, keepdims=True) * (sub16 == h).astype(F32)
            dcbb = _bf(dcb)
            dpost[:, bs] += _dot_tn(dcbb, _bf(xbcv[:, cs]))
            dpost[:, cs] += _dot(dcbb, _bf(xbcv[:, bs]))

        BD = bdbuf[...]
        dX = dpost[:, 0:D_SSD] + dsdE * BD
        dsd = jnp.exp(alast - acs)
        T = _headsum(X * BD, e) * dsd
        dalast = jnp.sum(T, axis=0, keepdims=True) + _headsum(
            jnp.sum(dST * ST, axis=0, keepdims=True), e) * jnp.exp(alast)
        is_last = (_iota2((CHUNK, 1), 0) == CHUNK - 1).astype(F32)
        dacs = dacs_col + _to_cols(dacs_row) + _headsum(dyp * yobuf[...], e) - T + is_last * dalast
        triu = (_iota2((CHUNK, CHUNK), 0) <= _iota2((CHUNK, CHUNK), 1)).astype(F32)
        da = jnp.dot(triu, dacs, preferred_element_type=F32, precision=HI)
        ddt_tot = _headsum(dX * xs, e) + da * A
        galog_ref[...] += jnp.sum(da * dt, axis=0, keepdims=True) * A
        ddtraw = ddt_tot * _sigmoid(dtraw)
        gdtb_ref[...] += jnp.sum(ddtraw, axis=0, keepdims=True)
        gdsk_ref[...] += _headsum(jnp.sum(dyp * xs, axis=0, keepdims=True), e)
        ddt_ref[...] = jnp.zeros_like(ddt_ref)
        ddt_ref[:, 0:SSD_HEADS] = ddtraw
        dpost[:, 0:D_SSD] = dX * dtE + dskE * dyp

        dconv = dpost[...] * (sig * (1.0 + u * (1.0 - sig)))
        gcb_ref[...] += jnp.sum(dconv, axis=0, keepdims=True)
        for k in range(CONV_K):
            gcw_ref[k:k + 1, :] += jnp.sum(dconv * taps[k], axis=0, keepdims=True)
        later = _shifted_rows(dconv, dhead[...], smat2_ref)
        dx = cw_ref[CONV_K - 1:CONV_K, :] * dconv
        for k in range(CONV_K - 1):
            dx = dx + cw_ref[k:k + 1, :] * later[k]
        dxbc_ref[...] = _bf(dx)
        dhead[...] = dconv[0:HALO, :]

    full = lambda shape: pl.BlockSpec(shape, lambda i: (0, 0))
    rev = lambda wd: pl.BlockSpec((CHUNK, wd), lambda i: (nc - 1 - i, 0))
    return pl.pallas_call(
        body, name="ssd_bwd", grid=(nc,),
        in_specs=[
            rev(D_SSD), rev(D_SSD), rev(D_SSD), rev(D_XBC),
            pl.BlockSpec((HALO, D_XBC), lambda i: (jnp.maximum((nc - 1 - i) * (CHUNK // HALO) - 1, 0), 0)),
            rev(DT_PAD),
            pl.BlockSpec((1, SSD_N, D_SSD), lambda i: (nc - 1 - i, 0, 0)),
            full((CONV_K, D_XBC)), full((1, D_XBC)), full((1, SSD_HEADS)), full((1, SSD_HEADS)), full((1, SSD_HEADS)),
            full((1, D_SSD)), full((3 * CHUNK, 2 * (CHUNK + HALO))), full((3 * CHUNK, 2 * (CHUNK + HALO))),
        ],
        out_specs=[
            rev(D_SSD), rev(D_XBC), rev(DT_PAD),
            full((CONV_K, D_XBC)), full((1, D_XBC)), full((1, SSD_HEADS)), full((1, SSD_HEADS)), full((1, SSD_HEADS)),
            full((1, D_SSD)),
        ],
        out_shape=[
            jax.ShapeDtypeStruct((L, D_SSD), BF16), jax.ShapeDtypeStruct((L, D_XBC), BF16),
            jax.ShapeDtypeStruct((L, DT_PAD), F32),
            jax.ShapeDtypeStruct((CONV_K, D_XBC), F32), jax.ShapeDtypeStruct((1, D_XBC), F32),
            jax.ShapeDtypeStruct((1, SSD_HEADS), F32), jax.ShapeDtypeStruct((1, SSD_HEADS), F32),
            jax.ShapeDtypeStruct((1, SSD_HEADS), F32), jax.ShapeDtypeStruct((1, D_SSD), F32),
        ],
        scratch_shapes=[
            pltpu.VMEM((SSD_N, D_SSD), F32),
            pltpu.VMEM((HALO, D_XBC), F32),
            pltpu.VMEM((CHUNK, D_XBC), F32),
            pltpu.VMEM((CHUNK, D_SSD), F32),
            pltpu.VMEM((CHUNK, D_SSD), F32),
            pltpu.VMEM((SSD_HEADS, CHUNK, CHUNK), F32),
            pltpu.VMEM((SSD_HEADS, CHUNK, CHUNK), F32),
            pltpu.VMEM((SSD_GROUPS, CHUNK, CHUNK), F32),
        ],
        compiler_params=_cparams(("arbitrary",)),
    )(dy, z, ypre, xbc, xbc, dtp, prev, conv_w, conv_b, dt_bias, a_log, d_skip, norm_w, _shift_matrix((13, 14, 15)),
      _shift_matrix((3, 2, 1)))


def _rope_tables(pos_ref, inv_ref):
    ang = pos_ref[...].astype(F32) * inv_ref[...]
    d = _iota2((1, 2 * ATT_HD), 1) % ATT_HD
    s = jnp.sin(ang)
    return jnp.cos(ang), jnp.where(d < ROPE_DIM // 2, -s, 0.0), jnp.where((d >= ROPE_DIM // 2) & (d < ROPE_DIM), s, 0.0)


def _rope(t, tabs):
    c, s1, s2 = tabs
    n = t.shape[1]
    rep = n // c.shape[1]
    return (t * jnp.tile(c, (1, rep)) + pltpu.roll(t, n - ROPE_DIM // 2, 1) * jnp.tile(s1, (1, rep))
            + pltpu.roll(t, ROPE_DIM // 2, 1) * jnp.tile(s2, (1, rep)))


def _rope_t(t, tabs):
    c, s1, s2 = tabs
    n = t.shape[1]
    rep = n // c.shape[1]
    return (t * jnp.tile(c, (1, rep)) + pltpu.roll(t * jnp.tile(s1, (1, rep)), ROPE_DIM // 2, 1)
            + pltpu.roll(t * jnp.tile(s2, (1, rep)), n - ROPE_DIM // 2, 1))


def _stack_heads(t, j):
    return jnp.concatenate([t[:, ATT_HD * (j * ATT_R + r):ATT_HD * (j * ATT_R + r + 1)] for r in range(ATT_R)], axis=0)


def _swa_mask_t(first):
    si = _iota2((2 * WINDOW, ATT_R * WINDOW), 0)
    qi = _iota2((2 * WINDOW, ATT_R * WINDOW), 1) % WINDOW
    band = (si > qi) & (si <= qi + WINDOW)
    return band & (jnp.logical_not(first) | (si >= WINDOW))


def _head_rows(ref, j):
    if ref.shape[0] == 1:
        parts = [jnp.broadcast_to(ref[:, j * ATT_R + r:j * ATT_R + r + 1], (1, WINDOW)) for r in range(ATT_R)]
    else:
        parts = [ref[j * ATT_R + r:j * ATT_R + r + 1, :] for r in range(ATT_R)]
    return jnp.concatenate(parts, axis=1)


def _swa_fwd(q, g, kv, sinks):
    L = q.shape[0]
    nb = L // WINDOW
    scale = ATT_HD ** -0.5

    def body(q_ref, g_ref, kvc_ref, kvp_ref, sink_ref, y_ref, o_ref, lse_ref, otbuf):
        n = pl.program_id(0)
        kk = jnp.concatenate([kvp_ref[:, 0:D_KV], kvc_ref[:, 0:D_KV]], axis=0)
        vv = jnp.concatenate([kvp_ref[:, D_KV:2 * D_KV], kvc_ref[:, D_KV:2 * D_KV]], axis=0)
        valid = _swa_mask_t(n == 0)
        qv = q_ref[...]
        for j in range(ATT_KVH):
            js = slice(ATT_HD * j, ATT_HD * (j + 1))
            st = _dot_nt(kk[:, js], _stack_heads(qv, j)) * scale
            st = jnp.where(valid, st, NEG_BIG)
            sink = _head_rows(sink_ref, j)
            m = jnp.maximum(jnp.max(st, axis=0, keepdims=True), sink)
            p = jnp.exp(st - m)
            denom = jnp.sum(p, axis=0, keepdims=True) + jnp.exp(sink - m)
            ot = _dot_tn(vv[:, js], _bf(p)) * (1.0 / denom)
            lse = m + jnp.log(denom)
            for r in range(ATT_R):
                h = j * ATT_R + r
                otbuf[ATT_HD * h:ATT_HD * (h + 1), :] = ot[:, WINDOW * r:WINDOW * (r + 1)]
                lse_ref[h:h + 1, :] = lse[:, WINDOW * r:WINDOW * (r + 1)]
        o = otbuf[...].T
        o_ref[...] = o
        gv = g_ref[...]
        y_ref[...] = _bf(o * (gv * _sigmoid(gv)))

    cur = lambda wd: pl.BlockSpec((WINDOW, wd), lambda n: (n, 0))
    prv = lambda wd: pl.BlockSpec((WINDOW, wd), lambda n: (jnp.maximum(n - 1, 0), 0))
    return pl.pallas_call(
        body, name="swa_fwd", grid=(nb,),
        in_specs=[cur(D_ATT), cur(D_ATT), cur(2 * D_KV), prv(2 * D_KV), pl.BlockSpec((1, ATT_QH), lambda n: (0, 0))],
        out_specs=[cur(D_ATT), cur(D_ATT), pl.BlockSpec((ATT_QH, WINDOW), lambda n: (0, n))],
        out_shape=[jax.ShapeDtypeStruct((L, D_ATT), BF16), jax.ShapeDtypeStruct((L, D_ATT), F32),
                   jax.ShapeDtypeStruct((ATT_QH, L), F32)],
        scratch_shapes=[pltpu.VMEM((D_ATT, WINDOW), F32)],
        compiler_params=_cparams(("parallel",)),
    )(q, g, kv, kv, sinks)


def _swa_bwd(dy, q, g, kv, o, lse, pos, inv, sinks):
    L = q.shape[0]
    nb = L // WINDOW
    scale = ATT_HD ** -0.5

    def body(dy_ref, q_ref, g_ref, kvc_ref, kvp_ref, o_ref, lse_ref, posc_ref, posp_ref, inv_ref, sink_ref,
             dq_ref, dg_ref, dkv_ref, dsink_ref, carry, dqbuf, dkbuf, dvbuf):
        n = pl.program_id(0)

        @pl.when(n == 0)
        def _():
            dsink_ref[...] = jnp.zeros_like(dsink_ref)

        @pl.when(n < nb)
        def _():
            tc = _rope_tables(posc_ref, inv_ref)
            tp = _rope_tables(posp_ref, inv_ref)
            kk = jnp.concatenate([kvp_ref[:, 0:D_KV], kvc_ref[:, 0:D_KV]], axis=0)
            vv = jnp.concatenate([kvp_ref[:, D_KV:2 * D_KV], kvc_ref[:, D_KV:2 * D_KV]], axis=0)
            valid = _swa_mask_t(n == 0)
            qv = q_ref[...]
            gv = g_ref[...]
            sg = _sigmoid(gv)
            dyv = dy_ref[...]
            ov = o_ref[...]
            dg_ref[...] = _bf(dyv * ov * (sg * (1.0 + gv * (1.0 - sg))))
            do = dyv * (gv * sg)
            dod = do * ov
            ones = jnp.ones((8, ATT_HD), BF16)
            lane16 = _iota2((1, ATT_QH), 1)
            dsink = jnp.zeros((1, ATT_QH), F32)
            for j in range(ATT_KVH):
                js = slice(ATT_HD * j, ATT_HD * (j + 1))
                kj = kk[:, js]
                vj = vv[:, js]
                qs = _stack_heads(qv, j)
                dos = _bf(_stack_heads(do, j))
                hi, lo = _hi_lo(_stack_heads(dod, j))
                delta = (_dot_nt(ones, hi) + _dot_nt(ones, lo))[0:1]
                lse = _head_rows(lse_ref, j)
                st = _dot_nt(kj, qs) * scale
                pt = jnp.exp(jnp.where(valid, st, NEG_BIG) - lse)
                dst = _bf(pt * (_dot_nt(vj, dos) - delta))
                dqt = _dot_tn(kj, dst) * scale
                dkbuf[:, js] = _dot(dst, qs) * scale
                dvbuf[:, js] = _dot(_bf(pt), dos)
                sd = jnp.exp(_head_rows(sink_ref, j) - lse) * delta
                for r in range(ATT_R):
                    h = j * ATT_R + r
                    ls = slice(WINDOW * r, WINDOW * (r + 1))
                    dqbuf[ATT_HD * h:ATT_HD * (h + 1), :] = dqt[:, ls]
                    dsink = dsink - jnp.sum(sd[:, ls], axis=1, keepdims=True) * (lane16 == h).astype(F32)
            dsink_ref[...] += dsink
            dq_ref[...] = _bf(_rope_t(dqbuf[...].T, tc))
            dkp = _rope_t(dkbuf[0:WINDOW, :], tp)
            dkc = _rope_t(dkbuf[WINDOW:2 * WINDOW, :], tc)

            @pl.when(n > 0)
            def _():
                dkv_ref[:, 0:D_KV] = _bf(carry[:, 0:D_KV] + dkp)
                dkv_ref[:, D_KV:2 * D_KV] = _bf(carry[:, D_KV:2 * D_KV] + dvbuf[0:WINDOW, :])

            carry[:, 0:D_KV] = dkc
            carry[:, D_KV:2 * D_KV] = dvbuf[WINDOW:2 * WINDOW, :]

        @pl.when(n == nb)
        def _():
            dkv_ref[...] = _bf(carry[...])

    last = nb - 1
    cur = lambda wd: pl.BlockSpec((WINDOW, wd), lambda n: (jnp.minimum(n, last), 0))
    prv = lambda wd: pl.BlockSpec((WINDOW, wd), lambda n: (jnp.maximum(jnp.minimum(n, last) - 1, 0), 0))
    return pl.pallas_call(
        body, name="swa_bwd", grid=(nb + 1,),
        in_specs=[cur(D_ATT), cur(D_ATT), cur(D_ATT), cur(2 * D_KV), prv(2 * D_KV), cur(D_ATT),
                  pl.BlockSpec((ATT_QH, WINDOW), lambda n: (0, jnp.minimum(n, last))), cur(1), prv(1),
                  pl.BlockSpec((1, 2 * ATT_HD), lambda n: (0, 0)), pl.BlockSpec((1, ATT_QH), lambda n: (0, 0))],
        out_specs=[cur(D_ATT), cur(D_ATT),
                   pl.BlockSpec((WINDOW, 2 * D_KV), lambda n: (jnp.maximum(n - 1, 0), 0)),
                   pl.BlockSpec((1, ATT_QH), lambda n: (0, 0))],
        out_shape=[jax.ShapeDtypeStruct((L, D_ATT), BF16), jax.ShapeDtypeStruct((L, D_ATT), BF16),
                   jax.ShapeDtypeStruct((L, 2 * D_KV), BF16), jax.ShapeDtypeStruct((1, ATT_QH), F32)],
        scratch_shapes=[pltpu.VMEM((WINDOW, 2 * D_KV), F32), pltpu.VMEM((D_ATT, WINDOW), F32),
                        pltpu.VMEM((2 * WINDOW, D_KV), F32), pltpu.VMEM((2 * WINDOW, D_KV), F32)],
        compiler_params=_cparams(("arbitrary",)),
    )(dy, q, g, kv, kv, o, lse, pos, pos, inv, sinks)


def _out_ln_loss(y_ssd, y_att, x, target, w_out, ln_g, ln_b):
    L = x.shape[0]
    tm = min(ROW_TILE, L)
    nt = L // tm
    inv_d = 1.0 / D_MODEL

    def body(ys_ref, ya_ref, x_ref, t_ref, w_ref, g_ref, b_ref, dr_ref, dys_ref, dya_ref, loss_ref, gg_ref, gb_ref,
             gwo_ref, acc_ref):
        i = pl.program_id(0)

        @pl.when(i == 0)
        def _():
            loss_ref[...] = jnp.zeros_like(loss_ref)
            gg_ref[...] = jnp.zeros_like(gg_ref)
            gb_ref[...] = jnp.zeros_like(gb_ref)
            acc_ref[...] = jnp.zeros_like(acc_ref)

        halves = [slice(0, tm // 2), slice(tm // 2, tm)]
        hs = [_dot(_bf(ys_ref[rs, :]), w_ref[0:D_SSD, :]) + _dot(_bf(ya_ref[rs, :]), w_ref[D_SSD:D_MIX, :]) for rs in halves]
        gam = g_ref[...]
        for rs, h in zip(halves, hs):
            r = ALPHA * x_ref[rs, :] + h
            mu = jnp.mean(r, axis=-1, keepdims=True)
            xc = r - mu
            rstd = lax.rsqrt(jnp.mean(xc * xc, axis=-1, keepdims=True) + LN_EPS)
            xhat = xc * rstd
            diff = xhat * gam + b_ref[...] - t_ref[rs, :]
            part = jnp.sum(jnp.sum(diff * diff, axis=-1, keepdims=True), axis=0, keepdims=True)
            loss_ref[...] += (0.5 * inv_d) * part
            dout = diff * inv_d
            gg_ref[...] += jnp.sum(dout * xhat, axis=0, keepdims=True)
            gb_ref[...] += jnp.sum(dout, axis=0, keepdims=True)
            dxh = dout * gam
            dr_ref[rs, :] = rstd * (dxh - jnp.mean(dxh, axis=-1, keepdims=True)
                                    - xhat * jnp.mean(dxh * xhat, axis=-1, keepdims=True))
        for rs in halves:
            drh = _bf(dr_ref[rs, :])
            dys_ref[rs, :] = _dot_nt(drh, w_ref[0:D_SSD, :])
            dya_ref[rs, :] = _dot_nt(drh, w_ref[D_SSD:D_MIX, :])
        drb = _bf(dr_ref[...])
        acc_ref[0:D_SSD, :] += _dot_tn(_bf(ys_ref[...]), drb)
        acc_ref[D_SSD:D_MIX, :] += _dot_tn(_bf(ya_ref[...]), drb)

        @pl.when(i == nt - 1)
        def _():
            gwo_ref[...] = _bf(acc_ref[...])

    row = pl.BlockSpec((tm, D_MODEL), lambda i: (i, 0))
    vec = pl.BlockSpec((1, D_MODEL), lambda i: (0, 0))
    return pl.pallas_call(
        body, name="out_ln_loss", grid=(nt,),
        in_specs=[row, row, row, row, pl.BlockSpec((D_MIX, D_MODEL), lambda i: (0, 0), pipeline_mode=pl.Buffered(1)), vec, vec],
        out_specs=[row, row, row, pl.BlockSpec((1, 128), lambda i: (0, 0)), vec, vec,
                   pl.BlockSpec((D_MIX, D_MODEL), lambda i: (0, 0))],
        out_shape=[jax.ShapeDtypeStruct((L, D_MODEL), F32)] * 3 + [jax.ShapeDtypeStruct((1, 128), F32)]
        + [jax.ShapeDtypeStruct((1, D_MODEL), F32)] * 2 + [jax.ShapeDtypeStruct((D_MIX, D_MODEL), BF16)],
        scratch_shapes=[pltpu.VMEM((D_MIX, D_MODEL), F32)],
        compiler_params=_cparams(("arbitrary",)),
    )(y_ssd, y_att, x, target, w_out, ln_g, ln_b)


def _local_step(x, pos, target, w, get_w_out, token, conv_w, conv_b, dt_bias, a_log, d_skip, norm_w, sinks, ln_g, ln_b):
    inv8 = ROPE_THETA ** (-jnp.arange(0, ROPE_DIM, 2, dtype=F32) / ROPE_DIM)
    inv = jnp.tile(jnp.concatenate([inv8, inv8, jnp.zeros((ATT_HD - ROPE_DIM,), F32)]), 2).reshape(1, 2 * ATT_HD)
    inv = inv + token

    z, g, q, xbc, kv, dtp, xb = _in_proj(x, w, pos, inv)
    y_ssd, y_pre, prev = _ssd_fwd(z, xbc, dtp, conv_w, conv_b, dt_bias, a_log, d_skip, norm_w)
    y_att, o, lse = _swa_fwd(q, g, kv, sinks)
    w_out = get_w_out(lse)
    dr, dy_ssd, dy_att, loss, g_ln_g, g_ln_b, gw_out = _out_ln_loss(y_ssd, y_att, x, target, w_out, ln_g, ln_b)
    w_out_red = _reduce_w_out_start(gw_out.reshape(N_CHIPS, W_OUT_ROWS, D_MODEL), loss)
    inv = inv + w_out_red[16][0:1, :]
    dq, dg, dkv, g_sinks = _swa_bwd(dy_att, q, g, kv, o, lse, pos, inv, sinks)
    dz, dxbc, ddt, g_conv_w, g_conv_b, g_dt_bias, g_a_log, g_d_skip, g_norm_w = _ssd_bwd(
        dy_ssd, z, y_pre, xbc, dtp, prev, conv_w, conv_b, dt_bias, a_log, d_skip, norm_w)
    gw_z, gw_g, gw_q = _matmuls_tn([dz, dg, dq], xb, "gw_zgq")
    gw_xbc, gw_kv, gw_dt = _matmuls_tn([dxbc, dkv, ddt], xb, "gw_xbc_kv_dt")
    gw_in = jnp.concatenate([gw_z, gw_xbc, gw_dt[0:SSD_HEADS], gw_q, gw_kv, gw_g], axis=0)
    small = dict(conv_w=g_conv_w, conv_b=g_conv_b, dt_bias=g_dt_bias, a_log=g_a_log, d_skip=g_d_skip,
                 ssd_norm_w=g_norm_w, attn_sinks=g_sinks, ln_g=g_ln_g, ln_b=g_ln_b)
    return loss, (dr, dz, dg, dq, dxbc, dkv, ddt, w), gw_in, w_out_red, small


def _mesh_pos():
    return lax.axis_index("x"), lax.axis_index("y"), lax.axis_index("c")


def _gather_weights(w_in_s, conv_w_s):
    hr = w_in_s.shape[0] // 2
    qa = 336
    quarters = ((0, qa), (qa, hr - qa))

    def body(win_ref, cw_ref, owin_ref, ocw_ref, stage, send_sems, recv_sems, small_send, small_recv, local_sems):
        x, y, c = _mesh_pos()
        me = 2 * x + y
        sibling = (x, y, 1 - c)
        xn, yn, dg = (1 - x, y), (x, 1 - y), (1 - x, 1 - y)
        chips = [xn, yn, dg]
        load = pltpu.make_async_copy(win_ref, stage, local_sems.at[1])
        load.start()
        locals_ = [pltpu.make_async_copy(cw_ref, ocw_ref.at[me], local_sems.at[0])]
        for cp in locals_:
            cp.start()
        started = []

        def piece(ref, chip, half, q):
            off, n = quarters[q]
            return ref.at[2 * chip[0] + chip[1]].at[pl.ds(half * hr + off, n), :]

        def mine(q):
            off, n = quarters[q]
            return win_ref.at[pl.ds(c * hr + off, n), :]

        def copy(src, dst, k, to):
            return pltpu.make_async_remote_copy(src_ref=src, dst_ref=dst, send_sem=send_sems.at[k], recv_sem=recv_sems.at[k],
                                                device_id=to, device_id_type=MESH)

        def go(cp):
            cp.start()
            started.append(cp)

        go(copy(mine(0), piece(owin_ref, (x, y), c, 0), 0, (*xn, c)))
        go(copy(mine(1), piece(owin_ref, (x, y), c, 1), 2, (*yn, c)))
        go(copy(mine(1), piece(owin_ref, (x, y), c, 1), 1, (*xn, c)))
        go(copy(mine(0), piece(owin_ref, (x, y), c, 0), 3, (*yn, c)))
        for j, (px, py) in enumerate(chips):
            cp = pltpu.make_async_remote_copy(
                src_ref=cw_ref, dst_ref=ocw_ref.at[me], send_sem=small_send.at[j], recv_sem=small_recv.at[j],
                device_id=(px, py, c), device_id_type=MESH)
            go(cp)
        load.wait()
        store = pltpu.make_async_copy(stage, owin_ref.at[me], local_sems.at[2])
        store.start()
        locals_.append(store)
        arrivals = [(0, xn, 0, (4, (*yn, c))), (2, yn, 1, (5, (*xn, c))), (1, xn, 1, None), (3, yn, 0, None),
                    (4, dg, 0, None), (5, dg, 1, None)]
        for n, (k, chip, q, onward) in enumerate(arrivals):
            blk = piece(owin_ref, chip, c, q)
            copy(blk, blk, k, sibling).wait_recv()
            if onward is not None:
                go(copy(blk, blk, onward[0], onward[1]))
            go(copy(blk, blk, 6 + n, sibling))
        for n, (k, chip, q, onward) in enumerate(arrivals):
            blk = piece(owin_ref, chip, 1 - c, q)
            copy(blk, blk, 6 + n, sibling).wait_recv()
        for j in range(3):
            pltpu.make_async_remote_copy(
                src_ref=cw_ref, dst_ref=ocw_ref.at[me], send_sem=small_send.at[j], recv_sem=small_recv.at[j],
                device_id=sibling, device_id_type=MESH).wait_recv()
        for cp in started:
            cp.wait_send()
        for cp in locals_:
            cp.wait()

    any_spec = pl.BlockSpec(memory_space=pl.ANY)
    return pl.pallas_call(
        body, name="gather_weights",
        in_specs=[any_spec] * 2, out_specs=[any_spec] * 2,
        out_shape=[jax.ShapeDtypeStruct((N_CHIPS,) + a.shape, a.dtype) for a in (w_in_s, conv_w_s)],
        scratch_shapes=[pltpu.VMEM(w_in_s.shape, w_in_s.dtype),
                        pltpu.SemaphoreType.DMA((12,)), pltpu.SemaphoreType.DMA((12,)),
                        pltpu.SemaphoreType.DMA((3,)), pltpu.SemaphoreType.DMA((3,)), pltpu.SemaphoreType.DMA((3,))],
    )(w_in_s, conv_w_s)


_HBM = pl.BlockSpec(memory_space=pltpu.HBM)
_SEM = pl.BlockSpec(memory_space=pltpu.SEMAPHORE)
_EFFECT = pltpu.SideEffectType.DATAFLOW_SIDE_EFFECTING


def _gather_w_out_start(w_out_s, after):
    def body(src_ref, land_ref, after_ref, s0, s1, s2, r0, r1, r2, src_thru, land_thru, token):
        x, y, c = _mesh_pos()
        me = 2 * x + y
        chips = [(1 - x, y), (x, 1 - y), (1 - x, 1 - y)]
        for (px, py), s, r in zip(chips, (s0, s1, s2), (r0, r1, r2)):
            pltpu.make_async_remote_copy(src_ref=src_ref, dst_ref=land_ref.at[me], send_sem=s, recv_sem=r,
                                         device_id=(px, py, c), device_id_type=MESH).start()
        token[...] = jnp.zeros_like(token)

    sem = pltpu.SemaphoreType.DMA(())
    land = lax.empty((N_CHIPS,) + w_out_s.shape, w_out_s.dtype)
    return pl.pallas_call(
        body, name="gather_w_out_start",
        out_shape=(sem,) * 6 + (pltpu.HBM(w_out_s.shape, w_out_s.dtype), pltpu.HBM(land.shape, land.dtype),
                                jax.ShapeDtypeStruct((8, 128), F32)),
        in_specs=(_HBM, _HBM, pl.BlockSpec(memory_space=pl.ANY)),
        out_specs=(_SEM,) * 6 + (_HBM, _HBM, pl.BlockSpec(memory_space=pltpu.VMEM)),
        input_output_aliases={0: 6, 1: 7},
        compiler_params=pltpu.CompilerParams(has_side_effects=_EFFECT),
    )(pltpu.with_memory_space_constraint(w_out_s, pltpu.HBM), pltpu.with_memory_space_constraint(land, pltpu.HBM), after)


def _gather_w_out_wait(sems, src_thru, land_thru, after):
    def body(src_ref, land_ref, s0, s1, s2, r0, r1, r2, after_ref, src_dead, got_ref):
        x, y, c = _mesh_pos()
        chips = [(1 - x, y), (x, 1 - y), (1 - x, 1 - y)]
        for (px, py), s, r in zip(chips, (s0, s1, s2), (r0, r1, r2)):
            cp = pltpu.make_async_remote_copy(src_ref=src_ref, dst_ref=land_ref.at[2 * px + py], send_sem=s, recv_sem=r,
                                              device_id=(px, py, c), device_id_type=MESH)
            cp.wait_send()
            cp.wait_recv()

    return pl.pallas_call(
        body, name="gather_w_out_wait",
        out_shape=(pltpu.HBM(src_thru.shape, src_thru.dtype), pltpu.HBM(land_thru.shape, land_thru.dtype)),
        in_specs=(_HBM, _HBM) + (_SEM,) * 6 + (pl.BlockSpec(memory_space=pl.ANY),),
        out_specs=(_HBM, _HBM), input_output_aliases={0: 0, 1: 1},
        compiler_params=pltpu.CompilerParams(has_side_effects=_EFFECT),
    )(src_thru, land_thru, *sems, after)[1]


def _pair_start(gw_in, after):
    hr = gw_in.shape[1] // 2

    def body(src_ref, land_ref, after_ref, *refs):
        x, y, c = _mesh_pos()
        for j in range(N_CHIPS):
            pltpu.make_async_remote_copy(
                src_ref=src_ref.at[j, pl.ds((1 - c) * hr, hr), :], dst_ref=land_ref.at[j], send_sem=refs[j],
                recv_sem=refs[N_CHIPS + j], device_id=(x, y, 1 - c), device_id_type=MESH).start()
        refs[10][...] = jnp.zeros_like(refs[10])

    sem = pltpu.SemaphoreType.DMA(())
    land = lax.empty((N_CHIPS, hr, D_MODEL), F32)
    return pl.pallas_call(
        body, name="pair_start",
        out_shape=(sem,) * 8 + (pltpu.HBM(gw_in.shape, F32), pltpu.HBM(land.shape, F32), jax.ShapeDtypeStruct((8, 128), F32)),
        in_specs=(_HBM, _HBM, pl.BlockSpec(memory_space=pl.ANY)),
        out_specs=(_SEM,) * 8 + (_HBM, _HBM, pl.BlockSpec(memory_space=pltpu.VMEM)),
        input_output_aliases={0: 8, 1: 9},
        compiler_params=pltpu.CompilerParams(has_side_effects=_EFFECT),
    )(pltpu.with_memory_space_constraint(gw_in, pltpu.HBM), pltpu.with_memory_space_constraint(land, pltpu.HBM), after)


def _pair_wait(sems, gw_thru, land_thru, after):
    hr = land_thru.shape[1]

    def body(src_ref, land_ref, *refs):
        x, y, c = _mesh_pos()
        for j in range(N_CHIPS):
            cp = pltpu.make_async_remote_copy(
                src_ref=src_ref.at[j, pl.ds((1 - c) * hr, hr), :], dst_ref=land_ref.at[j], send_sem=refs[j],
                recv_sem=refs[N_CHIPS + j], device_id=(x, y, 1 - c), device_id_type=MESH)
            cp.wait_send()
            cp.wait_recv()

    return pl.pallas_call(
        body, name="pair_wait",
        out_shape=(pltpu.HBM(gw_thru.shape, F32), pltpu.HBM(land_thru.shape, F32)),
        in_specs=(_HBM, _HBM) + (_SEM,) * 8 + (pl.BlockSpec(memory_space=pl.ANY),),
        out_specs=(_HBM, _HBM), input_output_aliases={0: 0, 1: 1},
        compiler_params=pltpu.CompilerParams(has_side_effects=_EFFECT),
    )(gw_thru, land_thru, *sems, after)


def _chip_start(s_in, after):
    def body(src_ref, land_ref, after_ref, *refs):
        x, y, c = _mesh_pos()
        me = 2 * x + y
        for j, (px, py) in enumerate([(1 - x, y), (x, 1 - y), (1 - x, 1 - y)]):
            pltpu.make_async_remote_copy(
                src_ref=src_ref.at[2 * px + py], dst_ref=land_ref.at[me], send_sem=refs[j], recv_sem=refs[3 + j],
                device_id=(px, py, c), device_id_type=MESH).start()
        refs[8][...] = jnp.zeros_like(refs[8])

    sem = pltpu.SemaphoreType.DMA(())
    land = lax.empty(s_in.shape, s_in.dtype)
    return pl.pallas_call(
        body, name="chip_start",
        out_shape=(sem,) * 6 + (pltpu.HBM(s_in.shape, s_in.dtype), pltpu.HBM(land.shape, land.dtype),
                                jax.ShapeDtypeStruct((8, 128), F32)),
        in_specs=(_HBM, _HBM, pl.BlockSpec(memory_space=pl.ANY)),
        out_specs=(_SEM,) * 6 + (_HBM, _HBM, pl.BlockSpec(memory_space=pltpu.VMEM)),
        input_output_aliases={0: 6, 1: 7},
        compiler_params=pltpu.CompilerParams(has_side_effects=_EFFECT),
    )(pltpu.with_memory_space_constraint(s_in, pltpu.HBM), pltpu.with_memory_space_constraint(land, pltpu.HBM), after)


def _chip_wait(sems, s_thru, land_thru, after):
    def body(src_ref, land_ref, *refs):
        x, y, c = _mesh_pos()
        for j, (px, py) in enumerate([(1 - x, y), (x, 1 - y), (1 - x, 1 - y)]):
            cp = pltpu.make_async_remote_copy(
                src_ref=src_ref.at[2 * px + py], dst_ref=land_ref.at[2 * px + py], send_sem=refs[j], recv_sem=refs[3 + j],
                device_id=(px, py, c), device_id_type=MESH)
            cp.wait_send()
            cp.wait_recv()

    return pl.pallas_call(
        body, name="chip_wait",
        out_shape=(pltpu.HBM(s_thru.shape, s_thru.dtype), pltpu.HBM(land_thru.shape, land_thru.dtype)),
        in_specs=(_HBM, _HBM) + (_SEM,) * 6 + (pl.BlockSpec(memory_space=pl.ANY),),
        out_specs=(_HBM, _HBM), input_output_aliases={0: 0, 1: 1},
        compiler_params=pltpu.CompilerParams(has_side_effects=_EFFECT),
    )(s_thru, land_thru, *sems, after)


def _pair_share(h_in, small):
    def body(hin_ref, sm_ref, rin_ref, slots_ref, send_sems, recv_sems, small_send, small_recv, local_sem):
        x, y, c = _mesh_pos()
        dev = 4 * x + 2 * y + c
        mine = pltpu.make_async_copy(sm_ref, slots_ref.at[dev], local_sem)
        mine.start()
        share = pltpu.make_async_remote_copy(
            src_ref=hin_ref, dst_ref=rin_ref, send_sem=send_sems.at[0], recv_sem=recv_sems.at[0],
            device_id=(x, y, 1 - c), device_id_type=MESH)
        share.start()
        started = []
        for k in range(1, 8):
            peer = (x ^ ((k >> 2) & 1), y ^ ((k >> 1) & 1), c ^ (k & 1))
            cp = pltpu.make_async_remote_copy(
                src_ref=sm_ref, dst_ref=slots_ref.at[dev], send_sem=small_send.at[k - 1], recv_sem=small_recv.at[k - 1],
                device_id=peer, device_id_type=MESH)
            cp.start()
            started.append(cp)
        share.wait()
        for k in range(1, 8):
            pltpu.make_async_remote_copy(
                src_ref=sm_ref, dst_ref=slots_ref.at[dev], send_sem=small_send.at[k - 1], recv_sem=small_recv.at[k - 1],
                device_id=(x, y, 1 - c), device_id_type=MESH).wait_recv()
        for cp in started:
            cp.wait_send()
        mine.wait()

    any_spec = pl.BlockSpec(memory_space=pl.ANY)
    return pl.pallas_call(
        body, name="pair_share",
        in_specs=[any_spec] * 2, out_specs=[any_spec] * 2,
        out_shape=[jax.ShapeDtypeStruct(h_in.shape, F32), jax.ShapeDtypeStruct((8,) + small.shape, F32)],
        scratch_shapes=[pltpu.SemaphoreType.DMA((1,)), pltpu.SemaphoreType.DMA((1,)),
                        pltpu.SemaphoreType.DMA((7,)), pltpu.SemaphoreType.DMA((7,)), pltpu.SemaphoreType.DMA],
    )(h_in, small)


def _reduce_w_out_start(slabs, after):
    def body(src_ref, land_ref, after_ref, *refs):
        x, y, c = _mesh_pos()
        me = 4 * x + 2 * y + c
        for k in range(1, 8):
            px, py, pc = x ^ ((k >> 2) & 1), y ^ ((k >> 1) & 1), c ^ (k & 1)
            pltpu.make_async_remote_copy(src_ref=src_ref.at[2 * px + py], dst_ref=land_ref.at[me], send_sem=refs[k - 1],
                                         recv_sem=refs[6 + k], device_id=(px, py, pc), device_id_type=MESH).start()
        refs[16][...] = jnp.zeros_like(refs[16])

    sem = pltpu.SemaphoreType.DMA(())
    land = lax.empty((8,) + slabs.shape[1:], slabs.dtype)
    return pl.pallas_call(
        body, name="reduce_w_out_start",
        out_shape=(sem,) * 14 + (pltpu.HBM(slabs.shape, slabs.dtype), pltpu.HBM(land.shape, land.dtype),
                                 jax.ShapeDtypeStruct((8, 128), F32)),
        in_specs=(_HBM, _HBM, pl.BlockSpec(memory_space=pl.ANY)),
        out_specs=(_SEM,) * 14 + (_HBM, _HBM, pl.BlockSpec(memory_space=pltpu.VMEM)),
        input_output_aliases={0: 14, 1: 15},
        compiler_params=pltpu.CompilerParams(has_side_effects=_EFFECT),
    )(pltpu.with_memory_space_constraint(slabs, pltpu.HBM), pltpu.with_memory_space_constraint(land, pltpu.HBM), after)


def _reduce_w_out_wait(sems, slabs_thru, land_thru, after):
    def body(src_ref, land_ref, *refs):
        x, y, c = _mesh_pos()
        for k in range(1, 8):
            px, py, pc = x ^ ((k >> 2) & 1), y ^ ((k >> 1) & 1), c ^ (k & 1)
            cp = pltpu.make_async_remote_copy(
                src_ref=src_ref.at[2 * px + py], dst_ref=land_ref.at[4 * px + 2 * py + pc], send_sem=refs[k - 1],
                recv_sem=refs[6 + k], device_id=(px, py, pc), device_id_type=MESH)
            cp.wait_send()
            cp.wait_recv()

    return pl.pallas_call(
        body, name="reduce_w_out_wait",
        out_shape=(pltpu.HBM(slabs_thru.shape, slabs_thru.dtype), pltpu.HBM(land_thru.shape, land_thru.dtype)),
        in_specs=(_HBM, _HBM) + (_SEM,) * 14 + (pl.BlockSpec(memory_space=pl.ANY),),
        out_specs=(_HBM, _HBM), input_output_aliases={0: 0, 1: 1},
        compiler_params=pltpu.CompilerParams(has_side_effects=_EFFECT),
    )(slabs_thru, land_thru, *sems, after)


def _pair_add(g, recv, core, name):
    _, rows, C = recv.shape
    tc = 256

    def body(core_ref, g_ref, r_ref, o_ref):
        o_ref[...] = _bf(g_ref[...] + r_ref[...])

    spec = pl.BlockSpec((1, rows, tc), lambda j, i, core: (j, 0, i))
    return pl.pallas_call(
        body, name=name,
        grid_spec=pltpu.PrefetchScalarGridSpec(
            num_scalar_prefetch=1, grid=(N_CHIPS, C // tc),
            in_specs=[pl.BlockSpec((1, rows, tc), lambda j, i, core: (j, core[0], i)), spec], out_specs=spec),
        out_shape=jax.ShapeDtypeStruct((N_CHIPS, rows, C), BF16),
        compiler_params=_cparams(("parallel", "parallel")),
    )(core, g, recv)


def _chip_add(own, parts, chip, name):
    _, rows, C = parts.shape
    tc = 256

    def body(chip_ref, own_ref, r0, r1, r2, r3, o_ref):
        acc = None
        for j, r in enumerate((r0, r1, r2, r3)):
            term = jnp.where(chip_ref[0] == j, own_ref[0], r[0]).astype(F32)
            acc = term if acc is None else acc + term
        o_ref[...] = acc

    def slab(j):
        return pl.BlockSpec((1, rows, tc), lambda i, chip: (jnp.where(chip[0] == j, (j + 1) % N_CHIPS, j), 0, i))

    return pl.pallas_call(
        body, name=name,
        grid_spec=pltpu.PrefetchScalarGridSpec(
            num_scalar_prefetch=1, grid=(C // tc,),
            in_specs=[pl.BlockSpec((1, rows, tc), lambda i, chip: (chip[0], 0, i))] + [slab(j) for j in range(N_CHIPS)],
            out_specs=pl.BlockSpec((rows, tc), lambda i, chip: (0, i))),
        out_shape=jax.ShapeDtypeStruct((rows, C), F32),
        compiler_params=_cparams(("parallel",)),
    )(chip, own, parts, parts, parts, parts)


def _adamw_math(w, g, m, v):
    m = ADAM_B1 * m + (1.0 - ADAM_B1) * g
    v = ADAM_B2 * v + (1.0 - ADAM_B2) * (g * g)
    m_hat = m / (1.0 - ADAM_B1 ** ADAM_STEP)
    v_hat = v / (1.0 - ADAM_B2 ** ADAM_STEP)
    delta = -ADAM_LR * (m_hat / (jnp.sqrt(v_hat) + ADAM_EPS) + ADAM_WD * w)
    return delta, m, v


def _adamw_rows(w, g_own, g_sib, m, v, core, name):
    R, C = w.shape[0], w.shape[-1]
    rows = g_own.shape[0]
    step = 256
    chunks = [(r, min(step, R - r)) for r in range(0, R, step)]
    sub = 64

    def body(core_ref, w_hbm, go_hbm, gs_hbm, m_hbm, v_hbm, d_hbm, nm_hbm, nv_hbm, g_hbm,
             wbuf, mbuf, vbuf, gbuf, dbuf, nmbuf, nvbuf, in_sems, g_sems, out_sems):
        c = core_ref[0]
        flat = lambda ref: ref.at[:, 0, :]
        g_in = [pltpu.make_async_copy(go_hbm, gbuf.at[pl.ds(pl.multiple_of(c * rows, 8), rows), :], g_sems.at[0]),
                pltpu.make_async_copy(gs_hbm, gbuf.at[pl.ds(pl.multiple_of((1 - c) * rows, 8), rows), :], g_sems.at[1])]
        for cp in g_in:
            cp.start()
        loads = []
        for k, (r0, n) in enumerate(chunks):
            cps = [pltpu.make_async_copy(flat(src).at[pl.ds(r0, n), :], dst.at[pl.ds(r0, n), :], in_sems.at[a, k])
                   for a, (src, dst) in enumerate(((w_hbm, wbuf), (m_hbm, mbuf), (v_hbm, vbuf)))]
            for cp in cps:
                cp.start()
            loads.append(cps)
        for cp in g_in:
            cp.wait()
        stores = []
        for k, (r0, n) in enumerate(chunks):
            for cp in loads[k]:
                cp.wait()

            def update(rs):
                g = gbuf[rs, :]
                dl, nm, nv = _adamw_math(wbuf[rs, :], g, mbuf[rs, :], vbuf[rs, :])
                dbuf[rs, :] = dl
                nmbuf[rs, :] = nm
                nvbuf[rs, :] = nv

            if n % sub == 0:
                def block(i, carry, r0=r0):
                    update(pl.ds(pl.multiple_of(r0 + i * sub, 8), sub))
                    return carry
                lax.fori_loop(0, n // sub, block, 0)
            else:
                update(pl.ds(r0, n))
            cps = [pltpu.make_async_copy(src.at[pl.ds(r0, n), :], flat(dst).at[pl.ds(r0, n), :], out_sems.at[a, k])
                   for a, (src, dst) in enumerate(((dbuf, d_hbm), (nmbuf, nm_hbm), (nvbuf, nv_hbm), (gbuf, g_hbm)))]
            for cp in cps:
                cp.start()
            stores += cps
        for cp in stores:
            cp.wait()

    any_spec = pl.BlockSpec(memory_space=pl.ANY)
    dense = pltpu.VMEM((R, C), F32)
    return pl.pallas_call(
        body, name=name,
        grid_spec=pltpu.PrefetchScalarGridSpec(
            num_scalar_prefetch=1, grid=(1,),
            in_specs=[any_spec] * 5, out_specs=[any_spec] * 4,
            scratch_shapes=[dense, dense, dense, pltpu.VMEM((2 * rows, C), F32), dense, dense, dense,
                            pltpu.SemaphoreType.DMA((3, len(chunks))), pltpu.SemaphoreType.DMA((2,)),
                            pltpu.SemaphoreType.DMA((4, len(chunks)))]),
        out_shape=[jax.ShapeDtypeStruct(w.shape, F32)] * 4,
        compiler_params=_cparams(),
    )(core, w, g_own, g_sib, m, v)


def _adamw_sum8(w, slabs, land, m, v, ids, name):
    R, C = w.shape
    tc = 128

    def body(ids_ref, w_ref, own_ref, *refs):
        lrefs, (m_ref, v_ref, d_ref, nm_ref, nv_ref, g_ref) = refs[:8], refs[8:]
        g = None
        for d, l_ref in enumerate(lrefs):
            term = jnp.where(ids_ref[0] == d, own_ref[0], l_ref[0]).astype(F32)
            g = term if g is None else g + term
        dl, nm, nv = _adamw_math(w_ref[...], g, m_ref[...], v_ref[...])
        d_ref[...] = dl
        nm_ref[...] = nm
        nv_ref[...] = nv
        g_ref[...] = g

    def slot(d):
        return pl.BlockSpec((1, R, tc), lambda i, ids: (jnp.where(ids[0] == d, (d + 1) % 8, d), 0, i))

    spec = pl.BlockSpec((R, tc), lambda i, ids: (0, i))
    return pl.pallas_call(
        body, name=name,
        grid_spec=pltpu.PrefetchScalarGridSpec(
            num_scalar_prefetch=1, grid=(C // tc,),
            in_specs=[spec, pl.BlockSpec((1, R, tc), lambda i, ids: (ids[1], 0, i))] + [slot(d) for d in range(8)]
            + [spec, spec],
            out_specs=[spec] * 4),
        out_shape=[jax.ShapeDtypeStruct((R, C), F32)] * 4,
        compiler_params=_cparams(("parallel",)),
    )(ids, w, slabs, *([land] * 8), m, v)


SMALL_NAMES = ("conv_b", "ssd_norm_w", "ln_g", "ln_b", "dt_bias", "a_log", "d_skip", "attn_sinks")
SMALL_FIELDS = ((4, 0, D_XBC), (5, 0, D_SSD), (6, 0, D_MODEL), (7, 0, D_MODEL), (5, 1024, SSD_HEADS), (5, 1152, SSD_HEADS),
                (5, 1280, SSD_HEADS), (5, 1408, ATT_QH))
LOSS_FIELD = (6, 1024, 128)
K_SMALL = D_XBC


def _pack_small(g_conv_w, vecs, loss):
    def body(cw_ref, *refs):
        o_ref = refs[-1]
        o_ref[...] = jnp.zeros_like(o_ref)
        o_ref[0:CONV_K, 0:D_XBC] = cw_ref[...]
        for v_ref, (row, off, n) in zip(refs[:-2], SMALL_FIELDS):
            o_ref[row:row + 1, off:off + n] = v_ref[...]
        o_ref[LOSS_FIELD[0]:LOSS_FIELD[0] + 1, LOSS_FIELD[1]:LOSS_FIELD[1] + LOSS_FIELD[2]] = refs[-2][...]

    return pl.pallas_call(
        body, name="pack_small", out_shape=jax.ShapeDtypeStruct((8, K_SMALL), F32), compiler_params=_cparams(),
    )(g_conv_w, *vecs, loss)


def _adamw_small(slots, chip, conv_w, m_conv_w, v_conv_w, params, moms, vars_):
    n_vec = len(SMALL_NAMES)

    def body(chip_ref, s_ref, *refs):
        ins = refs[:3 * (n_vec + 1)]
        outs = refs[3 * (n_vec + 1):-1]
        tot_ref = refs[-1]
        tot = s_ref[0]
        for d in range(1, 8):
            tot = tot + s_ref[d]
        outs[0][...] = tot[LOSS_FIELD[0]:LOSS_FIELD[0] + 1, LOSS_FIELD[1]:LOSS_FIELD[1] + 1]
        off = pl.multiple_of(chip_ref[0] * CONV_COLS, 128)
        tot_ref[...] = tot
        grads = [tot_ref[0:CONV_K, pl.ds(off, CONV_COLS)]]
        grads += [tot[row:row + 1, o:o + n] for row, o, n in SMALL_FIELDS]
        for k, g in enumerate(grads):
            w_ref, m_ref, v_ref = ins[3 * k:3 * k + 3]
            full = (0,) if k == 0 else (Ellipsis,)
            d, nm, nv = _adamw_math(w_ref[full], g, m_ref[full], v_ref[full])
            for o_ref, val in zip(outs[1 + 4 * k:5 + 4 * k], (g, d, nm, nv)):
                o_ref[full] = val

    args = [conv_w, m_conv_w, v_conv_w]
    for w, m, v in zip(params, moms, vars_):
        args += [w, m, v]
    shapes = [jax.ShapeDtypeStruct((1, 1), F32)] + [jax.ShapeDtypeStruct(conv_w.shape, F32)] * 4
    for w in params:
        shapes += [jax.ShapeDtypeStruct(w.shape, F32)] * 4
    vmem = pl.BlockSpec(memory_space=pltpu.VMEM)
    return pl.pallas_call(
        body, name="adamw_small",
        grid_spec=pltpu.PrefetchScalarGridSpec(
            num_scalar_prefetch=1, grid=(1,),
            in_specs=[pl.BlockSpec(slots.shape, lambda i, chip: (0, 0, 0))] + [vmem] * len(args),
            out_specs=[vmem] * len(shapes), scratch_shapes=[pltpu.VMEM((8, K_SMALL), F32)]),
        out_shape=shapes, compiler_params=_cparams(),
    )(chip, slots, *args)


def kernel(x, positions, w_in, conv_w, conv_b, dt_bias, a_log, d_skip, ssd_norm_w, attn_sinks, w_out, ln_g, ln_b, loss_target, m_w_in, m_conv_w, m_conv_b, m_dt_bias, m_a_log, m_d_skip, m_ssd_norm_w, m_attn_sinks, m_w_out, m_ln_g, m_ln_b, v_w_in, v_conv_w, v_conv_b, v_dt_bias, v_a_log, v_d_skip, v_ssd_norm_w, v_attn_sinks, v_w_out, v_ln_g, v_ln_b):
    mx, my, mc = _mesh_pos()
    chip = 2 * mx + my
    L = x.shape[1]

    conv_w_s8 = jnp.pad(conv_w[0], ((0, 8 - CONV_K), (0, 0)))
    pad_rows = ((0, SLAB_ROWS - W_IN_COLS), (0, 0))
    w_in_t = w_in[0].T
    w_in_b, w_out_b = jnp.pad(_bf(w_in_t), pad_rows), _bf(w_out[0])
    ag_in, ag_cw = _gather_weights(w_in_b, conv_w_s8)
    started = _gather_w_out_start(w_out_b, ag_cw)
    own = (jnp.arange(N_CHIPS) == chip)[:, None, None]

    def get_w_out(after):
        landed = _gather_w_out_wait(started[0:6], started[6], started[7], after)
        return jnp.where(own, w_out_b[None], landed).reshape(D_MIX, D_MODEL)

    w_full = jnp.concatenate([ag_in[j, 0:W_IN_COLS] for j in range(N_CHIPS)], axis=0)
    w = jnp.concatenate([
        w_full[O_Z:O_Z + D_SSD], w_full[O_G:O_G + D_ATT], w_full[O_Q:O_Q + D_ATT],
        w_full[O_XBC:O_XBC + D_XBC], w_full[O_K:O_K + 2 * D_KV], w_full[O_DT:O_DT + SSD_HEADS],
        jnp.zeros((DT_PAD - SSD_HEADS, D_MODEL), BF16)], axis=0)
    conv_w_full = jnp.concatenate([ag_cw[j, 0:CONV_K] for j in range(N_CHIPS)], axis=1)

    loss_part, gx_args, gw_in, w_out_red, small = _local_step(
        x[0], positions[0].reshape(L, 1), loss_target[0], w, get_w_out, started[8][0:1, :], conv_w_full,
        conv_b, dt_bias, a_log, d_skip, ssd_norm_w, attn_sinks, ln_g, ln_b)

    packed = _pack_small(small["conv_w"], [small[n] for n in SMALL_NAMES], loss_part)
    core_id = mc.reshape(1).astype(jnp.int32)
    chip_id = chip.reshape(1).astype(jnp.int32)
    ids = jnp.stack([4 * mx + 2 * my + mc, chip]).astype(jnp.int32)
    slabs = jnp.stack([jnp.pad(gw_in[W_IN_COLS * j:W_IN_COLS * (j + 1)], pad_rows) for j in range(N_CHIPS)])
    w_in_red = _pair_start(slabs, packed)
    grad_x = _grad_x(*gx_args, w_in_red[10], 0)
    gw_in_slabs, recv_in = _pair_wait(w_in_red[0:8], w_in_red[8], w_in_red[9], grad_x[0:8, 0:128])
    s_in = _pair_add(gw_in_slabs, recv_in, core_id, "pair_add_in")
    chip_red = _chip_start(s_in, packed)
    grad_x = _grad_x(*gx_args, chip_red[8], 1, grad_x)
    own_slabs, landed = _reduce_w_out_wait(w_out_red[0:14], w_out_red[14], w_out_red[15], grad_x)
    out_t = _adamw_sum8(w_out[0], own_slabs, landed, m_w_out[0], v_w_out[0], ids, "adamw_w_out")
    d_w_out, nm_w_out, nv_w_out, g_w_out = [a[None] for a in out_t]
    s_in, r_in = _chip_wait(chip_red[0:6], chip_red[6], chip_red[7], out_t[0])
    h_in = _chip_add(s_in, r_in, chip_id, "chip_add_in")
    sib_in, slots = _pair_share(h_in, packed)

    to_rows = lambda a: jnp.transpose(a, (2, 0, 1))
    in_t = _adamw_rows(to_rows(w_in), h_in, sib_in, to_rows(m_w_in), to_rows(v_w_in), core_id, "adamw_w_in")
    d_w_in, nm_w_in, nv_w_in, g_w_in = [jnp.transpose(a, (1, 2, 0)) for a in in_t]

    params = dict(conv_b=conv_b, ssd_norm_w=ssd_norm_w, ln_g=ln_g, ln_b=ln_b, dt_bias=dt_bias, a_log=a_log,
                  d_skip=d_skip, attn_sinks=attn_sinks)
    moms = dict(conv_b=m_conv_b, ssd_norm_w=m_ssd_norm_w, ln_g=m_ln_g, ln_b=m_ln_b, dt_bias=m_dt_bias, a_log=m_a_log,
                d_skip=m_d_skip, attn_sinks=m_attn_sinks)
    vars_ = dict(conv_b=v_conv_b, ssd_norm_w=v_ssd_norm_w, ln_g=v_ln_g, ln_b=v_ln_b, dt_bias=v_dt_bias, a_log=v_a_log,
                 d_skip=v_d_skip, attn_sinks=v_attn_sinks)
    res = _adamw_small(slots, chip_id, conv_w, m_conv_w, v_conv_w, [params[n] for n in SMALL_NAMES],
                       [moms[n] for n in SMALL_NAMES], [vars_[n] for n in SMALL_NAMES])
    loss = res[0][0, 0]
    grads, delta, new_m, new_v = {}, {}, {}, {}
    for k, n in enumerate(("conv_w",) + SMALL_NAMES):
        grads[n], delta[n], new_m[n], new_v[n] = res[1 + 4 * k:5 + 4 * k]
    for dd, a_in, a_out in ((grads, g_w_in, g_w_out), (delta, d_w_in, d_w_out), (new_m, nm_w_in, nm_w_out),
                            (new_v, nv_w_in, nv_w_out)):
        dd["w_in"] = a_in
        dd["w_out"] = a_out
    order = ("w_in", "conv_w", "conv_b", "dt_bias", "a_log", "d_skip", "ssd_norm_w", "attn_sinks", "w_out", "ln_g", "ln_b")
    return (loss, grad_x[None], *[grads[n] for n in order], *[delta[n] for n in order], *[new_m[n] for n in order],
            *[new_v[n] for n in order])
```

```python
import numpy as np
import jax
import jax.numpy as jnp
from jax import lax
from jax.experimental import pallas as pl
from jax.experimental.pallas import tpu as pltpu

F32 = jnp.float32
BF16 = jnp.bfloat16
MESH = pl.DeviceIdType.MESH

D_MODEL = 1024
D_SSD = 1024
D_ATT = 1024
D_MIX = 2048
SSD_HEADS = 16
SSD_P = 64
SSD_GROUPS = 2
SSD_R = 8
SSD_N = 128
D_BC = 256
D_XBC = 1536
CONV_K = 4
CHUNK = 128
ATT_HD = 64
ATT_QH = 16
ATT_KVH = 4
ATT_R = 4
D_KV = 256
WINDOW = 128
ROPE_THETA = 500000.0
ROPE_DIM = 16
ALPHA = 2.0 ** 0.25
LN_EPS = 1e-5
RMS_EPS = 1e-5
D_IN_PROJ = 5136
O_Z, O_XBC, O_DT, O_Q, O_K, O_V, O_G = 0, 1024, 2560, 2576, 3600, 3856, 4112
P_Z, P_G, P_Q, P_XBC, P_KV, P_DT, P_END = 0, 1024, 2048, 3072, 4608, 5120, 5248
DT_PAD = 128
N_CHIPS = 4
W_IN_COLS = D_IN_PROJ // N_CHIPS
SLAB_ROWS = 1312
W_OUT_ROWS = D_MIX // N_CHIPS
CONV_COLS = D_XBC // N_CHIPS

ADAM_LR = 0.001
ADAM_B1 = 0.9
ADAM_B2 = 0.999
ADAM_EPS = 1e-08
ADAM_WD = 0.01
ADAM_STEP = 10

VMEM_LIMIT = 56 * 1024 * 1024
ROW_TILE = 512
NEG_BIG = -1e30


def _cparams(sem=None, **kw):
    if sem is not None:
        kw["dimension_semantics"] = sem
    return pltpu.CompilerParams(vmem_limit_bytes=VMEM_LIMIT, **kw)


def _dot(a, b):
    return jnp.dot(a, b, preferred_element_type=F32)


def _dot_nt(a, b):
    return lax.dot_general(a, b, (((1,), (1,)), ((), ())), preferred_element_type=F32)


def _dot_tn(a, b):
    return lax.dot_general(a, b, (((0,), (0,)), ((), ())), preferred_element_type=F32)


def _bf(a):
    return a.astype(BF16)


def _iota2(shape, dim):
    return lax.broadcasted_iota(jnp.int32, shape, dim)


def _three_terms(x):
    hi = _bf(x)
    r = x - hi.astype(F32)
    mid = _bf(r)
    return hi, mid, _bf(r - mid.astype(F32))


def _dot01(m, a):
    return sum(_dot(m, t) for t in _three_terms(a))


def _to_rows(col):
    k = col.shape[1]
    eye = (_iota2((k, k), 0) == _iota2((k, k), 1)).astype(BF16)
    return sum(_dot_nt(eye, t) for t in _three_terms(col))


def _to_cols(row):
    n = row.shape[1]
    eye = (_iota2((n, n), 0) == _iota2((n, n), 1)).astype(BF16)
    return sum(_dot_nt(eye, t) for t in _three_terms(row))


def _sigmoid(x):
    return jax.nn.sigmoid(x)


def _in_proj(x, w, pos, inv):
    L = x.shape[0]
    tm = ROW_TILE
    widths = (D_SSD, D_ATT, D_ATT, D_XBC, 2 * D_KV, DT_PAD)

    def body(x_ref, w_ref, pos_ref, inv_ref, z_ref, g_ref, q_ref, xbc_ref, kv_ref, dt_ref, xb_ref):
        xb = _bf(x_ref[...])
        xb_ref[...] = xb
        tabs = _rope_tables(pos_ref, inv_ref)
        q_ref[...] = _bf(_rope(_dot_nt(xb, w_ref[P_Q:P_Q + D_ATT, :]), tabs))
        kv_ref[:, 0:D_KV] = _bf(_rope(_dot_nt(xb, w_ref[P_KV:P_KV + D_KV, :]), tabs))
        kv_ref[:, D_KV:2 * D_KV] = _bf(_dot_nt(xb, w_ref[P_KV + D_KV:P_KV + 2 * D_KV, :]))
        for o_ref, off, wd in zip((z_ref, g_ref, xbc_ref, dt_ref), (P_Z, P_G, P_XBC, P_DT), (D_SSD, D_ATT, D_XBC, DT_PAD)):
            o_ref[...] = _dot_nt(xb, w_ref[off:off + wd, :])

    row = lambda wd: pl.BlockSpec((tm, wd), lambda i: (i, 0))
    return pl.pallas_call(
        body, name="in_proj", grid=(L // tm,),
        in_specs=[row(D_MODEL), pl.BlockSpec((P_END, D_MODEL), lambda i: (0, 0), pipeline_mode=pl.Buffered(1)), row(1),
                  pl.BlockSpec((1, 2 * ATT_HD), lambda i: (0, 0))],
        out_specs=[row(wd) for wd in widths] + [row(D_MODEL)],
        out_shape=[jax.ShapeDtypeStruct((L, wd), dt) for wd, dt in zip(widths, (F32, F32, BF16, F32, BF16, F32))]
        + [jax.ShapeDtypeStruct((L, D_MODEL), BF16)],
        compiler_params=_cparams(("parallel",)),
    )(x, w, pos, inv)


def _matmuls_tn(a_list, b, name):
    K, N = b.shape
    tk = min(K, 1024)
    n = len(a_list)

    def body(*refs):
        b_ref = refs[n]

        @pl.when(pl.program_id(0) == 0)
        def _():
            for o_ref in refs[n + 1:]:
                o_ref[...] = jnp.zeros_like(o_ref)

        bb = _bf(b_ref[...])
        for a_ref, o_ref in zip(refs[:n], refs[n + 1:]):
            o_ref[...] += _dot_tn(_bf(a_ref[...]), bb)

    return pl.pallas_call(
        body, name=name, grid=(K // tk,),
        in_specs=[pl.BlockSpec((tk, a.shape[1]), lambda k: (k, 0)) for a in a_list] + [pl.BlockSpec((tk, N), lambda k: (k, 0))],
        out_specs=[pl.BlockSpec((a.shape[1], N), lambda k: (0, 0)) for a in a_list],
        out_shape=[jax.ShapeDtypeStruct((a.shape[1], N), F32) for a in a_list],
        compiler_params=_cparams(("arbitrary",)),
    )(*a_list, b)


def _grad_x(dr, dz, dg, dq, dxbc, dkv, ddt, w, after, part, prev=None):
    L = dr.shape[0]
    tm = min(ROW_TILE, L // 4)
    first = L // (4 * tm)
    n = first if part == 0 else L // tm - first
    widths = (D_SSD, D_ATT, D_ATT, D_XBC, 2 * D_KV, DT_PAD)
    offs = (P_Z, P_G, P_Q, P_XBC, P_KV, P_DT)

    def body(dr_ref, dz_ref, dg_ref, dq_ref, dxbc_ref, dkv_ref, ddt_ref, w_ref, after_ref, *rest):
        o_ref = rest[-1]
        acc = ALPHA * dr_ref[...]
        for p_ref, off, wd in zip((dz_ref, dg_ref, dq_ref, dxbc_ref, dkv_ref, ddt_ref), offs, widths):
            acc = acc + _dot(_bf(p_ref[...]), w_ref[off:off + wd, :])
        o_ref[...] = acc

    row = lambda wd: pl.BlockSpec((tm, wd), lambda i: (i + part * first, 0))
    ins = [dr, dz, dg, dq, dxbc, dkv, ddt, w, after]
    specs = ([row(D_MODEL)] + [row(wd) for wd in widths]
             + [pl.BlockSpec((P_END, D_MODEL), lambda i: (0, 0), pipeline_mode=pl.Buffered(1)),
                pl.BlockSpec((8, 128), lambda i: (0, 0))])
    if prev is not None:
        ins.append(prev)
        specs.append(pl.BlockSpec(memory_space=pl.ANY))
    return pl.pallas_call(
        body, name="grad_x_%d" % part, grid=(n,),
        in_specs=specs, out_specs=row(D_MODEL),
        out_shape=jax.ShapeDtypeStruct((L, D_MODEL), F32),
        input_output_aliases={} if prev is None else {len(ins) - 1: 0},
        compiler_params=_cparams(("parallel",)),
    )(*ins)


HALO = 16


def _shift_matrix(offsets):
    n = CHUNK + HALO
    m = np.zeros((len(offsets) * CHUNK, 2 * n), np.float32)
    for k, off in enumerate(offsets):
        t = np.arange(CHUNK)
        m[k * CHUNK + t, t + off] = 1.0
        m[k * CHUNK + t, n + t + off] = 1.0
    return jnp.asarray(m, BF16)


def _shifted_rows(first_part, second_part, smat_ref):
    h1, l1 = _hi_lo(first_part)
    h2, l2 = _hi_lo(second_part)
    sh = _dot(smat_ref[...], jnp.concatenate([h1, h2, l1, l2], axis=0))
    return sh[0:CHUNK], sh[CHUNK:2 * CHUNK], sh[2 * CHUNK:3 * CHUNK]


def _ssd_chunk_pre(first, xbc_ref, tail_ref, dt_ref, cw_ref, cb_ref, dtb_ref, alog_ref, smat_ref=None, ext=None):
    tail = jnp.where(first, 0.0, tail_ref[...])
    x = xbc_ref[...]
    if ext is None:
        taps = _shifted_rows(tail, x, smat_ref) + (x,)
    else:
        ext[0:HALO, :] = tail
        ext[HALO:HALO + CHUNK, :] = x
        taps = tuple(ext[pl.ds(HALO - (CONV_K - 1) + k, CHUNK), :] for k in range(CONV_K - 1)) + (x,)
    u = cb_ref[...] + cw_ref[0:1, :] * taps[0]
    for k in range(1, CONV_K):
        u = u + cw_ref[k:k + 1, :] * taps[k]
    sig = _sigmoid(u)
    xbc = u * sig
    dtraw = dt_ref[:, 0:SSD_HEADS] + dtb_ref[...]
    dt = jax.nn.softplus(dtraw)
    A = -jnp.exp(alog_ref[...])
    a = dt * A
    tril = (_iota2((CHUNK, CHUNK), 0) >= _iota2((CHUNK, CHUNK), 1)).astype(BF16)
    acs = _dot01(tril, a)
    acs_row = _to_rows(acs)
    return u, sig, xbc, dtraw, dt, A, acs, acs_row, taps


def _head_expander():
    return (_iota2((SSD_HEADS, D_SSD), 1) // SSD_P == _iota2((SSD_HEADS, D_SSD), 0)).astype(BF16)


def _hi_lo(x):
    hi = _bf(x)
    return hi, _bf(x - hi.astype(F32))


def _expand(v, e):
    hi, lo = _hi_lo(v)
    return _dot(hi, e) + _dot(lo, e)


def _headsum(t, e):
    m = t.shape[0]
    if m < 8:
        t = jnp.broadcast_to(t[0:1], (8, t.shape[1]))
    hi, lo = _hi_lo(t)
    return (_dot_nt(hi, e) + _dot_nt(lo, e))[0:m]


def _ssd_decays(dt, acs, dsk_ref, e):
    alast = acs[CHUNK - 1:CHUNK, :]
    stk = jnp.concatenate([dt, jnp.exp(acs), jnp.exp(alast - acs),
                           jnp.broadcast_to(jnp.exp(alast), (8, SSD_HEADS)),
                           jnp.broadcast_to(dsk_ref[...], (8, SSD_HEADS))], axis=0)
    ex = _expand(stk, e)
    return (ex[0:CHUNK], ex[CHUNK:2 * CHUNK], ex[2 * CHUNK:3 * CHUNK], ex[3 * CHUNK:3 * CHUNK + 1],
            ex[3 * CHUNK + 8:3 * CHUNK + 9])


def _ssd_fwd(z, xbc, dtp, conv_w, conv_b, dt_bias, a_log, d_skip, norm_w):
    L = z.shape[0]
    nc = L // CHUNK
    half = D_SSD // SSD_GROUPS

    def body(z_ref, xbc_ref, tail_ref, dt_ref, cw_ref, cb_ref, dtb_ref, alog_ref, dsk_ref, nw_ref,
             y_ref, ypre_ref, prev_ref, state, ybuf, mbuf, ext):
        c = pl.program_id(0)

        @pl.when(c == 0)
        def _():
            state[...] = jnp.zeros_like(state)

        u, sig, xbcv, dtraw, dt, A, acs, acs_row, _ = _ssd_chunk_pre(
            c == 0, xbc_ref, tail_ref, dt_ref, cw_ref, cb_ref, dtb_ref, alog_ref, ext=ext)
        e = _head_expander()
        dtE, eacsE, dsdE, ealE, dskE = _ssd_decays(dt, acs, dsk_ref, e)
        xs = xbcv[:, 0:D_SSD]
        X = xs * dtE
        prev_ref[0] = state[...]
        causal = _iota2((CHUNK, CHUNK), 0) >= _iota2((CHUNK, CHUNK), 1)
        for g in range(SSD_GROUPS):
            gs = slice(half * g, half * (g + 1))
            Bg = _bf(xbcv[:, D_SSD + SSD_N * g:D_SSD + SSD_N * (g + 1)])
            Cg = _bf(xbcv[:, D_SSD + D_BC + SSD_N * g:D_SSD + D_BC + SSD_N * (g + 1)])
            cb = _dot_nt(Cg, Bg)
            for r in range(SSD_R):
                h = g * SSD_R + r
                seg = acs[:, h:h + 1] - acs_row[h:h + 1, :]
                mbuf[h] = _bf(cb * jnp.where(causal, jnp.exp(jnp.where(causal, seg, 0.0)), 0.0))
            st = state[:, gs]
            ybuf[:, gs] = _dot(Cg, _bf(st)) * eacsE[:, gs] + dskE[:, gs] * xs[:, gs]
            state[:, gs] = st * ealE[:, gs] + _dot_tn(Bg, _bf(X[:, gs] * dsdE[:, gs]))
        Xb = _bf(X)
        for h in range(SSD_HEADS):
            hs = slice(SSD_P * h, SSD_P * (h + 1))
            ybuf[:, hs] += _dot(mbuf[h], Xb[:, hs])
        y = ybuf[...]
        ypre_ref[...] = y
        zv = z_ref[...]
        yf = y * (zv * _sigmoid(zv))
        for g in range(SSD_GROUPS):
            gs = slice(half * g, half * (g + 1))
            yg = yf[:, gs]
            ms = jnp.mean(yg * yg, axis=-1, keepdims=True)
            y_ref[:, gs] = _bf(yg * lax.rsqrt(ms + RMS_EPS) * nw_ref[:, gs])

    full = lambda shape: pl.BlockSpec(shape, lambda c: (0, 0))
    return pl.pallas_call(
        body, name="ssd_fwd", grid=(nc,),
        in_specs=[
            pl.BlockSpec((CHUNK, D_SSD), lambda c: (c, 0)),
            pl.BlockSpec((CHUNK, D_XBC), lambda c: (c, 0)),
            pl.BlockSpec((HALO, D_XBC), lambda c: (jnp.maximum(c * (CHUNK // HALO) - 1, 0), 0)),
            pl.BlockSpec((CHUNK, DT_PAD), lambda c: (c, 0)),
            full((CONV_K, D_XBC)), full((1, D_XBC)), full((1, SSD_HEADS)), full((1, SSD_HEADS)), full((1, SSD_HEADS)),
            full((1, D_SSD)),
        ],
        out_specs=[
            pl.BlockSpec((CHUNK, D_SSD), lambda c: (c, 0)),
            pl.BlockSpec((CHUNK, D_SSD), lambda c: (c, 0)),
            pl.BlockSpec((1, SSD_N, D_SSD), lambda c: (c, 0, 0)),
        ],
        out_shape=[
            jax.ShapeDtypeStruct((L, D_SSD), BF16),
            jax.ShapeDtypeStruct((L, D_SSD), F32),
            jax.ShapeDtypeStruct((nc, SSD_N, D_SSD), F32),
        ],
        scratch_shapes=[
            pltpu.VMEM((SSD_N, D_SSD), F32),
            pltpu.VMEM((CHUNK, D_SSD), F32),
            pltpu.VMEM((SSD_HEADS, CHUNK, CHUNK), BF16),
            pltpu.VMEM((CHUNK + HALO, D_XBC), F32),
        ],
        compiler_params=_cparams(("arbitrary",)),
    )(z, xbc, xbc, dtp, conv_w, conv_b, dt_bias, a_log, d_skip, norm_w)


def _ssd_bwd(dy, z, ypre, xbc, dtp, prev, conv_w, conv_b, dt_bias, a_log, d_skip, norm_w):
    L = z.shape[0]
    nc = L // CHUNK
    half = D_SSD // SSD_GROUPS

    def body(dy_ref, z_ref, ypre_ref, xbc_ref, tail_ref, dt_ref, prev_ref, cw_ref, cb_ref, dtb_ref, alog_ref, dsk_ref,
             nw_ref, smat_ref, smat2_ref, dz_ref, dxbc_ref, ddt_ref, gcw_ref, gcb_ref, gdtb_ref, galog_ref, gdsk_ref,
             gnw_ref, dstate, dhead, dpost, yobuf, bdbuf, lmbuf, dmbuf, cbbuf):
        i = pl.program_id(0)
        c = nc - 1 - i

        @pl.when(i == 0)
        def _():
            dstate[...] = jnp.zeros_like(dstate)
            dhead[...] = jnp.zeros_like(dhead)
            gcw_ref[...] = jnp.zeros_like(gcw_ref)
            gcb_ref[...] = jnp.zeros_like(gcb_ref)
            gdtb_ref[...] = jnp.zeros_like(gdtb_ref)
            galog_ref[...] = jnp.zeros_like(galog_ref)
            gdsk_ref[...] = jnp.zeros_like(gdsk_ref)
            gnw_ref[...] = jnp.zeros_like(gnw_ref)

        u, sig, xbcv, dtraw, dt, A, acs, acs_row, taps = _ssd_chunk_pre(
            c == 0, xbc_ref, tail_ref, dt_ref, cw_ref, cb_ref, dtb_ref, alog_ref, smat_ref)
        e = _head_expander()
        dtE, eacsE, dsdE, ealE, dskE = _ssd_decays(dt, acs, dsk_ref, e)
        alast = acs[CHUNK - 1:CHUNK, :]
        xs = xbcv[:, 0:D_SSD]
        X = xs * dtE
        Xb = _bf(X)

        zv = z_ref[...]
        ypre = ypre_ref[...]
        dyn = dy_ref[...]
        sz = _sigmoid(zv)
        silu_z = zv * sz
        yf = ypre * silu_z
        dyf_parts = []
        for g in range(SSD_GROUPS):
            gs = slice(half * g, half * (g + 1))
            yg = yf[:, gs]
            rstd = lax.rsqrt(jnp.mean(yg * yg, axis=-1, keepdims=True) + RMS_EPS)
            dout = dyn[:, gs]
            gnw_ref[:, gs] += jnp.sum(dout * yg * rstd, axis=0, keepdims=True)
            dyhat = dout * nw_ref[:, gs]
            dyf_parts.append(rstd * (dyhat - yg * (rstd * rstd) * jnp.mean(dyhat * yg, axis=-1, keepdims=True)))
        dyf = jnp.concatenate(dyf_parts, axis=1)
        dz_ref[...] = _bf(dyf * ypre * (sz * (1.0 + zv * (1.0 - sz))))
        dyp = dyf * silu_z
        dyb = _bf(dyp)
        G = dyp * eacsE

        causal = _iota2((CHUNK, CHUNK), 0) >= _iota2((CHUNK, CHUNK), 1)
        ST = prev_ref[0]
        dST = dstate[...]
        for g in range(SSD_GROUPS):
            gs = slice(half * g, half * (g + 1))
            bs = slice(D_SSD + SSD_N * g, D_SSD + SSD_N * (g + 1))
            cs = slice(D_SSD + D_BC + SSD_N * g, D_SSD + D_BC + SSD_N * (g + 1))
            Bg = _bf(xbcv[:, bs])
            Cg = _bf(xbcv[:, cs])
            Gb = _bf(G[:, gs])
            STb = _bf(ST[:, gs])
            dSTb = _bf(dST[:, gs])
            dstate[:, gs] = dST[:, gs] * ealE[:, gs] + _dot_tn(Cg, Gb)
            yobuf[:, gs] = _dot(Cg, STb) * eacsE[:, gs]
            bdbuf[:, gs] = _dot(Bg, dSTb)
            dpost[:, cs] = _dot_nt(Gb, STb)
            dpost[:, bs] = _dot_nt(_bf(X[:, gs] * dsdE[:, gs]), dSTb)
            cbbuf[g] = _dot_nt(Cg, Bg)
            for r in range(SSD_R):
                h = g * SSD_R + r
                seg = acs[:, h:h + 1] - acs_row[h:h + 1, :]
                lmbuf[h] = jnp.where(causal, jnp.exp(jnp.where(causal, seg, 0.0)), 0.0)
        for h in range(SSD_HEADS):
            hs = slice(SSD_P * h, SSD_P * (h + 1))
            Mb = _bf(cbbuf[h // SSD_R] * lmbuf[h])
            dmbuf[h] = _dot_nt(dyb[:, hs], Xb[:, hs])
            dpost[:, hs] = _dot_tn(Mb, dyb[:, hs])
        lane16 = _iota2((1, SSD_HEADS), 1)
        sub16 = _iota2((SSD_HEADS, 1), 0)
        dacs_col = jnp.zeros((CHUNK, SSD_HEADS), F32)
        dacs_row = jnp.zeros((SSD_HEADS, CHUNK), F32)
        for g in range(SSD_GROUPS):
            bs = slice(D_SSD + SSD_N * g, D_SSD + SSD_N * (g + 1))
            cs = slice(D_SSD + D_BC + SSD_N * g, D_SSD + D_BC + SSD_N * (g + 1))
            cb = cbbuf[g]
            dcb = jnp.zeros((CHUNK, CHUNK), F32)
            for r in range(SSD_R):
                h = g * SSD_R + r
                dM = dmbuf[h]
                Lm = lmbuf[h]
                dcb = dcb + dM * Lm
                dseg = dM * (cb * Lm)
                dacs_col = dacs_col + jnp.sum(dseg, axis=-1, keepdims=True) * (lane16 == h).astype(F32)
                dacs_row = dacs_row - jnp.sum(dseg, axis=0, keepdims=True) * (sub16 == h).astype(F32)
            dcbb = _bf(dcb)
            dpost[:, bs] += _dot_tn(dcbb, _bf(xbcv[:, cs]))
            dpost[:, cs] += _dot(dcbb, _bf(xbcv[:, bs]))

        BD = bdbuf[...]
        dX = dpost[:, 0:D_SSD] + dsdE * BD
        dsd = jnp.exp(alast - acs)
        T = _headsum(X * BD, e) * dsd
        dalast = jnp.sum(T, axis=0, keepdims=True) + _headsum(
            jnp.sum(dST * ST, axis=0, keepdims=True), e) * jnp.exp(alast)
        is_last = (_iota2((CHUNK, 1), 0) == CHUNK - 1).astype(F32)
        dacs = dacs_col + _to_cols(dacs_row) + _headsum(dyp * yobuf[...], e) - T + is_last * dalast
        triu = (_iota2((CHUNK, CHUNK), 0) <= _iota2((CHUNK, CHUNK), 1)).astype(BF16)
        da = _dot01(triu, dacs)
        ddt_tot = _headsum(dX * xs, e) + da * A
        galog_ref[...] += jnp.sum(da * dt, axis=0, keepdims=True) * A
        ddtraw = ddt_tot * _sigmoid(dtraw)
        gdtb_ref[...] += jnp.sum(ddtraw, axis=0, keepdims=True)
        gdsk_ref[...] += _headsum(jnp.sum(dyp * xs, axis=0, keepdims=True), e)
        ddt_ref[...] = jnp.zeros_like(ddt_ref)
        ddt_ref[:, 0:SSD_HEADS] = ddtraw
        dpost[:, 0:D_SSD] = dX * dtE + dskE * dyp

        dconv = dpost[...] * (sig * (1.0 + u * (1.0 - sig)))
        gcb_ref[...] += jnp.sum(dconv, axis=0, keepdims=True)
        for k in range(CONV_K):
            gcw_ref[k:k + 1, :] += jnp.sum(dconv * taps[k], axis=0, keepdims=True)
        later = _shifted_rows(dconv, dhead[...], smat2_ref)
        dx = cw_ref[CONV_K - 1:CONV_K, :] * dconv
        for k in range(CONV_K - 1):
            dx = dx + cw_ref[k:k + 1, :] * later[k]
        dxbc_ref[...] = _bf(dx)
        dhead[...] = dconv[0:HALO, :]

    full = lambda shape: pl.BlockSpec(shape, lambda i: (0, 0))
    rev = lambda wd: pl.BlockSpec((CHUNK, wd), lambda i: (nc - 1 - i, 0))
    return pl.pallas_call(
        body, name="ssd_bwd", grid=(nc,),
        in_specs=[
            rev(D_SSD), rev(D_SSD), rev(D_SSD), rev(D_XBC),
            pl.BlockSpec((HALO, D_XBC), lambda i: (jnp.maximum((nc - 1 - i) * (CHUNK // HALO) - 1, 0), 0)),
            rev(DT_PAD),
            pl.BlockSpec((1, SSD_N, D_SSD), lambda i: (nc - 1 - i, 0, 0)),
            full((CONV_K, D_XBC)), full((1, D_XBC)), full((1, SSD_HEADS)), full((1, SSD_HEADS)), full((1, SSD_HEADS)),
            full((1, D_SSD)), full((3 * CHUNK, 2 * (CHUNK + HALO))), full((3 * CHUNK, 2 * (CHUNK + HALO))),
        ],
        out_specs=[
            rev(D_SSD), rev(D_XBC), rev(DT_PAD),
            full((CONV_K, D_XBC)), full((1, D_XBC)), full((1, SSD_HEADS)), full((1, SSD_HEADS)), full((1, SSD_HEADS)),
            full((1, D_SSD)),
        ],
        out_shape=[
            jax.ShapeDtypeStruct((L, D_SSD), BF16), jax.ShapeDtypeStruct((L, D_XBC), BF16),
            jax.ShapeDtypeStruct((L, DT_PAD), F32),
            jax.ShapeDtypeStruct((CONV_K, D_XBC), F32), jax.ShapeDtypeStruct((1, D_XBC), F32),
            jax.ShapeDtypeStruct((1, SSD_HEADS), F32), jax.ShapeDtypeStruct((1, SSD_HEADS), F32),
            jax.ShapeDtypeStruct((1, SSD_HEADS), F32), jax.ShapeDtypeStruct((1, D_SSD), F32),
        ],
        scratch_shapes=[
            pltpu.VMEM((SSD_N, D_SSD), F32),
            pltpu.VMEM((HALO, D_XBC), F32),
            pltpu.VMEM((CHUNK, D_XBC), F32),
            pltpu.VMEM((CHUNK, D_SSD), F32),
            pltpu.VMEM((CHUNK, D_SSD), F32),
            pltpu.VMEM((SSD_HEADS, CHUNK, CHUNK), F32),
            pltpu.VMEM((SSD_HEADS, CHUNK, CHUNK), F32),
            pltpu.VMEM((SSD_GROUPS, CHUNK, CHUNK), F32),
        ],
        compiler_params=_cparams(("arbitrary",)),
    )(dy, z, ypre, xbc, xbc, dtp, prev, conv_w, conv_b, dt_bias, a_log, d_skip, norm_w, _shift_matrix((13, 14, 15)),
      _shift_matrix((3, 2, 1)))


def _rope_tables(pos_ref, inv_ref):
    ang = pos_ref[...].astype(F32) * inv_ref[...]
    d = _iota2((1, 2 * ATT_HD), 1) % ATT_HD
    s = jnp.sin(ang)
    return jnp.cos(ang), jnp.where(d < ROPE_DIM // 2, -s, 0.0), jnp.where((d >= ROPE_DIM // 2) & (d < ROPE_DIM), s, 0.0)


def _rope(t, tabs):
    c, s1, s2 = tabs
    n = t.shape[1]
    rep = n // c.shape[1]
    return (t * jnp.tile(c, (1, rep)) + pltpu.roll(t, n - ROPE_DIM // 2, 1) * jnp.tile(s1, (1, rep))
            + pltpu.roll(t, ROPE_DIM // 2, 1) * jnp.tile(s2, (1, rep)))


def _rope_t(t, tabs):
    c, s1, s2 = tabs
    n = t.shape[1]
    rep = n // c.shape[1]
    return (t * jnp.tile(c, (1, rep)) + pltpu.roll(t * jnp.tile(s1, (1, rep)), ROPE_DIM // 2, 1)
            + pltpu.roll(t * jnp.tile(s2, (1, rep)), n - ROPE_DIM // 2, 1))


def _stack_heads(t, j):
    return jnp.concatenate([t[:, ATT_HD * (j * ATT_R + r):ATT_HD * (j * ATT_R + r + 1)] for r in range(ATT_R)], axis=0)


def _swa_mask_t(first):
    si = _iota2((2 * WINDOW, ATT_R * WINDOW), 0)
    qi = _iota2((2 * WINDOW, ATT_R * WINDOW), 1) % WINDOW
    band = (si > qi) & (si <= qi + WINDOW)
    return band & (jnp.logical_not(first) | (si >= WINDOW))


def _head_rows(ref, j):
    if ref.shape[0] == 1:
        parts = [jnp.broadcast_to(ref[:, j * ATT_R + r:j * ATT_R + r + 1], (1, WINDOW)) for r in range(ATT_R)]
    else:
        parts = [ref[j * ATT_R + r:j * ATT_R + r + 1, :] for r in range(ATT_R)]
    return jnp.concatenate(parts, axis=1)


def _swa_fwd(q, g, kv, sinks):
    L = q.shape[0]
    nb = L // WINDOW
    scale = ATT_HD ** -0.5

    def body(q_ref, g_ref, kvc_ref, kvp_ref, sink_ref, y_ref, o_ref, lse_ref, otbuf):
        n = pl.program_id(0)
        kk = jnp.concatenate([kvp_ref[:, 0:D_KV], kvc_ref[:, 0:D_KV]], axis=0)
        vv = jnp.concatenate([kvp_ref[:, D_KV:2 * D_KV], kvc_ref[:, D_KV:2 * D_KV]], axis=0)
        valid = _swa_mask_t(n == 0)
        qv = q_ref[...]
        for j in range(ATT_KVH):
            js = slice(ATT_HD * j, ATT_HD * (j + 1))
            st = _dot_nt(kk[:, js], _stack_heads(qv, j)) * scale
            st = jnp.where(valid, st, NEG_BIG)
            sink = _head_rows(sink_ref, j)
            m = jnp.maximum(jnp.max(st, axis=0, keepdims=True), sink)
            p = jnp.exp(st - m)
            denom = jnp.sum(p, axis=0, keepdims=True) + jnp.exp(sink - m)
            ot = _dot_tn(vv[:, js], _bf(p)) * (1.0 / denom)
            lse = m + jnp.log(denom)
            for r in range(ATT_R):
                h = j * ATT_R + r
                otbuf[ATT_HD * h:ATT_HD * (h + 1), :] = ot[:, WINDOW * r:WINDOW * (r + 1)]
                lse_ref[h:h + 1, :] = lse[:, WINDOW * r:WINDOW * (r + 1)]
        o = otbuf[...].T
        o_ref[...] = o
        gv = g_ref[...]
        y_ref[...] = _bf(o * (gv * _sigmoid(gv)))

    cur = lambda wd: pl.BlockSpec((WINDOW, wd), lambda n: (n, 0))
    prv = lambda wd: pl.BlockSpec((WINDOW, wd), lambda n: (jnp.maximum(n - 1, 0), 0))
    return pl.pallas_call(
        body, name="swa_fwd", grid=(nb,),
        in_specs=[cur(D_ATT), cur(D_ATT), cur(2 * D_KV), prv(2 * D_KV), pl.BlockSpec((1, ATT_QH), lambda n: (0, 0))],
        out_specs=[cur(D_ATT), cur(D_ATT), pl.BlockSpec((ATT_QH, WINDOW), lambda n: (0, n))],
        out_shape=[jax.ShapeDtypeStruct((L, D_ATT), BF16), jax.ShapeDtypeStruct((L, D_ATT), F32),
                   jax.ShapeDtypeStruct((ATT_QH, L), F32)],
        scratch_shapes=[pltpu.VMEM((D_ATT, WINDOW), F32)],
        compiler_params=_cparams(("parallel",)),
    )(q, g, kv, kv, sinks)


def _swa_bwd(dy, q, g, kv, o, lse, pos, inv, sinks):
    L = q.shape[0]
    nb = L // WINDOW
    scale = ATT_HD ** -0.5

    def body(dy_ref, q_ref, g_ref, kvc_ref, kvp_ref, o_ref, lse_ref, posc_ref, posp_ref, inv_ref, sink_ref,
             dq_ref, dg_ref, dkv_ref, dsink_ref, carry, dqbuf, dkbuf, dvbuf):
        n = pl.program_id(0)

        @pl.when(n == 0)
        def _():
            dsink_ref[...] = jnp.zeros_like(dsink_ref)

        @pl.when(n < nb)
        def _():
            tc = _rope_tables(posc_ref, inv_ref)
            tp = _rope_tables(posp_ref, inv_ref)
            kk = jnp.concatenate([kvp_ref[:, 0:D_KV], kvc_ref[:, 0:D_KV]], axis=0)
            vv = jnp.concatenate([kvp_ref[:, D_KV:2 * D_KV], kvc_ref[:, D_KV:2 * D_KV]], axis=0)
            valid = _swa_mask_t(n == 0)
            qv = q_ref[...]
            gv = g_ref[...]
            sg = _sigmoid(gv)
            dyv = dy_ref[...]
            ov = o_ref[...]
            dg_ref[...] = _bf(dyv * ov * (sg * (1.0 + gv * (1.0 - sg))))
            do = dyv * (gv * sg)
            dod = do * ov
            ones = jnp.ones((8, ATT_HD), BF16)
            lane16 = _iota2((1, ATT_QH), 1)
            dsink = jnp.zeros((1, ATT_QH), F32)
            for j in range(ATT_KVH):
                js = slice(ATT_HD * j, ATT_HD * (j + 1))
                kj = kk[:, js]
                vj = vv[:, js]
                qs = _stack_heads(qv, j)
                dos = _bf(_stack_heads(do, j))
                hi, lo = _hi_lo(_stack_heads(dod, j))
                delta = (_dot_nt(ones, hi) + _dot_nt(ones, lo))[0:1]
                lse = _head_rows(lse_ref, j)
                st = _dot_nt(kj, qs) * scale
                pt = jnp.exp(jnp.where(valid, st, NEG_BIG) - lse)
                dst = _bf(pt * (_dot_nt(vj, dos) - delta))
                dqt = _dot_tn(kj, dst) * scale
                dkbuf[:, js] = _dot(dst, qs) * scale
                dvbuf[:, js] = _dot(_bf(pt), dos)
                sd = jnp.exp(_head_rows(sink_ref, j) - lse) * delta
                for r in range(ATT_R):
                    h = j * ATT_R + r
                    ls = slice(WINDOW * r, WINDOW * (r + 1))
                    dqbuf[ATT_HD * h:ATT_HD * (h + 1), :] = dqt[:, ls]
                    dsink = dsink - jnp.sum(sd[:, ls], axis=1, keepdims=True) * (lane16 == h).astype(F32)
            dsink_ref[...] += dsink
            dq_ref[...] = _bf(_rope_t(dqbuf[...].T, tc))
            dkp = _rope_t(dkbuf[0:WINDOW, :], tp)
            dkc = _rope_t(dkbuf[WINDOW:2 * WINDOW, :], tc)

            @pl.when(n > 0)
            def _():
                dkv_ref[:, 0:D_KV] = _bf(carry[:, 0:D_KV] + dkp)
                dkv_ref[:, D_KV:2 * D_KV] = _bf(carry[:, D_KV:2 * D_KV] + dvbuf[0:WINDOW, :])

            carry[:, 0:D_KV] = dkc
            carry[:, D_KV:2 * D_KV] = dvbuf[WINDOW:2 * WINDOW, :]

        @pl.when(n == nb)
        def _():
            dkv_ref[...] = _bf(carry[...])

    last = nb - 1
    cur = lambda wd: pl.BlockSpec((WINDOW, wd), lambda n: (jnp.minimum(n, last), 0))
    prv = lambda wd: pl.BlockSpec((WINDOW, wd), lambda n: (jnp.maximum(jnp.minimum(n, last) - 1, 0), 0))
    return pl.pallas_call(
        body, name="swa_bwd", grid=(nb + 1,),
        in_specs=[cur(D_ATT), cur(D_ATT), cur(D_ATT), cur(2 * D_KV), prv(2 * D_KV), cur(D_ATT),
                  pl.BlockSpec((ATT_QH, WINDOW), lambda n: (0, jnp.minimum(n, last))), cur(1), prv(1),
                  pl.BlockSpec((1, 2 * ATT_HD), lambda n: (0, 0)), pl.BlockSpec((1, ATT_QH), lambda n: (0, 0))],
        out_specs=[cur(D_ATT), cur(D_ATT),
                   pl.BlockSpec((WINDOW, 2 * D_KV), lambda n: (jnp.maximum(n - 1, 0), 0)),
                   pl.BlockSpec((1, ATT_QH), lambda n: (0, 0))],
        out_shape=[jax.ShapeDtypeStruct((L, D_ATT), BF16), jax.ShapeDtypeStruct((L, D_ATT), BF16),
                   jax.ShapeDtypeStruct((L, 2 * D_KV), BF16), jax.ShapeDtypeStruct((1, ATT_QH), F32)],
        scratch_shapes=[pltpu.VMEM((WINDOW, 2 * D_KV), F32), pltpu.VMEM((D_ATT, WINDOW), F32),
                        pltpu.VMEM((2 * WINDOW, D_KV), F32), pltpu.VMEM((2 * WINDOW, D_KV), F32)],
        compiler_params=_cparams(("arbitrary",)),
    )(dy, q, g, kv, kv, o, lse, pos, pos, inv, sinks)


def _out_ln_loss(y_ssd, y_att, x, target, w_out, ln_g, ln_b):
    L = x.shape[0]
    tm = min(ROW_TILE, L)
    nt = L // tm
    inv_d = 1.0 / D_MODEL

    def body(ys_ref, ya_ref, x_ref, t_ref, w_ref, g_ref, b_ref, dr_ref, dys_ref, dya_ref, loss_ref, gg_ref, gb_ref,
             gwo_ref, acc_ref):
        i = pl.program_id(0)

        @pl.when(i == 0)
        def _():
            loss_ref[...] = jnp.zeros_like(loss_ref)
            gg_ref[...] = jnp.zeros_like(gg_ref)
            gb_ref[...] = jnp.zeros_like(gb_ref)
            acc_ref[...] = jnp.zeros_like(acc_ref)

        halves = [slice(0, tm // 2), slice(tm // 2, tm)]
        hs = [_dot(_bf(ys_ref[rs, :]), w_ref[0:D_SSD, :]) + _dot(_bf(ya_ref[rs, :]), w_ref[D_SSD:D_MIX, :]) for rs in halves]
        gam = g_ref[...]
        for rs, h in zip(halves, hs):
            r = ALPHA * x_ref[rs, :] + h
            mu = jnp.mean(r, axis=-1, keepdims=True)
            xc = r - mu
            rstd = lax.rsqrt(jnp.mean(xc * xc, axis=-1, keepdims=True) + LN_EPS)
            xhat = xc * rstd
            diff = xhat * gam + b_ref[...] - t_ref[rs, :]
            part = jnp.sum(jnp.sum(diff * diff, axis=-1, keepdims=True), axis=0, keepdims=True)
            loss_ref[...] += (0.5 * inv_d) * part
            dout = diff * inv_d
            gg_ref[...] += jnp.sum(dout * xhat, axis=0, keepdims=True)
            gb_ref[...] += jnp.sum(dout, axis=0, keepdims=True)
            dxh = dout * gam
            dr_ref[rs, :] = rstd * (dxh - jnp.mean(dxh, axis=-1, keepdims=True)
                                    - xhat * jnp.mean(dxh * xhat, axis=-1, keepdims=True))
        for rs in halves:
            drh = _bf(dr_ref[rs, :])
            dys_ref[rs, :] = _dot_nt(drh, w_ref[0:D_SSD, :])
            dya_ref[rs, :] = _dot_nt(drh, w_ref[D_SSD:D_MIX, :])
        drb = _bf(dr_ref[...])
        acc_ref[0:D_SSD, :] += _dot_tn(_bf(ys_ref[...]), drb)
        acc_ref[D_SSD:D_MIX, :] += _dot_tn(_bf(ya_ref[...]), drb)

        @pl.when(i == nt - 1)
        def _():
            gwo_ref[...] = _bf(acc_ref[...])

    row = pl.BlockSpec((tm, D_MODEL), lambda i: (i, 0))
    vec = pl.BlockSpec((1, D_MODEL), lambda i: (0, 0))
    return pl.pallas_call(
        body, name="out_ln_loss", grid=(nt,),
        in_specs=[row, row, row, row, pl.BlockSpec((D_MIX, D_MODEL), lambda i: (0, 0), pipeline_mode=pl.Buffered(1)), vec, vec],
        out_specs=[row, row, row, pl.BlockSpec((1, 128), lambda i: (0, 0)), vec, vec,
                   pl.BlockSpec((D_MIX, D_MODEL), lambda i: (0, 0))],
        out_shape=[jax.ShapeDtypeStruct((L, D_MODEL), F32)] * 3 + [jax.ShapeDtypeStruct((1, 128), F32)]
        + [jax.ShapeDtypeStruct((1, D_MODEL), F32)] * 2 + [jax.ShapeDtypeStruct((D_MIX, D_MODEL), BF16)],
        scratch_shapes=[pltpu.VMEM((D_MIX, D_MODEL), F32)],
        compiler_params=_cparams(("arbitrary",)),
    )(y_ssd, y_att, x, target, w_out, ln_g, ln_b)


def _local_step(x, pos, target, w, get_w_out, token, conv_w, conv_b, dt_bias, a_log, d_skip, norm_w, sinks, ln_g, ln_b):
    inv8 = ROPE_THETA ** (-jnp.arange(0, ROPE_DIM, 2, dtype=F32) / ROPE_DIM)
    inv = jnp.tile(jnp.concatenate([inv8, inv8, jnp.zeros((ATT_HD - ROPE_DIM,), F32)]), 2).reshape(1, 2 * ATT_HD)
    inv = inv + token

    z, g, q, xbc, kv, dtp, xb = _in_proj(x, w, pos, inv)
    y_ssd, y_pre, prev = _ssd_fwd(z, xbc, dtp, conv_w, conv_b, dt_bias, a_log, d_skip, norm_w)
    y_att, o, lse = _swa_fwd(q, g, kv, sinks)
    w_out = get_w_out(lse)
    dr, dy_ssd, dy_att, loss, g_ln_g, g_ln_b, gw_out = _out_ln_loss(y_ssd, y_att, x, target, w_out, ln_g, ln_b)
    w_out_red = _reduce_w_out_start(gw_out.reshape(N_CHIPS, W_OUT_ROWS, D_MODEL), loss)
    inv = inv + w_out_red[16][0:1, :]
    dq, dg, dkv, g_sinks = _swa_bwd(dy_att, q, g, kv, o, lse, pos, inv, sinks)
    dz, dxbc, ddt, g_conv_w, g_conv_b, g_dt_bias, g_a_log, g_d_skip, g_norm_w = _ssd_bwd(
        dy_ssd, z, y_pre, xbc, dtp, prev, conv_w, conv_b, dt_bias, a_log, d_skip, norm_w)
    gw_z, gw_g, gw_q = _matmuls_tn([dz, dg, dq], xb, "gw_zgq")
    gw_xbc, gw_kv, gw_dt = _matmuls_tn([dxbc, dkv, ddt], xb, "gw_xbc_kv_dt")
    gw_in = jnp.concatenate([gw_z, gw_xbc, gw_dt[0:SSD_HEADS], gw_q, gw_kv, gw_g], axis=0)
    small = dict(conv_w=g_conv_w, conv_b=g_conv_b, dt_bias=g_dt_bias, a_log=g_a_log, d_skip=g_d_skip,
                 ssd_norm_w=g_norm_w, attn_sinks=g_sinks, ln_g=g_ln_g, ln_b=g_ln_b)
    return loss, (dr, dz, dg, dq, dxbc, dkv, ddt, w), gw_in, w_out_red, small


def _mesh_pos():
    return lax.axis_index("x"), lax.axis_index("y"), lax.axis_index("c")


def _gather_weights(w_in_s, conv_w_s):
    hr = w_in_s.shape[0] // 2
    qa = 336
    quarters = ((0, qa), (qa, hr - qa))

    def body(win_ref, cw_ref, owin_ref, ocw_ref, stage, send_sems, recv_sems, small_send, small_recv, local_sems):
        x, y, c = _mesh_pos()
        me = 2 * x + y
        sibling = (x, y, 1 - c)
        xn, yn, dg = (1 - x, y), (x, 1 - y), (1 - x, 1 - y)
        chips = [xn, yn, dg]
        load = pltpu.make_async_copy(win_ref, stage, local_sems.at[1])
        load.start()
        locals_ = [pltpu.make_async_copy(cw_ref, ocw_ref.at[me], local_sems.at[0])]
        for cp in locals_:
            cp.start()
        started = []

        def piece(ref, chip, half, q):
            off, n = quarters[q]
            return ref.at[2 * chip[0] + chip[1]].at[pl.ds(half * hr + off, n), :]

        def mine(q):
            off, n = quarters[q]
            return win_ref.at[pl.ds(c * hr + off, n), :]

        def copy(src, dst, k, to):
            return pltpu.make_async_remote_copy(src_ref=src, dst_ref=dst, send_sem=send_sems.at[k], recv_sem=recv_sems.at[k],
                                                device_id=to, device_id_type=MESH)

        def go(cp):
            cp.start()
            started.append(cp)

        go(copy(mine(0), piece(owin_ref, (x, y), c, 0), 0, (*xn, c)))
        go(copy(mine(1), piece(owin_ref, (x, y), c, 1), 2, (*yn, c)))
        go(copy(mine(1), piece(owin_ref, (x, y), c, 1), 1, (*xn, c)))
        go(copy(mine(0), piece(owin_ref, (x, y), c, 0), 3, (*yn, c)))
        for j, (px, py) in enumerate(chips):
            cp = pltpu.make_async_remote_copy(
                src_ref=cw_ref, dst_ref=ocw_ref.at[me], send_sem=small_send.at[j], recv_sem=small_recv.at[j],
                device_id=(px, py, c), device_id_type=MESH)
            go(cp)
        load.wait()
        store = pltpu.make_async_copy(stage, owin_ref.at[me], local_sems.at[2])
        store.start()
        locals_.append(store)
        arrivals = [(0, xn, 0, (4, (*yn, c))), (2, yn, 1, (5, (*xn, c))), (1, xn, 1, None), (3, yn, 0, None),
                    (4, dg, 0, None), (5, dg, 1, None)]
        for n, (k, chip, q, onward) in enumerate(arrivals):
            blk = piece(owin_ref, chip, c, q)
            copy(blk, blk, k, sibling).wait_recv()
            if onward is not None:
                go(copy(blk, blk, onward[0], onward[1]))
            go(copy(blk, blk, 6 + n, sibling))
        for n, (k, chip, q, onward) in enumerate(arrivals):
            blk = piece(owin_ref, chip, 1 - c, q)
            copy(blk, blk, 6 + n, sibling).wait_recv()
        for j in range(3):
            pltpu.make_async_remote_copy(
                src_ref=cw_ref, dst_ref=ocw_ref.at[me], send_sem=small_send.at[j], recv_sem=small_recv.at[j],
                device_id=sibling, device_id_type=MESH).wait_recv()
        for cp in started:
            cp.wait_send()
        for cp in locals_:
            cp.wait()

    any_spec = pl.BlockSpec(memory_space=pl.ANY)
    return pl.pallas_call(
        body, name="gather_weights",
        in_specs=[any_spec] * 2, out_specs=[any_spec] * 2,
        out_shape=[jax.ShapeDtypeStruct((N_CHIPS,) + a.shape, a.dtype) for a in (w_in_s, conv_w_s)],
        scratch_shapes=[pltpu.VMEM(w_in_s.shape, w_in_s.dtype),
                        pltpu.SemaphoreType.DMA((12,)), pltpu.SemaphoreType.DMA((12,)),
                        pltpu.SemaphoreType.DMA((3,)), pltpu.SemaphoreType.DMA((3,)), pltpu.SemaphoreType.DMA((3,))],
    )(w_in_s, conv_w_s)


_HBM = pl.BlockSpec(memory_space=pltpu.HBM)
_SEM = pl.BlockSpec(memory_space=pltpu.SEMAPHORE)
_EFFECT = pltpu.SideEffectType.DATAFLOW_SIDE_EFFECTING


def _gather_w_out_start(w_out_s, after):
    def body(src_ref, land_ref, after_ref, s0, s1, s2, r0, r1, r2, src_thru, land_thru, token):
        x, y, c = _mesh_pos()
        me = 2 * x + y
        chips = [(1 - x, y), (x, 1 - y), (1 - x, 1 - y)]
        for (px, py), s, r in zip(chips, (s0, s1, s2), (r0, r1, r2)):
            pltpu.make_async_remote_copy(src_ref=src_ref, dst_ref=land_ref.at[me], send_sem=s, recv_sem=r,
                                         device_id=(px, py, c), device_id_type=MESH).start()
        token[...] = jnp.zeros_like(token)

    sem = pltpu.SemaphoreType.DMA(())
    land = lax.empty((N_CHIPS,) + w_out_s.shape, w_out_s.dtype)
    return pl.pallas_call(
        body, name="gather_w_out_start",
        out_shape=(sem,) * 6 + (pltpu.HBM(w_out_s.shape, w_out_s.dtype), pltpu.HBM(land.shape, land.dtype),
                                jax.ShapeDtypeStruct((8, 128), F32)),
        in_specs=(_HBM, _HBM, pl.BlockSpec(memory_space=pl.ANY)),
        out_specs=(_SEM,) * 6 + (_HBM, _HBM, pl.BlockSpec(memory_space=pltpu.VMEM)),
        input_output_aliases={0: 6, 1: 7},
        compiler_params=pltpu.CompilerParams(has_side_effects=_EFFECT),
    )(pltpu.with_memory_space_constraint(w_out_s, pltpu.HBM), pltpu.with_memory_space_constraint(land, pltpu.HBM), after)


def _gather_w_out_wait(sems, src_thru, land_thru, after):
    def body(src_ref, land_ref, s0, s1, s2, r0, r1, r2, after_ref, src_dead, got_ref):
        x, y, c = _mesh_pos()
        chips = [(1 - x, y), (x, 1 - y), (1 - x, 1 - y)]
        for (px, py), s, r in zip(chips, (s0, s1, s2), (r0, r1, r2)):
            cp = pltpu.make_async_remote_copy(src_ref=src_ref, dst_ref=land_ref.at[2 * px + py], send_sem=s, recv_sem=r,
                                              device_id=(px, py, c), device_id_type=MESH)
            cp.wait_send()
            cp.wait_recv()

    return pl.pallas_call(
        body, name="gather_w_out_wait",
        out_shape=(pltpu.HBM(src_thru.shape, src_thru.dtype), pltpu.HBM(land_thru.shape, land_thru.dtype)),
        in_specs=(_HBM, _HBM) + (_SEM,) * 6 + (pl.BlockSpec(memory_space=pl.ANY),),
        out_specs=(_HBM, _HBM), input_output_aliases={0: 0, 1: 1},
        compiler_params=pltpu.CompilerParams(has_side_effects=_EFFECT),
    )(src_thru, land_thru, *sems, after)[1]


def _pair_start(gw_in, after):
    hr = gw_in.shape[1] // 2

    def body(src_ref, land_ref, after_ref, *refs):
        x, y, c = _mesh_pos()
        for j in range(N_CHIPS):
            pltpu.make_async_remote_copy(
                src_ref=src_ref.at[j, pl.ds((1 - c) * hr, hr), :], dst_ref=land_ref.at[j], send_sem=refs[j],
                recv_sem=refs[N_CHIPS + j], device_id=(x, y, 1 - c), device_id_type=MESH).start()
        refs[10][...] = jnp.zeros_like(refs[10])

    sem = pltpu.SemaphoreType.DMA(())
    land = lax.empty((N_CHIPS, hr, D_MODEL), F32)
    return pl.pallas_call(
        body, name="pair_start",
        out_shape=(sem,) * 8 + (pltpu.HBM(gw_in.shape, F32), pltpu.HBM(land.shape, F32), jax.ShapeDtypeStruct((8, 128), F32)),
        in_specs=(_HBM, _HBM, pl.BlockSpec(memory_space=pl.ANY)),
        out_specs=(_SEM,) * 8 + (_HBM, _HBM, pl.BlockSpec(memory_space=pltpu.VMEM)),
        input_output_aliases={0: 8, 1: 9},
        compiler_params=pltpu.CompilerParams(has_side_effects=_EFFECT),
    )(pltpu.with_memory_space_constraint(gw_in, pltpu.HBM), pltpu.with_memory_space_constraint(land, pltpu.HBM), after)


def _pair_wait(sems, gw_thru, land_thru, after):
    hr = land_thru.shape[1]

    def body(src_ref, land_ref, *refs):
        x, y, c = _mesh_pos()
        for j in range(N_CHIPS):
            cp = pltpu.make_async_remote_copy(
                src_ref=src_ref.at[j, pl.ds((1 - c) * hr, hr), :], dst_ref=land_ref.at[j], send_sem=refs[j],
                recv_sem=refs[N_CHIPS + j], device_id=(x, y, 1 - c), device_id_type=MESH)
            cp.wait_send()
            cp.wait_recv()

    return pl.pallas_call(
        body, name="pair_wait",
        out_shape=(pltpu.HBM(gw_thru.shape, F32), pltpu.HBM(land_thru.shape, F32)),
        in_specs=(_HBM, _HBM) + (_SEM,) * 8 + (pl.BlockSpec(memory_space=pl.ANY),),
        out_specs=(_HBM, _HBM), input_output_aliases={0: 0, 1: 1},
        compiler_params=pltpu.CompilerParams(has_side_effects=_EFFECT),
    )(gw_thru, land_thru, *sems, after)


def _chip_start(s_in, after):
    def body(src_ref, land_ref, after_ref, *refs):
        x, y, c = _mesh_pos()
        me = 2 * x + y
        for j, (px, py) in enumerate([(1 - x, y), (x, 1 - y), (1 - x, 1 - y)]):
            pltpu.make_async_remote_copy(
                src_ref=src_ref.at[2 * px + py], dst_ref=land_ref.at[me], send_sem=refs[j], recv_sem=refs[3 + j],
                device_id=(px, py, c), device_id_type=MESH).start()
        refs[8][...] = jnp.zeros_like(refs[8])

    sem = pltpu.SemaphoreType.DMA(())
    land = lax.empty(s_in.shape, s_in.dtype)
    return pl.pallas_call(
        body, name="chip_start",
        out_shape=(sem,) * 6 + (pltpu.HBM(s_in.shape, s_in.dtype), pltpu.HBM(land.shape, land.dtype),
                                jax.ShapeDtypeStruct((8, 128), F32)),
        in_specs=(_HBM, _HBM, pl.BlockSpec(memory_space=pl.ANY)),
        out_specs=(_SEM,) * 6 + (_HBM, _HBM, pl.BlockSpec(memory_space=pltpu.VMEM)),
        input_output_aliases={0: 6, 1: 7},
        compiler_params=pltpu.CompilerParams(has_side_effects=_EFFECT),
    )(pltpu.with_memory_space_constraint(s_in, pltpu.HBM), pltpu.with_memory_space_constraint(land, pltpu.HBM), after)


def _chip_wait(sems, s_thru, land_thru, after):
    def body(src_ref, land_ref, *refs):
        x, y, c = _mesh_pos()
        for j, (px, py) in enumerate([(1 - x, y), (x, 1 - y), (1 - x, 1 - y)]):
            cp = pltpu.make_async_remote_copy(
                src_ref=src_ref.at[2 * px + py], dst_ref=land_ref.at[2 * px + py], send_sem=refs[j], recv_sem=refs[3 + j],
                device_id=(px, py, c), device_id_type=MESH)
            cp.wait_send()
            cp.wait_recv()

    return pl.pallas_call(
        body, name="chip_wait",
        out_shape=(pltpu.HBM(s_thru.shape, s_thru.dtype), pltpu.HBM(land_thru.shape, land_thru.dtype)),
        in_specs=(_HBM, _HBM) + (_SEM,) * 6 + (pl.BlockSpec(memory_space=pl.ANY),),
        out_specs=(_HBM, _HBM), input_output_aliases={0: 0, 1: 1},
        compiler_params=pltpu.CompilerParams(has_side_effects=_EFFECT),
    )(s_thru, land_thru, *sems, after)


def _pair_share(h_in, small):
    def body(hin_ref, sm_ref, rin_ref, slots_ref, send_sems, recv_sems, small_send, small_recv, local_sem):
        x, y, c = _mesh_pos()
        dev = 4 * x + 2 * y + c
        mine = pltpu.make_async_copy(sm_ref, slots_ref.at[dev], local_sem)
        mine.start()
        share = pltpu.make_async_remote_copy(
            src_ref=hin_ref, dst_ref=rin_ref, send_sem=send_sems.at[0], recv_sem=recv_sems.at[0],
            device_id=(x, y, 1 - c), device_id_type=MESH)
        share.start()
        started = []
        for k in range(1, 8):
            peer = (x ^ ((k >> 2) & 1), y ^ ((k >> 1) & 1), c ^ (k & 1))
            cp = pltpu.make_async_remote_copy(
                src_ref=sm_ref, dst_ref=slots_ref.at[dev], send_sem=small_send.at[k - 1], recv_sem=small_recv.at[k - 1],
                device_id=peer, device_id_type=MESH)
            cp.start()
            started.append(cp)
        share.wait()
        for k in range(1, 8):
            pltpu.make_async_remote_copy(
                src_ref=sm_ref, dst_ref=slots_ref.at[dev], send_sem=small_send.at[k - 1], recv_sem=small_recv.at[k - 1],
                device_id=(x, y, 1 - c), device_id_type=MESH).wait_recv()
        for cp in started:
            cp.wait_send()
        mine.wait()

    any_spec = pl.BlockSpec(memory_space=pl.ANY)
    return pl.pallas_call(
        body, name="pair_share",
        in_specs=[any_spec] * 2, out_specs=[any_spec] * 2,
        out_shape=[jax.ShapeDtypeStruct(h_in.shape, F32), jax.ShapeDtypeStruct((8,) + small.shape, F32)],
        scratch_shapes=[pltpu.SemaphoreType.DMA((1,)), pltpu.SemaphoreType.DMA((1,)),
                        pltpu.SemaphoreType.DMA((7,)), pltpu.SemaphoreType.DMA((7,)), pltpu.SemaphoreType.DMA],
    )(h_in, small)


def _reduce_w_out_start(slabs, after):
    def body(src_ref, land_ref, after_ref, *refs):
        x, y, c = _mesh_pos()
        me = 4 * x + 2 * y + c
        for k in range(1, 8):
            px, py, pc = x ^ ((k >> 2) & 1), y ^ ((k >> 1) & 1), c ^ (k & 1)
            pltpu.make_async_remote_copy(src_ref=src_ref.at[2 * px + py], dst_ref=land_ref.at[me], send_sem=refs[k - 1],
                                         recv_sem=refs[6 + k], device_id=(px, py, pc), device_id_type=MESH).start()
        refs[16][...] = jnp.zeros_like(refs[16])

    sem = pltpu.SemaphoreType.DMA(())
    land = lax.empty((8,) + slabs.shape[1:], slabs.dtype)
    return pl.pallas_call(
        body, name="reduce_w_out_start",
        out_shape=(sem,) * 14 + (pltpu.HBM(slabs.shape, slabs.dtype), pltpu.HBM(land.shape, land.dtype),
                                 jax.ShapeDtypeStruct((8, 128), F32)),
        in_specs=(_HBM, _HBM, pl.BlockSpec(memory_space=pl.ANY)),
        out_specs=(_SEM,) * 14 + (_HBM, _HBM, pl.BlockSpec(memory_space=pltpu.VMEM)),
        input_output_aliases={0: 14, 1: 15},
        compiler_params=pltpu.CompilerParams(has_side_effects=_EFFECT),
    )(pltpu.with_memory_space_constraint(slabs, pltpu.HBM), pltpu.with_memory_space_constraint(land, pltpu.HBM), after)


def _reduce_w_out_wait(sems, slabs_thru, land_thru, after):
    def body(src_ref, land_ref, *refs):
        x, y, c = _mesh_pos()
        for k in range(1, 8):
            px, py, pc = x ^ ((k >> 2) & 1), y ^ ((k >> 1) & 1), c ^ (k & 1)
            cp = pltpu.make_async_remote_copy(
                src_ref=src_ref.at[2 * px + py], dst_ref=land_ref.at[4 * px + 2 * py + pc], send_sem=refs[k - 1],
                recv_sem=refs[6 + k], device_id=(px, py, pc), device_id_type=MESH)
            cp.wait_send()
            cp.wait_recv()

    return pl.pallas_call(
        body, name="reduce_w_out_wait",
        out_shape=(pltpu.HBM(slabs_thru.shape, slabs_thru.dtype), pltpu.HBM(land_thru.shape, land_thru.dtype)),
        in_specs=(_HBM, _HBM) + (_SEM,) * 14 + (pl.BlockSpec(memory_space=pl.ANY),),
        out_specs=(_HBM, _HBM), input_output_aliases={0: 0, 1: 1},
        compiler_params=pltpu.CompilerParams(has_side_effects=_EFFECT),
    )(slabs_thru, land_thru, *sems, after)


def _pair_add(g, recv, core, name):
    _, rows, C = recv.shape
    tc = 256

    def body(core_ref, g_ref, r_ref, o_ref):
        o_ref[...] = _bf(g_ref[...] + r_ref[...])

    spec = pl.BlockSpec((1, rows, tc), lambda j, i, core: (j, 0, i))
    return pl.pallas_call(
        body, name=name,
        grid_spec=pltpu.PrefetchScalarGridSpec(
            num_scalar_prefetch=1, grid=(N_CHIPS, C // tc),
            in_specs=[pl.BlockSpec((1, rows, tc), lambda j, i, core: (j, core[0], i)), spec], out_specs=spec),
        out_shape=jax.ShapeDtypeStruct((N_CHIPS, rows, C), BF16),
        compiler_params=_cparams(("parallel", "parallel")),
    )(core, g, recv)


def _chip_add(own, parts, chip, name):
    _, rows, C = parts.shape
    tc = 256

    def body(chip_ref, own_ref, r0, r1, r2, r3, o_ref):
        acc = None
        for j, r in enumerate((r0, r1, r2, r3)):
            term = jnp.where(chip_ref[0] == j, own_ref[0], r[0]).astype(F32)
            acc = term if acc is None else acc + term
        o_ref[...] = acc

    def slab(j):
        return pl.BlockSpec((1, rows, tc), lambda i, chip: (jnp.where(chip[0] == j, (j + 1) % N_CHIPS, j), 0, i))

    return pl.pallas_call(
        body, name=name,
        grid_spec=pltpu.PrefetchScalarGridSpec(
            num_scalar_prefetch=1, grid=(C // tc,),
            in_specs=[pl.BlockSpec((1, rows, tc), lambda i, chip: (chip[0], 0, i))] + [slab(j) for j in range(N_CHIPS)],
            out_specs=pl.BlockSpec((rows, tc), lambda i, chip: (0, i))),
        out_shape=jax.ShapeDtypeStruct((rows, C), F32),
        compiler_params=_cparams(("parallel",)),
    )(chip, own, parts, parts, parts, parts)


def _adamw_math(w, g, m, v):
    m = ADAM_B1 * m + (1.0 - ADAM_B1) * g
    v = ADAM_B2 * v + (1.0 - ADAM_B2) * (g * g)
    m_hat = m / (1.0 - ADAM_B1 ** ADAM_STEP)
    v_hat = v / (1.0 - ADAM_B2 ** ADAM_STEP)
    delta = -ADAM_LR * (m_hat / (jnp.sqrt(v_hat) + ADAM_EPS) + ADAM_WD * w)
    return delta, m, v


def _adamw_rows(w, g_own, g_sib, m, v, core, name):
    R, C = w.shape[0], w.shape[-1]
    rows = g_own.shape[0]
    step = 256
    chunks = [(r, min(step, R - r)) for r in range(0, R, step)]
    sub = 64

    def body(core_ref, w_hbm, go_hbm, gs_hbm, m_hbm, v_hbm, d_hbm, nm_hbm, nv_hbm, g_hbm,
             wbuf, mbuf, vbuf, gbuf, dbuf, nmbuf, nvbuf, in_sems, g_sems, out_sems):
        c = core_ref[0]
        flat = lambda ref: ref.at[:, 0, :]
        g_in = [pltpu.make_async_copy(go_hbm, gbuf.at[pl.ds(pl.multiple_of(c * rows, 8), rows), :], g_sems.at[0]),
                pltpu.make_async_copy(gs_hbm, gbuf.at[pl.ds(pl.multiple_of((1 - c) * rows, 8), rows), :], g_sems.at[1])]
        for cp in g_in:
            cp.start()
        loads = []
        for k, (r0, n) in enumerate(chunks):
            cps = [pltpu.make_async_copy(flat(src).at[pl.ds(r0, n), :], dst.at[pl.ds(r0, n), :], in_sems.at[a, k])
                   for a, (src, dst) in enumerate(((w_hbm, wbuf), (m_hbm, mbuf), (v_hbm, vbuf)))]
            for cp in cps:
                cp.start()
            loads.append(cps)
        for cp in g_in:
            cp.wait()
        stores = []
        for k, (r0, n) in enumerate(chunks):
            for cp in loads[k]:
                cp.wait()

            def update(rs):
                g = gbuf[rs, :]
                dl, nm, nv = _adamw_math(wbuf[rs, :], g, mbuf[rs, :], vbuf[rs, :])
                dbuf[rs, :] = dl
                nmbuf[rs, :] = nm
                nvbuf[rs, :] = nv

            if n % sub == 0:
                def block(i, carry, r0=r0):
                    update(pl.ds(pl.multiple_of(r0 + i * sub, 8), sub))
                    return carry
                lax.fori_loop(0, n // sub, block, 0)
            else:
                update(pl.ds(r0, n))
            cps = [pltpu.make_async_copy(src.at[pl.ds(r0, n), :], flat(dst).at[pl.ds(r0, n), :], out_sems.at[a, k])
                   for a, (src, dst) in enumerate(((dbuf, d_hbm), (nmbuf, nm_hbm), (nvbuf, nv_hbm), (gbuf, g_hbm)))]
            for cp in cps:
                cp.start()
            stores += cps
        for cp in stores:
            cp.wait()

    any_spec = pl.BlockSpec(memory_space=pl.ANY)
    dense = pltpu.VMEM((R, C), F32)
    return pl.pallas_call(
        body, name=name,
        grid_spec=pltpu.PrefetchScalarGridSpec(
            num_scalar_prefetch=1, grid=(1,),
            in_specs=[any_spec] * 5, out_specs=[any_spec] * 4,
            scratch_shapes=[dense, dense, dense, pltpu.VMEM((2 * rows, C), F32), dense, dense, dense,
                            pltpu.SemaphoreType.DMA((3, len(chunks))), pltpu.SemaphoreType.DMA((2,)),
                            pltpu.SemaphoreType.DMA((4, len(chunks)))]),
        out_shape=[jax.ShapeDtypeStruct(w.shape, F32)] * 4,
        compiler_params=_cparams(),
    )(core, w, g_own, g_sib, m, v)


def _adamw_sum8(w, slabs, land, m, v, ids, name):
    R, C = w.shape
    tc = 128

    def body(ids_ref, w_ref, own_ref, *refs):
        lrefs, (m_ref, v_ref, d_ref, nm_ref, nv_ref, g_ref) = refs[:8], refs[8:]
        g = None
        for d, l_ref in enumerate(lrefs):
            term = jnp.where(ids_ref[0] == d, own_ref[0], l_ref[0]).astype(F32)
            g = term if g is None else g + term
        dl, nm, nv = _adamw_math(w_ref[...], g, m_ref[...], v_ref[...])
        d_ref[...] = dl
        nm_ref[...] = nm
        nv_ref[...] = nv
        g_ref[...] = g

    def slot(d):
        return pl.BlockSpec((1, R, tc), lambda i, ids: (jnp.where(ids[0] == d, (d + 1) % 8, d), 0, i))

    spec = pl.BlockSpec((R, tc), lambda i, ids: (0, i))
    return pl.pallas_call(
        body, name=name,
        grid_spec=pltpu.PrefetchScalarGridSpec(
            num_scalar_prefetch=1, grid=(C // tc,),
            in_specs=[spec, pl.BlockSpec((1, R, tc), lambda i, ids: (ids[1], 0, i))] + [slot(d) for d in range(8)]
            + [spec, spec],
            out_specs=[spec] * 4),
        out_shape=[jax.ShapeDtypeStruct((R, C), F32)] * 4,
        compiler_params=_cparams(("parallel",)),
    )(ids, w, slabs, *([land] * 8), m, v)


SMALL_NAMES = ("conv_b", "ssd_norm_w", "ln_g", "ln_b", "dt_bias", "a_log", "d_skip", "attn_sinks")
SMALL_FIELDS = ((4, 0, D_XBC), (5, 0, D_SSD), (6, 0, D_MODEL), (7, 0, D_MODEL), (5, 1024, SSD_HEADS), (5, 1152, SSD_HEADS),
                (5, 1280, SSD_HEADS), (5, 1408, ATT_QH))
LOSS_FIELD = (6, 1024, 128)
K_SMALL = D_XBC


def _pack_small(g_conv_w, vecs, loss):
    def body(cw_ref, *refs):
        o_ref = refs[-1]
        o_ref[...] = jnp.zeros_like(o_ref)
        o_ref[0:CONV_K, 0:D_XBC] = cw_ref[...]
        for v_ref, (row, off, n) in zip(refs[:-2], SMALL_FIELDS):
            o_ref[row:row + 1, off:off + n] = v_ref[...]
        o_ref[LOSS_FIELD[0]:LOSS_FIELD[0] + 1, LOSS_FIELD[1]:LOSS_FIELD[1] + LOSS_FIELD[2]] = refs[-2][...]

    return pl.pallas_call(
        body, name="pack_small", out_shape=jax.ShapeDtypeStruct((8, K_SMALL), F32), compiler_params=_cparams(),
    )(g_conv_w, *vecs, loss)


def _adamw_small(slots, chip, conv_w, m_conv_w, v_conv_w, params, moms, vars_):
    n_vec = len(SMALL_NAMES)

    def body(chip_ref, s_ref, *refs):
        ins = refs[:3 * (n_vec + 1)]
        outs = refs[3 * (n_vec + 1):-1]
        tot_ref = refs[-1]
        tot = s_ref[0]
        for d in range(1, 8):
            tot = tot + s_ref[d]
        outs[0][...] = tot[LOSS_FIELD[0]:LOSS_FIELD[0] + 1, LOSS_FIELD[1]:LOSS_FIELD[1] + 1]
        off = pl.multiple_of(chip_ref[0] * CONV_COLS, 128)
        tot_ref[...] = tot
        grads = [tot_ref[0:CONV_K, pl.ds(off, CONV_COLS)]]
        grads += [tot[row:row + 1, o:o + n] for row, o, n in SMALL_FIELDS]
        for k, g in enumerate(grads):
            w_ref, m_ref, v_ref = ins[3 * k:3 * k + 3]
            full = (0,) if k == 0 else (Ellipsis,)
            d, nm, nv = _adamw_math(w_ref[full], g, m_ref[full], v_ref[full])
            for o_ref, val in zip(outs[1 + 4 * k:5 + 4 * k], (g, d, nm, nv)):
                o_ref[full] = val

    args = [conv_w, m_conv_w, v_conv_w]
    for w, m, v in zip(params, moms, vars_):
        args += [w, m, v]
    shapes = [jax.ShapeDtypeStruct((1, 1), F32)] + [jax.ShapeDtypeStruct(conv_w.shape, F32)] * 4
    for w in params:
        shapes += [jax.ShapeDtypeStruct(w.shape, F32)] * 4
    vmem = pl.BlockSpec(memory_space=pltpu.VMEM)
    return pl.pallas_call(
        body, name="adamw_small",
        grid_spec=pltpu.PrefetchScalarGridSpec(
            num_scalar_prefetch=1, grid=(1,),
            in_specs=[pl.BlockSpec(slots.shape, lambda i, chip: (0, 0, 0))] + [vmem] * len(args),
            out_specs=[vmem] * len(shapes), scratch_shapes=[pltpu.VMEM((8, K_SMALL), F32)]),
        out_shape=shapes, compiler_params=_cparams(),
    )(chip, slots, *args)


def kernel(x, positions, w_in, conv_w, conv_b, dt_bias, a_log, d_skip, ssd_norm_w, attn_sinks, w_out, ln_g, ln_b, loss_target, m_w_in, m_conv_w, m_conv_b, m_dt_bias, m_a_log, m_d_skip, m_ssd_norm_w, m_attn_sinks, m_w_out, m_ln_g, m_ln_b, v_w_in, v_conv_w, v_conv_b, v_dt_bias, v_a_log, v_d_skip, v_ssd_norm_w, v_attn_sinks, v_w_out, v_ln_g, v_ln_b):
    mx, my, mc = _mesh_pos()
    chip = 2 * mx + my
    L = x.shape[1]

    conv_w_s8 = jnp.pad(conv_w[0], ((0, 8 - CONV_K), (0, 0)))
    pad_rows = ((0, SLAB_ROWS - W_IN_COLS), (0, 0))
    w_in_t = w_in[0].T
    w_in_b, w_out_b = jnp.pad(_bf(w_in_t), pad_rows), _bf(w_out[0])
    ag_in, ag_cw = _gather_weights(w_in_b, conv_w_s8)
    started = _gather_w_out_start(w_out_b, ag_cw)
    own = (jnp.arange(N_CHIPS) == chip)[:, None, None]

    def get_w_out(after):
        landed = _gather_w_out_wait(started[0:6], started[6], started[7], after)
        return jnp.where(own, w_out_b[None], landed).reshape(D_MIX, D_MODEL)

    w_full = jnp.concatenate([ag_in[j, 0:W_IN_COLS] for j in range(N_CHIPS)], axis=0)
    w = jnp.concatenate([
        w_full[O_Z:O_Z + D_SSD], w_full[O_G:O_G + D_ATT], w_full[O_Q:O_Q + D_ATT],
        w_full[O_XBC:O_XBC + D_XBC], w_full[O_K:O_K + 2 * D_KV], w_full[O_DT:O_DT + SSD_HEADS],
        jnp.zeros((DT_PAD - SSD_HEADS, D_MODEL), BF16)], axis=0)
    conv_w_full = jnp.concatenate([ag_cw[j, 0:CONV_K] for j in range(N_CHIPS)], axis=1)

    loss_part, gx_args, gw_in, w_out_red, small = _local_step(
        x[0], positions[0].reshape(L, 1), loss_target[0], w, get_w_out, started[8][0:1, :], conv_w_full,
        conv_b, dt_bias, a_log, d_skip, ssd_norm_w, attn_sinks, ln_g, ln_b)

    packed = _pack_small(small["conv_w"], [small[n] for n in SMALL_NAMES], loss_part)
    core_id = mc.reshape(1).astype(jnp.int32)
    chip_id = chip.reshape(1).astype(jnp.int32)
    ids = jnp.stack([4 * mx + 2 * my + mc, chip]).astype(jnp.int32)
    slabs = jnp.stack([jnp.pad(gw_in[W_IN_COLS * j:W_IN_COLS * (j + 1)], pad_rows) for j in range(N_CHIPS)])
    w_in_red = _pair_start(slabs, packed)
    grad_x = _grad_x(*gx_args, w_in_red[10], 0)
    gw_in_slabs, recv_in = _pair_wait(w_in_red[0:8], w_in_red[8], w_in_red[9], grad_x[0:8, 0:128])
    s_in = _pair_add(gw_in_slabs, recv_in, core_id, "pair_add_in")
    chip_red = _chip_start(s_in, packed)
    grad_x = _grad_x(*gx_args, chip_red[8], 1, grad_x)
    own_slabs, landed = _reduce_w_out_wait(w_out_red[0:14], w_out_red[14], w_out_red[15], grad_x)
    out_t = _adamw_sum8(w_out[0], own_slabs, landed, m_w_out[0], v_w_out[0], ids, "adamw_w_out")
    d_w_out, nm_w_out, nv_w_out, g_w_out = [a[None] for a in out_t]
    s_in, r_in = _chip_wait(chip_red[0:6], chip_red[6], chip_red[7], out_t[0])
    h_in = _chip_add(s_in, r_in, chip_id, "chip_add_in")
    sib_in, slots = _pair_share(h_in, packed)

    to_rows = lambda a: jnp.transpose(a, (2, 0, 1))
    in_t = _adamw_rows(to_rows(w_in), h_in, sib_in, to_rows(m_w_in), to_rows(v_w_in), core_id, "adamw_w_in")
    d_w_in, nm_w_in, nv_w_in, g_w_in = [jnp.transpose(a, (1, 2, 0)) for a in in_t]

    params = dict(conv_b=conv_b, ssd_norm_w=ssd_norm_w, ln_g=ln_g, ln_b=ln_b, dt_bias=dt_bias, a_log=a_log,
                  d_skip=d_skip, attn_sinks=attn_sinks)
    moms = dict(conv_b=m_conv_b, ssd_norm_w=m_ssd_norm_w, ln_g=m_ln_g, ln_b=m_ln_b, dt_bias=m_dt_bias, a_log=m_a_log,
                d_skip=m_d_skip, attn_sinks=m_attn_sinks)
    vars_ = dict(conv_b=v_conv_b, ssd_norm_w=v_ssd_norm_w, ln_g=v_ln_g, ln_b=v_ln_b, dt_bias=v_dt_bias, a_log=v_a_log,
                 d_skip=v_d_skip, attn_sinks=v_attn_sinks)
    res = _adamw_small(slots, chip_id, conv_w, m_conv_w, v_conv_w, [params[n] for n in SMALL_NAMES],
                       [moms[n] for n in SMALL_NAMES], [vars_[n] for n in SMALL_NAMES])
    loss = res[0][0, 0]
    grads, delta, new_m, new_v = {}, {}, {}, {}
    for k, n in enumerate(("conv_w",) + SMALL_NAMES):
        grads[n], delta[n], new_m[n], new_v[n] = res[1 + 4 * k:5 + 4 * k]
    for dd, a_in, a_out in ((grads, g_w_in, g_w_out), (delta, d_w_in, d_w_out), (new_m, nm_w_in, nm_w_out),
                            (new_v, nv_w_in, nv_w_out)):
        dd["w_in"] = a_in
        dd["w_out"] = a_out
    order = ("w_in", "conv_w", "conv_b", "dt_bias", "a_log", "d_skip", "ssd_norm_w", "attn_sinks", "w_out", "ln_g", "ln_b")
    return (loss, grad_x[None], *[grads[n] for n in order], *[delta[n] for n in order], *[new_m[n] for n in order],
            *[new_v[n] for n in order])
```

```python
import numpy as np
import jax
import jax.numpy as jnp
from jax import lax
from jax.experimental import pallas as pl
from jax.experimental.pallas import tpu as pltpu

F32 = jnp.float32
BF16 = jnp.bfloat16
MESH = pl.DeviceIdType.MESH

D_MODEL = 1024
D_SSD = 1024
D_ATT = 1024
D_MIX = 2048
SSD_HEADS = 16
SSD_P = 64
SSD_GROUPS = 2
SSD_R = 8
SSD_N = 128
D_BC = 256
D_XBC = 1536
CONV_K = 4
CHUNK = 128
ATT_HD = 64
ATT_QH = 16
ATT_KVH = 4
ATT_R = 4
D_KV = 256
WINDOW = 128
ROPE_THETA = 500000.0
ROPE_DIM = 16
ALPHA = 2.0 ** 0.25
LN_EPS = 1e-5
RMS_EPS = 1e-5
D_IN_PROJ = 5136
O_Z, O_XBC, O_DT, O_Q, O_K, O_V, O_G = 0, 1024, 2560, 2576, 3600, 3856, 4112
P_Z, P_G, P_Q, P_XBC, P_KV, P_DT, P_END = 0, 1024, 2048, 3072, 4608, 5120, 5248
DT_PAD = 128
N_CHIPS = 4
W_IN_COLS = D_IN_PROJ // N_CHIPS
SLAB_ROWS = 1312
W_OUT_ROWS = D_MIX // N_CHIPS
CONV_COLS = D_XBC // N_CHIPS

ADAM_LR = 0.001
ADAM_B1 = 0.9
ADAM_B2 = 0.999
ADAM_EPS = 1e-08
ADAM_WD = 0.01
ADAM_STEP = 10

VMEM_LIMIT = 56 * 1024 * 1024
ROW_TILE = 512
NEG_BIG = -1e30


def _cparams(sem=None, **kw):
    if sem is not None:
        kw["dimension_semantics"] = sem
    return pltpu.CompilerParams(vmem_limit_bytes=VMEM_LIMIT, **kw)


def _dot(a, b):
    return jnp.dot(a, b, preferred_element_type=F32)


def _dot_nt(a, b):
    return lax.dot_general(a, b, (((1,), (1,)), ((), ())), preferred_element_type=F32)


def _dot_tn(a, b):
    return lax.dot_general(a, b, (((0,), (0,)), ((), ())), preferred_element_type=F32)


def _bf(a):
    return a.astype(BF16)


def _iota2(shape, dim):
    return lax.broadcasted_iota(jnp.int32, shape, dim)


def _three_terms(x):
    hi = _bf(x)
    r = x - hi.astype(F32)
    mid = _bf(r)
    return hi, mid, _bf(r - mid.astype(F32))


def _dot01(m, a):
    return sum(_dot(m, t) for t in _three_terms(a))


def _to_rows(col):
    k = col.shape[1]
    eye = (_iota2((k, k), 0) == _iota2((k, k), 1)).astype(BF16)
    return sum(_dot_nt(eye, t) for t in _three_terms(col))


def _to_cols(row):
    n = row.shape[1]
    eye = (_iota2((n, n), 0) == _iota2((n, n), 1)).astype(BF16)
    return sum(_dot_nt(eye, t) for t in _three_terms(row))


def _sigmoid(x):
    return jax.nn.sigmoid(x)


def _in_proj(x, w, pos, inv):
    L = x.shape[0]
    tm = ROW_TILE
    widths = (D_SSD, D_ATT, D_ATT, D_XBC, 2 * D_KV, DT_PAD)

    def body(x_ref, w_ref, pos_ref, inv_ref, z_ref, g_ref, q_ref, xbc_ref, kv_ref, dt_ref, xb_ref):
        xb = _bf(x_ref[...])
        xb_ref[...] = xb
        tabs = _rope_tables(pos_ref, inv_ref)
        q_ref[...] = _bf(_rope(_dot_nt(xb, w_ref[P_Q:P_Q + D_ATT, :]), tabs))
        kv_ref[:, 0:D_KV] = _bf(_rope(_dot_nt(xb, w_ref[P_KV:P_KV + D_KV, :]), tabs))
        kv_ref[:, D_KV:2 * D_KV] = _bf(_dot_nt(xb, w_ref[P_KV + D_KV:P_KV + 2 * D_KV, :]))
        for o_ref, off, wd in zip((z_ref, g_ref, xbc_ref, dt_ref), (P_Z, P_G, P_XBC, P_DT), (D_SSD, D_ATT, D_XBC, DT_PAD)):
            o_ref[...] = _dot_nt(xb, w_ref[off:off + wd, :])

    row = lambda wd: pl.BlockSpec((tm, wd), lambda i: (i, 0))
    return pl.pallas_call(
        body, name="in_proj", grid=(L // tm,),
        in_specs=[row(D_MODEL), pl.BlockSpec((P_END, D_MODEL), lambda i: (0, 0), pipeline_mode=pl.Buffered(1)), row(1),
                  pl.BlockSpec((1, 2 * ATT_HD), lambda i: (0, 0))],
        out_specs=[row(wd) for wd in widths] + [row(D_MODEL)],
        out_shape=[jax.ShapeDtypeStruct((L, wd), dt) for wd, dt in zip(widths, (F32, F32, BF16, F32, BF16, F32))]
        + [jax.ShapeDtypeStruct((L, D_MODEL), BF16)],
        compiler_params=_cparams(("parallel",)),
    )(x, w, pos, inv)


def _matmuls_tn(a_list, b, name):
    K, N = b.shape
    tk = min(K, 1024)
    n = len(a_list)

    def body(*refs):
        b_ref = refs[n]

        @pl.when(pl.program_id(0) == 0)
        def _():
            for o_ref in refs[n + 1:]:
                o_ref[...] = jnp.zeros_like(o_ref)

        bb = _bf(b_ref[...])
        for a_ref, o_ref in zip(refs[:n], refs[n + 1:]):
            o_ref[...] += _dot_tn(_bf(a_ref[...]), bb)

    return pl.pallas_call(
        body, name=name, grid=(K // tk,),
        in_specs=[pl.BlockSpec((tk, a.shape[1]), lambda k: (k, 0)) for a in a_list] + [pl.BlockSpec((tk, N), lambda k: (k, 0))],
        out_specs=[pl.BlockSpec((a.shape[1], N), lambda k: (0, 0)) for a in a_list],
        out_shape=[jax.ShapeDtypeStruct((a.shape[1], N), F32) for a in a_list],
        compiler_params=_cparams(("arbitrary",)),
    )(*a_list, b)


def _grad_x(dr, dz, dg, dq, dxbc, dkv, ddt, w, after, part, prev=None):
    L = dr.shape[0]
    tm = min(ROW_TILE, L // 4)
    first = L // (4 * tm)
    n = first if part == 0 else L // tm - first
    widths = (D_SSD, D_ATT, D_ATT, D_XBC, 2 * D_KV, DT_PAD)
    offs = (P_Z, P_G, P_Q, P_XBC, P_KV, P_DT)

    def body(dr_ref, dz_ref, dg_ref, dq_ref, dxbc_ref, dkv_ref, ddt_ref, w_ref, after_ref, *rest):
        o_ref = rest[-1]
        acc = ALPHA * dr_ref[...]
        for p_ref, off, wd in zip((dz_ref, dg_ref, dq_ref, dxbc_ref, dkv_ref, ddt_ref), offs, widths):
            acc = acc + _dot(_bf(p_ref[...]), w_ref[off:off + wd, :])
        o_ref[...] = acc

    row = lambda wd: pl.BlockSpec((tm, wd), lambda i: (i + part * first, 0))
    ins = [dr, dz, dg, dq, dxbc, dkv, ddt, w, after]
    specs = ([row(D_MODEL)] + [row(wd) for wd in widths]
             + [pl.BlockSpec((P_END, D_MODEL), lambda i: (0, 0), pipeline_mode=pl.Buffered(1)),
                pl.BlockSpec((8, 128), lambda i: (0, 0))])
    if prev is not None:
        ins.append(prev)
        specs.append(pl.BlockSpec(memory_space=pl.ANY))
    return pl.pallas_call(
        body, name="grad_x_%d" % part, grid=(n,),
        in_specs=specs, out_specs=row(D_MODEL),
        out_shape=jax.ShapeDtypeStruct((L, D_MODEL), F32),
        input_output_aliases={} if prev is None else {len(ins) - 1: 0},
        compiler_params=_cparams(("parallel",)),
    )(*ins)


HALO = 16


def _shift_matrix(offsets):
    n = CHUNK + HALO
    m = np.zeros((len(offsets) * CHUNK, 2 * n), np.float32)
    for k, off in enumerate(offsets):
        t = np.arange(CHUNK)
        m[k * CHUNK + t, t + off] = 1.0
        m[k * CHUNK + t, n + t + off] = 1.0
    return jnp.asarray(m, BF16)


def _shifted_rows(first_part, second_part, smat_ref):
    h1, l1 = _hi_lo(first_part)
    h2, l2 = _hi_lo(second_part)
    sh = _dot(smat_ref[...], jnp.concatenate([h1, h2, l1, l2], axis=0))
    return sh[0:CHUNK], sh[CHUNK:2 * CHUNK], sh[2 * CHUNK:3 * CHUNK]


def _ssd_chunk_pre(first, xbc_ref, tail_ref, dt_ref, cw_ref, cb_ref, dtb_ref, alog_ref, smat_ref=None, ext=None):
    tail = jnp.where(first, 0.0, tail_ref[...])
    x = xbc_ref[...]
    if ext is None:
        taps = _shifted_rows(tail, x, smat_ref) + (x,)
    else:
        ext[0:HALO, :] = tail
        ext[HALO:HALO + CHUNK, :] = x
        taps = tuple(ext[pl.ds(HALO - (CONV_K - 1) + k, CHUNK), :] for k in range(CONV_K - 1)) + (x,)
    u = cb_ref[...] + cw_ref[0:1, :] * taps[0]
    for k in range(1, CONV_K):
        u = u + cw_ref[k:k + 1, :] * taps[k]
    sig = _sigmoid(u)
    xbc = u * sig
    dtraw = dt_ref[:, 0:SSD_HEADS] + dtb_ref[...]
    dt = jax.nn.softplus(dtraw)
    A = -jnp.exp(alog_ref[...])
    a = dt * A
    tril = (_iota2((CHUNK, CHUNK), 0) >= _iota2((CHUNK, CHUNK), 1)).astype(BF16)
    acs = _dot01(tril, a)
    acs_row = _to_rows(acs)
    return u, sig, xbc, dtraw, dt, A, acs, acs_row, taps


def _head_expander():
    return (_iota2((SSD_HEADS, D_SSD), 1) // SSD_P == _iota2((SSD_HEADS, D_SSD), 0)).astype(BF16)


def _hi_lo(x):
    hi = _bf(x)
    return hi, _bf(x - hi.astype(F32))


def _expand(v, e):
    hi, lo = _hi_lo(v)
    return _dot(hi, e) + _dot(lo, e)


def _headsum(t, e):
    m = t.shape[0]
    if m < 8:
        t = jnp.broadcast_to(t[0:1], (8, t.shape[1]))
    hi, lo = _hi_lo(t)
    return (_dot_nt(hi, e) + _dot_nt(lo, e))[0:m]


def _ssd_decays(dt, acs, dsk_ref, e):
    alast = acs[CHUNK - 1:CHUNK, :]
    stk = jnp.concatenate([dt, jnp.exp(acs), jnp.exp(alast - acs),
                           jnp.broadcast_to(jnp.exp(alast), (8, SSD_HEADS)),
                           jnp.broadcast_to(dsk_ref[...], (8, SSD_HEADS))], axis=0)
    ex = _expand(stk, e)
    return (ex[0:CHUNK], ex[CHUNK:2 * CHUNK], ex[2 * CHUNK:3 * CHUNK], ex[3 * CHUNK:3 * CHUNK + 1],
            ex[3 * CHUNK + 8:3 * CHUNK + 9])


def _ssd_fwd(z, xbc, dtp, conv_w, conv_b, dt_bias, a_log, d_skip, norm_w):
    L = z.shape[0]
    nc = L // CHUNK
    half = D_SSD // SSD_GROUPS

    def body(z_ref, xbc_ref, tail_ref, dt_ref, cw_ref, cb_ref, dtb_ref, alog_ref, dsk_ref, nw_ref,
             y_ref, ypre_ref, prev_ref, state, ybuf, mbuf, ext):
        c = pl.program_id(0)

        @pl.when(c == 0)
        def _():
            state[...] = jnp.zeros_like(state)

        u, sig, xbcv, dtraw, dt, A, acs, acs_row, _ = _ssd_chunk_pre(
            c == 0, xbc_ref, tail_ref, dt_ref, cw_ref, cb_ref, dtb_ref, alog_ref, ext=ext)
        e = _head_expander()
        dtE, eacsE, dsdE, ealE, dskE = _ssd_decays(dt, acs, dsk_ref, e)
        xs = xbcv[:, 0:D_SSD]
        X = xs * dtE
        prev_ref[0] = state[...]
        causal = _iota2((CHUNK, CHUNK), 0) >= _iota2((CHUNK, CHUNK), 1)
        for g in range(SSD_GROUPS):
            gs = slice(half * g, half * (g + 1))
            Bg = _bf(xbcv[:, D_SSD + SSD_N * g:D_SSD + SSD_N * (g + 1)])
            Cg = _bf(xbcv[:, D_SSD + D_BC + SSD_N * g:D_SSD + D_BC + SSD_N * (g + 1)])
            cb = _dot_nt(Cg, Bg)
            for r in range(SSD_R):
                h = g * SSD_R + r
                seg = acs[:, h:h + 1] - acs_row[h:h + 1, :]
                mbuf[h] = _bf(cb * jnp.where(causal, jnp.exp(jnp.where(causal, seg, 0.0)), 0.0))
            st = state[:, gs]
            ybuf[:, gs] = _dot(Cg, _bf(st)) * eacsE[:, gs] + dskE[:, gs] * xs[:, gs]
            state[:, gs] = st * ealE[:, gs] + _dot_tn(Bg, _bf(X[:, gs] * dsdE[:, gs]))
        Xb = _bf(X)
        for h in range(SSD_HEADS):
            hs = slice(SSD_P * h, SSD_P * (h + 1))
            ybuf[:, hs] += _dot(mbuf[h], Xb[:, hs])
        y = ybuf[...]
        ypre_ref[...] = y
        zv = z_ref[...]
        yf = y * (zv * _sigmoid(zv))
        for g in range(SSD_GROUPS):
            gs = slice(half * g, half * (g + 1))
            yg = yf[:, gs]
            ms = jnp.mean(yg * yg, axis=-1, keepdims=True)
            y_ref[:, gs] = _bf(yg * lax.rsqrt(ms + RMS_EPS) * nw_ref[:, gs])

    full = lambda shape: pl.BlockSpec(shape, lambda c: (0, 0))
    return pl.pallas_call(
        body, name="ssd_fwd", grid=(nc,),
        in_specs=[
            pl.BlockSpec((CHUNK, D_SSD), lambda c: (c, 0)),
            pl.BlockSpec((CHUNK, D_XBC), lambda c: (c, 0)),
            pl.BlockSpec((HALO, D_XBC), lambda c: (jnp.maximum(c * (CHUNK // HALO) - 1, 0), 0)),
            pl.BlockSpec((CHUNK, DT_PAD), lambda c: (c, 0)),
            full((CONV_K, D_XBC)), full((1, D_XBC)), full((1, SSD_HEADS)), full((1, SSD_HEADS)), full((1, SSD_HEADS)),
            full((1, D_SSD)),
        ],
        out_specs=[
            pl.BlockSpec((CHUNK, D_SSD), lambda c: (c, 0)),
            pl.BlockSpec((CHUNK, D_SSD), lambda c: (c, 0)),
            pl.BlockSpec((1, SSD_N, D_SSD), lambda c: (c, 0, 0)),
        ],
        out_shape=[
            jax.ShapeDtypeStruct((L, D_SSD), BF16),
            jax.ShapeDtypeStruct((L, D_SSD), F32),
            jax.ShapeDtypeStruct((nc, SSD_N, D_SSD), F32),
        ],
        scratch_shapes=[
            pltpu.VMEM((SSD_N, D_SSD), F32),
            pltpu.VMEM((CHUNK, D_SSD), F32),
            pltpu.VMEM((SSD_HEADS, CHUNK, CHUNK), BF16),
            pltpu.VMEM((CHUNK + HALO, D_XBC), F32),
        ],
        compiler_params=_cparams(("arbitrary",)),
    )(z, xbc, xbc, dtp, conv_w, conv_b, dt_bias, a_log, d_skip, norm_w)


def _ssd_bwd(dy, z, ypre, xbc, dtp, prev, conv_w, conv_b, dt_bias, a_log, d_skip, norm_w):
    L = z.shape[0]
    nc = L // CHUNK
    half = D_SSD // SSD_GROUPS

    def body(dy_ref, z_ref, ypre_ref, xbc_ref, tail_ref, dt_ref, prev_ref, cw_ref, cb_ref, dtb_ref, alog_ref, dsk_ref,
             nw_ref, smat_ref, smat2_ref, dz_ref, dxbc_ref, ddt_ref, gcw_ref, gcb_ref, gdtb_ref, galog_ref, gdsk_ref,
             gnw_ref, dstate, dhead, dpost, yobuf, bdbuf, lmbuf, dmbuf, cbbuf):
        i = pl.program_id(0)
        c = nc - 1 - i

        @pl.when(i == 0)
        def _():
            dstate[...] = jnp.zeros_like(dstate)
            dhead[...] = jnp.zeros_like(dhead)
            gcw_ref[...] = jnp.zeros_like(gcw_ref)
            gcb_ref[...] = jnp.zeros_like(gcb_ref)
            gdtb_ref[...] = jnp.zeros_like(gdtb_ref)
            galog_ref[...] = jnp.zeros_like(galog_ref)
            gdsk_ref[...] = jnp.zeros_like(gdsk_ref)
            gnw_ref[...] = jnp.zeros_like(gnw_ref)

        u, sig, xbcv, dtraw, dt, A, acs, acs_row, taps = _ssd_chunk_pre(
            c == 0, xbc_ref, tail_ref, dt_ref, cw_ref, cb_ref, dtb_ref, alog_ref, smat_ref)
        e = _head_expander()
        dtE, eacsE, dsdE, ealE, dskE = _ssd_decays(dt, acs, dsk_ref, e)
        alast = acs[CHUNK - 1:CHUNK, :]
        xs = xbcv[:, 0:D_SSD]
        X = xs * dtE
        Xb = _bf(X)

        zv = z_ref[...]
        ypre = ypre_ref[...]
        dyn = dy_ref[...]
        sz = _sigmoid(zv)
        silu_z = zv * sz
        yf = ypre * silu_z
        dyf_parts = []
        for g in range(SSD_GROUPS):
            gs = slice(half * g, half * (g + 1))
            yg = yf[:, gs]
            rstd = lax.rsqrt(jnp.mean(yg * yg, axis=-1, keepdims=True) + RMS_EPS)
            dout = dyn[:, gs]
            gnw_ref[:, gs] += jnp.sum(dout * yg * rstd, axis=0, keepdims=True)
            dyhat = dout * nw_ref[:, gs]
            dyf_parts.append(rstd * (dyhat - yg * (rstd * rstd) * jnp.mean(dyhat * yg, axis=-1, keepdims=True)))
        dyf = jnp.concatenate(dyf_parts, axis=1)
        dz_ref[...] = _bf(dyf * ypre * (sz * (1.0 + zv * (1.0 - sz))))
        dyp = dyf * silu_z
        dyb = _bf(dyp)
        G = dyp * eacsE

        causal = _iota2((CHUNK, CHUNK), 0) >= _iota2((CHUNK, CHUNK), 1)
        ST = prev_ref[0]
        dST = dstate[...]
        for g in range(SSD_GROUPS):
            gs = slice(half * g, half * (g + 1))
            bs = slice(D_SSD + SSD_N * g, D_SSD + SSD_N * (g + 1))
            cs = slice(D_SSD + D_BC + SSD_N * g, D_SSD + D_BC + SSD_N * (g + 1))
            Bg = _bf(xbcv[:, bs])
            Cg = _bf(xbcv[:, cs])
            Gb = _bf(G[:, gs])
            STb = _bf(ST[:, gs])
            dSTb = _bf(dST[:, gs])
            dstate[:, gs] = dST[:, gs] * ealE[:, gs] + _dot_tn(Cg, Gb)
            yobuf[:, gs] = _dot(Cg, STb) * eacsE[:, gs]
            bdbuf[:, gs] = _dot(Bg, dSTb)
            dpost[:, cs] = _dot_nt(Gb, STb)
            dpost[:, bs] = _dot_nt(_bf(X[:, gs] * dsdE[:, gs]), dSTb)
            cbbuf[g] = _dot_nt(Cg, Bg)
            for r in range(SSD_R):
                h = g * SSD_R + r
                seg = acs[:, h:h + 1] - acs_row[h:h + 1, :]
                lmbuf[h] = jnp.where(causal, jnp.exp(jnp.where(causal, seg, 0.0)), 0.0)
        for h in range(SSD_HEADS):
            hs = slice(SSD_P * h, SSD_P * (h + 1))
            Mb = _bf(cbbuf[h // SSD_R] * lmbuf[h])
            dmbuf[h] = _dot_nt(dyb[:, hs], Xb[:, hs])
            dpost[:, hs] = _dot_tn(Mb, dyb[:, hs])
        lane16 = _iota2((1, SSD_HEADS), 1)
        sub16 = _iota2((SSD_HEADS, 1), 0)
        dacs_col = jnp.zeros((CHUNK, SSD_HEADS), F32)
        dacs_row = jnp.zeros((SSD_HEADS, CHUNK), F32)
        for g in range(SSD_GROUPS):
            bs = slice(D_SSD + SSD_N * g, D_SSD + SSD_N * (g + 1))
            cs = slice(D_SSD + D_BC + SSD_N * g, D_SSD + D_BC + SSD_N * (g + 1))
            cb = cbbuf[g]
            dcb = jnp.zeros((CHUNK, CHUNK), F32)
            for r in range(SSD_R):
                h = g * SSD_R + r
                dM = dmbuf[h]
                Lm = lmbuf[h]
                dcb = dcb + dM * Lm
                dseg = dM * (cb * Lm)
                dacs_col = dacs_col + jnp.sum(dseg, axis=-1, keepdims=True) * (lane16 == h).astype(F32)
                dacs_row = dacs_row - jnp.sum(dseg, axis=0, keepdims=True) * (sub16 == h).astype(F32)
            dcbb = _bf(dcb)
            dpost[:, bs] += _dot_tn(dcbb, _bf(xbcv[:, cs]))
            dpost[:, cs] += _dot(dcbb, _bf(xbcv[:, bs]))

        BD = bdbuf[...]
        dX = dpost[:, 0:D_SSD] + dsdE * BD
        dsd = jnp.exp(alast - acs)
        T = _headsum(X * BD, e) * dsd
        dalast = jnp.sum(T, axis=0, keepdims=True) + _headsum(
            jnp.sum(dST * ST, axis=0, keepdims=True), e) * jnp.exp(alast)
        is_last = (_iota2((CHUNK, 1), 0) == CHUNK - 1).astype(F32)
        dacs = dacs_col + _to_cols(dacs_row) + _headsum(dyp * yobuf[...], e) - T + is_last * dalast
        triu = (_iota2((CHUNK, CHUNK), 0) <= _iota2((CHUNK, CHUNK), 1)).astype(BF16)
        da = _dot01(triu, dacs)
        ddt_tot = _headsum(dX * xs, e) + da * A
        galog_ref[...] += jnp.sum(da * dt, axis=0, keepdims=True) * A
        ddtraw = ddt_tot * _sigmoid(dtraw)
        gdtb_ref[...] += jnp.sum(ddtraw, axis=0, keepdims=True)
        gdsk_ref[...] += _headsum(jnp.sum(dyp * xs, axis=0, keepdims=True), e)
        ddt_ref[...] = jnp.zeros_like(ddt_ref)
        ddt_ref[:, 0:SSD_HEADS] = ddtraw
        dpost[:, 0:D_SSD] = dX * dtE + dskE * dyp

        dconv = dpost[...] * (sig * (1.0 + u * (1.0 - sig)))
        gcb_ref[...] += jnp.sum(dconv, axis=0, keepdims=True)
        for k in range(CONV_K):
            gcw_ref[k:k + 1, :] += jnp.sum(dconv * taps[k], axis=0, keepdims=True)
        later = _shifted_rows(dconv, dhead[...], smat2_ref)
        dx = cw_ref[CONV_K - 1:CONV_K, :] * dconv
        for k in range(CONV_K - 1):
            dx = dx + cw_ref[k:k + 1, :] * later[k]
        dxbc_ref[...] = _bf(dx)
        dhead[...] = dconv[0:HALO, :]

    full = lambda shape: pl.BlockSpec(shape, lambda i: (0, 0))
    rev = lambda wd: pl.BlockSpec((CHUNK, wd), lambda i: (nc - 1 - i, 0))
    return pl.pallas_call(
        body, name="ssd_bwd", grid=(nc,),
        in_specs=[
            rev(D_SSD), rev(D_SSD), rev(D_SSD), rev(D_XBC),
            pl.BlockSpec((HALO, D_XBC), lambda i: (jnp.maximum((nc - 1 - i) * (CHUNK // HALO) - 1, 0), 0)),
            rev(DT_PAD),
            pl.BlockSpec((1, SSD_N, D_SSD), lambda i: (nc - 1 - i, 0, 0)),
            full((CONV_K, D_XBC)), full((1, D_XBC)), full((1, SSD_HEADS)), full((1, SSD_HEADS)), full((1, SSD_HEADS)),
            full((1, D_SSD)), full((3 * CHUNK, 2 * (CHUNK + HALO))), full((3 * CHUNK, 2 * (CHUNK + HALO))),
        ],
        out_specs=[
            rev(D_SSD), rev(D_XBC), rev(DT_PAD),
            full((CONV_K, D_XBC)), full((1, D_XBC)), full((1, SSD_HEADS)), full((1, SSD_HEADS)), full((1, SSD_HEADS)),
            full((1, D_SSD)),
        ],
        out_shape=[
            jax.ShapeDtypeStruct((L, D_SSD), BF16), jax.ShapeDtypeStruct((L, D_XBC), BF16),
            jax.ShapeDtypeStruct((L, DT_PAD), F32),
            jax.ShapeDtypeStruct((CONV_K, D_XBC), F32), jax.ShapeDtypeStruct((1, D_XBC), F32),
            jax.ShapeDtypeStruct((1, SSD_HEADS), F32), jax.ShapeDtypeStruct((1, SSD_HEADS), F32),
            jax.ShapeDtypeStruct((1, SSD_HEADS), F32), jax.ShapeDtypeStruct((1, D_SSD), F32),
        ],
        scratch_shapes=[
            pltpu.VMEM((SSD_N, D_SSD), F32),
            pltpu.VMEM((HALO, D_XBC), F32),
            pltpu.VMEM((CHUNK, D_XBC), F32),
            pltpu.VMEM((CHUNK, D_SSD), F32),
            pltpu.VMEM((CHUNK, D_SSD), F32),
            pltpu.VMEM((SSD_HEADS, CHUNK, CHUNK), F32),
            pltpu.VMEM((SSD_HEADS, CHUNK, CHUNK), F32),
            pltpu.VMEM((SSD_GROUPS, CHUNK, CHUNK), F32),
        ],
        compiler_params=_cparams(("arbitrary",)),
    )(dy, z, ypre, xbc, xbc, dtp, prev, conv_w, conv_b, dt_bias, a_log, d_skip, norm_w, _shift_matrix((13, 14, 15)),
      _shift_matrix((3, 2, 1)))


def _rope_tables(pos_ref, inv_ref):
    ang = pos_ref[...].astype(F32) * inv_ref[...]
    d = _iota2((1, 2 * ATT_HD), 1) % ATT_HD
    s = jnp.sin(ang)
    return jnp.cos(ang), jnp.where(d < ROPE_DIM // 2, -s, 0.0), jnp.where((d >= ROPE_DIM // 2) & (d < ROPE_DIM), s, 0.0)


def _rope(t, tabs):
    c, s1, s2 = tabs
    n = t.shape[1]
    rep = n // c.shape[1]
    return (t * jnp.tile(c, (1, rep)) + pltpu.roll(t, n - ROPE_DIM // 2, 1) * jnp.tile(s1, (1, rep))
            + pltpu.roll(t, ROPE_DIM // 2, 1) * jnp.tile(s2, (1, rep)))


def _rope_t(t, tabs):
    c, s1, s2 = tabs
    n = t.shape[1]
    rep = n // c.shape[1]
    return (t * jnp.tile(c, (1, rep)) + pltpu.roll(t * jnp.tile(s1, (1, rep)), ROPE_DIM // 2, 1)
            + pltpu.roll(t * jnp.tile(s2, (1, rep)), n - ROPE_DIM // 2, 1))


def _stack_heads(t, j):
    return jnp.concatenate([t[:, ATT_HD * (j * ATT_R + r):ATT_HD * (j * ATT_R + r + 1)] for r in range(ATT_R)], axis=0)


def _swa_mask_t(first):
    si = _iota2((2 * WINDOW, ATT_R * WINDOW), 0)
    qi = _iota2((2 * WINDOW, ATT_R * WINDOW), 1) % WINDOW
    band = (si > qi) & (si <= qi + WINDOW)
    return band & (jnp.logical_not(first) | (si >= WINDOW))


def _head_rows(ref, j):
    if ref.shape[0] == 1:
        parts = [jnp.broadcast_to(ref[:, j * ATT_R + r:j * ATT_R + r + 1], (1, WINDOW)) for r in range(ATT_R)]
    else:
        parts = [ref[j * ATT_R + r:j * ATT_R + r + 1, :] for r in range(ATT_R)]
    return jnp.concatenate(parts, axis=1)


def _swa_fwd(q, g, kv, sinks):
    L = q.shape[0]
    nb = L // WINDOW
    scale = ATT_HD ** -0.5

    def body(q_ref, g_ref, kvc_ref, kvp_ref, sink_ref, y_ref, o_ref, lse_ref, otbuf):
        n = pl.program_id(0)
        kk = jnp.concatenate([kvp_ref[:, 0:D_KV], kvc_ref[:, 0:D_KV]], axis=0)
        vv = jnp.concatenate([kvp_ref[:, D_KV:2 * D_KV], kvc_ref[:, D_KV:2 * D_KV]], axis=0)
        valid = _swa_mask_t(n == 0)
        qv = q_ref[...]
        for j in range(ATT_KVH):
            js = slice(ATT_HD * j, ATT_HD * (j + 1))
            st = _dot_nt(kk[:, js], _stack_heads(qv, j)) * scale
            st = jnp.where(valid, st, NEG_BIG)
            sink = _head_rows(sink_ref, j)
            m = jnp.maximum(jnp.max(st, axis=0, keepdims=True), sink)
            p = jnp.exp(st - m)
            vx = jnp.concatenate([vv[:, js], jnp.ones((2 * WINDOW, ATT_HD), BF16)], axis=1)
            otx = _dot_tn(vx, _bf(p))
            denom = otx[ATT_HD:ATT_HD + 1] + jnp.exp(sink - m)
            ot = otx[0:ATT_HD] * (1.0 / denom)
            lse = m + jnp.log(denom)
            for r in range(ATT_R):
                h = j * ATT_R + r
                otbuf[ATT_HD * h:ATT_HD * (h + 1), :] = ot[:, WINDOW * r:WINDOW * (r + 1)]
                lse_ref[h:h + 1, :] = lse[:, WINDOW * r:WINDOW * (r + 1)]
        o = otbuf[...].T
        o_ref[...] = o
        gv = g_ref[...]
        y_ref[...] = _bf(o * (gv * _sigmoid(gv)))

    cur = lambda wd: pl.BlockSpec((WINDOW, wd), lambda n: (n, 0))
    prv = lambda wd: pl.BlockSpec((WINDOW, wd), lambda n: (jnp.maximum(n - 1, 0), 0))
    return pl.pallas_call(
        body, name="swa_fwd", grid=(nb,),
        in_specs=[cur(D_ATT), cur(D_ATT), cur(2 * D_KV), prv(2 * D_KV), pl.BlockSpec((1, ATT_QH), lambda n: (0, 0))],
        out_specs=[cur(D_ATT), cur(D_ATT), pl.BlockSpec((ATT_QH, WINDOW), lambda n: (0, n))],
        out_shape=[jax.ShapeDtypeStruct((L, D_ATT), BF16), jax.ShapeDtypeStruct((L, D_ATT), F32),
                   jax.ShapeDtypeStruct((ATT_QH, L), F32)],
        scratch_shapes=[pltpu.VMEM((D_ATT, WINDOW), F32)],
        compiler_params=_cparams(("parallel",)),
    )(q, g, kv, kv, sinks)


def _swa_bwd(dy, q, g, kv, o, lse, pos, inv, sinks):
    L = q.shape[0]
    nb = L // WINDOW
    scale = ATT_HD ** -0.5

    def body(dy_ref, q_ref, g_ref, kvc_ref, kvp_ref, o_ref, lse_ref, posc_ref, posp_ref, inv_ref, sink_ref,
             dq_ref, dg_ref, dkv_ref, dsink_ref, carry, dqbuf, dkbuf, dvbuf):
        n = pl.program_id(0)

        @pl.when(n == 0)
        def _():
            dsink_ref[...] = jnp.zeros_like(dsink_ref)

        @pl.when(n < nb)
        def _():
            tc = _rope_tables(posc_ref, inv_ref)
            tp = _rope_tables(posp_ref, inv_ref)
            kk = jnp.concatenate([kvp_ref[:, 0:D_KV], kvc_ref[:, 0:D_KV]], axis=0)
            vv = jnp.concatenate([kvp_ref[:, D_KV:2 * D_KV], kvc_ref[:, D_KV:2 * D_KV]], axis=0)
            valid = _swa_mask_t(n == 0)
            qv = q_ref[...]
            gv = g_ref[...]
            sg = _sigmoid(gv)
            dyv = dy_ref[...]
            ov = o_ref[...]
            dg_ref[...] = _bf(dyv * ov * (sg * (1.0 + gv * (1.0 - sg))))
            do = dyv * (gv * sg)
            dod = do * ov
            ones = jnp.ones((8, ATT_HD), BF16)
            lane16 = _iota2((1, ATT_QH), 1)
            dsink = jnp.zeros((1, ATT_QH), F32)
            for j in range(ATT_KVH):
                js = slice(ATT_HD * j, ATT_HD * (j + 1))
                kj = kk[:, js]
                vj = vv[:, js]
                qs = _stack_heads(qv, j)
                dos = _bf(_stack_heads(do, j))
                hi, lo = _hi_lo(_stack_heads(dod, j))
                delta = (_dot_nt(ones, hi) + _dot_nt(ones, lo))[0:1]
                lse = _head_rows(lse_ref, j)
                st = _dot_nt(kj, qs) * scale
                pt = jnp.exp(jnp.where(valid, st, NEG_BIG) - lse)
                dst = _bf(pt * (_dot_nt(vj, dos) - delta))
                dqt = _dot_tn(kj, dst) * scale
                dkbuf[:, js] = _dot(dst, qs) * scale
                dvbuf[:, js] = _dot(_bf(pt), dos)
                sd = jnp.exp(_head_rows(sink_ref, j) - lse) * delta
                for r in range(ATT_R):
                    h = j * ATT_R + r
                    ls = slice(WINDOW * r, WINDOW * (r + 1))
                    dqbuf[ATT_HD * h:ATT_HD * (h + 1), :] = dqt[:, ls]
                    dsink = dsink - jnp.sum(sd[:, ls], axis=1, keepdims=True) * (lane16 == h).astype(F32)
            dsink_ref[...] += dsink
            dq_ref[...] = _bf(_rope_t(dqbuf[...].T, tc))
            dkp = _rope_t(dkbuf[0:WINDOW, :], tp)
            dkc = _rope_t(dkbuf[WINDOW:2 * WINDOW, :], tc)

            @pl.when(n > 0)
            def _():
                dkv_ref[:, 0:D_KV] = _bf(carry[:, 0:D_KV] + dkp)
                dkv_ref[:, D_KV:2 * D_KV] = _bf(carry[:, D_KV:2 * D_KV] + dvbuf[0:WINDOW, :])

            carry[:, 0:D_KV] = dkc
            carry[:, D_KV:2 * D_KV] = dvbuf[WINDOW:2 * WINDOW, :]

        @pl.when(n == nb)
        def _():
            dkv_ref[...] = _bf(carry[...])

    last = nb - 1
    cur = lambda wd: pl.BlockSpec((WINDOW, wd), lambda n: (jnp.minimum(n, last), 0))
    prv = lambda wd: pl.BlockSpec((WINDOW, wd), lambda n: (jnp.maximum(jnp.minimum(n, last) - 1, 0), 0))
    return pl.pallas_call(
        body, name="swa_bwd", grid=(nb + 1,),
        in_specs=[cur(D_ATT), cur(D_ATT), cur(D_ATT), cur(2 * D_KV), prv(2 * D_KV), cur(D_ATT),
                  pl.BlockSpec((ATT_QH, WINDOW), lambda n: (0, jnp.minimum(n, last))), cur(1), prv(1),
                  pl.BlockSpec((1, 2 * ATT_HD), lambda n: (0, 0)), pl.BlockSpec((1, ATT_QH), lambda n: (0, 0))],
        out_specs=[cur(D_ATT), cur(D_ATT),
                   pl.BlockSpec((WINDOW, 2 * D_KV), lambda n: (jnp.maximum(n - 1, 0), 0)),
                   pl.BlockSpec((1, ATT_QH), lambda n: (0, 0))],
        out_shape=[jax.ShapeDtypeStruct((L, D_ATT), BF16), jax.ShapeDtypeStruct((L, D_ATT), BF16),
                   jax.ShapeDtypeStruct((L, 2 * D_KV), BF16), jax.ShapeDtypeStruct((1, ATT_QH), F32)],
        scratch_shapes=[pltpu.VMEM((WINDOW, 2 * D_KV), F32), pltpu.VMEM((D_ATT, WINDOW), F32),
                        pltpu.VMEM((2 * WINDOW, D_KV), F32), pltpu.VMEM((2 * WINDOW, D_KV), F32)],
        compiler_params=_cparams(("arbitrary",)),
    )(dy, q, g, kv, kv, o, lse, pos, pos, inv, sinks)


def _out_ln_loss(y_ssd, y_att, x, target, w_out, ln_g, ln_b):
    L = x.shape[0]
    tm = min(ROW_TILE, L)
    nt = L // tm
    inv_d = 1.0 / D_MODEL

    def body(ys_ref, ya_ref, x_ref, t_ref, w_ref, g_ref, b_ref, dr_ref, dys_ref, dya_ref, loss_ref, gg_ref, gb_ref,
             gwo_ref, acc_ref):
        i = pl.program_id(0)

        @pl.when(i == 0)
        def _():
            loss_ref[...] = jnp.zeros_like(loss_ref)
            gg_ref[...] = jnp.zeros_like(gg_ref)
            gb_ref[...] = jnp.zeros_like(gb_ref)
            acc_ref[...] = jnp.zeros_like(acc_ref)

        halves = [slice(0, tm // 2), slice(tm // 2, tm)]
        hs = [_dot(_bf(ys_ref[rs, :]), w_ref[0:D_SSD, :]) + _dot(_bf(ya_ref[rs, :]), w_ref[D_SSD:D_MIX, :]) for rs in halves]
        gam = g_ref[...]
        for rs, h in zip(halves, hs):
            r = ALPHA * x_ref[rs, :] + h
            mu = jnp.mean(r, axis=-1, keepdims=True)
            xc = r - mu
            rstd = lax.rsqrt(jnp.mean(xc * xc, axis=-1, keepdims=True) + LN_EPS)
            xhat = xc * rstd
            diff = xhat * gam + b_ref[...] - t_ref[rs, :]
            part = jnp.sum(jnp.sum(diff * diff, axis=-1, keepdims=True), axis=0, keepdims=True)
            loss_ref[...] += (0.5 * inv_d) * part
            dout = diff * inv_d
            gg_ref[...] += jnp.sum(dout * xhat, axis=0, keepdims=True)
            gb_ref[...] += jnp.sum(dout, axis=0, keepdims=True)
            dxh = dout * gam
            dr_ref[rs, :] = rstd * (dxh - jnp.mean(dxh, axis=-1, keepdims=True)
                                    - xhat * jnp.mean(dxh * xhat, axis=-1, keepdims=True))
        for rs in halves:
            drh = _bf(dr_ref[rs, :])
            dys_ref[rs, :] = _dot_nt(drh, w_ref[0:D_SSD, :])
            dya_ref[rs, :] = _dot_nt(drh, w_ref[D_SSD:D_MIX, :])
        drb = _bf(dr_ref[...])
        acc_ref[0:D_SSD, :] += _dot_tn(_bf(ys_ref[...]), drb)
        acc_ref[D_SSD:D_MIX, :] += _dot_tn(_bf(ya_ref[...]), drb)

        @pl.when(i == nt - 1)
        def _():
            gwo_ref[...] = _bf(acc_ref[...])

    row = pl.BlockSpec((tm, D_MODEL), lambda i: (i, 0))
    vec = pl.BlockSpec((1, D_MODEL), lambda i: (0, 0))
    return pl.pallas_call(
        body, name="out_ln_loss", grid=(nt,),
        in_specs=[row, row, row, row, pl.BlockSpec((D_MIX, D_MODEL), lambda i: (0, 0), pipeline_mode=pl.Buffered(1)), vec, vec],
        out_specs=[row, row, row, pl.BlockSpec((1, 128), lambda i: (0, 0)), vec, vec,
                   pl.BlockSpec((D_MIX, D_MODEL), lambda i: (0, 0))],
        out_shape=[jax.ShapeDtypeStruct((L, D_MODEL), F32)] * 3 + [jax.ShapeDtypeStruct((1, 128), F32)]
        + [jax.ShapeDtypeStruct((1, D_MODEL), F32)] * 2 + [jax.ShapeDtypeStruct((D_MIX, D_MODEL), BF16)],
        scratch_shapes=[pltpu.VMEM((D_MIX, D_MODEL), F32)],
        compiler_params=_cparams(("arbitrary",)),
    )(y_ssd, y_att, x, target, w_out, ln_g, ln_b)


def _local_step(x, pos, target, w, get_w_out, token, conv_w, conv_b, dt_bias, a_log, d_skip, norm_w, sinks, ln_g, ln_b):
    inv8 = ROPE_THETA ** (-jnp.arange(0, ROPE_DIM, 2, dtype=F32) / ROPE_DIM)
    inv = jnp.tile(jnp.concatenate([inv8, inv8, jnp.zeros((ATT_HD - ROPE_DIM,), F32)]), 2).reshape(1, 2 * ATT_HD)
    inv = inv + token

    z, g, q, xbc, kv, dtp, xb = _in_proj(x, w, pos, inv)
    y_ssd, y_pre, prev = _ssd_fwd(z, xbc, dtp, conv_w, conv_b, dt_bias, a_log, d_skip, norm_w)
    y_att, o, lse = _swa_fwd(q, g, kv, sinks)
    w_out = get_w_out(lse)
    dr, dy_ssd, dy_att, loss, g_ln_g, g_ln_b, gw_out = _out_ln_loss(y_ssd, y_att, x, target, w_out, ln_g, ln_b)
    w_out_red = _reduce_w_out_start(gw_out.reshape(N_CHIPS, W_OUT_ROWS, D_MODEL), loss)
    inv = inv + w_out_red[16][0:1, :]
    dq, dg, dkv, g_sinks = _swa_bwd(dy_att, q, g, kv, o, lse, pos, inv, sinks)
    dz, dxbc, ddt, g_conv_w, g_conv_b, g_dt_bias, g_a_log, g_d_skip, g_norm_w = _ssd_bwd(
        dy_ssd, z, y_pre, xbc, dtp, prev, conv_w, conv_b, dt_bias, a_log, d_skip, norm_w)
    gw_z, gw_g, gw_q = _matmuls_tn([dz, dg, dq], xb, "gw_zgq")
    gw_xbc, gw_kv, gw_dt = _matmuls_tn([dxbc, dkv, ddt], xb, "gw_xbc_kv_dt")
    gw_in = jnp.concatenate([gw_z, gw_xbc, gw_dt[0:SSD_HEADS], gw_q, gw_kv, gw_g], axis=0)
    small = dict(conv_w=g_conv_w, conv_b=g_conv_b, dt_bias=g_dt_bias, a_log=g_a_log, d_skip=g_d_skip,
                 ssd_norm_w=g_norm_w, attn_sinks=g_sinks, ln_g=g_ln_g, ln_b=g_ln_b)
    return loss, (dr, dz, dg, dq, dxbc, dkv, ddt, w), gw_in, w_out_red, small


def _mesh_pos():
    return lax.axis_index("x"), lax.axis_index("y"), lax.axis_index("c")


def _gather_weights(w_in_s, conv_w_s):
    hr = w_in_s.shape[0] // 2
    qa = 336
    quarters = ((0, qa), (qa, hr - qa))

    def body(win_ref, cw_ref, owin_ref, ocw_ref, stage, send_sems, recv_sems, small_send, small_recv, local_sems):
        x, y, c = _mesh_pos()
        me = 2 * x + y
        sibling = (x, y, 1 - c)
        xn, yn, dg = (1 - x, y), (x, 1 - y), (1 - x, 1 - y)
        chips = [xn, yn, dg]
        load = pltpu.make_async_copy(win_ref, stage, local_sems.at[1])
        load.start()
        locals_ = [pltpu.make_async_copy(cw_ref, ocw_ref.at[me], local_sems.at[0])]
        for cp in locals_:
            cp.start()
        started = []

        def piece(ref, chip, half, q):
            off, n = quarters[q]
            return ref.at[2 * chip[0] + chip[1]].at[pl.ds(half * hr + off, n), :]

        def mine(q):
            off, n = quarters[q]
            return win_ref.at[pl.ds(c * hr + off, n), :]

        def copy(src, dst, k, to):
            return pltpu.make_async_remote_copy(src_ref=src, dst_ref=dst, send_sem=send_sems.at[k], recv_sem=recv_sems.at[k],
                                                device_id=to, device_id_type=MESH)

        def go(cp):
            cp.start()
            started.append(cp)

        go(copy(mine(0), piece(owin_ref, (x, y), c, 0), 0, (*xn, c)))
        go(copy(mine(1), piece(owin_ref, (x, y), c, 1), 2, (*yn, c)))
        go(copy(mine(1), piece(owin_ref, (x, y), c, 1), 1, (*xn, c)))
        go(copy(mine(0), piece(owin_ref, (x, y), c, 0), 3, (*yn, c)))
        for j, (px, py) in enumerate(chips):
            cp = pltpu.make_async_remote_copy(
                src_ref=cw_ref, dst_ref=ocw_ref.at[me], send_sem=small_send.at[j], recv_sem=small_recv.at[j],
                device_id=(px, py, c), device_id_type=MESH)
            go(cp)
        load.wait()
        store = pltpu.make_async_copy(stage, owin_ref.at[me], local_sems.at[2])
        store.start()
        locals_.append(store)
        arrivals = [(0, xn, 0, (4, (*yn, c))), (2, yn, 1, (5, (*xn, c))), (1, xn, 1, None), (3, yn, 0, None),
                    (4, dg, 0, None), (5, dg, 1, None)]
        for n, (k, chip, q, onward) in enumerate(arrivals):
            blk = piece(owin_ref, chip, c, q)
            copy(blk, blk, k, sibling).wait_recv()
            if onward is not None:
                go(copy(blk, blk, onward[0], onward[1]))
            go(copy(blk, blk, 6 + n, sibling))
        for n, (k, chip, q, onward) in enumerate(arrivals):
            blk = piece(owin_ref, chip, 1 - c, q)
            copy(blk, blk, 6 + n, sibling).wait_recv()
        for j in range(3):
            pltpu.make_async_remote_copy(
                src_ref=cw_ref, dst_ref=ocw_ref.at[me], send_sem=small_send.at[j], recv_sem=small_recv.at[j],
                device_id=sibling, device_id_type=MESH).wait_recv()
        for cp in started:
            cp.wait_send()
        for cp in locals_:
            cp.wait()

    any_spec = pl.BlockSpec(memory_space=pl.ANY)
    return pl.pallas_call(
        body, name="gather_weights",
        in_specs=[any_spec] * 2, out_specs=[any_spec] * 2,
        out_shape=[jax.ShapeDtypeStruct((N_CHIPS,) + a.shape, a.dtype) for a in (w_in_s, conv_w_s)],
        scratch_shapes=[pltpu.VMEM(w_in_s.shape, w_in_s.dtype),
                        pltpu.SemaphoreType.DMA((12,)), pltpu.SemaphoreType.DMA((12,)),
                        pltpu.SemaphoreType.DMA((3,)), pltpu.SemaphoreType.DMA((3,)), pltpu.SemaphoreType.DMA((3,))],
    )(w_in_s, conv_w_s)


_HBM = pl.BlockSpec(memory_space=pltpu.HBM)
_SEM = pl.BlockSpec(memory_space=pltpu.SEMAPHORE)
_EFFECT = pltpu.SideEffectType.DATAFLOW_SIDE_EFFECTING


def _gather_w_out_start(w_out_s, after):
    def body(src_ref, land_ref, after_ref, s0, s1, s2, r0, r1, r2, src_thru, land_thru, token):
        x, y, c = _mesh_pos()
        me = 2 * x + y
        chips = [(1 - x, y), (x, 1 - y), (1 - x, 1 - y)]
        for (px, py), s, r in zip(chips, (s0, s1, s2), (r0, r1, r2)):
            pltpu.make_async_remote_copy(src_ref=src_ref, dst_ref=land_ref.at[me], send_sem=s, recv_sem=r,
                                         device_id=(px, py, c), device_id_type=MESH).start()
        token[...] = jnp.zeros_like(token)

    sem = pltpu.SemaphoreType.DMA(())
    land = lax.empty((N_CHIPS,) + w_out_s.shape, w_out_s.dtype)
    return pl.pallas_call(
        body, name="gather_w_out_start",
        out_shape=(sem,) * 6 + (pltpu.HBM(w_out_s.shape, w_out_s.dtype), pltpu.HBM(land.shape, land.dtype),
                                jax.ShapeDtypeStruct((8, 128), F32)),
        in_specs=(_HBM, _HBM, pl.BlockSpec(memory_space=pl.ANY)),
        out_specs=(_SEM,) * 6 + (_HBM, _HBM, pl.BlockSpec(memory_space=pltpu.VMEM)),
        input_output_aliases={0: 6, 1: 7},
        compiler_params=pltpu.CompilerParams(has_side_effects=_EFFECT),
    )(pltpu.with_memory_space_constraint(w_out_s, pltpu.HBM), pltpu.with_memory_space_constraint(land, pltpu.HBM), after)


def _gather_w_out_wait(sems, src_thru, land_thru, after):
    def body(src_ref, land_ref, s0, s1, s2, r0, r1, r2, after_ref, src_dead, got_ref):
        x, y, c = _mesh_pos()
        chips = [(1 - x, y), (x, 1 - y), (1 - x, 1 - y)]
        for (px, py), s, r in zip(chips, (s0, s1, s2), (r0, r1, r2)):
            cp = pltpu.make_async_remote_copy(src_ref=src_ref, dst_ref=land_ref.at[2 * px + py], send_sem=s, recv_sem=r,
                                              device_id=(px, py, c), device_id_type=MESH)
            cp.wait_send()
            cp.wait_recv()

    return pl.pallas_call(
        body, name="gather_w_out_wait",
        out_shape=(pltpu.HBM(src_thru.shape, src_thru.dtype), pltpu.HBM(land_thru.shape, land_thru.dtype)),
        in_specs=(_HBM, _HBM) + (_SEM,) * 6 + (pl.BlockSpec(memory_space=pl.ANY),),
        out_specs=(_HBM, _HBM), input_output_aliases={0: 0, 1: 1},
        compiler_params=pltpu.CompilerParams(has_side_effects=_EFFECT),
    )(src_thru, land_thru, *sems, after)[1]


def _pair_start(gw_in, after):
    hr = gw_in.shape[1] // 2

    def body(src_ref, land_ref, after_ref, *refs):
        x, y, c = _mesh_pos()
        for j in range(N_CHIPS):
            pltpu.make_async_remote_copy(
                src_ref=src_ref.at[j, pl.ds((1 - c) * hr, hr), :], dst_ref=land_ref.at[j], send_sem=refs[j],
                recv_sem=refs[N_CHIPS + j], device_id=(x, y, 1 - c), device_id_type=MESH).start()
        refs[10][...] = jnp.zeros_like(refs[10])

    sem = pltpu.SemaphoreType.DMA(())
    land = lax.empty((N_CHIPS, hr, D_MODEL), F32)
    return pl.pallas_call(
        body, name="pair_start",
        out_shape=(sem,) * 8 + (pltpu.HBM(gw_in.shape, F32), pltpu.HBM(land.shape, F32), jax.ShapeDtypeStruct((8, 128), F32)),
        in_specs=(_HBM, _HBM, pl.BlockSpec(memory_space=pl.ANY)),
        out_specs=(_SEM,) * 8 + (_HBM, _HBM, pl.BlockSpec(memory_space=pltpu.VMEM)),
        input_output_aliases={0: 8, 1: 9},
        compiler_params=pltpu.CompilerParams(has_side_effects=_EFFECT),
    )(pltpu.with_memory_space_constraint(gw_in, pltpu.HBM), pltpu.with_memory_space_constraint(land, pltpu.HBM), after)


def _pair_wait(sems, gw_thru, land_thru, after):
    hr = land_thru.shape[1]

    def body(src_ref, land_ref, *refs):
        x, y, c = _mesh_pos()
        for j in range(N_CHIPS):
            cp = pltpu.make_async_remote_copy(
                src_ref=src_ref.at[j, pl.ds((1 - c) * hr, hr), :], dst_ref=land_ref.at[j], send_sem=refs[j],
                recv_sem=refs[N_CHIPS + j], device_id=(x, y, 1 - c), device_id_type=MESH)
            cp.wait_send()
            cp.wait_recv()

    return pl.pallas_call(
        body, name="pair_wait",
        out_shape=(pltpu.HBM(gw_thru.shape, F32), pltpu.HBM(land_thru.shape, F32)),
        in_specs=(_HBM, _HBM) + (_SEM,) * 8 + (pl.BlockSpec(memory_space=pl.ANY),),
        out_specs=(_HBM, _HBM), input_output_aliases={0: 0, 1: 1},
        compiler_params=pltpu.CompilerParams(has_side_effects=_EFFECT),
    )(gw_thru, land_thru, *sems, after)


def _chip_start(s_in, after):
    def body(src_ref, land_ref, after_ref, *refs):
        x, y, c = _mesh_pos()
        me = 2 * x + y
        for j, (px, py) in enumerate([(1 - x, y), (x, 1 - y), (1 - x, 1 - y)]):
            pltpu.make_async_remote_copy(
                src_ref=src_ref.at[2 * px + py], dst_ref=land_ref.at[me], send_sem=refs[j], recv_sem=refs[3 + j],
                device_id=(px, py, c), device_id_type=MESH).start()
        refs[8][...] = jnp.zeros_like(refs[8])

    sem = pltpu.SemaphoreType.DMA(())
    land = lax.empty(s_in.shape, s_in.dtype)
    return pl.pallas_call(
        body, name="chip_start",
        out_shape=(sem,) * 6 + (pltpu.HBM(s_in.shape, s_in.dtype), pltpu.HBM(land.shape, land.dtype),
                                jax.ShapeDtypeStruct((8, 128), F32)),
        in_specs=(_HBM, _HBM, pl.BlockSpec(memory_space=pl.ANY)),
        out_specs=(_SEM,) * 6 + (_HBM, _HBM, pl.BlockSpec(memory_space=pltpu.VMEM)),
        input_output_aliases={0: 6, 1: 7},
        compiler_params=pltpu.CompilerParams(has_side_effects=_EFFECT),
    )(pltpu.with_memory_space_constraint(s_in, pltpu.HBM), pltpu.with_memory_space_constraint(land, pltpu.HBM), after)


def _chip_wait(sems, s_thru, land_thru, after):
    def body(src_ref, land_ref, *refs):
        x, y, c = _mesh_pos()
        for j, (px, py) in enumerate([(1 - x, y), (x, 1 - y), (1 - x, 1 - y)]):
            cp = pltpu.make_async_remote_copy(
                src_ref=src_ref.at[2 * px + py], dst_ref=land_ref.at[2 * px + py], send_sem=refs[j], recv_sem=refs[3 + j],
                device_id=(px, py, c), device_id_type=MESH)
            cp.wait_send()
            cp.wait_recv()

    return pl.pallas_call(
        body, name="chip_wait",
        out_shape=(pltpu.HBM(s_thru.shape, s_thru.dtype), pltpu.HBM(land_thru.shape, land_thru.dtype)),
        in_specs=(_HBM, _HBM) + (_SEM,) * 6 + (pl.BlockSpec(memory_space=pl.ANY),),
        out_specs=(_HBM, _HBM), input_output_aliases={0: 0, 1: 1},
        compiler_params=pltpu.CompilerParams(has_side_effects=_EFFECT),
    )(s_thru, land_thru, *sems, after)


def _pair_share(h_in, small):
    def body(hin_ref, sm_ref, rin_ref, slots_ref, send_sems, recv_sems, small_send, small_recv, local_sem):
        x, y, c = _mesh_pos()
        dev = 4 * x + 2 * y + c
        mine = pltpu.make_async_copy(sm_ref, slots_ref.at[dev], local_sem)
        mine.start()
        share = pltpu.make_async_remote_copy(
            src_ref=hin_ref, dst_ref=rin_ref, send_sem=send_sems.at[0], recv_sem=recv_sems.at[0],
            device_id=(x, y, 1 - c), device_id_type=MESH)
        share.start()
        started = []
        for k in range(1, 8):
            peer = (x ^ ((k >> 2) & 1), y ^ ((k >> 1) & 1), c ^ (k & 1))
            cp = pltpu.make_async_remote_copy(
                src_ref=sm_ref, dst_ref=slots_ref.at[dev], send_sem=small_send.at[k - 1], recv_sem=small_recv.at[k - 1],
                device_id=peer, device_id_type=MESH)
            cp.start()
            started.append(cp)
        share.wait()
        for k in range(1, 8):
            pltpu.make_async_remote_copy(
                src_ref=sm_ref, dst_ref=slots_ref.at[dev], send_sem=small_send.at[k - 1], recv_sem=small_recv.at[k - 1],
                device_id=(x, y, 1 - c), device_id_type=MESH).wait_recv()
        for cp in started:
            cp.wait_send()
        mine.wait()

    any_spec = pl.BlockSpec(memory_space=pl.ANY)
    return pl.pallas_call(
        body, name="pair_share",
        in_specs=[any_spec] * 2, out_specs=[any_spec] * 2,
        out_shape=[jax.ShapeDtypeStruct(h_in.shape, F32), jax.ShapeDtypeStruct((8,) + small.shape, F32)],
        scratch_shapes=[pltpu.SemaphoreType.DMA((1,)), pltpu.SemaphoreType.DMA((1,)),
                        pltpu.SemaphoreType.DMA((7,)), pltpu.SemaphoreType.DMA((7,)), pltpu.SemaphoreType.DMA],
    )(h_in, small)


def _reduce_w_out_start(slabs, after):
    def body(src_ref, land_ref, after_ref, *refs):
        x, y, c = _mesh_pos()
        me = 4 * x + 2 * y + c
        for k in range(1, 8):
            px, py, pc = x ^ ((k >> 2) & 1), y ^ ((k >> 1) & 1), c ^ (k & 1)
            pltpu.make_async_remote_copy(src_ref=src_ref.at[2 * px + py], dst_ref=land_ref.at[me], send_sem=refs[k - 1],
                                         recv_sem=refs[6 + k], device_id=(px, py, pc), device_id_type=MESH).start()
        refs[16][...] = jnp.zeros_like(refs[16])

    sem = pltpu.SemaphoreType.DMA(())
    land = lax.empty((8,) + slabs.shape[1:], slabs.dtype)
    return pl.pallas_call(
        body, name="reduce_w_out_start",
        out_shape=(sem,) * 14 + (pltpu.HBM(slabs.shape, slabs.dtype), pltpu.HBM(land.shape, land.dtype),
                                 jax.ShapeDtypeStruct((8, 128), F32)),
        in_specs=(_HBM, _HBM, pl.BlockSpec(memory_space=pl.ANY)),
        out_specs=(_SEM,) * 14 + (_HBM, _HBM, pl.BlockSpec(memory_space=pltpu.VMEM)),
        input_output_aliases={0: 14, 1: 15},
        compiler_params=pltpu.CompilerParams(has_side_effects=_EFFECT),
    )(pltpu.with_memory_space_constraint(slabs, pltpu.HBM), pltpu.with_memory_space_constraint(land, pltpu.HBM), after)


def _reduce_w_out_wait(sems, slabs_thru, land_thru, after):
    def body(src_ref, land_ref, *refs):
        x, y, c = _mesh_pos()
        for k in range(1, 8):
            px, py, pc = x ^ ((k >> 2) & 1), y ^ ((k >> 1) & 1), c ^ (k & 1)
            cp = pltpu.make_async_remote_copy(
                src_ref=src_ref.at[2 * px + py], dst_ref=land_ref.at[4 * px + 2 * py + pc], send_sem=refs[k - 1],
                recv_sem=refs[6 + k], device_id=(px, py, pc), device_id_type=MESH)
            cp.wait_send()
            cp.wait_recv()

    return pl.pallas_call(
        body, name="reduce_w_out_wait",
        out_shape=(pltpu.HBM(slabs_thru.shape, slabs_thru.dtype), pltpu.HBM(land_thru.shape, land_thru.dtype)),
        in_specs=(_HBM, _HBM) + (_SEM,) * 14 + (pl.BlockSpec(memory_space=pl.ANY),),
        out_specs=(_HBM, _HBM), input_output_aliases={0: 0, 1: 1},
        compiler_params=pltpu.CompilerParams(has_side_effects=_EFFECT),
    )(slabs_thru, land_thru, *sems, after)


def _pair_add(g, recv, core, name):
    _, rows, C = recv.shape
    tc = 256

    def body(core_ref, g_ref, r_ref, o_ref):
        o_ref[...] = _bf(g_ref[...] + r_ref[...])

    spec = pl.BlockSpec((1, rows, tc), lambda j, i, core: (j, 0, i))
    return pl.pallas_call(
        body, name=name,
        grid_spec=pltpu.PrefetchScalarGridSpec(
            num_scalar_prefetch=1, grid=(N_CHIPS, C // tc),
            in_specs=[pl.BlockSpec((1, rows, tc), lambda j, i, core: (j, core[0], i)), spec], out_specs=spec),
        out_shape=jax.ShapeDtypeStruct((N_CHIPS, rows, C), BF16),
        compiler_params=_cparams(("parallel", "parallel")),
    )(core, g, recv)


def _chip_add(own, parts, chip, name):
    _, rows, C = parts.shape
    tc = 256

    def body(chip_ref, own_ref, r0, r1, r2, r3, o_ref):
        acc = None
        for j, r in enumerate((r0, r1, r2, r3)):
            term = jnp.where(chip_ref[0] == j, own_ref[0], r[0]).astype(F32)
            acc = term if acc is None else acc + term
        o_ref[...] = acc

    def slab(j):
        return pl.BlockSpec((1, rows, tc), lambda i, chip: (jnp.where(chip[0] == j, (j + 1) % N_CHIPS, j), 0, i))

    return pl.pallas_call(
        body, name=name,
        grid_spec=pltpu.PrefetchScalarGridSpec(
            num_scalar_prefetch=1, grid=(C // tc,),
            in_specs=[pl.BlockSpec((1, rows, tc), lambda i, chip: (chip[0], 0, i))] + [slab(j) for j in range(N_CHIPS)],
            out_specs=pl.BlockSpec((rows, tc), lambda i, chip: (0, i))),
        out_shape=jax.ShapeDtypeStruct((rows, C), F32),
        compiler_params=_cparams(("parallel",)),
    )(chip, own, parts, parts, parts, parts)


def _adamw_math(w, g, m, v):
    m = ADAM_B1 * m + (1.0 - ADAM_B1) * g
    v = ADAM_B2 * v + (1.0 - ADAM_B2) * (g * g)
    m_hat = m / (1.0 - ADAM_B1 ** ADAM_STEP)
    v_hat = v / (1.0 - ADAM_B2 ** ADAM_STEP)
    delta = -ADAM_LR * (m_hat / (jnp.sqrt(v_hat) + ADAM_EPS) + ADAM_WD * w)
    return delta, m, v


def _adamw_rows(w, g_own, g_sib, m, v, core, name):
    R, C = w.shape[0], w.shape[-1]
    rows = g_own.shape[0]
    step = 256
    chunks = [(r, min(step, R - r)) for r in range(0, R, step)]
    sub = 64

    def body(core_ref, w_hbm, go_hbm, gs_hbm, m_hbm, v_hbm, d_hbm, nm_hbm, nv_hbm, g_hbm,
             wbuf, mbuf, vbuf, gbuf, dbuf, nmbuf, nvbuf, in_sems, g_sems, out_sems):
        c = core_ref[0]
        flat = lambda ref: ref.at[:, 0, :]
        g_in = [pltpu.make_async_copy(go_hbm, gbuf.at[pl.ds(pl.multiple_of(c * rows, 8), rows), :], g_sems.at[0]),
                pltpu.make_async_copy(gs_hbm, gbuf.at[pl.ds(pl.multiple_of((1 - c) * rows, 8), rows), :], g_sems.at[1])]
        for cp in g_in:
            cp.start()
        loads = []
        for k, (r0, n) in enumerate(chunks):
            cps = [pltpu.make_async_copy(flat(src).at[pl.ds(r0, n), :], dst.at[pl.ds(r0, n), :], in_sems.at[a, k])
                   for a, (src, dst) in enumerate(((w_hbm, wbuf), (m_hbm, mbuf), (v_hbm, vbuf)))]
            for cp in cps:
                cp.start()
            loads.append(cps)
        for cp in g_in:
            cp.wait()
        stores = []
        for k, (r0, n) in enumerate(chunks):
            for cp in loads[k]:
                cp.wait()

            def update(rs):
                g = gbuf[rs, :]
                dl, nm, nv = _adamw_math(wbuf[rs, :], g, mbuf[rs, :], vbuf[rs, :])
                dbuf[rs, :] = dl
                nmbuf[rs, :] = nm
                nvbuf[rs, :] = nv

            if n % sub == 0:
                def block(i, carry, r0=r0):
                    update(pl.ds(pl.multiple_of(r0 + i * sub, 8), sub))
                    return carry
                lax.fori_loop(0, n // sub, block, 0)
            else:
                update(pl.ds(r0, n))
            cps = [pltpu.make_async_copy(src.at[pl.ds(r0, n), :], flat(dst).at[pl.ds(r0, n), :], out_sems.at[a, k])
                   for a, (src, dst) in enumerate(((dbuf, d_hbm), (nmbuf, nm_hbm), (nvbuf, nv_hbm), (gbuf, g_hbm)))]
            for cp in cps:
                cp.start()
            stores += cps
        for cp in stores:
            cp.wait()

    any_spec = pl.BlockSpec(memory_space=pl.ANY)
    dense = pltpu.VMEM((R, C), F32)
    return pl.pallas_call(
        body, name=name,
        grid_spec=pltpu.PrefetchScalarGridSpec(
            num_scalar_prefetch=1, grid=(1,),
            in_specs=[any_spec] * 5, out_specs=[any_spec] * 4,
            scratch_shapes=[dense, dense, dense, pltpu.VMEM((2 * rows, C), F32), dense, dense, dense,
                            pltpu.SemaphoreType.DMA((3, len(chunks))), pltpu.SemaphoreType.DMA((2,)),
                            pltpu.SemaphoreType.DMA((4, len(chunks)))]),
        out_shape=[jax.ShapeDtypeStruct(w.shape, F32)] * 4,
        compiler_params=_cparams(),
    )(core, w, g_own, g_sib, m, v)


def _adamw_sum8(w, slabs, land, m, v, ids, name):
    R, C = w.shape
    tc = 128

    def body(ids_ref, w_ref, own_ref, *refs):
        lrefs, (m_ref, v_ref, d_ref, nm_ref, nv_ref, g_ref) = refs[:8], refs[8:]
        g = None
        for d, l_ref in enumerate(lrefs):
            term = jnp.where(ids_ref[0] == d, own_ref[0], l_ref[0]).astype(F32)
            g = term if g is None else g + term
        dl, nm, nv = _adamw_math(w_ref[...], g, m_ref[...], v_ref[...])
        d_ref[...] = dl
        nm_ref[...] = nm
        nv_ref[...] = nv
        g_ref[...] = g

    def slot(d):
        return pl.BlockSpec((1, R, tc), lambda i, ids: (jnp.where(ids[0] == d, (d + 1) % 8, d), 0, i))

    spec = pl.BlockSpec((R, tc), lambda i, ids: (0, i))
    return pl.pallas_call(
        body, name=name,
        grid_spec=pltpu.PrefetchScalarGridSpec(
            num_scalar_prefetch=1, grid=(C // tc,),
            in_specs=[spec, pl.BlockSpec((1, R, tc), lambda i, ids: (ids[1], 0, i))] + [slot(d) for d in range(8)]
            + [spec, spec],
            out_specs=[spec] * 4),
        out_shape=[jax.ShapeDtypeStruct((R, C), F32)] * 4,
        compiler_params=_cparams(("parallel",)),
    )(ids, w, slabs, *([land] * 8), m, v)


SMALL_NAMES = ("conv_b", "ssd_norm_w", "ln_g", "ln_b", "dt_bias", "a_log", "d_skip", "attn_sinks")
SMALL_FIELDS = ((4, 0, D_XBC), (5, 0, D_SSD), (6, 0, D_MODEL), (7, 0, D_MODEL), (5, 1024, SSD_HEADS), (5, 1152, SSD_HEADS),
                (5, 1280, SSD_HEADS), (5, 1408, ATT_QH))
LOSS_FIELD = (6, 1024, 128)
K_SMALL = D_XBC


def _pack_small(g_conv_w, vecs, loss):
    def body(cw_ref, *refs):
        o_ref = refs[-1]
        o_ref[...] = jnp.zeros_like(o_ref)
        o_ref[0:CONV_K, 0:D_XBC] = cw_ref[...]
        for v_ref, (row, off, n) in zip(refs[:-2], SMALL_FIELDS):
            o_ref[row:row + 1, off:off + n] = v_ref[...]
        o_ref[LOSS_FIELD[0]:LOSS_FIELD[0] + 1, LOSS_FIELD[1]:LOSS_FIELD[1] + LOSS_FIELD[2]] = refs[-2][...]

    return pl.pallas_call(
        body, name="pack_small", out_shape=jax.ShapeDtypeStruct((8, K_SMALL), F32), compiler_params=_cparams(),
    )(g_conv_w, *vecs, loss)


def _adamw_small(slots, chip, conv_w, m_conv_w, v_conv_w, params, moms, vars_):
    n_vec = len(SMALL_NAMES)

    def body(chip_ref, s_ref, *refs):
        ins = refs[:3 * (n_vec + 1)]
        outs = refs[3 * (n_vec + 1):-1]
        tot_ref = refs[-1]
        tot = s_ref[0]
        for d in range(1, 8):
            tot = tot + s_ref[d]
        outs[0][...] = tot[LOSS_FIELD[0]:LOSS_FIELD[0] + 1, LOSS_FIELD[1]:LOSS_FIELD[1] + 1]
        off = pl.multiple_of(chip_ref[0] * CONV_COLS, 128)
        tot_ref[...] = tot
        grads = [tot_ref[0:CONV_K, pl.ds(off, CONV_COLS)]]
        grads += [tot[row:row + 1, o:o + n] for row, o, n in SMALL_FIELDS]
        for k, g in enumerate(grads):
            w_ref, m_ref, v_ref = ins[3 * k:3 * k + 3]
            full = (0,) if k == 0 else (Ellipsis,)
            d, nm, nv = _adamw_math(w_ref[full], g, m_ref[full], v_ref[full])
            for o_ref, val in zip(outs[1 + 4 * k:5 + 4 * k], (g, d, nm, nv)):
                o_ref[full] = val

    args = [conv_w, m_conv_w, v_conv_w]
    for w, m, v in zip(params, moms, vars_):
        args += [w, m, v]
    shapes = [jax.ShapeDtypeStruct((1, 1), F32)] + [jax.ShapeDtypeStruct(conv_w.shape, F32)] * 4
    for w in params:
        shapes += [jax.ShapeDtypeStruct(w.shape, F32)] * 4
    vmem = pl.BlockSpec(memory_space=pltpu.VMEM)
    return pl.pallas_call(
        body, name="adamw_small",
        grid_spec=pltpu.PrefetchScalarGridSpec(
            num_scalar_prefetch=1, grid=(1,),
            in_specs=[pl.BlockSpec(slots.shape, lambda i, chip: (0, 0, 0))] + [vmem] * len(args),
            out_specs=[vmem] * len(shapes), scratch_shapes=[pltpu.VMEM((8, K_SMALL), F32)]),
        out_shape=shapes, compiler_params=_cparams(),
    )(chip, slots, *args)


def kernel(x, positions, w_in, conv_w, conv_b, dt_bias, a_log, d_skip, ssd_norm_w, attn_sinks, w_out, ln_g, ln_b, loss_target, m_w_in, m_conv_w, m_conv_b, m_dt_bias, m_a_log, m_d_skip, m_ssd_norm_w, m_attn_sinks, m_w_out, m_ln_g, m_ln_b, v_w_in, v_conv_w, v_conv_b, v_dt_bias, v_a_log, v_d_skip, v_ssd_norm_w, v_attn_sinks, v_w_out, v_ln_g, v_ln_b):
    mx, my, mc = _mesh_pos()
    chip = 2 * mx + my
    L = x.shape[1]

    conv_w_s8 = jnp.pad(conv_w[0], ((0, 8 - CONV_K), (0, 0)))
    pad_rows = ((0, SLAB_ROWS - W_IN_COLS), (0, 0))
    w_in_t = w_in[0].T
    w_in_b, w_out_b = jnp.pad(_bf(w_in_t), pad_rows), _bf(w_out[0])
    ag_in, ag_cw = _gather_weights(w_in_b, conv_w_s8)
    started = _gather_w_out_start(w_out_b, ag_cw)
    own = (jnp.arange(N_CHIPS) == chip)[:, None, None]

    def get_w_out(after):
        landed = _gather_w_out_wait(started[0:6], started[6], started[7], after)
        return jnp.where(own, w_out_b[None], landed).reshape(D_MIX, D_MODEL)

    w_full = jnp.concatenate([ag_in[j, 0:W_IN_COLS] for j in range(N_CHIPS)], axis=0)
    w = jnp.concatenate([
        w_full[O_Z:O_Z + D_SSD], w_full[O_G:O_G + D_ATT], w_full[O_Q:O_Q + D_ATT],
        w_full[O_XBC:O_XBC + D_XBC], w_full[O_K:O_K + 2 * D_KV], w_full[O_DT:O_DT + SSD_HEADS],
        jnp.zeros((DT_PAD - SSD_HEADS, D_MODEL), BF16)], axis=0)
    conv_w_full = jnp.concatenate([ag_cw[j, 0:CONV_K] for j in range(N_CHIPS)], axis=1)

    loss_part, gx_args, gw_in, w_out_red, small = _local_step(
        x[0], positions[0].reshape(L, 1), loss_target[0], w, get_w_out, started[8][0:1, :], conv_w_full,
        conv_b, dt_bias, a_log, d_skip, ssd_norm_w, attn_sinks, ln_g, ln_b)

    packed = _pack_small(small["conv_w"], [small[n] for n in SMALL_NAMES], loss_part)
    core_id = mc.reshape(1).astype(jnp.int32)
    chip_id = chip.reshape(1).astype(jnp.int32)
    ids = jnp.stack([4 * mx + 2 * my + mc, chip]).astype(jnp.int32)
    slabs = jnp.stack([jnp.pad(gw_in[W_IN_COLS * j:W_IN_COLS * (j + 1)], pad_rows) for j in range(N_CHIPS)])
    w_in_red = _pair_start(slabs, packed)
    grad_x = _grad_x(*gx_args, w_in_red[10], 0)
    gw_in_slabs, recv_in = _pair_wait(w_in_red[0:8], w_in_red[8], w_in_red[9], grad_x[0:8, 0:128])
    s_in = _pair_add(gw_in_slabs, recv_in, core_id, "pair_add_in")
    chip_red = _chip_start(s_in, packed)
    grad_x = _grad_x(*gx_args, chip_red[8], 1, grad_x)
    own_slabs, landed = _reduce_w_out_wait(w_out_red[0:14], w_out_red[14], w_out_red[15], grad_x)
    out_t = _adamw_sum8(w_out[0], own_slabs, landed, m_w_out[0], v_w_out[0], ids, "adamw_w_out")
    d_w_out, nm_w_out, nv_w_out, g_w_out = [a[None] for a in out_t]
    s_in, r_in = _chip_wait(chip_red[0:6], chip_red[6], chip_red[7], out_t[0])
    h_in = _chip_add(s_in, r_in, chip_id, "chip_add_in")
    sib_in, slots = _pair_share(h_in, packed)

    to_rows = lambda a: jnp.transpose(a, (2, 0, 1))
    in_t = _adamw_rows(to_rows(w_in), h_in, sib_in, to_rows(m_w_in), to_rows(v_w_in), core_id, "adamw_w_in")
    d_w_in, nm_w_in, nv_w_in, g_w_in = [jnp.transpose(a, (1, 2, 0)) for a in in_t]

    params = dict(conv_b=conv_b, ssd_norm_w=ssd_norm_w, ln_g=ln_g, ln_b=ln_b, dt_bias=dt_bias, a_log=a_log,
                  d_skip=d_skip, attn_sinks=attn_sinks)
    moms = dict(conv_b=m_conv_b, ssd_norm_w=m_ssd_norm_w, ln_g=m_ln_g, ln_b=m_ln_b, dt_bias=m_dt_bias, a_log=m_a_log,
                d_skip=m_d_skip, attn_sinks=m_attn_sinks)
    vars_ = dict(conv_b=v_conv_b, ssd_norm_w=v_ssd_norm_w, ln_g=v_ln_g, ln_b=v_ln_b, dt_bias=v_dt_bias, a_log=v_a_log,
                 d_skip=v_d_skip, attn_sinks=v_attn_sinks)
    res = _adamw_small(slots, chip_id, conv_w, m_conv_w, v_conv_w, [params[n] for n in SMALL_NAMES],
                       [moms[n] for n in SMALL_NAMES], [vars_[n] for n in SMALL_NAMES])
    loss = res[0][0, 0]
    grads, delta, new_m, new_v = {}, {}, {}, {}
    for k, n in enumerate(("conv_w",) + SMALL_NAMES):
        grads[n], delta[n], new_m[n], new_v[n] = res[1 + 4 * k:5 + 4 * k]
    for dd, a_in, a_out in ((grads, g_w_in, g_w_out), (delta, d_w_in, d_w_out), (new_m, nm_w_in, nm_w_out),
                            (new_v, nv_w_in, nv_w_out)):
        dd["w_in"] = a_in
        dd["w_out"] = a_out
    order = ("w_in", "conv_w", "conv_b", "dt_bias", "a_log", "d_skip", "ssd_norm_w", "attn_sinks", "w_out", "ln_g", "ln_b")
    return (loss, grad_x[None], *[grads[n] for n in order], *[delta[n] for n in order], *[new_m[n] for n in order],
            *[new_v[n] for n in order])
```

```python
import numpy as np
import jax
import jax.numpy as jnp
from jax import lax
from jax.experimental import pallas as pl
from jax.experimental.pallas import tpu as pltpu

F32 = jnp.float32
BF16 = jnp.bfloat16
MESH = pl.DeviceIdType.MESH

D_MODEL = 1024
D_SSD = 1024
D_ATT = 1024
D_MIX = 2048
SSD_HEADS = 16
SSD_P = 64
SSD_GROUPS = 2
SSD_R = 8
SSD_N = 128
D_BC = 256
D_XBC = 1536
CONV_K = 4
CHUNK = 128
ATT_HD = 64
assert ATT_HD in (4, 16, 64, 256)
ATT_QH = 16
ATT_KVH = 4
ATT_R = 4
D_KV = 256
WINDOW = 128
ROPE_THETA = 500000.0
ROPE_DIM = 16
ALPHA = 2.0 ** 0.25
LN_EPS = 1e-5
RMS_EPS = 1e-5
D_IN_PROJ = 5136
O_Z, O_XBC, O_DT, O_Q, O_K, O_V, O_G = 0, 1024, 2560, 2576, 3600, 3856, 4112
P_Z, P_G, P_Q, P_XBC, P_KV, P_DT, P_END = 0, 1024, 2048, 3072, 4608, 5120, 5248
DT_PAD = 128
N_CHIPS = 4
W_IN_COLS = D_IN_PROJ // N_CHIPS
SLAB_ROWS = 1312
W_OUT_ROWS = D_MIX // N_CHIPS
CONV_COLS = D_XBC // N_CHIPS

ADAM_LR = 0.001
ADAM_B1 = 0.9
ADAM_B2 = 0.999
ADAM_EPS = 1e-08
ADAM_WD = 0.01
ADAM_STEP = 10

VMEM_LIMIT = 56 * 1024 * 1024
ROW_TILE = 512
NEG_BIG = -1e30


def _cparams(sem=None, **kw):
    if sem is not None:
        kw["dimension_semantics"] = sem
    return pltpu.CompilerParams(vmem_limit_bytes=VMEM_LIMIT, **kw)


def _dot(a, b):
    return jnp.dot(a, b, preferred_element_type=F32)


def _dot_nt(a, b):
    return lax.dot_general(a, b, (((1,), (1,)), ((), ())), preferred_element_type=F32)


def _dot_tn(a, b):
    return lax.dot_general(a, b, (((0,), (0,)), ((), ())), preferred_element_type=F32)


def _bf(a):
    return a.astype(BF16)


def _iota2(shape, dim):
    return lax.broadcasted_iota(jnp.int32, shape, dim)


def _three_terms(x):
    hi = _bf(x)
    r = x - hi.astype(F32)
    mid = _bf(r)
    return hi, mid, _bf(r - mid.astype(F32))


def _dot01(m, a):
    return sum(_dot(m, t) for t in _three_terms(a))


def _to_rows(col):
    k = col.shape[1]
    eye = (_iota2((k, k), 0) == _iota2((k, k), 1)).astype(BF16)
    return sum(_dot_nt(eye, t) for t in _three_terms(col))


def _to_cols(row):
    n = row.shape[1]
    eye = (_iota2((n, n), 0) == _iota2((n, n), 1)).astype(BF16)
    return sum(_dot_nt(eye, t) for t in _three_terms(row))


def _sigmoid(x):
    return jax.nn.sigmoid(x)


def _in_proj(x, w, pos, inv):
    L = x.shape[0]
    tm = ROW_TILE
    widths = (D_SSD, D_ATT, D_ATT, D_XBC, 2 * D_KV, DT_PAD)

    def body(x_ref, w_ref, pos_ref, inv_ref, z_ref, g_ref, q_ref, xbc_ref, kv_ref, dt_ref, xb_ref):
        xb = _bf(x_ref[...])
        xb_ref[...] = xb
        tabs = _rope_tables(pos_ref, inv_ref)
        q_ref[...] = _bf(_rope(_dot_nt(xb, w_ref[P_Q:P_Q + D_ATT, :]), tabs))
        kv_ref[:, 0:D_KV] = _bf(_rope(_dot_nt(xb, w_ref[P_KV:P_KV + D_KV, :]), tabs))
        kv_ref[:, D_KV:2 * D_KV] = _bf(_dot_nt(xb, w_ref[P_KV + D_KV:P_KV + 2 * D_KV, :]))
        for o_ref, off, wd in zip((z_ref, g_ref, xbc_ref, dt_ref), (P_Z, P_G, P_XBC, P_DT), (D_SSD, D_ATT, D_XBC, DT_PAD)):
            o_ref[...] = _dot_nt(xb, w_ref[off:off + wd, :])

    row = lambda wd: pl.BlockSpec((tm, wd), lambda i: (i, 0))
    return pl.pallas_call(
        body, name="in_proj", grid=(L // tm,),
        in_specs=[row(D_MODEL), pl.BlockSpec((P_END, D_MODEL), lambda i: (0, 0), pipeline_mode=pl.Buffered(1)), row(1),
                  pl.BlockSpec((1, 2 * ATT_HD), lambda i: (0, 0))],
        out_specs=[row(wd) for wd in widths] + [row(D_MODEL)],
        out_shape=[jax.ShapeDtypeStruct((L, wd), dt) for wd, dt in zip(widths, (F32, F32, BF16, F32, BF16, F32))]
        + [jax.ShapeDtypeStruct((L, D_MODEL), BF16)],
        compiler_params=_cparams(("parallel",)),
    )(x, w, pos, inv)


def _matmuls_tn(a_list, b, name):
    K, N = b.shape
    tk = min(K, 1024)
    n = len(a_list)

    def body(*refs):
        b_ref = refs[n]

        @pl.when(pl.program_id(0) == 0)
        def _():
            for o_ref in refs[n + 1:]:
                o_ref[...] = jnp.zeros_like(o_ref)

        bb = _bf(b_ref[...])
        for a_ref, o_ref in zip(refs[:n], refs[n + 1:]):
            o_ref[...] += _dot_tn(_bf(a_ref[...]), bb)

    return pl.pallas_call(
        body, name=name, grid=(K // tk,),
        in_specs=[pl.BlockSpec((tk, a.shape[1]), lambda k: (k, 0)) for a in a_list] + [pl.BlockSpec((tk, N), lambda k: (k, 0))],
        out_specs=[pl.BlockSpec((a.shape[1], N), lambda k: (0, 0)) for a in a_list],
        out_shape=[jax.ShapeDtypeStruct((a.shape[1], N), F32) for a in a_list],
        compiler_params=_cparams(("arbitrary",)),
    )(*a_list, b)


def _grad_x(dr, dz, dg, dq, dxbc, dkv, ddt, w, after, part, prev=None):
    L = dr.shape[0]
    tm = min(ROW_TILE, L // 4)
    first = L // (4 * tm)
    n = first if part == 0 else L // tm - first
    widths = (D_SSD, D_ATT, D_ATT, D_XBC, 2 * D_KV, DT_PAD)
    offs = (P_Z, P_G, P_Q, P_XBC, P_KV, P_DT)

    def body(dr_ref, dz_ref, dg_ref, dq_ref, dxbc_ref, dkv_ref, ddt_ref, w_ref, after_ref, *rest):
        o_ref = rest[-1]
        acc = ALPHA * dr_ref[...]
        for p_ref, off, wd in zip((dz_ref, dg_ref, dq_ref, dxbc_ref, dkv_ref, ddt_ref), offs, widths):
            acc = acc + _dot(_bf(p_ref[...]), w_ref[off:off + wd, :])
        o_ref[...] = acc

    row = lambda wd: pl.BlockSpec((tm, wd), lambda i: (i + part * first, 0))
    ins = [dr, dz, dg, dq, dxbc, dkv, ddt, w, after]
    specs = ([row(D_MODEL)] + [row(wd) for wd in widths]
             + [pl.BlockSpec((P_END, D_MODEL), lambda i: (0, 0), pipeline_mode=pl.Buffered(1)),
                pl.BlockSpec((8, 128), lambda i: (0, 0))])
    if prev is not None:
        ins.append(prev)
        specs.append(pl.BlockSpec(memory_space=pl.ANY))
    return pl.pallas_call(
        body, name="grad_x_%d" % part, grid=(n,),
        in_specs=specs, out_specs=row(D_MODEL),
        out_shape=jax.ShapeDtypeStruct((L, D_MODEL), F32),
        input_output_aliases={} if prev is None else {len(ins) - 1: 0},
        compiler_params=_cparams(("parallel",)),
    )(*ins)


HALO = 16


def _shift_matrix(offsets):
    n = CHUNK + HALO
    m = np.zeros((len(offsets) * CHUNK, 2 * n), np.float32)
    for k, off in enumerate(offsets):
        t = np.arange(CHUNK)
        m[k * CHUNK + t, t + off] = 1.0
        m[k * CHUNK + t, n + t + off] = 1.0
    return jnp.asarray(m, BF16)


def _shifted_rows(first_part, second_part, smat_ref):
    h1, l1 = _hi_lo(first_part)
    h2, l2 = _hi_lo(second_part)
    sh = _dot(smat_ref[...], jnp.concatenate([h1, h2, l1, l2], axis=0))
    return sh[0:CHUNK], sh[CHUNK:2 * CHUNK], sh[2 * CHUNK:3 * CHUNK]


def _ssd_chunk_pre(first, xbc_ref, tail_ref, dt_ref, cw_ref, cb_ref, dtb_ref, alog_ref, smat_ref=None, ext=None):
    tail = jnp.where(first, 0.0, tail_ref[...])
    x = xbc_ref[...]
    if ext is None:
        taps = _shifted_rows(tail, x, smat_ref) + (x,)
    else:
        ext[0:HALO, :] = tail
        ext[HALO:HALO + CHUNK, :] = x
        taps = tuple(ext[pl.ds(HALO - (CONV_K - 1) + k, CHUNK), :] for k in range(CONV_K - 1)) + (x,)
    u = cb_ref[...] + cw_ref[0:1, :] * taps[0]
    for k in range(1, CONV_K):
        u = u + cw_ref[k:k + 1, :] * taps[k]
    sig = _sigmoid(u)
    xbc = u * sig
    dtraw = dt_ref[:, 0:SSD_HEADS] + dtb_ref[...]
    dt = jax.nn.softplus(dtraw)
    A = -jnp.exp(alog_ref[...])
    a = dt * A
    tril = (_iota2((CHUNK, CHUNK), 0) >= _iota2((CHUNK, CHUNK), 1)).astype(BF16)
    acs = _dot01(tril, a)
    acs_row = _to_rows(acs)
    return u, sig, xbc, dtraw, dt, A, acs, acs_row, taps


def _head_expander():
    return (_iota2((SSD_HEADS, D_SSD), 1) // SSD_P == _iota2((SSD_HEADS, D_SSD), 0)).astype(BF16)


def _hi_lo(x):
    hi = _bf(x)
    return hi, _bf(x - hi.astype(F32))


def _expand(v, e):
    hi, lo = _hi_lo(v)
    return _dot(hi, e) + _dot(lo, e)


def _headsum(t, e):
    m = t.shape[0]
    if m < 8:
        t = jnp.broadcast_to(t[0:1], (8, t.shape[1]))
    hi, lo = _hi_lo(t)
    return (_dot_nt(hi, e) + _dot_nt(lo, e))[0:m]


def _ssd_decays(dt, acs, dsk_ref, e):
    alast = acs[CHUNK - 1:CHUNK, :]
    stk = jnp.concatenate([dt, jnp.exp(acs), jnp.exp(alast - acs),
                           jnp.broadcast_to(jnp.exp(alast), (8, SSD_HEADS)),
                           jnp.broadcast_to(dsk_ref[...], (8, SSD_HEADS))], axis=0)
    ex = _expand(stk, e)
    return (ex[0:CHUNK], ex[CHUNK:2 * CHUNK], ex[2 * CHUNK:3 * CHUNK], ex[3 * CHUNK:3 * CHUNK + 1],
            ex[3 * CHUNK + 8:3 * CHUNK + 9])


def _ssd_fwd(z, xbc, dtp, conv_w, conv_b, dt_bias, a_log, d_skip, norm_w):
    L = z.shape[0]
    nc = L // CHUNK
    half = D_SSD // SSD_GROUPS

    def body(z_ref, xbc_ref, tail_ref, dt_ref, cw_ref, cb_ref, dtb_ref, alog_ref, dsk_ref, nw_ref,
             y_ref, ypre_ref, prev_ref, state, ybuf, mbuf, ext):
        c = pl.program_id(0)

        @pl.when(c == 0)
        def _():
            state[...] = jnp.zeros_like(state)

        u, sig, xbcv, dtraw, dt, A, acs, acs_row, _ = _ssd_chunk_pre(
            c == 0, xbc_ref, tail_ref, dt_ref, cw_ref, cb_ref, dtb_ref, alog_ref, ext=ext)
        e = _head_expander()
        dtE, eacsE, dsdE, ealE, dskE = _ssd_decays(dt, acs, dsk_ref, e)
        xs = xbcv[:, 0:D_SSD]
        X = xs * dtE
        prev_ref[0] = state[...]
        causal = _iota2((CHUNK, CHUNK), 0) >= _iota2((CHUNK, CHUNK), 1)
        for g in range(SSD_GROUPS):
            gs = slice(half * g, half * (g + 1))
            Bg = _bf(xbcv[:, D_SSD + SSD_N * g:D_SSD + SSD_N * (g + 1)])
            Cg = _bf(xbcv[:, D_SSD + D_BC + SSD_N * g:D_SSD + D_BC + SSD_N * (g + 1)])
            cb = _dot_nt(Cg, Bg)
            for r in range(SSD_R):
                h = g * SSD_R + r
                seg = acs[:, h:h + 1] - acs_row[h:h + 1, :]
                mbuf[h] = _bf(cb * jnp.where(causal, jnp.exp(jnp.where(causal, seg, 0.0)), 0.0))
            st = state[:, gs]
            ybuf[:, gs] = _dot(Cg, _bf(st)) * eacsE[:, gs] + dskE[:, gs] * xs[:, gs]
            state[:, gs] = st * ealE[:, gs] + _dot_tn(Bg, _bf(X[:, gs] * dsdE[:, gs]))
        Xb = _bf(X)
        for h in range(SSD_HEADS):
            hs = slice(SSD_P * h, SSD_P * (h + 1))
            ybuf[:, hs] += _dot(mbuf[h], Xb[:, hs])
        y = ybuf[...]
        ypre_ref[...] = y
        zv = z_ref[...]
        yf = y * (zv * _sigmoid(zv))
        for g in range(SSD_GROUPS):
            gs = slice(half * g, half * (g + 1))
            yg = yf[:, gs]
            ms = jnp.mean(yg * yg, axis=-1, keepdims=True)
            y_ref[:, gs] = _bf(yg * lax.rsqrt(ms + RMS_EPS) * nw_ref[:, gs])

    full = lambda shape: pl.BlockSpec(shape, lambda c: (0, 0))
    return pl.pallas_call(
        body, name="ssd_fwd", grid=(nc,),
        in_specs=[
            pl.BlockSpec((CHUNK, D_SSD), lambda c: (c, 0)),
            pl.BlockSpec((CHUNK, D_XBC), lambda c: (c, 0)),
            pl.BlockSpec((HALO, D_XBC), lambda c: (jnp.maximum(c * (CHUNK // HALO) - 1, 0), 0)),
            pl.BlockSpec((CHUNK, DT_PAD), lambda c: (c, 0)),
            full((CONV_K, D_XBC)), full((1, D_XBC)), full((1, SSD_HEADS)), full((1, SSD_HEADS)), full((1, SSD_HEADS)),
            full((1, D_SSD)),
        ],
        out_specs=[
            pl.BlockSpec((CHUNK, D_SSD), lambda c: (c, 0)),
            pl.BlockSpec((CHUNK, D_SSD), lambda c: (c, 0)),
            pl.BlockSpec((1, SSD_N, D_SSD), lambda c: (c, 0, 0)),
        ],
        out_shape=[
            jax.ShapeDtypeStruct((L, D_SSD), BF16),
            jax.ShapeDtypeStruct((L, D_SSD), F32),
            jax.ShapeDtypeStruct((nc, SSD_N, D_SSD), F32),
        ],
        scratch_shapes=[
            pltpu.VMEM((SSD_N, D_SSD), F32),
            pltpu.VMEM((CHUNK, D_SSD), F32),
            pltpu.VMEM((SSD_HEADS, CHUNK, CHUNK), BF16),
            pltpu.VMEM((CHUNK + HALO, D_XBC), F32),
        ],
        compiler_params=_cparams(("arbitrary",)),
    )(z, xbc, xbc, dtp, conv_w, conv_b, dt_bias, a_log, d_skip, norm_w)


def _ssd_bwd(dy, z, ypre, xbc, dtp, prev, conv_w, conv_b, dt_bias, a_log, d_skip, norm_w):
    L = z.shape[0]
    nc = L // CHUNK
    half = D_SSD // SSD_GROUPS

    def body(dy_ref, z_ref, ypre_ref, xbc_ref, tail_ref, dt_ref, prev_ref, cw_ref, cb_ref, dtb_ref, alog_ref, dsk_ref,
             nw_ref, smat_ref, smat2_ref, dz_ref, dxbc_ref, ddt_ref, gcw_ref, gcb_ref, gdtb_ref, galog_ref, gdsk_ref,
             gnw_ref, dstate, dhead, dpost, yobuf, bdbuf, lmbuf, dmbuf, cbbuf):
        i = pl.program_id(0)
        c = nc - 1 - i

        @pl.when(i == 0)
        def _():
            dstate[...] = jnp.zeros_like(dstate)
            dhead[...] = jnp.zeros_like(dhead)
            gcw_ref[...] = jnp.zeros_like(gcw_ref)
            gcb_ref[...] = jnp.zeros_like(gcb_ref)
            gdtb_ref[...] = jnp.zeros_like(gdtb_ref)
            galog_ref[...] = jnp.zeros_like(galog_ref)
            gdsk_ref[...] = jnp.zeros_like(gdsk_ref)
            gnw_ref[...] = jnp.zeros_like(gnw_ref)

        u, sig, xbcv, dtraw, dt, A, acs, acs_row, taps = _ssd_chunk_pre(
            c == 0, xbc_ref, tail_ref, dt_ref, cw_ref, cb_ref, dtb_ref, alog_ref, smat_ref)
        e = _head_expander()
        dtE, eacsE, dsdE, ealE, dskE = _ssd_decays(dt, acs, dsk_ref, e)
        alast = acs[CHUNK - 1:CHUNK, :]
        xs = xbcv[:, 0:D_SSD]
        X = xs * dtE
        Xb = _bf(X)

        zv = z_ref[...]
        ypre = ypre_ref[...]
        dyn = dy_ref[...]
        sz = _sigmoid(zv)
        silu_z = zv * sz
        yf = ypre * silu_z
        dyf_parts = []
        for g in range(SSD_GROUPS):
            gs = slice(half * g, half * (g + 1))
            yg = yf[:, gs]
            rstd = lax.rsqrt(jnp.mean(yg * yg, axis=-1, keepdims=True) + RMS_EPS)
            dout = dyn[:, gs]
            gnw_ref[:, gs] += jnp.sum(dout * yg * rstd, axis=0, keepdims=True)
            dyhat = dout * nw_ref[:, gs]
            dyf_parts.append(rstd * (dyhat - yg * (rstd * rstd) * jnp.mean(dyhat * yg, axis=-1, keepdims=True)))
        dyf = jnp.concatenate(dyf_parts, axis=1)
        dz_ref[...] = _bf(dyf * ypre * (sz * (1.0 + zv * (1.0 - sz))))
        dyp = dyf * silu_z
        dyb = _bf(dyp)
        G = dyp * eacsE

        causal = _iota2((CHUNK, CHUNK), 0) >= _iota2((CHUNK, CHUNK), 1)
        ST = prev_ref[0]
        dST = dstate[...]
        for g in range(SSD_GROUPS):
            gs = slice(half * g, half * (g + 1))
            bs = slice(D_SSD + SSD_N * g, D_SSD + SSD_N * (g + 1))
            cs = slice(D_SSD + D_BC + SSD_N * g, D_SSD + D_BC + SSD_N * (g + 1))
            Bg = _bf(xbcv[:, bs])
            Cg = _bf(xbcv[:, cs])
            Gb = _bf(G[:, gs])
            STb = _bf(ST[:, gs])
            dSTb = _bf(dST[:, gs])
            dstate[:, gs] = dST[:, gs] * ealE[:, gs] + _dot_tn(Cg, Gb)
            yobuf[:, gs] = _dot(Cg, STb) * eacsE[:, gs]
            bdbuf[:, gs] = _dot(Bg, dSTb)
            dpost[:, cs] = _dot_nt(Gb, STb)
            dpost[:, bs] = _dot_nt(_bf(X[:, gs] * dsdE[:, gs]), dSTb)
            cbbuf[g] = _dot_nt(Cg, Bg)
            for r in range(SSD_R):
                h = g * SSD_R + r
                seg = acs[:, h:h + 1] - acs_row[h:h + 1, :]
                lmbuf[h] = jnp.where(causal, jnp.exp(jnp.where(causal, seg, 0.0)), 0.0)
        for h in range(SSD_HEADS):
            hs = slice(SSD_P * h, SSD_P * (h + 1))
            Mb = _bf(cbbuf[h // SSD_R] * lmbuf[h])
            dmbuf[h] = _dot_nt(dyb[:, hs], Xb[:, hs])
            dpost[:, hs] = _dot_tn(Mb, dyb[:, hs])
        lane16 = _iota2((1, SSD_HEADS), 1)
        sub16 = _iota2((SSD_HEADS, 1), 0)
        dacs_col = jnp.zeros((CHUNK, SSD_HEADS), F32)
        dacs_row = jnp.zeros((SSD_HEADS, CHUNK), F32)
        for g in range(SSD_GROUPS):
            bs = slice(D_SSD + SSD_N * g, D_SSD + SSD_N * (g + 1))
            cs = slice(D_SSD + D_BC + SSD_N * g, D_SSD + D_BC + SSD_N * (g + 1))
            cb = cbbuf[g]
            dcb = jnp.zeros((CHUNK, CHUNK), F32)
            for r in range(SSD_R):
                h = g * SSD_R + r
                dM = dmbuf[h]
                Lm = lmbuf[h]
                dcb = dcb + dM * Lm
                dseg = dM * (cb * Lm)
                dacs_col = dacs_col + jnp.sum(dseg, axis=-1, keepdims=True) * (lane16 == h).astype(F32)
                dacs_row = dacs_row - jnp.sum(dseg, axis=0, keepdims=True) * (sub16 == h).astype(F32)
            dcbb = _bf(dcb)
            dpost[:, bs] += _dot_tn(dcbb, _bf(xbcv[:, cs]))
            dpost[:, cs] += _dot(dcbb, _bf(xbcv[:, bs]))

        BD = bdbuf[...]
        dX = dpost[:, 0:D_SSD] + dsdE * BD
        dsd = jnp.exp(alast - acs)
        T = _headsum(X * BD, e) * dsd
        dalast = jnp.sum(T, axis=0, keepdims=True) + _headsum(
            jnp.sum(dST * ST, axis=0, keepdims=True), e) * jnp.exp(alast)
        is_last = (_iota2((CHUNK, 1), 0) == CHUNK - 1).astype(F32)
        dacs = dacs_col + _to_cols(dacs_row) + _headsum(dyp * yobuf[...], e) - T + is_last * dalast
        triu = (_iota2((CHUNK, CHUNK), 0) <= _iota2((CHUNK, CHUNK), 1)).astype(BF16)
        da = _dot01(triu, dacs)
        ddt_tot = _headsum(dX * xs, e) + da * A
        galog_ref[...] += jnp.sum(da * dt, axis=0, keepdims=True) * A
        ddtraw = ddt_tot * _sigmoid(dtraw)
        gdtb_ref[...] += jnp.sum(ddtraw, axis=0, keepdims=True)
        gdsk_ref[...] += _headsum(jnp.sum(dyp * xs, axis=0, keepdims=True), e)
        ddt_ref[...] = jnp.zeros_like(ddt_ref)
        ddt_ref[:, 0:SSD_HEADS] = ddtraw
        dpost[:, 0:D_SSD] = dX * dtE + dskE * dyp

        dconv = dpost[...] * (sig * (1.0 + u * (1.0 - sig)))
        gcb_ref[...] += jnp.sum(dconv, axis=0, keepdims=True)
        for k in range(CONV_K):
            gcw_ref[k:k + 1, :] += jnp.sum(dconv * taps[k], axis=0, keepdims=True)
        later = _shifted_rows(dconv, dhead[...], smat2_ref)
        dx = cw_ref[CONV_K - 1:CONV_K, :] * dconv
        for k in range(CONV_K - 1):
            dx = dx + cw_ref[k:k + 1, :] * later[k]
        dxbc_ref[...] = _bf(dx)
        dhead[...] = dconv[0:HALO, :]

    full = lambda shape: pl.BlockSpec(shape, lambda i: (0, 0))
    rev = lambda wd: pl.BlockSpec((CHUNK, wd), lambda i: (nc - 1 - i, 0))
    return pl.pallas_call(
        body, name="ssd_bwd", grid=(nc,),
        in_specs=[
            rev(D_SSD), rev(D_SSD), rev(D_SSD), rev(D_XBC),
            pl.BlockSpec((HALO, D_XBC), lambda i: (jnp.maximum((nc - 1 - i) * (CHUNK // HALO) - 1, 0), 0)),
            rev(DT_PAD),
            pl.BlockSpec((1, SSD_N, D_SSD), lambda i: (nc - 1 - i, 0, 0)),
            full((CONV_K, D_XBC)), full((1, D_XBC)), full((1, SSD_HEADS)), full((1, SSD_HEADS)), full((1, SSD_HEADS)),
            full((1, D_SSD)), full((3 * CHUNK, 2 * (CHUNK + HALO))), full((3 * CHUNK, 2 * (CHUNK + HALO))),
        ],
        out_specs=[
            rev(D_SSD), rev(D_XBC), rev(DT_PAD),
            full((CONV_K, D_XBC)), full((1, D_XBC)), full((1, SSD_HEADS)), full((1, SSD_HEADS)), full((1, SSD_HEADS)),
            full((1, D_SSD)),
        ],
        out_shape=[
            jax.ShapeDtypeStruct((L, D_SSD), BF16), jax.ShapeDtypeStruct((L, D_XBC), BF16),
            jax.ShapeDtypeStruct((L, DT_PAD), F32),
            jax.ShapeDtypeStruct((CONV_K, D_XBC), F32), jax.ShapeDtypeStruct((1, D_XBC), F32),
            jax.ShapeDtypeStruct((1, SSD_HEADS), F32), jax.ShapeDtypeStruct((1, SSD_HEADS), F32),
            jax.ShapeDtypeStruct((1, SSD_HEADS), F32), jax.ShapeDtypeStruct((1, D_SSD), F32),
        ],
        scratch_shapes=[
            pltpu.VMEM((SSD_N, D_SSD), F32),
            pltpu.VMEM((HALO, D_XBC), F32),
            pltpu.VMEM((CHUNK, D_XBC), F32),
            pltpu.VMEM((CHUNK, D_SSD), F32),
            pltpu.VMEM((CHUNK, D_SSD), F32),
            pltpu.VMEM((SSD_HEADS, CHUNK, CHUNK), F32),
            pltpu.VMEM((SSD_HEADS, CHUNK, CHUNK), F32),
            pltpu.VMEM((SSD_GROUPS, CHUNK, CHUNK), F32),
        ],
        compiler_params=_cparams(("arbitrary",)),
    )(dy, z, ypre, xbc, xbc, dtp, prev, conv_w, conv_b, dt_bias, a_log, d_skip, norm_w, _shift_matrix((13, 14, 15)),
      _shift_matrix((3, 2, 1)))


def _rope_tables(pos_ref, inv_ref):
    ang = pos_ref[...].astype(F32) * inv_ref[...]
    d = _iota2((1, 2 * ATT_HD), 1) % ATT_HD
    s = jnp.sin(ang)
    return jnp.cos(ang), jnp.where(d < ROPE_DIM // 2, -s, 0.0), jnp.where((d >= ROPE_DIM // 2) & (d < ROPE_DIM), s, 0.0)


def _rope(t, tabs):
    c, s1, s2 = tabs
    n = t.shape[1]
    rep = n // c.shape[1]
    return (t * jnp.tile(c, (1, rep)) + pltpu.roll(t, n - ROPE_DIM // 2, 1) * jnp.tile(s1, (1, rep))
            + pltpu.roll(t, ROPE_DIM // 2, 1) * jnp.tile(s2, (1, rep)))


def _rope_t(t, tabs):
    c, s1, s2 = tabs
    n = t.shape[1]
    rep = n // c.shape[1]
    return (t * jnp.tile(c, (1, rep)) + pltpu.roll(t * jnp.tile(s1, (1, rep)), ROPE_DIM // 2, 1)
            + pltpu.roll(t * jnp.tile(s2, (1, rep)), n - ROPE_DIM // 2, 1))


def _stack_heads(t, j):
    return jnp.concatenate([t[:, ATT_HD * (j * ATT_R + r):ATT_HD * (j * ATT_R + r + 1)] for r in range(ATT_R)], axis=0)


def _swa_mask_t(first):
    si = _iota2((2 * WINDOW, ATT_R * WINDOW), 0)
    qi = _iota2((2 * WINDOW, ATT_R * WINDOW), 1) % WINDOW
    band = (si > qi) & (si <= qi + WINDOW)
    return band & (jnp.logical_not(first) | (si >= WINDOW))


def _head_rows(ref, j):
    if ref.shape[0] == 1:
        parts = [jnp.broadcast_to(ref[:, j * ATT_R + r:j * ATT_R + r + 1], (1, WINDOW)) for r in range(ATT_R)]
    else:
        parts = [ref[j * ATT_R + r:j * ATT_R + r + 1, :] for r in range(ATT_R)]
    return jnp.concatenate(parts, axis=1)


def _swa_fwd(q, g, kv, sinks):
    L = q.shape[0]
    nb = L // WINDOW
    scale = ATT_HD ** -0.5

    def body(q_ref, g_ref, kvc_ref, kvp_ref, sink_ref, y_ref, o_ref, lse_ref, otbuf):
        n = pl.program_id(0)
        kk = jnp.concatenate([kvp_ref[:, 0:D_KV], kvc_ref[:, 0:D_KV]], axis=0) * scale
        vv = jnp.concatenate([kvp_ref[:, D_KV:2 * D_KV], kvc_ref[:, D_KV:2 * D_KV]], axis=0)
        valid = _swa_mask_t(n == 0)
        qv = q_ref[...]
        for j in range(ATT_KVH):
            js = slice(ATT_HD * j, ATT_HD * (j + 1))
            st = _dot_nt(kk[:, js], _stack_heads(qv, j))
            st = jnp.where(valid, st, NEG_BIG)
            sink = _head_rows(sink_ref, j)
            m = jnp.maximum(jnp.max(st, axis=0, keepdims=True), sink)
            p = jnp.exp(st - m)
            vx = jnp.concatenate([vv[:, js], jnp.ones((2 * WINDOW, ATT_HD), BF16)], axis=1)
            otx = _dot_tn(vx, _bf(p))
            denom = otx[ATT_HD:ATT_HD + 1] + jnp.exp(sink - m)
            ot = otx[0:ATT_HD] * (1.0 / denom)
            lse = m + jnp.log(denom)
            for r in range(ATT_R):
                h = j * ATT_R + r
                otbuf[ATT_HD * h:ATT_HD * (h + 1), :] = ot[:, WINDOW * r:WINDOW * (r + 1)]
                lse_ref[h:h + 1, :] = lse[:, WINDOW * r:WINDOW * (r + 1)]
        o = otbuf[...].T
        o_ref[...] = o
        gv = g_ref[...]
        y_ref[...] = _bf(o * (gv * _sigmoid(gv)))

    cur = lambda wd: pl.BlockSpec((WINDOW, wd), lambda n: (n, 0))
    prv = lambda wd: pl.BlockSpec((WINDOW, wd), lambda n: (jnp.maximum(n - 1, 0), 0))
    return pl.pallas_call(
        body, name="swa_fwd", grid=(nb,),
        in_specs=[cur(D_ATT), cur(D_ATT), cur(2 * D_KV), prv(2 * D_KV), pl.BlockSpec((1, ATT_QH), lambda n: (0, 0))],
        out_specs=[cur(D_ATT), cur(D_ATT), pl.BlockSpec((ATT_QH, WINDOW), lambda n: (0, n))],
        out_shape=[jax.ShapeDtypeStruct((L, D_ATT), BF16), jax.ShapeDtypeStruct((L, D_ATT), F32),
                   jax.ShapeDtypeStruct((ATT_QH, L), F32)],
        scratch_shapes=[pltpu.VMEM((D_ATT, WINDOW), F32)],
        compiler_params=_cparams(("parallel",)),
    )(q, g, kv, kv, sinks)


def _swa_bwd(dy, q, g, kv, o, lse, pos, inv, sinks):
    L = q.shape[0]
    nb = L // WINDOW
    scale = ATT_HD ** -0.5

    def body(dy_ref, q_ref, g_ref, kvc_ref, kvp_ref, o_ref, lse_ref, posc_ref, posp_ref, inv_ref, sink_ref,
             dq_ref, dg_ref, dkv_ref, dsink_ref, carry, dqbuf, dkbuf, dvbuf):
        n = pl.program_id(0)

        @pl.when(n == 0)
        def _():
            dsink_ref[...] = jnp.zeros_like(dsink_ref)

        @pl.when(n < nb)
        def _():
            tc = _rope_tables(posc_ref, inv_ref)
            tp = _rope_tables(posp_ref, inv_ref)
            kk = jnp.concatenate([kvp_ref[:, 0:D_KV], kvc_ref[:, 0:D_KV]], axis=0) * scale
            vv = jnp.concatenate([kvp_ref[:, D_KV:2 * D_KV], kvc_ref[:, D_KV:2 * D_KV]], axis=0)
            valid = _swa_mask_t(n == 0)
            qv = q_ref[...]
            gv = g_ref[...]
            sg = _sigmoid(gv)
            dyv = dy_ref[...]
            ov = o_ref[...]
            dg_ref[...] = _bf(dyv * ov * (sg * (1.0 + gv * (1.0 - sg))))
            do = dyv * (gv * sg)
            dod = do * ov
            ones = jnp.ones((8, ATT_HD), BF16)
            lane16 = _iota2((1, ATT_QH), 1)
            dsink = jnp.zeros((1, ATT_QH), F32)
            for j in range(ATT_KVH):
                js = slice(ATT_HD * j, ATT_HD * (j + 1))
                kj = kk[:, js]
                vj = vv[:, js]
                qs = _stack_heads(qv, j)
                dos = _bf(_stack_heads(do, j))
                hi, lo = _hi_lo(_stack_heads(dod, j))
                delta = (_dot_nt(ones, hi) + _dot_nt(ones, lo))[0:1]
                lse = _head_rows(lse_ref, j)
                st = _dot_nt(kj, qs)
                pt = jnp.exp(jnp.where(valid, st, NEG_BIG) - lse)
                dst = _bf(pt * (_dot_nt(vj, dos) - delta))
                dqt = _dot_tn(kj, dst)
                dkbuf[:, js] = _dot(dst, qs) * scale
                dvbuf[:, js] = _dot(_bf(pt), dos)
                sd = jnp.exp(_head_rows(sink_ref, j) - lse) * delta
                for r in range(ATT_R):
                    h = j * ATT_R + r
                    ls = slice(WINDOW * r, WINDOW * (r + 1))
                    dqbuf[ATT_HD * h:ATT_HD * (h + 1), :] = dqt[:, ls]
                    dsink = dsink - jnp.sum(sd[:, ls], axis=1, keepdims=True) * (lane16 == h).astype(F32)
            dsink_ref[...] += dsink
            dq_ref[...] = _bf(_rope_t(dqbuf[...].T, tc))
            dkp = _rope_t(dkbuf[0:WINDOW, :], tp)
            dkc = _rope_t(dkbuf[WINDOW:2 * WINDOW, :], tc)

            @pl.when(n > 0)
            def _():
                dkv_ref[:, 0:D_KV] = _bf(carry[:, 0:D_KV] + dkp)
                dkv_ref[:, D_KV:2 * D_KV] = _bf(carry[:, D_KV:2 * D_KV] + dvbuf[0:WINDOW, :])

            carry[:, 0:D_KV] = dkc
            carry[:, D_KV:2 * D_KV] = dvbuf[WINDOW:2 * WINDOW, :]

        @pl.when(n == nb)
        def _():
            dkv_ref[...] = _bf(carry[...])

    last = nb - 1
    cur = lambda wd: pl.BlockSpec((WINDOW, wd), lambda n: (jnp.minimum(n, last), 0))
    prv = lambda wd: pl.BlockSpec((WINDOW, wd), lambda n: (jnp.maximum(jnp.minimum(n, last) - 1, 0), 0))
    return pl.pallas_call(
        body, name="swa_bwd", grid=(nb + 1,),
        in_specs=[cur(D_ATT), cur(D_ATT), cur(D_ATT), cur(2 * D_KV), prv(2 * D_KV), cur(D_ATT),
                  pl.BlockSpec((ATT_QH, WINDOW), lambda n: (0, jnp.minimum(n, last))), cur(1), prv(1),
                  pl.BlockSpec((1, 2 * ATT_HD), lambda n: (0, 0)), pl.BlockSpec((1, ATT_QH), lambda n: (0, 0))],
        out_specs=[cur(D_ATT), cur(D_ATT),
                   pl.BlockSpec((WINDOW, 2 * D_KV), lambda n: (jnp.maximum(n - 1, 0), 0)),
                   pl.BlockSpec((1, ATT_QH), lambda n: (0, 0))],
        out_shape=[jax.ShapeDtypeStruct((L, D_ATT), BF16), jax.ShapeDtypeStruct((L, D_ATT), BF16),
                   jax.ShapeDtypeStruct((L, 2 * D_KV), BF16), jax.ShapeDtypeStruct((1, ATT_QH), F32)],
        scratch_shapes=[pltpu.VMEM((WINDOW, 2 * D_KV), F32), pltpu.VMEM((D_ATT, WINDOW), F32),
                        pltpu.VMEM((2 * WINDOW, D_KV), F32), pltpu.VMEM((2 * WINDOW, D_KV), F32)],
        compiler_params=_cparams(("arbitrary",)),
    )(dy, q, g, kv, kv, o, lse, pos, pos, inv, sinks)


def _out_ln_loss(y_ssd, y_att, x, target, w_out, ln_g, ln_b):
    L = x.shape[0]
    tm = min(ROW_TILE, L)
    nt = L // tm
    inv_d = 1.0 / D_MODEL

    def body(ys_ref, ya_ref, x_ref, t_ref, w_ref, g_ref, b_ref, dr_ref, dys_ref, dya_ref, loss_ref, gg_ref, gb_ref,
             gwo_ref, acc_ref):
        i = pl.program_id(0)

        @pl.when(i == 0)
        def _():
            loss_ref[...] = jnp.zeros_like(loss_ref)
            gg_ref[...] = jnp.zeros_like(gg_ref)
            gb_ref[...] = jnp.zeros_like(gb_ref)
            acc_ref[...] = jnp.zeros_like(acc_ref)

        halves = [slice(0, tm // 2), slice(tm // 2, tm)]
        hs = [_dot(_bf(ys_ref[rs, :]), w_ref[0:D_SSD, :]) + _dot(_bf(ya_ref[rs, :]), w_ref[D_SSD:D_MIX, :]) for rs in halves]
        gam = g_ref[...]
        for rs, h in zip(halves, hs):
            r = ALPHA * x_ref[rs, :] + h
            mu = jnp.mean(r, axis=-1, keepdims=True)
            xc = r - mu
            rstd = lax.rsqrt(jnp.mean(xc * xc, axis=-1, keepdims=True) + LN_EPS)
            xhat = xc * rstd
            diff = xhat * gam + b_ref[...] - t_ref[rs, :]
            part = jnp.sum(jnp.sum(diff * diff, axis=-1, keepdims=True), axis=0, keepdims=True)
            loss_ref[...] += (0.5 * inv_d) * part
            dout = diff * inv_d
            gg_ref[...] += jnp.sum(dout * xhat, axis=0, keepdims=True)
            gb_ref[...] += jnp.sum(dout, axis=0, keepdims=True)
            dxh = dout * gam
            dr_ref[rs, :] = rstd * (dxh - jnp.mean(dxh, axis=-1, keepdims=True)
                                    - xhat * jnp.mean(dxh * xhat, axis=-1, keepdims=True))
        for rs in halves:
            drh = _bf(dr_ref[rs, :])
            dys_ref[rs, :] = _dot_nt(drh, w_ref[0:D_SSD, :])
            dya_ref[rs, :] = _dot_nt(drh, w_ref[D_SSD:D_MIX, :])
        drb = _bf(dr_ref[...])
        acc_ref[0:D_SSD, :] += _dot_tn(_bf(ys_ref[...]), drb)
        acc_ref[D_SSD:D_MIX, :] += _dot_tn(_bf(ya_ref[...]), drb)

        @pl.when(i == nt - 1)
        def _():
            gwo_ref[...] = _bf(acc_ref[...])

    row = pl.BlockSpec((tm, D_MODEL), lambda i: (i, 0))
    vec = pl.BlockSpec((1, D_MODEL), lambda i: (0, 0))
    return pl.pallas_call(
        body, name="out_ln_loss", grid=(nt,),
        in_specs=[row, row, row, row, pl.BlockSpec((D_MIX, D_MODEL), lambda i: (0, 0), pipeline_mode=pl.Buffered(1)), vec, vec],
        out_specs=[row, row, row, pl.BlockSpec((1, 128), lambda i: (0, 0)), vec, vec,
                   pl.BlockSpec((D_MIX, D_MODEL), lambda i: (0, 0))],
        out_shape=[jax.ShapeDtypeStruct((L, D_MODEL), F32)] * 3 + [jax.ShapeDtypeStruct((1, 128), F32)]
        + [jax.ShapeDtypeStruct((1, D_MODEL), F32)] * 2 + [jax.ShapeDtypeStruct((D_MIX, D_MODEL), BF16)],
        scratch_shapes=[pltpu.VMEM((D_MIX, D_MODEL), F32)],
        compiler_params=_cparams(("arbitrary",)),
    )(y_ssd, y_att, x, target, w_out, ln_g, ln_b)


def _local_step(x, pos, target, w, get_w_out, token, conv_w, conv_b, dt_bias, a_log, d_skip, norm_w, sinks, ln_g, ln_b):
    inv8 = ROPE_THETA ** (-jnp.arange(0, ROPE_DIM, 2, dtype=F32) / ROPE_DIM)
    inv = jnp.tile(jnp.concatenate([inv8, inv8, jnp.zeros((ATT_HD - ROPE_DIM,), F32)]), 2).reshape(1, 2 * ATT_HD)
    inv = inv + token

    z, g, q, xbc, kv, dtp, xb = _in_proj(x, w, pos, inv)
    y_ssd, y_pre, prev = _ssd_fwd(z, xbc, dtp, conv_w, conv_b, dt_bias, a_log, d_skip, norm_w)
    y_att, o, lse = _swa_fwd(q, g, kv, sinks)
    w_out = get_w_out(lse)
    dr, dy_ssd, dy_att, loss, g_ln_g, g_ln_b, gw_out = _out_ln_loss(y_ssd, y_att, x, target, w_out, ln_g, ln_b)
    w_out_red = _reduce_w_out_start(gw_out.reshape(N_CHIPS, W_OUT_ROWS, D_MODEL), loss)
    inv = inv + w_out_red[16][0:1, :]
    dq, dg, dkv, g_sinks = _swa_bwd(dy_att, q, g, kv, o, lse, pos, inv, sinks)
    dz, dxbc, ddt, g_conv_w, g_conv_b, g_dt_bias, g_a_log, g_d_skip, g_norm_w = _ssd_bwd(
        dy_ssd, z, y_pre, xbc, dtp, prev, conv_w, conv_b, dt_bias, a_log, d_skip, norm_w)
    gw_z, gw_g, gw_q = _matmuls_tn([dz, dg, dq], xb, "gw_zgq")
    gw_xbc, gw_kv, gw_dt = _matmuls_tn([dxbc, dkv, ddt], xb, "gw_xbc_kv_dt")
    gw_in = jnp.concatenate([gw_z, gw_xbc, gw_dt[0:SSD_HEADS], gw_q, gw_kv, gw_g], axis=0)
    small = dict(conv_w=g_conv_w, conv_b=g_conv_b, dt_bias=g_dt_bias, a_log=g_a_log, d_skip=g_d_skip,
                 ssd_norm_w=g_norm_w, attn_sinks=g_sinks, ln_g=g_ln_g, ln_b=g_ln_b)
    return loss, (dr, dz, dg, dq, dxbc, dkv, ddt, w), gw_in, w_out_red, small


def _mesh_pos():
    return lax.axis_index("x"), lax.axis_index("y"), lax.axis_index("c")


def _gather_weights(w_in_s, conv_w_s):
    hr = w_in_s.shape[0] // 2
    qa = 336
    quarters = ((0, qa), (qa, hr - qa))

    def body(win_ref, cw_ref, owin_ref, ocw_ref, stage, send_sems, recv_sems, small_send, small_recv, local_sems):
        x, y, c = _mesh_pos()
        me = 2 * x + y
        sibling = (x, y, 1 - c)
        xn, yn, dg = (1 - x, y), (x, 1 - y), (1 - x, 1 - y)
        chips = [xn, yn, dg]
        load = pltpu.make_async_copy(win_ref, stage, local_sems.at[1])
        load.start()
        locals_ = [pltpu.make_async_copy(cw_ref, ocw_ref.at[me], local_sems.at[0])]
        for cp in locals_:
            cp.start()
        started = []

        def piece(ref, chip, half, q):
            off, n = quarters[q]
            return ref.at[2 * chip[0] + chip[1]].at[pl.ds(half * hr + off, n), :]

        def mine(q):
            off, n = quarters[q]
            return win_ref.at[pl.ds(c * hr + off, n), :]

        def copy(src, dst, k, to):
            return pltpu.make_async_remote_copy(src_ref=src, dst_ref=dst, send_sem=send_sems.at[k], recv_sem=recv_sems.at[k],
                                                device_id=to, device_id_type=MESH)

        def go(cp):
            cp.start()
            started.append(cp)

        go(copy(mine(0), piece(owin_ref, (x, y), c, 0), 0, (*xn, c)))
        go(copy(mine(1), piece(owin_ref, (x, y), c, 1), 2, (*yn, c)))
        go(copy(mine(1), piece(owin_ref, (x, y), c, 1), 1, (*xn, c)))
        go(copy(mine(0), piece(owin_ref, (x, y), c, 0), 3, (*yn, c)))
        for j, (px, py) in enumerate(chips):
            cp = pltpu.make_async_remote_copy(
                src_ref=cw_ref, dst_ref=ocw_ref.at[me], send_sem=small_send.at[j], recv_sem=small_recv.at[j],
                device_id=(px, py, c), device_id_type=MESH)
            go(cp)
        load.wait()
        store = pltpu.make_async_copy(stage, owin_ref.at[me], local_sems.at[2])
        store.start()
        locals_.append(store)
        arrivals = [(0, xn, 0, (4, (*yn, c))), (2, yn, 1, (5, (*xn, c))), (1, xn, 1, None), (3, yn, 0, None),
                    (4, dg, 0, None), (5, dg, 1, None)]
        for n, (k, chip, q, onward) in enumerate(arrivals):
            blk = piece(owin_ref, chip, c, q)
            copy(blk, blk, k, sibling).wait_recv()
            if onward is not None:
                go(copy(blk, blk, onward[0], onward[1]))
            go(copy(blk, blk, 6 + n, sibling))
        for n, (k, chip, q, onward) in enumerate(arrivals):
            blk = piece(owin_ref, chip, 1 - c, q)
            copy(blk, blk, 6 + n, sibling).wait_recv()
        for j in range(3):
            pltpu.make_async_remote_copy(
                src_ref=cw_ref, dst_ref=ocw_ref.at[me], send_sem=small_send.at[j], recv_sem=small_recv.at[j],
                device_id=sibling, device_id_type=MESH).wait_recv()
        for cp in started:
            cp.wait_send()
        for cp in locals_:
            cp.wait()

    any_spec = pl.BlockSpec(memory_space=pl.ANY)
    return pl.pallas_call(
        body, name="gather_weights",
        in_specs=[any_spec] * 2, out_specs=[any_spec] * 2,
        out_shape=[jax.ShapeDtypeStruct((N_CHIPS,) + a.shape, a.dtype) for a in (w_in_s, conv_w_s)],
        scratch_shapes=[pltpu.VMEM(w_in_s.shape, w_in_s.dtype),
                        pltpu.SemaphoreType.DMA((12,)), pltpu.SemaphoreType.DMA((12,)),
                        pltpu.SemaphoreType.DMA((3,)), pltpu.SemaphoreType.DMA((3,)), pltpu.SemaphoreType.DMA((3,))],
    )(w_in_s, conv_w_s)


_HBM = pl.BlockSpec(memory_space=pltpu.HBM)
_SEM = pl.BlockSpec(memory_space=pltpu.SEMAPHORE)
_EFFECT = pltpu.SideEffectType.DATAFLOW_SIDE_EFFECTING


def _gather_w_out_start(w_out_s, after):
    def body(src_ref, land_ref, after_ref, s0, s1, s2, r0, r1, r2, src_thru, land_thru, token):
        x, y, c = _mesh_pos()
        me = 2 * x + y
        chips = [(1 - x, y), (x, 1 - y), (1 - x, 1 - y)]
        for (px, py), s, r in zip(chips, (s0, s1, s2), (r0, r1, r2)):
            pltpu.make_async_remote_copy(src_ref=src_ref, dst_ref=land_ref.at[me], send_sem=s, recv_sem=r,
                                         device_id=(px, py, c), device_id_type=MESH).start()
        token[...] = jnp.zeros_like(token)

    sem = pltpu.SemaphoreType.DMA(())
    land = lax.empty((N_CHIPS,) + w_out_s.shape, w_out_s.dtype)
    return pl.pallas_call(
        body, name="gather_w_out_start",
        out_shape=(sem,) * 6 + (pltpu.HBM(w_out_s.shape, w_out_s.dtype), pltpu.HBM(land.shape, land.dtype),
                                jax.ShapeDtypeStruct((8, 128), F32)),
        in_specs=(_HBM, _HBM, pl.BlockSpec(memory_space=pl.ANY)),
        out_specs=(_SEM,) * 6 + (_HBM, _HBM, pl.BlockSpec(memory_space=pltpu.VMEM)),
        input_output_aliases={0: 6, 1: 7},
        compiler_params=pltpu.CompilerParams(has_side_effects=_EFFECT),
    )(pltpu.with_memory_space_constraint(w_out_s, pltpu.HBM), pltpu.with_memory_space_constraint(land, pltpu.HBM), after)


def _gather_w_out_wait(sems, src_thru, land_thru, after):
    def body(src_ref, land_ref, s0, s1, s2, r0, r1, r2, after_ref, src_dead, got_ref):
        x, y, c = _mesh_pos()
        chips = [(1 - x, y), (x, 1 - y), (1 - x, 1 - y)]
        for (px, py), s, r in zip(chips, (s0, s1, s2), (r0, r1, r2)):
            cp = pltpu.make_async_remote_copy(src_ref=src_ref, dst_ref=land_ref.at[2 * px + py], send_sem=s, recv_sem=r,
                                              device_id=(px, py, c), device_id_type=MESH)
            cp.wait_send()
            cp.wait_recv()

    return pl.pallas_call(
        body, name="gather_w_out_wait",
        out_shape=(pltpu.HBM(src_thru.shape, src_thru.dtype), pltpu.HBM(land_thru.shape, land_thru.dtype)),
        in_specs=(_HBM, _HBM) + (_SEM,) * 6 + (pl.BlockSpec(memory_space=pl.ANY),),
        out_specs=(_HBM, _HBM), input_output_aliases={0: 0, 1: 1},
        compiler_params=pltpu.CompilerParams(has_side_effects=_EFFECT),
    )(src_thru, land_thru, *sems, after)[1]


def _pair_start(gw_in, after):
    hr = gw_in.shape[1] // 2

    def body(src_ref, land_ref, after_ref, *refs):
        x, y, c = _mesh_pos()
        for j in range(N_CHIPS):
            pltpu.make_async_remote_copy(
                src_ref=src_ref.at[j, pl.ds((1 - c) * hr, hr), :], dst_ref=land_ref.at[j], send_sem=refs[j],
                recv_sem=refs[N_CHIPS + j], device_id=(x, y, 1 - c), device_id_type=MESH).start()
        refs[10][...] = jnp.zeros_like(refs[10])

    sem = pltpu.SemaphoreType.DMA(())
    land = lax.empty((N_CHIPS, hr, D_MODEL), F32)
    return pl.pallas_call(
        body, name="pair_start",
        out_shape=(sem,) * 8 + (pltpu.HBM(gw_in.shape, F32), pltpu.HBM(land.shape, F32), jax.ShapeDtypeStruct((8, 128), F32)),
        in_specs=(_HBM, _HBM, pl.BlockSpec(memory_space=pl.ANY)),
        out_specs=(_SEM,) * 8 + (_HBM, _HBM, pl.BlockSpec(memory_space=pltpu.VMEM)),
        input_output_aliases={0: 8, 1: 9},
        compiler_params=pltpu.CompilerParams(has_side_effects=_EFFECT),
    )(pltpu.with_memory_space_constraint(gw_in, pltpu.HBM), pltpu.with_memory_space_constraint(land, pltpu.HBM), after)


def _pair_wait(sems, gw_thru, land_thru, after):
    hr = land_thru.shape[1]

    def body(src_ref, land_ref, *refs):
        x, y, c = _mesh_pos()
        for j in range(N_CHIPS):
            cp = pltpu.make_async_remote_copy(
                src_ref=src_ref.at[j, pl.ds((1 - c) * hr, hr), :], dst_ref=land_ref.at[j], send_sem=refs[j],
                recv_sem=refs[N_CHIPS + j], device_id=(x, y, 1 - c), device_id_type=MESH)
            cp.wait_send()
            cp.wait_recv()

    return pl.pallas_call(
        body, name="pair_wait",
        out_shape=(pltpu.HBM(gw_thru.shape, F32), pltpu.HBM(land_thru.shape, F32)),
        in_specs=(_HBM, _HBM) + (_SEM,) * 8 + (pl.BlockSpec(memory_space=pl.ANY),),
        out_specs=(_HBM, _HBM), input_output_aliases={0: 0, 1: 1},
        compiler_params=pltpu.CompilerParams(has_side_effects=_EFFECT),
    )(gw_thru, land_thru, *sems, after)


def _chip_start(s_in, after):
    def body(src_ref, land_ref, after_ref, *refs):
        x, y, c = _mesh_pos()
        me = 2 * x + y
        for j, (px, py) in enumerate([(1 - x, y), (x, 1 - y), (1 - x, 1 - y)]):
            pltpu.make_async_remote_copy(
                src_ref=src_ref.at[2 * px + py], dst_ref=land_ref.at[me], send_sem=refs[j], recv_sem=refs[3 + j],
                device_id=(px, py, c), device_id_type=MESH).start()
        refs[8][...] = jnp.zeros_like(refs[8])

    sem = pltpu.SemaphoreType.DMA(())
    land = lax.empty(s_in.shape, s_in.dtype)
    return pl.pallas_call(
        body, name="chip_start",
        out_shape=(sem,) * 6 + (pltpu.HBM(s_in.shape, s_in.dtype), pltpu.HBM(land.shape, land.dtype),
                                jax.ShapeDtypeStruct((8, 128), F32)),
        in_specs=(_HBM, _HBM, pl.BlockSpec(memory_space=pl.ANY)),
        out_specs=(_SEM,) * 6 + (_HBM, _HBM, pl.BlockSpec(memory_space=pltpu.VMEM)),
        input_output_aliases={0: 6, 1: 7},
        compiler_params=pltpu.CompilerParams(has_side_effects=_EFFECT),
    )(pltpu.with_memory_space_constraint(s_in, pltpu.HBM), pltpu.with_memory_space_constraint(land, pltpu.HBM), after)


def _chip_wait(sems, s_thru, land_thru, after):
    def body(src_ref, land_ref, *refs):
        x, y, c = _mesh_pos()
        for j, (px, py) in enumerate([(1 - x, y), (x, 1 - y), (1 - x, 1 - y)]):
            cp = pltpu.make_async_remote_copy(
                src_ref=src_ref.at[2 * px + py], dst_ref=land_ref.at[2 * px + py], send_sem=refs[j], recv_sem=refs[3 + j],
                device_id=(px, py, c), device_id_type=MESH)
            cp.wait_send()
            cp.wait_recv()

    return pl.pallas_call(
        body, name="chip_wait",
        out_shape=(pltpu.HBM(s_thru.shape, s_thru.dtype), pltpu.HBM(land_thru.shape, land_thru.dtype)),
        in_specs=(_HBM, _HBM) + (_SEM,) * 6 + (pl.BlockSpec(memory_space=pl.ANY),),
        out_specs=(_HBM, _HBM), input_output_aliases={0: 0, 1: 1},
        compiler_params=pltpu.CompilerParams(has_side_effects=_EFFECT),
    )(s_thru, land_thru, *sems, after)


def _pair_share(h_in, small):
    def body(hin_ref, sm_ref, rin_ref, slots_ref, send_sems, recv_sems, small_send, small_recv, local_sem):
        x, y, c = _mesh_pos()
        dev = 4 * x + 2 * y + c
        mine = pltpu.make_async_copy(sm_ref, slots_ref.at[dev], local_sem)
        mine.start()
        share = pltpu.make_async_remote_copy(
            src_ref=hin_ref, dst_ref=rin_ref, send_sem=send_sems.at[0], recv_sem=recv_sems.at[0],
            device_id=(x, y, 1 - c), device_id_type=MESH)
        share.start()
        started = []
        for k in range(1, 8):
            peer = (x ^ ((k >> 2) & 1), y ^ ((k >> 1) & 1), c ^ (k & 1))
            cp = pltpu.make_async_remote_copy(
                src_ref=sm_ref, dst_ref=slots_ref.at[dev], send_sem=small_send.at[k - 1], recv_sem=small_recv.at[k - 1],
                device_id=peer, device_id_type=MESH)
            cp.start()
            started.append(cp)
        share.wait()
        for k in range(1, 8):
            pltpu.make_async_remote_copy(
                src_ref=sm_ref, dst_ref=slots_ref.at[dev], send_sem=small_send.at[k - 1], recv_sem=small_recv.at[k - 1],
                device_id=(x, y, 1 - c), device_id_type=MESH).wait_recv()
        for cp in started:
            cp.wait_send()
        mine.wait()

    any_spec = pl.BlockSpec(memory_space=pl.ANY)
    return pl.pallas_call(
        body, name="pair_share",
        in_specs=[any_spec] * 2, out_specs=[any_spec] * 2,
        out_shape=[jax.ShapeDtypeStruct(h_in.shape, F32), jax.ShapeDtypeStruct((8,) + small.shape, F32)],
        scratch_shapes=[pltpu.SemaphoreType.DMA((1,)), pltpu.SemaphoreType.DMA((1,)),
                        pltpu.SemaphoreType.DMA((7,)), pltpu.SemaphoreType.DMA((7,)), pltpu.SemaphoreType.DMA],
    )(h_in, small)


def _reduce_w_out_start(slabs, after):
    def body(src_ref, land_ref, after_ref, *refs):
        x, y, c = _mesh_pos()
        me = 4 * x + 2 * y + c
        for k in range(1, 8):
            px, py, pc = x ^ ((k >> 2) & 1), y ^ ((k >> 1) & 1), c ^ (k & 1)
            pltpu.make_async_remote_copy(src_ref=src_ref.at[2 * px + py], dst_ref=land_ref.at[me], send_sem=refs[k - 1],
                                         recv_sem=refs[6 + k], device_id=(px, py, pc), device_id_type=MESH).start()
        refs[16][...] = jnp.zeros_like(refs[16])

    sem = pltpu.SemaphoreType.DMA(())
    land = lax.empty((8,) + slabs.shape[1:], slabs.dtype)
    return pl.pallas_call(
        body, name="reduce_w_out_start",
        out_shape=(sem,) * 14 + (pltpu.HBM(slabs.shape, slabs.dtype), pltpu.HBM(land.shape, land.dtype),
                                 jax.ShapeDtypeStruct((8, 128), F32)),
        in_specs=(_HBM, _HBM, pl.BlockSpec(memory_space=pl.ANY)),
        out_specs=(_SEM,) * 14 + (_HBM, _HBM, pl.BlockSpec(memory_space=pltpu.VMEM)),
        input_output_aliases={0: 14, 1: 15},
        compiler_params=pltpu.CompilerParams(has_side_effects=_EFFECT),
    )(pltpu.with_memory_space_constraint(slabs, pltpu.HBM), pltpu.with_memory_space_constraint(land, pltpu.HBM), after)


def _reduce_w_out_wait(sems, slabs_thru, land_thru, after):
    def body(src_ref, land_ref, *refs):
        x, y, c = _mesh_pos()
        for k in range(1, 8):
            px, py, pc = x ^ ((k >> 2) & 1), y ^ ((k >> 1) & 1), c ^ (k & 1)
            cp = pltpu.make_async_remote_copy(
                src_ref=src_ref.at[2 * px + py], dst_ref=land_ref.at[4 * px + 2 * py + pc], send_sem=refs[k - 1],
                recv_sem=refs[6 + k], device_id=(px, py, pc), device_id_type=MESH)
            cp.wait_send()
            cp.wait_recv()

    return pl.pallas_call(
        body, name="reduce_w_out_wait",
        out_shape=(pltpu.HBM(slabs_thru.shape, slabs_thru.dtype), pltpu.HBM(land_thru.shape, land_thru.dtype)),
        in_specs=(_HBM, _HBM) + (_SEM,) * 14 + (pl.BlockSpec(memory_space=pl.ANY),),
        out_specs=(_HBM, _HBM), input_output_aliases={0: 0, 1: 1},
        compiler_params=pltpu.CompilerParams(has_side_effects=_EFFECT),
    )(slabs_thru, land_thru, *sems, after)


def _pair_add(g, recv, core, name):
    _, rows, C = recv.shape
    tc = 256

    def body(core_ref, g_ref, r_ref, o_ref):
        o_ref[...] = _bf(g_ref[...] + r_ref[...])

    spec = pl.BlockSpec((1, rows, tc), lambda j, i, core: (j, 0, i))
    return pl.pallas_call(
        body, name=name,
        grid_spec=pltpu.PrefetchScalarGridSpec(
            num_scalar_prefetch=1, grid=(N_CHIPS, C // tc),
            in_specs=[pl.BlockSpec((1, rows, tc), lambda j, i, core: (j, core[0], i)), spec], out_specs=spec),
        out_shape=jax.ShapeDtypeStruct((N_CHIPS, rows, C), BF16),
        compiler_params=_cparams(("parallel", "parallel")),
    )(core, g, recv)


def _chip_add(own, parts, chip, name):
    _, rows, C = parts.shape
    tc = 256

    def body(chip_ref, own_ref, r0, r1, r2, r3, o_ref):
        acc = None
        for j, r in enumerate((r0, r1, r2, r3)):
            term = jnp.where(chip_ref[0] == j, own_ref[0], r[0]).astype(F32)
            acc = term if acc is None else acc + term
        o_ref[...] = acc

    def slab(j):
        return pl.BlockSpec((1, rows, tc), lambda i, chip: (jnp.where(chip[0] == j, (j + 1) % N_CHIPS, j), 0, i))

    return pl.pallas_call(
        body, name=name,
        grid_spec=pltpu.PrefetchScalarGridSpec(
            num_scalar_prefetch=1, grid=(C // tc,),
            in_specs=[pl.BlockSpec((1, rows, tc), lambda i, chip: (chip[0], 0, i))] + [slab(j) for j in range(N_CHIPS)],
            out_specs=pl.BlockSpec((rows, tc), lambda i, chip: (0, i))),
        out_shape=jax.ShapeDtypeStruct((rows, C), F32),
        compiler_params=_cparams(("parallel",)),
    )(chip, own, parts, parts, parts, parts)


def _adamw_math(w, g, m, v):
    m = ADAM_B1 * m + (1.0 - ADAM_B1) * g
    v = ADAM_B2 * v + (1.0 - ADAM_B2) * (g * g)
    m_hat = m / (1.0 - ADAM_B1 ** ADAM_STEP)
    v_hat = v / (1.0 - ADAM_B2 ** ADAM_STEP)
    delta = -ADAM_LR * (m_hat / (jnp.sqrt(v_hat) + ADAM_EPS) + ADAM_WD * w)
    return delta, m, v


def _adamw_rows(w, g_own, g_sib, m, v, core, name):
    R, C = w.shape[0], w.shape[-1]
    rows = g_own.shape[0]
    step = 256
    chunks = [(r, min(step, R - r)) for r in range(0, R, step)]
    sub = 64

    def body(core_ref, w_hbm, go_hbm, gs_hbm, m_hbm, v_hbm, d_hbm, nm_hbm, nv_hbm, g_hbm,
             wbuf, mbuf, vbuf, gbuf, dbuf, nmbuf, nvbuf, in_sems, g_sems, out_sems):
        c = core_ref[0]
        flat = lambda ref: ref.at[:, 0, :]
        g_in = [pltpu.make_async_copy(go_hbm, gbuf.at[pl.ds(pl.multiple_of(c * rows, 8), rows), :], g_sems.at[0]),
                pltpu.make_async_copy(gs_hbm, gbuf.at[pl.ds(pl.multiple_of((1 - c) * rows, 8), rows), :], g_sems.at[1])]
        for cp in g_in:
            cp.start()
        loads = []
        for k, (r0, n) in enumerate(chunks):
            cps = [pltpu.make_async_copy(flat(src).at[pl.ds(r0, n), :], dst.at[pl.ds(r0, n), :], in_sems.at[a, k])
                   for a, (src, dst) in enumerate(((w_hbm, wbuf), (m_hbm, mbuf), (v_hbm, vbuf)))]
            for cp in cps:
                cp.start()
            loads.append(cps)
        for cp in g_in:
            cp.wait()
        stores = []
        for k, (r0, n) in enumerate(chunks):
            for cp in loads[k]:
                cp.wait()

            def update(rs):
                g = gbuf[rs, :]
                dl, nm, nv = _adamw_math(wbuf[rs, :], g, mbuf[rs, :], vbuf[rs, :])
                dbuf[rs, :] = dl
                nmbuf[rs, :] = nm
                nvbuf[rs, :] = nv

            if n % sub == 0:
                def block(i, carry, r0=r0):
                    update(pl.ds(pl.multiple_of(r0 + i * sub, 8), sub))
                    return carry
                lax.fori_loop(0, n // sub, block, 0)
            else:
                update(pl.ds(r0, n))
            cps = [pltpu.make_async_copy(src.at[pl.ds(r0, n), :], flat(dst).at[pl.ds(r0, n), :], out_sems.at[a, k])
                   for a, (src, dst) in enumerate(((dbuf, d_hbm), (nmbuf, nm_hbm), (nvbuf, nv_hbm), (gbuf, g_hbm)))]
            for cp in cps:
                cp.start()
            stores += cps
        for cp in stores:
            cp.wait()

    any_spec = pl.BlockSpec(memory_space=pl.ANY)
    dense = pltpu.VMEM((R, C), F32)
    return pl.pallas_call(
        body, name=name,
        grid_spec=pltpu.PrefetchScalarGridSpec(
            num_scalar_prefetch=1, grid=(1,),
            in_specs=[any_spec] * 5, out_specs=[any_spec] * 4,
            scratch_shapes=[dense, dense, dense, pltpu.VMEM((2 * rows, C), F32), dense, dense, dense,
                            pltpu.SemaphoreType.DMA((3, len(chunks))), pltpu.SemaphoreType.DMA((2,)),
                            pltpu.SemaphoreType.DMA((4, len(chunks)))]),
        out_shape=[jax.ShapeDtypeStruct(w.shape, F32)] * 4,
        compiler_params=_cparams(),
    )(core, w, g_own, g_sib, m, v)


def _adamw_sum8(w, slabs, land, m, v, ids, name):
    R, C = w.shape
    tc = 128

    def body(ids_ref, w_ref, own_ref, *refs):
        lrefs, (m_ref, v_ref, d_ref, nm_ref, nv_ref, g_ref) = refs[:8], refs[8:]
        g = None
        for d, l_ref in enumerate(lrefs):
            term = jnp.where(ids_ref[0] == d, own_ref[0], l_ref[0]).astype(F32)
            g = term if g is None else g + term
        dl, nm, nv = _adamw_math(w_ref[...], g, m_ref[...], v_ref[...])
        d_ref[...] = dl
        nm_ref[...] = nm
        nv_ref[...] = nv
        g_ref[...] = g

    def slot(d):
        return pl.BlockSpec((1, R, tc), lambda i, ids: (jnp.where(ids[0] == d, (d + 1) % 8, d), 0, i))

    spec = pl.BlockSpec((R, tc), lambda i, ids: (0, i))
    return pl.pallas_call(
        body, name=name,
        grid_spec=pltpu.PrefetchScalarGridSpec(
            num_scalar_prefetch=1, grid=(C // tc,),
            in_specs=[spec, pl.BlockSpec((1, R, tc), lambda i, ids: (ids[1], 0, i))] + [slot(d) for d in range(8)]
            + [spec, spec],
            out_specs=[spec] * 4),
        out_shape=[jax.ShapeDtypeStruct((R, C), F32)] * 4,
        compiler_params=_cparams(("parallel",)),
    )(ids, w, slabs, *([land] * 8), m, v)


SMALL_NAMES = ("conv_b", "ssd_norm_w", "ln_g", "ln_b", "dt_bias", "a_log", "d_skip", "attn_sinks")
SMALL_FIELDS = ((4, 0, D_XBC), (5, 0, D_SSD), (6, 0, D_MODEL), (7, 0, D_MODEL), (5, 1024, SSD_HEADS), (5, 1152, SSD_HEADS),
                (5, 1280, SSD_HEADS), (5, 1408, ATT_QH))
LOSS_FIELD = (6, 1024, 128)
K_SMALL = D_XBC


def _pack_small(g_conv_w, vecs, loss):
    def body(cw_ref, *refs):
        o_ref = refs[-1]
        o_ref[...] = jnp.zeros_like(o_ref)
        o_ref[0:CONV_K, 0:D_XBC] = cw_ref[...]
        for v_ref, (row, off, n) in zip(refs[:-2], SMALL_FIELDS):
            o_ref[row:row + 1, off:off + n] = v_ref[...]
        o_ref[LOSS_FIELD[0]:LOSS_FIELD[0] + 1, LOSS_FIELD[1]:LOSS_FIELD[1] + LOSS_FIELD[2]] = refs[-2][...]

    return pl.pallas_call(
        body, name="pack_small", out_shape=jax.ShapeDtypeStruct((8, K_SMALL), F32), compiler_params=_cparams(),
    )(g_conv_w, *vecs, loss)


def _adamw_small(slots, chip, conv_w, m_conv_w, v_conv_w, params, moms, vars_):
    n_vec = len(SMALL_NAMES)

    def body(chip_ref, s_ref, *refs):
        ins = refs[:3 * (n_vec + 1)]
        outs = refs[3 * (n_vec + 1):-1]
        tot_ref = refs[-1]
        tot = s_ref[0]
        for d in range(1, 8):
            tot = tot + s_ref[d]
        outs[0][...] = tot[LOSS_FIELD[0]:LOSS_FIELD[0] + 1, LOSS_FIELD[1]:LOSS_FIELD[1] + 1]
        off = pl.multiple_of(chip_ref[0] * CONV_COLS, 128)
        tot_ref[...] = tot
        grads = [tot_ref[0:CONV_K, pl.ds(off, CONV_COLS)]]
        grads += [tot[row:row + 1, o:o + n] for row, o, n in SMALL_FIELDS]
        for k, g in enumerate(grads):
            w_ref, m_ref, v_ref = ins[3 * k:3 * k + 3]
            full = (0,) if k == 0 else (Ellipsis,)
            d, nm, nv = _adamw_math(w_ref[full], g, m_ref[full], v_ref[full])
            for o_ref, val in zip(outs[1 + 4 * k:5 + 4 * k], (g, d, nm, nv)):
                o_ref[full] = val

    args = [conv_w, m_conv_w, v_conv_w]
    for w, m, v in zip(params, moms, vars_):
        args += [w, m, v]
    shapes = [jax.ShapeDtypeStruct((1, 1), F32)] + [jax.ShapeDtypeStruct(conv_w.shape, F32)] * 4
    for w in params:
        shapes += [jax.ShapeDtypeStruct(w.shape, F32)] * 4
    vmem = pl.BlockSpec(memory_space=pltpu.VMEM)
    return pl.pallas_call(
        body, name="adamw_small",
        grid_spec=pltpu.PrefetchScalarGridSpec(
            num_scalar_prefetch=1, grid=(1,),
            in_specs=[pl.BlockSpec(slots.shape, lambda i, chip: (0, 0, 0))] + [vmem] * len(args),
            out_specs=[vmem] * len(shapes), scratch_shapes=[pltpu.VMEM((8, K_SMALL), F32)]),
        out_shape=shapes, compiler_params=_cparams(),
    )(chip, slots, *args)


def kernel(x, positions, w_in, conv_w, conv_b, dt_bias, a_log, d_skip, ssd_norm_w, attn_sinks, w_out, ln_g, ln_b, loss_target, m_w_in, m_conv_w, m_conv_b, m_dt_bias, m_a_log, m_d_skip, m_ssd_norm_w, m_attn_sinks, m_w_out, m_ln_g, m_ln_b, v_w_in, v_conv_w, v_conv_b, v_dt_bias, v_a_log, v_d_skip, v_ssd_norm_w, v_attn_sinks, v_w_out, v_ln_g, v_ln_b):
    mx, my, mc = _mesh_pos()
    chip = 2 * mx + my
    L = x.shape[1]

    conv_w_s8 = jnp.pad(conv_w[0], ((0, 8 - CONV_K), (0, 0)))
    pad_rows = ((0, SLAB_ROWS - W_IN_COLS), (0, 0))
    w_in_t = w_in[0].T
    w_in_b, w_out_b = jnp.pad(_bf(w_in_t), pad_rows), _bf(w_out[0])
    ag_in, ag_cw = _gather_weights(w_in_b, conv_w_s8)
    started = _gather_w_out_start(w_out_b, ag_cw)
    own = (jnp.arange(N_CHIPS) == chip)[:, None, None]

    def get_w_out(after):
        landed = _gather_w_out_wait(started[0:6], started[6], started[7], after)
        return jnp.where(own, w_out_b[None], landed).reshape(D_MIX, D_MODEL)

    w_full = jnp.concatenate([ag_in[j, 0:W_IN_COLS] for j in range(N_CHIPS)], axis=0)
    w = jnp.concatenate([
        w_full[O_Z:O_Z + D_SSD], w_full[O_G:O_G + D_ATT], w_full[O_Q:O_Q + D_ATT],
        w_full[O_XBC:O_XBC + D_XBC], w_full[O_K:O_K + 2 * D_KV], w_full[O_DT:O_DT + SSD_HEADS],
        jnp.zeros((DT_PAD - SSD_HEADS, D_MODEL), BF16)], axis=0)
    conv_w_full = jnp.concatenate([ag_cw[j, 0:CONV_K] for j in range(N_CHIPS)], axis=1)

    loss_part, gx_args, gw_in, w_out_red, small = _local_step(
        x[0], positions[0].reshape(L, 1), loss_target[0], w, get_w_out, started[8][0:1, :], conv_w_full,
        conv_b, dt_bias, a_log, d_skip, ssd_norm_w, attn_sinks, ln_g, ln_b)

    packed = _pack_small(small["conv_w"], [small[n] for n in SMALL_NAMES], loss_part)
    core_id = mc.reshape(1).astype(jnp.int32)
    chip_id = chip.reshape(1).astype(jnp.int32)
    ids = jnp.stack([4 * mx + 2 * my + mc, chip]).astype(jnp.int32)
    slabs = jnp.stack([jnp.pad(gw_in[W_IN_COLS * j:W_IN_COLS * (j + 1)], pad_rows) for j in range(N_CHIPS)])
    w_in_red = _pair_start(slabs, packed)
    grad_x = _grad_x(*gx_args, w_in_red[10], 0)
    gw_in_slabs, recv_in = _pair_wait(w_in_red[0:8], w_in_red[8], w_in_red[9], grad_x[0:8, 0:128])
    s_in = _pair_add(gw_in_slabs, recv_in, core_id, "pair_add_in")
    chip_red = _chip_start(s_in, packed)
    grad_x = _grad_x(*gx_args, chip_red[8], 1, grad_x)
    own_slabs, landed = _reduce_w_out_wait(w_out_red[0:14], w_out_red[14], w_out_red[15], grad_x)
    out_t = _adamw_sum8(w_out[0], own_slabs, landed, m_w_out[0], v_w_out[0], ids, "adamw_w_out")
    d_w_out, nm_w_out, nv_w_out, g_w_out = [a[None] for a in out_t]
    s_in, r_in = _chip_wait(chip_red[0:6], chip_red[6], chip_red[7], out_t[0])
    h_in = _chip_add(s_in, r_in, chip_id, "chip_add_in")
    sib_in, slots = _pair_share(h_in, packed)

    to_rows = lambda a: jnp.transpose(a, (2, 0, 1))
    in_t = _adamw_rows(to_rows(w_in), h_in, sib_in, to_rows(m_w_in), to_rows(v_w_in), core_id, "adamw_w_in")
    d_w_in, nm_w_in, nv_w_in, g_w_in = [jnp.transpose(a, (1, 2, 0)) for a in in_t]

    params = dict(conv_b=conv_b, ssd_norm_w=ssd_norm_w, ln_g=ln_g, ln_b=ln_b, dt_bias=dt_bias, a_log=a_log,
                  d_skip=d_skip, attn_sinks=attn_sinks)
    moms = dict(conv_b=m_conv_b, ssd_norm_w=m_ssd_norm_w, ln_g=m_ln_g, ln_b=m_ln_b, dt_bias=m_dt_bias, a_log=m_a_log,
                d_skip=m_d_skip, attn_sinks=m_attn_sinks)
    vars_ = dict(conv_b=v_conv_b, ssd_norm_w=v_ssd_norm_w, ln_g=v_ln_g, ln_b=v_ln_b, dt_bias=v_dt_bias, a_log=v_a_log,
                 d_skip=v_d_skip, attn_sinks=v_attn_sinks)
    res = _adamw_small(slots, chip_id, conv_w, m_conv_w, v_conv_w, [params[n] for n in SMALL_NAMES],
                       [moms[n] for n in SMALL_NAMES], [vars_[n] for n in SMALL_NAMES])
    loss = res[0][0, 0]
    grads, delta, new_m, new_v = {}, {}, {}, {}
    for k, n in enumerate(("conv_w",) + SMALL_NAMES):
        grads[n], delta[n], new_m[n], new_v[n] = res[1 + 4 * k:5 + 4 * k]
    for dd, a_in, a_out in ((grads, g_w_in, g_w_out), (delta, d_w_in, d_w_out), (new_m, nm_w_in, nm_w_out),
                            (new_v, nv_w_in, nv_w_out)):
        dd["w_in"] = a_in
        dd["w_out"] = a_out
    order = ("w_in", "conv_w", "conv_b", "dt_bias", "a_log", "d_skip", "ssd_norm_w", "attn_sinks", "w_out", "ln_g", "ln_b")
    return (loss, grad_x[None], *[grads[n] for n in order], *[delta[n] for n in order], *[new_m[n] for n in order],
            *[new_v[n] for n in order])
```

```python
import numpy as np
import jax
import jax.numpy as jnp
from jax import lax
from jax.experimental import pallas as pl
from jax.experimental.pallas import tpu as pltpu

F32 = jnp.float32
BF16 = jnp.bfloat16
MESH = pl.DeviceIdType.MESH

D_MODEL = 1024
D_SSD = 1024
D_ATT = 1024
D_MIX = 2048
SSD_HEADS = 16
SSD_P = 64
SSD_GROUPS = 2
SSD_R = 8
SSD_N = 128
D_BC = 256
D_XBC = 1536
CONV_K = 4
CHUNK = 128
ATT_HD = 64
assert ATT_HD in (4, 16, 64, 256)
ATT_QH = 16
ATT_KVH = 4
ATT_R = 4
D_KV = 256
WINDOW = 128
ROPE_THETA = 500000.0
ROPE_DIM = 16
ALPHA = 2.0 ** 0.25
LN_EPS = 1e-5
RMS_EPS = 1e-5
D_IN_PROJ = 5136
O_Z, O_XBC, O_DT, O_Q, O_K, O_V, O_G = 0, 1024, 2560, 2576, 3600, 3856, 4112
P_Z, P_G, P_Q, P_XBC, P_KV, P_DT, P_END = 0, 1024, 2048, 3072, 4608, 5120, 5248
DT_PAD = 128
N_CHIPS = 4
W_IN_COLS = D_IN_PROJ // N_CHIPS
SLAB_ROWS = 1312
W_OUT_ROWS = D_MIX // N_CHIPS
CONV_COLS = D_XBC // N_CHIPS

ADAM_LR = 0.001
ADAM_B1 = 0.9
ADAM_B2 = 0.999
ADAM_EPS = 1e-08
ADAM_WD = 0.01
ADAM_STEP = 10

VMEM_LIMIT = 56 * 1024 * 1024
ROW_TILE = 512
NEG_BIG = -1e30


def _cparams(sem=None, **kw):
    if sem is not None:
        kw["dimension_semantics"] = sem
    return pltpu.CompilerParams(vmem_limit_bytes=VMEM_LIMIT, **kw)


def _dot(a, b):
    return jnp.dot(a, b, preferred_element_type=F32)


def _dot_nt(a, b):
    return lax.dot_general(a, b, (((1,), (1,)), ((), ())), preferred_element_type=F32)


def _dot_tn(a, b):
    return lax.dot_general(a, b, (((0,), (0,)), ((), ())), preferred_element_type=F32)


def _bf(a):
    return a.astype(BF16)


def _iota2(shape, dim):
    return lax.broadcasted_iota(jnp.int32, shape, dim)


def _three_terms(x):
    hi = _bf(x)
    r = x - hi.astype(F32)
    mid = _bf(r)
    return hi, mid, _bf(r - mid.astype(F32))


def _dot01(m, a):
    return sum(_dot(m, t) for t in _three_terms(a))


def _to_rows(col):
    k = col.shape[1]
    eye = (_iota2((k, k), 0) == _iota2((k, k), 1)).astype(BF16)
    return sum(_dot_nt(eye, t) for t in _three_terms(col))


def _to_cols(row):
    n = row.shape[1]
    eye = (_iota2((n, n), 0) == _iota2((n, n), 1)).astype(BF16)
    return sum(_dot_nt(eye, t) for t in _three_terms(row))


def _sigmoid(x):
    return jax.nn.sigmoid(x)


def _in_proj(x, w, pos, inv):
    L = x.shape[0]
    tm = ROW_TILE
    widths = (D_SSD, D_ATT, D_ATT, D_XBC, 2 * D_KV, DT_PAD)

    def body(x_ref, w_ref, pos_ref, inv_ref, z_ref, g_ref, q_ref, xbc_ref, kv_ref, dt_ref, xb_ref):
        xb = _bf(x_ref[...])
        xb_ref[...] = xb
        tabs = _rope_tables(pos_ref, inv_ref)
        q_ref[...] = _bf(_rope(_dot_nt(xb, w_ref[P_Q:P_Q + D_ATT, :]), tabs))
        kv_ref[:, 0:D_KV] = _bf(_rope(_dot_nt(xb, w_ref[P_KV:P_KV + D_KV, :]), tabs))
        kv_ref[:, D_KV:2 * D_KV] = _bf(_dot_nt(xb, w_ref[P_KV + D_KV:P_KV + 2 * D_KV, :]))
        for o_ref, off, wd in zip((z_ref, g_ref, xbc_ref, dt_ref), (P_Z, P_G, P_XBC, P_DT), (D_SSD, D_ATT, D_XBC, DT_PAD)):
            o_ref[...] = _dot_nt(xb, w_ref[off:off + wd, :])

    row = lambda wd: pl.BlockSpec((tm, wd), lambda i: (i, 0))
    return pl.pallas_call(
        body, name="in_proj", grid=(L // tm,),
        in_specs=[row(D_MODEL), pl.BlockSpec((P_END, D_MODEL), lambda i: (0, 0), pipeline_mode=pl.Buffered(1)), row(1),
                  pl.BlockSpec((1, 2 * ATT_HD), lambda i: (0, 0))],
        out_specs=[row(wd) for wd in widths] + [row(D_MODEL)],
        out_shape=[jax.ShapeDtypeStruct((L, wd), dt) for wd, dt in zip(widths, (F32, F32, BF16, F32, BF16, F32))]
        + [jax.ShapeDtypeStruct((L, D_MODEL), BF16)],
        compiler_params=_cparams(("parallel",)),
    )(x, w, pos, inv)


def _matmuls_tn(a_list, b, name):
    K, N = b.shape
    tk = min(K, 1024)
    n = len(a_list)

    def body(*refs):
        b_ref = refs[n]
        o_refs, acc_refs = refs[n + 1:2 * n + 1], refs[2 * n + 1:]

        @pl.when(pl.program_id(0) == 0)
        def _():
            for acc in acc_refs:
                acc[...] = jnp.zeros_like(acc)

        bb = _bf(b_ref[...])
        for a_ref, o_ref, acc in zip(refs[:n], o_refs, acc_refs):
            total = acc[...] + _dot_tn(_bf(a_ref[...]), bb)
            acc[...] = total
            o_ref[...] = _bf(total)

    return pl.pallas_call(
        body, name=name, grid=(K // tk,),
        in_specs=[pl.BlockSpec((tk, a.shape[1]), lambda k: (k, 0)) for a in a_list] + [pl.BlockSpec((tk, N), lambda k: (k, 0))],
        out_specs=[pl.BlockSpec((a.shape[1], N), lambda k: (0, 0)) for a in a_list],
        out_shape=[jax.ShapeDtypeStruct((a.shape[1], N), BF16) for a in a_list],
        scratch_shapes=[pltpu.VMEM((a.shape[1], N), F32) for a in a_list],
        compiler_params=_cparams(("arbitrary",)),
    )(*a_list, b)


def _grad_x(dr, dz, dg, dq, dxbc, dkv, ddt, w, after, part, prev=None):
    L = dr.shape[0]
    tm = min(ROW_TILE, L // 4)
    first = L // (4 * tm)
    n = first if part == 0 else L // tm - first
    widths = (D_SSD, D_ATT, D_ATT, D_XBC, 2 * D_KV, DT_PAD)
    offs = (P_Z, P_G, P_Q, P_XBC, P_KV, P_DT)

    def body(dr_ref, dz_ref, dg_ref, dq_ref, dxbc_ref, dkv_ref, ddt_ref, w_ref, after_ref, *rest):
        o_ref = rest[-1]
        acc = ALPHA * dr_ref[...]
        for p_ref, off, wd in zip((dz_ref, dg_ref, dq_ref, dxbc_ref, dkv_ref, ddt_ref), offs, widths):
            acc = acc + _dot(_bf(p_ref[...]), w_ref[off:off + wd, :])
        o_ref[...] = acc

    row = lambda wd: pl.BlockSpec((tm, wd), lambda i: (i + part * first, 0))
    ins = [dr, dz, dg, dq, dxbc, dkv, ddt, w, after]
    specs = ([row(D_MODEL)] + [row(wd) for wd in widths]
             + [pl.BlockSpec((P_END, D_MODEL), lambda i: (0, 0), pipeline_mode=pl.Buffered(1)),
                pl.BlockSpec((8, 128), lambda i: (0, 0))])
    if prev is not None:
        ins.append(prev)
        specs.append(pl.BlockSpec(memory_space=pl.ANY))
    return pl.pallas_call(
        body, name="grad_x_%d" % part, grid=(n,),
        in_specs=specs, out_specs=row(D_MODEL),
        out_shape=jax.ShapeDtypeStruct((L, D_MODEL), F32),
        input_output_aliases={} if prev is None else {len(ins) - 1: 0},
        compiler_params=_cparams(("parallel",)),
    )(*ins)


HALO = 16


def _shift_matrix(offsets):
    n = CHUNK + HALO
    m = np.zeros((len(offsets) * CHUNK, 2 * n), np.float32)
    for k, off in enumerate(offsets):
        t = np.arange(CHUNK)
        m[k * CHUNK + t, t + off] = 1.0
        m[k * CHUNK + t, n + t + off] = 1.0
    return jnp.asarray(m, BF16)


def _shifted_rows(first_part, second_part, smat_ref):
    h1, l1 = _hi_lo(first_part)
    h2, l2 = _hi_lo(second_part)
    sh = _dot(smat_ref[...], jnp.concatenate([h1, h2, l1, l2], axis=0))
    return sh[0:CHUNK], sh[CHUNK:2 * CHUNK], sh[2 * CHUNK:3 * CHUNK]


def _ssd_chunk_pre(first, xbc_ref, tail_ref, dt_ref, cw_ref, cb_ref, dtb_ref, alog_ref, smat_ref=None, ext=None):
    tail = jnp.where(first, 0.0, tail_ref[...])
    x = xbc_ref[...]
    if ext is None:
        taps = _shifted_rows(tail, x, smat_ref) + (x,)
    else:
        ext[0:HALO, :] = tail
        ext[HALO:HALO + CHUNK, :] = x
        taps = tuple(ext[pl.ds(HALO - (CONV_K - 1) + k, CHUNK), :] for k in range(CONV_K - 1)) + (x,)
    u = cb_ref[...] + cw_ref[0:1, :] * taps[0]
    for k in range(1, CONV_K):
        u = u + cw_ref[k:k + 1, :] * taps[k]
    sig = _sigmoid(u)
    xbc = u * sig
    dtraw = dt_ref[:, 0:SSD_HEADS] + dtb_ref[...]
    dt = jax.nn.softplus(dtraw)
    A = -jnp.exp(alog_ref[...])
    a = dt * A
    tril = (_iota2((CHUNK, CHUNK), 0) >= _iota2((CHUNK, CHUNK), 1)).astype(BF16)
    acs = _dot01(tril, a)
    acs_row = _to_rows(acs)
    return u, sig, xbc, dtraw, dt, A, acs, acs_row, taps


def _head_expander():
    return (_iota2((SSD_HEADS, D_SSD), 1) // SSD_P == _iota2((SSD_HEADS, D_SSD), 0)).astype(BF16)


def _hi_lo(x):
    hi = _bf(x)
    return hi, _bf(x - hi.astype(F32))


def _expand(v, e):
    hi, lo = _hi_lo(v)
    return _dot(hi, e) + _dot(lo, e)


def _headsum(t, e):
    m = t.shape[0]
    if m < 8:
        t = jnp.broadcast_to(t[0:1], (8, t.shape[1]))
    hi, lo = _hi_lo(t)
    return (_dot_nt(hi, e) + _dot_nt(lo, e))[0:m]


def _ssd_decays(dt, acs, dsk_ref, e):
    alast = acs[CHUNK - 1:CHUNK, :]
    stk = jnp.concatenate([dt, jnp.exp(acs), jnp.exp(alast - acs),
                           jnp.broadcast_to(jnp.exp(alast), (8, SSD_HEADS)),
                           jnp.broadcast_to(dsk_ref[...], (8, SSD_HEADS))], axis=0)
    ex = _expand(stk, e)
    return (ex[0:CHUNK], ex[CHUNK:2 * CHUNK], ex[2 * CHUNK:3 * CHUNK], ex[3 * CHUNK:3 * CHUNK + 1],
            ex[3 * CHUNK + 8:3 * CHUNK + 9])


def _ssd_fwd(z, xbc, dtp, conv_w, conv_b, dt_bias, a_log, d_skip, norm_w):
    L = z.shape[0]
    nc = L // CHUNK
    half = D_SSD // SSD_GROUPS

    def body(z_ref, xbc_ref, tail_ref, dt_ref, cw_ref, cb_ref, dtb_ref, alog_ref, dsk_ref, nw_ref,
             y_ref, ypre_ref, prev_ref, state, ybuf, mbuf, ext):
        c = pl.program_id(0)

        @pl.when(c == 0)
        def _():
            state[...] = jnp.zeros_like(state)

        u, sig, xbcv, dtraw, dt, A, acs, acs_row, _ = _ssd_chunk_pre(
            c == 0, xbc_ref, tail_ref, dt_ref, cw_ref, cb_ref, dtb_ref, alog_ref, ext=ext)
        e = _head_expander()
        dtE, eacsE, dsdE, ealE, dskE = _ssd_decays(dt, acs, dsk_ref, e)
        xs = xbcv[:, 0:D_SSD]
        X = xs * dtE
        prev_ref[0] = state[...]
        causal = _iota2((CHUNK, CHUNK), 0) >= _iota2((CHUNK, CHUNK), 1)
        for g in range(SSD_GROUPS):
            gs = slice(half * g, half * (g + 1))
            Bg = _bf(xbcv[:, D_SSD + SSD_N * g:D_SSD + SSD_N * (g + 1)])
            Cg = _bf(xbcv[:, D_SSD + D_BC + SSD_N * g:D_SSD + D_BC + SSD_N * (g + 1)])
            cb = _dot_nt(Cg, Bg)
            for r in range(SSD_R):
                h = g * SSD_R + r
                seg = acs[:, h:h + 1] - acs_row[h:h + 1, :]
                mbuf[h] = _bf(cb * jnp.where(causal, jnp.exp(jnp.where(causal, seg, 0.0)), 0.0))
            st = state[:, gs]
            ybuf[:, gs] = _dot(Cg, _bf(st)) * eacsE[:, gs] + dskE[:, gs] * xs[:, gs]
            state[:, gs] = st * ealE[:, gs] + _dot_tn(Bg, _bf(X[:, gs] * dsdE[:, gs]))
        Xb = _bf(X)
        for h in range(SSD_HEADS):
            hs = slice(SSD_P * h, SSD_P * (h + 1))
            ybuf[:, hs] += _dot(mbuf[h], Xb[:, hs])
        y = ybuf[...]
        ypre_ref[...] = y
        zv = z_ref[...]
        yf = y * (zv * _sigmoid(zv))
        for g in range(SSD_GROUPS):
            gs = slice(half * g, half * (g + 1))
            yg = yf[:, gs]
            ms = jnp.mean(yg * yg, axis=-1, keepdims=True)
            y_ref[:, gs] = _bf(yg * lax.rsqrt(ms + RMS_EPS) * nw_ref[:, gs])

    full = lambda shape: pl.BlockSpec(shape, lambda c: (0, 0))
    return pl.pallas_call(
        body, name="ssd_fwd", grid=(nc,),
        in_specs=[
            pl.BlockSpec((CHUNK, D_SSD), lambda c: (c, 0)),
            pl.BlockSpec((CHUNK, D_XBC), lambda c: (c, 0)),
            pl.BlockSpec((HALO, D_XBC), lambda c: (jnp.maximum(c * (CHUNK // HALO) - 1, 0), 0)),
            pl.BlockSpec((CHUNK, DT_PAD), lambda c: (c, 0)),
            full((CONV_K, D_XBC)), full((1, D_XBC)), full((1, SSD_HEADS)), full((1, SSD_HEADS)), full((1, SSD_HEADS)),
            full((1, D_SSD)),
        ],
        out_specs=[
            pl.BlockSpec((CHUNK, D_SSD), lambda c: (c, 0)),
            pl.BlockSpec((CHUNK, D_SSD), lambda c: (c, 0)),
            pl.BlockSpec((1, SSD_N, D_SSD), lambda c: (c, 0, 0)),
        ],
        out_shape=[
            jax.ShapeDtypeStruct((L, D_SSD), BF16),
            jax.ShapeDtypeStruct((L, D_SSD), F32),
            jax.ShapeDtypeStruct((nc, SSD_N, D_SSD), F32),
        ],
        scratch_shapes=[
            pltpu.VMEM((SSD_N, D_SSD), F32),
            pltpu.VMEM((CHUNK, D_SSD), F32),
            pltpu.VMEM((SSD_HEADS, CHUNK, CHUNK), BF16),
            pltpu.VMEM((CHUNK + HALO, D_XBC), F32),
        ],
        compiler_params=_cparams(("arbitrary",)),
    )(z, xbc, xbc, dtp, conv_w, conv_b, dt_bias, a_log, d_skip, norm_w)


def _ssd_bwd(dy, z, ypre, xbc, dtp, prev, conv_w, conv_b, dt_bias, a_log, d_skip, norm_w):
    L = z.shape[0]
    nc = L // CHUNK
    half = D_SSD // SSD_GROUPS

    def body(dy_ref, z_ref, ypre_ref, xbc_ref, tail_ref, dt_ref, prev_ref, cw_ref, cb_ref, dtb_ref, alog_ref, dsk_ref,
             nw_ref, smat_ref, smat2_ref, dz_ref, dxbc_ref, ddt_ref, gcw_ref, gcb_ref, gdtb_ref, galog_ref, gdsk_ref,
             gnw_ref, dstate, dhead, dpost, yobuf, bdbuf, lmbuf, dmbuf, cbbuf):
        i = pl.program_id(0)
        c = nc - 1 - i

        @pl.when(i == 0)
        def _():
            dstate[...] = jnp.zeros_like(dstate)
            dhead[...] = jnp.zeros_like(dhead)
            gcw_ref[...] = jnp.zeros_like(gcw_ref)
            gcb_ref[...] = jnp.zeros_like(gcb_ref)
            gdtb_ref[...] = jnp.zeros_like(gdtb_ref)
            galog_ref[...] = jnp.zeros_like(galog_ref)
            gdsk_ref[...] = jnp.zeros_like(gdsk_ref)
            gnw_ref[...] = jnp.zeros_like(gnw_ref)

        u, sig, xbcv, dtraw, dt, A, acs, acs_row, taps = _ssd_chunk_pre(
            c == 0, xbc_ref, tail_ref, dt_ref, cw_ref, cb_ref, dtb_ref, alog_ref, smat_ref)
        e = _head_expander()
        dtE, eacsE, dsdE, ealE, dskE = _ssd_decays(dt, acs, dsk_ref, e)
        alast = acs[CHUNK - 1:CHUNK, :]
        xs = xbcv[:, 0:D_SSD]
        X = xs * dtE
        Xb = _bf(X)

        zv = z_ref[...]
        ypre = ypre_ref[...]
        dyn = dy_ref[...]
        sz = _sigmoid(zv)
        silu_z = zv * sz
        yf = ypre * silu_z
        dyf_parts = []
        for g in range(SSD_GROUPS):
            gs = slice(half * g, half * (g + 1))
            yg = yf[:, gs]
            rstd = lax.rsqrt(jnp.mean(yg * yg, axis=-1, keepdims=True) + RMS_EPS)
            dout = dyn[:, gs]
            gnw_ref[:, gs] += jnp.sum(dout * yg * rstd, axis=0, keepdims=True)
            dyhat = dout * nw_ref[:, gs]
            dyf_parts.append(rstd * (dyhat - yg * (rstd * rstd) * jnp.mean(dyhat * yg, axis=-1, keepdims=True)))
        dyf = jnp.concatenate(dyf_parts, axis=1)
        dz_ref[...] = _bf(dyf * ypre * (sz * (1.0 + zv * (1.0 - sz))))
        dyp = dyf * silu_z
        dyb = _bf(dyp)
        G = dyp * eacsE

        causal = _iota2((CHUNK, CHUNK), 0) >= _iota2((CHUNK, CHUNK), 1)
        ST = prev_ref[0]
        dST = dstate[...]
        for g in range(SSD_GROUPS):
            gs = slice(half * g, half * (g + 1))
            bs = slice(D_SSD + SSD_N * g, D_SSD + SSD_N * (g + 1))
            cs = slice(D_SSD + D_BC + SSD_N * g, D_SSD + D_BC + SSD_N * (g + 1))
            Bg = _bf(xbcv[:, bs])
            Cg = _bf(xbcv[:, cs])
            Gb = _bf(G[:, gs])
            STb = _bf(ST[:, gs])
            dSTb = _bf(dST[:, gs])
            dstate[:, gs] = dST[:, gs] * ealE[:, gs] + _dot_tn(Cg, Gb)
            yobuf[:, gs] = _dot(Cg, STb) * eacsE[:, gs]
            bdbuf[:, gs] = _dot(Bg, dSTb)
            dpost[:, cs] = _dot_nt(Gb, STb)
            dpost[:, bs] = _dot_nt(_bf(X[:, gs] * dsdE[:, gs]), dSTb)
            cbbuf[g] = _dot_nt(Cg, Bg)
            for r in range(SSD_R):
                h = g * SSD_R + r
                seg = acs[:, h:h + 1] - acs_row[h:h + 1, :]
                lmbuf[h] = jnp.where(causal, jnp.exp(jnp.where(causal, seg, 0.0)), 0.0)
        for h in range(SSD_HEADS):
            hs = slice(SSD_P * h, SSD_P * (h + 1))
            Mb = _bf(cbbuf[h // SSD_R] * lmbuf[h])
            dmbuf[h] = _dot_nt(dyb[:, hs], Xb[:, hs])
            dpost[:, hs] = _dot_tn(Mb, dyb[:, hs])
        lane16 = _iota2((1, SSD_HEADS), 1)
        sub16 = _iota2((SSD_HEADS, 1), 0)
        dacs_col = jnp.zeros((CHUNK, SSD_HEADS), F32)
        dacs_row = jnp.zeros((SSD_HEADS, CHUNK), F32)
        for g in range(SSD_GROUPS):
            bs = slice(D_SSD + SSD_N * g, D_SSD + SSD_N * (g + 1))
            cs = slice(D_SSD + D_BC + SSD_N * g, D_SSD + D_BC + SSD_N * (g + 1))
            cb = cbbuf[g]
            dcb = jnp.zeros((CHUNK, CHUNK), F32)
            for r in range(SSD_R):
                h = g * SSD_R + r
                dM = dmbuf[h]
                Lm = lmbuf[h]
                dcb = dcb + dM * Lm
                dseg = dM * (cb * Lm)
                dacs_col = dacs_col + jnp.sum(dseg, axis=-1, keepdims=True) * (lane16 == h).astype(F32)
                dacs_row = dacs_row - jnp.sum(dseg, axis=0, keepdims=True) * (sub16 == h).astype(F32)
            dcbb = _bf(dcb)
            dpost[:, bs] += _dot_tn(dcbb, _bf(xbcv[:, cs]))
            dpost[:, cs] += _dot(dcbb, _bf(xbcv[:, bs]))

        BD = bdbuf[...]
        dX = dpost[:, 0:D_SSD] + dsdE * BD
        dsd = jnp.exp(alast - acs)
        T = _headsum(X * BD, e) * dsd
        dalast = jnp.sum(T, axis=0, keepdims=True) + _headsum(
            jnp.sum(dST * ST, axis=0, keepdims=True), e) * jnp.exp(alast)
        is_last = (_iota2((CHUNK, 1), 0) == CHUNK - 1).astype(F32)
        dacs = dacs_col + _to_cols(dacs_row) + _headsum(dyp * yobuf[...], e) - T + is_last * dalast
        triu = (_iota2((CHUNK, CHUNK), 0) <= _iota2((CHUNK, CHUNK), 1)).astype(BF16)
        da = _dot01(triu, dacs)
        ddt_tot = _headsum(dX * xs, e) + da * A
        galog_ref[...] += jnp.sum(da * dt, axis=0, keepdims=True) * A
        ddtraw = ddt_tot * _sigmoid(dtraw)
        gdtb_ref[...] += jnp.sum(ddtraw, axis=0, keepdims=True)
        gdsk_ref[...] += _headsum(jnp.sum(dyp * xs, axis=0, keepdims=True), e)
        ddt_ref[...] = jnp.zeros_like(ddt_ref)
        ddt_ref[:, 0:SSD_HEADS] = ddtraw
        dpost[:, 0:D_SSD] = dX * dtE + dskE * dyp

        dconv = dpost[...] * (sig * (1.0 + u * (1.0 - sig)))
        gcb_ref[...] += jnp.sum(dconv, axis=0, keepdims=True)
        for k in range(CONV_K):
            gcw_ref[k:k + 1, :] += jnp.sum(dconv * taps[k], axis=0, keepdims=True)
        later = _shifted_rows(dconv, dhead[...], smat2_ref)
        dx = cw_ref[CONV_K - 1:CONV_K, :] * dconv
        for k in range(CONV_K - 1):
            dx = dx + cw_ref[k:k + 1, :] * later[k]
        dxbc_ref[...] = _bf(dx)
        dhead[...] = dconv[0:HALO, :]

    full = lambda shape: pl.BlockSpec(shape, lambda i: (0, 0))
    rev = lambda wd: pl.BlockSpec((CHUNK, wd), lambda i: (nc - 1 - i, 0))
    return pl.pallas_call(
        body, name="ssd_bwd", grid=(nc,),
        in_specs=[
            rev(D_SSD), rev(D_SSD), rev(D_SSD), rev(D_XBC),
            pl.BlockSpec((HALO, D_XBC), lambda i: (jnp.maximum((nc - 1 - i) * (CHUNK // HALO) - 1, 0), 0)),
            rev(DT_PAD),
            pl.BlockSpec((1, SSD_N, D_SSD), lambda i: (nc - 1 - i, 0, 0)),
            full((CONV_K, D_XBC)), full((1, D_XBC)), full((1, SSD_HEADS)), full((1, SSD_HEADS)), full((1, SSD_HEADS)),
            full((1, D_SSD)), full((3 * CHUNK, 2 * (CHUNK + HALO))), full((3 * CHUNK, 2 * (CHUNK + HALO))),
        ],
        out_specs=[
            rev(D_SSD), rev(D_XBC), rev(DT_PAD),
            full((CONV_K, D_XBC)), full((1, D_XBC)), full((1, SSD_HEADS)), full((1, SSD_HEADS)), full((1, SSD_HEADS)),
            full((1, D_SSD)),
        ],
        out_shape=[
            jax.ShapeDtypeStruct((L, D_SSD), BF16), jax.ShapeDtypeStruct((L, D_XBC), BF16),
            jax.ShapeDtypeStruct((L, DT_PAD), F32),
            jax.ShapeDtypeStruct((CONV_K, D_XBC), F32), jax.ShapeDtypeStruct((1, D_XBC), F32),
            jax.ShapeDtypeStruct((1, SSD_HEADS), F32), jax.ShapeDtypeStruct((1, SSD_HEADS), F32),
            jax.ShapeDtypeStruct((1, SSD_HEADS), F32), jax.ShapeDtypeStruct((1, D_SSD), F32),
        ],
        scratch_shapes=[
            pltpu.VMEM((SSD_N, D_SSD), F32),
            pltpu.VMEM((HALO, D_XBC), F32),
            pltpu.VMEM((CHUNK, D_XBC), F32),
            pltpu.VMEM((CHUNK, D_SSD), F32),
            pltpu.VMEM((CHUNK, D_SSD), F32),
            pltpu.VMEM((SSD_HEADS, CHUNK, CHUNK), F32),
            pltpu.VMEM((SSD_HEADS, CHUNK, CHUNK), F32),
            pltpu.VMEM((SSD_GROUPS, CHUNK, CHUNK), F32),
        ],
        compiler_params=_cparams(("arbitrary",)),
    )(dy, z, ypre, xbc, xbc, dtp, prev, conv_w, conv_b, dt_bias, a_log, d_skip, norm_w, _shift_matrix((13, 14, 15)),
      _shift_matrix((3, 2, 1)))


def _rope_tables(pos_ref, inv_ref):
    ang = pos_ref[...].astype(F32) * inv_ref[...]
    d = _iota2((1, 2 * ATT_HD), 1) % ATT_HD
    s = jnp.sin(ang)
    return jnp.cos(ang), jnp.where(d < ROPE_DIM // 2, -s, 0.0), jnp.where((d >= ROPE_DIM // 2) & (d < ROPE_DIM), s, 0.0)


def _rope(t, tabs):
    c, s1, s2 = tabs
    n = t.shape[1]
    rep = n // c.shape[1]
    return (t * jnp.tile(c, (1, rep)) + pltpu.roll(t, n - ROPE_DIM // 2, 1) * jnp.tile(s1, (1, rep))
            + pltpu.roll(t, ROPE_DIM // 2, 1) * jnp.tile(s2, (1, rep)))


def _rope_t(t, tabs):
    c, s1, s2 = tabs
    n = t.shape[1]
    rep = n // c.shape[1]
    return (t * jnp.tile(c, (1, rep)) + pltpu.roll(t * jnp.tile(s1, (1, rep)), ROPE_DIM // 2, 1)
            + pltpu.roll(t * jnp.tile(s2, (1, rep)), n - ROPE_DIM // 2, 1))


def _stack_heads(t, j):
    return jnp.concatenate([t[:, ATT_HD * (j * ATT_R + r):ATT_HD * (j * ATT_R + r + 1)] for r in range(ATT_R)], axis=0)


def _swa_mask_t(first):
    si = _iota2((2 * WINDOW, ATT_R * WINDOW), 0)
    qi = _iota2((2 * WINDOW, ATT_R * WINDOW), 1) % WINDOW
    band = (si > qi) & (si <= qi + WINDOW)
    return band & (jnp.logical_not(first) | (si >= WINDOW))


def _head_rows(ref, j):
    if ref.shape[0] == 1:
        parts = [jnp.broadcast_to(ref[:, j * ATT_R + r:j * ATT_R + r + 1], (1, WINDOW)) for r in range(ATT_R)]
    else:
        parts = [ref[j * ATT_R + r:j * ATT_R + r + 1, :] for r in range(ATT_R)]
    return jnp.concatenate(parts, axis=1)


def _swa_fwd(q, g, kv, sinks):
    L = q.shape[0]
    nb = L // WINDOW
    scale = ATT_HD ** -0.5

    def body(q_ref, g_ref, kvc_ref, kvp_ref, sink_ref, y_ref, o_ref, lse_ref, otbuf):
        n = pl.program_id(0)
        kk = jnp.concatenate([kvp_ref[:, 0:D_KV], kvc_ref[:, 0:D_KV]], axis=0) * scale
        vv = jnp.concatenate([kvp_ref[:, D_KV:2 * D_KV], kvc_ref[:, D_KV:2 * D_KV]], axis=0)
        valid = _swa_mask_t(n == 0)
        qv = q_ref[...]
        for j in range(ATT_KVH):
            js = slice(ATT_HD * j, ATT_HD * (j + 1))
            st = _dot_nt(kk[:, js], _stack_heads(qv, j))
            st = jnp.where(valid, st, NEG_BIG)
            sink = _head_rows(sink_ref, j)
            m = jnp.maximum(jnp.max(st, axis=0, keepdims=True), sink)
            p = jnp.exp(st - m)
            vx = jnp.concatenate([vv[:, js], jnp.ones((2 * WINDOW, ATT_HD), BF16)], axis=1)
            otx = _dot_tn(vx, _bf(p))
            denom = otx[ATT_HD:ATT_HD + 1] + jnp.exp(sink - m)
            ot = otx[0:ATT_HD] * (1.0 / denom)
            lse = m + jnp.log(denom)
            for r in range(ATT_R):
                h = j * ATT_R + r
                otbuf[ATT_HD * h:ATT_HD * (h + 1), :] = ot[:, WINDOW * r:WINDOW * (r + 1)]
                lse_ref[h:h + 1, :] = lse[:, WINDOW * r:WINDOW * (r + 1)]
        o = otbuf[...].T
        o_ref[...] = o
        gv = g_ref[...]
        y_ref[...] = _bf(o * (gv * _sigmoid(gv)))

    cur = lambda wd: pl.BlockSpec((WINDOW, wd), lambda n: (n, 0))
    prv = lambda wd: pl.BlockSpec((WINDOW, wd), lambda n: (jnp.maximum(n - 1, 0), 0))
    return pl.pallas_call(
        body, name="swa_fwd", grid=(nb,),
        in_specs=[cur(D_ATT), cur(D_ATT), cur(2 * D_KV), prv(2 * D_KV), pl.BlockSpec((1, ATT_QH), lambda n: (0, 0))],
        out_specs=[cur(D_ATT), cur(D_ATT), pl.BlockSpec((ATT_QH, WINDOW), lambda n: (0, n))],
        out_shape=[jax.ShapeDtypeStruct((L, D_ATT), BF16), jax.ShapeDtypeStruct((L, D_ATT), F32),
                   jax.ShapeDtypeStruct((ATT_QH, L), F32)],
        scratch_shapes=[pltpu.VMEM((D_ATT, WINDOW), F32)],
        compiler_params=_cparams(("parallel",)),
    )(q, g, kv, kv, sinks)


def _swa_bwd(dy, q, g, kv, o, lse, pos, inv, sinks):
    L = q.shape[0]
    nb = L // WINDOW
    scale = ATT_HD ** -0.5

    def body(dy_ref, q_ref, g_ref, kvc_ref, kvp_ref, o_ref, lse_ref, posc_ref, posp_ref, inv_ref, sink_ref,
             dq_ref, dg_ref, dkv_ref, dsink_ref, carry, dqbuf, dkbuf, dvbuf):
        n = pl.program_id(0)

        @pl.when(n == 0)
        def _():
            dsink_ref[...] = jnp.zeros_like(dsink_ref)

        @pl.when(n < nb)
        def _():
            tc = _rope_tables(posc_ref, inv_ref)
            tp = _rope_tables(posp_ref, inv_ref)
            kk = jnp.concatenate([kvp_ref[:, 0:D_KV], kvc_ref[:, 0:D_KV]], axis=0) * scale
            vv = jnp.concatenate([kvp_ref[:, D_KV:2 * D_KV], kvc_ref[:, D_KV:2 * D_KV]], axis=0)
            valid = _swa_mask_t(n == 0)
            qv = q_ref[...]
            gv = g_ref[...]
            sg = _sigmoid(gv)
            dyv = dy_ref[...]
            ov = o_ref[...]
            dg_ref[...] = _bf(dyv * ov * (sg * (1.0 + gv * (1.0 - sg))))
            do = dyv * (gv * sg)
            dod = do * ov
            ones = jnp.ones((8, ATT_HD), BF16)
            lane16 = _iota2((1, ATT_QH), 1)
            dsink = jnp.zeros((1, ATT_QH), F32)
            for j in range(ATT_KVH):
                js = slice(ATT_HD * j, ATT_HD * (j + 1))
                kj = kk[:, js]
                vj = vv[:, js]
                qs = _stack_heads(qv, j)
                dos = _bf(_stack_heads(do, j))
                hi, lo = _hi_lo(_stack_heads(dod, j))
                delta = (_dot_nt(ones, hi) + _dot_nt(ones, lo))[0:1]
                lse = _head_rows(lse_ref, j)
                st = _dot_nt(kj, qs)
                pt = jnp.exp(jnp.where(valid, st, NEG_BIG) - lse)
                dst = _bf(pt * (_dot_nt(vj, dos) - delta))
                dqt = _dot_tn(kj, dst)
                dkbuf[:, js] = _dot(dst, qs) * scale
                dvbuf[:, js] = _dot(_bf(pt), dos)
                sd = jnp.exp(_head_rows(sink_ref, j) - lse) * delta
                for r in range(ATT_R):
                    h = j * ATT_R + r
                    ls = slice(WINDOW * r, WINDOW * (r + 1))
                    dqbuf[ATT_HD * h:ATT_HD * (h + 1), :] = dqt[:, ls]
                    dsink = dsink - jnp.sum(sd[:, ls], axis=1, keepdims=True) * (lane16 == h).astype(F32)
            dsink_ref[...] += dsink
            dq_ref[...] = _bf(_rope_t(dqbuf[...].T, tc))
            dkp = _rope_t(dkbuf[0:WINDOW, :], tp)
            dkc = _rope_t(dkbuf[WINDOW:2 * WINDOW, :], tc)

            @pl.when(n > 0)
            def _():
                dkv_ref[:, 0:D_KV] = _bf(carry[:, 0:D_KV] + dkp)
                dkv_ref[:, D_KV:2 * D_KV] = _bf(carry[:, D_KV:2 * D_KV] + dvbuf[0:WINDOW, :])

            carry[:, 0:D_KV] = dkc
            carry[:, D_KV:2 * D_KV] = dvbuf[WINDOW:2 * WINDOW, :]

        @pl.when(n == nb)
        def _():
            dkv_ref[...] = _bf(carry[...])

    last = nb - 1
    cur = lambda wd: pl.BlockSpec((WINDOW, wd), lambda n: (jnp.minimum(n, last), 0))
    prv = lambda wd: pl.BlockSpec((WINDOW, wd), lambda n: (jnp.maximum(jnp.minimum(n, last) - 1, 0), 0))
    return pl.pallas_call(
        body, name="swa_bwd", grid=(nb + 1,),
        in_specs=[cur(D_ATT), cur(D_ATT), cur(D_ATT), cur(2 * D_KV), prv(2 * D_KV), cur(D_ATT),
                  pl.BlockSpec((ATT_QH, WINDOW), lambda n: (0, jnp.minimum(n, last))), cur(1), prv(1),
                  pl.BlockSpec((1, 2 * ATT_HD), lambda n: (0, 0)), pl.BlockSpec((1, ATT_QH), lambda n: (0, 0))],
        out_specs=[cur(D_ATT), cur(D_ATT),
                   pl.BlockSpec((WINDOW, 2 * D_KV), lambda n: (jnp.maximum(n - 1, 0), 0)),
                   pl.BlockSpec((1, ATT_QH), lambda n: (0, 0))],
        out_shape=[jax.ShapeDtypeStruct((L, D_ATT), BF16), jax.ShapeDtypeStruct((L, D_ATT), BF16),
                   jax.ShapeDtypeStruct((L, 2 * D_KV), BF16), jax.ShapeDtypeStruct((1, ATT_QH), F32)],
        scratch_shapes=[pltpu.VMEM((WINDOW, 2 * D_KV), F32), pltpu.VMEM((D_ATT, WINDOW), F32),
                        pltpu.VMEM((2 * WINDOW, D_KV), F32), pltpu.VMEM((2 * WINDOW, D_KV), F32)],
        compiler_params=_cparams(("arbitrary",)),
    )(dy, q, g, kv, kv, o, lse, pos, pos, inv, sinks)


def _out_ln_loss(y_ssd, y_att, x, target, w_out, ln_g, ln_b):
    L = x.shape[0]
    tm = min(ROW_TILE, L)
    nt = L // tm
    inv_d = 1.0 / D_MODEL

    def body(ys_ref, ya_ref, x_ref, t_ref, w_ref, g_ref, b_ref, dr_ref, dys_ref, dya_ref, loss_ref, gg_ref, gb_ref,
             gwo_ref, acc_ref):
        i = pl.program_id(0)

        @pl.when(i == 0)
        def _():
            loss_ref[...] = jnp.zeros_like(loss_ref)
            gg_ref[...] = jnp.zeros_like(gg_ref)
            gb_ref[...] = jnp.zeros_like(gb_ref)
            acc_ref[...] = jnp.zeros_like(acc_ref)

        halves = [slice(0, tm // 2), slice(tm // 2, tm)]
        hs = [_dot(_bf(ys_ref[rs, :]), w_ref[0:D_SSD, :]) + _dot(_bf(ya_ref[rs, :]), w_ref[D_SSD:D_MIX, :]) for rs in halves]
        gam = g_ref[...]
        for rs, h in zip(halves, hs):
            r = ALPHA * x_ref[rs, :] + h
            mu = jnp.mean(r, axis=-1, keepdims=True)
            xc = r - mu
            rstd = lax.rsqrt(jnp.mean(xc * xc, axis=-1, keepdims=True) + LN_EPS)
            xhat = xc * rstd
            diff = xhat * gam + b_ref[...] - t_ref[rs, :]
            part = jnp.sum(jnp.sum(diff * diff, axis=-1, keepdims=True), axis=0, keepdims=True)
            loss_ref[...] += (0.5 * inv_d) * part
            dout = diff * inv_d
            gg_ref[...] += jnp.sum(dout * xhat, axis=0, keepdims=True)
            gb_ref[...] += jnp.sum(dout, axis=0, keepdims=True)
            dxh = dout * gam
            dr_ref[rs, :] = rstd * (dxh - jnp.mean(dxh, axis=-1, keepdims=True)
                                    - xhat * jnp.mean(dxh * xhat, axis=-1, keepdims=True))
        for rs in halves:
            drh = _bf(dr_ref[rs, :])
            dys_ref[rs, :] = _dot_nt(drh, w_ref[0:D_SSD, :])
            dya_ref[rs, :] = _dot_nt(drh, w_ref[D_SSD:D_MIX, :])
        drb = _bf(dr_ref[...])
        acc_ref[0:D_SSD, :] += _dot_tn(_bf(ys_ref[...]), drb)
        acc_ref[D_SSD:D_MIX, :] += _dot_tn(_bf(ya_ref[...]), drb)

        @pl.when(i == nt - 1)
        def _():
            gwo_ref[...] = _bf(acc_ref[...])

    row = pl.BlockSpec((tm, D_MODEL), lambda i: (i, 0))
    vec = pl.BlockSpec((1, D_MODEL), lambda i: (0, 0))
    return pl.pallas_call(
        body, name="out_ln_loss", grid=(nt,),
        in_specs=[row, row, row, row, pl.BlockSpec((D_MIX, D_MODEL), lambda i: (0, 0), pipeline_mode=pl.Buffered(1)), vec, vec],
        out_specs=[row, row, row, pl.BlockSpec((1, 128), lambda i: (0, 0)), vec, vec,
                   pl.BlockSpec((D_MIX, D_MODEL), lambda i: (0, 0))],
        out_shape=[jax.ShapeDtypeStruct((L, D_MODEL), F32)] * 3 + [jax.ShapeDtypeStruct((1, 128), F32)]
        + [jax.ShapeDtypeStruct((1, D_MODEL), F32)] * 2 + [jax.ShapeDtypeStruct((D_MIX, D_MODEL), BF16)],
        scratch_shapes=[pltpu.VMEM((D_MIX, D_MODEL), F32)],
        compiler_params=_cparams(("arbitrary",)),
    )(y_ssd, y_att, x, target, w_out, ln_g, ln_b)


def _local_step(x, pos, target, w, get_w_out, token, conv_w, conv_b, dt_bias, a_log, d_skip, norm_w, sinks, ln_g, ln_b):
    inv8 = ROPE_THETA ** (-jnp.arange(0, ROPE_DIM, 2, dtype=F32) / ROPE_DIM)
    inv = jnp.tile(jnp.concatenate([inv8, inv8, jnp.zeros((ATT_HD - ROPE_DIM,), F32)]), 2).reshape(1, 2 * ATT_HD)
    inv = inv + token

    z, g, q, xbc, kv, dtp, xb = _in_proj(x, w, pos, inv)
    y_ssd, y_pre, prev = _ssd_fwd(z, xbc, dtp, conv_w, conv_b, dt_bias, a_log, d_skip, norm_w)
    y_att, o, lse = _swa_fwd(q, g, kv, sinks)
    w_out = get_w_out(lse)
    dr, dy_ssd, dy_att, loss, g_ln_g, g_ln_b, gw_out = _out_ln_loss(y_ssd, y_att, x, target, w_out, ln_g, ln_b)
    w_out_red = _reduce_w_out_start(gw_out.reshape(N_CHIPS, W_OUT_ROWS, D_MODEL), loss)
    inv = inv + w_out_red[16][0:1, :]
    dq, dg, dkv, g_sinks = _swa_bwd(dy_att, q, g, kv, o, lse, pos, inv, sinks)
    dz, dxbc, ddt, g_conv_w, g_conv_b, g_dt_bias, g_a_log, g_d_skip, g_norm_w = _ssd_bwd(
        dy_ssd, z, y_pre, xbc, dtp, prev, conv_w, conv_b, dt_bias, a_log, d_skip, norm_w)
    gw_z, gw_g, gw_q = _matmuls_tn([dz, dg, dq], xb, "gw_zgq")
    gw_xbc, gw_kv, gw_dt = _matmuls_tn([dxbc, dkv, ddt], xb, "gw_xbc_kv_dt")
    gw_in = jnp.concatenate([gw_z, gw_xbc, gw_dt[0:SSD_HEADS], gw_q, gw_kv, gw_g], axis=0)
    small = dict(conv_w=g_conv_w, conv_b=g_conv_b, dt_bias=g_dt_bias, a_log=g_a_log, d_skip=g_d_skip,
                 ssd_norm_w=g_norm_w, attn_sinks=g_sinks, ln_g=g_ln_g, ln_b=g_ln_b)
    return loss, (dr, dz, dg, dq, dxbc, dkv, ddt, w), gw_in, w_out_red, small


def _mesh_pos():
    return lax.axis_index("x"), lax.axis_index("y"), lax.axis_index("c")


def _gather_weights(w_in_s, conv_w_s):
    hr = w_in_s.shape[0] // 2
    qa = 336
    quarters = ((0, qa), (qa, hr - qa))

    def body(win_ref, cw_ref, owin_ref, ocw_ref, stage, send_sems, recv_sems, small_send, small_recv, local_sems):
        x, y, c = _mesh_pos()
        me = 2 * x + y
        sibling = (x, y, 1 - c)
        xn, yn, dg = (1 - x, y), (x, 1 - y), (1 - x, 1 - y)
        chips = [xn, yn, dg]
        load = pltpu.make_async_copy(win_ref, stage, local_sems.at[1])
        load.start()
        locals_ = [pltpu.make_async_copy(cw_ref, ocw_ref.at[me], local_sems.at[0])]
        for cp in locals_:
            cp.start()
        started = []

        def piece(ref, chip, half, q):
            off, n = quarters[q]
            return ref.at[2 * chip[0] + chip[1]].at[pl.ds(half * hr + off, n), :]

        def mine(q):
            off, n = quarters[q]
            return win_ref.at[pl.ds(c * hr + off, n), :]

        def copy(src, dst, k, to):
            return pltpu.make_async_remote_copy(src_ref=src, dst_ref=dst, send_sem=send_sems.at[k], recv_sem=recv_sems.at[k],
                                                device_id=to, device_id_type=MESH)

        def go(cp):
            cp.start()
            started.append(cp)

        go(copy(mine(0), piece(owin_ref, (x, y), c, 0), 0, (*xn, c)))
        go(copy(mine(1), piece(owin_ref, (x, y), c, 1), 2, (*yn, c)))
        go(copy(mine(1), piece(owin_ref, (x, y), c, 1), 1, (*xn, c)))
        go(copy(mine(0), piece(owin_ref, (x, y), c, 0), 3, (*yn, c)))
        for j, (px, py) in enumerate(chips):
            cp = pltpu.make_async_remote_copy(
                src_ref=cw_ref, dst_ref=ocw_ref.at[me], send_sem=small_send.at[j], recv_sem=small_recv.at[j],
                device_id=(px, py, c), device_id_type=MESH)
            go(cp)
        load.wait()
        store = pltpu.make_async_copy(stage, owin_ref.at[me], local_sems.at[2])
        store.start()
        locals_.append(store)
        arrivals = [(0, xn, 0, (4, (*yn, c))), (2, yn, 1, (5, (*xn, c))), (1, xn, 1, None), (3, yn, 0, None),
                    (4, dg, 0, None), (5, dg, 1, None)]
        for n, (k, chip, q, onward) in enumerate(arrivals):
            blk = piece(owin_ref, chip, c, q)
            copy(blk, blk, k, sibling).wait_recv()
            if onward is not None:
                go(copy(blk, blk, onward[0], onward[1]))
            go(copy(blk, blk, 6 + n, sibling))
        for n, (k, chip, q, onward) in enumerate(arrivals):
            blk = piece(owin_ref, chip, 1 - c, q)
            copy(blk, blk, 6 + n, sibling).wait_recv()
        for j in range(3):
            pltpu.make_async_remote_copy(
                src_ref=cw_ref, dst_ref=ocw_ref.at[me], send_sem=small_send.at[j], recv_sem=small_recv.at[j],
                device_id=sibling, device_id_type=MESH).wait_recv()
        for cp in started:
            cp.wait_send()
        for cp in locals_:
            cp.wait()

    any_spec = pl.BlockSpec(memory_space=pl.ANY)
    return pl.pallas_call(
        body, name="gather_weights",
        in_specs=[any_spec] * 2, out_specs=[any_spec] * 2,
        out_shape=[jax.ShapeDtypeStruct((N_CHIPS,) + a.shape, a.dtype) for a in (w_in_s, conv_w_s)],
        scratch_shapes=[pltpu.VMEM(w_in_s.shape, w_in_s.dtype),
                        pltpu.SemaphoreType.DMA((12,)), pltpu.SemaphoreType.DMA((12,)),
                        pltpu.SemaphoreType.DMA((3,)), pltpu.SemaphoreType.DMA((3,)), pltpu.SemaphoreType.DMA((3,))],
    )(w_in_s, conv_w_s)


_HBM = pl.BlockSpec(memory_space=pltpu.HBM)
_SEM = pl.BlockSpec(memory_space=pltpu.SEMAPHORE)
_EFFECT = pltpu.SideEffectType.DATAFLOW_SIDE_EFFECTING


def _gather_w_out_start(w_out_s, after):
    def body(src_ref, land_ref, after_ref, s0, s1, s2, r0, r1, r2, src_thru, land_thru, token):
        x, y, c = _mesh_pos()
        me = 2 * x + y
        chips = [(1 - x, y), (x, 1 - y), (1 - x, 1 - y)]
        for (px, py), s, r in zip(chips, (s0, s1, s2), (r0, r1, r2)):
            pltpu.make_async_remote_copy(src_ref=src_ref, dst_ref=land_ref.at[me], send_sem=s, recv_sem=r,
                                         device_id=(px, py, c), device_id_type=MESH).start()
        token[...] = jnp.zeros_like(token)

    sem = pltpu.SemaphoreType.DMA(())
    land = lax.empty((N_CHIPS,) + w_out_s.shape, w_out_s.dtype)
    return pl.pallas_call(
        body, name="gather_w_out_start",
        out_shape=(sem,) * 6 + (pltpu.HBM(w_out_s.shape, w_out_s.dtype), pltpu.HBM(land.shape, land.dtype),
                                jax.ShapeDtypeStruct((8, 128), F32)),
        in_specs=(_HBM, _HBM, pl.BlockSpec(memory_space=pl.ANY)),
        out_specs=(_SEM,) * 6 + (_HBM, _HBM, pl.BlockSpec(memory_space=pltpu.VMEM)),
        input_output_aliases={0: 6, 1: 7},
        compiler_params=pltpu.CompilerParams(has_side_effects=_EFFECT),
    )(pltpu.with_memory_space_constraint(w_out_s, pltpu.HBM), pltpu.with_memory_space_constraint(land, pltpu.HBM), after)


def _gather_w_out_wait(sems, src_thru, land_thru, after):
    def body(src_ref, land_ref, s0, s1, s2, r0, r1, r2, after_ref, src_dead, got_ref):
        x, y, c = _mesh_pos()
        chips = [(1 - x, y), (x, 1 - y), (1 - x, 1 - y)]
        for (px, py), s, r in zip(chips, (s0, s1, s2), (r0, r1, r2)):
            cp = pltpu.make_async_remote_copy(src_ref=src_ref, dst_ref=land_ref.at[2 * px + py], send_sem=s, recv_sem=r,
                                              device_id=(px, py, c), device_id_type=MESH)
            cp.wait_send()
            cp.wait_recv()

    return pl.pallas_call(
        body, name="gather_w_out_wait",
        out_shape=(pltpu.HBM(src_thru.shape, src_thru.dtype), pltpu.HBM(land_thru.shape, land_thru.dtype)),
        in_specs=(_HBM, _HBM) + (_SEM,) * 6 + (pl.BlockSpec(memory_space=pl.ANY),),
        out_specs=(_HBM, _HBM), input_output_aliases={0: 0, 1: 1},
        compiler_params=pltpu.CompilerParams(has_side_effects=_EFFECT),
    )(src_thru, land_thru, *sems, after)[1]


def _pair_start(gw_in, after):
    hr = gw_in.shape[1] // 2

    def body(src_ref, land_ref, after_ref, *refs):
        x, y, c = _mesh_pos()
        for j in range(N_CHIPS):
            pltpu.make_async_remote_copy(
                src_ref=src_ref.at[j, pl.ds((1 - c) * hr, hr), :], dst_ref=land_ref.at[j], send_sem=refs[j],
                recv_sem=refs[N_CHIPS + j], device_id=(x, y, 1 - c), device_id_type=MESH).start()
        refs[10][...] = jnp.zeros_like(refs[10])

    sem = pltpu.SemaphoreType.DMA(())
    land = lax.empty((N_CHIPS, hr, D_MODEL), gw_in.dtype)
    return pl.pallas_call(
        body, name="pair_start",
        out_shape=(sem,) * 8 + (pltpu.HBM(gw_in.shape, gw_in.dtype), pltpu.HBM(land.shape, land.dtype),
                                jax.ShapeDtypeStruct((8, 128), F32)),
        in_specs=(_HBM, _HBM, pl.BlockSpec(memory_space=pl.ANY)),
        out_specs=(_SEM,) * 8 + (_HBM, _HBM, pl.BlockSpec(memory_space=pltpu.VMEM)),
        input_output_aliases={0: 8, 1: 9},
        compiler_params=pltpu.CompilerParams(has_side_effects=_EFFECT),
    )(pltpu.with_memory_space_constraint(gw_in, pltpu.HBM), pltpu.with_memory_space_constraint(land, pltpu.HBM), after)


def _pair_wait(sems, gw_thru, land_thru, after):
    hr = land_thru.shape[1]

    def body(src_ref, land_ref, *refs):
        x, y, c = _mesh_pos()
        for j in range(N_CHIPS):
            cp = pltpu.make_async_remote_copy(
                src_ref=src_ref.at[j, pl.ds((1 - c) * hr, hr), :], dst_ref=land_ref.at[j], send_sem=refs[j],
                recv_sem=refs[N_CHIPS + j], device_id=(x, y, 1 - c), device_id_type=MESH)
            cp.wait_send()
            cp.wait_recv()

    return pl.pallas_call(
        body, name="pair_wait",
        out_shape=(pltpu.HBM(gw_thru.shape, gw_thru.dtype), pltpu.HBM(land_thru.shape, land_thru.dtype)),
        in_specs=(_HBM, _HBM) + (_SEM,) * 8 + (pl.BlockSpec(memory_space=pl.ANY),),
        out_specs=(_HBM, _HBM), input_output_aliases={0: 0, 1: 1},
        compiler_params=pltpu.CompilerParams(has_side_effects=_EFFECT),
    )(gw_thru, land_thru, *sems, after)


def _chip_start(s_in, after):
    def body(src_ref, land_ref, after_ref, *refs):
        x, y, c = _mesh_pos()
        me = 2 * x + y
        for j, (px, py) in enumerate([(1 - x, y), (x, 1 - y), (1 - x, 1 - y)]):
            pltpu.make_async_remote_copy(
                src_ref=src_ref.at[2 * px + py], dst_ref=land_ref.at[me], send_sem=refs[j], recv_sem=refs[3 + j],
                device_id=(px, py, c), device_id_type=MESH).start()
        refs[8][...] = jnp.zeros_like(refs[8])

    sem = pltpu.SemaphoreType.DMA(())
    land = lax.empty(s_in.shape, s_in.dtype)
    return pl.pallas_call(
        body, name="chip_start",
        out_shape=(sem,) * 6 + (pltpu.HBM(s_in.shape, s_in.dtype), pltpu.HBM(land.shape, land.dtype),
                                jax.ShapeDtypeStruct((8, 128), F32)),
        in_specs=(_HBM, _HBM, pl.BlockSpec(memory_space=pl.ANY)),
        out_specs=(_SEM,) * 6 + (_HBM, _HBM, pl.BlockSpec(memory_space=pltpu.VMEM)),
        input_output_aliases={0: 6, 1: 7},
        compiler_params=pltpu.CompilerParams(has_side_effects=_EFFECT),
    )(pltpu.with_memory_space_constraint(s_in, pltpu.HBM), pltpu.with_memory_space_constraint(land, pltpu.HBM), after)


def _chip_wait(sems, s_thru, land_thru, after):
    def body(src_ref, land_ref, *refs):
        x, y, c = _mesh_pos()
        for j, (px, py) in enumerate([(1 - x, y), (x, 1 - y), (1 - x, 1 - y)]):
            cp = pltpu.make_async_remote_copy(
                src_ref=src_ref.at[2 * px + py], dst_ref=land_ref.at[2 * px + py], send_sem=refs[j], recv_sem=refs[3 + j],
                device_id=(px, py, c), device_id_type=MESH)
            cp.wait_send()
            cp.wait_recv()

    return pl.pallas_call(
        body, name="chip_wait",
        out_shape=(pltpu.HBM(s_thru.shape, s_thru.dtype), pltpu.HBM(land_thru.shape, land_thru.dtype)),
        in_specs=(_HBM, _HBM) + (_SEM,) * 6 + (pl.BlockSpec(memory_space=pl.ANY),),
        out_specs=(_HBM, _HBM), input_output_aliases={0: 0, 1: 1},
        compiler_params=pltpu.CompilerParams(has_side_effects=_EFFECT),
    )(s_thru, land_thru, *sems, after)


def _pair_share(h_in, small):
    def body(hin_ref, sm_ref, rin_ref, slots_ref, send_sems, recv_sems, small_send, small_recv, local_sem):
        x, y, c = _mesh_pos()
        dev = 4 * x + 2 * y + c
        mine = pltpu.make_async_copy(sm_ref, slots_ref.at[dev], local_sem)
        mine.start()
        share = pltpu.make_async_remote_copy(
            src_ref=hin_ref, dst_ref=rin_ref, send_sem=send_sems.at[0], recv_sem=recv_sems.at[0],
            device_id=(x, y, 1 - c), device_id_type=MESH)
        share.start()
        started = []
        for k in range(1, 8):
            peer = (x ^ ((k >> 2) & 1), y ^ ((k >> 1) & 1), c ^ (k & 1))
            cp = pltpu.make_async_remote_copy(
                src_ref=sm_ref, dst_ref=slots_ref.at[dev], send_sem=small_send.at[k - 1], recv_sem=small_recv.at[k - 1],
                device_id=peer, device_id_type=MESH)
            cp.start()
            started.append(cp)
        share.wait()
        for k in range(1, 8):
            pltpu.make_async_remote_copy(
                src_ref=sm_ref, dst_ref=slots_ref.at[dev], send_sem=small_send.at[k - 1], recv_sem=small_recv.at[k - 1],
                device_id=(x, y, 1 - c), device_id_type=MESH).wait_recv()
        for cp in started:
            cp.wait_send()
        mine.wait()

    any_spec = pl.BlockSpec(memory_space=pl.ANY)
    return pl.pallas_call(
        body, name="pair_share",
        in_specs=[any_spec] * 2, out_specs=[any_spec] * 2,
        out_shape=[jax.ShapeDtypeStruct(h_in.shape, F32), jax.ShapeDtypeStruct((8,) + small.shape, F32)],
        scratch_shapes=[pltpu.SemaphoreType.DMA((1,)), pltpu.SemaphoreType.DMA((1,)),
                        pltpu.SemaphoreType.DMA((7,)), pltpu.SemaphoreType.DMA((7,)), pltpu.SemaphoreType.DMA],
    )(h_in, small)


def _reduce_w_out_start(slabs, after):
    def body(src_ref, land_ref, after_ref, *refs):
        x, y, c = _mesh_pos()
        me = 4 * x + 2 * y + c
        for k in range(1, 8):
            px, py, pc = x ^ ((k >> 2) & 1), y ^ ((k >> 1) & 1), c ^ (k & 1)
            pltpu.make_async_remote_copy(src_ref=src_ref.at[2 * px + py], dst_ref=land_ref.at[me], send_sem=refs[k - 1],
                                         recv_sem=refs[6 + k], device_id=(px, py, pc), device_id_type=MESH).start()
        refs[16][...] = jnp.zeros_like(refs[16])

    sem = pltpu.SemaphoreType.DMA(())
    land = lax.empty((8,) + slabs.shape[1:], slabs.dtype)
    return pl.pallas_call(
        body, name="reduce_w_out_start",
        out_shape=(sem,) * 14 + (pltpu.HBM(slabs.shape, slabs.dtype), pltpu.HBM(land.shape, land.dtype),
                                 jax.ShapeDtypeStruct((8, 128), F32)),
        in_specs=(_HBM, _HBM, pl.BlockSpec(memory_space=pl.ANY)),
        out_specs=(_SEM,) * 14 + (_HBM, _HBM, pl.BlockSpec(memory_space=pltpu.VMEM)),
        input_output_aliases={0: 14, 1: 15},
        compiler_params=pltpu.CompilerParams(has_side_effects=_EFFECT),
    )(pltpu.with_memory_space_constraint(slabs, pltpu.HBM), pltpu.with_memory_space_constraint(land, pltpu.HBM), after)


def _reduce_w_out_wait(sems, slabs_thru, land_thru, after):
    def body(src_ref, land_ref, *refs):
        x, y, c = _mesh_pos()
        for k in range(1, 8):
            px, py, pc = x ^ ((k >> 2) & 1), y ^ ((k >> 1) & 1), c ^ (k & 1)
            cp = pltpu.make_async_remote_copy(
                src_ref=src_ref.at[2 * px + py], dst_ref=land_ref.at[4 * px + 2 * py + pc], send_sem=refs[k - 1],
                recv_sem=refs[6 + k], device_id=(px, py, pc), device_id_type=MESH)
            cp.wait_send()
            cp.wait_recv()

    return pl.pallas_call(
        body, name="reduce_w_out_wait",
        out_shape=(pltpu.HBM(slabs_thru.shape, slabs_thru.dtype), pltpu.HBM(land_thru.shape, land_thru.dtype)),
        in_specs=(_HBM, _HBM) + (_SEM,) * 14 + (pl.BlockSpec(memory_space=pl.ANY),),
        out_specs=(_HBM, _HBM), input_output_aliases={0: 0, 1: 1},
        compiler_params=pltpu.CompilerParams(has_side_effects=_EFFECT),
    )(slabs_thru, land_thru, *sems, after)


def _pair_add(g, recv, core, name):
    _, rows, C = recv.shape
    tc = 256

    def body(core_ref, g_ref, r_ref, o_ref):
        o_ref[...] = _bf(g_ref[...].astype(F32) + r_ref[...].astype(F32))

    spec = pl.BlockSpec((1, rows, tc), lambda j, i, core: (j, 0, i))
    return pl.pallas_call(
        body, name=name,
        grid_spec=pltpu.PrefetchScalarGridSpec(
            num_scalar_prefetch=1, grid=(N_CHIPS, C // tc),
            in_specs=[pl.BlockSpec((1, rows, tc), lambda j, i, core: (j, core[0], i)), spec], out_specs=spec),
        out_shape=jax.ShapeDtypeStruct((N_CHIPS, rows, C), BF16),
        compiler_params=_cparams(("parallel", "parallel")),
    )(core, g, recv)


def _chip_add(own, parts, chip, name):
    _, rows, C = parts.shape
    tc = 256

    def body(chip_ref, own_ref, r0, r1, r2, r3, o_ref):
        acc = None
        for j, r in enumerate((r0, r1, r2, r3)):
            term = jnp.where(chip_ref[0] == j, own_ref[0], r[0]).astype(F32)
            acc = term if acc is None else acc + term
        o_ref[...] = acc

    def slab(j):
        return pl.BlockSpec((1, rows, tc), lambda i, chip: (jnp.where(chip[0] == j, (j + 1) % N_CHIPS, j), 0, i))

    return pl.pallas_call(
        body, name=name,
        grid_spec=pltpu.PrefetchScalarGridSpec(
            num_scalar_prefetch=1, grid=(C // tc,),
            in_specs=[pl.BlockSpec((1, rows, tc), lambda i, chip: (chip[0], 0, i))] + [slab(j) for j in range(N_CHIPS)],
            out_specs=pl.BlockSpec((rows, tc), lambda i, chip: (0, i))),
        out_shape=jax.ShapeDtypeStruct((rows, C), F32),
        compiler_params=_cparams(("parallel",)),
    )(chip, own, parts, parts, parts, parts)


def _adamw_math(w, g, m, v):
    m = ADAM_B1 * m + (1.0 - ADAM_B1) * g
    v = ADAM_B2 * v + (1.0 - ADAM_B2) * (g * g)
    m_hat = m / (1.0 - ADAM_B1 ** ADAM_STEP)
    v_hat = v / (1.0 - ADAM_B2 ** ADAM_STEP)
    delta = -ADAM_LR * (m_hat / (jnp.sqrt(v_hat) + ADAM_EPS) + ADAM_WD * w)
    return delta, m, v


def _adamw_rows(w, g_own, g_sib, m, v, core, name):
    R, C = w.shape[0], w.shape[-1]
    rows = g_own.shape[0]
    step = 256
    chunks = [(r, min(step, R - r)) for r in range(0, R, step)]
    sub = 64

    def body(core_ref, w_hbm, go_hbm, gs_hbm, m_hbm, v_hbm, d_hbm, nm_hbm, nv_hbm, g_hbm,
             wbuf, mbuf, vbuf, gbuf, dbuf, nmbuf, nvbuf, in_sems, g_sems, out_sems):
        c = core_ref[0]
        flat = lambda ref: ref.at[:, 0, :]
        g_in = [pltpu.make_async_copy(go_hbm, gbuf.at[pl.ds(pl.multiple_of(c * rows, 8), rows), :], g_sems.at[0]),
                pltpu.make_async_copy(gs_hbm, gbuf.at[pl.ds(pl.multiple_of((1 - c) * rows, 8), rows), :], g_sems.at[1])]
        for cp in g_in:
            cp.start()
        loads = []
        for k, (r0, n) in enumerate(chunks):
            cps = [pltpu.make_async_copy(flat(src).at[pl.ds(r0, n), :], dst.at[pl.ds(r0, n), :], in_sems.at[a, k])
                   for a, (src, dst) in enumerate(((w_hbm, wbuf), (m_hbm, mbuf), (v_hbm, vbuf)))]
            for cp in cps:
                cp.start()
            loads.append(cps)
        for cp in g_in:
            cp.wait()
        stores = []
        for k, (r0, n) in enumerate(chunks):
            for cp in loads[k]:
                cp.wait()

            def update(rs):
                g = gbuf[rs, :]
                dl, nm, nv = _adamw_math(wbuf[rs, :], g, mbuf[rs, :], vbuf[rs, :])
                dbuf[rs, :] = dl
                nmbuf[rs, :] = nm
                nvbuf[rs, :] = nv

            if n % sub == 0:
                def block(i, carry, r0=r0):
                    update(pl.ds(pl.multiple_of(r0 + i * sub, 8), sub))
                    return carry
                lax.fori_loop(0, n // sub, block, 0)
            else:
                update(pl.ds(r0, n))
            cps = [pltpu.make_async_copy(src.at[pl.ds(r0, n), :], flat(dst).at[pl.ds(r0, n), :], out_sems.at[a, k])
                   for a, (src, dst) in enumerate(((dbuf, d_hbm), (nmbuf, nm_hbm), (nvbuf, nv_hbm), (gbuf, g_hbm)))]
            for cp in cps:
                cp.start()
            stores += cps
        for cp in stores:
            cp.wait()

    any_spec = pl.BlockSpec(memory_space=pl.ANY)
    dense = pltpu.VMEM((R, C), F32)
    return pl.pallas_call(
        body, name=name,
        grid_spec=pltpu.PrefetchScalarGridSpec(
            num_scalar_prefetch=1, grid=(1,),
            in_specs=[any_spec] * 5, out_specs=[any_spec] * 4,
            scratch_shapes=[dense, dense, dense, pltpu.VMEM((2 * rows, C), F32), dense, dense, dense,
                            pltpu.SemaphoreType.DMA((3, len(chunks))), pltpu.SemaphoreType.DMA((2,)),
                            pltpu.SemaphoreType.DMA((4, len(chunks)))]),
        out_shape=[jax.ShapeDtypeStruct(w.shape, F32)] * 4,
        compiler_params=_cparams(),
    )(core, w, g_own, g_sib, m, v)


def _adamw_sum8(w, slabs, land, m, v, ids, name):
    R, C = w.shape
    tc = 128

    def body(ids_ref, w_ref, own_ref, *refs):
        lrefs, (m_ref, v_ref, d_ref, nm_ref, nv_ref, g_ref) = refs[:8], refs[8:]
        g = None
        for d, l_ref in enumerate(lrefs):
            term = jnp.where(ids_ref[0] == d, own_ref[0], l_ref[0]).astype(F32)
            g = term if g is None else g + term
        dl, nm, nv = _adamw_math(w_ref[...], g, m_ref[...], v_ref[...])
        d_ref[...] = dl
        nm_ref[...] = nm
        nv_ref[...] = nv
        g_ref[...] = g

    def slot(d):
        return pl.BlockSpec((1, R, tc), lambda i, ids: (jnp.where(ids[0] == d, (d + 1) % 8, d), 0, i))

    spec = pl.BlockSpec((R, tc), lambda i, ids: (0, i))
    return pl.pallas_call(
        body, name=name,
        grid_spec=pltpu.PrefetchScalarGridSpec(
            num_scalar_prefetch=1, grid=(C // tc,),
            in_specs=[spec, pl.BlockSpec((1, R, tc), lambda i, ids: (ids[1], 0, i))] + [slot(d) for d in range(8)]
            + [spec, spec],
            out_specs=[spec] * 4),
        out_shape=[jax.ShapeDtypeStruct((R, C), F32)] * 4,
        compiler_params=_cparams(("parallel",)),
    )(ids, w, slabs, *([land] * 8), m, v)


SMALL_NAMES = ("conv_b", "ssd_norm_w", "ln_g", "ln_b", "dt_bias", "a_log", "d_skip", "attn_sinks")
SMALL_FIELDS = ((4, 0, D_XBC), (5, 0, D_SSD), (6, 0, D_MODEL), (7, 0, D_MODEL), (5, 1024, SSD_HEADS), (5, 1152, SSD_HEADS),
                (5, 1280, SSD_HEADS), (5, 1408, ATT_QH))
LOSS_FIELD = (6, 1024, 128)
K_SMALL = D_XBC


def _pack_small(g_conv_w, vecs, loss):
    def body(cw_ref, *refs):
        o_ref = refs[-1]
        o_ref[...] = jnp.zeros_like(o_ref)
        o_ref[0:CONV_K, 0:D_XBC] = cw_ref[...]
        for v_ref, (row, off, n) in zip(refs[:-2], SMALL_FIELDS):
            o_ref[row:row + 1, off:off + n] = v_ref[...]
        o_ref[LOSS_FIELD[0]:LOSS_FIELD[0] + 1, LOSS_FIELD[1]:LOSS_FIELD[1] + LOSS_FIELD[2]] = refs[-2][...]

    return pl.pallas_call(
        body, name="pack_small", out_shape=jax.ShapeDtypeStruct((8, K_SMALL), F32), compiler_params=_cparams(),
    )(g_conv_w, *vecs, loss)


def _adamw_small(slots, chip, conv_w, m_conv_w, v_conv_w, params, moms, vars_):
    n_vec = len(SMALL_NAMES)

    def body(chip_ref, s_ref, *refs):
        ins = refs[:3 * (n_vec + 1)]
        outs = refs[3 * (n_vec + 1):-1]
        tot_ref = refs[-1]
        tot = s_ref[0]
        for d in range(1, 8):
            tot = tot + s_ref[d]
        outs[0][...] = tot[LOSS_FIELD[0]:LOSS_FIELD[0] + 1, LOSS_FIELD[1]:LOSS_FIELD[1] + 1]
        off = pl.multiple_of(chip_ref[0] * CONV_COLS, 128)
        tot_ref[...] = tot
        grads = [tot_ref[0:CONV_K, pl.ds(off, CONV_COLS)]]
        grads += [tot[row:row + 1, o:o + n] for row, o, n in SMALL_FIELDS]
        for k, g in enumerate(grads):
            w_ref, m_ref, v_ref = ins[3 * k:3 * k + 3]
            full = (0,) if k == 0 else (Ellipsis,)
            d, nm, nv = _adamw_math(w_ref[full], g, m_ref[full], v_ref[full])
            for o_ref, val in zip(outs[1 + 4 * k:5 + 4 * k], (g, d, nm, nv)):
                o_ref[full] = val

    args = [conv_w, m_conv_w, v_conv_w]
    for w, m, v in zip(params, moms, vars_):
        args += [w, m, v]
    shapes = [jax.ShapeDtypeStruct((1, 1), F32)] + [jax.ShapeDtypeStruct(conv_w.shape, F32)] * 4
    for w in params:
        shapes += [jax.ShapeDtypeStruct(w.shape, F32)] * 4
    vmem = pl.BlockSpec(memory_space=pltpu.VMEM)
    return pl.pallas_call(
        body, name="adamw_small",
        grid_spec=pltpu.PrefetchScalarGridSpec(
            num_scalar_prefetch=1, grid=(1,),
            in_specs=[pl.BlockSpec(slots.shape, lambda i, chip: (0, 0, 0))] + [vmem] * len(args),
            out_specs=[vmem] * len(shapes), scratch_shapes=[pltpu.VMEM((8, K_SMALL), F32)]),
        out_shape=shapes, compiler_params=_cparams(),
    )(chip, slots, *args)


def kernel(x, positions, w_in, conv_w, conv_b, dt_bias, a_log, d_skip, ssd_norm_w, attn_sinks, w_out, ln_g, ln_b, loss_target, m_w_in, m_conv_w, m_conv_b, m_dt_bias, m_a_log, m_d_skip, m_ssd_norm_w, m_attn_sinks, m_w_out, m_ln_g, m_ln_b, v_w_in, v_conv_w, v_conv_b, v_dt_bias, v_a_log, v_d_skip, v_ssd_norm_w, v_attn_sinks, v_w_out, v_ln_g, v_ln_b):
    mx, my, mc = _mesh_pos()
    chip = 2 * mx + my
    L = x.shape[1]

    conv_w_s8 = jnp.pad(conv_w[0], ((0, 8 - CONV_K), (0, 0)))
    pad_rows = ((0, SLAB_ROWS - W_IN_COLS), (0, 0))
    w_in_t = w_in[0].T
    w_in_b, w_out_b = jnp.pad(_bf(w_in_t), pad_rows), _bf(w_out[0])
    ag_in, ag_cw = _gather_weights(w_in_b, conv_w_s8)
    started = _gather_w_out_start(w_out_b, ag_cw)
    own = (jnp.arange(N_CHIPS) == chip)[:, None, None]

    def get_w_out(after):
        landed = _gather_w_out_wait(started[0:6], started[6], started[7], after)
        return jnp.where(own, w_out_b[None], landed).reshape(D_MIX, D_MODEL)

    w_full = jnp.concatenate([ag_in[j, 0:W_IN_COLS] for j in range(N_CHIPS)], axis=0)
    w = jnp.concatenate([
        w_full[O_Z:O_Z + D_SSD], w_full[O_G:O_G + D_ATT], w_full[O_Q:O_Q + D_ATT],
        w_full[O_XBC:O_XBC + D_XBC], w_full[O_K:O_K + 2 * D_KV], w_full[O_DT:O_DT + SSD_HEADS],
        jnp.zeros((DT_PAD - SSD_HEADS, D_MODEL), BF16)], axis=0)
    conv_w_full = jnp.concatenate([ag_cw[j, 0:CONV_K] for j in range(N_CHIPS)], axis=1)

    loss_part, gx_args, gw_in, w_out_red, small = _local_step(
        x[0], positions[0].reshape(L, 1), loss_target[0], w, get_w_out, started[8][0:1, :], conv_w_full,
        conv_b, dt_bias, a_log, d_skip, ssd_norm_w, attn_sinks, ln_g, ln_b)

    packed = _pack_small(small["conv_w"], [small[n] for n in SMALL_NAMES], loss_part)
    core_id = mc.reshape(1).astype(jnp.int32)
    chip_id = chip.reshape(1).astype(jnp.int32)
    ids = jnp.stack([4 * mx + 2 * my + mc, chip]).astype(jnp.int32)
    slabs = jnp.stack([jnp.pad(gw_in[W_IN_COLS * j:W_IN_COLS * (j + 1)], pad_rows) for j in range(N_CHIPS)])
    w_in_red = _pair_start(slabs, packed)
    grad_x = _grad_x(*gx_args, w_in_red[10], 0)
    gw_in_slabs, recv_in = _pair_wait(w_in_red[0:8], w_in_red[8], w_in_red[9], grad_x[0:8, 0:128])
    s_in = _pair_add(gw_in_slabs, recv_in, core_id, "pair_add_in")
    chip_red = _chip_start(s_in, packed)
    grad_x = _grad_x(*gx_args, chip_red[8], 1, grad_x)
    own_slabs, landed = _reduce_w_out_wait(w_out_red[0:14], w_out_red[14], w_out_red[15], grad_x)
    out_t = _adamw_sum8(w_out[0], own_slabs, landed, m_w_out[0], v_w_out[0], ids, "adamw_w_out")
    d_w_out, nm_w_out, nv_w_out, g_w_out = [a[None] for a in out_t]
    s_in, r_in = _chip_wait(chip_red[0:6], chip_red[6], chip_red[7], out_t[0])
    h_in = _chip_add(s_in, r_in, chip_id, "chip_add_in")
    sib_in, slots = _pair_share(h_in, packed)

    to_rows = lambda a: jnp.transpose(a, (2, 0, 1))
    in_t = _adamw_rows(to_rows(w_in), h_in, sib_in, to_rows(m_w_in), to_rows(v_w_in), core_id, "adamw_w_in")
    d_w_in, nm_w_in, nv_w_in, g_w_in = [jnp.transpose(a, (1, 2, 0)) for a in in_t]

    params = dict(conv_b=conv_b, ssd_norm_w=ssd_norm_w, ln_g=ln_g, ln_b=ln_b, dt_bias=dt_bias, a_log=a_log,
                  d_skip=d_skip, attn_sinks=attn_sinks)
    moms = dict(conv_b=m_conv_b, ssd_norm_w=m_ssd_norm_w, ln_g=m_ln_g, ln_b=m_ln_b, dt_bias=m_dt_bias, a_log=m_a_log,
                d_skip=m_d_skip, attn_sinks=m_attn_sinks)
    vars_ = dict(conv_b=v_conv_b, ssd_norm_w=v_ssd_norm_w, ln_g=v_ln_g, ln_b=v_ln_b, dt_bias=v_dt_bias, a_log=v_a_log,
                 d_skip=v_d_skip, attn_sinks=v_attn_sinks)
    res = _adamw_small(slots, chip_id, conv_w, m_conv_w, v_conv_w, [params[n] for n in SMALL_NAMES],
                       [moms[n] for n in SMALL_NAMES], [vars_[n] for n in SMALL_NAMES])
    loss = res[0][0, 0]
    grads, delta, new_m, new_v = {}, {}, {}, {}
    for k, n in enumerate(("conv_w",) + SMALL_NAMES):
        grads[n], delta[n], new_m[n], new_v[n] = res[1 + 4 * k:5 + 4 * k]
    for dd, a_in, a_out in ((grads, g_w_in, g_w_out), (delta, d_w_in, d_w_out), (new_m, nm_w_in, nm_w_out),
                            (new_v, nv_w_in, nv_w_out)):
        dd["w_in"] = a_in
        dd["w_out"] = a_out
    order = ("w_in", "conv_w", "conv_b", "dt_bias", "a_log", "d_skip", "ssd_norm_w", "attn_sinks", "w_out", "ln_g", "ln_b")
    return (loss, grad_x[None], *[grads[n] for n in order], *[delta[n] for n in order], *[new_m[n] for n in order],
            *[new_v[n] for n in order])
```

```python
import numpy as np
import jax
import jax.numpy as jnp
from jax import lax
from jax.experimental import pallas as pl
from jax.experimental.pallas import tpu as pltpu

F32 = jnp.float32
BF16 = jnp.bfloat16
MESH = pl.DeviceIdType.MESH

D_MODEL = 1024
D_SSD = 1024
D_ATT = 1024
D_MIX = 2048
SSD_HEADS = 16
SSD_P = 64
SSD_GROUPS = 2
SSD_R = 8
SSD_N = 128
D_BC = 256
D_XBC = 1536
CONV_K = 4
CHUNK = 128
ATT_HD = 64
assert ATT_HD in (4, 16, 64, 256)
ATT_QH = 16
ATT_KVH = 4
ATT_R = 4
D_KV = 256
WINDOW = 128
ROPE_THETA = 500000.0
ROPE_DIM = 16
ALPHA = 2.0 ** 0.25
LN_EPS = 1e-5
RMS_EPS = 1e-5
D_IN_PROJ = 5136
O_Z, O_XBC, O_DT, O_Q, O_K, O_V, O_G = 0, 1024, 2560, 2576, 3600, 3856, 4112
P_Z, P_G, P_Q, P_XBC, P_KV, P_DT, P_END = 0, 1024, 2048, 3072, 4608, 5120, 5248
DT_PAD = 128
N_CHIPS = 4
W_IN_COLS = D_IN_PROJ // N_CHIPS
SLAB_ROWS = 1312
W_OUT_ROWS = D_MIX // N_CHIPS
CONV_COLS = D_XBC // N_CHIPS

ADAM_LR = 0.001
ADAM_B1 = 0.9
ADAM_B2 = 0.999
ADAM_EPS = 1e-08
ADAM_WD = 0.01
ADAM_STEP = 10

VMEM_LIMIT = 56 * 1024 * 1024
ROW_TILE = 512
NEG_BIG = -1e30


def _cparams(sem=None, **kw):
    if sem is not None:
        kw["dimension_semantics"] = sem
    return pltpu.CompilerParams(vmem_limit_bytes=VMEM_LIMIT, **kw)


def _dot(a, b):
    return jnp.dot(a, b, preferred_element_type=F32)


def _dot_nt(a, b):
    return lax.dot_general(a, b, (((1,), (1,)), ((), ())), preferred_element_type=F32)


def _dot_tn(a, b):
    return lax.dot_general(a, b, (((0,), (0,)), ((), ())), preferred_element_type=F32)


def _bf(a):
    return a.astype(BF16)


def _iota2(shape, dim):
    return lax.broadcasted_iota(jnp.int32, shape, dim)


def _three_terms(x):
    hi = _bf(x)
    r = x - hi.astype(F32)
    mid = _bf(r)
    return hi, mid, _bf(r - mid.astype(F32))


def _dot01(m, a):
    return sum(_dot(m, t) for t in _three_terms(a))


def _to_rows(col):
    k = col.shape[1]
    eye = (_iota2((k, k), 0) == _iota2((k, k), 1)).astype(BF16)
    return sum(_dot_nt(eye, t) for t in _three_terms(col))


def _to_cols(row):
    n = row.shape[1]
    eye = (_iota2((n, n), 0) == _iota2((n, n), 1)).astype(BF16)
    return sum(_dot_nt(eye, t) for t in _three_terms(row))


def _sigmoid(x):
    return jax.nn.sigmoid(x)


def _in_proj(x, w, pos, inv):
    L = x.shape[0]
    tm = ROW_TILE
    widths = (D_SSD, D_ATT, D_ATT, D_XBC, 2 * D_KV, DT_PAD)

    def body(x_ref, w_ref, pos_ref, inv_ref, z_ref, g_ref, q_ref, xbc_ref, kv_ref, dt_ref, xb_ref):
        xb = _bf(x_ref[...])
        xb_ref[...] = xb
        tabs = _rope_tables(pos_ref, inv_ref)
        q_ref[...] = _bf(_rope(_dot_nt(xb, w_ref[P_Q:P_Q + D_ATT, :]), tabs))
        kv_ref[:, 0:D_KV] = _bf(_rope(_dot_nt(xb, w_ref[P_KV:P_KV + D_KV, :]), tabs))
        kv_ref[:, D_KV:2 * D_KV] = _bf(_dot_nt(xb, w_ref[P_KV + D_KV:P_KV + 2 * D_KV, :]))
        for o_ref, off, wd in zip((z_ref, g_ref, xbc_ref, dt_ref), (P_Z, P_G, P_XBC, P_DT), (D_SSD, D_ATT, D_XBC, DT_PAD)):
            o_ref[...] = _dot_nt(xb, w_ref[off:off + wd, :])

    row = lambda wd: pl.BlockSpec((tm, wd), lambda i: (i, 0))
    return pl.pallas_call(
        body, name="in_proj", grid=(L // tm,),
        in_specs=[row(D_MODEL), pl.BlockSpec((P_END, D_MODEL), lambda i: (0, 0), pipeline_mode=pl.Buffered(1)), row(1),
                  pl.BlockSpec((1, 2 * ATT_HD), lambda i: (0, 0))],
        out_specs=[row(wd) for wd in widths] + [row(D_MODEL)],
        out_shape=[jax.ShapeDtypeStruct((L, wd), dt) for wd, dt in zip(widths, (F32, F32, BF16, F32, BF16, F32))]
        + [jax.ShapeDtypeStruct((L, D_MODEL), BF16)],
        compiler_params=_cparams(("parallel",)),
    )(x, w, pos, inv)


def _matmuls_tn(a_list, b, name):
    K, N = b.shape
    tk = min(K, 1024)
    n = len(a_list)

    def body(*refs):
        b_ref = refs[n]
        o_refs, acc_refs = refs[n + 1:2 * n + 1], refs[2 * n + 1:]

        @pl.when(pl.program_id(0) == 0)
        def _():
            for acc in acc_refs:
                acc[...] = jnp.zeros_like(acc)

        bb = _bf(b_ref[...])
        for a_ref, o_ref, acc in zip(refs[:n], o_refs, acc_refs):
            total = acc[...] + _dot_tn(_bf(a_ref[...]), bb)
            acc[...] = total
            o_ref[...] = _bf(total)

    return pl.pallas_call(
        body, name=name, grid=(K // tk,),
        in_specs=[pl.BlockSpec((tk, a.shape[1]), lambda k: (k, 0)) for a in a_list] + [pl.BlockSpec((tk, N), lambda k: (k, 0))],
        out_specs=[pl.BlockSpec((a.shape[1], N), lambda k: (0, 0)) for a in a_list],
        out_shape=[jax.ShapeDtypeStruct((a.shape[1], N), BF16) for a in a_list],
        scratch_shapes=[pltpu.VMEM((a.shape[1], N), F32) for a in a_list],
        compiler_params=_cparams(("arbitrary",)),
    )(*a_list, b)


def _grad_x(dr, dz, dg, dq, dxbc, dkv, ddt, w, after, part, prev=None):
    L = dr.shape[0]
    tm = min(ROW_TILE, L // 4)
    first = L // (4 * tm)
    n = first if part == 0 else L // tm - first
    widths = (D_SSD, D_ATT, D_ATT, D_XBC, 2 * D_KV, DT_PAD)
    offs = (P_Z, P_G, P_Q, P_XBC, P_KV, P_DT)

    def body(dr_ref, dz_ref, dg_ref, dq_ref, dxbc_ref, dkv_ref, ddt_ref, w_ref, after_ref, *rest):
        o_ref = rest[-1]
        acc = ALPHA * dr_ref[...]
        for p_ref, off, wd in zip((dz_ref, dg_ref, dq_ref, dxbc_ref, dkv_ref, ddt_ref), offs, widths):
            acc = acc + _dot(_bf(p_ref[...]), w_ref[off:off + wd, :])
        o_ref[...] = acc

    row = lambda wd: pl.BlockSpec((tm, wd), lambda i: (i + part * first, 0))
    ins = [dr, dz, dg, dq, dxbc, dkv, ddt, w, after]
    specs = ([row(D_MODEL)] + [row(wd) for wd in widths]
             + [pl.BlockSpec((P_END, D_MODEL), lambda i: (0, 0), pipeline_mode=pl.Buffered(1)),
                pl.BlockSpec((8, 128), lambda i: (0, 0))])
    if prev is not None:
        ins.append(prev)
        specs.append(pl.BlockSpec(memory_space=pl.ANY))
    return pl.pallas_call(
        body, name="grad_x_%d" % part, grid=(n,),
        in_specs=specs, out_specs=row(D_MODEL),
        out_shape=jax.ShapeDtypeStruct((L, D_MODEL), F32),
        input_output_aliases={} if prev is None else {len(ins) - 1: 0},
        compiler_params=_cparams(("parallel",)),
    )(*ins)


HALO = 16


def _shift_matrix(offsets):
    n = CHUNK + HALO
    m = np.zeros((len(offsets) * CHUNK, 2 * n), np.float32)
    for k, off in enumerate(offsets):
        t = np.arange(CHUNK)
        m[k * CHUNK + t, t + off] = 1.0
        m[k * CHUNK + t, n + t + off] = 1.0
    return jnp.asarray(m, BF16)


def _shifted_rows(first_part, second_part, smat_ref):
    h1, l1 = _hi_lo(first_part)
    h2, l2 = _hi_lo(second_part)
    sh = _dot(smat_ref[...], jnp.concatenate([h1, h2, l1, l2], axis=0))
    return sh[0:CHUNK], sh[CHUNK:2 * CHUNK], sh[2 * CHUNK:3 * CHUNK]


def _ssd_chunk_pre(first, xbc_ref, tail_ref, dt_ref, cw_ref, cb_ref, dtb_ref, alog_ref, smat_ref=None, ext=None):
    tail = jnp.where(first, 0.0, tail_ref[...])
    x = xbc_ref[...]
    if ext is None:
        taps = _shifted_rows(tail, x, smat_ref) + (x,)
    else:
        ext[0:HALO, :] = tail
        ext[HALO:HALO + CHUNK, :] = x
        taps = tuple(ext[pl.ds(HALO - (CONV_K - 1) + k, CHUNK), :] for k in range(CONV_K - 1)) + (x,)
    u = cb_ref[...] + cw_ref[0:1, :] * taps[0]
    for k in range(1, CONV_K):
        u = u + cw_ref[k:k + 1, :] * taps[k]
    sig = _sigmoid(u)
    xbc = u * sig
    dtraw = dt_ref[:, 0:SSD_HEADS] + dtb_ref[...]
    dt = jax.nn.softplus(dtraw)
    A = -jnp.exp(alog_ref[...])
    a = dt * A
    tril = (_iota2((CHUNK, CHUNK), 0) >= _iota2((CHUNK, CHUNK), 1)).astype(BF16)
    acs = _dot01(tril, a)
    acs_row = _to_rows(acs)
    return u, sig, xbc, dtraw, dt, A, acs, acs_row, taps


def _head_expander():
    return (_iota2((SSD_HEADS, D_SSD), 1) // SSD_P == _iota2((SSD_HEADS, D_SSD), 0)).astype(BF16)


def _hi_lo(x):
    hi = _bf(x)
    return hi, _bf(x - hi.astype(F32))


def _expand(v, e):
    hi, lo = _hi_lo(v)
    return _dot(hi, e) + _dot(lo, e)


def _headsum(t, e):
    m = t.shape[0]
    if m < 8:
        t = jnp.broadcast_to(t[0:1], (8, t.shape[1]))
    hi, lo = _hi_lo(t)
    return (_dot_nt(hi, e) + _dot_nt(lo, e))[0:m]


def _ssd_decays(dt, acs, dsk_ref, e):
    alast = acs[CHUNK - 1:CHUNK, :]
    stk = jnp.concatenate([dt, jnp.exp(acs), jnp.exp(alast - acs),
                           jnp.broadcast_to(jnp.exp(alast), (8, SSD_HEADS)),
                           jnp.broadcast_to(dsk_ref[...], (8, SSD_HEADS))], axis=0)
    ex = _expand(stk, e)
    return (ex[0:CHUNK], ex[CHUNK:2 * CHUNK], ex[2 * CHUNK:3 * CHUNK], ex[3 * CHUNK:3 * CHUNK + 1],
            ex[3 * CHUNK + 8:3 * CHUNK + 9])


def _ssd_fwd(z, xbc, dtp, conv_w, conv_b, dt_bias, a_log, d_skip, norm_w):
    L = z.shape[0]
    nc = L // CHUNK
    half = D_SSD // SSD_GROUPS

    def body(z_ref, xbc_ref, tail_ref, dt_ref, cw_ref, cb_ref, dtb_ref, alog_ref, dsk_ref, nw_ref,
             y_ref, ypre_ref, prev_ref, state, ybuf, mbuf, ext):
        c = pl.program_id(0)

        @pl.when(c == 0)
        def _():
            state[...] = jnp.zeros_like(state)

        u, sig, xbcv, dtraw, dt, A, acs, acs_row, _ = _ssd_chunk_pre(
            c == 0, xbc_ref, tail_ref, dt_ref, cw_ref, cb_ref, dtb_ref, alog_ref, ext=ext)
        e = _head_expander()
        dtE, eacsE, dsdE, ealE, dskE = _ssd_decays(dt, acs, dsk_ref, e)
        xs = xbcv[:, 0:D_SSD]
        X = xs * dtE
        prev_ref[0] = state[...]
        causal = _iota2((CHUNK, CHUNK), 0) >= _iota2((CHUNK, CHUNK), 1)
        for g in range(SSD_GROUPS):
            gs = slice(half * g, half * (g + 1))
            Bg = _bf(xbcv[:, D_SSD + SSD_N * g:D_SSD + SSD_N * (g + 1)])
            Cg = _bf(xbcv[:, D_SSD + D_BC + SSD_N * g:D_SSD + D_BC + SSD_N * (g + 1)])
            cb = _dot_nt(Cg, Bg)
            for r in range(SSD_R):
                h = g * SSD_R + r
                seg = acs[:, h:h + 1] - acs_row[h:h + 1, :]
                mbuf[h] = _bf(cb * jnp.where(causal, jnp.exp(jnp.where(causal, seg, 0.0)), 0.0))
            st = state[:, gs]
            ybuf[:, gs] = _dot(Cg, _bf(st)) * eacsE[:, gs] + dskE[:, gs] * xs[:, gs]
            state[:, gs] = st * ealE[:, gs] + _dot_tn(Bg, _bf(X[:, gs] * dsdE[:, gs]))
        Xb = _bf(X)
        for h in range(SSD_HEADS):
            hs = slice(SSD_P * h, SSD_P * (h + 1))
            ybuf[:, hs] += _dot(mbuf[h], Xb[:, hs])
        y = ybuf[...]
        ypre_ref[...] = y
        zv = z_ref[...]
        yf = y * (zv * _sigmoid(zv))
        for g in range(SSD_GROUPS):
            gs = slice(half * g, half * (g + 1))
            yg = yf[:, gs]
            ms = jnp.mean(yg * yg, axis=-1, keepdims=True)
            y_ref[:, gs] = _bf(yg * lax.rsqrt(ms + RMS_EPS) * nw_ref[:, gs])

    full = lambda shape: pl.BlockSpec(shape, lambda c: (0, 0))
    return pl.pallas_call(
        body, name="ssd_fwd", grid=(nc,),
        in_specs=[
            pl.BlockSpec((CHUNK, D_SSD), lambda c: (c, 0)),
            pl.BlockSpec((CHUNK, D_XBC), lambda c: (c, 0)),
            pl.BlockSpec((HALO, D_XBC), lambda c: (jnp.maximum(c * (CHUNK // HALO) - 1, 0), 0)),
            pl.BlockSpec((CHUNK, DT_PAD), lambda c: (c, 0)),
            full((CONV_K, D_XBC)), full((1, D_XBC)), full((1, SSD_HEADS)), full((1, SSD_HEADS)), full((1, SSD_HEADS)),
            full((1, D_SSD)),
        ],
        out_specs=[
            pl.BlockSpec((CHUNK, D_SSD), lambda c: (c, 0)),
            pl.BlockSpec((CHUNK, D_SSD), lambda c: (c, 0)),
            pl.BlockSpec((1, SSD_N, D_SSD), lambda c: (c, 0, 0)),
        ],
        out_shape=[
            jax.ShapeDtypeStruct((L, D_SSD), BF16),
            jax.ShapeDtypeStruct((L, D_SSD), F32),
            jax.ShapeDtypeStruct((nc, SSD_N, D_SSD), F32),
        ],
        scratch_shapes=[
            pltpu.VMEM((SSD_N, D_SSD), F32),
            pltpu.VMEM((CHUNK, D_SSD), F32),
            pltpu.VMEM((SSD_HEADS, CHUNK, CHUNK), BF16),
            pltpu.VMEM((CHUNK + HALO, D_XBC), F32),
        ],
        compiler_params=_cparams(("arbitrary",)),
    )(z, xbc, xbc, dtp, conv_w, conv_b, dt_bias, a_log, d_skip, norm_w)


def _ssd_bwd(dy, z, ypre, xbc, dtp, prev, conv_w, conv_b, dt_bias, a_log, d_skip, norm_w):
    L = z.shape[0]
    nc = L // CHUNK
    half = D_SSD // SSD_GROUPS

    def body(dy_ref, z_ref, ypre_ref, xbc_ref, tail_ref, dt_ref, prev_ref, cw_ref, cb_ref, dtb_ref, alog_ref, dsk_ref,
             nw_ref, smat_ref, smat2_ref, dz_ref, dxbc_ref, ddt_ref, gcw_ref, gcb_ref, gdtb_ref, galog_ref, gdsk_ref,
             gnw_ref, dstate, dhead, dpost, yobuf, bdbuf, lmbuf, dmbuf, cbbuf):
        i = pl.program_id(0)
        c = nc - 1 - i

        @pl.when(i == 0)
        def _():
            dstate[...] = jnp.zeros_like(dstate)
            dhead[...] = jnp.zeros_like(dhead)
            gcw_ref[...] = jnp.zeros_like(gcw_ref)
            gcb_ref[...] = jnp.zeros_like(gcb_ref)
            gdtb_ref[...] = jnp.zeros_like(gdtb_ref)
            galog_ref[...] = jnp.zeros_like(galog_ref)
            gdsk_ref[...] = jnp.zeros_like(gdsk_ref)
            gnw_ref[...] = jnp.zeros_like(gnw_ref)

        u, sig, xbcv, dtraw, dt, A, acs, acs_row, taps = _ssd_chunk_pre(
            c == 0, xbc_ref, tail_ref, dt_ref, cw_ref, cb_ref, dtb_ref, alog_ref, smat_ref)
        e = _head_expander()
        dtE, eacsE, dsdE, ealE, dskE = _ssd_decays(dt, acs, dsk_ref, e)
        alast = acs[CHUNK - 1:CHUNK, :]
        xs = xbcv[:, 0:D_SSD]
        X = xs * dtE
        Xb = _bf(X)

        zv = z_ref[...]
        ypre = ypre_ref[...]
        dyn = dy_ref[...]
        sz = _sigmoid(zv)
        silu_z = zv * sz
        yf = ypre * silu_z
        dyf_parts = []
        for g in range(SSD_GROUPS):
            gs = slice(half * g, half * (g + 1))
            yg = yf[:, gs]
            rstd = lax.rsqrt(jnp.mean(yg * yg, axis=-1, keepdims=True) + RMS_EPS)
            dout = dyn[:, gs]
            gnw_ref[:, gs] += jnp.sum(dout * yg * rstd, axis=0, keepdims=True)
            dyhat = dout * nw_ref[:, gs]
            dyf_parts.append(rstd * (dyhat - yg * (rstd * rstd) * jnp.mean(dyhat * yg, axis=-1, keepdims=True)))
        dyf = jnp.concatenate(dyf_parts, axis=1)
        dz_ref[...] = _bf(dyf * ypre * (sz * (1.0 + zv * (1.0 - sz))))
        dyp = dyf * silu_z
        dyb = _bf(dyp)
        G = dyp * eacsE

        causal = _iota2((CHUNK, CHUNK), 0) >= _iota2((CHUNK, CHUNK), 1)
        ST = prev_ref[0]
        dST = dstate[...]
        for g in range(SSD_GROUPS):
            gs = slice(half * g, half * (g + 1))
            bs = slice(D_SSD + SSD_N * g, D_SSD + SSD_N * (g + 1))
            cs = slice(D_SSD + D_BC + SSD_N * g, D_SSD + D_BC + SSD_N * (g + 1))
            Bg = _bf(xbcv[:, bs])
            Cg = _bf(xbcv[:, cs])
            Gb = _bf(G[:, gs])
            STb = _bf(ST[:, gs])
            dSTb = _bf(dST[:, gs])
            dstate[:, gs] = dST[:, gs] * ealE[:, gs] + _dot_tn(Cg, Gb)
            yobuf[:, gs] = _dot(Cg, STb) * eacsE[:, gs]
            bdbuf[:, gs] = _dot(Bg, dSTb)
            dpost[:, cs] = _dot_nt(Gb, STb)
            dpost[:, bs] = _dot_nt(_bf(X[:, gs] * dsdE[:, gs]), dSTb)
            cbbuf[g] = _dot_nt(Cg, Bg)
            for r in range(SSD_R):
                h = g * SSD_R + r
                seg = acs[:, h:h + 1] - acs_row[h:h + 1, :]
                lmbuf[h] = jnp.where(causal, jnp.exp(jnp.where(causal, seg, 0.0)), 0.0)
        for h in range(SSD_HEADS):
            hs = slice(SSD_P * h, SSD_P * (h + 1))
            Mb = _bf(cbbuf[h // SSD_R] * lmbuf[h])
            dmbuf[h] = _dot_nt(dyb[:, hs], Xb[:, hs])
            dpost[:, hs] = _dot_tn(Mb, dyb[:, hs])
        lane16 = _iota2((1, SSD_HEADS), 1)
        sub16 = _iota2((SSD_HEADS, 1), 0)
        dacs_col = jnp.zeros((CHUNK, SSD_HEADS), F32)
        dacs_row = jnp.zeros((SSD_HEADS, CHUNK), F32)
        for g in range(SSD_GROUPS):
            bs = slice(D_SSD + SSD_N * g, D_SSD + SSD_N * (g + 1))
            cs = slice(D_SSD + D_BC + SSD_N * g, D_SSD + D_BC + SSD_N * (g + 1))
            cb = cbbuf[g]
            dcb = jnp.zeros((CHUNK, CHUNK), F32)
            for r in range(SSD_R):
                h = g * SSD_R + r
                dM = dmbuf[h]
                Lm = lmbuf[h]
                dcb = dcb + dM * Lm
                dseg = dM * (cb * Lm)
                dacs_col = dacs_col + jnp.sum(dseg, axis=-1, keepdims=True) * (lane16 == h).astype(F32)
                dacs_row = dacs_row - jnp.sum(dseg, axis=0, keepdims=True) * (sub16 == h).astype(F32)
            dcbb = _bf(dcb)
            dpost[:, bs] += _dot_tn(dcbb, _bf(xbcv[:, cs]))
            dpost[:, cs] += _dot(dcbb, _bf(xbcv[:, bs]))

        BD = bdbuf[...]
        dX = dpost[:, 0:D_SSD] + dsdE * BD
        dsd = jnp.exp(alast - acs)
        T = _headsum(X * BD, e) * dsd
        dalast = jnp.sum(T, axis=0, keepdims=True) + _headsum(
            jnp.sum(dST * ST, axis=0, keepdims=True), e) * jnp.exp(alast)
        is_last = (_iota2((CHUNK, 1), 0) == CHUNK - 1).astype(F32)
        dacs = dacs_col + _to_cols(dacs_row) + _headsum(dyp * yobuf[...], e) - T + is_last * dalast
        triu = (_iota2((CHUNK, CHUNK), 0) <= _iota2((CHUNK, CHUNK), 1)).astype(BF16)
        da = _dot01(triu, dacs)
        ddt_tot = _headsum(dX * xs, e) + da * A
        galog_ref[...] += jnp.sum(da * dt, axis=0, keepdims=True) * A
        ddtraw = ddt_tot * _sigmoid(dtraw)
        gdtb_ref[...] += jnp.sum(ddtraw, axis=0, keepdims=True)
        gdsk_ref[...] += _headsum(jnp.sum(dyp * xs, axis=0, keepdims=True), e)
        ddt_ref[...] = jnp.zeros_like(ddt_ref)
        ddt_ref[:, 0:SSD_HEADS] = ddtraw
        dpost[:, 0:D_SSD] = dX * dtE + dskE * dyp

        dconv = dpost[...] * (sig * (1.0 + u * (1.0 - sig)))
        gcb_ref[...] += jnp.sum(dconv, axis=0, keepdims=True)
        for k in range(CONV_K):
            gcw_ref[k:k + 1, :] += jnp.sum(dconv * taps[k], axis=0, keepdims=True)
        later = _shifted_rows(dconv, dhead[...], smat2_ref)
        dx = cw_ref[CONV_K - 1:CONV_K, :] * dconv
        for k in range(CONV_K - 1):
            dx = dx + cw_ref[k:k + 1, :] * later[k]
        dxbc_ref[...] = _bf(dx)
        dhead[...] = dconv[0:HALO, :]

    full = lambda shape: pl.BlockSpec(shape, lambda i: (0, 0))
    rev = lambda wd: pl.BlockSpec((CHUNK, wd), lambda i: (nc - 1 - i, 0))
    return pl.pallas_call(
        body, name="ssd_bwd", grid=(nc,),
        in_specs=[
            rev(D_SSD), rev(D_SSD), rev(D_SSD), rev(D_XBC),
            pl.BlockSpec((HALO, D_XBC), lambda i: (jnp.maximum((nc - 1 - i) * (CHUNK // HALO) - 1, 0), 0)),
            rev(DT_PAD),
            pl.BlockSpec((1, SSD_N, D_SSD), lambda i: (nc - 1 - i, 0, 0)),
            full((CONV_K, D_XBC)), full((1, D_XBC)), full((1, SSD_HEADS)), full((1, SSD_HEADS)), full((1, SSD_HEADS)),
            full((1, D_SSD)), full((3 * CHUNK, 2 * (CHUNK + HALO))), full((3 * CHUNK, 2 * (CHUNK + HALO))),
        ],
        out_specs=[
            rev(D_SSD), rev(D_XBC), rev(DT_PAD),
            full((CONV_K, D_XBC)), full((1, D_XBC)), full((1, SSD_HEADS)), full((1, SSD_HEADS)), full((1, SSD_HEADS)),
            full((1, D_SSD)),
        ],
        out_shape=[
            jax.ShapeDtypeStruct((L, D_SSD), BF16), jax.ShapeDtypeStruct((L, D_XBC), BF16),
            jax.ShapeDtypeStruct((L, DT_PAD), F32),
            jax.ShapeDtypeStruct((CONV_K, D_XBC), F32), jax.ShapeDtypeStruct((1, D_XBC), F32),
            jax.ShapeDtypeStruct((1, SSD_HEADS), F32), jax.ShapeDtypeStruct((1, SSD_HEADS), F32),
            jax.ShapeDtypeStruct((1, SSD_HEADS), F32), jax.ShapeDtypeStruct((1, D_SSD), F32),
        ],
        scratch_shapes=[
            pltpu.VMEM((SSD_N, D_SSD), F32),
            pltpu.VMEM((HALO, D_XBC), F32),
            pltpu.VMEM((CHUNK, D_XBC), F32),
            pltpu.VMEM((CHUNK, D_SSD), F32),
            pltpu.VMEM((CHUNK, D_SSD), F32),
            pltpu.VMEM((SSD_HEADS, CHUNK, CHUNK), F32),
            pltpu.VMEM((SSD_HEADS, CHUNK, CHUNK), F32),
            pltpu.VMEM((SSD_GROUPS, CHUNK, CHUNK), F32),
        ],
        compiler_params=_cparams(("arbitrary",)),
    )(dy, z, ypre, xbc, xbc, dtp, prev, conv_w, conv_b, dt_bias, a_log, d_skip, norm_w, _shift_matrix((13, 14, 15)),
      _shift_matrix((3, 2, 1)))


def _rope_tables(pos_ref, inv_ref):
    ang = pos_ref[...].astype(F32) * inv_ref[...]
    d = _iota2((1, 2 * ATT_HD), 1) % ATT_HD
    s = jnp.sin(ang)
    return jnp.cos(ang), jnp.where(d < ROPE_DIM // 2, -s, 0.0), jnp.where((d >= ROPE_DIM // 2) & (d < ROPE_DIM), s, 0.0)


def _rope(t, tabs):
    c, s1, s2 = tabs
    n = t.shape[1]
    rep = n // c.shape[1]
    return (t * jnp.tile(c, (1, rep)) + pltpu.roll(t, n - ROPE_DIM // 2, 1) * jnp.tile(s1, (1, rep))
            + pltpu.roll(t, ROPE_DIM // 2, 1) * jnp.tile(s2, (1, rep)))


def _rope_t(t, tabs):
    c, s1, s2 = tabs
    n = t.shape[1]
    rep = n // c.shape[1]
    return (t * jnp.tile(c, (1, rep)) + pltpu.roll(t * jnp.tile(s1, (1, rep)), ROPE_DIM // 2, 1)
            + pltpu.roll(t * jnp.tile(s2, (1, rep)), n - ROPE_DIM // 2, 1))


def _stack_heads(t, j):
    return jnp.concatenate([t[:, ATT_HD * (j * ATT_R + r):ATT_HD * (j * ATT_R + r + 1)] for r in range(ATT_R)], axis=0)


def _swa_mask_t(first):
    si = _iota2((2 * WINDOW, ATT_R * WINDOW), 0)
    qi = _iota2((2 * WINDOW, ATT_R * WINDOW), 1) % WINDOW
    band = (si > qi) & (si <= qi + WINDOW)
    return band & (jnp.logical_not(first) | (si >= WINDOW))


def _head_rows(ref, j):
    if ref.shape[0] == 1:
        parts = [jnp.broadcast_to(ref[:, j * ATT_R + r:j * ATT_R + r + 1], (1, WINDOW)) for r in range(ATT_R)]
    else:
        parts = [ref[j * ATT_R + r:j * ATT_R + r + 1, :] for r in range(ATT_R)]
    return jnp.concatenate(parts, axis=1)


def _swa_fwd(q, g, kv, sinks):
    L = q.shape[0]
    nb = L // WINDOW
    scale = ATT_HD ** -0.5

    def body(q_ref, g_ref, kvc_ref, kvp_ref, sink_ref, y_ref, o_ref, lse_ref, otbuf):
        n = pl.program_id(0)
        kk = jnp.concatenate([kvp_ref[:, 0:D_KV], kvc_ref[:, 0:D_KV]], axis=0) * scale
        vv = jnp.concatenate([kvp_ref[:, D_KV:2 * D_KV], kvc_ref[:, D_KV:2 * D_KV]], axis=0)
        valid = _swa_mask_t(n == 0)
        qv = q_ref[...]
        for j in range(ATT_KVH):
            js = slice(ATT_HD * j, ATT_HD * (j + 1))
            st = _dot_nt(kk[:, js], _stack_heads(qv, j))
            st = jnp.where(valid, st, NEG_BIG)
            sink = _head_rows(sink_ref, j)
            m = jnp.maximum(jnp.max(st, axis=0, keepdims=True), sink)
            p = jnp.exp(st - m)
            vx = jnp.concatenate([vv[:, js], jnp.ones((2 * WINDOW, ATT_HD), BF16)], axis=1)
            otx = _dot_tn(vx, _bf(p))
            denom = otx[ATT_HD:ATT_HD + 1] + jnp.exp(sink - m)
            ot = otx[0:ATT_HD] * (1.0 / denom)
            lse = m + jnp.log(denom)
            for r in range(ATT_R):
                h = j * ATT_R + r
                otbuf[ATT_HD * h:ATT_HD * (h + 1), :] = ot[:, WINDOW * r:WINDOW * (r + 1)]
                lse_ref[h:h + 1, :] = lse[:, WINDOW * r:WINDOW * (r + 1)]
        o = otbuf[...].T
        o_ref[...] = o
        gv = g_ref[...]
        y_ref[...] = _bf(o * (gv * _sigmoid(gv)))

    cur = lambda wd: pl.BlockSpec((WINDOW, wd), lambda n: (n, 0))
    prv = lambda wd: pl.BlockSpec((WINDOW, wd), lambda n: (jnp.maximum(n - 1, 0), 0))
    return pl.pallas_call(
        body, name="swa_fwd", grid=(nb,),
        in_specs=[cur(D_ATT), cur(D_ATT), cur(2 * D_KV), prv(2 * D_KV), pl.BlockSpec((1, ATT_QH), lambda n: (0, 0))],
        out_specs=[cur(D_ATT), cur(D_ATT), pl.BlockSpec((ATT_QH, WINDOW), lambda n: (0, n))],
        out_shape=[jax.ShapeDtypeStruct((L, D_ATT), BF16), jax.ShapeDtypeStruct((L, D_ATT), F32),
                   jax.ShapeDtypeStruct((ATT_QH, L), F32)],
        scratch_shapes=[pltpu.VMEM((D_ATT, WINDOW), F32)],
        compiler_params=_cparams(("parallel",)),
    )(q, g, kv, kv, sinks)


def _swa_bwd(dy, q, g, kv, o, lse, pos, inv, sinks):
    L = q.shape[0]
    nb = L // WINDOW
    scale = ATT_HD ** -0.5

    def body(dy_ref, q_ref, g_ref, kvc_ref, kvp_ref, o_ref, lse_ref, posc_ref, posp_ref, inv_ref, sink_ref,
             dq_ref, dg_ref, dkv_ref, dsink_ref, carry, dqbuf, dkbuf, dvbuf):
        n = pl.program_id(0)

        @pl.when(n == 0)
        def _():
            dsink_ref[...] = jnp.zeros_like(dsink_ref)

        @pl.when(n < nb)
        def _():
            tc = _rope_tables(posc_ref, inv_ref)
            tp = _rope_tables(posp_ref, inv_ref)
            kk = jnp.concatenate([kvp_ref[:, 0:D_KV], kvc_ref[:, 0:D_KV]], axis=0) * scale
            vv = jnp.concatenate([kvp_ref[:, D_KV:2 * D_KV], kvc_ref[:, D_KV:2 * D_KV]], axis=0)
            valid = _swa_mask_t(n == 0)
            qv = q_ref[...]
            gv = g_ref[...]
            sg = _sigmoid(gv)
            dyv = dy_ref[...]
            ov = o_ref[...]
            dg_ref[...] = _bf(dyv * ov * (sg * (1.0 + gv * (1.0 - sg))))
            do = dyv * (gv * sg)
            dod = do * ov
            ones = jnp.ones((8, ATT_HD), BF16)
            lane16 = _iota2((1, ATT_QH), 1)
            dsink = jnp.zeros((1, ATT_QH), F32)
            for j in range(ATT_KVH):
                js = slice(ATT_HD * j, ATT_HD * (j + 1))
                kj = kk[:, js]
                vj = vv[:, js]
                qs = _stack_heads(qv, j)
                dos = _bf(_stack_heads(do, j))
                hi, lo = _hi_lo(_stack_heads(dod, j))
                delta = (_dot_nt(ones, hi) + _dot_nt(ones, lo))[0:1]
                lse = _head_rows(lse_ref, j)
                st = _dot_nt(kj, qs)
                pt = jnp.exp(jnp.where(valid, st, NEG_BIG) - lse)
                dst = _bf(pt * (_dot_nt(vj, dos) - delta))
                dqt = _dot_tn(kj, dst)
                dkbuf[:, js] = _dot(dst, qs) * scale
                dvbuf[:, js] = _dot(_bf(pt), dos)
                sd = jnp.exp(_head_rows(sink_ref, j) - lse) * delta
                for r in range(ATT_R):
                    h = j * ATT_R + r
                    ls = slice(WINDOW * r, WINDOW * (r + 1))
                    dqbuf[ATT_HD * h:ATT_HD * (h + 1), :] = dqt[:, ls]
                    dsink = dsink - jnp.sum(sd[:, ls], axis=1, keepdims=True) * (lane16 == h).astype(F32)
            dsink_ref[...] += dsink
            dq_ref[...] = _bf(_rope_t(dqbuf[...].T, tc))
            dkp = _rope_t(dkbuf[0:WINDOW, :], tp)
            dkc = _rope_t(dkbuf[WINDOW:2 * WINDOW, :], tc)

            @pl.when(n > 0)
            def _():
                dkv_ref[:, 0:D_KV] = _bf(carry[:, 0:D_KV] + dkp)
                dkv_ref[:, D_KV:2 * D_KV] = _bf(carry[:, D_KV:2 * D_KV] + dvbuf[0:WINDOW, :])

            carry[:, 0:D_KV] = dkc
            carry[:, D_KV:2 * D_KV] = dvbuf[WINDOW:2 * WINDOW, :]

        @pl.when(n == nb)
        def _():
            dkv_ref[...] = _bf(carry[...])

    last = nb - 1
    cur = lambda wd: pl.BlockSpec((WINDOW, wd), lambda n: (jnp.minimum(n, last), 0))
    prv = lambda wd: pl.BlockSpec((WINDOW, wd), lambda n: (jnp.maximum(jnp.minimum(n, last) - 1, 0), 0))
    return pl.pallas_call(
        body, name="swa_bwd", grid=(nb + 1,),
        in_specs=[cur(D_ATT), cur(D_ATT), cur(D_ATT), cur(2 * D_KV), prv(2 * D_KV), cur(D_ATT),
                  pl.BlockSpec((ATT_QH, WINDOW), lambda n: (0, jnp.minimum(n, last))), cur(1), prv(1),
                  pl.BlockSpec((1, 2 * ATT_HD), lambda n: (0, 0)), pl.BlockSpec((1, ATT_QH), lambda n: (0, 0))],
        out_specs=[cur(D_ATT), cur(D_ATT),
                   pl.BlockSpec((WINDOW, 2 * D_KV), lambda n: (jnp.maximum(n - 1, 0), 0)),
                   pl.BlockSpec((1, ATT_QH), lambda n: (0, 0))],
        out_shape=[jax.ShapeDtypeStruct((L, D_ATT), BF16), jax.ShapeDtypeStruct((L, D_ATT), BF16),
                   jax.ShapeDtypeStruct((L, 2 * D_KV), BF16), jax.ShapeDtypeStruct((1, ATT_QH), F32)],
        scratch_shapes=[pltpu.VMEM((WINDOW, 2 * D_KV), F32), pltpu.VMEM((D_ATT, WINDOW), F32),
                        pltpu.VMEM((2 * WINDOW, D_KV), F32), pltpu.VMEM((2 * WINDOW, D_KV), F32)],
        compiler_params=_cparams(("arbitrary",)),
    )(dy, q, g, kv, kv, o, lse, pos, pos, inv, sinks)


def _out_ln_loss(y_ssd, y_att, x, target, w_out, ln_g, ln_b):
    L = x.shape[0]
    tm = min(ROW_TILE, L)
    nt = L // tm
    inv_d = 1.0 / D_MODEL

    def body(ys_ref, ya_ref, x_ref, t_ref, w_ref, g_ref, b_ref, dr_ref, dys_ref, dya_ref, loss_ref, gg_ref, gb_ref,
             gwo_ref, acc_ref):
        i = pl.program_id(0)

        @pl.when(i == 0)
        def _():
            loss_ref[...] = jnp.zeros_like(loss_ref)
            gg_ref[...] = jnp.zeros_like(gg_ref)
            gb_ref[...] = jnp.zeros_like(gb_ref)
            acc_ref[...] = jnp.zeros_like(acc_ref)

        halves = [slice(0, tm // 2), slice(tm // 2, tm)]
        hs = [_dot(_bf(ys_ref[rs, :]), w_ref[0:D_SSD, :]) + _dot(_bf(ya_ref[rs, :]), w_ref[D_SSD:D_MIX, :]) for rs in halves]
        gam = g_ref[...]
        for rs, h in zip(halves, hs):
            r = ALPHA * x_ref[rs, :] + h
            mu = jnp.mean(r, axis=-1, keepdims=True)
            xc = r - mu
            rstd = lax.rsqrt(jnp.mean(xc * xc, axis=-1, keepdims=True) + LN_EPS)
            xhat = xc * rstd
            diff = xhat * gam + b_ref[...] - t_ref[rs, :]
            part = jnp.sum(jnp.sum(diff * diff, axis=-1, keepdims=True), axis=0, keepdims=True)
            loss_ref[...] += (0.5 * inv_d) * part
            dout = diff * inv_d
            gg_ref[...] += jnp.sum(dout * xhat, axis=0, keepdims=True)
            gb_ref[...] += jnp.sum(dout, axis=0, keepdims=True)
            dxh = dout * gam
            dr_ref[rs, :] = rstd * (dxh - jnp.mean(dxh, axis=-1, keepdims=True)
                                    - xhat * jnp.mean(dxh * xhat, axis=-1, keepdims=True))
        for rs in halves:
            drh = _bf(dr_ref[rs, :])
            dys_ref[rs, :] = _dot_nt(drh, w_ref[0:D_SSD, :])
            dya_ref[rs, :] = _dot_nt(drh, w_ref[D_SSD:D_MIX, :])
        drb = _bf(dr_ref[...])
        acc_ref[0:D_SSD, :] += _dot_tn(_bf(ys_ref[...]), drb)
        acc_ref[D_SSD:D_MIX, :] += _dot_tn(_bf(ya_ref[...]), drb)

        @pl.when(i == nt - 1)
        def _():
            gwo_ref[...] = _bf(acc_ref[...])

    row = pl.BlockSpec((tm, D_MODEL), lambda i: (i, 0))
    vec = pl.BlockSpec((1, D_MODEL), lambda i: (0, 0))
    return pl.pallas_call(
        body, name="out_ln_loss", grid=(nt,),
        in_specs=[row, row, row, row, pl.BlockSpec((D_MIX, D_MODEL), lambda i: (0, 0), pipeline_mode=pl.Buffered(1)), vec, vec],
        out_specs=[row, row, row, pl.BlockSpec((1, 128), lambda i: (0, 0)), vec, vec,
                   pl.BlockSpec((D_MIX, D_MODEL), lambda i: (0, 0))],
        out_shape=[jax.ShapeDtypeStruct((L, D_MODEL), F32)] * 3 + [jax.ShapeDtypeStruct((1, 128), F32)]
        + [jax.ShapeDtypeStruct((1, D_MODEL), F32)] * 2 + [jax.ShapeDtypeStruct((D_MIX, D_MODEL), BF16)],
        scratch_shapes=[pltpu.VMEM((D_MIX, D_MODEL), F32)],
        compiler_params=_cparams(("arbitrary",)),
    )(y_ssd, y_att, x, target, w_out, ln_g, ln_b)


def _local_step(x, pos, target, w, get_w_out, token, conv_w, conv_b, dt_bias, a_log, d_skip, norm_w, sinks, ln_g, ln_b):
    inv8 = ROPE_THETA ** (-jnp.arange(0, ROPE_DIM, 2, dtype=F32) / ROPE_DIM)
    inv = jnp.tile(jnp.concatenate([inv8, inv8, jnp.zeros((ATT_HD - ROPE_DIM,), F32)]), 2).reshape(1, 2 * ATT_HD)
    inv = inv + token

    z, g, q, xbc, kv, dtp, xb = _in_proj(x, w, pos, inv)
    y_ssd, y_pre, prev = _ssd_fwd(z, xbc, dtp, conv_w, conv_b, dt_bias, a_log, d_skip, norm_w)
    y_att, o, lse = _swa_fwd(q, g, kv, sinks)
    w_out = get_w_out(lse)
    dr, dy_ssd, dy_att, loss, g_ln_g, g_ln_b, gw_out = _out_ln_loss(y_ssd, y_att, x, target, w_out, ln_g, ln_b)
    w_out_red = _reduce_w_out_start(gw_out.reshape(N_CHIPS, W_OUT_ROWS, D_MODEL), loss)
    inv = inv + w_out_red[16][0:1, :]
    dq, dg, dkv, g_sinks = _swa_bwd(dy_att, q, g, kv, o, lse, pos, inv, sinks)
    dz, dxbc, ddt, g_conv_w, g_conv_b, g_dt_bias, g_a_log, g_d_skip, g_norm_w = _ssd_bwd(
        dy_ssd, z, y_pre, xbc, dtp, prev, conv_w, conv_b, dt_bias, a_log, d_skip, norm_w)
    gw_z, gw_g, gw_q = _matmuls_tn([dz, dg, dq], xb, "gw_zgq")
    gw_xbc, gw_kv, gw_dt = _matmuls_tn([dxbc, dkv, ddt], xb, "gw_xbc_kv_dt")
    gw_in = jnp.concatenate([gw_z, gw_xbc, gw_dt[0:SSD_HEADS], gw_q, gw_kv, gw_g], axis=0)
    small = dict(conv_w=g_conv_w, conv_b=g_conv_b, dt_bias=g_dt_bias, a_log=g_a_log, d_skip=g_d_skip,
                 ssd_norm_w=g_norm_w, attn_sinks=g_sinks, ln_g=g_ln_g, ln_b=g_ln_b)
    return loss, (dr, dz, dg, dq, dxbc, dkv, ddt, w), gw_in, w_out_red, small


def _mesh_pos():
    return lax.axis_index("x"), lax.axis_index("y"), lax.axis_index("c")


def _gather_weights(w_in_s, conv_w_s):
    hr = w_in_s.shape[0] // 2
    qa = 336
    quarters = ((0, qa), (qa, hr - qa))

    def body(win_ref, cw_ref, owin_ref, ocw_ref, stage, send_sems, recv_sems, small_send, small_recv, local_sems):
        x, y, c = _mesh_pos()
        me = 2 * x + y
        sibling = (x, y, 1 - c)
        xn, yn, dg = (1 - x, y), (x, 1 - y), (1 - x, 1 - y)
        chips = [xn, yn, dg]
        load = pltpu.make_async_copy(win_ref, stage, local_sems.at[1])
        load.start()
        locals_ = [pltpu.make_async_copy(cw_ref, ocw_ref.at[me], local_sems.at[0])]
        for cp in locals_:
            cp.start()
        started = []

        def piece(ref, chip, half, q):
            off, n = quarters[q]
            return ref.at[2 * chip[0] + chip[1]].at[pl.ds(half * hr + off, n), :]

        def mine(q):
            off, n = quarters[q]
            return win_ref.at[pl.ds(c * hr + off, n), :]

        def copy(src, dst, k, to):
            return pltpu.make_async_remote_copy(src_ref=src, dst_ref=dst, send_sem=send_sems.at[k], recv_sem=recv_sems.at[k],
                                                device_id=to, device_id_type=MESH)

        def go(cp):
            cp.start()
            started.append(cp)

        go(copy(mine(0), piece(owin_ref, (x, y), c, 0), 0, (*xn, c)))
        go(copy(mine(1), piece(owin_ref, (x, y), c, 1), 2, (*yn, c)))
        go(copy(mine(1), piece(owin_ref, (x, y), c, 1), 1, (*xn, c)))
        go(copy(mine(0), piece(owin_ref, (x, y), c, 0), 3, (*yn, c)))
        for j, (px, py) in enumerate(chips):
            cp = pltpu.make_async_remote_copy(
                src_ref=cw_ref, dst_ref=ocw_ref.at[me], send_sem=small_send.at[j], recv_sem=small_recv.at[j],
                device_id=(px, py, c), device_id_type=MESH)
            go(cp)
        load.wait()
        store = pltpu.make_async_copy(stage, owin_ref.at[me], local_sems.at[2])
        store.start()
        locals_.append(store)
        arrivals = [(0, xn, 0, (4, (*yn, c))), (2, yn, 1, (5, (*xn, c))), (1, xn, 1, None), (3, yn, 0, None),
                    (4, dg, 0, None), (5, dg, 1, None)]
        for n, (k, chip, q, onward) in enumerate(arrivals):
            blk = piece(owin_ref, chip, c, q)
            copy(blk, blk, k, sibling).wait_recv()
            if onward is not None:
                go(copy(blk, blk, onward[0], onward[1]))
            go(copy(blk, blk, 6 + n, sibling))
        for n, (k, chip, q, onward) in enumerate(arrivals):
            blk = piece(owin_ref, chip, 1 - c, q)
            copy(blk, blk, 6 + n, sibling).wait_recv()
        for j in range(3):
            pltpu.make_async_remote_copy(
                src_ref=cw_ref, dst_ref=ocw_ref.at[me], send_sem=small_send.at[j], recv_sem=small_recv.at[j],
                device_id=sibling, device_id_type=MESH).wait_recv()
        for cp in started:
            cp.wait_send()
        for cp in locals_:
            cp.wait()

    any_spec = pl.BlockSpec(memory_space=pl.ANY)
    return pl.pallas_call(
        body, name="gather_weights",
        in_specs=[any_spec] * 2, out_specs=[any_spec] * 2,
        out_shape=[jax.ShapeDtypeStruct((N_CHIPS,) + a.shape, a.dtype) for a in (w_in_s, conv_w_s)],
        scratch_shapes=[pltpu.VMEM(w_in_s.shape, w_in_s.dtype),
                        pltpu.SemaphoreType.DMA((12,)), pltpu.SemaphoreType.DMA((12,)),
                        pltpu.SemaphoreType.DMA((3,)), pltpu.SemaphoreType.DMA((3,)), pltpu.SemaphoreType.DMA((3,))],
    )(w_in_s, conv_w_s)


_HBM = pl.BlockSpec(memory_space=pltpu.HBM)
_SEM = pl.BlockSpec(memory_space=pltpu.SEMAPHORE)
_EFFECT = pltpu.SideEffectType.DATAFLOW_SIDE_EFFECTING


def _gather_w_out_start(w_out_s, after):
    def body(src_ref, land_ref, after_ref, s0, s1, s2, r0, r1, r2, src_thru, land_thru, token):
        x, y, c = _mesh_pos()
        me = 2 * x + y
        chips = [(1 - x, y), (x, 1 - y), (1 - x, 1 - y)]
        for (px, py), s, r in zip(chips, (s0, s1, s2), (r0, r1, r2)):
            pltpu.make_async_remote_copy(src_ref=src_ref, dst_ref=land_ref.at[me], send_sem=s, recv_sem=r,
                                         device_id=(px, py, c), device_id_type=MESH).start()
        token[...] = jnp.zeros_like(token)

    sem = pltpu.SemaphoreType.DMA(())
    land = lax.empty((N_CHIPS,) + w_out_s.shape, w_out_s.dtype)
    return pl.pallas_call(
        body, name="gather_w_out_start",
        out_shape=(sem,) * 6 + (pltpu.HBM(w_out_s.shape, w_out_s.dtype), pltpu.HBM(land.shape, land.dtype),
                                jax.ShapeDtypeStruct((8, 128), F32)),
        in_specs=(_HBM, _HBM, pl.BlockSpec(memory_space=pl.ANY)),
        out_specs=(_SEM,) * 6 + (_HBM, _HBM, pl.BlockSpec(memory_space=pltpu.VMEM)),
        input_output_aliases={0: 6, 1: 7},
        compiler_params=pltpu.CompilerParams(has_side_effects=_EFFECT),
    )(pltpu.with_memory_space_constraint(w_out_s, pltpu.HBM), pltpu.with_memory_space_constraint(land, pltpu.HBM), after)


def _gather_w_out_wait(sems, src_thru, land_thru, after):
    def body(src_ref, land_ref, s0, s1, s2, r0, r1, r2, after_ref, src_dead, got_ref):
        x, y, c = _mesh_pos()
        chips = [(1 - x, y), (x, 1 - y), (1 - x, 1 - y)]
        for (px, py), s, r in zip(chips, (s0, s1, s2), (r0, r1, r2)):
            cp = pltpu.make_async_remote_copy(src_ref=src_ref, dst_ref=land_ref.at[2 * px + py], send_sem=s, recv_sem=r,
                                              device_id=(px, py, c), device_id_type=MESH)
            cp.wait_send()
            cp.wait_recv()

    return pl.pallas_call(
        body, name="gather_w_out_wait",
        out_shape=(pltpu.HBM(src_thru.shape, src_thru.dtype), pltpu.HBM(land_thru.shape, land_thru.dtype)),
        in_specs=(_HBM, _HBM) + (_SEM,) * 6 + (pl.BlockSpec(memory_space=pl.ANY),),
        out_specs=(_HBM, _HBM), input_output_aliases={0: 0, 1: 1},
        compiler_params=pltpu.CompilerParams(has_side_effects=_EFFECT),
    )(src_thru, land_thru, *sems, after)[1]


def _pair_start(gw_in, after):
    hr = gw_in.shape[1] // 2

    def body(src_ref, land_ref, after_ref, *refs):
        x, y, c = _mesh_pos()
        for j in range(N_CHIPS):
            pltpu.make_async_remote_copy(
                src_ref=src_ref.at[j, pl.ds((1 - c) * hr, hr), :], dst_ref=land_ref.at[j], send_sem=refs[j],
                recv_sem=refs[N_CHIPS + j], device_id=(x, y, 1 - c), device_id_type=MESH).start()
        refs[10][...] = jnp.zeros_like(refs[10])

    sem = pltpu.SemaphoreType.DMA(())
    land = lax.empty((N_CHIPS, hr, D_MODEL), gw_in.dtype)
    return pl.pallas_call(
        body, name="pair_start",
        out_shape=(sem,) * 8 + (pltpu.HBM(gw_in.shape, gw_in.dtype), pltpu.HBM(land.shape, land.dtype),
                                jax.ShapeDtypeStruct((8, 128), F32)),
        in_specs=(_HBM, _HBM, pl.BlockSpec(memory_space=pl.ANY)),
        out_specs=(_SEM,) * 8 + (_HBM, _HBM, pl.BlockSpec(memory_space=pltpu.VMEM)),
        input_output_aliases={0: 8, 1: 9},
        compiler_params=pltpu.CompilerParams(has_side_effects=_EFFECT),
    )(pltpu.with_memory_space_constraint(gw_in, pltpu.HBM), pltpu.with_memory_space_constraint(land, pltpu.HBM), after)


def _pair_wait(sems, gw_thru, land_thru, after):
    hr = land_thru.shape[1]

    def body(src_ref, land_ref, *refs):
        x, y, c = _mesh_pos()
        for j in range(N_CHIPS):
            cp = pltpu.make_async_remote_copy(
                src_ref=src_ref.at[j, pl.ds((1 - c) * hr, hr), :], dst_ref=land_ref.at[j], send_sem=refs[j],
                recv_sem=refs[N_CHIPS + j], device_id=(x, y, 1 - c), device_id_type=MESH)
            cp.wait_send()
            cp.wait_recv()

    return pl.pallas_call(
        body, name="pair_wait",
        out_shape=(pltpu.HBM(gw_thru.shape, gw_thru.dtype), pltpu.HBM(land_thru.shape, land_thru.dtype)),
        in_specs=(_HBM, _HBM) + (_SEM,) * 8 + (pl.BlockSpec(memory_space=pl.ANY),),
        out_specs=(_HBM, _HBM), input_output_aliases={0: 0, 1: 1},
        compiler_params=pltpu.CompilerParams(has_side_effects=_EFFECT),
    )(gw_thru, land_thru, *sems, after)


def _chip_start(s_in, after):
    def body(src_ref, land_ref, after_ref, *refs):
        x, y, c = _mesh_pos()
        me = 2 * x + y
        for j, (px, py) in enumerate([(1 - x, y), (x, 1 - y), (1 - x, 1 - y)]):
            pltpu.make_async_remote_copy(
                src_ref=src_ref.at[2 * px + py], dst_ref=land_ref.at[me], send_sem=refs[j], recv_sem=refs[3 + j],
                device_id=(px, py, c), device_id_type=MESH).start()
        refs[8][...] = jnp.zeros_like(refs[8])

    sem = pltpu.SemaphoreType.DMA(())
    land = lax.empty(s_in.shape, s_in.dtype)
    return pl.pallas_call(
        body, name="chip_start",
        out_shape=(sem,) * 6 + (pltpu.HBM(s_in.shape, s_in.dtype), pltpu.HBM(land.shape, land.dtype),
                                jax.ShapeDtypeStruct((8, 128), F32)),
        in_specs=(_HBM, _HBM, pl.BlockSpec(memory_space=pl.ANY)),
        out_specs=(_SEM,) * 6 + (_HBM, _HBM, pl.BlockSpec(memory_space=pltpu.VMEM)),
        input_output_aliases={0: 6, 1: 7},
        compiler_params=pltpu.CompilerParams(has_side_effects=_EFFECT),
    )(pltpu.with_memory_space_constraint(s_in, pltpu.HBM), pltpu.with_memory_space_constraint(land, pltpu.HBM), after)


def _chip_wait(sems, s_thru, land_thru, after):
    def body(src_ref, land_ref, *refs):
        x, y, c = _mesh_pos()
        for j, (px, py) in enumerate([(1 - x, y), (x, 1 - y), (1 - x, 1 - y)]):
            cp = pltpu.make_async_remote_copy(
                src_ref=src_ref.at[2 * px + py], dst_ref=land_ref.at[2 * px + py], send_sem=refs[j], recv_sem=refs[3 + j],
                device_id=(px, py, c), device_id_type=MESH)
            cp.wait_send()
            cp.wait_recv()

    return pl.pallas_call(
        body, name="chip_wait",
        out_shape=(pltpu.HBM(s_thru.shape, s_thru.dtype), pltpu.HBM(land_thru.shape, land_thru.dtype)),
        in_specs=(_HBM, _HBM) + (_SEM,) * 6 + (pl.BlockSpec(memory_space=pl.ANY),),
        out_specs=(_HBM, _HBM), input_output_aliases={0: 0, 1: 1},
        compiler_params=pltpu.CompilerParams(has_side_effects=_EFFECT),
    )(s_thru, land_thru, *sems, after)


def _pair_share(h_in, small):
    def body(hin_ref, sm_ref, rin_ref, slots_ref, send_sems, recv_sems, small_send, small_recv, local_sem):
        x, y, c = _mesh_pos()
        dev = 4 * x + 2 * y + c
        mine = pltpu.make_async_copy(sm_ref, slots_ref.at[dev], local_sem)
        mine.start()
        share = pltpu.make_async_remote_copy(
            src_ref=hin_ref, dst_ref=rin_ref, send_sem=send_sems.at[0], recv_sem=recv_sems.at[0],
            device_id=(x, y, 1 - c), device_id_type=MESH)
        share.start()
        started = []
        for k in range(1, 8):
            peer = (x ^ ((k >> 2) & 1), y ^ ((k >> 1) & 1), c ^ (k & 1))
            cp = pltpu.make_async_remote_copy(
                src_ref=sm_ref, dst_ref=slots_ref.at[dev], send_sem=small_send.at[k - 1], recv_sem=small_recv.at[k - 1],
                device_id=peer, device_id_type=MESH)
            cp.start()
            started.append(cp)
        share.wait()
        for k in range(1, 8):
            pltpu.make_async_remote_copy(
                src_ref=sm_ref, dst_ref=slots_ref.at[dev], send_sem=small_send.at[k - 1], recv_sem=small_recv.at[k - 1],
                device_id=(x, y, 1 - c), device_id_type=MESH).wait_recv()
        for cp in started:
            cp.wait_send()
        mine.wait()

    any_spec = pl.BlockSpec(memory_space=pl.ANY)
    return pl.pallas_call(
        body, name="pair_share",
        in_specs=[any_spec] * 2, out_specs=[any_spec] * 2,
        out_shape=[jax.ShapeDtypeStruct(h_in.shape, F32), jax.ShapeDtypeStruct((8,) + small.shape, F32)],
        scratch_shapes=[pltpu.SemaphoreType.DMA((1,)), pltpu.SemaphoreType.DMA((1,)),
                        pltpu.SemaphoreType.DMA((7,)), pltpu.SemaphoreType.DMA((7,)), pltpu.SemaphoreType.DMA],
    )(h_in, small)


def _reduce_w_out_start(slabs, after):
    def body(src_ref, land_ref, after_ref, *refs):
        x, y, c = _mesh_pos()
        me = 4 * x + 2 * y + c
        for k in range(1, 8):
            px, py, pc = x ^ ((k >> 2) & 1), y ^ ((k >> 1) & 1), c ^ (k & 1)
            pltpu.make_async_remote_copy(src_ref=src_ref.at[2 * px + py], dst_ref=land_ref.at[me], send_sem=refs[k - 1],
                                         recv_sem=refs[6 + k], device_id=(px, py, pc), device_id_type=MESH).start()
        refs[16][...] = jnp.zeros_like(refs[16])

    sem = pltpu.SemaphoreType.DMA(())
    land = lax.empty((8,) + slabs.shape[1:], slabs.dtype)
    return pl.pallas_call(
        body, name="reduce_w_out_start",
        out_shape=(sem,) * 14 + (pltpu.HBM(slabs.shape, slabs.dtype), pltpu.HBM(land.shape, land.dtype),
                                 jax.ShapeDtypeStruct((8, 128), F32)),
        in_specs=(_HBM, _HBM, pl.BlockSpec(memory_space=pl.ANY)),
        out_specs=(_SEM,) * 14 + (_HBM, _HBM, pl.BlockSpec(memory_space=pltpu.VMEM)),
        input_output_aliases={0: 14, 1: 15},
        compiler_params=pltpu.CompilerParams(has_side_effects=_EFFECT),
    )(pltpu.with_memory_space_constraint(slabs, pltpu.HBM), pltpu.with_memory_space_constraint(land, pltpu.HBM), after)


def _reduce_w_out_wait(sems, slabs_thru, land_thru, after):
    def body(src_ref, land_ref, *refs):
        x, y, c = _mesh_pos()
        for k in range(1, 8):
            px, py, pc = x ^ ((k >> 2) & 1), y ^ ((k >> 1) & 1), c ^ (k & 1)
            cp = pltpu.make_async_remote_copy(
                src_ref=src_ref.at[2 * px + py], dst_ref=land_ref.at[4 * px + 2 * py + pc], send_sem=refs[k - 1],
                recv_sem=refs[6 + k], device_id=(px, py, pc), device_id_type=MESH)
            cp.wait_send()
            cp.wait_recv()

    return pl.pallas_call(
        body, name="reduce_w_out_wait",
        out_shape=(pltpu.HBM(slabs_thru.shape, slabs_thru.dtype), pltpu.HBM(land_thru.shape, land_thru.dtype)),
        in_specs=(_HBM, _HBM) + (_SEM,) * 14 + (pl.BlockSpec(memory_space=pl.ANY),),
        out_specs=(_HBM, _HBM), input_output_aliases={0: 0, 1: 1},
        compiler_params=pltpu.CompilerParams(has_side_effects=_EFFECT),
    )(slabs_thru, land_thru, *sems, after)


def _pair_add(g, recv, core, name):
    _, rows, C = recv.shape

    def body(core_ref, g_ref, r_ref, o_ref):
        o_ref[...] = _bf(g_ref[...].astype(F32) + r_ref[...].astype(F32))

    spec = pl.BlockSpec((1, rows, C), lambda j, core: (j, 0, 0))
    return pl.pallas_call(
        body, name=name,
        grid_spec=pltpu.PrefetchScalarGridSpec(
            num_scalar_prefetch=1, grid=(N_CHIPS,),
            in_specs=[pl.BlockSpec((1, rows, C), lambda j, core: (j, core[0], 0)), spec], out_specs=spec),
        out_shape=jax.ShapeDtypeStruct((N_CHIPS, rows, C), BF16),
        compiler_params=_cparams(("parallel",)),
    )(core, g, recv)


def _chip_add(own, parts, chip, name):
    _, rows, C = parts.shape
    tc = 512

    def body(chip_ref, own_ref, r0, r1, r2, r3, o_ref):
        acc = None
        for j, r in enumerate((r0, r1, r2, r3)):
            term = jnp.where(chip_ref[0] == j, own_ref[0], r[0]).astype(F32)
            acc = term if acc is None else acc + term
        o_ref[...] = acc

    def slab(j):
        return pl.BlockSpec((1, rows, tc), lambda i, chip: (jnp.where(chip[0] == j, (j + 1) % N_CHIPS, j), 0, i))

    return pl.pallas_call(
        body, name=name,
        grid_spec=pltpu.PrefetchScalarGridSpec(
            num_scalar_prefetch=1, grid=(C // tc,),
            in_specs=[pl.BlockSpec((1, rows, tc), lambda i, chip: (chip[0], 0, i))] + [slab(j) for j in range(N_CHIPS)],
            out_specs=pl.BlockSpec((rows, tc), lambda i, chip: (0, i))),
        out_shape=jax.ShapeDtypeStruct((rows, C), F32),
        compiler_params=_cparams(("parallel",)),
    )(chip, own, parts, parts, parts, parts)


def _adamw_math(w, g, m, v):
    m = ADAM_B1 * m + (1.0 - ADAM_B1) * g
    v = ADAM_B2 * v + (1.0 - ADAM_B2) * (g * g)
    m_hat = m / (1.0 - ADAM_B1 ** ADAM_STEP)
    v_hat = v / (1.0 - ADAM_B2 ** ADAM_STEP)
    delta = -ADAM_LR * (m_hat / (jnp.sqrt(v_hat) + ADAM_EPS) + ADAM_WD * w)
    return delta, m, v


def _adamw_rows(w, g_own, g_sib, m, v, core, name):
    R, C = w.shape[0], w.shape[-1]
    rows = g_own.shape[0]
    step = 256
    chunks = [(r, min(step, R - r)) for r in range(0, R, step)]
    sub = 64

    def body(core_ref, w_hbm, go_hbm, gs_hbm, m_hbm, v_hbm, d_hbm, nm_hbm, nv_hbm, g_hbm,
             wbuf, mbuf, vbuf, gbuf, dbuf, nmbuf, nvbuf, in_sems, g_sems, out_sems):
        c = core_ref[0]
        flat = lambda ref: ref.at[:, 0, :]
        g_in = [pltpu.make_async_copy(go_hbm, gbuf.at[pl.ds(pl.multiple_of(c * rows, 8), rows), :], g_sems.at[0]),
                pltpu.make_async_copy(gs_hbm, gbuf.at[pl.ds(pl.multiple_of((1 - c) * rows, 8), rows), :], g_sems.at[1])]
        for cp in g_in:
            cp.start()
        loads = []
        for k, (r0, n) in enumerate(chunks):
            cps = [pltpu.make_async_copy(flat(src).at[pl.ds(r0, n), :], dst.at[pl.ds(r0, n), :], in_sems.at[a, k])
                   for a, (src, dst) in enumerate(((w_hbm, wbuf), (m_hbm, mbuf), (v_hbm, vbuf)))]
            for cp in cps:
                cp.start()
            loads.append(cps)
        for cp in g_in:
            cp.wait()
        stores = []
        for k, (r0, n) in enumerate(chunks):
            for cp in loads[k]:
                cp.wait()

            def update(rs):
                g = gbuf[rs, :]
                dl, nm, nv = _adamw_math(wbuf[rs, :], g, mbuf[rs, :], vbuf[rs, :])
                dbuf[rs, :] = dl
                nmbuf[rs, :] = nm
                nvbuf[rs, :] = nv

            if n % sub == 0:
                def block(i, carry, r0=r0):
                    update(pl.ds(pl.multiple_of(r0 + i * sub, 8), sub))
                    return carry
                lax.fori_loop(0, n // sub, block, 0)
            else:
                update(pl.ds(r0, n))
            cps = [pltpu.make_async_copy(src.at[pl.ds(r0, n), :], flat(dst).at[pl.ds(r0, n), :], out_sems.at[a, k])
                   for a, (src, dst) in enumerate(((dbuf, d_hbm), (nmbuf, nm_hbm), (nvbuf, nv_hbm), (gbuf, g_hbm)))]
            for cp in cps:
                cp.start()
            stores += cps
        for cp in stores:
            cp.wait()

    any_spec = pl.BlockSpec(memory_space=pl.ANY)
    dense = pltpu.VMEM((R, C), F32)
    return pl.pallas_call(
        body, name=name,
        grid_spec=pltpu.PrefetchScalarGridSpec(
            num_scalar_prefetch=1, grid=(1,),
            in_specs=[any_spec] * 5, out_specs=[any_spec] * 4,
            scratch_shapes=[dense, dense, dense, pltpu.VMEM((2 * rows, C), F32), dense, dense, dense,
                            pltpu.SemaphoreType.DMA((3, len(chunks))), pltpu.SemaphoreType.DMA((2,)),
                            pltpu.SemaphoreType.DMA((4, len(chunks)))]),
        out_shape=[jax.ShapeDtypeStruct(w.shape, F32)] * 4,
        compiler_params=_cparams(),
    )(core, w, g_own, g_sib, m, v)


def _adamw_sum8(w, slabs, land, m, v, ids, name):
    R, C = w.shape
    tc = 128

    def body(ids_ref, w_ref, own_ref, *refs):
        lrefs, (m_ref, v_ref, d_ref, nm_ref, nv_ref, g_ref) = refs[:8], refs[8:]
        g = None
        for d, l_ref in enumerate(lrefs):
            term = jnp.where(ids_ref[0] == d, own_ref[0], l_ref[0]).astype(F32)
            g = term if g is None else g + term
        dl, nm, nv = _adamw_math(w_ref[...], g, m_ref[...], v_ref[...])
        d_ref[...] = dl
        nm_ref[...] = nm
        nv_ref[...] = nv
        g_ref[...] = g

    def slot(d):
        return pl.BlockSpec((1, R, tc), lambda i, ids: (jnp.where(ids[0] == d, (d + 1) % 8, d), 0, i))

    spec = pl.BlockSpec((R, tc), lambda i, ids: (0, i))
    return pl.pallas_call(
        body, name=name,
        grid_spec=pltpu.PrefetchScalarGridSpec(
            num_scalar_prefetch=1, grid=(C // tc,),
            in_specs=[spec, pl.BlockSpec((1, R, tc), lambda i, ids: (ids[1], 0, i))] + [slot(d) for d in range(8)]
            + [spec, spec],
            out_specs=[spec] * 4),
        out_shape=[jax.ShapeDtypeStruct((R, C), F32)] * 4,
        compiler_params=_cparams(("parallel",)),
    )(ids, w, slabs, *([land] * 8), m, v)


SMALL_NAMES = ("conv_b", "ssd_norm_w", "ln_g", "ln_b", "dt_bias", "a_log", "d_skip", "attn_sinks")
SMALL_FIELDS = ((4, 0, D_XBC), (5, 0, D_SSD), (6, 0, D_MODEL), (7, 0, D_MODEL), (5, 1024, SSD_HEADS), (5, 1152, SSD_HEADS),
                (5, 1280, SSD_HEADS), (5, 1408, ATT_QH))
LOSS_FIELD = (6, 1024, 128)
K_SMALL = D_XBC


def _pack_small(g_conv_w, vecs, loss):
    def body(cw_ref, *refs):
        o_ref = refs[-1]
        o_ref[...] = jnp.zeros_like(o_ref)
        o_ref[0:CONV_K, 0:D_XBC] = cw_ref[...]
        for v_ref, (row, off, n) in zip(refs[:-2], SMALL_FIELDS):
            o_ref[row:row + 1, off:off + n] = v_ref[...]
        o_ref[LOSS_FIELD[0]:LOSS_FIELD[0] + 1, LOSS_FIELD[1]:LOSS_FIELD[1] + LOSS_FIELD[2]] = refs[-2][...]

    return pl.pallas_call(
        body, name="pack_small", out_shape=jax.ShapeDtypeStruct((8, K_SMALL), F32), compiler_params=_cparams(),
    )(g_conv_w, *vecs, loss)


def _adamw_small(slots, chip, conv_w, m_conv_w, v_conv_w, params, moms, vars_):
    n_vec = len(SMALL_NAMES)

    def body(chip_ref, s_ref, *refs):
        ins = refs[:3 * (n_vec + 1)]
        outs = refs[3 * (n_vec + 1):-1]
        tot_ref = refs[-1]
        tot = s_ref[0]
        for d in range(1, 8):
            tot = tot + s_ref[d]
        outs[0][...] = tot[LOSS_FIELD[0]:LOSS_FIELD[0] + 1, LOSS_FIELD[1]:LOSS_FIELD[1] + 1]
        off = pl.multiple_of(chip_ref[0] * CONV_COLS, 128)
        tot_ref[...] = tot
        grads = [tot_ref[0:CONV_K, pl.ds(off, CONV_COLS)]]
        grads += [tot[row:row + 1, o:o + n] for row, o, n in SMALL_FIELDS]
        for k, g in enumerate(grads):
            w_ref, m_ref, v_ref = ins[3 * k:3 * k + 3]
            full = (0,) if k == 0 else (Ellipsis,)
            d, nm, nv = _adamw_math(w_ref[full], g, m_ref[full], v_ref[full])
            for o_ref, val in zip(outs[1 + 4 * k:5 + 4 * k], (g, d, nm, nv)):
                o_ref[full] = val

    args = [conv_w, m_conv_w, v_conv_w]
    for w, m, v in zip(params, moms, vars_):
        args += [w, m, v]
    shapes = [jax.ShapeDtypeStruct((1, 1), F32)] + [jax.ShapeDtypeStruct(conv_w.shape, F32)] * 4
    for w in params:
        shapes += [jax.ShapeDtypeStruct(w.shape, F32)] * 4
    vmem = pl.BlockSpec(memory_space=pltpu.VMEM)
    return pl.pallas_call(
        body, name="adamw_small",
        grid_spec=pltpu.PrefetchScalarGridSpec(
            num_scalar_prefetch=1, grid=(1,),
            in_specs=[pl.BlockSpec(slots.shape, lambda i, chip: (0, 0, 0))] + [vmem] * len(args),
            out_specs=[vmem] * len(shapes), scratch_shapes=[pltpu.VMEM((8, K_SMALL), F32)]),
        out_shape=shapes, compiler_params=_cparams(),
    )(chip, slots, *args)


def kernel(x, positions, w_in, conv_w, conv_b, dt_bias, a_log, d_skip, ssd_norm_w, attn_sinks, w_out, ln_g, ln_b, loss_target, m_w_in, m_conv_w, m_conv_b, m_dt_bias, m_a_log, m_d_skip, m_ssd_norm_w, m_attn_sinks, m_w_out, m_ln_g, m_ln_b, v_w_in, v_conv_w, v_conv_b, v_dt_bias, v_a_log, v_d_skip, v_ssd_norm_w, v_attn_sinks, v_w_out, v_ln_g, v_ln_b):
    mx, my, mc = _mesh_pos()
    chip = 2 * mx + my
    L = x.shape[1]

    conv_w_s8 = jnp.pad(conv_w[0], ((0, 8 - CONV_K), (0, 0)))
    pad_rows = ((0, SLAB_ROWS - W_IN_COLS), (0, 0))
    w_in_t = w_in[0].T
    w_in_b, w_out_b = jnp.pad(_bf(w_in_t), pad_rows), _bf(w_out[0])
    ag_in, ag_cw = _gather_weights(w_in_b, conv_w_s8)
    started = _gather_w_out_start(w_out_b, ag_cw)
    own = (jnp.arange(N_CHIPS) == chip)[:, None, None]

    def get_w_out(after):
        landed = _gather_w_out_wait(started[0:6], started[6], started[7], after)
        return jnp.where(own, w_out_b[None], landed).reshape(D_MIX, D_MODEL)

    w_full = jnp.concatenate([ag_in[j, 0:W_IN_COLS] for j in range(N_CHIPS)], axis=0)
    w = jnp.concatenate([
        w_full[O_Z:O_Z + D_SSD], w_full[O_G:O_G + D_ATT], w_full[O_Q:O_Q + D_ATT],
        w_full[O_XBC:O_XBC + D_XBC], w_full[O_K:O_K + 2 * D_KV], w_full[O_DT:O_DT + SSD_HEADS],
        jnp.zeros((DT_PAD - SSD_HEADS, D_MODEL), BF16)], axis=0)
    conv_w_full = jnp.concatenate([ag_cw[j, 0:CONV_K] for j in range(N_CHIPS)], axis=1)

    loss_part, gx_args, gw_in, w_out_red, small = _local_step(
        x[0], positions[0].reshape(L, 1), loss_target[0], w, get_w_out, started[8][0:1, :], conv_w_full,
        conv_b, dt_bias, a_log, d_skip, ssd_norm_w, attn_sinks, ln_g, ln_b)

    packed = _pack_small(small["conv_w"], [small[n] for n in SMALL_NAMES], loss_part)
    core_id = mc.reshape(1).astype(jnp.int32)
    chip_id = chip.reshape(1).astype(jnp.int32)
    ids = jnp.stack([4 * mx + 2 * my + mc, chip]).astype(jnp.int32)
    slabs = jnp.stack([jnp.pad(gw_in[W_IN_COLS * j:W_IN_COLS * (j + 1)], pad_rows) for j in range(N_CHIPS)])
    w_in_red = _pair_start(slabs, packed)
    grad_x = _grad_x(*gx_args, w_in_red[10], 0)
    gw_in_slabs, recv_in = _pair_wait(w_in_red[0:8], w_in_red[8], w_in_red[9], grad_x[0:8, 0:128])
    s_in = _pair_add(gw_in_slabs, recv_in, core_id, "pair_add_in")
    chip_red = _chip_start(s_in, packed)
    grad_x = _grad_x(*gx_args, chip_red[8], 1, grad_x)
    own_slabs, landed = _reduce_w_out_wait(w_out_red[0:14], w_out_red[14], w_out_red[15], grad_x)
    out_t = _adamw_sum8(w_out[0], own_slabs, landed, m_w_out[0], v_w_out[0], ids, "adamw_w_out")
    d_w_out, nm_w_out, nv_w_out, g_w_out = [a[None] for a in out_t]
    s_in, r_in = _chip_wait(chip_red[0:6], chip_red[6], chip_red[7], out_t[0])
    h_in = _chip_add(s_in, r_in, chip_id, "chip_add_in")
    sib_in, slots = _pair_share(h_in, packed)

    to_rows = lambda a: jnp.transpose(a, (2, 0, 1))
    in_t = _adamw_rows(to_rows(w_in), h_in, sib_in, to_rows(m_w_in), to_rows(v_w_in), core_id, "adamw_w_in")
    d_w_in, nm_w_in, nv_w_in, g_w_in = [jnp.transpose(a, (1, 2, 0)) for a in in_t]

    params = dict(conv_b=conv_b, ssd_norm_w=ssd_norm_w, ln_g=ln_g, ln_b=ln_b, dt_bias=dt_bias, a_log=a_log,
                  d_skip=d_skip, attn_sinks=attn_sinks)
    moms = dict(conv_b=m_conv_b, ssd_norm_w=m_ssd_norm_w, ln_g=m_ln_g, ln_b=m_ln_b, dt_bias=m_dt_bias, a_log=m_a_log,
                d_skip=m_d_skip, attn_sinks=m_attn_sinks)
    vars_ = dict(conv_b=v_conv_b, ssd_norm_w=v_ssd_norm_w, ln_g=v_ln_g, ln_b=v_ln_b, dt_bias=v_dt_bias, a_log=v_a_log,
                 d_skip=v_d_skip, attn_sinks=v_attn_sinks)
    res = _adamw_small(slots, chip_id, conv_w, m_conv_w, v_conv_w, [params[n] for n in SMALL_NAMES],
                       [moms[n] for n in SMALL_NAMES], [vars_[n] for n in SMALL_NAMES])
    loss = res[0][0, 0]
    grads, delta, new_m, new_v = {}, {}, {}, {}
    for k, n in enumerate(("conv_w",) + SMALL_NAMES):
        grads[n], delta[n], new_m[n], new_v[n] = res[1 + 4 * k:5 + 4 * k]
    for dd, a_in, a_out in ((grads, g_w_in, g_w_out), (delta, d_w_in, d_w_out), (new_m, nm_w_in, nm_w_out),
                            (new_v, nv_w_in, nv_w_out)):
        dd["w_in"] = a_in
        dd["w_out"] = a_out
    order = ("w_in", "conv_w", "conv_b", "dt_bias", "a_log", "d_skip", "ssd_norm_w", "attn_sinks", "w_out", "ln_g", "ln_b")
    return (loss, grad_x[None], *[grads[n] for n in order], *[delta[n] for n in order], *[new_m[n] for n in order],
            *[new_v[n] for n in order])
```

```python
import numpy as np
import jax
import jax.numpy as jnp
from jax import lax
from jax.experimental import pallas as pl
from jax.experimental.pallas import tpu as pltpu

F32 = jnp.float32
BF16 = jnp.bfloat16
MESH = pl.DeviceIdType.MESH

D_MODEL = 1024
D_SSD = 1024
D_ATT = 1024
D_MIX = 2048
SSD_HEADS = 16
SSD_P = 64
SSD_GROUPS = 2
SSD_R = 8
SSD_N = 128
D_BC = 256
D_XBC = 1536
CONV_K = 4
CHUNK = 128
ATT_HD = 64
assert ATT_HD in (4, 16, 64, 256)
ATT_QH = 16
ATT_KVH = 4
ATT_R = 4
D_KV = 256
WINDOW = 128
ROPE_THETA = 500000.0
ROPE_DIM = 16
ALPHA = 2.0 ** 0.25
LN_EPS = 1e-5
RMS_EPS = 1e-5
D_IN_PROJ = 5136
O_Z, O_XBC, O_DT, O_Q, O_K, O_V, O_G = 0, 1024, 2560, 2576, 3600, 3856, 4112
P_Z, P_G, P_Q, P_XBC, P_KV, P_DT, P_END = 0, 1024, 2048, 3072, 4608, 5120, 5248
DT_PAD = 128
N_CHIPS = 4
W_IN_COLS = D_IN_PROJ // N_CHIPS
SLAB_ROWS = 1312
W_OUT_ROWS = D_MIX // N_CHIPS
CONV_COLS = D_XBC // N_CHIPS

ADAM_LR = 0.001
ADAM_B1 = 0.9
ADAM_B2 = 0.999
ADAM_EPS = 1e-08
ADAM_WD = 0.01
ADAM_STEP = 10

VMEM_LIMIT = 56 * 1024 * 1024
ROW_TILE = 512
NEG_BIG = -1e30


def _cparams(sem=None, **kw):
    if sem is not None:
        kw["dimension_semantics"] = sem
    return pltpu.CompilerParams(vmem_limit_bytes=VMEM_LIMIT, **kw)


def _dot(a, b):
    return jnp.dot(a, b, preferred_element_type=F32)


def _dot_nt(a, b):
    return lax.dot_general(a, b, (((1,), (1,)), ((), ())), preferred_element_type=F32)


def _dot_tn(a, b):
    return lax.dot_general(a, b, (((0,), (0,)), ((), ())), preferred_element_type=F32)


def _bf(a):
    return a.astype(BF16)


def _iota2(shape, dim):
    return lax.broadcasted_iota(jnp.int32, shape, dim)


def _three_terms(x):
    hi = _bf(x)
    r = x - hi.astype(F32)
    mid = _bf(r)
    return hi, mid, _bf(r - mid.astype(F32))


def _dot01(m, a):
    return sum(_dot(m, t) for t in _three_terms(a))


def _to_rows(col):
    k = col.shape[1]
    eye = (_iota2((k, k), 0) == _iota2((k, k), 1)).astype(BF16)
    return sum(_dot_nt(eye, t) for t in _three_terms(col))


def _to_cols(row):
    n = row.shape[1]
    eye = (_iota2((n, n), 0) == _iota2((n, n), 1)).astype(BF16)
    return sum(_dot_nt(eye, t) for t in _three_terms(row))


def _sigmoid(x):
    return jax.nn.sigmoid(x)


def _in_proj(x, w, pos, inv):
    L = x.shape[0]
    tm = ROW_TILE
    widths = (D_SSD, D_ATT, D_ATT, D_XBC, 2 * D_KV, DT_PAD)

    def body(x_ref, w_ref, pos_ref, inv_ref, z_ref, g_ref, q_ref, xbc_ref, kv_ref, dt_ref, xb_ref):
        xb = _bf(x_ref[...])
        xb_ref[...] = xb
        tabs = _rope_tables(pos_ref, inv_ref)
        q_ref[...] = _bf(_rope(_dot_nt(xb, w_ref[P_Q:P_Q + D_ATT, :]), tabs))
        kv_ref[:, 0:D_KV] = _bf(_rope(_dot_nt(xb, w_ref[P_KV:P_KV + D_KV, :]), tabs))
        kv_ref[:, D_KV:2 * D_KV] = _bf(_dot_nt(xb, w_ref[P_KV + D_KV:P_KV + 2 * D_KV, :]))
        for o_ref, off, wd in zip((z_ref, g_ref, xbc_ref, dt_ref), (P_Z, P_G, P_XBC, P_DT), (D_SSD, D_ATT, D_XBC, DT_PAD)):
            o_ref[...] = _dot_nt(xb, w_ref[off:off + wd, :])

    row = lambda wd: pl.BlockSpec((tm, wd), lambda i: (i, 0))
    return pl.pallas_call(
        body, name="in_proj", grid=(L // tm,),
        in_specs=[row(D_MODEL), pl.BlockSpec((P_END, D_MODEL), lambda i: (0, 0), pipeline_mode=pl.Buffered(1)), row(1),
                  pl.BlockSpec((1, 2 * ATT_HD), lambda i: (0, 0))],
        out_specs=[row(wd) for wd in widths] + [row(D_MODEL)],
        out_shape=[jax.ShapeDtypeStruct((L, wd), dt) for wd, dt in zip(widths, (F32, F32, BF16, F32, BF16, F32))]
        + [jax.ShapeDtypeStruct((L, D_MODEL), BF16)],
        compiler_params=_cparams(("parallel",)),
    )(x, w, pos, inv)


def _matmuls_tn(a_list, b, name):
    K, N = b.shape
    tk = min(K, 1024)
    n = len(a_list)

    def body(*refs):
        b_ref = refs[n]
        o_refs, acc_refs = refs[n + 1:2 * n + 1], refs[2 * n + 1:]

        @pl.when(pl.program_id(0) == 0)
        def _():
            for acc in acc_refs:
                acc[...] = jnp.zeros_like(acc)

        bb = _bf(b_ref[...])
        for a_ref, o_ref, acc in zip(refs[:n], o_refs, acc_refs):
            total = acc[...] + _dot_tn(_bf(a_ref[...]), bb)
            acc[...] = total
            o_ref[...] = _bf(total)

    return pl.pallas_call(
        body, name=name, grid=(K // tk,),
        in_specs=[pl.BlockSpec((tk, a.shape[1]), lambda k: (k, 0)) for a in a_list] + [pl.BlockSpec((tk, N), lambda k: (k, 0))],
        out_specs=[pl.BlockSpec((a.shape[1], N), lambda k: (0, 0)) for a in a_list],
        out_shape=[jax.ShapeDtypeStruct((a.shape[1], N), BF16) for a in a_list],
        scratch_shapes=[pltpu.VMEM((a.shape[1], N), F32) for a in a_list],
        compiler_params=_cparams(("arbitrary",)),
    )(*a_list, b)


def _grad_x(dr, dz, dg, dq, dxbc, dkv, ddt, w, after, part, prev=None):
    L = dr.shape[0]
    tm = min(ROW_TILE, L // 4)
    first = L // (4 * tm)
    n = first if part == 0 else L // tm - first
    widths = (D_SSD, D_ATT, D_ATT, D_XBC, 2 * D_KV, DT_PAD)
    offs = (P_Z, P_G, P_Q, P_XBC, P_KV, P_DT)

    def body(dr_ref, dz_ref, dg_ref, dq_ref, dxbc_ref, dkv_ref, ddt_ref, w_ref, after_ref, *rest):
        o_ref = rest[-1]
        acc = ALPHA * dr_ref[...]
        for p_ref, off, wd in zip((dz_ref, dg_ref, dq_ref, dxbc_ref, dkv_ref, ddt_ref), offs, widths):
            acc = acc + _dot(_bf(p_ref[...]), w_ref[off:off + wd, :])
        o_ref[...] = acc

    row = lambda wd: pl.BlockSpec((tm, wd), lambda i: (i + part * first, 0))
    ins = [dr, dz, dg, dq, dxbc, dkv, ddt, w, after]
    specs = ([row(D_MODEL)] + [row(wd) for wd in widths]
             + [pl.BlockSpec((P_END, D_MODEL), lambda i: (0, 0), pipeline_mode=pl.Buffered(1)),
                pl.BlockSpec((8, 128), lambda i: (0, 0))])
    if prev is not None:
        ins.append(prev)
        specs.append(pl.BlockSpec(memory_space=pl.ANY))
    return pl.pallas_call(
        body, name="grad_x_%d" % part, grid=(n,),
        in_specs=specs, out_specs=row(D_MODEL),
        out_shape=jax.ShapeDtypeStruct((L, D_MODEL), F32),
        input_output_aliases={} if prev is None else {len(ins) - 1: 0},
        compiler_params=_cparams(("parallel",)),
    )(*ins)


HALO = 16


def _shift_matrix(offsets):
    n = CHUNK + HALO
    m = np.zeros((len(offsets) * CHUNK, 2 * n), np.float32)
    for k, off in enumerate(offsets):
        t = np.arange(CHUNK)
        m[k * CHUNK + t, t + off] = 1.0
        m[k * CHUNK + t, n + t + off] = 1.0
    return jnp.asarray(m, BF16)


def _shifted_rows(first_part, second_part, smat_ref):
    h1, l1 = _hi_lo(first_part)
    h2, l2 = _hi_lo(second_part)
    sh = _dot(smat_ref[...], jnp.concatenate([h1, h2, l1, l2], axis=0))
    return sh[0:CHUNK], sh[CHUNK:2 * CHUNK], sh[2 * CHUNK:3 * CHUNK]


def _ssd_chunk_pre(first, xbc_ref, tail_ref, dt_ref, cw_ref, cb_ref, dtb_ref, alog_ref, smat_ref=None, ext=None):
    tail = jnp.where(first, 0.0, tail_ref[...])
    x = xbc_ref[...]
    if ext is None:
        taps = _shifted_rows(tail, x, smat_ref) + (x,)
    else:
        ext[0:HALO, :] = tail
        ext[HALO:HALO + CHUNK, :] = x
        taps = tuple(ext[pl.ds(HALO - (CONV_K - 1) + k, CHUNK), :] for k in range(CONV_K - 1)) + (x,)
    u = cb_ref[...] + cw_ref[0:1, :] * taps[0]
    for k in range(1, CONV_K):
        u = u + cw_ref[k:k + 1, :] * taps[k]
    sig = _sigmoid(u)
    xbc = u * sig
    dtraw = dt_ref[:, 0:SSD_HEADS] + dtb_ref[...]
    dt = jax.nn.softplus(dtraw)
    A = -jnp.exp(alog_ref[...])
    a = dt * A
    tril = (_iota2((CHUNK, CHUNK), 0) >= _iota2((CHUNK, CHUNK), 1)).astype(BF16)
    acs = _dot01(tril, a)
    acs_row = _to_rows(acs)
    return u, sig, xbc, dtraw, dt, A, acs, acs_row, taps


def _head_expander():
    return (_iota2((SSD_HEADS, D_SSD), 1) // SSD_P == _iota2((SSD_HEADS, D_SSD), 0)).astype(BF16)


def _hi_lo(x):
    hi = _bf(x)
    return hi, _bf(x - hi.astype(F32))


def _expand(v, e):
    hi, lo = _hi_lo(v)
    return _dot(hi, e) + _dot(lo, e)


def _headsum(t, e):
    m = t.shape[0]
    if m < 8:
        t = jnp.broadcast_to(t[0:1], (8, t.shape[1]))
    hi, lo = _hi_lo(t)
    return (_dot_nt(hi, e) + _dot_nt(lo, e))[0:m]


def _ssd_decays(dt, acs, dsk_ref, e):
    alast = acs[CHUNK - 1:CHUNK, :]
    stk = jnp.concatenate([dt, jnp.exp(acs), jnp.exp(alast - acs),
                           jnp.broadcast_to(jnp.exp(alast), (8, SSD_HEADS)),
                           jnp.broadcast_to(dsk_ref[...], (8, SSD_HEADS))], axis=0)
    ex = _expand(stk, e)
    return (ex[0:CHUNK], ex[CHUNK:2 * CHUNK], ex[2 * CHUNK:3 * CHUNK], ex[3 * CHUNK:3 * CHUNK + 1],
            ex[3 * CHUNK + 8:3 * CHUNK + 9])


def _ssd_fwd(z, xbc, dtp, conv_w, conv_b, dt_bias, a_log, d_skip, norm_w):
    L = z.shape[0]
    nc = L // CHUNK
    half = D_SSD // SSD_GROUPS

    def body(z_ref, xbc_ref, tail_ref, dt_ref, cw_ref, cb_ref, dtb_ref, alog_ref, dsk_ref, nw_ref,
             y_ref, ypre_ref, prev_ref, state, ybuf, mbuf, ext):
        c = pl.program_id(0)

        @pl.when(c == 0)
        def _():
            state[...] = jnp.zeros_like(state)

        u, sig, xbcv, dtraw, dt, A, acs, acs_row, _ = _ssd_chunk_pre(
            c == 0, xbc_ref, tail_ref, dt_ref, cw_ref, cb_ref, dtb_ref, alog_ref, ext=ext)
        e = _head_expander()
        dtE, eacsE, dsdE, ealE, dskE = _ssd_decays(dt, acs, dsk_ref, e)
        xs = xbcv[:, 0:D_SSD]
        X = xs * dtE
        prev_ref[0] = state[...]
        causal = _iota2((CHUNK, CHUNK), 0) >= _iota2((CHUNK, CHUNK), 1)
        for g in range(SSD_GROUPS):
            gs = slice(half * g, half * (g + 1))
            Bg = _bf(xbcv[:, D_SSD + SSD_N * g:D_SSD + SSD_N * (g + 1)])
            Cg = _bf(xbcv[:, D_SSD + D_BC + SSD_N * g:D_SSD + D_BC + SSD_N * (g + 1)])
            cb = _dot_nt(Cg, Bg)
            for r in range(SSD_R):
                h = g * SSD_R + r
                seg = acs[:, h:h + 1] - acs_row[h:h + 1, :]
                mbuf[h] = _bf(cb * jnp.where(causal, jnp.exp(jnp.where(causal, seg, 0.0)), 0.0))
            st = state[:, gs]
            ybuf[:, gs] = _dot(Cg, _bf(st)) * eacsE[:, gs] + dskE[:, gs] * xs[:, gs]
            state[:, gs] = st * ealE[:, gs] + _dot_tn(Bg, _bf(X[:, gs] * dsdE[:, gs]))
        Xb = _bf(X)
        for h in range(SSD_HEADS):
            hs = slice(SSD_P * h, SSD_P * (h + 1))
            ybuf[:, hs] += _dot(mbuf[h], Xb[:, hs])
        y = ybuf[...]
        ypre_ref[...] = y
        zv = z_ref[...]
        yf = y * (zv * _sigmoid(zv))
        for g in range(SSD_GROUPS):
            gs = slice(half * g, half * (g + 1))
            yg = yf[:, gs]
            ms = jnp.mean(yg * yg, axis=-1, keepdims=True)
            y_ref[:, gs] = _bf(yg * lax.rsqrt(ms + RMS_EPS) * nw_ref[:, gs])

    full = lambda shape: pl.BlockSpec(shape, lambda c: (0, 0))
    return pl.pallas_call(
        body, name="ssd_fwd", grid=(nc,),
        in_specs=[
            pl.BlockSpec((CHUNK, D_SSD), lambda c: (c, 0)),
            pl.BlockSpec((CHUNK, D_XBC), lambda c: (c, 0)),
            pl.BlockSpec((HALO, D_XBC), lambda c: (jnp.maximum(c * (CHUNK // HALO) - 1, 0), 0)),
            pl.BlockSpec((CHUNK, DT_PAD), lambda c: (c, 0)),
            full((CONV_K, D_XBC)), full((1, D_XBC)), full((1, SSD_HEADS)), full((1, SSD_HEADS)), full((1, SSD_HEADS)),
            full((1, D_SSD)),
        ],
        out_specs=[
            pl.BlockSpec((CHUNK, D_SSD), lambda c: (c, 0)),
            pl.BlockSpec((CHUNK, D_SSD), lambda c: (c, 0)),
            pl.BlockSpec((1, SSD_N, D_SSD), lambda c: (c, 0, 0)),
        ],
        out_shape=[
            jax.ShapeDtypeStruct((L, D_SSD), BF16),
            jax.ShapeDtypeStruct((L, D_SSD), F32),
            jax.ShapeDtypeStruct((nc, SSD_N, D_SSD), F32),
        ],
        scratch_shapes=[
            pltpu.VMEM((SSD_N, D_SSD), F32),
            pltpu.VMEM((CHUNK, D_SSD), F32),
            pltpu.VMEM((SSD_HEADS, CHUNK, CHUNK), BF16),
            pltpu.VMEM((CHUNK + HALO, D_XBC), F32),
        ],
        compiler_params=_cparams(("arbitrary",)),
    )(z, xbc, xbc, dtp, conv_w, conv_b, dt_bias, a_log, d_skip, norm_w)


def _ssd_bwd(dy, z, ypre, xbc, dtp, prev, conv_w, conv_b, dt_bias, a_log, d_skip, norm_w):
    L = z.shape[0]
    nc = L // CHUNK
    half = D_SSD // SSD_GROUPS

    def body(dy_ref, z_ref, ypre_ref, xbc_ref, tail_ref, dt_ref, prev_ref, cw_ref, cb_ref, dtb_ref, alog_ref, dsk_ref,
             nw_ref, smat_ref, smat2_ref, dz_ref, dxbc_ref, ddt_ref, gcw_ref, gcb_ref, gdtb_ref, galog_ref, gdsk_ref,
             gnw_ref, dstate, dhead, dpost, yobuf, bdbuf, lmbuf, dmbuf, cbbuf):
        i = pl.program_id(0)
        c = nc - 1 - i

        @pl.when(i == 0)
        def _():
            dstate[...] = jnp.zeros_like(dstate)
            dhead[...] = jnp.zeros_like(dhead)
            gcw_ref[...] = jnp.zeros_like(gcw_ref)
            gcb_ref[...] = jnp.zeros_like(gcb_ref)
            gdtb_ref[...] = jnp.zeros_like(gdtb_ref)
            galog_ref[...] = jnp.zeros_like(galog_ref)
            gdsk_ref[...] = jnp.zeros_like(gdsk_ref)
            gnw_ref[...] = jnp.zeros_like(gnw_ref)

        u, sig, xbcv, dtraw, dt, A, acs, acs_row, taps = _ssd_chunk_pre(
            c == 0, xbc_ref, tail_ref, dt_ref, cw_ref, cb_ref, dtb_ref, alog_ref, smat_ref)
        e = _head_expander()
        dtE, eacsE, dsdE, ealE, dskE = _ssd_decays(dt, acs, dsk_ref, e)
        alast = acs[CHUNK - 1:CHUNK, :]
        xs = xbcv[:, 0:D_SSD]
        X = xs * dtE
        Xb = _bf(X)

        zv = z_ref[...]
        ypre = ypre_ref[...]
        dyn = dy_ref[...]
        sz = _sigmoid(zv)
        silu_z = zv * sz
        yf = ypre * silu_z
        dyf_parts = []
        for g in range(SSD_GROUPS):
            gs = slice(half * g, half * (g + 1))
            yg = yf[:, gs]
            rstd = lax.rsqrt(jnp.mean(yg * yg, axis=-1, keepdims=True) + RMS_EPS)
            dout = dyn[:, gs]
            gnw_ref[:, gs] += jnp.sum(dout * yg * rstd, axis=0, keepdims=True)
            dyhat = dout * nw_ref[:, gs]
            dyf_parts.append(rstd * (dyhat - yg * (rstd * rstd) * jnp.mean(dyhat * yg, axis=-1, keepdims=True)))
        dyf = jnp.concatenate(dyf_parts, axis=1)
        dz_ref[...] = _bf(dyf * ypre * (sz * (1.0 + zv * (1.0 - sz))))
        dyp = dyf * silu_z
        dyb = _bf(dyp)
        G = dyp * eacsE

        causal = _iota2((CHUNK, CHUNK), 0) >= _iota2((CHUNK, CHUNK), 1)
        ST = prev_ref[0]
        dST = dstate[...]
        for g in range(SSD_GROUPS):
            gs = slice(half * g, half * (g + 1))
            bs = slice(D_SSD + SSD_N * g, D_SSD + SSD_N * (g + 1))
            cs = slice(D_SSD + D_BC + SSD_N * g, D_SSD + D_BC + SSD_N * (g + 1))
            Bg = _bf(xbcv[:, bs])
            Cg = _bf(xbcv[:, cs])
            Gb = _bf(G[:, gs])
            STb = _bf(ST[:, gs])
            dSTb = _bf(dST[:, gs])
            dstate[:, gs] = dST[:, gs] * ealE[:, gs] + _dot_tn(Cg, Gb)
            yobuf[:, gs] = _dot(Cg, STb) * eacsE[:, gs]
            bdbuf[:, gs] = _dot(Bg, dSTb)
            dpost[:, cs] = _dot_nt(Gb, STb)
            dpost[:, bs] = _dot_nt(_bf(X[:, gs] * dsdE[:, gs]), dSTb)
            cbbuf[g] = _dot_nt(Cg, Bg)
            for r in range(SSD_R):
                h = g * SSD_R + r
                seg = acs[:, h:h + 1] - acs_row[h:h + 1, :]
                lmbuf[h] = jnp.where(causal, jnp.exp(jnp.where(causal, seg, 0.0)), 0.0)
        for h in range(SSD_HEADS):
            hs = slice(SSD_P * h, SSD_P * (h + 1))
            Mb = _bf(cbbuf[h // SSD_R] * lmbuf[h])
            dmbuf[h] = _dot_nt(dyb[:, hs], Xb[:, hs])
            dpost[:, hs] = _dot_tn(Mb, dyb[:, hs])
        lane16 = _iota2((1, SSD_HEADS), 1)
        sub16 = _iota2((SSD_HEADS, 1), 0)
        dacs_col = jnp.zeros((CHUNK, SSD_HEADS), F32)
        dacs_row = jnp.zeros((SSD_HEADS, CHUNK), F32)
        for g in range(SSD_GROUPS):
            bs = slice(D_SSD + SSD_N * g, D_SSD + SSD_N * (g + 1))
            cs = slice(D_SSD + D_BC + SSD_N * g, D_SSD + D_BC + SSD_N * (g + 1))
            cb = cbbuf[g]
            dcb = jnp.zeros((CHUNK, CHUNK), F32)
            for r in range(SSD_R):
                h = g * SSD_R + r
                dM = dmbuf[h]
                Lm = lmbuf[h]
                dcb = dcb + dM * Lm
                dseg = dM * (cb * Lm)
                dacs_col = dacs_col + jnp.sum(dseg, axis=-1, keepdims=True) * (lane16 == h).astype(F32)
                dacs_row = dacs_row - jnp.sum(dseg, axis=0, keepdims=True) * (sub16 == h).astype(F32)
            dcbb = _bf(dcb)
            dpost[:, bs] += _dot_tn(dcbb, _bf(xbcv[:, cs]))
            dpost[:, cs] += _dot(dcbb, _bf(xbcv[:, bs]))

        BD = bdbuf[...]
        dX = dpost[:, 0:D_SSD] + dsdE * BD
        dsd = jnp.exp(alast - acs)
        T = _headsum(X * BD, e) * dsd
        dalast = jnp.sum(T, axis=0, keepdims=True) + _headsum(
            jnp.sum(dST * ST, axis=0, keepdims=True), e) * jnp.exp(alast)
        is_last = (_iota2((CHUNK, 1), 0) == CHUNK - 1).astype(F32)
        dacs = dacs_col + _to_cols(dacs_row) + _headsum(dyp * yobuf[...], e) - T + is_last * dalast
        triu = (_iota2((CHUNK, CHUNK), 0) <= _iota2((CHUNK, CHUNK), 1)).astype(BF16)
        da = _dot01(triu, dacs)
        ddt_tot = _headsum(dX * xs, e) + da * A
        galog_ref[...] += jnp.sum(da * dt, axis=0, keepdims=True) * A
        ddtraw = ddt_tot * _sigmoid(dtraw)
        gdtb_ref[...] += jnp.sum(ddtraw, axis=0, keepdims=True)
        gdsk_ref[...] += _headsum(jnp.sum(dyp * xs, axis=0, keepdims=True), e)
        ddt_ref[...] = jnp.zeros_like(ddt_ref)
        ddt_ref[:, 0:SSD_HEADS] = ddtraw
        dpost[:, 0:D_SSD] = dX * dtE + dskE * dyp

        dconv = dpost[...] * (sig * (1.0 + u * (1.0 - sig)))
        gcb_ref[...] += jnp.sum(dconv, axis=0, keepdims=True)
        for k in range(CONV_K):
            gcw_ref[k:k + 1, :] += jnp.sum(dconv * taps[k], axis=0, keepdims=True)
        later = _shifted_rows(dconv, dhead[...], smat2_ref)
        dx = cw_ref[CONV_K - 1:CONV_K, :] * dconv
        for k in range(CONV_K - 1):
            dx = dx + cw_ref[k:k + 1, :] * later[k]
        dxbc_ref[...] = _bf(dx)
        dhead[...] = dconv[0:HALO, :]

    full = lambda shape: pl.BlockSpec(shape, lambda i: (0, 0))
    rev = lambda wd: pl.BlockSpec((CHUNK, wd), lambda i: (nc - 1 - i, 0))
    return pl.pallas_call(
        body, name="ssd_bwd", grid=(nc,),
        in_specs=[
            rev(D_SSD), rev(D_SSD), rev(D_SSD), rev(D_XBC),
            pl.BlockSpec((HALO, D_XBC), lambda i: (jnp.maximum((nc - 1 - i) * (CHUNK // HALO) - 1, 0), 0)),
            rev(DT_PAD),
            pl.BlockSpec((1, SSD_N, D_SSD), lambda i: (nc - 1 - i, 0, 0)),
            full((CONV_K, D_XBC)), full((1, D_XBC)), full((1, SSD_HEADS)), full((1, SSD_HEADS)), full((1, SSD_HEADS)),
            full((1, D_SSD)), full((3 * CHUNK, 2 * (CHUNK + HALO))), full((3 * CHUNK, 2 * (CHUNK + HALO))),
        ],
        out_specs=[
            rev(D_SSD), rev(D_XBC), rev(DT_PAD),
            full((CONV_K, D_XBC)), full((1, D_XBC)), full((1, SSD_HEADS)), full((1, SSD_HEADS)), full((1, SSD_HEADS)),
            full((1, D_SSD)),
        ],
        out_shape=[
            jax.ShapeDtypeStruct((L, D_SSD), BF16), jax.ShapeDtypeStruct((L, D_XBC), BF16),
            jax.ShapeDtypeStruct((L, DT_PAD), F32),
            jax.ShapeDtypeStruct((CONV_K, D_XBC), F32), jax.ShapeDtypeStruct((1, D_XBC), F32),
            jax.ShapeDtypeStruct((1, SSD_HEADS), F32), jax.ShapeDtypeStruct((1, SSD_HEADS), F32),
            jax.ShapeDtypeStruct((1, SSD_HEADS), F32), jax.ShapeDtypeStruct((1, D_SSD), F32),
        ],
        scratch_shapes=[
            pltpu.VMEM((SSD_N, D_SSD), F32),
            pltpu.VMEM((HALO, D_XBC), F32),
            pltpu.VMEM((CHUNK, D_XBC), F32),
            pltpu.VMEM((CHUNK, D_SSD), F32),
            pltpu.VMEM((CHUNK, D_SSD), F32),
            pltpu.VMEM((SSD_HEADS, CHUNK, CHUNK), F32),
            pltpu.VMEM((SSD_HEADS, CHUNK, CHUNK), F32),
            pltpu.VMEM((SSD_GROUPS, CHUNK, CHUNK), F32),
        ],
        compiler_params=_cparams(("arbitrary",)),
    )(dy, z, ypre, xbc, xbc, dtp, prev, conv_w, conv_b, dt_bias, a_log, d_skip, norm_w, _shift_matrix((13, 14, 15)),
      _shift_matrix((3, 2, 1)))


def _rope_tables(pos_ref, inv_ref):
    ang = pos_ref[...].astype(F32) * inv_ref[...]
    d = _iota2((1, 2 * ATT_HD), 1) % ATT_HD
    s = jnp.sin(ang)
    return jnp.cos(ang), jnp.where(d < ROPE_DIM // 2, -s, 0.0), jnp.where((d >= ROPE_DIM // 2) & (d < ROPE_DIM), s, 0.0)


def _rope(t, tabs):
    c, s1, s2 = tabs
    n = t.shape[1]
    rep = n // c.shape[1]
    return (t * jnp.tile(c, (1, rep)) + pltpu.roll(t, n - ROPE_DIM // 2, 1) * jnp.tile(s1, (1, rep))
            + pltpu.roll(t, ROPE_DIM // 2, 1) * jnp.tile(s2, (1, rep)))


def _rope_t(t, tabs):
    c, s1, s2 = tabs
    n = t.shape[1]
    rep = n // c.shape[1]
    return (t * jnp.tile(c, (1, rep)) + pltpu.roll(t * jnp.tile(s1, (1, rep)), ROPE_DIM // 2, 1)
            + pltpu.roll(t * jnp.tile(s2, (1, rep)), n - ROPE_DIM // 2, 1))


def _stack_heads(t, j):
    return jnp.concatenate([t[:, ATT_HD * (j * ATT_R + r):ATT_HD * (j * ATT_R + r + 1)] for r in range(ATT_R)], axis=0)


def _swa_mask_t(first):
    si = _iota2((2 * WINDOW, ATT_R * WINDOW), 0)
    qi = _iota2((2 * WINDOW, ATT_R * WINDOW), 1) % WINDOW
    band = (si > qi) & (si <= qi + WINDOW)
    return band & (jnp.logical_not(first) | (si >= WINDOW))


def _head_rows(ref, j):
    if ref.shape[0] == 1:
        parts = [jnp.broadcast_to(ref[:, j * ATT_R + r:j * ATT_R + r + 1], (1, WINDOW)) for r in range(ATT_R)]
    else:
        parts = [ref[j * ATT_R + r:j * ATT_R + r + 1, :] for r in range(ATT_R)]
    return jnp.concatenate(parts, axis=1)


def _swa_fwd(q, g, kv, sinks):
    L = q.shape[0]
    nb = L // WINDOW
    scale = ATT_HD ** -0.5

    def body(q_ref, g_ref, kvc_ref, kvp_ref, sink_ref, y_ref, o_ref, lse_ref, otbuf):
        n = pl.program_id(0)
        kk = jnp.concatenate([kvp_ref[:, 0:D_KV], kvc_ref[:, 0:D_KV]], axis=0) * scale
        vv = jnp.concatenate([kvp_ref[:, D_KV:2 * D_KV], kvc_ref[:, D_KV:2 * D_KV]], axis=0)
        valid = _swa_mask_t(n == 0)
        qv = q_ref[...]
        for j in range(ATT_KVH):
            js = slice(ATT_HD * j, ATT_HD * (j + 1))
            st = _dot_nt(kk[:, js], _stack_heads(qv, j))
            st = jnp.where(valid, st, NEG_BIG)
            sink = _head_rows(sink_ref, j)
            m = jnp.maximum(jnp.max(st, axis=0, keepdims=True), sink)
            p = jnp.exp(st - m)
            vx = jnp.concatenate([vv[:, js], jnp.ones((2 * WINDOW, ATT_HD), BF16)], axis=1)
            otx = _dot_tn(vx, _bf(p))
            denom = otx[ATT_HD:ATT_HD + 1] + jnp.exp(sink - m)
            ot = otx[0:ATT_HD] * (1.0 / denom)
            lse = m + jnp.log(denom)
            for r in range(ATT_R):
                h = j * ATT_R + r
                otbuf[ATT_HD * h:ATT_HD * (h + 1), :] = ot[:, WINDOW * r:WINDOW * (r + 1)]
                lse_ref[h:h + 1, :] = lse[:, WINDOW * r:WINDOW * (r + 1)]
        o = otbuf[...].T
        o_ref[...] = o
        gv = g_ref[...]
        y_ref[...] = _bf(o * (gv * _sigmoid(gv)))

    cur = lambda wd: pl.BlockSpec((WINDOW, wd), lambda n: (n, 0))
    prv = lambda wd: pl.BlockSpec((WINDOW, wd), lambda n: (jnp.maximum(n - 1, 0), 0))
    return pl.pallas_call(
        body, name="swa_fwd", grid=(nb,),
        in_specs=[cur(D_ATT), cur(D_ATT), cur(2 * D_KV), prv(2 * D_KV), pl.BlockSpec((1, ATT_QH), lambda n: (0, 0))],
        out_specs=[cur(D_ATT), cur(D_ATT), pl.BlockSpec((ATT_QH, WINDOW), lambda n: (0, n))],
        out_shape=[jax.ShapeDtypeStruct((L, D_ATT), BF16), jax.ShapeDtypeStruct((L, D_ATT), F32),
                   jax.ShapeDtypeStruct((ATT_QH, L), F32)],
        scratch_shapes=[pltpu.VMEM((D_ATT, WINDOW), F32)],
        compiler_params=_cparams(("parallel",)),
    )(q, g, kv, kv, sinks)


def _swa_bwd(dy, q, g, kv, o, lse, pos, inv, sinks):
    L = q.shape[0]
    nb = L // WINDOW
    scale = ATT_HD ** -0.5

    def body(dy_ref, q_ref, g_ref, kvc_ref, kvp_ref, o_ref, lse_ref, posc_ref, posp_ref, inv_ref, sink_ref,
             dq_ref, dg_ref, dkv_ref, dsink_ref, carry, dqbuf, dkbuf, dvbuf):
        n = pl.program_id(0)

        @pl.when(n == 0)
        def _():
            dsink_ref[...] = jnp.zeros_like(dsink_ref)

        @pl.when(n < nb)
        def _():
            tc = _rope_tables(posc_ref, inv_ref)
            tp = _rope_tables(posp_ref, inv_ref)
            kk = jnp.concatenate([kvp_ref[:, 0:D_KV], kvc_ref[:, 0:D_KV]], axis=0) * scale
            vv = jnp.concatenate([kvp_ref[:, D_KV:2 * D_KV], kvc_ref[:, D_KV:2 * D_KV]], axis=0)
            valid = _swa_mask_t(n == 0)
            qv = q_ref[...]
            gv = g_ref[...]
            sg = _sigmoid(gv)
            dyv = dy_ref[...]
            ov = o_ref[...]
            dg_ref[...] = _bf(dyv * ov * (sg * (1.0 + gv * (1.0 - sg))))
            do = dyv * (gv * sg)
            dod = do * ov
            ones = jnp.ones((8, ATT_HD), BF16)
            lane16 = _iota2((1, ATT_QH), 1)
            dsink = jnp.zeros((1, ATT_QH), F32)
            for j in range(ATT_KVH):
                js = slice(ATT_HD * j, ATT_HD * (j + 1))
                kj = kk[:, js]
                vj = vv[:, js]
                qs = _stack_heads(qv, j)
                dos = _bf(_stack_heads(do, j))
                hi, lo = _hi_lo(_stack_heads(dod, j))
                delta = (_dot_nt(ones, hi) + _dot_nt(ones, lo))[0:1]
                lse = _head_rows(lse_ref, j)
                st = _dot_nt(kj, qs)
                pt = jnp.exp(jnp.where(valid, st, NEG_BIG) - lse)
                dst = _bf(pt * (_dot_nt(vj, dos) - delta))
                dqt = _dot_tn(kj, dst)
                dkbuf[:, js] = _dot(dst, qs) * scale
                dvbuf[:, js] = _dot(_bf(pt), dos)
                sd = jnp.exp(_head_rows(sink_ref, j) - lse) * delta
                for r in range(ATT_R):
                    h = j * ATT_R + r
                    ls = slice(WINDOW * r, WINDOW * (r + 1))
                    dqbuf[ATT_HD * h:ATT_HD * (h + 1), :] = dqt[:, ls]
                    dsink = dsink - jnp.sum(sd[:, ls], axis=1, keepdims=True) * (lane16 == h).astype(F32)
            dsink_ref[...] += dsink
            dq_ref[...] = _bf(_rope_t(dqbuf[...].T, tc))
            dkp = _rope_t(dkbuf[0:WINDOW, :], tp)
            dkc = _rope_t(dkbuf[WINDOW:2 * WINDOW, :], tc)

            @pl.when(n > 0)
            def _():
                dkv_ref[:, 0:D_KV] = _bf(carry[:, 0:D_KV] + dkp)
                dkv_ref[:, D_KV:2 * D_KV] = _bf(carry[:, D_KV:2 * D_KV] + dvbuf[0:WINDOW, :])

            carry[:, 0:D_KV] = dkc
            carry[:, D_KV:2 * D_KV] = dvbuf[WINDOW:2 * WINDOW, :]

        @pl.when(n == nb)
        def _():
            dkv_ref[...] = _bf(carry[...])

    last = nb - 1
    cur = lambda wd: pl.BlockSpec((WINDOW, wd), lambda n: (jnp.minimum(n, last), 0))
    prv = lambda wd: pl.BlockSpec((WINDOW, wd), lambda n: (jnp.maximum(jnp.minimum(n, last) - 1, 0), 0))
    return pl.pallas_call(
        body, name="swa_bwd", grid=(nb + 1,),
        in_specs=[cur(D_ATT), cur(D_ATT), cur(D_ATT), cur(2 * D_KV), prv(2 * D_KV), cur(D_ATT),
                  pl.BlockSpec((ATT_QH, WINDOW), lambda n: (0, jnp.minimum(n, last))), cur(1), prv(1),
                  pl.BlockSpec((1, 2 * ATT_HD), lambda n: (0, 0)), pl.BlockSpec((1, ATT_QH), lambda n: (0, 0))],
        out_specs=[cur(D_ATT), cur(D_ATT),
                   pl.BlockSpec((WINDOW, 2 * D_KV), lambda n: (jnp.maximum(n - 1, 0), 0)),
                   pl.BlockSpec((1, ATT_QH), lambda n: (0, 0))],
        out_shape=[jax.ShapeDtypeStruct((L, D_ATT), BF16), jax.ShapeDtypeStruct((L, D_ATT), BF16),
                   jax.ShapeDtypeStruct((L, 2 * D_KV), BF16), jax.ShapeDtypeStruct((1, ATT_QH), F32)],
        scratch_shapes=[pltpu.VMEM((WINDOW, 2 * D_KV), F32), pltpu.VMEM((D_ATT, WINDOW), F32),
                        pltpu.VMEM((2 * WINDOW, D_KV), F32), pltpu.VMEM((2 * WINDOW, D_KV), F32)],
        compiler_params=_cparams(("arbitrary",)),
    )(dy, q, g, kv, kv, o, lse, pos, pos, inv, sinks)


def _out_ln_loss(y_ssd, y_att, x, target, w_out, ln_g, ln_b):
    L = x.shape[0]
    tm = min(ROW_TILE, L)
    nt = L // tm
    inv_d = 1.0 / D_MODEL

    def body(ys_ref, ya_ref, x_ref, t_ref, w_ref, g_ref, b_ref, dr_ref, dys_ref, dya_ref, loss_ref, gg_ref, gb_ref,
             gwo_ref, acc_ref):
        i = pl.program_id(0)

        @pl.when(i == 0)
        def _():
            loss_ref[...] = jnp.zeros_like(loss_ref)
            gg_ref[...] = jnp.zeros_like(gg_ref)
            gb_ref[...] = jnp.zeros_like(gb_ref)
            acc_ref[...] = jnp.zeros_like(acc_ref)

        halves = [slice(0, tm // 2), slice(tm // 2, tm)]
        hs = [_dot(_bf(ys_ref[rs, :]), w_ref[0:D_SSD, :]) + _dot(_bf(ya_ref[rs, :]), w_ref[D_SSD:D_MIX, :]) for rs in halves]
        gam = g_ref[...]
        for rs, h in zip(halves, hs):
            r = ALPHA * x_ref[rs, :] + h
            mu = jnp.mean(r, axis=-1, keepdims=True)
            xc = r - mu
            rstd = lax.rsqrt(jnp.mean(xc * xc, axis=-1, keepdims=True) + LN_EPS)
            xhat = xc * rstd
            diff = xhat * gam + b_ref[...] - t_ref[rs, :]
            part = jnp.sum(jnp.sum(diff * diff, axis=-1, keepdims=True), axis=0, keepdims=True)
            loss_ref[...] += (0.5 * inv_d) * part
            dout = diff * inv_d
            gg_ref[...] += jnp.sum(dout * xhat, axis=0, keepdims=True)
            gb_ref[...] += jnp.sum(dout, axis=0, keepdims=True)
            dxh = dout * gam
            dr_ref[rs, :] = rstd * (dxh - jnp.mean(dxh, axis=-1, keepdims=True)
                                    - xhat * jnp.mean(dxh * xhat, axis=-1, keepdims=True))
        for rs in halves:
            drh = _bf(dr_ref[rs, :])
            dys_ref[rs, :] = _dot_nt(drh, w_ref[0:D_SSD, :])
            dya_ref[rs, :] = _dot_nt(drh, w_ref[D_SSD:D_MIX, :])
        drb = _bf(dr_ref[...])
        acc_ref[0:D_SSD, :] += _dot_tn(_bf(ys_ref[...]), drb)
        acc_ref[D_SSD:D_MIX, :] += _dot_tn(_bf(ya_ref[...]), drb)

        @pl.when(i == nt - 1)
        def _():
            gwo_ref[...] = _bf(acc_ref[...])

    row = pl.BlockSpec((tm, D_MODEL), lambda i: (i, 0))
    vec = pl.BlockSpec((1, D_MODEL), lambda i: (0, 0))
    return pl.pallas_call(
        body, name="out_ln_loss", grid=(nt,),
        in_specs=[row, row, row, row, pl.BlockSpec((D_MIX, D_MODEL), lambda i: (0, 0), pipeline_mode=pl.Buffered(1)), vec, vec],
        out_specs=[row, row, row, pl.BlockSpec((1, 128), lambda i: (0, 0)), vec, vec,
                   pl.BlockSpec((D_MIX, D_MODEL), lambda i: (0, 0))],
        out_shape=[jax.ShapeDtypeStruct((L, D_MODEL), F32)] * 3 + [jax.ShapeDtypeStruct((1, 128), F32)]
        + [jax.ShapeDtypeStruct((1, D_MODEL), F32)] * 2 + [jax.ShapeDtypeStruct((D_MIX, D_MODEL), BF16)],
        scratch_shapes=[pltpu.VMEM((D_MIX, D_MODEL), F32)],
        compiler_params=_cparams(("arbitrary",)),
    )(y_ssd, y_att, x, target, w_out, ln_g, ln_b)


def _local_step(x, pos, target, w, get_w_out, token, conv_w, conv_b, dt_bias, a_log, d_skip, norm_w, sinks, ln_g, ln_b):
    inv8 = ROPE_THETA ** (-jnp.arange(0, ROPE_DIM, 2, dtype=F32) / ROPE_DIM)
    inv = jnp.tile(jnp.concatenate([inv8, inv8, jnp.zeros((ATT_HD - ROPE_DIM,), F32)]), 2).reshape(1, 2 * ATT_HD)
    inv = inv + token

    z, g, q, xbc, kv, dtp, xb = _in_proj(x, w, pos, inv)
    y_ssd, y_pre, prev = _ssd_fwd(z, xbc, dtp, conv_w, conv_b, dt_bias, a_log, d_skip, norm_w)
    y_att, o, lse = _swa_fwd(q, g, kv, sinks)
    w_out = get_w_out(lse)
    dr, dy_ssd, dy_att, loss, g_ln_g, g_ln_b, gw_out = _out_ln_loss(y_ssd, y_att, x, target, w_out, ln_g, ln_b)
    w_out_red = _reduce_w_out_start(gw_out.reshape(N_CHIPS, W_OUT_ROWS, D_MODEL), loss)
    inv = inv + w_out_red[16][0:1, :]
    dq, dg, dkv, g_sinks = _swa_bwd(dy_att, q, g, kv, o, lse, pos, inv, sinks)
    dz, dxbc, ddt, g_conv_w, g_conv_b, g_dt_bias, g_a_log, g_d_skip, g_norm_w = _ssd_bwd(
        dy_ssd, z, y_pre, xbc, dtp, prev, conv_w, conv_b, dt_bias, a_log, d_skip, norm_w)
    gw_z, gw_g, gw_q = _matmuls_tn([dz, dg, dq], xb, "gw_zgq")
    gw_xbc, gw_kv, gw_dt = _matmuls_tn([dxbc, dkv, ddt], xb, "gw_xbc_kv_dt")
    gw_in = jnp.concatenate([gw_z, gw_xbc, gw_dt[0:SSD_HEADS], gw_q, gw_kv, gw_g], axis=0)
    small = dict(conv_w=g_conv_w, conv_b=g_conv_b, dt_bias=g_dt_bias, a_log=g_a_log, d_skip=g_d_skip,
                 ssd_norm_w=g_norm_w, attn_sinks=g_sinks, ln_g=g_ln_g, ln_b=g_ln_b)
    return loss, (dr, dz, dg, dq, dxbc, dkv, ddt, w), gw_in, w_out_red, small


def _mesh_pos():
    return lax.axis_index("x"), lax.axis_index("y"), lax.axis_index("c")


def _gather_weights(w_in_s, conv_w_s):
    hr = w_in_s.shape[0] // 2
    qa = 336
    quarters = ((0, qa), (qa, hr - qa))

    def body(win_ref, cw_ref, owin_ref, ocw_ref, stage, send_sems, recv_sems, small_send, small_recv, local_sems):
        x, y, c = _mesh_pos()
        me = 2 * x + y
        sibling = (x, y, 1 - c)
        xn, yn, dg = (1 - x, y), (x, 1 - y), (1 - x, 1 - y)
        chips = [xn, yn, dg]
        load = pltpu.make_async_copy(win_ref, stage, local_sems.at[1])
        load.start()
        locals_ = [pltpu.make_async_copy(cw_ref, ocw_ref.at[me], local_sems.at[0])]
        for cp in locals_:
            cp.start()
        started = []

        def piece(ref, chip, half, q):
            off, n = quarters[q]
            return ref.at[2 * chip[0] + chip[1]].at[pl.ds(half * hr + off, n), :]

        def mine(q):
            off, n = quarters[q]
            return win_ref.at[pl.ds(c * hr + off, n), :]

        def copy(src, dst, k, to):
            return pltpu.make_async_remote_copy(src_ref=src, dst_ref=dst, send_sem=send_sems.at[k], recv_sem=recv_sems.at[k],
                                                device_id=to, device_id_type=MESH)

        def go(cp):
            cp.start()
            started.append(cp)

        go(copy(mine(0), piece(owin_ref, (x, y), c, 0), 0, (*xn, c)))
        go(copy(mine(1), piece(owin_ref, (x, y), c, 1), 2, (*yn, c)))
        go(copy(mine(1), piece(owin_ref, (x, y), c, 1), 1, (*xn, c)))
        go(copy(mine(0), piece(owin_ref, (x, y), c, 0), 3, (*yn, c)))
        for j, (px, py) in enumerate(chips):
            cp = pltpu.make_async_remote_copy(
                src_ref=cw_ref, dst_ref=ocw_ref.at[me], send_sem=small_send.at[j], recv_sem=small_recv.at[j],
                device_id=(px, py, c), device_id_type=MESH)
            go(cp)
        load.wait()
        store = pltpu.make_async_copy(stage, owin_ref.at[me], local_sems.at[2])
        store.start()
        locals_.append(store)
        arrivals = [(0, xn, 0, (4, (*yn, c))), (2, yn, 1, (5, (*xn, c))), (1, xn, 1, None), (3, yn, 0, None),
                    (4, dg, 0, None), (5, dg, 1, None)]
        for n, (k, chip, q, onward) in enumerate(arrivals):
            blk = piece(owin_ref, chip, c, q)
            copy(blk, blk, k, sibling).wait_recv()
            if onward is not None:
                go(copy(blk, blk, onward[0], onward[1]))
            go(copy(blk, blk, 6 + n, sibling))
        for n, (k, chip, q, onward) in enumerate(arrivals):
            blk = piece(owin_ref, chip, 1 - c, q)
            copy(blk, blk, 6 + n, sibling).wait_recv()
        for j in range(3):
            pltpu.make_async_remote_copy(
                src_ref=cw_ref, dst_ref=ocw_ref.at[me], send_sem=small_send.at[j], recv_sem=small_recv.at[j],
                device_id=sibling, device_id_type=MESH).wait_recv()
        for cp in started:
            cp.wait_send()
        for cp in locals_:
            cp.wait()

    any_spec = pl.BlockSpec(memory_space=pl.ANY)
    return pl.pallas_call(
        body, name="gather_weights",
        in_specs=[any_spec] * 2, out_specs=[any_spec] * 2,
        out_shape=[jax.ShapeDtypeStruct((N_CHIPS,) + a.shape, a.dtype) for a in (w_in_s, conv_w_s)],
        scratch_shapes=[pltpu.VMEM(w_in_s.shape, w_in_s.dtype),
                        pltpu.SemaphoreType.DMA((12,)), pltpu.SemaphoreType.DMA((12,)),
                        pltpu.SemaphoreType.DMA((3,)), pltpu.SemaphoreType.DMA((3,)), pltpu.SemaphoreType.DMA((3,))],
    )(w_in_s, conv_w_s)


_HBM = pl.BlockSpec(memory_space=pltpu.HBM)
_SEM = pl.BlockSpec(memory_space=pltpu.SEMAPHORE)
_EFFECT = pltpu.SideEffectType.DATAFLOW_SIDE_EFFECTING


def _gather_w_out_start(w_out_s, after):
    def body(src_ref, land_ref, after_ref, s0, s1, s2, r0, r1, r2, src_thru, land_thru, token):
        x, y, c = _mesh_pos()
        me = 2 * x + y
        chips = [(1 - x, y), (x, 1 - y), (1 - x, 1 - y)]
        for (px, py), s, r in zip(chips, (s0, s1, s2), (r0, r1, r2)):
            pltpu.make_async_remote_copy(src_ref=src_ref, dst_ref=land_ref.at[me], send_sem=s, recv_sem=r,
                                         device_id=(px, py, c), device_id_type=MESH).start()
        token[...] = jnp.zeros_like(token)

    sem = pltpu.SemaphoreType.DMA(())
    land = lax.empty((N_CHIPS,) + w_out_s.shape, w_out_s.dtype)
    return pl.pallas_call(
        body, name="gather_w_out_start",
        out_shape=(sem,) * 6 + (pltpu.HBM(w_out_s.shape, w_out_s.dtype), pltpu.HBM(land.shape, land.dtype),
                                jax.ShapeDtypeStruct((8, 128), F32)),
        in_specs=(_HBM, _HBM, pl.BlockSpec(memory_space=pl.ANY)),
        out_specs=(_SEM,) * 6 + (_HBM, _HBM, pl.BlockSpec(memory_space=pltpu.VMEM)),
        input_output_aliases={0: 6, 1: 7},
        compiler_params=pltpu.CompilerParams(has_side_effects=_EFFECT),
    )(pltpu.with_memory_space_constraint(w_out_s, pltpu.HBM), pltpu.with_memory_space_constraint(land, pltpu.HBM), after)


def _gather_w_out_wait(sems, src_thru, land_thru, after):
    def body(src_ref, land_ref, s0, s1, s2, r0, r1, r2, after_ref, src_dead, got_ref):
        x, y, c = _mesh_pos()
        chips = [(1 - x, y), (x, 1 - y), (1 - x, 1 - y)]
        for (px, py), s, r in zip(chips, (s0, s1, s2), (r0, r1, r2)):
            cp = pltpu.make_async_remote_copy(src_ref=src_ref, dst_ref=land_ref.at[2 * px + py], send_sem=s, recv_sem=r,
                                              device_id=(px, py, c), device_id_type=MESH)
            cp.wait_send()
            cp.wait_recv()

    return pl.pallas_call(
        body, name="gather_w_out_wait",
        out_shape=(pltpu.HBM(src_thru.shape, src_thru.dtype), pltpu.HBM(land_thru.shape, land_thru.dtype)),
        in_specs=(_HBM, _HBM) + (_SEM,) * 6 + (pl.BlockSpec(memory_space=pl.ANY),),
        out_specs=(_HBM, _HBM), input_output_aliases={0: 0, 1: 1},
        compiler_params=pltpu.CompilerParams(has_side_effects=_EFFECT),
    )(src_thru, land_thru, *sems, after)[1]


def _pair_start(gw_in, after):
    hr = gw_in.shape[1] // 2

    def body(src_ref, land_ref, after_ref, *refs):
        x, y, c = _mesh_pos()
        for j in range(N_CHIPS):
            pltpu.make_async_remote_copy(
                src_ref=src_ref.at[j, pl.ds((1 - c) * hr, hr), :], dst_ref=land_ref.at[j], send_sem=refs[j],
                recv_sem=refs[N_CHIPS + j], device_id=(x, y, 1 - c), device_id_type=MESH).start()
        refs[10][...] = jnp.zeros_like(refs[10])

    sem = pltpu.SemaphoreType.DMA(())
    land = lax.empty((N_CHIPS, hr, D_MODEL), gw_in.dtype)
    return pl.pallas_call(
        body, name="pair_start",
        out_shape=(sem,) * 8 + (pltpu.HBM(gw_in.shape, gw_in.dtype), pltpu.HBM(land.shape, land.dtype),
                                jax.ShapeDtypeStruct((8, 128), F32)),
        in_specs=(_HBM, _HBM, pl.BlockSpec(memory_space=pl.ANY)),
        out_specs=(_SEM,) * 8 + (_HBM, _HBM, pl.BlockSpec(memory_space=pltpu.VMEM)),
        input_output_aliases={0: 8, 1: 9},
        compiler_params=pltpu.CompilerParams(has_side_effects=_EFFECT),
    )(pltpu.with_memory_space_constraint(gw_in, pltpu.HBM), pltpu.with_memory_space_constraint(land, pltpu.HBM), after)


def _pair_wait(sems, gw_thru, land_thru, after):
    hr = land_thru.shape[1]

    def body(src_ref, land_ref, *refs):
        x, y, c = _mesh_pos()
        for j in range(N_CHIPS):
            cp = pltpu.make_async_remote_copy(
                src_ref=src_ref.at[j, pl.ds((1 - c) * hr, hr), :], dst_ref=land_ref.at[j], send_sem=refs[j],
                recv_sem=refs[N_CHIPS + j], device_id=(x, y, 1 - c), device_id_type=MESH)
            cp.wait_send()
            cp.wait_recv()

    return pl.pallas_call(
        body, name="pair_wait",
        out_shape=(pltpu.HBM(gw_thru.shape, gw_thru.dtype), pltpu.HBM(land_thru.shape, land_thru.dtype)),
        in_specs=(_HBM, _HBM) + (_SEM,) * 8 + (pl.BlockSpec(memory_space=pl.ANY),),
        out_specs=(_HBM, _HBM), input_output_aliases={0: 0, 1: 1},
        compiler_params=pltpu.CompilerParams(has_side_effects=_EFFECT),
    )(gw_thru, land_thru, *sems, after)


def _chip_start(s_in, after):
    def body(src_ref, land_ref, after_ref, *refs):
        x, y, c = _mesh_pos()
        me = 2 * x + y
        for j, (px, py) in enumerate([(1 - x, y), (x, 1 - y), (1 - x, 1 - y)]):
            pltpu.make_async_remote_copy(
                src_ref=src_ref.at[2 * px + py], dst_ref=land_ref.at[me], send_sem=refs[j], recv_sem=refs[3 + j],
                device_id=(px, py, c), device_id_type=MESH).start()
        refs[8][...] = jnp.zeros_like(refs[8])

    sem = pltpu.SemaphoreType.DMA(())
    land = lax.empty(s_in.shape, s_in.dtype)
    return pl.pallas_call(
        body, name="chip_start",
        out_shape=(sem,) * 6 + (pltpu.HBM(s_in.shape, s_in.dtype), pltpu.HBM(land.shape, land.dtype),
                                jax.ShapeDtypeStruct((8, 128), F32)),
        in_specs=(_HBM, _HBM, pl.BlockSpec(memory_space=pl.ANY)),
        out_specs=(_SEM,) * 6 + (_HBM, _HBM, pl.BlockSpec(memory_space=pltpu.VMEM)),
        input_output_aliases={0: 6, 1: 7},
        compiler_params=pltpu.CompilerParams(has_side_effects=_EFFECT),
    )(pltpu.with_memory_space_constraint(s_in, pltpu.HBM), pltpu.with_memory_space_constraint(land, pltpu.HBM), after)


def _chip_wait(sems, s_thru, land_thru, after):
    def body(src_ref, land_ref, *refs):
        x, y, c = _mesh_pos()
        for j, (px, py) in enumerate([(1 - x, y), (x, 1 - y), (1 - x, 1 - y)]):
            cp = pltpu.make_async_remote_copy(
                src_ref=src_ref.at[2 * px + py], dst_ref=land_ref.at[2 * px + py], send_sem=refs[j], recv_sem=refs[3 + j],
                device_id=(px, py, c), device_id_type=MESH)
            cp.wait_send()
            cp.wait_recv()

    return pl.pallas_call(
        body, name="chip_wait",
        out_shape=(pltpu.HBM(s_thru.shape, s_thru.dtype), pltpu.HBM(land_thru.shape, land_thru.dtype)),
        in_specs=(_HBM, _HBM) + (_SEM,) * 6 + (pl.BlockSpec(memory_space=pl.ANY),),
        out_specs=(_HBM, _HBM), input_output_aliases={0: 0, 1: 1},
        compiler_params=pltpu.CompilerParams(has_side_effects=_EFFECT),
    )(s_thru, land_thru, *sems, after)


def _pair_share(h_in, small):
    def body(hin_ref, sm_ref, rin_ref, slots_ref, send_sems, recv_sems, small_send, small_recv, local_sem):
        x, y, c = _mesh_pos()
        dev = 4 * x + 2 * y + c
        mine = pltpu.make_async_copy(sm_ref, slots_ref.at[dev], local_sem)
        mine.start()
        share = pltpu.make_async_remote_copy(
            src_ref=hin_ref, dst_ref=rin_ref, send_sem=send_sems.at[0], recv_sem=recv_sems.at[0],
            device_id=(x, y, 1 - c), device_id_type=MESH)
        share.start()
        started = []
        for k in range(1, 8):
            peer = (x ^ ((k >> 2) & 1), y ^ ((k >> 1) & 1), c ^ (k & 1))
            cp = pltpu.make_async_remote_copy(
                src_ref=sm_ref, dst_ref=slots_ref.at[dev], send_sem=small_send.at[k - 1], recv_sem=small_recv.at[k - 1],
                device_id=peer, device_id_type=MESH)
            cp.start()
            started.append(cp)
        share.wait()
        for k in range(1, 8):
            pltpu.make_async_remote_copy(
                src_ref=sm_ref, dst_ref=slots_ref.at[dev], send_sem=small_send.at[k - 1], recv_sem=small_recv.at[k - 1],
                device_id=(x, y, 1 - c), device_id_type=MESH).wait_recv()
        for cp in started:
            cp.wait_send()
        mine.wait()

    any_spec = pl.BlockSpec(memory_space=pl.ANY)
    return pl.pallas_call(
        body, name="pair_share",
        in_specs=[any_spec] * 2, out_specs=[any_spec] * 2,
        out_shape=[jax.ShapeDtypeStruct(h_in.shape, F32), jax.ShapeDtypeStruct((8,) + small.shape, F32)],
        scratch_shapes=[pltpu.SemaphoreType.DMA((1,)), pltpu.SemaphoreType.DMA((1,)),
                        pltpu.SemaphoreType.DMA((7,)), pltpu.SemaphoreType.DMA((7,)), pltpu.SemaphoreType.DMA],
    )(h_in, small)


def _reduce_w_out_start(slabs, after):
    def body(src_ref, land_ref, after_ref, *refs):
        x, y, c = _mesh_pos()
        me = 4 * x + 2 * y + c
        for k in range(1, 8):
            px, py, pc = x ^ ((k >> 2) & 1), y ^ ((k >> 1) & 1), c ^ (k & 1)
            pltpu.make_async_remote_copy(src_ref=src_ref.at[2 * px + py], dst_ref=land_ref.at[me], send_sem=refs[k - 1],
                                         recv_sem=refs[6 + k], device_id=(px, py, pc), device_id_type=MESH).start()
        refs[16][...] = jnp.zeros_like(refs[16])

    sem = pltpu.SemaphoreType.DMA(())
    land = lax.empty((8,) + slabs.shape[1:], slabs.dtype)
    return pl.pallas_call(
        body, name="reduce_w_out_start",
        out_shape=(sem,) * 14 + (pltpu.HBM(slabs.shape, slabs.dtype), pltpu.HBM(land.shape, land.dtype),
                                 jax.ShapeDtypeStruct((8, 128), F32)),
        in_specs=(_HBM, _HBM, pl.BlockSpec(memory_space=pl.ANY)),
        out_specs=(_SEM,) * 14 + (_HBM, _HBM, pl.BlockSpec(memory_space=pltpu.VMEM)),
        input_output_aliases={0: 14, 1: 15},
        compiler_params=pltpu.CompilerParams(has_side_effects=_EFFECT),
    )(pltpu.with_memory_space_constraint(slabs, pltpu.HBM), pltpu.with_memory_space_constraint(land, pltpu.HBM), after)


def _reduce_w_out_wait(sems, slabs_thru, land_thru, after):
    def body(src_ref, land_ref, *refs):
        x, y, c = _mesh_pos()
        for k in range(1, 8):
            px, py, pc = x ^ ((k >> 2) & 1), y ^ ((k >> 1) & 1), c ^ (k & 1)
            cp = pltpu.make_async_remote_copy(
                src_ref=src_ref.at[2 * px + py], dst_ref=land_ref.at[4 * px + 2 * py + pc], send_sem=refs[k - 1],
                recv_sem=refs[6 + k], device_id=(px, py, pc), device_id_type=MESH)
            cp.wait_send()
            cp.wait_recv()

    return pl.pallas_call(
        body, name="reduce_w_out_wait",
        out_shape=(pltpu.HBM(slabs_thru.shape, slabs_thru.dtype), pltpu.HBM(land_thru.shape, land_thru.dtype)),
        in_specs=(_HBM, _HBM) + (_SEM,) * 14 + (pl.BlockSpec(memory_space=pl.ANY),),
        out_specs=(_HBM, _HBM), input_output_aliases={0: 0, 1: 1},
        compiler_params=pltpu.CompilerParams(has_side_effects=_EFFECT),
    )(slabs_thru, land_thru, *sems, after)


def _pair_add(g, recv, core, name):
    _, rows, C = recv.shape

    def body(core_ref, g_ref, r_ref, o_ref):
        o_ref[...] = _bf(g_ref[...].astype(F32) + r_ref[...].astype(F32))

    spec = pl.BlockSpec((1, rows, C), lambda j, core: (j, 0, 0))
    return pl.pallas_call(
        body, name=name,
        grid_spec=pltpu.PrefetchScalarGridSpec(
            num_scalar_prefetch=1, grid=(N_CHIPS,),
            in_specs=[pl.BlockSpec((1, rows, C), lambda j, core: (j, core[0], 0)), spec], out_specs=spec),
        out_shape=jax.ShapeDtypeStruct((N_CHIPS, rows, C), BF16),
        compiler_params=_cparams(("parallel",)),
    )(core, g, recv)


def _chip_add(own, parts, chip, name):
    _, rows, C = parts.shape
    tc = 512

    def body(chip_ref, own_ref, r0, r1, r2, r3, o_ref):
        acc = None
        for j, r in enumerate((r0, r1, r2, r3)):
            term = jnp.where(chip_ref[0] == j, own_ref[0], r[0]).astype(F32)
            acc = term if acc is None else acc + term
        o_ref[...] = acc

    def slab(j):
        return pl.BlockSpec((1, rows, tc), lambda i, chip: (jnp.where(chip[0] == j, (j + 1) % N_CHIPS, j), 0, i))

    return pl.pallas_call(
        body, name=name,
        grid_spec=pltpu.PrefetchScalarGridSpec(
            num_scalar_prefetch=1, grid=(C // tc,),
            in_specs=[pl.BlockSpec((1, rows, tc), lambda i, chip: (chip[0], 0, i))] + [slab(j) for j in range(N_CHIPS)],
            out_specs=pl.BlockSpec((rows, tc), lambda i, chip: (0, i))),
        out_shape=jax.ShapeDtypeStruct((rows, C), F32),
        compiler_params=_cparams(("parallel",)),
    )(chip, own, parts, parts, parts, parts)


def _adamw_math(w, g, m, v):
    m = ADAM_B1 * m + (1.0 - ADAM_B1) * g
    v = ADAM_B2 * v + (1.0 - ADAM_B2) * (g * g)
    m_hat = m / (1.0 - ADAM_B1 ** ADAM_STEP)
    v_hat = v / (1.0 - ADAM_B2 ** ADAM_STEP)
    delta = -ADAM_LR * (m_hat / (jnp.sqrt(v_hat) + ADAM_EPS) + ADAM_WD * w)
    return delta, m, v


def _adamw_rows(w, g_own, g_sib, m, v, core, name):
    R, C = w.shape[0], w.shape[-1]
    rows = g_own.shape[0]
    step = 256
    chunks = [(r, min(step, R - r)) for r in range(0, R, step)]
    sub = 64

    def body(core_ref, w_hbm, go_hbm, gs_hbm, m_hbm, v_hbm, d_hbm, nm_hbm, nv_hbm, g_hbm,
             wbuf, mbuf, vbuf, gbuf, dbuf, nmbuf, nvbuf, in_sems, g_sems, out_sems):
        c = core_ref[0]
        flat = lambda ref: ref.at[:, 0, :]
        g_in = [pltpu.make_async_copy(go_hbm, gbuf.at[pl.ds(pl.multiple_of(c * rows, 8), rows), :], g_sems.at[0]),
                pltpu.make_async_copy(gs_hbm, gbuf.at[pl.ds(pl.multiple_of((1 - c) * rows, 8), rows), :], g_sems.at[1])]
        for cp in g_in:
            cp.start()
        loads = []
        for k, (r0, n) in enumerate(chunks):
            cps = [pltpu.make_async_copy(flat(src).at[pl.ds(r0, n), :], dst.at[pl.ds(r0, n), :], in_sems.at[a, k])
                   for a, (src, dst) in enumerate(((w_hbm, wbuf), (m_hbm, mbuf), (v_hbm, vbuf)))]
            for cp in cps:
                cp.start()
            loads.append(cps)
        for cp in g_in:
            cp.wait()
        stores = []
        for k, (r0, n) in enumerate(chunks):
            for cp in loads[k]:
                cp.wait()

            def update(rs):
                g = gbuf[rs, :]
                dl, nm, nv = _adamw_math(wbuf[rs, :], g, mbuf[rs, :], vbuf[rs, :])
                dbuf[rs, :] = dl
                nmbuf[rs, :] = nm
                nvbuf[rs, :] = nv

            if n % sub == 0:
                def block(i, carry, r0=r0):
                    update(pl.ds(pl.multiple_of(r0 + i * sub, 8), sub))
                    return carry
                lax.fori_loop(0, n // sub, block, 0)
            else:
                update(pl.ds(r0, n))
            cps = [pltpu.make_async_copy(src.at[pl.ds(r0, n), :], flat(dst).at[pl.ds(r0, n), :], out_sems.at[a, k])
                   for a, (src, dst) in enumerate(((dbuf, d_hbm), (nmbuf, nm_hbm), (nvbuf, nv_hbm), (gbuf, g_hbm)))]
            for cp in cps:
                cp.start()
            stores += cps
        for cp in stores:
            cp.wait()

    any_spec = pl.BlockSpec(memory_space=pl.ANY)
    dense = pltpu.VMEM((R, C), F32)
    return pl.pallas_call(
        body, name=name,
        grid_spec=pltpu.PrefetchScalarGridSpec(
            num_scalar_prefetch=1, grid=(1,),
            in_specs=[any_spec] * 5, out_specs=[any_spec] * 4,
            scratch_shapes=[dense, dense, dense, pltpu.VMEM((2 * rows, C), F32), dense, dense, dense,
                            pltpu.SemaphoreType.DMA((3, len(chunks))), pltpu.SemaphoreType.DMA((2,)),
                            pltpu.SemaphoreType.DMA((4, len(chunks)))]),
        out_shape=[jax.ShapeDtypeStruct(w.shape, F32)] * 4,
        compiler_params=_cparams(),
    )(core, w, g_own, g_sib, m, v)


def _adamw_sum8(w, slabs, land, m, v, ids, name):
    R, C = w.shape
    tr = 128

    def body(ids_ref, w_ref, own_ref, *refs):
        lrefs, (m_ref, v_ref, d_ref, nm_ref, nv_ref, g_ref) = refs[:8], refs[8:]
        g = None
        for d, l_ref in enumerate(lrefs):
            term = jnp.where(ids_ref[0] == d, own_ref[0], l_ref[0]).astype(F32)
            g = term if g is None else g + term
        dl, nm, nv = _adamw_math(w_ref[...], g, m_ref[...], v_ref[...])
        d_ref[...] = dl
        nm_ref[...] = nm
        nv_ref[...] = nv
        g_ref[...] = g

    def slot(d):
        return pl.BlockSpec((1, tr, C), lambda i, ids: (jnp.where(ids[0] == d, (d + 1) % 8, d), i, 0))

    spec = pl.BlockSpec((tr, C), lambda i, ids: (i, 0))
    return pl.pallas_call(
        body, name=name,
        grid_spec=pltpu.PrefetchScalarGridSpec(
            num_scalar_prefetch=1, grid=(R // tr,),
            in_specs=[spec, pl.BlockSpec((1, tr, C), lambda i, ids: (ids[1], i, 0))] + [slot(d) for d in range(8)]
            + [spec, spec],
            out_specs=[spec] * 4),
        out_shape=[jax.ShapeDtypeStruct((R, C), F32)] * 4,
        compiler_params=_cparams(("parallel",)),
    )(ids, w, slabs, *([land] * 8), m, v)


SMALL_NAMES = ("conv_b", "ssd_norm_w", "ln_g", "ln_b", "dt_bias", "a_log", "d_skip", "attn_sinks")
SMALL_FIELDS = ((4, 0, D_XBC), (5, 0, D_SSD), (6, 0, D_MODEL), (7, 0, D_MODEL), (5, 1024, SSD_HEADS), (5, 1152, SSD_HEADS),
                (5, 1280, SSD_HEADS), (5, 1408, ATT_QH))
LOSS_FIELD = (6, 1024, 128)
K_SMALL = D_XBC


def _pack_small(g_conv_w, vecs, loss):
    def body(cw_ref, *refs):
        o_ref = refs[-1]
        o_ref[...] = jnp.zeros_like(o_ref)
        o_ref[0:CONV_K, 0:D_XBC] = cw_ref[...]
        for v_ref, (row, off, n) in zip(refs[:-2], SMALL_FIELDS):
            o_ref[row:row + 1, off:off + n] = v_ref[...]
        o_ref[LOSS_FIELD[0]:LOSS_FIELD[0] + 1, LOSS_FIELD[1]:LOSS_FIELD[1] + LOSS_FIELD[2]] = refs[-2][...]

    return pl.pallas_call(
        body, name="pack_small", out_shape=jax.ShapeDtypeStruct((8, K_SMALL), F32), compiler_params=_cparams(),
    )(g_conv_w, *vecs, loss)


def _adamw_small(slots, chip, conv_w, m_conv_w, v_conv_w, params, moms, vars_):
    n_vec = len(SMALL_NAMES)

    def body(chip_ref, s_ref, *refs):
        ins = refs[:3 * (n_vec + 1)]
        outs = refs[3 * (n_vec + 1):-1]
        tot_ref = refs[-1]
        tot = s_ref[0]
        for d in range(1, 8):
            tot = tot + s_ref[d]
        outs[0][...] = tot[LOSS_FIELD[0]:LOSS_FIELD[0] + 1, LOSS_FIELD[1]:LOSS_FIELD[1] + 1]
        off = pl.multiple_of(chip_ref[0] * CONV_COLS, 128)
        tot_ref[...] = tot
        grads = [tot_ref[0:CONV_K, pl.ds(off, CONV_COLS)]]
        grads += [tot[row:row + 1, o:o + n] for row, o, n in SMALL_FIELDS]
        for k, g in enumerate(grads):
            w_ref, m_ref, v_ref = ins[3 * k:3 * k + 3]
            full = (0,) if k == 0 else (Ellipsis,)
            d, nm, nv = _adamw_math(w_ref[full], g, m_ref[full], v_ref[full])
            for o_ref, val in zip(outs[1 + 4 * k:5 + 4 * k], (g, d, nm, nv)):
                o_ref[full] = val

    args = [conv_w, m_conv_w, v_conv_w]
    for w, m, v in zip(params, moms, vars_):
        args += [w, m, v]
    shapes = [jax.ShapeDtypeStruct((1, 1), F32)] + [jax.ShapeDtypeStruct(conv_w.shape, F32)] * 4
    for w in params:
        shapes += [jax.ShapeDtypeStruct(w.shape, F32)] * 4
    vmem = pl.BlockSpec(memory_space=pltpu.VMEM)
    return pl.pallas_call(
        body, name="adamw_small",
        grid_spec=pltpu.PrefetchScalarGridSpec(
            num_scalar_prefetch=1, grid=(1,),
            in_specs=[pl.BlockSpec(slots.shape, lambda i, chip: (0, 0, 0))] + [vmem] * len(args),
            out_specs=[vmem] * len(shapes), scratch_shapes=[pltpu.VMEM((8, K_SMALL), F32)]),
        out_shape=shapes, compiler_params=_cparams(),
    )(chip, slots, *args)


def kernel(x, positions, w_in, conv_w, conv_b, dt_bias, a_log, d_skip, ssd_norm_w, attn_sinks, w_out, ln_g, ln_b, loss_target, m_w_in, m_conv_w, m_conv_b, m_dt_bias, m_a_log, m_d_skip, m_ssd_norm_w, m_attn_sinks, m_w_out, m_ln_g, m_ln_b, v_w_in, v_conv_w, v_conv_b, v_dt_bias, v_a_log, v_d_skip, v_ssd_norm_w, v_attn_sinks, v_w_out, v_ln_g, v_ln_b):
    mx, my, mc = _mesh_pos()
    chip = 2 * mx + my
    L = x.shape[1]

    conv_w_s8 = jnp.pad(conv_w[0], ((0, 8 - CONV_K), (0, 0)))
    pad_rows = ((0, SLAB_ROWS - W_IN_COLS), (0, 0))
    w_in_t = w_in[0].T
    w_in_b, w_out_b = jnp.pad(_bf(w_in_t), pad_rows), _bf(w_out[0])
    ag_in, ag_cw = _gather_weights(w_in_b, conv_w_s8)
    started = _gather_w_out_start(w_out_b, ag_cw)
    own = (jnp.arange(N_CHIPS) == chip)[:, None, None]

    def get_w_out(after):
        landed = _gather_w_out_wait(started[0:6], started[6], started[7], after)
        return jnp.where(own, w_out_b[None], landed).reshape(D_MIX, D_MODEL)

    w_full = jnp.concatenate([ag_in[j, 0:W_IN_COLS] for j in range(N_CHIPS)], axis=0)
    w = jnp.concatenate([
        w_full[O_Z:O_Z + D_SSD], w_full[O_G:O_G + D_ATT], w_full[O_Q:O_Q + D_ATT],
        w_full[O_XBC:O_XBC + D_XBC], w_full[O_K:O_K + 2 * D_KV], w_full[O_DT:O_DT + SSD_HEADS],
        jnp.zeros((DT_PAD - SSD_HEADS, D_MODEL), BF16)], axis=0)
    conv_w_full = jnp.concatenate([ag_cw[j, 0:CONV_K] for j in range(N_CHIPS)], axis=1)

    loss_part, gx_args, gw_in, w_out_red, small = _local_step(
        x[0], positions[0].reshape(L, 1), loss_target[0], w, get_w_out, started[8][0:1, :], conv_w_full,
        conv_b, dt_bias, a_log, d_skip, ssd_norm_w, attn_sinks, ln_g, ln_b)

    packed = _pack_small(small["conv_w"], [small[n] for n in SMALL_NAMES], loss_part)
    core_id = mc.reshape(1).astype(jnp.int32)
    chip_id = chip.reshape(1).astype(jnp.int32)
    ids = jnp.stack([4 * mx + 2 * my + mc, chip]).astype(jnp.int32)
    slabs = jnp.stack([jnp.pad(gw_in[W_IN_COLS * j:W_IN_COLS * (j + 1)], pad_rows) for j in range(N_CHIPS)])
    w_in_red = _pair_start(slabs, packed)
    grad_x = _grad_x(*gx_args, w_in_red[10], 0)
    gw_in_slabs, recv_in = _pair_wait(w_in_red[0:8], w_in_red[8], w_in_red[9], grad_x[0:8, 0:128])
    s_in = _pair_add(gw_in_slabs, recv_in, core_id, "pair_add_in")
    chip_red = _chip_start(s_in, packed)
    grad_x = _grad_x(*gx_args, chip_red[8], 1, grad_x)
    own_slabs, landed = _reduce_w_out_wait(w_out_red[0:14], w_out_red[14], w_out_red[15], grad_x)
    out_t = _adamw_sum8(w_out[0], own_slabs, landed, m_w_out[0], v_w_out[0], ids, "adamw_w_out")
    d_w_out, nm_w_out, nv_w_out, g_w_out = [a[None] for a in out_t]
    s_in, r_in = _chip_wait(chip_red[0:6], chip_red[6], chip_red[7], out_t[0])
    h_in = _chip_add(s_in, r_in, chip_id, "chip_add_in")
    sib_in, slots = _pair_share(h_in, packed)

    to_rows = lambda a: jnp.transpose(a, (2, 0, 1))
    in_t = _adamw_rows(to_rows(w_in), h_in, sib_in, to_rows(m_w_in), to_rows(v_w_in), core_id, "adamw_w_in")
    d_w_in, nm_w_in, nv_w_in, g_w_in = [jnp.transpose(a, (1, 2, 0)) for a in in_t]

    params = dict(conv_b=conv_b, ssd_norm_w=ssd_norm_w, ln_g=ln_g, ln_b=ln_b, dt_bias=dt_bias, a_log=a_log,
                  d_skip=d_skip, attn_sinks=attn_sinks)
    moms = dict(conv_b=m_conv_b, ssd_norm_w=m_ssd_norm_w, ln_g=m_ln_g, ln_b=m_ln_b, dt_bias=m_dt_bias, a_log=m_a_log,
                d_skip=m_d_skip, attn_sinks=m_attn_sinks)
    vars_ = dict(conv_b=v_conv_b, ssd_norm_w=v_ssd_norm_w, ln_g=v_ln_g, ln_b=v_ln_b, dt_bias=v_dt_bias, a_log=v_a_log,
                 d_skip=v_d_skip, attn_sinks=v_attn_sinks)
    res = _adamw_small(slots, chip_id, conv_w, m_conv_w, v_conv_w, [params[n] for n in SMALL_NAMES],
                       [moms[n] for n in SMALL_NAMES], [vars_[n] for n in SMALL_NAMES])
    loss = res[0][0, 0]
    grads, delta, new_m, new_v = {}, {}, {}, {}
    for k, n in enumerate(("conv_w",) + SMALL_NAMES):
        grads[n], delta[n], new_m[n], new_v[n] = res[1 + 4 * k:5 + 4 * k]
    for dd, a_in, a_out in ((grads, g_w_in, g_w_out), (delta, d_w_in, d_w_out), (new_m, nm_w_in, nm_w_out),
                            (new_v, nv_w_in, nv_w_out)):
        dd["w_in"] = a_in
        dd["w_out"] = a_out
    order = ("w_in", "conv_w", "conv_b", "dt_bias", "a_log", "d_skip", "ssd_norm_w", "attn_sinks", "w_out", "ln_g", "ln_b")
    return (loss, grad_x[None], *[grads[n] for n in order], *[delta[n] for n in order], *[new_m[n] for n in order],
            *[new_v[n] for n in order])
```

```python
import numpy as np
import jax
import jax.numpy as jnp
from jax import lax
from jax.experimental import pallas as pl
from jax.experimental.pallas import tpu as pltpu

F32 = jnp.float32
BF16 = jnp.bfloat16
MESH = pl.DeviceIdType.MESH

D_MODEL = 1024
D_SSD = 1024
D_ATT = 1024
D_MIX = 2048
SSD_HEADS = 16
SSD_P = 64
SSD_GROUPS = 2
SSD_R = 8
SSD_N = 128
D_BC = 256
D_XBC = 1536
CONV_K = 4
CHUNK = 128
ATT_HD = 64
assert ATT_HD in (4, 16, 64, 256)
ATT_QH = 16
ATT_KVH = 4
ATT_R = 4
D_KV = 256
WINDOW = 128
ROPE_THETA = 500000.0
ROPE_DIM = 16
ALPHA = 2.0 ** 0.25
LN_EPS = 1e-5
RMS_EPS = 1e-5
D_IN_PROJ = 5136
O_Z, O_XBC, O_DT, O_Q, O_K, O_V, O_G = 0, 1024, 2560, 2576, 3600, 3856, 4112
P_Z, P_G, P_Q, P_XBC, P_KV, P_DT, P_END = 0, 1024, 2048, 3072, 4608, 5120, 5248
DT_PAD = 128
N_CHIPS = 4
W_IN_COLS = D_IN_PROJ // N_CHIPS
SLAB_ROWS = 1312
W_OUT_ROWS = D_MIX // N_CHIPS
CONV_COLS = D_XBC // N_CHIPS

ADAM_LR = 0.001
ADAM_B1 = 0.9
ADAM_B2 = 0.999
ADAM_EPS = 1e-08
ADAM_WD = 0.01
ADAM_STEP = 10

VMEM_LIMIT = 56 * 1024 * 1024
ROW_TILE = 512
NEG_BIG = -1e30


def _cparams(sem=None, **kw):
    if sem is not None:
        kw["dimension_semantics"] = sem
    return pltpu.CompilerParams(vmem_limit_bytes=VMEM_LIMIT, **kw)


def _dot(a, b):
    return jnp.dot(a, b, preferred_element_type=F32)


def _dot_nt(a, b):
    return lax.dot_general(a, b, (((1,), (1,)), ((), ())), preferred_element_type=F32)


def _dot_tn(a, b):
    return lax.dot_general(a, b, (((0,), (0,)), ((), ())), preferred_element_type=F32)


def _bf(a):
    return a.astype(BF16)


def _iota2(shape, dim):
    return lax.broadcasted_iota(jnp.int32, shape, dim)


def _three_terms(x):
    hi = _bf(x)
    r = x - hi.astype(F32)
    mid = _bf(r)
    return hi, mid, _bf(r - mid.astype(F32))


def _dot01(m, a):
    return sum(_dot(m, t) for t in _three_terms(a))


def _to_rows(col):
    k = col.shape[1]
    eye = (_iota2((k, k), 0) == _iota2((k, k), 1)).astype(BF16)
    return sum(_dot_nt(eye, t) for t in _three_terms(col))


def _to_cols(row):
    n = row.shape[1]
    eye = (_iota2((n, n), 0) == _iota2((n, n), 1)).astype(BF16)
    return sum(_dot_nt(eye, t) for t in _three_terms(row))


def _sigmoid(x):
    return jax.nn.sigmoid(x)


def _in_proj(x, w, pos, inv):
    L = x.shape[0]
    tm = ROW_TILE
    widths = (D_SSD, D_ATT, D_ATT, D_XBC, 2 * D_KV, DT_PAD)

    def body(x_ref, w_ref, pos_ref, inv_ref, z_ref, g_ref, q_ref, xbc_ref, kv_ref, dt_ref, xb_ref):
        xb = _bf(x_ref[...])
        xb_ref[...] = xb
        tabs = _rope_tables(pos_ref, inv_ref)
        q_ref[...] = _bf(_rope(_dot_nt(xb, w_ref[P_Q:P_Q + D_ATT, :]), tabs))
        kv_ref[:, 0:D_KV] = _bf(_rope(_dot_nt(xb, w_ref[P_KV:P_KV + D_KV, :]), tabs))
        kv_ref[:, D_KV:2 * D_KV] = _bf(_dot_nt(xb, w_ref[P_KV + D_KV:P_KV + 2 * D_KV, :]))
        for o_ref, off, wd in zip((z_ref, g_ref, xbc_ref, dt_ref), (P_Z, P_G, P_XBC, P_DT), (D_SSD, D_ATT, D_XBC, DT_PAD)):
            o_ref[...] = _dot_nt(xb, w_ref[off:off + wd, :])

    row = lambda wd: pl.BlockSpec((tm, wd), lambda i: (i, 0))
    return pl.pallas_call(
        body, name="in_proj", grid=(L // tm,),
        in_specs=[row(D_MODEL), pl.BlockSpec((P_END, D_MODEL), lambda i: (0, 0), pipeline_mode=pl.Buffered(1)), row(1),
                  pl.BlockSpec((1, 2 * ATT_HD), lambda i: (0, 0))],
        out_specs=[row(wd) for wd in widths] + [row(D_MODEL)],
        out_shape=[jax.ShapeDtypeStruct((L, wd), dt) for wd, dt in zip(widths, (F32, F32, BF16, F32, BF16, F32))]
        + [jax.ShapeDtypeStruct((L, D_MODEL), BF16)],
        compiler_params=_cparams(("parallel",)),
    )(x, w, pos, inv)


def _matmuls_tn(a_list, b, name):
    K, N = b.shape
    tk = min(K, 1024)
    n = len(a_list)

    def body(*refs):
        b_ref = refs[n]
        o_refs, acc_refs = refs[n + 1:2 * n + 1], refs[2 * n + 1:]

        @pl.when(pl.program_id(0) == 0)
        def _():
            for acc in acc_refs:
                acc[...] = jnp.zeros_like(acc)

        bb = _bf(b_ref[...])
        for a_ref, o_ref, acc in zip(refs[:n], o_refs, acc_refs):
            total = acc[...] + _dot_tn(_bf(a_ref[...]), bb)
            acc[...] = total
            o_ref[...] = _bf(total)

    return pl.pallas_call(
        body, name=name, grid=(K // tk,),
        in_specs=[pl.BlockSpec((tk, a.shape[1]), lambda k: (k, 0)) for a in a_list] + [pl.BlockSpec((tk, N), lambda k: (k, 0))],
        out_specs=[pl.BlockSpec((a.shape[1], N), lambda k: (0, 0)) for a in a_list],
        out_shape=[jax.ShapeDtypeStruct((a.shape[1], N), BF16) for a in a_list],
        scratch_shapes=[pltpu.VMEM((a.shape[1], N), F32) for a in a_list],
        compiler_params=_cparams(("arbitrary",)),
    )(*a_list, b)


def _grad_x(dr, dz, dg, dq, dxbc, dkv, ddt, w, after):
    L = dr.shape[0]
    tm = min(ROW_TILE, L)
    widths = (D_SSD, D_ATT, D_ATT, D_XBC, 2 * D_KV, DT_PAD)
    offs = (P_Z, P_G, P_Q, P_XBC, P_KV, P_DT)

    def body(dr_ref, dz_ref, dg_ref, dq_ref, dxbc_ref, dkv_ref, ddt_ref, w_ref, after_ref, o_ref):
        acc = ALPHA * dr_ref[...]
        for p_ref, off, wd in zip((dz_ref, dg_ref, dq_ref, dxbc_ref, dkv_ref, ddt_ref), offs, widths):
            acc = acc + _dot(_bf(p_ref[...]), w_ref[off:off + wd, :])
        o_ref[...] = acc

    row = lambda wd: pl.BlockSpec((tm, wd), lambda i: (i, 0))
    specs = ([row(D_MODEL)] + [row(wd) for wd in widths]
             + [pl.BlockSpec((P_END, D_MODEL), lambda i: (0, 0), pipeline_mode=pl.Buffered(1)),
                pl.BlockSpec((8, 128), lambda i: (0, 0))])
    return pl.pallas_call(
        body, name="grad_x", grid=(L // tm,),
        in_specs=specs, out_specs=row(D_MODEL),
        out_shape=jax.ShapeDtypeStruct((L, D_MODEL), F32),
        compiler_params=_cparams(("parallel",)),
    )(dr, dz, dg, dq, dxbc, dkv, ddt, w, after)


HALO = 16


def _shift_matrix(offsets):
    n = CHUNK + HALO
    m = np.zeros((len(offsets) * CHUNK, 2 * n), np.float32)
    for k, off in enumerate(offsets):
        t = np.arange(CHUNK)
        m[k * CHUNK + t, t + off] = 1.0
        m[k * CHUNK + t, n + t + off] = 1.0
    return jnp.asarray(m, BF16)


def _shifted_rows(first_part, second_part, smat_ref):
    h1, l1 = _hi_lo(first_part)
    h2, l2 = _hi_lo(second_part)
    sh = _dot(smat_ref[...], jnp.concatenate([h1, h2, l1, l2], axis=0))
    return sh[0:CHUNK], sh[CHUNK:2 * CHUNK], sh[2 * CHUNK:3 * CHUNK]


def _ssd_chunk_pre(first, xbc_ref, tail_ref, dt_ref, cw_ref, cb_ref, dtb_ref, alog_ref, smat_ref=None, ext=None):
    tail = jnp.where(first, 0.0, tail_ref[...])
    x = xbc_ref[...]
    if ext is None:
        taps = _shifted_rows(tail, x, smat_ref) + (x,)
    else:
        ext[0:HALO, :] = tail
        ext[HALO:HALO + CHUNK, :] = x
        taps = tuple(ext[pl.ds(HALO - (CONV_K - 1) + k, CHUNK), :] for k in range(CONV_K - 1)) + (x,)
    u = cb_ref[...] + cw_ref[0:1, :] * taps[0]
    for k in range(1, CONV_K):
        u = u + cw_ref[k:k + 1, :] * taps[k]
    sig = _sigmoid(u)
    xbc = u * sig
    dtraw = dt_ref[:, 0:SSD_HEADS] + dtb_ref[...]
    dt = jax.nn.softplus(dtraw)
    A = -jnp.exp(alog_ref[...])
    a = dt * A
    tril = (_iota2((CHUNK, CHUNK), 0) >= _iota2((CHUNK, CHUNK), 1)).astype(BF16)
    acs = _dot01(tril, a)
    acs_row = _to_rows(acs)
    return u, sig, xbc, dtraw, dt, A, acs, acs_row, taps


def _head_expander():
    return (_iota2((SSD_HEADS, D_SSD), 1) // SSD_P == _iota2((SSD_HEADS, D_SSD), 0)).astype(BF16)


def _hi_lo(x):
    hi = _bf(x)
    return hi, _bf(x - hi.astype(F32))


def _expand(v, e):
    hi, lo = _hi_lo(v)
    return _dot(hi, e) + _dot(lo, e)


def _headsum(t, e):
    m = t.shape[0]
    if m < 8:
        t = jnp.broadcast_to(t[0:1], (8, t.shape[1]))
    hi, lo = _hi_lo(t)
    return (_dot_nt(hi, e) + _dot_nt(lo, e))[0:m]


def _ssd_decays(dt, acs, dsk_ref, e):
    alast = acs[CHUNK - 1:CHUNK, :]
    stk = jnp.concatenate([dt, jnp.exp(acs), jnp.exp(alast - acs),
                           jnp.broadcast_to(jnp.exp(alast), (8, SSD_HEADS)),
                           jnp.broadcast_to(dsk_ref[...], (8, SSD_HEADS))], axis=0)
    ex = _expand(stk, e)
    return (ex[0:CHUNK], ex[CHUNK:2 * CHUNK], ex[2 * CHUNK:3 * CHUNK], ex[3 * CHUNK:3 * CHUNK + 1],
            ex[3 * CHUNK + 8:3 * CHUNK + 9])


def _ssd_fwd(z, xbc, dtp, conv_w, conv_b, dt_bias, a_log, d_skip, norm_w):
    L = z.shape[0]
    nc = L // CHUNK
    half = D_SSD // SSD_GROUPS

    def body(z_ref, xbc_ref, tail_ref, dt_ref, cw_ref, cb_ref, dtb_ref, alog_ref, dsk_ref, nw_ref,
             y_ref, ypre_ref, prev_ref, state, ybuf, mbuf, ext):
        c = pl.program_id(0)

        @pl.when(c == 0)
        def _():
            state[...] = jnp.zeros_like(state)

        u, sig, xbcv, dtraw, dt, A, acs, acs_row, _ = _ssd_chunk_pre(
            c == 0, xbc_ref, tail_ref, dt_ref, cw_ref, cb_ref, dtb_ref, alog_ref, ext=ext)
        e = _head_expander()
        dtE, eacsE, dsdE, ealE, dskE = _ssd_decays(dt, acs, dsk_ref, e)
        xs = xbcv[:, 0:D_SSD]
        X = xs * dtE
        prev_ref[0] = state[...]
        causal = _iota2((CHUNK, CHUNK), 0) >= _iota2((CHUNK, CHUNK), 1)
        for g in range(SSD_GROUPS):
            gs = slice(half * g, half * (g + 1))
            Bg = _bf(xbcv[:, D_SSD + SSD_N * g:D_SSD + SSD_N * (g + 1)])
            Cg = _bf(xbcv[:, D_SSD + D_BC + SSD_N * g:D_SSD + D_BC + SSD_N * (g + 1)])
            cb = _dot_nt(Cg, Bg)
            for r in range(SSD_R):
                h = g * SSD_R + r
                seg = acs[:, h:h + 1] - acs_row[h:h + 1, :]
                mbuf[h] = _bf(cb * jnp.where(causal, jnp.exp(jnp.where(causal, seg, 0.0)), 0.0))
            st = state[:, gs]
            ybuf[:, gs] = _dot(Cg, _bf(st)) * eacsE[:, gs] + dskE[:, gs] * xs[:, gs]
            state[:, gs] = st * ealE[:, gs] + _dot_tn(Bg, _bf(X[:, gs] * dsdE[:, gs]))
        Xb = _bf(X)
        for h in range(SSD_HEADS):
            hs = slice(SSD_P * h, SSD_P * (h + 1))
            ybuf[:, hs] += _dot(mbuf[h], Xb[:, hs])
        y = ybuf[...]
        ypre_ref[...] = y
        zv = z_ref[...]
        yf = y * (zv * _sigmoid(zv))
        for g in range(SSD_GROUPS):
            gs = slice(half * g, half * (g + 1))
            yg = yf[:, gs]
            ms = jnp.mean(yg * yg, axis=-1, keepdims=True)
            y_ref[:, gs] = _bf(yg * lax.rsqrt(ms + RMS_EPS) * nw_ref[:, gs])

    full = lambda shape: pl.BlockSpec(shape, lambda c: (0, 0))
    return pl.pallas_call(
        body, name="ssd_fwd", grid=(nc,),
        in_specs=[
            pl.BlockSpec((CHUNK, D_SSD), lambda c: (c, 0)),
            pl.BlockSpec((CHUNK, D_XBC), lambda c: (c, 0)),
            pl.BlockSpec((HALO, D_XBC), lambda c: (jnp.maximum(c * (CHUNK // HALO) - 1, 0), 0)),
            pl.BlockSpec((CHUNK, DT_PAD), lambda c: (c, 0)),
            full((CONV_K, D_XBC)), full((1, D_XBC)), full((1, SSD_HEADS)), full((1, SSD_HEADS)), full((1, SSD_HEADS)),
            full((1, D_SSD)),
        ],
        out_specs=[
            pl.BlockSpec((CHUNK, D_SSD), lambda c: (c, 0)),
            pl.BlockSpec((CHUNK, D_SSD), lambda c: (c, 0)),
            pl.BlockSpec((1, SSD_N, D_SSD), lambda c: (c, 0, 0)),
        ],
        out_shape=[
            jax.ShapeDtypeStruct((L, D_SSD), BF16),
            jax.ShapeDtypeStruct((L, D_SSD), F32),
            jax.ShapeDtypeStruct((nc, SSD_N, D_SSD), F32),
        ],
        scratch_shapes=[
            pltpu.VMEM((SSD_N, D_SSD), F32),
            pltpu.VMEM((CHUNK, D_SSD), F32),
            pltpu.VMEM((SSD_HEADS, CHUNK, CHUNK), BF16),
            pltpu.VMEM((CHUNK + HALO, D_XBC), F32),
        ],
        compiler_params=_cparams(("arbitrary",)),
    )(z, xbc, xbc, dtp, conv_w, conv_b, dt_bias, a_log, d_skip, norm_w)


def _ssd_bwd(dy, z, ypre, xbc, dtp, prev, conv_w, conv_b, dt_bias, a_log, d_skip, norm_w):
    L = z.shape[0]
    nc = L // CHUNK
    half = D_SSD // SSD_GROUPS

    def body(dy_ref, z_ref, ypre_ref, xbc_ref, tail_ref, dt_ref, prev_ref, cw_ref, cb_ref, dtb_ref, alog_ref, dsk_ref,
             nw_ref, smat_ref, smat2_ref, dz_ref, dxbc_ref, ddt_ref, gcw_ref, gcb_ref, gdtb_ref, galog_ref, gdsk_ref,
             gnw_ref, dstate, dhead, dpost, yobuf, bdbuf, lmbuf, dmbuf, cbbuf):
        i = pl.program_id(0)
        c = nc - 1 - i

        @pl.when(i == 0)
        def _():
            dstate[...] = jnp.zeros_like(dstate)
            dhead[...] = jnp.zeros_like(dhead)
            gcw_ref[...] = jnp.zeros_like(gcw_ref)
            gcb_ref[...] = jnp.zeros_like(gcb_ref)
            gdtb_ref[...] = jnp.zeros_like(gdtb_ref)
            galog_ref[...] = jnp.zeros_like(galog_ref)
            gdsk_ref[...] = jnp.zeros_like(gdsk_ref)
            gnw_ref[...] = jnp.zeros_like(gnw_ref)

        u, sig, xbcv, dtraw, dt, A, acs, acs_row, taps = _ssd_chunk_pre(
            c == 0, xbc_ref, tail_ref, dt_ref, cw_ref, cb_ref, dtb_ref, alog_ref, smat_ref)
        e = _head_expander()
        dtE, eacsE, dsdE, ealE, dskE = _ssd_decays(dt, acs, dsk_ref, e)
        alast = acs[CHUNK - 1:CHUNK, :]
        xs = xbcv[:, 0:D_SSD]
        X = xs * dtE
        Xb = _bf(X)

        zv = z_ref[...]
        ypre = ypre_ref[...]
        dyn = dy_ref[...]
        sz = _sigmoid(zv)
        silu_z = zv * sz
        yf = ypre * silu_z
        dyf_parts = []
        for g in range(SSD_GROUPS):
            gs = slice(half * g, half * (g + 1))
            yg = yf[:, gs]
            rstd = lax.rsqrt(jnp.mean(yg * yg, axis=-1, keepdims=True) + RMS_EPS)
            dout = dyn[:, gs]
            gnw_ref[:, gs] += jnp.sum(dout * yg * rstd, axis=0, keepdims=True)
            dyhat = dout * nw_ref[:, gs]
            dyf_parts.append(rstd * (dyhat - yg * (rstd * rstd) * jnp.mean(dyhat * yg, axis=-1, keepdims=True)))
        dyf = jnp.concatenate(dyf_parts, axis=1)
        dz_ref[...] = _bf(dyf * ypre * (sz * (1.0 + zv * (1.0 - sz))))
        dyp = dyf * silu_z
        dyb = _bf(dyp)
        G = dyp * eacsE

        causal = _iota2((CHUNK, CHUNK), 0) >= _iota2((CHUNK, CHUNK), 1)
        ST = prev_ref[0]
        dST = dstate[...]
        for g in range(SSD_GROUPS):
            gs = slice(half * g, half * (g + 1))
            bs = slice(D_SSD + SSD_N * g, D_SSD + SSD_N * (g + 1))
            cs = slice(D_SSD + D_BC + SSD_N * g, D_SSD + D_BC + SSD_N * (g + 1))
            Bg = _bf(xbcv[:, bs])
            Cg = _bf(xbcv[:, cs])
            Gb = _bf(G[:, gs])
            STb = _bf(ST[:, gs])
            dSTb = _bf(dST[:, gs])
            dstate[:, gs] = dST[:, gs] * ealE[:, gs] + _dot_tn(Cg, Gb)
            yobuf[:, gs] = _dot(Cg, STb) * eacsE[:, gs]
            bdbuf[:, gs] = _dot(Bg, dSTb)
            dpost[:, cs] = _dot_nt(Gb, STb)
            dpost[:, bs] = _dot_nt(_bf(X[:, gs] * dsdE[:, gs]), dSTb)
            cbbuf[g] = _dot_nt(Cg, Bg)
            for r in range(SSD_R):
                h = g * SSD_R + r
                seg = acs[:, h:h + 1] - acs_row[h:h + 1, :]
                lmbuf[h] = jnp.where(causal, jnp.exp(jnp.where(causal, seg, 0.0)), 0.0)
        for h in range(SSD_HEADS):
            hs = slice(SSD_P * h, SSD_P * (h + 1))
            Mb = _bf(cbbuf[h // SSD_R] * lmbuf[h])
            dmbuf[h] = _dot_nt(dyb[:, hs], Xb[:, hs])
            dpost[:, hs] = _dot_tn(Mb, dyb[:, hs])
        lane16 = _iota2((1, SSD_HEADS), 1)
        sub16 = _iota2((SSD_HEADS, 1), 0)
        dacs_col = jnp.zeros((CHUNK, SSD_HEADS), F32)
        dacs_row = jnp.zeros((SSD_HEADS, CHUNK), F32)
        for g in range(SSD_GROUPS):
            bs = slice(D_SSD + SSD_N * g, D_SSD + SSD_N * (g + 1))
            cs = slice(D_SSD + D_BC + SSD_N * g, D_SSD + D_BC + SSD_N * (g + 1))
            cb = cbbuf[g]
            dcb = jnp.zeros((CHUNK, CHUNK), F32)
            for r in range(SSD_R):
                h = g * SSD_R + r
                dM = dmbuf[h]
                Lm = lmbuf[h]
                dcb = dcb + dM * Lm
                dseg = dM * (cb * Lm)
                dacs_col = dacs_col + jnp.sum(dseg, axis=-1, keepdims=True) * (lane16 == h).astype(F32)
                dacs_row = dacs_row - jnp.sum(dseg, axis=0, keepdims=True) * (sub16 == h).astype(F32)
            dcbb = _bf(dcb)
            dpost[:, bs] += _dot_tn(dcbb, _bf(xbcv[:, cs]))
            dpost[:, cs] += _dot(dcbb, _bf(xbcv[:, bs]))

        BD = bdbuf[...]
        dX = dpost[:, 0:D_SSD] + dsdE * BD
        dsd = jnp.exp(alast - acs)
        T = _headsum(X * BD, e) * dsd
        dalast = jnp.sum(T, axis=0, keepdims=True) + _headsum(
            jnp.sum(dST * ST, axis=0, keepdims=True), e) * jnp.exp(alast)
        is_last = (_iota2((CHUNK, 1), 0) == CHUNK - 1).astype(F32)
        dacs = dacs_col + _to_cols(dacs_row) + _headsum(dyp * yobuf[...], e) - T + is_last * dalast
        triu = (_iota2((CHUNK, CHUNK), 0) <= _iota2((CHUNK, CHUNK), 1)).astype(BF16)
        da = _dot01(triu, dacs)
        ddt_tot = _headsum(dX * xs, e) + da * A
        galog_ref[...] += jnp.sum(da * dt, axis=0, keepdims=True) * A
        ddtraw = ddt_tot * _sigmoid(dtraw)
        gdtb_ref[...] += jnp.sum(ddtraw, axis=0, keepdims=True)
        gdsk_ref[...] += _headsum(jnp.sum(dyp * xs, axis=0, keepdims=True), e)
        ddt_ref[...] = jnp.zeros_like(ddt_ref)
        ddt_ref[:, 0:SSD_HEADS] = ddtraw
        dpost[:, 0:D_SSD] = dX * dtE + dskE * dyp

        dconv = dpost[...] * (sig * (1.0 + u * (1.0 - sig)))
        gcb_ref[...] += jnp.sum(dconv, axis=0, keepdims=True)
        for k in range(CONV_K):
            gcw_ref[k:k + 1, :] += jnp.sum(dconv * taps[k], axis=0, keepdims=True)
        later = _shifted_rows(dconv, dhead[...], smat2_ref)
        dx = cw_ref[CONV_K - 1:CONV_K, :] * dconv
        for k in range(CONV_K - 1):
            dx = dx + cw_ref[k:k + 1, :] * later[k]
        dxbc_ref[...] = _bf(dx)
        dhead[...] = dconv[0:HALO, :]

    full = lambda shape: pl.BlockSpec(shape, lambda i: (0, 0))
    rev = lambda wd: pl.BlockSpec((CHUNK, wd), lambda i: (nc - 1 - i, 0))
    return pl.pallas_call(
        body, name="ssd_bwd", grid=(nc,),
        in_specs=[
            rev(D_SSD), rev(D_SSD), rev(D_SSD), rev(D_XBC),
            pl.BlockSpec((HALO, D_XBC), lambda i: (jnp.maximum((nc - 1 - i) * (CHUNK // HALO) - 1, 0), 0)),
            rev(DT_PAD),
            pl.BlockSpec((1, SSD_N, D_SSD), lambda i: (nc - 1 - i, 0, 0)),
            full((CONV_K, D_XBC)), full((1, D_XBC)), full((1, SSD_HEADS)), full((1, SSD_HEADS)), full((1, SSD_HEADS)),
            full((1, D_SSD)), full((3 * CHUNK, 2 * (CHUNK + HALO))), full((3 * CHUNK, 2 * (CHUNK + HALO))),
        ],
        out_specs=[
            rev(D_SSD), rev(D_XBC), rev(DT_PAD),
            full((CONV_K, D_XBC)), full((1, D_XBC)), full((1, SSD_HEADS)), full((1, SSD_HEADS)), full((1, SSD_HEADS)),
            full((1, D_SSD)),
        ],
        out_shape=[
            jax.ShapeDtypeStruct((L, D_SSD), BF16), jax.ShapeDtypeStruct((L, D_XBC), BF16),
            jax.ShapeDtypeStruct((L, DT_PAD), F32),
            jax.ShapeDtypeStruct((CONV_K, D_XBC), F32), jax.ShapeDtypeStruct((1, D_XBC), F32),
            jax.ShapeDtypeStruct((1, SSD_HEADS), F32), jax.ShapeDtypeStruct((1, SSD_HEADS), F32),
            jax.ShapeDtypeStruct((1, SSD_HEADS), F32), jax.ShapeDtypeStruct((1, D_SSD), F32),
        ],
        scratch_shapes=[
            pltpu.VMEM((SSD_N, D_SSD), F32),
            pltpu.VMEM((HALO, D_XBC), F32),
            pltpu.VMEM((CHUNK, D_XBC), F32),
            pltpu.VMEM((CHUNK, D_SSD), F32),
            pltpu.VMEM((CHUNK, D_SSD), F32),
            pltpu.VMEM((SSD_HEADS, CHUNK, CHUNK), F32),
            pltpu.VMEM((SSD_HEADS, CHUNK, CHUNK), F32),
            pltpu.VMEM((SSD_GROUPS, CHUNK, CHUNK), F32),
        ],
        compiler_params=_cparams(("arbitrary",)),
    )(dy, z, ypre, xbc, xbc, dtp, prev, conv_w, conv_b, dt_bias, a_log, d_skip, norm_w, _shift_matrix((13, 14, 15)),
      _shift_matrix((3, 2, 1)))


def _rope_tables(pos_ref, inv_ref):
    ang = pos_ref[...].astype(F32) * inv_ref[...]
    d = _iota2((1, 2 * ATT_HD), 1) % ATT_HD
    s = jnp.sin(ang)
    return jnp.cos(ang), jnp.where(d < ROPE_DIM // 2, -s, 0.0), jnp.where((d >= ROPE_DIM // 2) & (d < ROPE_DIM), s, 0.0)


def _rope(t, tabs):
    c, s1, s2 = tabs
    n = t.shape[1]
    rep = n // c.shape[1]
    return (t * jnp.tile(c, (1, rep)) + pltpu.roll(t, n - ROPE_DIM // 2, 1) * jnp.tile(s1, (1, rep))
            + pltpu.roll(t, ROPE_DIM // 2, 1) * jnp.tile(s2, (1, rep)))


def _rope_t(t, tabs):
    c, s1, s2 = tabs
    n = t.shape[1]
    rep = n // c.shape[1]
    return (t * jnp.tile(c, (1, rep)) + pltpu.roll(t * jnp.tile(s1, (1, rep)), ROPE_DIM // 2, 1)
            + pltpu.roll(t * jnp.tile(s2, (1, rep)), n - ROPE_DIM // 2, 1))


def _stack_heads(t, j):
    return jnp.concatenate([t[:, ATT_HD * (j * ATT_R + r):ATT_HD * (j * ATT_R + r + 1)] for r in range(ATT_R)], axis=0)


def _swa_mask_t(first):
    si = _iota2((2 * WINDOW, ATT_R * WINDOW), 0)
    qi = _iota2((2 * WINDOW, ATT_R * WINDOW), 1) % WINDOW
    band = (si > qi) & (si <= qi + WINDOW)
    return band & (jnp.logical_not(first) | (si >= WINDOW))


def _head_rows(ref, j):
    if ref.shape[0] == 1:
        parts = [jnp.broadcast_to(ref[:, j * ATT_R + r:j * ATT_R + r + 1], (1, WINDOW)) for r in range(ATT_R)]
    else:
        parts = [ref[j * ATT_R + r:j * ATT_R + r + 1, :] for r in range(ATT_R)]
    return jnp.concatenate(parts, axis=1)


def _swa_fwd(q, g, kv, sinks):
    L = q.shape[0]
    nb = L // WINDOW
    scale = ATT_HD ** -0.5

    def body(q_ref, g_ref, kvc_ref, kvp_ref, sink_ref, y_ref, o_ref, lse_ref, otbuf):
        n = pl.program_id(0)
        kk = jnp.concatenate([kvp_ref[:, 0:D_KV], kvc_ref[:, 0:D_KV]], axis=0) * scale
        vv = jnp.concatenate([kvp_ref[:, D_KV:2 * D_KV], kvc_ref[:, D_KV:2 * D_KV]], axis=0)
        valid = _swa_mask_t(n == 0)
        qv = q_ref[...]
        for j in range(ATT_KVH):
            js = slice(ATT_HD * j, ATT_HD * (j + 1))
            st = _dot_nt(kk[:, js], _stack_heads(qv, j))
            st = jnp.where(valid, st, NEG_BIG)
            sink = _head_rows(sink_ref, j)
            m = jnp.maximum(jnp.max(st, axis=0, keepdims=True), sink)
            p = jnp.exp(st - m)
            vx = jnp.concatenate([vv[:, js], jnp.ones((2 * WINDOW, ATT_HD), BF16)], axis=1)
            otx = _dot_tn(vx, _bf(p))
            denom = otx[ATT_HD:ATT_HD + 1] + jnp.exp(sink - m)
            ot = otx[0:ATT_HD] * (1.0 / denom)
            lse = m + jnp.log(denom)
            for r in range(ATT_R):
                h = j * ATT_R + r
                otbuf[ATT_HD * h:ATT_HD * (h + 1), :] = ot[:, WINDOW * r:WINDOW * (r + 1)]
                lse_ref[h:h + 1, :] = lse[:, WINDOW * r:WINDOW * (r + 1)]
        o = otbuf[...].T
        o_ref[...] = o
        gv = g_ref[...]
        y_ref[...] = _bf(o * (gv * _sigmoid(gv)))

    cur = lambda wd: pl.BlockSpec((WINDOW, wd), lambda n: (n, 0))
    prv = lambda wd: pl.BlockSpec((WINDOW, wd), lambda n: (jnp.maximum(n - 1, 0), 0))
    return pl.pallas_call(
        body, name="swa_fwd", grid=(nb,),
        in_specs=[cur(D_ATT), cur(D_ATT), cur(2 * D_KV), prv(2 * D_KV), pl.BlockSpec((1, ATT_QH), lambda n: (0, 0))],
        out_specs=[cur(D_ATT), cur(D_ATT), pl.BlockSpec((ATT_QH, WINDOW), lambda n: (0, n))],
        out_shape=[jax.ShapeDtypeStruct((L, D_ATT), BF16), jax.ShapeDtypeStruct((L, D_ATT), F32),
                   jax.ShapeDtypeStruct((ATT_QH, L), F32)],
        scratch_shapes=[pltpu.VMEM((D_ATT, WINDOW), F32)],
        compiler_params=_cparams(("parallel",)),
    )(q, g, kv, kv, sinks)


def _swa_bwd(dy, q, g, kv, o, lse, pos, inv, sinks):
    L = q.shape[0]
    nb = L // WINDOW
    scale = ATT_HD ** -0.5

    def body(dy_ref, q_ref, g_ref, kvc_ref, kvp_ref, o_ref, lse_ref, posc_ref, posp_ref, inv_ref, sink_ref,
             dq_ref, dg_ref, dkv_ref, dsink_ref, carry, dqbuf, dkbuf, dvbuf):
        n = pl.program_id(0)

        @pl.when(n == 0)
        def _():
            dsink_ref[...] = jnp.zeros_like(dsink_ref)

        @pl.when(n < nb)
        def _():
            tc = _rope_tables(posc_ref, inv_ref)
            tp = _rope_tables(posp_ref, inv_ref)
            kk = jnp.concatenate([kvp_ref[:, 0:D_KV], kvc_ref[:, 0:D_KV]], axis=0) * scale
            vv = jnp.concatenate([kvp_ref[:, D_KV:2 * D_KV], kvc_ref[:, D_KV:2 * D_KV]], axis=0)
            valid = _swa_mask_t(n == 0)
            qv = q_ref[...]
            gv = g_ref[...]
            sg = _sigmoid(gv)
            dyv = dy_ref[...]
            ov = o_ref[...]
            dg_ref[...] = _bf(dyv * ov * (sg * (1.0 + gv * (1.0 - sg))))
            do = dyv * (gv * sg)
            dod = do * ov
            ones = jnp.ones((8, ATT_HD), BF16)
            lane16 = _iota2((1, ATT_QH), 1)
            dsink = jnp.zeros((1, ATT_QH), F32)
            for j in range(ATT_KVH):
                js = slice(ATT_HD * j, ATT_HD * (j + 1))
                kj = kk[:, js]
                vj = vv[:, js]
                qs = _stack_heads(qv, j)
                dos = _bf(_stack_heads(do, j))
                hi, lo = _hi_lo(_stack_heads(dod, j))
                delta = (_dot_nt(ones, hi) + _dot_nt(ones, lo))[0:1]
                lse = _head_rows(lse_ref, j)
                st = _dot_nt(kj, qs)
                pt = jnp.exp(jnp.where(valid, st, NEG_BIG) - lse)
                dst = _bf(pt * (_dot_nt(vj, dos) - delta))
                dqt = _dot_tn(kj, dst)
                dkbuf[:, js] = _dot(dst, qs) * scale
                dvbuf[:, js] = _dot(_bf(pt), dos)
                sd = jnp.exp(_head_rows(sink_ref, j) - lse) * delta
                for r in range(ATT_R):
                    h = j * ATT_R + r
                    ls = slice(WINDOW * r, WINDOW * (r + 1))
                    dqbuf[ATT_HD * h:ATT_HD * (h + 1), :] = dqt[:, ls]
                    dsink = dsink - jnp.sum(sd[:, ls], axis=1, keepdims=True) * (lane16 == h).astype(F32)
            dsink_ref[...] += dsink
            dq_ref[...] = _bf(_rope_t(dqbuf[...].T, tc))
            dkp = _rope_t(dkbuf[0:WINDOW, :], tp)
            dkc = _rope_t(dkbuf[WINDOW:2 * WINDOW, :], tc)

            @pl.when(n > 0)
            def _():
                dkv_ref[:, 0:D_KV] = _bf(carry[:, 0:D_KV] + dkp)
                dkv_ref[:, D_KV:2 * D_KV] = _bf(carry[:, D_KV:2 * D_KV] + dvbuf[0:WINDOW, :])

            carry[:, 0:D_KV] = dkc
            carry[:, D_KV:2 * D_KV] = dvbuf[WINDOW:2 * WINDOW, :]

        @pl.when(n == nb)
        def _():
            dkv_ref[...] = _bf(carry[...])

    last = nb - 1
    cur = lambda wd: pl.BlockSpec((WINDOW, wd), lambda n: (jnp.minimum(n, last), 0))
    prv = lambda wd: pl.BlockSpec((WINDOW, wd), lambda n: (jnp.maximum(jnp.minimum(n, last) - 1, 0), 0))
    return pl.pallas_call(
        body, name="swa_bwd", grid=(nb + 1,),
        in_specs=[cur(D_ATT), cur(D_ATT), cur(D_ATT), cur(2 * D_KV), prv(2 * D_KV), cur(D_ATT),
                  pl.BlockSpec((ATT_QH, WINDOW), lambda n: (0, jnp.minimum(n, last))), cur(1), prv(1),
                  pl.BlockSpec((1, 2 * ATT_HD), lambda n: (0, 0)), pl.BlockSpec((1, ATT_QH), lambda n: (0, 0))],
        out_specs=[cur(D_ATT), cur(D_ATT),
                   pl.BlockSpec((WINDOW, 2 * D_KV), lambda n: (jnp.maximum(n - 1, 0), 0)),
                   pl.BlockSpec((1, ATT_QH), lambda n: (0, 0))],
        out_shape=[jax.ShapeDtypeStruct((L, D_ATT), BF16), jax.ShapeDtypeStruct((L, D_ATT), BF16),
                   jax.ShapeDtypeStruct((L, 2 * D_KV), BF16), jax.ShapeDtypeStruct((1, ATT_QH), F32)],
        scratch_shapes=[pltpu.VMEM((WINDOW, 2 * D_KV), F32), pltpu.VMEM((D_ATT, WINDOW), F32),
                        pltpu.VMEM((2 * WINDOW, D_KV), F32), pltpu.VMEM((2 * WINDOW, D_KV), F32)],
        compiler_params=_cparams(("arbitrary",)),
    )(dy, q, g, kv, kv, o, lse, pos, pos, inv, sinks)


def _out_ln_loss(y_ssd, y_att, x, target, w_out, ln_g, ln_b):
    L = x.shape[0]
    tm = min(ROW_TILE, L)
    nt = L // tm
    inv_d = 1.0 / D_MODEL

    def body(ys_ref, ya_ref, x_ref, t_ref, w_ref, g_ref, b_ref, dr_ref, dys_ref, dya_ref, loss_ref, gg_ref, gb_ref,
             gwo_ref, acc_ref):
        i = pl.program_id(0)

        @pl.when(i == 0)
        def _():
            loss_ref[...] = jnp.zeros_like(loss_ref)
            gg_ref[...] = jnp.zeros_like(gg_ref)
            gb_ref[...] = jnp.zeros_like(gb_ref)
            acc_ref[...] = jnp.zeros_like(acc_ref)

        halves = [slice(0, tm // 2), slice(tm // 2, tm)]
        hs = [_dot(_bf(ys_ref[rs, :]), w_ref[0:D_SSD, :]) + _dot(_bf(ya_ref[rs, :]), w_ref[D_SSD:D_MIX, :]) for rs in halves]
        gam = g_ref[...]
        for rs, h in zip(halves, hs):
            r = ALPHA * x_ref[rs, :] + h
            mu = jnp.mean(r, axis=-1, keepdims=True)
            xc = r - mu
            rstd = lax.rsqrt(jnp.mean(xc * xc, axis=-1, keepdims=True) + LN_EPS)
            xhat = xc * rstd
            diff = xhat * gam + b_ref[...] - t_ref[rs, :]
            part = jnp.sum(jnp.sum(diff * diff, axis=-1, keepdims=True), axis=0, keepdims=True)
            loss_ref[...] += (0.5 * inv_d) * part
            dout = diff * inv_d
            gg_ref[...] += jnp.sum(dout * xhat, axis=0, keepdims=True)
            gb_ref[...] += jnp.sum(dout, axis=0, keepdims=True)
            dxh = dout * gam
            dr_ref[rs, :] = rstd * (dxh - jnp.mean(dxh, axis=-1, keepdims=True)
                                    - xhat * jnp.mean(dxh * xhat, axis=-1, keepdims=True))
        for rs in halves:
            drh = _bf(dr_ref[rs, :])
            dys_ref[rs, :] = _dot_nt(drh, w_ref[0:D_SSD, :])
            dya_ref[rs, :] = _dot_nt(drh, w_ref[D_SSD:D_MIX, :])
        drb = _bf(dr_ref[...])
        acc_ref[0:D_SSD, :] += _dot_tn(_bf(ys_ref[...]), drb)
        acc_ref[D_SSD:D_MIX, :] += _dot_tn(_bf(ya_ref[...]), drb)

        @pl.when(i == nt - 1)
        def _():
            gwo_ref[...] = _bf(acc_ref[...])

    row = pl.BlockSpec((tm, D_MODEL), lambda i: (i, 0))
    vec = pl.BlockSpec((1, D_MODEL), lambda i: (0, 0))
    return pl.pallas_call(
        body, name="out_ln_loss", grid=(nt,),
        in_specs=[row, row, row, row, pl.BlockSpec((D_MIX, D_MODEL), lambda i: (0, 0), pipeline_mode=pl.Buffered(1)), vec, vec],
        out_specs=[row, row, row, pl.BlockSpec((1, 128), lambda i: (0, 0)), vec, vec,
                   pl.BlockSpec((D_MIX, D_MODEL), lambda i: (0, 0))],
        out_shape=[jax.ShapeDtypeStruct((L, D_MODEL), F32)] * 3 + [jax.ShapeDtypeStruct((1, 128), F32)]
        + [jax.ShapeDtypeStruct((1, D_MODEL), F32)] * 2 + [jax.ShapeDtypeStruct((D_MIX, D_MODEL), BF16)],
        scratch_shapes=[pltpu.VMEM((D_MIX, D_MODEL), F32)],
        compiler_params=_cparams(("arbitrary",)),
    )(y_ssd, y_att, x, target, w_out, ln_g, ln_b)


def _local_step(x, pos, target, w, get_w_out, token, conv_w, conv_b, dt_bias, a_log, d_skip, norm_w, sinks, ln_g, ln_b):
    inv8 = ROPE_THETA ** (-jnp.arange(0, ROPE_DIM, 2, dtype=F32) / ROPE_DIM)
    inv = jnp.tile(jnp.concatenate([inv8, inv8, jnp.zeros((ATT_HD - ROPE_DIM,), F32)]), 2).reshape(1, 2 * ATT_HD)
    inv = inv + token

    z, g, q, xbc, kv, dtp, xb = _in_proj(x, w, pos, inv)
    y_ssd, y_pre, prev = _ssd_fwd(z, xbc, dtp, conv_w, conv_b, dt_bias, a_log, d_skip, norm_w)
    y_att, o, lse = _swa_fwd(q, g, kv, sinks)
    w_out = get_w_out(lse)
    dr, dy_ssd, dy_att, loss, g_ln_g, g_ln_b, gw_out = _out_ln_loss(y_ssd, y_att, x, target, w_out, ln_g, ln_b)
    w_out_red = _reduce_w_out_start(gw_out.reshape(N_CHIPS, W_OUT_ROWS, D_MODEL), loss)
    inv = inv + w_out_red[16][0:1, :]
    dq, dg, dkv, g_sinks = _swa_bwd(dy_att, q, g, kv, o, lse, pos, inv, sinks)
    dz, dxbc, ddt, g_conv_w, g_conv_b, g_dt_bias, g_a_log, g_d_skip, g_norm_w = _ssd_bwd(
        dy_ssd, z, y_pre, xbc, dtp, prev, conv_w, conv_b, dt_bias, a_log, d_skip, norm_w)
    gw_z, gw_g, gw_q = _matmuls_tn([dz, dg, dq], xb, "gw_zgq")
    gw_xbc, gw_kv, gw_dt = _matmuls_tn([dxbc, dkv, ddt], xb, "gw_xbc_kv_dt")
    gw_in = jnp.concatenate([gw_z, gw_xbc, gw_dt[0:SSD_HEADS], gw_q, gw_kv, gw_g], axis=0)
    small = dict(conv_w=g_conv_w, conv_b=g_conv_b, dt_bias=g_dt_bias, a_log=g_a_log, d_skip=g_d_skip,
                 ssd_norm_w=g_norm_w, attn_sinks=g_sinks, ln_g=g_ln_g, ln_b=g_ln_b)
    return loss, (dr, dz, dg, dq, dxbc, dkv, ddt, w), gw_in, w_out_red, small


def _mesh_pos():
    return lax.axis_index("x"), lax.axis_index("y"), lax.axis_index("c")


def _gather_weights(w_in_s, conv_w_s):
    hr = w_in_s.shape[0] // 2
    qa = 336
    quarters = ((0, qa), (qa, hr - qa))

    def body(win_ref, cw_ref, owin_ref, ocw_ref, stage, send_sems, recv_sems, small_send, small_recv, local_sems):
        x, y, c = _mesh_pos()
        me = 2 * x + y
        sibling = (x, y, 1 - c)
        xn, yn, dg = (1 - x, y), (x, 1 - y), (1 - x, 1 - y)
        chips = [xn, yn, dg]
        load = pltpu.make_async_copy(win_ref, stage, local_sems.at[1])
        load.start()
        locals_ = [pltpu.make_async_copy(cw_ref, ocw_ref.at[me], local_sems.at[0])]
        for cp in locals_:
            cp.start()
        started = []

        def piece(ref, chip, half, q):
            off, n = quarters[q]
            return ref.at[2 * chip[0] + chip[1]].at[pl.ds(half * hr + off, n), :]

        def mine(q):
            off, n = quarters[q]
            return win_ref.at[pl.ds(c * hr + off, n), :]

        def copy(src, dst, k, to):
            return pltpu.make_async_remote_copy(src_ref=src, dst_ref=dst, send_sem=send_sems.at[k], recv_sem=recv_sems.at[k],
                                                device_id=to, device_id_type=MESH)

        def go(cp):
            cp.start()
            started.append(cp)

        go(copy(mine(0), piece(owin_ref, (x, y), c, 0), 0, (*xn, c)))
        go(copy(mine(1), piece(owin_ref, (x, y), c, 1), 2, (*yn, c)))
        go(copy(mine(1), piece(owin_ref, (x, y), c, 1), 1, (*xn, c)))
        go(copy(mine(0), piece(owin_ref, (x, y), c, 0), 3, (*yn, c)))
        for j, (px, py) in enumerate(chips):
            cp = pltpu.make_async_remote_copy(
                src_ref=cw_ref, dst_ref=ocw_ref.at[me], send_sem=small_send.at[j], recv_sem=small_recv.at[j],
                device_id=(px, py, c), device_id_type=MESH)
            go(cp)
        load.wait()
        store = pltpu.make_async_copy(stage, owin_ref.at[me], local_sems.at[2])
        store.start()
        locals_.append(store)
        arrivals = [(0, xn, 0, (4, (*yn, c))), (2, yn, 1, (5, (*xn, c))), (1, xn, 1, None), (3, yn, 0, None),
                    (4, dg, 0, None), (5, dg, 1, None)]
        for n, (k, chip, q, onward) in enumerate(arrivals):
            blk = piece(owin_ref, chip, c, q)
            copy(blk, blk, k, sibling).wait_recv()
            if onward is not None:
                go(copy(blk, blk, onward[0], onward[1]))
            go(copy(blk, blk, 6 + n, sibling))
        for n, (k, chip, q, onward) in enumerate(arrivals):
            blk = piece(owin_ref, chip, 1 - c, q)
            copy(blk, blk, 6 + n, sibling).wait_recv()
        for j in range(3):
            pltpu.make_async_remote_copy(
                src_ref=cw_ref, dst_ref=ocw_ref.at[me], send_sem=small_send.at[j], recv_sem=small_recv.at[j],
                device_id=sibling, device_id_type=MESH).wait_recv()
        for cp in started:
            cp.wait_send()
        for cp in locals_:
            cp.wait()

    any_spec = pl.BlockSpec(memory_space=pl.ANY)
    return pl.pallas_call(
        body, name="gather_weights",
        in_specs=[any_spec] * 2, out_specs=[any_spec] * 2,
        out_shape=[jax.ShapeDtypeStruct((N_CHIPS,) + a.shape, a.dtype) for a in (w_in_s, conv_w_s)],
        scratch_shapes=[pltpu.VMEM(w_in_s.shape, w_in_s.dtype),
                        pltpu.SemaphoreType.DMA((12,)), pltpu.SemaphoreType.DMA((12,)),
                        pltpu.SemaphoreType.DMA((3,)), pltpu.SemaphoreType.DMA((3,)), pltpu.SemaphoreType.DMA((3,))],
    )(w_in_s, conv_w_s)


_HBM = pl.BlockSpec(memory_space=pltpu.HBM)
_SEM = pl.BlockSpec(memory_space=pltpu.SEMAPHORE)
_EFFECT = pltpu.SideEffectType.DATAFLOW_SIDE_EFFECTING


def _gather_w_out_start(w_out_s, after):
    def body(src_ref, land_ref, after_ref, s0, s1, s2, r0, r1, r2, src_thru, land_thru, token):
        x, y, c = _mesh_pos()
        me = 2 * x + y
        chips = [(1 - x, y), (x, 1 - y), (1 - x, 1 - y)]
        for (px, py), s, r in zip(chips, (s0, s1, s2), (r0, r1, r2)):
            pltpu.make_async_remote_copy(src_ref=src_ref, dst_ref=land_ref.at[me], send_sem=s, recv_sem=r,
                                         device_id=(px, py, c), device_id_type=MESH).start()
        token[...] = jnp.zeros_like(token)

    sem = pltpu.SemaphoreType.DMA(())
    land = lax.empty((N_CHIPS,) + w_out_s.shape, w_out_s.dtype)
    return pl.pallas_call(
        body, name="gather_w_out_start",
        out_shape=(sem,) * 6 + (pltpu.HBM(w_out_s.shape, w_out_s.dtype), pltpu.HBM(land.shape, land.dtype),
                                jax.ShapeDtypeStruct((8, 128), F32)),
        in_specs=(_HBM, _HBM, pl.BlockSpec(memory_space=pl.ANY)),
        out_specs=(_SEM,) * 6 + (_HBM, _HBM, pl.BlockSpec(memory_space=pltpu.VMEM)),
        input_output_aliases={0: 6, 1: 7},
        compiler_params=pltpu.CompilerParams(has_side_effects=_EFFECT),
    )(pltpu.with_memory_space_constraint(w_out_s, pltpu.HBM), pltpu.with_memory_space_constraint(land, pltpu.HBM), after)


def _gather_w_out_wait(sems, src_thru, land_thru, after):
    def body(src_ref, land_ref, s0, s1, s2, r0, r1, r2, after_ref, src_dead, got_ref):
        x, y, c = _mesh_pos()
        chips = [(1 - x, y), (x, 1 - y), (1 - x, 1 - y)]
        for (px, py), s, r in zip(chips, (s0, s1, s2), (r0, r1, r2)):
            cp = pltpu.make_async_remote_copy(src_ref=src_ref, dst_ref=land_ref.at[2 * px + py], send_sem=s, recv_sem=r,
                                              device_id=(px, py, c), device_id_type=MESH)
            cp.wait_send()
            cp.wait_recv()

    return pl.pallas_call(
        body, name="gather_w_out_wait",
        out_shape=(pltpu.HBM(src_thru.shape, src_thru.dtype), pltpu.HBM(land_thru.shape, land_thru.dtype)),
        in_specs=(_HBM, _HBM) + (_SEM,) * 6 + (pl.BlockSpec(memory_space=pl.ANY),),
        out_specs=(_HBM, _HBM), input_output_aliases={0: 0, 1: 1},
        compiler_params=pltpu.CompilerParams(has_side_effects=_EFFECT),
    )(src_thru, land_thru, *sems, after)[1]


def _pair_start(gw_in, after):
    hr = gw_in.shape[1] // 2

    def body(src_ref, land_ref, after_ref, *refs):
        x, y, c = _mesh_pos()
        for j in range(N_CHIPS):
            pltpu.make_async_remote_copy(
                src_ref=src_ref.at[j, pl.ds((1 - c) * hr, hr), :], dst_ref=land_ref.at[j], send_sem=refs[j],
                recv_sem=refs[N_CHIPS + j], device_id=(x, y, 1 - c), device_id_type=MESH).start()
        refs[10][...] = jnp.zeros_like(refs[10])

    sem = pltpu.SemaphoreType.DMA(())
    land = lax.empty((N_CHIPS, hr, D_MODEL), gw_in.dtype)
    return pl.pallas_call(
        body, name="pair_start",
        out_shape=(sem,) * 8 + (pltpu.HBM(gw_in.shape, gw_in.dtype), pltpu.HBM(land.shape, land.dtype),
                                jax.ShapeDtypeStruct((8, 128), F32)),
        in_specs=(_HBM, _HBM, pl.BlockSpec(memory_space=pl.ANY)),
        out_specs=(_SEM,) * 8 + (_HBM, _HBM, pl.BlockSpec(memory_space=pltpu.VMEM)),
        input_output_aliases={0: 8, 1: 9},
        compiler_params=pltpu.CompilerParams(has_side_effects=_EFFECT),
    )(pltpu.with_memory_space_constraint(gw_in, pltpu.HBM), pltpu.with_memory_space_constraint(land, pltpu.HBM), after)


def _pair_wait(sems, gw_thru, land_thru, after):
    hr = land_thru.shape[1]

    def body(src_ref, land_ref, *refs):
        x, y, c = _mesh_pos()
        for j in range(N_CHIPS):
            cp = pltpu.make_async_remote_copy(
                src_ref=src_ref.at[j, pl.ds((1 - c) * hr, hr), :], dst_ref=land_ref.at[j], send_sem=refs[j],
                recv_sem=refs[N_CHIPS + j], device_id=(x, y, 1 - c), device_id_type=MESH)
            cp.wait_send()
            cp.wait_recv()

    return pl.pallas_call(
        body, name="pair_wait",
        out_shape=(pltpu.HBM(gw_thru.shape, gw_thru.dtype), pltpu.HBM(land_thru.shape, land_thru.dtype)),
        in_specs=(_HBM, _HBM) + (_SEM,) * 8 + (pl.BlockSpec(memory_space=pl.ANY),),
        out_specs=(_HBM, _HBM), input_output_aliases={0: 0, 1: 1},
        compiler_params=pltpu.CompilerParams(has_side_effects=_EFFECT),
    )(gw_thru, land_thru, *sems, after)


def _chip_start(s_in, after):
    def body(src_ref, land_ref, after_ref, *refs):
        x, y, c = _mesh_pos()
        me = 2 * x + y
        for j, (px, py) in enumerate([(1 - x, y), (x, 1 - y), (1 - x, 1 - y)]):
            pltpu.make_async_remote_copy(
                src_ref=src_ref.at[2 * px + py], dst_ref=land_ref.at[me], send_sem=refs[j], recv_sem=refs[3 + j],
                device_id=(px, py, c), device_id_type=MESH).start()
        refs[8][...] = jnp.zeros_like(refs[8])

    sem = pltpu.SemaphoreType.DMA(())
    land = lax.empty(s_in.shape, s_in.dtype)
    return pl.pallas_call(
        body, name="chip_start",
        out_shape=(sem,) * 6 + (pltpu.HBM(s_in.shape, s_in.dtype), pltpu.HBM(land.shape, land.dtype),
                                jax.ShapeDtypeStruct((8, 128), F32)),
        in_specs=(_HBM, _HBM, pl.BlockSpec(memory_space=pl.ANY)),
        out_specs=(_SEM,) * 6 + (_HBM, _HBM, pl.BlockSpec(memory_space=pltpu.VMEM)),
        input_output_aliases={0: 6, 1: 7},
        compiler_params=pltpu.CompilerParams(has_side_effects=_EFFECT),
    )(pltpu.with_memory_space_constraint(s_in, pltpu.HBM), pltpu.with_memory_space_constraint(land, pltpu.HBM), after)


def _chip_wait(sems, s_thru, land_thru, after):
    def body(src_ref, land_ref, *refs):
        x, y, c = _mesh_pos()
        for j, (px, py) in enumerate([(1 - x, y), (x, 1 - y), (1 - x, 1 - y)]):
            cp = pltpu.make_async_remote_copy(
                src_ref=src_ref.at[2 * px + py], dst_ref=land_ref.at[2 * px + py], send_sem=refs[j], recv_sem=refs[3 + j],
                device_id=(px, py, c), device_id_type=MESH)
            cp.wait_send()
            cp.wait_recv()

    return pl.pallas_call(
        body, name="chip_wait",
        out_shape=(pltpu.HBM(s_thru.shape, s_thru.dtype), pltpu.HBM(land_thru.shape, land_thru.dtype)),
        in_specs=(_HBM, _HBM) + (_SEM,) * 6 + (pl.BlockSpec(memory_space=pl.ANY),),
        out_specs=(_HBM, _HBM), input_output_aliases={0: 0, 1: 1},
        compiler_params=pltpu.CompilerParams(has_side_effects=_EFFECT),
    )(s_thru, land_thru, *sems, after)


def _pair_share(h_in, small):
    def body(hin_ref, sm_ref, rin_ref, slots_ref, send_sems, recv_sems, small_send, small_recv, local_sem):
        x, y, c = _mesh_pos()
        dev = 4 * x + 2 * y + c
        mine = pltpu.make_async_copy(sm_ref, slots_ref.at[dev], local_sem)
        mine.start()
        share = pltpu.make_async_remote_copy(
            src_ref=hin_ref, dst_ref=rin_ref, send_sem=send_sems.at[0], recv_sem=recv_sems.at[0],
            device_id=(x, y, 1 - c), device_id_type=MESH)
        share.start()
        started = []
        for k in range(1, 8):
            peer = (x ^ ((k >> 2) & 1), y ^ ((k >> 1) & 1), c ^ (k & 1))
            cp = pltpu.make_async_remote_copy(
                src_ref=sm_ref, dst_ref=slots_ref.at[dev], send_sem=small_send.at[k - 1], recv_sem=small_recv.at[k - 1],
                device_id=peer, device_id_type=MESH)
            cp.start()
            started.append(cp)
        share.wait()
        for k in range(1, 8):
            pltpu.make_async_remote_copy(
                src_ref=sm_ref, dst_ref=slots_ref.at[dev], send_sem=small_send.at[k - 1], recv_sem=small_recv.at[k - 1],
                device_id=(x, y, 1 - c), device_id_type=MESH).wait_recv()
        for cp in started:
            cp.wait_send()
        mine.wait()

    any_spec = pl.BlockSpec(memory_space=pl.ANY)
    return pl.pallas_call(
        body, name="pair_share",
        in_specs=[any_spec] * 2, out_specs=[any_spec] * 2,
        out_shape=[jax.ShapeDtypeStruct(h_in.shape, F32), jax.ShapeDtypeStruct((8,) + small.shape, F32)],
        scratch_shapes=[pltpu.SemaphoreType.DMA((1,)), pltpu.SemaphoreType.DMA((1,)),
                        pltpu.SemaphoreType.DMA((7,)), pltpu.SemaphoreType.DMA((7,)), pltpu.SemaphoreType.DMA],
    )(h_in, small)


def _reduce_w_out_start(slabs, after):
    def body(src_ref, land_ref, after_ref, *refs):
        x, y, c = _mesh_pos()
        me = 4 * x + 2 * y + c
        for k in range(1, 8):
            px, py, pc = x ^ ((k >> 2) & 1), y ^ ((k >> 1) & 1), c ^ (k & 1)
            pltpu.make_async_remote_copy(src_ref=src_ref.at[2 * px + py], dst_ref=land_ref.at[me], send_sem=refs[k - 1],
                                         recv_sem=refs[6 + k], device_id=(px, py, pc), device_id_type=MESH).start()
        refs[16][...] = jnp.zeros_like(refs[16])

    sem = pltpu.SemaphoreType.DMA(())
    land = lax.empty((8,) + slabs.shape[1:], slabs.dtype)
    return pl.pallas_call(
        body, name="reduce_w_out_start",
        out_shape=(sem,) * 14 + (pltpu.HBM(slabs.shape, slabs.dtype), pltpu.HBM(land.shape, land.dtype),
                                 jax.ShapeDtypeStruct((8, 128), F32)),
        in_specs=(_HBM, _HBM, pl.BlockSpec(memory_space=pl.ANY)),
        out_specs=(_SEM,) * 14 + (_HBM, _HBM, pl.BlockSpec(memory_space=pltpu.VMEM)),
        input_output_aliases={0: 14, 1: 15},
        compiler_params=pltpu.CompilerParams(has_side_effects=_EFFECT),
    )(pltpu.with_memory_space_constraint(slabs, pltpu.HBM), pltpu.with_memory_space_constraint(land, pltpu.HBM), after)


def _reduce_w_out_wait(sems, slabs_thru, land_thru, after):
    def body(src_ref, land_ref, *refs):
        x, y, c = _mesh_pos()
        for k in range(1, 8):
            px, py, pc = x ^ ((k >> 2) & 1), y ^ ((k >> 1) & 1), c ^ (k & 1)
            cp = pltpu.make_async_remote_copy(
                src_ref=src_ref.at[2 * px + py], dst_ref=land_ref.at[4 * px + 2 * py + pc], send_sem=refs[k - 1],
                recv_sem=refs[6 + k], device_id=(px, py, pc), device_id_type=MESH)
            cp.wait_send()
            cp.wait_recv()

    return pl.pallas_call(
        body, name="reduce_w_out_wait",
        out_shape=(pltpu.HBM(slabs_thru.shape, slabs_thru.dtype), pltpu.HBM(land_thru.shape, land_thru.dtype)),
        in_specs=(_HBM, _HBM) + (_SEM,) * 14 + (pl.BlockSpec(memory_space=pl.ANY),),
        out_specs=(_HBM, _HBM), input_output_aliases={0: 0, 1: 1},
        compiler_params=pltpu.CompilerParams(has_side_effects=_EFFECT),
    )(slabs_thru, land_thru, *sems, after)


def _pair_add(g, recv, core, name):
    _, rows, C = recv.shape

    def body(core_ref, g_ref, r_ref, o_ref):
        o_ref[...] = _bf(g_ref[...].astype(F32) + r_ref[...].astype(F32))

    spec = pl.BlockSpec((1, rows, C), lambda j, core: (j, 0, 0))
    return pl.pallas_call(
        body, name=name,
        grid_spec=pltpu.PrefetchScalarGridSpec(
            num_scalar_prefetch=1, grid=(N_CHIPS,),
            in_specs=[pl.BlockSpec((1, rows, C), lambda j, core: (j, core[0], 0)), spec], out_specs=spec),
        out_shape=jax.ShapeDtypeStruct((N_CHIPS, rows, C), BF16),
        compiler_params=_cparams(("parallel",)),
    )(core, g, recv)


def _chip_add(own, parts, chip, name):
    _, rows, C = parts.shape
    tc = 512

    def body(chip_ref, own_ref, r0, r1, r2, r3, o_ref):
        acc = None
        for j, r in enumerate((r0, r1, r2, r3)):
            term = jnp.where(chip_ref[0] == j, own_ref[0], r[0]).astype(F32)
            acc = term if acc is None else acc + term
        o_ref[...] = acc

    def slab(j):
        return pl.BlockSpec((1, rows, tc), lambda i, chip: (jnp.where(chip[0] == j, (j + 1) % N_CHIPS, j), 0, i))

    return pl.pallas_call(
        body, name=name,
        grid_spec=pltpu.PrefetchScalarGridSpec(
            num_scalar_prefetch=1, grid=(C // tc,),
            in_specs=[pl.BlockSpec((1, rows, tc), lambda i, chip: (chip[0], 0, i))] + [slab(j) for j in range(N_CHIPS)],
            out_specs=pl.BlockSpec((rows, tc), lambda i, chip: (0, i))),
        out_shape=jax.ShapeDtypeStruct((rows, C), F32),
        compiler_params=_cparams(("parallel",)),
    )(chip, own, parts, parts, parts, parts)


def _adamw_math(w, g, m, v):
    m = ADAM_B1 * m + (1.0 - ADAM_B1) * g
    v = ADAM_B2 * v + (1.0 - ADAM_B2) * (g * g)
    m_hat = m / (1.0 - ADAM_B1 ** ADAM_STEP)
    v_hat = v / (1.0 - ADAM_B2 ** ADAM_STEP)
    delta = -ADAM_LR * (m_hat / (jnp.sqrt(v_hat) + ADAM_EPS) + ADAM_WD * w)
    return delta, m, v


def _adamw_rows(w, g_own, g_sib, m, v, core, name):
    R, C = w.shape[0], w.shape[-1]
    rows = g_own.shape[0]
    step = 256
    chunks = [(r, min(step, R - r)) for r in range(0, R, step)]
    sub = 64

    def body(core_ref, w_hbm, go_hbm, gs_hbm, m_hbm, v_hbm, d_hbm, nm_hbm, nv_hbm, g_hbm,
             wbuf, mbuf, vbuf, gbuf, dbuf, nmbuf, nvbuf, in_sems, g_sems, out_sems):
        c = core_ref[0]
        flat = lambda ref: ref.at[:, 0, :]
        g_in = [pltpu.make_async_copy(go_hbm, gbuf.at[pl.ds(pl.multiple_of(c * rows, 8), rows), :], g_sems.at[0]),
                pltpu.make_async_copy(gs_hbm, gbuf.at[pl.ds(pl.multiple_of((1 - c) * rows, 8), rows), :], g_sems.at[1])]
        for cp in g_in:
            cp.start()
        loads = []
        for k, (r0, n) in enumerate(chunks):
            cps = [pltpu.make_async_copy(flat(src).at[pl.ds(r0, n), :], dst.at[pl.ds(r0, n), :], in_sems.at[a, k])
                   for a, (src, dst) in enumerate(((w_hbm, wbuf), (m_hbm, mbuf), (v_hbm, vbuf)))]
            for cp in cps:
                cp.start()
            loads.append(cps)
        for cp in g_in:
            cp.wait()
        stores = []
        for k, (r0, n) in enumerate(chunks):
            for cp in loads[k]:
                cp.wait()

            def update(rs):
                g = gbuf[rs, :]
                dl, nm, nv = _adamw_math(wbuf[rs, :], g, mbuf[rs, :], vbuf[rs, :])
                dbuf[rs, :] = dl
                nmbuf[rs, :] = nm
                nvbuf[rs, :] = nv

            if n % sub == 0:
                def block(i, carry, r0=r0):
                    update(pl.ds(pl.multiple_of(r0 + i * sub, 8), sub))
                    return carry
                lax.fori_loop(0, n // sub, block, 0)
            else:
                update(pl.ds(r0, n))
            cps = [pltpu.make_async_copy(src.at[pl.ds(r0, n), :], flat(dst).at[pl.ds(r0, n), :], out_sems.at[a, k])
                   for a, (src, dst) in enumerate(((dbuf, d_hbm), (nmbuf, nm_hbm), (nvbuf, nv_hbm), (gbuf, g_hbm)))]
            for cp in cps:
                cp.start()
            stores += cps
        for cp in stores:
            cp.wait()

    any_spec = pl.BlockSpec(memory_space=pl.ANY)
    dense = pltpu.VMEM((R, C), F32)
    return pl.pallas_call(
        body, name=name,
        grid_spec=pltpu.PrefetchScalarGridSpec(
            num_scalar_prefetch=1, grid=(1,),
            in_specs=[any_spec] * 5, out_specs=[any_spec] * 4,
            scratch_shapes=[dense, dense, dense, pltpu.VMEM((2 * rows, C), F32), dense, dense, dense,
                            pltpu.SemaphoreType.DMA((3, len(chunks))), pltpu.SemaphoreType.DMA((2,)),
                            pltpu.SemaphoreType.DMA((4, len(chunks)))]),
        out_shape=[jax.ShapeDtypeStruct(w.shape, F32)] * 4,
        compiler_params=_cparams(),
    )(core, w, g_own, g_sib, m, v)


def _adamw_sum8(w, slabs, land, m, v, ids, name):
    R, C = w.shape
    tc = 128

    def body(ids_ref, w_ref, own_ref, *refs):
        lrefs, (m_ref, v_ref, d_ref, nm_ref, nv_ref, g_ref) = refs[:8], refs[8:]
        g = None
        for d, l_ref in enumerate(lrefs):
            term = jnp.where(ids_ref[0] == d, own_ref[0], l_ref[0]).astype(F32)
            g = term if g is None else g + term
        dl, nm, nv = _adamw_math(w_ref[...], g, m_ref[...], v_ref[...])
        d_ref[...] = dl
        nm_ref[...] = nm
        nv_ref[...] = nv
        g_ref[...] = g

    def slot(d):
        return pl.BlockSpec((1, R, tc), lambda i, ids: (jnp.where(ids[0] == d, (d + 1) % 8, d), 0, i))

    spec = pl.BlockSpec((R, tc), lambda i, ids: (0, i))
    return pl.pallas_call(
        body, name=name,
        grid_spec=pltpu.PrefetchScalarGridSpec(
            num_scalar_prefetch=1, grid=(C // tc,),
            in_specs=[spec, pl.BlockSpec((1, R, tc), lambda i, ids: (ids[1], 0, i))] + [slot(d) for d in range(8)]
            + [spec, spec],
            out_specs=[spec] * 4),
        out_shape=[jax.ShapeDtypeStruct((R, C), F32)] * 4,
        compiler_params=_cparams(("parallel",)),
    )(ids, w, slabs, *([land] * 8), m, v)


SMALL_NAMES = ("conv_b", "ssd_norm_w", "ln_g", "ln_b", "dt_bias", "a_log", "d_skip", "attn_sinks")
SMALL_FIELDS = ((4, 0, D_XBC), (5, 0, D_SSD), (6, 0, D_MODEL), (7, 0, D_MODEL), (5, 1024, SSD_HEADS), (5, 1152, SSD_HEADS),
                (5, 1280, SSD_HEADS), (5, 1408, ATT_QH))
LOSS_FIELD = (6, 1024, 128)
K_SMALL = D_XBC


def _pack_small(g_conv_w, vecs, loss):
    def body(cw_ref, *refs):
        o_ref = refs[-1]
        o_ref[...] = jnp.zeros_like(o_ref)
        o_ref[0:CONV_K, 0:D_XBC] = cw_ref[...]
        for v_ref, (row, off, n) in zip(refs[:-2], SMALL_FIELDS):
            o_ref[row:row + 1, off:off + n] = v_ref[...]
        o_ref[LOSS_FIELD[0]:LOSS_FIELD[0] + 1, LOSS_FIELD[1]:LOSS_FIELD[1] + LOSS_FIELD[2]] = refs[-2][...]

    return pl.pallas_call(
        body, name="pack_small", out_shape=jax.ShapeDtypeStruct((8, K_SMALL), F32), compiler_params=_cparams(),
    )(g_conv_w, *vecs, loss)


def _adamw_small(slots, chip, conv_w, m_conv_w, v_conv_w, params, moms, vars_):
    n_vec = len(SMALL_NAMES)

    def body(chip_ref, s_ref, *refs):
        ins = refs[:3 * (n_vec + 1)]
        outs = refs[3 * (n_vec + 1):-1]
        tot_ref = refs[-1]
        tot = s_ref[0]
        for d in range(1, 8):
            tot = tot + s_ref[d]
        outs[0][...] = tot[LOSS_FIELD[0]:LOSS_FIELD[0] + 1, LOSS_FIELD[1]:LOSS_FIELD[1] + 1]
        off = pl.multiple_of(chip_ref[0] * CONV_COLS, 128)
        tot_ref[...] = tot
        grads = [tot_ref[0:CONV_K, pl.ds(off, CONV_COLS)]]
        grads += [tot[row:row + 1, o:o + n] for row, o, n in SMALL_FIELDS]
        for k, g in enumerate(grads):
            w_ref, m_ref, v_ref = ins[3 * k:3 * k + 3]
            full = (0,) if k == 0 else (Ellipsis,)
            d, nm, nv = _adamw_math(w_ref[full], g, m_ref[full], v_ref[full])
            for o_ref, val in zip(outs[1 + 4 * k:5 + 4 * k], (g, d, nm, nv)):
                o_ref[full] = val

    args = [conv_w, m_conv_w, v_conv_w]
    for w, m, v in zip(params, moms, vars_):
        args += [w, m, v]
    shapes = [jax.ShapeDtypeStruct((1, 1), F32)] + [jax.ShapeDtypeStruct(conv_w.shape, F32)] * 4
    for w in params:
        shapes += [jax.ShapeDtypeStruct(w.shape, F32)] * 4
    vmem = pl.BlockSpec(memory_space=pltpu.VMEM)
    return pl.pallas_call(
        body, name="adamw_small",
        grid_spec=pltpu.PrefetchScalarGridSpec(
            num_scalar_prefetch=1, grid=(1,),
            in_specs=[pl.BlockSpec(slots.shape, lambda i, chip: (0, 0, 0))] + [vmem] * len(args),
            out_specs=[vmem] * len(shapes), scratch_shapes=[pltpu.VMEM((8, K_SMALL), F32)]),
        out_shape=shapes, compiler_params=_cparams(),
    )(chip, slots, *args)


def kernel(x, positions, w_in, conv_w, conv_b, dt_bias, a_log, d_skip, ssd_norm_w, attn_sinks, w_out, ln_g, ln_b, loss_target, m_w_in, m_conv_w, m_conv_b, m_dt_bias, m_a_log, m_d_skip, m_ssd_norm_w, m_attn_sinks, m_w_out, m_ln_g, m_ln_b, v_w_in, v_conv_w, v_conv_b, v_dt_bias, v_a_log, v_d_skip, v_ssd_norm_w, v_attn_sinks, v_w_out, v_ln_g, v_ln_b):
    mx, my, mc = _mesh_pos()
    chip = 2 * mx + my
    L = x.shape[1]

    conv_w_s8 = jnp.pad(conv_w[0], ((0, 8 - CONV_K), (0, 0)))
    pad_rows = ((0, SLAB_ROWS - W_IN_COLS), (0, 0))
    w_in_t = w_in[0].T
    w_in_b, w_out_b = jnp.pad(_bf(w_in_t), pad_rows), _bf(w_out[0])
    ag_in, ag_cw = _gather_weights(w_in_b, conv_w_s8)
    started = _gather_w_out_start(w_out_b, ag_cw)
    own = (jnp.arange(N_CHIPS) == chip)[:, None, None]

    def get_w_out(after):
        landed = _gather_w_out_wait(started[0:6], started[6], started[7], after)
        return jnp.where(own, w_out_b[None], landed).reshape(D_MIX, D_MODEL)

    w_full = jnp.concatenate([ag_in[j, 0:W_IN_COLS] for j in range(N_CHIPS)], axis=0)
    w = jnp.concatenate([
        w_full[O_Z:O_Z + D_SSD], w_full[O_G:O_G + D_ATT], w_full[O_Q:O_Q + D_ATT],
        w_full[O_XBC:O_XBC + D_XBC], w_full[O_K:O_K + 2 * D_KV], w_full[O_DT:O_DT + SSD_HEADS],
        jnp.zeros((DT_PAD - SSD_HEADS, D_MODEL), BF16)], axis=0)
    conv_w_full = jnp.concatenate([ag_cw[j, 0:CONV_K] for j in range(N_CHIPS)], axis=1)

    loss_part, gx_args, gw_in, w_out_red, small = _local_step(
        x[0], positions[0].reshape(L, 1), loss_target[0], w, get_w_out, started[8][0:1, :], conv_w_full,
        conv_b, dt_bias, a_log, d_skip, ssd_norm_w, attn_sinks, ln_g, ln_b)

    packed = _pack_small(small["conv_w"], [small[n] for n in SMALL_NAMES], loss_part)
    core_id = mc.reshape(1).astype(jnp.int32)
    chip_id = chip.reshape(1).astype(jnp.int32)
    ids = jnp.stack([4 * mx + 2 * my + mc, chip]).astype(jnp.int32)
    slabs = jnp.stack([jnp.pad(gw_in[W_IN_COLS * j:W_IN_COLS * (j + 1)], pad_rows) for j in range(N_CHIPS)])
    w_in_red = _pair_start(slabs, packed)
    own_slabs, landed = _reduce_w_out_wait(w_out_red[0:14], w_out_red[14], w_out_red[15], w_in_red[10])
    out_t = _adamw_sum8(w_out[0], own_slabs, landed, m_w_out[0], v_w_out[0], ids, "adamw_w_out")
    d_w_out, nm_w_out, nv_w_out, g_w_out = [a[None] for a in out_t]
    gw_in_slabs, recv_in = _pair_wait(w_in_red[0:8], w_in_red[8], w_in_red[9], out_t[0])
    s_in = _pair_add(gw_in_slabs, recv_in, core_id, "pair_add_in")
    chip_red = _chip_start(s_in, packed)
    grad_x = _grad_x(*gx_args, chip_red[8])
    s_in, r_in = _chip_wait(chip_red[0:6], chip_red[6], chip_red[7], grad_x)
    h_in = _chip_add(s_in, r_in, chip_id, "chip_add_in")
    sib_in, slots = _pair_share(h_in, packed)

    to_rows = lambda a: jnp.transpose(a, (2, 0, 1))
    in_t = _adamw_rows(to_rows(w_in), h_in, sib_in, to_rows(m_w_in), to_rows(v_w_in), core_id, "adamw_w_in")
    d_w_in, nm_w_in, nv_w_in, g_w_in = [jnp.transpose(a, (1, 2, 0)) for a in in_t]

    params = dict(conv_b=conv_b, ssd_norm_w=ssd_norm_w, ln_g=ln_g, ln_b=ln_b, dt_bias=dt_bias, a_log=a_log,
                  d_skip=d_skip, attn_sinks=attn_sinks)
    moms = dict(conv_b=m_conv_b, ssd_norm_w=m_ssd_norm_w, ln_g=m_ln_g, ln_b=m_ln_b, dt_bias=m_dt_bias, a_log=m_a_log,
                d_skip=m_d_skip, attn_sinks=m_attn_sinks)
    vars_ = dict(conv_b=v_conv_b, ssd_norm_w=v_ssd_norm_w, ln_g=v_ln_g, ln_b=v_ln_b, dt_bias=v_dt_bias, a_log=v_a_log,
                 d_skip=v_d_skip, attn_sinks=v_attn_sinks)
    res = _adamw_small(slots, chip_id, conv_w, m_conv_w, v_conv_w, [params[n] for n in SMALL_NAMES],
                       [moms[n] for n in SMALL_NAMES], [vars_[n] for n in SMALL_NAMES])
    loss = res[0][0, 0]
    grads, delta, new_m, new_v = {}, {}, {}, {}
    for k, n in enumerate(("conv_w",) + SMALL_NAMES):
        grads[n], delta[n], new_m[n], new_v[n] = res[1 + 4 * k:5 + 4 * k]
    for dd, a_in, a_out in ((grads, g_w_in, g_w_out), (delta, d_w_in, d_w_out), (new_m, nm_w_in, nm_w_out),
                            (new_v, nv_w_in, nv_w_out)):
        dd["w_in"] = a_in
        dd["w_out"] = a_out
    order = ("w_in", "conv_w", "conv_b", "dt_bias", "a_log", "d_skip", "ssd_norm_w", "attn_sinks", "w_out", "ln_g", "ln_b")
    return (loss, grad_x[None], *[grads[n] for n in order], *[delta[n] for n in order], *[new_m[n] for n in order],
            *[new_v[n] for n in order])
```

```python
import numpy as np
import jax
import jax.numpy as jnp
from jax import lax
from jax.experimental import pallas as pl
from jax.experimental.pallas import tpu as pltpu

F32 = jnp.float32
BF16 = jnp.bfloat16
MESH = pl.DeviceIdType.MESH

D_MODEL = 1024
D_SSD = 1024
D_ATT = 1024
D_MIX = 2048
SSD_HEADS = 16
SSD_P = 64
SSD_GROUPS = 2
SSD_R = 8
SSD_N = 128
D_BC = 256
D_XBC = 1536
CONV_K = 4
CHUNK = 128
ATT_HD = 64
assert ATT_HD in (4, 16, 64, 256)
ATT_QH = 16
ATT_KVH = 4
ATT_R = 4
D_KV = 256
WINDOW = 128
ROPE_THETA = 500000.0
ROPE_DIM = 16
ALPHA = 2.0 ** 0.25
LN_EPS = 1e-5
RMS_EPS = 1e-5
D_IN_PROJ = 5136
O_Z, O_XBC, O_DT, O_Q, O_K, O_V, O_G = 0, 1024, 2560, 2576, 3600, 3856, 4112
P_Z, P_G, P_Q, P_XBC, P_KV, P_DT, P_END = 0, 1024, 2048, 3072, 4608, 5120, 5248
DT_PAD = 128
N_CHIPS = 4
W_IN_COLS = D_IN_PROJ // N_CHIPS
SLAB_ROWS = 1312
W_OUT_ROWS = D_MIX // N_CHIPS
CONV_COLS = D_XBC // N_CHIPS

ADAM_LR = 0.001
ADAM_B1 = 0.9
ADAM_B2 = 0.999
ADAM_EPS = 1e-08
ADAM_WD = 0.01
ADAM_STEP = 10

VMEM_LIMIT = 56 * 1024 * 1024
ROW_TILE = 512
NEG_BIG = -1e30


def _cparams(sem=None, **kw):
    if sem is not None:
        kw["dimension_semantics"] = sem
    return pltpu.CompilerParams(vmem_limit_bytes=VMEM_LIMIT, **kw)


def _dot(a, b):
    return jnp.dot(a, b, preferred_element_type=F32)


def _dot_nt(a, b):
    return lax.dot_general(a, b, (((1,), (1,)), ((), ())), preferred_element_type=F32)


def _dot_tn(a, b):
    return lax.dot_general(a, b, (((0,), (0,)), ((), ())), preferred_element_type=F32)


def _bf(a):
    return a.astype(BF16)


def _iota2(shape, dim):
    return lax.broadcasted_iota(jnp.int32, shape, dim)


def _three_terms(x):
    hi = _bf(x)
    r = x - hi.astype(F32)
    mid = _bf(r)
    return hi, mid, _bf(r - mid.astype(F32))


def _dot01(m, a):
    return sum(_dot(m, t) for t in _three_terms(a))


def _to_rows(col):
    k = col.shape[1]
    eye = (_iota2((k, k), 0) == _iota2((k, k), 1)).astype(BF16)
    return sum(_dot_nt(eye, t) for t in _three_terms(col))


def _to_cols(row):
    n = row.shape[1]
    eye = (_iota2((n, n), 0) == _iota2((n, n), 1)).astype(BF16)
    return sum(_dot_nt(eye, t) for t in _three_terms(row))


def _sigmoid(x):
    return jax.nn.sigmoid(x)


def _in_proj(x, w, pos, inv):
    L = x.shape[0]
    tm = ROW_TILE
    widths = (D_SSD, D_ATT, D_ATT, D_XBC, 2 * D_KV, DT_PAD)

    def body(x_ref, w_ref, pos_ref, inv_ref, z_ref, g_ref, q_ref, xbc_ref, kv_ref, dt_ref, xb_ref):
        xb = _bf(x_ref[...])
        xb_ref[...] = xb
        tabs = _rope_tables(pos_ref, inv_ref)
        q_ref[...] = _bf(_rope(_dot_nt(xb, w_ref[P_Q:P_Q + D_ATT, :]), tabs))
        kv_ref[:, 0:D_KV] = _bf(_rope(_dot_nt(xb, w_ref[P_KV:P_KV + D_KV, :]), tabs))
        kv_ref[:, D_KV:2 * D_KV] = _bf(_dot_nt(xb, w_ref[P_KV + D_KV:P_KV + 2 * D_KV, :]))
        for o_ref, off, wd in zip((z_ref, g_ref, xbc_ref, dt_ref), (P_Z, P_G, P_XBC, P_DT), (D_SSD, D_ATT, D_XBC, DT_PAD)):
            o_ref[...] = _dot_nt(xb, w_ref[off:off + wd, :])

    row = lambda wd: pl.BlockSpec((tm, wd), lambda i: (i, 0))
    return pl.pallas_call(
        body, name="in_proj", grid=(L // tm,),
        in_specs=[row(D_MODEL), pl.BlockSpec((P_END, D_MODEL), lambda i: (0, 0), pipeline_mode=pl.Buffered(1)), row(1),
                  pl.BlockSpec((1, 2 * ATT_HD), lambda i: (0, 0))],
        out_specs=[row(wd) for wd in widths] + [row(D_MODEL)],
        out_shape=[jax.ShapeDtypeStruct((L, wd), dt) for wd, dt in zip(widths, (F32, F32, BF16, F32, BF16, F32))]
        + [jax.ShapeDtypeStruct((L, D_MODEL), BF16)],
        compiler_params=_cparams(("parallel",)),
    )(x, w, pos, inv)


def _matmuls_tn(a_list, b, name):
    K, N = b.shape
    tk = min(K, 1024)
    n = len(a_list)

    def body(*refs):
        b_ref = refs[n]
        o_refs, acc_refs = refs[n + 1:2 * n + 1], refs[2 * n + 1:]

        @pl.when(pl.program_id(0) == 0)
        def _():
            for acc in acc_refs:
                acc[...] = jnp.zeros_like(acc)

        bb = _bf(b_ref[...])
        for a_ref, o_ref, acc in zip(refs[:n], o_refs, acc_refs):
            total = acc[...] + _dot_tn(_bf(a_ref[...]), bb)
            acc[...] = total
            o_ref[...] = _bf(total)

    return pl.pallas_call(
        body, name=name, grid=(K // tk,),
        in_specs=[pl.BlockSpec((tk, a.shape[1]), lambda k: (k, 0)) for a in a_list] + [pl.BlockSpec((tk, N), lambda k: (k, 0))],
        out_specs=[pl.BlockSpec((a.shape[1], N), lambda k: (0, 0)) for a in a_list],
        out_shape=[jax.ShapeDtypeStruct((a.shape[1], N), BF16) for a in a_list],
        scratch_shapes=[pltpu.VMEM((a.shape[1], N), F32) for a in a_list],
        compiler_params=_cparams(("arbitrary",)),
    )(*a_list, b)


def _grad_x(dr, dz, dg, dq, dxbc, dkv, ddt, w, after):
    L = dr.shape[0]
    tm = min(ROW_TILE, L)
    widths = (D_SSD, D_ATT, D_ATT, D_XBC, 2 * D_KV, DT_PAD)
    offs = (P_Z, P_G, P_Q, P_XBC, P_KV, P_DT)

    def body(dr_ref, dz_ref, dg_ref, dq_ref, dxbc_ref, dkv_ref, ddt_ref, w_ref, after_ref, o_ref):
        acc = ALPHA * dr_ref[...]
        for p_ref, off, wd in zip((dz_ref, dg_ref, dq_ref, dxbc_ref, dkv_ref, ddt_ref), offs, widths):
            acc = acc + _dot(_bf(p_ref[...]), w_ref[off:off + wd, :])
        o_ref[...] = acc

    row = lambda wd: pl.BlockSpec((tm, wd), lambda i: (i, 0))
    specs = ([row(D_MODEL)] + [row(wd) for wd in widths]
             + [pl.BlockSpec((P_END, D_MODEL), lambda i: (0, 0), pipeline_mode=pl.Buffered(1)),
                pl.BlockSpec((8, 128), lambda i: (0, 0))])
    return pl.pallas_call(
        body, name="grad_x", grid=(L // tm,),
        in_specs=specs, out_specs=row(D_MODEL),
        out_shape=jax.ShapeDtypeStruct((L, D_MODEL), F32),
        compiler_params=_cparams(("parallel",)),
    )(dr, dz, dg, dq, dxbc, dkv, ddt, w, after)


HALO = 16


def _shift_matrix(offsets):
    n = CHUNK + HALO
    m = np.zeros((len(offsets) * CHUNK, 2 * n), np.float32)
    for k, off in enumerate(offsets):
        t = np.arange(CHUNK)
        m[k * CHUNK + t, t + off] = 1.0
        m[k * CHUNK + t, n + t + off] = 1.0
    return jnp.asarray(m, BF16)


def _shifted_rows(first_part, second_part, smat_ref):
    h1, l1 = _hi_lo(first_part)
    h2, l2 = _hi_lo(second_part)
    sh = _dot(smat_ref[...], jnp.concatenate([h1, h2, l1, l2], axis=0))
    return sh[0:CHUNK], sh[CHUNK:2 * CHUNK], sh[2 * CHUNK:3 * CHUNK]


def _ssd_chunk_pre(first, xbc_ref, tail_ref, dt_ref, cw_ref, cb_ref, dtb_ref, alog_ref, smat_ref=None, ext=None):
    tail = jnp.where(first, 0.0, tail_ref[...])
    x = xbc_ref[...]
    if ext is None:
        taps = _shifted_rows(tail, x, smat_ref) + (x,)
    else:
        ext[0:HALO, :] = tail
        ext[HALO:HALO + CHUNK, :] = x
        taps = tuple(ext[pl.ds(HALO - (CONV_K - 1) + k, CHUNK), :] for k in range(CONV_K - 1)) + (x,)
    u = cb_ref[...] + cw_ref[0:1, :] * taps[0]
    for k in range(1, CONV_K):
        u = u + cw_ref[k:k + 1, :] * taps[k]
    sig = _sigmoid(u)
    xbc = u * sig
    dtraw = dt_ref[:, 0:SSD_HEADS] + dtb_ref[...]
    dt = jax.nn.softplus(dtraw)
    A = -jnp.exp(alog_ref[...])
    a = dt * A
    tril = (_iota2((CHUNK, CHUNK), 0) >= _iota2((CHUNK, CHUNK), 1)).astype(BF16)
    acs = _dot01(tril, a)
    acs_row = _to_rows(acs)
    return u, sig, xbc, dtraw, dt, A, acs, acs_row, taps


def _head_expander():
    return (_iota2((SSD_HEADS, D_SSD), 1) // SSD_P == _iota2((SSD_HEADS, D_SSD), 0)).astype(BF16)


def _hi_lo(x):
    hi = _bf(x)
    return hi, _bf(x - hi.astype(F32))


def _expand(v, e):
    hi, lo = _hi_lo(v)
    return _dot(hi, e) + _dot(lo, e)


def _headsum(t, e):
    m = t.shape[0]
    if m < 8:
        t = jnp.broadcast_to(t[0:1], (8, t.shape[1]))
    hi, lo = _hi_lo(t)
    return (_dot_nt(hi, e) + _dot_nt(lo, e))[0:m]


def _ssd_decays(dt, acs, dsk_ref, e):
    alast = acs[CHUNK - 1:CHUNK, :]
    stk = jnp.concatenate([dt, jnp.exp(acs), jnp.exp(alast - acs),
                           jnp.broadcast_to(jnp.exp(alast), (8, SSD_HEADS)),
                           jnp.broadcast_to(dsk_ref[...], (8, SSD_HEADS))], axis=0)
    ex = _expand(stk, e)
    return (ex[0:CHUNK], ex[CHUNK:2 * CHUNK], ex[2 * CHUNK:3 * CHUNK], ex[3 * CHUNK:3 * CHUNK + 1],
            ex[3 * CHUNK + 8:3 * CHUNK + 9])


def _ssd_fwd(z, xbc, dtp, conv_w, conv_b, dt_bias, a_log, d_skip, norm_w):
    L = z.shape[0]
    nc = L // CHUNK
    half = D_SSD // SSD_GROUPS

    def body(z_ref, xbc_ref, tail_ref, dt_ref, cw_ref, cb_ref, dtb_ref, alog_ref, dsk_ref, nw_ref,
             y_ref, ypre_ref, prev_ref, state, ybuf, mbuf, ext):
        c = pl.program_id(0)

        @pl.when(c == 0)
        def _():
            state[...] = jnp.zeros_like(state)

        u, sig, xbcv, dtraw, dt, A, acs, acs_row, _ = _ssd_chunk_pre(
            c == 0, xbc_ref, tail_ref, dt_ref, cw_ref, cb_ref, dtb_ref, alog_ref, ext=ext)
        e = _head_expander()
        dtE, eacsE, dsdE, ealE, dskE = _ssd_decays(dt, acs, dsk_ref, e)
        xs = xbcv[:, 0:D_SSD]
        X = xs * dtE
        prev_ref[0] = state[...]
        causal = _iota2((CHUNK, CHUNK), 0) >= _iota2((CHUNK, CHUNK), 1)
        for g in range(SSD_GROUPS):
            gs = slice(half * g, half * (g + 1))
            Bg = _bf(xbcv[:, D_SSD + SSD_N * g:D_SSD + SSD_N * (g + 1)])
            Cg = _bf(xbcv[:, D_SSD + D_BC + SSD_N * g:D_SSD + D_BC + SSD_N * (g + 1)])
            cb = _dot_nt(Cg, Bg)
            for r in range(SSD_R):
                h = g * SSD_R + r
                seg = acs[:, h:h + 1] - acs_row[h:h + 1, :]
                mbuf[h] = _bf(cb * jnp.where(causal, jnp.exp(jnp.where(causal, seg, 0.0)), 0.0))
            st = state[:, gs]
            ybuf[:, gs] = _dot(Cg, _bf(st)) * eacsE[:, gs] + dskE[:, gs] * xs[:, gs]
            state[:, gs] = st * ealE[:, gs] + _dot_tn(Bg, _bf(X[:, gs] * dsdE[:, gs]))
        Xb = _bf(X)
        for h in range(SSD_HEADS):
            hs = slice(SSD_P * h, SSD_P * (h + 1))
            ybuf[:, hs] += _dot(mbuf[h], Xb[:, hs])
        y = ybuf[...]
        ypre_ref[...] = y
        zv = z_ref[...]
        yf = y * (zv * _sigmoid(zv))
        for g in range(SSD_GROUPS):
            gs = slice(half * g, half * (g + 1))
            yg = yf[:, gs]
            ms = jnp.mean(yg * yg, axis=-1, keepdims=True)
            y_ref[:, gs] = _bf(yg * lax.rsqrt(ms + RMS_EPS) * nw_ref[:, gs])

    full = lambda shape: pl.BlockSpec(shape, lambda c: (0, 0))
    return pl.pallas_call(
        body, name="ssd_fwd", grid=(nc,),
        in_specs=[
            pl.BlockSpec((CHUNK, D_SSD), lambda c: (c, 0)),
            pl.BlockSpec((CHUNK, D_XBC), lambda c: (c, 0)),
            pl.BlockSpec((HALO, D_XBC), lambda c: (jnp.maximum(c * (CHUNK // HALO) - 1, 0), 0)),
            pl.BlockSpec((CHUNK, DT_PAD), lambda c: (c, 0)),
            full((CONV_K, D_XBC)), full((1, D_XBC)), full((1, SSD_HEADS)), full((1, SSD_HEADS)), full((1, SSD_HEADS)),
            full((1, D_SSD)),
        ],
        out_specs=[
            pl.BlockSpec((CHUNK, D_SSD), lambda c: (c, 0)),
            pl.BlockSpec((CHUNK, D_SSD), lambda c: (c, 0)),
            pl.BlockSpec((1, SSD_N, D_SSD), lambda c: (c, 0, 0)),
        ],
        out_shape=[
            jax.ShapeDtypeStruct((L, D_SSD), BF16),
            jax.ShapeDtypeStruct((L, D_SSD), F32),
            jax.ShapeDtypeStruct((nc, SSD_N, D_SSD), F32),
        ],
        scratch_shapes=[
            pltpu.VMEM((SSD_N, D_SSD), F32),
            pltpu.VMEM((CHUNK, D_SSD), F32),
            pltpu.VMEM((SSD_HEADS, CHUNK, CHUNK), BF16),
            pltpu.VMEM((CHUNK + HALO, D_XBC), F32),
        ],
        compiler_params=_cparams(("arbitrary",)),
    )(z, xbc, xbc, dtp, conv_w, conv_b, dt_bias, a_log, d_skip, norm_w)


def _ssd_bwd(dy, z, ypre, xbc, dtp, prev, conv_w, conv_b, dt_bias, a_log, d_skip, norm_w):
    L = z.shape[0]
    nc = L // CHUNK
    half = D_SSD // SSD_GROUPS

    def body(dy_ref, z_ref, ypre_ref, xbc_ref, tail_ref, dt_ref, prev_ref, cw_ref, cb_ref, dtb_ref, alog_ref, dsk_ref,
             nw_ref, smat_ref, smat2_ref, dz_ref, dxbc_ref, ddt_ref, gcw_ref, gcb_ref, gdtb_ref, galog_ref, gdsk_ref,
             gnw_ref, dstate, dhead, dpost, yobuf, bdbuf, lmbuf, dmbuf, cbbuf):
        i = pl.program_id(0)
        c = nc - 1 - i

        @pl.when(i == 0)
        def _():
            dstate[...] = jnp.zeros_like(dstate)
            dhead[...] = jnp.zeros_like(dhead)
            gcw_ref[...] = jnp.zeros_like(gcw_ref)
            gcb_ref[...] = jnp.zeros_like(gcb_ref)
            gdtb_ref[...] = jnp.zeros_like(gdtb_ref)
            galog_ref[...] = jnp.zeros_like(galog_ref)
            gdsk_ref[...] = jnp.zeros_like(gdsk_ref)
            gnw_ref[...] = jnp.zeros_like(gnw_ref)

        u, sig, xbcv, dtraw, dt, A, acs, acs_row, taps = _ssd_chunk_pre(
            c == 0, xbc_ref, tail_ref, dt_ref, cw_ref, cb_ref, dtb_ref, alog_ref, smat_ref)
        e = _head_expander()
        dtE, eacsE, dsdE, ealE, dskE = _ssd_decays(dt, acs, dsk_ref, e)
        alast = acs[CHUNK - 1:CHUNK, :]
        xs = xbcv[:, 0:D_SSD]
        X = xs * dtE
        Xb = _bf(X)

        zv = z_ref[...]
        ypre = ypre_ref[...]
        dyn = dy_ref[...]
        sz = _sigmoid(zv)
        silu_z = zv * sz
        yf = ypre * silu_z
        dyf_parts = []
        for g in range(SSD_GROUPS):
            gs = slice(half * g, half * (g + 1))
            yg = yf[:, gs]
            rstd = lax.rsqrt(jnp.mean(yg * yg, axis=-1, keepdims=True) + RMS_EPS)
            dout = dyn[:, gs]
            gnw_ref[:, gs] += jnp.sum(dout * yg * rstd, axis=0, keepdims=True)
            dyhat = dout * nw_ref[:, gs]
            dyf_parts.append(rstd * (dyhat - yg * (rstd * rstd) * jnp.mean(dyhat * yg, axis=-1, keepdims=True)))
        dyf = jnp.concatenate(dyf_parts, axis=1)
        dz_ref[...] = _bf(dyf * ypre * (sz * (1.0 + zv * (1.0 - sz))))
        dyp = dyf * silu_z
        dyb = _bf(dyp)
        G = dyp * eacsE

        causal = _iota2((CHUNK, CHUNK), 0) >= _iota2((CHUNK, CHUNK), 1)
        ST = prev_ref[0]
        dST = dstate[...]
        for g in range(SSD_GROUPS):
            gs = slice(half * g, half * (g + 1))
            bs = slice(D_SSD + SSD_N * g, D_SSD + SSD_N * (g + 1))
            cs = slice(D_SSD + D_BC + SSD_N * g, D_SSD + D_BC + SSD_N * (g + 1))
            Bg = _bf(xbcv[:, bs])
            Cg = _bf(xbcv[:, cs])
            Gb = _bf(G[:, gs])
            STb = _bf(ST[:, gs])
            dSTb = _bf(dST[:, gs])
            dstate[:, gs] = dST[:, gs] * ealE[:, gs] + _dot_tn(Cg, Gb)
            yobuf[:, gs] = _dot(Cg, STb) * eacsE[:, gs]
            bdbuf[:, gs] = _dot(Bg, dSTb)
            dpost[:, cs] = _dot_nt(Gb, STb)
            dpost[:, bs] = _dot_nt(_bf(X[:, gs] * dsdE[:, gs]), dSTb)
            cbbuf[g] = _dot_nt(Cg, Bg)
            for r in range(SSD_R):
                h = g * SSD_R + r
                seg = acs[:, h:h + 1] - acs_row[h:h + 1, :]
                lmbuf[h] = jnp.where(causal, jnp.exp(jnp.where(causal, seg, 0.0)), 0.0)
        for h in range(SSD_HEADS):
            hs = slice(SSD_P * h, SSD_P * (h + 1))
            Mb = _bf(cbbuf[h // SSD_R] * lmbuf[h])
            dmbuf[h] = _dot_nt(dyb[:, hs], Xb[:, hs])
            dpost[:, hs] = _dot_tn(Mb, dyb[:, hs])
        lane16 = _iota2((1, SSD_HEADS), 1)
        sub16 = _iota2((SSD_HEADS, 1), 0)
        dacs_col = jnp.zeros((CHUNK, SSD_HEADS), F32)
        dacs_row = jnp.zeros((SSD_HEADS, CHUNK), F32)
        for g in range(SSD_GROUPS):
            bs = slice(D_SSD + SSD_N * g, D_SSD + SSD_N * (g + 1))
            cs = slice(D_SSD + D_BC + SSD_N * g, D_SSD + D_BC + SSD_N * (g + 1))
            cb = cbbuf[g]
            dcb = jnp.zeros((CHUNK, CHUNK), F32)
            for r in range(SSD_R):
                h = g * SSD_R + r
                dM = dmbuf[h]
                Lm = lmbuf[h]
                dcb = dcb + dM * Lm
                dseg = dM * (cb * Lm)
                dacs_col = dacs_col + jnp.sum(dseg, axis=-1, keepdims=True) * (lane16 == h).astype(F32)
                dacs_row = dacs_row - jnp.sum(dseg, axis=0, keepdims=True) * (sub16 == h).astype(F32)
            dcbb = _bf(dcb)
            dpost[:, bs] += _dot_tn(dcbb, _bf(xbcv[:, cs]))
            dpost[:, cs] += _dot(dcbb, _bf(xbcv[:, bs]))

        BD = bdbuf[...]
        dX = dpost[:, 0:D_SSD] + dsdE * BD
        dsd = jnp.exp(alast - acs)
        T = _headsum(X * BD, e) * dsd
        dalast = jnp.sum(T, axis=0, keepdims=True) + _headsum(
            jnp.sum(dST * ST, axis=0, keepdims=True), e) * jnp.exp(alast)
        is_last = (_iota2((CHUNK, 1), 0) == CHUNK - 1).astype(F32)
        dacs = dacs_col + _to_cols(dacs_row) + _headsum(dyp * yobuf[...], e) - T + is_last * dalast
        triu = (_iota2((CHUNK, CHUNK), 0) <= _iota2((CHUNK, CHUNK), 1)).astype(BF16)
        da = _dot01(triu, dacs)
        ddt_tot = _headsum(dX * xs, e) + da * A
        galog_ref[...] += jnp.sum(da * dt, axis=0, keepdims=True) * A
        ddtraw = ddt_tot * _sigmoid(dtraw)
        gdtb_ref[...] += jnp.sum(ddtraw, axis=0, keepdims=True)
        gdsk_ref[...] += _headsum(jnp.sum(dyp * xs, axis=0, keepdims=True), e)
        ddt_ref[...] = jnp.zeros_like(ddt_ref)
        ddt_ref[:, 0:SSD_HEADS] = ddtraw
        dpost[:, 0:D_SSD] = dX * dtE + dskE * dyp

        dconv = dpost[...] * (sig * (1.0 + u * (1.0 - sig)))
        gcb_ref[...] += jnp.sum(dconv, axis=0, keepdims=True)
        for k in range(CONV_K):
            gcw_ref[k:k + 1, :] += jnp.sum(dconv * taps[k], axis=0, keepdims=True)
        later = _shifted_rows(dconv, dhead[...], smat2_ref)
        dx = cw_ref[CONV_K - 1:CONV_K, :] * dconv
        for k in range(CONV_K - 1):
            dx = dx + cw_ref[k:k + 1, :] * later[k]
        dxbc_ref[...] = _bf(dx)
        dhead[...] = dconv[0:HALO, :]

    full = lambda shape: pl.BlockSpec(shape, lambda i: (0, 0))
    rev = lambda wd: pl.BlockSpec((CHUNK, wd), lambda i: (nc - 1 - i, 0))
    return pl.pallas_call(
        body, name="ssd_bwd", grid=(nc,),
        in_specs=[
            rev(D_SSD), rev(D_SSD), rev(D_SSD), rev(D_XBC),
            pl.BlockSpec((HALO, D_XBC), lambda i: (jnp.maximum((nc - 1 - i) * (CHUNK // HALO) - 1, 0), 0)),
            rev(DT_PAD),
            pl.BlockSpec((1, SSD_N, D_SSD), lambda i: (nc - 1 - i, 0, 0)),
            full((CONV_K, D_XBC)), full((1, D_XBC)), full((1, SSD_HEADS)), full((1, SSD_HEADS)), full((1, SSD_HEADS)),
            full((1, D_SSD)), full((3 * CHUNK, 2 * (CHUNK + HALO))), full((3 * CHUNK, 2 * (CHUNK + HALO))),
        ],
        out_specs=[
            rev(D_SSD), rev(D_XBC), rev(DT_PAD),
            full((CONV_K, D_XBC)), full((1, D_XBC)), full((1, SSD_HEADS)), full((1, SSD_HEADS)), full((1, SSD_HEADS)),
            full((1, D_SSD)),
        ],
        out_shape=[
            jax.ShapeDtypeStruct((L, D_SSD), BF16), jax.ShapeDtypeStruct((L, D_XBC), BF16),
            jax.ShapeDtypeStruct((L, DT_PAD), F32),
            jax.ShapeDtypeStruct((CONV_K, D_XBC), F32), jax.ShapeDtypeStruct((1, D_XBC), F32),
            jax.ShapeDtypeStruct((1, SSD_HEADS), F32), jax.ShapeDtypeStruct((1, SSD_HEADS), F32),
            jax.ShapeDtypeStruct((1, SSD_HEADS), F32), jax.ShapeDtypeStruct((1, D_SSD), F32),
        ],
        scratch_shapes=[
            pltpu.VMEM((SSD_N, D_SSD), F32),
            pltpu.VMEM((HALO, D_XBC), F32),
            pltpu.VMEM((CHUNK, D_XBC), F32),
            pltpu.VMEM((CHUNK, D_SSD), F32),
            pltpu.VMEM((CHUNK, D_SSD), F32),
            pltpu.VMEM((SSD_HEADS, CHUNK, CHUNK), F32),
            pltpu.VMEM((SSD_HEADS, CHUNK, CHUNK), F32),
            pltpu.VMEM((SSD_GROUPS, CHUNK, CHUNK), F32),
        ],
        compiler_params=_cparams(("arbitrary",)),
    )(dy, z, ypre, xbc, xbc, dtp, prev, conv_w, conv_b, dt_bias, a_log, d_skip, norm_w, _shift_matrix((13, 14, 15)),
      _shift_matrix((3, 2, 1)))


def _rope_tables(pos_ref, inv_ref):
    ang = pos_ref[...].astype(F32) * inv_ref[...]
    d = _iota2((1, 2 * ATT_HD), 1) % ATT_HD
    s = jnp.sin(ang)
    return jnp.cos(ang), jnp.where(d < ROPE_DIM // 2, -s, 0.0), jnp.where((d >= ROPE_DIM // 2) & (d < ROPE_DIM), s, 0.0)


def _rope(t, tabs):
    c, s1, s2 = tabs
    n = t.shape[1]
    rep = n // c.shape[1]
    return (t * jnp.tile(c, (1, rep)) + pltpu.roll(t, n - ROPE_DIM // 2, 1) * jnp.tile(s1, (1, rep))
            + pltpu.roll(t, ROPE_DIM // 2, 1) * jnp.tile(s2, (1, rep)))


def _rope_t(t, tabs):
    c, s1, s2 = tabs
    n = t.shape[1]
    rep = n // c.shape[1]
    return (t * jnp.tile(c, (1, rep)) + pltpu.roll(t * jnp.tile(s1, (1, rep)), ROPE_DIM // 2, 1)
            + pltpu.roll(t * jnp.tile(s2, (1, rep)), n - ROPE_DIM // 2, 1))


def _stack_heads(t, j):
    return jnp.concatenate([t[:, ATT_HD * (j * ATT_R + r):ATT_HD * (j * ATT_R + r + 1)] for r in range(ATT_R)], axis=0)


def _swa_mask_t(first):
    si = _iota2((2 * WINDOW, ATT_R * WINDOW), 0)
    qi = _iota2((2 * WINDOW, ATT_R * WINDOW), 1) % WINDOW
    band = (si > qi) & (si <= qi + WINDOW)
    return band & (jnp.logical_not(first) | (si >= WINDOW))


def _head_rows(ref, j):
    if ref.shape[0] == 1:
        parts = [jnp.broadcast_to(ref[:, j * ATT_R + r:j * ATT_R + r + 1], (1, WINDOW)) for r in range(ATT_R)]
    else:
        parts = [ref[j * ATT_R + r:j * ATT_R + r + 1, :] for r in range(ATT_R)]
    return jnp.concatenate(parts, axis=1)


def _swa_fwd(q, g, kv, sinks):
    L = q.shape[0]
    nb = L // WINDOW
    scale = ATT_HD ** -0.5

    def body(q_ref, g_ref, kvc_ref, kvp_ref, sink_ref, y_ref, o_ref, lse_ref, otbuf):
        n = pl.program_id(0)
        kk = jnp.concatenate([kvp_ref[:, 0:D_KV], kvc_ref[:, 0:D_KV]], axis=0) * scale
        vv = jnp.concatenate([kvp_ref[:, D_KV:2 * D_KV], kvc_ref[:, D_KV:2 * D_KV]], axis=0)
        valid = _swa_mask_t(n == 0)
        qv = q_ref[...]
        for j in range(ATT_KVH):
            js = slice(ATT_HD * j, ATT_HD * (j + 1))
            st = _dot_nt(kk[:, js], _stack_heads(qv, j))
            st = jnp.where(valid, st, NEG_BIG)
            sink = _head_rows(sink_ref, j)
            m = jnp.maximum(jnp.max(st, axis=0, keepdims=True), sink)
            p = jnp.exp(st - m)
            vx = jnp.concatenate([vv[:, js], jnp.ones((2 * WINDOW, ATT_HD), BF16)], axis=1)
            otx = _dot_tn(vx, _bf(p))
            denom = otx[ATT_HD:ATT_HD + 1] + jnp.exp(sink - m)
            ot = otx[0:ATT_HD] * (1.0 / denom)
            lse = m + jnp.log(denom)
            for r in range(ATT_R):
                h = j * ATT_R + r
                otbuf[ATT_HD * h:ATT_HD * (h + 1), :] = ot[:, WINDOW * r:WINDOW * (r + 1)]
                lse_ref[h:h + 1, :] = lse[:, WINDOW * r:WINDOW * (r + 1)]
        o = otbuf[...].T
        o_ref[...] = o
        gv = g_ref[...]
        y_ref[...] = _bf(o * (gv * _sigmoid(gv)))

    cur = lambda wd: pl.BlockSpec((WINDOW, wd), lambda n: (n, 0))
    prv = lambda wd: pl.BlockSpec((WINDOW, wd), lambda n: (jnp.maximum(n - 1, 0), 0))
    return pl.pallas_call(
        body, name="swa_fwd", grid=(nb,),
        in_specs=[cur(D_ATT), cur(D_ATT), cur(2 * D_KV), prv(2 * D_KV), pl.BlockSpec((1, ATT_QH), lambda n: (0, 0))],
        out_specs=[cur(D_ATT), cur(D_ATT), pl.BlockSpec((ATT_QH, WINDOW), lambda n: (0, n))],
        out_shape=[jax.ShapeDtypeStruct((L, D_ATT), BF16), jax.ShapeDtypeStruct((L, D_ATT), F32),
                   jax.ShapeDtypeStruct((ATT_QH, L), F32)],
        scratch_shapes=[pltpu.VMEM((D_ATT, WINDOW), F32)],
        compiler_params=_cparams(("parallel",)),
    )(q, g, kv, kv, sinks)


def _swa_bwd(dy, q, g, kv, o, lse, pos, inv, sinks):
    L = q.shape[0]
    nb = L // WINDOW
    scale = ATT_HD ** -0.5

    def body(dy_ref, q_ref, g_ref, kvc_ref, kvp_ref, o_ref, lse_ref, posc_ref, posp_ref, inv_ref, sink_ref,
             dq_ref, dg_ref, dkv_ref, dsink_ref, carry, dqbuf, dkbuf, dvbuf):
        n = pl.program_id(0)

        @pl.when(n == 0)
        def _():
            dsink_ref[...] = jnp.zeros_like(dsink_ref)

        @pl.when(n < nb)
        def _():
            tc = _rope_tables(posc_ref, inv_ref)
            tp = _rope_tables(posp_ref, inv_ref)
            kk = jnp.concatenate([kvp_ref[:, 0:D_KV], kvc_ref[:, 0:D_KV]], axis=0) * scale
            vv = jnp.concatenate([kvp_ref[:, D_KV:2 * D_KV], kvc_ref[:, D_KV:2 * D_KV]], axis=0)
            valid = _swa_mask_t(n == 0)
            qv = q_ref[...]
            gv = g_ref[...]
            sg = _sigmoid(gv)
            dyv = dy_ref[...]
            ov = o_ref[...]
            dg_ref[...] = _bf(dyv * ov * (sg * (1.0 + gv * (1.0 - sg))))
            do = dyv * (gv * sg)
            dod = do * ov
            ones = jnp.ones((8, ATT_HD), BF16)
            lane16 = _iota2((1, ATT_QH), 1)
            dsink = jnp.zeros((1, ATT_QH), F32)
            for j in range(ATT_KVH):
                js = slice(ATT_HD * j, ATT_HD * (j + 1))
                kj = kk[:, js]
                vj = vv[:, js]
                qs = _stack_heads(qv, j)
                dos = _bf(_stack_heads(do, j))
                hi, lo = _hi_lo(_stack_heads(dod, j))
                delta = (_dot_nt(ones, hi) + _dot_nt(ones, lo))[0:1]
                lse = _head_rows(lse_ref, j)
                st = _dot_nt(kj, qs)
                pt = jnp.exp(jnp.where(valid, st, NEG_BIG) - lse)
                dst = _bf(pt * (_dot_nt(vj, dos) - delta))
                dqt = _dot_tn(kj, dst)
                dkbuf[:, js] = _dot(dst, qs) * scale
                dvbuf[:, js] = _dot(_bf(pt), dos)
                sd = jnp.exp(_head_rows(sink_ref, j) - lse) * delta
                for r in range(ATT_R):
                    h = j * ATT_R + r
                    ls = slice(WINDOW * r, WINDOW * (r + 1))
                    dqbuf[ATT_HD * h:ATT_HD * (h + 1), :] = dqt[:, ls]
                    dsink = dsink - jnp.sum(sd[:, ls], axis=1, keepdims=True) * (lane16 == h).astype(F32)
            dsink_ref[...] += dsink
            dq_ref[...] = _bf(_rope_t(dqbuf[...].T, tc))
            dkp = _rope_t(dkbuf[0:WINDOW, :], tp)
            dkc = _rope_t(dkbuf[WINDOW:2 * WINDOW, :], tc)

            @pl.when(n > 0)
            def _():
                dkv_ref[:, 0:D_KV] = _bf(carry[:, 0:D_KV] + dkp)
                dkv_ref[:, D_KV:2 * D_KV] = _bf(carry[:, D_KV:2 * D_KV] + dvbuf[0:WINDOW, :])

            carry[:, 0:D_KV] = dkc
            carry[:, D_KV:2 * D_KV] = dvbuf[WINDOW:2 * WINDOW, :]

        @pl.when(n == nb)
        def _():
            dkv_ref[...] = _bf(carry[...])

    last = nb - 1
    cur = lambda wd: pl.BlockSpec((WINDOW, wd), lambda n: (jnp.minimum(n, last), 0))
    prv = lambda wd: pl.BlockSpec((WINDOW, wd), lambda n: (jnp.maximum(jnp.minimum(n, last) - 1, 0), 0))
    return pl.pallas_call(
        body, name="swa_bwd", grid=(nb + 1,),
        in_specs=[cur(D_ATT), cur(D_ATT), cur(D_ATT), cur(2 * D_KV), prv(2 * D_KV), cur(D_ATT),
                  pl.BlockSpec((ATT_QH, WINDOW), lambda n: (0, jnp.minimum(n, last))), cur(1), prv(1),
                  pl.BlockSpec((1, 2 * ATT_HD), lambda n: (0, 0)), pl.BlockSpec((1, ATT_QH), lambda n: (0, 0))],
        out_specs=[cur(D_ATT), cur(D_ATT),
                   pl.BlockSpec((WINDOW, 2 * D_KV), lambda n: (jnp.maximum(n - 1, 0), 0)),
                   pl.BlockSpec((1, ATT_QH), lambda n: (0, 0))],
        out_shape=[jax.ShapeDtypeStruct((L, D_ATT), BF16), jax.ShapeDtypeStruct((L, D_ATT), BF16),
                   jax.ShapeDtypeStruct((L, 2 * D_KV), BF16), jax.ShapeDtypeStruct((1, ATT_QH), F32)],
        scratch_shapes=[pltpu.VMEM((WINDOW, 2 * D_KV), F32), pltpu.VMEM((D_ATT, WINDOW), F32),
                        pltpu.VMEM((2 * WINDOW, D_KV), F32), pltpu.VMEM((2 * WINDOW, D_KV), F32)],
        compiler_params=_cparams(("arbitrary",)),
    )(dy, q, g, kv, kv, o, lse, pos, pos, inv, sinks)


def _out_ln_loss(y_ssd, y_att, x, target, w_out, ln_g, ln_b):
    L = x.shape[0]
    tm = min(ROW_TILE, L)
    nt = L // tm
    inv_d = 1.0 / D_MODEL

    def body(ys_ref, ya_ref, x_ref, t_ref, w_ref, g_ref, b_ref, dr_ref, dys_ref, dya_ref, loss_ref, gg_ref, gb_ref,
             gwo_ref, acc_ref):
        i = pl.program_id(0)

        @pl.when(i == 0)
        def _():
            loss_ref[...] = jnp.zeros_like(loss_ref)
            gg_ref[...] = jnp.zeros_like(gg_ref)
            gb_ref[...] = jnp.zeros_like(gb_ref)
            acc_ref[...] = jnp.zeros_like(acc_ref)

        halves = [slice(0, tm // 2), slice(tm // 2, tm)]
        hs = [_dot(_bf(ys_ref[rs, :]), w_ref[0:D_SSD, :]) + _dot(_bf(ya_ref[rs, :]), w_ref[D_SSD:D_MIX, :]) for rs in halves]
        gam = g_ref[...]
        for rs, h in zip(halves, hs):
            r = ALPHA * x_ref[rs, :] + h
            mu = jnp.mean(r, axis=-1, keepdims=True)
            xc = r - mu
            rstd = lax.rsqrt(jnp.mean(xc * xc, axis=-1, keepdims=True) + LN_EPS)
            xhat = xc * rstd
            diff = xhat * gam + b_ref[...] - t_ref[rs, :]
            part = jnp.sum(jnp.sum(diff * diff, axis=-1, keepdims=True), axis=0, keepdims=True)
            loss_ref[...] += (0.5 * inv_d) * part
            dout = diff * inv_d
            gg_ref[...] += jnp.sum(dout * xhat, axis=0, keepdims=True)
            gb_ref[...] += jnp.sum(dout, axis=0, keepdims=True)
            dxh = dout * gam
            dr_ref[rs, :] = rstd * (dxh - jnp.mean(dxh, axis=-1, keepdims=True)
                                    - xhat * jnp.mean(dxh * xhat, axis=-1, keepdims=True))
        for rs in halves:
            drh = _bf(dr_ref[rs, :])
            dys_ref[rs, :] = _dot_nt(drh, w_ref[0:D_SSD, :])
            dya_ref[rs, :] = _dot_nt(drh, w_ref[D_SSD:D_MIX, :])
        drb = _bf(dr_ref[...])
        acc_ref[0:D_SSD, :] += _dot_tn(_bf(ys_ref[...]), drb)
        acc_ref[D_SSD:D_MIX, :] += _dot_tn(_bf(ya_ref[...]), drb)

        @pl.when(i == nt - 1)
        def _():
            gwo_ref[...] = _bf(acc_ref[...])

    row = pl.BlockSpec((tm, D_MODEL), lambda i: (i, 0))
    vec = pl.BlockSpec((1, D_MODEL), lambda i: (0, 0))
    return pl.pallas_call(
        body, name="out_ln_loss", grid=(nt,),
        in_specs=[row, row, row, row, pl.BlockSpec((D_MIX, D_MODEL), lambda i: (0, 0), pipeline_mode=pl.Buffered(1)), vec, vec],
        out_specs=[row, row, row, pl.BlockSpec((1, 128), lambda i: (0, 0)), vec, vec,
                   pl.BlockSpec((D_MIX, D_MODEL), lambda i: (0, 0))],
        out_shape=[jax.ShapeDtypeStruct((L, D_MODEL), F32)] * 3 + [jax.ShapeDtypeStruct((1, 128), F32)]
        + [jax.ShapeDtypeStruct((1, D_MODEL), F32)] * 2 + [jax.ShapeDtypeStruct((D_MIX, D_MODEL), BF16)],
        scratch_shapes=[pltpu.VMEM((D_MIX, D_MODEL), F32)],
        compiler_params=_cparams(("arbitrary",)),
    )(y_ssd, y_att, x, target, w_out, ln_g, ln_b)


def _local_step(x, pos, target, w, get_w_out, token, conv_w, conv_b, dt_bias, a_log, d_skip, norm_w, sinks, ln_g, ln_b):
    inv8 = ROPE_THETA ** (-jnp.arange(0, ROPE_DIM, 2, dtype=F32) / ROPE_DIM)
    inv = jnp.tile(jnp.concatenate([inv8, inv8, jnp.zeros((ATT_HD - ROPE_DIM,), F32)]), 2).reshape(1, 2 * ATT_HD)
    inv = inv + token

    z, g, q, xbc, kv, dtp, xb = _in_proj(x, w, pos, inv)
    y_ssd, y_pre, prev = _ssd_fwd(z, xbc, dtp, conv_w, conv_b, dt_bias, a_log, d_skip, norm_w)
    y_att, o, lse = _swa_fwd(q, g, kv, sinks)
    w_out = get_w_out(lse)
    dr, dy_ssd, dy_att, loss, g_ln_g, g_ln_b, gw_out = _out_ln_loss(y_ssd, y_att, x, target, w_out, ln_g, ln_b)
    w_out_red = _reduce_w_out_start(gw_out.reshape(N_CHIPS, W_OUT_ROWS, D_MODEL), loss)
    inv = inv + w_out_red[16][0:1, :]
    dq, dg, dkv, g_sinks = _swa_bwd(dy_att, q, g, kv, o, lse, pos, inv, sinks)
    dz, dxbc, ddt, g_conv_w, g_conv_b, g_dt_bias, g_a_log, g_d_skip, g_norm_w = _ssd_bwd(
        dy_ssd, z, y_pre, xbc, dtp, prev, conv_w, conv_b, dt_bias, a_log, d_skip, norm_w)
    gw_z, gw_g, gw_q = _matmuls_tn([dz, dg, dq], xb, "gw_zgq")
    gw_xbc, gw_kv, gw_dt = _matmuls_tn([dxbc, dkv, ddt], xb, "gw_xbc_kv_dt")
    gw_in = jnp.concatenate([gw_z, gw_xbc, gw_dt[0:SSD_HEADS], gw_q, gw_kv, gw_g], axis=0)
    small = dict(conv_w=g_conv_w, conv_b=g_conv_b, dt_bias=g_dt_bias, a_log=g_a_log, d_skip=g_d_skip,
                 ssd_norm_w=g_norm_w, attn_sinks=g_sinks, ln_g=g_ln_g, ln_b=g_ln_b)
    return loss, (dr, dz, dg, dq, dxbc, dkv, ddt, w), gw_in, w_out_red, small


def _mesh_pos():
    return lax.axis_index("x"), lax.axis_index("y"), lax.axis_index("c")


def _gather_weights(w_in_s, conv_w_s):
    hr = w_in_s.shape[0] // 2
    qa = 336
    quarters = ((0, qa), (qa, hr - qa))

    def body(win_ref, cw_ref, owin_ref, ocw_ref, stage, send_sems, recv_sems, small_send, small_recv, local_sems):
        x, y, c = _mesh_pos()
        me = 2 * x + y
        sibling = (x, y, 1 - c)
        xn, yn, dg = (1 - x, y), (x, 1 - y), (1 - x, 1 - y)
        chips = [xn, yn, dg]
        load = pltpu.make_async_copy(win_ref, stage, local_sems.at[1])
        load.start()
        locals_ = [pltpu.make_async_copy(cw_ref, ocw_ref.at[me], local_sems.at[0])]
        for cp in locals_:
            cp.start()
        started = []

        def piece(ref, chip, half, q):
            off, n = quarters[q]
            return ref.at[2 * chip[0] + chip[1]].at[pl.ds(half * hr + off, n), :]

        def mine(q):
            off, n = quarters[q]
            return win_ref.at[pl.ds(c * hr + off, n), :]

        def copy(src, dst, k, to):
            return pltpu.make_async_remote_copy(src_ref=src, dst_ref=dst, send_sem=send_sems.at[k], recv_sem=recv_sems.at[k],
                                                device_id=to, device_id_type=MESH)

        def go(cp):
            cp.start()
            started.append(cp)

        go(copy(mine(0), piece(owin_ref, (x, y), c, 0), 0, (*xn, c)))
        go(copy(mine(1), piece(owin_ref, (x, y), c, 1), 2, (*yn, c)))
        go(copy(mine(1), piece(owin_ref, (x, y), c, 1), 1, (*xn, c)))
        go(copy(mine(0), piece(owin_ref, (x, y), c, 0), 3, (*yn, c)))
        for j, (px, py) in enumerate(chips):
            cp = pltpu.make_async_remote_copy(
                src_ref=cw_ref, dst_ref=ocw_ref.at[me], send_sem=small_send.at[j], recv_sem=small_recv.at[j],
                device_id=(px, py, c), device_id_type=MESH)
            go(cp)
        load.wait()
        store = pltpu.make_async_copy(stage, owin_ref.at[me], local_sems.at[2])
        store.start()
        locals_.append(store)
        arrivals = [(0, xn, 0, (4, (*yn, c))), (2, yn, 1, (5, (*xn, c))), (1, xn, 1, None), (3, yn, 0, None),
                    (4, dg, 0, None), (5, dg, 1, None)]
        for n, (k, chip, q, onward) in enumerate(arrivals):
            blk = piece(owin_ref, chip, c, q)
            copy(blk, blk, k, sibling).wait_recv()
            if onward is not None:
                go(copy(blk, blk, onward[0], onward[1]))
            go(copy(blk, blk, 6 + n, sibling))
        for n, (k, chip, q, onward) in enumerate(arrivals):
            blk = piece(owin_ref, chip, 1 - c, q)
            copy(blk, blk, 6 + n, sibling).wait_recv()
        for j in range(3):
            pltpu.make_async_remote_copy(
                src_ref=cw_ref, dst_ref=ocw_ref.at[me], send_sem=small_send.at[j], recv_sem=small_recv.at[j],
                device_id=sibling, device_id_type=MESH).wait_recv()
        for cp in started:
            cp.wait_send()
        for cp in locals_:
            cp.wait()

    any_spec = pl.BlockSpec(memory_space=pl.ANY)
    return pl.pallas_call(
        body, name="gather_weights",
        in_specs=[any_spec] * 2, out_specs=[any_spec] * 2,
        out_shape=[jax.ShapeDtypeStruct((N_CHIPS,) + a.shape, a.dtype) for a in (w_in_s, conv_w_s)],
        scratch_shapes=[pltpu.VMEM(w_in_s.shape, w_in_s.dtype),
                        pltpu.SemaphoreType.DMA((12,)), pltpu.SemaphoreType.DMA((12,)),
                        pltpu.SemaphoreType.DMA((3,)), pltpu.SemaphoreType.DMA((3,)), pltpu.SemaphoreType.DMA((3,))],
    )(w_in_s, conv_w_s)


_HBM = pl.BlockSpec(memory_space=pltpu.HBM)
_SEM = pl.BlockSpec(memory_space=pltpu.SEMAPHORE)
_EFFECT = pltpu.SideEffectType.DATAFLOW_SIDE_EFFECTING


def _gather_w_out_start(w_out_s, after):
    def body(src_ref, land_ref, after_ref, s0, s1, s2, r0, r1, r2, src_thru, land_thru, token):
        x, y, c = _mesh_pos()
        me = 2 * x + y
        chips = [(1 - x, y), (x, 1 - y), (1 - x, 1 - y)]
        for (px, py), s, r in zip(chips, (s0, s1, s2), (r0, r1, r2)):
            pltpu.make_async_remote_copy(src_ref=src_ref, dst_ref=land_ref.at[me], send_sem=s, recv_sem=r,
                                         device_id=(px, py, c), device_id_type=MESH).start()
        token[...] = jnp.zeros_like(token)

    sem = pltpu.SemaphoreType.DMA(())
    land = lax.empty((N_CHIPS,) + w_out_s.shape, w_out_s.dtype)
    return pl.pallas_call(
        body, name="gather_w_out_start",
        out_shape=(sem,) * 6 + (pltpu.HBM(w_out_s.shape, w_out_s.dtype), pltpu.HBM(land.shape, land.dtype),
                                jax.ShapeDtypeStruct((8, 128), F32)),
        in_specs=(_HBM, _HBM, pl.BlockSpec(memory_space=pl.ANY)),
        out_specs=(_SEM,) * 6 + (_HBM, _HBM, pl.BlockSpec(memory_space=pltpu.VMEM)),
        input_output_aliases={0: 6, 1: 7},
        compiler_params=pltpu.CompilerParams(has_side_effects=_EFFECT),
    )(pltpu.with_memory_space_constraint(w_out_s, pltpu.HBM), pltpu.with_memory_space_constraint(land, pltpu.HBM), after)


def _gather_w_out_wait(sems, src_thru, land_thru, after):
    def body(src_ref, land_ref, s0, s1, s2, r0, r1, r2, after_ref, src_dead, got_ref):
        x, y, c = _mesh_pos()
        chips = [(1 - x, y), (x, 1 - y), (1 - x, 1 - y)]
        for (px, py), s, r in zip(chips, (s0, s1, s2), (r0, r1, r2)):
            cp = pltpu.make_async_remote_copy(src_ref=src_ref, dst_ref=land_ref.at[2 * px + py], send_sem=s, recv_sem=r,
                                              device_id=(px, py, c), device_id_type=MESH)
            cp.wait_send()
            cp.wait_recv()

    return pl.pallas_call(
        body, name="gather_w_out_wait",
        out_shape=(pltpu.HBM(src_thru.shape, src_thru.dtype), pltpu.HBM(land_thru.shape, land_thru.dtype)),
        in_specs=(_HBM, _HBM) + (_SEM,) * 6 + (pl.BlockSpec(memory_space=pl.ANY),),
        out_specs=(_HBM, _HBM), input_output_aliases={0: 0, 1: 1},
        compiler_params=pltpu.CompilerParams(has_side_effects=_EFFECT),
    )(src_thru, land_thru, *sems, after)[1]


def _pair_start(gw_in, after):
    hr = gw_in.shape[1] // 2

    def body(src_ref, land_ref, after_ref, *refs):
        x, y, c = _mesh_pos()
        for j in range(N_CHIPS):
            pltpu.make_async_remote_copy(
                src_ref=src_ref.at[j, pl.ds((1 - c) * hr, hr), :], dst_ref=land_ref.at[j], send_sem=refs[j],
                recv_sem=refs[N_CHIPS + j], device_id=(x, y, 1 - c), device_id_type=MESH).start()
        refs[10][...] = jnp.zeros_like(refs[10])

    sem = pltpu.SemaphoreType.DMA(())
    land = lax.empty((N_CHIPS, hr, D_MODEL), gw_in.dtype)
    return pl.pallas_call(
        body, name="pair_start",
        out_shape=(sem,) * 8 + (pltpu.HBM(gw_in.shape, gw_in.dtype), pltpu.HBM(land.shape, land.dtype),
                                jax.ShapeDtypeStruct((8, 128), F32)),
        in_specs=(_HBM, _HBM, pl.BlockSpec(memory_space=pl.ANY)),
        out_specs=(_SEM,) * 8 + (_HBM, _HBM, pl.BlockSpec(memory_space=pltpu.VMEM)),
        input_output_aliases={0: 8, 1: 9},
        compiler_params=pltpu.CompilerParams(has_side_effects=_EFFECT),
    )(pltpu.with_memory_space_constraint(gw_in, pltpu.HBM), pltpu.with_memory_space_constraint(land, pltpu.HBM), after)


def _pair_wait(sems, gw_thru, land_thru, after):
    hr = land_thru.shape[1]

    def body(src_ref, land_ref, *refs):
        x, y, c = _mesh_pos()
        for j in range(N_CHIPS):
            cp = pltpu.make_async_remote_copy(
                src_ref=src_ref.at[j, pl.ds((1 - c) * hr, hr), :], dst_ref=land_ref.at[j], send_sem=refs[j],
                recv_sem=refs[N_CHIPS + j], device_id=(x, y, 1 - c), device_id_type=MESH)
            cp.wait_send()
            cp.wait_recv()

    return pl.pallas_call(
        body, name="pair_wait",
        out_shape=(pltpu.HBM(gw_thru.shape, gw_thru.dtype), pltpu.HBM(land_thru.shape, land_thru.dtype)),
        in_specs=(_HBM, _HBM) + (_SEM,) * 8 + (pl.BlockSpec(memory_space=pl.ANY),),
        out_specs=(_HBM, _HBM), input_output_aliases={0: 0, 1: 1},
        compiler_params=pltpu.CompilerParams(has_side_effects=_EFFECT),
    )(gw_thru, land_thru, *sems, after)


def _chip_start(s_in, after):
    def body(src_ref, land_ref, after_ref, *refs):
        x, y, c = _mesh_pos()
        me = 2 * x + y
        for j, (px, py) in enumerate([(1 - x, y), (x, 1 - y), (1 - x, 1 - y)]):
            pltpu.make_async_remote_copy(
                src_ref=src_ref.at[2 * px + py], dst_ref=land_ref.at[me], send_sem=refs[j], recv_sem=refs[3 + j],
                device_id=(px, py, c), device_id_type=MESH).start()
        refs[8][...] = jnp.zeros_like(refs[8])

    sem = pltpu.SemaphoreType.DMA(())
    land = lax.empty(s_in.shape, s_in.dtype)
    return pl.pallas_call(
        body, name="chip_start",
        out_shape=(sem,) * 6 + (pltpu.HBM(s_in.shape, s_in.dtype), pltpu.HBM(land.shape, land.dtype),
                                jax.ShapeDtypeStruct((8, 128), F32)),
        in_specs=(_HBM, _HBM, pl.BlockSpec(memory_space=pl.ANY)),
        out_specs=(_SEM,) * 6 + (_HBM, _HBM, pl.BlockSpec(memory_space=pltpu.VMEM)),
        input_output_aliases={0: 6, 1: 7},
        compiler_params=pltpu.CompilerParams(has_side_effects=_EFFECT),
    )(pltpu.with_memory_space_constraint(s_in, pltpu.HBM), pltpu.with_memory_space_constraint(land, pltpu.HBM), after)


def _chip_wait(sems, s_thru, land_thru, after):
    def body(src_ref, land_ref, *refs):
        x, y, c = _mesh_pos()
        for j, (px, py) in enumerate([(1 - x, y), (x, 1 - y), (1 - x, 1 - y)]):
            cp = pltpu.make_async_remote_copy(
                src_ref=src_ref.at[2 * px + py], dst_ref=land_ref.at[2 * px + py], send_sem=refs[j], recv_sem=refs[3 + j],
                device_id=(px, py, c), device_id_type=MESH)
            cp.wait_send()
            cp.wait_recv()

    return pl.pallas_call(
        body, name="chip_wait",
        out_shape=(pltpu.HBM(s_thru.shape, s_thru.dtype), pltpu.HBM(land_thru.shape, land_thru.dtype)),
        in_specs=(_HBM, _HBM) + (_SEM,) * 6 + (pl.BlockSpec(memory_space=pl.ANY),),
        out_specs=(_HBM, _HBM), input_output_aliases={0: 0, 1: 1},
        compiler_params=pltpu.CompilerParams(has_side_effects=_EFFECT),
    )(s_thru, land_thru, *sems, after)


def _pair_share(h_in, small):
    def body(hin_ref, sm_ref, rin_ref, slots_ref, send_sems, recv_sems, small_send, small_recv, local_sem):
        x, y, c = _mesh_pos()
        dev = 4 * x + 2 * y + c
        mine = pltpu.make_async_copy(sm_ref, slots_ref.at[dev], local_sem)
        mine.start()
        share = pltpu.make_async_remote_copy(
            src_ref=hin_ref, dst_ref=rin_ref, send_sem=send_sems.at[0], recv_sem=recv_sems.at[0],
            device_id=(x, y, 1 - c), device_id_type=MESH)
        share.start()
        started = []
        for k in range(1, 8):
            peer = (x ^ ((k >> 2) & 1), y ^ ((k >> 1) & 1), c ^ (k & 1))
            cp = pltpu.make_async_remote_copy(
                src_ref=sm_ref, dst_ref=slots_ref.at[dev], send_sem=small_send.at[k - 1], recv_sem=small_recv.at[k - 1],
                device_id=peer, device_id_type=MESH)
            cp.start()
            started.append(cp)
        share.wait()
        for k in range(1, 8):
            pltpu.make_async_remote_copy(
                src_ref=sm_ref, dst_ref=slots_ref.at[dev], send_sem=small_send.at[k - 1], recv_sem=small_recv.at[k - 1],
                device_id=(x, y, 1 - c), device_id_type=MESH).wait_recv()
        for cp in started:
            cp.wait_send()
        mine.wait()

    any_spec = pl.BlockSpec(memory_space=pl.ANY)
    return pl.pallas_call(
        body, name="pair_share",
        in_specs=[any_spec] * 2, out_specs=[any_spec] * 2,
        out_shape=[jax.ShapeDtypeStruct(h_in.shape, F32), jax.ShapeDtypeStruct((8,) + small.shape, F32)],
        scratch_shapes=[pltpu.SemaphoreType.DMA((1,)), pltpu.SemaphoreType.DMA((1,)),
                        pltpu.SemaphoreType.DMA((7,)), pltpu.SemaphoreType.DMA((7,)), pltpu.SemaphoreType.DMA],
    )(h_in, small)


def _reduce_w_out_start(slabs, after):
    def body(src_ref, land_ref, after_ref, *refs):
        x, y, c = _mesh_pos()
        me = 4 * x + 2 * y + c
        for k in range(1, 8):
            px, py, pc = x ^ ((k >> 2) & 1), y ^ ((k >> 1) & 1), c ^ (k & 1)
            pltpu.make_async_remote_copy(src_ref=src_ref.at[2 * px + py], dst_ref=land_ref.at[me], send_sem=refs[k - 1],
                                         recv_sem=refs[6 + k], device_id=(px, py, pc), device_id_type=MESH).start()
        refs[16][...] = jnp.zeros_like(refs[16])

    sem = pltpu.SemaphoreType.DMA(())
    land = lax.empty((8,) + slabs.shape[1:], slabs.dtype)
    return pl.pallas_call(
        body, name="reduce_w_out_start",
        out_shape=(sem,) * 14 + (pltpu.HBM(slabs.shape, slabs.dtype), pltpu.HBM(land.shape, land.dtype),
                                 jax.ShapeDtypeStruct((8, 128), F32)),
        in_specs=(_HBM, _HBM, pl.BlockSpec(memory_space=pl.ANY)),
        out_specs=(_SEM,) * 14 + (_HBM, _HBM, pl.BlockSpec(memory_space=pltpu.VMEM)),
        input_output_aliases={0: 14, 1: 15},
        compiler_params=pltpu.CompilerParams(has_side_effects=_EFFECT),
    )(pltpu.with_memory_space_constraint(slabs, pltpu.HBM), pltpu.with_memory_space_constraint(land, pltpu.HBM), after)


def _reduce_w_out_wait(sems, slabs_thru, land_thru, after):
    def body(src_ref, land_ref, *refs):
        x, y, c = _mesh_pos()
        for k in range(1, 8):
            px, py, pc = x ^ ((k >> 2) & 1), y ^ ((k >> 1) & 1), c ^ (k & 1)
            cp = pltpu.make_async_remote_copy(
                src_ref=src_ref.at[2 * px + py], dst_ref=land_ref.at[4 * px + 2 * py + pc], send_sem=refs[k - 1],
                recv_sem=refs[6 + k], device_id=(px, py, pc), device_id_type=MESH)
            cp.wait_send()
            cp.wait_recv()

    return pl.pallas_call(
        body, name="reduce_w_out_wait",
        out_shape=(pltpu.HBM(slabs_thru.shape, slabs_thru.dtype), pltpu.HBM(land_thru.shape, land_thru.dtype)),
        in_specs=(_HBM, _HBM) + (_SEM,) * 14 + (pl.BlockSpec(memory_space=pl.ANY),),
        out_specs=(_HBM, _HBM), input_output_aliases={0: 0, 1: 1},
        compiler_params=pltpu.CompilerParams(has_side_effects=_EFFECT),
    )(slabs_thru, land_thru, *sems, after)


def _pair_add(g, recv, core, name):
    _, rows, C = recv.shape

    def body(core_ref, g_ref, r_ref, o_ref):
        o_ref[...] = _bf(g_ref[...].astype(F32) + r_ref[...].astype(F32))

    spec = pl.BlockSpec((1, rows, C), lambda j, core: (j, 0, 0))
    return pl.pallas_call(
        body, name=name,
        grid_spec=pltpu.PrefetchScalarGridSpec(
            num_scalar_prefetch=1, grid=(N_CHIPS,),
            in_specs=[pl.BlockSpec((1, rows, C), lambda j, core: (j, core[0], 0)), spec], out_specs=spec),
        out_shape=jax.ShapeDtypeStruct((N_CHIPS, rows, C), BF16),
        compiler_params=_cparams(("parallel",)),
    )(core, g, recv)


def _chip_add(own, parts, chip, name):
    _, rows, C = parts.shape
    tc = 512

    def body(chip_ref, own_ref, r0, r1, r2, r3, o_ref):
        acc = None
        for j, r in enumerate((r0, r1, r2, r3)):
            term = jnp.where(chip_ref[0] == j, own_ref[0], r[0]).astype(F32)
            acc = term if acc is None else acc + term
        o_ref[...] = acc

    def slab(j):
        return pl.BlockSpec((1, rows, tc), lambda i, chip: (jnp.where(chip[0] == j, (j + 1) % N_CHIPS, j), 0, i))

    return pl.pallas_call(
        body, name=name,
        grid_spec=pltpu.PrefetchScalarGridSpec(
            num_scalar_prefetch=1, grid=(C // tc,),
            in_specs=[pl.BlockSpec((1, rows, tc), lambda i, chip: (chip[0], 0, i))] + [slab(j) for j in range(N_CHIPS)],
            out_specs=pl.BlockSpec((rows, tc), lambda i, chip: (0, i))),
        out_shape=jax.ShapeDtypeStruct((rows, C), F32),
        compiler_params=_cparams(("parallel",)),
    )(chip, own, parts, parts, parts, parts)


def _adamw_math(w, g, m, v):
    m = ADAM_B1 * m + (1.0 - ADAM_B1) * g
    v = ADAM_B2 * v + (1.0 - ADAM_B2) * (g * g)
    m_hat = m / (1.0 - ADAM_B1 ** ADAM_STEP)
    v_hat = v / (1.0 - ADAM_B2 ** ADAM_STEP)
    delta = -ADAM_LR * (m_hat / (jnp.sqrt(v_hat) + ADAM_EPS) + ADAM_WD * w)
    return delta, m, v


def _adamw_rows(w, g_own, g_sib, m, v, core, name):
    R, C = w.shape[0], w.shape[-1]
    rows = g_own.shape[0]
    step = 256
    chunks = [(r, min(step, R - r)) for r in range(0, R, step)]
    sub = 64

    def body(core_ref, w_hbm, go_hbm, gs_hbm, m_hbm, v_hbm, d_hbm, nm_hbm, nv_hbm, g_hbm,
             wbuf, mbuf, vbuf, gbuf, dbuf, nmbuf, nvbuf, in_sems, g_sems, out_sems):
        c = core_ref[0]
        flat = lambda ref: ref.at[:, 0, :]
        g_in = [pltpu.make_async_copy(go_hbm, gbuf.at[pl.ds(pl.multiple_of(c * rows, 8), rows), :], g_sems.at[0]),
                pltpu.make_async_copy(gs_hbm, gbuf.at[pl.ds(pl.multiple_of((1 - c) * rows, 8), rows), :], g_sems.at[1])]
        for cp in g_in:
            cp.start()
        loads = []
        for k, (r0, n) in enumerate(chunks):
            cps = [pltpu.make_async_copy(flat(src).at[pl.ds(r0, n), :], dst.at[pl.ds(r0, n), :], in_sems.at[a, k])
                   for a, (src, dst) in enumerate(((w_hbm, wbuf), (m_hbm, mbuf), (v_hbm, vbuf)))]
            for cp in cps:
                cp.start()
            loads.append(cps)
        for cp in g_in:
            cp.wait()
        stores = []
        for k, (r0, n) in enumerate(chunks):
            for cp in loads[k]:
                cp.wait()

            def update(rs):
                g = gbuf[rs, :]
                dl, nm, nv = _adamw_math(wbuf[rs, :], g, mbuf[rs, :], vbuf[rs, :])
                dbuf[rs, :] = dl
                nmbuf[rs, :] = nm
                nvbuf[rs, :] = nv

            if n % sub == 0:
                def block(i, carry, r0=r0):
                    update(pl.ds(pl.multiple_of(r0 + i * sub, 8), sub))
                    return carry
                lax.fori_loop(0, n // sub, block, 0)
            else:
                update(pl.ds(r0, n))
            cps = [pltpu.make_async_copy(src.at[pl.ds(r0, n), :], flat(dst).at[pl.ds(r0, n), :], out_sems.at[a, k])
                   for a, (src, dst) in enumerate(((dbuf, d_hbm), (nmbuf, nm_hbm), (nvbuf, nv_hbm), (gbuf, g_hbm)))]
            for cp in cps:
                cp.start()
            stores += cps
        for cp in stores:
            cp.wait()

    any_spec = pl.BlockSpec(memory_space=pl.ANY)
    dense = pltpu.VMEM((R, C), F32)
    return pl.pallas_call(
        body, name=name,
        grid_spec=pltpu.PrefetchScalarGridSpec(
            num_scalar_prefetch=1, grid=(1,),
            in_specs=[any_spec] * 5, out_specs=[any_spec] * 4,
            scratch_shapes=[dense, dense, dense, pltpu.VMEM((2 * rows, C), F32), dense, dense, dense,
                            pltpu.SemaphoreType.DMA((3, len(chunks))), pltpu.SemaphoreType.DMA((2,)),
                            pltpu.SemaphoreType.DMA((4, len(chunks)))]),
        out_shape=[jax.ShapeDtypeStruct(w.shape, F32)] * 4,
        compiler_params=_cparams(),
    )(core, w, g_own, g_sib, m, v)


def _adamw_sum8(w, slabs, land, m, v, ids, name):
    R, C = w.shape
    tr = 128

    def body(ids_ref, w_ref, own_ref, *refs):
        lrefs, (m_ref, v_ref, d_ref, nm_ref, nv_ref, g_ref) = refs[:8], refs[8:]
        g = None
        for d, l_ref in enumerate(lrefs):
            term = jnp.where(ids_ref[0] == d, own_ref[0], l_ref[0]).astype(F32)
            g = term if g is None else g + term
        dl, nm, nv = _adamw_math(w_ref[...], g, m_ref[...], v_ref[...])
        d_ref[...] = dl
        nm_ref[...] = nm
        nv_ref[...] = nv
        g_ref[...] = g

    def slot(d):
        return pl.BlockSpec((1, tr, C), lambda i, ids: (jnp.where(ids[0] == d, (d + 1) % 8, d), i, 0))

    spec = pl.BlockSpec((tr, C), lambda i, ids: (i, 0))
    return pl.pallas_call(
        body, name=name,
        grid_spec=pltpu.PrefetchScalarGridSpec(
            num_scalar_prefetch=1, grid=(R // tr,),
            in_specs=[spec, pl.BlockSpec((1, tr, C), lambda i, ids: (ids[1], i, 0))] + [slot(d) for d in range(8)]
            + [spec, spec],
            out_specs=[spec] * 4),
        out_shape=[jax.ShapeDtypeStruct((R, C), F32)] * 4,
        compiler_params=_cparams(("parallel",)),
    )(ids, w, slabs, *([land] * 8), m, v)


SMALL_NAMES = ("conv_b", "ssd_norm_w", "ln_g", "ln_b", "dt_bias", "a_log", "d_skip", "attn_sinks")
SMALL_FIELDS = ((4, 0, D_XBC), (5, 0, D_SSD), (6, 0, D_MODEL), (7, 0, D_MODEL), (5, 1024, SSD_HEADS), (5, 1152, SSD_HEADS),
                (5, 1280, SSD_HEADS), (5, 1408, ATT_QH))
LOSS_FIELD = (6, 1024, 128)
K_SMALL = D_XBC


def _pack_small(g_conv_w, vecs, loss):
    def body(cw_ref, *refs):
        o_ref = refs[-1]
        o_ref[...] = jnp.zeros_like(o_ref)
        o_ref[0:CONV_K, 0:D_XBC] = cw_ref[...]
        for v_ref, (row, off, n) in zip(refs[:-2], SMALL_FIELDS):
            o_ref[row:row + 1, off:off + n] = v_ref[...]
        o_ref[LOSS_FIELD[0]:LOSS_FIELD[0] + 1, LOSS_FIELD[1]:LOSS_FIELD[1] + LOSS_FIELD[2]] = refs[-2][...]

    return pl.pallas_call(
        body, name="pack_small", out_shape=jax.ShapeDtypeStruct((8, K_SMALL), F32), compiler_params=_cparams(),
    )(g_conv_w, *vecs, loss)


def _adamw_small(slots, chip, conv_w, m_conv_w, v_conv_w, params, moms, vars_):
    n_vec = len(SMALL_NAMES)

    def body(chip_ref, s_ref, *refs):
        ins = refs[:3 * (n_vec + 1)]
        outs = refs[3 * (n_vec + 1):-1]
        tot_ref = refs[-1]
        tot = s_ref[0]
        for d in range(1, 8):
            tot = tot + s_ref[d]
        outs[0][...] = tot[LOSS_FIELD[0]:LOSS_FIELD[0] + 1, LOSS_FIELD[1]:LOSS_FIELD[1] + 1]
        off = pl.multiple_of(chip_ref[0] * CONV_COLS, 128)
        tot_ref[...] = tot
        grads = [tot_ref[0:CONV_K, pl.ds(off, CONV_COLS)]]
        grads += [tot[row:row + 1, o:o + n] for row, o, n in SMALL_FIELDS]
        for k, g in enumerate(grads):
            w_ref, m_ref, v_ref = ins[3 * k:3 * k + 3]
            full = (0,) if k == 0 else (Ellipsis,)
            d, nm, nv = _adamw_math(w_ref[full], g, m_ref[full], v_ref[full])
            for o_ref, val in zip(outs[1 + 4 * k:5 + 4 * k], (g, d, nm, nv)):
                o_ref[full] = val

    args = [conv_w, m_conv_w, v_conv_w]
    for w, m, v in zip(params, moms, vars_):
        args += [w, m, v]
    shapes = [jax.ShapeDtypeStruct((1, 1), F32)] + [jax.ShapeDtypeStruct(conv_w.shape, F32)] * 4
    for w in params:
        shapes += [jax.ShapeDtypeStruct(w.shape, F32)] * 4
    vmem = pl.BlockSpec(memory_space=pltpu.VMEM)
    return pl.pallas_call(
        body, name="adamw_small",
        grid_spec=pltpu.PrefetchScalarGridSpec(
            num_scalar_prefetch=1, grid=(1,),
            in_specs=[pl.BlockSpec(slots.shape, lambda i, chip: (0, 0, 0))] + [vmem] * len(args),
            out_specs=[vmem] * len(shapes), scratch_shapes=[pltpu.VMEM((8, K_SMALL), F32)]),
        out_shape=shapes, compiler_params=_cparams(),
    )(chip, slots, *args)


def kernel(x, positions, w_in, conv_w, conv_b, dt_bias, a_log, d_skip, ssd_norm_w, attn_sinks, w_out, ln_g, ln_b, loss_target, m_w_in, m_conv_w, m_conv_b, m_dt_bias, m_a_log, m_d_skip, m_ssd_norm_w, m_attn_sinks, m_w_out, m_ln_g, m_ln_b, v_w_in, v_conv_w, v_conv_b, v_dt_bias, v_a_log, v_d_skip, v_ssd_norm_w, v_attn_sinks, v_w_out, v_ln_g, v_ln_b):
    mx, my, mc = _mesh_pos()
    chip = 2 * mx + my
    L = x.shape[1]

    conv_w_s8 = jnp.pad(conv_w[0], ((0, 8 - CONV_K), (0, 0)))
    pad_rows = ((0, SLAB_ROWS - W_IN_COLS), (0, 0))
    w_in_t = w_in[0].T
    w_in_b, w_out_b = jnp.pad(_bf(w_in_t), pad_rows), _bf(w_out[0])
    ag_in, ag_cw = _gather_weights(w_in_b, conv_w_s8)
    started = _gather_w_out_start(w_out_b, ag_cw)
    own = (jnp.arange(N_CHIPS) == chip)[:, None, None]

    def get_w_out(after):
        landed = _gather_w_out_wait(started[0:6], started[6], started[7], after)
        return jnp.where(own, w_out_b[None], landed).reshape(D_MIX, D_MODEL)

    w_full = jnp.concatenate([ag_in[j, 0:W_IN_COLS] for j in range(N_CHIPS)], axis=0)
    w = jnp.concatenate([
        w_full[O_Z:O_Z + D_SSD], w_full[O_G:O_G + D_ATT], w_full[O_Q:O_Q + D_ATT],
        w_full[O_XBC:O_XBC + D_XBC], w_full[O_K:O_K + 2 * D_KV], w_full[O_DT:O_DT + SSD_HEADS],
        jnp.zeros((DT_PAD - SSD_HEADS, D_MODEL), BF16)], axis=0)
    conv_w_full = jnp.concatenate([ag_cw[j, 0:CONV_K] for j in range(N_CHIPS)], axis=1)

    loss_part, gx_args, gw_in, w_out_red, small = _local_step(
        x[0], positions[0].reshape(L, 1), loss_target[0], w, get_w_out, started[8][0:1, :], conv_w_full,
        conv_b, dt_bias, a_log, d_skip, ssd_norm_w, attn_sinks, ln_g, ln_b)

    packed = _pack_small(small["conv_w"], [small[n] for n in SMALL_NAMES], loss_part)
    core_id = mc.reshape(1).astype(jnp.int32)
    chip_id = chip.reshape(1).astype(jnp.int32)
    ids = jnp.stack([4 * mx + 2 * my + mc, chip]).astype(jnp.int32)
    slabs = jnp.stack([jnp.pad(gw_in[W_IN_COLS * j:W_IN_COLS * (j + 1)], pad_rows) for j in range(N_CHIPS)])
    w_in_red = _pair_start(slabs, packed)
    own_slabs, landed = _reduce_w_out_wait(w_out_red[0:14], w_out_red[14], w_out_red[15], w_in_red[10])
    out_t = _adamw_sum8(w_out[0], own_slabs, landed, m_w_out[0], v_w_out[0], ids, "adamw_w_out")
    d_w_out, nm_w_out, nv_w_out, g_w_out = [a[None] for a in out_t]
    gw_in_slabs, recv_in = _pair_wait(w_in_red[0:8], w_in_red[8], w_in_red[9], out_t[0])
    s_in = _pair_add(gw_in_slabs, recv_in, core_id, "pair_add_in")
    chip_red = _chip_start(s_in, packed)
    grad_x = _grad_x(*gx_args, chip_red[8])
    s_in, r_in = _chip_wait(chip_red[0:6], chip_red[6], chip_red[7], grad_x)
    h_in = _chip_add(s_in, r_in, chip_id, "chip_add_in")
    sib_in, slots = _pair_share(h_in, packed)

    to_rows = lambda a: jnp.transpose(a, (2, 0, 1))
    in_t = _adamw_rows(to_rows(w_in), h_in, sib_in, to_rows(m_w_in), to_rows(v_w_in), core_id, "adamw_w_in")
    d_w_in, nm_w_in, nv_w_in, g_w_in = [jnp.transpose(a, (1, 2, 0)) for a in in_t]

    params = dict(conv_b=conv_b, ssd_norm_w=ssd_norm_w, ln_g=ln_g, ln_b=ln_b, dt_bias=dt_bias, a_log=a_log,
                  d_skip=d_skip, attn_sinks=attn_sinks)
    moms = dict(conv_b=m_conv_b, ssd_norm_w=m_ssd_norm_w, ln_g=m_ln_g, ln_b=m_ln_b, dt_bias=m_dt_bias, a_log=m_a_log,
                d_skip=m_d_skip, attn_sinks=m_attn_sinks)
    vars_ = dict(conv_b=v_conv_b, ssd_norm_w=v_ssd_norm_w, ln_g=v_ln_g, ln_b=v_ln_b, dt_bias=v_dt_bias, a_log=v_a_log,
                 d_skip=v_d_skip, attn_sinks=v_attn_sinks)
    res = _adamw_small(slots, chip_id, conv_w, m_conv_w, v_conv_w, [params[n] for n in SMALL_NAMES],
                       [moms[n] for n in SMALL_NAMES], [vars_[n] for n in SMALL_NAMES])
    loss = res[0][0, 0]
    grads, delta, new_m, new_v = {}, {}, {}, {}
    for k, n in enumerate(("conv_w",) + SMALL_NAMES):
        grads[n], delta[n], new_m[n], new_v[n] = res[1 + 4 * k:5 + 4 * k]
    for dd, a_in, a_out in ((grads, g_w_in, g_w_out), (delta, d_w_in, d_w_out), (new_m, nm_w_in, nm_w_out),
                            (new_v, nv_w_in, nv_w_out)):
        dd["w_in"] = a_in
        dd["w_out"] = a_out
    order = ("w_in", "conv_w", "conv_b", "dt_bias", "a_log", "d_skip", "ssd_norm_w", "attn_sinks", "w_out", "ln_g", "ln_b")
    return (loss, grad_x[None], *[grads[n] for n in order], *[delta[n] for n in order], *[new_m[n] for n in order],
            *[new_v[n] for n in order])
```

```python
import numpy as np
import jax
import jax.numpy as jnp
from jax import lax
from jax.experimental import pallas as pl
from jax.experimental.pallas import tpu as pltpu

F32 = jnp.float32
BF16 = jnp.bfloat16
MESH = pl.DeviceIdType.MESH

D_MODEL = 1024
D_SSD = 1024
D_ATT = 1024
D_MIX = 2048
SSD_HEADS = 16
SSD_P = 64
SSD_GROUPS = 2
SSD_R = 8
SSD_N = 128
D_BC = 256
D_XBC = 1536
CONV_K = 4
CHUNK = 128
ATT_HD = 64
assert ATT_HD in (4, 16, 64, 256)
ATT_QH = 16
ATT_KVH = 4
ATT_R = 4
D_KV = 256
WINDOW = 128
ROPE_THETA = 500000.0
ROPE_DIM = 16
ALPHA = 2.0 ** 0.25
LN_EPS = 1e-5
RMS_EPS = 1e-5
D_IN_PROJ = 5136
O_Z, O_XBC, O_DT, O_Q, O_K, O_V, O_G = 0, 1024, 2560, 2576, 3600, 3856, 4112
P_Z, P_G, P_Q, P_XBC, P_KV, P_DT, P_END = 0, 1024, 2048, 3072, 4608, 5120, 5248
DT_PAD = 128
N_CHIPS = 4
W_IN_COLS = D_IN_PROJ // N_CHIPS
SLAB_ROWS = 1312
W_OUT_ROWS = D_MIX // N_CHIPS
CONV_COLS = D_XBC // N_CHIPS

ADAM_LR = 0.001
ADAM_B1 = 0.9
ADAM_B2 = 0.999
ADAM_EPS = 1e-08
ADAM_WD = 0.01
ADAM_STEP = 10

VMEM_LIMIT = 56 * 1024 * 1024
ROW_TILE = 512
NEG_BIG = -1e30


def _cparams(sem=None, **kw):
    if sem is not None:
        kw["dimension_semantics"] = sem
    return pltpu.CompilerParams(vmem_limit_bytes=VMEM_LIMIT, **kw)


def _dot(a, b):
    return jnp.dot(a, b, preferred_element_type=F32)


def _dot_nt(a, b):
    return lax.dot_general(a, b, (((1,), (1,)), ((), ())), preferred_element_type=F32)


def _dot_tn(a, b):
    return lax.dot_general(a, b, (((0,), (0,)), ((), ())), preferred_element_type=F32)


def _bf(a):
    return a.astype(BF16)


def _iota2(shape, dim):
    return lax.broadcasted_iota(jnp.int32, shape, dim)


def _three_terms(x):
    hi = _bf(x)
    r = x - hi.astype(F32)
    mid = _bf(r)
    return hi, mid, _bf(r - mid.astype(F32))


def _dot01(m, a):
    return sum(_dot(m, t) for t in _three_terms(a))


def _to_rows(col):
    k = col.shape[1]
    eye = (_iota2((k, k), 0) == _iota2((k, k), 1)).astype(BF16)
    return sum(_dot_nt(eye, t) for t in _three_terms(col))


def _to_cols(row):
    n = row.shape[1]
    eye = (_iota2((n, n), 0) == _iota2((n, n), 1)).astype(BF16)
    return sum(_dot_nt(eye, t) for t in _three_terms(row))


def _sigmoid(x):
    return jax.nn.sigmoid(x)


def _in_proj(x, w, pos, inv):
    L = x.shape[0]
    tm = ROW_TILE
    widths = (D_SSD, D_ATT, D_ATT, D_XBC, 2 * D_KV, DT_PAD)

    def body(x_ref, w_ref, pos_ref, inv_ref, z_ref, g_ref, q_ref, xbc_ref, kv_ref, dt_ref, xb_ref):
        xb = _bf(x_ref[...])
        xb_ref[...] = xb
        tabs = _rope_tables(pos_ref, inv_ref)
        q_ref[...] = _bf(_rope(_dot_nt(xb, w_ref[P_Q:P_Q + D_ATT, :]), tabs))
        kv_ref[:, 0:D_KV] = _bf(_rope(_dot_nt(xb, w_ref[P_KV:P_KV + D_KV, :]), tabs))
        kv_ref[:, D_KV:2 * D_KV] = _bf(_dot_nt(xb, w_ref[P_KV + D_KV:P_KV + 2 * D_KV, :]))
        for o_ref, off, wd in zip((z_ref, g_ref, xbc_ref, dt_ref), (P_Z, P_G, P_XBC, P_DT), (D_SSD, D_ATT, D_XBC, DT_PAD)):
            o_ref[...] = _dot_nt(xb, w_ref[off:off + wd, :])

    row = lambda wd: pl.BlockSpec((tm, wd), lambda i: (i, 0))
    return pl.pallas_call(
        body, name="in_proj", grid=(L // tm,),
        in_specs=[row(D_MODEL), pl.BlockSpec((P_END, D_MODEL), lambda i: (0, 0), pipeline_mode=pl.Buffered(1)), row(1),
                  pl.BlockSpec((1, 2 * ATT_HD), lambda i: (0, 0))],
        out_specs=[row(wd) for wd in widths] + [row(D_MODEL)],
        out_shape=[jax.ShapeDtypeStruct((L, wd), dt) for wd, dt in zip(widths, (F32, F32, BF16, F32, BF16, F32))]
        + [jax.ShapeDtypeStruct((L, D_MODEL), BF16)],
        compiler_params=_cparams(("parallel",)),
    )(x, w, pos, inv)


def _matmuls_tn(a_list, b, name):
    K, N = b.shape
    tk = min(K, 1024)
    n = len(a_list)

    def body(*refs):
        b_ref = refs[n]
        o_refs, acc_refs = refs[n + 1:2 * n + 1], refs[2 * n + 1:]

        @pl.when(pl.program_id(0) == 0)
        def _():
            for acc in acc_refs:
                acc[...] = jnp.zeros_like(acc)

        bb = _bf(b_ref[...])
        for a_ref, o_ref, acc in zip(refs[:n], o_refs, acc_refs):
            total = acc[...] + _dot_tn(_bf(a_ref[...]), bb)
            acc[...] = total
            o_ref[...] = _bf(total)

    return pl.pallas_call(
        body, name=name, grid=(K // tk,),
        in_specs=[pl.BlockSpec((tk, a.shape[1]), lambda k: (k, 0)) for a in a_list] + [pl.BlockSpec((tk, N), lambda k: (k, 0))],
        out_specs=[pl.BlockSpec((a.shape[1], N), lambda k: (0, 0)) for a in a_list],
        out_shape=[jax.ShapeDtypeStruct((a.shape[1], N), BF16) for a in a_list],
        scratch_shapes=[pltpu.VMEM((a.shape[1], N), F32) for a in a_list],
        compiler_params=_cparams(("arbitrary",)),
    )(*a_list, b)


def _grad_x(dr, dz, dg, dq, dxbc, dkv, ddt, w, after):
    L = dr.shape[0]
    tm = min(ROW_TILE, L)
    widths = (D_SSD, D_ATT, D_ATT, D_XBC, 2 * D_KV, DT_PAD)
    offs = (P_Z, P_G, P_Q, P_XBC, P_KV, P_DT)

    def body(dr_ref, dz_ref, dg_ref, dq_ref, dxbc_ref, dkv_ref, ddt_ref, w_ref, after_ref, o_ref):
        acc = ALPHA * dr_ref[...]
        for p_ref, off, wd in zip((dz_ref, dg_ref, dq_ref, dxbc_ref, dkv_ref, ddt_ref), offs, widths):
            acc = acc + _dot(_bf(p_ref[...]), w_ref[off:off + wd, :])
        o_ref[...] = acc

    row = lambda wd: pl.BlockSpec((tm, wd), lambda i: (i, 0))
    specs = ([row(D_MODEL)] + [row(wd) for wd in widths]
             + [pl.BlockSpec((P_END, D_MODEL), lambda i: (0, 0), pipeline_mode=pl.Buffered(1)),
                pl.BlockSpec((8, 128), lambda i: (0, 0))])
    return pl.pallas_call(
        body, name="grad_x", grid=(L // tm,),
        in_specs=specs, out_specs=row(D_MODEL),
        out_shape=jax.ShapeDtypeStruct((L, D_MODEL), F32),
        compiler_params=_cparams(("parallel",)),
    )(dr, dz, dg, dq, dxbc, dkv, ddt, w, after)


HALO = 16


def _shift_matrix(offsets):
    n = CHUNK + HALO
    m = np.zeros((len(offsets) * CHUNK, 2 * n), np.float32)
    for k, off in enumerate(offsets):
        t = np.arange(CHUNK)
        m[k * CHUNK + t, t + off] = 1.0
        m[k * CHUNK + t, n + t + off] = 1.0
    return jnp.asarray(m, BF16)


def _shifted_rows(first_part, second_part, smat_ref):
    h1, l1 = _hi_lo(first_part)
    h2, l2 = _hi_lo(second_part)
    sh = _dot(smat_ref[...], jnp.concatenate([h1, h2, l1, l2], axis=0))
    return sh[0:CHUNK], sh[CHUNK:2 * CHUNK], sh[2 * CHUNK:3 * CHUNK]


def _ssd_chunk_pre(first, xbc_ref, tail_ref, dt_ref, cw_ref, cb_ref, dtb_ref, alog_ref, smat_ref=None, ext=None):
    tail = jnp.where(first, 0.0, tail_ref[...])
    x = xbc_ref[...]
    if ext is None:
        taps = _shifted_rows(tail, x, smat_ref) + (x,)
    else:
        ext[0:HALO, :] = tail
        ext[HALO:HALO + CHUNK, :] = x
        taps = tuple(ext[pl.ds(HALO - (CONV_K - 1) + k, CHUNK), :] for k in range(CONV_K - 1)) + (x,)
    u = cb_ref[...] + cw_ref[0:1, :] * taps[0]
    for k in range(1, CONV_K):
        u = u + cw_ref[k:k + 1, :] * taps[k]
    sig = _sigmoid(u)
    xbc = u * sig
    dtraw = dt_ref[:, 0:SSD_HEADS] + dtb_ref[...]
    dt = jax.nn.softplus(dtraw)
    A = -jnp.exp(alog_ref[...])
    a = dt * A
    tril = (_iota2((CHUNK, CHUNK), 0) >= _iota2((CHUNK, CHUNK), 1)).astype(BF16)
    acs = _dot01(tril, a)
    acs_row = _to_rows(acs)
    return u, sig, xbc, dtraw, dt, A, acs, acs_row, taps


def _head_expander():
    return (_iota2((SSD_HEADS, D_SSD), 1) // SSD_P == _iota2((SSD_HEADS, D_SSD), 0)).astype(BF16)


def _hi_lo(x):
    hi = _bf(x)
    return hi, _bf(x - hi.astype(F32))


def _expand(v, e):
    hi, lo = _hi_lo(v)
    return _dot(hi, e) + _dot(lo, e)


def _headsum(t, e):
    m = t.shape[0]
    if m < 8:
        t = jnp.broadcast_to(t[0:1], (8, t.shape[1]))
    hi, lo = _hi_lo(t)
    return (_dot_nt(hi, e) + _dot_nt(lo, e))[0:m]


def _ssd_decays(dt, acs, dsk_ref, e):
    alast = acs[CHUNK - 1:CHUNK, :]
    stk = jnp.concatenate([dt, jnp.exp(acs), jnp.exp(alast - acs),
                           jnp.broadcast_to(jnp.exp(alast), (8, SSD_HEADS)),
                           jnp.broadcast_to(dsk_ref[...], (8, SSD_HEADS))], axis=0)
    ex = _expand(stk, e)
    return (ex[0:CHUNK], ex[CHUNK:2 * CHUNK], ex[2 * CHUNK:3 * CHUNK], ex[3 * CHUNK:3 * CHUNK + 1],
            ex[3 * CHUNK + 8:3 * CHUNK + 9])


def _ssd_fwd(z, xbc, dtp, conv_w, conv_b, dt_bias, a_log, d_skip, norm_w):
    L = z.shape[0]
    nc = L // CHUNK
    half = D_SSD // SSD_GROUPS

    def body(z_ref, xbc_ref, tail_ref, dt_ref, cw_ref, cb_ref, dtb_ref, alog_ref, dsk_ref, nw_ref,
             y_ref, ypre_ref, prev_ref, state, ybuf, mbuf, ext):
        c = pl.program_id(0)

        @pl.when(c == 0)
        def _():
            state[...] = jnp.zeros_like(state)

        u, sig, xbcv, dtraw, dt, A, acs, acs_row, _ = _ssd_chunk_pre(
            c == 0, xbc_ref, tail_ref, dt_ref, cw_ref, cb_ref, dtb_ref, alog_ref, ext=ext)
        e = _head_expander()
        dtE, eacsE, dsdE, ealE, dskE = _ssd_decays(dt, acs, dsk_ref, e)
        xs = xbcv[:, 0:D_SSD]
        X = xs * dtE
        prev_ref[0] = state[...]
        causal = _iota2((CHUNK, CHUNK), 0) >= _iota2((CHUNK, CHUNK), 1)
        for g in range(SSD_GROUPS):
            gs = slice(half * g, half * (g + 1))
            Bg = _bf(xbcv[:, D_SSD + SSD_N * g:D_SSD + SSD_N * (g + 1)])
            Cg = _bf(xbcv[:, D_SSD + D_BC + SSD_N * g:D_SSD + D_BC + SSD_N * (g + 1)])
            cb = _dot_nt(Cg, Bg)
            for r in range(SSD_R):
                h = g * SSD_R + r
                seg = acs[:, h:h + 1] - acs_row[h:h + 1, :]
                mbuf[h] = _bf(cb * jnp.where(causal, jnp.exp(jnp.where(causal, seg, 0.0)), 0.0))
            st = state[:, gs]
            ybuf[:, gs] = _dot(Cg, _bf(st)) * eacsE[:, gs] + dskE[:, gs] * xs[:, gs]
            state[:, gs] = st * ealE[:, gs] + _dot_tn(Bg, _bf(X[:, gs] * dsdE[:, gs]))
        Xb = _bf(X)
        for h in range(SSD_HEADS):
            hs = slice(SSD_P * h, SSD_P * (h + 1))
            ybuf[:, hs] += _dot(mbuf[h], Xb[:, hs])
        y = ybuf[...]
        ypre_ref[...] = y
        zv = z_ref[...]
        yf = y * (zv * _sigmoid(zv))
        for g in range(SSD_GROUPS):
            gs = slice(half * g, half * (g + 1))
            yg = yf[:, gs]
            ms = jnp.mean(yg * yg, axis=-1, keepdims=True)
            y_ref[:, gs] = _bf(yg * lax.rsqrt(ms + RMS_EPS) * nw_ref[:, gs])

    full = lambda shape: pl.BlockSpec(shape, lambda c: (0, 0))
    return pl.pallas_call(
        body, name="ssd_fwd", grid=(nc,),
        in_specs=[
            pl.BlockSpec((CHUNK, D_SSD), lambda c: (c, 0)),
            pl.BlockSpec((CHUNK, D_XBC), lambda c: (c, 0)),
            pl.BlockSpec((HALO, D_XBC), lambda c: (jnp.maximum(c * (CHUNK // HALO) - 1, 0), 0)),
            pl.BlockSpec((CHUNK, DT_PAD), lambda c: (c, 0)),
            full((CONV_K, D_XBC)), full((1, D_XBC)), full((1, SSD_HEADS)), full((1, SSD_HEADS)), full((1, SSD_HEADS)),
            full((1, D_SSD)),
        ],
        out_specs=[
            pl.BlockSpec((CHUNK, D_SSD), lambda c: (c, 0)),
            pl.BlockSpec((CHUNK, D_SSD), lambda c: (c, 0)),
            pl.BlockSpec((1, SSD_N, D_SSD), lambda c: (c, 0, 0)),
        ],
        out_shape=[
            jax.ShapeDtypeStruct((L, D_SSD), BF16),
            jax.ShapeDtypeStruct((L, D_SSD), F32),
            jax.ShapeDtypeStruct((nc, SSD_N, D_SSD), F32),
        ],
        scratch_shapes=[
            pltpu.VMEM((SSD_N, D_SSD), F32),
            pltpu.VMEM((CHUNK, D_SSD), F32),
            pltpu.VMEM((SSD_HEADS, CHUNK, CHUNK), BF16),
            pltpu.VMEM((CHUNK + HALO, D_XBC), F32),
        ],
        compiler_params=_cparams(("arbitrary",)),
    )(z, xbc, xbc, dtp, conv_w, conv_b, dt_bias, a_log, d_skip, norm_w)


def _ssd_bwd(dy, z, ypre, xbc, dtp, prev, conv_w, conv_b, dt_bias, a_log, d_skip, norm_w):
    L = z.shape[0]
    nc = L // CHUNK
    half = D_SSD // SSD_GROUPS

    def body(dy_ref, z_ref, ypre_ref, xbc_ref, tail_ref, dt_ref, prev_ref, cw_ref, cb_ref, dtb_ref, alog_ref, dsk_ref,
             nw_ref, smat_ref, smat2_ref, dz_ref, dxbc_ref, ddt_ref, gcw_ref, gcb_ref, gdtb_ref, galog_ref, gdsk_ref,
             gnw_ref, dstate, dhead, dpost, yobuf, bdbuf, lmbuf, dmbuf, cbbuf):
        i = pl.program_id(0)
        c = nc - 1 - i

        @pl.when(i == 0)
        def _():
            dstate[...] = jnp.zeros_like(dstate)
            dhead[...] = jnp.zeros_like(dhead)
            gcw_ref[...] = jnp.zeros_like(gcw_ref)
            gcb_ref[...] = jnp.zeros_like(gcb_ref)
            gdtb_ref[...] = jnp.zeros_like(gdtb_ref)
            galog_ref[...] = jnp.zeros_like(galog_ref)
            gdsk_ref[...] = jnp.zeros_like(gdsk_ref)
            gnw_ref[...] = jnp.zeros_like(gnw_ref)

        u, sig, xbcv, dtraw, dt, A, acs, acs_row, taps = _ssd_chunk_pre(
            c == 0, xbc_ref, tail_ref, dt_ref, cw_ref, cb_ref, dtb_ref, alog_ref, smat_ref)
        e = _head_expander()
        dtE, eacsE, dsdE, ealE, dskE = _ssd_decays(dt, acs, dsk_ref, e)
        alast = acs[CHUNK - 1:CHUNK, :]
        xs = xbcv[:, 0:D_SSD]
        X = xs * dtE
        Xb = _bf(X)

        zv = z_ref[...]
        ypre = ypre_ref[...]
        dyn = dy_ref[...]
        sz = _sigmoid(zv)
        silu_z = zv * sz
        yf = ypre * silu_z
        dyf_parts = []
        for g in range(SSD_GROUPS):
            gs = slice(half * g, half * (g + 1))
            yg = yf[:, gs]
            rstd = lax.rsqrt(jnp.mean(yg * yg, axis=-1, keepdims=True) + RMS_EPS)
            dout = dyn[:, gs]
            gnw_ref[:, gs] += jnp.sum(dout * yg * rstd, axis=0, keepdims=True)
            dyhat = dout * nw_ref[:, gs]
            dyf_parts.append(rstd * (dyhat - yg * (rstd * rstd) * jnp.mean(dyhat * yg, axis=-1, keepdims=True)))
        dyf = jnp.concatenate(dyf_parts, axis=1)
        dz_ref[...] = _bf(dyf * ypre * (sz * (1.0 + zv * (1.0 - sz))))
        dyp = dyf * silu_z
        dyb = _bf(dyp)
        G = dyp * eacsE

        causal = _iota2((CHUNK, CHUNK), 0) >= _iota2((CHUNK, CHUNK), 1)
        ST = prev_ref[0]
        dST = dstate[...]
        for g in range(SSD_GROUPS):
            gs = slice(half * g, half * (g + 1))
            bs = slice(D_SSD + SSD_N * g, D_SSD + SSD_N * (g + 1))
            cs = slice(D_SSD + D_BC + SSD_N * g, D_SSD + D_BC + SSD_N * (g + 1))
            Bg = _bf(xbcv[:, bs])
            Cg = _bf(xbcv[:, cs])
            Gb = _bf(G[:, gs])
            STb = _bf(ST[:, gs])
            dSTb = _bf(dST[:, gs])
            dstate[:, gs] = dST[:, gs] * ealE[:, gs] + _dot_tn(Cg, Gb)
            yobuf[:, gs] = _dot(Cg, STb) * eacsE[:, gs]
            bdbuf[:, gs] = _dot(Bg, dSTb)
            dpost[:, cs] = _dot_nt(Gb, STb)
            dpost[:, bs] = _dot_nt(_bf(X[:, gs] * dsdE[:, gs]), dSTb)
            cbbuf[g] = _dot_nt(Cg, Bg)
            for r in range(SSD_R):
                h = g * SSD_R + r
                seg = acs[:, h:h + 1] - acs_row[h:h + 1, :]
                lmbuf[h] = jnp.where(causal, jnp.exp(jnp.where(causal, seg, 0.0)), 0.0)
        for h in range(SSD_HEADS):
            hs = slice(SSD_P * h, SSD_P * (h + 1))
            Mb = _bf(cbbuf[h // SSD_R] * lmbuf[h])
            dmbuf[h] = _dot_nt(dyb[:, hs], Xb[:, hs])
            dpost[:, hs] = _dot_tn(Mb, dyb[:, hs])
        lane16 = _iota2((1, SSD_HEADS), 1)
        sub16 = _iota2((SSD_HEADS, 1), 0)
        dacs_col = jnp.zeros((CHUNK, SSD_HEADS), F32)
        dacs_row = jnp.zeros((SSD_HEADS, CHUNK), F32)
        for g in range(SSD_GROUPS):
            bs = slice(D_SSD + SSD_N * g, D_SSD + SSD_N * (g + 1))
            cs = slice(D_SSD + D_BC + SSD_N * g, D_SSD + D_BC + SSD_N * (g + 1))
            cb = cbbuf[g]
            dcb = jnp.zeros((CHUNK, CHUNK), F32)
            for r in range(SSD_R):
                h = g * SSD_R + r
                dM = dmbuf[h]
                Lm = lmbuf[h]
                dcb = dcb + dM * Lm
                dseg = dM * (cb * Lm)
                dacs_col = dacs_col + jnp.sum(dseg, axis=-1, keepdims=True) * (lane16 == h).astype(F32)
                dacs_row = dacs_row - jnp.sum(dseg, axis=0, keepdims=True) * (sub16 == h).astype(F32)
            dcbb = _bf(dcb)
            dpost[:, bs] += _dot_tn(dcbb, _bf(xbcv[:, cs]))
            dpost[:, cs] += _dot(dcbb, _bf(xbcv[:, bs]))

        BD = bdbuf[...]
        dX = dpost[:, 0:D_SSD] + dsdE * BD
        dsd = jnp.exp(alast - acs)
        T = _headsum(X * BD, e) * dsd
        dalast = jnp.sum(T, axis=0, keepdims=True) + _headsum(
            jnp.sum(dST * ST, axis=0, keepdims=True), e) * jnp.exp(alast)
        is_last = (_iota2((CHUNK, 1), 0) == CHUNK - 1).astype(F32)
        dacs = dacs_col + _to_cols(dacs_row) + _headsum(dyp * yobuf[...], e) - T + is_last * dalast
        triu = (_iota2((CHUNK, CHUNK), 0) <= _iota2((CHUNK, CHUNK), 1)).astype(BF16)
        da = _dot01(triu, dacs)
        ddt_tot = _headsum(dX * xs, e) + da * A
        galog_ref[...] += jnp.sum(da * dt, axis=0, keepdims=True) * A
        ddtraw = ddt_tot * _sigmoid(dtraw)
        gdtb_ref[...] += jnp.sum(ddtraw, axis=0, keepdims=True)
        gdsk_ref[...] += _headsum(jnp.sum(dyp * xs, axis=0, keepdims=True), e)
        ddt_ref[...] = jnp.zeros_like(ddt_ref)
        ddt_ref[:, 0:SSD_HEADS] = ddtraw
        dpost[:, 0:D_SSD] = dX * dtE + dskE * dyp

        dconv = dpost[...] * (sig * (1.0 + u * (1.0 - sig)))
        gcb_ref[...] += jnp.sum(dconv, axis=0, keepdims=True)
        for k in range(CONV_K):
            gcw_ref[k:k + 1, :] += jnp.sum(dconv * taps[k], axis=0, keepdims=True)
        later = _shifted_rows(dconv, dhead[...], smat2_ref)
        dx = cw_ref[CONV_K - 1:CONV_K, :] * dconv
        for k in range(CONV_K - 1):
            dx = dx + cw_ref[k:k + 1, :] * later[k]
        dxbc_ref[...] = _bf(dx)
        dhead[...] = dconv[0:HALO, :]

    full = lambda shape: pl.BlockSpec(shape, lambda i: (0, 0))
    rev = lambda wd: pl.BlockSpec((CHUNK, wd), lambda i: (nc - 1 - i, 0))
    return pl.pallas_call(
        body, name="ssd_bwd", grid=(nc,),
        in_specs=[
            rev(D_SSD), rev(D_SSD), rev(D_SSD), rev(D_XBC),
            pl.BlockSpec((HALO, D_XBC), lambda i: (jnp.maximum((nc - 1 - i) * (CHUNK // HALO) - 1, 0), 0)),
            rev(DT_PAD),
            pl.BlockSpec((1, SSD_N, D_SSD), lambda i: (nc - 1 - i, 0, 0)),
            full((CONV_K, D_XBC)), full((1, D_XBC)), full((1, SSD_HEADS)), full((1, SSD_HEADS)), full((1, SSD_HEADS)),
            full((1, D_SSD)), full((3 * CHUNK, 2 * (CHUNK + HALO))), full((3 * CHUNK, 2 * (CHUNK + HALO))),
        ],
        out_specs=[
            rev(D_SSD), rev(D_XBC), rev(DT_PAD),
            full((CONV_K, D_XBC)), full((1, D_XBC)), full((1, SSD_HEADS)), full((1, SSD_HEADS)), full((1, SSD_HEADS)),
            full((1, D_SSD)),
        ],
        out_shape=[
            jax.ShapeDtypeStruct((L, D_SSD), BF16), jax.ShapeDtypeStruct((L, D_XBC), BF16),
            jax.ShapeDtypeStruct((L, DT_PAD), F32),
            jax.ShapeDtypeStruct((CONV_K, D_XBC), F32), jax.ShapeDtypeStruct((1, D_XBC), F32),
            jax.ShapeDtypeStruct((1, SSD_HEADS), F32), jax.ShapeDtypeStruct((1, SSD_HEADS), F32),
            jax.ShapeDtypeStruct((1, SSD_HEADS), F32), jax.ShapeDtypeStruct((1, D_SSD), F32),
        ],
        scratch_shapes=[
            pltpu.VMEM((SSD_N, D_SSD), F32),
            pltpu.VMEM((HALO, D_XBC), F32),
            pltpu.VMEM((CHUNK, D_XBC), F32),
            pltpu.VMEM((CHUNK, D_SSD), F32),
            pltpu.VMEM((CHUNK, D_SSD), F32),
            pltpu.VMEM((SSD_HEADS, CHUNK, CHUNK), F32),
            pltpu.VMEM((SSD_HEADS, CHUNK, CHUNK), F32),
            pltpu.VMEM((SSD_GROUPS, CHUNK, CHUNK), F32),
        ],
        compiler_params=_cparams(("arbitrary",)),
    )(dy, z, ypre, xbc, xbc, dtp, prev, conv_w, conv_b, dt_bias, a_log, d_skip, norm_w, _shift_matrix((13, 14, 15)),
      _shift_matrix((3, 2, 1)))


def _rope_tables(pos_ref, inv_ref):
    ang = pos_ref[...].astype(F32) * inv_ref[...]
    d = _iota2((1, 2 * ATT_HD), 1) % ATT_HD
    s = jnp.sin(ang)
    return jnp.cos(ang), jnp.where(d < ROPE_DIM // 2, -s, 0.0), jnp.where((d >= ROPE_DIM // 2) & (d < ROPE_DIM), s, 0.0)


def _rope(t, tabs):
    c, s1, s2 = tabs
    n = t.shape[1]
    rep = n // c.shape[1]
    return (t * jnp.tile(c, (1, rep)) + pltpu.roll(t, n - ROPE_DIM // 2, 1) * jnp.tile(s1, (1, rep))
            + pltpu.roll(t, ROPE_DIM // 2, 1) * jnp.tile(s2, (1, rep)))


def _rope_t(t, tabs):
    c, s1, s2 = tabs
    n = t.shape[1]
    rep = n // c.shape[1]
    return (t * jnp.tile(c, (1, rep)) + pltpu.roll(t * jnp.tile(s1, (1, rep)), ROPE_DIM // 2, 1)
            + pltpu.roll(t * jnp.tile(s2, (1, rep)), n - ROPE_DIM // 2, 1))


def _stack_heads(t, j):
    return jnp.concatenate([t[:, ATT_HD * (j * ATT_R + r):ATT_HD * (j * ATT_R + r + 1)] for r in range(ATT_R)], axis=0)


def _swa_mask_t(first):
    si = _iota2((2 * WINDOW, ATT_R * WINDOW), 0)
    qi = _iota2((2 * WINDOW, ATT_R * WINDOW), 1) % WINDOW
    band = (si > qi) & (si <= qi + WINDOW)
    return band & (jnp.logical_not(first) | (si >= WINDOW))


def _head_rows(ref, j):
    if ref.shape[0] == 1:
        parts = [jnp.broadcast_to(ref[:, j * ATT_R + r:j * ATT_R + r + 1], (1, WINDOW)) for r in range(ATT_R)]
    else:
        parts = [ref[j * ATT_R + r:j * ATT_R + r + 1, :] for r in range(ATT_R)]
    return jnp.concatenate(parts, axis=1)


def _swa_fwd(q, g, kv, sinks):
    L = q.shape[0]
    nb = L // WINDOW
    scale = ATT_HD ** -0.5

    def body(q_ref, g_ref, kvc_ref, kvp_ref, sink_ref, y_ref, o_ref, lse_ref, otbuf):
        n = pl.program_id(0)
        kk = jnp.concatenate([kvp_ref[:, 0:D_KV], kvc_ref[:, 0:D_KV]], axis=0) * scale
        vv = jnp.concatenate([kvp_ref[:, D_KV:2 * D_KV], kvc_ref[:, D_KV:2 * D_KV]], axis=0)
        valid = _swa_mask_t(n == 0)
        qv = q_ref[...]
        for j in range(ATT_KVH):
            js = slice(ATT_HD * j, ATT_HD * (j + 1))
            st = _dot_nt(kk[:, js], _stack_heads(qv, j))
            st = jnp.where(valid, st, NEG_BIG)
            sink = _head_rows(sink_ref, j)
            m = jnp.maximum(jnp.max(st, axis=0, keepdims=True), sink)
            p = jnp.exp(st - m)
            vx = jnp.concatenate([vv[:, js], jnp.ones((2 * WINDOW, ATT_HD), BF16)], axis=1)
            otx = _dot_tn(vx, _bf(p))
            denom = otx[ATT_HD:ATT_HD + 1] + jnp.exp(sink - m)
            ot = otx[0:ATT_HD] * (1.0 / denom)
            lse = m + jnp.log(denom)
            for r in range(ATT_R):
                h = j * ATT_R + r
                otbuf[ATT_HD * h:ATT_HD * (h + 1), :] = ot[:, WINDOW * r:WINDOW * (r + 1)]
                lse_ref[h:h + 1, :] = lse[:, WINDOW * r:WINDOW * (r + 1)]
        o = otbuf[...].T
        o_ref[...] = o
        gv = g_ref[...]
        y_ref[...] = _bf(o * (gv * _sigmoid(gv)))

    cur = lambda wd: pl.BlockSpec((WINDOW, wd), lambda n: (n, 0))
    prv = lambda wd: pl.BlockSpec((WINDOW, wd), lambda n: (jnp.maximum(n - 1, 0), 0))
    return pl.pallas_call(
        body, name="swa_fwd", grid=(nb,),
        in_specs=[cur(D_ATT), cur(D_ATT), cur(2 * D_KV), prv(2 * D_KV), pl.BlockSpec((1, ATT_QH), lambda n: (0, 0))],
        out_specs=[cur(D_ATT), cur(D_ATT), pl.BlockSpec((ATT_QH, WINDOW), lambda n: (0, n))],
        out_shape=[jax.ShapeDtypeStruct((L, D_ATT), BF16), jax.ShapeDtypeStruct((L, D_ATT), F32),
                   jax.ShapeDtypeStruct((ATT_QH, L), F32)],
        scratch_shapes=[pltpu.VMEM((D_ATT, WINDOW), F32)],
        compiler_params=_cparams(("parallel",)),
    )(q, g, kv, kv, sinks)


def _swa_bwd(dy, q, g, kv, o, lse, pos, inv, sinks):
    L = q.shape[0]
    nb = L // WINDOW
    scale = ATT_HD ** -0.5

    def body(dy_ref, q_ref, g_ref, kvc_ref, kvp_ref, o_ref, lse_ref, posc_ref, posp_ref, inv_ref, sink_ref,
             dq_ref, dg_ref, dkv_ref, dsink_ref, carry, dqbuf, dkbuf, dvbuf):
        n = pl.program_id(0)

        @pl.when(n == 0)
        def _():
            dsink_ref[...] = jnp.zeros_like(dsink_ref)

        @pl.when(n < nb)
        def _():
            tc = _rope_tables(posc_ref, inv_ref)
            tp = _rope_tables(posp_ref, inv_ref)
            kk = jnp.concatenate([kvp_ref[:, 0:D_KV], kvc_ref[:, 0:D_KV]], axis=0) * scale
            vv = jnp.concatenate([kvp_ref[:, D_KV:2 * D_KV], kvc_ref[:, D_KV:2 * D_KV]], axis=0)
            valid = _swa_mask_t(n == 0)
            qv = q_ref[...]
            gv = g_ref[...]
            sg = _sigmoid(gv)
            dyv = dy_ref[...]
            ov = o_ref[...]
            dg_ref[...] = _bf(dyv * ov * (sg * (1.0 + gv * (1.0 - sg))))
            do = dyv * (gv * sg)
            dod = do * ov
            ones = jnp.ones((8, ATT_HD), BF16)
            lane16 = _iota2((1, ATT_QH), 1)
            dsink = jnp.zeros((1, ATT_QH), F32)
            for j in range(ATT_KVH):
                js = slice(ATT_HD * j, ATT_HD * (j + 1))
                kj = kk[:, js]
                vj = vv[:, js]
                qs = _stack_heads(qv, j)
                dos = _bf(_stack_heads(do, j))
                hi, lo = _hi_lo(_stack_heads(dod, j))
                delta = (_dot_nt(ones, hi) + _dot_nt(ones, lo))[0:1]
                lse = _head_rows(lse_ref, j)
                st = _dot_nt(kj, qs)
                pt = jnp.exp(jnp.where(valid, st, NEG_BIG) - lse)
                dst = _bf(pt * (_dot_nt(vj, dos) - delta))
                dqt = _dot_tn(kj, dst)
                dkbuf[:, js] = _dot(dst, qs) * scale
                dvbuf[:, js] = _dot(_bf(pt), dos)
                sd = jnp.exp(_head_rows(sink_ref, j) - lse) * delta
                for r in range(ATT_R):
                    h = j * ATT_R + r
                    ls = slice(WINDOW * r, WINDOW * (r + 1))
                    dqbuf[ATT_HD * h:ATT_HD * (h + 1), :] = dqt[:, ls]
                    dsink = dsink - jnp.sum(sd[:, ls], axis=1, keepdims=True) * (lane16 == h).astype(F32)
            dsink_ref[...] += dsink
            dq_ref[...] = _bf(_rope_t(dqbuf[...].T, tc))
            dkp = _rope_t(dkbuf[0:WINDOW, :], tp)
            dkc = _rope_t(dkbuf[WINDOW:2 * WINDOW, :], tc)

            @pl.when(n > 0)
            def _():
                dkv_ref[:, 0:D_KV] = _bf(carry[:, 0:D_KV] + dkp)
                dkv_ref[:, D_KV:2 * D_KV] = _bf(carry[:, D_KV:2 * D_KV] + dvbuf[0:WINDOW, :])

            carry[:, 0:D_KV] = dkc
            carry[:, D_KV:2 * D_KV] = dvbuf[WINDOW:2 * WINDOW, :]

        @pl.when(n == nb)
        def _():
            dkv_ref[...] = _bf(carry[...])

    last = nb - 1
    cur = lambda wd: pl.BlockSpec((WINDOW, wd), lambda n: (jnp.minimum(n, last), 0))
    prv = lambda wd: pl.BlockSpec((WINDOW, wd), lambda n: (jnp.maximum(jnp.minimum(n, last) - 1, 0), 0))
    return pl.pallas_call(
        body, name="swa_bwd", grid=(nb + 1,),
        in_specs=[cur(D_ATT), cur(D_ATT), cur(D_ATT), cur(2 * D_KV), prv(2 * D_KV), cur(D_ATT),
                  pl.BlockSpec((ATT_QH, WINDOW), lambda n: (0, jnp.minimum(n, last))), cur(1), prv(1),
                  pl.BlockSpec((1, 2 * ATT_HD), lambda n: (0, 0)), pl.BlockSpec((1, ATT_QH), lambda n: (0, 0))],
        out_specs=[cur(D_ATT), cur(D_ATT),
                   pl.BlockSpec((WINDOW, 2 * D_KV), lambda n: (jnp.maximum(n - 1, 0), 0)),
                   pl.BlockSpec((1, ATT_QH), lambda n: (0, 0))],
        out_shape=[jax.ShapeDtypeStruct((L, D_ATT), BF16), jax.ShapeDtypeStruct((L, D_ATT), BF16),
                   jax.ShapeDtypeStruct((L, 2 * D_KV), BF16), jax.ShapeDtypeStruct((1, ATT_QH), F32)],
        scratch_shapes=[pltpu.VMEM((WINDOW, 2 * D_KV), F32), pltpu.VMEM((D_ATT, WINDOW), F32),
                        pltpu.VMEM((2 * WINDOW, D_KV), F32), pltpu.VMEM((2 * WINDOW, D_KV), F32)],
        compiler_params=_cparams(("arbitrary",)),
    )(dy, q, g, kv, kv, o, lse, pos, pos, inv, sinks)


def _out_ln_loss(y_ssd, y_att, x, target, w_out, ln_g, ln_b):
    L = x.shape[0]
    tm = min(ROW_TILE, L)
    nt = L // tm
    inv_d = 1.0 / D_MODEL

    def body(ys_ref, ya_ref, x_ref, t_ref, w_ref, g_ref, b_ref, dr_ref, dys_ref, dya_ref, loss_ref, gg_ref, gb_ref,
             gwo_ref, acc_ref):
        i = pl.program_id(0)

        @pl.when(i == 0)
        def _():
            loss_ref[...] = jnp.zeros_like(loss_ref)
            gg_ref[...] = jnp.zeros_like(gg_ref)
            gb_ref[...] = jnp.zeros_like(gb_ref)
            acc_ref[...] = jnp.zeros_like(acc_ref)

        halves = [slice(0, tm // 2), slice(tm // 2, tm)]
        hs = [_dot(_bf(ys_ref[rs, :]), w_ref[0:D_SSD, :]) + _dot(_bf(ya_ref[rs, :]), w_ref[D_SSD:D_MIX, :]) for rs in halves]
        gam = g_ref[...]
        for rs, h in zip(halves, hs):
            r = ALPHA * x_ref[rs, :] + h
            mu = jnp.mean(r, axis=-1, keepdims=True)
            xc = r - mu
            rstd = lax.rsqrt(jnp.mean(xc * xc, axis=-1, keepdims=True) + LN_EPS)
            xhat = xc * rstd
            diff = xhat * gam + b_ref[...] - t_ref[rs, :]
            part = jnp.sum(jnp.sum(diff * diff, axis=-1, keepdims=True), axis=0, keepdims=True)
            loss_ref[...] += (0.5 * inv_d) * part
            dout = diff * inv_d
            gg_ref[...] += jnp.sum(dout * xhat, axis=0, keepdims=True)
            gb_ref[...] += jnp.sum(dout, axis=0, keepdims=True)
            dxh = dout * gam
            dr_ref[rs, :] = rstd * (dxh - jnp.mean(dxh, axis=-1, keepdims=True)
                                    - xhat * jnp.mean(dxh * xhat, axis=-1, keepdims=True))
        for rs in halves:
            drh = _bf(dr_ref[rs, :])
            dys_ref[rs, :] = _dot_nt(drh, w_ref[0:D_SSD, :])
            dya_ref[rs, :] = _dot_nt(drh, w_ref[D_SSD:D_MIX, :])
        drb = _bf(dr_ref[...])
        acc_ref[0:D_SSD, :] += _dot_tn(_bf(ys_ref[...]), drb)
        acc_ref[D_SSD:D_MIX, :] += _dot_tn(_bf(ya_ref[...]), drb)

        @pl.when(i == nt - 1)
        def _():
            gwo_ref[...] = _bf(acc_ref[...])

    row = pl.BlockSpec((tm, D_MODEL), lambda i: (i, 0))
    vec = pl.BlockSpec((1, D_MODEL), lambda i: (0, 0))
    return pl.pallas_call(
        body, name="out_ln_loss", grid=(nt,),
        in_specs=[row, row, row, row, pl.BlockSpec((D_MIX, D_MODEL), lambda i: (0, 0), pipeline_mode=pl.Buffered(1)), vec, vec],
        out_specs=[row, row, row, pl.BlockSpec((1, 128), lambda i: (0, 0)), vec, vec,
                   pl.BlockSpec((D_MIX, D_MODEL), lambda i: (0, 0))],
        out_shape=[jax.ShapeDtypeStruct((L, D_MODEL), F32)] * 3 + [jax.ShapeDtypeStruct((1, 128), F32)]
        + [jax.ShapeDtypeStruct((1, D_MODEL), F32)] * 2 + [jax.ShapeDtypeStruct((D_MIX, D_MODEL), BF16)],
        scratch_shapes=[pltpu.VMEM((D_MIX, D_MODEL), F32)],
        compiler_params=_cparams(("arbitrary",)),
    )(y_ssd, y_att, x, target, w_out, ln_g, ln_b)


def _local_step(x, pos, target, w, get_w_out, token, conv_w, conv_b, dt_bias, a_log, d_skip, norm_w, sinks, ln_g, ln_b):
    inv8 = ROPE_THETA ** (-jnp.arange(0, ROPE_DIM, 2, dtype=F32) / ROPE_DIM)
    inv = jnp.tile(jnp.concatenate([inv8, inv8, jnp.zeros((ATT_HD - ROPE_DIM,), F32)]), 2).reshape(1, 2 * ATT_HD)
    inv = inv + token

    z, g, q, xbc, kv, dtp, xb = _in_proj(x, w, pos, inv)
    y_ssd, y_pre, prev = _ssd_fwd(z, xbc, dtp, conv_w, conv_b, dt_bias, a_log, d_skip, norm_w)
    y_att, o, lse = _swa_fwd(q, g, kv, sinks)
    w_out = get_w_out(lse)
    dr, dy_ssd, dy_att, loss, g_ln_g, g_ln_b, gw_out = _out_ln_loss(y_ssd, y_att, x, target, w_out, ln_g, ln_b)
    w_out_red = _reduce_w_out_start(gw_out.reshape(N_CHIPS, W_OUT_ROWS, D_MODEL), loss)
    inv = inv + w_out_red[16][0:1, :]
    dq, dg, dkv, g_sinks = _swa_bwd(dy_att, q, g, kv, o, lse, pos, inv, sinks)
    dz, dxbc, ddt, g_conv_w, g_conv_b, g_dt_bias, g_a_log, g_d_skip, g_norm_w = _ssd_bwd(
        dy_ssd, z, y_pre, xbc, dtp, prev, conv_w, conv_b, dt_bias, a_log, d_skip, norm_w)
    gw_z, gw_g, gw_q = _matmuls_tn([dz, dg, dq], xb, "gw_zgq")
    gw_xbc, gw_kv, gw_dt = _matmuls_tn([dxbc, dkv, ddt], xb, "gw_xbc_kv_dt")
    gw_in = jnp.concatenate([gw_z, gw_xbc, gw_dt[0:SSD_HEADS], gw_q, gw_kv, gw_g], axis=0)
    small = dict(conv_w=g_conv_w, conv_b=g_conv_b, dt_bias=g_dt_bias, a_log=g_a_log, d_skip=g_d_skip,
                 ssd_norm_w=g_norm_w, attn_sinks=g_sinks, ln_g=g_ln_g, ln_b=g_ln_b)
    return loss, (dr, dz, dg, dq, dxbc, dkv, ddt, w), gw_in, w_out_red, small


def _mesh_pos():
    return lax.axis_index("x"), lax.axis_index("y"), lax.axis_index("c")


def _gather_weights(w_in_s, conv_w_s):
    hr = w_in_s.shape[0] // 2
    qa = 336
    quarters = ((0, qa), (qa, hr - qa))

    def body(win_ref, cw_ref, owin_ref, ocw_ref, stage, send_sems, recv_sems, small_send, small_recv, local_sems):
        x, y, c = _mesh_pos()
        me = 2 * x + y
        sibling = (x, y, 1 - c)
        xn, yn, dg = (1 - x, y), (x, 1 - y), (1 - x, 1 - y)
        chips = [xn, yn, dg]
        load = pltpu.make_async_copy(win_ref, stage, local_sems.at[1])
        load.start()
        locals_ = [pltpu.make_async_copy(cw_ref, ocw_ref.at[me], local_sems.at[0])]
        for cp in locals_:
            cp.start()
        started = []

        def piece(ref, chip, half, q):
            off, n = quarters[q]
            return ref.at[2 * chip[0] + chip[1]].at[pl.ds(half * hr + off, n), :]

        def mine(q):
            off, n = quarters[q]
            return win_ref.at[pl.ds(c * hr + off, n), :]

        def copy(src, dst, k, to):
            return pltpu.make_async_remote_copy(src_ref=src, dst_ref=dst, send_sem=send_sems.at[k], recv_sem=recv_sems.at[k],
                                                device_id=to, device_id_type=MESH)

        def go(cp):
            cp.start()
            started.append(cp)

        go(copy(mine(0), piece(owin_ref, (x, y), c, 0), 0, (*xn, c)))
        go(copy(mine(1), piece(owin_ref, (x, y), c, 1), 2, (*yn, c)))
        go(copy(mine(1), piece(owin_ref, (x, y), c, 1), 1, (*xn, c)))
        go(copy(mine(0), piece(owin_ref, (x, y), c, 0), 3, (*yn, c)))
        for j, (px, py) in enumerate(chips):
            cp = pltpu.make_async_remote_copy(
                src_ref=cw_ref, dst_ref=ocw_ref.at[me], send_sem=small_send.at[j], recv_sem=small_recv.at[j],
                device_id=(px, py, c), device_id_type=MESH)
            go(cp)
        load.wait()
        store = pltpu.make_async_copy(stage, owin_ref.at[me], local_sems.at[2])
        store.start()
        locals_.append(store)
        arrivals = [(0, xn, 0, (4, (*yn, c))), (2, yn, 1, (5, (*xn, c))), (1, xn, 1, None), (3, yn, 0, None),
                    (4, dg, 0, None), (5, dg, 1, None)]
        for n, (k, chip, q, onward) in enumerate(arrivals):
            blk = piece(owin_ref, chip, c, q)
            copy(blk, blk, k, sibling).wait_recv()
            if onward is not None:
                go(copy(blk, blk, onward[0], onward[1]))
            go(copy(blk, blk, 6 + n, sibling))
        for n, (k, chip, q, onward) in enumerate(arrivals):
            blk = piece(owin_ref, chip, 1 - c, q)
            copy(blk, blk, 6 + n, sibling).wait_recv()
        for j in range(3):
            pltpu.make_async_remote_copy(
                src_ref=cw_ref, dst_ref=ocw_ref.at[me], send_sem=small_send.at[j], recv_sem=small_recv.at[j],
                device_id=sibling, device_id_type=MESH).wait_recv()
        for cp in started:
            cp.wait_send()
        for cp in locals_:
            cp.wait()

    any_spec = pl.BlockSpec(memory_space=pl.ANY)
    return pl.pallas_call(
        body, name="gather_weights",
        in_specs=[any_spec] * 2, out_specs=[any_spec] * 2,
        out_shape=[jax.ShapeDtypeStruct((N_CHIPS,) + a.shape, a.dtype) for a in (w_in_s, conv_w_s)],
        scratch_shapes=[pltpu.VMEM(w_in_s.shape, w_in_s.dtype),
                        pltpu.SemaphoreType.DMA((12,)), pltpu.SemaphoreType.DMA((12,)),
                        pltpu.SemaphoreType.DMA((3,)), pltpu.SemaphoreType.DMA((3,)), pltpu.SemaphoreType.DMA((3,))],
    )(w_in_s, conv_w_s)


_HBM = pl.BlockSpec(memory_space=pltpu.HBM)
_SEM = pl.BlockSpec(memory_space=pltpu.SEMAPHORE)
_EFFECT = pltpu.SideEffectType.DATAFLOW_SIDE_EFFECTING


def _gather_w_out_start(w_out_s, after):
    def body(src_ref, land_ref, after_ref, s0, s1, s2, r0, r1, r2, own, src_thru, land_thru, token):
        x, y, c = _mesh_pos()
        me = 2 * x + y
        chips = [(1 - x, y), (x, 1 - y), (1 - x, 1 - y)]
        for (px, py), s, r in zip(chips, (s0, s1, s2), (r0, r1, r2)):
            pltpu.make_async_remote_copy(src_ref=src_ref, dst_ref=land_ref.at[me], send_sem=s, recv_sem=r,
                                         device_id=(px, py, c), device_id_type=MESH).start()
        pltpu.make_async_copy(src_ref, land_ref.at[me], own).start()
        token[...] = jnp.zeros_like(token)

    sem = pltpu.SemaphoreType.DMA(())
    land = lax.empty((N_CHIPS,) + w_out_s.shape, w_out_s.dtype)
    return pl.pallas_call(
        body, name="gather_w_out_start",
        out_shape=(sem,) * 7 + (pltpu.HBM(w_out_s.shape, w_out_s.dtype), pltpu.HBM(land.shape, land.dtype),
                                jax.ShapeDtypeStruct((8, 128), F32)),
        in_specs=(_HBM, _HBM, pl.BlockSpec(memory_space=pl.ANY)),
        out_specs=(_SEM,) * 7 + (_HBM, _HBM, pl.BlockSpec(memory_space=pltpu.VMEM)),
        input_output_aliases={0: 7, 1: 8},
        compiler_params=pltpu.CompilerParams(has_side_effects=_EFFECT),
    )(pltpu.with_memory_space_constraint(w_out_s, pltpu.HBM), pltpu.with_memory_space_constraint(land, pltpu.HBM), after)


def _gather_w_out_wait(sems, src_thru, land_thru, after):
    def body(src_ref, land_ref, s0, s1, s2, r0, r1, r2, own, after_ref, src_dead, got_ref):
        x, y, c = _mesh_pos()
        chips = [(1 - x, y), (x, 1 - y), (1 - x, 1 - y)]
        for (px, py), s, r in zip(chips, (s0, s1, s2), (r0, r1, r2)):
            cp = pltpu.make_async_remote_copy(src_ref=src_ref, dst_ref=land_ref.at[2 * px + py], send_sem=s, recv_sem=r,
                                              device_id=(px, py, c), device_id_type=MESH)
            cp.wait_send()
            cp.wait_recv()
        pltpu.make_async_copy(src_ref, land_ref.at[2 * x + y], own).wait()

    return pl.pallas_call(
        body, name="gather_w_out_wait",
        out_shape=(pltpu.HBM(src_thru.shape, src_thru.dtype), pltpu.HBM(land_thru.shape, land_thru.dtype)),
        in_specs=(_HBM, _HBM) + (_SEM,) * 7 + (pl.BlockSpec(memory_space=pl.ANY),),
        out_specs=(_HBM, _HBM), input_output_aliases={0: 0, 1: 1},
        compiler_params=pltpu.CompilerParams(has_side_effects=_EFFECT),
    )(src_thru, land_thru, *sems, after)[1]


def _pair_start(gw_in, after):
    hr = gw_in.shape[1] // 2

    def body(src_ref, land_ref, after_ref, *refs):
        x, y, c = _mesh_pos()
        for j in range(N_CHIPS):
            pltpu.make_async_remote_copy(
                src_ref=src_ref.at[j, pl.ds((1 - c) * hr, hr), :], dst_ref=land_ref.at[j], send_sem=refs[j],
                recv_sem=refs[N_CHIPS + j], device_id=(x, y, 1 - c), device_id_type=MESH).start()
        refs[10][...] = jnp.zeros_like(refs[10])

    sem = pltpu.SemaphoreType.DMA(())
    land = lax.empty((N_CHIPS, hr, D_MODEL), gw_in.dtype)
    return pl.pallas_call(
        body, name="pair_start",
        out_shape=(sem,) * 8 + (pltpu.HBM(gw_in.shape, gw_in.dtype), pltpu.HBM(land.shape, land.dtype),
                                jax.ShapeDtypeStruct((8, 128), F32)),
        in_specs=(_HBM, _HBM, pl.BlockSpec(memory_space=pl.ANY)),
        out_specs=(_SEM,) * 8 + (_HBM, _HBM, pl.BlockSpec(memory_space=pltpu.VMEM)),
        input_output_aliases={0: 8, 1: 9},
        compiler_params=pltpu.CompilerParams(has_side_effects=_EFFECT),
    )(pltpu.with_memory_space_constraint(gw_in, pltpu.HBM), pltpu.with_memory_space_constraint(land, pltpu.HBM), after)


def _pair_wait(sems, gw_thru, land_thru, after):
    hr = land_thru.shape[1]

    def body(src_ref, land_ref, *refs):
        x, y, c = _mesh_pos()
        for j in range(N_CHIPS):
            cp = pltpu.make_async_remote_copy(
                src_ref=src_ref.at[j, pl.ds((1 - c) * hr, hr), :], dst_ref=land_ref.at[j], send_sem=refs[j],
                recv_sem=refs[N_CHIPS + j], device_id=(x, y, 1 - c), device_id_type=MESH)
            cp.wait_send()
            cp.wait_recv()

    return pl.pallas_call(
        body, name="pair_wait",
        out_shape=(pltpu.HBM(gw_thru.shape, gw_thru.dtype), pltpu.HBM(land_thru.shape, land_thru.dtype)),
        in_specs=(_HBM, _HBM) + (_SEM,) * 8 + (pl.BlockSpec(memory_space=pl.ANY),),
        out_specs=(_HBM, _HBM), input_output_aliases={0: 0, 1: 1},
        compiler_params=pltpu.CompilerParams(has_side_effects=_EFFECT),
    )(gw_thru, land_thru, *sems, after)


def _chip_start(s_in, after):
    def body(src_ref, land_ref, after_ref, *refs):
        x, y, c = _mesh_pos()
        me = 2 * x + y
        for j, (px, py) in enumerate([(1 - x, y), (x, 1 - y), (1 - x, 1 - y)]):
            pltpu.make_async_remote_copy(
                src_ref=src_ref.at[2 * px + py], dst_ref=land_ref.at[me], send_sem=refs[j], recv_sem=refs[3 + j],
                device_id=(px, py, c), device_id_type=MESH).start()
        refs[8][...] = jnp.zeros_like(refs[8])

    sem = pltpu.SemaphoreType.DMA(())
    land = lax.empty(s_in.shape, s_in.dtype)
    return pl.pallas_call(
        body, name="chip_start",
        out_shape=(sem,) * 6 + (pltpu.HBM(s_in.shape, s_in.dtype), pltpu.HBM(land.shape, land.dtype),
                                jax.ShapeDtypeStruct((8, 128), F32)),
        in_specs=(_HBM, _HBM, pl.BlockSpec(memory_space=pl.ANY)),
        out_specs=(_SEM,) * 6 + (_HBM, _HBM, pl.BlockSpec(memory_space=pltpu.VMEM)),
        input_output_aliases={0: 6, 1: 7},
        compiler_params=pltpu.CompilerParams(has_side_effects=_EFFECT),
    )(pltpu.with_memory_space_constraint(s_in, pltpu.HBM), pltpu.with_memory_space_constraint(land, pltpu.HBM), after)


def _chip_wait(sems, s_thru, land_thru, after):
    def body(src_ref, land_ref, *refs):
        x, y, c = _mesh_pos()
        for j, (px, py) in enumerate([(1 - x, y), (x, 1 - y), (1 - x, 1 - y)]):
            cp = pltpu.make_async_remote_copy(
                src_ref=src_ref.at[2 * px + py], dst_ref=land_ref.at[2 * px + py], send_sem=refs[j], recv_sem=refs[3 + j],
                device_id=(px, py, c), device_id_type=MESH)
            cp.wait_send()
            cp.wait_recv()

    return pl.pallas_call(
        body, name="chip_wait",
        out_shape=(pltpu.HBM(s_thru.shape, s_thru.dtype), pltpu.HBM(land_thru.shape, land_thru.dtype)),
        in_specs=(_HBM, _HBM) + (_SEM,) * 6 + (pl.BlockSpec(memory_space=pl.ANY),),
        out_specs=(_HBM, _HBM), input_output_aliases={0: 0, 1: 1},
        compiler_params=pltpu.CompilerParams(has_side_effects=_EFFECT),
    )(s_thru, land_thru, *sems, after)


def _pair_share(h_in, small):
    def body(hin_ref, sm_ref, rin_ref, slots_ref, send_sems, recv_sems, small_send, small_recv, local_sem):
        x, y, c = _mesh_pos()
        dev = 4 * x + 2 * y + c
        mine = pltpu.make_async_copy(sm_ref, slots_ref.at[dev], local_sem)
        mine.start()
        share = pltpu.make_async_remote_copy(
            src_ref=hin_ref, dst_ref=rin_ref, send_sem=send_sems.at[0], recv_sem=recv_sems.at[0],
            device_id=(x, y, 1 - c), device_id_type=MESH)
        share.start()
        started = []
        for k in range(1, 8):
            peer = (x ^ ((k >> 2) & 1), y ^ ((k >> 1) & 1), c ^ (k & 1))
            cp = pltpu.make_async_remote_copy(
                src_ref=sm_ref, dst_ref=slots_ref.at[dev], send_sem=small_send.at[k - 1], recv_sem=small_recv.at[k - 1],
                device_id=peer, device_id_type=MESH)
            cp.start()
            started.append(cp)
        share.wait()
        for k in range(1, 8):
            pltpu.make_async_remote_copy(
                src_ref=sm_ref, dst_ref=slots_ref.at[dev], send_sem=small_send.at[k - 1], recv_sem=small_recv.at[k - 1],
                device_id=(x, y, 1 - c), device_id_type=MESH).wait_recv()
        for cp in started:
            cp.wait_send()
        mine.wait()

    any_spec = pl.BlockSpec(memory_space=pl.ANY)
    return pl.pallas_call(
        body, name="pair_share",
        in_specs=[any_spec] * 2, out_specs=[any_spec] * 2,
        out_shape=[jax.ShapeDtypeStruct(h_in.shape, F32), jax.ShapeDtypeStruct((8,) + small.shape, F32)],
        scratch_shapes=[pltpu.SemaphoreType.DMA((1,)), pltpu.SemaphoreType.DMA((1,)),
                        pltpu.SemaphoreType.DMA((7,)), pltpu.SemaphoreType.DMA((7,)), pltpu.SemaphoreType.DMA],
    )(h_in, small)


def _reduce_w_out_start(slabs, after):
    def body(src_ref, land_ref, after_ref, *refs):
        x, y, c = _mesh_pos()
        me = 4 * x + 2 * y + c
        for k in range(1, 8):
            px, py, pc = x ^ ((k >> 2) & 1), y ^ ((k >> 1) & 1), c ^ (k & 1)
            pltpu.make_async_remote_copy(src_ref=src_ref.at[2 * px + py], dst_ref=land_ref.at[me], send_sem=refs[k - 1],
                                         recv_sem=refs[6 + k], device_id=(px, py, pc), device_id_type=MESH).start()
        refs[16][...] = jnp.zeros_like(refs[16])

    sem = pltpu.SemaphoreType.DMA(())
    land = lax.empty((8,) + slabs.shape[1:], slabs.dtype)
    return pl.pallas_call(
        body, name="reduce_w_out_start",
        out_shape=(sem,) * 14 + (pltpu.HBM(slabs.shape, slabs.dtype), pltpu.HBM(land.shape, land.dtype),
                                 jax.ShapeDtypeStruct((8, 128), F32)),
        in_specs=(_HBM, _HBM, pl.BlockSpec(memory_space=pl.ANY)),
        out_specs=(_SEM,) * 14 + (_HBM, _HBM, pl.BlockSpec(memory_space=pltpu.VMEM)),
        input_output_aliases={0: 14, 1: 15},
        compiler_params=pltpu.CompilerParams(has_side_effects=_EFFECT),
    )(pltpu.with_memory_space_constraint(slabs, pltpu.HBM), pltpu.with_memory_space_constraint(land, pltpu.HBM), after)


def _reduce_w_out_wait(sems, slabs_thru, land_thru, after):
    def body(src_ref, land_ref, *refs):
        x, y, c = _mesh_pos()
        for k in range(1, 8):
            px, py, pc = x ^ ((k >> 2) & 1), y ^ ((k >> 1) & 1), c ^ (k & 1)
            cp = pltpu.make_async_remote_copy(
                src_ref=src_ref.at[2 * px + py], dst_ref=land_ref.at[4 * px + 2 * py + pc], send_sem=refs[k - 1],
                recv_sem=refs[6 + k], device_id=(px, py, pc), device_id_type=MESH)
            cp.wait_send()
            cp.wait_recv()

    return pl.pallas_call(
        body, name="reduce_w_out_wait",
        out_shape=(pltpu.HBM(slabs_thru.shape, slabs_thru.dtype), pltpu.HBM(land_thru.shape, land_thru.dtype)),
        in_specs=(_HBM, _HBM) + (_SEM,) * 14 + (pl.BlockSpec(memory_space=pl.ANY),),
        out_specs=(_HBM, _HBM), input_output_aliases={0: 0, 1: 1},
        compiler_params=pltpu.CompilerParams(has_side_effects=_EFFECT),
    )(slabs_thru, land_thru, *sems, after)


def _pair_add(g, recv, core, name):
    _, rows, C = recv.shape

    def body(core_ref, g_ref, r_ref, o_ref):
        o_ref[...] = _bf(g_ref[...].astype(F32) + r_ref[...].astype(F32))

    spec = pl.BlockSpec((1, rows, C), lambda j, core: (j, 0, 0))
    return pl.pallas_call(
        body, name=name,
        grid_spec=pltpu.PrefetchScalarGridSpec(
            num_scalar_prefetch=1, grid=(N_CHIPS,),
            in_specs=[pl.BlockSpec((1, rows, C), lambda j, core: (j, core[0], 0)), spec], out_specs=spec),
        out_shape=jax.ShapeDtypeStruct((N_CHIPS, rows, C), BF16),
        compiler_params=_cparams(("parallel",)),
    )(core, g, recv)


def _chip_add(own, parts, chip, name):
    _, rows, C = parts.shape
    tc = 512

    def body(chip_ref, own_ref, r0, r1, r2, r3, o_ref):
        acc = None
        for j, r in enumerate((r0, r1, r2, r3)):
            term = jnp.where(chip_ref[0] == j, own_ref[0], r[0]).astype(F32)
            acc = term if acc is None else acc + term
        o_ref[...] = acc

    def slab(j):
        return pl.BlockSpec((1, rows, tc), lambda i, chip: (jnp.where(chip[0] == j, (j + 1) % N_CHIPS, j), 0, i))

    return pl.pallas_call(
        body, name=name,
        grid_spec=pltpu.PrefetchScalarGridSpec(
            num_scalar_prefetch=1, grid=(C // tc,),
            in_specs=[pl.BlockSpec((1, rows, tc), lambda i, chip: (chip[0], 0, i))] + [slab(j) for j in range(N_CHIPS)],
            out_specs=pl.BlockSpec((rows, tc), lambda i, chip: (0, i))),
        out_shape=jax.ShapeDtypeStruct((rows, C), F32),
        compiler_params=_cparams(("parallel",)),
    )(chip, own, parts, parts, parts, parts)


def _adamw_math(w, g, m, v):
    m = ADAM_B1 * m + (1.0 - ADAM_B1) * g
    v = ADAM_B2 * v + (1.0 - ADAM_B2) * (g * g)
    m_hat = m / (1.0 - ADAM_B1 ** ADAM_STEP)
    v_hat = v / (1.0 - ADAM_B2 ** ADAM_STEP)
    delta = -ADAM_LR * (m_hat / (jnp.sqrt(v_hat) + ADAM_EPS) + ADAM_WD * w)
    return delta, m, v


def _adamw_rows(w, g_own, g_sib, m, v, core, name):
    R, C = w.shape[0], w.shape[-1]
    rows = g_own.shape[0]
    step = 256
    chunks = [(r, min(step, R - r)) for r in range(0, R, step)]
    sub = 64

    def body(core_ref, w_hbm, go_hbm, gs_hbm, m_hbm, v_hbm, d_hbm, nm_hbm, nv_hbm, g_hbm,
             wbuf, mbuf, vbuf, gbuf, dbuf, nmbuf, nvbuf, in_sems, g_sems, out_sems):
        c = core_ref[0]
        flat = lambda ref: ref.at[:, 0, :]
        g_in = [pltpu.make_async_copy(go_hbm, gbuf.at[pl.ds(pl.multiple_of(c * rows, 8), rows), :], g_sems.at[0]),
                pltpu.make_async_copy(gs_hbm, gbuf.at[pl.ds(pl.multiple_of((1 - c) * rows, 8), rows), :], g_sems.at[1])]
        for cp in g_in:
            cp.start()
        loads = []
        for k, (r0, n) in enumerate(chunks):
            cps = [pltpu.make_async_copy(flat(src).at[pl.ds(r0, n), :], dst.at[pl.ds(r0, n), :], in_sems.at[a, k])
                   for a, (src, dst) in enumerate(((w_hbm, wbuf), (m_hbm, mbuf), (v_hbm, vbuf)))]
            for cp in cps:
                cp.start()
            loads.append(cps)
        for cp in g_in:
            cp.wait()
        stores = []
        for k, (r0, n) in enumerate(chunks):
            for cp in loads[k]:
                cp.wait()

            def update(rs):
                g = gbuf[rs, :]
                dl, nm, nv = _adamw_math(wbuf[rs, :], g, mbuf[rs, :], vbuf[rs, :])
                dbuf[rs, :] = dl
                nmbuf[rs, :] = nm
                nvbuf[rs, :] = nv

            if n % sub == 0:
                def block(i, carry, r0=r0):
                    update(pl.ds(pl.multiple_of(r0 + i * sub, 8), sub))
                    return carry
                lax.fori_loop(0, n // sub, block, 0)
            else:
                update(pl.ds(r0, n))
            cps = [pltpu.make_async_copy(src.at[pl.ds(r0, n), :], flat(dst).at[pl.ds(r0, n), :], out_sems.at[a, k])
                   for a, (src, dst) in enumerate(((dbuf, d_hbm), (nmbuf, nm_hbm), (nvbuf, nv_hbm), (gbuf, g_hbm)))]
            for cp in cps:
                cp.start()
            stores += cps
        for cp in stores:
            cp.wait()

    any_spec = pl.BlockSpec(memory_space=pl.ANY)
    dense = pltpu.VMEM((R, C), F32)
    return pl.pallas_call(
        body, name=name,
        grid_spec=pltpu.PrefetchScalarGridSpec(
            num_scalar_prefetch=1, grid=(1,),
            in_specs=[any_spec] * 5, out_specs=[any_spec] * 4,
            scratch_shapes=[dense, dense, dense, pltpu.VMEM((2 * rows, C), F32), dense, dense, dense,
                            pltpu.SemaphoreType.DMA((3, len(chunks))), pltpu.SemaphoreType.DMA((2,)),
                            pltpu.SemaphoreType.DMA((4, len(chunks)))]),
        out_shape=[jax.ShapeDtypeStruct(w.shape, F32)] * 4,
        compiler_params=_cparams(),
    )(core, w, g_own, g_sib, m, v)


def _adamw_sum8(w, slabs, land, m, v, ids, name):
    R, C = w.shape
    tr = 128

    def body(ids_ref, w_ref, own_ref, *refs):
        lrefs, (m_ref, v_ref, d_ref, nm_ref, nv_ref, g_ref) = refs[:8], refs[8:]
        g = None
        for d, l_ref in enumerate(lrefs):
            term = jnp.where(ids_ref[0] == d, own_ref[0], l_ref[0]).astype(F32)
            g = term if g is None else g + term
        dl, nm, nv = _adamw_math(w_ref[...], g, m_ref[...], v_ref[...])
        d_ref[...] = dl
        nm_ref[...] = nm
        nv_ref[...] = nv
        g_ref[...] = g

    def slot(d):
        return pl.BlockSpec((1, tr, C), lambda i, ids: (jnp.where(ids[0] == d, (d + 1) % 8, d), i, 0))

    spec = pl.BlockSpec((tr, C), lambda i, ids: (i, 0))
    return pl.pallas_call(
        body, name=name,
        grid_spec=pltpu.PrefetchScalarGridSpec(
            num_scalar_prefetch=1, grid=(R // tr,),
            in_specs=[spec, pl.BlockSpec((1, tr, C), lambda i, ids: (ids[1], i, 0))] + [slot(d) for d in range(8)]
            + [spec, spec],
            out_specs=[spec] * 4),
        out_shape=[jax.ShapeDtypeStruct((R, C), F32)] * 4,
        compiler_params=_cparams(("parallel",)),
    )(ids, w, slabs, *([land] * 8), m, v)


SMALL_NAMES = ("conv_b", "ssd_norm_w", "ln_g", "ln_b", "dt_bias", "a_log", "d_skip", "attn_sinks")
SMALL_FIELDS = ((4, 0, D_XBC), (5, 0, D_SSD), (6, 0, D_MODEL), (7, 0, D_MODEL), (5, 1024, SSD_HEADS), (5, 1152, SSD_HEADS),
                (5, 1280, SSD_HEADS), (5, 1408, ATT_QH))
LOSS_FIELD = (6, 1024, 128)
K_SMALL = D_XBC


def _pack_small(g_conv_w, vecs, loss):
    def body(cw_ref, *refs):
        o_ref = refs[-1]
        o_ref[...] = jnp.zeros_like(o_ref)
        o_ref[0:CONV_K, 0:D_XBC] = cw_ref[...]
        for v_ref, (row, off, n) in zip(refs[:-2], SMALL_FIELDS):
            o_ref[row:row + 1, off:off + n] = v_ref[...]
        o_ref[LOSS_FIELD[0]:LOSS_FIELD[0] + 1, LOSS_FIELD[1]:LOSS_FIELD[1] + LOSS_FIELD[2]] = refs[-2][...]

    return pl.pallas_call(
        body, name="pack_small", out_shape=jax.ShapeDtypeStruct((8, K_SMALL), F32), compiler_params=_cparams(),
    )(g_conv_w, *vecs, loss)


def _adamw_small(slots, chip, conv_w, m_conv_w, v_conv_w, params, moms, vars_):
    n_vec = len(SMALL_NAMES)

    def body(chip_ref, s_ref, *refs):
        ins = refs[:3 * (n_vec + 1)]
        outs = refs[3 * (n_vec + 1):-1]
        tot_ref = refs[-1]
        tot = s_ref[0]
        for d in range(1, 8):
            tot = tot + s_ref[d]
        outs[0][...] = tot[LOSS_FIELD[0]:LOSS_FIELD[0] + 1, LOSS_FIELD[1]:LOSS_FIELD[1] + 1]
        off = pl.multiple_of(chip_ref[0] * CONV_COLS, 128)
        tot_ref[...] = tot
        grads = [tot_ref[0:CONV_K, pl.ds(off, CONV_COLS)]]
        grads += [tot[row:row + 1, o:o + n] for row, o, n in SMALL_FIELDS]
        for k, g in enumerate(grads):
            w_ref, m_ref, v_ref = ins[3 * k:3 * k + 3]
            full = (0,) if k == 0 else (Ellipsis,)
            d, nm, nv = _adamw_math(w_ref[full], g, m_ref[full], v_ref[full])
            for o_ref, val in zip(outs[1 + 4 * k:5 + 4 * k], (g, d, nm, nv)):
                o_ref[full] = val

    args = [conv_w, m_conv_w, v_conv_w]
    for w, m, v in zip(params, moms, vars_):
        args += [w, m, v]
    shapes = [jax.ShapeDtypeStruct((1, 1), F32)] + [jax.ShapeDtypeStruct(conv_w.shape, F32)] * 4
    for w in params:
        shapes += [jax.ShapeDtypeStruct(w.shape, F32)] * 4
    vmem = pl.BlockSpec(memory_space=pltpu.VMEM)
    return pl.pallas_call(
        body, name="adamw_small",
        grid_spec=pltpu.PrefetchScalarGridSpec(
            num_scalar_prefetch=1, grid=(1,),
            in_specs=[pl.BlockSpec(slots.shape, lambda i, chip: (0, 0, 0))] + [vmem] * len(args),
            out_specs=[vmem] * len(shapes), scratch_shapes=[pltpu.VMEM((8, K_SMALL), F32)]),
        out_shape=shapes, compiler_params=_cparams(),
    )(chip, slots, *args)


def kernel(x, positions, w_in, conv_w, conv_b, dt_bias, a_log, d_skip, ssd_norm_w, attn_sinks, w_out, ln_g, ln_b, loss_target, m_w_in, m_conv_w, m_conv_b, m_dt_bias, m_a_log, m_d_skip, m_ssd_norm_w, m_attn_sinks, m_w_out, m_ln_g, m_ln_b, v_w_in, v_conv_w, v_conv_b, v_dt_bias, v_a_log, v_d_skip, v_ssd_norm_w, v_attn_sinks, v_w_out, v_ln_g, v_ln_b):
    mx, my, mc = _mesh_pos()
    chip = 2 * mx + my
    L = x.shape[1]

    conv_w_s8 = jnp.pad(conv_w[0], ((0, 8 - CONV_K), (0, 0)))
    pad_rows = ((0, SLAB_ROWS - W_IN_COLS), (0, 0))
    w_in_t = w_in[0].T
    w_in_b, w_out_b = jnp.pad(_bf(w_in_t), pad_rows), _bf(w_out[0])
    ag_in, ag_cw = _gather_weights(w_in_b, conv_w_s8)
    started = _gather_w_out_start(w_out_b, ag_cw)

    def get_w_out(after):
        return _gather_w_out_wait(started[0:7], started[7], started[8], after).reshape(D_MIX, D_MODEL)

    w_full = jnp.concatenate([ag_in[j, 0:W_IN_COLS] for j in range(N_CHIPS)], axis=0)
    w = jnp.concatenate([
        w_full[O_Z:O_Z + D_SSD], w_full[O_G:O_G + D_ATT], w_full[O_Q:O_Q + D_ATT],
        w_full[O_XBC:O_XBC + D_XBC], w_full[O_K:O_K + 2 * D_KV], w_full[O_DT:O_DT + SSD_HEADS],
        jnp.zeros((DT_PAD - SSD_HEADS, D_MODEL), BF16)], axis=0)
    conv_w_full = jnp.concatenate([ag_cw[j, 0:CONV_K] for j in range(N_CHIPS)], axis=1)

    loss_part, gx_args, gw_in, w_out_red, small = _local_step(
        x[0], positions[0].reshape(L, 1), loss_target[0], w, get_w_out, started[9][0:1, :], conv_w_full,
        conv_b, dt_bias, a_log, d_skip, ssd_norm_w, attn_sinks, ln_g, ln_b)

    packed = _pack_small(small["conv_w"], [small[n] for n in SMALL_NAMES], loss_part)
    core_id = mc.reshape(1).astype(jnp.int32)
    chip_id = chip.reshape(1).astype(jnp.int32)
    ids = jnp.stack([4 * mx + 2 * my + mc, chip]).astype(jnp.int32)
    slabs = jnp.stack([jnp.pad(gw_in[W_IN_COLS * j:W_IN_COLS * (j + 1)], pad_rows) for j in range(N_CHIPS)])
    w_in_red = _pair_start(slabs, packed)
    own_slabs, landed = _reduce_w_out_wait(w_out_red[0:14], w_out_red[14], w_out_red[15], w_in_red[10])
    out_t = _adamw_sum8(w_out[0], own_slabs, landed, m_w_out[0], v_w_out[0], ids, "adamw_w_out")
    d_w_out, nm_w_out, nv_w_out, g_w_out = [a[None] for a in out_t]
    gw_in_slabs, recv_in = _pair_wait(w_in_red[0:8], w_in_red[8], w_in_red[9], out_t[0])
    s_in = _pair_add(gw_in_slabs, recv_in, core_id, "pair_add_in")
    chip_red = _chip_start(s_in, packed)
    grad_x = _grad_x(*gx_args, chip_red[8])
    s_in, r_in = _chip_wait(chip_red[0:6], chip_red[6], chip_red[7], grad_x)
    h_in = _chip_add(s_in, r_in, chip_id, "chip_add_in")
    sib_in, slots = _pair_share(h_in, packed)

    to_rows = lambda a: jnp.transpose(a, (2, 0, 1))
    in_t = _adamw_rows(to_rows(w_in), h_in, sib_in, to_rows(m_w_in), to_rows(v_w_in), core_id, "adamw_w_in")
    d_w_in, nm_w_in, nv_w_in, g_w_in = [jnp.transpose(a, (1, 2, 0)) for a in in_t]

    params = dict(conv_b=conv_b, ssd_norm_w=ssd_norm_w, ln_g=ln_g, ln_b=ln_b, dt_bias=dt_bias, a_log=a_log,
                  d_skip=d_skip, attn_sinks=attn_sinks)
    moms = dict(conv_b=m_conv_b, ssd_norm_w=m_ssd_norm_w, ln_g=m_ln_g, ln_b=m_ln_b, dt_bias=m_dt_bias, a_log=m_a_log,
                d_skip=m_d_skip, attn_sinks=m_attn_sinks)
    vars_ = dict(conv_b=v_conv_b, ssd_norm_w=v_ssd_norm_w, ln_g=v_ln_g, ln_b=v_ln_b, dt_bias=v_dt_bias, a_log=v_a_log,
                 d_skip=v_d_skip, attn_sinks=v_attn_sinks)
    res = _adamw_small(slots, chip_id, conv_w, m_conv_w, v_conv_w, [params[n] for n in SMALL_NAMES],
                       [moms[n] for n in SMALL_NAMES], [vars_[n] for n in SMALL_NAMES])
    loss = res[0][0, 0]
    grads, delta, new_m, new_v = {}, {}, {}, {}
    for k, n in enumerate(("conv_w",) + SMALL_NAMES):
        grads[n], delta[n], new_m[n], new_v[n] = res[1 + 4 * k:5 + 4 * k]
    for dd, a_in, a_out in ((grads, g_w_in, g_w_out), (delta, d_w_in, d_w_out), (new_m, nm_w_in, nm_w_out),
                            (new_v, nv_w_in, nv_w_out)):
        dd["w_in"] = a_in
        dd["w_out"] = a_out
    order = ("w_in", "conv_w", "conv_b", "dt_bias", "a_log", "d_skip", "ssd_norm_w", "attn_sinks", "w_out", "ln_g", "ln_b")
    return (loss, grad_x[None], *[grads[n] for n in order], *[delta[n] for n in order], *[new_m[n] for n in order],
            *[new_v[n] for n in order])
```
